```python
import math
import jax, jax.numpy as jnp
from jax import lax
import numpy as np

D_MODEL = 1024
BATCH = 16
SEQ = 2048
DEPTH = 1

MLA_HEADS = 4
Q_LORA_RANK = 256
KV_LORA_RANK = 256
QK_NOPE_DIM = 128
QK_ROPE_DIM = 64
QK_HEAD_DIM = QK_NOPE_DIM + QK_ROPE_DIM
V_HEAD_DIM = 128
MLA_WIDTH = MLA_HEADS * V_HEAD_DIM
ROPE_THETA = 10000.0
Q_BLOCK = 128
GDN_HEADS = 4
GDN_HEAD_DIM = 128
GDN_WIDTH = GDN_HEADS * GDN_HEAD_DIM
CONV_WIDTH = 4
CHUNK = 64
MIX_WIDTH = MLA_WIDTH + GDN_WIDTH
D_FF = 4 * D_MODEL
EPS = 1e-6
IN_SPLITS = (Q_LORA_RANK, KV_LORA_RANK, QK_ROPE_DIM,
             GDN_WIDTH, GDN_WIDTH, GDN_WIDTH, GDN_WIDTH, GDN_HEADS, GDN_HEADS)
D_IN = sum(IN_SPLITS)

kernel_name = "hymba_mla_gdn_sqrelu_layer"


def rms_norm(x, w):
    xf = x.astype(jnp.float32)
    y = xf * lax.rsqrt(jnp.mean(xf * xf, axis=-1, keepdims=True) + EPS)
    return (y * w.astype(jnp.float32)).astype(x.dtype)


def l2_norm(x):
    return x * lax.rsqrt(jnp.sum(x * x, axis=-1, keepdims=True) + EPS)


def split_cols(t, sizes):
    offs = np.cumsum(sizes)[:-1].tolist()
    return jnp.split(t, offs, axis=-1)


def rope_angles(positions):
    half = QK_ROPE_DIM // 2
    inv_freq = ROPE_THETA ** (-jnp.arange(half, dtype=jnp.float32) / half)
    ang = positions.astype(jnp.float32)[..., None] * inv_freq
    return jnp.cos(ang)[:, :, None, :], jnp.sin(ang)[:, :, None, :]


def apply_rope(t, cos, sin):
    tf = t.astype(jnp.float32)
    t1, t2 = jnp.split(tf, 2, axis=-1)
    return jnp.concatenate([t1 * cos - t2 * sin, t2 * cos + t1 * sin], axis=-1).astype(t.dtype)


def causal_attention(q, k, v):
    B, S, H, _ = q.shape
    n_blocks = S // Q_BLOCK
    scale = QK_HEAD_DIM ** -0.5
    qb = jnp.moveaxis(q.reshape(B, n_blocks, Q_BLOCK, H, QK_HEAD_DIM), 1, 0)
    key_pos = jnp.arange(S)

    def one_block(args):
        q_blk, blk = args
        s = jnp.einsum('bqhd,bkhd->bhqk', q_blk, k,
                       preferred_element_type=jnp.float32) * scale
        q_pos = blk * Q_BLOCK + jnp.arange(Q_BLOCK)
        s = jnp.where(key_pos[None, :] <= q_pos[:, None], s, -jnp.inf)
        p = jax.nn.softmax(s, axis=-1).astype(v.dtype)
        return jnp.einsum('bhqk,bkhd->bqhd', p, v)

    o = lax.map(one_block, (qb, jnp.arange(n_blocks)))
    return jnp.moveaxis(o, 0, 1).reshape(B, S, H, V_HEAD_DIM)


def mla_group(q_lat, kv_lat, k_pe, cos, sin, q_lat_norm_w, w_uq, kv_lat_norm_w, w_ukv,
              q_norm_w, k_norm_w, mla_out_norm_w):
    B, S, _ = q_lat.shape
    q = (rms_norm(q_lat, q_lat_norm_w) @ w_uq).reshape(B, S, MLA_HEADS, QK_HEAD_DIM)
    kv = (rms_norm(kv_lat, kv_lat_norm_w) @ w_ukv).reshape(B, S, MLA_HEADS, QK_NOPE_DIM + V_HEAD_DIM)
    k_nope, v = jnp.split(kv, [QK_NOPE_DIM], axis=-1)
    q_nope = rms_norm(q[..., :QK_NOPE_DIM], q_norm_w[:QK_NOPE_DIM])
    q_pe = apply_rope(rms_norm(q[..., QK_NOPE_DIM:], q_norm_w[QK_NOPE_DIM:]), cos, sin)
    k_nope = rms_norm(k_nope, k_norm_w[:QK_NOPE_DIM])
    k_pe = apply_rope(rms_norm(k_pe[:, :, None, :], k_norm_w[QK_NOPE_DIM:]), cos, sin)
    q = jnp.concatenate([q_nope, q_pe], axis=-1)
    k = jnp.concatenate([k_nope, jnp.broadcast_to(k_pe, (B, S, MLA_HEADS, QK_ROPE_DIM))], axis=-1)
    o = causal_attention(q, k, v)
    o = rms_norm(o, mla_out_norm_w)
    return o.reshape(B, S, MLA_WIDTH)


def causal_conv(x, w):
    S = x.shape[1]
    xp = jnp.pad(x, ((0, 0), (CONV_WIDTH - 1, 0), (0, 0)))
    return sum(w[i] * xp[:, i:i + S] for i in range(CONV_WIDTH))


def chunk_gated_delta(q, k, v, g, beta):
    B, H, S, D = q.shape
    N = S // CHUNK
    q, k, v = [t.reshape(B, H, N, CHUNK, D) for t in (q, k, v)]
    g = g.reshape(B, H, N, CHUNK)
    beta = beta.reshape(B, H, N, CHUNK)
    G = jnp.cumsum(g, axis=-1)
    idx = jnp.arange(CHUNK)
    causal = idx[:, None] >= idx[None, :]
    strict = idx[:, None] > idx[None, :]
    decay = jnp.exp(jnp.where(causal, G[..., :, None] - G[..., None, :], -jnp.inf))
    kk = jnp.einsum('bhncd,bhnjd->bhncj', k, k)
    L = jnp.where(strict, beta[..., :, None] * kk * decay, 0.0)
    A = L + jnp.eye(CHUNK, dtype=L.dtype)
    rhs = jnp.concatenate([v * beta[..., None], k * (beta * jnp.exp(G))[..., None]], axis=-1)
    sol = lax.linalg.triangular_solve(A, rhs, left_side=True, lower=True, unit_diagonal=True)
    u, w = jnp.split(sol, 2, axis=-1)
    attn_intra = jnp.einsum('bhncd,bhnjd->bhncj', q, k) * decay
    q_dec = q * jnp.exp(G)[..., None]
    k_dec = k * jnp.exp(G[..., -1:] - G)[..., None]
    chunk_decay = jnp.exp(G[..., -1])

    def step(state, xs):
        u_c, w_c, a_c, qd_c, kd_c, cd_c = xs
        v_new = u_c - jnp.einsum('bhcd,bhde->bhce', w_c, state)
        o_c = jnp.einsum('bhcd,bhde->bhce', qd_c, state) + jnp.einsum('bhcj,bhje->bhce', a_c, v_new)
        state = state * cd_c[..., None, None] + jnp.einsum('bhcd,bhce->bhde', kd_c, v_new)
        return state, o_c

    xs = tuple(jnp.moveaxis(t, 2, 0) for t in (u, w, attn_intra, q_dec, k_dec, chunk_decay))
    state0 = jnp.zeros((B, H, D, D), jnp.float32)
    _, o = lax.scan(step, state0, xs)
    return jnp.moveaxis(o, 0, 2).reshape(B, H, S, D)


def gdn_group(q, k, v, z, a, b, conv_w, a_log, dt_bias, gdn_norm_w):
    B, S, _ = q.shape
    qkv = jax.nn.silu(causal_conv(jnp.concatenate([q, k, v], axis=-1), conv_w))
    q, k, v = [t.reshape(B, S, GDN_HEADS, GDN_HEAD_DIM).transpose(0, 2, 1, 3).astype(jnp.float32)
               for t in jnp.split(qkv, 3, axis=-1)]
    q = l2_norm(q) * (GDN_HEAD_DIM ** -0.5)
    k = l2_norm(k)
    beta = jax.nn.sigmoid(b.astype(jnp.float32)).transpose(0, 2, 1)
    g = (-jnp.exp(a_log.astype(jnp.float32))
         * jax.nn.softplus(a.astype(jnp.float32) + dt_bias.astype(jnp.float32))).transpose(0, 2, 1)
    o = chunk_gated_delta(q, k, v, g, beta).transpose(0, 2, 1, 3).astype(z.dtype)
    zh = z.reshape(B, S, GDN_HEADS, GDN_HEAD_DIM)
    o = rms_norm(o, gdn_norm_w) * jax.nn.silu(zh)
    return o.reshape(B, S, GDN_WIDTH)


def _fwd_setup_inputs(seed: int = 0) -> dict:
    key = jax.random.key(seed)
    ks = jax.random.split(key, 20)
    L = DEPTH

    def normal(k, shape, fan_in):
        return jax.random.normal(k, shape, jnp.float32) * (fan_in ** -0.5)

    def gain(k, shape):
        return 1.0 + 0.02 * jax.random.normal(k, shape, jnp.float32)

    return {
        "x": jax.random.normal(ks[0], (BATCH, SEQ, D_MODEL), jnp.float32),
        "positions": jnp.broadcast_to(jnp.arange(SEQ, dtype=jnp.int32), (BATCH, SEQ)),
        "attn_norm_w": gain(ks[1], (L, D_MODEL)),
        "w_in": normal(ks[2], (L, D_MODEL, D_IN), D_MODEL),
        "q_lat_norm_w": gain(ks[3], (L, Q_LORA_RANK)),
        "w_uq": normal(ks[4], (L, Q_LORA_RANK, MLA_HEADS * QK_HEAD_DIM), Q_LORA_RANK),
        "kv_lat_norm_w": gain(ks[5], (L, KV_LORA_RANK)),
        "w_ukv": normal(ks[6], (L, KV_LORA_RANK, MLA_HEADS * (QK_NOPE_DIM + V_HEAD_DIM)), KV_LORA_RANK),
        "q_norm_w": gain(ks[7], (L, QK_HEAD_DIM)),
        "k_norm_w": gain(ks[8], (L, QK_HEAD_DIM)),
        "mla_out_norm_w": gain(ks[9], (L, MLA_HEADS, V_HEAD_DIM)),
        "conv_w": normal(ks[10], (L, CONV_WIDTH, 3 * GDN_WIDTH), CONV_WIDTH),
        "a_log": jnp.log(jax.random.uniform(ks[11], (L, GDN_HEADS), jnp.float32, 1.0, 16.0)),
        "dt_bias": 0.1 * jax.random.normal(ks[12], (L, GDN_HEADS), jnp.float32),
        "gdn_norm_w": gain(ks[13], (L, GDN_HEAD_DIM)),
        "w_out": normal(ks[14], (L, MIX_WIDTH, D_MODEL), MIX_WIDTH),
        "mlp_norm_w": gain(ks[15], (L, D_MODEL)),
        "w_up": normal(ks[16], (L, D_MODEL, D_FF), D_MODEL),
        "w_down": normal(ks[17], (L, D_FF, D_MODEL), D_FF),
    }


def _fwd_reference(x, positions, attn_norm_w, w_in, q_lat_norm_w, w_uq, kv_lat_norm_w, w_ukv,
              q_norm_w, k_norm_w, mla_out_norm_w, conv_w, a_log, dt_bias, gdn_norm_w,
              w_out, mlp_norm_w, w_up, w_down):
    cos, sin = rope_angles(positions)
    h = x
    for l in range(DEPTH):
        xn = rms_norm(h, attn_norm_w[l])
        proj = xn @ w_in[l]
        q_lat, kv_lat, k_pe, gq, gk, gv, gz, ga, gb = split_cols(proj, IN_SPLITS)
        mla_o = mla_group(q_lat, kv_lat, k_pe, cos, sin, q_lat_norm_w[l], w_uq[l],
                          kv_lat_norm_w[l], w_ukv[l], q_norm_w[l], k_norm_w[l], mla_out_norm_w[l])
        gdn_o = gdn_group(gq, gk, gv, gz, ga, gb, conv_w[l], a_log[l], dt_bias[l], gdn_norm_w[l])
        h = h + jnp.concatenate([mla_o, gdn_o], axis=-1) @ w_out[l]
        hn = rms_norm(h, mlp_norm_w[l])
        h = h + jnp.square(jax.nn.relu(hn @ w_up[l])) @ w_down[l]
    return h


import jax as _jax
import jax.numpy as _jnp

TWIN_FORMAT = 'train_step'
FWD_PARAMS = ['x', 'positions', 'attn_norm_w', 'w_in', 'q_lat_norm_w', 'w_uq', 'kv_lat_norm_w', 'w_ukv', 'q_norm_w', 'k_norm_w', 'mla_out_norm_w', 'conv_w', 'a_log', 'dt_bias', 'gdn_norm_w', 'w_out', 'mlp_norm_w', 'w_up', 'w_down']
TWIN_WEIGHTS = ['attn_norm_w', 'w_in', 'q_lat_norm_w', 'w_uq', 'kv_lat_norm_w', 'w_ukv', 'q_norm_w', 'k_norm_w', 'mla_out_norm_w', 'conv_w', 'a_log', 'dt_bias', 'gdn_norm_w', 'w_out', 'mlp_norm_w', 'w_up', 'w_down']
TWIN_DIFF_INPUT = 'x'
TWIN_INPUTS = ['x', 'positions', 'attn_norm_w', 'w_in', 'q_lat_norm_w', 'w_uq', 'kv_lat_norm_w', 'w_ukv', 'q_norm_w', 'k_norm_w', 'mla_out_norm_w', 'conv_w', 'a_log', 'dt_bias', 'gdn_norm_w', 'w_out', 'mlp_norm_w', 'w_up', 'w_down', 'loss_target', 'm_attn_norm_w', 'm_w_in', 'm_q_lat_norm_w', 'm_w_uq', 'm_kv_lat_norm_w', 'm_w_ukv', 'm_q_norm_w', 'm_k_norm_w', 'm_mla_out_norm_w', 'm_conv_w', 'm_a_log', 'm_dt_bias', 'm_gdn_norm_w', 'm_w_out', 'm_mlp_norm_w', 'm_w_up', 'm_w_down', 'v_attn_norm_w', 'v_w_in', 'v_q_lat_norm_w', 'v_w_uq', 'v_kv_lat_norm_w', 'v_w_ukv', 'v_q_norm_w', 'v_k_norm_w', 'v_mla_out_norm_w', 'v_conv_w', 'v_a_log', 'v_dt_bias', 'v_gdn_norm_w', 'v_w_out', 'v_mlp_norm_w', 'v_w_up', 'v_w_down']
TWIN_OUTPUTS = ['loss', 'grad_x', 'grad_attn_norm_w', 'grad_w_in', 'grad_q_lat_norm_w', 'grad_w_uq', 'grad_kv_lat_norm_w', 'grad_w_ukv', 'grad_q_norm_w', 'grad_k_norm_w', 'grad_mla_out_norm_w', 'grad_conv_w', 'grad_a_log', 'grad_dt_bias', 'grad_gdn_norm_w', 'grad_w_out', 'grad_mlp_norm_w', 'grad_w_up', 'grad_w_down', 'delta_attn_norm_w', 'delta_w_in', 'delta_q_lat_norm_w', 'delta_w_uq', 'delta_kv_lat_norm_w', 'delta_w_ukv', 'delta_q_norm_w', 'delta_k_norm_w', 'delta_mla_out_norm_w', 'delta_conv_w', 'delta_a_log', 'delta_dt_bias', 'delta_gdn_norm_w', 'delta_w_out', 'delta_mlp_norm_w', 'delta_w_up', 'delta_w_down', 'new_m_attn_norm_w', 'new_m_w_in', 'new_m_q_lat_norm_w', 'new_m_w_uq', 'new_m_kv_lat_norm_w', 'new_m_w_ukv', 'new_m_q_norm_w', 'new_m_k_norm_w', 'new_m_mla_out_norm_w', 'new_m_conv_w', 'new_m_a_log', 'new_m_dt_bias', 'new_m_gdn_norm_w', 'new_m_w_out', 'new_m_mlp_norm_w', 'new_m_w_up', 'new_m_w_down', 'new_v_attn_norm_w', 'new_v_w_in', 'new_v_q_lat_norm_w', 'new_v_w_uq', 'new_v_kv_lat_norm_w', 'new_v_w_ukv', 'new_v_q_norm_w', 'new_v_k_norm_w', 'new_v_mla_out_norm_w', 'new_v_conv_w', 'new_v_a_log', 'new_v_dt_bias', 'new_v_gdn_norm_w', 'new_v_w_out', 'new_v_mlp_norm_w', 'new_v_w_up', 'new_v_w_down']
TWIN_LEAF_KINDS = {'loss': 'loss', 'grad_x': 'grad_x', 'grad_attn_norm_w': 'grad_w', 'grad_w_in': 'grad_w', 'grad_q_lat_norm_w': 'grad_w', 'grad_w_uq': 'grad_w', 'grad_kv_lat_norm_w': 'grad_w', 'grad_w_ukv': 'grad_w', 'grad_q_norm_w': 'grad_w', 'grad_k_norm_w': 'grad_w', 'grad_mla_out_norm_w': 'grad_w', 'grad_conv_w': 'grad_w', 'grad_a_log': 'grad_w', 'grad_dt_bias': 'grad_w', 'grad_gdn_norm_w': 'grad_w', 'grad_w_out': 'grad_w', 'grad_mlp_norm_w': 'grad_w', 'grad_w_up': 'grad_w', 'grad_w_down': 'grad_w', 'delta_attn_norm_w': 'delta_w', 'delta_w_in': 'delta_w', 'delta_q_lat_norm_w': 'delta_w', 'delta_w_uq': 'delta_w', 'delta_kv_lat_norm_w': 'delta_w', 'delta_w_ukv': 'delta_w', 'delta_q_norm_w': 'delta_w', 'delta_k_norm_w': 'delta_w', 'delta_mla_out_norm_w': 'delta_w', 'delta_conv_w': 'delta_w', 'delta_a_log': 'delta_w', 'delta_dt_bias': 'delta_w', 'delta_gdn_norm_w': 'delta_w', 'delta_w_out': 'delta_w', 'delta_mlp_norm_w': 'delta_w', 'delta_w_up': 'delta_w', 'delta_w_down': 'delta_w', 'new_m_attn_norm_w': 'new_m', 'new_m_w_in': 'new_m', 'new_m_q_lat_norm_w': 'new_m', 'new_m_w_uq': 'new_m', 'new_m_kv_lat_norm_w': 'new_m', 'new_m_w_ukv': 'new_m', 'new_m_q_norm_w': 'new_m', 'new_m_k_norm_w': 'new_m', 'new_m_mla_out_norm_w': 'new_m', 'new_m_conv_w': 'new_m', 'new_m_a_log': 'new_m', 'new_m_dt_bias': 'new_m', 'new_m_gdn_norm_w': 'new_m', 'new_m_w_out': 'new_m', 'new_m_mlp_norm_w': 'new_m', 'new_m_w_up': 'new_m', 'new_m_w_down': 'new_m', 'new_v_attn_norm_w': 'new_v', 'new_v_w_in': 'new_v', 'new_v_q_lat_norm_w': 'new_v', 'new_v_w_uq': 'new_v', 'new_v_kv_lat_norm_w': 'new_v', 'new_v_w_ukv': 'new_v', 'new_v_q_norm_w': 'new_v', 'new_v_k_norm_w': 'new_v', 'new_v_mla_out_norm_w': 'new_v', 'new_v_conv_w': 'new_v', 'new_v_a_log': 'new_v', 'new_v_dt_bias': 'new_v', 'new_v_gdn_norm_w': 'new_v', 'new_v_w_out': 'new_v', 'new_v_mlp_norm_w': 'new_v', 'new_v_w_up': 'new_v', 'new_v_w_down': 'new_v'}


def _forward(args):
    return _fwd_reference(*[args[k] for k in FWD_PARAMS])


def _output_shape():
    out = _jax.eval_shape(lambda: _forward(_fwd_setup_inputs(0)))
    return out.shape, out.dtype

N_MICROBATCH = 1
ADAM_LR = 0.001
ADAM_B1 = 0.9
ADAM_B2 = 0.999
ADAM_EPS = 1e-08
ADAM_WD = 0.01
ADAM_STEP = 10
PER_EXAMPLE_BATCH_AXIS = {'x': 0, 'positions': 0, 'loss_target': 0}
SHARED_INPUTS = []
_WEIGHT_DTYPES = {'attn_norm_w': _jnp.float32, 'w_in': _jnp.float32, 'q_lat_norm_w': _jnp.float32, 'w_uq': _jnp.float32, 'kv_lat_norm_w': _jnp.float32, 'w_ukv': _jnp.float32, 'q_norm_w': _jnp.float32, 'k_norm_w': _jnp.float32, 'mla_out_norm_w': _jnp.float32, 'conv_w': _jnp.float32, 'a_log': _jnp.float32, 'dt_bias': _jnp.float32, 'gdn_norm_w': _jnp.float32, 'w_out': _jnp.float32, 'mlp_norm_w': _jnp.float32, 'w_up': _jnp.float32, 'w_down': _jnp.float32}
MOMENT_SCALE = {'attn_norm_w': 5.830807e+00, 'w_in': 1.714942e+00, 'q_lat_norm_w': 2.177298e+00, 'w_uq': 1.013411e+00, 'kv_lat_norm_w': 8.449655e+00, 'w_ukv': 3.263270e+00, 'q_norm_w': 2.974498e+00, 'k_norm_w': 2.773567e+00, 'mla_out_norm_w': 3.087142e+01, 'conv_w': 8.312191e-01, 'a_log': 1.160781e+01, 'dt_bias': 1.026601e+01, 'gdn_norm_w': 3.605992e+01, 'w_out': 3.491443e+00, 'mlp_norm_w': 9.504480e+01, 'w_up': 1.593017e+00, 'w_down': 8.119717e+00}


def _to_microbatches(a, axis):
    t = _jnp.moveaxis(a, axis, 0)
    t = t.reshape((N_MICROBATCH, t.shape[0] // N_MICROBATCH) + t.shape[1:])
    return _jnp.moveaxis(t, 1, axis + 1)


def setup_inputs(seed: int = 0) -> dict:
    inp = _fwd_setup_inputs(seed)
    key = _jax.random.fold_in(_jax.random.key(seed), 7919)
    shape, _ = _output_shape()
    out = dict(inp)
    out["loss_target"] = _jax.random.normal(_jax.random.fold_in(key, 0), shape, _jnp.float32)
    for i, name in enumerate(TWIN_WEIGHTS):
        w = inp[name].astype(_jnp.float32)
        if MOMENT_SCALE is None:
            s = _jnp.sqrt(_jnp.mean(_jnp.square(w)) + 1e-30)
        else:
            s = MOMENT_SCALE[name]
        km, kv = _jax.random.split(_jax.random.fold_in(key, i + 1))
        out[name] = w
        out["m_" + name] = s * _jax.random.normal(km, w.shape, _jnp.float32)
        out["v_" + name] = (s * s) * _jax.random.uniform(kv, w.shape, _jnp.float32, 0.5, 1.5)
    if N_MICROBATCH > 1:
        for name, axis in PER_EXAMPLE_BATCH_AXIS.items():
            out[name] = _to_microbatches(out[name], axis)
    return {'x': out['x'], 'positions': out['positions'], 'attn_norm_w': out['attn_norm_w'], 'w_in': out['w_in'], 'q_lat_norm_w': out['q_lat_norm_w'], 'w_uq': out['w_uq'], 'kv_lat_norm_w': out['kv_lat_norm_w'], 'w_ukv': out['w_ukv'], 'q_norm_w': out['q_norm_w'], 'k_norm_w': out['k_norm_w'], 'mla_out_norm_w': out['mla_out_norm_w'], 'conv_w': out['conv_w'], 'a_log': out['a_log'], 'dt_bias': out['dt_bias'], 'gdn_norm_w': out['gdn_norm_w'], 'w_out': out['w_out'], 'mlp_norm_w': out['mlp_norm_w'], 'w_up': out['w_up'], 'w_down': out['w_down'], 'loss_target': out['loss_target'], 'm_attn_norm_w': out['m_attn_norm_w'], 'm_w_in': out['m_w_in'], 'm_q_lat_norm_w': out['m_q_lat_norm_w'], 'm_w_uq': out['m_w_uq'], 'm_kv_lat_norm_w': out['m_kv_lat_norm_w'], 'm_w_ukv': out['m_w_ukv'], 'm_q_norm_w': out['m_q_norm_w'], 'm_k_norm_w': out['m_k_norm_w'], 'm_mla_out_norm_w': out['m_mla_out_norm_w'], 'm_conv_w': out['m_conv_w'], 'm_a_log': out['m_a_log'], 'm_dt_bias': out['m_dt_bias'], 'm_gdn_norm_w': out['m_gdn_norm_w'], 'm_w_out': out['m_w_out'], 'm_mlp_norm_w': out['m_mlp_norm_w'], 'm_w_up': out['m_w_up'], 'm_w_down': out['m_w_down'], 'v_attn_norm_w': out['v_attn_norm_w'], 'v_w_in': out['v_w_in'], 'v_q_lat_norm_w': out['v_q_lat_norm_w'], 'v_w_uq': out['v_w_uq'], 'v_kv_lat_norm_w': out['v_kv_lat_norm_w'], 'v_w_ukv': out['v_w_ukv'], 'v_q_norm_w': out['v_q_norm_w'], 'v_k_norm_w': out['v_k_norm_w'], 'v_mla_out_norm_w': out['v_mla_out_norm_w'], 'v_conv_w': out['v_conv_w'], 'v_a_log': out['v_a_log'], 'v_dt_bias': out['v_dt_bias'], 'v_gdn_norm_w': out['v_gdn_norm_w'], 'v_w_out': out['v_w_out'], 'v_mlp_norm_w': out['v_mlp_norm_w'], 'v_w_up': out['v_w_up'], 'v_w_down': out['v_w_down']}


def _loss(weights, diff, rest, loss_target):
    with _jax.named_scope("forward"):
        args = {**rest, TWIN_DIFF_INPUT: diff, **{k: w.astype(_WEIGHT_DTYPES[k]) for k, w in weights.items()}}
        y = _forward(args)
    with _jax.named_scope("loss_head"):
        err = _jnp.square(y.astype(_jnp.float32) - loss_target)
        return 0.5 * _jnp.sum(_jnp.mean(err, axis=-1)) if err.ndim else 0.5 * err


def _adamw(w, g, m, v):
    m = ADAM_B1 * m + (1.0 - ADAM_B1) * g
    v = ADAM_B2 * v + (1.0 - ADAM_B2) * _jnp.square(g)
    m_hat = m / (1.0 - ADAM_B1 ** ADAM_STEP)
    v_hat = v / (1.0 - ADAM_B2 ** ADAM_STEP)
    delta = -ADAM_LR * (m_hat / (_jnp.sqrt(v_hat) + ADAM_EPS) + ADAM_WD * w)
    return delta, m, v


def reference(x, positions, attn_norm_w, w_in, q_lat_norm_w, w_uq, kv_lat_norm_w, w_ukv, q_norm_w, k_norm_w, mla_out_norm_w, conv_w, a_log, dt_bias, gdn_norm_w, w_out, mlp_norm_w, w_up, w_down, loss_target, m_attn_norm_w, m_w_in, m_q_lat_norm_w, m_w_uq, m_kv_lat_norm_w, m_w_ukv, m_q_norm_w, m_k_norm_w, m_mla_out_norm_w, m_conv_w, m_a_log, m_dt_bias, m_gdn_norm_w, m_w_out, m_mlp_norm_w, m_w_up, m_w_down, v_attn_norm_w, v_w_in, v_q_lat_norm_w, v_w_uq, v_kv_lat_norm_w, v_w_ukv, v_q_norm_w, v_k_norm_w, v_mla_out_norm_w, v_conv_w, v_a_log, v_dt_bias, v_gdn_norm_w, v_w_out, v_mlp_norm_w, v_w_up, v_w_down):
    given = dict(x=x, positions=positions, attn_norm_w=attn_norm_w, w_in=w_in, q_lat_norm_w=q_lat_norm_w, w_uq=w_uq, kv_lat_norm_w=kv_lat_norm_w, w_ukv=w_ukv, q_norm_w=q_norm_w, k_norm_w=k_norm_w, mla_out_norm_w=mla_out_norm_w, conv_w=conv_w, a_log=a_log, dt_bias=dt_bias, gdn_norm_w=gdn_norm_w, w_out=w_out, mlp_norm_w=mlp_norm_w, w_up=w_up, w_down=w_down, loss_target=loss_target, m_attn_norm_w=m_attn_norm_w, m_w_in=m_w_in, m_q_lat_norm_w=m_q_lat_norm_w, m_w_uq=m_w_uq, m_kv_lat_norm_w=m_kv_lat_norm_w, m_w_ukv=m_w_ukv, m_q_norm_w=m_q_norm_w, m_k_norm_w=m_k_norm_w, m_mla_out_norm_w=m_mla_out_norm_w, m_conv_w=m_conv_w, m_a_log=m_a_log, m_dt_bias=m_dt_bias, m_gdn_norm_w=m_gdn_norm_w, m_w_out=m_w_out, m_mlp_norm_w=m_mlp_norm_w, m_w_up=m_w_up, m_w_down=m_w_down, v_attn_norm_w=v_attn_norm_w, v_w_in=v_w_in, v_q_lat_norm_w=v_q_lat_norm_w, v_w_uq=v_w_uq, v_kv_lat_norm_w=v_kv_lat_norm_w, v_w_ukv=v_w_ukv, v_q_norm_w=v_q_norm_w, v_k_norm_w=v_k_norm_w, v_mla_out_norm_w=v_mla_out_norm_w, v_conv_w=v_conv_w, v_a_log=v_a_log, v_dt_bias=v_dt_bias, v_gdn_norm_w=v_gdn_norm_w, v_w_out=v_w_out, v_mlp_norm_w=v_mlp_norm_w, v_w_up=v_w_up, v_w_down=v_w_down)
    weights = {n: given[n] for n in TWIN_WEIGHTS}
    shared = {n: given[n] for n in SHARED_INPUTS}
    per_example = {n: given[n] for n in ['x', 'positions']}
    grad_fn = _jax.value_and_grad(_loss, argnums=(0, 1))

    def one_microbatch(ex, loss_target):
        ex = dict(ex)
        diff = ex.pop(TWIN_DIFF_INPUT)
        return grad_fn(weights, diff, {**shared, **ex}, loss_target)

    if N_MICROBATCH == 1:
        loss, (grad_w, grad_x) = one_microbatch(per_example, given["loss_target"])
    else:
        def body(carry, xs):
            loss_sum, grad_sum = carry
            l_k, (gw_k, gx_k) = one_microbatch(xs[0], xs[1])
            with _jax.named_scope("update"):
                return (loss_sum + l_k, _jax.tree.map(_jnp.add, grad_sum, gw_k)), gx_k

        init = (_jnp.zeros((), _jnp.float32), _jax.tree.map(_jnp.zeros_like, weights))
        (loss, grad_w), grad_x = _jax.lax.scan(body, init, (per_example, given["loss_target"]))
    with _jax.named_scope("update"):
        delta_w, new_m, new_v = {}, {}, {}
        for n in TWIN_WEIGHTS:
            delta_w[n], new_m[n], new_v[n] = _adamw(weights[n], grad_w[n], given["m_" + n], given["v_" + n])
    return (loss, grad_x, *[grad_w[n] for n in TWIN_WEIGHTS], *[delta_w[n] for n in TWIN_WEIGHTS],
            *[new_m[n] for n in TWIN_WEIGHTS], *[new_v[n] for n in TWIN_WEIGHTS])
```

```python
import functools

import jax
import jax.numpy as jnp
import numpy as np
from jax import lax
from jax.experimental import pallas as pl
from jax.experimental.pallas import tpu as pltpu

F32 = jnp.float32
MXU_DTYPE = jnp.bfloat16
SDS = jax.ShapeDtypeStruct
HIGHEST = lax.Precision.HIGHEST
MESH_ID = pl.DeviceIdType.MESH

D_MODEL = 1024
MLA_HEADS = 4
Q_LORA = 256
KV_LORA = 256
NOPE = 128
ROPE = 64
QK_DIM = NOPE + ROPE
V_DIM = 128
ROPE_THETA = 10000.0
GDN_HEADS = 4
GDN_DIM = 128
GDN_WIDTH = GDN_HEADS * GDN_DIM
CONV_W = 4
CHUNK = 64
D_FF = 4 * D_MODEL
EPS = 1e-6
ATT_SCALE = QK_DIM ** -0.5
GDN_QSCALE = GDN_DIM ** -0.5
N_DEV = 8

ADAM_LR = 0.001
ADAM_B1 = 0.9
ADAM_B2 = 0.999
ADAM_EPS = 1e-08
ADAM_WD = 0.01
ADAM_STEP = 10

LANES = 128
SUBLANES = 8
FLAT_ROWS = 512
VMEM_LIMIT = 56 * 1024 * 1024

P_GQKV, P_GZ, P_QLAT, P_KVLAT, P_KPE, P_GAB = 0, 1536, 2048, 2304, 2560, 2688
P_WIDTH = 2816
O_QLAT, O_KVLAT, O_KPE, O_GQKV, O_GZ, O_GAB, O_END = 0, 256, 512, 576, 2112, 2624, 2632


def _params(sem=None, vmem=VMEM_LIMIT):
    kw = dict(vmem_limit_bytes=vmem)
    if sem is not None:
        kw["dimension_semantics"] = sem
    return pltpu.CompilerParams(**kw)


def _mm(a, b):
    return jnp.dot(a.astype(MXU_DTYPE), b.astype(MXU_DTYPE), preferred_element_type=F32)


def _mm_nt(a, b):
    return lax.dot_general(a.astype(MXU_DTYPE), b.astype(MXU_DTYPE), (((1,), (1,)), ((), ())),
                           preferred_element_type=F32)


def _mm_tn(a, b):
    return lax.dot_general(a.astype(MXU_DTYPE), b.astype(MXU_DTYPE), (((0,), (0,)), ((), ())),
                           preferred_element_type=F32)


def _mm_exact(a, b):
    return jnp.dot(a, b, precision=HIGHEST, preferred_element_type=F32)


def _rms(x, w):
    r = lax.rsqrt(jnp.mean(x * x, axis=-1, keepdims=True) + EPS)
    return x * r * w, r


def _rms_bwd(dy, x, w, r):
    xh = x * r
    dyw = dy * w
    dx = r * (dyw - xh * jnp.mean(dyw * xh, axis=-1, keepdims=True))
    dw = jnp.sum(dy * xh, axis=0, keepdims=True)
    return dx, dw


def _l2n_bwd(dy, x, scale):
    r = lax.rsqrt(jnp.sum(x * x, axis=-1, keepdims=True) + EPS)
    xh = x * r
    return (scale * r) * (dy - xh * jnp.sum(dy * xh, axis=-1, keepdims=True))


def _rot(t):
    return jnp.concatenate([-t[:, ROPE // 2:], t[:, :ROPE // 2]], axis=-1)


def _rot_t(t):
    return jnp.concatenate([t[:, ROPE // 2:], -t[:, :ROPE // 2]], axis=-1)


def _rope(t, cos, sin):
    return t * cos + _rot(t) * sin


def _rope_bwd(d, cos, sin):
    return d * cos + _rot_t(d * sin)


def _sigmoid(x):
    return jax.nn.sigmoid(x)


def _shift_down(x, halo, j):
    if j == 0:
        return x
    xr = pltpu.roll(x, j, 0)
    hr = pltpu.roll(halo, j, 0)
    row = lax.broadcasted_iota(jnp.int32, halo.shape, 0)
    top = jnp.where(row < j, hr, xr[:SUBLANES])
    return jnp.concatenate([top, xr[SUBLANES:]], axis=0)


def _shift_up(x, nxt, j):
    if j == 0:
        return x
    n = x.shape[0]
    xr = pltpu.roll(x, n - j, 0)
    nr = pltpu.roll(nxt, SUBLANES - j, 0)
    row = lax.broadcasted_iota(jnp.int32, nxt.shape, 0)
    bot = jnp.where(row >= SUBLANES - j, nr, xr[n - SUBLANES:])
    return jnp.concatenate([xr[:n - SUBLANES], bot], axis=0)


def _chunk_cumsum(y, row_in_chunk):
    s = 1
    while s < CHUNK:
        y = y + jnp.where(row_in_chunk >= s, pltpu.roll(y, s, 0), 0.0)
        s *= 2
    return y


def _chunk_rev_cumsum(y, row_in_chunk):
    n = y.shape[0]
    s = 1
    while s < CHUNK:
        y = y + jnp.where(row_in_chunk + s < CHUNK, pltpu.roll(y, n - s, 0), 0.0)
        s *= 2
    return y


def _pick_lane(tile, lane, idx):
    return jnp.sum(jnp.where(lane == idx, tile, 0.0), axis=-1, keepdims=True)


def _divisor_tile(n, cap, unit=LANES):
    best = unit
    t = unit
    while t <= min(n, cap):
        if n % t == 0:
            best = t
        t += unit
    return n if n <= cap else best


def _in_proj(x2, w_an, w_in_p):
    T, D = x2.shape
    N = w_in_p.shape[1]
    tm = min(512, T)

    def body(x_ref, wn_ref, w_ref, proj_ref, xn_ref):
        xn, _ = _rms(x_ref[...], wn_ref[...])
        xn = xn.astype(MXU_DTYPE)
        xn_ref[...] = xn
        proj_ref[...] = jnp.dot(xn, w_ref[...], preferred_element_type=F32)

    return pl.pallas_call(
        body, grid=(T // tm,), name="in_proj",
        in_specs=[pl.BlockSpec((tm, D), lambda i: (i, 0)), pl.BlockSpec((1, D), lambda i: (0, 0)),
                  pl.BlockSpec((D, N), lambda i: (0, 0))],
        out_specs=[pl.BlockSpec((tm, N), lambda i: (i, 0)), pl.BlockSpec((tm, D), lambda i: (i, 0))],
        out_shape=[SDS((T, N), F32), SDS((T, D), MXU_DTYPE)],
        compiler_params=_params(("arbitrary",)),
    )(x2, w_an, w_in_p)


def _mla_pre(proj, cosf, sinf, w_qln, w_kvln, w_uq_p, w_ukv, qnw, knw):
    T = proj.shape[0]
    tm = min(256, T)
    H = MLA_HEADS

    def body(ql_ref, kvl_ref, kpe_ref, cos_ref, sin_ref, wq_ref, wkv_ref, uq_ref, ukv_ref, qnw_ref, knw_ref,
             q_out, k_out, v_out):
        cos, sin = cos_ref[...], sin_ref[...]
        qnw_, knw_ = qnw_ref[...], knw_ref[...]
        qn, _ = _rms(ql_ref[...], wq_ref[...])
        kvn, _ = _rms(kvl_ref[...], wkv_ref[...])
        qraw = _mm(qn, uq_ref[...])
        kvraw = _mm(kvn, ukv_ref[...])
        kpe = _rope(_rms(kpe_ref[...][:, :ROPE], knw_[:, NOPE:])[0], cos, sin)
        for h in range(H):
            qn_h = _rms(qraw[:, h * NOPE:(h + 1) * NOPE], qnw_[:, :NOPE])[0]
            qp_h = _rope(_rms(qraw[:, H * NOPE + h * ROPE:H * NOPE + (h + 1) * ROPE], qnw_[:, NOPE:])[0], cos, sin)
            q_out[h] = jnp.concatenate([qn_h, qp_h], axis=-1).astype(MXU_DTYPE)
            kn_h = _rms(kvraw[:, h * 256:h * 256 + NOPE], knw_[:, :NOPE])[0]
            k_out[h] = jnp.concatenate([kn_h, kpe], axis=-1).astype(MXU_DTYPE)
            v_out[h] = kvraw[:, h * 256 + NOPE:(h + 1) * 256].astype(MXU_DTYPE)

    full = lambda a: pl.BlockSpec(a.shape, lambda i: (0,) * a.ndim)
    return pl.pallas_call(
        body, grid=(T // tm,), name="mla_pre",
        in_specs=[pl.BlockSpec((tm, 256), lambda i: (i, P_QLAT // 256)),
                  pl.BlockSpec((tm, 256), lambda i: (i, P_KVLAT // 256)),
                  pl.BlockSpec((tm, 128), lambda i: (i, P_KPE // 128)),
                  pl.BlockSpec((tm, ROPE), lambda i: (i, 0)), pl.BlockSpec((tm, ROPE), lambda i: (i, 0)),
                  full(w_qln), full(w_kvln), full(w_uq_p), full(w_ukv), full(qnw), full(knw)],
        out_specs=[pl.BlockSpec((H, tm, QK_DIM), lambda i: (0, i, 0)),
                   pl.BlockSpec((H, tm, QK_DIM), lambda i: (0, i, 0)),
                   pl.BlockSpec((H, tm, V_DIM), lambda i: (0, i, 0))],
        out_shape=[SDS((H, T, QK_DIM), MXU_DTYPE), SDS((H, T, QK_DIM), MXU_DTYPE), SDS((H, T, V_DIM), MXU_DTYPE)],
        compiler_params=_params(("arbitrary",)),
    )(proj, proj, proj, cosf, sinf, w_qln, w_kvln, w_uq_p, w_ukv, qnw, knw)


def _attn_fwd(q4, k4, v4, B, S):
    H = MLA_HEADS
    bq = min(256, S)
    nq = S // bq

    def body(q_ref, k_ref, v_ref, o_ref, lse_ref):
        def q_step(qi, carry):
            qs = pl.multiple_of(qi * bq, bq)
            q = q_ref[0, pl.ds(qs, bq), :]
            rows = qs + lax.broadcasted_iota(jnp.int32, (bq, bq), 0)

            def k_step(kj, c):
                m, l, acc = c
                ks = pl.multiple_of(kj * bq, bq)
                k = k_ref[0, pl.ds(ks, bq), :]
                v = v_ref[0, pl.ds(ks, bq), :]
                s = _mm_nt(q, k) * ATT_SCALE
                cols = ks + lax.broadcasted_iota(jnp.int32, (bq, bq), 1)
                s = jnp.where(cols <= rows, s, -jnp.inf)
                m_new = jnp.maximum(m, jnp.max(s, axis=-1, keepdims=True))
                p = jnp.exp(s - m_new)
                a = jnp.exp(m - m_new)
                return m_new, a * l + jnp.sum(p, axis=-1, keepdims=True), a * acc + _mm(p, v)

            m, l, acc = lax.fori_loop(
                0, qi + 1, k_step,
                (jnp.full((bq, 1), -jnp.inf, F32), jnp.zeros((bq, 1), F32), jnp.zeros((bq, V_DIM), F32)))
            o_ref[0, pl.ds(qs, bq), :] = acc / l
            lse_ref[0, pl.ds(qs, bq), :] = m + jnp.log(l)
            return carry

        lax.fori_loop(0, nq, q_step, 0)

    spec = lambda d: pl.BlockSpec((1, S, d), lambda h, b: (h, b, 0))
    return pl.pallas_call(
        body, grid=(H, B), name="attn_fwd",
        in_specs=[spec(QK_DIM), spec(QK_DIM), spec(V_DIM)],
        out_specs=[spec(V_DIM), spec(1)],
        out_shape=[SDS((H, B * S, V_DIM), F32), SDS((H, B * S, 1), F32)],
        compiler_params=_params(("arbitrary", "arbitrary")),
    )(q4, k4, v4)


def _conv_taps(u, halo, w):
    sh = [_shift_down(u, halo, j) for j in range(CONV_W)]
    c = w[0:1] * sh[3] + w[1:2] * sh[2] + w[2:3] * sh[1] + w[3:4] * sh[0]
    return c, sh


def _gate_values(gab, alog_l, dt_l, lane):
    g = -jnp.exp(alog_l) * jax.nn.softplus(gab + dt_l)
    g = jnp.where(lane < GDN_HEADS, g, 0.0)
    beta = jnp.where((lane >= GDN_HEADS) & (lane < 2 * GDN_HEADS), _sigmoid(gab), 0.0)
    return g, beta


def _gdn_pre(proj, conv_w, alog_l, dt_l, S):
    T = proj.shape[0]
    tm = min(256, T)
    tiles_per_seq = S // tm
    C3 = 3 * GDN_WIDTH
    H = GDN_HEADS

    def body(u_ref, halo_ref, gab_ref, w_ref, alog_ref, dt_ref, q_out, k_out, v_out, gates_out):
        i = pl.program_id(0)
        halo = jnp.where(i % tiles_per_seq == 0, 0.0, halo_ref[...])
        c, _ = _conv_taps(u_ref[...], halo, w_ref[...])
        a = c * _sigmoid(c)
        for h in range(H):
            xq = a[:, h * GDN_DIM:(h + 1) * GDN_DIM]
            xk = a[:, GDN_WIDTH + h * GDN_DIM:GDN_WIDTH + (h + 1) * GDN_DIM]
            q_out[h] = xq * lax.rsqrt(jnp.sum(xq * xq, axis=-1, keepdims=True) + EPS) * GDN_QSCALE
            k_out[h] = xk * lax.rsqrt(jnp.sum(xk * xk, axis=-1, keepdims=True) + EPS)
            v_out[h] = a[:, 2 * GDN_WIDTH + h * GDN_DIM:2 * GDN_WIDTH + (h + 1) * GDN_DIM]
        lane = lax.broadcasted_iota(jnp.int32, (tm, LANES), 1)
        ric = lax.broadcasted_iota(jnp.int32, (tm, LANES), 0) % CHUNK
        g, beta = _gate_values(gab_ref[...], alog_ref[...], dt_ref[...], lane)
        gates_out[...] = _chunk_cumsum(g, ric) + beta

    hspec = pl.BlockSpec((H, tm, GDN_DIM), lambda i: (0, i, 0))
    return pl.pallas_call(
        body, grid=(T // tm,), name="gdn_pre",
        in_specs=[pl.BlockSpec((tm, C3), lambda i: (i, 0)),
                  pl.BlockSpec((SUBLANES, C3), lambda i: (jnp.maximum(i * (tm // SUBLANES) - 1, 0), 0)),
                  pl.BlockSpec((tm, LANES), lambda i: (i, P_GAB // LANES)),
                  pl.BlockSpec((CONV_W, C3), lambda i: (0, 0)),
                  pl.BlockSpec((1, LANES), lambda i: (0, 0)), pl.BlockSpec((1, LANES), lambda i: (0, 0))],
        out_specs=[hspec, hspec, hspec, pl.BlockSpec((tm, LANES), lambda i: (i, 0))],
        out_shape=[SDS((H, T, GDN_DIM), F32)] * 3 + [SDS((T, LANES), F32)],
        compiler_params=_params(("arbitrary",)),
    )(proj, proj, proj, conv_w, alog_l, dt_l)


def _unit_lower_inverse(L, eye):
    M = -L
    P = eye + M
    Mk = M
    for _ in range(5):
        Mk = _mm_exact(Mk, Mk)
        P = P + _mm_exact(P, Mk)
    return P


def _chunk_decays(gt, lane, h, ri, ci, rcol):
    Gc = _pick_lane(gt, lane, h)
    bt = _pick_lane(gt, lane, h + GDN_HEADS)
    Gb = jnp.broadcast_to(Gc, (CHUNK, CHUNK))
    Gam = jnp.where(ri >= ci, jnp.exp(Gb - Gb.T), 0.0)
    Gl = jnp.sum(jnp.where(rcol == CHUNK - 1, Gc, 0.0), axis=0, keepdims=True)
    return Gc, bt, Gam, jnp.exp(Gc), jnp.exp(Gl - Gc), jnp.exp(Gl)


def _gdn_fwd(qg, kg, vg, gates, B, S):
    H, D, C = GDN_HEADS, GDN_DIM, CHUNK
    NC = S // C

    def body(q_ref, k_ref, v_ref, g_ref, o_ref, st_ref, ai_ref):
        h = pl.program_id(0)
        lane = lax.broadcasted_iota(jnp.int32, (C, LANES), 1)
        ri = lax.broadcasted_iota(jnp.int32, (C, C), 0)
        ci = lax.broadcasted_iota(jnp.int32, (C, C), 1)
        rcol = lax.broadcasted_iota(jnp.int32, (C, 1), 0)
        eye = (ri == ci).astype(F32)

        def step(n, S_):
            cs = pl.multiple_of(n * C, C)
            q = q_ref[0, pl.ds(cs, C), :]
            k = k_ref[0, pl.ds(cs, C), :]
            v = v_ref[0, pl.ds(cs, C), :]
            Gc, bt, Gam, e, f, eL = _chunk_decays(g_ref[pl.ds(cs, C), :], lane, h, ri, ci, rcol)
            L = jnp.where(ri > ci, bt * _mm_nt(k, k) * Gam, 0.0)
            Ainv = _unit_lower_inverse(L, eye)
            sol = _mm_exact(Ainv, jnp.concatenate([v * bt, k * (bt * e)], axis=-1))
            u, w = sol[:, :D], sol[:, D:]
            At = _mm_nt(q, k) * Gam
            vn = u - _mm(w, S_)
            o_ref[0, pl.ds(cs, C), :] = _mm(q * e, S_) + _mm(At, vn)
            st_ref[0, n] = S_
            ai_ref[0, n] = Ainv
            return S_ * eL + _mm_tn(k * f, vn)

        lax.fori_loop(0, NC, step, jnp.zeros((D, D), F32))

    spec = pl.BlockSpec((1, S, D), lambda h, b: (h, b, 0))
    return pl.pallas_call(
        body, grid=(H, B), name="gdn_fwd",
        in_specs=[spec, spec, spec, pl.BlockSpec((S, LANES), lambda h, b: (b, 0))],
        out_specs=[spec, pl.BlockSpec((1, NC, D, D), lambda h, b: (h, b, 0, 0)),
                   pl.BlockSpec((1, NC, C, C), lambda h, b: (h, b, 0, 0))],
        out_shape=[SDS((H, B * S, D), F32), SDS((H, B * NC, D, D), F32), SDS((H, B * NC, C, C), F32)],
        compiler_params=_params(("arbitrary", "arbitrary")),
    )(qg, kg, vg, gates)


def _mix_out(o_mla, o_gdn, proj, x2, mla_w, gdn_w, w_out):
    T, D = x2.shape
    tm = min(512, T)
    H = MLA_HEADS

    def body(om_ref, og_ref, z_ref, x_ref, mw_ref, gw_ref, w_ref, h_ref, mix_ref):
        z = z_ref[...]
        parts = [_rms(om_ref[h], mw_ref[h:h + 1, :])[0] for h in range(H)]
        for h in range(GDN_HEADS):
            zh = z[:, h * GDN_DIM:(h + 1) * GDN_DIM]
            parts.append(_rms(og_ref[h], gw_ref[...])[0] * (zh * _sigmoid(zh)))
        mix = jnp.concatenate(parts, axis=-1).astype(MXU_DTYPE)
        mix_ref[...] = mix
        h_ref[...] = x_ref[...] + jnp.dot(mix, w_ref[...], preferred_element_type=F32)

    hspec = pl.BlockSpec((H, tm, V_DIM), lambda i: (0, i, 0))
    return pl.pallas_call(
        body, grid=(T // tm,), name="mix_out",
        in_specs=[hspec, hspec, pl.BlockSpec((tm, GDN_WIDTH), lambda i: (i, P_GZ // GDN_WIDTH)),
                  pl.BlockSpec((tm, D), lambda i: (i, 0)),
                  pl.BlockSpec((H, V_DIM), lambda i: (0, 0)), pl.BlockSpec((1, GDN_DIM), lambda i: (0, 0)),
                  pl.BlockSpec((D, D), lambda i: (0, 0))],
        out_specs=[pl.BlockSpec((tm, D), lambda i: (i, 0)), pl.BlockSpec((tm, D), lambda i: (i, 0))],
        out_shape=[SDS((T, D), F32), SDS((T, D), MXU_DTYPE)],
        compiler_params=_params(("arbitrary",)),
    )(o_mla, o_gdn, proj, x2, mla_w, gdn_w, w_out)


def _mlp_fwd(h2, w_mn, w_up, w_down, target):
    T, D = h2.shape
    F = w_up.shape[1]
    tm = min(512, T)
    tf = min(1024, F)
    nf = F // tf

    def body(h_ref, wn_ref, up_w, down_w, t_ref, up_ref, hn_ref, dy_ref, loss_ref, y_acc):
        j = pl.program_id(1)

        @pl.when(j == 0)
        def _():
            hn_ref[...] = _rms(h_ref[...], wn_ref[...])[0].astype(MXU_DTYPE)
            y_acc[...] = h_ref[...]

        up = jnp.dot(hn_ref[...], up_w[...], preferred_element_type=F32)
        up_ref[...] = up
        r = jnp.maximum(up, 0.0)
        y_acc[...] += _mm(r * r, down_w[...])

        @pl.when(j == nf - 1)
        def _():
            err = y_acc[...] - t_ref[...]
            dy_ref[...] = err / D
            loss_ref[...] = jnp.full((1, SUBLANES, LANES), jnp.sum(err * err), F32)

    return pl.pallas_call(
        body, grid=(T // tm, nf), name="mlp_fwd",
        in_specs=[pl.BlockSpec((tm, D), lambda i, j: (i, 0)), pl.BlockSpec((1, D), lambda i, j: (0, 0)),
                  pl.BlockSpec((D, tf), lambda i, j: (0, j)), pl.BlockSpec((tf, D), lambda i, j: (j, 0)),
                  pl.BlockSpec((tm, D), lambda i, j: (i, 0))],
        out_specs=[pl.BlockSpec((tm, tf), lambda i, j: (i, j)), pl.BlockSpec((tm, D), lambda i, j: (i, 0)),
                   pl.BlockSpec((tm, D), lambda i, j: (i, 0)),
                   pl.BlockSpec((1, SUBLANES, LANES), lambda i, j: (i, 0, 0))],
        out_shape=[SDS((T, F), F32), SDS((T, D), MXU_DTYPE), SDS((T, D), F32),
                   SDS((T // tm, SUBLANES, LANES), F32)],
        scratch_shapes=[pltpu.VMEM((tm, D), F32)],
        compiler_params=_params(("arbitrary", "arbitrary")),
    )(h2, w_mn, w_up, w_down, target)


def _mlp_bwd(dy, up, h2, w_mn, w_up, w_down):
    T, D = h2.shape
    F = w_up.shape[1]
    tm = min(512, T)
    tf = min(1024, F)
    nf = F // tf

    def body(dy_ref, up_ref, h_ref, wn_ref, up_w, down_w, dh_ref, dhb_ref, dup_ref, act_ref, dyb_ref, dwn_ref, acc):
        i, j = pl.program_id(0), pl.program_id(1)

        @pl.when((i == 0) & (j == 0))
        def _():
            dwn_ref[...] = jnp.zeros_like(dwn_ref)

        @pl.when(j == 0)
        def _():
            acc[...] = jnp.zeros_like(acc)
            dyb_ref[...] = dy_ref[...].astype(MXU_DTYPE)

        r = jnp.maximum(up_ref[...], 0.0)
        act_ref[...] = (r * r).astype(MXU_DTYPE)
        dup = (_mm_nt(dyb_ref[...], down_w[...]) * (2.0 * r)).astype(MXU_DTYPE)
        dup_ref[...] = dup
        acc[...] += _mm_nt(dup, up_w[...])

        @pl.when(j == nf - 1)
        def _():
            hv = h_ref[...]
            _, rr = _rms(hv, wn_ref[...])
            dx, dw = _rms_bwd(acc[...], hv, wn_ref[...], rr)
            dh = dy_ref[...] + dx
            dh_ref[...] = dh
            dhb_ref[...] = dh.astype(MXU_DTYPE)
            dwn_ref[...] += dw

    row = lambda i, j: (i, 0)
    return pl.pallas_call(
        body, grid=(T // tm, nf), name="mlp_bwd",
        in_specs=[pl.BlockSpec((tm, D), row), pl.BlockSpec((tm, tf), lambda i, j: (i, j)), pl.BlockSpec((tm, D), row),
                  pl.BlockSpec((1, D), lambda i, j: (0, 0)),
                  pl.BlockSpec((D, tf), lambda i, j: (0, j)), pl.BlockSpec((tf, D), lambda i, j: (j, 0))],
        out_specs=[pl.BlockSpec((tm, D), row), pl.BlockSpec((tm, D), row),
                   pl.BlockSpec((tm, tf), lambda i, j: (i, j)), pl.BlockSpec((tm, tf), lambda i, j: (i, j)),
                   pl.BlockSpec((tm, D), row), pl.BlockSpec((1, D), lambda i, j: (0, 0))],
        out_shape=[SDS((T, D), F32), SDS((T, D), MXU_DTYPE), SDS((T, F), MXU_DTYPE), SDS((T, F), MXU_DTYPE),
                   SDS((T, D), MXU_DTYPE), SDS((1, D), F32)],
        scratch_shapes=[pltpu.VMEM((tm, D), F32)],
        compiler_params=_params(("arbitrary", "arbitrary")),
    )(dy, up, h2, w_mn, w_up, w_down)


def _mix_bwd(dhb, o_mla, o_gdn, proj, mla_w, gdn_w, w_out):
    T, D = dhb.shape
    tm = min(512, T)
    H = MLA_HEADS

    def body(dh_ref, om_ref, og_ref, z_ref, mw_ref, gw_ref, w_ref, dom_ref, dog_ref, dz_ref, dmw_ref, dgw_ref):
        @pl.when(pl.program_id(0) == 0)
        def _():
            dmw_ref[...] = jnp.zeros_like(dmw_ref)
            dgw_ref[...] = jnp.zeros_like(dgw_ref)

        dmix = _mm_nt(dh_ref[...], w_ref[...])
        z = z_ref[...]
        dmw, dzs = [], []
        dgw = jnp.zeros((1, GDN_DIM), F32)
        for h in range(H):
            o = om_ref[h]
            w = mw_ref[h:h + 1, :]
            _, r = _rms(o, w)
            dx, dw = _rms_bwd(dmix[:, h * V_DIM:(h + 1) * V_DIM], o, w, r)
            dom_ref[h] = dx
            dmw.append(dw)
        for h in range(GDN_HEADS):
            o = og_ref[h]
            w = gw_ref[...]
            zh = z[:, h * GDN_DIM:(h + 1) * GDN_DIM]
            sg = _sigmoid(zh)
            yn, r = _rms(o, w)
            dy = dmix[:, H * V_DIM + h * GDN_DIM:H * V_DIM + (h + 1) * GDN_DIM]
            dzs.append(dy * yn * (sg * (1.0 + zh * (1.0 - sg))))
            dx, dw = _rms_bwd(dy * (zh * sg), o, w, r)
            dog_ref[h] = dx
            dgw = dgw + dw
        dz_ref[...] = jnp.concatenate(dzs, axis=-1)
        dmw_ref[...] += jnp.concatenate(dmw, axis=0)
        dgw_ref[...] += dgw

    hspec = pl.BlockSpec((H, tm, V_DIM), lambda i: (0, i, 0))
    return pl.pallas_call(
        body, grid=(T // tm,), name="mix_bwd",
        in_specs=[pl.BlockSpec((tm, D), lambda i: (i, 0)), hspec, hspec,
                  pl.BlockSpec((tm, GDN_WIDTH), lambda i: (i, P_GZ // GDN_WIDTH)),
                  pl.BlockSpec((H, V_DIM), lambda i: (0, 0)), pl.BlockSpec((1, GDN_DIM), lambda i: (0, 0)),
                  pl.BlockSpec((D, D), lambda i: (0, 0))],
        out_specs=[hspec, hspec, pl.BlockSpec((tm, GDN_WIDTH), lambda i: (i, 0)),
                   pl.BlockSpec((H, V_DIM), lambda i: (0, 0)), pl.BlockSpec((1, GDN_DIM), lambda i: (0, 0))],
        out_shape=[SDS((H, T, V_DIM), F32), SDS((H, T, GDN_DIM), F32), SDS((T, GDN_WIDTH), F32),
                   SDS((H, V_DIM), F32), SDS((1, GDN_DIM), F32)],
        compiler_params=_params(("arbitrary",)),
    )(dhb, o_mla, o_gdn, proj, mla_w, gdn_w, w_out)


def _attn_bwd(q4, k4, v4, do4, o4, lse4, B, S):
    H = MLA_HEADS
    bq = min(256, S)
    nq = S // bq

    def body(q_ref, k_ref, v_ref, do_ref, o_ref, lse_ref, dq_ref, dk_ref, dv_ref, delta):
        dq_ref[...] = jnp.zeros_like(dq_ref)
        dk_ref[...] = jnp.zeros_like(dk_ref)
        dv_ref[...] = jnp.zeros_like(dv_ref)
        delta[...] = jnp.sum(do_ref[0] * o_ref[0], axis=-1, keepdims=True)

        def k_step(kj, carry):
            ks = pl.multiple_of(kj * bq, bq)
            k = k_ref[0, pl.ds(ks, bq), :]
            v = v_ref[0, pl.ds(ks, bq), :]
            cols = ks + lax.broadcasted_iota(jnp.int32, (bq, bq), 1)

            def q_step(qi, c):
                qs = pl.multiple_of(qi * bq, bq)
                q = q_ref[0, pl.ds(qs, bq), :]
                do = do_ref[0, pl.ds(qs, bq), :].astype(MXU_DTYPE)
                rows = qs + lax.broadcasted_iota(jnp.int32, (bq, bq), 0)
                s = _mm_nt(q, k) * ATT_SCALE
                p = jnp.where(cols <= rows, jnp.exp(s - lse_ref[0, pl.ds(qs, bq), :]), 0.0)
                dv_ref[0, pl.ds(ks, bq), :] += _mm_tn(p, do)
                ds = p * (_mm_nt(do, v) - delta[pl.ds(qs, bq), :]) * ATT_SCALE
                dq_ref[0, pl.ds(qs, bq), :] += _mm(ds, k)
                dk_ref[0, pl.ds(ks, bq), :] += _mm_tn(ds, q)
                return c

            lax.fori_loop(kj, nq, q_step, 0)
            return carry

        lax.fori_loop(0, nq, k_step, 0)

    spec = lambda d: pl.BlockSpec((1, S, d), lambda h, b: (h, b, 0))
    return pl.pallas_call(
        body, grid=(H, B), name="attn_bwd",
        in_specs=[spec(QK_DIM), spec(QK_DIM), spec(V_DIM), spec(V_DIM), spec(V_DIM), spec(1)],
        out_specs=[spec(QK_DIM), spec(QK_DIM), spec(V_DIM)],
        out_shape=[SDS((H, B * S, QK_DIM), F32), SDS((H, B * S, QK_DIM), F32), SDS((H, B * S, V_DIM), F32)],
        scratch_shapes=[pltpu.VMEM((S, 1), F32)],
        compiler_params=_params(("arbitrary", "arbitrary")),
    )(q4, k4, v4, do4, o4, lse4)


def _gdn_bwd(qg, kg, vg, gates, states, ainv, do4, B, S):
    H, D, C = GDN_HEADS, GDN_DIM, CHUNK
    NC = S // C

    def body(q_ref, k_ref, v_ref, g_ref, st_ref, ai_ref, do_ref, dq_ref, dk_ref, dv_ref, dgb_ref):
        h = pl.program_id(0)
        lane = lax.broadcasted_iota(jnp.int32, (C, LANES), 1)
        ri = lax.broadcasted_iota(jnp.int32, (C, C), 0)
        ci = lax.broadcasted_iota(jnp.int32, (C, C), 1)
        rcol = lax.broadcasted_iota(jnp.int32, (C, 1), 0)

        def rsum(a):
            return jnp.sum(a, axis=-1, keepdims=True)

        def step(t, dS):
            n = NC - 1 - t
            cs = pl.multiple_of(n * C, C)
            q = q_ref[0, pl.ds(cs, C), :]
            k = k_ref[0, pl.ds(cs, C), :]
            v = v_ref[0, pl.ds(cs, C), :]
            do = do_ref[0, pl.ds(cs, C), :]
            Gc, bt, Gam, e, f, eL = _chunk_decays(g_ref[pl.ds(cs, C), :], lane, h, ri, ci, rcol)
            S0 = st_ref[0, n]
            Ainv = ai_ref[0, n]
            KK = _mm_nt(k, k)
            QK = _mm_nt(q, k)
            be = bt * e
            sol = _mm_exact(Ainv, jnp.concatenate([v * bt, k * be], axis=-1))
            u, w = sol[:, :D], sol[:, D:]
            At = QK * Gam
            qd = q * e
            kd = k * f
            vn = u - _mm(w, S0)
            dvn = _mm_tn(At, do) + _mm(kd, dS)
            dAt = jnp.where(ri >= ci, _mm_nt(do, vn), 0.0)
            dqd = _mm_nt(do, S0)
            dS0 = _mm_tn(qd, do) + eL * dS - _mm_tn(w, dvn)
            dw = -_mm_nt(dvn, S0)
            dkd = _mm_nt(vn, dS)
            deL = jnp.sum(rsum(dS * S0), axis=0, keepdims=True)
            dR = _mm_exact(Ainv.T, jnp.concatenate([dvn, dw], axis=-1))
            dR1, dR2 = dR[:, :D], dR[:, D:]
            dL = jnp.where(ri > ci, -_mm_nt(dR, sol), 0.0)
            dv_ref[0, pl.ds(cs, C), :] = dR1 * bt
            r2 = rsum(dR2 * k)
            X = dL * Gam
            dbt = rsum(dR1 * v) + r2 * e + rsum(X * KK)
            de = r2 * bt + rsum(dqd * q)
            dKK = X * bt
            dQK = dAt * Gam
            dq_ref[0, pl.ds(cs, C), :] = _mm(dQK, k) + dqd * e
            dk_ref[0, pl.ds(cs, C), :] = dR2 * be + _mm(dKK + dKK.T, k) + _mm_tn(dQK, q) + dkd * f
            df = rsum(dkd * k)
            Z = (dL * (bt * KK) + dAt * QK) * Gam
            dG = rsum(Z) - rsum(Z.T) + de * e - df * f
            dGl = jnp.sum(df * f, axis=0, keepdims=True) + deL * eL
            dG = dG + jnp.where(rcol == C - 1, dGl, 0.0)
            dgb_ref[0, pl.ds(cs, C), :] = jnp.where(lane == 0, dG, jnp.where(lane == 1, dbt, 0.0))
            return dS0

        lax.fori_loop(0, NC, step, jnp.zeros((D, D), F32))

    spec = pl.BlockSpec((1, S, D), lambda h, b: (h, b, 0))
    return pl.pallas_call(
        body, grid=(H, B), name="gdn_bwd",
        in_specs=[spec, spec, spec, pl.BlockSpec((S, LANES), lambda h, b: (b, 0)),
                  pl.BlockSpec((1, NC, D, D), lambda h, b: (h, b, 0, 0)),
                  pl.BlockSpec((1, NC, C, C), lambda h, b: (h, b, 0, 0)), spec],
        out_specs=[spec, spec, spec, spec],
        out_shape=[SDS((H, B * S, D), F32)] * 4,
        compiler_params=_params(("arbitrary", "arbitrary")),
    )(qg, kg, vg, gates, states, ainv, do4)


def _gdn_pre_bwd(proj, conv_w, alog_l, dt_l, dq4, dk4, dv4, dgb4, S):
    T = proj.shape[0]
    tm = min(256, T)
    tiles_per_seq = S // tm
    C3 = 3 * GDN_WIDTH
    H = GDN_HEADS

    def body(u_ref, halo_ref, gab_ref, w_ref, alog_ref, dt_ref, dq_ref, dk_ref, dv_ref, dgb_ref,
             dc_ref, dgab_ref, dcw_ref, dalog_ref, ddt_ref):
        i = pl.program_id(0)

        @pl.when(i == 0)
        def _():
            dcw_ref[...] = jnp.zeros_like(dcw_ref)
            dalog_ref[...] = jnp.zeros_like(dalog_ref)
            ddt_ref[...] = jnp.zeros_like(ddt_ref)

        halo = jnp.where(i % tiles_per_seq == 0, 0.0, halo_ref[...])
        c, sh = _conv_taps(u_ref[...], halo, w_ref[...])
        sg = _sigmoid(c)
        a = c * sg
        das = [None] * (3 * H)
        for h in range(H):
            xq = a[:, h * GDN_DIM:(h + 1) * GDN_DIM]
            xk = a[:, GDN_WIDTH + h * GDN_DIM:GDN_WIDTH + (h + 1) * GDN_DIM]
            das[h] = _l2n_bwd(dq_ref[h], xq, GDN_QSCALE)
            das[H + h] = _l2n_bwd(dk_ref[h], xk, 1.0)
            das[2 * H + h] = dv_ref[h]
        dc = jnp.concatenate(das, axis=-1) * (sg * (1.0 + c * (1.0 - sg)))
        dc_ref[...] = dc
        dcw_ref[...] += jnp.concatenate(
            [jnp.sum(dc * sh[CONV_W - 1 - t], axis=0, keepdims=True) for t in range(CONV_W)], axis=0)
        lane = lax.broadcasted_iota(jnp.int32, (tm, LANES), 1)
        ric = lax.broadcasted_iota(jnp.int32, (tm, LANES), 0) % CHUNK
        dG = jnp.zeros((tm, LANES), F32)
        for h in range(H):
            t = dgb_ref[h]
            dG = dG + jnp.where(lane == h, _pick_lane(t, lane, 0), 0.0) \
                    + jnp.where(lane == h + H, _pick_lane(t, lane, 1), 0.0)
        is_g = lane < H
        dg = jnp.where(is_g, _chunk_rev_cumsum(jnp.where(is_g, dG, 0.0), ric), 0.0)
        gab = gab_ref[...]
        g, beta = _gate_values(gab, alog_ref[...], dt_ref[...], lane)
        dga = jnp.where(is_g, dg * (-jnp.exp(alog_ref[...])) * _sigmoid(gab + dt_ref[...]), 0.0)
        dgb = jnp.where(is_g, 0.0, dG) * beta * (1.0 - beta)
        dgab_ref[...] = dga + dgb
        dalog_ref[...] += jnp.sum(dg * g, axis=0, keepdims=True)
        ddt_ref[...] += jnp.sum(dga, axis=0, keepdims=True)

    hspec = pl.BlockSpec((H, tm, GDN_DIM), lambda i: (0, i, 0))
    vec = pl.BlockSpec((1, LANES), lambda i: (0, 0))
    return pl.pallas_call(
        body, grid=(T // tm,), name="gdn_pre_bwd",
        in_specs=[pl.BlockSpec((tm, C3), lambda i: (i, 0)),
                  pl.BlockSpec((SUBLANES, C3), lambda i: (jnp.maximum(i * (tm // SUBLANES) - 1, 0), 0)),
                  pl.BlockSpec((tm, LANES), lambda i: (i, P_GAB // LANES)),
                  pl.BlockSpec((CONV_W, C3), lambda i: (0, 0)), vec, vec, hspec, hspec, hspec, hspec],
        out_specs=[pl.BlockSpec((tm, C3), lambda i: (i, 0)), pl.BlockSpec((tm, LANES), lambda i: (i, 0)),
                   pl.BlockSpec((CONV_W, C3), lambda i: (0, 0)), vec, vec],
        out_shape=[SDS((T, C3), F32), SDS((T, LANES), F32), SDS((CONV_W, C3), F32),
                   SDS((1, LANES), F32), SDS((1, LANES), F32)],
        compiler_params=_params(("arbitrary",)),
    )(proj, proj, proj, conv_w, alog_l, dt_l, dq4, dk4, dv4, dgb4)


def _conv_bwd_input(dc, conv_w, S):
    T, C3 = dc.shape
    tm = min(256, T)
    tiles_per_seq = S // tm
    nblk = T // SUBLANES

    def body(dc_ref, nxt_ref, w_ref, du_ref):
        i = pl.program_id(0)
        nxt = jnp.where(i % tiles_per_seq == tiles_per_seq - 1, 0.0, nxt_ref[...])
        x = dc_ref[...]
        w = w_ref[...]
        du = w[3:4] * x
        for j in range(1, CONV_W):
            du = du + w[3 - j:4 - j] * _shift_up(x, nxt, j)
        du_ref[...] = du

    return pl.pallas_call(
        body, grid=(T // tm,), name="conv_bwd_input",
        in_specs=[pl.BlockSpec((tm, C3), lambda i: (i, 0)),
                  pl.BlockSpec((SUBLANES, C3), lambda i: (jnp.minimum((i + 1) * (tm // SUBLANES), nblk - 1), 0)),
                  pl.BlockSpec((CONV_W, C3), lambda i: (0, 0))],
        out_specs=pl.BlockSpec((tm, C3), lambda i: (i, 0)),
        out_shape=SDS((T, C3), F32),
        compiler_params=_params(("arbitrary",)),
    )(dc, dc, conv_w)


def _mla_pre_bwd(proj, cosf, sinf, w_qln, w_kvln, w_uq_p, w_ukv, qnw, knw, dq4, dk4, dv4):
    T = proj.shape[0]
    tm = min(256, T)
    H = MLA_HEADS

    def body(ql_ref, kvl_ref, kpe_ref, cos_ref, sin_ref, wq_ref, wkv_ref, uq_ref, ukv_ref, qnw_ref, knw_ref,
             dq_ref, dk_ref, dv_ref,
             dql_ref, dkvl_ref, dkpe_ref, dqraw_ref, dkvraw_ref, qn_ref, kvn_ref, dwq_ref, dwkv_ref, dqnw_ref, dknw_ref):
        @pl.when(pl.program_id(0) == 0)
        def _():
            for r in (dwq_ref, dwkv_ref, dqnw_ref, dknw_ref):
                r[...] = jnp.zeros_like(r)

        cos, sin = cos_ref[...], sin_ref[...]
        qnw_, knw_ = qnw_ref[...], knw_ref[...]
        ql, kvl = ql_ref[...], kvl_ref[...]
        kpe_raw = kpe_ref[...][:, :ROPE]
        qn, rq = _rms(ql, wq_ref[...])
        kvn, rkv = _rms(kvl, wkv_ref[...])
        qn_ref[...] = qn.astype(MXU_DTYPE)
        kvn_ref[...] = kvn.astype(MXU_DTYPE)
        qraw = _mm(qn, uq_ref[...])
        kvraw = _mm(kvn, ukv_ref[...])
        dq_nope, dq_pe, dkv_parts = [], [], []
        dqnw_n = jnp.zeros((1, NOPE), F32)
        dqnw_p = jnp.zeros((1, ROPE), F32)
        dknw_n = jnp.zeros((1, NOPE), F32)
        dkpe = jnp.zeros((tm, ROPE), F32)
        for h in range(H):
            dq = dq_ref[h]
            x = qraw[:, h * NOPE:(h + 1) * NOPE]
            dx, dw = _rms_bwd(dq[:, :NOPE], x, qnw_[:, :NOPE], _rms(x, qnw_[:, :NOPE])[1])
            dq_nope.append(dx)
            dqnw_n = dqnw_n + dw
            x = qraw[:, H * NOPE + h * ROPE:H * NOPE + (h + 1) * ROPE]
            dx, dw = _rms_bwd(_rope_bwd(dq[:, NOPE:], cos, sin), x, qnw_[:, NOPE:], _rms(x, qnw_[:, NOPE:])[1])
            dq_pe.append(dx)
            dqnw_p = dqnw_p + dw
            dk = dk_ref[h]
            x = kvraw[:, h * 256:h * 256 + NOPE]
            dx, dw = _rms_bwd(dk[:, :NOPE], x, knw_[:, :NOPE], _rms(x, knw_[:, :NOPE])[1])
            dknw_n = dknw_n + dw
            dkpe = dkpe + dk[:, NOPE:]
            dkv_parts += [dx, dv_ref[h]]
        dx, dknw_p = _rms_bwd(_rope_bwd(dkpe, cos, sin), kpe_raw, knw_[:, NOPE:], _rms(kpe_raw, knw_[:, NOPE:])[1])
        dkpe_ref[...] = jnp.concatenate([dx, jnp.zeros((tm, LANES - ROPE), F32)], axis=-1)
        dqraw = jnp.concatenate(dq_nope + dq_pe, axis=-1).astype(MXU_DTYPE)
        dkvraw = jnp.concatenate(dkv_parts, axis=-1).astype(MXU_DTYPE)
        dqraw_ref[...] = dqraw
        dkvraw_ref[...] = dkvraw
        dx, dw = _rms_bwd(_mm_nt(dqraw, uq_ref[...]), ql, wq_ref[...], rq)
        dql_ref[...] = dx
        dwq_ref[...] += dw
        dx, dw = _rms_bwd(_mm_nt(dkvraw, ukv_ref[...]), kvl, wkv_ref[...], rkv)
        dkvl_ref[...] = dx
        dwkv_ref[...] += dw
        dqnw_ref[...] += jnp.concatenate([dqnw_n, dqnw_p], axis=-1)
        dknw_ref[...] += jnp.concatenate([dknw_n, dknw_p], axis=-1)

    full = lambda a: pl.BlockSpec(a.shape, lambda i: (0,) * a.ndim)
    rows = lambda n: pl.BlockSpec((tm, n), lambda i: (i, 0))
    const = lambda n: pl.BlockSpec((1, n), lambda i: (0, 0))
    NQ, NKV = w_uq_p.shape[1], w_ukv.shape[1]
    return pl.pallas_call(
        body, grid=(T // tm,), name="mla_pre_bwd",
        in_specs=[pl.BlockSpec((tm, 256), lambda i: (i, P_QLAT // 256)),
                  pl.BlockSpec((tm, 256), lambda i: (i, P_KVLAT // 256)),
                  pl.BlockSpec((tm, 128), lambda i: (i, P_KPE // 128)),
                  rows(ROPE), rows(ROPE),
                  full(w_qln), full(w_kvln), full(w_uq_p), full(w_ukv), full(qnw), full(knw),
                  pl.BlockSpec((H, tm, QK_DIM), lambda i: (0, i, 0)),
                  pl.BlockSpec((H, tm, QK_DIM), lambda i: (0, i, 0)),
                  pl.BlockSpec((H, tm, V_DIM), lambda i: (0, i, 0))],
        out_specs=[rows(Q_LORA), rows(KV_LORA), rows(LANES), rows(NQ), rows(NKV), rows(Q_LORA), rows(KV_LORA),
                   const(Q_LORA), const(KV_LORA), const(QK_DIM), const(QK_DIM)],
        out_shape=[SDS((T, Q_LORA), F32), SDS((T, KV_LORA), F32), SDS((T, LANES), F32),
                   SDS((T, NQ), MXU_DTYPE), SDS((T, NKV), MXU_DTYPE),
                   SDS((T, Q_LORA), MXU_DTYPE), SDS((T, KV_LORA), MXU_DTYPE),
                   SDS((1, Q_LORA), F32), SDS((1, KV_LORA), F32), SDS((1, QK_DIM), F32), SDS((1, QK_DIM), F32)],
        compiler_params=_params(("arbitrary",)),
    )(proj, proj, proj, cosf, sinf, w_qln, w_kvln, w_uq_p, w_ukv, qnw, knw, dq4, dk4, dv4)


def _in_proj_bwd(dgqkv, dgz, dql, dkvl, dkpe, dgab, w_in_p, dh, x2, w_an):
    T, D = x2.shape
    N = w_in_p.shape[1]
    tm = min(512, T)

    def body(a_ref, b_ref, c_ref, d_ref, e_ref, f_ref, w_ref, dh_ref, x_ref, wn_ref, dx_ref, dp_ref, dwn_ref):
        @pl.when(pl.program_id(0) == 0)
        def _():
            dwn_ref[...] = jnp.zeros_like(dwn_ref)

        dp = jnp.concatenate([a_ref[...], b_ref[...], c_ref[...], d_ref[...], e_ref[...], f_ref[...]],
                             axis=-1).astype(MXU_DTYPE)
        dp_ref[...] = dp
        x = x_ref[...]
        _, r = _rms(x, wn_ref[...])
        dx, dw = _rms_bwd(_mm_nt(dp, w_ref[...]), x, wn_ref[...], r)
        dx_ref[...] = dh_ref[...] + dx
        dwn_ref[...] += dw

    rows = lambda n: pl.BlockSpec((tm, n), lambda i: (i, 0))
    return pl.pallas_call(
        body, grid=(T // tm,), name="in_proj_bwd",
        in_specs=[rows(dgqkv.shape[1]), rows(dgz.shape[1]), rows(dql.shape[1]), rows(dkvl.shape[1]),
                  rows(dkpe.shape[1]), rows(dgab.shape[1]),
                  pl.BlockSpec((D, N), lambda i: (0, 0)), rows(D), rows(D), pl.BlockSpec((1, D), lambda i: (0, 0))],
        out_specs=[rows(D), rows(N), pl.BlockSpec((1, D), lambda i: (0, 0))],
        out_shape=[SDS((T, D), F32), SDS((T, N), MXU_DTYPE), SDS((1, D), F32)],
        compiler_params=_params(("arbitrary",)),
    )(dgqkv, dgz, dql, dkvl, dkpe, dgab, w_in_p, dh, x2, w_an)


def _wgrad(a, b, name):
    T, M = a.shape
    N = b.shape[1]
    tM = _divisor_tile(M, 512)
    tN = _divisor_tile(N, 1536)
    tk = min(T, 1024)

    def body(a_ref, b_ref, o_ref):
        @pl.when(pl.program_id(2) == 0)
        def _():
            o_ref[...] = jnp.zeros_like(o_ref)

        o_ref[...] += _mm_tn(a_ref[...], b_ref[...])

    return pl.pallas_call(
        body, grid=(M // tM, N // tN, T // tk), name=name,
        in_specs=[pl.BlockSpec((tk, tM), lambda i, j, k: (k, i)), pl.BlockSpec((tk, tN), lambda i, j, k: (k, j))],
        out_specs=pl.BlockSpec((tM, tN), lambda i, j, k: (i, j)),
        out_shape=SDS((M, N), F32),
        compiler_params=_params(("arbitrary", "arbitrary", "arbitrary")),
    )(a, b)


def _reduce_adamw(parts, w, m, v, name):
    R = w.shape[0]
    tr = R if R <= FLAT_ROWS else FLAT_ROWS
    c1 = 1.0 - ADAM_B1 ** ADAM_STEP
    c2 = 1.0 - ADAM_B2 ** ADAM_STEP

    def body(p_ref, w_ref, m_ref, v_ref, g_ref, d_ref, nm_ref, nv_ref):
        g = p_ref[0]
        for s in range(1, N_DEV):
            g = g + p_ref[s]
        g_ref[...] = g
        mm = ADAM_B1 * m_ref[...] + (1.0 - ADAM_B1) * g
        vv = ADAM_B2 * v_ref[...] + (1.0 - ADAM_B2) * jnp.square(g)
        nm_ref[...] = mm
        nv_ref[...] = vv
        d_ref[...] = -ADAM_LR * ((mm / c1) / (jnp.sqrt(vv / c2) + ADAM_EPS) + ADAM_WD * w_ref[...])

    spec = pl.BlockSpec((tr, LANES), lambda i: (i, 0))
    return pl.pallas_call(
        body, grid=(R // tr,), name=name,
        in_specs=[pl.BlockSpec((N_DEV, tr, LANES), lambda i: (0, i, 0)), spec, spec, spec],
        out_specs=[spec] * 4, out_shape=[SDS((R, LANES), F32)] * 4,
        compiler_params=_params(("arbitrary",)),
    )(parts, w, m, v)


def _all_gather(x, name):
    R, L = x.shape

    def body(x_ref, out_ref, send_sems, recv_sems, local_sem):
        mx, my, mc = lax.axis_index("x"), lax.axis_index("y"), lax.axis_index("c")
        me, sibling = (mx, my, mc), (mx, my, 1 - mc)
        chips = [(1 - mx, my), (mx, 1 - my), (1 - mx, 1 - my)]

        def slot(px, py, pc):
            return out_ref.at[4 * px + 2 * py + pc]

        def copy(k, block, to, src=None):
            return pltpu.make_async_remote_copy(
                src_ref=slot(*block) if src is None else src, dst_ref=slot(*block),
                send_sem=send_sems.at[k], recv_sem=recv_sems.at[k], device_id=to, device_id_type=MESH_ID)

        mine = pltpu.make_async_copy(x_ref, slot(*me), local_sem)
        mine.start()
        first = [copy(0, me, sibling, src=x_ref)]
        first += [copy(1 + j, me, (*chip, mc), src=x_ref) for j, chip in enumerate(chips)]
        for cp in first:
            cp.start()
        passed = [copy(4 + j, (*chip, mc), sibling) for j, chip in enumerate(chips)]
        for j, chip in enumerate(chips):
            copy(1 + j, (*chip, mc), me).wait_recv()
            passed[j].start()
        copy(0, sibling, me).wait_recv()
        for j, chip in enumerate(chips):
            copy(4 + j, (*chip, 1 - mc), me).wait_recv()
        for cp in first + passed:
            cp.wait_send()
        mine.wait()

    return pl.pallas_call(
        body, name=name,
        out_shape=SDS((N_DEV, R, L), x.dtype),
        in_specs=[pl.BlockSpec(memory_space=pl.ANY)],
        out_specs=pl.BlockSpec(memory_space=pl.ANY),
        scratch_shapes=[pltpu.SemaphoreType.DMA((7,)), pltpu.SemaphoreType.DMA((7,)), pltpu.SemaphoreType.DMA],
    )(x)


def _grad_exchange(g8, name):
    _, R, L = g8.shape
    flips = [(0, 0, 1), (1, 0, 0), (0, 1, 0), (1, 1, 0), (1, 0, 1), (0, 1, 1), (1, 1, 1)]

    def body(g_ref, out_ref, send_sems, recv_sems, local_sem):
        mx, my, mc = lax.axis_index("x"), lax.axis_index("y"), lax.axis_index("c")
        mine = pltpu.make_async_copy(g_ref.at[4 * mx + 2 * my + mc], out_ref.at[N_DEV - 1], local_sem)
        mine.start()
        copies = []
        for k, (fx, fy, fc) in enumerate(flips):
            px = 1 - mx if fx else mx
            py = 1 - my if fy else my
            pc = 1 - mc if fc else mc
            copies.append(pltpu.make_async_remote_copy(
                src_ref=g_ref.at[4 * px + 2 * py + pc], dst_ref=out_ref.at[k],
                send_sem=send_sems.at[k], recv_sem=recv_sems.at[k],
                device_id=(px, py, pc), device_id_type=MESH_ID))
        for cp in copies:
            cp.start()
        for cp in copies:
            cp.wait()
        mine.wait()

    return pl.pallas_call(
        body, name=name,
        out_shape=SDS((N_DEV, R, L), g8.dtype),
        in_specs=[pl.BlockSpec(memory_space=pl.ANY)],
        out_specs=pl.BlockSpec(memory_space=pl.ANY),
        scratch_shapes=[pltpu.SemaphoreType.DMA((7,)), pltpu.SemaphoreType.DMA((7,)), pltpu.SemaphoreType.DMA],
    )(g8)


def _w_in_to_padded(w):
    z = lambda n: jnp.zeros((w.shape[0], n), w.dtype)
    return jnp.concatenate([w[:, O_GQKV:O_GZ], w[:, O_GZ:O_GAB], w[:, O_QLAT:O_KVLAT], w[:, O_KVLAT:O_KPE],
                            w[:, O_KPE:O_GQKV], z(P_GAB - P_KPE - ROPE), w[:, O_GAB:O_END],
                            z(P_WIDTH - P_GAB - (O_END - O_GAB))], axis=1)


def _w_in_from_padded(wp):
    return jnp.concatenate([wp[:, P_QLAT:P_QLAT + 256], wp[:, P_KVLAT:P_KVLAT + 256], wp[:, P_KPE:P_KPE + ROPE],
                            wp[:, P_GQKV:P_GZ], wp[:, P_GZ:P_QLAT], wp[:, P_GAB:P_GAB + (O_END - O_GAB)]], axis=1)


def _w_uq_to_headsplit(w):
    w3 = w.reshape(w.shape[0], MLA_HEADS, QK_DIM)
    return jnp.concatenate([w3[:, :, :NOPE].reshape(w.shape[0], -1), w3[:, :, NOPE:].reshape(w.shape[0], -1)], axis=1)


def _w_uq_from_headsplit(wp):
    n = wp[:, :MLA_HEADS * NOPE].reshape(wp.shape[0], MLA_HEADS, NOPE)
    p = wp[:, MLA_HEADS * NOPE:].reshape(wp.shape[0], MLA_HEADS, ROPE)
    return jnp.concatenate([n, p], axis=2).reshape(wp.shape[0], -1)


def _lane_vec(v4):
    return jnp.pad(v4.reshape(1, -1), ((0, 0), (0, LANES - v4.shape[-1])))


def _local_step(x, positions, target, attn_norm_w, w_in, q_lat_norm_w, w_uq, kv_lat_norm_w, w_ukv, q_norm_w,
                k_norm_w, mla_out_norm_w, conv_w, a_log, dt_bias, gdn_norm_w, w_out, mlp_norm_w, w_up, w_down):
    B, S, D = x.shape
    T = B * S
    x2 = x.reshape(T, D)
    t2 = target.reshape(T, D)
    half = ROPE // 2
    inv_freq = ROPE_THETA ** (-jnp.arange(half, dtype=F32) / half)
    ang = positions.reshape(T, 1).astype(F32) * inv_freq
    cosf = jnp.concatenate([jnp.cos(ang)] * 2, axis=-1)
    sinf = jnp.concatenate([jnp.sin(ang)] * 2, axis=-1)
    w_in_p = _w_in_to_padded(w_in)
    w_uq_p = _w_uq_to_headsplit(w_uq)
    alog_l, dt_l = _lane_vec(a_log), _lane_vec(dt_bias)
    w_an, w_qln, w_kvln = attn_norm_w.reshape(1, -1), q_lat_norm_w.reshape(1, -1), kv_lat_norm_w.reshape(1, -1)
    qnw, knw, w_mn = q_norm_w.reshape(1, -1), k_norm_w.reshape(1, -1), mlp_norm_w.reshape(1, -1)
    gdn_w = gdn_norm_w.reshape(1, -1)

    proj, xn = _in_proj(x2, w_an, w_in_p)
    q4, k4, v4 = _mla_pre(proj, cosf, sinf, w_qln, w_kvln, w_uq_p, w_ukv, qnw, knw)
    o_mla, lse = _attn_fwd(q4, k4, v4, B, S)
    qg, kg, vg, gates = _gdn_pre(proj, conv_w, alog_l, dt_l, S)
    o_gdn, states, ainv = _gdn_fwd(qg, kg, vg, gates, B, S)
    h2, mix = _mix_out(o_mla, o_gdn, proj, x2, mla_out_norm_w, gdn_w, w_out)
    up, hn, dy, sq = _mlp_fwd(h2, w_mn, w_up, w_down, t2)
    loss = (0.5 / D) * jnp.sum(sq[:, 0, 0])

    dh, dhb, dup, act, dyb, d_mlp_norm = _mlp_bwd(dy, up, h2, w_mn, w_up, w_down)
    g_w_down = _wgrad(act, dyb, "wgrad_down")
    g_w_up = _wgrad(hn, dup, "wgrad_up")
    do_mla, do_gdn, dz, d_mla_w, d_gdn_w = _mix_bwd(dhb, o_mla, o_gdn, proj, mla_out_norm_w, gdn_w, w_out)
    g_w_out = _wgrad(mix, dhb, "wgrad_out")
    dq4, dk4, dv4 = _attn_bwd(q4, k4, v4, do_mla, o_mla, lse, B, S)
    dql, dkvl, dkpe, dqraw, dkvraw, qn, kvn, d_wqln, d_wkvln, d_qnw, d_knw = _mla_pre_bwd(
        proj, cosf, sinf, w_qln, w_kvln, w_uq_p, w_ukv, qnw, knw, dq4, dk4, dv4)
    g_w_uq = _w_uq_from_headsplit(_wgrad(qn, dqraw, "wgrad_uq"))
    g_w_ukv = _wgrad(kvn, dkvraw, "wgrad_ukv")
    dqg, dkg, dvg, dgb4 = _gdn_bwd(qg, kg, vg, gates, states, ainv, do_gdn, B, S)
    dc, dgab, g_conv, d_alog, d_dt = _gdn_pre_bwd(proj, conv_w, alog_l, dt_l, dqg, dkg, dvg, dgb4, S)
    dgqkv = _conv_bwd_input(dc, conv_w, S)
    grad_x2, dproj, d_attn_norm = _in_proj_bwd(dgqkv, dz, dql, dkvl, dkpe, dgab, w_in_p, dh, x2, w_an)
    g_w_in = _w_in_from_padded(_wgrad(xn, dproj, "wgrad_in"))

    grads = dict(
        attn_norm_w=d_attn_norm.reshape(-1), w_in=g_w_in, q_lat_norm_w=d_wqln.reshape(-1), w_uq=g_w_uq,
        kv_lat_norm_w=d_wkvln.reshape(-1), w_ukv=g_w_ukv, q_norm_w=d_qnw.reshape(-1), k_norm_w=d_knw.reshape(-1),
        mla_out_norm_w=d_mla_w, conv_w=g_conv, a_log=d_alog[0, :GDN_HEADS], dt_bias=d_dt[0, :GDN_HEADS],
        gdn_norm_w=d_gdn_w.reshape(-1), w_out=g_w_out, mlp_norm_w=d_mlp_norm.reshape(-1), w_up=g_w_up,
        w_down=g_w_down)
    return loss, grad_x2.reshape(B, S, D), grads


BIG = ("w_in", "w_uq", "w_ukv", "conv_w", "w_out", "w_up", "w_down")
SMALL = ("attn_norm_w", "q_lat_norm_w", "kv_lat_norm_w", "q_norm_w", "k_norm_w", "mla_out_norm_w", "a_log",
         "dt_bias", "gdn_norm_w", "mlp_norm_w")
ALL_W = ("attn_norm_w", "w_in", "q_lat_norm_w", "w_uq", "kv_lat_norm_w", "w_ukv", "q_norm_w", "k_norm_w",
         "mla_out_norm_w", "conv_w", "a_log", "dt_bias", "gdn_norm_w", "w_out", "mlp_norm_w", "w_up", "w_down")
COL_SHARDED = ("w_in", "w_uq", "w_ukv", "conv_w", "w_up")


def _flat_rows(arrs, row_unit):
    flat = jnp.concatenate([a.reshape(-1) for a in arrs])
    n = flat.shape[0]
    unit = row_unit * LANES
    total = -(-n // unit) * unit
    return jnp.pad(flat, (0, total - n)).reshape(-1, LANES)


def _split_flat(flat2, shapes):
    flat = flat2.reshape(-1)
    out, off = [], 0
    for s in shapes:
        n = int(np.prod(s))
        out.append(flat[off:off + n].reshape(s))
        off += n
    return out


def _unshard(stack, name):
    if name in COL_SHARDED:
        return jnp.moveaxis(stack, 0, -2).reshape(stack.shape[1:-1] + (N_DEV * stack.shape[-1],))
    return stack.reshape((N_DEV * stack.shape[1],) + stack.shape[2:])


def _to_shards(full, name):
    if name in COL_SHARDED:
        c = full.shape[-1] // N_DEV
        return jnp.moveaxis(full.reshape(full.shape[:-1] + (N_DEV, c)), -2, 0)
    return full.reshape((N_DEV, full.shape[0] // N_DEV) + full.shape[1:])


def kernel(x, positions, attn_norm_w, w_in, q_lat_norm_w, w_uq, kv_lat_norm_w, w_ukv, q_norm_w, k_norm_w, mla_out_norm_w, conv_w, a_log, dt_bias, gdn_norm_w, w_out, mlp_norm_w, w_up, w_down, loss_target, m_attn_norm_w, m_w_in, m_q_lat_norm_w, m_w_uq, m_kv_lat_norm_w, m_w_ukv, m_q_norm_w, m_k_norm_w, m_mla_out_norm_w, m_conv_w, m_a_log, m_dt_bias, m_gdn_norm_w, m_w_out, m_mlp_norm_w, m_w_up, m_w_down, v_attn_norm_w, v_w_in, v_q_lat_norm_w, v_w_uq, v_kv_lat_norm_w, v_w_ukv, v_q_norm_w, v_k_norm_w, v_mla_out_norm_w, v_conv_w, v_a_log, v_dt_bias, v_gdn_norm_w, v_w_out, v_mlp_norm_w, v_w_up, v_w_down):
    env = dict(locals())
    W = {n: env[n][0] for n in ALL_W}
    Mo = {n: env["m_" + n][0] for n in ALL_W}
    Vo = {n: env["v_" + n][0] for n in ALL_W}

    mats = [n for n in BIG if n != "conv_w"]
    payload = [W[n].astype(jnp.bfloat16) for n in mats]
    payload.append(lax.bitcast_convert_type(W["conv_w"], jnp.bfloat16))
    shard_shapes = [p.shape for p in payload]
    gathered = _all_gather(_flat_rows(payload, 2 * SUBLANES), "gather_weights").reshape(N_DEV, -1)
    full, off = {}, 0
    for n, s in zip(mats + ["conv_w"], shard_shapes):
        cnt = int(np.prod(s))
        st = gathered[:, off:off + cnt].reshape((N_DEV,) + s)
        off += cnt
        if n == "conv_w":
            st = lax.bitcast_convert_type(st, F32)
        full[n] = _unshard(st, n)

    loss, grad_x, g = _local_step(
        x, positions, loss_target, W["attn_norm_w"], full["w_in"], W["q_lat_norm_w"], full["w_uq"],
        W["kv_lat_norm_w"], full["w_ukv"], W["q_norm_w"], W["k_norm_w"], W["mla_out_norm_w"], full["conv_w"],
        W["a_log"], W["dt_bias"], W["gdn_norm_w"], full["w_out"], W["mlp_norm_w"], full["w_up"], full["w_down"])
    loss = lax.psum(loss, ("x", "y", "c"))

    big_shapes = [W[n].shape for n in BIG]
    g8 = jnp.stack([_flat_rows([_to_shards(g[n], n)[d] for n in BIG], FLAT_ROWS) for d in range(N_DEV)])
    parts = _grad_exchange(g8, "exchange_grads")
    outs_big = _reduce_adamw(parts, _flat_rows([W[n] for n in BIG], FLAT_ROWS),
                             _flat_rows([Mo[n] for n in BIG], FLAT_ROWS),
                             _flat_rows([Vo[n] for n in BIG], FLAT_ROWS), "adamw_sharded")
    small_shapes = [W[n].shape for n in SMALL]
    parts_s = _all_gather(_flat_rows([g[n] for n in SMALL], SUBLANES), "gather_small_grads")
    outs_small = _reduce_adamw(parts_s, _flat_rows([W[n] for n in SMALL], SUBLANES),
                               _flat_rows([Mo[n] for n in SMALL], SUBLANES),
                               _flat_rows([Vo[n] for n in SMALL], SUBLANES), "adamw_replicated")

    res = []
    for kind in range(4):
        d = dict(zip(BIG, _split_flat(outs_big[kind], big_shapes)))
        d.update(zip(SMALL, _split_flat(outs_small[kind], small_shapes)))
        res += [d[n][None] for n in ALL_W]
    return (loss, grad_x, *res)
```

```python
import jax
import jax.numpy as jnp
from jax import lax
from jax.experimental import pallas as pl
from jax.experimental.pallas import tpu as pltpu

F32 = jnp.float32
MXU_DTYPE = jnp.bfloat16
WIRE_DTYPE = jnp.bfloat16
SDS = jax.ShapeDtypeStruct
HIGHEST = lax.Precision.HIGHEST
MESH_ID = pl.DeviceIdType.MESH

D_MODEL = 1024
MLA_HEADS = 4
Q_LORA = 256
KV_LORA = 256
NOPE = 128
ROPE = 64
QK_DIM = NOPE + ROPE
V_DIM = 128
ROPE_THETA = 10000.0
GDN_HEADS = 4
GDN_DIM = 128
GDN_WIDTH = GDN_HEADS * GDN_DIM
CONV_W = 4
CHUNK = 64
D_FF = 4 * D_MODEL
EPS = 1e-6
ATT_SCALE = QK_DIM ** -0.5
GDN_QSCALE = GDN_DIM ** -0.5
N_DEV = 8

ADAM_LR = 0.001
ADAM_B1 = 0.9
ADAM_B2 = 0.999
ADAM_EPS = 1e-08
ADAM_WD = 0.01
ADAM_STEP = 10

LANES = 128
SUBLANES = 8
VMEM_LIMIT = 56 * 1024 * 1024

P_GQKV, P_GZ, P_QLAT, P_KVLAT, P_KPE, P_GAB = 0, 1536, 2048, 2304, 2560, 2688
P_WIDTH = 2816
O_QLAT, O_KVLAT, O_KPE, O_GQKV, O_GZ, O_GAB, O_END = 0, 256, 512, 576, 2112, 2624, 2632


def _params(sem=None, vmem=VMEM_LIMIT):
    kw = dict(vmem_limit_bytes=vmem)
    if sem is not None:
        kw["dimension_semantics"] = sem
    return pltpu.CompilerParams(**kw)


def _mm(a, b):
    return jnp.dot(a.astype(MXU_DTYPE), b.astype(MXU_DTYPE), preferred_element_type=F32)


def _mm_nt(a, b):
    return lax.dot_general(a.astype(MXU_DTYPE), b.astype(MXU_DTYPE), (((1,), (1,)), ((), ())),
                           preferred_element_type=F32)


def _mm_tn(a, b):
    return lax.dot_general(a.astype(MXU_DTYPE), b.astype(MXU_DTYPE), (((0,), (0,)), ((), ())),
                           preferred_element_type=F32)


def _mm_exact(a, b):
    return jnp.dot(a, b, precision=HIGHEST, preferred_element_type=F32)


def _rms(x, w):
    r = lax.rsqrt(jnp.mean(x * x, axis=-1, keepdims=True) + EPS)
    return x * r * w, r


def _rms_bwd(dy, x, w, r):
    xh = x * r
    dyw = dy * w
    dx = r * (dyw - xh * jnp.mean(dyw * xh, axis=-1, keepdims=True))
    dw = jnp.sum(dy * xh, axis=0, keepdims=True)
    return dx, dw


def _l2n_bwd(dy, x, scale):
    r = lax.rsqrt(jnp.sum(x * x, axis=-1, keepdims=True) + EPS)
    xh = x * r
    return (scale * r) * (dy - xh * jnp.sum(dy * xh, axis=-1, keepdims=True))


def _rot(t):
    return jnp.concatenate([-t[:, ROPE // 2:], t[:, :ROPE // 2]], axis=-1)


def _rot_t(t):
    return jnp.concatenate([t[:, ROPE // 2:], -t[:, :ROPE // 2]], axis=-1)


def _rope(t, cos, sin):
    return t * cos + _rot(t) * sin


def _rope_bwd(d, cos, sin):
    return d * cos + _rot_t(d * sin)


def _sigmoid(x):
    return jax.nn.sigmoid(x)


def _shift_down(x, halo, j):
    if j == 0:
        return x
    xr = pltpu.roll(x, j, 0)
    hr = pltpu.roll(halo, j, 0)
    row = lax.broadcasted_iota(jnp.int32, halo.shape, 0)
    top = jnp.where(row < j, hr, xr[:SUBLANES])
    return jnp.concatenate([top, xr[SUBLANES:]], axis=0)


def _shift_up(x, nxt, j):
    if j == 0:
        return x
    n = x.shape[0]
    xr = pltpu.roll(x, n - j, 0)
    nr = pltpu.roll(nxt, SUBLANES - j, 0)
    row = lax.broadcasted_iota(jnp.int32, nxt.shape, 0)
    bot = jnp.where(row >= SUBLANES - j, nr, xr[n - SUBLANES:])
    return jnp.concatenate([xr[:n - SUBLANES], bot], axis=0)


def _chunk_cumsum(y, row_in_chunk):
    s = 1
    while s < CHUNK:
        y = y + jnp.where(row_in_chunk >= s, pltpu.roll(y, s, 0), 0.0)
        s *= 2
    return y


def _chunk_rev_cumsum(y, row_in_chunk):
    n = y.shape[0]
    s = 1
    while s < CHUNK:
        y = y + jnp.where(row_in_chunk + s < CHUNK, pltpu.roll(y, n - s, 0), 0.0)
        s *= 2
    return y


def _pick_lane(tile, lane, idx):
    return jnp.sum(jnp.where(lane == idx, tile, 0.0), axis=-1, keepdims=True)


def _divisor_tile(n, cap, unit=LANES):
    best = unit
    t = unit
    while t <= min(n, cap):
        if n % t == 0:
            best = t
        t += unit
    return n if n <= cap else best


def _in_proj(x2, w_an, w_in_p):
    T, D = x2.shape
    N = w_in_p.shape[1]
    tm = min(512, T)

    def body(x_ref, wn_ref, w_ref, proj_ref, xn_ref):
        xn, _ = _rms(x_ref[...], wn_ref[...])
        xn = xn.astype(MXU_DTYPE)
        xn_ref[...] = xn
        proj_ref[...] = jnp.dot(xn, w_ref[...], preferred_element_type=F32)

    return pl.pallas_call(
        body, grid=(T // tm,), name="in_proj",
        in_specs=[pl.BlockSpec((tm, D), lambda i: (i, 0)), pl.BlockSpec((1, D), lambda i: (0, 0)),
                  pl.BlockSpec((D, N), lambda i: (0, 0))],
        out_specs=[pl.BlockSpec((tm, N), lambda i: (i, 0)), pl.BlockSpec((tm, D), lambda i: (i, 0))],
        out_shape=[SDS((T, N), F32), SDS((T, D), MXU_DTYPE)],
        compiler_params=_params(("arbitrary",)),
    )(x2, w_an, w_in_p)


def _mla_pre(proj, cosf, sinf, w_qln, w_kvln, w_uq_p, w_ukv, qnw, knw):
    T = proj.shape[0]
    tm = min(256, T)
    H = MLA_HEADS

    def body(ql_ref, kvl_ref, kpe_ref, cos_ref, sin_ref, wq_ref, wkv_ref, uq_ref, ukv_ref, qnw_ref, knw_ref,
             q_out, k_out, v_out):
        cos, sin = cos_ref[...], sin_ref[...]
        qnw_, knw_ = qnw_ref[...], knw_ref[...]
        qn, _ = _rms(ql_ref[...], wq_ref[...])
        kvn, _ = _rms(kvl_ref[...], wkv_ref[...])
        qraw = _mm(qn, uq_ref[...])
        kvraw = _mm(kvn, ukv_ref[...])
        kpe = _rope(_rms(kpe_ref[...][:, :ROPE], knw_[:, NOPE:])[0], cos, sin)
        for h in range(H):
            qn_h = _rms(qraw[:, h * NOPE:(h + 1) * NOPE], qnw_[:, :NOPE])[0]
            qp_h = _rope(_rms(qraw[:, H * NOPE + h * ROPE:H * NOPE + (h + 1) * ROPE], qnw_[:, NOPE:])[0], cos, sin)
            q_out[h] = (jnp.concatenate([qn_h, qp_h], axis=-1) * ATT_SCALE).astype(MXU_DTYPE)
            kn_h = _rms(kvraw[:, h * 256:h * 256 + NOPE], knw_[:, :NOPE])[0]
            k_out[h] = jnp.concatenate([kn_h, kpe], axis=-1).astype(MXU_DTYPE)
            v_out[h] = kvraw[:, h * 256 + NOPE:(h + 1) * 256].astype(MXU_DTYPE)

    full = lambda a: pl.BlockSpec(a.shape, lambda i: (0,) * a.ndim)
    return pl.pallas_call(
        body, grid=(T // tm,), name="mla_pre",
        in_specs=[pl.BlockSpec((tm, 256), lambda i: (i, P_QLAT // 256)),
                  pl.BlockSpec((tm, 256), lambda i: (i, P_KVLAT // 256)),
                  pl.BlockSpec((tm, 128), lambda i: (i, P_KPE // 128)),
                  pl.BlockSpec((tm, ROPE), lambda i: (i, 0)), pl.BlockSpec((tm, ROPE), lambda i: (i, 0)),
                  full(w_qln), full(w_kvln), full(w_uq_p), full(w_ukv), full(qnw), full(knw)],
        out_specs=[pl.BlockSpec((H, tm, QK_DIM), lambda i: (0, i, 0)),
                   pl.BlockSpec((H, tm, QK_DIM), lambda i: (0, i, 0)),
                   pl.BlockSpec((H, tm, V_DIM), lambda i: (0, i, 0))],
        out_shape=[SDS((H, T, QK_DIM), MXU_DTYPE), SDS((H, T, QK_DIM), MXU_DTYPE), SDS((H, T, V_DIM), MXU_DTYPE)],
        compiler_params=_params(("arbitrary",)),
    )(proj, proj, proj, cosf, sinf, w_qln, w_kvln, w_uq_p, w_ukv, qnw, knw)


def _attn_fwd(q4, k4, v4, B, S):
    H = MLA_HEADS
    bq = min(256, S)
    nq = S // bq

    def body(q_ref, k_ref, v_ref, o_ref, lse_ref):
        causal = (lax.broadcasted_iota(jnp.int32, (bq, bq), 1) <= lax.broadcasted_iota(jnp.int32, (bq, bq), 0))

        def q_step(qi, carry):
            qs = pl.multiple_of(qi * bq, bq)
            q = q_ref[0, pl.ds(qs, bq), :]

            def k_block(ks, c, diagonal):
                m, l, acc = c
                k = k_ref[0, pl.ds(ks, bq), :]
                v = v_ref[0, pl.ds(ks, bq), :]
                s = _mm_nt(q, k)
                if diagonal:
                    s = jnp.where(causal, s, -jnp.inf)
                m_new = jnp.maximum(m, jnp.max(s, axis=-1, keepdims=True))
                p = jnp.exp(s - m_new)
                a = jnp.exp(m - m_new)
                return m_new, a * l + jnp.sum(p, axis=-1, keepdims=True), a * acc + _mm(p, v)

            c = lax.fori_loop(
                0, qi, lambda kj, c: k_block(pl.multiple_of(kj * bq, bq), c, False),
                (jnp.full((bq, 1), -jnp.inf, F32), jnp.zeros((bq, 1), F32), jnp.zeros((bq, V_DIM), F32)))
            m, l, acc = k_block(qs, c, True)
            o_ref[0, pl.ds(qs, bq), :] = acc / l
            lse_ref[0, pl.ds(qs, bq), :] = m + jnp.log(l)
            return carry

        lax.fori_loop(0, nq, q_step, 0)

    spec = lambda d: pl.BlockSpec((1, S, d), lambda h, b: (h, b, 0))
    return pl.pallas_call(
        body, grid=(H, B), name="attn_fwd",
        in_specs=[spec(QK_DIM), spec(QK_DIM), spec(V_DIM)],
        out_specs=[spec(V_DIM), spec(1)],
        out_shape=[SDS((H, B * S, V_DIM), F32), SDS((H, B * S, 1), F32)],
        compiler_params=_params(("arbitrary", "arbitrary")),
    )(q4, k4, v4)


def _conv_taps(u, halo, w):
    sh = [_shift_down(u, halo, j) for j in range(CONV_W)]
    c = w[0:1] * sh[3] + w[1:2] * sh[2] + w[2:3] * sh[1] + w[3:4] * sh[0]
    return c, sh


def _gate_values(gab, alog_l, dt_l, lane):
    g = -jnp.exp(alog_l) * jax.nn.softplus(gab + dt_l)
    g = jnp.where(lane < GDN_HEADS, g, 0.0)
    beta = jnp.where((lane >= GDN_HEADS) & (lane < 2 * GDN_HEADS), _sigmoid(gab), 0.0)
    return g, beta


def _gdn_pre(proj, conv_w, alog_l, dt_l, S):
    T = proj.shape[0]
    tm = min(256, T)
    tiles_per_seq = S // tm
    C3 = 3 * GDN_WIDTH
    H = GDN_HEADS

    def body(u_ref, halo_ref, gab_ref, w_ref, alog_ref, dt_ref, q_out, k_out, v_out, gates_out):
        i = pl.program_id(0)
        halo = jnp.where(i % tiles_per_seq == 0, 0.0, halo_ref[...])
        c, _ = _conv_taps(u_ref[...], halo, w_ref[...])
        a = c * _sigmoid(c)
        for h in range(H):
            xq = a[:, h * GDN_DIM:(h + 1) * GDN_DIM]
            xk = a[:, GDN_WIDTH + h * GDN_DIM:GDN_WIDTH + (h + 1) * GDN_DIM]
            q_out[h] = xq * lax.rsqrt(jnp.sum(xq * xq, axis=-1, keepdims=True) + EPS) * GDN_QSCALE
            k_out[h] = xk * lax.rsqrt(jnp.sum(xk * xk, axis=-1, keepdims=True) + EPS)
            v_out[h] = a[:, 2 * GDN_WIDTH + h * GDN_DIM:2 * GDN_WIDTH + (h + 1) * GDN_DIM]
        lane = lax.broadcasted_iota(jnp.int32, (tm, LANES), 1)
        ric = lax.broadcasted_iota(jnp.int32, (tm, LANES), 0) % CHUNK
        g, beta = _gate_values(gab_ref[...], alog_ref[...], dt_ref[...], lane)
        gates_out[...] = _chunk_cumsum(g, ric) + beta

    hspec = pl.BlockSpec((H, tm, GDN_DIM), lambda i: (0, i, 0))
    return pl.pallas_call(
        body, grid=(T // tm,), name="gdn_pre",
        in_specs=[pl.BlockSpec((tm, C3), lambda i: (i, 0)),
                  pl.BlockSpec((SUBLANES, C3), lambda i: (jnp.maximum(i * (tm // SUBLANES) - 1, 0), 0)),
                  pl.BlockSpec((tm, LANES), lambda i: (i, P_GAB // LANES)),
                  pl.BlockSpec((CONV_W, C3), lambda i: (0, 0)),
                  pl.BlockSpec((1, LANES), lambda i: (0, 0)), pl.BlockSpec((1, LANES), lambda i: (0, 0))],
        out_specs=[hspec, hspec, hspec, pl.BlockSpec((tm, LANES), lambda i: (i, 0))],
        out_shape=[SDS((H, T, GDN_DIM), F32)] * 3 + [SDS((T, LANES), F32)],
        compiler_params=_params(("arbitrary",)),
    )(proj, proj, proj, conv_w, alog_l, dt_l)


def _unit_lower_inverse(L, eye):
    M = -L
    P = eye + M
    Mk = M
    for _ in range(5):
        Mk = _mm_exact(Mk, Mk)
        P = P + _mm_exact(P, Mk)
    return P


def _chunk_decays(gt, lane, h, ri, ci, rcol):
    Gc = _pick_lane(gt, lane, h)
    bt = _pick_lane(gt, lane, h + GDN_HEADS)
    Gb = jnp.broadcast_to(Gc, (CHUNK, CHUNK))
    Gam = jnp.where(ri >= ci, jnp.exp(Gb - Gb.T), 0.0)
    Gl = jnp.sum(jnp.where(rcol == CHUNK - 1, Gc, 0.0), axis=0, keepdims=True)
    return Gc, bt, Gam, jnp.exp(Gc), jnp.exp(Gl - Gc), jnp.exp(Gl)


def _gdn_fwd(qg, kg, vg, gates, B, S):
    H, D, C = GDN_HEADS, GDN_DIM, CHUNK
    NC = S // C

    def body(q_ref, k_ref, v_ref, g_ref, o_ref, st_ref, ai_ref):
        h = pl.program_id(0)
        lane = lax.broadcasted_iota(jnp.int32, (C, LANES), 1)
        ri = lax.broadcasted_iota(jnp.int32, (C, C), 0)
        ci = lax.broadcasted_iota(jnp.int32, (C, C), 1)
        rcol = lax.broadcasted_iota(jnp.int32, (C, 1), 0)
        eye = (ri == ci).astype(F32)

        def step(n, S_):
            cs = pl.multiple_of(n * C, C)
            q = q_ref[0, pl.ds(cs, C), :]
            k = k_ref[0, pl.ds(cs, C), :]
            v = v_ref[0, pl.ds(cs, C), :]
            Gc, bt, Gam, e, f, eL = _chunk_decays(g_ref[pl.ds(cs, C), :], lane, h, ri, ci, rcol)
            L = jnp.where(ri > ci, bt * _mm_nt(k, k) * Gam, 0.0)
            Ainv = _unit_lower_inverse(L, eye)
            sol = _mm_exact(Ainv, jnp.concatenate([v * bt, k * (bt * e)], axis=-1))
            u, w = sol[:, :D], sol[:, D:]
            At = _mm_nt(q, k) * Gam
            vn = u - _mm(w, S_)
            o_ref[0, pl.ds(cs, C), :] = _mm(q * e, S_) + _mm(At, vn)
            st_ref[0, n] = S_
            ai_ref[0, n] = Ainv
            return S_ * eL + _mm_tn(k * f, vn)

        lax.fori_loop(0, NC, step, jnp.zeros((D, D), F32))

    spec = pl.BlockSpec((1, S, D), lambda h, b: (h, b, 0))
    return pl.pallas_call(
        body, grid=(H, B), name="gdn_fwd",
        in_specs=[spec, spec, spec, pl.BlockSpec((S, LANES), lambda h, b: (b, 0))],
        out_specs=[spec, pl.BlockSpec((1, NC, D, D), lambda h, b: (h, b, 0, 0)),
                   pl.BlockSpec((1, NC, C, C), lambda h, b: (h, b, 0, 0))],
        out_shape=[SDS((H, B * S, D), F32), SDS((H, B * NC, D, D), F32), SDS((H, B * NC, C, C), F32)],
        compiler_params=_params(("arbitrary", "arbitrary")),
    )(qg, kg, vg, gates)


def _mix_out(o_mla, o_gdn, proj, x2, mla_w, gdn_w, w_out):
    T, D = x2.shape
    tm = min(512, T)
    H = MLA_HEADS

    def body(om_ref, og_ref, z_ref, x_ref, mw_ref, gw_ref, w_ref, h_ref, mix_ref):
        z = z_ref[...]
        parts = [_rms(om_ref[h], mw_ref[h:h + 1, :])[0] for h in range(H)]
        for h in range(GDN_HEADS):
            zh = z[:, h * GDN_DIM:(h + 1) * GDN_DIM]
            parts.append(_rms(og_ref[h], gw_ref[...])[0] * (zh * _sigmoid(zh)))
        mix = jnp.concatenate(parts, axis=-1).astype(MXU_DTYPE)
        mix_ref[...] = mix
        h_ref[...] = x_ref[...] + jnp.dot(mix, w_ref[...], preferred_element_type=F32)

    hspec = pl.BlockSpec((H, tm, V_DIM), lambda i: (0, i, 0))
    return pl.pallas_call(
        body, grid=(T // tm,), name="mix_out",
        in_specs=[hspec, hspec, pl.BlockSpec((tm, GDN_WIDTH), lambda i: (i, P_GZ // GDN_WIDTH)),
                  pl.BlockSpec((tm, D), lambda i: (i, 0)),
                  pl.BlockSpec((H, V_DIM), lambda i: (0, 0)), pl.BlockSpec((1, GDN_DIM), lambda i: (0, 0)),
                  pl.BlockSpec((D, D), lambda i: (0, 0))],
        out_specs=[pl.BlockSpec((tm, D), lambda i: (i, 0)), pl.BlockSpec((tm, D), lambda i: (i, 0))],
        out_shape=[SDS((T, D), F32), SDS((T, D), MXU_DTYPE)],
        compiler_params=_params(("arbitrary",)),
    )(o_mla, o_gdn, proj, x2, mla_w, gdn_w, w_out)


def _mlp_fwd(h2, w_mn, w_up, w_down, target):
    T, D = h2.shape
    nf, _, tf = w_up.shape
    F = nf * tf
    tm = min(512, T)

    def body(h_ref, wn_ref, up_w, down_w, t_ref, up_ref, hn_ref, dy_ref, loss_ref, y_acc):
        j = pl.program_id(1)

        @pl.when(j == 0)
        def _():
            hn_ref[...] = _rms(h_ref[...], wn_ref[...])[0].astype(MXU_DTYPE)
            y_acc[...] = h_ref[...]

        up = jnp.dot(hn_ref[...], up_w[0], preferred_element_type=F32)
        up_ref[...] = up
        r = jnp.maximum(up, 0.0)
        y_acc[...] += _mm(r * r, down_w[...])

        @pl.when(j == nf - 1)
        def _():
            err = y_acc[...] - t_ref[...]
            dy_ref[...] = err / D
            loss_ref[...] = jnp.full((1, SUBLANES, LANES), jnp.sum(err * err), F32)

    return pl.pallas_call(
        body, grid=(T // tm, nf), name="mlp_fwd",
        in_specs=[pl.BlockSpec((tm, D), lambda i, j: (i, 0)), pl.BlockSpec((1, D), lambda i, j: (0, 0)),
                  pl.BlockSpec((1, D, tf), lambda i, j: (j, 0, 0)), pl.BlockSpec((tf, D), lambda i, j: (j, 0)),
                  pl.BlockSpec((tm, D), lambda i, j: (i, 0))],
        out_specs=[pl.BlockSpec((tm, tf), lambda i, j: (i, j)), pl.BlockSpec((tm, D), lambda i, j: (i, 0)),
                   pl.BlockSpec((tm, D), lambda i, j: (i, 0)),
                   pl.BlockSpec((1, SUBLANES, LANES), lambda i, j: (i, 0, 0))],
        out_shape=[SDS((T, F), F32), SDS((T, D), MXU_DTYPE), SDS((T, D), F32),
                   SDS((T // tm, SUBLANES, LANES), F32)],
        scratch_shapes=[pltpu.VMEM((tm, D), F32)],
        compiler_params=_params(("arbitrary", "arbitrary")),
    )(h2, w_mn, w_up, w_down, target)


def _mlp_bwd(dy, up, h2, w_mn, w_up, w_down):
    T, D = h2.shape
    nf, _, tf = w_up.shape
    F = nf * tf
    tm = min(512, T)

    def body(dy_ref, up_ref, h_ref, wn_ref, up_w, down_w, dh_ref, dhb_ref, dup_ref, act_ref, dyb_ref, dwn_ref, acc):
        i, j = pl.program_id(0), pl.program_id(1)

        @pl.when((i == 0) & (j == 0))
        def _():
            dwn_ref[...] = jnp.zeros_like(dwn_ref)

        @pl.when(j == 0)
        def _():
            acc[...] = jnp.zeros_like(acc)
            dyb_ref[...] = dy_ref[...].astype(MXU_DTYPE)

        r = jnp.maximum(up_ref[...], 0.0)
        act_ref[...] = (r * r).astype(MXU_DTYPE)
        dup = (_mm_nt(dyb_ref[...], down_w[...]) * (2.0 * r)).astype(MXU_DTYPE)
        dup_ref[...] = dup
        acc[...] += _mm_nt(dup, up_w[0])

        @pl.when(j == nf - 1)
        def _():
            hv = h_ref[...]
            _, rr = _rms(hv, wn_ref[...])
            dx, dw = _rms_bwd(acc[...], hv, wn_ref[...], rr)
            dh = dy_ref[...] + dx
            dh_ref[...] = dh
            dhb_ref[...] = dh.astype(MXU_DTYPE)
            dwn_ref[...] += dw

    row = lambda i, j: (i, 0)
    return pl.pallas_call(
        body, grid=(T // tm, nf), name="mlp_bwd",
        in_specs=[pl.BlockSpec((tm, D), row), pl.BlockSpec((tm, tf), lambda i, j: (i, j)), pl.BlockSpec((tm, D), row),
                  pl.BlockSpec((1, D), lambda i, j: (0, 0)),
                  pl.BlockSpec((1, D, tf), lambda i, j: (j, 0, 0)), pl.BlockSpec((tf, D), lambda i, j: (j, 0))],
        out_specs=[pl.BlockSpec((tm, D), row), pl.BlockSpec((tm, D), row),
                   pl.BlockSpec((tm, tf), lambda i, j: (i, j)), pl.BlockSpec((tm, tf), lambda i, j: (i, j)),
                   pl.BlockSpec((tm, D), row), pl.BlockSpec((1, D), lambda i, j: (0, 0))],
        out_shape=[SDS((T, D), F32), SDS((T, D), MXU_DTYPE), SDS((T, F), MXU_DTYPE), SDS((T, F), MXU_DTYPE),
                   SDS((T, D), MXU_DTYPE), SDS((1, D), F32)],
        scratch_shapes=[pltpu.VMEM((tm, D), F32)],
        compiler_params=_params(("arbitrary", "arbitrary")),
    )(dy, up, h2, w_mn, w_up, w_down)


def _mix_bwd(dhb, o_mla, o_gdn, proj, mla_w, gdn_w, w_out):
    T, D = dhb.shape
    tm = min(512, T)
    H = MLA_HEADS

    def body(dh_ref, om_ref, og_ref, z_ref, mw_ref, gw_ref, w_ref, dom_ref, dog_ref, dz_ref, dmw_ref, dgw_ref):
        @pl.when(pl.program_id(0) == 0)
        def _():
            dmw_ref[...] = jnp.zeros_like(dmw_ref)
            dgw_ref[...] = jnp.zeros_like(dgw_ref)

        dmix = _mm_nt(dh_ref[...], w_ref[...])
        z = z_ref[...]
        dmw, dzs = [], []
        dgw = jnp.zeros((1, GDN_DIM), F32)
        for h in range(H):
            o = om_ref[h]
            w = mw_ref[h:h + 1, :]
            _, r = _rms(o, w)
            dx, dw = _rms_bwd(dmix[:, h * V_DIM:(h + 1) * V_DIM], o, w, r)
            dom_ref[h] = dx
            dmw.append(dw)
        for h in range(GDN_HEADS):
            o = og_ref[h]
            w = gw_ref[...]
            zh = z[:, h * GDN_DIM:(h + 1) * GDN_DIM]
            sg = _sigmoid(zh)
            yn, r = _rms(o, w)
            dy = dmix[:, H * V_DIM + h * GDN_DIM:H * V_DIM + (h + 1) * GDN_DIM]
            dzs.append(dy * yn * (sg * (1.0 + zh * (1.0 - sg))))
            dx, dw = _rms_bwd(dy * (zh * sg), o, w, r)
            dog_ref[h] = dx
            dgw = dgw + dw
        dz_ref[...] = jnp.concatenate(dzs, axis=-1)
        dmw_ref[...] += jnp.concatenate(dmw, axis=0)
        dgw_ref[...] += dgw

    hspec = pl.BlockSpec((H, tm, V_DIM), lambda i: (0, i, 0))
    return pl.pallas_call(
        body, grid=(T // tm,), name="mix_bwd",
        in_specs=[pl.BlockSpec((tm, D), lambda i: (i, 0)), hspec, hspec,
                  pl.BlockSpec((tm, GDN_WIDTH), lambda i: (i, P_GZ // GDN_WIDTH)),
                  pl.BlockSpec((H, V_DIM), lambda i: (0, 0)), pl.BlockSpec((1, GDN_DIM), lambda i: (0, 0)),
                  pl.BlockSpec((D, D), lambda i: (0, 0))],
        out_specs=[hspec, hspec, pl.BlockSpec((tm, GDN_WIDTH), lambda i: (i, 0)),
                   pl.BlockSpec((H, V_DIM), lambda i: (0, 0)), pl.BlockSpec((1, GDN_DIM), lambda i: (0, 0))],
        out_shape=[SDS((H, T, V_DIM), F32), SDS((H, T, GDN_DIM), F32), SDS((T, GDN_WIDTH), F32),
                   SDS((H, V_DIM), F32), SDS((1, GDN_DIM), F32)],
        compiler_params=_params(("arbitrary",)),
    )(dhb, o_mla, o_gdn, proj, mla_w, gdn_w, w_out)


def _attn_bwd(q4, k4, v4, do4, o4, lse4, B, S):
    H = MLA_HEADS
    bq = min(256, S)
    nq = S // bq

    def body(q_ref, k_ref, v_ref, do_ref, o_ref, lse_ref, dq_ref, dk_ref, dv_ref, delta):
        dq_ref[...] = jnp.zeros_like(dq_ref)
        dk_ref[...] = jnp.zeros_like(dk_ref)
        dv_ref[...] = jnp.zeros_like(dv_ref)
        delta[...] = jnp.sum(do_ref[0] * o_ref[0], axis=-1, keepdims=True)

        causal = (lax.broadcasted_iota(jnp.int32, (bq, bq), 1) <= lax.broadcasted_iota(jnp.int32, (bq, bq), 0))

        def k_step(kj, carry):
            ks = pl.multiple_of(kj * bq, bq)
            k = k_ref[0, pl.ds(ks, bq), :]
            v = v_ref[0, pl.ds(ks, bq), :]

            def q_block(qs, diagonal):
                q = q_ref[0, pl.ds(qs, bq), :]
                do = do_ref[0, pl.ds(qs, bq), :].astype(MXU_DTYPE)
                p = jnp.exp(_mm_nt(q, k) - lse_ref[0, pl.ds(qs, bq), :])
                if diagonal:
                    p = jnp.where(causal, p, 0.0)
                dv_ref[0, pl.ds(ks, bq), :] += _mm_tn(p, do)
                ds = p * (_mm_nt(do, v) - delta[pl.ds(qs, bq), :])
                dq_ref[0, pl.ds(qs, bq), :] += _mm(ds, k)
                dk_ref[0, pl.ds(ks, bq), :] += _mm_tn(ds, q)

            q_block(ks, True)

            def q_step(qi, c):
                q_block(pl.multiple_of(qi * bq, bq), False)
                return c

            lax.fori_loop(kj + 1, nq, q_step, 0)
            return carry

        lax.fori_loop(0, nq, k_step, 0)

    spec = lambda d: pl.BlockSpec((1, S, d), lambda h, b: (h, b, 0))
    return pl.pallas_call(
        body, grid=(H, B), name="attn_bwd",
        in_specs=[spec(QK_DIM), spec(QK_DIM), spec(V_DIM), spec(V_DIM), spec(V_DIM), spec(1)],
        out_specs=[spec(QK_DIM), spec(QK_DIM), spec(V_DIM)],
        out_shape=[SDS((H, B * S, QK_DIM), F32), SDS((H, B * S, QK_DIM), F32), SDS((H, B * S, V_DIM), F32)],
        scratch_shapes=[pltpu.VMEM((S, 1), F32)],
        compiler_params=_params(("arbitrary", "arbitrary")),
    )(q4, k4, v4, do4, o4, lse4)


def _gdn_bwd(qg, kg, vg, gates, states, ainv, do4, B, S):
    H, D, C = GDN_HEADS, GDN_DIM, CHUNK
    NC = S // C

    def body(q_ref, k_ref, v_ref, g_ref, st_ref, ai_ref, do_ref, dq_ref, dk_ref, dv_ref, dgb_ref):
        h = pl.program_id(0)
        lane = lax.broadcasted_iota(jnp.int32, (C, LANES), 1)
        ri = lax.broadcasted_iota(jnp.int32, (C, C), 0)
        ci = lax.broadcasted_iota(jnp.int32, (C, C), 1)
        rcol = lax.broadcasted_iota(jnp.int32, (C, 1), 0)

        def rsum(a):
            return jnp.sum(a, axis=-1, keepdims=True)

        def step(t, dS):
            n = NC - 1 - t
            cs = pl.multiple_of(n * C, C)
            q = q_ref[0, pl.ds(cs, C), :]
            k = k_ref[0, pl.ds(cs, C), :]
            v = v_ref[0, pl.ds(cs, C), :]
            do = do_ref[0, pl.ds(cs, C), :]
            Gc, bt, Gam, e, f, eL = _chunk_decays(g_ref[pl.ds(cs, C), :], lane, h, ri, ci, rcol)
            S0 = st_ref[0, n]
            Ainv = ai_ref[0, n]
            KK = _mm_nt(k, k)
            QK = _mm_nt(q, k)
            be = bt * e
            sol = _mm_exact(Ainv, jnp.concatenate([v * bt, k * be], axis=-1))
            u, w = sol[:, :D], sol[:, D:]
            At = QK * Gam
            qd = q * e
            kd = k * f
            vn = u - _mm(w, S0)
            dvn = _mm_tn(At, do) + _mm(kd, dS)
            dAt = jnp.where(ri >= ci, _mm_nt(do, vn), 0.0)
            dqd = _mm_nt(do, S0)
            dS0 = _mm_tn(qd, do) + eL * dS - _mm_tn(w, dvn)
            dw = -_mm_nt(dvn, S0)
            dkd = _mm_nt(vn, dS)
            deL = jnp.sum(rsum(dS * S0), axis=0, keepdims=True)
            dR = _mm_exact(Ainv.T, jnp.concatenate([dvn, dw], axis=-1))
            dR1, dR2 = dR[:, :D], dR[:, D:]
            dL = jnp.where(ri > ci, -_mm_nt(dR, sol), 0.0)
            dv_ref[0, pl.ds(cs, C), :] = dR1 * bt
            r2 = rsum(dR2 * k)
            X = dL * Gam
            dbt = rsum(dR1 * v) + r2 * e + rsum(X * KK)
            de = r2 * bt + rsum(dqd * q)
            dKK = X * bt
            dQK = dAt * Gam
            dq_ref[0, pl.ds(cs, C), :] = _mm(dQK, k) + dqd * e
            dk_ref[0, pl.ds(cs, C), :] = dR2 * be + _mm(dKK + dKK.T, k) + _mm_tn(dQK, q) + dkd * f
            df = rsum(dkd * k)
            Z = (dL * (bt * KK) + dAt * QK) * Gam
            dG = rsum(Z) - rsum(Z.T) + de * e - df * f
            dGl = jnp.sum(df * f, axis=0, keepdims=True) + deL * eL
            dG = dG + jnp.where(rcol == C - 1, dGl, 0.0)
            dgb_ref[0, pl.ds(cs, C), :] = jnp.where(lane == 0, dG, jnp.where(lane == 1, dbt, 0.0))
            return dS0

        lax.fori_loop(0, NC, step, jnp.zeros((D, D), F32))

    spec = pl.BlockSpec((1, S, D), lambda h, b: (h, b, 0))
    return pl.pallas_call(
        body, grid=(H, B), name="gdn_bwd",
        in_specs=[spec, spec, spec, pl.BlockSpec((S, LANES), lambda h, b: (b, 0)),
                  pl.BlockSpec((1, NC, D, D), lambda h, b: (h, b, 0, 0)),
                  pl.BlockSpec((1, NC, C, C), lambda h, b: (h, b, 0, 0)), spec],
        out_specs=[spec, spec, spec, spec],
        out_shape=[SDS((H, B * S, D), F32)] * 4,
        compiler_params=_params(("arbitrary", "arbitrary")),
    )(qg, kg, vg, gates, states, ainv, do4)


def _gdn_pre_bwd(proj, conv_w, alog_l, dt_l, dq4, dk4, dv4, dgb4, S):
    T = proj.shape[0]
    tm = min(256, T)
    tiles_per_seq = S // tm
    C3 = 3 * GDN_WIDTH
    H = GDN_HEADS

    def body(u_ref, halo_ref, gab_ref, w_ref, alog_ref, dt_ref, dq_ref, dk_ref, dv_ref, dgb_ref,
             dc_ref, dgab_ref, dcw_ref, dalog_ref, ddt_ref):
        i = pl.program_id(0)

        @pl.when(i == 0)
        def _():
            dcw_ref[...] = jnp.zeros_like(dcw_ref)
            dalog_ref[...] = jnp.zeros_like(dalog_ref)
            ddt_ref[...] = jnp.zeros_like(ddt_ref)

        halo = jnp.where(i % tiles_per_seq == 0, 0.0, halo_ref[...])
        c, sh = _conv_taps(u_ref[...], halo, w_ref[...])
        sg = _sigmoid(c)
        a = c * sg
        das = [None] * (3 * H)
        for h in range(H):
            xq = a[:, h * GDN_DIM:(h + 1) * GDN_DIM]
            xk = a[:, GDN_WIDTH + h * GDN_DIM:GDN_WIDTH + (h + 1) * GDN_DIM]
            das[h] = _l2n_bwd(dq_ref[h], xq, GDN_QSCALE)
            das[H + h] = _l2n_bwd(dk_ref[h], xk, 1.0)
            das[2 * H + h] = dv_ref[h]
        dc = jnp.concatenate(das, axis=-1) * (sg * (1.0 + c * (1.0 - sg)))
        dc_ref[...] = dc
        dcw_ref[...] += jnp.concatenate(
            [jnp.sum(dc * sh[CONV_W - 1 - t], axis=0, keepdims=True) for t in range(CONV_W)], axis=0)
        lane = lax.broadcasted_iota(jnp.int32, (tm, LANES), 1)
        ric = lax.broadcasted_iota(jnp.int32, (tm, LANES), 0) % CHUNK
        dG = jnp.zeros((tm, LANES), F32)
        for h in range(H):
            t = dgb_ref[h]
            dG = dG + jnp.where(lane == h, _pick_lane(t, lane, 0), 0.0) \
                    + jnp.where(lane == h + H, _pick_lane(t, lane, 1), 0.0)
        is_g = lane < H
        dg = jnp.where(is_g, _chunk_rev_cumsum(jnp.where(is_g, dG, 0.0), ric), 0.0)
        gab = gab_ref[...]
        g, beta = _gate_values(gab, alog_ref[...], dt_ref[...], lane)
        dga = jnp.where(is_g, dg * (-jnp.exp(alog_ref[...])) * _sigmoid(gab + dt_ref[...]), 0.0)
        dgb = jnp.where(is_g, 0.0, dG) * beta * (1.0 - beta)
        dgab_ref[...] = dga + dgb
        dalog_ref[...] += jnp.sum(dg * g, axis=0, keepdims=True)
        ddt_ref[...] += jnp.sum(dga, axis=0, keepdims=True)

    hspec = pl.BlockSpec((H, tm, GDN_DIM), lambda i: (0, i, 0))
    vec = pl.BlockSpec((1, LANES), lambda i: (0, 0))
    return pl.pallas_call(
        body, grid=(T // tm,), name="gdn_pre_bwd",
        in_specs=[pl.BlockSpec((tm, C3), lambda i: (i, 0)),
                  pl.BlockSpec((SUBLANES, C3), lambda i: (jnp.maximum(i * (tm // SUBLANES) - 1, 0), 0)),
                  pl.BlockSpec((tm, LANES), lambda i: (i, P_GAB // LANES)),
                  pl.BlockSpec((CONV_W, C3), lambda i: (0, 0)), vec, vec, hspec, hspec, hspec, hspec],
        out_specs=[pl.BlockSpec((tm, C3), lambda i: (i, 0)), pl.BlockSpec((tm, LANES), lambda i: (i, 0)),
                   pl.BlockSpec((CONV_W, C3), lambda i: (0, 0)), vec, vec],
        out_shape=[SDS((T, C3), F32), SDS((T, LANES), F32), SDS((CONV_W, C3), F32),
                   SDS((1, LANES), F32), SDS((1, LANES), F32)],
        compiler_params=_params(("arbitrary",)),
    )(proj, proj, proj, conv_w, alog_l, dt_l, dq4, dk4, dv4, dgb4)


def _conv_bwd_input(dc, conv_w, S):
    T, C3 = dc.shape
    tm = min(256, T)
    tiles_per_seq = S // tm
    nblk = T // SUBLANES

    def body(dc_ref, nxt_ref, w_ref, du_ref):
        i = pl.program_id(0)
        nxt = jnp.where(i % tiles_per_seq == tiles_per_seq - 1, 0.0, nxt_ref[...])
        x = dc_ref[...]
        w = w_ref[...]
        du = w[3:4] * x
        for j in range(1, CONV_W):
            du = du + w[3 - j:4 - j] * _shift_up(x, nxt, j)
        du_ref[...] = du

    return pl.pallas_call(
        body, grid=(T // tm,), name="conv_bwd_input",
        in_specs=[pl.BlockSpec((tm, C3), lambda i: (i, 0)),
                  pl.BlockSpec((SUBLANES, C3), lambda i: (jnp.minimum((i + 1) * (tm // SUBLANES), nblk - 1), 0)),
                  pl.BlockSpec((CONV_W, C3), lambda i: (0, 0))],
        out_specs=pl.BlockSpec((tm, C3), lambda i: (i, 0)),
        out_shape=SDS((T, C3), F32),
        compiler_params=_params(("arbitrary",)),
    )(dc, dc, conv_w)


def _mla_pre_bwd(proj, cosf, sinf, w_qln, w_kvln, w_uq_p, w_ukv, qnw, knw, dq4, dk4, dv4):
    T = proj.shape[0]
    tm = min(256, T)
    H = MLA_HEADS

    def body(ql_ref, kvl_ref, kpe_ref, cos_ref, sin_ref, wq_ref, wkv_ref, uq_ref, ukv_ref, qnw_ref, knw_ref,
             dq_ref, dk_ref, dv_ref,
             dql_ref, dkvl_ref, dkpe_ref, dqraw_ref, dkvraw_ref, qn_ref, kvn_ref, dwq_ref, dwkv_ref, dqnw_ref, dknw_ref):
        @pl.when(pl.program_id(0) == 0)
        def _():
            for r in (dwq_ref, dwkv_ref, dqnw_ref, dknw_ref):
                r[...] = jnp.zeros_like(r)

        cos, sin = cos_ref[...], sin_ref[...]
        qnw_, knw_ = qnw_ref[...], knw_ref[...]
        ql, kvl = ql_ref[...], kvl_ref[...]
        kpe_raw = kpe_ref[...][:, :ROPE]
        qn, rq = _rms(ql, wq_ref[...])
        kvn, rkv = _rms(kvl, wkv_ref[...])
        qn_ref[...] = qn.astype(MXU_DTYPE)
        kvn_ref[...] = kvn.astype(MXU_DTYPE)
        qraw = _mm(qn, uq_ref[...])
        kvraw = _mm(kvn, ukv_ref[...])
        dq_nope, dq_pe, dkv_parts = [], [], []
        dqnw_n = jnp.zeros((1, NOPE), F32)
        dqnw_p = jnp.zeros((1, ROPE), F32)
        dknw_n = jnp.zeros((1, NOPE), F32)
        dkpe = jnp.zeros((tm, ROPE), F32)
        for h in range(H):
            dq = dq_ref[h] * ATT_SCALE
            x = qraw[:, h * NOPE:(h + 1) * NOPE]
            dx, dw = _rms_bwd(dq[:, :NOPE], x, qnw_[:, :NOPE], _rms(x, qnw_[:, :NOPE])[1])
            dq_nope.append(dx)
            dqnw_n = dqnw_n + dw
            x = qraw[:, H * NOPE + h * ROPE:H * NOPE + (h + 1) * ROPE]
            dx, dw = _rms_bwd(_rope_bwd(dq[:, NOPE:], cos, sin), x, qnw_[:, NOPE:], _rms(x, qnw_[:, NOPE:])[1])
            dq_pe.append(dx)
            dqnw_p = dqnw_p + dw
            dk = dk_ref[h]
            x = kvraw[:, h * 256:h * 256 + NOPE]
            dx, dw = _rms_bwd(dk[:, :NOPE], x, knw_[:, :NOPE], _rms(x, knw_[:, :NOPE])[1])
            dknw_n = dknw_n + dw
            dkpe = dkpe + dk[:, NOPE:]
            dkv_parts += [dx, dv_ref[h]]
        dx, dknw_p = _rms_bwd(_rope_bwd(dkpe, cos, sin), kpe_raw, knw_[:, NOPE:], _rms(kpe_raw, knw_[:, NOPE:])[1])
        dkpe_ref[...] = jnp.concatenate([dx, jnp.zeros((tm, LANES - ROPE), F32)], axis=-1)
        dqraw = jnp.concatenate(dq_nope + dq_pe, axis=-1).astype(MXU_DTYPE)
        dkvraw = jnp.concatenate(dkv_parts, axis=-1).astype(MXU_DTYPE)
        dqraw_ref[...] = dqraw
        dkvraw_ref[...] = dkvraw
        dx, dw = _rms_bwd(_mm_nt(dqraw, uq_ref[...]), ql, wq_ref[...], rq)
        dql_ref[...] = dx
        dwq_ref[...] += dw
        dx, dw = _rms_bwd(_mm_nt(dkvraw, ukv_ref[...]), kvl, wkv_ref[...], rkv)
        dkvl_ref[...] = dx
        dwkv_ref[...] += dw
        dqnw_ref[...] += jnp.concatenate([dqnw_n, dqnw_p], axis=-1)
        dknw_ref[...] += jnp.concatenate([dknw_n, dknw_p], axis=-1)

    full = lambda a: pl.BlockSpec(a.shape, lambda i: (0,) * a.ndim)
    rows = lambda n: pl.BlockSpec((tm, n), lambda i: (i, 0))
    const = lambda n: pl.BlockSpec((1, n), lambda i: (0, 0))
    NQ, NKV = w_uq_p.shape[1], w_ukv.shape[1]
    return pl.pallas_call(
        body, grid=(T // tm,), name="mla_pre_bwd",
        in_specs=[pl.BlockSpec((tm, 256), lambda i: (i, P_QLAT // 256)),
                  pl.BlockSpec((tm, 256), lambda i: (i, P_KVLAT // 256)),
                  pl.BlockSpec((tm, 128), lambda i: (i, P_KPE // 128)),
                  rows(ROPE), rows(ROPE),
                  full(w_qln), full(w_kvln), full(w_uq_p), full(w_ukv), full(qnw), full(knw),
                  pl.BlockSpec((H, tm, QK_DIM), lambda i: (0, i, 0)),
                  pl.BlockSpec((H, tm, QK_DIM), lambda i: (0, i, 0)),
                  pl.BlockSpec((H, tm, V_DIM), lambda i: (0, i, 0))],
        out_specs=[rows(Q_LORA), rows(KV_LORA), rows(LANES), rows(NQ), rows(NKV), rows(Q_LORA), rows(KV_LORA),
                   const(Q_LORA), const(KV_LORA), const(QK_DIM), const(QK_DIM)],
        out_shape=[SDS((T, Q_LORA), F32), SDS((T, KV_LORA), F32), SDS((T, LANES), F32),
                   SDS((T, NQ), MXU_DTYPE), SDS((T, NKV), MXU_DTYPE),
                   SDS((T, Q_LORA), MXU_DTYPE), SDS((T, KV_LORA), MXU_DTYPE),
                   SDS((1, Q_LORA), F32), SDS((1, KV_LORA), F32), SDS((1, QK_DIM), F32), SDS((1, QK_DIM), F32)],
        compiler_params=_params(("arbitrary",)),
    )(proj, proj, proj, cosf, sinf, w_qln, w_kvln, w_uq_p, w_ukv, qnw, knw, dq4, dk4, dv4)


def _in_proj_bwd(dgqkv, dgz, dql, dkvl, dkpe, dgab, w_in_p, dh, x2, w_an):
    T, D = x2.shape
    N = w_in_p.shape[1]
    tm = min(512, T)

    def body(a_ref, b_ref, c_ref, d_ref, e_ref, f_ref, w_ref, dh_ref, x_ref, wn_ref, dx_ref, dp_ref, dwn_ref):
        @pl.when(pl.program_id(0) == 0)
        def _():
            dwn_ref[...] = jnp.zeros_like(dwn_ref)

        dp = jnp.concatenate([a_ref[...], b_ref[...], c_ref[...], d_ref[...], e_ref[...], f_ref[...]],
                             axis=-1).astype(MXU_DTYPE)
        dp_ref[...] = dp
        x = x_ref[...]
        _, r = _rms(x, wn_ref[...])
        dx, dw = _rms_bwd(_mm_nt(dp, w_ref[...]), x, wn_ref[...], r)
        dx_ref[...] = dh_ref[...] + dx
        dwn_ref[...] += dw

    rows = lambda n: pl.BlockSpec((tm, n), lambda i: (i, 0))
    return pl.pallas_call(
        body, grid=(T // tm,), name="in_proj_bwd",
        in_specs=[rows(dgqkv.shape[1]), rows(dgz.shape[1]), rows(dql.shape[1]), rows(dkvl.shape[1]),
                  rows(dkpe.shape[1]), rows(dgab.shape[1]),
                  pl.BlockSpec((D, N), lambda i: (0, 0)), rows(D), rows(D), pl.BlockSpec((1, D), lambda i: (0, 0))],
        out_specs=[rows(D), rows(N), pl.BlockSpec((1, D), lambda i: (0, 0))],
        out_shape=[SDS((T, D), F32), SDS((T, N), MXU_DTYPE), SDS((1, D), F32)],
        compiler_params=_params(("arbitrary",)),
    )(dgqkv, dgz, dql, dkvl, dkpe, dgab, w_in_p, dh, x2, w_an)


def _wgrad(a, b, name, column_shards=False):
    T, M = a.shape
    N = b.shape[1]
    tM = _divisor_tile(M, 512)
    tN = N // N_DEV if column_shards else _divisor_tile(N, 1536)
    tk = min(T, 1024)
    nk = T // tk

    def body(a_ref, b_ref, o_ref, acc):
        k = pl.program_id(2)

        @pl.when(k == 0)
        def _():
            acc[...] = jnp.zeros_like(acc)

        acc[...] += _mm_tn(a_ref[...], b_ref[...])

        @pl.when(k == nk - 1)
        def _():
            o_ref[...] = acc[...].astype(WIRE_DTYPE).reshape(o_ref.shape)

    if column_shards:
        out_spec, out_shape = pl.BlockSpec((1, tM, tN), lambda i, j, k: (j, i, 0)), SDS((N_DEV, M, tN), WIRE_DTYPE)
    else:
        out_spec, out_shape = pl.BlockSpec((tM, tN), lambda i, j, k: (i, j)), SDS((M, N), WIRE_DTYPE)
    return pl.pallas_call(
        body, grid=(M // tM, N // tN, nk), name=name,
        in_specs=[pl.BlockSpec((tk, tM), lambda i, j, k: (k, i)), pl.BlockSpec((tk, tN), lambda i, j, k: (k, j))],
        out_specs=out_spec, out_shape=out_shape,
        scratch_shapes=[pltpu.VMEM((tM, tN), F32)],
        compiler_params=_params(("arbitrary", "arbitrary", "arbitrary")),
    )(a, b)


def _adamw(g, w, m, v):
    m = ADAM_B1 * m + (1.0 - ADAM_B1) * g
    v = ADAM_B2 * v + (1.0 - ADAM_B2) * jnp.square(g)
    m_hat = m / (1.0 - ADAM_B1 ** ADAM_STEP)
    v_hat = v / (1.0 - ADAM_B2 ** ADAM_STEP)
    return -ADAM_LR * (m_hat / (jnp.sqrt(v_hat) + ADAM_EPS) + ADAM_WD * w), m, v


def _reduce_adamw(parts, w, m, v, name):
    R, C = w.shape
    _, Rp, Cp = parts.shape
    tr = min(R, 256)
    tp = tr if Rp == R else Rp

    def body(p_ref, w_ref, m_ref, v_ref, g_ref, d_ref, nm_ref, nv_ref):
        g = p_ref[0].astype(F32)
        for s in range(1, N_DEV):
            g = g + p_ref[s].astype(F32)
        g = g[:tr, :C]
        g_ref[...] = g
        d_ref[...], nm_ref[...], nv_ref[...] = _adamw(g, w_ref[...], m_ref[...], v_ref[...])

    spec = pl.BlockSpec((tr, C), lambda i: (i, 0))
    return pl.pallas_call(
        body, grid=(R // tr,), name=name,
        in_specs=[pl.BlockSpec((N_DEV, tp, Cp), lambda i: (0, i, 0)), spec, spec, spec],
        out_specs=[spec] * 4, out_shape=[SDS((R, C), F32)] * 4,
        compiler_params=_params(("arbitrary",)),
    )(parts, w, m, v)


SMALL_ROWS, SMALL_COLS = 16, 1024
SMALL_LAYOUT = (
    ("attn_norm_w", 0, 1, 1024, 1024), ("mlp_norm_w", 1, 1, 1024, 1024), ("q_lat_norm_w", 2, 1, 256, 256),
    ("kv_lat_norm_w", 3, 1, 256, 256), ("q_norm_w", 4, 1, 192, 192), ("k_norm_w", 5, 1, 192, 192),
    ("mla_out_norm_w", 6, 4, 128, 128), ("a_log", 10, 1, 128, 4), ("dt_bias", 11, 1, 128, 4),
    ("gdn_norm_w", 12, 1, 128, 128))


def _adamw_replicated(parts, ws, ms, vs):
    n = len(SMALL_LAYOUT)

    def body(*refs):
        p_ref = refs[0]
        w_refs, m_refs, v_refs = refs[1:1 + n], refs[1 + n:1 + 2 * n], refs[1 + 2 * n:1 + 3 * n]
        outs = refs[1 + 3 * n:]
        s = p_ref[0]
        for d in range(1, N_DEV):
            s = s + p_ref[d]
        for i, (_, r0, nr, _, pw) in enumerate(SMALL_LAYOUT):
            g = s[r0:r0 + nr, :pw]
            outs[i][...] = g
            outs[n + i][...], outs[2 * n + i][...], outs[3 * n + i][...] = _adamw(
                g, w_refs[i][...], m_refs[i][...], v_refs[i][...])

    res = pl.pallas_call(
        body, name="adamw_replicated",
        out_shape=[SDS(w.shape, F32) for w in ws] * 4,
        compiler_params=_params(),
    )(parts, *ws, *ms, *vs)
    return [res[k * n:(k + 1) * n] for k in range(4)]


COPIES_PER_ARRAY = N_DEV - 1


def _two_level_gather(srcs, outs, send_sems, recv_sems):
    mx, my, mc = lax.axis_index("x"), lax.axis_index("y"), lax.axis_index("c")
    me, sibling = (mx, my, mc), (mx, my, 1 - mc)
    chips = [(1 - mx, my), (mx, 1 - my), (1 - mx, 1 - my)]
    arrays = range(len(srcs))

    def copy(a, k, block, to, src=None):
        px, py, pc = block
        slot = outs[a].at[4 * px + 2 * py + pc]
        sem = a * COPIES_PER_ARRAY + k
        return pltpu.make_async_remote_copy(
            src_ref=slot if src is None else src, dst_ref=slot,
            send_sem=send_sems.at[sem], recv_sem=recv_sems.at[sem], device_id=to, device_id_type=MESH_ID)

    started = []
    for a in arrays:
        started.append(copy(a, 0, me, sibling, src=srcs[a]))
        started += [copy(a, 1 + j, me, (*chip, mc), src=srcs[a]) for j, chip in enumerate(chips)]
    for cp in started:
        cp.start()
    for j, chip in enumerate(chips):
        for a in arrays:
            copy(a, 1 + j, (*chip, mc), me).wait_recv()
            fwd = copy(a, 4 + j, (*chip, mc), sibling)
            fwd.start()
            started.append(fwd)
    for a in arrays:
        copy(a, 0, sibling, me).wait_recv()
    for j, chip in enumerate(chips):
        for a in arrays:
            copy(a, 4 + j, (*chip, 1 - mc), me).wait_recv()
    for cp in started:
        cp.wait_send()


def _gather_weights(shards):
    n = len(shards)

    def body(*refs):
        srcs, outs = refs[:n], refs[n:2 * n]
        send_sems, recv_sems, local_sems = refs[2 * n:]
        me = 4 * lax.axis_index("x") + 2 * lax.axis_index("y") + lax.axis_index("c")
        mine = [pltpu.make_async_copy(srcs[a], outs[a].at[me], local_sems.at[a]) for a in range(n)]
        for cp in mine:
            cp.start()
        _two_level_gather(srcs, outs, send_sems, recv_sems)
        for cp in mine:
            cp.wait()

    return pl.pallas_call(
        body, name="gather_weights",
        out_shape=[SDS((N_DEV,) + s.shape, s.dtype) for s in shards],
        in_specs=[pl.BlockSpec(memory_space=pl.ANY)] * n,
        out_specs=[pl.BlockSpec(memory_space=pl.ANY)] * n,
        scratch_shapes=[pltpu.SemaphoreType.DMA((n * COPIES_PER_ARRAY,)),
                        pltpu.SemaphoreType.DMA((n * COPIES_PER_ARRAY,)), pltpu.SemaphoreType.DMA((n,))],
    )(*shards)


def _gather_small_grads(gs):
    n = len(gs)

    def body(*refs):
        g_refs, out_ref = refs[:n], refs[n]
        tile, send_sems, recv_sems = refs[n + 1:]
        tile[...] = jnp.zeros_like(tile)
        for (_, r0, nr, gw, _), g in zip(SMALL_LAYOUT, g_refs):
            tile[r0:r0 + nr, 0:gw] = g[...]
        me = 4 * lax.axis_index("x") + 2 * lax.axis_index("y") + lax.axis_index("c")
        out_ref[me] = tile[...]
        _two_level_gather([tile], [out_ref], send_sems, recv_sems)

    return pl.pallas_call(
        body, name="gather_small_grads",
        out_shape=SDS((N_DEV, SMALL_ROWS, SMALL_COLS), F32),
        in_specs=[pl.BlockSpec(memory_space=pltpu.VMEM)] * n,
        out_specs=pl.BlockSpec(memory_space=pltpu.VMEM),
        scratch_shapes=[pltpu.VMEM((SMALL_ROWS, SMALL_COLS), F32),
                        pltpu.SemaphoreType.DMA((COPIES_PER_ARRAY,)), pltpu.SemaphoreType.DMA((COPIES_PER_ARRAY,))],
    )(*gs)


def _exchange_grads(slabs):
    n = len(slabs)
    flips = [(0, 0, 1), (1, 0, 0), (0, 1, 0), (1, 1, 0), (1, 0, 1), (0, 1, 1), (1, 1, 1)]

    def body(*refs):
        srcs, outs = refs[:n], refs[n:2 * n]
        send_sems, recv_sems, local_sems = refs[2 * n:]
        mx, my, mc = lax.axis_index("x"), lax.axis_index("y"), lax.axis_index("c")
        mine = [pltpu.make_async_copy(srcs[a].at[4 * mx + 2 * my + mc], outs[a].at[N_DEV - 1], local_sems.at[a])
                for a in range(n)]
        for cp in mine:
            cp.start()
        copies = []
        for k, (fx, fy, fc) in enumerate(flips):
            px = 1 - mx if fx else mx
            py = 1 - my if fy else my
            pc = 1 - mc if fc else mc
            for a in range(n):
                sem = a * COPIES_PER_ARRAY + k
                copies.append(pltpu.make_async_remote_copy(
                    src_ref=srcs[a].at[4 * px + 2 * py + pc], dst_ref=outs[a].at[k],
                    send_sem=send_sems.at[sem], recv_sem=recv_sems.at[sem],
                    device_id=(px, py, pc), device_id_type=MESH_ID))
        for cp in copies:
            cp.start()
        for cp in copies:
            cp.wait()
        for cp in mine:
            cp.wait()

    return pl.pallas_call(
        body, name="exchange_grads",
        out_shape=[SDS(s.shape, s.dtype) for s in slabs],
        in_specs=[pl.BlockSpec(memory_space=pl.ANY)] * n,
        out_specs=[pl.BlockSpec(memory_space=pl.ANY)] * n,
        scratch_shapes=[pltpu.SemaphoreType.DMA((n * COPIES_PER_ARRAY,)),
                        pltpu.SemaphoreType.DMA((n * COPIES_PER_ARRAY,)), pltpu.SemaphoreType.DMA((n,))],
    )(*slabs)


def _w_in_to_padded(w):
    z = lambda n: jnp.zeros((w.shape[0], n), w.dtype)
    return jnp.concatenate([w[:, O_GQKV:O_GZ], w[:, O_GZ:O_GAB], w[:, O_QLAT:O_KVLAT], w[:, O_KVLAT:O_KPE],
                            w[:, O_KPE:O_GQKV], z(P_GAB - P_KPE - ROPE), w[:, O_GAB:O_END],
                            z(P_WIDTH - P_GAB - (O_END - O_GAB))], axis=1)


def _w_in_from_padded(wp):
    return jnp.concatenate([wp[:, P_QLAT:P_QLAT + 256], wp[:, P_KVLAT:P_KVLAT + 256], wp[:, P_KPE:P_KPE + ROPE],
                            wp[:, P_GQKV:P_GZ], wp[:, P_GZ:P_QLAT], wp[:, P_GAB:P_GAB + (O_END - O_GAB)]], axis=1)


def _w_uq_to_headsplit(w):
    w3 = w.reshape(w.shape[0], MLA_HEADS, QK_DIM)
    return jnp.concatenate([w3[:, :, :NOPE].reshape(w.shape[0], -1), w3[:, :, NOPE:].reshape(w.shape[0], -1)], axis=1)


def _w_uq_from_headsplit(wp):
    n = wp[:, :MLA_HEADS * NOPE].reshape(wp.shape[0], MLA_HEADS, NOPE)
    p = wp[:, MLA_HEADS * NOPE:].reshape(wp.shape[0], MLA_HEADS, ROPE)
    return jnp.concatenate([n, p], axis=2).reshape(wp.shape[0], -1)


def _lane_vec(v4):
    return jnp.pad(v4.reshape(1, -1), ((0, 0), (0, LANES - v4.shape[-1])))


def _local_step(x, positions, target, attn_norm_w, w_in, q_lat_norm_w, w_uq, kv_lat_norm_w, w_ukv, q_norm_w,
                k_norm_w, mla_out_norm_w, conv_w, a_log, dt_bias, gdn_norm_w, w_out, mlp_norm_w, w_up, w_down):
    B, S, D = x.shape
    T = B * S
    x2 = x.reshape(T, D)
    t2 = target.reshape(T, D)
    half = ROPE // 2
    inv_freq = ROPE_THETA ** (-jnp.arange(half, dtype=F32) / half)
    ang = positions.reshape(T, 1).astype(F32) * inv_freq
    cosf = jnp.concatenate([jnp.cos(ang)] * 2, axis=-1)
    sinf = jnp.concatenate([jnp.sin(ang)] * 2, axis=-1)
    w_in_p = _w_in_to_padded(w_in)
    w_uq_p = _w_uq_to_headsplit(w_uq)
    alog_l, dt_l = _lane_vec(a_log), _lane_vec(dt_bias)
    w_an, w_qln, w_kvln, qnw, knw, w_mn, gdn_w = (
        attn_norm_w, q_lat_norm_w, kv_lat_norm_w, q_norm_w, k_norm_w, mlp_norm_w, gdn_norm_w)

    proj, xn = _in_proj(x2, w_an, w_in_p)
    q4, k4, v4 = _mla_pre(proj, cosf, sinf, w_qln, w_kvln, w_uq_p, w_ukv, qnw, knw)
    o_mla, lse = _attn_fwd(q4, k4, v4, B, S)
    qg, kg, vg, gates = _gdn_pre(proj, conv_w, alog_l, dt_l, S)
    o_gdn, states, ainv = _gdn_fwd(qg, kg, vg, gates, B, S)
    h2, mix = _mix_out(o_mla, o_gdn, proj, x2, mla_out_norm_w, gdn_w, w_out)
    up, hn, dy, sq = _mlp_fwd(h2, w_mn, w_up, w_down, t2)
    loss = (0.5 / D) * jnp.sum(sq[:, 0, 0])

    dh, dhb, dup, act, dyb, d_mlp_norm = _mlp_bwd(dy, up, h2, w_mn, w_up, w_down)
    g_w_down = _wgrad(act, dyb, "wgrad_down")
    g_w_up = _wgrad(hn, dup, "wgrad_up", column_shards=True)
    do_mla, do_gdn, dz, d_mla_w, d_gdn_w = _mix_bwd(dhb, o_mla, o_gdn, proj, mla_out_norm_w, gdn_w, w_out)
    g_w_out = _wgrad(mix, dhb, "wgrad_out")
    dq4, dk4, dv4 = _attn_bwd(q4, k4, v4, do_mla, o_mla, lse, B, S)
    dql, dkvl, dkpe, dqraw, dkvraw, qn, kvn, d_wqln, d_wkvln, d_qnw, d_knw = _mla_pre_bwd(
        proj, cosf, sinf, w_qln, w_kvln, w_uq_p, w_ukv, qnw, knw, dq4, dk4, dv4)
    g_w_uq_p = _wgrad(qn, dqraw, "wgrad_uq")
    g_w_ukv = _wgrad(kvn, dkvraw, "wgrad_ukv")
    dqg, dkg, dvg, dgb4 = _gdn_bwd(qg, kg, vg, gates, states, ainv, do_gdn, B, S)
    dc, dgab, g_conv, d_alog, d_dt = _gdn_pre_bwd(proj, conv_w, alog_l, dt_l, dqg, dkg, dvg, dgb4, S)
    dgqkv = _conv_bwd_input(dc, conv_w, S)
    grad_x2, dproj, d_attn_norm = _in_proj_bwd(dgqkv, dz, dql, dkvl, dkpe, dgab, w_in_p, dh, x2, w_an)
    g_w_in_p = _wgrad(xn, dproj, "wgrad_in")

    mats = dict(w_in=g_w_in_p, w_uq=g_w_uq_p, w_ukv=g_w_ukv, conv_w=g_conv, w_out=g_w_out, w_up=g_w_up,
                w_down=g_w_down)
    small = dict(attn_norm_w=d_attn_norm, mlp_norm_w=d_mlp_norm, q_lat_norm_w=d_wqln, kv_lat_norm_w=d_wkvln,
                 q_norm_w=d_qnw, k_norm_w=d_knw, mla_out_norm_w=d_mla_w, a_log=d_alog, dt_bias=d_dt,
                 gdn_norm_w=d_gdn_w)
    return loss, grad_x2.reshape(B, S, D), mats, [small[n] for n, *_ in SMALL_LAYOUT]


BIG = ("w_in", "w_uq", "w_ukv", "conv_w", "w_out", "w_up", "w_down")
ALL_W = ("attn_norm_w", "w_in", "q_lat_norm_w", "w_uq", "kv_lat_norm_w", "w_ukv", "q_norm_w", "k_norm_w",
         "mla_out_norm_w", "conv_w", "a_log", "dt_bias", "gdn_norm_w", "w_out", "mlp_norm_w", "w_up", "w_down")
WIRE_SHAPE = {"w_in": (1024, 384), "w_uq": (256, 128), "conv_w": (16, 256)}


def _pad2(a, rows, cols):
    return jnp.pad(a, [(0, 0)] * (a.ndim - 2) + [(0, rows - a.shape[-2]), (0, cols - a.shape[-1])])


def _cols_to_full(stack, cols):
    return jnp.moveaxis(stack[:, :, :cols], 0, 1).reshape(stack.shape[1], N_DEV * cols)


def _full_to_cols(full, wire_cols):
    r, n = full.shape
    return _pad2(jnp.moveaxis(full.reshape(r, N_DEV, n // N_DEV), 1, 0), r, wire_cols)


def kernel(x, positions, attn_norm_w, w_in, q_lat_norm_w, w_uq, kv_lat_norm_w, w_ukv, q_norm_w, k_norm_w, mla_out_norm_w, conv_w, a_log, dt_bias, gdn_norm_w, w_out, mlp_norm_w, w_up, w_down, loss_target, m_attn_norm_w, m_w_in, m_q_lat_norm_w, m_w_uq, m_kv_lat_norm_w, m_w_ukv, m_q_norm_w, m_k_norm_w, m_mla_out_norm_w, m_conv_w, m_a_log, m_dt_bias, m_gdn_norm_w, m_w_out, m_mlp_norm_w, m_w_up, m_w_down, v_attn_norm_w, v_w_in, v_q_lat_norm_w, v_w_uq, v_kv_lat_norm_w, v_w_ukv, v_q_norm_w, v_k_norm_w, v_mla_out_norm_w, v_conv_w, v_a_log, v_dt_bias, v_gdn_norm_w, v_w_out, v_mlp_norm_w, v_w_up, v_w_down):
    env = dict(locals())
    W = {n: env[n][0] for n in ALL_W}
    Mo = {n: env["m_" + n][0] for n in ALL_W}
    Vo = {n: env["v_" + n][0] for n in ALL_W}

    two_d = lambda a: a.reshape(1, -1) if a.ndim == 1 else a
    D = x.shape[-1]

    s_in, s_uq, s_ukv, s_conv, s_out, s_up, s_down = _gather_weights([
        _pad2(W["w_in"].astype(WIRE_DTYPE), *WIRE_SHAPE["w_in"]),
        _pad2(W["w_uq"].astype(WIRE_DTYPE), *WIRE_SHAPE["w_uq"]),
        W["w_ukv"].astype(WIRE_DTYPE), _pad2(W["conv_w"], *WIRE_SHAPE["conv_w"]),
        W["w_out"].astype(WIRE_DTYPE), W["w_up"].astype(WIRE_DTYPE), W["w_down"].astype(WIRE_DTYPE)])

    loss, grad_x, gm, gs = _local_step(
        x, positions, loss_target, two_d(W["attn_norm_w"]), _cols_to_full(s_in, W["w_in"].shape[1]),
        two_d(W["q_lat_norm_w"]), _cols_to_full(s_uq, W["w_uq"].shape[1]), two_d(W["kv_lat_norm_w"]),
        _cols_to_full(s_ukv, W["w_ukv"].shape[1]), two_d(W["q_norm_w"]), two_d(W["k_norm_w"]),
        W["mla_out_norm_w"], _cols_to_full(s_conv[:, :CONV_W], W["conv_w"].shape[1]), two_d(W["a_log"]),
        two_d(W["dt_bias"]), two_d(W["gdn_norm_w"]), s_out.reshape(-1, D), two_d(W["mlp_norm_w"]), s_up,
        s_down.reshape(-1, D))
    loss = lax.psum(loss, ("x", "y", "c"))

    slabs = [
        _full_to_cols(_w_in_from_padded(gm["w_in"]), WIRE_SHAPE["w_in"][1]),
        _full_to_cols(_w_uq_from_headsplit(gm["w_uq"]), WIRE_SHAPE["w_uq"][1]),
        _full_to_cols(gm["w_ukv"], W["w_ukv"].shape[1]),
        _pad2(_full_to_cols(gm["conv_w"].astype(WIRE_DTYPE), W["conv_w"].shape[1]), *WIRE_SHAPE["conv_w"]),
        gm["w_out"].reshape(N_DEV, -1, D), gm["w_up"], gm["w_down"].reshape(N_DEV, -1, D)]
    parts = _exchange_grads(slabs)
    done = {n: _reduce_adamw(p, W[n], Mo[n], Vo[n], "adamw_" + n) for n, p in zip(BIG, parts)}
    names = [n for n, *_ in SMALL_LAYOUT]
    small = _adamw_replicated(_gather_small_grads(gs), [two_d(W[n]) for n in names], [two_d(Mo[n]) for n in names],
                              [two_d(Vo[n]) for n in names])
    for i, n in enumerate(names):
        done[n] = [small[kind][i] for kind in range(4)]
    res = [done[n][kind].reshape(env[n].shape) for kind in range(4) for n in ALL_W]
    return (loss, grad_x, *res)
```

```python
import jax
import jax.numpy as jnp
from jax import lax
from jax.experimental import pallas as pl
from jax.experimental.pallas import tpu as pltpu

F32 = jnp.float32
MXU_DTYPE = jnp.bfloat16
WIRE_DTYPE = jnp.bfloat16
SDS = jax.ShapeDtypeStruct
HIGHEST = lax.Precision.HIGHEST
MESH_ID = pl.DeviceIdType.MESH

D_MODEL = 1024
MLA_HEADS = 4
Q_LORA = 256
KV_LORA = 256
NOPE = 128
ROPE = 64
QK_DIM = NOPE + ROPE
V_DIM = 128
ROPE_THETA = 10000.0
GDN_HEADS = 4
GDN_DIM = 128
GDN_WIDTH = GDN_HEADS * GDN_DIM
CONV_W = 4
CHUNK = 64
D_FF = 4 * D_MODEL
EPS = 1e-6
ATT_SCALE = QK_DIM ** -0.5
GDN_QSCALE = GDN_DIM ** -0.5
N_DEV = 8

ADAM_LR = 0.001
ADAM_B1 = 0.9
ADAM_B2 = 0.999
ADAM_EPS = 1e-08
ADAM_WD = 0.01
ADAM_STEP = 10

LANES = 128
SUBLANES = 8
VMEM_LIMIT = 56 * 1024 * 1024

P_GQKV, P_GZ, P_QLAT, P_KVLAT, P_KPE, P_GAB = 0, 1536, 2048, 2304, 2560, 2688
P_WIDTH = 2816
O_QLAT, O_KVLAT, O_KPE, O_GQKV, O_GZ, O_GAB, O_END = 0, 256, 512, 576, 2112, 2624, 2632


def _params(sem=None, vmem=VMEM_LIMIT):
    kw = dict(vmem_limit_bytes=vmem)
    if sem is not None:
        kw["dimension_semantics"] = sem
    return pltpu.CompilerParams(**kw)


def _mm(a, b):
    return jnp.dot(a.astype(MXU_DTYPE), b.astype(MXU_DTYPE), preferred_element_type=F32)


def _mm_nt(a, b):
    return lax.dot_general(a.astype(MXU_DTYPE), b.astype(MXU_DTYPE), (((1,), (1,)), ((), ())),
                           preferred_element_type=F32)


def _mm_tn(a, b):
    return lax.dot_general(a.astype(MXU_DTYPE), b.astype(MXU_DTYPE), (((0,), (0,)), ((), ())),
                           preferred_element_type=F32)


def _split(a):
    hi = a.astype(MXU_DTYPE)
    return hi, (a - hi.astype(F32)).astype(MXU_DTYPE)


def _mm_split(a, b):
    (ah, al), (bh, bl) = a, b
    dot = lambda x, y: jnp.dot(x, y, preferred_element_type=F32)
    if MXU_DTYPE == F32:
        return dot(ah, bh)
    return dot(ah, bh) + dot(ah, bl) + dot(al, bh)


def _mm_exact(a, b):
    return _mm_split(_split(a), _split(b))


def _rms(x, w):
    r = lax.rsqrt(jnp.mean(x * x, axis=-1, keepdims=True) + EPS)
    return x * r * w, r


def _rms_bwd(dy, x, w, r):
    xh = x * r
    dyw = dy * w
    dx = r * (dyw - xh * jnp.mean(dyw * xh, axis=-1, keepdims=True))
    dw = jnp.sum(dy * xh, axis=0, keepdims=True)
    return dx, dw


def _l2n_bwd(dy, x, scale):
    r = lax.rsqrt(jnp.sum(x * x, axis=-1, keepdims=True) + EPS)
    xh = x * r
    return (scale * r) * (dy - xh * jnp.sum(dy * xh, axis=-1, keepdims=True))


def _rot(t):
    return jnp.concatenate([-t[:, ROPE // 2:], t[:, :ROPE // 2]], axis=-1)


def _rot_t(t):
    return jnp.concatenate([t[:, ROPE // 2:], -t[:, :ROPE // 2]], axis=-1)


def _rope(t, cos, sin):
    return t * cos + _rot(t) * sin


def _rope_bwd(d, cos, sin):
    return d * cos + _rot_t(d * sin)


def _sigmoid(x):
    return jax.nn.sigmoid(x)


def _shift_down(x, halo, j):
    if j == 0:
        return x
    xr = pltpu.roll(x, j, 0)
    hr = pltpu.roll(halo, j, 0)
    row = lax.broadcasted_iota(jnp.int32, halo.shape, 0)
    top = jnp.where(row < j, hr, xr[:SUBLANES])
    return jnp.concatenate([top, xr[SUBLANES:]], axis=0)


def _shift_up(x, nxt, j):
    if j == 0:
        return x
    n = x.shape[0]
    xr = pltpu.roll(x, n - j, 0)
    nr = pltpu.roll(nxt, SUBLANES - j, 0)
    row = lax.broadcasted_iota(jnp.int32, nxt.shape, 0)
    bot = jnp.where(row >= SUBLANES - j, nr, xr[n - SUBLANES:])
    return jnp.concatenate([xr[:n - SUBLANES], bot], axis=0)


def _chunk_cumsum(y, row_in_chunk):
    s = 1
    while s < CHUNK:
        y = y + jnp.where(row_in_chunk >= s, pltpu.roll(y, s, 0), 0.0)
        s *= 2
    return y


def _chunk_rev_cumsum(y, row_in_chunk):
    n = y.shape[0]
    s = 1
    while s < CHUNK:
        y = y + jnp.where(row_in_chunk + s < CHUNK, pltpu.roll(y, n - s, 0), 0.0)
        s *= 2
    return y


def _lockstep(generators):
    alive = list(generators)
    while alive:
        nxt = []
        for g in alive:
            try:
                next(g)
                nxt.append(g)
            except StopIteration:
                pass
        alive = nxt


def _pick_lane(tile, lane, idx):
    return jnp.sum(jnp.where(lane == idx, tile, 0.0), axis=-1, keepdims=True)


def _divisor_tile(n, cap, unit=LANES):
    best = unit
    t = unit
    while t <= min(n, cap):
        if n % t == 0:
            best = t
        t += unit
    return n if n <= cap else best


def _in_proj(x2, w_an, w_in_p):
    T, D = x2.shape
    N = w_in_p.shape[1]
    tm = min(512, T)

    def body(x_ref, wn_ref, w_ref, proj_ref, xn_ref):
        xn, _ = _rms(x_ref[...], wn_ref[...])
        xn = xn.astype(MXU_DTYPE)
        xn_ref[...] = xn
        proj_ref[...] = jnp.dot(xn, w_ref[...], preferred_element_type=F32)

    return pl.pallas_call(
        body, grid=(T // tm,), name="in_proj",
        in_specs=[pl.BlockSpec((tm, D), lambda i: (i, 0)), pl.BlockSpec((1, D), lambda i: (0, 0)),
                  pl.BlockSpec((D, N), lambda i: (0, 0))],
        out_specs=[pl.BlockSpec((tm, N), lambda i: (i, 0)), pl.BlockSpec((tm, D), lambda i: (i, 0))],
        out_shape=[SDS((T, N), F32), SDS((T, D), MXU_DTYPE)],
        compiler_params=_params(("arbitrary",)),
    )(x2, w_an, w_in_p)


def _mla_pre(proj, cosf, sinf, w_qln, w_kvln, w_uq_p, w_ukv, qnw, knw):
    T = proj.shape[0]
    tm = min(256, T)
    H = MLA_HEADS

    def body(ql_ref, kvl_ref, kpe_ref, cos_ref, sin_ref, wq_ref, wkv_ref, uq_ref, ukv_ref, qnw_ref, knw_ref,
             q_out, k_out, v_out):
        cos, sin = cos_ref[...], sin_ref[...]
        qnw_, knw_ = qnw_ref[...], knw_ref[...]
        qn, _ = _rms(ql_ref[...], wq_ref[...])
        kvn, _ = _rms(kvl_ref[...], wkv_ref[...])
        qraw = _mm(qn, uq_ref[...])
        kvraw = _mm(kvn, ukv_ref[...])
        kpe = _rope(_rms(kpe_ref[...][:, :ROPE], knw_[:, NOPE:])[0], cos, sin)
        for h in range(H):
            qn_h = _rms(qraw[:, h * NOPE:(h + 1) * NOPE], qnw_[:, :NOPE])[0]
            qp_h = _rope(_rms(qraw[:, H * NOPE + h * ROPE:H * NOPE + (h + 1) * ROPE], qnw_[:, NOPE:])[0], cos, sin)
            q_out[h] = (jnp.concatenate([qn_h, qp_h], axis=-1) * ATT_SCALE).astype(MXU_DTYPE)
            kn_h = _rms(kvraw[:, h * 256:h * 256 + NOPE], knw_[:, :NOPE])[0]
            k_out[h] = jnp.concatenate([kn_h, kpe], axis=-1).astype(MXU_DTYPE)
            v_out[h] = kvraw[:, h * 256 + NOPE:(h + 1) * 256].astype(MXU_DTYPE)

    full = lambda a: pl.BlockSpec(a.shape, lambda i: (0,) * a.ndim)
    return pl.pallas_call(
        body, grid=(T // tm,), name="mla_pre",
        in_specs=[pl.BlockSpec((tm, 256), lambda i: (i, P_QLAT // 256)),
                  pl.BlockSpec((tm, 256), lambda i: (i, P_KVLAT // 256)),
                  pl.BlockSpec((tm, 128), lambda i: (i, P_KPE // 128)),
                  pl.BlockSpec((tm, ROPE), lambda i: (i, 0)), pl.BlockSpec((tm, ROPE), lambda i: (i, 0)),
                  full(w_qln), full(w_kvln), full(w_uq_p), full(w_ukv), full(qnw), full(knw)],
        out_specs=[pl.BlockSpec((H, tm, QK_DIM), lambda i: (0, i, 0)),
                   pl.BlockSpec((H, tm, QK_DIM), lambda i: (0, i, 0)),
                   pl.BlockSpec((H, tm, V_DIM), lambda i: (0, i, 0))],
        out_shape=[SDS((H, T, QK_DIM), MXU_DTYPE), SDS((H, T, QK_DIM), MXU_DTYPE), SDS((H, T, V_DIM), MXU_DTYPE)],
        compiler_params=_params(("arbitrary",)),
    )(proj, proj, proj, cosf, sinf, w_qln, w_kvln, w_uq_p, w_ukv, qnw, knw)


def _attn_fwd(q4, k4, v4, B, S):
    H = MLA_HEADS
    bq = min(256, S)
    nq = S // bq

    def body(q_ref, k_ref, v_ref, o_ref, lse_ref):
        causal = (lax.broadcasted_iota(jnp.int32, (bq, bq), 1) <= lax.broadcasted_iota(jnp.int32, (bq, bq), 0))

        def q_step(qi, carry):
            qs = pl.multiple_of(qi * bq, bq)
            q = q_ref[0, pl.ds(qs, bq), :]

            def k_block(ks, c, diagonal):
                m, l, acc = c
                k = k_ref[0, pl.ds(ks, bq), :]
                v = v_ref[0, pl.ds(ks, bq), :]
                s = _mm_nt(q, k)
                if diagonal:
                    s = jnp.where(causal, s, -jnp.inf)
                m_new = jnp.maximum(m, jnp.max(s, axis=-1, keepdims=True))
                p = jnp.exp(s - m_new)
                a = jnp.exp(m - m_new)
                return m_new, a * l + jnp.sum(p, axis=-1, keepdims=True), a * acc + _mm(p, v)

            c = lax.fori_loop(
                0, qi, lambda kj, c: k_block(pl.multiple_of(kj * bq, bq), c, False),
                (jnp.full((bq, 1), -jnp.inf, F32), jnp.zeros((bq, 1), F32), jnp.zeros((bq, V_DIM), F32)))
            m, l, acc = k_block(qs, c, True)
            o_ref[0, pl.ds(qs, bq), :] = acc / l
            lse_ref[0, pl.ds(qs, bq), :] = m + jnp.log(l)
            return carry

        lax.fori_loop(0, nq, q_step, 0)

    spec = lambda d: pl.BlockSpec((1, S, d), lambda h, b: (h, b, 0))
    return pl.pallas_call(
        body, grid=(H, B), name="attn_fwd",
        in_specs=[spec(QK_DIM), spec(QK_DIM), spec(V_DIM)],
        out_specs=[spec(V_DIM), spec(1)],
        out_shape=[SDS((H, B * S, V_DIM), F32), SDS((H, B * S, 1), F32)],
        compiler_params=_params(("arbitrary", "arbitrary")),
    )(q4, k4, v4)


def _conv_taps(u, halo, w):
    sh = [_shift_down(u, halo, j) for j in range(CONV_W)]
    c = w[0:1] * sh[3] + w[1:2] * sh[2] + w[2:3] * sh[1] + w[3:4] * sh[0]
    return c, sh


def _gate_values(gab, alog_l, dt_l, lane):
    g = -jnp.exp(alog_l) * jax.nn.softplus(gab + dt_l)
    g = jnp.where(lane < GDN_HEADS, g, 0.0)
    beta = jnp.where((lane >= GDN_HEADS) & (lane < 2 * GDN_HEADS), _sigmoid(gab), 0.0)
    return g, beta


def _gdn_pre(proj, conv_w, alog_l, dt_l, S):
    T = proj.shape[0]
    tm = min(256, T)
    tiles_per_seq = S // tm
    C3 = 3 * GDN_WIDTH
    H = GDN_HEADS

    def body(u_ref, halo_ref, gab_ref, w_ref, alog_ref, dt_ref, q_out, k_out, v_out, gates_out):
        i = pl.program_id(0)
        halo = jnp.where(i % tiles_per_seq == 0, 0.0, halo_ref[...])
        c, _ = _conv_taps(u_ref[...], halo, w_ref[...])
        a = c * _sigmoid(c)
        for h in range(H):
            xq = a[:, h * GDN_DIM:(h + 1) * GDN_DIM]
            xk = a[:, GDN_WIDTH + h * GDN_DIM:GDN_WIDTH + (h + 1) * GDN_DIM]
            q_out[h] = xq * lax.rsqrt(jnp.sum(xq * xq, axis=-1, keepdims=True) + EPS) * GDN_QSCALE
            k_out[h] = xk * lax.rsqrt(jnp.sum(xk * xk, axis=-1, keepdims=True) + EPS)
            v_out[h] = a[:, 2 * GDN_WIDTH + h * GDN_DIM:2 * GDN_WIDTH + (h + 1) * GDN_DIM]
        lane = lax.broadcasted_iota(jnp.int32, (tm, LANES), 1)
        ric = lax.broadcasted_iota(jnp.int32, (tm, LANES), 0) % CHUNK
        g, beta = _gate_values(gab_ref[...], alog_ref[...], dt_ref[...], lane)
        gates_out[...] = _chunk_cumsum(g, ric) + beta

    hspec = pl.BlockSpec((H, tm, GDN_DIM), lambda i: (0, i, 0))
    return pl.pallas_call(
        body, grid=(T // tm,), name="gdn_pre",
        in_specs=[pl.BlockSpec((tm, C3), lambda i: (i, 0)),
                  pl.BlockSpec((SUBLANES, C3), lambda i: (jnp.maximum(i * (tm // SUBLANES) - 1, 0), 0)),
                  pl.BlockSpec((tm, LANES), lambda i: (i, P_GAB // LANES)),
                  pl.BlockSpec((CONV_W, C3), lambda i: (0, 0)),
                  pl.BlockSpec((1, LANES), lambda i: (0, 0)), pl.BlockSpec((1, LANES), lambda i: (0, 0))],
        out_specs=[hspec, hspec, hspec, pl.BlockSpec((tm, LANES), lambda i: (i, 0))],
        out_shape=[SDS((H, T, GDN_DIM), F32)] * 3 + [SDS((T, LANES), F32)],
        compiler_params=_params(("arbitrary",)),
    )(proj, proj, proj, conv_w, alog_l, dt_l)


def _unit_lower_inverses(Ls, eye):
    Ps = [eye - L for L in Ls]
    Ms = [_split(-L) for L in Ls]
    for _ in range(5):
        sq = [_mm_split(m, m) for m in Ms]
        Ms = [_split(s) for s in sq]
        Ps = [p + _mm_split(_split(p), m) for p, m in zip(Ps, Ms)]
    return Ps


def _chunk_decays(gt, lane, h, ri, ci, rcol):
    Gc = _pick_lane(gt, lane, h)
    bt = _pick_lane(gt, lane, h + GDN_HEADS)
    Gb = jnp.broadcast_to(Gc, (CHUNK, CHUNK))
    Gam = jnp.where(ri >= ci, jnp.exp(Gb - Gb.T), 0.0)
    Gl = jnp.sum(jnp.where(rcol == CHUNK - 1, Gc, 0.0), axis=0, keepdims=True)
    return Gc, bt, Gam, jnp.exp(Gc), jnp.exp(Gl - Gc), jnp.exp(Gl)


GDN_UNROLL = 4


def _gdn_fwd(qg, kg, vg, gates, B, S):
    H, D, C = GDN_HEADS, GDN_DIM, CHUNK
    NC = S // C
    U = GDN_UNROLL if NC % GDN_UNROLL == 0 else 1

    def body(q_ref, k_ref, v_ref, g_ref, o_ref, st_ref, ai_ref, u_ref, w_ref, q2_s, au_s, bc_s, w2_s, el_s):
        h = pl.program_id(0)
        lane = lax.broadcasted_iota(jnp.int32, (C, LANES), 1)
        ri = lax.broadcasted_iota(jnp.int32, (C, C), 0)
        ci = lax.broadcasted_iota(jnp.int32, (C, C), 1)
        rcol = lax.broadcasted_iota(jnp.int32, (C, 1), 0)
        eye = (ri == ci).astype(F32)

        def group(gi, c):
            ns = [gi * U + j for j in range(U)]
            css = [pl.multiple_of(n * C, C) for n in ns]
            qs = [q_ref[0, pl.ds(cs, C), :] for cs in css]
            ks = [k_ref[0, pl.ds(cs, C), :] for cs in css]
            vs = [v_ref[0, pl.ds(cs, C), :] for cs in css]
            decs = [_chunk_decays(g_ref[pl.ds(cs, C), :], lane, h, ri, ci, rcol) for cs in css]
            qks = [_mm_nt(jnp.concatenate([q, k], axis=0), k) for q, k in zip(qs, ks)]
            ainvs = _unit_lower_inverses(
                [jnp.where(ri > ci, d[1] * qk[C:] * d[2], 0.0) for qk, d in zip(qks, decs)], eye)
            sols = [_mm_exact(a, jnp.concatenate([v * d[1], k * (d[1] * d[3])], axis=-1))
                    for a, k, v, d in zip(ainvs, ks, vs, decs)]
            atuw = [_mm(qk[:C] * d[2], sol) for qk, d, sol in zip(qks, decs, sols)]
            kduw = [_mm_tn(k * d[4], sol) for k, d, sol in zip(ks, decs, sols)]
            for n, cs, q, a, sol, au, ku, (Gc, bt, Gam, e, f, eL) in zip(ns, css, qs, ainvs, sols, atuw, kduw, decs):
                u_ref[0, pl.ds(cs, C), :] = sol[:, :D]
                w_ref[0, pl.ds(cs, C), :] = sol[:, D:]
                au_s[pl.ds(cs, C), :] = au[:, :D]
                q2_s[pl.ds(cs, C), :] = q * e - au[:, D:]
                bc_s[n] = ku[:, :D]
                w2_s[n] = ku[:, D:]
                el_s[n] = jnp.broadcast_to(eL, (SUBLANES, LANES))
                ai_ref[0, n] = a
            return c

        lax.fori_loop(0, NC // U, group, 0)

        def step(n, S_):
            cs = pl.multiple_of(n * C, C)
            o_ref[0, pl.ds(cs, C), :] = _mm(q2_s[pl.ds(cs, C), :], S_) + au_s[pl.ds(cs, C), :]
            st_ref[0, n] = S_
            return S_ * el_s[n, 0:1, :] + bc_s[n] - _mm(w2_s[n], S_)

        lax.fori_loop(0, NC, step, jnp.zeros((D, D), F32))

    spec = pl.BlockSpec((1, S, D), lambda h, b: (h, b, 0))
    return pl.pallas_call(
        body, grid=(H, B), name="gdn_fwd",
        in_specs=[spec, spec, spec, pl.BlockSpec((S, LANES), lambda h, b: (b, 0))],
        out_specs=[spec, pl.BlockSpec((1, NC, D, D), lambda h, b: (h, b, 0, 0)),
                   pl.BlockSpec((1, NC, C, C), lambda h, b: (h, b, 0, 0)), spec, spec],
        out_shape=[SDS((H, B * S, D), F32), SDS((H, B * NC, D, D), F32), SDS((H, B * NC, C, C), F32),
                   SDS((H, B * S, D), F32), SDS((H, B * S, D), F32)],
        scratch_shapes=[pltpu.VMEM((S, D), F32), pltpu.VMEM((S, D), F32), pltpu.VMEM((NC, D, D), F32),
                        pltpu.VMEM((NC, D, D), F32), pltpu.VMEM((NC, SUBLANES, LANES), F32)],
        compiler_params=_params(("arbitrary", "arbitrary")),
    )(qg, kg, vg, gates)


def _mix_out(o_mla, o_gdn, proj, x2, mla_w, gdn_w, w_out):
    T, D = x2.shape
    tm = min(512, T)
    H = MLA_HEADS

    def body(om_ref, og_ref, z_ref, x_ref, mw_ref, gw_ref, w_ref, h_ref, mix_ref):
        z = z_ref[...]
        parts = [_rms(om_ref[h], mw_ref[h:h + 1, :])[0] for h in range(H)]
        for h in range(GDN_HEADS):
            zh = z[:, h * GDN_DIM:(h + 1) * GDN_DIM]
            parts.append(_rms(og_ref[h], gw_ref[...])[0] * (zh * _sigmoid(zh)))
        mix = jnp.concatenate(parts, axis=-1).astype(MXU_DTYPE)
        mix_ref[...] = mix
        h_ref[...] = x_ref[...] + jnp.dot(mix, w_ref[...], preferred_element_type=F32)

    hspec = pl.BlockSpec((H, tm, V_DIM), lambda i: (0, i, 0))
    return pl.pallas_call(
        body, grid=(T // tm,), name="mix_out",
        in_specs=[hspec, hspec, pl.BlockSpec((tm, GDN_WIDTH), lambda i: (i, P_GZ // GDN_WIDTH)),
                  pl.BlockSpec((tm, D), lambda i: (i, 0)),
                  pl.BlockSpec((H, V_DIM), lambda i: (0, 0)), pl.BlockSpec((1, GDN_DIM), lambda i: (0, 0)),
                  pl.BlockSpec((D, D), lambda i: (0, 0))],
        out_specs=[pl.BlockSpec((tm, D), lambda i: (i, 0)), pl.BlockSpec((tm, D), lambda i: (i, 0))],
        out_shape=[SDS((T, D), F32), SDS((T, D), MXU_DTYPE)],
        compiler_params=_params(("arbitrary",)),
    )(o_mla, o_gdn, proj, x2, mla_w, gdn_w, w_out)


def _mlp_fwd(h2, w_mn, w_up, w_down, target):
    T, D = h2.shape
    nf, _, tf = w_up.shape
    F = nf * tf
    tm = min(512, T)

    def body(h_ref, wn_ref, up_w, down_w, t_ref, up_ref, hn_ref, dy_ref, loss_ref, y_acc):
        j = pl.program_id(1)

        @pl.when(j == 0)
        def _():
            hn_ref[...] = _rms(h_ref[...], wn_ref[...])[0].astype(MXU_DTYPE)
            y_acc[...] = h_ref[...]

        up = jnp.dot(hn_ref[...], up_w[0], preferred_element_type=F32)
        up_ref[...] = up
        r = jnp.maximum(up, 0.0)
        y_acc[...] += _mm(r * r, down_w[...])

        @pl.when(j == nf - 1)
        def _():
            err = y_acc[...] - t_ref[...]
            dy_ref[...] = err / D
            loss_ref[...] = jnp.full((1, SUBLANES, LANES), jnp.sum(err * err), F32)

    return pl.pallas_call(
        body, grid=(T // tm, nf), name="mlp_fwd",
        in_specs=[pl.BlockSpec((tm, D), lambda i, j: (i, 0)), pl.BlockSpec((1, D), lambda i, j: (0, 0)),
                  pl.BlockSpec((1, D, tf), lambda i, j: (j, 0, 0)), pl.BlockSpec((tf, D), lambda i, j: (j, 0)),
                  pl.BlockSpec((tm, D), lambda i, j: (i, 0))],
        out_specs=[pl.BlockSpec((tm, tf), lambda i, j: (i, j)), pl.BlockSpec((tm, D), lambda i, j: (i, 0)),
                   pl.BlockSpec((tm, D), lambda i, j: (i, 0)),
                   pl.BlockSpec((1, SUBLANES, LANES), lambda i, j: (i, 0, 0))],
        out_shape=[SDS((T, F), F32), SDS((T, D), MXU_DTYPE), SDS((T, D), F32),
                   SDS((T // tm, SUBLANES, LANES), F32)],
        scratch_shapes=[pltpu.VMEM((tm, D), F32)],
        compiler_params=_params(("arbitrary", "arbitrary")),
    )(h2, w_mn, w_up, w_down, target)


def _mlp_bwd(dy, up, h2, w_mn, w_up, w_down):
    T, D = h2.shape
    nf, _, tf = w_up.shape
    F = nf * tf
    tm = min(512, T)

    def body(dy_ref, up_ref, h_ref, wn_ref, up_w, down_w, dh_ref, dhb_ref, dup_ref, act_ref, dyb_ref, dwn_ref, acc):
        i, j = pl.program_id(0), pl.program_id(1)

        @pl.when((i == 0) & (j == 0))
        def _():
            dwn_ref[...] = jnp.zeros_like(dwn_ref)

        @pl.when(j == 0)
        def _():
            acc[...] = jnp.zeros_like(acc)
            dyb_ref[...] = dy_ref[...].astype(MXU_DTYPE)

        r = jnp.maximum(up_ref[...], 0.0)
        act_ref[...] = (r * r).astype(MXU_DTYPE)
        dup = (_mm_nt(dyb_ref[...], down_w[...]) * (2.0 * r)).astype(MXU_DTYPE)
        dup_ref[...] = dup
        acc[...] += _mm_nt(dup, up_w[0])

        @pl.when(j == nf - 1)
        def _():
            hv = h_ref[...]
            _, rr = _rms(hv, wn_ref[...])
            dx, dw = _rms_bwd(acc[...], hv, wn_ref[...], rr)
            dh = dy_ref[...] + dx
            dh_ref[...] = dh
            dhb_ref[...] = dh.astype(MXU_DTYPE)
            dwn_ref[...] += dw

    row = lambda i, j: (i, 0)
    return pl.pallas_call(
        body, grid=(T // tm, nf), name="mlp_bwd",
        in_specs=[pl.BlockSpec((tm, D), row), pl.BlockSpec((tm, tf), lambda i, j: (i, j)), pl.BlockSpec((tm, D), row),
                  pl.BlockSpec((1, D), lambda i, j: (0, 0)),
                  pl.BlockSpec((1, D, tf), lambda i, j: (j, 0, 0)), pl.BlockSpec((tf, D), lambda i, j: (j, 0))],
        out_specs=[pl.BlockSpec((tm, D), row), pl.BlockSpec((tm, D), row),
                   pl.BlockSpec((tm, tf), lambda i, j: (i, j)), pl.BlockSpec((tm, tf), lambda i, j: (i, j)),
                   pl.BlockSpec((tm, D), row), pl.BlockSpec((1, D), lambda i, j: (0, 0))],
        out_shape=[SDS((T, D), F32), SDS((T, D), MXU_DTYPE), SDS((T, F), MXU_DTYPE), SDS((T, F), MXU_DTYPE),
                   SDS((T, D), MXU_DTYPE), SDS((1, D), F32)],
        scratch_shapes=[pltpu.VMEM((tm, D), F32)],
        compiler_params=_params(("arbitrary", "arbitrary")),
    )(dy, up, h2, w_mn, w_up, w_down)


def _mix_bwd(dhb, o_mla, o_gdn, proj, mla_w, gdn_w, w_out):
    T, D = dhb.shape
    tm = min(512, T)
    H = MLA_HEADS

    def body(dh_ref, om_ref, og_ref, z_ref, mw_ref, gw_ref, w_ref, dom_ref, dog_ref, dz_ref, dmw_ref, dgw_ref):
        @pl.when(pl.program_id(0) == 0)
        def _():
            dmw_ref[...] = jnp.zeros_like(dmw_ref)
            dgw_ref[...] = jnp.zeros_like(dgw_ref)

        dmix = _mm_nt(dh_ref[...], w_ref[...])
        z = z_ref[...]
        dmw, dzs = [], []
        dgw = jnp.zeros((1, GDN_DIM), F32)
        for h in range(H):
            o = om_ref[h]
            w = mw_ref[h:h + 1, :]
            _, r = _rms(o, w)
            dx, dw = _rms_bwd(dmix[:, h * V_DIM:(h + 1) * V_DIM], o, w, r)
            dom_ref[h] = dx
            dmw.append(dw)
        for h in range(GDN_HEADS):
            o = og_ref[h]
            w = gw_ref[...]
            zh = z[:, h * GDN_DIM:(h + 1) * GDN_DIM]
            sg = _sigmoid(zh)
            yn, r = _rms(o, w)
            dy = dmix[:, H * V_DIM + h * GDN_DIM:H * V_DIM + (h + 1) * GDN_DIM]
            dzs.append(dy * yn * (sg * (1.0 + zh * (1.0 - sg))))
            dx, dw = _rms_bwd(dy * (zh * sg), o, w, r)
            dog_ref[h] = dx
            dgw = dgw + dw
        dz_ref[...] = jnp.concatenate(dzs, axis=-1)
        dmw_ref[...] += jnp.concatenate(dmw, axis=0)
        dgw_ref[...] += dgw

    hspec = pl.BlockSpec((H, tm, V_DIM), lambda i: (0, i, 0))
    return pl.pallas_call(
        body, grid=(T // tm,), name="mix_bwd",
        in_specs=[pl.BlockSpec((tm, D), lambda i: (i, 0)), hspec, hspec,
                  pl.BlockSpec((tm, GDN_WIDTH), lambda i: (i, P_GZ // GDN_WIDTH)),
                  pl.BlockSpec((H, V_DIM), lambda i: (0, 0)), pl.BlockSpec((1, GDN_DIM), lambda i: (0, 0)),
                  pl.BlockSpec((D, D), lambda i: (0, 0))],
        out_specs=[hspec, hspec, pl.BlockSpec((tm, GDN_WIDTH), lambda i: (i, 0)),
                   pl.BlockSpec((H, V_DIM), lambda i: (0, 0)), pl.BlockSpec((1, GDN_DIM), lambda i: (0, 0))],
        out_shape=[SDS((H, T, V_DIM), F32), SDS((H, T, GDN_DIM), F32), SDS((T, GDN_WIDTH), F32),
                   SDS((H, V_DIM), F32), SDS((1, GDN_DIM), F32)],
        compiler_params=_params(("arbitrary",)),
    )(dhb, o_mla, o_gdn, proj, mla_w, gdn_w, w_out)


def _attn_bwd(q4, k4, v4, do4, o4, lse4, B, S):
    H = MLA_HEADS
    bq = min(256, S)
    nq = S // bq

    def body(q_ref, k_ref, v_ref, do_ref, o_ref, lse_ref, dq_ref, dk_ref, dv_ref, delta):
        dq_ref[...] = jnp.zeros_like(dq_ref)
        dk_ref[...] = jnp.zeros_like(dk_ref)
        dv_ref[...] = jnp.zeros_like(dv_ref)
        delta[...] = jnp.sum(do_ref[0] * o_ref[0], axis=-1, keepdims=True)

        causal = (lax.broadcasted_iota(jnp.int32, (bq, bq), 1) <= lax.broadcasted_iota(jnp.int32, (bq, bq), 0))

        def k_step(kj, carry):
            ks = pl.multiple_of(kj * bq, bq)
            k = k_ref[0, pl.ds(ks, bq), :]
            v = v_ref[0, pl.ds(ks, bq), :]

            def q_block(qs, diagonal):
                q = q_ref[0, pl.ds(qs, bq), :]
                do = do_ref[0, pl.ds(qs, bq), :].astype(MXU_DTYPE)
                p = jnp.exp(_mm_nt(q, k) - lse_ref[0, pl.ds(qs, bq), :])
                if diagonal:
                    p = jnp.where(causal, p, 0.0)
                dv_ref[0, pl.ds(ks, bq), :] += _mm_tn(p, do)
                ds = p * (_mm_nt(do, v) - delta[pl.ds(qs, bq), :])
                dq_ref[0, pl.ds(qs, bq), :] += _mm(ds, k)
                dk_ref[0, pl.ds(ks, bq), :] += _mm_tn(ds, q)

            q_block(ks, True)

            def q_step(qi, c):
                q_block(pl.multiple_of(qi * bq, bq), False)
                return c

            lax.fori_loop(kj + 1, nq, q_step, 0)
            return carry

        lax.fori_loop(0, nq, k_step, 0)

    spec = lambda d: pl.BlockSpec((1, S, d), lambda h, b: (h, b, 0))
    return pl.pallas_call(
        body, grid=(H, B), name="attn_bwd",
        in_specs=[spec(QK_DIM), spec(QK_DIM), spec(V_DIM), spec(V_DIM), spec(V_DIM), spec(1)],
        out_specs=[spec(QK_DIM), spec(QK_DIM), spec(V_DIM)],
        out_shape=[SDS((H, B * S, QK_DIM), F32), SDS((H, B * S, QK_DIM), F32), SDS((H, B * S, V_DIM), F32)],
        scratch_shapes=[pltpu.VMEM((S, 1), F32)],
        compiler_params=_params(("arbitrary", "arbitrary")),
    )(q4, k4, v4, do4, o4, lse4)


def _gdn_bwd(qg, kg, vg, gates, states, ainv, u4, w4, do4, B, S):
    H, D, C = GDN_HEADS, GDN_DIM, CHUNK
    NC = S // C
    U = GDN_UNROLL if NC % GDN_UNROLL == 0 else 1

    def body(q_ref, k_ref, v_ref, g_ref, st_ref, ai_ref, u_ref, w_ref, do_ref, dq_ref, dk_ref, dv_ref, dgb_ref,
             kd_s, x1_s, x2_s, el_s, dvn_s, ds_s, w2t_s):
        h = pl.program_id(0)
        lane = lax.broadcasted_iota(jnp.int32, (C, LANES), 1)
        ri = lax.broadcasted_iota(jnp.int32, (C, C), 0)
        ci = lax.broadcasted_iota(jnp.int32, (C, C), 1)
        rcol = lax.broadcasted_iota(jnp.int32, (C, 1), 0)

        def rsum(a):
            return jnp.sum(a, axis=-1, keepdims=True)

        def blocks(fn):
            def group(gi, c):
                _lockstep([fn(gi * U + j) for j in range(U)])
                return c
            lax.fori_loop(0, NC // U, group, 0)

        def prepare(n):
            cs = pl.multiple_of(n * C, C)
            q = q_ref[0, pl.ds(cs, C), :]
            k = k_ref[0, pl.ds(cs, C), :]
            do = do_ref[0, pl.ds(cs, C), :]
            Gc, bt, Gam, e, f, eL = _chunk_decays(g_ref[pl.ds(cs, C), :], lane, h, ri, ci, rcol)
            At = _mm_nt(q, k) * Gam
            yield
            x1 = _mm_tn(At, do)
            x2 = _mm_tn(q * e, do)
            kd = k * f
            w = w_ref[0, pl.ds(cs, C), :]
            yield
            x1_s[pl.ds(cs, C), :] = x1
            x2_s[n] = x2 - _mm_tn(w, x1)
            w2t_s[n] = _mm_tn(w, kd)
            kd_s[pl.ds(cs, C), :] = kd
            el_s[n] = jnp.broadcast_to(eL, (SUBLANES, LANES))

        blocks(prepare)

        def recur(t, dS):
            n = NC - 1 - t
            cs = pl.multiple_of(n * C, C)
            ds_s[n] = dS
            dvn_s[pl.ds(cs, C), :] = x1_s[pl.ds(cs, C), :] + _mm(kd_s[pl.ds(cs, C), :], dS)
            return x2_s[n] + el_s[n, 0:1, :] * dS - _mm(w2t_s[n], dS)

        lax.fori_loop(0, NC, recur, jnp.zeros((D, D), F32))

        def local(n):
            cs = pl.multiple_of(n * C, C)
            q = q_ref[0, pl.ds(cs, C), :]
            k = k_ref[0, pl.ds(cs, C), :]
            v = v_ref[0, pl.ds(cs, C), :]
            do = do_ref[0, pl.ds(cs, C), :]
            u = u_ref[0, pl.ds(cs, C), :]
            w = w_ref[0, pl.ds(cs, C), :]
            dvn = dvn_s[pl.ds(cs, C), :]
            dS = ds_s[n]
            Gc, bt, Gam, e, f, eL = _chunk_decays(g_ref[pl.ds(cs, C), :], lane, h, ri, ci, rcol)
            S0 = st_ref[0, n]
            Ainv = ai_ref[0, n]
            qk = _mm_nt(jnp.concatenate([q, k], axis=0), k)
            QK, KK = qk[:C], qk[C:]
            be = bt * e
            sol = jnp.concatenate([u, w], axis=-1)
            vn = u - _mm(w, S0)
            yield
            dAt = jnp.where(ri >= ci, _mm_nt(do, vn), 0.0)
            dqd = _mm_nt(do, S0)
            dw = -_mm_nt(dvn, S0)
            dkd = _mm_nt(vn, dS)
            deL = jnp.sum(rsum(dS * S0), axis=0, keepdims=True)
            yield
            dR = _mm_exact(Ainv.T, jnp.concatenate([dvn, dw], axis=-1))
            dR1, dR2 = dR[:, :D], dR[:, D:]
            yield
            dL = jnp.where(ri > ci, -_mm_nt(dR, sol), 0.0)
            yield
            dv_ref[0, pl.ds(cs, C), :] = dR1 * bt
            r2 = rsum(dR2 * k)
            X = dL * Gam
            dbt = rsum(dR1 * v) + r2 * e + rsum(X * KK)
            de = r2 * bt + rsum(dqd * q)
            dKK = X * bt
            dQK = dAt * Gam
            dq_ref[0, pl.ds(cs, C), :] = _mm(dQK, k) + dqd * e
            dk_ref[0, pl.ds(cs, C), :] = dR2 * be + _mm(dKK + dKK.T, k) + _mm_tn(dQK, q) + dkd * f
            df = rsum(dkd * k)
            Z = (dL * (bt * KK) + dAt * QK) * Gam
            dG = rsum(Z) - rsum(Z.T) + de * e - df * f
            dGl = jnp.sum(df * f, axis=0, keepdims=True) + deL * eL
            dG = dG + jnp.where(rcol == C - 1, dGl, 0.0)
            dgb_ref[0, pl.ds(cs, C), :] = jnp.where(lane == 0, dG, jnp.where(lane == 1, dbt, 0.0))

        blocks(local)

    spec = pl.BlockSpec((1, S, D), lambda h, b: (h, b, 0))
    return pl.pallas_call(
        body, grid=(H, B), name="gdn_bwd",
        in_specs=[spec, spec, spec, pl.BlockSpec((S, LANES), lambda h, b: (b, 0)),
                  pl.BlockSpec((1, NC, D, D), lambda h, b: (h, b, 0, 0)),
                  pl.BlockSpec((1, NC, C, C), lambda h, b: (h, b, 0, 0)), spec, spec, spec],
        out_specs=[spec, spec, spec, spec],
        out_shape=[SDS((H, B * S, D), F32)] * 4,
        scratch_shapes=[pltpu.VMEM((S, D), F32), pltpu.VMEM((S, D), F32), pltpu.VMEM((NC, D, D), F32),
                        pltpu.VMEM((NC, SUBLANES, LANES), F32), pltpu.VMEM((S, D), F32),
                        pltpu.VMEM((NC, D, D), F32), pltpu.VMEM((NC, D, D), F32)],
        compiler_params=_params(("arbitrary", "arbitrary")),
    )(qg, kg, vg, gates, states, ainv, u4, w4, do4)


def _gdn_pre_bwd(proj, conv_w, alog_l, dt_l, dq4, dk4, dv4, dgb4, S):
    T = proj.shape[0]
    tm = min(256, T)
    tiles_per_seq = S // tm
    C3 = 3 * GDN_WIDTH
    H = GDN_HEADS

    def body(u_ref, halo_ref, gab_ref, w_ref, alog_ref, dt_ref, dq_ref, dk_ref, dv_ref, dgb_ref,
             dc_ref, dgab_ref, dcw_ref, dalog_ref, ddt_ref):
        i = pl.program_id(0)

        @pl.when(i == 0)
        def _():
            dcw_ref[...] = jnp.zeros_like(dcw_ref)
            dalog_ref[...] = jnp.zeros_like(dalog_ref)
            ddt_ref[...] = jnp.zeros_like(ddt_ref)

        halo = jnp.where(i % tiles_per_seq == 0, 0.0, halo_ref[...])
        c, sh = _conv_taps(u_ref[...], halo, w_ref[...])
        sg = _sigmoid(c)
        a = c * sg
        das = [None] * (3 * H)
        for h in range(H):
            xq = a[:, h * GDN_DIM:(h + 1) * GDN_DIM]
            xk = a[:, GDN_WIDTH + h * GDN_DIM:GDN_WIDTH + (h + 1) * GDN_DIM]
            das[h] = _l2n_bwd(dq_ref[h], xq, GDN_QSCALE)
            das[H + h] = _l2n_bwd(dk_ref[h], xk, 1.0)
            das[2 * H + h] = dv_ref[h]
        dc = jnp.concatenate(das, axis=-1) * (sg * (1.0 + c * (1.0 - sg)))
        dc_ref[...] = dc
        dcw_ref[...] += jnp.concatenate(
            [jnp.sum(dc * sh[CONV_W - 1 - t], axis=0, keepdims=True) for t in range(CONV_W)], axis=0)
        lane = lax.broadcasted_iota(jnp.int32, (tm, LANES), 1)
        ric = lax.broadcasted_iota(jnp.int32, (tm, LANES), 0) % CHUNK
        dG = jnp.zeros((tm, LANES), F32)
        for h in range(H):
            t = dgb_ref[h]
            dG = dG + jnp.where(lane == h, _pick_lane(t, lane, 0), 0.0) \
                    + jnp.where(lane == h + H, _pick_lane(t, lane, 1), 0.0)
        is_g = lane < H
        dg = jnp.where(is_g, _chunk_rev_cumsum(jnp.where(is_g, dG, 0.0), ric), 0.0)
        gab = gab_ref[...]
        g, beta = _gate_values(gab, alog_ref[...], dt_ref[...], lane)
        dga = jnp.where(is_g, dg * (-jnp.exp(alog_ref[...])) * _sigmoid(gab + dt_ref[...]), 0.0)
        dgb = jnp.where(is_g, 0.0, dG) * beta * (1.0 - beta)
        dgab_ref[...] = dga + dgb
        dalog_ref[...] += jnp.sum(dg * g, axis=0, keepdims=True)
        ddt_ref[...] += jnp.sum(dga, axis=0, keepdims=True)

    hspec = pl.BlockSpec((H, tm, GDN_DIM), lambda i: (0, i, 0))
    vec = pl.BlockSpec((1, LANES), lambda i: (0, 0))
    return pl.pallas_call(
        body, grid=(T // tm,), name="gdn_pre_bwd",
        in_specs=[pl.BlockSpec((tm, C3), lambda i: (i, 0)),
                  pl.BlockSpec((SUBLANES, C3), lambda i: (jnp.maximum(i * (tm // SUBLANES) - 1, 0), 0)),
                  pl.BlockSpec((tm, LANES), lambda i: (i, P_GAB // LANES)),
                  pl.BlockSpec((CONV_W, C3), lambda i: (0, 0)), vec, vec, hspec, hspec, hspec, hspec],
        out_specs=[pl.BlockSpec((tm, C3), lambda i: (i, 0)), pl.BlockSpec((tm, LANES), lambda i: (i, 0)),
                   pl.BlockSpec((CONV_W, C3), lambda i: (0, 0)), vec, vec],
        out_shape=[SDS((T, C3), F32), SDS((T, LANES), F32), SDS((CONV_W, C3), F32),
                   SDS((1, LANES), F32), SDS((1, LANES), F32)],
        compiler_params=_params(("arbitrary",)),
    )(proj, proj, proj, conv_w, alog_l, dt_l, dq4, dk4, dv4, dgb4)


def _conv_bwd_input(dc, conv_w, S):
    T, C3 = dc.shape
    tm = min(256, T)
    tiles_per_seq = S // tm
    nblk = T // SUBLANES

    def body(dc_ref, nxt_ref, w_ref, du_ref):
        i = pl.program_id(0)
        nxt = jnp.where(i % tiles_per_seq == tiles_per_seq - 1, 0.0, nxt_ref[...])
        x = dc_ref[...]
        w = w_ref[...]
        du = w[3:4] * x
        for j in range(1, CONV_W):
            du = du + w[3 - j:4 - j] * _shift_up(x, nxt, j)
        du_ref[...] = du

    return pl.pallas_call(
        body, grid=(T // tm,), name="conv_bwd_input",
        in_specs=[pl.BlockSpec((tm, C3), lambda i: (i, 0)),
                  pl.BlockSpec((SUBLANES, C3), lambda i: (jnp.minimum((i + 1) * (tm // SUBLANES), nblk - 1), 0)),
                  pl.BlockSpec((CONV_W, C3), lambda i: (0, 0))],
        out_specs=pl.BlockSpec((tm, C3), lambda i: (i, 0)),
        out_shape=SDS((T, C3), F32),
        compiler_params=_params(("arbitrary",)),
    )(dc, dc, conv_w)


def _mla_pre_bwd(proj, cosf, sinf, w_qln, w_kvln, w_uq_p, w_ukv, qnw, knw, dq4, dk4, dv4):
    T = proj.shape[0]
    tm = min(256, T)
    H = MLA_HEADS

    def body(ql_ref, kvl_ref, kpe_ref, cos_ref, sin_ref, wq_ref, wkv_ref, uq_ref, ukv_ref, qnw_ref, knw_ref,
             dq_ref, dk_ref, dv_ref,
             dql_ref, dkvl_ref, dkpe_ref, dqraw_ref, dkvraw_ref, qn_ref, kvn_ref, dwq_ref, dwkv_ref, dqnw_ref, dknw_ref):
        @pl.when(pl.program_id(0) == 0)
        def _():
            for r in (dwq_ref, dwkv_ref, dqnw_ref, dknw_ref):
                r[...] = jnp.zeros_like(r)

        cos, sin = cos_ref[...], sin_ref[...]
        qnw_, knw_ = qnw_ref[...], knw_ref[...]
        ql, kvl = ql_ref[...], kvl_ref[...]
        kpe_raw = kpe_ref[...][:, :ROPE]
        qn, rq = _rms(ql, wq_ref[...])
        kvn, rkv = _rms(kvl, wkv_ref[...])
        qn_ref[...] = qn.astype(MXU_DTYPE)
        kvn_ref[...] = kvn.astype(MXU_DTYPE)
        qraw = _mm(qn, uq_ref[...])
        kvraw = _mm(kvn, ukv_ref[...])
        dq_nope, dq_pe, dkv_parts = [], [], []
        dqnw_n = jnp.zeros((1, NOPE), F32)
        dqnw_p = jnp.zeros((1, ROPE), F32)
        dknw_n = jnp.zeros((1, NOPE), F32)
        dkpe = jnp.zeros((tm, ROPE), F32)
        for h in range(H):
            dq = dq_ref[h] * ATT_SCALE
            x = qraw[:, h * NOPE:(h + 1) * NOPE]
            dx, dw = _rms_bwd(dq[:, :NOPE], x, qnw_[:, :NOPE], _rms(x, qnw_[:, :NOPE])[1])
            dq_nope.append(dx)
            dqnw_n = dqnw_n + dw
            x = qraw[:, H * NOPE + h * ROPE:H * NOPE + (h + 1) * ROPE]
            dx, dw = _rms_bwd(_rope_bwd(dq[:, NOPE:], cos, sin), x, qnw_[:, NOPE:], _rms(x, qnw_[:, NOPE:])[1])
            dq_pe.append(dx)
            dqnw_p = dqnw_p + dw
            dk = dk_ref[h]
            x = kvraw[:, h * 256:h * 256 + NOPE]
            dx, dw = _rms_bwd(dk[:, :NOPE], x, knw_[:, :NOPE], _rms(x, knw_[:, :NOPE])[1])
            dknw_n = dknw_n + dw
            dkpe = dkpe + dk[:, NOPE:]
            dkv_parts += [dx, dv_ref[h]]
        dx, dknw_p = _rms_bwd(_rope_bwd(dkpe, cos, sin), kpe_raw, knw_[:, NOPE:], _rms(kpe_raw, knw_[:, NOPE:])[1])
        dkpe_ref[...] = jnp.concatenate([dx, jnp.zeros((tm, LANES - ROPE), F32)], axis=-1)
        dqraw = jnp.concatenate(dq_nope + dq_pe, axis=-1).astype(MXU_DTYPE)
        dkvraw = jnp.concatenate(dkv_parts, axis=-1).astype(MXU_DTYPE)
        dqraw_ref[...] = dqraw
        dkvraw_ref[...] = dkvraw
        dx, dw = _rms_bwd(_mm_nt(dqraw, uq_ref[...]), ql, wq_ref[...], rq)
        dql_ref[...] = dx
        dwq_ref[...] += dw
        dx, dw = _rms_bwd(_mm_nt(dkvraw, ukv_ref[...]), kvl, wkv_ref[...], rkv)
        dkvl_ref[...] = dx
        dwkv_ref[...] += dw
        dqnw_ref[...] += jnp.concatenate([dqnw_n, dqnw_p], axis=-1)
        dknw_ref[...] += jnp.concatenate([dknw_n, dknw_p], axis=-1)

    full = lambda a: pl.BlockSpec(a.shape, lambda i: (0,) * a.ndim)
    rows = lambda n: pl.BlockSpec((tm, n), lambda i: (i, 0))
    const = lambda n: pl.BlockSpec((1, n), lambda i: (0, 0))
    NQ, NKV = w_uq_p.shape[1], w_ukv.shape[1]
    return pl.pallas_call(
        body, grid=(T // tm,), name="mla_pre_bwd",
        in_specs=[pl.BlockSpec((tm, 256), lambda i: (i, P_QLAT // 256)),
                  pl.BlockSpec((tm, 256), lambda i: (i, P_KVLAT // 256)),
                  pl.BlockSpec((tm, 128), lambda i: (i, P_KPE // 128)),
                  rows(ROPE), rows(ROPE),
                  full(w_qln), full(w_kvln), full(w_uq_p), full(w_ukv), full(qnw), full(knw),
                  pl.BlockSpec((H, tm, QK_DIM), lambda i: (0, i, 0)),
                  pl.BlockSpec((H, tm, QK_DIM), lambda i: (0, i, 0)),
                  pl.BlockSpec((H, tm, V_DIM), lambda i: (0, i, 0))],
        out_specs=[rows(Q_LORA), rows(KV_LORA), rows(LANES), rows(NQ), rows(NKV), rows(Q_LORA), rows(KV_LORA),
                   const(Q_LORA), const(KV_LORA), const(QK_DIM), const(QK_DIM)],
        out_shape=[SDS((T, Q_LORA), F32), SDS((T, KV_LORA), F32), SDS((T, LANES), F32),
                   SDS((T, NQ), MXU_DTYPE), SDS((T, NKV), MXU_DTYPE),
                   SDS((T, Q_LORA), MXU_DTYPE), SDS((T, KV_LORA), MXU_DTYPE),
                   SDS((1, Q_LORA), F32), SDS((1, KV_LORA), F32), SDS((1, QK_DIM), F32), SDS((1, QK_DIM), F32)],
        compiler_params=_params(("arbitrary",)),
    )(proj, proj, proj, cosf, sinf, w_qln, w_kvln, w_uq_p, w_ukv, qnw, knw, dq4, dk4, dv4)


def _in_proj_bwd(dgqkv, dgz, dql, dkvl, dkpe, dgab, w_in_p, dh, x2, w_an):
    T, D = x2.shape
    N = w_in_p.shape[1]
    tm = min(512, T)

    def body(a_ref, b_ref, c_ref, d_ref, e_ref, f_ref, w_ref, dh_ref, x_ref, wn_ref, dx_ref, dp_ref, dwn_ref):
        @pl.when(pl.program_id(0) == 0)
        def _():
            dwn_ref[...] = jnp.zeros_like(dwn_ref)

        dp = jnp.concatenate([a_ref[...], b_ref[...], c_ref[...], d_ref[...], e_ref[...], f_ref[...]],
                             axis=-1).astype(MXU_DTYPE)
        dp_ref[...] = dp
        x = x_ref[...]
        _, r = _rms(x, wn_ref[...])
        dx, dw = _rms_bwd(_mm_nt(dp, w_ref[...]), x, wn_ref[...], r)
        dx_ref[...] = dh_ref[...] + dx
        dwn_ref[...] += dw

    rows = lambda n: pl.BlockSpec((tm, n), lambda i: (i, 0))
    return pl.pallas_call(
        body, grid=(T // tm,), name="in_proj_bwd",
        in_specs=[rows(dgqkv.shape[1]), rows(dgz.shape[1]), rows(dql.shape[1]), rows(dkvl.shape[1]),
                  rows(dkpe.shape[1]), rows(dgab.shape[1]),
                  pl.BlockSpec((D, N), lambda i: (0, 0)), rows(D), rows(D), pl.BlockSpec((1, D), lambda i: (0, 0))],
        out_specs=[rows(D), rows(N), pl.BlockSpec((1, D), lambda i: (0, 0))],
        out_shape=[SDS((T, D), F32), SDS((T, N), MXU_DTYPE), SDS((1, D), F32)],
        compiler_params=_params(("arbitrary",)),
    )(dgqkv, dgz, dql, dkvl, dkpe, dgab, w_in_p, dh, x2, w_an)


def _wgrad(a, b, name, column_shards=False):
    T, M = a.shape
    N = b.shape[1]
    tM = _divisor_tile(M, 512)
    tN = N // N_DEV if column_shards else _divisor_tile(N, 1536)
    tk = min(T, 1024)
    nk = T // tk

    def body(a_ref, b_ref, o_ref, acc):
        k = pl.program_id(2)

        @pl.when(k == 0)
        def _():
            acc[...] = jnp.zeros_like(acc)

        acc[...] += _mm_tn(a_ref[...], b_ref[...])

        @pl.when(k == nk - 1)
        def _():
            o_ref[...] = acc[...].astype(WIRE_DTYPE).reshape(o_ref.shape)

    if column_shards:
        out_spec, out_shape = pl.BlockSpec((1, tM, tN), lambda i, j, k: (j, i, 0)), SDS((N_DEV, M, tN), WIRE_DTYPE)
    else:
        out_spec, out_shape = pl.BlockSpec((tM, tN), lambda i, j, k: (i, j)), SDS((M, N), WIRE_DTYPE)
    return pl.pallas_call(
        body, grid=(M // tM, N // tN, nk), name=name,
        in_specs=[pl.BlockSpec((tk, tM), lambda i, j, k: (k, i)), pl.BlockSpec((tk, tN), lambda i, j, k: (k, j))],
        out_specs=out_spec, out_shape=out_shape,
        scratch_shapes=[pltpu.VMEM((tM, tN), F32)],
        compiler_params=_params(("arbitrary", "arbitrary", "arbitrary")),
    )(a, b)


def _adamw(g, w, m, v):
    m = ADAM_B1 * m + (1.0 - ADAM_B1) * g
    v = ADAM_B2 * v + (1.0 - ADAM_B2) * jnp.square(g)
    m_hat = m / (1.0 - ADAM_B1 ** ADAM_STEP)
    v_hat = v / (1.0 - ADAM_B2 ** ADAM_STEP)
    return -ADAM_LR * (m_hat / (jnp.sqrt(v_hat) + ADAM_EPS) + ADAM_WD * w), m, v


def _reduce_adamw(parts, w, m, v, name):
    R, C = w.shape
    _, Rp, Cp = parts.shape
    tr = min(R, 256)
    tp = tr if Rp == R else Rp

    def body(p_ref, w_ref, m_ref, v_ref, g_ref, d_ref, nm_ref, nv_ref):
        g = p_ref[0].astype(F32)
        for s in range(1, N_DEV):
            g = g + p_ref[s].astype(F32)
        g = g[:tr, :C]
        g_ref[...] = g
        d_ref[...], nm_ref[...], nv_ref[...] = _adamw(g, w_ref[...], m_ref[...], v_ref[...])

    spec = pl.BlockSpec((tr, C), lambda i: (i, 0))
    return pl.pallas_call(
        body, grid=(R // tr,), name=name,
        in_specs=[pl.BlockSpec((N_DEV, tp, Cp), lambda i: (0, i, 0)), spec, spec, spec],
        out_specs=[spec] * 4, out_shape=[SDS((R, C), F32)] * 4,
        compiler_params=_params(("arbitrary",)),
    )(parts, w, m, v)


SMALL_ROWS, SMALL_COLS = 16, 1024
SMALL_LAYOUT = (
    ("attn_norm_w", 0, 1, 1024, 1024), ("mlp_norm_w", 1, 1, 1024, 1024), ("q_lat_norm_w", 2, 1, 256, 256),
    ("kv_lat_norm_w", 3, 1, 256, 256), ("q_norm_w", 4, 1, 192, 192), ("k_norm_w", 5, 1, 192, 192),
    ("mla_out_norm_w", 6, 4, 128, 128), ("a_log", 10, 1, 128, 4), ("dt_bias", 11, 1, 128, 4),
    ("gdn_norm_w", 12, 1, 128, 128))


def _adamw_replicated(parts, ws, ms, vs):
    n = len(SMALL_LAYOUT)

    def body(*refs):
        p_ref = refs[0]
        w_refs, m_refs, v_refs = refs[1:1 + n], refs[1 + n:1 + 2 * n], refs[1 + 2 * n:1 + 3 * n]
        outs = refs[1 + 3 * n:]
        s = p_ref[0]
        for d in range(1, N_DEV):
            s = s + p_ref[d]
        for i, (_, r0, nr, _, pw) in enumerate(SMALL_LAYOUT):
            g = s[r0:r0 + nr, :pw]
            outs[i][...] = g
            outs[n + i][...], outs[2 * n + i][...], outs[3 * n + i][...] = _adamw(
                g, w_refs[i][...], m_refs[i][...], v_refs[i][...])

    res = pl.pallas_call(
        body, name="adamw_replicated",
        out_shape=[SDS(w.shape, F32) for w in ws] * 4,
        compiler_params=_params(),
    )(parts, *ws, *ms, *vs)
    return [res[k * n:(k + 1) * n] for k in range(4)]


COPIES_PER_ARRAY = N_DEV - 1


def _two_level_gather(srcs, outs, send_sems, recv_sems):
    mx, my, mc = lax.axis_index("x"), lax.axis_index("y"), lax.axis_index("c")
    me, sibling = (mx, my, mc), (mx, my, 1 - mc)
    chips = [(1 - mx, my), (mx, 1 - my), (1 - mx, 1 - my)]
    arrays = range(len(srcs))

    def copy(a, k, block, to, src=None):
        px, py, pc = block
        slot = outs[a].at[4 * px + 2 * py + pc]
        sem = a * COPIES_PER_ARRAY + k
        return pltpu.make_async_remote_copy(
            src_ref=slot if src is None else src, dst_ref=slot,
            send_sem=send_sems.at[sem], recv_sem=recv_sems.at[sem], device_id=to, device_id_type=MESH_ID)

    started = []
    for a in arrays:
        started.append(copy(a, 0, me, sibling, src=srcs[a]))
        started += [copy(a, 1 + j, me, (*chip, mc), src=srcs[a]) for j, chip in enumerate(chips)]
    for cp in started:
        cp.start()
    for j, chip in enumerate(chips):
        for a in arrays:
            copy(a, 1 + j, (*chip, mc), me).wait_recv()
            fwd = copy(a, 4 + j, (*chip, mc), sibling)
            fwd.start()
            started.append(fwd)
    for a in arrays:
        copy(a, 0, sibling, me).wait_recv()
    for j, chip in enumerate(chips):
        for a in arrays:
            copy(a, 4 + j, (*chip, 1 - mc), me).wait_recv()
    for cp in started:
        cp.wait_send()


def _gather_weights(shards):
    n = len(shards)

    def body(*refs):
        srcs, outs = refs[:n], refs[n:2 * n]
        send_sems, recv_sems, local_sems = refs[2 * n:]
        me = 4 * lax.axis_index("x") + 2 * lax.axis_index("y") + lax.axis_index("c")
        mine = [pltpu.make_async_copy(srcs[a], outs[a].at[me], local_sems.at[a]) for a in range(n)]
        for cp in mine:
            cp.start()
        _two_level_gather(srcs, outs, send_sems, recv_sems)
        for cp in mine:
            cp.wait()

    return pl.pallas_call(
        body, name="gather_weights",
        out_shape=[SDS((N_DEV,) + s.shape, s.dtype) for s in shards],
        in_specs=[pl.BlockSpec(memory_space=pl.ANY)] * n,
        out_specs=[pl.BlockSpec(memory_space=pl.ANY)] * n,
        scratch_shapes=[pltpu.SemaphoreType.DMA((n * COPIES_PER_ARRAY,)),
                        pltpu.SemaphoreType.DMA((n * COPIES_PER_ARRAY,)), pltpu.SemaphoreType.DMA((n,))],
    )(*shards)


def _gather_small_grads(gs):
    n = len(gs)

    def body(*refs):
        g_refs, out_ref = refs[:n], refs[n]
        tile, send_sems, recv_sems = refs[n + 1:]
        tile[...] = jnp.zeros_like(tile)
        for (_, r0, nr, gw, _), g in zip(SMALL_LAYOUT, g_refs):
            tile[r0:r0 + nr, 0:gw] = g[...]
        me = 4 * lax.axis_index("x") + 2 * lax.axis_index("y") + lax.axis_index("c")
        out_ref[me] = tile[...]
        _two_level_gather([tile], [out_ref], send_sems, recv_sems)

    return pl.pallas_call(
        body, name="gather_small_grads",
        out_shape=SDS((N_DEV, SMALL_ROWS, SMALL_COLS), F32),
        in_specs=[pl.BlockSpec(memory_space=pltpu.VMEM)] * n,
        out_specs=pl.BlockSpec(memory_space=pltpu.VMEM),
        scratch_shapes=[pltpu.VMEM((SMALL_ROWS, SMALL_COLS), F32),
                        pltpu.SemaphoreType.DMA((COPIES_PER_ARRAY,)), pltpu.SemaphoreType.DMA((COPIES_PER_ARRAY,))],
    )(*gs)


def _exchange_grads(slabs):
    n = len(slabs)
    flips = [(0, 0, 1), (1, 0, 0), (0, 1, 0), (1, 1, 0), (1, 0, 1), (0, 1, 1), (1, 1, 1)]

    def body(*refs):
        srcs, outs = refs[:n], refs[n:2 * n]
        send_sems, recv_sems, local_sems = refs[2 * n:]
        mx, my, mc = lax.axis_index("x"), lax.axis_index("y"), lax.axis_index("c")
        mine = [pltpu.make_async_copy(srcs[a].at[4 * mx + 2 * my + mc], outs[a].at[N_DEV - 1], local_sems.at[a])
                for a in range(n)]
        for cp in mine:
            cp.start()
        copies = []
        for k, (fx, fy, fc) in enumerate(flips):
            px = 1 - mx if fx else mx
            py = 1 - my if fy else my
            pc = 1 - mc if fc else mc
            for a in range(n):
                sem = a * COPIES_PER_ARRAY + k
                copies.append(pltpu.make_async_remote_copy(
                    src_ref=srcs[a].at[4 * px + 2 * py + pc], dst_ref=outs[a].at[k],
                    send_sem=send_sems.at[sem], recv_sem=recv_sems.at[sem],
                    device_id=(px, py, pc), device_id_type=MESH_ID))
        for cp in copies:
            cp.start()
        for cp in copies:
            cp.wait()
        for cp in mine:
            cp.wait()

    return pl.pallas_call(
        body, name="exchange_grads",
        out_shape=[SDS(s.shape, s.dtype) for s in slabs],
        in_specs=[pl.BlockSpec(memory_space=pl.ANY)] * n,
        out_specs=[pl.BlockSpec(memory_space=pl.ANY)] * n,
        scratch_shapes=[pltpu.SemaphoreType.DMA((n * COPIES_PER_ARRAY,)),
                        pltpu.SemaphoreType.DMA((n * COPIES_PER_ARRAY,)), pltpu.SemaphoreType.DMA((n,))],
    )(*slabs)


def _w_in_to_padded(w):
    z = lambda n: jnp.zeros((w.shape[0], n), w.dtype)
    return jnp.concatenate([w[:, O_GQKV:O_GZ], w[:, O_GZ:O_GAB], w[:, O_QLAT:O_KVLAT], w[:, O_KVLAT:O_KPE],
                            w[:, O_KPE:O_GQKV], z(P_GAB - P_KPE - ROPE), w[:, O_GAB:O_END],
                            z(P_WIDTH - P_GAB - (O_END - O_GAB))], axis=1)


def _w_in_from_padded(wp):
    return jnp.concatenate([wp[:, P_QLAT:P_QLAT + 256], wp[:, P_KVLAT:P_KVLAT + 256], wp[:, P_KPE:P_KPE + ROPE],
                            wp[:, P_GQKV:P_GZ], wp[:, P_GZ:P_QLAT], wp[:, P_GAB:P_GAB + (O_END - O_GAB)]], axis=1)


def _w_uq_to_headsplit(w):
    w3 = w.reshape(w.shape[0], MLA_HEADS, QK_DIM)
    return jnp.concatenate([w3[:, :, :NOPE].reshape(w.shape[0], -1), w3[:, :, NOPE:].reshape(w.shape[0], -1)], axis=1)


def _w_uq_from_headsplit(wp):
    n = wp[:, :MLA_HEADS * NOPE].reshape(wp.shape[0], MLA_HEADS, NOPE)
    p = wp[:, MLA_HEADS * NOPE:].reshape(wp.shape[0], MLA_HEADS, ROPE)
    return jnp.concatenate([n, p], axis=2).reshape(wp.shape[0], -1)


def _lane_vec(v4):
    return jnp.pad(v4.reshape(1, -1), ((0, 0), (0, LANES - v4.shape[-1])))


def _local_step(x, positions, target, attn_norm_w, w_in, q_lat_norm_w, w_uq, kv_lat_norm_w, w_ukv, q_norm_w,
                k_norm_w, mla_out_norm_w, conv_w, a_log, dt_bias, gdn_norm_w, w_out, mlp_norm_w, w_up, w_down):
    B, S, D = x.shape
    T = B * S
    x2 = x.reshape(T, D)
    t2 = target.reshape(T, D)
    half = ROPE // 2
    inv_freq = ROPE_THETA ** (-jnp.arange(half, dtype=F32) / half)
    ang = positions.reshape(T, 1).astype(F32) * inv_freq
    cosf = jnp.concatenate([jnp.cos(ang)] * 2, axis=-1)
    sinf = jnp.concatenate([jnp.sin(ang)] * 2, axis=-1)
    w_in_p = _w_in_to_padded(w_in)
    w_uq_p = _w_uq_to_headsplit(w_uq)
    alog_l, dt_l = _lane_vec(a_log), _lane_vec(dt_bias)
    w_an, w_qln, w_kvln, qnw, knw, w_mn, gdn_w = (
        attn_norm_w, q_lat_norm_w, kv_lat_norm_w, q_norm_w, k_norm_w, mlp_norm_w, gdn_norm_w)

    proj, xn = _in_proj(x2, w_an, w_in_p)
    q4, k4, v4 = _mla_pre(proj, cosf, sinf, w_qln, w_kvln, w_uq_p, w_ukv, qnw, knw)
    o_mla, lse = _attn_fwd(q4, k4, v4, B, S)
    qg, kg, vg, gates = _gdn_pre(proj, conv_w, alog_l, dt_l, S)
    o_gdn, states, ainv, u4, w4 = _gdn_fwd(qg, kg, vg, gates, B, S)
    h2, mix = _mix_out(o_mla, o_gdn, proj, x2, mla_out_norm_w, gdn_w, w_out)
    up, hn, dy, sq = _mlp_fwd(h2, w_mn, w_up, w_down, t2)
    loss = (0.5 / D) * jnp.sum(sq[:, 0, 0])

    dh, dhb, dup, act, dyb, d_mlp_norm = _mlp_bwd(dy, up, h2, w_mn, w_up, w_down)
    g_w_down = _wgrad(act, dyb, "wgrad_down")
    g_w_up = _wgrad(hn, dup, "wgrad_up", column_shards=True)
    do_mla, do_gdn, dz, d_mla_w, d_gdn_w = _mix_bwd(dhb, o_mla, o_gdn, proj, mla_out_norm_w, gdn_w, w_out)
    g_w_out = _wgrad(mix, dhb, "wgrad_out")
    dq4, dk4, dv4 = _attn_bwd(q4, k4, v4, do_mla, o_mla, lse, B, S)
    dql, dkvl, dkpe, dqraw, dkvraw, qn, kvn, d_wqln, d_wkvln, d_qnw, d_knw = _mla_pre_bwd(
        proj, cosf, sinf, w_qln, w_kvln, w_uq_p, w_ukv, qnw, knw, dq4, dk4, dv4)
    g_w_uq_p = _wgrad(qn, dqraw, "wgrad_uq")
    g_w_ukv = _wgrad(kvn, dkvraw, "wgrad_ukv")
    dqg, dkg, dvg, dgb4 = _gdn_bwd(qg, kg, vg, gates, states, ainv, u4, w4, do_gdn, B, S)
    dc, dgab, g_conv, d_alog, d_dt = _gdn_pre_bwd(proj, conv_w, alog_l, dt_l, dqg, dkg, dvg, dgb4, S)
    dgqkv = _conv_bwd_input(dc, conv_w, S)
    grad_x2, dproj, d_attn_norm = _in_proj_bwd(dgqkv, dz, dql, dkvl, dkpe, dgab, w_in_p, dh, x2, w_an)
    g_w_in_p = _wgrad(xn, dproj, "wgrad_in")

    mats = dict(w_in=g_w_in_p, w_uq=g_w_uq_p, w_ukv=g_w_ukv, conv_w=g_conv, w_out=g_w_out, w_up=g_w_up,
                w_down=g_w_down)
    small = dict(attn_norm_w=d_attn_norm, mlp_norm_w=d_mlp_norm, q_lat_norm_w=d_wqln, kv_lat_norm_w=d_wkvln,
                 q_norm_w=d_qnw, k_norm_w=d_knw, mla_out_norm_w=d_mla_w, a_log=d_alog, dt_bias=d_dt,
                 gdn_norm_w=d_gdn_w)
    return loss, grad_x2.reshape(B, S, D), mats, [small[n] for n, *_ in SMALL_LAYOUT]


BIG = ("w_in", "w_uq", "w_ukv", "conv_w", "w_out", "w_up", "w_down")
ALL_W = ("attn_norm_w", "w_in", "q_lat_norm_w", "w_uq", "kv_lat_norm_w", "w_ukv", "q_norm_w", "k_norm_w",
         "mla_out_norm_w", "conv_w", "a_log", "dt_bias", "gdn_norm_w", "w_out", "mlp_norm_w", "w_up", "w_down")
WIRE_SHAPE = {"w_in": (1024, 384), "w_uq": (256, 128), "conv_w": (16, 256)}


def _pad2(a, rows, cols):
    return jnp.pad(a, [(0, 0)] * (a.ndim - 2) + [(0, rows - a.shape[-2]), (0, cols - a.shape[-1])])


def _cols_to_full(stack, cols):
    return jnp.moveaxis(stack[:, :, :cols], 0, 1).reshape(stack.shape[1], N_DEV * cols)


def _full_to_cols(full, wire_cols):
    r, n = full.shape
    return _pad2(jnp.moveaxis(full.reshape(r, N_DEV, n // N_DEV), 1, 0), r, wire_cols)


def kernel(x, positions, attn_norm_w, w_in, q_lat_norm_w, w_uq, kv_lat_norm_w, w_ukv, q_norm_w, k_norm_w, mla_out_norm_w, conv_w, a_log, dt_bias, gdn_norm_w, w_out, mlp_norm_w, w_up, w_down, loss_target, m_attn_norm_w, m_w_in, m_q_lat_norm_w, m_w_uq, m_kv_lat_norm_w, m_w_ukv, m_q_norm_w, m_k_norm_w, m_mla_out_norm_w, m_conv_w, m_a_log, m_dt_bias, m_gdn_norm_w, m_w_out, m_mlp_norm_w, m_w_up, m_w_down, v_attn_norm_w, v_w_in, v_q_lat_norm_w, v_w_uq, v_kv_lat_norm_w, v_w_ukv, v_q_norm_w, v_k_norm_w, v_mla_out_norm_w, v_conv_w, v_a_log, v_dt_bias, v_gdn_norm_w, v_w_out, v_mlp_norm_w, v_w_up, v_w_down):
    env = dict(locals())
    W = {n: env[n][0] for n in ALL_W}
    Mo = {n: env["m_" + n][0] for n in ALL_W}
    Vo = {n: env["v_" + n][0] for n in ALL_W}

    two_d = lambda a: a.reshape(1, -1) if a.ndim == 1 else a
    D = x.shape[-1]

    s_in, s_uq, s_ukv, s_conv, s_out, s_up, s_down = _gather_weights([
        _pad2(W["w_in"].astype(WIRE_DTYPE), *WIRE_SHAPE["w_in"]),
        _pad2(W["w_uq"].astype(WIRE_DTYPE), *WIRE_SHAPE["w_uq"]),
        W["w_ukv"].astype(WIRE_DTYPE), _pad2(W["conv_w"], *WIRE_SHAPE["conv_w"]),
        W["w_out"].astype(WIRE_DTYPE), W["w_up"].astype(WIRE_DTYPE), W["w_down"].astype(WIRE_DTYPE)])

    loss, grad_x, gm, gs = _local_step(
        x, positions, loss_target, two_d(W["attn_norm_w"]), _cols_to_full(s_in, W["w_in"].shape[1]),
        two_d(W["q_lat_norm_w"]), _cols_to_full(s_uq, W["w_uq"].shape[1]), two_d(W["kv_lat_norm_w"]),
        _cols_to_full(s_ukv, W["w_ukv"].shape[1]), two_d(W["q_norm_w"]), two_d(W["k_norm_w"]),
        W["mla_out_norm_w"], _cols_to_full(s_conv[:, :CONV_W], W["conv_w"].shape[1]), two_d(W["a_log"]),
        two_d(W["dt_bias"]), two_d(W["gdn_norm_w"]), s_out.reshape(-1, D), two_d(W["mlp_norm_w"]), s_up,
        s_down.reshape(-1, D))
    loss = lax.psum(loss, ("x", "y", "c"))

    slabs = [
        _full_to_cols(_w_in_from_padded(gm["w_in"]), WIRE_SHAPE["w_in"][1]),
        _full_to_cols(_w_uq_from_headsplit(gm["w_uq"]), WIRE_SHAPE["w_uq"][1]),
        _full_to_cols(gm["w_ukv"], W["w_ukv"].shape[1]),
        _pad2(_full_to_cols(gm["conv_w"].astype(WIRE_DTYPE), W["conv_w"].shape[1]), *WIRE_SHAPE["conv_w"]),
        gm["w_out"].reshape(N_DEV, -1, D), gm["w_up"], gm["w_down"].reshape(N_DEV, -1, D)]
    parts = _exchange_grads(slabs)
    done = {n: _reduce_adamw(p, W[n], Mo[n], Vo[n], "adamw_" + n) for n, p in zip(BIG, parts)}
    names = [n for n, *_ in SMALL_LAYOUT]
    small = _adamw_replicated(_gather_small_grads(gs), [two_d(W[n]) for n in names], [two_d(Mo[n]) for n in names],
                              [two_d(Vo[n]) for n in names])
    for i, n in enumerate(names):
        done[n] = [small[kind][i] for kind in range(4)]
    res = [done[n][kind].reshape(env[n].shape) for kind in range(4) for n in ALL_W]
    return (loss, grad_x, *res)
```

```python
import functools

import jax
import jax.numpy as jnp
from jax import lax
from jax.experimental import pallas as pl
from jax.experimental.pallas import tpu as pltpu

F32 = jnp.float32
MXU_DTYPE = jnp.bfloat16
WIRE_DTYPE = jnp.bfloat16
SDS = jax.ShapeDtypeStruct
HIGHEST = lax.Precision.HIGHEST
MESH_ID = pl.DeviceIdType.MESH

D_MODEL = 1024
MLA_HEADS = 4
Q_LORA = 256
KV_LORA = 256
NOPE = 128
ROPE = 64
QK_DIM = NOPE + ROPE
V_DIM = 128
ROPE_THETA = 10000.0
GDN_HEADS = 4
GDN_DIM = 128
GDN_WIDTH = GDN_HEADS * GDN_DIM
CONV_W = 4
CHUNK = 64
D_FF = 4 * D_MODEL
EPS = 1e-6
ATT_SCALE = QK_DIM ** -0.5
GDN_QSCALE = GDN_DIM ** -0.5
N_DEV = 8

ADAM_LR = 0.001
ADAM_B1 = 0.9
ADAM_B2 = 0.999
ADAM_EPS = 1e-08
ADAM_WD = 0.01
ADAM_STEP = 10

LANES = 128
SUBLANES = 8
VMEM_LIMIT = 56 * 1024 * 1024

P_GQKV, P_GZ, P_QLAT, P_KVLAT, P_KPE, P_GAB = 0, 1536, 2048, 2304, 2560, 2688
P_WIDTH = 2816
O_QLAT, O_KVLAT, O_KPE, O_GQKV, O_GZ, O_GAB, O_END = 0, 256, 512, 576, 2112, 2624, 2632


def _params(sem=None, vmem=VMEM_LIMIT):
    kw = dict(vmem_limit_bytes=vmem)
    if sem is not None:
        kw["dimension_semantics"] = sem
    return pltpu.CompilerParams(**kw)


def _mm(a, b):
    return jnp.dot(a.astype(MXU_DTYPE), b.astype(MXU_DTYPE), preferred_element_type=F32)


def _mm_nt(a, b):
    return lax.dot_general(a.astype(MXU_DTYPE), b.astype(MXU_DTYPE), (((1,), (1,)), ((), ())),
                           preferred_element_type=F32)


def _mm_tn(a, b):
    return lax.dot_general(a.astype(MXU_DTYPE), b.astype(MXU_DTYPE), (((0,), (0,)), ((), ())),
                           preferred_element_type=F32)


def _split(a):
    hi = a.astype(MXU_DTYPE)
    return hi, (a - hi.astype(F32)).astype(MXU_DTYPE)


def _mm_split(a, b):
    (ah, al), (bh, bl) = a, b
    dot = lambda x, y: jnp.dot(x, y, preferred_element_type=F32)
    if MXU_DTYPE == F32:
        return dot(ah, bh)
    return dot(ah, bh) + dot(ah, bl) + dot(al, bh)


def _mm_exact(a, b):
    return _mm_split(_split(a), _split(b))


def _rms(x, w):
    r = lax.rsqrt(jnp.mean(x * x, axis=-1, keepdims=True) + EPS)
    return x * r * w, r


def _rms_bwd(dy, x, w, r):
    xh = x * r
    dyw = dy * w
    dx = r * (dyw - xh * jnp.mean(dyw * xh, axis=-1, keepdims=True))
    dw = jnp.sum(dy * xh, axis=0, keepdims=True)
    return dx, dw


def _l2n_bwd(dy, x, scale):
    r = lax.rsqrt(jnp.sum(x * x, axis=-1, keepdims=True) + EPS)
    xh = x * r
    return (scale * r) * (dy - xh * jnp.sum(dy * xh, axis=-1, keepdims=True))


def _rot(t):
    return jnp.concatenate([-t[:, ROPE // 2:], t[:, :ROPE // 2]], axis=-1)


def _rot_t(t):
    return jnp.concatenate([t[:, ROPE // 2:], -t[:, :ROPE // 2]], axis=-1)


def _rope(t, cos, sin):
    return t * cos + _rot(t) * sin


def _rope_bwd(d, cos, sin):
    return d * cos + _rot_t(d * sin)


def _sigmoid(x):
    return jax.nn.sigmoid(x)


def _shift_down(x, halo, j):
    if j == 0:
        return x
    xr = pltpu.roll(x, j, 0)
    hr = pltpu.roll(halo, j, 0)
    row = lax.broadcasted_iota(jnp.int32, halo.shape, 0)
    top = jnp.where(row < j, hr, xr[:SUBLANES])
    return jnp.concatenate([top, xr[SUBLANES:]], axis=0)


def _shift_up(x, nxt, j):
    if j == 0:
        return x
    n = x.shape[0]
    xr = pltpu.roll(x, n - j, 0)
    nr = pltpu.roll(nxt, SUBLANES - j, 0)
    row = lax.broadcasted_iota(jnp.int32, nxt.shape, 0)
    bot = jnp.where(row >= SUBLANES - j, nr, xr[n - SUBLANES:])
    return jnp.concatenate([xr[:n - SUBLANES], bot], axis=0)


def _chunk_cumsum(y, row_in_chunk):
    s = 1
    while s < CHUNK:
        y = y + jnp.where(row_in_chunk >= s, pltpu.roll(y, s, 0), 0.0)
        s *= 2
    return y


def _chunk_rev_cumsum(y, row_in_chunk):
    n = y.shape[0]
    s = 1
    while s < CHUNK:
        y = y + jnp.where(row_in_chunk + s < CHUNK, pltpu.roll(y, n - s, 0), 0.0)
        s *= 2
    return y


def _lockstep(generators):
    alive = list(generators)
    while alive:
        nxt = []
        for g in alive:
            try:
                next(g)
                nxt.append(g)
            except StopIteration:
                pass
        alive = nxt


def _pick_lane(tile, lane, idx):
    return jnp.sum(jnp.where(lane == idx, tile, 0.0), axis=-1, keepdims=True)


def _divisor_tile(n, cap, unit=LANES):
    best = unit
    t = unit
    while t <= min(n, cap):
        if n % t == 0:
            best = t
        t += unit
    return n if n <= cap else best


def _in_proj(x2, w_an, w_in_p):
    T, D = x2.shape
    N = w_in_p.shape[1]
    tm = min(512, T)

    def body(x_ref, wn_ref, w_ref, proj_ref, xn_ref):
        xn, _ = _rms(x_ref[...], wn_ref[...])
        xn = xn.astype(MXU_DTYPE)
        xn_ref[...] = xn
        proj_ref[...] = jnp.dot(xn, w_ref[...], preferred_element_type=F32)

    return pl.pallas_call(
        body, grid=(T // tm,), name="in_proj",
        in_specs=[pl.BlockSpec((tm, D), lambda i: (i, 0)), pl.BlockSpec((1, D), lambda i: (0, 0)),
                  pl.BlockSpec((D, N), lambda i: (0, 0))],
        out_specs=[pl.BlockSpec((tm, N), lambda i: (i, 0)), pl.BlockSpec((tm, D), lambda i: (i, 0))],
        out_shape=[SDS((T, N), F32), SDS((T, D), MXU_DTYPE)],
        compiler_params=_params(("arbitrary",)),
    )(x2, w_an, w_in_p)


def _mla_pre(proj, cosf, sinf, w_qln, w_kvln, w_uq_p, w_ukv, qnw, knw):
    T = proj.shape[0]
    tm = min(256, T)
    H = MLA_HEADS

    def body(ql_ref, kvl_ref, kpe_ref, cos_ref, sin_ref, wq_ref, wkv_ref, uq_ref, ukv_ref, qnw_ref, knw_ref,
             q_out, k_out, v_out):
        cos, sin = cos_ref[...], sin_ref[...]
        qnw_, knw_ = qnw_ref[...], knw_ref[...]
        qn, _ = _rms(ql_ref[...], wq_ref[...])
        kvn, _ = _rms(kvl_ref[...], wkv_ref[...])
        qraw = _mm(qn, uq_ref[...])
        kvraw = _mm(kvn, ukv_ref[...])
        kpe = _rope(_rms(kpe_ref[...][:, :ROPE], knw_[:, NOPE:])[0], cos, sin)
        for h in range(H):
            qn_h = _rms(qraw[:, h * NOPE:(h + 1) * NOPE], qnw_[:, :NOPE])[0]
            qp_h = _rope(_rms(qraw[:, H * NOPE + h * ROPE:H * NOPE + (h + 1) * ROPE], qnw_[:, NOPE:])[0], cos, sin)
            q_out[h] = (jnp.concatenate([qn_h, qp_h], axis=-1) * ATT_SCALE).astype(MXU_DTYPE)
            kn_h = _rms(kvraw[:, h * 256:h * 256 + NOPE], knw_[:, :NOPE])[0]
            k_out[h] = jnp.concatenate([kn_h, kpe], axis=-1).astype(MXU_DTYPE)
            v_out[h] = kvraw[:, h * 256 + NOPE:(h + 1) * 256].astype(MXU_DTYPE)

    full = lambda a: pl.BlockSpec(a.shape, lambda i: (0,) * a.ndim)
    return pl.pallas_call(
        body, grid=(T // tm,), name="mla_pre",
        in_specs=[pl.BlockSpec((tm, 256), lambda i: (i, P_QLAT // 256)),
                  pl.BlockSpec((tm, 256), lambda i: (i, P_KVLAT // 256)),
                  pl.BlockSpec((tm, 128), lambda i: (i, P_KPE // 128)),
                  pl.BlockSpec((tm, ROPE), lambda i: (i, 0)), pl.BlockSpec((tm, ROPE), lambda i: (i, 0)),
                  full(w_qln), full(w_kvln), full(w_uq_p), full(w_ukv), full(qnw), full(knw)],
        out_specs=[pl.BlockSpec((H, tm, QK_DIM), lambda i: (0, i, 0)),
                   pl.BlockSpec((H, tm, QK_DIM), lambda i: (0, i, 0)),
                   pl.BlockSpec((H, tm, V_DIM), lambda i: (0, i, 0))],
        out_shape=[SDS((H, T, QK_DIM), MXU_DTYPE), SDS((H, T, QK_DIM), MXU_DTYPE), SDS((H, T, V_DIM), MXU_DTYPE)],
        compiler_params=_params(("arbitrary",)),
    )(proj, proj, proj, cosf, sinf, w_qln, w_kvln, w_uq_p, w_ukv, qnw, knw)


def _attn_fwd(q4, k4, v4, B, S, transfer=None):
    H = MLA_HEADS
    bq = min(256, S)
    nq = S // bq

    def body(q_ref, k_ref, v_ref, o_ref, lse_ref):
        causal = (lax.broadcasted_iota(jnp.int32, (bq, bq), 1) <= lax.broadcasted_iota(jnp.int32, (bq, bq), 0))

        def q_step(qi, carry):
            qs = pl.multiple_of(qi * bq, bq)
            q = q_ref[0, pl.ds(qs, bq), :]

            def k_block(ks, c, diagonal):
                m, l, acc = c
                k = k_ref[0, pl.ds(ks, bq), :]
                v = v_ref[0, pl.ds(ks, bq), :]
                s = _mm_nt(q, k)
                if diagonal:
                    s = jnp.where(causal, s, -jnp.inf)
                m_new = jnp.maximum(m, jnp.max(s, axis=-1, keepdims=True))
                p = jnp.exp(s - m_new)
                a = jnp.exp(m - m_new)
                return m_new, a * l + jnp.sum(p, axis=-1, keepdims=True), a * acc + _mm(p, v)

            c = lax.fori_loop(
                0, qi, lambda kj, c: k_block(pl.multiple_of(kj * bq, bq), c, False),
                (jnp.full((bq, 1), -jnp.inf, F32), jnp.zeros((bq, 1), F32), jnp.zeros((bq, V_DIM), F32)))
            m, l, acc = k_block(qs, c, True)
            o_ref[0, pl.ds(qs, bq), :] = acc / l
            lse_ref[0, pl.ds(qs, bq), :] = m + jnp.log(l)
            return carry

        lax.fori_loop(0, nq, q_step, 0)

    spec = lambda d: pl.BlockSpec((1, S, d), lambda h, b: (h, b, 0))
    return _call_beside(
        body, transfer, grid=(H, B), name="attn_fwd",
        in_specs=[spec(QK_DIM), spec(QK_DIM), spec(V_DIM)],
        out_specs=[spec(V_DIM), spec(1)],
        out_shape=[SDS((H, B * S, V_DIM), F32), SDS((H, B * S, 1), F32)],
        scratch_shapes=[], semantics=("arbitrary", "arbitrary"), args=(q4, k4, v4))


def _conv_taps(u, halo, w):
    sh = [_shift_down(u, halo, j) for j in range(CONV_W)]
    c = w[0:1] * sh[3] + w[1:2] * sh[2] + w[2:3] * sh[1] + w[3:4] * sh[0]
    return c, sh


def _gate_values(gab, alog_l, dt_l, lane):
    g = -jnp.exp(alog_l) * jax.nn.softplus(gab + dt_l)
    g = jnp.where(lane < GDN_HEADS, g, 0.0)
    beta = jnp.where((lane >= GDN_HEADS) & (lane < 2 * GDN_HEADS), _sigmoid(gab), 0.0)
    return g, beta


def _gdn_pre(proj, conv_w, alog_l, dt_l, S):
    T = proj.shape[0]
    tm = min(256, T)
    tiles_per_seq = S // tm
    C3 = 3 * GDN_WIDTH
    H = GDN_HEADS

    def body(u_ref, halo_ref, gab_ref, w_ref, alog_ref, dt_ref, q_out, k_out, v_out, gates_out):
        i = pl.program_id(0)
        halo = jnp.where(i % tiles_per_seq == 0, 0.0, halo_ref[...])
        c, _ = _conv_taps(u_ref[...], halo, w_ref[...])
        a = c * _sigmoid(c)
        for h in range(H):
            xq = a[:, h * GDN_DIM:(h + 1) * GDN_DIM]
            xk = a[:, GDN_WIDTH + h * GDN_DIM:GDN_WIDTH + (h + 1) * GDN_DIM]
            q_out[h] = xq * lax.rsqrt(jnp.sum(xq * xq, axis=-1, keepdims=True) + EPS) * GDN_QSCALE
            k_out[h] = xk * lax.rsqrt(jnp.sum(xk * xk, axis=-1, keepdims=True) + EPS)
            v_out[h] = a[:, 2 * GDN_WIDTH + h * GDN_DIM:2 * GDN_WIDTH + (h + 1) * GDN_DIM]
        lane = lax.broadcasted_iota(jnp.int32, (tm, LANES), 1)
        ric = lax.broadcasted_iota(jnp.int32, (tm, LANES), 0) % CHUNK
        g, beta = _gate_values(gab_ref[...], alog_ref[...], dt_ref[...], lane)
        gates_out[...] = _chunk_cumsum(g, ric) + beta

    hspec = pl.BlockSpec((H, tm, GDN_DIM), lambda i: (0, i, 0))
    return pl.pallas_call(
        body, grid=(T // tm,), name="gdn_pre",
        in_specs=[pl.BlockSpec((tm, C3), lambda i: (i, 0)),
                  pl.BlockSpec((SUBLANES, C3), lambda i: (jnp.maximum(i * (tm // SUBLANES) - 1, 0), 0)),
                  pl.BlockSpec((tm, LANES), lambda i: (i, P_GAB // LANES)),
                  pl.BlockSpec((CONV_W, C3), lambda i: (0, 0)),
                  pl.BlockSpec((1, LANES), lambda i: (0, 0)), pl.BlockSpec((1, LANES), lambda i: (0, 0))],
        out_specs=[hspec, hspec, hspec, pl.BlockSpec((tm, LANES), lambda i: (i, 0))],
        out_shape=[SDS((H, T, GDN_DIM), F32)] * 3 + [SDS((T, LANES), F32)],
        compiler_params=_params(("arbitrary",)),
    )(proj, proj, proj, conv_w, alog_l, dt_l)


def _unit_lower_inverses(Ls, eye):
    Ps = [eye - L for L in Ls]
    Ms = [_split(-L) for L in Ls]
    for _ in range(5):
        sq = [_mm_split(m, m) for m in Ms]
        Ms = [_split(s) for s in sq]
        Ps = [p + _mm_split(_split(p), m) for p, m in zip(Ps, Ms)]
    return Ps


def _chunk_decays(gt, lane, h, ri, ci, rcol):
    Gc = _pick_lane(gt, lane, h)
    bt = _pick_lane(gt, lane, h + GDN_HEADS)
    Gb = jnp.broadcast_to(Gc, (CHUNK, CHUNK))
    Gam = jnp.where(ri >= ci, jnp.exp(Gb - Gb.T), 0.0)
    Gl = jnp.sum(jnp.where(rcol == CHUNK - 1, Gc, 0.0), axis=0, keepdims=True)
    return Gc, bt, Gam, jnp.exp(Gc), jnp.exp(Gl - Gc), jnp.exp(Gl)


GDN_UNROLL = 4


def _gdn_fwd(qg, kg, vg, gates, B, S):
    H, D, C = GDN_HEADS, GDN_DIM, CHUNK
    NC = S // C
    U = GDN_UNROLL if NC % GDN_UNROLL == 0 else 1

    def body(q_ref, k_ref, v_ref, g_ref, o_ref, st_ref, ai_ref, u_ref, w_ref, q2_s, au_s, bc_s, w2_s, el_s):
        h = pl.program_id(0)
        lane = lax.broadcasted_iota(jnp.int32, (C, LANES), 1)
        ri = lax.broadcasted_iota(jnp.int32, (C, C), 0)
        ci = lax.broadcasted_iota(jnp.int32, (C, C), 1)
        rcol = lax.broadcasted_iota(jnp.int32, (C, 1), 0)
        eye = (ri == ci).astype(F32)

        def group(gi, c):
            ns = [gi * U + j for j in range(U)]
            css = [pl.multiple_of(n * C, C) for n in ns]
            qs = [q_ref[0, pl.ds(cs, C), :] for cs in css]
            ks = [k_ref[0, pl.ds(cs, C), :] for cs in css]
            vs = [v_ref[0, pl.ds(cs, C), :] for cs in css]
            decs = [_chunk_decays(g_ref[pl.ds(cs, C), :], lane, h, ri, ci, rcol) for cs in css]
            qks = [_mm_nt(jnp.concatenate([q, k], axis=0), k) for q, k in zip(qs, ks)]
            ainvs = _unit_lower_inverses(
                [jnp.where(ri > ci, d[1] * qk[C:] * d[2], 0.0) for qk, d in zip(qks, decs)], eye)
            sols = [_mm_exact(a, jnp.concatenate([v * d[1], k * (d[1] * d[3])], axis=-1))
                    for a, k, v, d in zip(ainvs, ks, vs, decs)]
            atuw = [_mm(qk[:C] * d[2], sol) for qk, d, sol in zip(qks, decs, sols)]
            kduw = [_mm_tn(k * d[4], sol) for k, d, sol in zip(ks, decs, sols)]
            for n, cs, q, a, sol, au, ku, (Gc, bt, Gam, e, f, eL) in zip(ns, css, qs, ainvs, sols, atuw, kduw, decs):
                u_ref[0, pl.ds(cs, C), :] = sol[:, :D]
                w_ref[0, pl.ds(cs, C), :] = sol[:, D:]
                au_s[pl.ds(cs, C), :] = au[:, :D]
                q2_s[pl.ds(cs, C), :] = q * e - au[:, D:]
                bc_s[n] = ku[:, :D]
                w2_s[n] = ku[:, D:]
                el_s[n] = jnp.broadcast_to(eL, (SUBLANES, LANES))
                ai_ref[0, n] = a
            return c

        lax.fori_loop(0, NC // U, group, 0)

        def step(n, S_):
            cs = pl.multiple_of(n * C, C)
            o_ref[0, pl.ds(cs, C), :] = _mm(q2_s[pl.ds(cs, C), :], S_) + au_s[pl.ds(cs, C), :]
            st_ref[0, n] = S_
            return S_ * el_s[n, 0:1, :] + bc_s[n] - _mm(w2_s[n], S_)

        lax.fori_loop(0, NC, step, jnp.zeros((D, D), F32))

    spec = pl.BlockSpec((1, S, D), lambda h, b: (h, b, 0))
    return pl.pallas_call(
        body, grid=(H, B), name="gdn_fwd",
        in_specs=[spec, spec, spec, pl.BlockSpec((S, LANES), lambda h, b: (b, 0))],
        out_specs=[spec, pl.BlockSpec((1, NC, D, D), lambda h, b: (h, b, 0, 0)),
                   pl.BlockSpec((1, NC, C, C), lambda h, b: (h, b, 0, 0)), spec, spec],
        out_shape=[SDS((H, B * S, D), F32), SDS((H, B * NC, D, D), F32), SDS((H, B * NC, C, C), F32),
                   SDS((H, B * S, D), F32), SDS((H, B * S, D), F32)],
        scratch_shapes=[pltpu.VMEM((S, D), F32), pltpu.VMEM((S, D), F32), pltpu.VMEM((NC, D, D), F32),
                        pltpu.VMEM((NC, D, D), F32), pltpu.VMEM((NC, SUBLANES, LANES), F32)],
        compiler_params=_params(("arbitrary", "arbitrary")),
    )(qg, kg, vg, gates)


def _mix_out(o_mla, o_gdn, proj, x2, mla_w, gdn_w, w_out):
    T, D = x2.shape
    tm = min(512, T)
    H = MLA_HEADS

    def body(om_ref, og_ref, z_ref, x_ref, mw_ref, gw_ref, w_ref, h_ref, mix_ref):
        z = z_ref[...]
        parts = [_rms(om_ref[h], mw_ref[h:h + 1, :])[0] for h in range(H)]
        for h in range(GDN_HEADS):
            zh = z[:, h * GDN_DIM:(h + 1) * GDN_DIM]
            parts.append(_rms(og_ref[h], gw_ref[...])[0] * (zh * _sigmoid(zh)))
        mix = jnp.concatenate(parts, axis=-1).astype(MXU_DTYPE)
        mix_ref[...] = mix
        h_ref[...] = x_ref[...] + jnp.dot(mix, w_ref[...], preferred_element_type=F32)

    hspec = pl.BlockSpec((H, tm, V_DIM), lambda i: (0, i, 0))
    return pl.pallas_call(
        body, grid=(T // tm,), name="mix_out",
        in_specs=[hspec, hspec, pl.BlockSpec((tm, GDN_WIDTH), lambda i: (i, P_GZ // GDN_WIDTH)),
                  pl.BlockSpec((tm, D), lambda i: (i, 0)),
                  pl.BlockSpec((H, V_DIM), lambda i: (0, 0)), pl.BlockSpec((1, GDN_DIM), lambda i: (0, 0)),
                  pl.BlockSpec((D, D), lambda i: (0, 0))],
        out_specs=[pl.BlockSpec((tm, D), lambda i: (i, 0)), pl.BlockSpec((tm, D), lambda i: (i, 0))],
        out_shape=[SDS((T, D), F32), SDS((T, D), MXU_DTYPE)],
        compiler_params=_params(("arbitrary",)),
    )(o_mla, o_gdn, proj, x2, mla_w, gdn_w, w_out)


def _mlp_fwd(h2, w_mn, w_up, w_down, target):
    T, D = h2.shape
    nf, _, tf = w_up.shape
    F = nf * tf
    tm = min(512, T)

    def body(h_ref, wn_ref, up_w, down_w, t_ref, up_ref, hn_ref, dy_ref, loss_ref, y_acc):
        j = pl.program_id(1)

        @pl.when(j == 0)
        def _():
            hn_ref[...] = _rms(h_ref[...], wn_ref[...])[0].astype(MXU_DTYPE)
            y_acc[...] = h_ref[...]

        up = jnp.dot(hn_ref[...], up_w[0], preferred_element_type=F32)
        up_ref[...] = up
        r = jnp.maximum(up, 0.0)
        y_acc[...] += _mm(r * r, down_w[...])

        @pl.when(j == nf - 1)
        def _():
            err = y_acc[...] - t_ref[...]
            dy_ref[...] = err / D
            loss_ref[...] = jnp.full((1, SUBLANES, LANES), jnp.sum(err * err), F32)

    return pl.pallas_call(
        body, grid=(T // tm, nf), name="mlp_fwd",
        in_specs=[pl.BlockSpec((tm, D), lambda i, j: (i, 0)), pl.BlockSpec((1, D), lambda i, j: (0, 0)),
                  pl.BlockSpec((1, D, tf), lambda i, j: (j, 0, 0)), pl.BlockSpec((tf, D), lambda i, j: (j, 0)),
                  pl.BlockSpec((tm, D), lambda i, j: (i, 0))],
        out_specs=[pl.BlockSpec((tm, tf), lambda i, j: (i, j)), pl.BlockSpec((tm, D), lambda i, j: (i, 0)),
                   pl.BlockSpec((tm, D), lambda i, j: (i, 0)),
                   pl.BlockSpec((1, SUBLANES, LANES), lambda i, j: (i, 0, 0))],
        out_shape=[SDS((T, F), F32), SDS((T, D), MXU_DTYPE), SDS((T, D), F32),
                   SDS((T // tm, SUBLANES, LANES), F32)],
        scratch_shapes=[pltpu.VMEM((tm, D), F32)],
        compiler_params=_params(("arbitrary", "arbitrary")),
    )(h2, w_mn, w_up, w_down, target)


def _mlp_bwd(dy, up, h2, w_mn, w_up, w_down):
    T, D = h2.shape
    nf, _, tf = w_up.shape
    F = nf * tf
    tm = min(512, T)

    def body(dy_ref, up_ref, h_ref, wn_ref, up_w, down_w, dh_ref, dhb_ref, dup_ref, act_ref, dyb_ref, dwn_ref, acc):
        i, j = pl.program_id(0), pl.program_id(1)

        @pl.when((i == 0) & (j == 0))
        def _():
            dwn_ref[...] = jnp.zeros_like(dwn_ref)

        @pl.when(j == 0)
        def _():
            acc[...] = jnp.zeros_like(acc)
            dyb_ref[...] = dy_ref[...].astype(MXU_DTYPE)

        r = jnp.maximum(up_ref[...], 0.0)
        act_ref[...] = (r * r).astype(MXU_DTYPE)
        dup = (_mm_nt(dyb_ref[...], down_w[...]) * (2.0 * r)).astype(MXU_DTYPE)
        dup_ref[...] = dup
        acc[...] += _mm_nt(dup, up_w[0])

        @pl.when(j == nf - 1)
        def _():
            hv = h_ref[...]
            _, rr = _rms(hv, wn_ref[...])
            dx, dw = _rms_bwd(acc[...], hv, wn_ref[...], rr)
            dh = dy_ref[...] + dx
            dh_ref[...] = dh
            dhb_ref[...] = dh.astype(MXU_DTYPE)
            dwn_ref[...] += dw

    row = lambda i, j: (i, 0)
    return pl.pallas_call(
        body, grid=(T // tm, nf), name="mlp_bwd",
        in_specs=[pl.BlockSpec((tm, D), row), pl.BlockSpec((tm, tf), lambda i, j: (i, j)), pl.BlockSpec((tm, D), row),
                  pl.BlockSpec((1, D), lambda i, j: (0, 0)),
                  pl.BlockSpec((1, D, tf), lambda i, j: (j, 0, 0)), pl.BlockSpec((tf, D), lambda i, j: (j, 0))],
        out_specs=[pl.BlockSpec((tm, D), row), pl.BlockSpec((tm, D), row),
                   pl.BlockSpec((tm, tf), lambda i, j: (i, j)), pl.BlockSpec((tm, tf), lambda i, j: (i, j)),
                   pl.BlockSpec((tm, D), row), pl.BlockSpec((1, D), lambda i, j: (0, 0))],
        out_shape=[SDS((T, D), F32), SDS((T, D), MXU_DTYPE), SDS((T, F), MXU_DTYPE), SDS((T, F), MXU_DTYPE),
                   SDS((T, D), MXU_DTYPE), SDS((1, D), F32)],
        scratch_shapes=[pltpu.VMEM((tm, D), F32)],
        compiler_params=_params(("arbitrary", "arbitrary")),
    )(dy, up, h2, w_mn, w_up, w_down)


def _mix_bwd(dhb, o_mla, o_gdn, proj, mla_w, gdn_w, w_out):
    T, D = dhb.shape
    tm = min(512, T)
    H = MLA_HEADS

    def body(dh_ref, om_ref, og_ref, z_ref, mw_ref, gw_ref, w_ref, dom_ref, dog_ref, dz_ref, dmw_ref, dgw_ref):
        @pl.when(pl.program_id(0) == 0)
        def _():
            dmw_ref[...] = jnp.zeros_like(dmw_ref)
            dgw_ref[...] = jnp.zeros_like(dgw_ref)

        dmix = _mm_nt(dh_ref[...], w_ref[...])
        z = z_ref[...]
        dmw, dzs = [], []
        dgw = jnp.zeros((1, GDN_DIM), F32)
        for h in range(H):
            o = om_ref[h]
            w = mw_ref[h:h + 1, :]
            _, r = _rms(o, w)
            dx, dw = _rms_bwd(dmix[:, h * V_DIM:(h + 1) * V_DIM], o, w, r)
            dom_ref[h] = dx
            dmw.append(dw)
        for h in range(GDN_HEADS):
            o = og_ref[h]
            w = gw_ref[...]
            zh = z[:, h * GDN_DIM:(h + 1) * GDN_DIM]
            sg = _sigmoid(zh)
            yn, r = _rms(o, w)
            dy = dmix[:, H * V_DIM + h * GDN_DIM:H * V_DIM + (h + 1) * GDN_DIM]
            dzs.append(dy * yn * (sg * (1.0 + zh * (1.0 - sg))))
            dx, dw = _rms_bwd(dy * (zh * sg), o, w, r)
            dog_ref[h] = dx
            dgw = dgw + dw
        dz_ref[...] = jnp.concatenate(dzs, axis=-1)
        dmw_ref[...] += jnp.concatenate(dmw, axis=0)
        dgw_ref[...] += dgw

    hspec = pl.BlockSpec((H, tm, V_DIM), lambda i: (0, i, 0))
    return pl.pallas_call(
        body, grid=(T // tm,), name="mix_bwd",
        in_specs=[pl.BlockSpec((tm, D), lambda i: (i, 0)), hspec, hspec,
                  pl.BlockSpec((tm, GDN_WIDTH), lambda i: (i, P_GZ // GDN_WIDTH)),
                  pl.BlockSpec((H, V_DIM), lambda i: (0, 0)), pl.BlockSpec((1, GDN_DIM), lambda i: (0, 0)),
                  pl.BlockSpec((D, D), lambda i: (0, 0))],
        out_specs=[hspec, hspec, pl.BlockSpec((tm, GDN_WIDTH), lambda i: (i, 0)),
                   pl.BlockSpec((H, V_DIM), lambda i: (0, 0)), pl.BlockSpec((1, GDN_DIM), lambda i: (0, 0))],
        out_shape=[SDS((H, T, V_DIM), F32), SDS((H, T, GDN_DIM), F32), SDS((T, GDN_WIDTH), F32),
                   SDS((H, V_DIM), F32), SDS((1, GDN_DIM), F32)],
        compiler_params=_params(("arbitrary",)),
    )(dhb, o_mla, o_gdn, proj, mla_w, gdn_w, w_out)


def _attn_bwd(q4, k4, v4, do4, o4, lse4, B, S, transfer=None):
    H = MLA_HEADS
    bq = min(256, S)
    nq = S // bq

    def body(q_ref, k_ref, v_ref, do_ref, o_ref, lse_ref, dq_ref, dk_ref, dv_ref, delta):
        dq_ref[...] = jnp.zeros_like(dq_ref)
        dk_ref[...] = jnp.zeros_like(dk_ref)
        dv_ref[...] = jnp.zeros_like(dv_ref)
        delta[...] = jnp.sum(do_ref[0] * o_ref[0], axis=-1, keepdims=True)

        causal = (lax.broadcasted_iota(jnp.int32, (bq, bq), 1) <= lax.broadcasted_iota(jnp.int32, (bq, bq), 0))

        def k_step(kj, carry):
            ks = pl.multiple_of(kj * bq, bq)
            k = k_ref[0, pl.ds(ks, bq), :]
            v = v_ref[0, pl.ds(ks, bq), :]

            def q_block(qs, diagonal):
                q = q_ref[0, pl.ds(qs, bq), :]
                do = do_ref[0, pl.ds(qs, bq), :].astype(MXU_DTYPE)
                p = jnp.exp(_mm_nt(q, k) - lse_ref[0, pl.ds(qs, bq), :])
                if diagonal:
                    p = jnp.where(causal, p, 0.0)
                dv_ref[0, pl.ds(ks, bq), :] += _mm_tn(p, do)
                ds = p * (_mm_nt(do, v) - delta[pl.ds(qs, bq), :])
                dq_ref[0, pl.ds(qs, bq), :] += _mm(ds, k)
                dk_ref[0, pl.ds(ks, bq), :] += _mm_tn(ds, q)

            q_block(ks, True)

            def q_step(qi, c):
                q_block(pl.multiple_of(qi * bq, bq), False)
                return c

            lax.fori_loop(kj + 1, nq, q_step, 0)
            return carry

        lax.fori_loop(0, nq, k_step, 0)

    spec = lambda d: pl.BlockSpec((1, S, d), lambda h, b: (h, b, 0))
    return _call_beside(
        body, transfer, grid=(H, B), name="attn_bwd",
        in_specs=[spec(QK_DIM), spec(QK_DIM), spec(V_DIM), spec(V_DIM), spec(V_DIM), spec(1)],
        out_specs=[spec(QK_DIM), spec(QK_DIM), spec(V_DIM)],
        out_shape=[SDS((H, B * S, QK_DIM), F32), SDS((H, B * S, QK_DIM), F32), SDS((H, B * S, V_DIM), F32)],
        scratch_shapes=[pltpu.VMEM((S, 1), F32)], semantics=("arbitrary", "arbitrary"),
        args=(q4, k4, v4, do4, o4, lse4))


def _gdn_bwd(qg, kg, vg, gates, states, ainv, u4, w4, do4, B, S, transfer=None):
    H, D, C = GDN_HEADS, GDN_DIM, CHUNK
    NC = S // C
    U = GDN_UNROLL if NC % GDN_UNROLL == 0 else 1

    def body(q_ref, k_ref, v_ref, g_ref, st_ref, ai_ref, u_ref, w_ref, do_ref, dq_ref, dk_ref, dv_ref, dgb_ref,
             kd_s, x1_s, x2_s, el_s, dvn_s, ds_s, w2t_s):
        h = pl.program_id(0)
        lane = lax.broadcasted_iota(jnp.int32, (C, LANES), 1)
        ri = lax.broadcasted_iota(jnp.int32, (C, C), 0)
        ci = lax.broadcasted_iota(jnp.int32, (C, C), 1)
        rcol = lax.broadcasted_iota(jnp.int32, (C, 1), 0)

        def rsum(a):
            return jnp.sum(a, axis=-1, keepdims=True)

        def blocks(fn):
            def group(gi, c):
                _lockstep([fn(gi * U + j) for j in range(U)])
                return c
            lax.fori_loop(0, NC // U, group, 0)

        def prepare(n):
            cs = pl.multiple_of(n * C, C)
            q = q_ref[0, pl.ds(cs, C), :]
            k = k_ref[0, pl.ds(cs, C), :]
            do = do_ref[0, pl.ds(cs, C), :]
            Gc, bt, Gam, e, f, eL = _chunk_decays(g_ref[pl.ds(cs, C), :], lane, h, ri, ci, rcol)
            At = _mm_nt(q, k) * Gam
            yield
            x1 = _mm_tn(At, do)
            x2 = _mm_tn(q * e, do)
            kd = k * f
            w = w_ref[0, pl.ds(cs, C), :]
            yield
            x1_s[pl.ds(cs, C), :] = x1
            x2_s[n] = x2 - _mm_tn(w, x1)
            w2t_s[n] = _mm_tn(w, kd)
            kd_s[pl.ds(cs, C), :] = kd
            el_s[n] = jnp.broadcast_to(eL, (SUBLANES, LANES))

        blocks(prepare)

        def recur(t, dS):
            n = NC - 1 - t
            cs = pl.multiple_of(n * C, C)
            ds_s[n] = dS
            dvn_s[pl.ds(cs, C), :] = x1_s[pl.ds(cs, C), :] + _mm(kd_s[pl.ds(cs, C), :], dS)
            return x2_s[n] + el_s[n, 0:1, :] * dS - _mm(w2t_s[n], dS)

        lax.fori_loop(0, NC, recur, jnp.zeros((D, D), F32))

        def local(n):
            cs = pl.multiple_of(n * C, C)
            q = q_ref[0, pl.ds(cs, C), :]
            k = k_ref[0, pl.ds(cs, C), :]
            v = v_ref[0, pl.ds(cs, C), :]
            do = do_ref[0, pl.ds(cs, C), :]
            u = u_ref[0, pl.ds(cs, C), :]
            w = w_ref[0, pl.ds(cs, C), :]
            dvn = dvn_s[pl.ds(cs, C), :]
            dS = ds_s[n]
            Gc, bt, Gam, e, f, eL = _chunk_decays(g_ref[pl.ds(cs, C), :], lane, h, ri, ci, rcol)
            S0 = st_ref[0, n]
            Ainv = ai_ref[0, n]
            qk = _mm_nt(jnp.concatenate([q, k], axis=0), k)
            QK, KK = qk[:C], qk[C:]
            be = bt * e
            sol = jnp.concatenate([u, w], axis=-1)
            vn = u - _mm(w, S0)
            yield
            dAt = jnp.where(ri >= ci, _mm_nt(do, vn), 0.0)
            dqd = _mm_nt(do, S0)
            dw = -_mm_nt(dvn, S0)
            dkd = _mm_nt(vn, dS)
            deL = jnp.sum(rsum(dS * S0), axis=0, keepdims=True)
            yield
            dR = _mm_exact(Ainv.T, jnp.concatenate([dvn, dw], axis=-1))
            dR1, dR2 = dR[:, :D], dR[:, D:]
            yield
            dL = jnp.where(ri > ci, -_mm_nt(dR, sol), 0.0)
            yield
            dv_ref[0, pl.ds(cs, C), :] = dR1 * bt
            r2 = rsum(dR2 * k)
            X = dL * Gam
            dbt = rsum(dR1 * v) + r2 * e + rsum(X * KK)
            de = r2 * bt + rsum(dqd * q)
            dKK = X * bt
            dQK = dAt * Gam
            dq_ref[0, pl.ds(cs, C), :] = _mm(dQK, k) + dqd * e
            dk_ref[0, pl.ds(cs, C), :] = dR2 * be + _mm(dKK + dKK.T, k) + _mm_tn(dQK, q) + dkd * f
            df = rsum(dkd * k)
            Z = (dL * (bt * KK) + dAt * QK) * Gam
            dG = rsum(Z) - rsum(Z.T) + de * e - df * f
            dGl = jnp.sum(df * f, axis=0, keepdims=True) + deL * eL
            dG = dG + jnp.where(rcol == C - 1, dGl, 0.0)
            dgb_ref[0, pl.ds(cs, C), :] = jnp.where(lane == 0, dG, jnp.where(lane == 1, dbt, 0.0))

        blocks(local)

    spec = pl.BlockSpec((1, S, D), lambda h, b: (h, b, 0))
    return _call_beside(
        body, transfer, grid=(H, B), name="gdn_bwd",
        in_specs=[spec, spec, spec, pl.BlockSpec((S, LANES), lambda h, b: (b, 0)),
                  pl.BlockSpec((1, NC, D, D), lambda h, b: (h, b, 0, 0)),
                  pl.BlockSpec((1, NC, C, C), lambda h, b: (h, b, 0, 0)), spec, spec, spec],
        out_specs=[spec, spec, spec, spec],
        out_shape=[SDS((H, B * S, D), F32)] * 4,
        scratch_shapes=[pltpu.VMEM((S, D), F32), pltpu.VMEM((S, D), F32), pltpu.VMEM((NC, D, D), F32),
                        pltpu.VMEM((NC, SUBLANES, LANES), F32), pltpu.VMEM((S, D), F32),
                        pltpu.VMEM((NC, D, D), F32), pltpu.VMEM((NC, D, D), F32)],
        semantics=("arbitrary", "arbitrary"), args=(qg, kg, vg, gates, states, ainv, u4, w4, do4))


def _gdn_pre_bwd(proj, conv_w, alog_l, dt_l, dq4, dk4, dv4, dgb4, S):
    T = proj.shape[0]
    tm = min(256, T)
    tiles_per_seq = S // tm
    C3 = 3 * GDN_WIDTH
    H = GDN_HEADS

    def body(u_ref, halo_ref, gab_ref, w_ref, alog_ref, dt_ref, dq_ref, dk_ref, dv_ref, dgb_ref,
             dc_ref, dgab_ref, dcw_ref, dalog_ref, ddt_ref):
        i = pl.program_id(0)

        @pl.when(i == 0)
        def _():
            dcw_ref[...] = jnp.zeros_like(dcw_ref)
            dalog_ref[...] = jnp.zeros_like(dalog_ref)
            ddt_ref[...] = jnp.zeros_like(ddt_ref)

        halo = jnp.where(i % tiles_per_seq == 0, 0.0, halo_ref[...])
        c, sh = _conv_taps(u_ref[...], halo, w_ref[...])
        sg = _sigmoid(c)
        a = c * sg
        das = [None] * (3 * H)
        for h in range(H):
            xq = a[:, h * GDN_DIM:(h + 1) * GDN_DIM]
            xk = a[:, GDN_WIDTH + h * GDN_DIM:GDN_WIDTH + (h + 1) * GDN_DIM]
            das[h] = _l2n_bwd(dq_ref[h], xq, GDN_QSCALE)
            das[H + h] = _l2n_bwd(dk_ref[h], xk, 1.0)
            das[2 * H + h] = dv_ref[h]
        dc = jnp.concatenate(das, axis=-1) * (sg * (1.0 + c * (1.0 - sg)))
        dc_ref[...] = dc
        dcw_ref[...] += jnp.concatenate(
            [jnp.sum(dc * sh[CONV_W - 1 - t], axis=0, keepdims=True) for t in range(CONV_W)], axis=0)
        lane = lax.broadcasted_iota(jnp.int32, (tm, LANES), 1)
        ric = lax.broadcasted_iota(jnp.int32, (tm, LANES), 0) % CHUNK
        dG = jnp.zeros((tm, LANES), F32)
        for h in range(H):
            t = dgb_ref[h]
            dG = dG + jnp.where(lane == h, _pick_lane(t, lane, 0), 0.0) \
                    + jnp.where(lane == h + H, _pick_lane(t, lane, 1), 0.0)
        is_g = lane < H
        dg = jnp.where(is_g, _chunk_rev_cumsum(jnp.where(is_g, dG, 0.0), ric), 0.0)
        gab = gab_ref[...]
        g, beta = _gate_values(gab, alog_ref[...], dt_ref[...], lane)
        dga = jnp.where(is_g, dg * (-jnp.exp(alog_ref[...])) * _sigmoid(gab + dt_ref[...]), 0.0)
        dgb = jnp.where(is_g, 0.0, dG) * beta * (1.0 - beta)
        dgab_ref[...] = dga + dgb
        dalog_ref[...] += jnp.sum(dg * g, axis=0, keepdims=True)
        ddt_ref[...] += jnp.sum(dga, axis=0, keepdims=True)

    hspec = pl.BlockSpec((H, tm, GDN_DIM), lambda i: (0, i, 0))
    vec = pl.BlockSpec((1, LANES), lambda i: (0, 0))
    return pl.pallas_call(
        body, grid=(T // tm,), name="gdn_pre_bwd",
        in_specs=[pl.BlockSpec((tm, C3), lambda i: (i, 0)),
                  pl.BlockSpec((SUBLANES, C3), lambda i: (jnp.maximum(i * (tm // SUBLANES) - 1, 0), 0)),
                  pl.BlockSpec((tm, LANES), lambda i: (i, P_GAB // LANES)),
                  pl.BlockSpec((CONV_W, C3), lambda i: (0, 0)), vec, vec, hspec, hspec, hspec, hspec],
        out_specs=[pl.BlockSpec((tm, C3), lambda i: (i, 0)), pl.BlockSpec((tm, LANES), lambda i: (i, 0)),
                   pl.BlockSpec((CONV_W, C3), lambda i: (0, 0)), vec, vec],
        out_shape=[SDS((T, C3), F32), SDS((T, LANES), F32), SDS((CONV_W, C3), F32),
                   SDS((1, LANES), F32), SDS((1, LANES), F32)],
        compiler_params=_params(("arbitrary",)),
    )(proj, proj, proj, conv_w, alog_l, dt_l, dq4, dk4, dv4, dgb4)


def _conv_bwd_input(dc, conv_w, S):
    T, C3 = dc.shape
    tm = min(256, T)
    tiles_per_seq = S // tm
    nblk = T // SUBLANES

    def body(dc_ref, nxt_ref, w_ref, du_ref):
        i = pl.program_id(0)
        nxt = jnp.where(i % tiles_per_seq == tiles_per_seq - 1, 0.0, nxt_ref[...])
        x = dc_ref[...]
        w = w_ref[...]
        du = w[3:4] * x
        for j in range(1, CONV_W):
            du = du + w[3 - j:4 - j] * _shift_up(x, nxt, j)
        du_ref[...] = du

    return pl.pallas_call(
        body, grid=(T // tm,), name="conv_bwd_input",
        in_specs=[pl.BlockSpec((tm, C3), lambda i: (i, 0)),
                  pl.BlockSpec((SUBLANES, C3), lambda i: (jnp.minimum((i + 1) * (tm // SUBLANES), nblk - 1), 0)),
                  pl.BlockSpec((CONV_W, C3), lambda i: (0, 0))],
        out_specs=pl.BlockSpec((tm, C3), lambda i: (i, 0)),
        out_shape=SDS((T, C3), F32),
        compiler_params=_params(("arbitrary",)),
    )(dc, dc, conv_w)


def _mla_pre_bwd(proj, cosf, sinf, w_qln, w_kvln, w_uq_p, w_ukv, qnw, knw, dq4, dk4, dv4):
    T = proj.shape[0]
    tm = min(256, T)
    H = MLA_HEADS

    def body(ql_ref, kvl_ref, kpe_ref, cos_ref, sin_ref, wq_ref, wkv_ref, uq_ref, ukv_ref, qnw_ref, knw_ref,
             dq_ref, dk_ref, dv_ref,
             dql_ref, dkvl_ref, dkpe_ref, dqraw_ref, dkvraw_ref, qn_ref, kvn_ref, dwq_ref, dwkv_ref, dqnw_ref, dknw_ref):
        @pl.when(pl.program_id(0) == 0)
        def _():
            for r in (dwq_ref, dwkv_ref, dqnw_ref, dknw_ref):
                r[...] = jnp.zeros_like(r)

        cos, sin = cos_ref[...], sin_ref[...]
        qnw_, knw_ = qnw_ref[...], knw_ref[...]
        ql, kvl = ql_ref[...], kvl_ref[...]
        kpe_raw = kpe_ref[...][:, :ROPE]
        qn, rq = _rms(ql, wq_ref[...])
        kvn, rkv = _rms(kvl, wkv_ref[...])
        qn_ref[...] = qn.astype(MXU_DTYPE)
        kvn_ref[...] = kvn.astype(MXU_DTYPE)
        qraw = _mm(qn, uq_ref[...])
        kvraw = _mm(kvn, ukv_ref[...])
        dq_nope, dq_pe, dkv_parts = [], [], []
        dqnw_n = jnp.zeros((1, NOPE), F32)
        dqnw_p = jnp.zeros((1, ROPE), F32)
        dknw_n = jnp.zeros((1, NOPE), F32)
        dkpe = jnp.zeros((tm, ROPE), F32)
        for h in range(H):
            dq = dq_ref[h] * ATT_SCALE
            x = qraw[:, h * NOPE:(h + 1) * NOPE]
            dx, dw = _rms_bwd(dq[:, :NOPE], x, qnw_[:, :NOPE], _rms(x, qnw_[:, :NOPE])[1])
            dq_nope.append(dx)
            dqnw_n = dqnw_n + dw
            x = qraw[:, H * NOPE + h * ROPE:H * NOPE + (h + 1) * ROPE]
            dx, dw = _rms_bwd(_rope_bwd(dq[:, NOPE:], cos, sin), x, qnw_[:, NOPE:], _rms(x, qnw_[:, NOPE:])[1])
            dq_pe.append(dx)
            dqnw_p = dqnw_p + dw
            dk = dk_ref[h]
            x = kvraw[:, h * 256:h * 256 + NOPE]
            dx, dw = _rms_bwd(dk[:, :NOPE], x, knw_[:, :NOPE], _rms(x, knw_[:, :NOPE])[1])
            dknw_n = dknw_n + dw
            dkpe = dkpe + dk[:, NOPE:]
            dkv_parts += [dx, dv_ref[h]]
        dx, dknw_p = _rms_bwd(_rope_bwd(dkpe, cos, sin), kpe_raw, knw_[:, NOPE:], _rms(kpe_raw, knw_[:, NOPE:])[1])
        dkpe_ref[...] = jnp.concatenate([dx, jnp.zeros((tm, LANES - ROPE), F32)], axis=-1)
        dqraw = jnp.concatenate(dq_nope + dq_pe, axis=-1).astype(MXU_DTYPE)
        dkvraw = jnp.concatenate(dkv_parts, axis=-1).astype(MXU_DTYPE)
        dqraw_ref[...] = dqraw
        dkvraw_ref[...] = dkvraw
        dx, dw = _rms_bwd(_mm_nt(dqraw, uq_ref[...]), ql, wq_ref[...], rq)
        dql_ref[...] = dx
        dwq_ref[...] += dw
        dx, dw = _rms_bwd(_mm_nt(dkvraw, ukv_ref[...]), kvl, wkv_ref[...], rkv)
        dkvl_ref[...] = dx
        dwkv_ref[...] += dw
        dqnw_ref[...] += jnp.concatenate([dqnw_n, dqnw_p], axis=-1)
        dknw_ref[...] += jnp.concatenate([dknw_n, dknw_p], axis=-1)

    full = lambda a: pl.BlockSpec(a.shape, lambda i: (0,) * a.ndim)
    rows = lambda n: pl.BlockSpec((tm, n), lambda i: (i, 0))
    const = lambda n: pl.BlockSpec((1, n), lambda i: (0, 0))
    NQ, NKV = w_uq_p.shape[1], w_ukv.shape[1]
    return pl.pallas_call(
        body, grid=(T // tm,), name="mla_pre_bwd",
        in_specs=[pl.BlockSpec((tm, 256), lambda i: (i, P_QLAT // 256)),
                  pl.BlockSpec((tm, 256), lambda i: (i, P_KVLAT // 256)),
                  pl.BlockSpec((tm, 128), lambda i: (i, P_KPE // 128)),
                  rows(ROPE), rows(ROPE),
                  full(w_qln), full(w_kvln), full(w_uq_p), full(w_ukv), full(qnw), full(knw),
                  pl.BlockSpec((H, tm, QK_DIM), lambda i: (0, i, 0)),
                  pl.BlockSpec((H, tm, QK_DIM), lambda i: (0, i, 0)),
                  pl.BlockSpec((H, tm, V_DIM), lambda i: (0, i, 0))],
        out_specs=[rows(Q_LORA), rows(KV_LORA), rows(LANES), rows(NQ), rows(NKV), rows(Q_LORA), rows(KV_LORA),
                   const(Q_LORA), const(KV_LORA), const(QK_DIM), const(QK_DIM)],
        out_shape=[SDS((T, Q_LORA), F32), SDS((T, KV_LORA), F32), SDS((T, LANES), F32),
                   SDS((T, NQ), MXU_DTYPE), SDS((T, NKV), MXU_DTYPE),
                   SDS((T, Q_LORA), MXU_DTYPE), SDS((T, KV_LORA), MXU_DTYPE),
                   SDS((1, Q_LORA), F32), SDS((1, KV_LORA), F32), SDS((1, QK_DIM), F32), SDS((1, QK_DIM), F32)],
        compiler_params=_params(("arbitrary",)),
    )(proj, proj, proj, cosf, sinf, w_qln, w_kvln, w_uq_p, w_ukv, qnw, knw, dq4, dk4, dv4)


def _in_proj_bwd(dgqkv, dgz, dql, dkvl, dkpe, dgab, w_in_p, dh, x2, w_an):
    T, D = x2.shape
    N = w_in_p.shape[1]
    tm = min(512, T)

    def body(a_ref, b_ref, c_ref, d_ref, e_ref, f_ref, w_ref, dh_ref, x_ref, wn_ref, dx_ref, dp_ref, dwn_ref):
        @pl.when(pl.program_id(0) == 0)
        def _():
            dwn_ref[...] = jnp.zeros_like(dwn_ref)

        dp = jnp.concatenate([a_ref[...], b_ref[...], c_ref[...], d_ref[...], e_ref[...], f_ref[...]],
                             axis=-1).astype(MXU_DTYPE)
        dp_ref[...] = dp
        x = x_ref[...]
        _, r = _rms(x, wn_ref[...])
        dx, dw = _rms_bwd(_mm_nt(dp, w_ref[...]), x, wn_ref[...], r)
        dx_ref[...] = dh_ref[...] + dx
        dwn_ref[...] += dw

    rows = lambda n: pl.BlockSpec((tm, n), lambda i: (i, 0))
    return pl.pallas_call(
        body, grid=(T // tm,), name="in_proj_bwd",
        in_specs=[rows(dgqkv.shape[1]), rows(dgz.shape[1]), rows(dql.shape[1]), rows(dkvl.shape[1]),
                  rows(dkpe.shape[1]), rows(dgab.shape[1]),
                  pl.BlockSpec((D, N), lambda i: (0, 0)), rows(D), rows(D), pl.BlockSpec((1, D), lambda i: (0, 0))],
        out_specs=[rows(D), rows(N), pl.BlockSpec((1, D), lambda i: (0, 0))],
        out_shape=[SDS((T, D), F32), SDS((T, N), MXU_DTYPE), SDS((1, D), F32)],
        compiler_params=_params(("arbitrary",)),
    )(dgqkv, dgz, dql, dkvl, dkpe, dgab, w_in_p, dh, x2, w_an)


def _wgrad(a, b, name, column_shards=False):
    T, M = a.shape
    N = b.shape[1]
    tM = _divisor_tile(M, 512)
    tN = N // N_DEV if column_shards else _divisor_tile(N, 1536)
    tk = min(T, 1024)
    nk = T // tk

    def body(a_ref, b_ref, o_ref, acc):
        k = pl.program_id(2)

        @pl.when(k == 0)
        def _():
            acc[...] = jnp.zeros_like(acc)

        acc[...] += _mm_tn(a_ref[...], b_ref[...])

        @pl.when(k == nk - 1)
        def _():
            o_ref[...] = acc[...].astype(WIRE_DTYPE).reshape(o_ref.shape)

    if column_shards:
        out_spec, out_shape = pl.BlockSpec((1, tM, tN), lambda i, j, k: (j, i, 0)), SDS((N_DEV, M, tN), WIRE_DTYPE)
    else:
        out_spec, out_shape = pl.BlockSpec((tM, tN), lambda i, j, k: (i, j)), SDS((M, N), WIRE_DTYPE)
    return pl.pallas_call(
        body, grid=(M // tM, N // tN, nk), name=name,
        in_specs=[pl.BlockSpec((tk, tM), lambda i, j, k: (k, i)), pl.BlockSpec((tk, tN), lambda i, j, k: (k, j))],
        out_specs=out_spec, out_shape=out_shape,
        scratch_shapes=[pltpu.VMEM((tM, tN), F32)],
        compiler_params=_params(("arbitrary", "arbitrary", "arbitrary")),
    )(a, b)


def _adamw(g, w, m, v):
    m = ADAM_B1 * m + (1.0 - ADAM_B1) * g
    v = ADAM_B2 * v + (1.0 - ADAM_B2) * jnp.square(g)
    m_hat = m / (1.0 - ADAM_B1 ** ADAM_STEP)
    v_hat = v / (1.0 - ADAM_B2 ** ADAM_STEP)
    return -ADAM_LR * (m_hat / (jnp.sqrt(v_hat) + ADAM_EPS) + ADAM_WD * w), m, v


def _reduce_adamw(parts, w, m, v, name):
    R, C = w.shape
    _, Rp, Cp = parts.shape
    tr = min(R, 256)
    tp = tr if Rp == R else Rp

    def body(p_ref, w_ref, m_ref, v_ref, g_ref, d_ref, nm_ref, nv_ref):
        g = p_ref[0].astype(F32)
        for s in range(1, N_DEV):
            g = g + p_ref[s].astype(F32)
        g = g[:tr, :C]
        g_ref[...] = g
        d_ref[...], nm_ref[...], nv_ref[...] = _adamw(g, w_ref[...], m_ref[...], v_ref[...])

    spec = pl.BlockSpec((tr, C), lambda i: (i, 0))
    return pl.pallas_call(
        body, grid=(R // tr,), name=name,
        in_specs=[pl.BlockSpec((N_DEV, tp, Cp), lambda i: (0, i, 0)), spec, spec, spec],
        out_specs=[spec] * 4, out_shape=[SDS((R, C), F32)] * 4,
        compiler_params=_params(("arbitrary",)),
    )(parts, w, m, v)


SMALL_ROWS, SMALL_COLS = 16, 1024
SMALL_LAYOUT = (
    ("attn_norm_w", 0, 1, 1024, 1024), ("mlp_norm_w", 1, 1, 1024, 1024), ("q_lat_norm_w", 2, 1, 256, 256),
    ("kv_lat_norm_w", 3, 1, 256, 256), ("q_norm_w", 4, 1, 192, 192), ("k_norm_w", 5, 1, 192, 192),
    ("mla_out_norm_w", 6, 4, 128, 128), ("a_log", 10, 1, 128, 4), ("dt_bias", 11, 1, 128, 4),
    ("gdn_norm_w", 12, 1, 128, 128))


def _adamw_replicated(parts, ws, ms, vs):
    n = len(SMALL_LAYOUT)

    def body(*refs):
        p_ref = refs[0]
        w_refs, m_refs, v_refs = refs[1:1 + n], refs[1 + n:1 + 2 * n], refs[1 + 2 * n:1 + 3 * n]
        outs = refs[1 + 3 * n:]
        s = p_ref[0]
        for d in range(1, N_DEV):
            s = s + p_ref[d]
        for i, (_, r0, nr, _, pw) in enumerate(SMALL_LAYOUT):
            g = s[r0:r0 + nr, :pw]
            outs[i][...] = g
            outs[n + i][...], outs[2 * n + i][...], outs[3 * n + i][...] = _adamw(
                g, w_refs[i][...], m_refs[i][...], v_refs[i][...])

    res = pl.pallas_call(
        body, name="adamw_replicated",
        out_shape=[SDS(w.shape, F32) for w in ws] * 4,
        compiler_params=_params(),
    )(parts, *ws, *ms, *vs)
    return [res[k * n:(k + 1) * n] for k in range(4)]


COPIES_PER_ARRAY = N_DEV - 1


def _two_level_gather(srcs, outs, send_sems, recv_sems, local_sems=None, stage="all"):
    mx, my, mc = lax.axis_index("x"), lax.axis_index("y"), lax.axis_index("c")
    me, sibling = (mx, my, mc), (mx, my, 1 - mc)
    chips = [(1 - mx, my), (mx, 1 - my), (1 - mx, 1 - my)]
    arrays = range(len(srcs))

    def copy(a, k, block, to, src=None):
        px, py, pc = block
        slot = outs[a].at[4 * px + 2 * py + pc]
        sem = a * COPIES_PER_ARRAY + k
        return pltpu.make_async_remote_copy(
            src_ref=slot if src is None else src, dst_ref=slot,
            send_sem=send_sems.at[sem], recv_sem=recv_sems.at[sem], device_id=to, device_id_type=MESH_ID)

    mine = [] if local_sems is None else [
        pltpu.make_async_copy(srcs[a], outs[a].at[4 * mx + 2 * my + mc], local_sems.at[a]) for a in arrays]
    first = []
    for a in arrays:
        first.append(copy(a, 0, me, sibling, src=srcs[a]))
        first += [copy(a, 1 + j, me, (*chip, mc), src=srcs[a]) for j, chip in enumerate(chips)]
    if stage in ("all", "start"):
        for cp in mine + first:
            cp.start()
    if stage in ("all", "finish"):
        forwards = []
        for j, chip in enumerate(chips):
            for a in arrays:
                copy(a, 1 + j, (*chip, mc), me).wait_recv()
                fwd = copy(a, 4 + j, (*chip, mc), sibling)
                fwd.start()
                forwards.append(fwd)
        for a in arrays:
            copy(a, 0, sibling, me).wait_recv()
        for j, chip in enumerate(chips):
            for a in arrays:
                copy(a, 4 + j, (*chip, 1 - mc), me).wait_recv()
        for cp in first + forwards:
            cp.wait_send()
        for cp in mine:
            cp.wait()


def _comm_scratch(n):
    return [pltpu.SemaphoreType.DMA((n * COPIES_PER_ARRAY,)), pltpu.SemaphoreType.DMA((n * COPIES_PER_ARRAY,)),
            pltpu.SemaphoreType.DMA((n,))]


def _any_specs(n):
    return [pl.BlockSpec(memory_space=pl.ANY)] * n


def _gather_weights(shards):
    n = len(shards)

    def body(*refs):
        _two_level_gather(refs[:n], refs[n:2 * n], *refs[2 * n:])

    return pl.pallas_call(
        body, name="gather_weights",
        out_shape=[SDS((N_DEV,) + s.shape, s.dtype) for s in shards],
        in_specs=_any_specs(n), out_specs=_any_specs(n), scratch_shapes=_comm_scratch(n),
    )(*shards)


def _gather_small_grads(gs):
    n = len(gs)

    def body(*refs):
        g_refs, out_ref = refs[:n], refs[n]
        tile, send_sems, recv_sems = refs[n + 1:]
        tile[...] = jnp.zeros_like(tile)
        for (_, r0, nr, gw, _), g in zip(SMALL_LAYOUT, g_refs):
            tile[r0:r0 + nr, 0:gw] = g[...]
        me = 4 * lax.axis_index("x") + 2 * lax.axis_index("y") + lax.axis_index("c")
        out_ref[me] = tile[...]
        _two_level_gather([tile], [out_ref], send_sems, recv_sems)

    return pl.pallas_call(
        body, name="gather_small_grads",
        out_shape=SDS((N_DEV, SMALL_ROWS, SMALL_COLS), F32),
        in_specs=[pl.BlockSpec(memory_space=pltpu.VMEM)] * n,
        out_specs=pl.BlockSpec(memory_space=pltpu.VMEM),
        scratch_shapes=[pltpu.VMEM((SMALL_ROWS, SMALL_COLS), F32),
                        pltpu.SemaphoreType.DMA((COPIES_PER_ARRAY,)), pltpu.SemaphoreType.DMA((COPIES_PER_ARRAY,))],
    )(*gs)


def _exchange_grads(slabs):
    n = len(slabs)

    def body(*refs):
        _exchange(refs[:n], refs[n:2 * n], *refs[2 * n:])

    return pl.pallas_call(
        body, name="exchange_grads",
        out_shape=[SDS(s.shape, s.dtype) for s in slabs],
        in_specs=_any_specs(n), out_specs=_any_specs(n), scratch_shapes=_comm_scratch(n),
    )(*slabs)


class _Transfer:
    def __init__(self, kind, arrays):
        self.kind, self.arrays, self.n = kind, list(arrays), len(arrays)

    def out_shapes(self):
        if self.kind == "gather":
            return [SDS((N_DEV,) + a.shape, a.dtype) for a in self.arrays]
        return [SDS(a.shape, a.dtype) for a in self.arrays]

    def run(self, srcs, outs, sems, stage):
        fn = _two_level_gather if self.kind == "gather" else _exchange
        fn(srcs, outs, *sems, stage=stage)


def _call_beside(body, transfer, *, grid, in_specs, out_specs, out_shape, scratch_shapes, name, semantics, args):
    if transfer is None:
        res = pl.pallas_call(body, grid=grid, in_specs=in_specs, out_specs=out_specs, out_shape=out_shape,
                             scratch_shapes=scratch_shapes, name=name, compiler_params=_params(semantics))(*args)
        return list(res), []
    n_in, n_out, n_s, n = len(in_specs), len(out_specs), len(scratch_shapes), transfer.n

    def wrapped(*refs):
        ins, refs = refs[:n_in], refs[n_in:]
        t_in, refs = refs[:n], refs[n:]
        outs, refs = refs[:n_out], refs[n_out:]
        t_out, refs = refs[:n], refs[n:]
        scratch, sems = refs[:n_s], refs[n_s:]
        first = functools.reduce(jnp.logical_and, [pl.program_id(i) == 0 for i in range(len(grid))])
        last = functools.reduce(jnp.logical_and, [pl.program_id(i) == g - 1 for i, g in enumerate(grid)])

        @pl.when(first)
        def _():
            transfer.run(t_in, t_out, sems, "start")

        body(*ins, *outs, *scratch)

        @pl.when(last)
        def _():
            transfer.run(t_in, t_out, sems, "finish")

    res = pl.pallas_call(
        wrapped, grid=grid, in_specs=list(in_specs) + _any_specs(n), out_specs=list(out_specs) + _any_specs(n),
        out_shape=list(out_shape) + transfer.out_shapes(), scratch_shapes=list(scratch_shapes) + _comm_scratch(n),
        name=name, compiler_params=_params(semantics))(*args, *transfer.arrays)
    return list(res[:n_out]), list(res[n_out:])


EXCHANGE_FLIPS = ((0, 0, 1), (1, 0, 0), (0, 1, 0), (1, 1, 0), (1, 0, 1), (0, 1, 1), (1, 1, 1))


def _exchange(srcs, outs, send_sems, recv_sems, local_sems, stage="all"):
    mx, my, mc = lax.axis_index("x"), lax.axis_index("y"), lax.axis_index("c")
    arrays = range(len(srcs))
    copies = [pltpu.make_async_copy(srcs[a].at[4 * mx + 2 * my + mc], outs[a].at[N_DEV - 1], local_sems.at[a])
              for a in arrays]
    for k, (fx, fy, fc) in enumerate(EXCHANGE_FLIPS):
        px = 1 - mx if fx else mx
        py = 1 - my if fy else my
        pc = 1 - mc if fc else mc
        for a in arrays:
            sem = a * COPIES_PER_ARRAY + k
            copies.append(pltpu.make_async_remote_copy(
                src_ref=srcs[a].at[4 * px + 2 * py + pc], dst_ref=outs[a].at[k],
                send_sem=send_sems.at[sem], recv_sem=recv_sems.at[sem],
                device_id=(px, py, pc), device_id_type=MESH_ID))
    if stage in ("all", "start"):
        for cp in copies:
            cp.start()
    if stage in ("all", "finish"):
        for cp in copies:
            cp.wait()


def _w_in_to_padded(w):
    z = lambda n: jnp.zeros((w.shape[0], n), w.dtype)
    return jnp.concatenate([w[:, O_GQKV:O_GZ], w[:, O_GZ:O_GAB], w[:, O_QLAT:O_KVLAT], w[:, O_KVLAT:O_KPE],
                            w[:, O_KPE:O_GQKV], z(P_GAB - P_KPE - ROPE), w[:, O_GAB:O_END],
                            z(P_WIDTH - P_GAB - (O_END - O_GAB))], axis=1)


def _w_in_from_padded(wp):
    return jnp.concatenate([wp[:, P_QLAT:P_QLAT + 256], wp[:, P_KVLAT:P_KVLAT + 256], wp[:, P_KPE:P_KPE + ROPE],
                            wp[:, P_GQKV:P_GZ], wp[:, P_GZ:P_QLAT], wp[:, P_GAB:P_GAB + (O_END - O_GAB)]], axis=1)


def _w_uq_to_headsplit(w):
    w3 = w.reshape(w.shape[0], MLA_HEADS, QK_DIM)
    return jnp.concatenate([w3[:, :, :NOPE].reshape(w.shape[0], -1), w3[:, :, NOPE:].reshape(w.shape[0], -1)], axis=1)


def _w_uq_from_headsplit(wp):
    n = wp[:, :MLA_HEADS * NOPE].reshape(wp.shape[0], MLA_HEADS, NOPE)
    p = wp[:, MLA_HEADS * NOPE:].reshape(wp.shape[0], MLA_HEADS, ROPE)
    return jnp.concatenate([n, p], axis=2).reshape(wp.shape[0], -1)


def _lane_vec(v4):
    return jnp.pad(v4.reshape(1, -1), ((0, 0), (0, LANES - v4.shape[-1])))


def _local_step(x, positions, target, attn_norm_w, w_in, q_lat_norm_w, w_uq, kv_lat_norm_w, w_ukv, q_norm_w,
                k_norm_w, mla_out_norm_w, conv_w, a_log, dt_bias, gdn_norm_w, w_out, mlp_norm_w, w_up, w_down,
                late_shards=None, exchange=False):
    B, S, D = x.shape
    T = B * S
    x2 = x.reshape(T, D)
    t2 = target.reshape(T, D)
    half = ROPE // 2
    inv_freq = ROPE_THETA ** (-jnp.arange(half, dtype=F32) / half)
    ang = positions.reshape(T, 1).astype(F32) * inv_freq
    cosf = jnp.concatenate([jnp.cos(ang)] * 2, axis=-1)
    sinf = jnp.concatenate([jnp.sin(ang)] * 2, axis=-1)
    w_in_p = _w_in_to_padded(w_in)
    w_uq_p = _w_uq_to_headsplit(w_uq)
    alog_l, dt_l = _lane_vec(a_log), _lane_vec(dt_bias)
    w_an, w_qln, w_kvln, qnw, knw, w_mn, gdn_w = (
        attn_norm_w, q_lat_norm_w, kv_lat_norm_w, q_norm_w, k_norm_w, mlp_norm_w, gdn_norm_w)

    proj, xn = _in_proj(x2, w_an, w_in_p)
    q4, k4, v4 = _mla_pre(proj, cosf, sinf, w_qln, w_kvln, w_uq_p, w_ukv, qnw, knw)
    gather = None if late_shards is None else _Transfer("gather", late_shards)
    (o_mla, lse), late = _attn_fwd(q4, k4, v4, B, S, gather)
    if late:
        w_out, w_up, w_down = late[0].reshape(-1, D), late[1], late[2].reshape(-1, D)
    qg, kg, vg, gates = _gdn_pre(proj, conv_w, alog_l, dt_l, S)
    o_gdn, states, ainv, u4, w4 = _gdn_fwd(qg, kg, vg, gates, B, S)
    h2, mix = _mix_out(o_mla, o_gdn, proj, x2, mla_out_norm_w, gdn_w, w_out)
    up, hn, dy, sq = _mlp_fwd(h2, w_mn, w_up, w_down, t2)
    loss = (0.5 / D) * jnp.sum(sq[:, 0, 0])

    dh, dhb, dup, act, dyb, d_mlp_norm = _mlp_bwd(dy, up, h2, w_mn, w_up, w_down)
    g_w_down = _wgrad(act, dyb, "wgrad_down")
    g_w_up = _wgrad(hn, dup, "wgrad_up", column_shards=True)
    do_mla, do_gdn, dz, d_mla_w, d_gdn_w = _mix_bwd(dhb, o_mla, o_gdn, proj, mla_out_norm_w, gdn_w, w_out)
    g_w_out = _wgrad(mix, dhb, "wgrad_out")
    first = ("w_up", "w_down", "w_out")
    second = ("w_uq", "w_ukv")
    mats = dict(w_up=g_w_up, w_down=g_w_down, w_out=g_w_out)
    send = _Transfer("exchange", [_slabs(n, mats[n]) for n in first]) if exchange else None
    (dq4, dk4, dv4), got = _attn_bwd(q4, k4, v4, do_mla, o_mla, lse, B, S, send)
    mats.update(zip(first, got))
    dql, dkvl, dkpe, dqraw, dkvraw, qn, kvn, d_wqln, d_wkvln, d_qnw, d_knw = _mla_pre_bwd(
        proj, cosf, sinf, w_qln, w_kvln, w_uq_p, w_ukv, qnw, knw, dq4, dk4, dv4)
    mats.update(w_uq=_wgrad(qn, dqraw, "wgrad_uq"), w_ukv=_wgrad(kvn, dkvraw, "wgrad_ukv"))
    send = _Transfer("exchange", [_slabs(n, mats[n]) for n in second]) if exchange else None
    (dqg, dkg, dvg, dgb4), got = _gdn_bwd(qg, kg, vg, gates, states, ainv, u4, w4, do_gdn, B, S, send)
    mats.update(zip(second, got))
    dc, dgab, g_conv, d_alog, d_dt = _gdn_pre_bwd(proj, conv_w, alog_l, dt_l, dqg, dkg, dvg, dgb4, S)
    dgqkv = _conv_bwd_input(dc, conv_w, S)
    grad_x2, dproj, d_attn_norm = _in_proj_bwd(dgqkv, dz, dql, dkvl, dkpe, dgab, w_in_p, dh, x2, w_an)
    mats.update(w_in=_wgrad(xn, dproj, "wgrad_in"), conv_w=g_conv)
    if exchange:
        last = ("w_in", "conv_w")
        mats.update(zip(last, _exchange_grads([_slabs(n, mats[n]) for n in last])))
    small = dict(attn_norm_w=d_attn_norm, mlp_norm_w=d_mlp_norm, q_lat_norm_w=d_wqln, kv_lat_norm_w=d_wkvln,
                 q_norm_w=d_qnw, k_norm_w=d_knw, mla_out_norm_w=d_mla_w, a_log=d_alog, dt_bias=d_dt,
                 gdn_norm_w=d_gdn_w)
    return loss, grad_x2.reshape(B, S, D), mats, [small[n] for n, *_ in SMALL_LAYOUT]


BIG = ("w_in", "w_uq", "w_ukv", "conv_w", "w_out", "w_up", "w_down")
ALL_W = ("attn_norm_w", "w_in", "q_lat_norm_w", "w_uq", "kv_lat_norm_w", "w_ukv", "q_norm_w", "k_norm_w",
         "mla_out_norm_w", "conv_w", "a_log", "dt_bias", "gdn_norm_w", "w_out", "mlp_norm_w", "w_up", "w_down")
WIRE_SHAPE = {"w_in": (1024, 384), "w_uq": (256, 128), "conv_w": (16, 256)}


def _pad2(a, rows, cols):
    return jnp.pad(a, [(0, 0)] * (a.ndim - 2) + [(0, rows - a.shape[-2]), (0, cols - a.shape[-1])])


def _cols_to_full(stack, cols):
    return jnp.moveaxis(stack[:, :, :cols], 0, 1).reshape(stack.shape[1], N_DEV * cols)


def _full_to_cols(full, wire_cols):
    r, n = full.shape
    return _pad2(jnp.moveaxis(full.reshape(r, N_DEV, n // N_DEV), 1, 0), r, wire_cols)


def _slabs(name, g):
    if name == "w_in":
        return _full_to_cols(_w_in_from_padded(g), WIRE_SHAPE["w_in"][1])
    if name == "w_uq":
        return _full_to_cols(_w_uq_from_headsplit(g), WIRE_SHAPE["w_uq"][1])
    if name == "w_ukv":
        return _full_to_cols(g, g.shape[1] // N_DEV)
    if name == "conv_w":
        return _pad2(_full_to_cols(g.astype(WIRE_DTYPE), g.shape[1] // N_DEV), *WIRE_SHAPE["conv_w"])
    if name == "w_up":
        return g
    return g.reshape(N_DEV, -1, g.shape[-1])


def kernel(x, positions, attn_norm_w, w_in, q_lat_norm_w, w_uq, kv_lat_norm_w, w_ukv, q_norm_w, k_norm_w, mla_out_norm_w, conv_w, a_log, dt_bias, gdn_norm_w, w_out, mlp_norm_w, w_up, w_down, loss_target, m_attn_norm_w, m_w_in, m_q_lat_norm_w, m_w_uq, m_kv_lat_norm_w, m_w_ukv, m_q_norm_w, m_k_norm_w, m_mla_out_norm_w, m_conv_w, m_a_log, m_dt_bias, m_gdn_norm_w, m_w_out, m_mlp_norm_w, m_w_up, m_w_down, v_attn_norm_w, v_w_in, v_q_lat_norm_w, v_w_uq, v_kv_lat_norm_w, v_w_ukv, v_q_norm_w, v_k_norm_w, v_mla_out_norm_w, v_conv_w, v_a_log, v_dt_bias, v_gdn_norm_w, v_w_out, v_mlp_norm_w, v_w_up, v_w_down):
    env = dict(locals())
    W = {n: env[n][0] for n in ALL_W}
    Mo = {n: env["m_" + n][0] for n in ALL_W}
    Vo = {n: env["v_" + n][0] for n in ALL_W}

    two_d = lambda a: a.reshape(1, -1) if a.ndim == 1 else a
    D = x.shape[-1]

    s_in, s_uq, s_ukv, s_conv = _gather_weights([
        _pad2(W["w_in"].astype(WIRE_DTYPE), *WIRE_SHAPE["w_in"]),
        _pad2(W["w_uq"].astype(WIRE_DTYPE), *WIRE_SHAPE["w_uq"]),
        W["w_ukv"].astype(WIRE_DTYPE), _pad2(W["conv_w"], *WIRE_SHAPE["conv_w"])])
    late = [W["w_out"].astype(WIRE_DTYPE), W["w_up"].astype(WIRE_DTYPE), W["w_down"].astype(WIRE_DTYPE)]

    loss, grad_x, parts, gs = _local_step(
        x, positions, loss_target, two_d(W["attn_norm_w"]), _cols_to_full(s_in, W["w_in"].shape[1]),
        two_d(W["q_lat_norm_w"]), _cols_to_full(s_uq, W["w_uq"].shape[1]), two_d(W["kv_lat_norm_w"]),
        _cols_to_full(s_ukv, W["w_ukv"].shape[1]), two_d(W["q_norm_w"]), two_d(W["k_norm_w"]),
        W["mla_out_norm_w"], _cols_to_full(s_conv[:, :CONV_W], W["conv_w"].shape[1]), two_d(W["a_log"]),
        two_d(W["dt_bias"]), two_d(W["gdn_norm_w"]), None, two_d(W["mlp_norm_w"]), None, None,
        late_shards=late, exchange=True)
    loss = lax.psum(loss, ("x", "y", "c"))
    done = {n: _reduce_adamw(parts[n], W[n], Mo[n], Vo[n], "adamw_" + n) for n in BIG}
    names = [n for n, *_ in SMALL_LAYOUT]
    small = _adamw_replicated(_gather_small_grads(gs), [two_d(W[n]) for n in names], [two_d(Mo[n]) for n in names],
                              [two_d(Vo[n]) for n in names])
    for i, n in enumerate(names):
        done[n] = [small[kind][i] for kind in range(4)]
    res = [done[n][kind].reshape(env[n].shape) for kind in range(4) for n in ALL_W]
    return (loss, grad_x, *res)
```

```python
import functools

import jax
import jax.numpy as jnp
from jax import lax
from jax.experimental import pallas as pl
from jax.experimental.pallas import tpu as pltpu

F32 = jnp.float32
MXU_DTYPE = jnp.bfloat16
WIRE_DTYPE = jnp.bfloat16
SDS = jax.ShapeDtypeStruct
HIGHEST = lax.Precision.HIGHEST
MESH_ID = pl.DeviceIdType.MESH

D_MODEL = 1024
MLA_HEADS = 4
Q_LORA = 256
KV_LORA = 256
NOPE = 128
ROPE = 64
QK_DIM = NOPE + ROPE
V_DIM = 128
ROPE_THETA = 10000.0
GDN_HEADS = 4
GDN_DIM = 128
GDN_WIDTH = GDN_HEADS * GDN_DIM
CONV_W = 4
CHUNK = 64
D_FF = 4 * D_MODEL
EPS = 1e-6
ATT_SCALE = QK_DIM ** -0.5
GDN_QSCALE = GDN_DIM ** -0.5
N_DEV = 8
ATTN_BLOCK = 512
ATTN_CHAINS = 2

ADAM_LR = 0.001
ADAM_B1 = 0.9
ADAM_B2 = 0.999
ADAM_EPS = 1e-08
ADAM_WD = 0.01
ADAM_STEP = 10

LANES = 128
SUBLANES = 8
VMEM_LIMIT = 56 * 1024 * 1024

P_GQKV, P_GZ, P_QLAT, P_KVLAT, P_KPE, P_GAB = 0, 1536, 2048, 2304, 2560, 2688
P_WIDTH = 2816
O_QLAT, O_KVLAT, O_KPE, O_GQKV, O_GZ, O_GAB, O_END = 0, 256, 512, 576, 2112, 2624, 2632


def _params(sem=None, vmem=VMEM_LIMIT):
    kw = dict(vmem_limit_bytes=vmem)
    if sem is not None:
        kw["dimension_semantics"] = sem
    return pltpu.CompilerParams(**kw)


def _mm(a, b):
    return jnp.dot(a.astype(MXU_DTYPE), b.astype(MXU_DTYPE), preferred_element_type=F32)


def _mm_nt(a, b):
    return lax.dot_general(a.astype(MXU_DTYPE), b.astype(MXU_DTYPE), (((1,), (1,)), ((), ())),
                           preferred_element_type=F32)


def _mm_tn(a, b):
    return lax.dot_general(a.astype(MXU_DTYPE), b.astype(MXU_DTYPE), (((0,), (0,)), ((), ())),
                           preferred_element_type=F32)


def _split(a):
    hi = a.astype(MXU_DTYPE)
    return hi, (a - hi.astype(F32)).astype(MXU_DTYPE)


def _mm_split(a, b):
    (ah, al), (bh, bl) = a, b
    dot = lambda x, y: jnp.dot(x, y, preferred_element_type=F32)
    if MXU_DTYPE == F32:
        return dot(ah, bh)
    return dot(ah, bh) + dot(ah, bl) + dot(al, bh)


def _mm_exact(a, b):
    return _mm_split(_split(a), _split(b))


def _rms(x, w):
    r = lax.rsqrt(jnp.mean(x * x, axis=-1, keepdims=True) + EPS)
    return x * r * w, r


def _rms_bwd(dy, x, w, r):
    xh = x * r
    dyw = dy * w
    dx = r * (dyw - xh * jnp.mean(dyw * xh, axis=-1, keepdims=True))
    dw = jnp.sum(dy * xh, axis=0, keepdims=True)
    return dx, dw


def _l2n_bwd(dy, x, scale):
    r = lax.rsqrt(jnp.sum(x * x, axis=-1, keepdims=True) + EPS)
    xh = x * r
    return (scale * r) * (dy - xh * jnp.sum(dy * xh, axis=-1, keepdims=True))


def _rot(t):
    return jnp.concatenate([-t[:, ROPE // 2:], t[:, :ROPE // 2]], axis=-1)


def _rot_t(t):
    return jnp.concatenate([t[:, ROPE // 2:], -t[:, :ROPE // 2]], axis=-1)


def _rope(t, cos, sin):
    return t * cos + _rot(t) * sin


def _rope_bwd(d, cos, sin):
    return d * cos + _rot_t(d * sin)


def _sigmoid(x):
    return jax.nn.sigmoid(x)


def _shift_down(x, halo, j):
    if j == 0:
        return x
    xr = pltpu.roll(x, j, 0)
    hr = pltpu.roll(halo, j, 0)
    row = lax.broadcasted_iota(jnp.int32, halo.shape, 0)
    top = jnp.where(row < j, hr, xr[:SUBLANES])
    return jnp.concatenate([top, xr[SUBLANES:]], axis=0)


def _shift_up(x, nxt, j):
    if j == 0:
        return x
    n = x.shape[0]
    xr = pltpu.roll(x, n - j, 0)
    nr = pltpu.roll(nxt, SUBLANES - j, 0)
    row = lax.broadcasted_iota(jnp.int32, nxt.shape, 0)
    bot = jnp.where(row >= SUBLANES - j, nr, xr[n - SUBLANES:])
    return jnp.concatenate([xr[:n - SUBLANES], bot], axis=0)


def _chunk_cumsum(y, row_in_chunk):
    s = 1
    while s < CHUNK:
        y = y + jnp.where(row_in_chunk >= s, pltpu.roll(y, s, 0), 0.0)
        s *= 2
    return y


def _chunk_rev_cumsum(y, row_in_chunk):
    n = y.shape[0]
    s = 1
    while s < CHUNK:
        y = y + jnp.where(row_in_chunk + s < CHUNK, pltpu.roll(y, n - s, 0), 0.0)
        s *= 2
    return y


def _lockstep(generators):
    alive = list(generators)
    while alive:
        nxt = []
        for g in alive:
            try:
                next(g)
                nxt.append(g)
            except StopIteration:
                pass
        alive = nxt


def _pick_lane(tile, lane, idx):
    return jnp.sum(jnp.where(lane == idx, tile, 0.0), axis=-1, keepdims=True)


def _divisor_tile(n, cap, unit=LANES):
    best = unit
    t = unit
    while t <= min(n, cap):
        if n % t == 0:
            best = t
        t += unit
    return n if n <= cap else best


def _in_proj(x2, w_an, w_in_p):
    T, D = x2.shape
    N = w_in_p.shape[1]
    tm = min(512, T)

    def body(x_ref, wn_ref, w_ref, proj_ref, xn_ref):
        xn, _ = _rms(x_ref[...], wn_ref[...])
        xn = xn.astype(MXU_DTYPE)
        xn_ref[...] = xn
        proj_ref[...] = jnp.dot(xn, w_ref[...], preferred_element_type=F32)

    return pl.pallas_call(
        body, grid=(T // tm,), name="in_proj",
        in_specs=[pl.BlockSpec((tm, D), lambda i: (i, 0)), pl.BlockSpec((1, D), lambda i: (0, 0)),
                  pl.BlockSpec((D, N), lambda i: (0, 0))],
        out_specs=[pl.BlockSpec((tm, N), lambda i: (i, 0)), pl.BlockSpec((tm, D), lambda i: (i, 0))],
        out_shape=[SDS((T, N), F32), SDS((T, D), MXU_DTYPE)],
        compiler_params=_params(("arbitrary",)),
    )(x2, w_an, w_in_p)


def _mla_pre(proj, cosf, sinf, w_qln, w_kvln, w_uq_p, w_ukv, qnw, knw):
    T = proj.shape[0]
    tm = min(256, T)
    H = MLA_HEADS

    def body(ql_ref, kvl_ref, kpe_ref, cos_ref, sin_ref, wq_ref, wkv_ref, uq_ref, ukv_ref, qnw_ref, knw_ref,
             q_out, k_out, v_out):
        cos, sin = cos_ref[...], sin_ref[...]
        qnw_, knw_ = qnw_ref[...], knw_ref[...]
        qn, _ = _rms(ql_ref[...], wq_ref[...])
        kvn, _ = _rms(kvl_ref[...], wkv_ref[...])
        qraw = _mm(qn, uq_ref[...])
        kvraw = _mm(kvn, ukv_ref[...])
        kpe = _rope(_rms(kpe_ref[...][:, :ROPE], knw_[:, NOPE:])[0], cos, sin)
        for h in range(H):
            qn_h = _rms(qraw[:, h * NOPE:(h + 1) * NOPE], qnw_[:, :NOPE])[0]
            qp_h = _rope(_rms(qraw[:, H * NOPE + h * ROPE:H * NOPE + (h + 1) * ROPE], qnw_[:, NOPE:])[0], cos, sin)
            q_out[h] = (jnp.concatenate([qn_h, qp_h], axis=-1) * ATT_SCALE).astype(MXU_DTYPE)
            kn_h = _rms(kvraw[:, h * 256:h * 256 + NOPE], knw_[:, :NOPE])[0]
            k_out[h] = jnp.concatenate([kn_h, kpe], axis=-1).astype(MXU_DTYPE)
            v_out[h] = kvraw[:, h * 256 + NOPE:(h + 1) * 256].astype(MXU_DTYPE)

    full = lambda a: pl.BlockSpec(a.shape, lambda i: (0,) * a.ndim)
    return pl.pallas_call(
        body, grid=(T // tm,), name="mla_pre",
        in_specs=[pl.BlockSpec((tm, 256), lambda i: (i, P_QLAT // 256)),
                  pl.BlockSpec((tm, 256), lambda i: (i, P_KVLAT // 256)),
                  pl.BlockSpec((tm, 128), lambda i: (i, P_KPE // 128)),
                  pl.BlockSpec((tm, ROPE), lambda i: (i, 0)), pl.BlockSpec((tm, ROPE), lambda i: (i, 0)),
                  full(w_qln), full(w_kvln), full(w_uq_p), full(w_ukv), full(qnw), full(knw)],
        out_specs=[pl.BlockSpec((H, tm, QK_DIM), lambda i: (0, i, 0)),
                   pl.BlockSpec((H, tm, QK_DIM), lambda i: (0, i, 0)),
                   pl.BlockSpec((H, tm, V_DIM), lambda i: (0, i, 0))],
        out_shape=[SDS((H, T, QK_DIM), MXU_DTYPE), SDS((H, T, QK_DIM), MXU_DTYPE), SDS((H, T, V_DIM), MXU_DTYPE)],
        compiler_params=_params(("arbitrary",)),
    )(proj, proj, proj, cosf, sinf, w_qln, w_kvln, w_uq_p, w_ukv, qnw, knw)


def _attn_fwd(q4, k4, v4, B, S, transfer=None):
    H = MLA_HEADS
    bq = min(ATTN_BLOCK, S)
    nq = S // bq
    rows = bq // ATTN_CHAINS

    def body(q_ref, k_ref, v_ref, o_ref, lse_ref):
        col = lax.broadcasted_iota(jnp.int32, (rows, bq), 1)
        row = lax.broadcasted_iota(jnp.int32, (rows, bq), 0)

        def q_step(qi, carry):
            qs = pl.multiple_of(qi * bq, bq)
            qsub = [q_ref[0, pl.ds(qs + j * rows, rows), :] for j in range(ATTN_CHAINS)]

            def k_block(ks, cs, diagonal):
                k = k_ref[0, pl.ds(ks, bq), :]
                v = v_ref[0, pl.ds(ks, bq), :]
                out = [None] * ATTN_CHAINS

                def chain(j):
                    m, l, acc = cs[j]
                    s = _mm_nt(qsub[j], k)
                    yield
                    if diagonal:
                        s = jnp.where(col <= row + j * rows, s, -jnp.inf)
                    m_new = jnp.maximum(m, jnp.max(s, axis=-1, keepdims=True))
                    p = jnp.exp(s - m_new)
                    a = jnp.exp(m - m_new)
                    l_new = a * l + jnp.sum(p, axis=-1, keepdims=True)
                    yield
                    out[j] = (m_new, l_new, a * acc + _mm(p, v))

                _lockstep([chain(j) for j in range(ATTN_CHAINS)])
                return tuple(out)

            init = tuple((jnp.full((rows, 1), -jnp.inf, F32), jnp.zeros((rows, 1), F32),
                          jnp.zeros((rows, V_DIM), F32)) for _ in range(ATTN_CHAINS))
            cs = lax.fori_loop(0, qi, lambda kj, c: k_block(pl.multiple_of(kj * bq, bq), c, False), init)
            for j, (m, l, acc) in enumerate(k_block(qs, cs, True)):
                o_ref[0, pl.ds(qs + j * rows, rows), :] = acc / l
                lse_ref[0, pl.ds(qs + j * rows, rows), :] = m + jnp.log(l)
            return carry

        lax.fori_loop(0, nq, q_step, 0)

    spec = lambda d: pl.BlockSpec((1, S, d), lambda h, b: (h, b, 0))
    return _call_beside(
        body, transfer, grid=(H, B), name="attn_fwd",
        in_specs=[spec(QK_DIM), spec(QK_DIM), spec(V_DIM)],
        out_specs=[spec(V_DIM), spec(1)],
        out_shape=[SDS((H, B * S, V_DIM), F32), SDS((H, B * S, 1), F32)],
        scratch_shapes=[], semantics=("arbitrary", "arbitrary"), args=(q4, k4, v4))


def _conv_taps(u, halo, w):
    sh = [_shift_down(u, halo, j) for j in range(CONV_W)]
    c = w[0:1] * sh[3] + w[1:2] * sh[2] + w[2:3] * sh[1] + w[3:4] * sh[0]
    return c, sh


def _gate_values(gab, alog_l, dt_l, lane):
    g = -jnp.exp(alog_l) * jax.nn.softplus(gab + dt_l)
    g = jnp.where(lane < GDN_HEADS, g, 0.0)
    beta = jnp.where((lane >= GDN_HEADS) & (lane < 2 * GDN_HEADS), _sigmoid(gab), 0.0)
    return g, beta


def _gdn_pre(proj, conv_w, alog_l, dt_l, S):
    T = proj.shape[0]
    tm = min(256, T)
    tiles_per_seq = S // tm
    C3 = 3 * GDN_WIDTH
    H = GDN_HEADS

    def body(u_ref, halo_ref, gab_ref, w_ref, alog_ref, dt_ref, q_out, k_out, v_out, gates_out):
        i = pl.program_id(0)
        halo = jnp.where(i % tiles_per_seq == 0, 0.0, halo_ref[...])
        c, _ = _conv_taps(u_ref[...], halo, w_ref[...])
        a = c * _sigmoid(c)
        for h in range(H):
            xq = a[:, h * GDN_DIM:(h + 1) * GDN_DIM]
            xk = a[:, GDN_WIDTH + h * GDN_DIM:GDN_WIDTH + (h + 1) * GDN_DIM]
            q_out[h] = xq * lax.rsqrt(jnp.sum(xq * xq, axis=-1, keepdims=True) + EPS) * GDN_QSCALE
            k_out[h] = xk * lax.rsqrt(jnp.sum(xk * xk, axis=-1, keepdims=True) + EPS)
            v_out[h] = a[:, 2 * GDN_WIDTH + h * GDN_DIM:2 * GDN_WIDTH + (h + 1) * GDN_DIM]
        lane = lax.broadcasted_iota(jnp.int32, (tm, LANES), 1)
        ric = lax.broadcasted_iota(jnp.int32, (tm, LANES), 0) % CHUNK
        g, beta = _gate_values(gab_ref[...], alog_ref[...], dt_ref[...], lane)
        gates_out[...] = _chunk_cumsum(g, ric) + beta

    hspec = pl.BlockSpec((H, tm, GDN_DIM), lambda i: (0, i, 0))
    return pl.pallas_call(
        body, grid=(T // tm,), name="gdn_pre",
        in_specs=[pl.BlockSpec((tm, C3), lambda i: (i, 0)),
                  pl.BlockSpec((SUBLANES, C3), lambda i: (jnp.maximum(i * (tm // SUBLANES) - 1, 0), 0)),
                  pl.BlockSpec((tm, LANES), lambda i: (i, P_GAB // LANES)),
                  pl.BlockSpec((CONV_W, C3), lambda i: (0, 0)),
                  pl.BlockSpec((1, LANES), lambda i: (0, 0)), pl.BlockSpec((1, LANES), lambda i: (0, 0))],
        out_specs=[hspec, hspec, hspec, pl.BlockSpec((tm, LANES), lambda i: (i, 0))],
        out_shape=[SDS((H, T, GDN_DIM), F32)] * 3 + [SDS((T, LANES), F32)],
        compiler_params=_params(("arbitrary",)),
    )(proj, proj, proj, conv_w, alog_l, dt_l)


def _unit_lower_inverses(Ls, eye):
    Ps = [eye - L for L in Ls]
    Ms = [_split(-L) for L in Ls]
    for _ in range(5):
        sq = [_mm_split(m, m) for m in Ms]
        Ms = [_split(s) for s in sq]
        Ps = [p + _mm_split(_split(p), m) for p, m in zip(Ps, Ms)]
    return Ps


def _chunk_decays(gt, lane, h, ri, ci, rcol):
    Gc = _pick_lane(gt, lane, h)
    bt = _pick_lane(gt, lane, h + GDN_HEADS)
    Gb = jnp.broadcast_to(Gc, (CHUNK, CHUNK))
    Gam = jnp.where(ri >= ci, jnp.exp(Gb - Gb.T), 0.0)
    Gl = jnp.sum(jnp.where(rcol == CHUNK - 1, Gc, 0.0), axis=0, keepdims=True)
    return Gc, bt, Gam, jnp.exp(Gc), jnp.exp(Gl - Gc), jnp.exp(Gl)


GDN_UNROLL = 4


def _gdn_fwd(qg, kg, vg, gates, B, S):
    H, D, C = GDN_HEADS, GDN_DIM, CHUNK
    NC = S // C
    U = GDN_UNROLL if NC % GDN_UNROLL == 0 else 1

    def body(q_ref, k_ref, v_ref, g_ref, o_ref, st_ref, ai_ref, u_ref, w_ref, q2_s, au_s, bc_s, w2_s, el_s):
        h = pl.program_id(0)
        lane = lax.broadcasted_iota(jnp.int32, (C, LANES), 1)
        ri = lax.broadcasted_iota(jnp.int32, (C, C), 0)
        ci = lax.broadcasted_iota(jnp.int32, (C, C), 1)
        rcol = lax.broadcasted_iota(jnp.int32, (C, 1), 0)
        eye = (ri == ci).astype(F32)

        def group(gi, c):
            ns = [gi * U + j for j in range(U)]
            css = [pl.multiple_of(n * C, C) for n in ns]
            qs = [q_ref[0, pl.ds(cs, C), :] for cs in css]
            ks = [k_ref[0, pl.ds(cs, C), :] for cs in css]
            vs = [v_ref[0, pl.ds(cs, C), :] for cs in css]
            decs = [_chunk_decays(g_ref[pl.ds(cs, C), :], lane, h, ri, ci, rcol) for cs in css]
            qks = [_mm_nt(jnp.concatenate([q, k], axis=0), k) for q, k in zip(qs, ks)]
            ainvs = _unit_lower_inverses(
                [jnp.where(ri > ci, d[1] * qk[C:] * d[2], 0.0) for qk, d in zip(qks, decs)], eye)
            sols = [_mm_exact(a, jnp.concatenate([v * d[1], k * (d[1] * d[3])], axis=-1))
                    for a, k, v, d in zip(ainvs, ks, vs, decs)]
            atuw = [_mm(qk[:C] * d[2], sol) for qk, d, sol in zip(qks, decs, sols)]
            kduw = [_mm_tn(k * d[4], sol) for k, d, sol in zip(ks, decs, sols)]
            for n, cs, q, a, sol, au, ku, (Gc, bt, Gam, e, f, eL) in zip(ns, css, qs, ainvs, sols, atuw, kduw, decs):
                u_ref[0, pl.ds(cs, C), :] = sol[:, :D]
                w_ref[0, pl.ds(cs, C), :] = sol[:, D:]
                au_s[pl.ds(cs, C), :] = au[:, :D]
                q2_s[pl.ds(cs, C), :] = q * e - au[:, D:]
                bc_s[n] = ku[:, :D]
                w2_s[n] = ku[:, D:]
                el_s[n] = jnp.broadcast_to(eL, (SUBLANES, LANES))
                ai_ref[0, n] = a
            return c

        lax.fori_loop(0, NC // U, group, 0)

        def step(n, S_):
            cs = pl.multiple_of(n * C, C)
            o_ref[0, pl.ds(cs, C), :] = _mm(q2_s[pl.ds(cs, C), :], S_) + au_s[pl.ds(cs, C), :]
            st_ref[0, n] = S_
            return S_ * el_s[n, 0:1, :] + bc_s[n] - _mm(w2_s[n], S_)

        lax.fori_loop(0, NC, step, jnp.zeros((D, D), F32))

    spec = pl.BlockSpec((1, S, D), lambda h, b: (h, b, 0))
    return pl.pallas_call(
        body, grid=(H, B), name="gdn_fwd",
        in_specs=[spec, spec, spec, pl.BlockSpec((S, LANES), lambda h, b: (b, 0))],
        out_specs=[spec, pl.BlockSpec((1, NC, D, D), lambda h, b: (h, b, 0, 0)),
                   pl.BlockSpec((1, NC, C, C), lambda h, b: (h, b, 0, 0)), spec, spec],
        out_shape=[SDS((H, B * S, D), F32), SDS((H, B * NC, D, D), F32), SDS((H, B * NC, C, C), F32),
                   SDS((H, B * S, D), F32), SDS((H, B * S, D), F32)],
        scratch_shapes=[pltpu.VMEM((S, D), F32), pltpu.VMEM((S, D), F32), pltpu.VMEM((NC, D, D), F32),
                        pltpu.VMEM((NC, D, D), F32), pltpu.VMEM((NC, SUBLANES, LANES), F32)],
        compiler_params=_params(("arbitrary", "arbitrary")),
    )(qg, kg, vg, gates)


def _mix_out(o_mla, o_gdn, proj, x2, mla_w, gdn_w, w_out):
    T, D = x2.shape
    tm = min(512, T)
    H = MLA_HEADS

    def body(om_ref, og_ref, z_ref, x_ref, mw_ref, gw_ref, w_ref, h_ref, mix_ref):
        z = z_ref[...]
        parts = [_rms(om_ref[h], mw_ref[h:h + 1, :])[0] for h in range(H)]
        for h in range(GDN_HEADS):
            zh = z[:, h * GDN_DIM:(h + 1) * GDN_DIM]
            parts.append(_rms(og_ref[h], gw_ref[...])[0] * (zh * _sigmoid(zh)))
        mix = jnp.concatenate(parts, axis=-1).astype(MXU_DTYPE)
        mix_ref[...] = mix
        h_ref[...] = x_ref[...] + jnp.dot(mix, w_ref[...], preferred_element_type=F32)

    hspec = pl.BlockSpec((H, tm, V_DIM), lambda i: (0, i, 0))
    return pl.pallas_call(
        body, grid=(T // tm,), name="mix_out",
        in_specs=[hspec, hspec, pl.BlockSpec((tm, GDN_WIDTH), lambda i: (i, P_GZ // GDN_WIDTH)),
                  pl.BlockSpec((tm, D), lambda i: (i, 0)),
                  pl.BlockSpec((H, V_DIM), lambda i: (0, 0)), pl.BlockSpec((1, GDN_DIM), lambda i: (0, 0)),
                  pl.BlockSpec((D, D), lambda i: (0, 0))],
        out_specs=[pl.BlockSpec((tm, D), lambda i: (i, 0)), pl.BlockSpec((tm, D), lambda i: (i, 0))],
        out_shape=[SDS((T, D), F32), SDS((T, D), MXU_DTYPE)],
        compiler_params=_params(("arbitrary",)),
    )(o_mla, o_gdn, proj, x2, mla_w, gdn_w, w_out)


def _mlp_fwd(h2, w_mn, w_up, w_down, target):
    T, D = h2.shape
    nf, _, tf = w_up.shape
    F = nf * tf
    tm = min(512, T)

    def body(h_ref, wn_ref, up_w, down_w, t_ref, up_ref, hn_ref, dy_ref, loss_ref, y_acc):
        j = pl.program_id(1)

        @pl.when(j == 0)
        def _():
            hn_ref[...] = _rms(h_ref[...], wn_ref[...])[0].astype(MXU_DTYPE)
            y_acc[...] = h_ref[...]

        up = jnp.dot(hn_ref[...], up_w[0], preferred_element_type=F32)
        up_ref[...] = up
        r = jnp.maximum(up, 0.0)
        y_acc[...] += _mm(r * r, down_w[...])

        @pl.when(j == nf - 1)
        def _():
            err = y_acc[...] - t_ref[...]
            dy_ref[...] = err / D
            loss_ref[...] = jnp.full((1, SUBLANES, LANES), jnp.sum(err * err), F32)

    return pl.pallas_call(
        body, grid=(T // tm, nf), name="mlp_fwd",
        in_specs=[pl.BlockSpec((tm, D), lambda i, j: (i, 0)), pl.BlockSpec((1, D), lambda i, j: (0, 0)),
                  pl.BlockSpec((1, D, tf), lambda i, j: (j, 0, 0)), pl.BlockSpec((tf, D), lambda i, j: (j, 0)),
                  pl.BlockSpec((tm, D), lambda i, j: (i, 0))],
        out_specs=[pl.BlockSpec((tm, tf), lambda i, j: (i, j)), pl.BlockSpec((tm, D), lambda i, j: (i, 0)),
                   pl.BlockSpec((tm, D), lambda i, j: (i, 0)),
                   pl.BlockSpec((1, SUBLANES, LANES), lambda i, j: (i, 0, 0))],
        out_shape=[SDS((T, F), F32), SDS((T, D), MXU_DTYPE), SDS((T, D), F32),
                   SDS((T // tm, SUBLANES, LANES), F32)],
        scratch_shapes=[pltpu.VMEM((tm, D), F32)],
        compiler_params=_params(("arbitrary", "arbitrary")),
    )(h2, w_mn, w_up, w_down, target)


def _mlp_bwd(dy, up, h2, w_mn, w_up, w_down):
    T, D = h2.shape
    nf, _, tf = w_up.shape
    F = nf * tf
    tm = min(512, T)

    def body(dy_ref, up_ref, h_ref, wn_ref, up_w, down_w, dh_ref, dhb_ref, dup_ref, act_ref, dyb_ref, dwn_ref, acc):
        i, j = pl.program_id(0), pl.program_id(1)

        @pl.when((i == 0) & (j == 0))
        def _():
            dwn_ref[...] = jnp.zeros_like(dwn_ref)

        @pl.when(j == 0)
        def _():
            acc[...] = jnp.zeros_like(acc)
            dyb_ref[...] = dy_ref[...].astype(MXU_DTYPE)

        r = jnp.maximum(up_ref[...], 0.0)
        act_ref[...] = (r * r).astype(MXU_DTYPE)
        dup = (_mm_nt(dyb_ref[...], down_w[...]) * (2.0 * r)).astype(MXU_DTYPE)
        dup_ref[...] = dup
        acc[...] += _mm_nt(dup, up_w[0])

        @pl.when(j == nf - 1)
        def _():
            hv = h_ref[...]
            _, rr = _rms(hv, wn_ref[...])
            dx, dw = _rms_bwd(acc[...], hv, wn_ref[...], rr)
            dh = dy_ref[...] + dx
            dh_ref[...] = dh
            dhb_ref[...] = dh.astype(MXU_DTYPE)
            dwn_ref[...] += dw

    row = lambda i, j: (i, 0)
    return pl.pallas_call(
        body, grid=(T // tm, nf), name="mlp_bwd",
        in_specs=[pl.BlockSpec((tm, D), row), pl.BlockSpec((tm, tf), lambda i, j: (i, j)), pl.BlockSpec((tm, D), row),
                  pl.BlockSpec((1, D), lambda i, j: (0, 0)),
                  pl.BlockSpec((1, D, tf), lambda i, j: (j, 0, 0)), pl.BlockSpec((tf, D), lambda i, j: (j, 0))],
        out_specs=[pl.BlockSpec((tm, D), row), pl.BlockSpec((tm, D), row),
                   pl.BlockSpec((tm, tf), lambda i, j: (i, j)), pl.BlockSpec((tm, tf), lambda i, j: (i, j)),
                   pl.BlockSpec((tm, D), row), pl.BlockSpec((1, D), lambda i, j: (0, 0))],
        out_shape=[SDS((T, D), F32), SDS((T, D), MXU_DTYPE), SDS((T, F), MXU_DTYPE), SDS((T, F), MXU_DTYPE),
                   SDS((T, D), MXU_DTYPE), SDS((1, D), F32)],
        scratch_shapes=[pltpu.VMEM((tm, D), F32)],
        compiler_params=_params(("arbitrary", "arbitrary")),
    )(dy, up, h2, w_mn, w_up, w_down)


def _mix_bwd(dhb, o_mla, o_gdn, proj, mla_w, gdn_w, w_out):
    T, D = dhb.shape
    tm = min(512, T)
    H = MLA_HEADS

    def body(dh_ref, om_ref, og_ref, z_ref, mw_ref, gw_ref, w_ref, dom_ref, dog_ref, dz_ref, dmw_ref, dgw_ref):
        @pl.when(pl.program_id(0) == 0)
        def _():
            dmw_ref[...] = jnp.zeros_like(dmw_ref)
            dgw_ref[...] = jnp.zeros_like(dgw_ref)

        dmix = _mm_nt(dh_ref[...], w_ref[...])
        z = z_ref[...]
        dmw, dzs = [], []
        dgw = jnp.zeros((1, GDN_DIM), F32)
        for h in range(H):
            o = om_ref[h]
            w = mw_ref[h:h + 1, :]
            _, r = _rms(o, w)
            dx, dw = _rms_bwd(dmix[:, h * V_DIM:(h + 1) * V_DIM], o, w, r)
            dom_ref[h] = dx
            dmw.append(dw)
        for h in range(GDN_HEADS):
            o = og_ref[h]
            w = gw_ref[...]
            zh = z[:, h * GDN_DIM:(h + 1) * GDN_DIM]
            sg = _sigmoid(zh)
            yn, r = _rms(o, w)
            dy = dmix[:, H * V_DIM + h * GDN_DIM:H * V_DIM + (h + 1) * GDN_DIM]
            dzs.append(dy * yn * (sg * (1.0 + zh * (1.0 - sg))))
            dx, dw = _rms_bwd(dy * (zh * sg), o, w, r)
            dog_ref[h] = dx
            dgw = dgw + dw
        dz_ref[...] = jnp.concatenate(dzs, axis=-1)
        dmw_ref[...] += jnp.concatenate(dmw, axis=0)
        dgw_ref[...] += dgw

    hspec = pl.BlockSpec((H, tm, V_DIM), lambda i: (0, i, 0))
    return pl.pallas_call(
        body, grid=(T // tm,), name="mix_bwd",
        in_specs=[pl.BlockSpec((tm, D), lambda i: (i, 0)), hspec, hspec,
                  pl.BlockSpec((tm, GDN_WIDTH), lambda i: (i, P_GZ // GDN_WIDTH)),
                  pl.BlockSpec((H, V_DIM), lambda i: (0, 0)), pl.BlockSpec((1, GDN_DIM), lambda i: (0, 0)),
                  pl.BlockSpec((D, D), lambda i: (0, 0))],
        out_specs=[hspec, hspec, pl.BlockSpec((tm, GDN_WIDTH), lambda i: (i, 0)),
                   pl.BlockSpec((H, V_DIM), lambda i: (0, 0)), pl.BlockSpec((1, GDN_DIM), lambda i: (0, 0))],
        out_shape=[SDS((H, T, V_DIM), F32), SDS((H, T, GDN_DIM), F32), SDS((T, GDN_WIDTH), F32),
                   SDS((H, V_DIM), F32), SDS((1, GDN_DIM), F32)],
        compiler_params=_params(("arbitrary",)),
    )(dhb, o_mla, o_gdn, proj, mla_w, gdn_w, w_out)


def _attn_bwd(q4, k4, v4, do4, o4, lse4, B, S, transfer=None):
    H = MLA_HEADS
    bq = min(ATTN_BLOCK, S)
    nq = S // bq
    rows = bq // ATTN_CHAINS

    def body(q_ref, k_ref, v_ref, do_ref, o_ref, lse_ref, dq_ref, dk_ref, dv_ref, delta):
        dq_ref[...] = jnp.zeros_like(dq_ref)
        dk_ref[...] = jnp.zeros_like(dk_ref)
        dv_ref[...] = jnp.zeros_like(dv_ref)
        delta[...] = jnp.sum(do_ref[0] * o_ref[0], axis=-1, keepdims=True)

        col = lax.broadcasted_iota(jnp.int32, (rows, bq), 1)
        row = lax.broadcasted_iota(jnp.int32, (rows, bq), 0)

        def k_step(kj, carry):
            ks = pl.multiple_of(kj * bq, bq)
            k = k_ref[0, pl.ds(ks, bq), :]
            v = v_ref[0, pl.ds(ks, bq), :]

            def q_block(qs, diagonal):
                dks, dvs = [None] * ATTN_CHAINS, [None] * ATTN_CHAINS

                def chain(j):
                    sl = pl.ds(qs + j * rows, rows)
                    q = q_ref[0, sl, :]
                    do = do_ref[0, sl, :].astype(MXU_DTYPE)
                    s = _mm_nt(q, k)
                    dp = _mm_nt(do, v)
                    yield
                    p = jnp.exp(s - lse_ref[0, sl, :])
                    if diagonal:
                        p = jnp.where(col <= row + j * rows, p, 0.0)
                    ds = p * (dp - delta[sl, :])
                    yield
                    dvs[j] = _mm_tn(p, do)
                    dks[j] = _mm_tn(ds, q)
                    dq_ref[0, sl, :] += _mm(ds, k)

                _lockstep([chain(j) for j in range(ATTN_CHAINS)])
                dv_ref[0, pl.ds(ks, bq), :] += functools.reduce(jnp.add, dvs)
                dk_ref[0, pl.ds(ks, bq), :] += functools.reduce(jnp.add, dks)

            q_block(ks, True)

            def q_step(qi, c):
                q_block(pl.multiple_of(qi * bq, bq), False)
                return c

            lax.fori_loop(kj + 1, nq, q_step, 0)
            return carry

        lax.fori_loop(0, nq, k_step, 0)

    spec = lambda d: pl.BlockSpec((1, S, d), lambda h, b: (h, b, 0))
    return _call_beside(
        body, transfer, grid=(H, B), name="attn_bwd",
        in_specs=[spec(QK_DIM), spec(QK_DIM), spec(V_DIM), spec(V_DIM), spec(V_DIM), spec(1)],
        out_specs=[spec(QK_DIM), spec(QK_DIM), spec(V_DIM)],
        out_shape=[SDS((H, B * S, QK_DIM), F32), SDS((H, B * S, QK_DIM), F32), SDS((H, B * S, V_DIM), F32)],
        scratch_shapes=[pltpu.VMEM((S, 1), F32)], semantics=("arbitrary", "arbitrary"),
        args=(q4, k4, v4, do4, o4, lse4))


def _gdn_bwd(qg, kg, vg, gates, states, ainv, u4, w4, do4, B, S, transfer=None):
    H, D, C = GDN_HEADS, GDN_DIM, CHUNK
    NC = S // C
    U = GDN_UNROLL if NC % GDN_UNROLL == 0 else 1

    def body(q_ref, k_ref, v_ref, g_ref, st_ref, ai_ref, u_ref, w_ref, do_ref, dq_ref, dk_ref, dv_ref, dgb_ref,
             kd_s, x1_s, x2_s, el_s, dvn_s, ds_s, w2t_s):
        h = pl.program_id(0)
        lane = lax.broadcasted_iota(jnp.int32, (C, LANES), 1)
        ri = lax.broadcasted_iota(jnp.int32, (C, C), 0)
        ci = lax.broadcasted_iota(jnp.int32, (C, C), 1)
        rcol = lax.broadcasted_iota(jnp.int32, (C, 1), 0)

        def rsum(a):
            return jnp.sum(a, axis=-1, keepdims=True)

        def blocks(fn):
            def group(gi, c):
                _lockstep([fn(gi * U + j) for j in range(U)])
                return c
            lax.fori_loop(0, NC // U, group, 0)

        def prepare(n):
            cs = pl.multiple_of(n * C, C)
            q = q_ref[0, pl.ds(cs, C), :]
            k = k_ref[0, pl.ds(cs, C), :]
            do = do_ref[0, pl.ds(cs, C), :]
            Gc, bt, Gam, e, f, eL = _chunk_decays(g_ref[pl.ds(cs, C), :], lane, h, ri, ci, rcol)
            At = _mm_nt(q, k) * Gam
            yield
            x1 = _mm_tn(At, do)
            x2 = _mm_tn(q * e, do)
            kd = k * f
            w = w_ref[0, pl.ds(cs, C), :]
            yield
            x1_s[pl.ds(cs, C), :] = x1
            x2_s[n] = x2 - _mm_tn(w, x1)
            w2t_s[n] = _mm_tn(w, kd)
            kd_s[pl.ds(cs, C), :] = kd
            el_s[n] = jnp.broadcast_to(eL, (SUBLANES, LANES))

        blocks(prepare)

        def recur(t, dS):
            n = NC - 1 - t
            cs = pl.multiple_of(n * C, C)
            ds_s[n] = dS
            dvn_s[pl.ds(cs, C), :] = x1_s[pl.ds(cs, C), :] + _mm(kd_s[pl.ds(cs, C), :], dS)
            return x2_s[n] + el_s[n, 0:1, :] * dS - _mm(w2t_s[n], dS)

        lax.fori_loop(0, NC, recur, jnp.zeros((D, D), F32))

        def local(n):
            cs = pl.multiple_of(n * C, C)
            q = q_ref[0, pl.ds(cs, C), :]
            k = k_ref[0, pl.ds(cs, C), :]
            v = v_ref[0, pl.ds(cs, C), :]
            do = do_ref[0, pl.ds(cs, C), :]
            u = u_ref[0, pl.ds(cs, C), :]
            w = w_ref[0, pl.ds(cs, C), :]
            dvn = dvn_s[pl.ds(cs, C), :]
            dS = ds_s[n]
            Gc, bt, Gam, e, f, eL = _chunk_decays(g_ref[pl.ds(cs, C), :], lane, h, ri, ci, rcol)
            S0 = st_ref[0, n]
            Ainv = ai_ref[0, n]
            qk = _mm_nt(jnp.concatenate([q, k], axis=0), k)
            QK, KK = qk[:C], qk[C:]
            be = bt * e
            sol = jnp.concatenate([u, w], axis=-1)
            vn = u - _mm(w, S0)
            yield
            dAt = jnp.where(ri >= ci, _mm_nt(do, vn), 0.0)
            dqd = _mm_nt(do, S0)
            dw = -_mm_nt(dvn, S0)
            dkd = _mm_nt(vn, dS)
            deL = jnp.sum(rsum(dS * S0), axis=0, keepdims=True)
            yield
            dR = _mm_exact(Ainv.T, jnp.concatenate([dvn, dw], axis=-1))
            dR1, dR2 = dR[:, :D], dR[:, D:]
            yield
            dL = jnp.where(ri > ci, -_mm_nt(dR, sol), 0.0)
            yield
            dv_ref[0, pl.ds(cs, C), :] = dR1 * bt
            r2 = rsum(dR2 * k)
            X = dL * Gam
            dbt = rsum(dR1 * v) + r2 * e + rsum(X * KK)
            de = r2 * bt + rsum(dqd * q)
            dKK = X * bt
            dQK = dAt * Gam
            dq_ref[0, pl.ds(cs, C), :] = _mm(dQK, k) + dqd * e
            dk_ref[0, pl.ds(cs, C), :] = dR2 * be + _mm(dKK + dKK.T, k) + _mm_tn(dQK, q) + dkd * f
            df = rsum(dkd * k)
            Z = (dL * (bt * KK) + dAt * QK) * Gam
            dG = rsum(Z) - rsum(Z.T) + de * e - df * f
            dGl = jnp.sum(df * f, axis=0, keepdims=True) + deL * eL
            dG = dG + jnp.where(rcol == C - 1, dGl, 0.0)
            dgb_ref[0, pl.ds(cs, C), :] = jnp.where(lane == 0, dG, jnp.where(lane == 1, dbt, 0.0))

        blocks(local)

    spec = pl.BlockSpec((1, S, D), lambda h, b: (h, b, 0))
    return _call_beside(
        body, transfer, grid=(H, B), name="gdn_bwd",
        in_specs=[spec, spec, spec, pl.BlockSpec((S, LANES), lambda h, b: (b, 0)),
                  pl.BlockSpec((1, NC, D, D), lambda h, b: (h, b, 0, 0)),
                  pl.BlockSpec((1, NC, C, C), lambda h, b: (h, b, 0, 0)), spec, spec, spec],
        out_specs=[spec, spec, spec, spec],
        out_shape=[SDS((H, B * S, D), F32)] * 4,
        scratch_shapes=[pltpu.VMEM((S, D), F32), pltpu.VMEM((S, D), F32), pltpu.VMEM((NC, D, D), F32),
                        pltpu.VMEM((NC, SUBLANES, LANES), F32), pltpu.VMEM((S, D), F32),
                        pltpu.VMEM((NC, D, D), F32), pltpu.VMEM((NC, D, D), F32)],
        semantics=("arbitrary", "arbitrary"), args=(qg, kg, vg, gates, states, ainv, u4, w4, do4))


def _gdn_pre_bwd(proj, conv_w, alog_l, dt_l, dq4, dk4, dv4, dgb4, S):
    T = proj.shape[0]
    tm = min(256, T)
    tiles_per_seq = S // tm
    C3 = 3 * GDN_WIDTH
    H = GDN_HEADS

    def body(u_ref, halo_ref, gab_ref, w_ref, alog_ref, dt_ref, dq_ref, dk_ref, dv_ref, dgb_ref,
             dc_ref, dgab_ref, dcw_ref, dalog_ref, ddt_ref):
        i = pl.program_id(0)

        @pl.when(i == 0)
        def _():
            dcw_ref[...] = jnp.zeros_like(dcw_ref)
            dalog_ref[...] = jnp.zeros_like(dalog_ref)
            ddt_ref[...] = jnp.zeros_like(ddt_ref)

        halo = jnp.where(i % tiles_per_seq == 0, 0.0, halo_ref[...])
        c, sh = _conv_taps(u_ref[...], halo, w_ref[...])
        sg = _sigmoid(c)
        a = c * sg
        das = [None] * (3 * H)
        for h in range(H):
            xq = a[:, h * GDN_DIM:(h + 1) * GDN_DIM]
            xk = a[:, GDN_WIDTH + h * GDN_DIM:GDN_WIDTH + (h + 1) * GDN_DIM]
            das[h] = _l2n_bwd(dq_ref[h], xq, GDN_QSCALE)
            das[H + h] = _l2n_bwd(dk_ref[h], xk, 1.0)
            das[2 * H + h] = dv_ref[h]
        dc = jnp.concatenate(das, axis=-1) * (sg * (1.0 + c * (1.0 - sg)))
        dc_ref[...] = dc
        dcw_ref[...] += jnp.concatenate(
            [jnp.sum(dc * sh[CONV_W - 1 - t], axis=0, keepdims=True) for t in range(CONV_W)], axis=0)
        lane = lax.broadcasted_iota(jnp.int32, (tm, LANES), 1)
        ric = lax.broadcasted_iota(jnp.int32, (tm, LANES), 0) % CHUNK
        dG = jnp.zeros((tm, LANES), F32)
        for h in range(H):
            t = dgb_ref[h]
            dG = dG + jnp.where(lane == h, _pick_lane(t, lane, 0), 0.0) \
                    + jnp.where(lane == h + H, _pick_lane(t, lane, 1), 0.0)
        is_g = lane < H
        dg = jnp.where(is_g, _chunk_rev_cumsum(jnp.where(is_g, dG, 0.0), ric), 0.0)
        gab = gab_ref[...]
        g, beta = _gate_values(gab, alog_ref[...], dt_ref[...], lane)
        dga = jnp.where(is_g, dg * (-jnp.exp(alog_ref[...])) * _sigmoid(gab + dt_ref[...]), 0.0)
        dgb = jnp.where(is_g, 0.0, dG) * beta * (1.0 - beta)
        dgab_ref[...] = dga + dgb
        dalog_ref[...] += jnp.sum(dg * g, axis=0, keepdims=True)
        ddt_ref[...] += jnp.sum(dga, axis=0, keepdims=True)

    hspec = pl.BlockSpec((H, tm, GDN_DIM), lambda i: (0, i, 0))
    vec = pl.BlockSpec((1, LANES), lambda i: (0, 0))
    return pl.pallas_call(
        body, grid=(T // tm,), name="gdn_pre_bwd",
        in_specs=[pl.BlockSpec((tm, C3), lambda i: (i, 0)),
                  pl.BlockSpec((SUBLANES, C3), lambda i: (jnp.maximum(i * (tm // SUBLANES) - 1, 0), 0)),
                  pl.BlockSpec((tm, LANES), lambda i: (i, P_GAB // LANES)),
                  pl.BlockSpec((CONV_W, C3), lambda i: (0, 0)), vec, vec, hspec, hspec, hspec, hspec],
        out_specs=[pl.BlockSpec((tm, C3), lambda i: (i, 0)), pl.BlockSpec((tm, LANES), lambda i: (i, 0)),
                   pl.BlockSpec((CONV_W, C3), lambda i: (0, 0)), vec, vec],
        out_shape=[SDS((T, C3), F32), SDS((T, LANES), F32), SDS((CONV_W, C3), F32),
                   SDS((1, LANES), F32), SDS((1, LANES), F32)],
        compiler_params=_params(("arbitrary",)),
    )(proj, proj, proj, conv_w, alog_l, dt_l, dq4, dk4, dv4, dgb4)


def _conv_bwd_input(dc, conv_w, S):
    T, C3 = dc.shape
    tm = min(256, T)
    tiles_per_seq = S // tm
    nblk = T // SUBLANES

    def body(dc_ref, nxt_ref, w_ref, du_ref):
        i = pl.program_id(0)
        nxt = jnp.where(i % tiles_per_seq == tiles_per_seq - 1, 0.0, nxt_ref[...])
        x = dc_ref[...]
        w = w_ref[...]
        du = w[3:4] * x
        for j in range(1, CONV_W):
            du = du + w[3 - j:4 - j] * _shift_up(x, nxt, j)
        du_ref[...] = du

    return pl.pallas_call(
        body, grid=(T // tm,), name="conv_bwd_input",
        in_specs=[pl.BlockSpec((tm, C3), lambda i: (i, 0)),
                  pl.BlockSpec((SUBLANES, C3), lambda i: (jnp.minimum((i + 1) * (tm // SUBLANES), nblk - 1), 0)),
                  pl.BlockSpec((CONV_W, C3), lambda i: (0, 0))],
        out_specs=pl.BlockSpec((tm, C3), lambda i: (i, 0)),
        out_shape=SDS((T, C3), F32),
        compiler_params=_params(("arbitrary",)),
    )(dc, dc, conv_w)


def _mla_pre_bwd(proj, cosf, sinf, w_qln, w_kvln, w_uq_p, w_ukv, qnw, knw, dq4, dk4, dv4):
    T = proj.shape[0]
    tm = min(256, T)
    H = MLA_HEADS

    def body(ql_ref, kvl_ref, kpe_ref, cos_ref, sin_ref, wq_ref, wkv_ref, uq_ref, ukv_ref, qnw_ref, knw_ref,
             dq_ref, dk_ref, dv_ref,
             dql_ref, dkvl_ref, dkpe_ref, dqraw_ref, dkvraw_ref, qn_ref, kvn_ref, dwq_ref, dwkv_ref, dqnw_ref, dknw_ref):
        @pl.when(pl.program_id(0) == 0)
        def _():
            for r in (dwq_ref, dwkv_ref, dqnw_ref, dknw_ref):
                r[...] = jnp.zeros_like(r)

        cos, sin = cos_ref[...], sin_ref[...]
        qnw_, knw_ = qnw_ref[...], knw_ref[...]
        ql, kvl = ql_ref[...], kvl_ref[...]
        kpe_raw = kpe_ref[...][:, :ROPE]
        qn, rq = _rms(ql, wq_ref[...])
        kvn, rkv = _rms(kvl, wkv_ref[...])
        qn_ref[...] = qn.astype(MXU_DTYPE)
        kvn_ref[...] = kvn.astype(MXU_DTYPE)
        qraw = _mm(qn, uq_ref[...])
        kvraw = _mm(kvn, ukv_ref[...])
        dq_nope, dq_pe, dkv_parts = [], [], []
        dqnw_n = jnp.zeros((1, NOPE), F32)
        dqnw_p = jnp.zeros((1, ROPE), F32)
        dknw_n = jnp.zeros((1, NOPE), F32)
        dkpe = jnp.zeros((tm, ROPE), F32)
        for h in range(H):
            dq = dq_ref[h] * ATT_SCALE
            x = qraw[:, h * NOPE:(h + 1) * NOPE]
            dx, dw = _rms_bwd(dq[:, :NOPE], x, qnw_[:, :NOPE], _rms(x, qnw_[:, :NOPE])[1])
            dq_nope.append(dx)
            dqnw_n = dqnw_n + dw
            x = qraw[:, H * NOPE + h * ROPE:H * NOPE + (h + 1) * ROPE]
            dx, dw = _rms_bwd(_rope_bwd(dq[:, NOPE:], cos, sin), x, qnw_[:, NOPE:], _rms(x, qnw_[:, NOPE:])[1])
            dq_pe.append(dx)
            dqnw_p = dqnw_p + dw
            dk = dk_ref[h]
            x = kvraw[:, h * 256:h * 256 + NOPE]
            dx, dw = _rms_bwd(dk[:, :NOPE], x, knw_[:, :NOPE], _rms(x, knw_[:, :NOPE])[1])
            dknw_n = dknw_n + dw
            dkpe = dkpe + dk[:, NOPE:]
            dkv_parts += [dx, dv_ref[h]]
        dx, dknw_p = _rms_bwd(_rope_bwd(dkpe, cos, sin), kpe_raw, knw_[:, NOPE:], _rms(kpe_raw, knw_[:, NOPE:])[1])
        dkpe_ref[...] = jnp.concatenate([dx, jnp.zeros((tm, LANES - ROPE), F32)], axis=-1)
        dqraw = jnp.concatenate(dq_nope + dq_pe, axis=-1).astype(MXU_DTYPE)
        dkvraw = jnp.concatenate(dkv_parts, axis=-1).astype(MXU_DTYPE)
        dqraw_ref[...] = dqraw
        dkvraw_ref[...] = dkvraw
        dx, dw = _rms_bwd(_mm_nt(dqraw, uq_ref[...]), ql, wq_ref[...], rq)
        dql_ref[...] = dx
        dwq_ref[...] += dw
        dx, dw = _rms_bwd(_mm_nt(dkvraw, ukv_ref[...]), kvl, wkv_ref[...], rkv)
        dkvl_ref[...] = dx
        dwkv_ref[...] += dw
        dqnw_ref[...] += jnp.concatenate([dqnw_n, dqnw_p], axis=-1)
        dknw_ref[...] += jnp.concatenate([dknw_n, dknw_p], axis=-1)

    full = lambda a: pl.BlockSpec(a.shape, lambda i: (0,) * a.ndim)
    rows = lambda n: pl.BlockSpec((tm, n), lambda i: (i, 0))
    const = lambda n: pl.BlockSpec((1, n), lambda i: (0, 0))
    NQ, NKV = w_uq_p.shape[1], w_ukv.shape[1]
    return pl.pallas_call(
        body, grid=(T // tm,), name="mla_pre_bwd",
        in_specs=[pl.BlockSpec((tm, 256), lambda i: (i, P_QLAT // 256)),
                  pl.BlockSpec((tm, 256), lambda i: (i, P_KVLAT // 256)),
                  pl.BlockSpec((tm, 128), lambda i: (i, P_KPE // 128)),
                  rows(ROPE), rows(ROPE),
                  full(w_qln), full(w_kvln), full(w_uq_p), full(w_ukv), full(qnw), full(knw),
                  pl.BlockSpec((H, tm, QK_DIM), lambda i: (0, i, 0)),
                  pl.BlockSpec((H, tm, QK_DIM), lambda i: (0, i, 0)),
                  pl.BlockSpec((H, tm, V_DIM), lambda i: (0, i, 0))],
        out_specs=[rows(Q_LORA), rows(KV_LORA), rows(LANES), rows(NQ), rows(NKV), rows(Q_LORA), rows(KV_LORA),
                   const(Q_LORA), const(KV_LORA), const(QK_DIM), const(QK_DIM)],
        out_shape=[SDS((T, Q_LORA), F32), SDS((T, KV_LORA), F32), SDS((T, LANES), F32),
                   SDS((T, NQ), MXU_DTYPE), SDS((T, NKV), MXU_DTYPE),
                   SDS((T, Q_LORA), MXU_DTYPE), SDS((T, KV_LORA), MXU_DTYPE),
                   SDS((1, Q_LORA), F32), SDS((1, KV_LORA), F32), SDS((1, QK_DIM), F32), SDS((1, QK_DIM), F32)],
        compiler_params=_params(("arbitrary",)),
    )(proj, proj, proj, cosf, sinf, w_qln, w_kvln, w_uq_p, w_ukv, qnw, knw, dq4, dk4, dv4)


def _in_proj_bwd(dgqkv, dgz, dql, dkvl, dkpe, dgab, w_in_p, dh, x2, w_an):
    T, D = x2.shape
    N = w_in_p.shape[1]
    tm = min(512, T)

    def body(a_ref, b_ref, c_ref, d_ref, e_ref, f_ref, w_ref, dh_ref, x_ref, wn_ref, dx_ref, dp_ref, dwn_ref):
        @pl.when(pl.program_id(0) == 0)
        def _():
            dwn_ref[...] = jnp.zeros_like(dwn_ref)

        dp = jnp.concatenate([a_ref[...], b_ref[...], c_ref[...], d_ref[...], e_ref[...], f_ref[...]],
                             axis=-1).astype(MXU_DTYPE)
        dp_ref[...] = dp
        x = x_ref[...]
        _, r = _rms(x, wn_ref[...])
        dx, dw = _rms_bwd(_mm_nt(dp, w_ref[...]), x, wn_ref[...], r)
        dx_ref[...] = dh_ref[...] + dx
        dwn_ref[...] += dw

    rows = lambda n: pl.BlockSpec((tm, n), lambda i: (i, 0))
    return pl.pallas_call(
        body, grid=(T // tm,), name="in_proj_bwd",
        in_specs=[rows(dgqkv.shape[1]), rows(dgz.shape[1]), rows(dql.shape[1]), rows(dkvl.shape[1]),
                  rows(dkpe.shape[1]), rows(dgab.shape[1]),
                  pl.BlockSpec((D, N), lambda i: (0, 0)), rows(D), rows(D), pl.BlockSpec((1, D), lambda i: (0, 0))],
        out_specs=[rows(D), rows(N), pl.BlockSpec((1, D), lambda i: (0, 0))],
        out_shape=[SDS((T, D), F32), SDS((T, N), MXU_DTYPE), SDS((1, D), F32)],
        compiler_params=_params(("arbitrary",)),
    )(dgqkv, dgz, dql, dkvl, dkpe, dgab, w_in_p, dh, x2, w_an)


def _wgrad(a, b, name, column_shards=False):
    T, M = a.shape
    N = b.shape[1]
    tM = _divisor_tile(M, 1024)
    tN = N // N_DEV if column_shards else _divisor_tile(N, 1536)
    tk = min(T, 1024)
    nk = T // tk

    def body(a_ref, b_ref, o_ref, acc):
        k = pl.program_id(2)

        @pl.when(k == 0)
        def _():
            acc[...] = jnp.zeros_like(acc)

        acc[...] += _mm_tn(a_ref[...], b_ref[...])

        @pl.when(k == nk - 1)
        def _():
            o_ref[...] = acc[...].astype(WIRE_DTYPE).reshape(o_ref.shape)

    if column_shards:
        out_spec, out_shape = pl.BlockSpec((1, tM, tN), lambda i, j, k: (j, i, 0)), SDS((N_DEV, M, tN), WIRE_DTYPE)
    else:
        out_spec, out_shape = pl.BlockSpec((tM, tN), lambda i, j, k: (i, j)), SDS((M, N), WIRE_DTYPE)
    return pl.pallas_call(
        body, grid=(M // tM, N // tN, nk), name=name,
        in_specs=[pl.BlockSpec((tk, tM), lambda i, j, k: (k, i)), pl.BlockSpec((tk, tN), lambda i, j, k: (k, j))],
        out_specs=out_spec, out_shape=out_shape,
        scratch_shapes=[pltpu.VMEM((tM, tN), F32)],
        compiler_params=_params(("arbitrary", "arbitrary", "arbitrary")),
    )(a, b)


def _adamw(g, w, m, v):
    m = ADAM_B1 * m + (1.0 - ADAM_B1) * g
    v = ADAM_B2 * v + (1.0 - ADAM_B2) * jnp.square(g)
    m_hat = m / (1.0 - ADAM_B1 ** ADAM_STEP)
    v_hat = v / (1.0 - ADAM_B2 ** ADAM_STEP)
    return -ADAM_LR * (m_hat / (jnp.sqrt(v_hat) + ADAM_EPS) + ADAM_WD * w), m, v


def _reduce_adamw(parts, w, m, v, name):
    R, C = w.shape
    _, Rp, Cp = parts.shape
    tr = min(R, 256)
    tp = tr if Rp == R else Rp

    def body(p_ref, w_ref, m_ref, v_ref, g_ref, d_ref, nm_ref, nv_ref):
        g = p_ref[0].astype(F32)
        for s in range(1, N_DEV):
            g = g + p_ref[s].astype(F32)
        g = g[:tr, :C]
        g_ref[...] = g
        d_ref[...], nm_ref[...], nv_ref[...] = _adamw(g, w_ref[...], m_ref[...], v_ref[...])

    spec = pl.BlockSpec((tr, C), lambda i: (i, 0))
    return pl.pallas_call(
        body, grid=(R // tr,), name=name,
        in_specs=[pl.BlockSpec((N_DEV, tp, Cp), lambda i: (0, i, 0)), spec, spec, spec],
        out_specs=[spec] * 4, out_shape=[SDS((R, C), F32)] * 4,
        compiler_params=_params(("arbitrary",)),
    )(parts, w, m, v)


SMALL_ROWS, SMALL_COLS = 16, 1024
SMALL_LAYOUT = (
    ("attn_norm_w", 0, 1, 1024, 1024), ("mlp_norm_w", 1, 1, 1024, 1024), ("q_lat_norm_w", 2, 1, 256, 256),
    ("kv_lat_norm_w", 3, 1, 256, 256), ("q_norm_w", 4, 1, 192, 192), ("k_norm_w", 5, 1, 192, 192),
    ("mla_out_norm_w", 6, 4, 128, 128), ("a_log", 10, 1, 128, 4), ("dt_bias", 11, 1, 128, 4),
    ("gdn_norm_w", 12, 1, 128, 128))


def _adamw_replicated(parts, ws, ms, vs):
    n = len(SMALL_LAYOUT)

    def body(*refs):
        p_ref = refs[0]
        w_refs, m_refs, v_refs = refs[1:1 + n], refs[1 + n:1 + 2 * n], refs[1 + 2 * n:1 + 3 * n]
        outs = refs[1 + 3 * n:]
        s = p_ref[0]
        for d in range(1, N_DEV):
            s = s + p_ref[d]
        for i, (_, r0, nr, _, pw) in enumerate(SMALL_LAYOUT):
            g = s[r0:r0 + nr, :pw]
            outs[i][...] = g
            outs[n + i][...], outs[2 * n + i][...], outs[3 * n + i][...] = _adamw(
                g, w_refs[i][...], m_refs[i][...], v_refs[i][...])

    res = pl.pallas_call(
        body, name="adamw_replicated",
        out_shape=[SDS(w.shape, F32) for w in ws] * 4,
        compiler_params=_params(),
    )(parts, *ws, *ms, *vs)
    return [res[k * n:(k + 1) * n] for k in range(4)]


COPIES_PER_ARRAY = N_DEV - 1


def _two_level_gather(srcs, outs, send_sems, recv_sems, local_sems=None, stage="all"):
    mx, my, mc = lax.axis_index("x"), lax.axis_index("y"), lax.axis_index("c")
    me, sibling = (mx, my, mc), (mx, my, 1 - mc)
    chips = [(1 - mx, my), (mx, 1 - my), (1 - mx, 1 - my)]
    arrays = range(len(srcs))

    def copy(a, k, block, to, src=None):
        px, py, pc = block
        slot = outs[a].at[4 * px + 2 * py + pc]
        sem = a * COPIES_PER_ARRAY + k
        return pltpu.make_async_remote_copy(
            src_ref=slot if src is None else src, dst_ref=slot,
            send_sem=send_sems.at[sem], recv_sem=recv_sems.at[sem], device_id=to, device_id_type=MESH_ID)

    mine = [] if local_sems is None else [
        pltpu.make_async_copy(srcs[a], outs[a].at[4 * mx + 2 * my + mc], local_sems.at[a]) for a in arrays]
    first = []
    for a in arrays:
        first.append(copy(a, 0, me, sibling, src=srcs[a]))
        first += [copy(a, 1 + j, me, (*chip, mc), src=srcs[a]) for j, chip in enumerate(chips)]
    if stage in ("all", "start"):
        for cp in mine + first:
            cp.start()
    if stage in ("all", "finish"):
        forwards = []
        for j, chip in enumerate(chips):
            for a in arrays:
                copy(a, 1 + j, (*chip, mc), me).wait_recv()
                fwd = copy(a, 4 + j, (*chip, mc), sibling)
                fwd.start()
                forwards.append(fwd)
        for a in arrays:
            copy(a, 0, sibling, me).wait_recv()
        for j, chip in enumerate(chips):
            for a in arrays:
                copy(a, 4 + j, (*chip, 1 - mc), me).wait_recv()
        for cp in first + forwards:
            cp.wait_send()
        for cp in mine:
            cp.wait()


def _comm_scratch(n):
    return [pltpu.SemaphoreType.DMA((n * COPIES_PER_ARRAY,)), pltpu.SemaphoreType.DMA((n * COPIES_PER_ARRAY,)),
            pltpu.SemaphoreType.DMA((n,))]


def _any_specs(n):
    return [pl.BlockSpec(memory_space=pl.ANY)] * n


def _gather_weights(shards):
    n = len(shards)

    def body(*refs):
        _two_level_gather(refs[:n], refs[n:2 * n], *refs[2 * n:])

    return pl.pallas_call(
        body, name="gather_weights",
        out_shape=[SDS((N_DEV,) + s.shape, s.dtype) for s in shards],
        in_specs=_any_specs(n), out_specs=_any_specs(n), scratch_shapes=_comm_scratch(n),
    )(*shards)


def _gather_small_grads(gs):
    n = len(gs)

    def body(*refs):
        g_refs, out_ref = refs[:n], refs[n]
        tile, send_sems, recv_sems = refs[n + 1:]
        tile[...] = jnp.zeros_like(tile)
        for (_, r0, nr, gw, _), g in zip(SMALL_LAYOUT, g_refs):
            tile[r0:r0 + nr, 0:gw] = g[...]
        me = 4 * lax.axis_index("x") + 2 * lax.axis_index("y") + lax.axis_index("c")
        out_ref[me] = tile[...]
        _two_level_gather([tile], [out_ref], send_sems, recv_sems)

    return pl.pallas_call(
        body, name="gather_small_grads",
        out_shape=SDS((N_DEV, SMALL_ROWS, SMALL_COLS), F32),
        in_specs=[pl.BlockSpec(memory_space=pltpu.VMEM)] * n,
        out_specs=pl.BlockSpec(memory_space=pltpu.VMEM),
        scratch_shapes=[pltpu.VMEM((SMALL_ROWS, SMALL_COLS), F32),
                        pltpu.SemaphoreType.DMA((COPIES_PER_ARRAY,)), pltpu.SemaphoreType.DMA((COPIES_PER_ARRAY,))],
    )(*gs)


def _exchange_grads(slabs):
    n = len(slabs)

    def body(*refs):
        _exchange(refs[:n], refs[n:2 * n], *refs[2 * n:])

    return pl.pallas_call(
        body, name="exchange_grads",
        out_shape=[SDS(s.shape, s.dtype) for s in slabs],
        in_specs=_any_specs(n), out_specs=_any_specs(n), scratch_shapes=_comm_scratch(n),
    )(*slabs)


class _Transfer:
    def __init__(self, kind, arrays):
        self.kind, self.arrays, self.n = kind, list(arrays), len(arrays)

    def out_shapes(self):
        if self.kind == "gather":
            return [SDS((N_DEV,) + a.shape, a.dtype) for a in self.arrays]
        return [SDS(a.shape, a.dtype) for a in self.arrays]

    def run(self, srcs, outs, sems, stage):
        fn = _two_level_gather if self.kind == "gather" else _exchange
        fn(srcs, outs, *sems, stage=stage)


def _call_beside(body, transfer, *, grid, in_specs, out_specs, out_shape, scratch_shapes, name, semantics, args):
    if transfer is None:
        res = pl.pallas_call(body, grid=grid, in_specs=in_specs, out_specs=out_specs, out_shape=out_shape,
                             scratch_shapes=scratch_shapes, name=name, compiler_params=_params(semantics))(*args)
        return list(res), []
    n_in, n_out, n_s, n = len(in_specs), len(out_specs), len(scratch_shapes), transfer.n

    def wrapped(*refs):
        ins, refs = refs[:n_in], refs[n_in:]
        t_in, refs = refs[:n], refs[n:]
        outs, refs = refs[:n_out], refs[n_out:]
        t_out, refs = refs[:n], refs[n:]
        scratch, sems = refs[:n_s], refs[n_s:]
        first = functools.reduce(jnp.logical_and, [pl.program_id(i) == 0 for i in range(len(grid))])
        last = functools.reduce(jnp.logical_and, [pl.program_id(i) == g - 1 for i, g in enumerate(grid)])

        @pl.when(first)
        def _():
            transfer.run(t_in, t_out, sems, "start")

        body(*ins, *outs, *scratch)

        @pl.when(last)
        def _():
            transfer.run(t_in, t_out, sems, "finish")

    res = pl.pallas_call(
        wrapped, grid=grid, in_specs=list(in_specs) + _any_specs(n), out_specs=list(out_specs) + _any_specs(n),
        out_shape=list(out_shape) + transfer.out_shapes(), scratch_shapes=list(scratch_shapes) + _comm_scratch(n),
        name=name, compiler_params=_params(semantics))(*args, *transfer.arrays)
    return list(res[:n_out]), list(res[n_out:])


EXCHANGE_FLIPS = ((0, 0, 1), (1, 0, 0), (0, 1, 0), (1, 1, 0), (1, 0, 1), (0, 1, 1), (1, 1, 1))


def _exchange(srcs, outs, send_sems, recv_sems, local_sems, stage="all"):
    mx, my, mc = lax.axis_index("x"), lax.axis_index("y"), lax.axis_index("c")
    arrays = range(len(srcs))
    copies = [pltpu.make_async_copy(srcs[a].at[4 * mx + 2 * my + mc], outs[a].at[N_DEV - 1], local_sems.at[a])
              for a in arrays]
    for k, (fx, fy, fc) in enumerate(EXCHANGE_FLIPS):
        px = 1 - mx if fx else mx
        py = 1 - my if fy else my
        pc = 1 - mc if fc else mc
        for a in arrays:
            sem = a * COPIES_PER_ARRAY + k
            copies.append(pltpu.make_async_remote_copy(
                src_ref=srcs[a].at[4 * px + 2 * py + pc], dst_ref=outs[a].at[k],
                send_sem=send_sems.at[sem], recv_sem=recv_sems.at[sem],
                device_id=(px, py, pc), device_id_type=MESH_ID))
    if stage in ("all", "start"):
        for cp in copies:
            cp.start()
    if stage in ("all", "finish"):
        for cp in copies:
            cp.wait()


def _w_in_to_padded(w):
    z = lambda n: jnp.zeros((w.shape[0], n), w.dtype)
    return jnp.concatenate([w[:, O_GQKV:O_GZ], w[:, O_GZ:O_GAB], w[:, O_QLAT:O_KVLAT], w[:, O_KVLAT:O_KPE],
                            w[:, O_KPE:O_GQKV], z(P_GAB - P_KPE - ROPE), w[:, O_GAB:O_END],
                            z(P_WIDTH - P_GAB - (O_END - O_GAB))], axis=1)


def _w_in_from_padded(wp):
    return jnp.concatenate([wp[:, P_QLAT:P_QLAT + 256], wp[:, P_KVLAT:P_KVLAT + 256], wp[:, P_KPE:P_KPE + ROPE],
                            wp[:, P_GQKV:P_GZ], wp[:, P_GZ:P_QLAT], wp[:, P_GAB:P_GAB + (O_END - O_GAB)]], axis=1)


def _w_uq_to_headsplit(w):
    w3 = w.reshape(w.shape[0], MLA_HEADS, QK_DIM)
    return jnp.concatenate([w3[:, :, :NOPE].reshape(w.shape[0], -1), w3[:, :, NOPE:].reshape(w.shape[0], -1)], axis=1)


def _w_uq_from_headsplit(wp):
    n = wp[:, :MLA_HEADS * NOPE].reshape(wp.shape[0], MLA_HEADS, NOPE)
    p = wp[:, MLA_HEADS * NOPE:].reshape(wp.shape[0], MLA_HEADS, ROPE)
    return jnp.concatenate([n, p], axis=2).reshape(wp.shape[0], -1)


def _lane_vec(v4):
    return jnp.pad(v4.reshape(1, -1), ((0, 0), (0, LANES - v4.shape[-1])))


def _local_step(x, positions, target, attn_norm_w, w_in, q_lat_norm_w, w_uq, kv_lat_norm_w, w_ukv, q_norm_w,
                k_norm_w, mla_out_norm_w, conv_w, a_log, dt_bias, gdn_norm_w, w_out, mlp_norm_w, w_up, w_down,
                late_shards=None, exchange=False):
    B, S, D = x.shape
    T = B * S
    x2 = x.reshape(T, D)
    t2 = target.reshape(T, D)
    half = ROPE // 2
    inv_freq = ROPE_THETA ** (-jnp.arange(half, dtype=F32) / half)
    ang = positions.reshape(T, 1).astype(F32) * inv_freq
    cosf = jnp.concatenate([jnp.cos(ang)] * 2, axis=-1)
    sinf = jnp.concatenate([jnp.sin(ang)] * 2, axis=-1)
    w_in_p = _w_in_to_padded(w_in)
    w_uq_p = _w_uq_to_headsplit(w_uq)
    alog_l, dt_l = _lane_vec(a_log), _lane_vec(dt_bias)
    w_an, w_qln, w_kvln, qnw, knw, w_mn, gdn_w = (
        attn_norm_w, q_lat_norm_w, kv_lat_norm_w, q_norm_w, k_norm_w, mlp_norm_w, gdn_norm_w)

    proj, xn = _in_proj(x2, w_an, w_in_p)
    q4, k4, v4 = _mla_pre(proj, cosf, sinf, w_qln, w_kvln, w_uq_p, w_ukv, qnw, knw)
    gather = None if late_shards is None else _Transfer("gather", late_shards)
    (o_mla, lse), late = _attn_fwd(q4, k4, v4, B, S, gather)
    if late:
        w_out, w_up, w_down = late[0].reshape(-1, D), late[1], late[2].reshape(-1, D)
    qg, kg, vg, gates = _gdn_pre(proj, conv_w, alog_l, dt_l, S)
    o_gdn, states, ainv, u4, w4 = _gdn_fwd(qg, kg, vg, gates, B, S)
    h2, mix = _mix_out(o_mla, o_gdn, proj, x2, mla_out_norm_w, gdn_w, w_out)
    up, hn, dy, sq = _mlp_fwd(h2, w_mn, w_up, w_down, t2)
    loss = (0.5 / D) * jnp.sum(sq[:, 0, 0])

    dh, dhb, dup, act, dyb, d_mlp_norm = _mlp_bwd(dy, up, h2, w_mn, w_up, w_down)
    g_w_down = _wgrad(act, dyb, "wgrad_down")
    g_w_up = _wgrad(hn, dup, "wgrad_up", column_shards=True)
    do_mla, do_gdn, dz, d_mla_w, d_gdn_w = _mix_bwd(dhb, o_mla, o_gdn, proj, mla_out_norm_w, gdn_w, w_out)
    g_w_out = _wgrad(mix, dhb, "wgrad_out")
    first = ("w_up", "w_down", "w_out")
    second = ("w_uq", "w_ukv")
    mats = dict(w_up=g_w_up, w_down=g_w_down, w_out=g_w_out)
    send = _Transfer("exchange", [_slabs(n, mats[n]) for n in first]) if exchange else None
    (dq4, dk4, dv4), got = _attn_bwd(q4, k4, v4, do_mla, o_mla, lse, B, S, send)
    mats.update(zip(first, got))
    dql, dkvl, dkpe, dqraw, dkvraw, qn, kvn, d_wqln, d_wkvln, d_qnw, d_knw = _mla_pre_bwd(
        proj, cosf, sinf, w_qln, w_kvln, w_uq_p, w_ukv, qnw, knw, dq4, dk4, dv4)
    mats.update(w_uq=_wgrad(qn, dqraw, "wgrad_uq"), w_ukv=_wgrad(kvn, dkvraw, "wgrad_ukv"))
    send = _Transfer("exchange", [_slabs(n, mats[n]) for n in second]) if exchange else None
    (dqg, dkg, dvg, dgb4), got = _gdn_bwd(qg, kg, vg, gates, states, ainv, u4, w4, do_gdn, B, S, send)
    mats.update(zip(second, got))
    dc, dgab, g_conv, d_alog, d_dt = _gdn_pre_bwd(proj, conv_w, alog_l, dt_l, dqg, dkg, dvg, dgb4, S)
    dgqkv = _conv_bwd_input(dc, conv_w, S)
    grad_x2, dproj, d_attn_norm = _in_proj_bwd(dgqkv, dz, dql, dkvl, dkpe, dgab, w_in_p, dh, x2, w_an)
    mats.update(w_in=_wgrad(xn, dproj, "wgrad_in"), conv_w=g_conv)
    if exchange:
        last = ("w_in", "conv_w")
        mats.update(zip(last, _exchange_grads([_slabs(n, mats[n]) for n in last])))
    small = dict(attn_norm_w=d_attn_norm, mlp_norm_w=d_mlp_norm, q_lat_norm_w=d_wqln, kv_lat_norm_w=d_wkvln,
                 q_norm_w=d_qnw, k_norm_w=d_knw, mla_out_norm_w=d_mla_w, a_log=d_alog, dt_bias=d_dt,
                 gdn_norm_w=d_gdn_w)
    return loss, grad_x2.reshape(B, S, D), mats, [small[n] for n, *_ in SMALL_LAYOUT]


BIG = ("w_in", "w_uq", "w_ukv", "conv_w", "w_out", "w_up", "w_down")
ALL_W = ("attn_norm_w", "w_in", "q_lat_norm_w", "w_uq", "kv_lat_norm_w", "w_ukv", "q_norm_w", "k_norm_w",
         "mla_out_norm_w", "conv_w", "a_log", "dt_bias", "gdn_norm_w", "w_out", "mlp_norm_w", "w_up", "w_down")
WIRE_SHAPE = {"w_in": (1024, 384), "w_uq": (256, 128), "conv_w": (16, 256)}


def _pad2(a, rows, cols):
    return jnp.pad(a, [(0, 0)] * (a.ndim - 2) + [(0, rows - a.shape[-2]), (0, cols - a.shape[-1])])


def _cols_to_full(stack, cols):
    return jnp.moveaxis(stack[:, :, :cols], 0, 1).reshape(stack.shape[1], N_DEV * cols)


def _full_to_cols(full, wire_cols):
    r, n = full.shape
    return _pad2(jnp.moveaxis(full.reshape(r, N_DEV, n // N_DEV), 1, 0), r, wire_cols)


def _slabs(name, g):
    if name == "w_in":
        return _full_to_cols(_w_in_from_padded(g), WIRE_SHAPE["w_in"][1])
    if name == "w_uq":
        return _full_to_cols(_w_uq_from_headsplit(g), WIRE_SHAPE["w_uq"][1])
    if name == "w_ukv":
        return _full_to_cols(g, g.shape[1] // N_DEV)
    if name == "conv_w":
        return _pad2(_full_to_cols(g.astype(WIRE_DTYPE), g.shape[1] // N_DEV), *WIRE_SHAPE["conv_w"])
    if name == "w_up":
        return g
    return g.reshape(N_DEV, -1, g.shape[-1])


def kernel(x, positions, attn_norm_w, w_in, q_lat_norm_w, w_uq, kv_lat_norm_w, w_ukv, q_norm_w, k_norm_w, mla_out_norm_w, conv_w, a_log, dt_bias, gdn_norm_w, w_out, mlp_norm_w, w_up, w_down, loss_target, m_attn_norm_w, m_w_in, m_q_lat_norm_w, m_w_uq, m_kv_lat_norm_w, m_w_ukv, m_q_norm_w, m_k_norm_w, m_mla_out_norm_w, m_conv_w, m_a_log, m_dt_bias, m_gdn_norm_w, m_w_out, m_mlp_norm_w, m_w_up, m_w_down, v_attn_norm_w, v_w_in, v_q_lat_norm_w, v_w_uq, v_kv_lat_norm_w, v_w_ukv, v_q_norm_w, v_k_norm_w, v_mla_out_norm_w, v_conv_w, v_a_log, v_dt_bias, v_gdn_norm_w, v_w_out, v_mlp_norm_w, v_w_up, v_w_down):
    env = dict(locals())
    W = {n: env[n][0] for n in ALL_W}
    Mo = {n: env["m_" + n][0] for n in ALL_W}
    Vo = {n: env["v_" + n][0] for n in ALL_W}

    two_d = lambda a: a.reshape(1, -1) if a.ndim == 1 else a
    D = x.shape[-1]

    s_in, s_uq, s_ukv, s_conv = _gather_weights([
        _pad2(W["w_in"].astype(WIRE_DTYPE), *WIRE_SHAPE["w_in"]),
        _pad2(W["w_uq"].astype(WIRE_DTYPE), *WIRE_SHAPE["w_uq"]),
        W["w_ukv"].astype(WIRE_DTYPE), _pad2(W["conv_w"], *WIRE_SHAPE["conv_w"])])
    late = [W["w_out"].astype(WIRE_DTYPE), W["w_up"].astype(WIRE_DTYPE), W["w_down"].astype(WIRE_DTYPE)]

    loss, grad_x, parts, gs = _local_step(
        x, positions, loss_target, two_d(W["attn_norm_w"]), _cols_to_full(s_in, W["w_in"].shape[1]),
        two_d(W["q_lat_norm_w"]), _cols_to_full(s_uq, W["w_uq"].shape[1]), two_d(W["kv_lat_norm_w"]),
        _cols_to_full(s_ukv, W["w_ukv"].shape[1]), two_d(W["q_norm_w"]), two_d(W["k_norm_w"]),
        W["mla_out_norm_w"], _cols_to_full(s_conv[:, :CONV_W], W["conv_w"].shape[1]), two_d(W["a_log"]),
        two_d(W["dt_bias"]), two_d(W["gdn_norm_w"]), None, two_d(W["mlp_norm_w"]), None, None,
        late_shards=late, exchange=True)
    loss = lax.psum(loss, ("x", "y", "c"))
    done = {n: _reduce_adamw(parts[n], W[n], Mo[n], Vo[n], "adamw_" + n) for n in BIG}
    names = [n for n, *_ in SMALL_LAYOUT]
    small = _adamw_replicated(_gather_small_grads(gs), [two_d(W[n]) for n in names], [two_d(Mo[n]) for n in names],
                              [two_d(Vo[n]) for n in names])
    for i, n in enumerate(names):
        done[n] = [small[kind][i] for kind in range(4)]
    res = [done[n][kind].reshape(env[n].shape) for kind in range(4) for n in ALL_W]
    return (loss, grad_x, *res)
```

```python
import functools

import jax
import jax.numpy as jnp
from jax import lax
from jax.experimental import pallas as pl
from jax.experimental.pallas import tpu as pltpu

F32 = jnp.float32
MXU_DTYPE = jnp.bfloat16
WIRE_DTYPE = jnp.bfloat16
SDS = jax.ShapeDtypeStruct
HIGHEST = lax.Precision.HIGHEST
MESH_ID = pl.DeviceIdType.MESH

D_MODEL = 1024
MLA_HEADS = 4
Q_LORA = 256
KV_LORA = 256
NOPE = 128
ROPE = 64
QK_DIM = NOPE + ROPE
V_DIM = 128
ROPE_THETA = 10000.0
GDN_HEADS = 4
GDN_DIM = 128
GDN_WIDTH = GDN_HEADS * GDN_DIM
CONV_W = 4
CHUNK = 64
D_FF = 4 * D_MODEL
EPS = 1e-6
ATT_SCALE = QK_DIM ** -0.5
GDN_QSCALE = GDN_DIM ** -0.5
N_DEV = 8
ATTN_BLOCK = 512
ATTN_CHAINS = 2

ADAM_LR = 0.001
ADAM_B1 = 0.9
ADAM_B2 = 0.999
ADAM_EPS = 1e-08
ADAM_WD = 0.01
ADAM_STEP = 10

LANES = 128
SUBLANES = 8
VMEM_LIMIT = 56 * 1024 * 1024

P_GQKV, P_GZ, P_QLAT, P_KVLAT, P_KPE, P_GAB = 0, 1536, 2048, 2304, 2560, 2688
P_WIDTH = 2816
O_QLAT, O_KVLAT, O_KPE, O_GQKV, O_GZ, O_GAB, O_END = 0, 256, 512, 576, 2112, 2624, 2632


def _params(sem=None, vmem=VMEM_LIMIT):
    kw = dict(vmem_limit_bytes=vmem)
    if sem is not None:
        kw["dimension_semantics"] = sem
    return pltpu.CompilerParams(**kw)


def _mm(a, b):
    return jnp.dot(a.astype(MXU_DTYPE), b.astype(MXU_DTYPE), preferred_element_type=F32)


def _mm_nt(a, b):
    return lax.dot_general(a.astype(MXU_DTYPE), b.astype(MXU_DTYPE), (((1,), (1,)), ((), ())),
                           preferred_element_type=F32)


def _mm_tn(a, b):
    return lax.dot_general(a.astype(MXU_DTYPE), b.astype(MXU_DTYPE), (((0,), (0,)), ((), ())),
                           preferred_element_type=F32)


def _split(a):
    hi = a.astype(MXU_DTYPE)
    return hi, (a - hi.astype(F32)).astype(MXU_DTYPE)


def _mm_split(a, b):
    (ah, al), (bh, bl) = a, b
    dot = lambda x, y: jnp.dot(x, y, preferred_element_type=F32)
    if MXU_DTYPE == F32:
        return dot(ah, bh)
    return dot(ah, bh) + dot(ah, bl) + dot(al, bh)


def _mm_exact(a, b):
    return _mm_split(_split(a), _split(b))


def _rms(x, w):
    r = lax.rsqrt(jnp.mean(x * x, axis=-1, keepdims=True) + EPS)
    return x * r * w, r


def _rms_bwd(dy, x, w, r):
    xh = x * r
    dyw = dy * w
    dx = r * (dyw - xh * jnp.mean(dyw * xh, axis=-1, keepdims=True))
    dw = jnp.sum(dy * xh, axis=0, keepdims=True)
    return dx, dw


def _l2n_bwd(dy, x, scale):
    r = lax.rsqrt(jnp.sum(x * x, axis=-1, keepdims=True) + EPS)
    xh = x * r
    return (scale * r) * (dy - xh * jnp.sum(dy * xh, axis=-1, keepdims=True))


def _rot(t):
    return jnp.concatenate([-t[:, ROPE // 2:], t[:, :ROPE // 2]], axis=-1)


def _rot_t(t):
    return jnp.concatenate([t[:, ROPE // 2:], -t[:, :ROPE // 2]], axis=-1)


def _rope(t, cos, sin):
    return t * cos + _rot(t) * sin


def _rope_bwd(d, cos, sin):
    return d * cos + _rot_t(d * sin)


def _sigmoid(x):
    return jax.nn.sigmoid(x)


def _shift_down(x, halo, j):
    if j == 0:
        return x
    xr = pltpu.roll(x, j, 0)
    hr = pltpu.roll(halo, j, 0)
    row = lax.broadcasted_iota(jnp.int32, halo.shape, 0)
    top = jnp.where(row < j, hr, xr[:SUBLANES])
    return jnp.concatenate([top, xr[SUBLANES:]], axis=0)


def _shift_up(x, nxt, j):
    if j == 0:
        return x
    n = x.shape[0]
    xr = pltpu.roll(x, n - j, 0)
    nr = pltpu.roll(nxt, SUBLANES - j, 0)
    row = lax.broadcasted_iota(jnp.int32, nxt.shape, 0)
    bot = jnp.where(row >= SUBLANES - j, nr, xr[n - SUBLANES:])
    return jnp.concatenate([xr[:n - SUBLANES], bot], axis=0)


def _chunk_cumsum(y, row_in_chunk):
    s = 1
    while s < CHUNK:
        y = y + jnp.where(row_in_chunk >= s, pltpu.roll(y, s, 0), 0.0)
        s *= 2
    return y


def _chunk_rev_cumsum(y, row_in_chunk):
    n = y.shape[0]
    s = 1
    while s < CHUNK:
        y = y + jnp.where(row_in_chunk + s < CHUNK, pltpu.roll(y, n - s, 0), 0.0)
        s *= 2
    return y


def _lockstep(generators):
    alive = list(generators)
    while alive:
        nxt = []
        for g in alive:
            try:
                next(g)
                nxt.append(g)
            except StopIteration:
                pass
        alive = nxt


def _pick_lane(tile, lane, idx):
    return jnp.sum(jnp.where(lane == idx, tile, 0.0), axis=-1, keepdims=True)


def _divisor_tile(n, cap, unit=LANES):
    best = unit
    t = unit
    while t <= min(n, cap):
        if n % t == 0:
            best = t
        t += unit
    return n if n <= cap else best


def _in_proj(x2, w_an, w_in_p):
    T, D = x2.shape
    N = w_in_p.shape[1]
    tm = min(512, T)

    def body(x_ref, wn_ref, w_ref, proj_ref, xn_ref):
        xn, _ = _rms(x_ref[...], wn_ref[...])
        xn = xn.astype(MXU_DTYPE)
        xn_ref[...] = xn
        proj_ref[...] = jnp.dot(xn, w_ref[...], preferred_element_type=F32)

    return pl.pallas_call(
        body, grid=(T // tm,), name="in_proj",
        in_specs=[pl.BlockSpec((tm, D), lambda i: (i, 0)), pl.BlockSpec((1, D), lambda i: (0, 0)),
                  pl.BlockSpec((D, N), lambda i: (0, 0))],
        out_specs=[pl.BlockSpec((tm, N), lambda i: (i, 0)), pl.BlockSpec((tm, D), lambda i: (i, 0))],
        out_shape=[SDS((T, N), F32), SDS((T, D), MXU_DTYPE)],
        compiler_params=_params(("arbitrary",)),
    )(x2, w_an, w_in_p)


def _mla_pre(proj, cosf, sinf, w_qln, w_kvln, w_uq_p, w_ukv, qnw, knw):
    T = proj.shape[0]
    tm = min(256, T)
    H = MLA_HEADS

    def body(ql_ref, kvl_ref, kpe_ref, cos_ref, sin_ref, wq_ref, wkv_ref, uq_ref, ukv_ref, qnw_ref, knw_ref,
             q_out, k_out, v_out):
        cos, sin = cos_ref[...], sin_ref[...]
        qnw_, knw_ = qnw_ref[...], knw_ref[...]
        qn, _ = _rms(ql_ref[...], wq_ref[...])
        kvn, _ = _rms(kvl_ref[...], wkv_ref[...])
        qraw = _mm(qn, uq_ref[...])
        kvraw = _mm(kvn, ukv_ref[...])
        kpe = _rope(_rms(kpe_ref[...][:, :ROPE], knw_[:, NOPE:])[0], cos, sin)
        for h in range(H):
            qn_h = _rms(qraw[:, h * NOPE:(h + 1) * NOPE], qnw_[:, :NOPE])[0]
            qp_h = _rope(_rms(qraw[:, H * NOPE + h * ROPE:H * NOPE + (h + 1) * ROPE], qnw_[:, NOPE:])[0], cos, sin)
            q_out[h] = (jnp.concatenate([qn_h, qp_h], axis=-1) * ATT_SCALE).astype(MXU_DTYPE)
            kn_h = _rms(kvraw[:, h * 256:h * 256 + NOPE], knw_[:, :NOPE])[0]
            k_out[h] = jnp.concatenate([kn_h, kpe], axis=-1).astype(MXU_DTYPE)
            v_out[h] = kvraw[:, h * 256 + NOPE:(h + 1) * 256].astype(MXU_DTYPE)

    full = lambda a: pl.BlockSpec(a.shape, lambda i: (0,) * a.ndim)
    return pl.pallas_call(
        body, grid=(T // tm,), name="mla_pre",
        in_specs=[pl.BlockSpec((tm, 256), lambda i: (i, P_QLAT // 256)),
                  pl.BlockSpec((tm, 256), lambda i: (i, P_KVLAT // 256)),
                  pl.BlockSpec((tm, 128), lambda i: (i, P_KPE // 128)),
                  pl.BlockSpec((tm, ROPE), lambda i: (i, 0)), pl.BlockSpec((tm, ROPE), lambda i: (i, 0)),
                  full(w_qln), full(w_kvln), full(w_uq_p), full(w_ukv), full(qnw), full(knw)],
        out_specs=[pl.BlockSpec((H, tm, QK_DIM), lambda i: (0, i, 0)),
                   pl.BlockSpec((H, tm, QK_DIM), lambda i: (0, i, 0)),
                   pl.BlockSpec((H, tm, V_DIM), lambda i: (0, i, 0))],
        out_shape=[SDS((H, T, QK_DIM), MXU_DTYPE), SDS((H, T, QK_DIM), MXU_DTYPE), SDS((H, T, V_DIM), MXU_DTYPE)],
        compiler_params=_params(("arbitrary",)),
    )(proj, proj, proj, cosf, sinf, w_qln, w_kvln, w_uq_p, w_ukv, qnw, knw)


def _attn_fwd(q4, k4, v4, B, S, transfer=None):
    H = MLA_HEADS
    bq = min(ATTN_BLOCK, S)
    nq = S // bq
    rows = bq // ATTN_CHAINS

    def body(q_ref, k_ref, v_ref, o_ref, lse_ref):
        col = lax.broadcasted_iota(jnp.int32, (rows, bq), 1)
        row = lax.broadcasted_iota(jnp.int32, (rows, bq), 0)

        def q_step(qi, carry):
            qs = pl.multiple_of(qi * bq, bq)
            qsub = [q_ref[0, pl.ds(qs + j * rows, rows), :] for j in range(ATTN_CHAINS)]

            def k_block(ks, cs, diagonal):
                k = k_ref[0, pl.ds(ks, bq), :]
                v = v_ref[0, pl.ds(ks, bq), :]
                out = [None] * ATTN_CHAINS

                def chain(j):
                    m, l, acc = cs[j]
                    s = _mm_nt(qsub[j], k)
                    yield
                    if diagonal:
                        s = jnp.where(col <= row + j * rows, s, -jnp.inf)
                    m_new = jnp.maximum(m, jnp.max(s, axis=-1, keepdims=True))
                    p = jnp.exp(s - m_new)
                    a = jnp.exp(m - m_new)
                    l_new = a * l + jnp.sum(p, axis=-1, keepdims=True)
                    yield
                    out[j] = (m_new, l_new, a * acc + _mm(p, v))

                _lockstep([chain(j) for j in range(ATTN_CHAINS)])
                return tuple(out)

            init = tuple((jnp.full((rows, 1), -jnp.inf, F32), jnp.zeros((rows, 1), F32),
                          jnp.zeros((rows, V_DIM), F32)) for _ in range(ATTN_CHAINS))
            cs = lax.fori_loop(0, qi, lambda kj, c: k_block(pl.multiple_of(kj * bq, bq), c, False), init)
            for j, (m, l, acc) in enumerate(k_block(qs, cs, True)):
                o_ref[0, pl.ds(qs + j * rows, rows), :] = acc / l
                lse_ref[0, pl.ds(qs + j * rows, rows), :] = m + jnp.log(l)
            return carry

        lax.fori_loop(0, nq, q_step, 0)

    spec = lambda d: pl.BlockSpec((1, S, d), lambda h, b: (h, b, 0))
    return _call_beside(
        body, transfer, grid=(H, B), name="attn_fwd",
        in_specs=[spec(QK_DIM), spec(QK_DIM), spec(V_DIM)],
        out_specs=[spec(V_DIM), spec(1)],
        out_shape=[SDS((H, B * S, V_DIM), F32), SDS((H, B * S, 1), F32)],
        scratch_shapes=[], semantics=("arbitrary", "arbitrary"), args=(q4, k4, v4))


def _conv_taps(u, halo, w):
    sh = [_shift_down(u, halo, j) for j in range(CONV_W)]
    c = w[0:1] * sh[3] + w[1:2] * sh[2] + w[2:3] * sh[1] + w[3:4] * sh[0]
    return c, sh


def _gate_values(gab, alog_l, dt_l, lane):
    g = -jnp.exp(alog_l) * jax.nn.softplus(gab + dt_l)
    g = jnp.where(lane < GDN_HEADS, g, 0.0)
    beta = jnp.where((lane >= GDN_HEADS) & (lane < 2 * GDN_HEADS), _sigmoid(gab), 0.0)
    return g, beta


def _gdn_pre(proj, conv_w, alog_l, dt_l, S):
    T = proj.shape[0]
    tm = min(256, T)
    tiles_per_seq = S // tm
    C3 = 3 * GDN_WIDTH
    H = GDN_HEADS

    def body(u_ref, halo_ref, gab_ref, w_ref, alog_ref, dt_ref, q_out, k_out, v_out, gates_out):
        i = pl.program_id(0)
        halo = jnp.where(i % tiles_per_seq == 0, 0.0, halo_ref[...])
        c, _ = _conv_taps(u_ref[...], halo, w_ref[...])
        a = c * _sigmoid(c)
        for h in range(H):
            xq = a[:, h * GDN_DIM:(h + 1) * GDN_DIM]
            xk = a[:, GDN_WIDTH + h * GDN_DIM:GDN_WIDTH + (h + 1) * GDN_DIM]
            q_out[h] = xq * lax.rsqrt(jnp.sum(xq * xq, axis=-1, keepdims=True) + EPS) * GDN_QSCALE
            k_out[h] = xk * lax.rsqrt(jnp.sum(xk * xk, axis=-1, keepdims=True) + EPS)
            v_out[h] = a[:, 2 * GDN_WIDTH + h * GDN_DIM:2 * GDN_WIDTH + (h + 1) * GDN_DIM]
        lane = lax.broadcasted_iota(jnp.int32, (tm, LANES), 1)
        ric = lax.broadcasted_iota(jnp.int32, (tm, LANES), 0) % CHUNK
        g, beta = _gate_values(gab_ref[...], alog_ref[...], dt_ref[...], lane)
        gates_out[...] = _chunk_cumsum(g, ric) + beta

    hspec = pl.BlockSpec((H, tm, GDN_DIM), lambda i: (0, i, 0))
    return pl.pallas_call(
        body, grid=(T // tm,), name="gdn_pre",
        in_specs=[pl.BlockSpec((tm, C3), lambda i: (i, 0)),
                  pl.BlockSpec((SUBLANES, C3), lambda i: (jnp.maximum(i * (tm // SUBLANES) - 1, 0), 0)),
                  pl.BlockSpec((tm, LANES), lambda i: (i, P_GAB // LANES)),
                  pl.BlockSpec((CONV_W, C3), lambda i: (0, 0)),
                  pl.BlockSpec((1, LANES), lambda i: (0, 0)), pl.BlockSpec((1, LANES), lambda i: (0, 0))],
        out_specs=[hspec, hspec, hspec, pl.BlockSpec((tm, LANES), lambda i: (i, 0))],
        out_shape=[SDS((H, T, GDN_DIM), F32)] * 3 + [SDS((T, LANES), F32)],
        compiler_params=_params(("arbitrary",)),
    )(proj, proj, proj, conv_w, alog_l, dt_l)


def _unit_lower_inverses(Ls, eye):
    Ps = [eye - L for L in Ls]
    Ms = [_split(-L) for L in Ls]
    for _ in range(5):
        sq = [_mm_split(m, m) for m in Ms]
        Ms = [_split(s) for s in sq]
        Ps = [p + _mm_split(_split(p), m) for p, m in zip(Ps, Ms)]
    return Ps


def _chunk_decays(gt, lane, h, ri, ci, rcol):
    Gc = _pick_lane(gt, lane, h)
    bt = _pick_lane(gt, lane, h + GDN_HEADS)
    Gb = jnp.broadcast_to(Gc, (CHUNK, CHUNK))
    Gam = jnp.where(ri >= ci, jnp.exp(Gb - Gb.T), 0.0)
    Gl = jnp.sum(jnp.where(rcol == CHUNK - 1, Gc, 0.0), axis=0, keepdims=True)
    return Gc, bt, Gam, jnp.exp(Gc), jnp.exp(Gl - Gc), jnp.exp(Gl)


GDN_UNROLL = 4


def _gdn_fwd(qg, kg, vg, gates, B, S, transfer=None):
    H, D, C = GDN_HEADS, GDN_DIM, CHUNK
    NC = S // C
    U = GDN_UNROLL if NC % GDN_UNROLL == 0 else 1

    def body(q_ref, k_ref, v_ref, g_ref, o_ref, st_ref, ai_ref, u_ref, w_ref, q2_s, au_s, bc_s, w2_s, el_s):
        h = pl.program_id(0)
        lane = lax.broadcasted_iota(jnp.int32, (C, LANES), 1)
        ri = lax.broadcasted_iota(jnp.int32, (C, C), 0)
        ci = lax.broadcasted_iota(jnp.int32, (C, C), 1)
        rcol = lax.broadcasted_iota(jnp.int32, (C, 1), 0)
        eye = (ri == ci).astype(F32)

        def group(gi, c):
            ns = [gi * U + j for j in range(U)]
            css = [pl.multiple_of(n * C, C) for n in ns]
            qs = [q_ref[0, pl.ds(cs, C), :] for cs in css]
            ks = [k_ref[0, pl.ds(cs, C), :] for cs in css]
            vs = [v_ref[0, pl.ds(cs, C), :] for cs in css]
            decs = [_chunk_decays(g_ref[pl.ds(cs, C), :], lane, h, ri, ci, rcol) for cs in css]
            qks = [_mm_nt(jnp.concatenate([q, k], axis=0), k) for q, k in zip(qs, ks)]
            ainvs = _unit_lower_inverses(
                [jnp.where(ri > ci, d[1] * qk[C:] * d[2], 0.0) for qk, d in zip(qks, decs)], eye)
            sols = [_mm_exact(a, jnp.concatenate([v * d[1], k * (d[1] * d[3])], axis=-1))
                    for a, k, v, d in zip(ainvs, ks, vs, decs)]
            atuw = [_mm(qk[:C] * d[2], sol) for qk, d, sol in zip(qks, decs, sols)]
            kduw = [_mm_tn(k * d[4], sol) for k, d, sol in zip(ks, decs, sols)]
            for n, cs, q, a, sol, au, ku, (Gc, bt, Gam, e, f, eL) in zip(ns, css, qs, ainvs, sols, atuw, kduw, decs):
                u_ref[0, pl.ds(cs, C), :] = sol[:, :D]
                w_ref[0, pl.ds(cs, C), :] = sol[:, D:]
                au_s[pl.ds(cs, C), :] = au[:, :D]
                q2_s[pl.ds(cs, C), :] = q * e - au[:, D:]
                bc_s[n] = ku[:, :D]
                w2_s[n] = ku[:, D:]
                el_s[n] = jnp.broadcast_to(eL, (SUBLANES, LANES))
                ai_ref[0, n] = a
            return c

        lax.fori_loop(0, NC // U, group, 0)

        def step(n, S_):
            cs = pl.multiple_of(n * C, C)
            o_ref[0, pl.ds(cs, C), :] = _mm(q2_s[pl.ds(cs, C), :], S_) + au_s[pl.ds(cs, C), :]
            st_ref[0, n] = S_
            return S_ * el_s[n, 0:1, :] + bc_s[n] - _mm(w2_s[n], S_)

        lax.fori_loop(0, NC, step, jnp.zeros((D, D), F32))

    spec = pl.BlockSpec((1, S, D), lambda h, b: (h, b, 0))
    return _call_beside(
        body, transfer, grid=(H, B), name="gdn_fwd",
        in_specs=[spec, spec, spec, pl.BlockSpec((S, LANES), lambda h, b: (b, 0))],
        out_specs=[spec, pl.BlockSpec((1, NC, D, D), lambda h, b: (h, b, 0, 0)),
                   pl.BlockSpec((1, NC, C, C), lambda h, b: (h, b, 0, 0)), spec, spec],
        out_shape=[SDS((H, B * S, D), F32), SDS((H, B * NC, D, D), F32), SDS((H, B * NC, C, C), F32),
                   SDS((H, B * S, D), F32), SDS((H, B * S, D), F32)],
        scratch_shapes=[pltpu.VMEM((S, D), F32), pltpu.VMEM((S, D), F32), pltpu.VMEM((NC, D, D), F32),
                        pltpu.VMEM((NC, D, D), F32), pltpu.VMEM((NC, SUBLANES, LANES), F32)],
        semantics=("arbitrary", "arbitrary"), args=(qg, kg, vg, gates))


def _mix_out(o_mla, o_gdn, proj, x2, mla_w, gdn_w, w_out):
    T, D = x2.shape
    tm = min(512, T)
    H = MLA_HEADS

    def body(om_ref, og_ref, z_ref, x_ref, mw_ref, gw_ref, w_ref, h_ref, mix_ref):
        z = z_ref[...]
        parts = [_rms(om_ref[h], mw_ref[h:h + 1, :])[0] for h in range(H)]
        for h in range(GDN_HEADS):
            zh = z[:, h * GDN_DIM:(h + 1) * GDN_DIM]
            parts.append(_rms(og_ref[h], gw_ref[...])[0] * (zh * _sigmoid(zh)))
        mix = jnp.concatenate(parts, axis=-1).astype(MXU_DTYPE)
        mix_ref[...] = mix
        h_ref[...] = x_ref[...] + jnp.dot(mix, w_ref[...], preferred_element_type=F32)

    hspec = pl.BlockSpec((H, tm, V_DIM), lambda i: (0, i, 0))
    return pl.pallas_call(
        body, grid=(T // tm,), name="mix_out",
        in_specs=[hspec, hspec, pl.BlockSpec((tm, GDN_WIDTH), lambda i: (i, P_GZ // GDN_WIDTH)),
                  pl.BlockSpec((tm, D), lambda i: (i, 0)),
                  pl.BlockSpec((H, V_DIM), lambda i: (0, 0)), pl.BlockSpec((1, GDN_DIM), lambda i: (0, 0)),
                  pl.BlockSpec((D, D), lambda i: (0, 0))],
        out_specs=[pl.BlockSpec((tm, D), lambda i: (i, 0)), pl.BlockSpec((tm, D), lambda i: (i, 0))],
        out_shape=[SDS((T, D), F32), SDS((T, D), MXU_DTYPE)],
        compiler_params=_params(("arbitrary",)),
    )(o_mla, o_gdn, proj, x2, mla_w, gdn_w, w_out)


def _mlp_fwd(h2, w_mn, w_up, w_down, target):
    T, D = h2.shape
    nf, _, tf = w_up.shape
    F = nf * tf
    tm = min(512, T)

    def body(h_ref, wn_ref, up_w, down_w, t_ref, up_ref, hn_ref, dy_ref, loss_ref, y_acc):
        j = pl.program_id(1)

        @pl.when(j == 0)
        def _():
            hn_ref[...] = _rms(h_ref[...], wn_ref[...])[0].astype(MXU_DTYPE)
            y_acc[...] = h_ref[...]

        up = jnp.dot(hn_ref[...], up_w[0], preferred_element_type=F32)
        up_ref[...] = up
        r = jnp.maximum(up, 0.0)
        y_acc[...] += _mm(r * r, down_w[...])

        @pl.when(j == nf - 1)
        def _():
            err = y_acc[...] - t_ref[...]
            dy_ref[...] = err / D
            loss_ref[...] = jnp.full((1, SUBLANES, LANES), jnp.sum(err * err), F32)

    return pl.pallas_call(
        body, grid=(T // tm, nf), name="mlp_fwd",
        in_specs=[pl.BlockSpec((tm, D), lambda i, j: (i, 0)), pl.BlockSpec((1, D), lambda i, j: (0, 0)),
                  pl.BlockSpec((1, D, tf), lambda i, j: (j, 0, 0)), pl.BlockSpec((tf, D), lambda i, j: (j, 0)),
                  pl.BlockSpec((tm, D), lambda i, j: (i, 0))],
        out_specs=[pl.BlockSpec((tm, tf), lambda i, j: (i, j)), pl.BlockSpec((tm, D), lambda i, j: (i, 0)),
                   pl.BlockSpec((tm, D), lambda i, j: (i, 0)),
                   pl.BlockSpec((1, SUBLANES, LANES), lambda i, j: (i, 0, 0))],
        out_shape=[SDS((T, F), F32), SDS((T, D), MXU_DTYPE), SDS((T, D), F32),
                   SDS((T // tm, SUBLANES, LANES), F32)],
        scratch_shapes=[pltpu.VMEM((tm, D), F32)],
        compiler_params=_params(("arbitrary", "arbitrary")),
    )(h2, w_mn, w_up, w_down, target)


def _mlp_bwd(dy, up, h2, w_mn, w_up, w_down):
    T, D = h2.shape
    nf, _, tf = w_up.shape
    F = nf * tf
    tm = min(512, T)

    def body(dy_ref, up_ref, h_ref, wn_ref, up_w, down_w, dh_ref, dhb_ref, dup_ref, act_ref, dyb_ref, dwn_ref, acc):
        i, j = pl.program_id(0), pl.program_id(1)

        @pl.when((i == 0) & (j == 0))
        def _():
            dwn_ref[...] = jnp.zeros_like(dwn_ref)

        @pl.when(j == 0)
        def _():
            acc[...] = jnp.zeros_like(acc)
            dyb_ref[...] = dy_ref[...].astype(MXU_DTYPE)

        r = jnp.maximum(up_ref[...], 0.0)
        act_ref[...] = (r * r).astype(MXU_DTYPE)
        dup = (_mm_nt(dyb_ref[...], down_w[...]) * (2.0 * r)).astype(MXU_DTYPE)
        dup_ref[...] = dup
        acc[...] += _mm_nt(dup, up_w[0])

        @pl.when(j == nf - 1)
        def _():
            hv = h_ref[...]
            _, rr = _rms(hv, wn_ref[...])
            dx, dw = _rms_bwd(acc[...], hv, wn_ref[...], rr)
            dh = dy_ref[...] + dx
            dh_ref[...] = dh
            dhb_ref[...] = dh.astype(MXU_DTYPE)
            dwn_ref[...] += dw

    row = lambda i, j: (i, 0)
    return pl.pallas_call(
        body, grid=(T // tm, nf), name="mlp_bwd",
        in_specs=[pl.BlockSpec((tm, D), row), pl.BlockSpec((tm, tf), lambda i, j: (i, j)), pl.BlockSpec((tm, D), row),
                  pl.BlockSpec((1, D), lambda i, j: (0, 0)),
                  pl.BlockSpec((1, D, tf), lambda i, j: (j, 0, 0)), pl.BlockSpec((tf, D), lambda i, j: (j, 0))],
        out_specs=[pl.BlockSpec((tm, D), row), pl.BlockSpec((tm, D), row),
                   pl.BlockSpec((tm, tf), lambda i, j: (i, j)), pl.BlockSpec((tm, tf), lambda i, j: (i, j)),
                   pl.BlockSpec((tm, D), row), pl.BlockSpec((1, D), lambda i, j: (0, 0))],
        out_shape=[SDS((T, D), F32), SDS((T, D), MXU_DTYPE), SDS((T, F), MXU_DTYPE), SDS((T, F), MXU_DTYPE),
                   SDS((T, D), MXU_DTYPE), SDS((1, D), F32)],
        scratch_shapes=[pltpu.VMEM((tm, D), F32)],
        compiler_params=_params(("arbitrary", "arbitrary")),
    )(dy, up, h2, w_mn, w_up, w_down)


def _mix_bwd(dhb, o_mla, o_gdn, proj, mla_w, gdn_w, w_out):
    T, D = dhb.shape
    tm = min(512, T)
    H = MLA_HEADS

    def body(dh_ref, om_ref, og_ref, z_ref, mw_ref, gw_ref, w_ref, dom_ref, dog_ref, dz_ref, dmw_ref, dgw_ref):
        @pl.when(pl.program_id(0) == 0)
        def _():
            dmw_ref[...] = jnp.zeros_like(dmw_ref)
            dgw_ref[...] = jnp.zeros_like(dgw_ref)

        dmix = _mm_nt(dh_ref[...], w_ref[...])
        z = z_ref[...]
        dmw, dzs = [], []
        dgw = jnp.zeros((1, GDN_DIM), F32)
        for h in range(H):
            o = om_ref[h]
            w = mw_ref[h:h + 1, :]
            _, r = _rms(o, w)
            dx, dw = _rms_bwd(dmix[:, h * V_DIM:(h + 1) * V_DIM], o, w, r)
            dom_ref[h] = dx
            dmw.append(dw)
        for h in range(GDN_HEADS):
            o = og_ref[h]
            w = gw_ref[...]
            zh = z[:, h * GDN_DIM:(h + 1) * GDN_DIM]
            sg = _sigmoid(zh)
            yn, r = _rms(o, w)
            dy = dmix[:, H * V_DIM + h * GDN_DIM:H * V_DIM + (h + 1) * GDN_DIM]
            dzs.append(dy * yn * (sg * (1.0 + zh * (1.0 - sg))))
            dx, dw = _rms_bwd(dy * (zh * sg), o, w, r)
            dog_ref[h] = dx
            dgw = dgw + dw
        dz_ref[...] = jnp.concatenate(dzs, axis=-1)
        dmw_ref[...] += jnp.concatenate(dmw, axis=0)
        dgw_ref[...] += dgw

    hspec = pl.BlockSpec((H, tm, V_DIM), lambda i: (0, i, 0))
    return pl.pallas_call(
        body, grid=(T // tm,), name="mix_bwd",
        in_specs=[pl.BlockSpec((tm, D), lambda i: (i, 0)), hspec, hspec,
                  pl.BlockSpec((tm, GDN_WIDTH), lambda i: (i, P_GZ // GDN_WIDTH)),
                  pl.BlockSpec((H, V_DIM), lambda i: (0, 0)), pl.BlockSpec((1, GDN_DIM), lambda i: (0, 0)),
                  pl.BlockSpec((D, D), lambda i: (0, 0))],
        out_specs=[hspec, hspec, pl.BlockSpec((tm, GDN_WIDTH), lambda i: (i, 0)),
                   pl.BlockSpec((H, V_DIM), lambda i: (0, 0)), pl.BlockSpec((1, GDN_DIM), lambda i: (0, 0))],
        out_shape=[SDS((H, T, V_DIM), F32), SDS((H, T, GDN_DIM), F32), SDS((T, GDN_WIDTH), F32),
                   SDS((H, V_DIM), F32), SDS((1, GDN_DIM), F32)],
        compiler_params=_params(("arbitrary",)),
    )(dhb, o_mla, o_gdn, proj, mla_w, gdn_w, w_out)


def _attn_bwd(q4, k4, v4, do4, o4, lse4, B, S, transfer=None):
    H = MLA_HEADS
    bq = min(ATTN_BLOCK, S)
    nq = S // bq
    rows = bq // ATTN_CHAINS

    def body(q_ref, k_ref, v_ref, do_ref, o_ref, lse_ref, dq_ref, dk_ref, dv_ref, delta):
        dq_ref[...] = jnp.zeros_like(dq_ref)
        dk_ref[...] = jnp.zeros_like(dk_ref)
        dv_ref[...] = jnp.zeros_like(dv_ref)
        delta[...] = jnp.sum(do_ref[0] * o_ref[0], axis=-1, keepdims=True)

        col = lax.broadcasted_iota(jnp.int32, (rows, bq), 1)
        row = lax.broadcasted_iota(jnp.int32, (rows, bq), 0)

        def k_step(kj, carry):
            ks = pl.multiple_of(kj * bq, bq)
            k = k_ref[0, pl.ds(ks, bq), :]
            v = v_ref[0, pl.ds(ks, bq), :]

            def q_block(qs, diagonal):
                dks, dvs = [None] * ATTN_CHAINS, [None] * ATTN_CHAINS

                def chain(j):
                    sl = pl.ds(qs + j * rows, rows)
                    q = q_ref[0, sl, :]
                    do = do_ref[0, sl, :].astype(MXU_DTYPE)
                    s = _mm_nt(q, k)
                    dp = _mm_nt(do, v)
                    yield
                    p = jnp.exp(s - lse_ref[0, sl, :])
                    if diagonal:
                        p = jnp.where(col <= row + j * rows, p, 0.0)
                    ds = p * (dp - delta[sl, :])
                    yield
                    dvs[j] = _mm_tn(p, do)
                    dks[j] = _mm_tn(ds, q)
                    dq_ref[0, sl, :] += _mm(ds, k)

                _lockstep([chain(j) for j in range(ATTN_CHAINS)])
                dv_ref[0, pl.ds(ks, bq), :] += functools.reduce(jnp.add, dvs)
                dk_ref[0, pl.ds(ks, bq), :] += functools.reduce(jnp.add, dks)

            q_block(ks, True)

            def q_step(qi, c):
                q_block(pl.multiple_of(qi * bq, bq), False)
                return c

            lax.fori_loop(kj + 1, nq, q_step, 0)
            return carry

        lax.fori_loop(0, nq, k_step, 0)

    spec = lambda d: pl.BlockSpec((1, S, d), lambda h, b: (h, b, 0))
    return _call_beside(
        body, transfer, grid=(H, B), name="attn_bwd",
        in_specs=[spec(QK_DIM), spec(QK_DIM), spec(V_DIM), spec(V_DIM), spec(V_DIM), spec(1)],
        out_specs=[spec(QK_DIM), spec(QK_DIM), spec(V_DIM)],
        out_shape=[SDS((H, B * S, QK_DIM), F32), SDS((H, B * S, QK_DIM), F32), SDS((H, B * S, V_DIM), F32)],
        scratch_shapes=[pltpu.VMEM((S, 1), F32)], semantics=("arbitrary", "arbitrary"),
        args=(q4, k4, v4, do4, o4, lse4))


def _gdn_bwd(qg, kg, vg, gates, states, ainv, u4, w4, do4, B, S, transfer=None):
    H, D, C = GDN_HEADS, GDN_DIM, CHUNK
    NC = S // C
    U = GDN_UNROLL if NC % GDN_UNROLL == 0 else 1

    def body(q_ref, k_ref, v_ref, g_ref, st_ref, ai_ref, u_ref, w_ref, do_ref, dq_ref, dk_ref, dv_ref, dgb_ref,
             kd_s, x1_s, x2_s, el_s, dvn_s, ds_s, w2t_s):
        h = pl.program_id(0)
        lane = lax.broadcasted_iota(jnp.int32, (C, LANES), 1)
        ri = lax.broadcasted_iota(jnp.int32, (C, C), 0)
        ci = lax.broadcasted_iota(jnp.int32, (C, C), 1)
        rcol = lax.broadcasted_iota(jnp.int32, (C, 1), 0)

        def rsum(a):
            return jnp.sum(a, axis=-1, keepdims=True)

        def blocks(fn):
            def group(gi, c):
                _lockstep([fn(gi * U + j) for j in range(U)])
                return c
            lax.fori_loop(0, NC // U, group, 0)

        def prepare(n):
            cs = pl.multiple_of(n * C, C)
            q = q_ref[0, pl.ds(cs, C), :]
            k = k_ref[0, pl.ds(cs, C), :]
            do = do_ref[0, pl.ds(cs, C), :]
            Gc, bt, Gam, e, f, eL = _chunk_decays(g_ref[pl.ds(cs, C), :], lane, h, ri, ci, rcol)
            At = _mm_nt(q, k) * Gam
            yield
            x1 = _mm_tn(At, do)
            x2 = _mm_tn(q * e, do)
            kd = k * f
            w = w_ref[0, pl.ds(cs, C), :]
            yield
            x1_s[pl.ds(cs, C), :] = x1
            x2_s[n] = x2 - _mm_tn(w, x1)
            w2t_s[n] = _mm_tn(w, kd)
            kd_s[pl.ds(cs, C), :] = kd
            el_s[n] = jnp.broadcast_to(eL, (SUBLANES, LANES))

        blocks(prepare)

        def recur(t, dS):
            n = NC - 1 - t
            cs = pl.multiple_of(n * C, C)
            ds_s[n] = dS
            dvn_s[pl.ds(cs, C), :] = x1_s[pl.ds(cs, C), :] + _mm(kd_s[pl.ds(cs, C), :], dS)
            return x2_s[n] + el_s[n, 0:1, :] * dS - _mm(w2t_s[n], dS)

        lax.fori_loop(0, NC, recur, jnp.zeros((D, D), F32))

        def local(n):
            cs = pl.multiple_of(n * C, C)
            q = q_ref[0, pl.ds(cs, C), :]
            k = k_ref[0, pl.ds(cs, C), :]
            v = v_ref[0, pl.ds(cs, C), :]
            do = do_ref[0, pl.ds(cs, C), :]
            u = u_ref[0, pl.ds(cs, C), :]
            w = w_ref[0, pl.ds(cs, C), :]
            dvn = dvn_s[pl.ds(cs, C), :]
            dS = ds_s[n]
            Gc, bt, Gam, e, f, eL = _chunk_decays(g_ref[pl.ds(cs, C), :], lane, h, ri, ci, rcol)
            S0 = st_ref[0, n]
            Ainv = ai_ref[0, n]
            qk = _mm_nt(jnp.concatenate([q, k], axis=0), k)
            QK, KK = qk[:C], qk[C:]
            be = bt * e
            sol = jnp.concatenate([u, w], axis=-1)
            vn = u - _mm(w, S0)
            yield
            dAt = jnp.where(ri >= ci, _mm_nt(do, vn), 0.0)
            dqd = _mm_nt(do, S0)
            dw = -_mm_nt(dvn, S0)
            dkd = _mm_nt(vn, dS)
            deL = jnp.sum(rsum(dS * S0), axis=0, keepdims=True)
            yield
            dR = _mm_exact(Ainv.T, jnp.concatenate([dvn, dw], axis=-1))
            dR1, dR2 = dR[:, :D], dR[:, D:]
            yield
            dL = jnp.where(ri > ci, -_mm_nt(dR, sol), 0.0)
            yield
            dv_ref[0, pl.ds(cs, C), :] = dR1 * bt
            r2 = rsum(dR2 * k)
            X = dL * Gam
            dbt = rsum(dR1 * v) + r2 * e + rsum(X * KK)
            de = r2 * bt + rsum(dqd * q)
            dKK = X * bt
            dQK = dAt * Gam
            dq_ref[0, pl.ds(cs, C), :] = _mm(dQK, k) + dqd * e
            dk_ref[0, pl.ds(cs, C), :] = dR2 * be + _mm(dKK + dKK.T, k) + _mm_tn(dQK, q) + dkd * f
            df = rsum(dkd * k)
            Z = (dL * (bt * KK) + dAt * QK) * Gam
            dG = rsum(Z) - rsum(Z.T) + de * e - df * f
            dGl = jnp.sum(df * f, axis=0, keepdims=True) + deL * eL
            dG = dG + jnp.where(rcol == C - 1, dGl, 0.0)
            dgb_ref[0, pl.ds(cs, C), :] = jnp.where(lane == 0, dG, jnp.where(lane == 1, dbt, 0.0))

        blocks(local)

    spec = pl.BlockSpec((1, S, D), lambda h, b: (h, b, 0))
    return _call_beside(
        body, transfer, grid=(H, B), name="gdn_bwd",
        in_specs=[spec, spec, spec, pl.BlockSpec((S, LANES), lambda h, b: (b, 0)),
                  pl.BlockSpec((1, NC, D, D), lambda h, b: (h, b, 0, 0)),
                  pl.BlockSpec((1, NC, C, C), lambda h, b: (h, b, 0, 0)), spec, spec, spec],
        out_specs=[spec, spec, spec, spec],
        out_shape=[SDS((H, B * S, D), F32)] * 4,
        scratch_shapes=[pltpu.VMEM((S, D), F32), pltpu.VMEM((S, D), F32), pltpu.VMEM((NC, D, D), F32),
                        pltpu.VMEM((NC, SUBLANES, LANES), F32), pltpu.VMEM((S, D), F32),
                        pltpu.VMEM((NC, D, D), F32), pltpu.VMEM((NC, D, D), F32)],
        semantics=("arbitrary", "arbitrary"), args=(qg, kg, vg, gates, states, ainv, u4, w4, do4))


def _gdn_pre_bwd(proj, conv_w, alog_l, dt_l, dq4, dk4, dv4, dgb4, S):
    T = proj.shape[0]
    tm = min(256, T)
    tiles_per_seq = S // tm
    C3 = 3 * GDN_WIDTH
    H = GDN_HEADS

    def body(u_ref, halo_ref, gab_ref, w_ref, alog_ref, dt_ref, dq_ref, dk_ref, dv_ref, dgb_ref,
             dc_ref, dgab_ref, dcw_ref, dalog_ref, ddt_ref):
        i = pl.program_id(0)

        @pl.when(i == 0)
        def _():
            dcw_ref[...] = jnp.zeros_like(dcw_ref)
            dalog_ref[...] = jnp.zeros_like(dalog_ref)
            ddt_ref[...] = jnp.zeros_like(ddt_ref)

        halo = jnp.where(i % tiles_per_seq == 0, 0.0, halo_ref[...])
        c, sh = _conv_taps(u_ref[...], halo, w_ref[...])
        sg = _sigmoid(c)
        a = c * sg
        das = [None] * (3 * H)
        for h in range(H):
            xq = a[:, h * GDN_DIM:(h + 1) * GDN_DIM]
            xk = a[:, GDN_WIDTH + h * GDN_DIM:GDN_WIDTH + (h + 1) * GDN_DIM]
            das[h] = _l2n_bwd(dq_ref[h], xq, GDN_QSCALE)
            das[H + h] = _l2n_bwd(dk_ref[h], xk, 1.0)
            das[2 * H + h] = dv_ref[h]
        dc = jnp.concatenate(das, axis=-1) * (sg * (1.0 + c * (1.0 - sg)))
        dc_ref[...] = dc
        dcw_ref[...] += jnp.concatenate(
            [jnp.sum(dc * sh[CONV_W - 1 - t], axis=0, keepdims=True) for t in range(CONV_W)], axis=0)
        lane = lax.broadcasted_iota(jnp.int32, (tm, LANES), 1)
        ric = lax.broadcasted_iota(jnp.int32, (tm, LANES), 0) % CHUNK
        dG = jnp.zeros((tm, LANES), F32)
        for h in range(H):
            t = dgb_ref[h]
            dG = dG + jnp.where(lane == h, _pick_lane(t, lane, 0), 0.0) \
                    + jnp.where(lane == h + H, _pick_lane(t, lane, 1), 0.0)
        is_g = lane < H
        dg = jnp.where(is_g, _chunk_rev_cumsum(jnp.where(is_g, dG, 0.0), ric), 0.0)
        gab = gab_ref[...]
        g, beta = _gate_values(gab, alog_ref[...], dt_ref[...], lane)
        dga = jnp.where(is_g, dg * (-jnp.exp(alog_ref[...])) * _sigmoid(gab + dt_ref[...]), 0.0)
        dgb = jnp.where(is_g, 0.0, dG) * beta * (1.0 - beta)
        dgab_ref[...] = dga + dgb
        dalog_ref[...] += jnp.sum(dg * g, axis=0, keepdims=True)
        ddt_ref[...] += jnp.sum(dga, axis=0, keepdims=True)

    hspec = pl.BlockSpec((H, tm, GDN_DIM), lambda i: (0, i, 0))
    vec = pl.BlockSpec((1, LANES), lambda i: (0, 0))
    return pl.pallas_call(
        body, grid=(T // tm,), name="gdn_pre_bwd",
        in_specs=[pl.BlockSpec((tm, C3), lambda i: (i, 0)),
                  pl.BlockSpec((SUBLANES, C3), lambda i: (jnp.maximum(i * (tm // SUBLANES) - 1, 0), 0)),
                  pl.BlockSpec((tm, LANES), lambda i: (i, P_GAB // LANES)),
                  pl.BlockSpec((CONV_W, C3), lambda i: (0, 0)), vec, vec, hspec, hspec, hspec, hspec],
        out_specs=[pl.BlockSpec((tm, C3), lambda i: (i, 0)), pl.BlockSpec((tm, LANES), lambda i: (i, 0)),
                   pl.BlockSpec((CONV_W, C3), lambda i: (0, 0)), vec, vec],
        out_shape=[SDS((T, C3), F32), SDS((T, LANES), F32), SDS((CONV_W, C3), F32),
                   SDS((1, LANES), F32), SDS((1, LANES), F32)],
        compiler_params=_params(("arbitrary",)),
    )(proj, proj, proj, conv_w, alog_l, dt_l, dq4, dk4, dv4, dgb4)


def _conv_bwd_input(dc, conv_w, S):
    T, C3 = dc.shape
    tm = min(256, T)
    tiles_per_seq = S // tm
    nblk = T // SUBLANES

    def body(dc_ref, nxt_ref, w_ref, du_ref):
        i = pl.program_id(0)
        nxt = jnp.where(i % tiles_per_seq == tiles_per_seq - 1, 0.0, nxt_ref[...])
        x = dc_ref[...]
        w = w_ref[...]
        du = w[3:4] * x
        for j in range(1, CONV_W):
            du = du + w[3 - j:4 - j] * _shift_up(x, nxt, j)
        du_ref[...] = du

    return pl.pallas_call(
        body, grid=(T // tm,), name="conv_bwd_input",
        in_specs=[pl.BlockSpec((tm, C3), lambda i: (i, 0)),
                  pl.BlockSpec((SUBLANES, C3), lambda i: (jnp.minimum((i + 1) * (tm // SUBLANES), nblk - 1), 0)),
                  pl.BlockSpec((CONV_W, C3), lambda i: (0, 0))],
        out_specs=pl.BlockSpec((tm, C3), lambda i: (i, 0)),
        out_shape=SDS((T, C3), F32),
        compiler_params=_params(("arbitrary",)),
    )(dc, dc, conv_w)


def _mla_pre_bwd(proj, cosf, sinf, w_qln, w_kvln, w_uq_p, w_ukv, qnw, knw, dq4, dk4, dv4):
    T = proj.shape[0]
    tm = min(256, T)
    H = MLA_HEADS

    def body(ql_ref, kvl_ref, kpe_ref, cos_ref, sin_ref, wq_ref, wkv_ref, uq_ref, ukv_ref, qnw_ref, knw_ref,
             dq_ref, dk_ref, dv_ref,
             dql_ref, dkvl_ref, dkpe_ref, dqraw_ref, dkvraw_ref, qn_ref, kvn_ref, dwq_ref, dwkv_ref, dqnw_ref, dknw_ref):
        @pl.when(pl.program_id(0) == 0)
        def _():
            for r in (dwq_ref, dwkv_ref, dqnw_ref, dknw_ref):
                r[...] = jnp.zeros_like(r)

        cos, sin = cos_ref[...], sin_ref[...]
        qnw_, knw_ = qnw_ref[...], knw_ref[...]
        ql, kvl = ql_ref[...], kvl_ref[...]
        kpe_raw = kpe_ref[...][:, :ROPE]
        qn, rq = _rms(ql, wq_ref[...])
        kvn, rkv = _rms(kvl, wkv_ref[...])
        qn_ref[...] = qn.astype(MXU_DTYPE)
        kvn_ref[...] = kvn.astype(MXU_DTYPE)
        qraw = _mm(qn, uq_ref[...])
        kvraw = _mm(kvn, ukv_ref[...])
        dq_nope, dq_pe, dkv_parts = [], [], []
        dqnw_n = jnp.zeros((1, NOPE), F32)
        dqnw_p = jnp.zeros((1, ROPE), F32)
        dknw_n = jnp.zeros((1, NOPE), F32)
        dkpe = jnp.zeros((tm, ROPE), F32)
        for h in range(H):
            dq = dq_ref[h] * ATT_SCALE
            x = qraw[:, h * NOPE:(h + 1) * NOPE]
            dx, dw = _rms_bwd(dq[:, :NOPE], x, qnw_[:, :NOPE], _rms(x, qnw_[:, :NOPE])[1])
            dq_nope.append(dx)
            dqnw_n = dqnw_n + dw
            x = qraw[:, H * NOPE + h * ROPE:H * NOPE + (h + 1) * ROPE]
            dx, dw = _rms_bwd(_rope_bwd(dq[:, NOPE:], cos, sin), x, qnw_[:, NOPE:], _rms(x, qnw_[:, NOPE:])[1])
            dq_pe.append(dx)
            dqnw_p = dqnw_p + dw
            dk = dk_ref[h]
            x = kvraw[:, h * 256:h * 256 + NOPE]
            dx, dw = _rms_bwd(dk[:, :NOPE], x, knw_[:, :NOPE], _rms(x, knw_[:, :NOPE])[1])
            dknw_n = dknw_n + dw
            dkpe = dkpe + dk[:, NOPE:]
            dkv_parts += [dx, dv_ref[h]]
        dx, dknw_p = _rms_bwd(_rope_bwd(dkpe, cos, sin), kpe_raw, knw_[:, NOPE:], _rms(kpe_raw, knw_[:, NOPE:])[1])
        dkpe_ref[...] = jnp.concatenate([dx, jnp.zeros((tm, LANES - ROPE), F32)], axis=-1)
        dqraw = jnp.concatenate(dq_nope + dq_pe, axis=-1).astype(MXU_DTYPE)
        dkvraw = jnp.concatenate(dkv_parts, axis=-1).astype(MXU_DTYPE)
        dqraw_ref[...] = dqraw
        dkvraw_ref[...] = dkvraw
        dx, dw = _rms_bwd(_mm_nt(dqraw, uq_ref[...]), ql, wq_ref[...], rq)
        dql_ref[...] = dx
        dwq_ref[...] += dw
        dx, dw = _rms_bwd(_mm_nt(dkvraw, ukv_ref[...]), kvl, wkv_ref[...], rkv)
        dkvl_ref[...] = dx
        dwkv_ref[...] += dw
        dqnw_ref[...] += jnp.concatenate([dqnw_n, dqnw_p], axis=-1)
        dknw_ref[...] += jnp.concatenate([dknw_n, dknw_p], axis=-1)

    full = lambda a: pl.BlockSpec(a.shape, lambda i: (0,) * a.ndim)
    rows = lambda n: pl.BlockSpec((tm, n), lambda i: (i, 0))
    const = lambda n: pl.BlockSpec((1, n), lambda i: (0, 0))
    NQ, NKV = w_uq_p.shape[1], w_ukv.shape[1]
    return pl.pallas_call(
        body, grid=(T // tm,), name="mla_pre_bwd",
        in_specs=[pl.BlockSpec((tm, 256), lambda i: (i, P_QLAT // 256)),
                  pl.BlockSpec((tm, 256), lambda i: (i, P_KVLAT // 256)),
                  pl.BlockSpec((tm, 128), lambda i: (i, P_KPE // 128)),
                  rows(ROPE), rows(ROPE),
                  full(w_qln), full(w_kvln), full(w_uq_p), full(w_ukv), full(qnw), full(knw),
                  pl.BlockSpec((H, tm, QK_DIM), lambda i: (0, i, 0)),
                  pl.BlockSpec((H, tm, QK_DIM), lambda i: (0, i, 0)),
                  pl.BlockSpec((H, tm, V_DIM), lambda i: (0, i, 0))],
        out_specs=[rows(Q_LORA), rows(KV_LORA), rows(LANES), rows(NQ), rows(NKV), rows(Q_LORA), rows(KV_LORA),
                   const(Q_LORA), const(KV_LORA), const(QK_DIM), const(QK_DIM)],
        out_shape=[SDS((T, Q_LORA), F32), SDS((T, KV_LORA), F32), SDS((T, LANES), F32),
                   SDS((T, NQ), MXU_DTYPE), SDS((T, NKV), MXU_DTYPE),
                   SDS((T, Q_LORA), MXU_DTYPE), SDS((T, KV_LORA), MXU_DTYPE),
                   SDS((1, Q_LORA), F32), SDS((1, KV_LORA), F32), SDS((1, QK_DIM), F32), SDS((1, QK_DIM), F32)],
        compiler_params=_params(("arbitrary",)),
    )(proj, proj, proj, cosf, sinf, w_qln, w_kvln, w_uq_p, w_ukv, qnw, knw, dq4, dk4, dv4)


def _in_proj_bwd(dgqkv, dgz, dql, dkvl, dkpe, dgab, w_in_p, dh, x2, w_an):
    T, D = x2.shape
    N = w_in_p.shape[1]
    tm = min(512, T)

    def body(a_ref, b_ref, c_ref, d_ref, e_ref, f_ref, w_ref, dh_ref, x_ref, wn_ref, dx_ref, dp_ref, dwn_ref):
        @pl.when(pl.program_id(0) == 0)
        def _():
            dwn_ref[...] = jnp.zeros_like(dwn_ref)

        dp = jnp.concatenate([a_ref[...], b_ref[...], c_ref[...], d_ref[...], e_ref[...], f_ref[...]],
                             axis=-1).astype(MXU_DTYPE)
        dp_ref[...] = dp
        x = x_ref[...]
        _, r = _rms(x, wn_ref[...])
        dx, dw = _rms_bwd(_mm_nt(dp, w_ref[...]), x, wn_ref[...], r)
        dx_ref[...] = dh_ref[...] + dx
        dwn_ref[...] += dw

    rows = lambda n: pl.BlockSpec((tm, n), lambda i: (i, 0))
    return pl.pallas_call(
        body, grid=(T // tm,), name="in_proj_bwd",
        in_specs=[rows(dgqkv.shape[1]), rows(dgz.shape[1]), rows(dql.shape[1]), rows(dkvl.shape[1]),
                  rows(dkpe.shape[1]), rows(dgab.shape[1]),
                  pl.BlockSpec((D, N), lambda i: (0, 0)), rows(D), rows(D), pl.BlockSpec((1, D), lambda i: (0, 0))],
        out_specs=[rows(D), rows(N), pl.BlockSpec((1, D), lambda i: (0, 0))],
        out_shape=[SDS((T, D), F32), SDS((T, N), MXU_DTYPE), SDS((1, D), F32)],
        compiler_params=_params(("arbitrary",)),
    )(dgqkv, dgz, dql, dkvl, dkpe, dgab, w_in_p, dh, x2, w_an)


def _wgrad(a, b, name, column_shards=False):
    T, M = a.shape
    N = b.shape[1]
    tM = _divisor_tile(M, 1024)
    tN = N // N_DEV if column_shards else _divisor_tile(N, 1536)
    tk = min(T, 1024)
    nk = T // tk

    def body(a_ref, b_ref, o_ref, acc):
        k = pl.program_id(2)

        @pl.when(k == 0)
        def _():
            acc[...] = jnp.zeros_like(acc)

        acc[...] += _mm_tn(a_ref[...], b_ref[...])

        @pl.when(k == nk - 1)
        def _():
            o_ref[...] = acc[...].astype(WIRE_DTYPE).reshape(o_ref.shape)

    if column_shards:
        out_spec, out_shape = pl.BlockSpec((1, tM, tN), lambda i, j, k: (j, i, 0)), SDS((N_DEV, M, tN), WIRE_DTYPE)
    else:
        out_spec, out_shape = pl.BlockSpec((tM, tN), lambda i, j, k: (i, j)), SDS((M, N), WIRE_DTYPE)
    return pl.pallas_call(
        body, grid=(M // tM, N // tN, nk), name=name,
        in_specs=[pl.BlockSpec((tk, tM), lambda i, j, k: (k, i)), pl.BlockSpec((tk, tN), lambda i, j, k: (k, j))],
        out_specs=out_spec, out_shape=out_shape,
        scratch_shapes=[pltpu.VMEM((tM, tN), F32)],
        compiler_params=_params(("arbitrary", "arbitrary", "arbitrary")),
    )(a, b)


def _adamw(g, w, m, v):
    m = ADAM_B1 * m + (1.0 - ADAM_B1) * g
    v = ADAM_B2 * v + (1.0 - ADAM_B2) * jnp.square(g)
    m_hat = m / (1.0 - ADAM_B1 ** ADAM_STEP)
    v_hat = v / (1.0 - ADAM_B2 ** ADAM_STEP)
    return -ADAM_LR * (m_hat / (jnp.sqrt(v_hat) + ADAM_EPS) + ADAM_WD * w), m, v


def _reduce_adamw(parts, w, m, v, name):
    R, C = w.shape
    _, Rp, Cp = parts.shape
    tr = min(R, 256)
    tp = tr if Rp == R else Rp

    def body(p_ref, w_ref, m_ref, v_ref, g_ref, d_ref, nm_ref, nv_ref):
        g = p_ref[0].astype(F32)
        for s in range(1, N_DEV):
            g = g + p_ref[s].astype(F32)
        g = g[:tr, :C]
        g_ref[...] = g
        d_ref[...], nm_ref[...], nv_ref[...] = _adamw(g, w_ref[...], m_ref[...], v_ref[...])

    spec = pl.BlockSpec((tr, C), lambda i: (i, 0))
    return pl.pallas_call(
        body, grid=(R // tr,), name=name,
        in_specs=[pl.BlockSpec((N_DEV, tp, Cp), lambda i: (0, i, 0)), spec, spec, spec],
        out_specs=[spec] * 4, out_shape=[SDS((R, C), F32)] * 4,
        compiler_params=_params(("arbitrary",)),
    )(parts, w, m, v)


SMALL_ROWS, SMALL_COLS = 16, 1024
SMALL_LAYOUT = (
    ("attn_norm_w", 0, 1, 1024, 1024), ("mlp_norm_w", 1, 1, 1024, 1024), ("q_lat_norm_w", 2, 1, 256, 256),
    ("kv_lat_norm_w", 3, 1, 256, 256), ("q_norm_w", 4, 1, 192, 192), ("k_norm_w", 5, 1, 192, 192),
    ("mla_out_norm_w", 6, 4, 128, 128), ("a_log", 10, 1, 128, 4), ("dt_bias", 11, 1, 128, 4),
    ("gdn_norm_w", 12, 1, 128, 128))


def _adamw_replicated(parts, ws, ms, vs):
    n = len(SMALL_LAYOUT)

    def body(*refs):
        p_ref = refs[0]
        w_refs, m_refs, v_refs = refs[1:1 + n], refs[1 + n:1 + 2 * n], refs[1 + 2 * n:1 + 3 * n]
        outs = refs[1 + 3 * n:]
        s = p_ref[0]
        for d in range(1, N_DEV):
            s = s + p_ref[d]
        for i, (_, r0, nr, _, pw) in enumerate(SMALL_LAYOUT):
            g = s[r0:r0 + nr, :pw]
            outs[i][...] = g
            outs[n + i][...], outs[2 * n + i][...], outs[3 * n + i][...] = _adamw(
                g, w_refs[i][...], m_refs[i][...], v_refs[i][...])

    res = pl.pallas_call(
        body, name="adamw_replicated",
        out_shape=[SDS(w.shape, F32) for w in ws] * 4,
        compiler_params=_params(),
    )(parts, *ws, *ms, *vs)
    return [res[k * n:(k + 1) * n] for k in range(4)]


COPIES_PER_ARRAY = N_DEV - 1


def _two_level_gather(srcs, outs, send_sems, recv_sems, local_sems=None, stage="all"):
    mx, my, mc = lax.axis_index("x"), lax.axis_index("y"), lax.axis_index("c")
    me, sibling = (mx, my, mc), (mx, my, 1 - mc)
    chips = [(1 - mx, my), (mx, 1 - my), (1 - mx, 1 - my)]
    arrays = range(len(srcs))

    def copy(a, k, block, to, src=None):
        px, py, pc = block
        slot = outs[a].at[4 * px + 2 * py + pc]
        sem = a * COPIES_PER_ARRAY + k
        return pltpu.make_async_remote_copy(
            src_ref=slot if src is None else src, dst_ref=slot,
            send_sem=send_sems.at[sem], recv_sem=recv_sems.at[sem], device_id=to, device_id_type=MESH_ID)

    mine = [] if local_sems is None else [
        pltpu.make_async_copy(srcs[a], outs[a].at[4 * mx + 2 * my + mc], local_sems.at[a]) for a in arrays]
    first = []
    for a in arrays:
        first.append(copy(a, 0, me, sibling, src=srcs[a]))
        first += [copy(a, 1 + j, me, (*chip, mc), src=srcs[a]) for j, chip in enumerate(chips)]
    if stage in ("all", "start"):
        for cp in mine + first:
            cp.start()
    if stage in ("all", "finish"):
        forwards = []
        for j, chip in enumerate(chips):
            for a in arrays:
                copy(a, 1 + j, (*chip, mc), me).wait_recv()
                fwd = copy(a, 4 + j, (*chip, mc), sibling)
                fwd.start()
                forwards.append(fwd)
        for a in arrays:
            copy(a, 0, sibling, me).wait_recv()
        for j, chip in enumerate(chips):
            for a in arrays:
                copy(a, 4 + j, (*chip, 1 - mc), me).wait_recv()
        for cp in first + forwards:
            cp.wait_send()
        for cp in mine:
            cp.wait()


def _comm_scratch(n):
    return [pltpu.SemaphoreType.DMA((n * COPIES_PER_ARRAY,)), pltpu.SemaphoreType.DMA((n * COPIES_PER_ARRAY,)),
            pltpu.SemaphoreType.DMA((n,))]


def _any_specs(n):
    return [pl.BlockSpec(memory_space=pl.ANY)] * n


def _gather_weights(shards):
    n = len(shards)

    def body(*refs):
        _two_level_gather(refs[:n], refs[n:2 * n], *refs[2 * n:])

    return pl.pallas_call(
        body, name="gather_weights",
        out_shape=[SDS((N_DEV,) + s.shape, s.dtype) for s in shards],
        in_specs=_any_specs(n), out_specs=_any_specs(n), scratch_shapes=_comm_scratch(n),
    )(*shards)


def _gather_small_grads(gs):
    n = len(gs)

    def body(*refs):
        g_refs, out_ref = refs[:n], refs[n]
        tile, send_sems, recv_sems = refs[n + 1:]
        tile[...] = jnp.zeros_like(tile)
        for (_, r0, nr, gw, _), g in zip(SMALL_LAYOUT, g_refs):
            tile[r0:r0 + nr, 0:gw] = g[...]
        me = 4 * lax.axis_index("x") + 2 * lax.axis_index("y") + lax.axis_index("c")
        out_ref[me] = tile[...]
        _two_level_gather([tile], [out_ref], send_sems, recv_sems)

    return pl.pallas_call(
        body, name="gather_small_grads",
        out_shape=SDS((N_DEV, SMALL_ROWS, SMALL_COLS), F32),
        in_specs=[pl.BlockSpec(memory_space=pltpu.VMEM)] * n,
        out_specs=pl.BlockSpec(memory_space=pltpu.VMEM),
        scratch_shapes=[pltpu.VMEM((SMALL_ROWS, SMALL_COLS), F32),
                        pltpu.SemaphoreType.DMA((COPIES_PER_ARRAY,)), pltpu.SemaphoreType.DMA((COPIES_PER_ARRAY,))],
    )(*gs)


def _exchange_grads(slabs):
    n = len(slabs)

    def body(*refs):
        _exchange(refs[:n], refs[n:2 * n], *refs[2 * n:])

    return pl.pallas_call(
        body, name="exchange_grads",
        out_shape=[SDS(s.shape, s.dtype) for s in slabs],
        in_specs=_any_specs(n), out_specs=_any_specs(n), scratch_shapes=_comm_scratch(n),
    )(*slabs)


class _Transfer:
    def __init__(self, kind, arrays):
        self.kind, self.arrays, self.n = kind, list(arrays), len(arrays)

    def out_shapes(self):
        if self.kind == "gather":
            return [SDS((N_DEV,) + a.shape, a.dtype) for a in self.arrays]
        return [SDS(a.shape, a.dtype) for a in self.arrays]

    def run(self, srcs, outs, sems, stage):
        fn = _two_level_gather if self.kind == "gather" else _exchange
        fn(srcs, outs, *sems, stage=stage)


def _call_beside(body, transfer, *, grid, in_specs, out_specs, out_shape, scratch_shapes, name, semantics, args):
    if transfer is None:
        res = pl.pallas_call(body, grid=grid, in_specs=in_specs, out_specs=out_specs, out_shape=out_shape,
                             scratch_shapes=scratch_shapes, name=name, compiler_params=_params(semantics))(*args)
        return list(res), []
    n_in, n_out, n_s, n = len(in_specs), len(out_specs), len(scratch_shapes), transfer.n

    def wrapped(*refs):
        ins, refs = refs[:n_in], refs[n_in:]
        t_in, refs = refs[:n], refs[n:]
        outs, refs = refs[:n_out], refs[n_out:]
        t_out, refs = refs[:n], refs[n:]
        scratch, sems = refs[:n_s], refs[n_s:]
        first = functools.reduce(jnp.logical_and, [pl.program_id(i) == 0 for i in range(len(grid))])
        last = functools.reduce(jnp.logical_and, [pl.program_id(i) == g - 1 for i, g in enumerate(grid)])

        @pl.when(first)
        def _():
            transfer.run(t_in, t_out, sems, "start")

        body(*ins, *outs, *scratch)

        @pl.when(last)
        def _():
            transfer.run(t_in, t_out, sems, "finish")

    res = pl.pallas_call(
        wrapped, grid=grid, in_specs=list(in_specs) + _any_specs(n), out_specs=list(out_specs) + _any_specs(n),
        out_shape=list(out_shape) + transfer.out_shapes(), scratch_shapes=list(scratch_shapes) + _comm_scratch(n),
        name=name, compiler_params=_params(semantics))(*args, *transfer.arrays)
    return list(res[:n_out]), list(res[n_out:])


EXCHANGE_FLIPS = ((0, 0, 1), (1, 0, 0), (0, 1, 0), (1, 1, 0), (1, 0, 1), (0, 1, 1), (1, 1, 1))


def _exchange(srcs, outs, send_sems, recv_sems, local_sems, stage="all"):
    mx, my, mc = lax.axis_index("x"), lax.axis_index("y"), lax.axis_index("c")
    arrays = range(len(srcs))
    copies = [pltpu.make_async_copy(srcs[a].at[4 * mx + 2 * my + mc], outs[a].at[N_DEV - 1], local_sems.at[a])
              for a in arrays]
    for k, (fx, fy, fc) in enumerate(EXCHANGE_FLIPS):
        px = 1 - mx if fx else mx
        py = 1 - my if fy else my
        pc = 1 - mc if fc else mc
        for a in arrays:
            sem = a * COPIES_PER_ARRAY + k
            copies.append(pltpu.make_async_remote_copy(
                src_ref=srcs[a].at[4 * px + 2 * py + pc], dst_ref=outs[a].at[k],
                send_sem=send_sems.at[sem], recv_sem=recv_sems.at[sem],
                device_id=(px, py, pc), device_id_type=MESH_ID))
    if stage in ("all", "start"):
        for cp in copies:
            cp.start()
    if stage in ("all", "finish"):
        for cp in copies:
            cp.wait()


def _w_in_to_padded(w):
    z = lambda n: jnp.zeros((w.shape[0], n), w.dtype)
    return jnp.concatenate([w[:, O_GQKV:O_GZ], w[:, O_GZ:O_GAB], w[:, O_QLAT:O_KVLAT], w[:, O_KVLAT:O_KPE],
                            w[:, O_KPE:O_GQKV], z(P_GAB - P_KPE - ROPE), w[:, O_GAB:O_END],
                            z(P_WIDTH - P_GAB - (O_END - O_GAB))], axis=1)


def _w_in_from_padded(wp):
    return jnp.concatenate([wp[:, P_QLAT:P_QLAT + 256], wp[:, P_KVLAT:P_KVLAT + 256], wp[:, P_KPE:P_KPE + ROPE],
                            wp[:, P_GQKV:P_GZ], wp[:, P_GZ:P_QLAT], wp[:, P_GAB:P_GAB + (O_END - O_GAB)]], axis=1)


def _w_uq_to_headsplit(w):
    w3 = w.reshape(w.shape[0], MLA_HEADS, QK_DIM)
    return jnp.concatenate([w3[:, :, :NOPE].reshape(w.shape[0], -1), w3[:, :, NOPE:].reshape(w.shape[0], -1)], axis=1)


def _w_uq_from_headsplit(wp):
    n = wp[:, :MLA_HEADS * NOPE].reshape(wp.shape[0], MLA_HEADS, NOPE)
    p = wp[:, MLA_HEADS * NOPE:].reshape(wp.shape[0], MLA_HEADS, ROPE)
    return jnp.concatenate([n, p], axis=2).reshape(wp.shape[0], -1)


def _lane_vec(v4):
    return jnp.pad(v4.reshape(1, -1), ((0, 0), (0, LANES - v4.shape[-1])))


def _local_step(x, positions, target, attn_norm_w, w_in, q_lat_norm_w, w_uq, kv_lat_norm_w, w_ukv, q_norm_w,
                k_norm_w, mla_out_norm_w, conv_w, a_log, dt_bias, gdn_norm_w, w_out, mlp_norm_w, w_up, w_down,
                late_shards=None, exchange=False):
    B, S, D = x.shape
    T = B * S
    x2 = x.reshape(T, D)
    t2 = target.reshape(T, D)
    half = ROPE // 2
    inv_freq = ROPE_THETA ** (-jnp.arange(half, dtype=F32) / half)
    ang = positions.reshape(T, 1).astype(F32) * inv_freq
    cosf = jnp.concatenate([jnp.cos(ang)] * 2, axis=-1)
    sinf = jnp.concatenate([jnp.sin(ang)] * 2, axis=-1)
    w_in_p = _w_in_to_padded(w_in)
    w_uq_p = _w_uq_to_headsplit(w_uq)
    alog_l, dt_l = _lane_vec(a_log), _lane_vec(dt_bias)
    w_an, w_qln, w_kvln, qnw, knw, w_mn, gdn_w = (
        attn_norm_w, q_lat_norm_w, kv_lat_norm_w, q_norm_w, k_norm_w, mlp_norm_w, gdn_norm_w)

    proj, xn = _in_proj(x2, w_an, w_in_p)
    q4, k4, v4 = _mla_pre(proj, cosf, sinf, w_qln, w_kvln, w_uq_p, w_ukv, qnw, knw)
    gather = None if late_shards is None else _Transfer("gather", late_shards[:1])
    (o_mla, lse), late = _attn_fwd(q4, k4, v4, B, S, gather)
    if late:
        w_out = late[0].reshape(-1, D)
    qg, kg, vg, gates = _gdn_pre(proj, conv_w, alog_l, dt_l, S)
    gather = None if late_shards is None else _Transfer("gather", late_shards[1:])
    (o_gdn, states, ainv, u4, w4), late = _gdn_fwd(qg, kg, vg, gates, B, S, gather)
    if late:
        w_up, w_down = late[0], late[1].reshape(-1, D)
    h2, mix = _mix_out(o_mla, o_gdn, proj, x2, mla_out_norm_w, gdn_w, w_out)
    up, hn, dy, sq = _mlp_fwd(h2, w_mn, w_up, w_down, t2)
    loss = (0.5 / D) * jnp.sum(sq[:, 0, 0])

    dh, dhb, dup, act, dyb, d_mlp_norm = _mlp_bwd(dy, up, h2, w_mn, w_up, w_down)
    g_w_down = _wgrad(act, dyb, "wgrad_down")
    g_w_up = _wgrad(hn, dup, "wgrad_up", column_shards=True)
    do_mla, do_gdn, dz, d_mla_w, d_gdn_w = _mix_bwd(dhb, o_mla, o_gdn, proj, mla_out_norm_w, gdn_w, w_out)
    g_w_out = _wgrad(mix, dhb, "wgrad_out")
    first = ("w_down", "w_out")
    second = ("w_up", "w_uq", "w_ukv")
    mats = dict(w_up=g_w_up, w_down=g_w_down, w_out=g_w_out)
    send = _Transfer("exchange", [_slabs(n, mats[n]) for n in first]) if exchange else None
    (dq4, dk4, dv4), got = _attn_bwd(q4, k4, v4, do_mla, o_mla, lse, B, S, send)
    mats.update(zip(first, got))
    dql, dkvl, dkpe, dqraw, dkvraw, qn, kvn, d_wqln, d_wkvln, d_qnw, d_knw = _mla_pre_bwd(
        proj, cosf, sinf, w_qln, w_kvln, w_uq_p, w_ukv, qnw, knw, dq4, dk4, dv4)
    mats.update(w_uq=_wgrad(qn, dqraw, "wgrad_uq"), w_ukv=_wgrad(kvn, dkvraw, "wgrad_ukv"))
    send = _Transfer("exchange", [_slabs(n, mats[n]) for n in second]) if exchange else None
    (dqg, dkg, dvg, dgb4), got = _gdn_bwd(qg, kg, vg, gates, states, ainv, u4, w4, do_gdn, B, S, send)
    mats.update(zip(second, got))
    dc, dgab, g_conv, d_alog, d_dt = _gdn_pre_bwd(proj, conv_w, alog_l, dt_l, dqg, dkg, dvg, dgb4, S)
    dgqkv = _conv_bwd_input(dc, conv_w, S)
    grad_x2, dproj, d_attn_norm = _in_proj_bwd(dgqkv, dz, dql, dkvl, dkpe, dgab, w_in_p, dh, x2, w_an)
    mats.update(w_in=_wgrad(xn, dproj, "wgrad_in"), conv_w=g_conv)
    if exchange:
        last = ("w_in", "conv_w")
        mats.update(zip(last, _exchange_grads([_slabs(n, mats[n]) for n in last])))
    small = dict(attn_norm_w=d_attn_norm, mlp_norm_w=d_mlp_norm, q_lat_norm_w=d_wqln, kv_lat_norm_w=d_wkvln,
                 q_norm_w=d_qnw, k_norm_w=d_knw, mla_out_norm_w=d_mla_w, a_log=d_alog, dt_bias=d_dt,
                 gdn_norm_w=d_gdn_w)
    return loss, grad_x2.reshape(B, S, D), mats, [small[n] for n, *_ in SMALL_LAYOUT]


BIG = ("w_in", "w_uq", "w_ukv", "conv_w", "w_out", "w_up", "w_down")
ALL_W = ("attn_norm_w", "w_in", "q_lat_norm_w", "w_uq", "kv_lat_norm_w", "w_ukv", "q_norm_w", "k_norm_w",
         "mla_out_norm_w", "conv_w", "a_log", "dt_bias", "gdn_norm_w", "w_out", "mlp_norm_w", "w_up", "w_down")
WIRE_SHAPE = {"w_in": (1024, 384), "w_uq": (256, 128), "conv_w": (16, 256)}


def _pad2(a, rows, cols):
    return jnp.pad(a, [(0, 0)] * (a.ndim - 2) + [(0, rows - a.shape[-2]), (0, cols - a.shape[-1])])


def _cols_to_full(stack, cols):
    return jnp.moveaxis(stack[:, :, :cols], 0, 1).reshape(stack.shape[1], N_DEV * cols)


def _full_to_cols(full, wire_cols):
    r, n = full.shape
    return _pad2(jnp.moveaxis(full.reshape(r, N_DEV, n // N_DEV), 1, 0), r, wire_cols)


def _slabs(name, g):
    if name == "w_in":
        return _full_to_cols(_w_in_from_padded(g), WIRE_SHAPE["w_in"][1])
    if name == "w_uq":
        return _full_to_cols(_w_uq_from_headsplit(g), WIRE_SHAPE["w_uq"][1])
    if name == "w_ukv":
        return _full_to_cols(g, g.shape[1] // N_DEV)
    if name == "conv_w":
        return _pad2(_full_to_cols(g.astype(WIRE_DTYPE), g.shape[1] // N_DEV), *WIRE_SHAPE["conv_w"])
    if name == "w_up":
        return g
    return g.reshape(N_DEV, -1, g.shape[-1])


def kernel(x, positions, attn_norm_w, w_in, q_lat_norm_w, w_uq, kv_lat_norm_w, w_ukv, q_norm_w, k_norm_w, mla_out_norm_w, conv_w, a_log, dt_bias, gdn_norm_w, w_out, mlp_norm_w, w_up, w_down, loss_target, m_attn_norm_w, m_w_in, m_q_lat_norm_w, m_w_uq, m_kv_lat_norm_w, m_w_ukv, m_q_norm_w, m_k_norm_w, m_mla_out_norm_w, m_conv_w, m_a_log, m_dt_bias, m_gdn_norm_w, m_w_out, m_mlp_norm_w, m_w_up, m_w_down, v_attn_norm_w, v_w_in, v_q_lat_norm_w, v_w_uq, v_kv_lat_norm_w, v_w_ukv, v_q_norm_w, v_k_norm_w, v_mla_out_norm_w, v_conv_w, v_a_log, v_dt_bias, v_gdn_norm_w, v_w_out, v_mlp_norm_w, v_w_up, v_w_down):
    env = dict(locals())
    W = {n: env[n][0] for n in ALL_W}
    Mo = {n: env["m_" + n][0] for n in ALL_W}
    Vo = {n: env["v_" + n][0] for n in ALL_W}

    two_d = lambda a: a.reshape(1, -1) if a.ndim == 1 else a
    D = x.shape[-1]

    s_in, s_uq, s_ukv, s_conv = _gather_weights([
        _pad2(W["w_in"].astype(WIRE_DTYPE), *WIRE_SHAPE["w_in"]),
        _pad2(W["w_uq"].astype(WIRE_DTYPE), *WIRE_SHAPE["w_uq"]),
        W["w_ukv"].astype(WIRE_DTYPE), _pad2(W["conv_w"], *WIRE_SHAPE["conv_w"])])
    late = [W["w_out"].astype(WIRE_DTYPE), W["w_up"].astype(WIRE_DTYPE), W["w_down"].astype(WIRE_DTYPE)]

    loss, grad_x, parts, gs = _local_step(
        x, positions, loss_target, two_d(W["attn_norm_w"]), _cols_to_full(s_in, W["w_in"].shape[1]),
        two_d(W["q_lat_norm_w"]), _cols_to_full(s_uq, W["w_uq"].shape[1]), two_d(W["kv_lat_norm_w"]),
        _cols_to_full(s_ukv, W["w_ukv"].shape[1]), two_d(W["q_norm_w"]), two_d(W["k_norm_w"]),
        W["mla_out_norm_w"], _cols_to_full(s_conv[:, :CONV_W], W["conv_w"].shape[1]), two_d(W["a_log"]),
        two_d(W["dt_bias"]), two_d(W["gdn_norm_w"]), None, two_d(W["mlp_norm_w"]), None, None,
        late_shards=late, exchange=True)
    loss = lax.psum(loss, ("x", "y", "c"))
    done = {n: _reduce_adamw(parts[n], W[n], Mo[n], Vo[n], "adamw_" + n) for n in BIG}
    names = [n for n, *_ in SMALL_LAYOUT]
    small = _adamw_replicated(_gather_small_grads(gs), [two_d(W[n]) for n in names], [two_d(Mo[n]) for n in names],
                              [two_d(Vo[n]) for n in names])
    for i, n in enumerate(names):
        done[n] = [small[kind][i] for kind in range(4)]
    res = [done[n][kind].reshape(env[n].shape) for kind in range(4) for n in ALL_W]
    return (loss, grad_x, *res)
```

```python
import functools

import jax
import jax.numpy as jnp
from jax import lax
from jax.experimental import pallas as pl
from jax.experimental.pallas import tpu as pltpu

F32 = jnp.float32
MXU_DTYPE = jnp.bfloat16
WIRE_DTYPE = jnp.bfloat16
SDS = jax.ShapeDtypeStruct
HIGHEST = lax.Precision.HIGHEST
MESH_ID = pl.DeviceIdType.MESH

D_MODEL = 1024
MLA_HEADS = 4
Q_LORA = 256
KV_LORA = 256
NOPE = 128
ROPE = 64
QK_DIM = NOPE + ROPE
V_DIM = 128
ROPE_THETA = 10000.0
GDN_HEADS = 4
GDN_DIM = 128
GDN_WIDTH = GDN_HEADS * GDN_DIM
CONV_W = 4
CHUNK = 64
D_FF = 4 * D_MODEL
EPS = 1e-6
ATT_SCALE = QK_DIM ** -0.5
GDN_QSCALE = GDN_DIM ** -0.5
N_DEV = 8
ATTN_BLOCK = 512
ATTN_CHAINS = 2
MLP_FWD_SHARDS = 4
MLP_BWD_SHARDS = 2

ADAM_LR = 0.001
ADAM_B1 = 0.9
ADAM_B2 = 0.999
ADAM_EPS = 1e-08
ADAM_WD = 0.01
ADAM_STEP = 10

LANES = 128
SUBLANES = 8
VMEM_LIMIT = 56 * 1024 * 1024

P_GQKV, P_GZ, P_QLAT, P_KVLAT, P_KPE, P_GAB = 0, 1536, 2048, 2304, 2560, 2688
P_WIDTH = 2816
O_QLAT, O_KVLAT, O_KPE, O_GQKV, O_GZ, O_GAB, O_END = 0, 256, 512, 576, 2112, 2624, 2632


def _params(sem=None, vmem=VMEM_LIMIT):
    kw = dict(vmem_limit_bytes=vmem)
    if sem is not None:
        kw["dimension_semantics"] = sem
    return pltpu.CompilerParams(**kw)


def _mm(a, b):
    return jnp.dot(a.astype(MXU_DTYPE), b.astype(MXU_DTYPE), preferred_element_type=F32)


def _mm_nt(a, b):
    return lax.dot_general(a.astype(MXU_DTYPE), b.astype(MXU_DTYPE), (((1,), (1,)), ((), ())),
                           preferred_element_type=F32)


def _mm_tn(a, b):
    return lax.dot_general(a.astype(MXU_DTYPE), b.astype(MXU_DTYPE), (((0,), (0,)), ((), ())),
                           preferred_element_type=F32)


def _split(a):
    hi = a.astype(MXU_DTYPE)
    return hi, (a - hi.astype(F32)).astype(MXU_DTYPE)


def _mm_split(a, b):
    (ah, al), (bh, bl) = a, b
    dot = lambda x, y: jnp.dot(x, y, preferred_element_type=F32)
    if MXU_DTYPE == F32:
        return dot(ah, bh)
    return dot(ah, bh) + dot(ah, bl) + dot(al, bh)


def _mm_exact(a, b):
    return _mm_split(_split(a), _split(b))


def _rms(x, w):
    r = lax.rsqrt(jnp.mean(x * x, axis=-1, keepdims=True) + EPS)
    return x * r * w, r


def _rms_bwd(dy, x, w, r):
    xh = x * r
    dyw = dy * w
    dx = r * (dyw - xh * jnp.mean(dyw * xh, axis=-1, keepdims=True))
    dw = jnp.sum(dy * xh, axis=0, keepdims=True)
    return dx, dw


def _l2n_bwd(dy, x, scale):
    r = lax.rsqrt(jnp.sum(x * x, axis=-1, keepdims=True) + EPS)
    xh = x * r
    return (scale * r) * (dy - xh * jnp.sum(dy * xh, axis=-1, keepdims=True))


def _rot(t):
    return jnp.concatenate([-t[:, ROPE // 2:], t[:, :ROPE // 2]], axis=-1)


def _rot_t(t):
    return jnp.concatenate([t[:, ROPE // 2:], -t[:, :ROPE // 2]], axis=-1)


def _rope(t, cos, sin):
    return t * cos + _rot(t) * sin


def _rope_bwd(d, cos, sin):
    return d * cos + _rot_t(d * sin)


def _sigmoid(x):
    return jax.nn.sigmoid(x)


def _shift_down(x, halo, j):
    if j == 0:
        return x
    xr = pltpu.roll(x, j, 0)
    hr = pltpu.roll(halo, j, 0)
    row = lax.broadcasted_iota(jnp.int32, halo.shape, 0)
    top = jnp.where(row < j, hr, xr[:SUBLANES])
    return jnp.concatenate([top, xr[SUBLANES:]], axis=0)


def _shift_up(x, nxt, j):
    if j == 0:
        return x
    n = x.shape[0]
    xr = pltpu.roll(x, n - j, 0)
    nr = pltpu.roll(nxt, SUBLANES - j, 0)
    row = lax.broadcasted_iota(jnp.int32, nxt.shape, 0)
    bot = jnp.where(row >= SUBLANES - j, nr, xr[n - SUBLANES:])
    return jnp.concatenate([xr[:n - SUBLANES], bot], axis=0)


def _chunk_cumsum(y, row_in_chunk):
    s = 1
    while s < CHUNK:
        y = y + jnp.where(row_in_chunk >= s, pltpu.roll(y, s, 0), 0.0)
        s *= 2
    return y


def _chunk_rev_cumsum(y, row_in_chunk):
    n = y.shape[0]
    s = 1
    while s < CHUNK:
        y = y + jnp.where(row_in_chunk + s < CHUNK, pltpu.roll(y, n - s, 0), 0.0)
        s *= 2
    return y


def _lockstep(generators):
    alive = list(generators)
    while alive:
        nxt = []
        for g in alive:
            try:
                next(g)
                nxt.append(g)
            except StopIteration:
                pass
        alive = nxt


def _pick_lane(tile, lane, idx):
    return jnp.sum(jnp.where(lane == idx, tile, 0.0), axis=-1, keepdims=True)


def _divisor_tile(n, cap, unit=LANES):
    best = unit
    t = unit
    while t <= min(n, cap):
        if n % t == 0:
            best = t
        t += unit
    return n if n <= cap else best


def _in_proj(x2, w_an, w_in_p):
    T, D = x2.shape
    N = w_in_p.shape[1]
    tm = min(512, T)

    def body(x_ref, wn_ref, w_ref, proj_ref, xn_ref):
        xn, _ = _rms(x_ref[...], wn_ref[...])
        xn = xn.astype(MXU_DTYPE)
        xn_ref[...] = xn
        proj_ref[...] = jnp.dot(xn, w_ref[...], preferred_element_type=F32)

    return pl.pallas_call(
        body, grid=(T // tm,), name="in_proj",
        in_specs=[pl.BlockSpec((tm, D), lambda i: (i, 0)), pl.BlockSpec((1, D), lambda i: (0, 0)),
                  pl.BlockSpec((D, N), lambda i: (0, 0))],
        out_specs=[pl.BlockSpec((tm, N), lambda i: (i, 0)), pl.BlockSpec((tm, D), lambda i: (i, 0))],
        out_shape=[SDS((T, N), F32), SDS((T, D), MXU_DTYPE)],
        compiler_params=_params(("arbitrary",)),
    )(x2, w_an, w_in_p)


def _mla_pre(proj, cosf, sinf, w_qln, w_kvln, w_uq_p, w_ukv, qnw, knw):
    T = proj.shape[0]
    tm = min(256, T)
    H = MLA_HEADS

    def body(ql_ref, kvl_ref, kpe_ref, cos_ref, sin_ref, wq_ref, wkv_ref, uq_ref, ukv_ref, qnw_ref, knw_ref,
             q_out, k_out, v_out):
        cos, sin = cos_ref[...], sin_ref[...]
        qnw_, knw_ = qnw_ref[...], knw_ref[...]
        qn, _ = _rms(ql_ref[...], wq_ref[...])
        kvn, _ = _rms(kvl_ref[...], wkv_ref[...])
        qraw = _mm(qn, uq_ref[...])
        kvraw = _mm(kvn, ukv_ref[...])
        kpe = _rope(_rms(kpe_ref[...][:, :ROPE], knw_[:, NOPE:])[0], cos, sin)
        for h in range(H):
            qn_h = _rms(qraw[:, h * NOPE:(h + 1) * NOPE], qnw_[:, :NOPE])[0]
            qp_h = _rope(_rms(qraw[:, H * NOPE + h * ROPE:H * NOPE + (h + 1) * ROPE], qnw_[:, NOPE:])[0], cos, sin)
            q_out[h] = (jnp.concatenate([qn_h, qp_h], axis=-1) * ATT_SCALE).astype(MXU_DTYPE)
            kn_h = _rms(kvraw[:, h * 256:h * 256 + NOPE], knw_[:, :NOPE])[0]
            k_out[h] = jnp.concatenate([kn_h, kpe], axis=-1).astype(MXU_DTYPE)
            v_out[h] = kvraw[:, h * 256 + NOPE:(h + 1) * 256].astype(MXU_DTYPE)

    full = lambda a: pl.BlockSpec(a.shape, lambda i: (0,) * a.ndim)
    return pl.pallas_call(
        body, grid=(T // tm,), name="mla_pre",
        in_specs=[pl.BlockSpec((tm, 256), lambda i: (i, P_QLAT // 256)),
                  pl.BlockSpec((tm, 256), lambda i: (i, P_KVLAT // 256)),
                  pl.BlockSpec((tm, 128), lambda i: (i, P_KPE // 128)),
                  pl.BlockSpec((tm, ROPE), lambda i: (i, 0)), pl.BlockSpec((tm, ROPE), lambda i: (i, 0)),
                  full(w_qln), full(w_kvln), full(w_uq_p), full(w_ukv), full(qnw), full(knw)],
        out_specs=[pl.BlockSpec((H, tm, QK_DIM), lambda i: (0, i, 0)),
                   pl.BlockSpec((H, tm, QK_DIM), lambda i: (0, i, 0)),
                   pl.BlockSpec((H, tm, V_DIM), lambda i: (0, i, 0))],
        out_shape=[SDS((H, T, QK_DIM), MXU_DTYPE), SDS((H, T, QK_DIM), MXU_DTYPE), SDS((H, T, V_DIM), MXU_DTYPE)],
        compiler_params=_params(("arbitrary",)),
    )(proj, proj, proj, cosf, sinf, w_qln, w_kvln, w_uq_p, w_ukv, qnw, knw)


def _attn_fwd(q4, k4, v4, B, S, transfer=None):
    H = MLA_HEADS
    bq = min(ATTN_BLOCK, S)
    nq = S // bq
    rows = bq // ATTN_CHAINS

    def body(q_ref, k_ref, v_ref, o_ref, lse_ref):
        col = lax.broadcasted_iota(jnp.int32, (rows, bq), 1)
        row = lax.broadcasted_iota(jnp.int32, (rows, bq), 0)

        def q_step(qi, carry):
            qs = pl.multiple_of(qi * bq, bq)
            qsub = [q_ref[0, pl.ds(qs + j * rows, rows), :] for j in range(ATTN_CHAINS)]

            def k_block(ks, cs, diagonal):
                k = k_ref[0, pl.ds(ks, bq), :]
                v = v_ref[0, pl.ds(ks, bq), :]
                out = [None] * ATTN_CHAINS

                def chain(j):
                    m, l, acc = cs[j]
                    s = _mm_nt(qsub[j], k)
                    yield
                    if diagonal:
                        s = jnp.where(col <= row + j * rows, s, -jnp.inf)
                    m_new = jnp.maximum(m, jnp.max(s, axis=-1, keepdims=True))
                    p = jnp.exp(s - m_new)
                    a = jnp.exp(m - m_new)
                    l_new = a * l + jnp.sum(p, axis=-1, keepdims=True)
                    yield
                    out[j] = (m_new, l_new, a * acc + _mm(p, v))

                _lockstep([chain(j) for j in range(ATTN_CHAINS)])
                return tuple(out)

            init = tuple((jnp.full((rows, 1), -jnp.inf, F32), jnp.zeros((rows, 1), F32),
                          jnp.zeros((rows, V_DIM), F32)) for _ in range(ATTN_CHAINS))
            cs = lax.fori_loop(0, qi, lambda kj, c: k_block(pl.multiple_of(kj * bq, bq), c, False), init)
            for j, (m, l, acc) in enumerate(k_block(qs, cs, True)):
                o_ref[0, pl.ds(qs + j * rows, rows), :] = acc / l
                lse_ref[0, pl.ds(qs + j * rows, rows), :] = m + jnp.log(l)
            return carry

        lax.fori_loop(0, nq, q_step, 0)

    spec = lambda d: pl.BlockSpec((1, S, d), lambda h, b: (h, b, 0))
    return _call_beside(
        body, transfer, grid=(H, B), name="attn_fwd",
        in_specs=[spec(QK_DIM), spec(QK_DIM), spec(V_DIM)],
        out_specs=[spec(V_DIM), spec(1)],
        out_shape=[SDS((H, B * S, V_DIM), F32), SDS((H, B * S, 1), F32)],
        scratch_shapes=[], semantics=("arbitrary", "arbitrary"), args=(q4, k4, v4))


def _conv_taps(u, halo, w):
    sh = [_shift_down(u, halo, j) for j in range(CONV_W)]
    c = w[0:1] * sh[3] + w[1:2] * sh[2] + w[2:3] * sh[1] + w[3:4] * sh[0]
    return c, sh


def _gate_values(gab, alog_l, dt_l, lane):
    g = -jnp.exp(alog_l) * jax.nn.softplus(gab + dt_l)
    g = jnp.where(lane < GDN_HEADS, g, 0.0)
    beta = jnp.where((lane >= GDN_HEADS) & (lane < 2 * GDN_HEADS), _sigmoid(gab), 0.0)
    return g, beta


def _gdn_pre(proj, conv_w, alog_l, dt_l, S):
    T = proj.shape[0]
    tm = min(256, T)
    tiles_per_seq = S // tm
    C3 = 3 * GDN_WIDTH
    H = GDN_HEADS

    def body(u_ref, halo_ref, gab_ref, w_ref, alog_ref, dt_ref, q_out, k_out, v_out, gates_out):
        i = pl.program_id(0)
        halo = jnp.where(i % tiles_per_seq == 0, 0.0, halo_ref[...])
        c, _ = _conv_taps(u_ref[...], halo, w_ref[...])
        a = c * _sigmoid(c)
        for h in range(H):
            xq = a[:, h * GDN_DIM:(h + 1) * GDN_DIM]
            xk = a[:, GDN_WIDTH + h * GDN_DIM:GDN_WIDTH + (h + 1) * GDN_DIM]
            q_out[h] = xq * lax.rsqrt(jnp.sum(xq * xq, axis=-1, keepdims=True) + EPS) * GDN_QSCALE
            k_out[h] = xk * lax.rsqrt(jnp.sum(xk * xk, axis=-1, keepdims=True) + EPS)
            v_out[h] = a[:, 2 * GDN_WIDTH + h * GDN_DIM:2 * GDN_WIDTH + (h + 1) * GDN_DIM]
        lane = lax.broadcasted_iota(jnp.int32, (tm, LANES), 1)
        ric = lax.broadcasted_iota(jnp.int32, (tm, LANES), 0) % CHUNK
        g, beta = _gate_values(gab_ref[...], alog_ref[...], dt_ref[...], lane)
        gates_out[...] = _chunk_cumsum(g, ric) + beta

    hspec = pl.BlockSpec((H, tm, GDN_DIM), lambda i: (0, i, 0))
    return pl.pallas_call(
        body, grid=(T // tm,), name="gdn_pre",
        in_specs=[pl.BlockSpec((tm, C3), lambda i: (i, 0)),
                  pl.BlockSpec((SUBLANES, C3), lambda i: (jnp.maximum(i * (tm // SUBLANES) - 1, 0), 0)),
                  pl.BlockSpec((tm, LANES), lambda i: (i, P_GAB // LANES)),
                  pl.BlockSpec((CONV_W, C3), lambda i: (0, 0)),
                  pl.BlockSpec((1, LANES), lambda i: (0, 0)), pl.BlockSpec((1, LANES), lambda i: (0, 0))],
        out_specs=[hspec, hspec, hspec, pl.BlockSpec((tm, LANES), lambda i: (i, 0))],
        out_shape=[SDS((H, T, GDN_DIM), F32)] * 3 + [SDS((T, LANES), F32)],
        compiler_params=_params(("arbitrary",)),
    )(proj, proj, proj, conv_w, alog_l, dt_l)


def _unit_lower_inverses(Ls, eye):
    Ps = [eye - L for L in Ls]
    Ms = [_split(-L) for L in Ls]
    for _ in range(5):
        sq = [_mm_split(m, m) for m in Ms]
        Ms = [_split(s) for s in sq]
        Ps = [p + _mm_split(_split(p), m) for p, m in zip(Ps, Ms)]
    return Ps


def _chunk_decays(gt, lane, h, ri, ci, rcol):
    Gc = _pick_lane(gt, lane, h)
    bt = _pick_lane(gt, lane, h + GDN_HEADS)
    Gb = jnp.broadcast_to(Gc, (CHUNK, CHUNK))
    Gam = jnp.where(ri >= ci, jnp.exp(Gb - Gb.T), 0.0)
    Gl = jnp.sum(jnp.where(rcol == CHUNK - 1, Gc, 0.0), axis=0, keepdims=True)
    return Gc, bt, Gam, jnp.exp(Gc), jnp.exp(Gl - Gc), jnp.exp(Gl)


GDN_UNROLL = 4


def _gdn_fwd(qg, kg, vg, gates, B, S, transfer=None):
    H, D, C = GDN_HEADS, GDN_DIM, CHUNK
    NC = S // C
    U = GDN_UNROLL if NC % GDN_UNROLL == 0 else 1

    def body(q_ref, k_ref, v_ref, g_ref, o_ref, st_ref, ai_ref, u_ref, w_ref, q2_s, au_s, bc_s, w2_s, el_s):
        h = pl.program_id(0)
        lane = lax.broadcasted_iota(jnp.int32, (C, LANES), 1)
        ri = lax.broadcasted_iota(jnp.int32, (C, C), 0)
        ci = lax.broadcasted_iota(jnp.int32, (C, C), 1)
        rcol = lax.broadcasted_iota(jnp.int32, (C, 1), 0)
        eye = (ri == ci).astype(F32)

        def group(gi, c):
            ns = [gi * U + j for j in range(U)]
            css = [pl.multiple_of(n * C, C) for n in ns]
            qs = [q_ref[0, pl.ds(cs, C), :] for cs in css]
            ks = [k_ref[0, pl.ds(cs, C), :] for cs in css]
            vs = [v_ref[0, pl.ds(cs, C), :] for cs in css]
            decs = [_chunk_decays(g_ref[pl.ds(cs, C), :], lane, h, ri, ci, rcol) for cs in css]
            qks = [_mm_nt(jnp.concatenate([q, k], axis=0), k) for q, k in zip(qs, ks)]
            ainvs = _unit_lower_inverses(
                [jnp.where(ri > ci, d[1] * qk[C:] * d[2], 0.0) for qk, d in zip(qks, decs)], eye)
            sols = [_mm_exact(a, jnp.concatenate([v * d[1], k * (d[1] * d[3])], axis=-1))
                    for a, k, v, d in zip(ainvs, ks, vs, decs)]
            atuw = [_mm(qk[:C] * d[2], sol) for qk, d, sol in zip(qks, decs, sols)]
            kduw = [_mm_tn(k * d[4], sol) for k, d, sol in zip(ks, decs, sols)]
            for n, cs, q, a, sol, au, ku, (Gc, bt, Gam, e, f, eL) in zip(ns, css, qs, ainvs, sols, atuw, kduw, decs):
                u_ref[0, pl.ds(cs, C), :] = sol[:, :D]
                w_ref[0, pl.ds(cs, C), :] = sol[:, D:]
                au_s[pl.ds(cs, C), :] = au[:, :D]
                q2_s[pl.ds(cs, C), :] = q * e - au[:, D:]
                bc_s[n] = ku[:, :D]
                w2_s[n] = ku[:, D:]
                el_s[n] = jnp.broadcast_to(eL, (SUBLANES, LANES))
                ai_ref[0, n] = a
            return c

        lax.fori_loop(0, NC // U, group, 0)

        def step(n, S_):
            cs = pl.multiple_of(n * C, C)
            o_ref[0, pl.ds(cs, C), :] = _mm(q2_s[pl.ds(cs, C), :], S_) + au_s[pl.ds(cs, C), :]
            st_ref[0, n] = S_
            return S_ * el_s[n, 0:1, :] + bc_s[n] - _mm(w2_s[n], S_)

        lax.fori_loop(0, NC, step, jnp.zeros((D, D), F32))

    spec = pl.BlockSpec((1, S, D), lambda h, b: (h, b, 0))
    return _call_beside(
        body, transfer, grid=(H, B), name="gdn_fwd",
        in_specs=[spec, spec, spec, pl.BlockSpec((S, LANES), lambda h, b: (b, 0))],
        out_specs=[spec, pl.BlockSpec((1, NC, D, D), lambda h, b: (h, b, 0, 0)),
                   pl.BlockSpec((1, NC, C, C), lambda h, b: (h, b, 0, 0)), spec, spec],
        out_shape=[SDS((H, B * S, D), F32), SDS((H, B * NC, D, D), F32), SDS((H, B * NC, C, C), F32),
                   SDS((H, B * S, D), F32), SDS((H, B * S, D), F32)],
        scratch_shapes=[pltpu.VMEM((S, D), F32), pltpu.VMEM((S, D), F32), pltpu.VMEM((NC, D, D), F32),
                        pltpu.VMEM((NC, D, D), F32), pltpu.VMEM((NC, SUBLANES, LANES), F32)],
        semantics=("arbitrary", "arbitrary"), args=(qg, kg, vg, gates))


def _mix_out(o_mla, o_gdn, proj, x2, mla_w, gdn_w, w_out):
    T, D = x2.shape
    tm = min(512, T)
    H = MLA_HEADS

    def body(om_ref, og_ref, z_ref, x_ref, mw_ref, gw_ref, w_ref, h_ref, mix_ref):
        z = z_ref[...]
        parts = [_rms(om_ref[h], mw_ref[h:h + 1, :])[0] for h in range(H)]
        for h in range(GDN_HEADS):
            zh = z[:, h * GDN_DIM:(h + 1) * GDN_DIM]
            parts.append(_rms(og_ref[h], gw_ref[...])[0] * (zh * _sigmoid(zh)))
        mix = jnp.concatenate(parts, axis=-1).astype(MXU_DTYPE)
        mix_ref[...] = mix
        h_ref[...] = x_ref[...] + jnp.dot(mix, w_ref[...], preferred_element_type=F32)

    hspec = pl.BlockSpec((H, tm, V_DIM), lambda i: (0, i, 0))
    return pl.pallas_call(
        body, grid=(T // tm,), name="mix_out",
        in_specs=[hspec, hspec, pl.BlockSpec((tm, GDN_WIDTH), lambda i: (i, P_GZ // GDN_WIDTH)),
                  pl.BlockSpec((tm, D), lambda i: (i, 0)),
                  pl.BlockSpec((H, V_DIM), lambda i: (0, 0)), pl.BlockSpec((1, GDN_DIM), lambda i: (0, 0)),
                  pl.BlockSpec((D, D), lambda i: (0, 0))],
        out_specs=[pl.BlockSpec((tm, D), lambda i: (i, 0)), pl.BlockSpec((tm, D), lambda i: (i, 0))],
        out_shape=[SDS((T, D), F32), SDS((T, D), MXU_DTYPE)],
        compiler_params=_params(("arbitrary",)),
    )(o_mla, o_gdn, proj, x2, mla_w, gdn_w, w_out)


def _mlp_fwd(h2, w_mn, w_up, w_down, target):
    T, D = h2.shape
    ns, _, ts = w_up.shape
    F = ns * ts
    tm = min(512, T)
    G = MLP_FWD_SHARDS
    tf, nf = G * ts, ns // G

    def body(h_ref, wn_ref, up_w, down_w, t_ref, up_ref, hn_ref, dy_ref, loss_ref, y_acc):
        j = pl.program_id(1)

        @pl.when(j == 0)
        def _():
            hn_ref[...] = _rms(h_ref[...], wn_ref[...])[0].astype(MXU_DTYPE)
            y_acc[...] = h_ref[...]

        parts = []
        for c in range(G):
            up = jnp.dot(hn_ref[...], up_w[c], preferred_element_type=F32)
            up_ref[:, c * ts:(c + 1) * ts] = up
            r = jnp.maximum(up, 0.0)
            parts.append(_mm(r * r, down_w[c * ts:(c + 1) * ts, :]))
        y_acc[...] += functools.reduce(jnp.add, parts)

        @pl.when(j == nf - 1)
        def _():
            err = y_acc[...] - t_ref[...]
            dy_ref[...] = err / D
            loss_ref[...] = jnp.full((1, SUBLANES, LANES), jnp.sum(err * err), F32)

    return pl.pallas_call(
        body, grid=(T // tm, nf), name="mlp_fwd",
        in_specs=[pl.BlockSpec((tm, D), lambda i, j: (i, 0)), pl.BlockSpec((1, D), lambda i, j: (0, 0)),
                  pl.BlockSpec((G, D, ts), lambda i, j: (j, 0, 0)), pl.BlockSpec((tf, D), lambda i, j: (j, 0)),
                  pl.BlockSpec((tm, D), lambda i, j: (i, 0))],
        out_specs=[pl.BlockSpec((tm, tf), lambda i, j: (i, j)), pl.BlockSpec((tm, D), lambda i, j: (i, 0)),
                   pl.BlockSpec((tm, D), lambda i, j: (i, 0)),
                   pl.BlockSpec((1, SUBLANES, LANES), lambda i, j: (i, 0, 0))],
        out_shape=[SDS((T, F), F32), SDS((T, D), MXU_DTYPE), SDS((T, D), F32),
                   SDS((T // tm, SUBLANES, LANES), F32)],
        scratch_shapes=[pltpu.VMEM((tm, D), F32)],
        compiler_params=_params(("arbitrary", "arbitrary")),
    )(h2, w_mn, w_up, w_down, target)


def _mlp_bwd(dy, up, h2, w_mn, w_up, w_down):
    T, D = h2.shape
    ns, _, ts = w_up.shape
    F = ns * ts
    tm = min(512, T)
    G = MLP_BWD_SHARDS
    tf, nf = G * ts, ns // G

    def body(dy_ref, up_ref, h_ref, wn_ref, up_w, down_w, dh_ref, dhb_ref, dup_ref, act_ref, dyb_ref, dwn_ref, acc):
        i, j = pl.program_id(0), pl.program_id(1)

        @pl.when((i == 0) & (j == 0))
        def _():
            dwn_ref[...] = jnp.zeros_like(dwn_ref)

        @pl.when(j == 0)
        def _():
            acc[...] = jnp.zeros_like(acc)
            dyb_ref[...] = dy_ref[...].astype(MXU_DTYPE)

        parts = []
        for c in range(G):
            cols = slice(c * ts, (c + 1) * ts)
            r = jnp.maximum(up_ref[:, cols], 0.0)
            act_ref[:, cols] = (r * r).astype(MXU_DTYPE)
            dup = (_mm_nt(dyb_ref[...], down_w[cols, :]) * (2.0 * r)).astype(MXU_DTYPE)
            dup_ref[:, cols] = dup
            parts.append(_mm_nt(dup, up_w[c]))
        acc[...] += functools.reduce(jnp.add, parts)

        @pl.when(j == nf - 1)
        def _():
            hv = h_ref[...]
            _, rr = _rms(hv, wn_ref[...])
            dx, dw = _rms_bwd(acc[...], hv, wn_ref[...], rr)
            dh = dy_ref[...] + dx
            dh_ref[...] = dh
            dhb_ref[...] = dh.astype(MXU_DTYPE)
            dwn_ref[...] += dw

    row = lambda i, j: (i, 0)
    return pl.pallas_call(
        body, grid=(T // tm, nf), name="mlp_bwd",
        in_specs=[pl.BlockSpec((tm, D), row), pl.BlockSpec((tm, tf), lambda i, j: (i, j)), pl.BlockSpec((tm, D), row),
                  pl.BlockSpec((1, D), lambda i, j: (0, 0)),
                  pl.BlockSpec((G, D, ts), lambda i, j: (j, 0, 0)), pl.BlockSpec((tf, D), lambda i, j: (j, 0))],
        out_specs=[pl.BlockSpec((tm, D), row), pl.BlockSpec((tm, D), row),
                   pl.BlockSpec((tm, tf), lambda i, j: (i, j)), pl.BlockSpec((tm, tf), lambda i, j: (i, j)),
                   pl.BlockSpec((tm, D), row), pl.BlockSpec((1, D), lambda i, j: (0, 0))],
        out_shape=[SDS((T, D), F32), SDS((T, D), MXU_DTYPE), SDS((T, F), MXU_DTYPE), SDS((T, F), MXU_DTYPE),
                   SDS((T, D), MXU_DTYPE), SDS((1, D), F32)],
        scratch_shapes=[pltpu.VMEM((tm, D), F32)],
        compiler_params=_params(("arbitrary", "arbitrary")),
    )(dy, up, h2, w_mn, w_up, w_down)


def _mix_bwd(dhb, o_mla, o_gdn, proj, mla_w, gdn_w, w_out):
    T, D = dhb.shape
    tm = min(512, T)
    H = MLA_HEADS

    def body(dh_ref, om_ref, og_ref, z_ref, mw_ref, gw_ref, w_ref, dom_ref, dog_ref, dz_ref, dmw_ref, dgw_ref):
        @pl.when(pl.program_id(0) == 0)
        def _():
            dmw_ref[...] = jnp.zeros_like(dmw_ref)
            dgw_ref[...] = jnp.zeros_like(dgw_ref)

        dmix = _mm_nt(dh_ref[...], w_ref[...])
        z = z_ref[...]
        dmw, dzs = [], []
        dgw = jnp.zeros((1, GDN_DIM), F32)
        for h in range(H):
            o = om_ref[h]
            w = mw_ref[h:h + 1, :]
            _, r = _rms(o, w)
            dx, dw = _rms_bwd(dmix[:, h * V_DIM:(h + 1) * V_DIM], o, w, r)
            dom_ref[h] = dx
            dmw.append(dw)
        for h in range(GDN_HEADS):
            o = og_ref[h]
            w = gw_ref[...]
            zh = z[:, h * GDN_DIM:(h + 1) * GDN_DIM]
            sg = _sigmoid(zh)
            yn, r = _rms(o, w)
            dy = dmix[:, H * V_DIM + h * GDN_DIM:H * V_DIM + (h + 1) * GDN_DIM]
            dzs.append(dy * yn * (sg * (1.0 + zh * (1.0 - sg))))
            dx, dw = _rms_bwd(dy * (zh * sg), o, w, r)
            dog_ref[h] = dx
            dgw = dgw + dw
        dz_ref[...] = jnp.concatenate(dzs, axis=-1)
        dmw_ref[...] += jnp.concatenate(dmw, axis=0)
        dgw_ref[...] += dgw

    hspec = pl.BlockSpec((H, tm, V_DIM), lambda i: (0, i, 0))
    return pl.pallas_call(
        body, grid=(T // tm,), name="mix_bwd",
        in_specs=[pl.BlockSpec((tm, D), lambda i: (i, 0)), hspec, hspec,
                  pl.BlockSpec((tm, GDN_WIDTH), lambda i: (i, P_GZ // GDN_WIDTH)),
                  pl.BlockSpec((H, V_DIM), lambda i: (0, 0)), pl.BlockSpec((1, GDN_DIM), lambda i: (0, 0)),
                  pl.BlockSpec((D, D), lambda i: (0, 0))],
        out_specs=[hspec, hspec, pl.BlockSpec((tm, GDN_WIDTH), lambda i: (i, 0)),
                   pl.BlockSpec((H, V_DIM), lambda i: (0, 0)), pl.BlockSpec((1, GDN_DIM), lambda i: (0, 0))],
        out_shape=[SDS((H, T, V_DIM), F32), SDS((H, T, GDN_DIM), F32), SDS((T, GDN_WIDTH), F32),
                   SDS((H, V_DIM), F32), SDS((1, GDN_DIM), F32)],
        compiler_params=_params(("arbitrary",)),
    )(dhb, o_mla, o_gdn, proj, mla_w, gdn_w, w_out)


def _attn_bwd(q4, k4, v4, do4, o4, lse4, B, S, transfer=None):
    H = MLA_HEADS
    bq = min(ATTN_BLOCK, S)
    nq = S // bq
    rows = bq // ATTN_CHAINS

    def body(q_ref, k_ref, v_ref, do_ref, o_ref, lse_ref, dq_ref, dk_ref, dv_ref, delta):
        dq_ref[...] = jnp.zeros_like(dq_ref)
        dk_ref[...] = jnp.zeros_like(dk_ref)
        dv_ref[...] = jnp.zeros_like(dv_ref)
        delta[...] = jnp.sum(do_ref[0] * o_ref[0], axis=-1, keepdims=True)

        col = lax.broadcasted_iota(jnp.int32, (rows, bq), 1)
        row = lax.broadcasted_iota(jnp.int32, (rows, bq), 0)

        def k_step(kj, carry):
            ks = pl.multiple_of(kj * bq, bq)
            k = k_ref[0, pl.ds(ks, bq), :]
            v = v_ref[0, pl.ds(ks, bq), :]

            def q_block(qs, diagonal):
                dks, dvs = [None] * ATTN_CHAINS, [None] * ATTN_CHAINS

                def chain(j):
                    sl = pl.ds(qs + j * rows, rows)
                    q = q_ref[0, sl, :]
                    do = do_ref[0, sl, :].astype(MXU_DTYPE)
                    s = _mm_nt(q, k)
                    dp = _mm_nt(do, v)
                    yield
                    p = jnp.exp(s - lse_ref[0, sl, :])
                    if diagonal:
                        p = jnp.where(col <= row + j * rows, p, 0.0)
                    ds = p * (dp - delta[sl, :])
                    yield
                    dvs[j] = _mm_tn(p, do)
                    dks[j] = _mm_tn(ds, q)
                    dq_ref[0, sl, :] += _mm(ds, k)

                _lockstep([chain(j) for j in range(ATTN_CHAINS)])
                dv_ref[0, pl.ds(ks, bq), :] += functools.reduce(jnp.add, dvs)
                dk_ref[0, pl.ds(ks, bq), :] += functools.reduce(jnp.add, dks)

            q_block(ks, True)

            def q_step(qi, c):
                q_block(pl.multiple_of(qi * bq, bq), False)
                return c

            lax.fori_loop(kj + 1, nq, q_step, 0)
            return carry

        lax.fori_loop(0, nq, k_step, 0)

    spec = lambda d: pl.BlockSpec((1, S, d), lambda h, b: (h, b, 0))
    return _call_beside(
        body, transfer, grid=(H, B), name="attn_bwd",
        in_specs=[spec(QK_DIM), spec(QK_DIM), spec(V_DIM), spec(V_DIM), spec(V_DIM), spec(1)],
        out_specs=[spec(QK_DIM), spec(QK_DIM), spec(V_DIM)],
        out_shape=[SDS((H, B * S, QK_DIM), F32), SDS((H, B * S, QK_DIM), F32), SDS((H, B * S, V_DIM), F32)],
        scratch_shapes=[pltpu.VMEM((S, 1), F32)], semantics=("arbitrary", "arbitrary"),
        args=(q4, k4, v4, do4, o4, lse4))


def _gdn_bwd(qg, kg, vg, gates, states, ainv, u4, w4, do4, B, S, transfer=None):
    H, D, C = GDN_HEADS, GDN_DIM, CHUNK
    NC = S // C
    U = GDN_UNROLL if NC % GDN_UNROLL == 0 else 1

    def body(q_ref, k_ref, v_ref, g_ref, st_ref, ai_ref, u_ref, w_ref, do_ref, dq_ref, dk_ref, dv_ref, dgb_ref,
             kd_s, x1_s, x2_s, el_s, dvn_s, ds_s, w2t_s):
        h = pl.program_id(0)
        lane = lax.broadcasted_iota(jnp.int32, (C, LANES), 1)
        ri = lax.broadcasted_iota(jnp.int32, (C, C), 0)
        ci = lax.broadcasted_iota(jnp.int32, (C, C), 1)
        rcol = lax.broadcasted_iota(jnp.int32, (C, 1), 0)

        def rsum(a):
            return jnp.sum(a, axis=-1, keepdims=True)

        def blocks(fn):
            def group(gi, c):
                _lockstep([fn(gi * U + j) for j in range(U)])
                return c
            lax.fori_loop(0, NC // U, group, 0)

        def prepare(n):
            cs = pl.multiple_of(n * C, C)
            q = q_ref[0, pl.ds(cs, C), :]
            k = k_ref[0, pl.ds(cs, C), :]
            do = do_ref[0, pl.ds(cs, C), :]
            Gc, bt, Gam, e, f, eL = _chunk_decays(g_ref[pl.ds(cs, C), :], lane, h, ri, ci, rcol)
            At = _mm_nt(q, k) * Gam
            yield
            x1 = _mm_tn(At, do)
            x2 = _mm_tn(q * e, do)
            kd = k * f
            w = w_ref[0, pl.ds(cs, C), :]
            yield
            x1_s[pl.ds(cs, C), :] = x1
            x2_s[n] = x2 - _mm_tn(w, x1)
            w2t_s[n] = _mm_tn(w, kd)
            kd_s[pl.ds(cs, C), :] = kd
            el_s[n] = jnp.broadcast_to(eL, (SUBLANES, LANES))

        blocks(prepare)

        def recur(t, dS):
            n = NC - 1 - t
            cs = pl.multiple_of(n * C, C)
            ds_s[n] = dS
            dvn_s[pl.ds(cs, C), :] = x1_s[pl.ds(cs, C), :] + _mm(kd_s[pl.ds(cs, C), :], dS)
            return x2_s[n] + el_s[n, 0:1, :] * dS - _mm(w2t_s[n], dS)

        lax.fori_loop(0, NC, recur, jnp.zeros((D, D), F32))

        def local(n):
            cs = pl.multiple_of(n * C, C)
            q = q_ref[0, pl.ds(cs, C), :]
            k = k_ref[0, pl.ds(cs, C), :]
            v = v_ref[0, pl.ds(cs, C), :]
            do = do_ref[0, pl.ds(cs, C), :]
            u = u_ref[0, pl.ds(cs, C), :]
            w = w_ref[0, pl.ds(cs, C), :]
            dvn = dvn_s[pl.ds(cs, C), :]
            dS = ds_s[n]
            Gc, bt, Gam, e, f, eL = _chunk_decays(g_ref[pl.ds(cs, C), :], lane, h, ri, ci, rcol)
            S0 = st_ref[0, n]
            Ainv = ai_ref[0, n]
            qk = _mm_nt(jnp.concatenate([q, k], axis=0), k)
            QK, KK = qk[:C], qk[C:]
            be = bt * e
            sol = jnp.concatenate([u, w], axis=-1)
            vn = u - _mm(w, S0)
            yield
            dAt = jnp.where(ri >= ci, _mm_nt(do, vn), 0.0)
            dqd = _mm_nt(do, S0)
            dw = -_mm_nt(dvn, S0)
            dkd = _mm_nt(vn, dS)
            deL = jnp.sum(rsum(dS * S0), axis=0, keepdims=True)
            yield
            dR = _mm_exact(Ainv.T, jnp.concatenate([dvn, dw], axis=-1))
            dR1, dR2 = dR[:, :D], dR[:, D:]
            yield
            dL = jnp.where(ri > ci, -_mm_nt(dR, sol), 0.0)
            yield
            dv_ref[0, pl.ds(cs, C), :] = dR1 * bt
            r2 = rsum(dR2 * k)
            X = dL * Gam
            dbt = rsum(dR1 * v) + r2 * e + rsum(X * KK)
            de = r2 * bt + rsum(dqd * q)
            dKK = X * bt
            dQK = dAt * Gam
            dq_ref[0, pl.ds(cs, C), :] = _mm(dQK, k) + dqd * e
            dk_ref[0, pl.ds(cs, C), :] = dR2 * be + _mm(dKK + dKK.T, k) + _mm_tn(dQK, q) + dkd * f
            df = rsum(dkd * k)
            Z = (dL * (bt * KK) + dAt * QK) * Gam
            dG = rsum(Z) - rsum(Z.T) + de * e - df * f
            dGl = jnp.sum(df * f, axis=0, keepdims=True) + deL * eL
            dG = dG + jnp.where(rcol == C - 1, dGl, 0.0)
            dgb_ref[0, pl.ds(cs, C), :] = jnp.where(lane == 0, dG, jnp.where(lane == 1, dbt, 0.0))

        blocks(local)

    spec = pl.BlockSpec((1, S, D), lambda h, b: (h, b, 0))
    return _call_beside(
        body, transfer, grid=(H, B), name="gdn_bwd",
        in_specs=[spec, spec, spec, pl.BlockSpec((S, LANES), lambda h, b: (b, 0)),
                  pl.BlockSpec((1, NC, D, D), lambda h, b: (h, b, 0, 0)),
                  pl.BlockSpec((1, NC, C, C), lambda h, b: (h, b, 0, 0)), spec, spec, spec],
        out_specs=[spec, spec, spec, spec],
        out_shape=[SDS((H, B * S, D), F32)] * 4,
        scratch_shapes=[pltpu.VMEM((S, D), F32), pltpu.VMEM((S, D), F32), pltpu.VMEM((NC, D, D), F32),
                        pltpu.VMEM((NC, SUBLANES, LANES), F32), pltpu.VMEM((S, D), F32),
                        pltpu.VMEM((NC, D, D), F32), pltpu.VMEM((NC, D, D), F32)],
        semantics=("arbitrary", "arbitrary"), args=(qg, kg, vg, gates, states, ainv, u4, w4, do4))


def _gdn_pre_bwd(proj, conv_w, alog_l, dt_l, dq4, dk4, dv4, dgb4, S):
    T = proj.shape[0]
    tm = min(256, T)
    tiles_per_seq = S // tm
    C3 = 3 * GDN_WIDTH
    H = GDN_HEADS

    def body(u_ref, halo_ref, gab_ref, w_ref, alog_ref, dt_ref, dq_ref, dk_ref, dv_ref, dgb_ref,
             dc_ref, dgab_ref, dcw_ref, dalog_ref, ddt_ref):
        i = pl.program_id(0)

        @pl.when(i == 0)
        def _():
            dcw_ref[...] = jnp.zeros_like(dcw_ref)
            dalog_ref[...] = jnp.zeros_like(dalog_ref)
            ddt_ref[...] = jnp.zeros_like(ddt_ref)

        halo = jnp.where(i % tiles_per_seq == 0, 0.0, halo_ref[...])
        c, sh = _conv_taps(u_ref[...], halo, w_ref[...])
        sg = _sigmoid(c)
        a = c * sg
        das = [None] * (3 * H)
        for h in range(H):
            xq = a[:, h * GDN_DIM:(h + 1) * GDN_DIM]
            xk = a[:, GDN_WIDTH + h * GDN_DIM:GDN_WIDTH + (h + 1) * GDN_DIM]
            das[h] = _l2n_bwd(dq_ref[h], xq, GDN_QSCALE)
            das[H + h] = _l2n_bwd(dk_ref[h], xk, 1.0)
            das[2 * H + h] = dv_ref[h]
        dc = jnp.concatenate(das, axis=-1) * (sg * (1.0 + c * (1.0 - sg)))
        dc_ref[...] = dc
        dcw_ref[...] += jnp.concatenate(
            [jnp.sum(dc * sh[CONV_W - 1 - t], axis=0, keepdims=True) for t in range(CONV_W)], axis=0)
        lane = lax.broadcasted_iota(jnp.int32, (tm, LANES), 1)
        ric = lax.broadcasted_iota(jnp.int32, (tm, LANES), 0) % CHUNK
        dG = jnp.zeros((tm, LANES), F32)
        for h in range(H):
            t = dgb_ref[h]
            dG = dG + jnp.where(lane == h, _pick_lane(t, lane, 0), 0.0) \
                    + jnp.where(lane == h + H, _pick_lane(t, lane, 1), 0.0)
        is_g = lane < H
        dg = jnp.where(is_g, _chunk_rev_cumsum(jnp.where(is_g, dG, 0.0), ric), 0.0)
        gab = gab_ref[...]
        g, beta = _gate_values(gab, alog_ref[...], dt_ref[...], lane)
        dga = jnp.where(is_g, dg * (-jnp.exp(alog_ref[...])) * _sigmoid(gab + dt_ref[...]), 0.0)
        dgb = jnp.where(is_g, 0.0, dG) * beta * (1.0 - beta)
        dgab_ref[...] = dga + dgb
        dalog_ref[...] += jnp.sum(dg * g, axis=0, keepdims=True)
        ddt_ref[...] += jnp.sum(dga, axis=0, keepdims=True)

    hspec = pl.BlockSpec((H, tm, GDN_DIM), lambda i: (0, i, 0))
    vec = pl.BlockSpec((1, LANES), lambda i: (0, 0))
    return pl.pallas_call(
        body, grid=(T // tm,), name="gdn_pre_bwd",
        in_specs=[pl.BlockSpec((tm, C3), lambda i: (i, 0)),
                  pl.BlockSpec((SUBLANES, C3), lambda i: (jnp.maximum(i * (tm // SUBLANES) - 1, 0), 0)),
                  pl.BlockSpec((tm, LANES), lambda i: (i, P_GAB // LANES)),
                  pl.BlockSpec((CONV_W, C3), lambda i: (0, 0)), vec, vec, hspec, hspec, hspec, hspec],
        out_specs=[pl.BlockSpec((tm, C3), lambda i: (i, 0)), pl.BlockSpec((tm, LANES), lambda i: (i, 0)),
                   pl.BlockSpec((CONV_W, C3), lambda i: (0, 0)), vec, vec],
        out_shape=[SDS((T, C3), F32), SDS((T, LANES), F32), SDS((CONV_W, C3), F32),
                   SDS((1, LANES), F32), SDS((1, LANES), F32)],
        compiler_params=_params(("arbitrary",)),
    )(proj, proj, proj, conv_w, alog_l, dt_l, dq4, dk4, dv4, dgb4)


def _conv_bwd_input(dc, conv_w, S):
    T, C3 = dc.shape
    tm = min(256, T)
    tiles_per_seq = S // tm
    nblk = T // SUBLANES

    def body(dc_ref, nxt_ref, w_ref, du_ref):
        i = pl.program_id(0)
        nxt = jnp.where(i % tiles_per_seq == tiles_per_seq - 1, 0.0, nxt_ref[...])
        x = dc_ref[...]
        w = w_ref[...]
        du = w[3:4] * x
        for j in range(1, CONV_W):
            du = du + w[3 - j:4 - j] * _shift_up(x, nxt, j)
        du_ref[...] = du

    return pl.pallas_call(
        body, grid=(T // tm,), name="conv_bwd_input",
        in_specs=[pl.BlockSpec((tm, C3), lambda i: (i, 0)),
                  pl.BlockSpec((SUBLANES, C3), lambda i: (jnp.minimum((i + 1) * (tm // SUBLANES), nblk - 1), 0)),
                  pl.BlockSpec((CONV_W, C3), lambda i: (0, 0))],
        out_specs=pl.BlockSpec((tm, C3), lambda i: (i, 0)),
        out_shape=SDS((T, C3), F32),
        compiler_params=_params(("arbitrary",)),
    )(dc, dc, conv_w)


def _mla_pre_bwd(proj, cosf, sinf, w_qln, w_kvln, w_uq_p, w_ukv, qnw, knw, dq4, dk4, dv4):
    T = proj.shape[0]
    tm = min(256, T)
    H = MLA_HEADS

    def body(ql_ref, kvl_ref, kpe_ref, cos_ref, sin_ref, wq_ref, wkv_ref, uq_ref, ukv_ref, qnw_ref, knw_ref,
             dq_ref, dk_ref, dv_ref,
             dql_ref, dkvl_ref, dkpe_ref, dqraw_ref, dkvraw_ref, qn_ref, kvn_ref, dwq_ref, dwkv_ref, dqnw_ref, dknw_ref):
        @pl.when(pl.program_id(0) == 0)
        def _():
            for r in (dwq_ref, dwkv_ref, dqnw_ref, dknw_ref):
                r[...] = jnp.zeros_like(r)

        cos, sin = cos_ref[...], sin_ref[...]
        qnw_, knw_ = qnw_ref[...], knw_ref[...]
        ql, kvl = ql_ref[...], kvl_ref[...]
        kpe_raw = kpe_ref[...][:, :ROPE]
        qn, rq = _rms(ql, wq_ref[...])
        kvn, rkv = _rms(kvl, wkv_ref[...])
        qn_ref[...] = qn.astype(MXU_DTYPE)
        kvn_ref[...] = kvn.astype(MXU_DTYPE)
        qraw = _mm(qn, uq_ref[...])
        kvraw = _mm(kvn, ukv_ref[...])
        dq_nope, dq_pe, dkv_parts = [], [], []
        dqnw_n = jnp.zeros((1, NOPE), F32)
        dqnw_p = jnp.zeros((1, ROPE), F32)
        dknw_n = jnp.zeros((1, NOPE), F32)
        dkpe = jnp.zeros((tm, ROPE), F32)
        for h in range(H):
            dq = dq_ref[h] * ATT_SCALE
            x = qraw[:, h * NOPE:(h + 1) * NOPE]
            dx, dw = _rms_bwd(dq[:, :NOPE], x, qnw_[:, :NOPE], _rms(x, qnw_[:, :NOPE])[1])
            dq_nope.append(dx)
            dqnw_n = dqnw_n + dw
            x = qraw[:, H * NOPE + h * ROPE:H * NOPE + (h + 1) * ROPE]
            dx, dw = _rms_bwd(_rope_bwd(dq[:, NOPE:], cos, sin), x, qnw_[:, NOPE:], _rms(x, qnw_[:, NOPE:])[1])
            dq_pe.append(dx)
            dqnw_p = dqnw_p + dw
            dk = dk_ref[h]
            x = kvraw[:, h * 256:h * 256 + NOPE]
            dx, dw = _rms_bwd(dk[:, :NOPE], x, knw_[:, :NOPE], _rms(x, knw_[:, :NOPE])[1])
            dknw_n = dknw_n + dw
            dkpe = dkpe + dk[:, NOPE:]
            dkv_parts += [dx, dv_ref[h]]
        dx, dknw_p = _rms_bwd(_rope_bwd(dkpe, cos, sin), kpe_raw, knw_[:, NOPE:], _rms(kpe_raw, knw_[:, NOPE:])[1])
        dkpe_ref[...] = jnp.concatenate([dx, jnp.zeros((tm, LANES - ROPE), F32)], axis=-1)
        dqraw = jnp.concatenate(dq_nope + dq_pe, axis=-1).astype(MXU_DTYPE)
        dkvraw = jnp.concatenate(dkv_parts, axis=-1).astype(MXU_DTYPE)
        dqraw_ref[...] = dqraw
        dkvraw_ref[...] = dkvraw
        dx, dw = _rms_bwd(_mm_nt(dqraw, uq_ref[...]), ql, wq_ref[...], rq)
        dql_ref[...] = dx
        dwq_ref[...] += dw
        dx, dw = _rms_bwd(_mm_nt(dkvraw, ukv_ref[...]), kvl, wkv_ref[...], rkv)
        dkvl_ref[...] = dx
        dwkv_ref[...] += dw
        dqnw_ref[...] += jnp.concatenate([dqnw_n, dqnw_p], axis=-1)
        dknw_ref[...] += jnp.concatenate([dknw_n, dknw_p], axis=-1)

    full = lambda a: pl.BlockSpec(a.shape, lambda i: (0,) * a.ndim)
    rows = lambda n: pl.BlockSpec((tm, n), lambda i: (i, 0))
    const = lambda n: pl.BlockSpec((1, n), lambda i: (0, 0))
    NQ, NKV = w_uq_p.shape[1], w_ukv.shape[1]
    return pl.pallas_call(
        body, grid=(T // tm,), name="mla_pre_bwd",
        in_specs=[pl.BlockSpec((tm, 256), lambda i: (i, P_QLAT // 256)),
                  pl.BlockSpec((tm, 256), lambda i: (i, P_KVLAT // 256)),
                  pl.BlockSpec((tm, 128), lambda i: (i, P_KPE // 128)),
                  rows(ROPE), rows(ROPE),
                  full(w_qln), full(w_kvln), full(w_uq_p), full(w_ukv), full(qnw), full(knw),
                  pl.BlockSpec((H, tm, QK_DIM), lambda i: (0, i, 0)),
                  pl.BlockSpec((H, tm, QK_DIM), lambda i: (0, i, 0)),
                  pl.BlockSpec((H, tm, V_DIM), lambda i: (0, i, 0))],
        out_specs=[rows(Q_LORA), rows(KV_LORA), rows(LANES), rows(NQ), rows(NKV), rows(Q_LORA), rows(KV_LORA),
                   const(Q_LORA), const(KV_LORA), const(QK_DIM), const(QK_DIM)],
        out_shape=[SDS((T, Q_LORA), F32), SDS((T, KV_LORA), F32), SDS((T, LANES), F32),
                   SDS((T, NQ), MXU_DTYPE), SDS((T, NKV), MXU_DTYPE),
                   SDS((T, Q_LORA), MXU_DTYPE), SDS((T, KV_LORA), MXU_DTYPE),
                   SDS((1, Q_LORA), F32), SDS((1, KV_LORA), F32), SDS((1, QK_DIM), F32), SDS((1, QK_DIM), F32)],
        compiler_params=_params(("arbitrary",)),
    )(proj, proj, proj, cosf, sinf, w_qln, w_kvln, w_uq_p, w_ukv, qnw, knw, dq4, dk4, dv4)


def _in_proj_bwd(dgqkv, dgz, dql, dkvl, dkpe, dgab, w_in_p, dh, x2, w_an):
    T, D = x2.shape
    N = w_in_p.shape[1]
    tm = min(512, T)

    def body(a_ref, b_ref, c_ref, d_ref, e_ref, f_ref, w_ref, dh_ref, x_ref, wn_ref, dx_ref, dp_ref, dwn_ref):
        @pl.when(pl.program_id(0) == 0)
        def _():
            dwn_ref[...] = jnp.zeros_like(dwn_ref)

        dp = jnp.concatenate([a_ref[...], b_ref[...], c_ref[...], d_ref[...], e_ref[...], f_ref[...]],
                             axis=-1).astype(MXU_DTYPE)
        dp_ref[...] = dp
        x = x_ref[...]
        _, r = _rms(x, wn_ref[...])
        dx, dw = _rms_bwd(_mm_nt(dp, w_ref[...]), x, wn_ref[...], r)
        dx_ref[...] = dh_ref[...] + dx
        dwn_ref[...] += dw

    rows = lambda n: pl.BlockSpec((tm, n), lambda i: (i, 0))
    return pl.pallas_call(
        body, grid=(T // tm,), name="in_proj_bwd",
        in_specs=[rows(dgqkv.shape[1]), rows(dgz.shape[1]), rows(dql.shape[1]), rows(dkvl.shape[1]),
                  rows(dkpe.shape[1]), rows(dgab.shape[1]),
                  pl.BlockSpec((D, N), lambda i: (0, 0)), rows(D), rows(D), pl.BlockSpec((1, D), lambda i: (0, 0))],
        out_specs=[rows(D), rows(N), pl.BlockSpec((1, D), lambda i: (0, 0))],
        out_shape=[SDS((T, D), F32), SDS((T, N), MXU_DTYPE), SDS((1, D), F32)],
        compiler_params=_params(("arbitrary",)),
    )(dgqkv, dgz, dql, dkvl, dkpe, dgab, w_in_p, dh, x2, w_an)


def _wgrad(a, b, name, column_shards=False):
    T, M = a.shape
    N = b.shape[1]
    tM = _divisor_tile(M, 1024)
    tN = N // N_DEV if column_shards else _divisor_tile(N, 1536)
    tk = min(T, 1024)
    nk = T // tk

    def body(a_ref, b_ref, o_ref, acc):
        k = pl.program_id(2)

        @pl.when(k == 0)
        def _():
            acc[...] = jnp.zeros_like(acc)

        acc[...] += _mm_tn(a_ref[...], b_ref[...])

        @pl.when(k == nk - 1)
        def _():
            o_ref[...] = acc[...].astype(WIRE_DTYPE).reshape(o_ref.shape)

    if column_shards:
        out_spec, out_shape = pl.BlockSpec((1, tM, tN), lambda i, j, k: (j, i, 0)), SDS((N_DEV, M, tN), WIRE_DTYPE)
    else:
        out_spec, out_shape = pl.BlockSpec((tM, tN), lambda i, j, k: (i, j)), SDS((M, N), WIRE_DTYPE)
    return pl.pallas_call(
        body, grid=(M // tM, N // tN, nk), name=name,
        in_specs=[pl.BlockSpec((tk, tM), lambda i, j, k: (k, i)), pl.BlockSpec((tk, tN), lambda i, j, k: (k, j))],
        out_specs=out_spec, out_shape=out_shape,
        scratch_shapes=[pltpu.VMEM((tM, tN), F32)],
        compiler_params=_params(("arbitrary", "arbitrary", "arbitrary")),
    )(a, b)


def _adamw(g, w, m, v):
    m = ADAM_B1 * m + (1.0 - ADAM_B1) * g
    v = ADAM_B2 * v + (1.0 - ADAM_B2) * jnp.square(g)
    m_hat = m / (1.0 - ADAM_B1 ** ADAM_STEP)
    v_hat = v / (1.0 - ADAM_B2 ** ADAM_STEP)
    return -ADAM_LR * (m_hat / (jnp.sqrt(v_hat) + ADAM_EPS) + ADAM_WD * w), m, v


def _reduce_adamw(parts, w, m, v, name):
    R, C = w.shape
    _, Rp, Cp = parts.shape
    tr = min(R, 256)
    tp = tr if Rp == R else Rp

    def body(p_ref, w_ref, m_ref, v_ref, g_ref, d_ref, nm_ref, nv_ref):
        g = p_ref[0].astype(F32)
        for s in range(1, N_DEV):
            g = g + p_ref[s].astype(F32)
        g = g[:tr, :C]
        g_ref[...] = g
        d_ref[...], nm_ref[...], nv_ref[...] = _adamw(g, w_ref[...], m_ref[...], v_ref[...])

    spec = pl.BlockSpec((tr, C), lambda i: (i, 0))
    return pl.pallas_call(
        body, grid=(R // tr,), name=name,
        in_specs=[pl.BlockSpec((N_DEV, tp, Cp), lambda i: (0, i, 0)), spec, spec, spec],
        out_specs=[spec] * 4, out_shape=[SDS((R, C), F32)] * 4,
        compiler_params=_params(("arbitrary",)),
    )(parts, w, m, v)


SMALL_ROWS, SMALL_COLS = 16, 1024
SMALL_LAYOUT = (
    ("attn_norm_w", 0, 1, 1024, 1024), ("mlp_norm_w", 1, 1, 1024, 1024), ("q_lat_norm_w", 2, 1, 256, 256),
    ("kv_lat_norm_w", 3, 1, 256, 256), ("q_norm_w", 4, 1, 192, 192), ("k_norm_w", 5, 1, 192, 192),
    ("mla_out_norm_w", 6, 4, 128, 128), ("a_log", 10, 1, 128, 4), ("dt_bias", 11, 1, 128, 4),
    ("gdn_norm_w", 12, 1, 128, 128))


def _adamw_replicated(parts, ws, ms, vs):
    n = len(SMALL_LAYOUT)

    def body(*refs):
        p_ref = refs[0]
        w_refs, m_refs, v_refs = refs[1:1 + n], refs[1 + n:1 + 2 * n], refs[1 + 2 * n:1 + 3 * n]
        outs = refs[1 + 3 * n:]
        s = p_ref[0]
        for d in range(1, N_DEV):
            s = s + p_ref[d]
        for i, (_, r0, nr, _, pw) in enumerate(SMALL_LAYOUT):
            g = s[r0:r0 + nr, :pw]
            outs[i][...] = g
            outs[n + i][...], outs[2 * n + i][...], outs[3 * n + i][...] = _adamw(
                g, w_refs[i][...], m_refs[i][...], v_refs[i][...])

    res = pl.pallas_call(
        body, name="adamw_replicated",
        out_shape=[SDS(w.shape, F32) for w in ws] * 4,
        compiler_params=_params(),
    )(parts, *ws, *ms, *vs)
    return [res[k * n:(k + 1) * n] for k in range(4)]


COPIES_PER_ARRAY = N_DEV - 1


def _two_level_gather(srcs, outs, send_sems, recv_sems, local_sems=None, stage="all"):
    mx, my, mc = lax.axis_index("x"), lax.axis_index("y"), lax.axis_index("c")
    me, sibling = (mx, my, mc), (mx, my, 1 - mc)
    chips = [(1 - mx, my), (mx, 1 - my), (1 - mx, 1 - my)]
    arrays = range(len(srcs))

    def copy(a, k, block, to, src=None):
        px, py, pc = block
        slot = outs[a].at[4 * px + 2 * py + pc]
        sem = a * COPIES_PER_ARRAY + k
        return pltpu.make_async_remote_copy(
            src_ref=slot if src is None else src, dst_ref=slot,
            send_sem=send_sems.at[sem], recv_sem=recv_sems.at[sem], device_id=to, device_id_type=MESH_ID)

    mine = [] if local_sems is None else [
        pltpu.make_async_copy(srcs[a], outs[a].at[4 * mx + 2 * my + mc], local_sems.at[a]) for a in arrays]
    first = []
    for a in arrays:
        first.append(copy(a, 0, me, sibling, src=srcs[a]))
        first += [copy(a, 1 + j, me, (*chip, mc), src=srcs[a]) for j, chip in enumerate(chips)]
    if stage in ("all", "start"):
        for cp in mine + first:
            cp.start()
    if stage in ("all", "finish"):
        forwards = []
        for j, chip in enumerate(chips):
            for a in arrays:
                copy(a, 1 + j, (*chip, mc), me).wait_recv()
                fwd = copy(a, 4 + j, (*chip, mc), sibling)
                fwd.start()
                forwards.append(fwd)
        for a in arrays:
            copy(a, 0, sibling, me).wait_recv()
        for j, chip in enumerate(chips):
            for a in arrays:
                copy(a, 4 + j, (*chip, 1 - mc), me).wait_recv()
        for cp in first + forwards:
            cp.wait_send()
        for cp in mine:
            cp.wait()


def _comm_scratch(n):
    return [pltpu.SemaphoreType.DMA((n * COPIES_PER_ARRAY,)), pltpu.SemaphoreType.DMA((n * COPIES_PER_ARRAY,)),
            pltpu.SemaphoreType.DMA((n,))]


def _any_specs(n):
    return [pl.BlockSpec(memory_space=pl.ANY)] * n


def _gather_weights(shards):
    n = len(shards)

    def body(*refs):
        _two_level_gather(refs[:n], refs[n:2 * n], *refs[2 * n:])

    return pl.pallas_call(
        body, name="gather_weights",
        out_shape=[SDS((N_DEV,) + s.shape, s.dtype) for s in shards],
        in_specs=_any_specs(n), out_specs=_any_specs(n), scratch_shapes=_comm_scratch(n),
    )(*shards)


def _gather_small_grads(gs):
    n = len(gs)

    def body(*refs):
        g_refs, out_ref = refs[:n], refs[n]
        tile, send_sems, recv_sems = refs[n + 1:]
        tile[...] = jnp.zeros_like(tile)
        for (_, r0, nr, gw, _), g in zip(SMALL_LAYOUT, g_refs):
            tile[r0:r0 + nr, 0:gw] = g[...]
        me = 4 * lax.axis_index("x") + 2 * lax.axis_index("y") + lax.axis_index("c")
        out_ref[me] = tile[...]
        _two_level_gather([tile], [out_ref], send_sems, recv_sems)

    return pl.pallas_call(
        body, name="gather_small_grads",
        out_shape=SDS((N_DEV, SMALL_ROWS, SMALL_COLS), F32),
        in_specs=[pl.BlockSpec(memory_space=pltpu.VMEM)] * n,
        out_specs=pl.BlockSpec(memory_space=pltpu.VMEM),
        scratch_shapes=[pltpu.VMEM((SMALL_ROWS, SMALL_COLS), F32),
                        pltpu.SemaphoreType.DMA((COPIES_PER_ARRAY,)), pltpu.SemaphoreType.DMA((COPIES_PER_ARRAY,))],
    )(*gs)


def _exchange_grads(slabs):
    n = len(slabs)

    def body(*refs):
        _exchange(refs[:n], refs[n:2 * n], *refs[2 * n:])

    return pl.pallas_call(
        body, name="exchange_grads",
        out_shape=[SDS(s.shape, s.dtype) for s in slabs],
        in_specs=_any_specs(n), out_specs=_any_specs(n), scratch_shapes=_comm_scratch(n),
    )(*slabs)


class _Transfer:
    def __init__(self, kind, arrays):
        self.kind, self.arrays, self.n = kind, list(arrays), len(arrays)

    def out_shapes(self):
        if self.kind == "gather":
            return [SDS((N_DEV,) + a.shape, a.dtype) for a in self.arrays]
        return [SDS(a.shape, a.dtype) for a in self.arrays]

    def run(self, srcs, outs, sems, stage):
        fn = _two_level_gather if self.kind == "gather" else _exchange
        fn(srcs, outs, *sems, stage=stage)


def _call_beside(body, transfer, *, grid, in_specs, out_specs, out_shape, scratch_shapes, name, semantics, args):
    if transfer is None:
        res = pl.pallas_call(body, grid=grid, in_specs=in_specs, out_specs=out_specs, out_shape=out_shape,
                             scratch_shapes=scratch_shapes, name=name, compiler_params=_params(semantics))(*args)
        return list(res), []
    n_in, n_out, n_s, n = len(in_specs), len(out_specs), len(scratch_shapes), transfer.n

    def wrapped(*refs):
        ins, refs = refs[:n_in], refs[n_in:]
        t_in, refs = refs[:n], refs[n:]
        outs, refs = refs[:n_out], refs[n_out:]
        t_out, refs = refs[:n], refs[n:]
        scratch, sems = refs[:n_s], refs[n_s:]
        first = functools.reduce(jnp.logical_and, [pl.program_id(i) == 0 for i in range(len(grid))])
        last = functools.reduce(jnp.logical_and, [pl.program_id(i) == g - 1 for i, g in enumerate(grid)])

        @pl.when(first)
        def _():
            transfer.run(t_in, t_out, sems, "start")

        body(*ins, *outs, *scratch)

        @pl.when(last)
        def _():
            transfer.run(t_in, t_out, sems, "finish")

    res = pl.pallas_call(
        wrapped, grid=grid, in_specs=list(in_specs) + _any_specs(n), out_specs=list(out_specs) + _any_specs(n),
        out_shape=list(out_shape) + transfer.out_shapes(), scratch_shapes=list(scratch_shapes) + _comm_scratch(n),
        name=name, compiler_params=_params(semantics))(*args, *transfer.arrays)
    return list(res[:n_out]), list(res[n_out:])


EXCHANGE_FLIPS = ((0, 0, 1), (1, 0, 0), (0, 1, 0), (1, 1, 0), (1, 0, 1), (0, 1, 1), (1, 1, 1))


def _exchange(srcs, outs, send_sems, recv_sems, local_sems, stage="all"):
    mx, my, mc = lax.axis_index("x"), lax.axis_index("y"), lax.axis_index("c")
    arrays = range(len(srcs))
    copies = [pltpu.make_async_copy(srcs[a].at[4 * mx + 2 * my + mc], outs[a].at[N_DEV - 1], local_sems.at[a])
              for a in arrays]
    for k, (fx, fy, fc) in enumerate(EXCHANGE_FLIPS):
        px = 1 - mx if fx else mx
        py = 1 - my if fy else my
        pc = 1 - mc if fc else mc
        for a in arrays:
            sem = a * COPIES_PER_ARRAY + k
            copies.append(pltpu.make_async_remote_copy(
                src_ref=srcs[a].at[4 * px + 2 * py + pc], dst_ref=outs[a].at[k],
                send_sem=send_sems.at[sem], recv_sem=recv_sems.at[sem],
                device_id=(px, py, pc), device_id_type=MESH_ID))
    if stage in ("all", "start"):
        for cp in copies:
            cp.start()
    if stage in ("all", "finish"):
        for cp in copies:
            cp.wait()


def _w_in_to_padded(w):
    z = lambda n: jnp.zeros((w.shape[0], n), w.dtype)
    return jnp.concatenate([w[:, O_GQKV:O_GZ], w[:, O_GZ:O_GAB], w[:, O_QLAT:O_KVLAT], w[:, O_KVLAT:O_KPE],
                            w[:, O_KPE:O_GQKV], z(P_GAB - P_KPE - ROPE), w[:, O_GAB:O_END],
                            z(P_WIDTH - P_GAB - (O_END - O_GAB))], axis=1)


def _w_in_from_padded(wp):
    return jnp.concatenate([wp[:, P_QLAT:P_QLAT + 256], wp[:, P_KVLAT:P_KVLAT + 256], wp[:, P_KPE:P_KPE + ROPE],
                            wp[:, P_GQKV:P_GZ], wp[:, P_GZ:P_QLAT], wp[:, P_GAB:P_GAB + (O_END - O_GAB)]], axis=1)


def _w_uq_to_headsplit(w):
    w3 = w.reshape(w.shape[0], MLA_HEADS, QK_DIM)
    return jnp.concatenate([w3[:, :, :NOPE].reshape(w.shape[0], -1), w3[:, :, NOPE:].reshape(w.shape[0], -1)], axis=1)


def _w_uq_from_headsplit(wp):
    n = wp[:, :MLA_HEADS * NOPE].reshape(wp.shape[0], MLA_HEADS, NOPE)
    p = wp[:, MLA_HEADS * NOPE:].reshape(wp.shape[0], MLA_HEADS, ROPE)
    return jnp.concatenate([n, p], axis=2).reshape(wp.shape[0], -1)


def _lane_vec(v4):
    return jnp.pad(v4.reshape(1, -1), ((0, 0), (0, LANES - v4.shape[-1])))


def _local_step(x, positions, target, attn_norm_w, w_in, q_lat_norm_w, w_uq, kv_lat_norm_w, w_ukv, q_norm_w,
                k_norm_w, mla_out_norm_w, conv_w, a_log, dt_bias, gdn_norm_w, w_out, mlp_norm_w, w_up, w_down,
                late_shards=None, exchange=False):
    B, S, D = x.shape
    T = B * S
    x2 = x.reshape(T, D)
    t2 = target.reshape(T, D)
    half = ROPE // 2
    inv_freq = ROPE_THETA ** (-jnp.arange(half, dtype=F32) / half)
    ang = positions.reshape(T, 1).astype(F32) * inv_freq
    cosf = jnp.concatenate([jnp.cos(ang)] * 2, axis=-1)
    sinf = jnp.concatenate([jnp.sin(ang)] * 2, axis=-1)
    w_in_p = _w_in_to_padded(w_in)
    w_uq_p = _w_uq_to_headsplit(w_uq)
    alog_l, dt_l = _lane_vec(a_log), _lane_vec(dt_bias)
    w_an, w_qln, w_kvln, qnw, knw, w_mn, gdn_w = (
        attn_norm_w, q_lat_norm_w, kv_lat_norm_w, q_norm_w, k_norm_w, mlp_norm_w, gdn_norm_w)

    proj, xn = _in_proj(x2, w_an, w_in_p)
    q4, k4, v4 = _mla_pre(proj, cosf, sinf, w_qln, w_kvln, w_uq_p, w_ukv, qnw, knw)
    gather = None if late_shards is None else _Transfer("gather", late_shards[:1])
    (o_mla, lse), late = _attn_fwd(q4, k4, v4, B, S, gather)
    if late:
        w_out = late[0].reshape(-1, D)
    qg, kg, vg, gates = _gdn_pre(proj, conv_w, alog_l, dt_l, S)
    gather = None if late_shards is None else _Transfer("gather", late_shards[1:])
    (o_gdn, states, ainv, u4, w4), late = _gdn_fwd(qg, kg, vg, gates, B, S, gather)
    if late:
        w_up, w_down = late[0], late[1].reshape(-1, D)
    h2, mix = _mix_out(o_mla, o_gdn, proj, x2, mla_out_norm_w, gdn_w, w_out)
    up, hn, dy, sq = _mlp_fwd(h2, w_mn, w_up, w_down, t2)
    loss = (0.5 / D) * jnp.sum(sq[:, 0, 0])

    dh, dhb, dup, act, dyb, d_mlp_norm = _mlp_bwd(dy, up, h2, w_mn, w_up, w_down)
    g_w_down = _wgrad(act, dyb, "wgrad_down")
    g_w_up = _wgrad(hn, dup, "wgrad_up", column_shards=True)
    do_mla, do_gdn, dz, d_mla_w, d_gdn_w = _mix_bwd(dhb, o_mla, o_gdn, proj, mla_out_norm_w, gdn_w, w_out)
    g_w_out = _wgrad(mix, dhb, "wgrad_out")
    first = ("w_down",)
    second = ("w_up", "w_out", "w_uq", "w_ukv")
    mats = dict(w_up=g_w_up, w_down=g_w_down, w_out=g_w_out)
    send = _Transfer("exchange", [_slabs(n, mats[n]) for n in first]) if exchange else None
    (dq4, dk4, dv4), got = _attn_bwd(q4, k4, v4, do_mla, o_mla, lse, B, S, send)
    mats.update(zip(first, got))
    dql, dkvl, dkpe, dqraw, dkvraw, qn, kvn, d_wqln, d_wkvln, d_qnw, d_knw = _mla_pre_bwd(
        proj, cosf, sinf, w_qln, w_kvln, w_uq_p, w_ukv, qnw, knw, dq4, dk4, dv4)
    mats.update(w_uq=_wgrad(qn, dqraw, "wgrad_uq"), w_ukv=_wgrad(kvn, dkvraw, "wgrad_ukv"))
    send = _Transfer("exchange", [_slabs(n, mats[n]) for n in second]) if exchange else None
    (dqg, dkg, dvg, dgb4), got = _gdn_bwd(qg, kg, vg, gates, states, ainv, u4, w4, do_gdn, B, S, send)
    mats.update(zip(second, got))
    dc, dgab, g_conv, d_alog, d_dt = _gdn_pre_bwd(proj, conv_w, alog_l, dt_l, dqg, dkg, dvg, dgb4, S)
    dgqkv = _conv_bwd_input(dc, conv_w, S)
    grad_x2, dproj, d_attn_norm = _in_proj_bwd(dgqkv, dz, dql, dkvl, dkpe, dgab, w_in_p, dh, x2, w_an)
    mats.update(w_in=_wgrad(xn, dproj, "wgrad_in"), conv_w=g_conv)
    if exchange:
        last = ("w_in", "conv_w")
        mats.update(zip(last, _exchange_grads([_slabs(n, mats[n]) for n in last])))
    small = dict(attn_norm_w=d_attn_norm, mlp_norm_w=d_mlp_norm, q_lat_norm_w=d_wqln, kv_lat_norm_w=d_wkvln,
                 q_norm_w=d_qnw, k_norm_w=d_knw, mla_out_norm_w=d_mla_w, a_log=d_alog, dt_bias=d_dt,
                 gdn_norm_w=d_gdn_w)
    return loss, grad_x2.reshape(B, S, D), mats, [small[n] for n, *_ in SMALL_LAYOUT]


BIG = ("w_in", "w_uq", "w_ukv", "conv_w", "w_out", "w_up", "w_down")
ALL_W = ("attn_norm_w", "w_in", "q_lat_norm_w", "w_uq", "kv_lat_norm_w", "w_ukv", "q_norm_w", "k_norm_w",
         "mla_out_norm_w", "conv_w", "a_log", "dt_bias", "gdn_norm_w", "w_out", "mlp_norm_w", "w_up", "w_down")
WIRE_SHAPE = {"w_in": (1024, 384), "w_uq": (256, 128), "conv_w": (16, 256)}


def _pad2(a, rows, cols):
    return jnp.pad(a, [(0, 0)] * (a.ndim - 2) + [(0, rows - a.shape[-2]), (0, cols - a.shape[-1])])


def _cols_to_full(stack, cols):
    return jnp.moveaxis(stack[:, :, :cols], 0, 1).reshape(stack.shape[1], N_DEV * cols)


def _full_to_cols(full, wire_cols):
    r, n = full.shape
    return _pad2(jnp.moveaxis(full.reshape(r, N_DEV, n // N_DEV), 1, 0), r, wire_cols)


def _slabs(name, g):
    if name == "w_in":
        return _full_to_cols(_w_in_from_padded(g), WIRE_SHAPE["w_in"][1])
    if name == "w_uq":
        return _full_to_cols(_w_uq_from_headsplit(g), WIRE_SHAPE["w_uq"][1])
    if name == "w_ukv":
        return _full_to_cols(g, g.shape[1] // N_DEV)
    if name == "conv_w":
        return _pad2(_full_to_cols(g.astype(WIRE_DTYPE), g.shape[1] // N_DEV), *WIRE_SHAPE["conv_w"])
    if name == "w_up":
        return g
    return g.reshape(N_DEV, -1, g.shape[-1])


def kernel(x, positions, attn_norm_w, w_in, q_lat_norm_w, w_uq, kv_lat_norm_w, w_ukv, q_norm_w, k_norm_w, mla_out_norm_w, conv_w, a_log, dt_bias, gdn_norm_w, w_out, mlp_norm_w, w_up, w_down, loss_target, m_attn_norm_w, m_w_in, m_q_lat_norm_w, m_w_uq, m_kv_lat_norm_w, m_w_ukv, m_q_norm_w, m_k_norm_w, m_mla_out_norm_w, m_conv_w, m_a_log, m_dt_bias, m_gdn_norm_w, m_w_out, m_mlp_norm_w, m_w_up, m_w_down, v_attn_norm_w, v_w_in, v_q_lat_norm_w, v_w_uq, v_kv_lat_norm_w, v_w_ukv, v_q_norm_w, v_k_norm_w, v_mla_out_norm_w, v_conv_w, v_a_log, v_dt_bias, v_gdn_norm_w, v_w_out, v_mlp_norm_w, v_w_up, v_w_down):
    env = dict(locals())
    W = {n: env[n][0] for n in ALL_W}
    Mo = {n: env["m_" + n][0] for n in ALL_W}
    Vo = {n: env["v_" + n][0] for n in ALL_W}

    two_d = lambda a: a.reshape(1, -1) if a.ndim == 1 else a
    D = x.shape[-1]

    s_in, s_uq, s_ukv, s_conv = _gather_weights([
        _pad2(W["w_in"].astype(WIRE_DTYPE), *WIRE_SHAPE["w_in"]),
        _pad2(W["w_uq"].astype(WIRE_DTYPE), *WIRE_SHAPE["w_uq"]),
        W["w_ukv"].astype(WIRE_DTYPE), _pad2(W["conv_w"], *WIRE_SHAPE["conv_w"])])
    late = [W["w_out"].astype(WIRE_DTYPE), W["w_up"].astype(WIRE_DTYPE), W["w_down"].astype(WIRE_DTYPE)]

    loss, grad_x, parts, gs = _local_step(
        x, positions, loss_target, two_d(W["attn_norm_w"]), _cols_to_full(s_in, W["w_in"].shape[1]),
        two_d(W["q_lat_norm_w"]), _cols_to_full(s_uq, W["w_uq"].shape[1]), two_d(W["kv_lat_norm_w"]),
        _cols_to_full(s_ukv, W["w_ukv"].shape[1]), two_d(W["q_norm_w"]), two_d(W["k_norm_w"]),
        W["mla_out_norm_w"], _cols_to_full(s_conv[:, :CONV_W], W["conv_w"].shape[1]), two_d(W["a_log"]),
        two_d(W["dt_bias"]), two_d(W["gdn_norm_w"]), None, two_d(W["mlp_norm_w"]), None, None,
        late_shards=late, exchange=True)
    loss = lax.psum(loss, ("x", "y", "c"))
    done = {n: _reduce_adamw(parts[n], W[n], Mo[n], Vo[n], "adamw_" + n) for n in BIG}
    names = [n for n, *_ in SMALL_LAYOUT]
    small = _adamw_replicated(_gather_small_grads(gs), [two_d(W[n]) for n in names], [two_d(Mo[n]) for n in names],
                              [two_d(Vo[n]) for n in names])
    for i, n in enumerate(names):
        done[n] = [small[kind][i] for kind in range(4)]
    res = [done[n][kind].reshape(env[n].shape) for kind in range(4) for n in ALL_W]
    return (loss, grad_x, *res)
```

```python
import functools

import jax
import jax.numpy as jnp
from jax import lax
from jax.experimental import pallas as pl
from jax.experimental.pallas import tpu as pltpu

F32 = jnp.float32
MXU_DTYPE = jnp.bfloat16
WIRE_DTYPE = jnp.bfloat16
SDS = jax.ShapeDtypeStruct
HIGHEST = lax.Precision.HIGHEST
MESH_ID = pl.DeviceIdType.MESH

D_MODEL = 1024
MLA_HEADS = 4
Q_LORA = 256
KV_LORA = 256
NOPE = 128
ROPE = 64
QK_DIM = NOPE + ROPE
V_DIM = 128
ROPE_THETA = 10000.0
GDN_HEADS = 4
GDN_DIM = 128
GDN_WIDTH = GDN_HEADS * GDN_DIM
CONV_W = 4
CHUNK = 64
D_FF = 4 * D_MODEL
EPS = 1e-6
ATT_SCALE = QK_DIM ** -0.5
GDN_QSCALE = GDN_DIM ** -0.5
N_DEV = 8
ATTN_BLOCK = 512
ATTN_CHAINS = 2
MLP_FWD_SHARDS = 4
MLP_BWD_SHARDS = 2

ADAM_LR = 0.001
ADAM_B1 = 0.9
ADAM_B2 = 0.999
ADAM_EPS = 1e-08
ADAM_WD = 0.01
ADAM_STEP = 10

LANES = 128
SUBLANES = 8
VMEM_LIMIT = 56 * 1024 * 1024

P_GQKV, P_GZ, P_QLAT, P_KVLAT, P_KPE, P_GAB = 0, 1536, 2048, 2304, 2560, 2688
P_WIDTH = 2816
O_QLAT, O_KVLAT, O_KPE, O_GQKV, O_GZ, O_GAB, O_END = 0, 256, 512, 576, 2112, 2624, 2632


def _params(sem=None, vmem=VMEM_LIMIT):
    kw = dict(vmem_limit_bytes=vmem)
    if sem is not None:
        kw["dimension_semantics"] = sem
    return pltpu.CompilerParams(**kw)


def _mm(a, b):
    return jnp.dot(a.astype(MXU_DTYPE), b.astype(MXU_DTYPE), preferred_element_type=F32)


def _mm_nt(a, b):
    return lax.dot_general(a.astype(MXU_DTYPE), b.astype(MXU_DTYPE), (((1,), (1,)), ((), ())),
                           preferred_element_type=F32)


def _mm_tn(a, b):
    return lax.dot_general(a.astype(MXU_DTYPE), b.astype(MXU_DTYPE), (((0,), (0,)), ((), ())),
                           preferred_element_type=F32)


def _split(a):
    hi = a.astype(MXU_DTYPE)
    return hi, (a - hi.astype(F32)).astype(MXU_DTYPE)


def _mm_split(a, b):
    (ah, al), (bh, bl) = a, b
    dot = lambda x, y: jnp.dot(x, y, preferred_element_type=F32)
    if MXU_DTYPE == F32:
        return dot(ah, bh)
    return dot(ah, bh) + dot(ah, bl) + dot(al, bh)


def _mm_exact(a, b):
    return _mm_split(_split(a), _split(b))


def _rms(x, w):
    r = lax.rsqrt(jnp.mean(x * x, axis=-1, keepdims=True) + EPS)
    return x * r * w, r


def _rms_bwd(dy, x, w, r):
    xh = x * r
    dyw = dy * w
    dx = r * (dyw - xh * jnp.mean(dyw * xh, axis=-1, keepdims=True))
    dw = jnp.sum(dy * xh, axis=0, keepdims=True)
    return dx, dw


def _l2n_bwd(dy, x, scale):
    r = lax.rsqrt(jnp.sum(x * x, axis=-1, keepdims=True) + EPS)
    xh = x * r
    return (scale * r) * (dy - xh * jnp.sum(dy * xh, axis=-1, keepdims=True))


def _rot(t):
    return jnp.concatenate([-t[:, ROPE // 2:], t[:, :ROPE // 2]], axis=-1)


def _rot_t(t):
    return jnp.concatenate([t[:, ROPE // 2:], -t[:, :ROPE // 2]], axis=-1)


def _rope(t, cos, sin):
    return t * cos + _rot(t) * sin


def _rope_bwd(d, cos, sin):
    return d * cos + _rot_t(d * sin)


def _sigmoid(x):
    return jax.nn.sigmoid(x)


def _shift_down(x, halo, j):
    if j == 0:
        return x
    xr = pltpu.roll(x, j, 0)
    hr = pltpu.roll(halo, j, 0)
    row = lax.broadcasted_iota(jnp.int32, halo.shape, 0)
    top = jnp.where(row < j, hr, xr[:SUBLANES])
    return jnp.concatenate([top, xr[SUBLANES:]], axis=0)


def _shift_up(x, nxt, j):
    if j == 0:
        return x
    n = x.shape[0]
    xr = pltpu.roll(x, n - j, 0)
    nr = pltpu.roll(nxt, SUBLANES - j, 0)
    row = lax.broadcasted_iota(jnp.int32, nxt.shape, 0)
    bot = jnp.where(row >= SUBLANES - j, nr, xr[n - SUBLANES:])
    return jnp.concatenate([xr[:n - SUBLANES], bot], axis=0)


def _chunk_cumsum(y, row_in_chunk):
    s = 1
    while s < CHUNK:
        y = y + jnp.where(row_in_chunk >= s, pltpu.roll(y, s, 0), 0.0)
        s *= 2
    return y


def _chunk_rev_cumsum(y, row_in_chunk):
    n = y.shape[0]
    s = 1
    while s < CHUNK:
        y = y + jnp.where(row_in_chunk + s < CHUNK, pltpu.roll(y, n - s, 0), 0.0)
        s *= 2
    return y


def _lockstep(generators):
    alive = list(generators)
    while alive:
        nxt = []
        for g in alive:
            try:
                next(g)
                nxt.append(g)
            except StopIteration:
                pass
        alive = nxt


def _pick_lane(tile, lane, idx):
    return jnp.sum(jnp.where(lane == idx, tile, 0.0), axis=-1, keepdims=True)


def _divisor_tile(n, cap, unit=LANES):
    best = unit
    t = unit
    while t <= min(n, cap):
        if n % t == 0:
            best = t
        t += unit
    return n if n <= cap else best


def _in_proj(x2, w_an, w_in_p):
    T, D = x2.shape
    N = w_in_p.shape[1]
    tm = min(512, T)

    def body(x_ref, wn_ref, w_ref, proj_ref, xn_ref):
        xn, _ = _rms(x_ref[...], wn_ref[...])
        xn = xn.astype(MXU_DTYPE)
        xn_ref[...] = xn
        proj_ref[...] = jnp.dot(xn, w_ref[...], preferred_element_type=F32)

    return pl.pallas_call(
        body, grid=(T // tm,), name="in_proj",
        in_specs=[pl.BlockSpec((tm, D), lambda i: (i, 0)), pl.BlockSpec((1, D), lambda i: (0, 0)),
                  pl.BlockSpec((D, N), lambda i: (0, 0))],
        out_specs=[pl.BlockSpec((tm, N), lambda i: (i, 0)), pl.BlockSpec((tm, D), lambda i: (i, 0))],
        out_shape=[SDS((T, N), F32), SDS((T, D), MXU_DTYPE)],
        compiler_params=_params(("arbitrary",)),
    )(x2, w_an, w_in_p)


def _mla_pre(proj, cosf, sinf, w_qln, w_kvln, w_uq_p, w_ukv, qnw, knw):
    T = proj.shape[0]
    tm = min(256, T)
    H = MLA_HEADS

    def body(ql_ref, kvl_ref, kpe_ref, cos_ref, sin_ref, wq_ref, wkv_ref, uq_ref, ukv_ref, qnw_ref, knw_ref,
             q_out, k_out, v_out):
        cos, sin = cos_ref[...], sin_ref[...]
        qnw_, knw_ = qnw_ref[...], knw_ref[...]
        qn, _ = _rms(ql_ref[...], wq_ref[...])
        kvn, _ = _rms(kvl_ref[...], wkv_ref[...])
        qraw = _mm(qn, uq_ref[...])
        kvraw = _mm(kvn, ukv_ref[...])
        kpe = _rope(_rms(kpe_ref[...][:, :ROPE], knw_[:, NOPE:])[0], cos, sin)
        for h in range(H):
            qn_h = _rms(qraw[:, h * NOPE:(h + 1) * NOPE], qnw_[:, :NOPE])[0]
            qp_h = _rope(_rms(qraw[:, H * NOPE + h * ROPE:H * NOPE + (h + 1) * ROPE], qnw_[:, NOPE:])[0], cos, sin)
            q_out[h] = (jnp.concatenate([qn_h, qp_h], axis=-1) * ATT_SCALE).astype(MXU_DTYPE)
            kn_h = _rms(kvraw[:, h * 256:h * 256 + NOPE], knw_[:, :NOPE])[0]
            k_out[h] = jnp.concatenate([kn_h, kpe], axis=-1).astype(MXU_DTYPE)
            v_out[h] = kvraw[:, h * 256 + NOPE:(h + 1) * 256].astype(MXU_DTYPE)

    full = lambda a: pl.BlockSpec(a.shape, lambda i: (0,) * a.ndim)
    return pl.pallas_call(
        body, grid=(T // tm,), name="mla_pre",
        in_specs=[pl.BlockSpec((tm, 256), lambda i: (i, P_QLAT // 256)),
                  pl.BlockSpec((tm, 256), lambda i: (i, P_KVLAT // 256)),
                  pl.BlockSpec((tm, 128), lambda i: (i, P_KPE // 128)),
                  pl.BlockSpec((tm, ROPE), lambda i: (i, 0)), pl.BlockSpec((tm, ROPE), lambda i: (i, 0)),
                  full(w_qln), full(w_kvln), full(w_uq_p), full(w_ukv), full(qnw), full(knw)],
        out_specs=[pl.BlockSpec((H, tm, QK_DIM), lambda i: (0, i, 0)),
                   pl.BlockSpec((H, tm, QK_DIM), lambda i: (0, i, 0)),
                   pl.BlockSpec((H, tm, V_DIM), lambda i: (0, i, 0))],
        out_shape=[SDS((H, T, QK_DIM), MXU_DTYPE), SDS((H, T, QK_DIM), MXU_DTYPE), SDS((H, T, V_DIM), MXU_DTYPE)],
        compiler_params=_params(("arbitrary",)),
    )(proj, proj, proj, cosf, sinf, w_qln, w_kvln, w_uq_p, w_ukv, qnw, knw)


def _attn_fwd(q4, k4, v4, B, S, transfer=None):
    H = MLA_HEADS
    bq = min(ATTN_BLOCK, S)
    nq = S // bq
    rows = bq // ATTN_CHAINS

    def body(q_ref, k_ref, v_ref, o_ref, lse_ref):
        col = lax.broadcasted_iota(jnp.int32, (rows, bq), 1)
        row = lax.broadcasted_iota(jnp.int32, (rows, bq), 0)

        def q_step(qi, carry):
            qs = pl.multiple_of(qi * bq, bq)
            qsub = [q_ref[0, pl.ds(qs + j * rows, rows), :] for j in range(ATTN_CHAINS)]

            def k_block(ks, cs, diagonal):
                k = k_ref[0, pl.ds(ks, bq), :]
                v = v_ref[0, pl.ds(ks, bq), :]
                out = [None] * ATTN_CHAINS

                def chain(j):
                    m, l, acc = cs[j]
                    s = _mm_nt(qsub[j], k)
                    yield
                    if diagonal:
                        s = jnp.where(col <= row + j * rows, s, -jnp.inf)
                    m_new = jnp.maximum(m, jnp.max(s, axis=-1, keepdims=True))
                    p = jnp.exp(s - m_new)
                    a = jnp.exp(m - m_new)
                    l_new = a * l + jnp.sum(p, axis=-1, keepdims=True)
                    yield
                    out[j] = (m_new, l_new, a * acc + _mm(p, v))

                _lockstep([chain(j) for j in range(ATTN_CHAINS)])
                return tuple(out)

            init = tuple((jnp.full((rows, 1), -jnp.inf, F32), jnp.zeros((rows, 1), F32),
                          jnp.zeros((rows, V_DIM), F32)) for _ in range(ATTN_CHAINS))
            cs = lax.fori_loop(0, qi, lambda kj, c: k_block(pl.multiple_of(kj * bq, bq), c, False), init)
            for j, (m, l, acc) in enumerate(k_block(qs, cs, True)):
                o_ref[0, pl.ds(qs + j * rows, rows), :] = acc / l
                lse_ref[0, pl.ds(qs + j * rows, rows), :] = m + jnp.log(l)
            return carry

        lax.fori_loop(0, nq, q_step, 0)

    spec = lambda d: pl.BlockSpec((1, S, d), lambda h, b: (h, b, 0))
    return _call_beside(
        body, transfer, grid=(H, B), name="attn_fwd",
        in_specs=[spec(QK_DIM), spec(QK_DIM), spec(V_DIM)],
        out_specs=[spec(V_DIM), spec(1)],
        out_shape=[SDS((H, B * S, V_DIM), F32), SDS((H, B * S, 1), F32)],
        scratch_shapes=[], semantics=("arbitrary", "arbitrary"), args=(q4, k4, v4))


def _conv_taps(u, halo, w):
    sh = [_shift_down(u, halo, j) for j in range(CONV_W)]
    c = w[0:1] * sh[3] + w[1:2] * sh[2] + w[2:3] * sh[1] + w[3:4] * sh[0]
    return c, sh


def _gate_values(gab, alog_l, dt_l, lane):
    g = -jnp.exp(alog_l) * jax.nn.softplus(gab + dt_l)
    g = jnp.where(lane < GDN_HEADS, g, 0.0)
    beta = jnp.where((lane >= GDN_HEADS) & (lane < 2 * GDN_HEADS), _sigmoid(gab), 0.0)
    return g, beta


def _gdn_pre(proj, conv_w, alog_l, dt_l, S):
    T = proj.shape[0]
    tm = min(256, T)
    tiles_per_seq = S // tm
    C3 = 3 * GDN_WIDTH
    H = GDN_HEADS

    def body(u_ref, halo_ref, gab_ref, w_ref, alog_ref, dt_ref, q_out, k_out, v_out, gates_out):
        i = pl.program_id(0)
        halo = jnp.where(i % tiles_per_seq == 0, 0.0, halo_ref[...])
        c, _ = _conv_taps(u_ref[...], halo, w_ref[...])
        a = c * _sigmoid(c)
        for h in range(H):
            xq = a[:, h * GDN_DIM:(h + 1) * GDN_DIM]
            xk = a[:, GDN_WIDTH + h * GDN_DIM:GDN_WIDTH + (h + 1) * GDN_DIM]
            q_out[h] = xq * lax.rsqrt(jnp.sum(xq * xq, axis=-1, keepdims=True) + EPS) * GDN_QSCALE
            k_out[h] = xk * lax.rsqrt(jnp.sum(xk * xk, axis=-1, keepdims=True) + EPS)
            v_out[h] = a[:, 2 * GDN_WIDTH + h * GDN_DIM:2 * GDN_WIDTH + (h + 1) * GDN_DIM]
        lane = lax.broadcasted_iota(jnp.int32, (tm, LANES), 1)
        ric = lax.broadcasted_iota(jnp.int32, (tm, LANES), 0) % CHUNK
        g, beta = _gate_values(gab_ref[...], alog_ref[...], dt_ref[...], lane)
        gates_out[...] = _chunk_cumsum(g, ric) + beta

    hspec = pl.BlockSpec((H, tm, GDN_DIM), lambda i: (0, i, 0))
    return pl.pallas_call(
        body, grid=(T // tm,), name="gdn_pre",
        in_specs=[pl.BlockSpec((tm, C3), lambda i: (i, 0)),
                  pl.BlockSpec((SUBLANES, C3), lambda i: (jnp.maximum(i * (tm // SUBLANES) - 1, 0), 0)),
                  pl.BlockSpec((tm, LANES), lambda i: (i, P_GAB // LANES)),
                  pl.BlockSpec((CONV_W, C3), lambda i: (0, 0)),
                  pl.BlockSpec((1, LANES), lambda i: (0, 0)), pl.BlockSpec((1, LANES), lambda i: (0, 0))],
        out_specs=[hspec, hspec, hspec, pl.BlockSpec((tm, LANES), lambda i: (i, 0))],
        out_shape=[SDS((H, T, GDN_DIM), F32)] * 3 + [SDS((T, LANES), F32)],
        compiler_params=_params(("arbitrary",)),
    )(proj, proj, proj, conv_w, alog_l, dt_l)


def _unit_lower_inverses(Ls, eye):
    Ps = [eye - L for L in Ls]
    Ms = [_split(-L) for L in Ls]
    for _ in range(5):
        sq = [_mm_split(m, m) for m in Ms]
        Ms = [_split(s) for s in sq]
        Ps = [p + _mm_split(_split(p), m) for p, m in zip(Ps, Ms)]
    return Ps


def _chunk_decays(gt, lane, h, ri, ci, rcol):
    Gc = _pick_lane(gt, lane, h)
    bt = _pick_lane(gt, lane, h + GDN_HEADS)
    Gb = jnp.broadcast_to(Gc, (CHUNK, CHUNK))
    Gam = jnp.where(ri >= ci, jnp.exp(Gb - Gb.T), 0.0)
    Gl = jnp.sum(jnp.where(rcol == CHUNK - 1, Gc, 0.0), axis=0, keepdims=True)
    return Gc, bt, Gam, jnp.exp(Gc), jnp.exp(Gl - Gc), jnp.exp(Gl)


GDN_UNROLL = 16


def _gdn_fwd(qg, kg, vg, gates, B, S, transfer=None):
    H, D, C = GDN_HEADS, GDN_DIM, CHUNK
    NC = S // C
    U = GDN_UNROLL if NC % GDN_UNROLL == 0 else 1

    def body(q_ref, k_ref, v_ref, g_ref, o_ref, st_ref, ai_ref, u_ref, w_ref, q2_s, au_s, bc_s, w2_s, el_s):
        h = pl.program_id(0)
        lane = lax.broadcasted_iota(jnp.int32, (C, LANES), 1)
        ri = lax.broadcasted_iota(jnp.int32, (C, C), 0)
        ci = lax.broadcasted_iota(jnp.int32, (C, C), 1)
        rcol = lax.broadcasted_iota(jnp.int32, (C, 1), 0)
        eye = (ri == ci).astype(F32)

        def group(gi, c):
            ns = [gi * U + j for j in range(U)]
            css = [pl.multiple_of(n * C, C) for n in ns]
            qs = [q_ref[0, pl.ds(cs, C), :] for cs in css]
            ks = [k_ref[0, pl.ds(cs, C), :] for cs in css]
            vs = [v_ref[0, pl.ds(cs, C), :] for cs in css]
            decs = [_chunk_decays(g_ref[pl.ds(cs, C), :], lane, h, ri, ci, rcol) for cs in css]
            qks = [_mm_nt(jnp.concatenate([q, k], axis=0), k) for q, k in zip(qs, ks)]
            ainvs = _unit_lower_inverses(
                [jnp.where(ri > ci, d[1] * qk[C:] * d[2], 0.0) for qk, d in zip(qks, decs)], eye)
            sols = [_mm_exact(a, jnp.concatenate([v * d[1], k * (d[1] * d[3])], axis=-1))
                    for a, k, v, d in zip(ainvs, ks, vs, decs)]
            atuw = [_mm(qk[:C] * d[2], sol) for qk, d, sol in zip(qks, decs, sols)]
            kduw = [_mm_tn(k * d[4], sol) for k, d, sol in zip(ks, decs, sols)]
            for n, cs, q, a, sol, au, ku, (Gc, bt, Gam, e, f, eL) in zip(ns, css, qs, ainvs, sols, atuw, kduw, decs):
                u_ref[0, pl.ds(cs, C), :] = sol[:, :D]
                w_ref[0, pl.ds(cs, C), :] = sol[:, D:]
                au_s[pl.ds(cs, C), :] = au[:, :D]
                q2_s[pl.ds(cs, C), :] = q * e - au[:, D:]
                bc_s[n] = ku[:, :D]
                w2_s[n] = ku[:, D:]
                el_s[n] = jnp.broadcast_to(eL, (SUBLANES, LANES))
                ai_ref[0, n] = a.T
            return c

        lax.fori_loop(0, NC // U, group, 0)

        def step(n, S_):
            cs = pl.multiple_of(n * C, C)
            o_ref[0, pl.ds(cs, C), :] = _mm(q2_s[pl.ds(cs, C), :], S_) + au_s[pl.ds(cs, C), :]
            st_ref[0, n] = S_
            return S_ * el_s[n, 0:1, :] + bc_s[n] - _mm(w2_s[n], S_)

        lax.fori_loop(0, NC, step, jnp.zeros((D, D), F32))

    spec = pl.BlockSpec((1, S, D), lambda h, b: (h, b, 0))
    return _call_beside(
        body, transfer, grid=(H, B), name="gdn_fwd",
        in_specs=[spec, spec, spec, pl.BlockSpec((S, LANES), lambda h, b: (b, 0))],
        out_specs=[spec, pl.BlockSpec((1, NC, D, D), lambda h, b: (h, b, 0, 0)),
                   pl.BlockSpec((1, NC, C, C), lambda h, b: (h, b, 0, 0)), spec, spec],
        out_shape=[SDS((H, B * S, D), F32), SDS((H, B * NC, D, D), F32), SDS((H, B * NC, C, C), F32),
                   SDS((H, B * S, D), F32), SDS((H, B * S, D), F32)],
        scratch_shapes=[pltpu.VMEM((S, D), F32), pltpu.VMEM((S, D), F32), pltpu.VMEM((NC, D, D), F32),
                        pltpu.VMEM((NC, D, D), F32), pltpu.VMEM((NC, SUBLANES, LANES), F32)],
        semantics=("arbitrary", "arbitrary"), args=(qg, kg, vg, gates))


def _mix_out(o_mla, o_gdn, proj, x2, mla_w, gdn_w, w_out):
    T, D = x2.shape
    tm = min(512, T)
    H = MLA_HEADS

    def body(om_ref, og_ref, z_ref, x_ref, mw_ref, gw_ref, w_ref, h_ref, mix_ref):
        z = z_ref[...]
        parts = [_rms(om_ref[h], mw_ref[h:h + 1, :])[0] for h in range(H)]
        for h in range(GDN_HEADS):
            zh = z[:, h * GDN_DIM:(h + 1) * GDN_DIM]
            parts.append(_rms(og_ref[h], gw_ref[...])[0] * (zh * _sigmoid(zh)))
        mix = jnp.concatenate(parts, axis=-1).astype(MXU_DTYPE)
        mix_ref[...] = mix
        h_ref[...] = x_ref[...] + jnp.dot(mix, w_ref[...], preferred_element_type=F32)

    hspec = pl.BlockSpec((H, tm, V_DIM), lambda i: (0, i, 0))
    return pl.pallas_call(
        body, grid=(T // tm,), name="mix_out",
        in_specs=[hspec, hspec, pl.BlockSpec((tm, GDN_WIDTH), lambda i: (i, P_GZ // GDN_WIDTH)),
                  pl.BlockSpec((tm, D), lambda i: (i, 0)),
                  pl.BlockSpec((H, V_DIM), lambda i: (0, 0)), pl.BlockSpec((1, GDN_DIM), lambda i: (0, 0)),
                  pl.BlockSpec((D, D), lambda i: (0, 0))],
        out_specs=[pl.BlockSpec((tm, D), lambda i: (i, 0)), pl.BlockSpec((tm, D), lambda i: (i, 0))],
        out_shape=[SDS((T, D), F32), SDS((T, D), MXU_DTYPE)],
        compiler_params=_params(("arbitrary",)),
    )(o_mla, o_gdn, proj, x2, mla_w, gdn_w, w_out)


def _mlp_fwd(h2, w_mn, w_up, w_down, target):
    T, D = h2.shape
    ns, _, ts = w_up.shape
    F = ns * ts
    tm = min(512, T)
    G = MLP_FWD_SHARDS
    tf, nf = G * ts, ns // G

    def body(h_ref, wn_ref, up_w, down_w, t_ref, up_ref, hn_ref, dy_ref, loss_ref, y_acc):
        j = pl.program_id(1)

        @pl.when(j == 0)
        def _():
            hn_ref[...] = _rms(h_ref[...], wn_ref[...])[0].astype(MXU_DTYPE)
            y_acc[...] = h_ref[...]

        parts = []
        for c in range(G):
            up = jnp.dot(hn_ref[...], up_w[c], preferred_element_type=F32)
            up_ref[:, c * ts:(c + 1) * ts] = up
            r = jnp.maximum(up, 0.0)
            parts.append(_mm(r * r, down_w[c * ts:(c + 1) * ts, :]))
        y_acc[...] += functools.reduce(jnp.add, parts)

        @pl.when(j == nf - 1)
        def _():
            err = y_acc[...] - t_ref[...]
            dy_ref[...] = err / D
            loss_ref[...] = jnp.full((1, SUBLANES, LANES), jnp.sum(err * err), F32)

    return pl.pallas_call(
        body, grid=(T // tm, nf), name="mlp_fwd",
        in_specs=[pl.BlockSpec((tm, D), lambda i, j: (i, 0)), pl.BlockSpec((1, D), lambda i, j: (0, 0)),
                  pl.BlockSpec((G, D, ts), lambda i, j: (j, 0, 0)), pl.BlockSpec((tf, D), lambda i, j: (j, 0)),
                  pl.BlockSpec((tm, D), lambda i, j: (i, 0))],
        out_specs=[pl.BlockSpec((tm, tf), lambda i, j: (i, j)), pl.BlockSpec((tm, D), lambda i, j: (i, 0)),
                   pl.BlockSpec((tm, D), lambda i, j: (i, 0)),
                   pl.BlockSpec((1, SUBLANES, LANES), lambda i, j: (i, 0, 0))],
        out_shape=[SDS((T, F), F32), SDS((T, D), MXU_DTYPE), SDS((T, D), F32),
                   SDS((T // tm, SUBLANES, LANES), F32)],
        scratch_shapes=[pltpu.VMEM((tm, D), F32)],
        compiler_params=_params(("arbitrary", "arbitrary")),
    )(h2, w_mn, w_up, w_down, target)


def _mlp_bwd(dy, up, h2, w_mn, w_up, w_down):
    T, D = h2.shape
    ns, _, ts = w_up.shape
    F = ns * ts
    tm = min(512, T)
    G = MLP_BWD_SHARDS
    tf, nf = G * ts, ns // G

    def body(dy_ref, up_ref, h_ref, wn_ref, up_w, down_w, dh_ref, dhb_ref, dup_ref, act_ref, dyb_ref, dwn_ref, acc):
        i, j = pl.program_id(0), pl.program_id(1)

        @pl.when((i == 0) & (j == 0))
        def _():
            dwn_ref[...] = jnp.zeros_like(dwn_ref)

        @pl.when(j == 0)
        def _():
            acc[...] = jnp.zeros_like(acc)
            dyb_ref[...] = dy_ref[...].astype(MXU_DTYPE)

        parts = []
        for c in range(G):
            cols = slice(c * ts, (c + 1) * ts)
            r = jnp.maximum(up_ref[:, cols], 0.0)
            act_ref[:, cols] = (r * r).astype(MXU_DTYPE)
            dup = (_mm_nt(dyb_ref[...], down_w[cols, :]) * (2.0 * r)).astype(MXU_DTYPE)
            dup_ref[:, cols] = dup
            parts.append(_mm_nt(dup, up_w[c]))
        acc[...] += functools.reduce(jnp.add, parts)

        @pl.when(j == nf - 1)
        def _():
            hv = h_ref[...]
            _, rr = _rms(hv, wn_ref[...])
            dx, dw = _rms_bwd(acc[...], hv, wn_ref[...], rr)
            dh = dy_ref[...] + dx
            dh_ref[...] = dh
            dhb_ref[...] = dh.astype(MXU_DTYPE)
            dwn_ref[...] += dw

    row = lambda i, j: (i, 0)
    return pl.pallas_call(
        body, grid=(T // tm, nf), name="mlp_bwd",
        in_specs=[pl.BlockSpec((tm, D), row), pl.BlockSpec((tm, tf), lambda i, j: (i, j)), pl.BlockSpec((tm, D), row),
                  pl.BlockSpec((1, D), lambda i, j: (0, 0)),
                  pl.BlockSpec((G, D, ts), lambda i, j: (j, 0, 0)), pl.BlockSpec((tf, D), lambda i, j: (j, 0))],
        out_specs=[pl.BlockSpec((tm, D), row), pl.BlockSpec((tm, D), row),
                   pl.BlockSpec((tm, tf), lambda i, j: (i, j)), pl.BlockSpec((tm, tf), lambda i, j: (i, j)),
                   pl.BlockSpec((tm, D), row), pl.BlockSpec((1, D), lambda i, j: (0, 0))],
        out_shape=[SDS((T, D), F32), SDS((T, D), MXU_DTYPE), SDS((T, F), MXU_DTYPE), SDS((T, F), MXU_DTYPE),
                   SDS((T, D), MXU_DTYPE), SDS((1, D), F32)],
        scratch_shapes=[pltpu.VMEM((tm, D), F32)],
        compiler_params=_params(("arbitrary", "arbitrary")),
    )(dy, up, h2, w_mn, w_up, w_down)


def _mix_bwd(dhb, o_mla, o_gdn, proj, mla_w, gdn_w, w_out):
    T, D = dhb.shape
    tm = min(512, T)
    H = MLA_HEADS

    def body(dh_ref, om_ref, og_ref, z_ref, mw_ref, gw_ref, w_ref, dom_ref, dog_ref, dz_ref, dmw_ref, dgw_ref):
        @pl.when(pl.program_id(0) == 0)
        def _():
            dmw_ref[...] = jnp.zeros_like(dmw_ref)
            dgw_ref[...] = jnp.zeros_like(dgw_ref)

        dmix = _mm_nt(dh_ref[...], w_ref[...])
        z = z_ref[...]
        dmw, dzs = [], []
        dgw = jnp.zeros((1, GDN_DIM), F32)
        for h in range(H):
            o = om_ref[h]
            w = mw_ref[h:h + 1, :]
            _, r = _rms(o, w)
            dx, dw = _rms_bwd(dmix[:, h * V_DIM:(h + 1) * V_DIM], o, w, r)
            dom_ref[h] = dx
            dmw.append(dw)
        for h in range(GDN_HEADS):
            o = og_ref[h]
            w = gw_ref[...]
            zh = z[:, h * GDN_DIM:(h + 1) * GDN_DIM]
            sg = _sigmoid(zh)
            yn, r = _rms(o, w)
            dy = dmix[:, H * V_DIM + h * GDN_DIM:H * V_DIM + (h + 1) * GDN_DIM]
            dzs.append(dy * yn * (sg * (1.0 + zh * (1.0 - sg))))
            dx, dw = _rms_bwd(dy * (zh * sg), o, w, r)
            dog_ref[h] = dx
            dgw = dgw + dw
        dz_ref[...] = jnp.concatenate(dzs, axis=-1)
        dmw_ref[...] += jnp.concatenate(dmw, axis=0)
        dgw_ref[...] += dgw

    hspec = pl.BlockSpec((H, tm, V_DIM), lambda i: (0, i, 0))
    return pl.pallas_call(
        body, grid=(T // tm,), name="mix_bwd",
        in_specs=[pl.BlockSpec((tm, D), lambda i: (i, 0)), hspec, hspec,
                  pl.BlockSpec((tm, GDN_WIDTH), lambda i: (i, P_GZ // GDN_WIDTH)),
                  pl.BlockSpec((H, V_DIM), lambda i: (0, 0)), pl.BlockSpec((1, GDN_DIM), lambda i: (0, 0)),
                  pl.BlockSpec((D, D), lambda i: (0, 0))],
        out_specs=[hspec, hspec, pl.BlockSpec((tm, GDN_WIDTH), lambda i: (i, 0)),
                   pl.BlockSpec((H, V_DIM), lambda i: (0, 0)), pl.BlockSpec((1, GDN_DIM), lambda i: (0, 0))],
        out_shape=[SDS((H, T, V_DIM), F32), SDS((H, T, GDN_DIM), F32), SDS((T, GDN_WIDTH), F32),
                   SDS((H, V_DIM), F32), SDS((1, GDN_DIM), F32)],
        compiler_params=_params(("arbitrary",)),
    )(dhb, o_mla, o_gdn, proj, mla_w, gdn_w, w_out)


def _attn_bwd(q4, k4, v4, do4, o4, lse4, B, S, transfer=None):
    H = MLA_HEADS
    bq = min(ATTN_BLOCK, S)
    nq = S // bq
    rows = bq // ATTN_CHAINS

    def body(q_ref, k_ref, v_ref, do_ref, o_ref, lse_ref, dq_ref, dk_ref, dv_ref, delta):
        dq_ref[...] = jnp.zeros_like(dq_ref)
        dk_ref[...] = jnp.zeros_like(dk_ref)
        dv_ref[...] = jnp.zeros_like(dv_ref)
        delta[...] = jnp.sum(do_ref[0] * o_ref[0], axis=-1, keepdims=True)

        col = lax.broadcasted_iota(jnp.int32, (rows, bq), 1)
        row = lax.broadcasted_iota(jnp.int32, (rows, bq), 0)

        def k_step(kj, carry):
            ks = pl.multiple_of(kj * bq, bq)
            k = k_ref[0, pl.ds(ks, bq), :]
            v = v_ref[0, pl.ds(ks, bq), :]

            def q_block(qs, diagonal):
                dks, dvs = [None] * ATTN_CHAINS, [None] * ATTN_CHAINS

                def chain(j):
                    sl = pl.ds(qs + j * rows, rows)
                    q = q_ref[0, sl, :]
                    do = do_ref[0, sl, :].astype(MXU_DTYPE)
                    s = _mm_nt(q, k)
                    dp = _mm_nt(do, v)
                    yield
                    p = jnp.exp(s - lse_ref[0, sl, :])
                    if diagonal:
                        p = jnp.where(col <= row + j * rows, p, 0.0)
                    ds = p * (dp - delta[sl, :])
                    yield
                    dvs[j] = _mm_tn(p, do)
                    dks[j] = _mm_tn(ds, q)
                    dq_ref[0, sl, :] += _mm(ds, k)

                _lockstep([chain(j) for j in range(ATTN_CHAINS)])
                dv_ref[0, pl.ds(ks, bq), :] += functools.reduce(jnp.add, dvs)
                dk_ref[0, pl.ds(ks, bq), :] += functools.reduce(jnp.add, dks)

            q_block(ks, True)

            def q_step(qi, c):
                q_block(pl.multiple_of(qi * bq, bq), False)
                return c

            lax.fori_loop(kj + 1, nq, q_step, 0)
            return carry

        lax.fori_loop(0, nq, k_step, 0)

    spec = lambda d: pl.BlockSpec((1, S, d), lambda h, b: (h, b, 0))
    return _call_beside(
        body, transfer, grid=(H, B), name="attn_bwd",
        in_specs=[spec(QK_DIM), spec(QK_DIM), spec(V_DIM), spec(V_DIM), spec(V_DIM), spec(1)],
        out_specs=[spec(QK_DIM), spec(QK_DIM), spec(V_DIM)],
        out_shape=[SDS((H, B * S, QK_DIM), F32), SDS((H, B * S, QK_DIM), F32), SDS((H, B * S, V_DIM), F32)],
        scratch_shapes=[pltpu.VMEM((S, 1), F32)], semantics=("arbitrary", "arbitrary"),
        args=(q4, k4, v4, do4, o4, lse4))


def _gdn_bwd(qg, kg, vg, gates, states, ainv, u4, w4, do4, B, S, transfer=None):
    H, D, C = GDN_HEADS, GDN_DIM, CHUNK
    NC = S // C
    U = GDN_UNROLL if NC % GDN_UNROLL == 0 else 1

    def body(q_ref, k_ref, v_ref, g_ref, st_ref, ai_ref, u_ref, w_ref, do_ref, dq_ref, dk_ref, dv_ref, dgb_ref,
             kd_s, x1_s, x2_s, el_s, dvn_s, ds_s, w2t_s):
        h = pl.program_id(0)
        lane = lax.broadcasted_iota(jnp.int32, (C, LANES), 1)
        ri = lax.broadcasted_iota(jnp.int32, (C, C), 0)
        ci = lax.broadcasted_iota(jnp.int32, (C, C), 1)
        rcol = lax.broadcasted_iota(jnp.int32, (C, 1), 0)

        def rsum(a):
            return jnp.sum(a, axis=-1, keepdims=True)

        def blocks(fn):
            def group(gi, c):
                _lockstep([fn(gi * U + j) for j in range(U)])
                return c
            lax.fori_loop(0, NC // U, group, 0)

        def prepare(n):
            cs = pl.multiple_of(n * C, C)
            q = q_ref[0, pl.ds(cs, C), :]
            k = k_ref[0, pl.ds(cs, C), :]
            do = do_ref[0, pl.ds(cs, C), :]
            Gc, bt, Gam, e, f, eL = _chunk_decays(g_ref[pl.ds(cs, C), :], lane, h, ri, ci, rcol)
            At = _mm_nt(q, k) * Gam
            yield
            x1 = _mm_tn(At, do)
            x2 = _mm_tn(q * e, do)
            kd = k * f
            w = w_ref[0, pl.ds(cs, C), :]
            yield
            x1_s[pl.ds(cs, C), :] = x1
            x2_s[n] = x2 - _mm_tn(w, x1)
            w2t_s[n] = _mm_tn(w, kd)
            kd_s[pl.ds(cs, C), :] = kd
            el_s[n] = jnp.broadcast_to(eL, (SUBLANES, LANES))

        blocks(prepare)

        def recur(t, dS):
            n = NC - 1 - t
            cs = pl.multiple_of(n * C, C)
            ds_s[n] = dS
            dvn_s[pl.ds(cs, C), :] = x1_s[pl.ds(cs, C), :] + _mm(kd_s[pl.ds(cs, C), :], dS)
            return x2_s[n] + el_s[n, 0:1, :] * dS - _mm(w2t_s[n], dS)

        lax.fori_loop(0, NC, recur, jnp.zeros((D, D), F32))

        def local(n):
            cs = pl.multiple_of(n * C, C)
            q = q_ref[0, pl.ds(cs, C), :]
            k = k_ref[0, pl.ds(cs, C), :]
            v = v_ref[0, pl.ds(cs, C), :]
            do = do_ref[0, pl.ds(cs, C), :]
            u = u_ref[0, pl.ds(cs, C), :]
            w = w_ref[0, pl.ds(cs, C), :]
            dvn = dvn_s[pl.ds(cs, C), :]
            dS = ds_s[n]
            Gc, bt, Gam, e, f, eL = _chunk_decays(g_ref[pl.ds(cs, C), :], lane, h, ri, ci, rcol)
            S0 = st_ref[0, n]
            AinvT = ai_ref[0, n]
            qk = _mm_nt(jnp.concatenate([q, k], axis=0), k)
            QK, KK = qk[:C], qk[C:]
            be = bt * e
            sol = jnp.concatenate([u, w], axis=-1)
            vn = u - _mm(w, S0)
            yield
            dAt = jnp.where(ri >= ci, _mm_nt(do, vn), 0.0)
            dqd = _mm_nt(do, S0)
            dw = -_mm_nt(dvn, S0)
            dkd = _mm_nt(vn, dS)
            deL = jnp.sum(rsum(dS * S0), axis=0, keepdims=True)
            yield
            dR = _mm_exact(AinvT, jnp.concatenate([dvn, dw], axis=-1))
            dR1, dR2 = dR[:, :D], dR[:, D:]
            yield
            dL = jnp.where(ri > ci, -_mm_nt(dR, sol), 0.0)
            yield
            dv_ref[0, pl.ds(cs, C), :] = dR1 * bt
            r2 = rsum(dR2 * k)
            X = dL * Gam
            dbt = rsum(dR1 * v) + r2 * e + rsum(X * KK)
            de = r2 * bt + rsum(dqd * q)
            dKK = X * bt
            dQK = dAt * Gam
            dq_ref[0, pl.ds(cs, C), :] = _mm(dQK, k) + dqd * e
            dk_ref[0, pl.ds(cs, C), :] = dR2 * be + _mm(dKK + dKK.T, k) + _mm_tn(dQK, q) + dkd * f
            df = rsum(dkd * k)
            Z = (dL * (bt * KK) + dAt * QK) * Gam
            dG = rsum(Z) - rsum(Z.T) + de * e - df * f
            dGl = jnp.sum(df * f, axis=0, keepdims=True) + deL * eL
            dG = dG + jnp.where(rcol == C - 1, dGl, 0.0)
            dgb_ref[0, pl.ds(cs, C), :] = jnp.where(lane == 0, dG, jnp.where(lane == 1, dbt, 0.0))

        blocks(local)

    spec = pl.BlockSpec((1, S, D), lambda h, b: (h, b, 0))
    return _call_beside(
        body, transfer, grid=(H, B), name="gdn_bwd",
        in_specs=[spec, spec, spec, pl.BlockSpec((S, LANES), lambda h, b: (b, 0)),
                  pl.BlockSpec((1, NC, D, D), lambda h, b: (h, b, 0, 0)),
                  pl.BlockSpec((1, NC, C, C), lambda h, b: (h, b, 0, 0)), spec, spec, spec],
        out_specs=[spec, spec, spec, spec],
        out_shape=[SDS((H, B * S, D), F32)] * 4,
        scratch_shapes=[pltpu.VMEM((S, D), F32), pltpu.VMEM((S, D), F32), pltpu.VMEM((NC, D, D), F32),
                        pltpu.VMEM((NC, SUBLANES, LANES), F32), pltpu.VMEM((S, D), F32),
                        pltpu.VMEM((NC, D, D), F32), pltpu.VMEM((NC, D, D), F32)],
        semantics=("arbitrary", "arbitrary"), args=(qg, kg, vg, gates, states, ainv, u4, w4, do4))


def _gdn_pre_bwd(proj, conv_w, alog_l, dt_l, dq4, dk4, dv4, dgb4, S):
    T = proj.shape[0]
    tm = min(256, T)
    tiles_per_seq = S // tm
    C3 = 3 * GDN_WIDTH
    H = GDN_HEADS

    def body(u_ref, halo_ref, gab_ref, w_ref, alog_ref, dt_ref, dq_ref, dk_ref, dv_ref, dgb_ref,
             dc_ref, dgab_ref, dcw_ref, dalog_ref, ddt_ref):
        i = pl.program_id(0)

        @pl.when(i == 0)
        def _():
            dcw_ref[...] = jnp.zeros_like(dcw_ref)
            dalog_ref[...] = jnp.zeros_like(dalog_ref)
            ddt_ref[...] = jnp.zeros_like(ddt_ref)

        halo = jnp.where(i % tiles_per_seq == 0, 0.0, halo_ref[...])
        c, sh = _conv_taps(u_ref[...], halo, w_ref[...])
        sg = _sigmoid(c)
        a = c * sg
        das = [None] * (3 * H)
        for h in range(H):
            xq = a[:, h * GDN_DIM:(h + 1) * GDN_DIM]
            xk = a[:, GDN_WIDTH + h * GDN_DIM:GDN_WIDTH + (h + 1) * GDN_DIM]
            das[h] = _l2n_bwd(dq_ref[h], xq, GDN_QSCALE)
            das[H + h] = _l2n_bwd(dk_ref[h], xk, 1.0)
            das[2 * H + h] = dv_ref[h]
        dc = jnp.concatenate(das, axis=-1) * (sg * (1.0 + c * (1.0 - sg)))
        dc_ref[...] = dc
        dcw_ref[...] += jnp.concatenate(
            [jnp.sum(dc * sh[CONV_W - 1 - t], axis=0, keepdims=True) for t in range(CONV_W)], axis=0)
        lane = lax.broadcasted_iota(jnp.int32, (tm, LANES), 1)
        ric = lax.broadcasted_iota(jnp.int32, (tm, LANES), 0) % CHUNK
        dG = jnp.zeros((tm, LANES), F32)
        for h in range(H):
            t = dgb_ref[h]
            dG = dG + jnp.where(lane == h, _pick_lane(t, lane, 0), 0.0) \
                    + jnp.where(lane == h + H, _pick_lane(t, lane, 1), 0.0)
        is_g = lane < H
        dg = jnp.where(is_g, _chunk_rev_cumsum(jnp.where(is_g, dG, 0.0), ric), 0.0)
        gab = gab_ref[...]
        g, beta = _gate_values(gab, alog_ref[...], dt_ref[...], lane)
        dga = jnp.where(is_g, dg * (-jnp.exp(alog_ref[...])) * _sigmoid(gab + dt_ref[...]), 0.0)
        dgb = jnp.where(is_g, 0.0, dG) * beta * (1.0 - beta)
        dgab_ref[...] = dga + dgb
        dalog_ref[...] += jnp.sum(dg * g, axis=0, keepdims=True)
        ddt_ref[...] += jnp.sum(dga, axis=0, keepdims=True)

    hspec = pl.BlockSpec((H, tm, GDN_DIM), lambda i: (0, i, 0))
    vec = pl.BlockSpec((1, LANES), lambda i: (0, 0))
    return pl.pallas_call(
        body, grid=(T // tm,), name="gdn_pre_bwd",
        in_specs=[pl.BlockSpec((tm, C3), lambda i: (i, 0)),
                  pl.BlockSpec((SUBLANES, C3), lambda i: (jnp.maximum(i * (tm // SUBLANES) - 1, 0), 0)),
                  pl.BlockSpec((tm, LANES), lambda i: (i, P_GAB // LANES)),
                  pl.BlockSpec((CONV_W, C3), lambda i: (0, 0)), vec, vec, hspec, hspec, hspec, hspec],
        out_specs=[pl.BlockSpec((tm, C3), lambda i: (i, 0)), pl.BlockSpec((tm, LANES), lambda i: (i, 0)),
                   pl.BlockSpec((CONV_W, C3), lambda i: (0, 0)), vec, vec],
        out_shape=[SDS((T, C3), F32), SDS((T, LANES), F32), SDS((CONV_W, C3), F32),
                   SDS((1, LANES), F32), SDS((1, LANES), F32)],
        compiler_params=_params(("arbitrary",)),
    )(proj, proj, proj, conv_w, alog_l, dt_l, dq4, dk4, dv4, dgb4)


def _conv_bwd_input(dc, conv_w, S):
    T, C3 = dc.shape
    tm = min(256, T)
    tiles_per_seq = S // tm
    nblk = T // SUBLANES

    def body(dc_ref, nxt_ref, w_ref, du_ref):
        i = pl.program_id(0)
        nxt = jnp.where(i % tiles_per_seq == tiles_per_seq - 1, 0.0, nxt_ref[...])
        x = dc_ref[...]
        w = w_ref[...]
        du = w[3:4] * x
        for j in range(1, CONV_W):
            du = du + w[3 - j:4 - j] * _shift_up(x, nxt, j)
        du_ref[...] = du

    return pl.pallas_call(
        body, grid=(T // tm,), name="conv_bwd_input",
        in_specs=[pl.BlockSpec((tm, C3), lambda i: (i, 0)),
                  pl.BlockSpec((SUBLANES, C3), lambda i: (jnp.minimum((i + 1) * (tm // SUBLANES), nblk - 1), 0)),
                  pl.BlockSpec((CONV_W, C3), lambda i: (0, 0))],
        out_specs=pl.BlockSpec((tm, C3), lambda i: (i, 0)),
        out_shape=SDS((T, C3), F32),
        compiler_params=_params(("arbitrary",)),
    )(dc, dc, conv_w)


def _mla_pre_bwd(proj, cosf, sinf, w_qln, w_kvln, w_uq_p, w_ukv, qnw, knw, dq4, dk4, dv4):
    T = proj.shape[0]
    tm = min(256, T)
    H = MLA_HEADS

    def body(ql_ref, kvl_ref, kpe_ref, cos_ref, sin_ref, wq_ref, wkv_ref, uq_ref, ukv_ref, qnw_ref, knw_ref,
             dq_ref, dk_ref, dv_ref,
             dql_ref, dkvl_ref, dkpe_ref, dqraw_ref, dkvraw_ref, qn_ref, kvn_ref, dwq_ref, dwkv_ref, dqnw_ref, dknw_ref):
        @pl.when(pl.program_id(0) == 0)
        def _():
            for r in (dwq_ref, dwkv_ref, dqnw_ref, dknw_ref):
                r[...] = jnp.zeros_like(r)

        cos, sin = cos_ref[...], sin_ref[...]
        qnw_, knw_ = qnw_ref[...], knw_ref[...]
        ql, kvl = ql_ref[...], kvl_ref[...]
        kpe_raw = kpe_ref[...][:, :ROPE]
        qn, rq = _rms(ql, wq_ref[...])
        kvn, rkv = _rms(kvl, wkv_ref[...])
        qn_ref[...] = qn.astype(MXU_DTYPE)
        kvn_ref[...] = kvn.astype(MXU_DTYPE)
        qraw = _mm(qn, uq_ref[...])
        kvraw = _mm(kvn, ukv_ref[...])
        dq_nope, dq_pe, dkv_parts = [], [], []
        dqnw_n = jnp.zeros((1, NOPE), F32)
        dqnw_p = jnp.zeros((1, ROPE), F32)
        dknw_n = jnp.zeros((1, NOPE), F32)
        dkpe = jnp.zeros((tm, ROPE), F32)
        for h in range(H):
            dq = dq_ref[h] * ATT_SCALE
            x = qraw[:, h * NOPE:(h + 1) * NOPE]
            dx, dw = _rms_bwd(dq[:, :NOPE], x, qnw_[:, :NOPE], _rms(x, qnw_[:, :NOPE])[1])
            dq_nope.append(dx)
            dqnw_n = dqnw_n + dw
            x = qraw[:, H * NOPE + h * ROPE:H * NOPE + (h + 1) * ROPE]
            dx, dw = _rms_bwd(_rope_bwd(dq[:, NOPE:], cos, sin), x, qnw_[:, NOPE:], _rms(x, qnw_[:, NOPE:])[1])
            dq_pe.append(dx)
            dqnw_p = dqnw_p + dw
            dk = dk_ref[h]
            x = kvraw[:, h * 256:h * 256 + NOPE]
            dx, dw = _rms_bwd(dk[:, :NOPE], x, knw_[:, :NOPE], _rms(x, knw_[:, :NOPE])[1])
            dknw_n = dknw_n + dw
            dkpe = dkpe + dk[:, NOPE:]
            dkv_parts += [dx, dv_ref[h]]
        dx, dknw_p = _rms_bwd(_rope_bwd(dkpe, cos, sin), kpe_raw, knw_[:, NOPE:], _rms(kpe_raw, knw_[:, NOPE:])[1])
        dkpe_ref[...] = jnp.concatenate([dx, jnp.zeros((tm, LANES - ROPE), F32)], axis=-1)
        dqraw = jnp.concatenate(dq_nope + dq_pe, axis=-1).astype(MXU_DTYPE)
        dkvraw = jnp.concatenate(dkv_parts, axis=-1).astype(MXU_DTYPE)
        dqraw_ref[...] = dqraw
        dkvraw_ref[...] = dkvraw
        dx, dw = _rms_bwd(_mm_nt(dqraw, uq_ref[...]), ql, wq_ref[...], rq)
        dql_ref[...] = dx
        dwq_ref[...] += dw
        dx, dw = _rms_bwd(_mm_nt(dkvraw, ukv_ref[...]), kvl, wkv_ref[...], rkv)
        dkvl_ref[...] = dx
        dwkv_ref[...] += dw
        dqnw_ref[...] += jnp.concatenate([dqnw_n, dqnw_p], axis=-1)
        dknw_ref[...] += jnp.concatenate([dknw_n, dknw_p], axis=-1)

    full = lambda a: pl.BlockSpec(a.shape, lambda i: (0,) * a.ndim)
    rows = lambda n: pl.BlockSpec((tm, n), lambda i: (i, 0))
    const = lambda n: pl.BlockSpec((1, n), lambda i: (0, 0))
    NQ, NKV = w_uq_p.shape[1], w_ukv.shape[1]
    return pl.pallas_call(
        body, grid=(T // tm,), name="mla_pre_bwd",
        in_specs=[pl.BlockSpec((tm, 256), lambda i: (i, P_QLAT // 256)),
                  pl.BlockSpec((tm, 256), lambda i: (i, P_KVLAT // 256)),
                  pl.BlockSpec((tm, 128), lambda i: (i, P_KPE // 128)),
                  rows(ROPE), rows(ROPE),
                  full(w_qln), full(w_kvln), full(w_uq_p), full(w_ukv), full(qnw), full(knw),
                  pl.BlockSpec((H, tm, QK_DIM), lambda i: (0, i, 0)),
                  pl.BlockSpec((H, tm, QK_DIM), lambda i: (0, i, 0)),
                  pl.BlockSpec((H, tm, V_DIM), lambda i: (0, i, 0))],
        out_specs=[rows(Q_LORA), rows(KV_LORA), rows(LANES), rows(NQ), rows(NKV), rows(Q_LORA), rows(KV_LORA),
                   const(Q_LORA), const(KV_LORA), const(QK_DIM), const(QK_DIM)],
        out_shape=[SDS((T, Q_LORA), F32), SDS((T, KV_LORA), F32), SDS((T, LANES), F32),
                   SDS((T, NQ), MXU_DTYPE), SDS((T, NKV), MXU_DTYPE),
                   SDS((T, Q_LORA), MXU_DTYPE), SDS((T, KV_LORA), MXU_DTYPE),
                   SDS((1, Q_LORA), F32), SDS((1, KV_LORA), F32), SDS((1, QK_DIM), F32), SDS((1, QK_DIM), F32)],
        compiler_params=_params(("arbitrary",)),
    )(proj, proj, proj, cosf, sinf, w_qln, w_kvln, w_uq_p, w_ukv, qnw, knw, dq4, dk4, dv4)


def _in_proj_bwd(dgqkv, dgz, dql, dkvl, dkpe, dgab, w_in_p, dh, x2, w_an):
    T, D = x2.shape
    N = w_in_p.shape[1]
    tm = min(512, T)

    def body(a_ref, b_ref, c_ref, d_ref, e_ref, f_ref, w_ref, dh_ref, x_ref, wn_ref, dx_ref, dp_ref, dwn_ref):
        @pl.when(pl.program_id(0) == 0)
        def _():
            dwn_ref[...] = jnp.zeros_like(dwn_ref)

        dp = jnp.concatenate([a_ref[...], b_ref[...], c_ref[...], d_ref[...], e_ref[...], f_ref[...]],
                             axis=-1).astype(MXU_DTYPE)
        dp_ref[...] = dp
        x = x_ref[...]
        _, r = _rms(x, wn_ref[...])
        dx, dw = _rms_bwd(_mm_nt(dp, w_ref[...]), x, wn_ref[...], r)
        dx_ref[...] = dh_ref[...] + dx
        dwn_ref[...] += dw

    rows = lambda n: pl.BlockSpec((tm, n), lambda i: (i, 0))
    return pl.pallas_call(
        body, grid=(T // tm,), name="in_proj_bwd",
        in_specs=[rows(dgqkv.shape[1]), rows(dgz.shape[1]), rows(dql.shape[1]), rows(dkvl.shape[1]),
                  rows(dkpe.shape[1]), rows(dgab.shape[1]),
                  pl.BlockSpec((D, N), lambda i: (0, 0)), rows(D), rows(D), pl.BlockSpec((1, D), lambda i: (0, 0))],
        out_specs=[rows(D), rows(N), pl.BlockSpec((1, D), lambda i: (0, 0))],
        out_shape=[SDS((T, D), F32), SDS((T, N), MXU_DTYPE), SDS((1, D), F32)],
        compiler_params=_params(("arbitrary",)),
    )(dgqkv, dgz, dql, dkvl, dkpe, dgab, w_in_p, dh, x2, w_an)


def _wgrad(a, b, name, column_shards=False):
    T, M = a.shape
    N = b.shape[1]
    tM = _divisor_tile(M, 1024)
    tN = N // N_DEV if column_shards else _divisor_tile(N, 1536)
    tk = min(T, 1024)
    nk = T // tk

    def body(a_ref, b_ref, o_ref, acc):
        k = pl.program_id(2)

        @pl.when(k == 0)
        def _():
            acc[...] = jnp.zeros_like(acc)

        acc[...] += _mm_tn(a_ref[...], b_ref[...])

        @pl.when(k == nk - 1)
        def _():
            o_ref[...] = acc[...].astype(WIRE_DTYPE).reshape(o_ref.shape)

    if column_shards:
        out_spec, out_shape = pl.BlockSpec((1, tM, tN), lambda i, j, k: (j, i, 0)), SDS((N_DEV, M, tN), WIRE_DTYPE)
    else:
        out_spec, out_shape = pl.BlockSpec((tM, tN), lambda i, j, k: (i, j)), SDS((M, N), WIRE_DTYPE)
    return pl.pallas_call(
        body, grid=(M // tM, N // tN, nk), name=name,
        in_specs=[pl.BlockSpec((tk, tM), lambda i, j, k: (k, i)), pl.BlockSpec((tk, tN), lambda i, j, k: (k, j))],
        out_specs=out_spec, out_shape=out_shape,
        scratch_shapes=[pltpu.VMEM((tM, tN), F32)],
        compiler_params=_params(("arbitrary", "arbitrary", "arbitrary")),
    )(a, b)


def _adamw(g, w, m, v):
    m = ADAM_B1 * m + (1.0 - ADAM_B1) * g
    v = ADAM_B2 * v + (1.0 - ADAM_B2) * jnp.square(g)
    m_hat = m / (1.0 - ADAM_B1 ** ADAM_STEP)
    v_hat = v / (1.0 - ADAM_B2 ** ADAM_STEP)
    return -ADAM_LR * (m_hat / (jnp.sqrt(v_hat) + ADAM_EPS) + ADAM_WD * w), m, v


def _reduce_adamw(parts, w, m, v, name):
    R, C = w.shape
    _, Rp, Cp = parts.shape
    tr = min(R, 256)
    tp = tr if Rp == R else Rp

    def body(p_ref, w_ref, m_ref, v_ref, g_ref, d_ref, nm_ref, nv_ref):
        g = p_ref[0].astype(F32)
        for s in range(1, N_DEV):
            g = g + p_ref[s].astype(F32)
        g = g[:tr, :C]
        g_ref[...] = g
        d_ref[...], nm_ref[...], nv_ref[...] = _adamw(g, w_ref[...], m_ref[...], v_ref[...])

    spec = pl.BlockSpec((tr, C), lambda i: (i, 0))
    return pl.pallas_call(
        body, grid=(R // tr,), name=name,
        in_specs=[pl.BlockSpec((N_DEV, tp, Cp), lambda i: (0, i, 0)), spec, spec, spec],
        out_specs=[spec] * 4, out_shape=[SDS((R, C), F32)] * 4,
        compiler_params=_params(("arbitrary",)),
    )(parts, w, m, v)


SMALL_ROWS, SMALL_COLS = 16, 1024
SMALL_LAYOUT = (
    ("attn_norm_w", 0, 1, 1024, 1024), ("mlp_norm_w", 1, 1, 1024, 1024), ("q_lat_norm_w", 2, 1, 256, 256),
    ("kv_lat_norm_w", 3, 1, 256, 256), ("q_norm_w", 4, 1, 192, 192), ("k_norm_w", 5, 1, 192, 192),
    ("mla_out_norm_w", 6, 4, 128, 128), ("a_log", 10, 1, 128, 4), ("dt_bias", 11, 1, 128, 4),
    ("gdn_norm_w", 12, 1, 128, 128))
LOSS_ENTRY = ("loss", 13, 1, 128, 128)


def _adamw_replicated(parts, ws, ms, vs):
    n = len(SMALL_LAYOUT)

    def body(*refs):
        p_ref = refs[0]
        w_refs, m_refs, v_refs = refs[1:1 + n], refs[1 + n:1 + 2 * n], refs[1 + 2 * n:1 + 3 * n]
        outs = refs[1 + 3 * n:]
        s = p_ref[0]
        for d in range(1, N_DEV):
            s = s + p_ref[d]
        for i, (_, r0, nr, _, pw) in enumerate(SMALL_LAYOUT):
            g = s[r0:r0 + nr, :pw]
            outs[i][...] = g
            outs[n + i][...], outs[2 * n + i][...], outs[3 * n + i][...] = _adamw(
                g, w_refs[i][...], m_refs[i][...], v_refs[i][...])
        _, r0, nr, gw, _ = LOSS_ENTRY
        outs[4 * n][...] = s[r0:r0 + nr, :gw]

    res = pl.pallas_call(
        body, name="adamw_replicated",
        out_shape=[SDS(w.shape, F32) for w in ws] * 4 + [SDS((1, LANES), F32)],
        compiler_params=_params(),
    )(parts, *ws, *ms, *vs)
    return [res[k * n:(k + 1) * n] for k in range(4)], res[4 * n][0, 0]


COPIES_PER_ARRAY = N_DEV - 1


def _two_level_gather(srcs, outs, send_sems, recv_sems, local_sems=None, stage="all"):
    mx, my, mc = lax.axis_index("x"), lax.axis_index("y"), lax.axis_index("c")
    me, sibling = (mx, my, mc), (mx, my, 1 - mc)
    chips = [(1 - mx, my), (mx, 1 - my), (1 - mx, 1 - my)]
    arrays = range(len(srcs))

    def copy(a, k, block, to, src=None):
        px, py, pc = block
        slot = outs[a].at[4 * px + 2 * py + pc]
        sem = a * COPIES_PER_ARRAY + k
        return pltpu.make_async_remote_copy(
            src_ref=slot if src is None else src, dst_ref=slot,
            send_sem=send_sems.at[sem], recv_sem=recv_sems.at[sem], device_id=to, device_id_type=MESH_ID)

    mine = [] if local_sems is None else [
        pltpu.make_async_copy(srcs[a], outs[a].at[4 * mx + 2 * my + mc], local_sems.at[a]) for a in arrays]
    first = []
    for a in arrays:
        first.append(copy(a, 0, me, sibling, src=srcs[a]))
        first += [copy(a, 1 + j, me, (*chip, mc), src=srcs[a]) for j, chip in enumerate(chips)]
    if stage in ("all", "start"):
        for cp in mine + first:
            cp.start()
    if stage in ("all", "finish"):
        forwards = []
        for j, chip in enumerate(chips):
            for a in arrays:
                copy(a, 1 + j, (*chip, mc), me).wait_recv()
                fwd = copy(a, 4 + j, (*chip, mc), sibling)
                fwd.start()
                forwards.append(fwd)
        for a in arrays:
            copy(a, 0, sibling, me).wait_recv()
        for j, chip in enumerate(chips):
            for a in arrays:
                copy(a, 4 + j, (*chip, 1 - mc), me).wait_recv()
        for cp in first + forwards:
            cp.wait_send()
        for cp in mine:
            cp.wait()


def _comm_scratch(n):
    return [pltpu.SemaphoreType.DMA((n * COPIES_PER_ARRAY,)), pltpu.SemaphoreType.DMA((n * COPIES_PER_ARRAY,)),
            pltpu.SemaphoreType.DMA((n,))]


def _any_specs(n):
    return [pl.BlockSpec(memory_space=pl.ANY)] * n


def _gather_weights(shards):
    n = len(shards)

    def body(*refs):
        _two_level_gather(refs[:n], refs[n:2 * n], *refs[2 * n:])

    return pl.pallas_call(
        body, name="gather_weights",
        out_shape=[SDS((N_DEV,) + s.shape, s.dtype) for s in shards],
        in_specs=_any_specs(n), out_specs=_any_specs(n), scratch_shapes=_comm_scratch(n),
    )(*shards)


def _gather_small_grads(gs, loss_lanes):
    gs = list(gs) + [loss_lanes]
    n = len(gs)

    def body(*refs):
        g_refs, out_ref = refs[:n], refs[n]
        tile, send_sems, recv_sems = refs[n + 1:]
        tile[...] = jnp.zeros_like(tile)
        for (_, r0, nr, gw, _), g in zip(SMALL_LAYOUT + (LOSS_ENTRY,), g_refs):
            tile[r0:r0 + nr, 0:gw] = g[...]
        me = 4 * lax.axis_index("x") + 2 * lax.axis_index("y") + lax.axis_index("c")
        out_ref[me] = tile[...]
        _two_level_gather([tile], [out_ref], send_sems, recv_sems)

    return pl.pallas_call(
        body, name="gather_small_grads",
        out_shape=SDS((N_DEV, SMALL_ROWS, SMALL_COLS), F32),
        in_specs=[pl.BlockSpec(memory_space=pltpu.VMEM)] * n,
        out_specs=pl.BlockSpec(memory_space=pltpu.VMEM),
        scratch_shapes=[pltpu.VMEM((SMALL_ROWS, SMALL_COLS), F32),
                        pltpu.SemaphoreType.DMA((COPIES_PER_ARRAY,)), pltpu.SemaphoreType.DMA((COPIES_PER_ARRAY,))],
    )(*gs)


def _exchange_grads(slabs):
    n = len(slabs)

    def body(*refs):
        _exchange(refs[:n], refs[n:2 * n], *refs[2 * n:])

    return pl.pallas_call(
        body, name="exchange_grads",
        out_shape=[SDS(s.shape, s.dtype) for s in slabs],
        in_specs=_any_specs(n), out_specs=_any_specs(n), scratch_shapes=_comm_scratch(n),
    )(*slabs)


class _Transfer:
    def __init__(self, kind, arrays):
        self.kind, self.arrays, self.n = kind, list(arrays), len(arrays)

    def out_shapes(self):
        if self.kind == "gather":
            return [SDS((N_DEV,) + a.shape, a.dtype) for a in self.arrays]
        return [SDS(a.shape, a.dtype) for a in self.arrays]

    def run(self, srcs, outs, sems, stage):
        fn = _two_level_gather if self.kind == "gather" else _exchange
        fn(srcs, outs, *sems, stage=stage)


def _call_beside(body, transfer, *, grid, in_specs, out_specs, out_shape, scratch_shapes, name, semantics, args):
    if transfer is None:
        res = pl.pallas_call(body, grid=grid, in_specs=in_specs, out_specs=out_specs, out_shape=out_shape,
                             scratch_shapes=scratch_shapes, name=name, compiler_params=_params(semantics))(*args)
        return list(res), []
    n_in, n_out, n_s, n = len(in_specs), len(out_specs), len(scratch_shapes), transfer.n

    def wrapped(*refs):
        ins, refs = refs[:n_in], refs[n_in:]
        t_in, refs = refs[:n], refs[n:]
        outs, refs = refs[:n_out], refs[n_out:]
        t_out, refs = refs[:n], refs[n:]
        scratch, sems = refs[:n_s], refs[n_s:]
        first = functools.reduce(jnp.logical_and, [pl.program_id(i) == 0 for i in range(len(grid))])
        last = functools.reduce(jnp.logical_and, [pl.program_id(i) == g - 1 for i, g in enumerate(grid)])

        @pl.when(first)
        def _():
            transfer.run(t_in, t_out, sems, "start")

        body(*ins, *outs, *scratch)

        @pl.when(last)
        def _():
            transfer.run(t_in, t_out, sems, "finish")

    res = pl.pallas_call(
        wrapped, grid=grid, in_specs=list(in_specs) + _any_specs(n), out_specs=list(out_specs) + _any_specs(n),
        out_shape=list(out_shape) + transfer.out_shapes(), scratch_shapes=list(scratch_shapes) + _comm_scratch(n),
        name=name, compiler_params=_params(semantics))(*args, *transfer.arrays)
    return list(res[:n_out]), list(res[n_out:])


EXCHANGE_FLIPS = ((0, 0, 1), (1, 0, 0), (0, 1, 0), (1, 1, 0), (1, 0, 1), (0, 1, 1), (1, 1, 1))


def _exchange(srcs, outs, send_sems, recv_sems, local_sems, stage="all"):
    mx, my, mc = lax.axis_index("x"), lax.axis_index("y"), lax.axis_index("c")
    arrays = range(len(srcs))
    copies = [pltpu.make_async_copy(srcs[a].at[4 * mx + 2 * my + mc], outs[a].at[N_DEV - 1], local_sems.at[a])
              for a in arrays]
    for k, (fx, fy, fc) in enumerate(EXCHANGE_FLIPS):
        px = 1 - mx if fx else mx
        py = 1 - my if fy else my
        pc = 1 - mc if fc else mc
        for a in arrays:
            sem = a * COPIES_PER_ARRAY + k
            copies.append(pltpu.make_async_remote_copy(
                src_ref=srcs[a].at[4 * px + 2 * py + pc], dst_ref=outs[a].at[k],
                send_sem=send_sems.at[sem], recv_sem=recv_sems.at[sem],
                device_id=(px, py, pc), device_id_type=MESH_ID))
    if stage in ("all", "start"):
        for cp in copies:
            cp.start()
    if stage in ("all", "finish"):
        for cp in copies:
            cp.wait()


def _w_in_to_padded(w):
    z = lambda n: jnp.zeros((w.shape[0], n), w.dtype)
    return jnp.concatenate([w[:, O_GQKV:O_GZ], w[:, O_GZ:O_GAB], w[:, O_QLAT:O_KVLAT], w[:, O_KVLAT:O_KPE],
                            w[:, O_KPE:O_GQKV], z(P_GAB - P_KPE - ROPE), w[:, O_GAB:O_END],
                            z(P_WIDTH - P_GAB - (O_END - O_GAB))], axis=1)


def _w_in_from_padded(wp):
    return jnp.concatenate([wp[:, P_QLAT:P_QLAT + 256], wp[:, P_KVLAT:P_KVLAT + 256], wp[:, P_KPE:P_KPE + ROPE],
                            wp[:, P_GQKV:P_GZ], wp[:, P_GZ:P_QLAT], wp[:, P_GAB:P_GAB + (O_END - O_GAB)]], axis=1)


def _w_uq_to_headsplit(w):
    w3 = w.reshape(w.shape[0], MLA_HEADS, QK_DIM)
    return jnp.concatenate([w3[:, :, :NOPE].reshape(w.shape[0], -1), w3[:, :, NOPE:].reshape(w.shape[0], -1)], axis=1)


def _w_uq_from_headsplit(wp):
    n = wp[:, :MLA_HEADS * NOPE].reshape(wp.shape[0], MLA_HEADS, NOPE)
    p = wp[:, MLA_HEADS * NOPE:].reshape(wp.shape[0], MLA_HEADS, ROPE)
    return jnp.concatenate([n, p], axis=2).reshape(wp.shape[0], -1)


def _lane_vec(v4):
    return jnp.pad(v4.reshape(1, -1), ((0, 0), (0, LANES - v4.shape[-1])))


def _local_step(x, positions, target, attn_norm_w, w_in, q_lat_norm_w, w_uq, kv_lat_norm_w, w_ukv, q_norm_w,
                k_norm_w, mla_out_norm_w, conv_w, a_log, dt_bias, gdn_norm_w, w_out, mlp_norm_w, w_up, w_down,
                late_shards=None, exchange=False):
    B, S, D = x.shape
    T = B * S
    x2 = x.reshape(T, D)
    t2 = target.reshape(T, D)
    half = ROPE // 2
    inv_freq = ROPE_THETA ** (-jnp.arange(half, dtype=F32) / half)
    ang = positions.reshape(T, 1).astype(F32) * inv_freq
    cosf = jnp.concatenate([jnp.cos(ang)] * 2, axis=-1)
    sinf = jnp.concatenate([jnp.sin(ang)] * 2, axis=-1)
    w_in_p = _w_in_to_padded(w_in)
    w_uq_p = _w_uq_to_headsplit(w_uq)
    alog_l, dt_l = _lane_vec(a_log), _lane_vec(dt_bias)
    w_an, w_qln, w_kvln, qnw, knw, w_mn, gdn_w = (
        attn_norm_w, q_lat_norm_w, kv_lat_norm_w, q_norm_w, k_norm_w, mlp_norm_w, gdn_norm_w)

    proj, xn = _in_proj(x2, w_an, w_in_p)
    q4, k4, v4 = _mla_pre(proj, cosf, sinf, w_qln, w_kvln, w_uq_p, w_ukv, qnw, knw)
    gather = None if late_shards is None else _Transfer("gather", late_shards[:1])
    (o_mla, lse), late = _attn_fwd(q4, k4, v4, B, S, gather)
    if late:
        w_out = late[0].reshape(-1, D)
    qg, kg, vg, gates = _gdn_pre(proj, conv_w, alog_l, dt_l, S)
    gather = None if late_shards is None else _Transfer("gather", late_shards[1:])
    (o_gdn, states, ainv, u4, w4), late = _gdn_fwd(qg, kg, vg, gates, B, S, gather)
    if late:
        w_up, w_down = late[0], late[1].reshape(-1, D)
    h2, mix = _mix_out(o_mla, o_gdn, proj, x2, mla_out_norm_w, gdn_w, w_out)
    up, hn, dy, sq = _mlp_fwd(h2, w_mn, w_up, w_down, t2)
    loss = (0.5 / D) * jnp.sum(sq[:, 0, 0])

    dh, dhb, dup, act, dyb, d_mlp_norm = _mlp_bwd(dy, up, h2, w_mn, w_up, w_down)
    g_w_down = _wgrad(act, dyb, "wgrad_down")
    g_w_up = _wgrad(hn, dup, "wgrad_up", column_shards=True)
    do_mla, do_gdn, dz, d_mla_w, d_gdn_w = _mix_bwd(dhb, o_mla, o_gdn, proj, mla_out_norm_w, gdn_w, w_out)
    g_w_out = _wgrad(mix, dhb, "wgrad_out")
    first = ("w_down",)
    second = ("w_up", "w_out", "w_uq", "w_ukv")
    mats = dict(w_up=g_w_up, w_down=g_w_down, w_out=g_w_out)
    send = _Transfer("exchange", [_slabs(n, mats[n]) for n in first]) if exchange else None
    (dq4, dk4, dv4), got = _attn_bwd(q4, k4, v4, do_mla, o_mla, lse, B, S, send)
    mats.update(zip(first, got))
    dql, dkvl, dkpe, dqraw, dkvraw, qn, kvn, d_wqln, d_wkvln, d_qnw, d_knw = _mla_pre_bwd(
        proj, cosf, sinf, w_qln, w_kvln, w_uq_p, w_ukv, qnw, knw, dq4, dk4, dv4)
    mats.update(w_uq=_wgrad(qn, dqraw, "wgrad_uq"), w_ukv=_wgrad(kvn, dkvraw, "wgrad_ukv"))
    send = _Transfer("exchange", [_slabs(n, mats[n]) for n in second]) if exchange else None
    (dqg, dkg, dvg, dgb4), got = _gdn_bwd(qg, kg, vg, gates, states, ainv, u4, w4, do_gdn, B, S, send)
    mats.update(zip(second, got))
    dc, dgab, g_conv, d_alog, d_dt = _gdn_pre_bwd(proj, conv_w, alog_l, dt_l, dqg, dkg, dvg, dgb4, S)
    dgqkv = _conv_bwd_input(dc, conv_w, S)
    grad_x2, dproj, d_attn_norm = _in_proj_bwd(dgqkv, dz, dql, dkvl, dkpe, dgab, w_in_p, dh, x2, w_an)
    mats.update(w_in=_wgrad(xn, dproj, "wgrad_in"), conv_w=g_conv)
    if exchange:
        last = ("w_in", "conv_w")
        mats.update(zip(last, _exchange_grads([_slabs(n, mats[n]) for n in last])))
    small = dict(attn_norm_w=d_attn_norm, mlp_norm_w=d_mlp_norm, q_lat_norm_w=d_wqln, kv_lat_norm_w=d_wkvln,
                 q_norm_w=d_qnw, k_norm_w=d_knw, mla_out_norm_w=d_mla_w, a_log=d_alog, dt_bias=d_dt,
                 gdn_norm_w=d_gdn_w)
    return loss, grad_x2.reshape(B, S, D), mats, [small[n] for n, *_ in SMALL_LAYOUT]


BIG = ("w_in", "w_uq", "w_ukv", "conv_w", "w_out", "w_up", "w_down")
ALL_W = ("attn_norm_w", "w_in", "q_lat_norm_w", "w_uq", "kv_lat_norm_w", "w_ukv", "q_norm_w", "k_norm_w",
         "mla_out_norm_w", "conv_w", "a_log", "dt_bias", "gdn_norm_w", "w_out", "mlp_norm_w", "w_up", "w_down")
WIRE_SHAPE = {"w_in": (1024, 384), "w_uq": (256, 128), "conv_w": (16, 256)}


def _pad2(a, rows, cols):
    return jnp.pad(a, [(0, 0)] * (a.ndim - 2) + [(0, rows - a.shape[-2]), (0, cols - a.shape[-1])])


def _cols_to_full(stack, cols):
    return jnp.moveaxis(stack[:, :, :cols], 0, 1).reshape(stack.shape[1], N_DEV * cols)


def _full_to_cols(full, wire_cols):
    r, n = full.shape
    return _pad2(jnp.moveaxis(full.reshape(r, N_DEV, n // N_DEV), 1, 0), r, wire_cols)


def _slabs(name, g):
    if name == "w_in":
        return _full_to_cols(_w_in_from_padded(g), WIRE_SHAPE["w_in"][1])
    if name == "w_uq":
        return _full_to_cols(_w_uq_from_headsplit(g), WIRE_SHAPE["w_uq"][1])
    if name == "w_ukv":
        return _full_to_cols(g, g.shape[1] // N_DEV)
    if name == "conv_w":
        return _pad2(_full_to_cols(g.astype(WIRE_DTYPE), g.shape[1] // N_DEV), *WIRE_SHAPE["conv_w"])
    if name == "w_up":
        return g
    return g.reshape(N_DEV, -1, g.shape[-1])


def kernel(x, positions, attn_norm_w, w_in, q_lat_norm_w, w_uq, kv_lat_norm_w, w_ukv, q_norm_w, k_norm_w, mla_out_norm_w, conv_w, a_log, dt_bias, gdn_norm_w, w_out, mlp_norm_w, w_up, w_down, loss_target, m_attn_norm_w, m_w_in, m_q_lat_norm_w, m_w_uq, m_kv_lat_norm_w, m_w_ukv, m_q_norm_w, m_k_norm_w, m_mla_out_norm_w, m_conv_w, m_a_log, m_dt_bias, m_gdn_norm_w, m_w_out, m_mlp_norm_w, m_w_up, m_w_down, v_attn_norm_w, v_w_in, v_q_lat_norm_w, v_w_uq, v_kv_lat_norm_w, v_w_ukv, v_q_norm_w, v_k_norm_w, v_mla_out_norm_w, v_conv_w, v_a_log, v_dt_bias, v_gdn_norm_w, v_w_out, v_mlp_norm_w, v_w_up, v_w_down):
    env = dict(locals())
    W = {n: env[n][0] for n in ALL_W}
    Mo = {n: env["m_" + n][0] for n in ALL_W}
    Vo = {n: env["v_" + n][0] for n in ALL_W}

    two_d = lambda a: a.reshape(1, -1) if a.ndim == 1 else a
    D = x.shape[-1]

    s_in, s_uq, s_ukv, s_conv = _gather_weights([
        _pad2(W["w_in"].astype(WIRE_DTYPE), *WIRE_SHAPE["w_in"]),
        _pad2(W["w_uq"].astype(WIRE_DTYPE), *WIRE_SHAPE["w_uq"]),
        W["w_ukv"].astype(WIRE_DTYPE), _pad2(W["conv_w"], *WIRE_SHAPE["conv_w"])])
    late = [W["w_out"].astype(WIRE_DTYPE), W["w_up"].astype(WIRE_DTYPE), W["w_down"].astype(WIRE_DTYPE)]

    loss, grad_x, parts, gs = _local_step(
        x, positions, loss_target, two_d(W["attn_norm_w"]), _cols_to_full(s_in, W["w_in"].shape[1]),
        two_d(W["q_lat_norm_w"]), _cols_to_full(s_uq, W["w_uq"].shape[1]), two_d(W["kv_lat_norm_w"]),
        _cols_to_full(s_ukv, W["w_ukv"].shape[1]), two_d(W["q_norm_w"]), two_d(W["k_norm_w"]),
        W["mla_out_norm_w"], _cols_to_full(s_conv[:, :CONV_W], W["conv_w"].shape[1]), two_d(W["a_log"]),
        two_d(W["dt_bias"]), two_d(W["gdn_norm_w"]), None, two_d(W["mlp_norm_w"]), None, None,
        late_shards=late, exchange=True)
    done = {n: _reduce_adamw(parts[n], W[n], Mo[n], Vo[n], "adamw_" + n) for n in BIG}
    names = [n for n, *_ in SMALL_LAYOUT]
    tiles = _gather_small_grads(gs, jnp.full((1, LANES), loss, F32))
    small, loss = _adamw_replicated(tiles, [two_d(W[n]) for n in names], [two_d(Mo[n]) for n in names],
                                    [two_d(Vo[n]) for n in names])
    for i, n in enumerate(names):
        done[n] = [small[kind][i] for kind in range(4)]
    res = [done[n][kind].reshape(env[n].shape) for kind in range(4) for n in ALL_W]
    return (loss, grad_x, *res)
```

```python
import functools

import jax
import jax.numpy as jnp
from jax import lax
from jax.experimental import pallas as pl
from jax.experimental.pallas import tpu as pltpu

F32 = jnp.float32
MXU_DTYPE = jnp.bfloat16
WIRE_DTYPE = jnp.bfloat16
SDS = jax.ShapeDtypeStruct
HIGHEST = lax.Precision.HIGHEST
MESH_ID = pl.DeviceIdType.MESH

D_MODEL = 1024
MLA_HEADS = 4
Q_LORA = 256
KV_LORA = 256
NOPE = 128
ROPE = 64
QK_DIM = NOPE + ROPE
V_DIM = 128
ROPE_THETA = 10000.0
GDN_HEADS = 4
GDN_DIM = 128
GDN_WIDTH = GDN_HEADS * GDN_DIM
CONV_W = 4
CHUNK = 64
D_FF = 4 * D_MODEL
EPS = 1e-6
ATT_SCALE = QK_DIM ** -0.5
GDN_QSCALE = GDN_DIM ** -0.5
N_DEV = 8
ATTN_BLOCK = 512
ATTN_CHAINS = 2
MLP_FWD_SHARDS = 4
MLP_BWD_SHARDS = 2

ADAM_LR = 0.001
ADAM_B1 = 0.9
ADAM_B2 = 0.999
ADAM_EPS = 1e-08
ADAM_WD = 0.01
ADAM_STEP = 10

LANES = 128
SUBLANES = 8
VMEM_LIMIT = 56 * 1024 * 1024

P_GQKV, P_GZ, P_QLAT, P_KVLAT, P_KPE, P_GAB = 0, 1536, 2048, 2304, 2560, 2688
P_WIDTH = 2816
O_QLAT, O_KVLAT, O_KPE, O_GQKV, O_GZ, O_GAB, O_END = 0, 256, 512, 576, 2112, 2624, 2632


def _params(sem=None, vmem=VMEM_LIMIT):
    kw = dict(vmem_limit_bytes=vmem)
    if sem is not None:
        kw["dimension_semantics"] = sem
    return pltpu.CompilerParams(**kw)


def _mm(a, b):
    return jnp.dot(a.astype(MXU_DTYPE), b.astype(MXU_DTYPE), preferred_element_type=F32)


def _mm_nt(a, b):
    return lax.dot_general(a.astype(MXU_DTYPE), b.astype(MXU_DTYPE), (((1,), (1,)), ((), ())),
                           preferred_element_type=F32)


def _mm_tn(a, b):
    return lax.dot_general(a.astype(MXU_DTYPE), b.astype(MXU_DTYPE), (((0,), (0,)), ((), ())),
                           preferred_element_type=F32)


def _split(a):
    hi = a.astype(MXU_DTYPE)
    return hi, (a - hi.astype(F32)).astype(MXU_DTYPE)


def _mm_split(a, b):
    (ah, al), (bh, bl) = a, b
    dot = lambda x, y: jnp.dot(x, y, preferred_element_type=F32)
    if MXU_DTYPE == F32:
        return dot(ah, bh)
    return dot(ah, bh) + dot(ah, bl) + dot(al, bh)


def _mm_exact(a, b):
    return _mm_split(_split(a), _split(b))


def _row_sum(v, on_mxu=False):
    if not on_mxu:
        return jnp.sum(v, axis=-1, keepdims=True)
    d = v.shape[-1]
    ones = jnp.ones((d, LANES), MXU_DTYPE)
    s = sum(jnp.dot(p, ones, preferred_element_type=F32) for p in _split(v))
    return s[:, :d] if d <= LANES else jnp.tile(s, (1, d // LANES))


def _rms(x, w, on_mxu=False):
    r = lax.rsqrt(_row_sum(x * x, on_mxu) * (1.0 / x.shape[-1]) + EPS)
    return x * r * w, r


def _rms_bwd(dy, x, w, r, on_mxu=False):
    xh = x * r
    dyw = dy * w
    dx = r * (dyw - xh * (_row_sum(dyw * xh, on_mxu) * (1.0 / x.shape[-1])))
    dw = jnp.sum(dy * xh, axis=0, keepdims=True)
    return dx, dw


def _l2n(x, scale):
    return x * (lax.rsqrt(_row_sum(x * x) + EPS) * scale)


def _l2n_bwd(dy, x, scale):
    r = lax.rsqrt(_row_sum(x * x) + EPS)
    xh = x * r
    return (scale * r) * (dy - xh * _row_sum(dy * xh))


def _rot(t):
    return jnp.concatenate([-t[:, ROPE // 2:], t[:, :ROPE // 2]], axis=-1)


def _rot_t(t):
    return jnp.concatenate([t[:, ROPE // 2:], -t[:, :ROPE // 2]], axis=-1)


def _rope(t, cos, sin):
    return t * cos + _rot(t) * sin


def _rope_bwd(d, cos, sin):
    return d * cos + _rot_t(d * sin)


def _sigmoid(x):
    return jax.nn.sigmoid(x)


def _shift_down(x, halo, j):
    if j == 0:
        return x
    xr = pltpu.roll(x, j, 0)
    hr = pltpu.roll(halo, j, 0)
    row = lax.broadcasted_iota(jnp.int32, halo.shape, 0)
    top = jnp.where(row < j, hr, xr[:SUBLANES])
    return jnp.concatenate([top, xr[SUBLANES:]], axis=0)


def _shift_up(x, nxt, j):
    if j == 0:
        return x
    n = x.shape[0]
    xr = pltpu.roll(x, n - j, 0)
    nr = pltpu.roll(nxt, SUBLANES - j, 0)
    row = lax.broadcasted_iota(jnp.int32, nxt.shape, 0)
    bot = jnp.where(row >= SUBLANES - j, nr, xr[n - SUBLANES:])
    return jnp.concatenate([xr[:n - SUBLANES], bot], axis=0)


def _chunk_cumsum(y, row_in_chunk):
    s = 1
    while s < CHUNK:
        y = y + jnp.where(row_in_chunk >= s, pltpu.roll(y, s, 0), 0.0)
        s *= 2
    return y


def _chunk_rev_cumsum(y, row_in_chunk):
    n = y.shape[0]
    s = 1
    while s < CHUNK:
        y = y + jnp.where(row_in_chunk + s < CHUNK, pltpu.roll(y, n - s, 0), 0.0)
        s *= 2
    return y


def _together(generators):
    alive = list(generators)
    while alive:
        nxt = []
        for g in alive:
            try:
                next(g)
                nxt.append(g)
            except StopIteration:
                pass
        alive = nxt
        yield


def _lockstep(generators):
    for _ in _together(generators):
        pass


def _pick_lane(tile, lane, idx):
    return jnp.sum(jnp.where(lane == idx, tile, 0.0), axis=-1, keepdims=True)


def _divisor_tile(n, cap, unit=LANES):
    best = unit
    t = unit
    while t <= min(n, cap):
        if n % t == 0:
            best = t
        t += unit
    return n if n <= cap else best


def _in_proj(x2, w_an, w_in_p):
    T, D = x2.shape
    N = w_in_p.shape[1]
    tm = min(512, T)

    def body(x_ref, wn_ref, w_ref, proj_ref, xn_ref):
        xn, _ = _rms(x_ref[...], wn_ref[...])
        xn = xn.astype(MXU_DTYPE)
        xn_ref[...] = xn
        proj_ref[...] = jnp.dot(xn, w_ref[...], preferred_element_type=F32)

    return pl.pallas_call(
        body, grid=(T // tm,), name="in_proj",
        in_specs=[pl.BlockSpec((tm, D), lambda i: (i, 0)), pl.BlockSpec((1, D), lambda i: (0, 0)),
                  pl.BlockSpec((D, N), lambda i: (0, 0))],
        out_specs=[pl.BlockSpec((tm, N), lambda i: (i, 0)), pl.BlockSpec((tm, D), lambda i: (i, 0))],
        out_shape=[SDS((T, N), F32), SDS((T, D), MXU_DTYPE)],
        compiler_params=_params(("arbitrary",)),
    )(x2, w_an, w_in_p)


def _mla_pre(proj, cosf, sinf, w_qln, w_kvln, w_uq_p, w_ukv, qnw, knw):
    T = proj.shape[0]
    tm = min(256, T)
    H = MLA_HEADS

    def body(ql_ref, kvl_ref, kpe_ref, cos_ref, sin_ref, wq_ref, wkv_ref, uq_ref, ukv_ref, qnw_ref, knw_ref,
             q_out, k_out, v_out):
        rms = functools.partial(_rms, on_mxu=True)
        cos, sin = cos_ref[...], sin_ref[...]
        qnw_, knw_ = qnw_ref[...], knw_ref[...]
        qn, _ = rms(ql_ref[...], wq_ref[...])
        kvn, _ = rms(kvl_ref[...], wkv_ref[...])
        qraw = _mm(qn, uq_ref[...])
        kvraw = _mm(kvn, ukv_ref[...])
        kpe = _rope(rms(kpe_ref[...][:, :ROPE], knw_[:, NOPE:])[0], cos, sin)
        for h in range(H):
            qn_h = rms(qraw[:, h * NOPE:(h + 1) * NOPE], qnw_[:, :NOPE])[0]
            qp_h = _rope(rms(qraw[:, H * NOPE + h * ROPE:H * NOPE + (h + 1) * ROPE], qnw_[:, NOPE:])[0], cos, sin)
            q_out[h] = (jnp.concatenate([qn_h, qp_h], axis=-1) * ATT_SCALE).astype(MXU_DTYPE)
            kn_h = rms(kvraw[:, h * 256:h * 256 + NOPE], knw_[:, :NOPE])[0]
            k_out[h] = jnp.concatenate([kn_h, kpe], axis=-1).astype(MXU_DTYPE)
            v_out[h] = kvraw[:, h * 256 + NOPE:(h + 1) * 256].astype(MXU_DTYPE)

    full = lambda a: pl.BlockSpec(a.shape, lambda i: (0,) * a.ndim)
    return pl.pallas_call(
        body, grid=(T // tm,), name="mla_pre",
        in_specs=[pl.BlockSpec((tm, 256), lambda i: (i, P_QLAT // 256)),
                  pl.BlockSpec((tm, 256), lambda i: (i, P_KVLAT // 256)),
                  pl.BlockSpec((tm, 128), lambda i: (i, P_KPE // 128)),
                  pl.BlockSpec((tm, ROPE), lambda i: (i, 0)), pl.BlockSpec((tm, ROPE), lambda i: (i, 0)),
                  full(w_qln), full(w_kvln), full(w_uq_p), full(w_ukv), full(qnw), full(knw)],
        out_specs=[pl.BlockSpec((H, tm, QK_DIM), lambda i: (0, i, 0)),
                   pl.BlockSpec((H, tm, QK_DIM), lambda i: (0, i, 0)),
                   pl.BlockSpec((H, tm, V_DIM), lambda i: (0, i, 0))],
        out_shape=[SDS((H, T, QK_DIM), MXU_DTYPE), SDS((H, T, QK_DIM), MXU_DTYPE), SDS((H, T, V_DIM), MXU_DTYPE)],
        compiler_params=_params(("arbitrary",)),
    )(proj, proj, proj, cosf, sinf, w_qln, w_kvln, w_uq_p, w_ukv, qnw, knw)


def _attn_fwd(q4, k4, v4, B, S, transfer=None):
    H = MLA_HEADS
    bq = min(ATTN_BLOCK, S)
    nq = S // bq
    rows = bq // ATTN_CHAINS

    def body(q_ref, k_ref, v_ref, o_ref, lse_ref):
        col = lax.broadcasted_iota(jnp.int32, (rows, bq), 1)
        row = lax.broadcasted_iota(jnp.int32, (rows, bq), 0)

        def q_step(qi, carry):
            qs = pl.multiple_of(qi * bq, bq)
            qsub = [q_ref[0, pl.ds(qs + j * rows, rows), :] for j in range(ATTN_CHAINS)]

            def k_block(ks, cs, diagonal):
                k = k_ref[0, pl.ds(ks, bq), :]
                v = v_ref[0, pl.ds(ks, bq), :]
                out = [None] * ATTN_CHAINS

                def chain(j):
                    m, l, acc = cs[j]
                    s = _mm_nt(qsub[j], k)
                    yield
                    if diagonal:
                        s = jnp.where(col <= row + j * rows, s, -jnp.inf)
                    m_new = jnp.maximum(m, jnp.max(s, axis=-1, keepdims=True))
                    p = jnp.exp(s - m_new)
                    a = jnp.exp(m - m_new)
                    l_new = a * l + jnp.sum(p, axis=-1, keepdims=True)
                    yield
                    out[j] = (m_new, l_new, a * acc + _mm(p, v))

                _lockstep([chain(j) for j in range(ATTN_CHAINS)])
                return tuple(out)

            init = tuple((jnp.full((rows, 1), -jnp.inf, F32), jnp.zeros((rows, 1), F32),
                          jnp.zeros((rows, V_DIM), F32)) for _ in range(ATTN_CHAINS))
            cs = lax.fori_loop(0, qi, lambda kj, c: k_block(pl.multiple_of(kj * bq, bq), c, False), init)
            for j, (m, l, acc) in enumerate(k_block(qs, cs, True)):
                o_ref[0, pl.ds(qs + j * rows, rows), :] = acc / l
                lse_ref[0, pl.ds(qs + j * rows, rows), :] = m + jnp.log(l)
            return carry

        lax.fori_loop(0, nq, q_step, 0)

    spec = lambda d: pl.BlockSpec((1, S, d), lambda h, b: (h, b, 0))
    return _call_beside(
        body, transfer, grid=(H, B), name="attn_fwd",
        in_specs=[spec(QK_DIM), spec(QK_DIM), spec(V_DIM)],
        out_specs=[spec(V_DIM), spec(1)],
        out_shape=[SDS((H, B * S, V_DIM), F32), SDS((H, B * S, 1), F32)],
        scratch_shapes=[], semantics=("arbitrary", "arbitrary"), args=(q4, k4, v4))


def _conv_taps(u, halo, w):
    sh = [_shift_down(u, halo, j) for j in range(CONV_W)]
    c = w[0:1] * sh[3] + w[1:2] * sh[2] + w[2:3] * sh[1] + w[3:4] * sh[0]
    return c, sh


def _gate_values(gab, alog_l, dt_l, lane):
    g = -jnp.exp(alog_l) * jax.nn.softplus(gab + dt_l)
    g = jnp.where(lane < GDN_HEADS, g, 0.0)
    beta = jnp.where((lane >= GDN_HEADS) & (lane < 2 * GDN_HEADS), _sigmoid(gab), 0.0)
    return g, beta


def _gdn_pre(proj, conv_w, alog_l, dt_l, S):
    T = proj.shape[0]
    tm = min(256, T)
    tiles_per_seq = S // tm
    C3 = 3 * GDN_WIDTH
    H = GDN_HEADS

    def body(u_ref, halo_ref, gab_ref, w_ref, alog_ref, dt_ref, q_out, k_out, v_out, gates_out):
        i = pl.program_id(0)
        halo = jnp.where(i % tiles_per_seq == 0, 0.0, halo_ref[...])
        c, _ = _conv_taps(u_ref[...], halo, w_ref[...])
        a = c * _sigmoid(c)
        for h in range(H):
            xq = a[:, h * GDN_DIM:(h + 1) * GDN_DIM]
            xk = a[:, GDN_WIDTH + h * GDN_DIM:GDN_WIDTH + (h + 1) * GDN_DIM]
            q_out[h] = _l2n(xq, GDN_QSCALE)
            k_out[h] = _l2n(xk, 1.0)
            v_out[h] = a[:, 2 * GDN_WIDTH + h * GDN_DIM:2 * GDN_WIDTH + (h + 1) * GDN_DIM]
        lane = lax.broadcasted_iota(jnp.int32, (tm, LANES), 1)
        ric = lax.broadcasted_iota(jnp.int32, (tm, LANES), 0) % CHUNK
        g, beta = _gate_values(gab_ref[...], alog_ref[...], dt_ref[...], lane)
        gates_out[...] = _chunk_cumsum(g, ric) + beta

    hspec = pl.BlockSpec((H, tm, GDN_DIM), lambda i: (0, i, 0))
    return pl.pallas_call(
        body, grid=(T // tm,), name="gdn_pre",
        in_specs=[pl.BlockSpec((tm, C3), lambda i: (i, 0)),
                  pl.BlockSpec((SUBLANES, C3), lambda i: (jnp.maximum(i * (tm // SUBLANES) - 1, 0), 0)),
                  pl.BlockSpec((tm, LANES), lambda i: (i, P_GAB // LANES)),
                  pl.BlockSpec((CONV_W, C3), lambda i: (0, 0)),
                  pl.BlockSpec((1, LANES), lambda i: (0, 0)), pl.BlockSpec((1, LANES), lambda i: (0, 0))],
        out_specs=[hspec, hspec, hspec, pl.BlockSpec((tm, LANES), lambda i: (i, 0))],
        out_shape=[SDS((H, T, GDN_DIM), F32)] * 3 + [SDS((T, LANES), F32)],
        compiler_params=_params(("arbitrary",)),
    )(proj, proj, proj, conv_w, alog_l, dt_l)


def _unit_lower_inverses(Ls, eye):
    Ps = [eye - L for L in Ls]
    Ms = [_split(-L) for L in Ls]
    for _ in range(5):
        sq = [_mm_split(m, m) for m in Ms]
        Ms = [_split(s) for s in sq]
        Ps = [p + _mm_split(_split(p), m) for p, m in zip(Ps, Ms)]
    return Ps


def _chunk_decays(gt, lane, h, ri, ci, rcol):
    Gc = _pick_lane(gt, lane, h)
    bt = _pick_lane(gt, lane, h + GDN_HEADS)
    Gb = jnp.broadcast_to(Gc, (CHUNK, CHUNK))
    Gam = jnp.where(ri >= ci, jnp.exp(Gb - Gb.T), 0.0)
    Gl = jnp.sum(jnp.where(rcol == CHUNK - 1, Gc, 0.0), axis=0, keepdims=True)
    return Gc, bt, Gam, jnp.exp(Gc), jnp.exp(Gl - Gc), jnp.exp(Gl)


GDN_FWD_UNROLL = 16
GDN_BWD_UNROLL = 8
GDN_RECUR_STEPS_PER_STAGE = 2


def _gdn_fwd(qg, kg, vg, gates, B, S, transfer=None):
    H, D, C = GDN_HEADS, GDN_DIM, CHUNK
    NC = S // C
    U = GDN_FWD_UNROLL if NC % GDN_FWD_UNROLL == 0 else 1
    NG = NC // U

    def body(q_ref, k_ref, v_ref, g_ref, o_ref, st_ref, ai_ref, u_ref, w_ref, q2_s, au_s, bc_s, w2_s, el_s):
        h = pl.program_id(0)
        lane = lax.broadcasted_iota(jnp.int32, (C, LANES), 1)
        ri = lax.broadcasted_iota(jnp.int32, (C, C), 0)
        ci = lax.broadcasted_iota(jnp.int32, (C, C), 1)
        rcol = lax.broadcasted_iota(jnp.int32, (C, 1), 0)
        eye = (ri == ci).astype(F32)

        def group(gi, c):
            ns = [gi * U + j for j in range(U)]
            css = [pl.multiple_of(n * C, C) for n in ns]
            qs = [q_ref[0, pl.ds(cs, C), :] for cs in css]
            ks = [k_ref[0, pl.ds(cs, C), :] for cs in css]
            vs = [v_ref[0, pl.ds(cs, C), :] for cs in css]
            decs = [_chunk_decays(g_ref[pl.ds(cs, C), :], lane, h, ri, ci, rcol) for cs in css]
            qks = [_mm_nt(jnp.concatenate([q, k], axis=0), k) for q, k in zip(qs, ks)]
            ainvs = _unit_lower_inverses(
                [jnp.where(ri > ci, d[1] * qk[C:] * d[2], 0.0) for qk, d in zip(qks, decs)], eye)
            sols = [_mm_exact(a, jnp.concatenate([v * d[1], k * (d[1] * d[3])], axis=-1))
                    for a, k, v, d in zip(ainvs, ks, vs, decs)]
            atuw = [_mm(qk[:C] * d[2], sol) for qk, d, sol in zip(qks, decs, sols)]
            kduw = [_mm_tn(k * d[4], sol) for k, d, sol in zip(ks, decs, sols)]
            for n, cs, q, a, sol, au, ku, (Gc, bt, Gam, e, f, eL) in zip(ns, css, qs, ainvs, sols, atuw, kduw, decs):
                u_ref[0, pl.ds(cs, C), :] = sol[:, :D]
                w_ref[0, pl.ds(cs, C), :] = sol[:, D:]
                au_s[pl.ds(cs, C), :] = au[:, :D]
                q2_s[pl.ds(cs, C), :] = q * e - au[:, D:]
                bc_s[n] = ku[:, :D]
                w2_s[n] = ku[:, D:]
                el_s[n] = jnp.broadcast_to(eL, (SUBLANES, LANES))
                ai_ref[0, n] = a.T
            return c

        lax.fori_loop(0, NG, group, 0)

        def step(n, S_):
            cs = pl.multiple_of(n * C, C)
            o_ref[0, pl.ds(cs, C), :] = _mm(q2_s[pl.ds(cs, C), :], S_) + au_s[pl.ds(cs, C), :]
            st_ref[0, n] = S_
            return S_ * el_s[n, 0:1, :] + bc_s[n] - _mm(w2_s[n], S_)

        lax.fori_loop(0, NC, step, jnp.zeros((D, D), F32))

    spec = pl.BlockSpec((1, S, D), lambda h, b: (h, b, 0))
    return _call_beside(
        body, transfer, grid=(H, B), name="gdn_fwd",
        in_specs=[spec, spec, spec, pl.BlockSpec((S, LANES), lambda h, b: (b, 0))],
        out_specs=[spec, pl.BlockSpec((1, NC, D, D), lambda h, b: (h, b, 0, 0)),
                   pl.BlockSpec((1, NC, C, C), lambda h, b: (h, b, 0, 0)), spec, spec],
        out_shape=[SDS((H, B * S, D), F32), SDS((H, B * NC, D, D), F32), SDS((H, B * NC, C, C), F32),
                   SDS((H, B * S, D), F32), SDS((H, B * S, D), F32)],
        scratch_shapes=[pltpu.VMEM((S, D), F32), pltpu.VMEM((S, D), F32), pltpu.VMEM((NC, D, D), F32),
                        pltpu.VMEM((NC, D, D), F32), pltpu.VMEM((NC, SUBLANES, LANES), F32)],
        semantics=("arbitrary", "arbitrary"), args=(qg, kg, vg, gates))


def _mix_out(o_mla, o_gdn, proj, x2, mla_w, gdn_w, w_out):
    T, D = x2.shape
    tm = min(512, T)
    H = MLA_HEADS

    def body(om_ref, og_ref, z_ref, x_ref, mw_ref, gw_ref, w_ref, h_ref, mix_ref):
        z = z_ref[...]
        parts = [_rms(om_ref[h], mw_ref[h:h + 1, :])[0] for h in range(H)]
        for h in range(GDN_HEADS):
            zh = z[:, h * GDN_DIM:(h + 1) * GDN_DIM]
            parts.append(_rms(og_ref[h], gw_ref[...])[0] * (zh * _sigmoid(zh)))
        mix = jnp.concatenate(parts, axis=-1).astype(MXU_DTYPE)
        mix_ref[...] = mix
        h_ref[...] = x_ref[...] + jnp.dot(mix, w_ref[...], preferred_element_type=F32)

    hspec = pl.BlockSpec((H, tm, V_DIM), lambda i: (0, i, 0))
    return pl.pallas_call(
        body, grid=(T // tm,), name="mix_out",
        in_specs=[hspec, hspec, pl.BlockSpec((tm, GDN_WIDTH), lambda i: (i, P_GZ // GDN_WIDTH)),
                  pl.BlockSpec((tm, D), lambda i: (i, 0)),
                  pl.BlockSpec((H, V_DIM), lambda i: (0, 0)), pl.BlockSpec((1, GDN_DIM), lambda i: (0, 0)),
                  pl.BlockSpec((D, D), lambda i: (0, 0))],
        out_specs=[pl.BlockSpec((tm, D), lambda i: (i, 0)), pl.BlockSpec((tm, D), lambda i: (i, 0))],
        out_shape=[SDS((T, D), F32), SDS((T, D), MXU_DTYPE)],
        compiler_params=_params(("arbitrary",)),
    )(o_mla, o_gdn, proj, x2, mla_w, gdn_w, w_out)


def _mlp_fwd(h2, w_mn, w_up, w_down, target):
    T, D = h2.shape
    ns, _, ts = w_up.shape
    F = ns * ts
    tm = min(512, T)
    G = MLP_FWD_SHARDS
    tf, nf = G * ts, ns // G

    def body(h_ref, wn_ref, up_w, down_w, t_ref, up_ref, hn_ref, dy_ref, loss_ref, y_acc):
        j = pl.program_id(1)

        @pl.when(j == 0)
        def _():
            hn_ref[...] = _rms(h_ref[...], wn_ref[...])[0].astype(MXU_DTYPE)
            y_acc[...] = h_ref[...]

        parts = []
        for c in range(G):
            up = jnp.dot(hn_ref[...], up_w[c], preferred_element_type=F32)
            up_ref[:, c * ts:(c + 1) * ts] = up
            r = jnp.maximum(up, 0.0)
            parts.append(_mm(r * r, down_w[c * ts:(c + 1) * ts, :]))
        y_acc[...] += functools.reduce(jnp.add, parts)

        @pl.when(j == nf - 1)
        def _():
            err = y_acc[...] - t_ref[...]
            dy_ref[...] = err / D
            loss_ref[...] = jnp.full((1, SUBLANES, LANES), jnp.sum(err * err), F32)

    return pl.pallas_call(
        body, grid=(T // tm, nf), name="mlp_fwd",
        in_specs=[pl.BlockSpec((tm, D), lambda i, j: (i, 0)), pl.BlockSpec((1, D), lambda i, j: (0, 0)),
                  pl.BlockSpec((G, D, ts), lambda i, j: (j, 0, 0)), pl.BlockSpec((tf, D), lambda i, j: (j, 0)),
                  pl.BlockSpec((tm, D), lambda i, j: (i, 0))],
        out_specs=[pl.BlockSpec((tm, tf), lambda i, j: (i, j)), pl.BlockSpec((tm, D), lambda i, j: (i, 0)),
                   pl.BlockSpec((tm, D), lambda i, j: (i, 0)),
                   pl.BlockSpec((1, SUBLANES, LANES), lambda i, j: (i, 0, 0))],
        out_shape=[SDS((T, F), F32), SDS((T, D), MXU_DTYPE), SDS((T, D), F32),
                   SDS((T // tm, SUBLANES, LANES), F32)],
        scratch_shapes=[pltpu.VMEM((tm, D), F32)],
        compiler_params=_params(("arbitrary", "arbitrary")),
    )(h2, w_mn, w_up, w_down, target)


def _mlp_bwd(dy, up, h2, w_mn, w_up, w_down):
    T, D = h2.shape
    ns, _, ts = w_up.shape
    F = ns * ts
    tm = min(512, T)
    G = MLP_BWD_SHARDS
    tf, nf = G * ts, ns // G

    def body(dy_ref, up_ref, h_ref, wn_ref, up_w, down_w, dh_ref, dhb_ref, dup_ref, act_ref, dyb_ref, dwn_ref, acc):
        i, j = pl.program_id(0), pl.program_id(1)

        @pl.when((i == 0) & (j == 0))
        def _():
            dwn_ref[...] = jnp.zeros_like(dwn_ref)

        @pl.when(j == 0)
        def _():
            acc[...] = jnp.zeros_like(acc)
            dyb_ref[...] = dy_ref[...].astype(MXU_DTYPE)

        parts = []
        for c in range(G):
            cols = slice(c * ts, (c + 1) * ts)
            r = jnp.maximum(up_ref[:, cols], 0.0)
            act_ref[:, cols] = (r * r).astype(MXU_DTYPE)
            dup = (_mm_nt(dyb_ref[...], down_w[cols, :]) * (2.0 * r)).astype(MXU_DTYPE)
            dup_ref[:, cols] = dup
            parts.append(_mm_nt(dup, up_w[c]))
        acc[...] += functools.reduce(jnp.add, parts)

        @pl.when(j == nf - 1)
        def _():
            hv = h_ref[...]
            _, rr = _rms(hv, wn_ref[...])
            dx, dw = _rms_bwd(acc[...], hv, wn_ref[...], rr)
            dh = dy_ref[...] + dx
            dh_ref[...] = dh
            dhb_ref[...] = dh.astype(MXU_DTYPE)
            dwn_ref[...] += dw

    row = lambda i, j: (i, 0)
    return pl.pallas_call(
        body, grid=(T // tm, nf), name="mlp_bwd",
        in_specs=[pl.BlockSpec((tm, D), row), pl.BlockSpec((tm, tf), lambda i, j: (i, j)), pl.BlockSpec((tm, D), row),
                  pl.BlockSpec((1, D), lambda i, j: (0, 0)),
                  pl.BlockSpec((G, D, ts), lambda i, j: (j, 0, 0)), pl.BlockSpec((tf, D), lambda i, j: (j, 0))],
        out_specs=[pl.BlockSpec((tm, D), row), pl.BlockSpec((tm, D), row),
                   pl.BlockSpec((tm, tf), lambda i, j: (i, j)), pl.BlockSpec((tm, tf), lambda i, j: (i, j)),
                   pl.BlockSpec((tm, D), row), pl.BlockSpec((1, D), lambda i, j: (0, 0))],
        out_shape=[SDS((T, D), F32), SDS((T, D), MXU_DTYPE), SDS((T, F), MXU_DTYPE), SDS((T, F), MXU_DTYPE),
                   SDS((T, D), MXU_DTYPE), SDS((1, D), F32)],
        scratch_shapes=[pltpu.VMEM((tm, D), F32)],
        compiler_params=_params(("arbitrary", "arbitrary")),
    )(dy, up, h2, w_mn, w_up, w_down)


def _mix_bwd(dhb, o_mla, o_gdn, proj, mla_w, gdn_w, w_out):
    T, D = dhb.shape
    tm = min(512, T)
    H = MLA_HEADS

    def body(dh_ref, om_ref, og_ref, z_ref, mw_ref, gw_ref, w_ref, dom_ref, dog_ref, dz_ref, dmw_ref, dgw_ref):
        @pl.when(pl.program_id(0) == 0)
        def _():
            dmw_ref[...] = jnp.zeros_like(dmw_ref)
            dgw_ref[...] = jnp.zeros_like(dgw_ref)

        dmix = _mm_nt(dh_ref[...], w_ref[...])
        z = z_ref[...]
        dmw, dzs = [], []
        dgw = jnp.zeros((1, GDN_DIM), F32)
        for h in range(H):
            o = om_ref[h]
            w = mw_ref[h:h + 1, :]
            _, r = _rms(o, w)
            dx, dw = _rms_bwd(dmix[:, h * V_DIM:(h + 1) * V_DIM], o, w, r)
            dom_ref[h] = dx
            dmw.append(dw)
        for h in range(GDN_HEADS):
            o = og_ref[h]
            w = gw_ref[...]
            zh = z[:, h * GDN_DIM:(h + 1) * GDN_DIM]
            sg = _sigmoid(zh)
            yn, r = _rms(o, w)
            dy = dmix[:, H * V_DIM + h * GDN_DIM:H * V_DIM + (h + 1) * GDN_DIM]
            dzs.append(dy * yn * (sg * (1.0 + zh * (1.0 - sg))))
            dx, dw = _rms_bwd(dy * (zh * sg), o, w, r)
            dog_ref[h] = dx
            dgw = dgw + dw
        dz_ref[...] = jnp.concatenate(dzs, axis=-1)
        dmw_ref[...] += jnp.concatenate(dmw, axis=0)
        dgw_ref[...] += dgw

    hspec = pl.BlockSpec((H, tm, V_DIM), lambda i: (0, i, 0))
    return pl.pallas_call(
        body, grid=(T // tm,), name="mix_bwd",
        in_specs=[pl.BlockSpec((tm, D), lambda i: (i, 0)), hspec, hspec,
                  pl.BlockSpec((tm, GDN_WIDTH), lambda i: (i, P_GZ // GDN_WIDTH)),
                  pl.BlockSpec((H, V_DIM), lambda i: (0, 0)), pl.BlockSpec((1, GDN_DIM), lambda i: (0, 0)),
                  pl.BlockSpec((D, D), lambda i: (0, 0))],
        out_specs=[hspec, hspec, pl.BlockSpec((tm, GDN_WIDTH), lambda i: (i, 0)),
                   pl.BlockSpec((H, V_DIM), lambda i: (0, 0)), pl.BlockSpec((1, GDN_DIM), lambda i: (0, 0))],
        out_shape=[SDS((H, T, V_DIM), F32), SDS((H, T, GDN_DIM), F32), SDS((T, GDN_WIDTH), F32),
                   SDS((H, V_DIM), F32), SDS((1, GDN_DIM), F32)],
        compiler_params=_params(("arbitrary",)),
    )(dhb, o_mla, o_gdn, proj, mla_w, gdn_w, w_out)


def _attn_bwd(q4, k4, v4, do4, o4, lse4, B, S, transfer=None):
    H = MLA_HEADS
    bq = min(ATTN_BLOCK, S)
    nq = S // bq
    rows = bq // ATTN_CHAINS

    def body(q_ref, k_ref, v_ref, do_ref, o_ref, lse_ref, dq_ref, dk_ref, dv_ref, delta):
        dq_ref[...] = jnp.zeros_like(dq_ref)
        dk_ref[...] = jnp.zeros_like(dk_ref)
        dv_ref[...] = jnp.zeros_like(dv_ref)
        delta[...] = jnp.sum(do_ref[0] * o_ref[0], axis=-1, keepdims=True)

        col = lax.broadcasted_iota(jnp.int32, (rows, bq), 1)
        row = lax.broadcasted_iota(jnp.int32, (rows, bq), 0)

        def k_step(kj, carry):
            ks = pl.multiple_of(kj * bq, bq)
            k = k_ref[0, pl.ds(ks, bq), :]
            v = v_ref[0, pl.ds(ks, bq), :]

            def q_block(qs, diagonal):
                dks, dvs = [None] * ATTN_CHAINS, [None] * ATTN_CHAINS

                def chain(j):
                    sl = pl.ds(qs + j * rows, rows)
                    q = q_ref[0, sl, :]
                    do = do_ref[0, sl, :].astype(MXU_DTYPE)
                    s = _mm_nt(q, k)
                    dp = _mm_nt(do, v)
                    yield
                    p = jnp.exp(s - lse_ref[0, sl, :])
                    if diagonal:
                        p = jnp.where(col <= row + j * rows, p, 0.0)
                    ds = p * (dp - delta[sl, :])
                    yield
                    dvs[j] = _mm_tn(p, do)
                    dks[j] = _mm_tn(ds, q)
                    dq_ref[0, sl, :] += _mm(ds, k)

                _lockstep([chain(j) for j in range(ATTN_CHAINS)])
                dv_ref[0, pl.ds(ks, bq), :] += functools.reduce(jnp.add, dvs)
                dk_ref[0, pl.ds(ks, bq), :] += functools.reduce(jnp.add, dks)

            q_block(ks, True)

            def q_step(qi, c):
                q_block(pl.multiple_of(qi * bq, bq), False)
                return c

            lax.fori_loop(kj + 1, nq, q_step, 0)
            return carry

        lax.fori_loop(0, nq, k_step, 0)

    spec = lambda d: pl.BlockSpec((1, S, d), lambda h, b: (h, b, 0))
    return _call_beside(
        body, transfer, grid=(H, B), name="attn_bwd",
        in_specs=[spec(QK_DIM), spec(QK_DIM), spec(V_DIM), spec(V_DIM), spec(V_DIM), spec(1)],
        out_specs=[spec(QK_DIM), spec(QK_DIM), spec(V_DIM)],
        out_shape=[SDS((H, B * S, QK_DIM), F32), SDS((H, B * S, QK_DIM), F32), SDS((H, B * S, V_DIM), F32)],
        scratch_shapes=[pltpu.VMEM((S, 1), F32)], semantics=("arbitrary", "arbitrary"),
        args=(q4, k4, v4, do4, o4, lse4))


def _gdn_bwd(qg, kg, vg, gates, states, ainv, u4, w4, do4, B, S, transfer=None):
    H, D, C = GDN_HEADS, GDN_DIM, CHUNK
    NC = S // C
    U = GDN_BWD_UNROLL if NC % GDN_BWD_UNROLL == 0 else 1
    NG = NC // U

    def body(q_ref, k_ref, v_ref, g_ref, st_ref, ai_ref, u_ref, w_ref, do_ref, dq_ref, dk_ref, dv_ref, dgb_ref,
             kd_s, x1_s, x2_s, el_s, dvn_s, ds_s, w2t_s):
        h = pl.program_id(0)
        lane = lax.broadcasted_iota(jnp.int32, (C, LANES), 1)
        ri = lax.broadcasted_iota(jnp.int32, (C, C), 0)
        ci = lax.broadcasted_iota(jnp.int32, (C, C), 1)
        rcol = lax.broadcasted_iota(jnp.int32, (C, 1), 0)

        def rsum(a):
            return jnp.sum(a, axis=-1, keepdims=True)

        def prepare(n):
            cs = n * C
            q = q_ref[0, pl.ds(cs, C), :]
            k = k_ref[0, pl.ds(cs, C), :]
            do = do_ref[0, pl.ds(cs, C), :]
            Gc, bt, Gam, e, f, eL = _chunk_decays(g_ref[pl.ds(cs, C), :], lane, h, ri, ci, rcol)
            At = _mm_nt(q, k) * Gam
            yield
            x1 = _mm_tn(At, do)
            x2 = _mm_tn(q * e, do)
            kd = k * f
            w = w_ref[0, pl.ds(cs, C), :]
            yield
            x1_s[pl.ds(cs, C), :] = x1
            x2_s[n] = x2 - _mm_tn(w, x1)
            w2t_s[n] = _mm_tn(w, kd)
            kd_s[pl.ds(cs, C), :] = kd
            el_s[n] = jnp.broadcast_to(eL, (SUBLANES, LANES))

        def recur(n, dS):
            cs = n * C
            ds_s[n] = dS
            dvn_s[pl.ds(cs, C), :] = x1_s[pl.ds(cs, C), :] + _mm(kd_s[pl.ds(cs, C), :], dS)
            return x2_s[n] + el_s[n, 0:1, :] * dS - _mm(w2t_s[n], dS)

        def local(n):
            cs = n * C
            q = q_ref[0, pl.ds(cs, C), :]
            k = k_ref[0, pl.ds(cs, C), :]
            v = v_ref[0, pl.ds(cs, C), :]
            do = do_ref[0, pl.ds(cs, C), :]
            u = u_ref[0, pl.ds(cs, C), :]
            w = w_ref[0, pl.ds(cs, C), :]
            dvn = dvn_s[pl.ds(cs, C), :]
            dS = ds_s[n]
            Gc, bt, Gam, e, f, eL = _chunk_decays(g_ref[pl.ds(cs, C), :], lane, h, ri, ci, rcol)
            S0 = st_ref[0, n]
            AinvT = ai_ref[0, n]
            qk = _mm_nt(jnp.concatenate([q, k], axis=0), k)
            QK, KK = qk[:C], qk[C:]
            be = bt * e
            sol = jnp.concatenate([u, w], axis=-1)
            vn = u - _mm(w, S0)
            yield
            dAt = jnp.where(ri >= ci, _mm_nt(do, vn), 0.0)
            dqd = _mm_nt(do, S0)
            dw = -_mm_nt(dvn, S0)
            dkd = _mm_nt(vn, dS)
            deL = jnp.sum(rsum(dS * S0), axis=0, keepdims=True)
            yield
            dR = _mm_exact(AinvT, jnp.concatenate([dvn, dw], axis=-1))
            dR1, dR2 = dR[:, :D], dR[:, D:]
            yield
            dL = jnp.where(ri > ci, -_mm_nt(dR, sol), 0.0)
            yield
            dv_ref[0, pl.ds(cs, C), :] = dR1 * bt
            r2 = rsum(dR2 * k)
            X = dL * Gam
            dbt = rsum(dR1 * v) + r2 * e + rsum(X * KK)
            de = r2 * bt + rsum(dqd * q)
            dKK = X * bt
            dQK = dAt * Gam
            dq_ref[0, pl.ds(cs, C), :] = _mm(dQK, k) + dqd * e
            dk_ref[0, pl.ds(cs, C), :] = dR2 * be + _mm(dKK + dKK.T, k) + _mm_tn(dQK, q) + dkd * f
            df = rsum(dkd * k)
            Z = (dL * (bt * KK) + dAt * QK) * Gam
            dG = rsum(Z) - rsum(Z.T) + de * e - df * f
            dGl = jnp.sum(df * f, axis=0, keepdims=True) + deL * eL
            dG = dG + jnp.where(rcol == C - 1, dGl, 0.0)
            dgb_ref[0, pl.ds(cs, C), :] = jnp.where(lane == 0, dG, jnp.where(lane == 1, dbt, 0.0))

        state = [jnp.zeros((D, D), F32)]

        def recur_group(g):
            for j, n in enumerate(reversed(range(g * U, (g + 1) * U))):
                state[0] = recur(n, state[0])
                if j % GDN_RECUR_STEPS_PER_STAGE == GDN_RECUR_STEPS_PER_STAGE - 1:
                    yield

        def stage(fn, g):
            return _together([fn(g * U + j) for j in range(U)])

        for step in range(NG + 2):
            jobs = [(stage, prepare, NG - 1 - step), (None, None, NG - step), (stage, local, NG + 1 - step)]
            _lockstep([recur_group(g) if make is None else make(fn, g) for make, fn, g in jobs if 0 <= g < NG])

    spec = pl.BlockSpec((1, S, D), lambda h, b: (h, b, 0))
    return _call_beside(
        body, transfer, grid=(H, B), name="gdn_bwd",
        in_specs=[spec, spec, spec, pl.BlockSpec((S, LANES), lambda h, b: (b, 0)),
                  pl.BlockSpec((1, NC, D, D), lambda h, b: (h, b, 0, 0)),
                  pl.BlockSpec((1, NC, C, C), lambda h, b: (h, b, 0, 0)), spec, spec, spec],
        out_specs=[spec, spec, spec, spec],
        out_shape=[SDS((H, B * S, D), F32)] * 4,
        scratch_shapes=[pltpu.VMEM((S, D), F32), pltpu.VMEM((S, D), F32), pltpu.VMEM((NC, D, D), F32),
                        pltpu.VMEM((NC, SUBLANES, LANES), F32), pltpu.VMEM((S, D), F32),
                        pltpu.VMEM((NC, D, D), F32), pltpu.VMEM((NC, D, D), F32)],
        semantics=("arbitrary", "arbitrary"), args=(qg, kg, vg, gates, states, ainv, u4, w4, do4))


def _gdn_pre_bwd(proj, conv_w, alog_l, dt_l, dq4, dk4, dv4, dgb4, S):
    T = proj.shape[0]
    tm = min(256, T)
    tiles_per_seq = S // tm
    C3 = 3 * GDN_WIDTH
    H = GDN_HEADS

    def body(u_ref, halo_ref, gab_ref, w_ref, alog_ref, dt_ref, dq_ref, dk_ref, dv_ref, dgb_ref,
             dc_ref, dgab_ref, dcw_ref, dalog_ref, ddt_ref):
        i = pl.program_id(0)

        @pl.when(i == 0)
        def _():
            dcw_ref[...] = jnp.zeros_like(dcw_ref)
            dalog_ref[...] = jnp.zeros_like(dalog_ref)
            ddt_ref[...] = jnp.zeros_like(ddt_ref)

        halo = jnp.where(i % tiles_per_seq == 0, 0.0, halo_ref[...])
        c, sh = _conv_taps(u_ref[...], halo, w_ref[...])
        sg = _sigmoid(c)
        a = c * sg
        das = [None] * (3 * H)
        for h in range(H):
            xq = a[:, h * GDN_DIM:(h + 1) * GDN_DIM]
            xk = a[:, GDN_WIDTH + h * GDN_DIM:GDN_WIDTH + (h + 1) * GDN_DIM]
            das[h] = _l2n_bwd(dq_ref[h], xq, GDN_QSCALE)
            das[H + h] = _l2n_bwd(dk_ref[h], xk, 1.0)
            das[2 * H + h] = dv_ref[h]
        dc = jnp.concatenate(das, axis=-1) * (sg * (1.0 + c * (1.0 - sg)))
        dc_ref[...] = dc
        dcw_ref[...] += jnp.concatenate(
            [jnp.sum(dc * sh[CONV_W - 1 - t], axis=0, keepdims=True) for t in range(CONV_W)], axis=0)
        lane = lax.broadcasted_iota(jnp.int32, (tm, LANES), 1)
        ric = lax.broadcasted_iota(jnp.int32, (tm, LANES), 0) % CHUNK
        dG = jnp.zeros((tm, LANES), F32)
        for h in range(H):
            t = dgb_ref[h]
            dG = dG + jnp.where(lane == h, _pick_lane(t, lane, 0), 0.0) \
                    + jnp.where(lane == h + H, _pick_lane(t, lane, 1), 0.0)
        is_g = lane < H
        dg = jnp.where(is_g, _chunk_rev_cumsum(jnp.where(is_g, dG, 0.0), ric), 0.0)
        gab = gab_ref[...]
        g, beta = _gate_values(gab, alog_ref[...], dt_ref[...], lane)
        dga = jnp.where(is_g, dg * (-jnp.exp(alog_ref[...])) * _sigmoid(gab + dt_ref[...]), 0.0)
        dgb = jnp.where(is_g, 0.0, dG) * beta * (1.0 - beta)
        dgab_ref[...] = dga + dgb
        dalog_ref[...] += jnp.sum(dg * g, axis=0, keepdims=True)
        ddt_ref[...] += jnp.sum(dga, axis=0, keepdims=True)

    hspec = pl.BlockSpec((H, tm, GDN_DIM), lambda i: (0, i, 0))
    vec = pl.BlockSpec((1, LANES), lambda i: (0, 0))
    return pl.pallas_call(
        body, grid=(T // tm,), name="gdn_pre_bwd",
        in_specs=[pl.BlockSpec((tm, C3), lambda i: (i, 0)),
                  pl.BlockSpec((SUBLANES, C3), lambda i: (jnp.maximum(i * (tm // SUBLANES) - 1, 0), 0)),
                  pl.BlockSpec((tm, LANES), lambda i: (i, P_GAB // LANES)),
                  pl.BlockSpec((CONV_W, C3), lambda i: (0, 0)), vec, vec, hspec, hspec, hspec, hspec],
        out_specs=[pl.BlockSpec((tm, C3), lambda i: (i, 0)), pl.BlockSpec((tm, LANES), lambda i: (i, 0)),
                   pl.BlockSpec((CONV_W, C3), lambda i: (0, 0)), vec, vec],
        out_shape=[SDS((T, C3), F32), SDS((T, LANES), F32), SDS((CONV_W, C3), F32),
                   SDS((1, LANES), F32), SDS((1, LANES), F32)],
        compiler_params=_params(("arbitrary",)),
    )(proj, proj, proj, conv_w, alog_l, dt_l, dq4, dk4, dv4, dgb4)


def _conv_bwd_input(dc, conv_w, S):
    T, C3 = dc.shape
    tm = min(256, T)
    tiles_per_seq = S // tm
    nblk = T // SUBLANES

    def body(dc_ref, nxt_ref, w_ref, du_ref):
        i = pl.program_id(0)
        nxt = jnp.where(i % tiles_per_seq == tiles_per_seq - 1, 0.0, nxt_ref[...])
        x = dc_ref[...]
        w = w_ref[...]
        du = w[3:4] * x
        for j in range(1, CONV_W):
            du = du + w[3 - j:4 - j] * _shift_up(x, nxt, j)
        du_ref[...] = du

    return pl.pallas_call(
        body, grid=(T // tm,), name="conv_bwd_input",
        in_specs=[pl.BlockSpec((tm, C3), lambda i: (i, 0)),
                  pl.BlockSpec((SUBLANES, C3), lambda i: (jnp.minimum((i + 1) * (tm // SUBLANES), nblk - 1), 0)),
                  pl.BlockSpec((CONV_W, C3), lambda i: (0, 0))],
        out_specs=pl.BlockSpec((tm, C3), lambda i: (i, 0)),
        out_shape=SDS((T, C3), F32),
        compiler_params=_params(("arbitrary",)),
    )(dc, dc, conv_w)


def _mla_pre_bwd(proj, cosf, sinf, w_qln, w_kvln, w_uq_p, w_ukv, qnw, knw, dq4, dk4, dv4):
    T = proj.shape[0]
    tm = min(256, T)
    H = MLA_HEADS

    def body(ql_ref, kvl_ref, kpe_ref, cos_ref, sin_ref, wq_ref, wkv_ref, uq_ref, ukv_ref, qnw_ref, knw_ref,
             dq_ref, dk_ref, dv_ref,
             dql_ref, dkvl_ref, dkpe_ref, dqraw_ref, dkvraw_ref, qn_ref, kvn_ref, dwq_ref, dwkv_ref, dqnw_ref, dknw_ref):
        @pl.when(pl.program_id(0) == 0)
        def _():
            for r in (dwq_ref, dwkv_ref, dqnw_ref, dknw_ref):
                r[...] = jnp.zeros_like(r)

        cos, sin = cos_ref[...], sin_ref[...]
        qnw_, knw_ = qnw_ref[...], knw_ref[...]
        ql, kvl = ql_ref[...], kvl_ref[...]
        kpe_raw = kpe_ref[...][:, :ROPE]
        rms = functools.partial(_rms, on_mxu=True)
        rms_bwd = functools.partial(_rms_bwd, on_mxu=True)
        qn, rq = rms(ql, wq_ref[...])
        kvn, rkv = rms(kvl, wkv_ref[...])
        qn_ref[...] = qn.astype(MXU_DTYPE)
        kvn_ref[...] = kvn.astype(MXU_DTYPE)
        qraw = _mm(qn, uq_ref[...])
        kvraw = _mm(kvn, ukv_ref[...])
        dq_nope, dq_pe, dkv_parts = [], [], []
        dqnw_n = jnp.zeros((1, NOPE), F32)
        dqnw_p = jnp.zeros((1, ROPE), F32)
        dknw_n = jnp.zeros((1, NOPE), F32)
        dkpe = jnp.zeros((tm, ROPE), F32)
        for h in range(H):
            dq = dq_ref[h] * ATT_SCALE
            x = qraw[:, h * NOPE:(h + 1) * NOPE]
            dx, dw = rms_bwd(dq[:, :NOPE], x, qnw_[:, :NOPE], rms(x, qnw_[:, :NOPE])[1])
            dq_nope.append(dx)
            dqnw_n = dqnw_n + dw
            x = qraw[:, H * NOPE + h * ROPE:H * NOPE + (h + 1) * ROPE]
            dx, dw = rms_bwd(_rope_bwd(dq[:, NOPE:], cos, sin), x, qnw_[:, NOPE:], rms(x, qnw_[:, NOPE:])[1])
            dq_pe.append(dx)
            dqnw_p = dqnw_p + dw
            dk = dk_ref[h]
            x = kvraw[:, h * 256:h * 256 + NOPE]
            dx, dw = rms_bwd(dk[:, :NOPE], x, knw_[:, :NOPE], rms(x, knw_[:, :NOPE])[1])
            dknw_n = dknw_n + dw
            dkpe = dkpe + dk[:, NOPE:]
            dkv_parts += [dx, dv_ref[h]]
        dx, dknw_p = rms_bwd(_rope_bwd(dkpe, cos, sin), kpe_raw, knw_[:, NOPE:], rms(kpe_raw, knw_[:, NOPE:])[1])
        dkpe_ref[...] = jnp.concatenate([dx, jnp.zeros((tm, LANES - ROPE), F32)], axis=-1)
        dqraw = jnp.concatenate(dq_nope + dq_pe, axis=-1).astype(MXU_DTYPE)
        dkvraw = jnp.concatenate(dkv_parts, axis=-1).astype(MXU_DTYPE)
        dqraw_ref[...] = dqraw
        dkvraw_ref[...] = dkvraw
        dx, dw = rms_bwd(_mm_nt(dqraw, uq_ref[...]), ql, wq_ref[...], rq)
        dql_ref[...] = dx
        dwq_ref[...] += dw
        dx, dw = rms_bwd(_mm_nt(dkvraw, ukv_ref[...]), kvl, wkv_ref[...], rkv)
        dkvl_ref[...] = dx
        dwkv_ref[...] += dw
        dqnw_ref[...] += jnp.concatenate([dqnw_n, dqnw_p], axis=-1)
        dknw_ref[...] += jnp.concatenate([dknw_n, dknw_p], axis=-1)

    full = lambda a: pl.BlockSpec(a.shape, lambda i: (0,) * a.ndim)
    rows = lambda n: pl.BlockSpec((tm, n), lambda i: (i, 0))
    const = lambda n: pl.BlockSpec((1, n), lambda i: (0, 0))
    NQ, NKV = w_uq_p.shape[1], w_ukv.shape[1]
    return pl.pallas_call(
        body, grid=(T // tm,), name="mla_pre_bwd",
        in_specs=[pl.BlockSpec((tm, 256), lambda i: (i, P_QLAT // 256)),
                  pl.BlockSpec((tm, 256), lambda i: (i, P_KVLAT // 256)),
                  pl.BlockSpec((tm, 128), lambda i: (i, P_KPE // 128)),
                  rows(ROPE), rows(ROPE),
                  full(w_qln), full(w_kvln), full(w_uq_p), full(w_ukv), full(qnw), full(knw),
                  pl.BlockSpec((H, tm, QK_DIM), lambda i: (0, i, 0)),
                  pl.BlockSpec((H, tm, QK_DIM), lambda i: (0, i, 0)),
                  pl.BlockSpec((H, tm, V_DIM), lambda i: (0, i, 0))],
        out_specs=[rows(Q_LORA), rows(KV_LORA), rows(LANES), rows(NQ), rows(NKV), rows(Q_LORA), rows(KV_LORA),
                   const(Q_LORA), const(KV_LORA), const(QK_DIM), const(QK_DIM)],
        out_shape=[SDS((T, Q_LORA), F32), SDS((T, KV_LORA), F32), SDS((T, LANES), F32),
                   SDS((T, NQ), MXU_DTYPE), SDS((T, NKV), MXU_DTYPE),
                   SDS((T, Q_LORA), MXU_DTYPE), SDS((T, KV_LORA), MXU_DTYPE),
                   SDS((1, Q_LORA), F32), SDS((1, KV_LORA), F32), SDS((1, QK_DIM), F32), SDS((1, QK_DIM), F32)],
        compiler_params=_params(("arbitrary",)),
    )(proj, proj, proj, cosf, sinf, w_qln, w_kvln, w_uq_p, w_ukv, qnw, knw, dq4, dk4, dv4)


def _in_proj_bwd(dgqkv, dgz, dql, dkvl, dkpe, dgab, w_in_p, dh, x2, w_an):
    T, D = x2.shape
    N = w_in_p.shape[1]
    tm = min(512, T)

    def body(a_ref, b_ref, c_ref, d_ref, e_ref, f_ref, w_ref, dh_ref, x_ref, wn_ref, dx_ref, dp_ref, dwn_ref):
        @pl.when(pl.program_id(0) == 0)
        def _():
            dwn_ref[...] = jnp.zeros_like(dwn_ref)

        dp = jnp.concatenate([a_ref[...], b_ref[...], c_ref[...], d_ref[...], e_ref[...], f_ref[...]],
                             axis=-1).astype(MXU_DTYPE)
        dp_ref[...] = dp
        x = x_ref[...]
        _, r = _rms(x, wn_ref[...])
        dx, dw = _rms_bwd(_mm_nt(dp, w_ref[...]), x, wn_ref[...], r)
        dx_ref[...] = dh_ref[...] + dx
        dwn_ref[...] += dw

    rows = lambda n: pl.BlockSpec((tm, n), lambda i: (i, 0))
    return pl.pallas_call(
        body, grid=(T // tm,), name="in_proj_bwd",
        in_specs=[rows(dgqkv.shape[1]), rows(dgz.shape[1]), rows(dql.shape[1]), rows(dkvl.shape[1]),
                  rows(dkpe.shape[1]), rows(dgab.shape[1]),
                  pl.BlockSpec((D, N), lambda i: (0, 0)), rows(D), rows(D), pl.BlockSpec((1, D), lambda i: (0, 0))],
        out_specs=[rows(D), rows(N), pl.BlockSpec((1, D), lambda i: (0, 0))],
        out_shape=[SDS((T, D), F32), SDS((T, N), MXU_DTYPE), SDS((1, D), F32)],
        compiler_params=_params(("arbitrary",)),
    )(dgqkv, dgz, dql, dkvl, dkpe, dgab, w_in_p, dh, x2, w_an)


def _wgrad(a, b, name, column_shards=False):
    T, M = a.shape
    N = b.shape[1]
    tM = _divisor_tile(M, 1024)
    tN = N // N_DEV if column_shards else _divisor_tile(N, 1536)
    tk = min(T, 2048)
    nk = T // tk

    def body(a_ref, b_ref, o_ref, acc):
        k = pl.program_id(2)

        @pl.when(k == 0)
        def _():
            acc[...] = jnp.zeros_like(acc)

        acc[...] += _mm_tn(a_ref[...], b_ref[...])

        @pl.when(k == nk - 1)
        def _():
            o_ref[...] = acc[...].astype(WIRE_DTYPE).reshape(o_ref.shape)

    if column_shards:
        out_spec, out_shape = pl.BlockSpec((1, tM, tN), lambda i, j, k: (j, i, 0)), SDS((N_DEV, M, tN), WIRE_DTYPE)
    else:
        out_spec, out_shape = pl.BlockSpec((tM, tN), lambda i, j, k: (i, j)), SDS((M, N), WIRE_DTYPE)
    return pl.pallas_call(
        body, grid=(M // tM, N // tN, nk), name=name,
        in_specs=[pl.BlockSpec((tk, tM), lambda i, j, k: (k, i)), pl.BlockSpec((tk, tN), lambda i, j, k: (k, j))],
        out_specs=out_spec, out_shape=out_shape,
        scratch_shapes=[pltpu.VMEM((tM, tN), F32)],
        compiler_params=_params(("arbitrary", "arbitrary", "arbitrary")),
    )(a, b)


def _adamw(g, w, m, v):
    m = ADAM_B1 * m + (1.0 - ADAM_B1) * g
    v = ADAM_B2 * v + (1.0 - ADAM_B2) * jnp.square(g)
    m_hat = m / (1.0 - ADAM_B1 ** ADAM_STEP)
    v_hat = v / (1.0 - ADAM_B2 ** ADAM_STEP)
    return -ADAM_LR * (m_hat / (jnp.sqrt(v_hat) + ADAM_EPS) + ADAM_WD * w), m, v


def _reduce_adamw(parts, w, m, v, name):
    R, C = w.shape
    _, Rp, Cp = parts.shape
    tr = min(R, 256)
    tp = tr if Rp == R else Rp

    def body(p_ref, w_ref, m_ref, v_ref, g_ref, d_ref, nm_ref, nv_ref):
        g = p_ref[0].astype(F32)
        for s in range(1, N_DEV):
            g = g + p_ref[s].astype(F32)
        g = g[:tr, :C]
        g_ref[...] = g
        d_ref[...], nm_ref[...], nv_ref[...] = _adamw(g, w_ref[...], m_ref[...], v_ref[...])

    spec = pl.BlockSpec((tr, C), lambda i: (i, 0))
    return pl.pallas_call(
        body, grid=(R // tr,), name=name,
        in_specs=[pl.BlockSpec((N_DEV, tp, Cp), lambda i: (0, i, 0)), spec, spec, spec],
        out_specs=[spec] * 4, out_shape=[SDS((R, C), F32)] * 4,
        compiler_params=_params(("arbitrary",)),
    )(parts, w, m, v)


SMALL_ROWS, SMALL_COLS = 16, 1024
SMALL_LAYOUT = (
    ("attn_norm_w", 0, 1, 1024, 1024), ("mlp_norm_w", 1, 1, 1024, 1024), ("q_lat_norm_w", 2, 1, 256, 256),
    ("kv_lat_norm_w", 3, 1, 256, 256), ("q_norm_w", 4, 1, 192, 192), ("k_norm_w", 5, 1, 192, 192),
    ("mla_out_norm_w", 6, 4, 128, 128), ("a_log", 10, 1, 128, 4), ("dt_bias", 11, 1, 128, 4),
    ("gdn_norm_w", 12, 1, 128, 128))
LOSS_ENTRY = ("loss", 13, 1, 128, 128)


def _adamw_replicated(parts, ws, ms, vs):
    n = len(SMALL_LAYOUT)

    def body(*refs):
        p_ref = refs[0]
        w_refs, m_refs, v_refs = refs[1:1 + n], refs[1 + n:1 + 2 * n], refs[1 + 2 * n:1 + 3 * n]
        outs = refs[1 + 3 * n:]
        s = p_ref[0]
        for d in range(1, N_DEV):
            s = s + p_ref[d]
        for i, (_, r0, nr, _, pw) in enumerate(SMALL_LAYOUT):
            g = s[r0:r0 + nr, :pw]
            outs[i][...] = g
            outs[n + i][...], outs[2 * n + i][...], outs[3 * n + i][...] = _adamw(
                g, w_refs[i][...], m_refs[i][...], v_refs[i][...])
        _, r0, nr, gw, _ = LOSS_ENTRY
        outs[4 * n][...] = s[r0:r0 + nr, :gw]

    res = pl.pallas_call(
        body, name="adamw_replicated",
        out_shape=[SDS(w.shape, F32) for w in ws] * 4 + [SDS((1, LANES), F32)],
        compiler_params=_params(),
    )(parts, *ws, *ms, *vs)
    return [res[k * n:(k + 1) * n] for k in range(4)], res[4 * n][0, 0]


COPIES_PER_ARRAY = N_DEV - 1


def _two_level_gather(srcs, outs, send_sems, recv_sems, local_sems=None, stage="all"):
    mx, my, mc = lax.axis_index("x"), lax.axis_index("y"), lax.axis_index("c")
    me, sibling = (mx, my, mc), (mx, my, 1 - mc)
    chips = [(1 - mx, my), (mx, 1 - my), (1 - mx, 1 - my)]
    arrays = range(len(srcs))

    def copy(a, k, block, to, src=None):
        px, py, pc = block
        slot = outs[a].at[4 * px + 2 * py + pc]
        sem = a * COPIES_PER_ARRAY + k
        return pltpu.make_async_remote_copy(
            src_ref=slot if src is None else src, dst_ref=slot,
            send_sem=send_sems.at[sem], recv_sem=recv_sems.at[sem], device_id=to, device_id_type=MESH_ID)

    mine = [] if local_sems is None else [
        pltpu.make_async_copy(srcs[a], outs[a].at[4 * mx + 2 * my + mc], local_sems.at[a]) for a in arrays]
    first = []
    for a in arrays:
        first.append(copy(a, 0, me, sibling, src=srcs[a]))
        first += [copy(a, 1 + j, me, (*chip, mc), src=srcs[a]) for j, chip in enumerate(chips)]
    if stage in ("all", "start"):
        for cp in mine + first:
            cp.start()
    if stage in ("all", "finish"):
        forwards = []
        for j, chip in enumerate(chips):
            for a in arrays:
                copy(a, 1 + j, (*chip, mc), me).wait_recv()
                fwd = copy(a, 4 + j, (*chip, mc), sibling)
                fwd.start()
                forwards.append(fwd)
        for a in arrays:
            copy(a, 0, sibling, me).wait_recv()
        for j, chip in enumerate(chips):
            for a in arrays:
                copy(a, 4 + j, (*chip, 1 - mc), me).wait_recv()
        for cp in first + forwards:
            cp.wait_send()
        for cp in mine:
            cp.wait()


def _comm_scratch(n):
    return [pltpu.SemaphoreType.DMA((n * COPIES_PER_ARRAY,)), pltpu.SemaphoreType.DMA((n * COPIES_PER_ARRAY,)),
            pltpu.SemaphoreType.DMA((n,))]


def _any_specs(n):
    return [pl.BlockSpec(memory_space=pl.ANY)] * n


def _gather_weights(shards):
    n = len(shards)

    def body(*refs):
        _two_level_gather(refs[:n], refs[n:2 * n], *refs[2 * n:])

    return pl.pallas_call(
        body, name="gather_weights",
        out_shape=[SDS((N_DEV,) + s.shape, s.dtype) for s in shards],
        in_specs=_any_specs(n), out_specs=_any_specs(n), scratch_shapes=_comm_scratch(n),
    )(*shards)


def _gather_small_grads(gs, loss_lanes):
    gs = list(gs) + [loss_lanes]
    n = len(gs)

    def body(*refs):
        g_refs, out_ref = refs[:n], refs[n]
        tile, send_sems, recv_sems = refs[n + 1:]
        tile[...] = jnp.zeros_like(tile)
        for (_, r0, nr, gw, _), g in zip(SMALL_LAYOUT + (LOSS_ENTRY,), g_refs):
            tile[r0:r0 + nr, 0:gw] = g[...]
        me = 4 * lax.axis_index("x") + 2 * lax.axis_index("y") + lax.axis_index("c")
        out_ref[me] = tile[...]
        _two_level_gather([tile], [out_ref], send_sems, recv_sems)

    return pl.pallas_call(
        body, name="gather_small_grads",
        out_shape=SDS((N_DEV, SMALL_ROWS, SMALL_COLS), F32),
        in_specs=[pl.BlockSpec(memory_space=pltpu.VMEM)] * n,
        out_specs=pl.BlockSpec(memory_space=pltpu.VMEM),
        scratch_shapes=[pltpu.VMEM((SMALL_ROWS, SMALL_COLS), F32),
                        pltpu.SemaphoreType.DMA((COPIES_PER_ARRAY,)), pltpu.SemaphoreType.DMA((COPIES_PER_ARRAY,))],
    )(*gs)


def _exchange_grads(slabs):
    n = len(slabs)

    def body(*refs):
        _exchange(refs[:n], refs[n:2 * n], *refs[2 * n:])

    return pl.pallas_call(
        body, name="exchange_grads",
        out_shape=[SDS(s.shape, s.dtype) for s in slabs],
        in_specs=_any_specs(n), out_specs=_any_specs(n), scratch_shapes=_comm_scratch(n),
    )(*slabs)


class _Transfer:
    def __init__(self, kind, arrays):
        self.kind, self.arrays, self.n = kind, list(arrays), len(arrays)

    def out_shapes(self):
        if self.kind == "gather":
            return [SDS((N_DEV,) + a.shape, a.dtype) for a in self.arrays]
        return [SDS(a.shape, a.dtype) for a in self.arrays]

    def run(self, srcs, outs, sems, stage):
        fn = _two_level_gather if self.kind == "gather" else _exchange
        fn(srcs, outs, *sems, stage=stage)


def _call_beside(body, transfer, *, grid, in_specs, out_specs, out_shape, scratch_shapes, name, semantics, args):
    if transfer is None:
        res = pl.pallas_call(body, grid=grid, in_specs=in_specs, out_specs=out_specs, out_shape=out_shape,
                             scratch_shapes=scratch_shapes, name=name, compiler_params=_params(semantics))(*args)
        return list(res), []
    n_in, n_out, n_s, n = len(in_specs), len(out_specs), len(scratch_shapes), transfer.n

    def wrapped(*refs):
        ins, refs = refs[:n_in], refs[n_in:]
        t_in, refs = refs[:n], refs[n:]
        outs, refs = refs[:n_out], refs[n_out:]
        t_out, refs = refs[:n], refs[n:]
        scratch, sems = refs[:n_s], refs[n_s:]
        first = functools.reduce(jnp.logical_and, [pl.program_id(i) == 0 for i in range(len(grid))])
        last = functools.reduce(jnp.logical_and, [pl.program_id(i) == g - 1 for i, g in enumerate(grid)])

        @pl.when(first)
        def _():
            transfer.run(t_in, t_out, sems, "start")

        body(*ins, *outs, *scratch)

        @pl.when(last)
        def _():
            transfer.run(t_in, t_out, sems, "finish")

    res = pl.pallas_call(
        wrapped, grid=grid, in_specs=list(in_specs) + _any_specs(n), out_specs=list(out_specs) + _any_specs(n),
        out_shape=list(out_shape) + transfer.out_shapes(), scratch_shapes=list(scratch_shapes) + _comm_scratch(n),
        name=name, compiler_params=_params(semantics))(*args, *transfer.arrays)
    return list(res[:n_out]), list(res[n_out:])


EXCHANGE_FLIPS = ((0, 0, 1), (1, 0, 0), (0, 1, 0), (1, 1, 0), (1, 0, 1), (0, 1, 1), (1, 1, 1))


def _exchange(srcs, outs, send_sems, recv_sems, local_sems, stage="all"):
    mx, my, mc = lax.axis_index("x"), lax.axis_index("y"), lax.axis_index("c")
    arrays = range(len(srcs))
    copies = [pltpu.make_async_copy(srcs[a].at[4 * mx + 2 * my + mc], outs[a].at[N_DEV - 1], local_sems.at[a])
              for a in arrays]
    for k, (fx, fy, fc) in enumerate(EXCHANGE_FLIPS):
        px = 1 - mx if fx else mx
        py = 1 - my if fy else my
        pc = 1 - mc if fc else mc
        for a in arrays:
            sem = a * COPIES_PER_ARRAY + k
            copies.append(pltpu.make_async_remote_copy(
                src_ref=srcs[a].at[4 * px + 2 * py + pc], dst_ref=outs[a].at[k],
                send_sem=send_sems.at[sem], recv_sem=recv_sems.at[sem],
                device_id=(px, py, pc), device_id_type=MESH_ID))
    if stage in ("all", "start"):
        for cp in copies:
            cp.start()
    if stage in ("all", "finish"):
        for cp in copies:
            cp.wait()


def _w_in_to_padded(w):
    z = lambda n: jnp.zeros((w.shape[0], n), w.dtype)
    return jnp.concatenate([w[:, O_GQKV:O_GZ], w[:, O_GZ:O_GAB], w[:, O_QLAT:O_KVLAT], w[:, O_KVLAT:O_KPE],
                            w[:, O_KPE:O_GQKV], z(P_GAB - P_KPE - ROPE), w[:, O_GAB:O_END],
                            z(P_WIDTH - P_GAB - (O_END - O_GAB))], axis=1)


def _w_in_from_padded(wp):
    return jnp.concatenate([wp[:, P_QLAT:P_QLAT + 256], wp[:, P_KVLAT:P_KVLAT + 256], wp[:, P_KPE:P_KPE + ROPE],
                            wp[:, P_GQKV:P_GZ], wp[:, P_GZ:P_QLAT], wp[:, P_GAB:P_GAB + (O_END - O_GAB)]], axis=1)


def _w_uq_to_headsplit(w):
    w3 = w.reshape(w.shape[0], MLA_HEADS, QK_DIM)
    return jnp.concatenate([w3[:, :, :NOPE].reshape(w.shape[0], -1), w3[:, :, NOPE:].reshape(w.shape[0], -1)], axis=1)


def _w_uq_from_headsplit(wp):
    n = wp[:, :MLA_HEADS * NOPE].reshape(wp.shape[0], MLA_HEADS, NOPE)
    p = wp[:, MLA_HEADS * NOPE:].reshape(wp.shape[0], MLA_HEADS, ROPE)
    return jnp.concatenate([n, p], axis=2).reshape(wp.shape[0], -1)


def _lane_vec(v4):
    return jnp.pad(v4.reshape(1, -1), ((0, 0), (0, LANES - v4.shape[-1])))


def _local_step(x, positions, target, attn_norm_w, w_in, q_lat_norm_w, w_uq, kv_lat_norm_w, w_ukv, q_norm_w,
                k_norm_w, mla_out_norm_w, conv_w, a_log, dt_bias, gdn_norm_w, w_out, mlp_norm_w, w_up, w_down,
                late_shards=None, exchange=False):
    B, S, D = x.shape
    T = B * S
    x2 = x.reshape(T, D)
    t2 = target.reshape(T, D)
    half = ROPE // 2
    inv_freq = ROPE_THETA ** (-jnp.arange(half, dtype=F32) / half)
    ang = positions.reshape(T, 1).astype(F32) * inv_freq
    cosf = jnp.concatenate([jnp.cos(ang)] * 2, axis=-1)
    sinf = jnp.concatenate([jnp.sin(ang)] * 2, axis=-1)
    w_in_p = _w_in_to_padded(w_in)
    w_uq_p = _w_uq_to_headsplit(w_uq)
    alog_l, dt_l = _lane_vec(a_log), _lane_vec(dt_bias)
    w_an, w_qln, w_kvln, qnw, knw, w_mn, gdn_w = (
        attn_norm_w, q_lat_norm_w, kv_lat_norm_w, q_norm_w, k_norm_w, mlp_norm_w, gdn_norm_w)

    proj, xn = _in_proj(x2, w_an, w_in_p)
    q4, k4, v4 = _mla_pre(proj, cosf, sinf, w_qln, w_kvln, w_uq_p, w_ukv, qnw, knw)
    gather = None if late_shards is None else _Transfer("gather", late_shards[:1])
    (o_mla, lse), late = _attn_fwd(q4, k4, v4, B, S, gather)
    if late:
        w_out = late[0].reshape(-1, D)
    qg, kg, vg, gates = _gdn_pre(proj, conv_w, alog_l, dt_l, S)
    gather = None if late_shards is None else _Transfer("gather", late_shards[1:])
    (o_gdn, states, ainv, u4, w4), late = _gdn_fwd(qg, kg, vg, gates, B, S, gather)
    if late:
        w_up, w_down = late[0], late[1].reshape(-1, D)
    h2, mix = _mix_out(o_mla, o_gdn, proj, x2, mla_out_norm_w, gdn_w, w_out)
    up, hn, dy, sq = _mlp_fwd(h2, w_mn, w_up, w_down, t2)
    loss = (0.5 / D) * jnp.sum(sq[:, 0, 0])

    dh, dhb, dup, act, dyb, d_mlp_norm = _mlp_bwd(dy, up, h2, w_mn, w_up, w_down)
    g_w_down = _wgrad(act, dyb, "wgrad_down")
    g_w_up = _wgrad(hn, dup, "wgrad_up", column_shards=True)
    do_mla, do_gdn, dz, d_mla_w, d_gdn_w = _mix_bwd(dhb, o_mla, o_gdn, proj, mla_out_norm_w, gdn_w, w_out)
    g_w_out = _wgrad(mix, dhb, "wgrad_out")
    first = ("w_down",)
    second = ("w_up", "w_out", "w_uq", "w_ukv")
    mats = dict(w_up=g_w_up, w_down=g_w_down, w_out=g_w_out)
    send = _Transfer("exchange", [_slabs(n, mats[n]) for n in first]) if exchange else None
    (dq4, dk4, dv4), got = _attn_bwd(q4, k4, v4, do_mla, o_mla, lse, B, S, send)
    mats.update(zip(first, got))
    dql, dkvl, dkpe, dqraw, dkvraw, qn, kvn, d_wqln, d_wkvln, d_qnw, d_knw = _mla_pre_bwd(
        proj, cosf, sinf, w_qln, w_kvln, w_uq_p, w_ukv, qnw, knw, dq4, dk4, dv4)
    mats.update(w_uq=_wgrad(qn, dqraw, "wgrad_uq"), w_ukv=_wgrad(kvn, dkvraw, "wgrad_ukv"))
    send = _Transfer("exchange", [_slabs(n, mats[n]) for n in second]) if exchange else None
    (dqg, dkg, dvg, dgb4), got = _gdn_bwd(qg, kg, vg, gates, states, ainv, u4, w4, do_gdn, B, S, send)
    mats.update(zip(second, got))
    dc, dgab, g_conv, d_alog, d_dt = _gdn_pre_bwd(proj, conv_w, alog_l, dt_l, dqg, dkg, dvg, dgb4, S)
    dgqkv = _conv_bwd_input(dc, conv_w, S)
    grad_x2, dproj, d_attn_norm = _in_proj_bwd(dgqkv, dz, dql, dkvl, dkpe, dgab, w_in_p, dh, x2, w_an)
    mats.update(w_in=_wgrad(xn, dproj, "wgrad_in"), conv_w=g_conv)
    if exchange:
        last = ("w_in", "conv_w")
        mats.update(zip(last, _exchange_grads([_slabs(n, mats[n]) for n in last])))
    small = dict(attn_norm_w=d_attn_norm, mlp_norm_w=d_mlp_norm, q_lat_norm_w=d_wqln, kv_lat_norm_w=d_wkvln,
                 q_norm_w=d_qnw, k_norm_w=d_knw, mla_out_norm_w=d_mla_w, a_log=d_alog, dt_bias=d_dt,
                 gdn_norm_w=d_gdn_w)
    return loss, grad_x2.reshape(B, S, D), mats, [small[n] for n, *_ in SMALL_LAYOUT]


BIG = ("w_in", "w_uq", "w_ukv", "conv_w", "w_out", "w_up", "w_down")
ALL_W = ("attn_norm_w", "w_in", "q_lat_norm_w", "w_uq", "kv_lat_norm_w", "w_ukv", "q_norm_w", "k_norm_w",
         "mla_out_norm_w", "conv_w", "a_log", "dt_bias", "gdn_norm_w", "w_out", "mlp_norm_w", "w_up", "w_down")
WIRE_SHAPE = {"w_in": (1024, 384), "w_uq": (256, 128), "conv_w": (16, 256)}


def _pad2(a, rows, cols):
    return jnp.pad(a, [(0, 0)] * (a.ndim - 2) + [(0, rows - a.shape[-2]), (0, cols - a.shape[-1])])


def _cols_to_full(stack, cols):
    return jnp.moveaxis(stack[:, :, :cols], 0, 1).reshape(stack.shape[1], N_DEV * cols)


def _full_to_cols(full, wire_cols):
    r, n = full.shape
    return _pad2(jnp.moveaxis(full.reshape(r, N_DEV, n // N_DEV), 1, 0), r, wire_cols)


def _slabs(name, g):
    if name == "w_in":
        return _full_to_cols(_w_in_from_padded(g), WIRE_SHAPE["w_in"][1])
    if name == "w_uq":
        return _full_to_cols(_w_uq_from_headsplit(g), WIRE_SHAPE["w_uq"][1])
    if name == "w_ukv":
        return _full_to_cols(g, g.shape[1] // N_DEV)
    if name == "conv_w":
        return _pad2(_full_to_cols(g.astype(WIRE_DTYPE), g.shape[1] // N_DEV), *WIRE_SHAPE["conv_w"])
    if name == "w_up":
        return g
    return g.reshape(N_DEV, -1, g.shape[-1])


def kernel(x, positions, attn_norm_w, w_in, q_lat_norm_w, w_uq, kv_lat_norm_w, w_ukv, q_norm_w, k_norm_w, mla_out_norm_w, conv_w, a_log, dt_bias, gdn_norm_w, w_out, mlp_norm_w, w_up, w_down, loss_target, m_attn_norm_w, m_w_in, m_q_lat_norm_w, m_w_uq, m_kv_lat_norm_w, m_w_ukv, m_q_norm_w, m_k_norm_w, m_mla_out_norm_w, m_conv_w, m_a_log, m_dt_bias, m_gdn_norm_w, m_w_out, m_mlp_norm_w, m_w_up, m_w_down, v_attn_norm_w, v_w_in, v_q_lat_norm_w, v_w_uq, v_kv_lat_norm_w, v_w_ukv, v_q_norm_w, v_k_norm_w, v_mla_out_norm_w, v_conv_w, v_a_log, v_dt_bias, v_gdn_norm_w, v_w_out, v_mlp_norm_w, v_w_up, v_w_down):
    env = dict(locals())
    W = {n: env[n][0] for n in ALL_W}
    Mo = {n: env["m_" + n][0] for n in ALL_W}
    Vo = {n: env["v_" + n][0] for n in ALL_W}

    two_d = lambda a: a.reshape(1, -1) if a.ndim == 1 else a
    D = x.shape[-1]

    s_in, s_uq, s_ukv, s_conv = _gather_weights([
        _pad2(W["w_in"].astype(WIRE_DTYPE), *WIRE_SHAPE["w_in"]),
        _pad2(W["w_uq"].astype(WIRE_DTYPE), *WIRE_SHAPE["w_uq"]),
        W["w_ukv"].astype(WIRE_DTYPE), _pad2(W["conv_w"], *WIRE_SHAPE["conv_w"])])
    late = [W["w_out"].astype(WIRE_DTYPE), W["w_up"].astype(WIRE_DTYPE), W["w_down"].astype(WIRE_DTYPE)]

    loss, grad_x, parts, gs = _local_step(
        x, positions, loss_target, two_d(W["attn_norm_w"]), _cols_to_full(s_in, W["w_in"].shape[1]),
        two_d(W["q_lat_norm_w"]), _cols_to_full(s_uq, W["w_uq"].shape[1]), two_d(W["kv_lat_norm_w"]),
        _cols_to_full(s_ukv, W["w_ukv"].shape[1]), two_d(W["q_norm_w"]), two_d(W["k_norm_w"]),
        W["mla_out_norm_w"], _cols_to_full(s_conv[:, :CONV_W], W["conv_w"].shape[1]), two_d(W["a_log"]),
        two_d(W["dt_bias"]), two_d(W["gdn_norm_w"]), None, two_d(W["mlp_norm_w"]), None, None,
        late_shards=late, exchange=True)
    done = {n: _reduce_adamw(parts[n], W[n], Mo[n], Vo[n], "adamw_" + n) for n in BIG}
    names = [n for n, *_ in SMALL_LAYOUT]
    tiles = _gather_small_grads(gs, jnp.full((1, LANES), loss, F32))
    small, loss = _adamw_replicated(tiles, [two_d(W[n]) for n in names], [two_d(Mo[n]) for n in names],
                                    [two_d(Vo[n]) for n in names])
    for i, n in enumerate(names):
        done[n] = [small[kind][i] for kind in range(4)]
    res = [done[n][kind].reshape(env[n].shape) for kind in range(4) for n in ALL_W]
    return (loss, grad_x, *res)
```

```python
import functools

import jax
import jax.numpy as jnp
from jax import lax
from jax.experimental import pallas as pl
from jax.experimental.pallas import tpu as pltpu

F32 = jnp.float32
MXU_DTYPE = jnp.bfloat16
WIRE_DTYPE = jnp.bfloat16
SDS = jax.ShapeDtypeStruct
HIGHEST = lax.Precision.HIGHEST
MESH_ID = pl.DeviceIdType.MESH

D_MODEL = 1024
MLA_HEADS = 4
Q_LORA = 256
KV_LORA = 256
NOPE = 128
ROPE = 64
QK_DIM = NOPE + ROPE
V_DIM = 128
ROPE_THETA = 10000.0
GDN_HEADS = 4
GDN_DIM = 128
GDN_WIDTH = GDN_HEADS * GDN_DIM
CONV_W = 4
CHUNK = 64
D_FF = 4 * D_MODEL
EPS = 1e-6
ATT_SCALE = QK_DIM ** -0.5
GDN_QSCALE = GDN_DIM ** -0.5
N_DEV = 8
ATTN_BLOCK = 512
ATTN_CHAINS = 2
MLP_FWD_SHARDS = 4
MLP_BWD_SHARDS = 2

ADAM_LR = 0.001
ADAM_B1 = 0.9
ADAM_B2 = 0.999
ADAM_EPS = 1e-08
ADAM_WD = 0.01
ADAM_STEP = 10

LANES = 128
SUBLANES = 8
VMEM_LIMIT = 56 * 1024 * 1024

P_GQKV, P_GZ, P_QLAT, P_KVLAT, P_KPE, P_GAB = 0, 1536, 2048, 2304, 2560, 2688
P_WIDTH = 2816
O_QLAT, O_KVLAT, O_KPE, O_GQKV, O_GZ, O_GAB, O_END = 0, 256, 512, 576, 2112, 2624, 2632


def _params(sem=None, vmem=VMEM_LIMIT):
    kw = dict(vmem_limit_bytes=vmem)
    if sem is not None:
        kw["dimension_semantics"] = sem
    return pltpu.CompilerParams(**kw)


def _mm(a, b):
    return jnp.dot(a.astype(MXU_DTYPE), b.astype(MXU_DTYPE), preferred_element_type=F32)


def _mm_nt(a, b):
    return lax.dot_general(a.astype(MXU_DTYPE), b.astype(MXU_DTYPE), (((1,), (1,)), ((), ())),
                           preferred_element_type=F32)


def _mm_tn(a, b):
    return lax.dot_general(a.astype(MXU_DTYPE), b.astype(MXU_DTYPE), (((0,), (0,)), ((), ())),
                           preferred_element_type=F32)


def _split(a):
    hi = a.astype(MXU_DTYPE)
    return hi, (a - hi.astype(F32)).astype(MXU_DTYPE)


def _mm_split(a, b):
    (ah, al), (bh, bl) = a, b
    dot = lambda x, y: jnp.dot(x, y, preferred_element_type=F32)
    if MXU_DTYPE == F32:
        return dot(ah, bh)
    return dot(ah, bh) + dot(ah, bl) + dot(al, bh)


def _mm_exact(a, b):
    return _mm_split(_split(a), _split(b))


def _row_sum(v, on_mxu=False):
    if not on_mxu:
        return jnp.sum(v, axis=-1, keepdims=True)
    d = v.shape[-1]
    ones = jnp.ones((d, LANES), MXU_DTYPE)
    s = sum(jnp.dot(p, ones, preferred_element_type=F32) for p in _split(v))
    return s[:, :d] if d <= LANES else jnp.tile(s, (1, d // LANES))


def _rms(x, w, on_mxu=False):
    r = lax.rsqrt(_row_sum(x * x, on_mxu) * (1.0 / x.shape[-1]) + EPS)
    return x * r * w, r


def _rms_bwd(dy, x, w, r, on_mxu=False):
    xh = x * r
    dyw = dy * w
    dx = r * (dyw - xh * (_row_sum(dyw * xh, on_mxu) * (1.0 / x.shape[-1])))
    dw = jnp.sum(dy * xh, axis=0, keepdims=True)
    return dx, dw


def _l2n(x, scale):
    return x * (lax.rsqrt(_row_sum(x * x) + EPS) * scale)


def _l2n_bwd(dy, x, scale):
    r = lax.rsqrt(_row_sum(x * x) + EPS)
    xh = x * r
    return (scale * r) * (dy - xh * _row_sum(dy * xh))


def _rot(t):
    return jnp.concatenate([-t[:, ROPE // 2:], t[:, :ROPE // 2]], axis=-1)


def _rot_t(t):
    return jnp.concatenate([t[:, ROPE // 2:], -t[:, :ROPE // 2]], axis=-1)


def _rope(t, cos, sin):
    return t * cos + _rot(t) * sin


def _rope_bwd(d, cos, sin):
    return d * cos + _rot_t(d * sin)


def _sigmoid(x):
    return jax.nn.sigmoid(x)


def _shift_down(x, halo, j):
    if j == 0:
        return x
    xr = pltpu.roll(x, j, 0)
    hr = pltpu.roll(halo, j, 0)
    row = lax.broadcasted_iota(jnp.int32, halo.shape, 0)
    top = jnp.where(row < j, hr, xr[:SUBLANES])
    return jnp.concatenate([top, xr[SUBLANES:]], axis=0)


def _shift_up(x, nxt, j):
    if j == 0:
        return x
    n = x.shape[0]
    xr = pltpu.roll(x, n - j, 0)
    nr = pltpu.roll(nxt, SUBLANES - j, 0)
    row = lax.broadcasted_iota(jnp.int32, nxt.shape, 0)
    bot = jnp.where(row >= SUBLANES - j, nr, xr[n - SUBLANES:])
    return jnp.concatenate([xr[:n - SUBLANES], bot], axis=0)


def _chunk_cumsum(y, row_in_chunk):
    s = 1
    while s < CHUNK:
        y = y + jnp.where(row_in_chunk >= s, pltpu.roll(y, s, 0), 0.0)
        s *= 2
    return y


def _chunk_rev_cumsum(y, row_in_chunk):
    n = y.shape[0]
    s = 1
    while s < CHUNK:
        y = y + jnp.where(row_in_chunk + s < CHUNK, pltpu.roll(y, n - s, 0), 0.0)
        s *= 2
    return y


def _together(generators):
    alive = list(generators)
    while alive:
        nxt = []
        for g in alive:
            try:
                next(g)
                nxt.append(g)
            except StopIteration:
                pass
        alive = nxt
        yield


def _lockstep(generators):
    for _ in _together(generators):
        pass


def _pick_lane(tile, lane, idx):
    return jnp.sum(jnp.where(lane == idx, tile, 0.0), axis=-1, keepdims=True)


def _divisor_tile(n, cap, unit=LANES):
    best = unit
    t = unit
    while t <= min(n, cap):
        if n % t == 0:
            best = t
        t += unit
    return n if n <= cap else best


def _in_proj(x2, w_an, w_in_p):
    T, D = x2.shape
    N = w_in_p.shape[1]
    tm = min(512, T)

    def body(x_ref, wn_ref, w_ref, proj_ref, xn_ref):
        xn, _ = _rms(x_ref[...], wn_ref[...])
        xn = xn.astype(MXU_DTYPE)
        xn_ref[...] = xn
        proj_ref[...] = jnp.dot(xn, w_ref[...], preferred_element_type=F32)

    return pl.pallas_call(
        body, grid=(T // tm,), name="in_proj",
        in_specs=[pl.BlockSpec((tm, D), lambda i: (i, 0)), pl.BlockSpec((1, D), lambda i: (0, 0)),
                  pl.BlockSpec((D, N), lambda i: (0, 0))],
        out_specs=[pl.BlockSpec((tm, N), lambda i: (i, 0)), pl.BlockSpec((tm, D), lambda i: (i, 0))],
        out_shape=[SDS((T, N), F32), SDS((T, D), MXU_DTYPE)],
        compiler_params=_params(("arbitrary",)),
    )(x2, w_an, w_in_p)


def _mla_pre(proj, cosf, sinf, w_qln, w_kvln, w_uq_p, w_ukv, qnw, knw, transfer=None):
    T = proj.shape[0]
    tm = min(256, T)
    H = MLA_HEADS

    def body(ql_ref, kvl_ref, kpe_ref, cos_ref, sin_ref, wq_ref, wkv_ref, uq_ref, ukv_ref, qnw_ref, knw_ref,
             q_out, k_out, v_out):
        rms = functools.partial(_rms, on_mxu=True)
        cos, sin = cos_ref[...], sin_ref[...]
        qnw_, knw_ = qnw_ref[...], knw_ref[...]
        qn, _ = rms(ql_ref[...], wq_ref[...])
        kvn, _ = rms(kvl_ref[...], wkv_ref[...])
        qraw = _mm(qn, uq_ref[...])
        kvraw = _mm(kvn, ukv_ref[...])
        kpe = _rope(rms(kpe_ref[...][:, :ROPE], knw_[:, NOPE:])[0], cos, sin)
        for h in range(H):
            qn_h = rms(qraw[:, h * NOPE:(h + 1) * NOPE], qnw_[:, :NOPE])[0]
            qp_h = _rope(rms(qraw[:, H * NOPE + h * ROPE:H * NOPE + (h + 1) * ROPE], qnw_[:, NOPE:])[0], cos, sin)
            q_out[h] = (jnp.concatenate([qn_h, qp_h], axis=-1) * ATT_SCALE).astype(MXU_DTYPE)
            kn_h = rms(kvraw[:, h * 256:h * 256 + NOPE], knw_[:, :NOPE])[0]
            k_out[h] = jnp.concatenate([kn_h, kpe], axis=-1).astype(MXU_DTYPE)
            v_out[h] = kvraw[:, h * 256 + NOPE:(h + 1) * 256].astype(MXU_DTYPE)

    full = lambda a: pl.BlockSpec(a.shape, lambda i: (0,) * a.ndim)
    return _call_beside(
        body, transfer, grid=(T // tm,), name="mla_pre", scratch_shapes=[], semantics=("arbitrary",),
        args=(proj, proj, proj, cosf, sinf, w_qln, w_kvln, w_uq_p, w_ukv, qnw, knw),
        in_specs=[pl.BlockSpec((tm, 256), lambda i: (i, P_QLAT // 256)),
                  pl.BlockSpec((tm, 256), lambda i: (i, P_KVLAT // 256)),
                  pl.BlockSpec((tm, 128), lambda i: (i, P_KPE // 128)),
                  pl.BlockSpec((tm, ROPE), lambda i: (i, 0)), pl.BlockSpec((tm, ROPE), lambda i: (i, 0)),
                  full(w_qln), full(w_kvln), full(w_uq_p), full(w_ukv), full(qnw), full(knw)],
        out_specs=[pl.BlockSpec((H, tm, QK_DIM), lambda i: (0, i, 0)),
                   pl.BlockSpec((H, tm, QK_DIM), lambda i: (0, i, 0)),
                   pl.BlockSpec((H, tm, V_DIM), lambda i: (0, i, 0))],
        out_shape=[SDS((H, T, QK_DIM), MXU_DTYPE), SDS((H, T, QK_DIM), MXU_DTYPE), SDS((H, T, V_DIM), MXU_DTYPE)])


def _attn_fwd(q4, k4, v4, B, S, transfer=None):
    H = MLA_HEADS
    bq = min(ATTN_BLOCK, S)
    nq = S // bq
    rows = bq // ATTN_CHAINS

    def body(q_ref, k_ref, v_ref, o_ref, lse_ref):
        col = lax.broadcasted_iota(jnp.int32, (rows, bq), 1)
        row = lax.broadcasted_iota(jnp.int32, (rows, bq), 0)

        def q_step(qi, carry):
            qs = pl.multiple_of(qi * bq, bq)
            qsub = [q_ref[0, pl.ds(qs + j * rows, rows), :] for j in range(ATTN_CHAINS)]

            def k_block(ks, cs, diagonal):
                k = k_ref[0, pl.ds(ks, bq), :]
                v = v_ref[0, pl.ds(ks, bq), :]
                out = [None] * ATTN_CHAINS

                def chain(j):
                    m, l, acc = cs[j]
                    s = _mm_nt(qsub[j], k)
                    yield
                    if diagonal:
                        s = jnp.where(col <= row + j * rows, s, -jnp.inf)
                    m_new = jnp.maximum(m, jnp.max(s, axis=-1, keepdims=True))
                    p = jnp.exp(s - m_new)
                    a = jnp.exp(m - m_new)
                    l_new = a * l + jnp.sum(p, axis=-1, keepdims=True)
                    yield
                    out[j] = (m_new, l_new, a * acc + _mm(p, v))

                _lockstep([chain(j) for j in range(ATTN_CHAINS)])
                return tuple(out)

            init = tuple((jnp.full((rows, 1), -jnp.inf, F32), jnp.zeros((rows, 1), F32),
                          jnp.zeros((rows, V_DIM), F32)) for _ in range(ATTN_CHAINS))
            cs = lax.fori_loop(0, qi, lambda kj, c: k_block(pl.multiple_of(kj * bq, bq), c, False), init)
            for j, (m, l, acc) in enumerate(k_block(qs, cs, True)):
                o_ref[0, pl.ds(qs + j * rows, rows), :] = acc / l
                lse_ref[0, pl.ds(qs + j * rows, rows), :] = m + jnp.log(l)
            return carry

        lax.fori_loop(0, nq, q_step, 0)

    spec = lambda d: pl.BlockSpec((1, S, d), lambda h, b: (h, b, 0))
    return _call_beside(
        body, transfer, grid=(H, B), name="attn_fwd",
        in_specs=[spec(QK_DIM), spec(QK_DIM), spec(V_DIM)],
        out_specs=[spec(V_DIM), spec(1)],
        out_shape=[SDS((H, B * S, V_DIM), F32), SDS((H, B * S, 1), F32)],
        scratch_shapes=[], semantics=("arbitrary", "arbitrary"), args=(q4, k4, v4))


def _conv_taps(u, halo, w):
    sh = [_shift_down(u, halo, j) for j in range(CONV_W)]
    c = w[0:1] * sh[3] + w[1:2] * sh[2] + w[2:3] * sh[1] + w[3:4] * sh[0]
    return c, sh


def _gate_values(gab, alog_l, dt_l, lane):
    g = -jnp.exp(alog_l) * jax.nn.softplus(gab + dt_l)
    g = jnp.where(lane < GDN_HEADS, g, 0.0)
    beta = jnp.where((lane >= GDN_HEADS) & (lane < 2 * GDN_HEADS), _sigmoid(gab), 0.0)
    return g, beta


def _gdn_pre(proj, conv_w, alog_l, dt_l, S):
    T = proj.shape[0]
    tm = min(256, T)
    tiles_per_seq = S // tm
    C3 = 3 * GDN_WIDTH
    H = GDN_HEADS

    def body(u_ref, halo_ref, gab_ref, w_ref, alog_ref, dt_ref, q_out, k_out, v_out, gates_out):
        i = pl.program_id(0)
        halo = jnp.where(i % tiles_per_seq == 0, 0.0, halo_ref[...])
        c, _ = _conv_taps(u_ref[...], halo, w_ref[...])
        a = c * _sigmoid(c)
        for h in range(H):
            xq = a[:, h * GDN_DIM:(h + 1) * GDN_DIM]
            xk = a[:, GDN_WIDTH + h * GDN_DIM:GDN_WIDTH + (h + 1) * GDN_DIM]
            q_out[h] = _l2n(xq, GDN_QSCALE)
            k_out[h] = _l2n(xk, 1.0)
            v_out[h] = a[:, 2 * GDN_WIDTH + h * GDN_DIM:2 * GDN_WIDTH + (h + 1) * GDN_DIM]
        lane = lax.broadcasted_iota(jnp.int32, (tm, LANES), 1)
        ric = lax.broadcasted_iota(jnp.int32, (tm, LANES), 0) % CHUNK
        g, beta = _gate_values(gab_ref[...], alog_ref[...], dt_ref[...], lane)
        gates_out[...] = _chunk_cumsum(g, ric) + beta

    hspec = pl.BlockSpec((H, tm, GDN_DIM), lambda i: (0, i, 0))
    return pl.pallas_call(
        body, grid=(T // tm,), name="gdn_pre",
        in_specs=[pl.BlockSpec((tm, C3), lambda i: (i, 0)),
                  pl.BlockSpec((SUBLANES, C3), lambda i: (jnp.maximum(i * (tm // SUBLANES) - 1, 0), 0)),
                  pl.BlockSpec((tm, LANES), lambda i: (i, P_GAB // LANES)),
                  pl.BlockSpec((CONV_W, C3), lambda i: (0, 0)),
                  pl.BlockSpec((1, LANES), lambda i: (0, 0)), pl.BlockSpec((1, LANES), lambda i: (0, 0))],
        out_specs=[hspec, hspec, hspec, pl.BlockSpec((tm, LANES), lambda i: (i, 0))],
        out_shape=[SDS((H, T, GDN_DIM), F32)] * 3 + [SDS((T, LANES), F32)],
        compiler_params=_params(("arbitrary",)),
    )(proj, proj, proj, conv_w, alog_l, dt_l)


def _unit_lower_inverses(Ls, eye):
    Ps = [eye - L for L in Ls]
    Ms = [_split(-L) for L in Ls]
    for _ in range(5):
        sq = [_mm_split(m, m) for m in Ms]
        Ms = [_split(s) for s in sq]
        Ps = [p + _mm_split(_split(p), m) for p, m in zip(Ps, Ms)]
    return Ps


def _chunk_decays(gt, lane, h, ri, ci, rcol):
    Gc = _pick_lane(gt, lane, h)
    bt = _pick_lane(gt, lane, h + GDN_HEADS)
    Gb = jnp.broadcast_to(Gc, (CHUNK, CHUNK))
    Gam = jnp.where(ri >= ci, jnp.exp(Gb - Gb.T), 0.0)
    Gl = jnp.sum(jnp.where(rcol == CHUNK - 1, Gc, 0.0), axis=0, keepdims=True)
    return Gc, bt, Gam, jnp.exp(Gc), jnp.exp(Gl - Gc), jnp.exp(Gl)


GDN_FWD_UNROLL = 16
GDN_BWD_UNROLL = 8
GDN_RECUR_STEPS_PER_STAGE = 2


def _gdn_fwd(qg, kg, vg, gates, B, S, transfer=None):
    H, D, C = GDN_HEADS, GDN_DIM, CHUNK
    NC = S // C
    U = GDN_FWD_UNROLL if NC % GDN_FWD_UNROLL == 0 else 1
    NG = NC // U

    def body(q_ref, k_ref, v_ref, g_ref, o_ref, st_ref, ai_ref, u_ref, w_ref, q2_s, au_s, bc_s, w2_s, el_s):
        h = pl.program_id(0)
        lane = lax.broadcasted_iota(jnp.int32, (C, LANES), 1)
        ri = lax.broadcasted_iota(jnp.int32, (C, C), 0)
        ci = lax.broadcasted_iota(jnp.int32, (C, C), 1)
        rcol = lax.broadcasted_iota(jnp.int32, (C, 1), 0)
        eye = (ri == ci).astype(F32)

        def group(gi, c):
            ns = [gi * U + j for j in range(U)]
            css = [pl.multiple_of(n * C, C) for n in ns]
            qs = [q_ref[0, pl.ds(cs, C), :] for cs in css]
            ks = [k_ref[0, pl.ds(cs, C), :] for cs in css]
            vs = [v_ref[0, pl.ds(cs, C), :] for cs in css]
            decs = [_chunk_decays(g_ref[pl.ds(cs, C), :], lane, h, ri, ci, rcol) for cs in css]
            qks = [_mm_nt(jnp.concatenate([q, k], axis=0), k) for q, k in zip(qs, ks)]
            ainvs = _unit_lower_inverses(
                [jnp.where(ri > ci, d[1] * qk[C:] * d[2], 0.0) for qk, d in zip(qks, decs)], eye)
            sols = [_mm_exact(a, jnp.concatenate([v * d[1], k * (d[1] * d[3])], axis=-1))
                    for a, k, v, d in zip(ainvs, ks, vs, decs)]
            atuw = [_mm(qk[:C] * d[2], sol) for qk, d, sol in zip(qks, decs, sols)]
            kduw = [_mm_tn(k * d[4], sol) for k, d, sol in zip(ks, decs, sols)]
            for n, cs, q, a, sol, au, ku, (Gc, bt, Gam, e, f, eL) in zip(ns, css, qs, ainvs, sols, atuw, kduw, decs):
                u_ref[0, pl.ds(cs, C), :] = sol[:, :D]
                w_ref[0, pl.ds(cs, C), :] = sol[:, D:]
                au_s[pl.ds(cs, C), :] = au[:, :D]
                q2_s[pl.ds(cs, C), :] = q * e - au[:, D:]
                bc_s[n] = ku[:, :D]
                w2_s[n] = ku[:, D:]
                el_s[n] = jnp.broadcast_to(eL, (SUBLANES, LANES))
                ai_ref[0, n] = a.T
            return c

        lax.fori_loop(0, NG, group, 0)

        def step(n, S_):
            cs = pl.multiple_of(n * C, C)
            o_ref[0, pl.ds(cs, C), :] = _mm(q2_s[pl.ds(cs, C), :], S_) + au_s[pl.ds(cs, C), :]
            st_ref[0, n] = S_
            return S_ * el_s[n, 0:1, :] + bc_s[n] - _mm(w2_s[n], S_)

        lax.fori_loop(0, NC, step, jnp.zeros((D, D), F32))

    spec = pl.BlockSpec((1, S, D), lambda h, b: (h, b, 0))
    return _call_beside(
        body, transfer, grid=(H, B), name="gdn_fwd",
        in_specs=[spec, spec, spec, pl.BlockSpec((S, LANES), lambda h, b: (b, 0))],
        out_specs=[spec, pl.BlockSpec((1, NC, D, D), lambda h, b: (h, b, 0, 0)),
                   pl.BlockSpec((1, NC, C, C), lambda h, b: (h, b, 0, 0)), spec, spec],
        out_shape=[SDS((H, B * S, D), F32), SDS((H, B * NC, D, D), F32), SDS((H, B * NC, C, C), F32),
                   SDS((H, B * S, D), F32), SDS((H, B * S, D), F32)],
        scratch_shapes=[pltpu.VMEM((S, D), F32), pltpu.VMEM((S, D), F32), pltpu.VMEM((NC, D, D), F32),
                        pltpu.VMEM((NC, D, D), F32), pltpu.VMEM((NC, SUBLANES, LANES), F32)],
        semantics=("arbitrary", "arbitrary"), args=(qg, kg, vg, gates))


def _mix_out(o_mla, o_gdn, proj, x2, mla_w, gdn_w, w_out):
    T, D = x2.shape
    tm = min(512, T)
    H = MLA_HEADS

    def body(om_ref, og_ref, z_ref, x_ref, mw_ref, gw_ref, w_ref, h_ref, mix_ref):
        z = z_ref[...]
        parts = [_rms(om_ref[h], mw_ref[h:h + 1, :])[0] for h in range(H)]
        for h in range(GDN_HEADS):
            zh = z[:, h * GDN_DIM:(h + 1) * GDN_DIM]
            parts.append(_rms(og_ref[h], gw_ref[...])[0] * (zh * _sigmoid(zh)))
        mix = jnp.concatenate(parts, axis=-1).astype(MXU_DTYPE)
        mix_ref[...] = mix
        h_ref[...] = x_ref[...] + jnp.dot(mix, w_ref[...], preferred_element_type=F32)

    hspec = pl.BlockSpec((H, tm, V_DIM), lambda i: (0, i, 0))
    return pl.pallas_call(
        body, grid=(T // tm,), name="mix_out",
        in_specs=[hspec, hspec, pl.BlockSpec((tm, GDN_WIDTH), lambda i: (i, P_GZ // GDN_WIDTH)),
                  pl.BlockSpec((tm, D), lambda i: (i, 0)),
                  pl.BlockSpec((H, V_DIM), lambda i: (0, 0)), pl.BlockSpec((1, GDN_DIM), lambda i: (0, 0)),
                  pl.BlockSpec((D, D), lambda i: (0, 0))],
        out_specs=[pl.BlockSpec((tm, D), lambda i: (i, 0)), pl.BlockSpec((tm, D), lambda i: (i, 0))],
        out_shape=[SDS((T, D), F32), SDS((T, D), MXU_DTYPE)],
        compiler_params=_params(("arbitrary",)),
    )(o_mla, o_gdn, proj, x2, mla_w, gdn_w, w_out)


def _mlp_fwd(h2, w_mn, w_up, w_down, target):
    T, D = h2.shape
    ns, _, ts = w_up.shape
    F = ns * ts
    tm = min(512, T)
    G = MLP_FWD_SHARDS
    tf, nf = G * ts, ns // G

    def body(h_ref, wn_ref, up_w, down_w, t_ref, up_ref, hn_ref, dy_ref, loss_ref, y_acc):
        j = pl.program_id(1)

        @pl.when(j == 0)
        def _():
            hn_ref[...] = _rms(h_ref[...], wn_ref[...])[0].astype(MXU_DTYPE)
            y_acc[...] = h_ref[...]

        parts = []
        for c in range(G):
            up = jnp.dot(hn_ref[...], up_w[c], preferred_element_type=F32)
            up_ref[:, c * ts:(c + 1) * ts] = up
            r = jnp.maximum(up, 0.0)
            parts.append(_mm(r * r, down_w[c * ts:(c + 1) * ts, :]))
        y_acc[...] += functools.reduce(jnp.add, parts)

        @pl.when(j == nf - 1)
        def _():
            err = y_acc[...] - t_ref[...]
            dy_ref[...] = err / D
            loss_ref[...] = jnp.full((1, SUBLANES, LANES), jnp.sum(err * err), F32)

    return pl.pallas_call(
        body, grid=(T // tm, nf), name="mlp_fwd",
        in_specs=[pl.BlockSpec((tm, D), lambda i, j: (i, 0)), pl.BlockSpec((1, D), lambda i, j: (0, 0)),
                  pl.BlockSpec((G, D, ts), lambda i, j: (j, 0, 0)), pl.BlockSpec((tf, D), lambda i, j: (j, 0)),
                  pl.BlockSpec((tm, D), lambda i, j: (i, 0))],
        out_specs=[pl.BlockSpec((tm, tf), lambda i, j: (i, j)), pl.BlockSpec((tm, D), lambda i, j: (i, 0)),
                   pl.BlockSpec((tm, D), lambda i, j: (i, 0)),
                   pl.BlockSpec((1, SUBLANES, LANES), lambda i, j: (i, 0, 0))],
        out_shape=[SDS((T, F), F32), SDS((T, D), MXU_DTYPE), SDS((T, D), F32),
                   SDS((T // tm, SUBLANES, LANES), F32)],
        scratch_shapes=[pltpu.VMEM((tm, D), F32)],
        compiler_params=_params(("arbitrary", "arbitrary")),
    )(h2, w_mn, w_up, w_down, target)


def _mlp_bwd(dy, up, h2, w_mn, w_up, w_down):
    T, D = h2.shape
    ns, _, ts = w_up.shape
    F = ns * ts
    tm = min(512, T)
    G = MLP_BWD_SHARDS
    tf, nf = G * ts, ns // G

    def body(dy_ref, up_ref, h_ref, wn_ref, up_w, down_w, dh_ref, dhb_ref, dup_ref, act_ref, dyb_ref, dwn_ref, acc):
        i, j = pl.program_id(0), pl.program_id(1)

        @pl.when((i == 0) & (j == 0))
        def _():
            dwn_ref[...] = jnp.zeros_like(dwn_ref)

        @pl.when(j == 0)
        def _():
            acc[...] = jnp.zeros_like(acc)
            dyb_ref[...] = dy_ref[...].astype(MXU_DTYPE)

        parts = []
        for c in range(G):
            cols = slice(c * ts, (c + 1) * ts)
            r = jnp.maximum(up_ref[:, cols], 0.0)
            act_ref[:, cols] = (r * r).astype(MXU_DTYPE)
            dup = (_mm_nt(dyb_ref[...], down_w[cols, :]) * (2.0 * r)).astype(MXU_DTYPE)
            dup_ref[:, cols] = dup
            parts.append(_mm_nt(dup, up_w[c]))
        acc[...] += functools.reduce(jnp.add, parts)

        @pl.when(j == nf - 1)
        def _():
            hv = h_ref[...]
            _, rr = _rms(hv, wn_ref[...])
            dx, dw = _rms_bwd(acc[...], hv, wn_ref[...], rr)
            dh = dy_ref[...] + dx
            dh_ref[...] = dh
            dhb_ref[...] = dh.astype(MXU_DTYPE)
            dwn_ref[...] += dw

    row = lambda i, j: (i, 0)
    return pl.pallas_call(
        body, grid=(T // tm, nf), name="mlp_bwd",
        in_specs=[pl.BlockSpec((tm, D), row), pl.BlockSpec((tm, tf), lambda i, j: (i, j)), pl.BlockSpec((tm, D), row),
                  pl.BlockSpec((1, D), lambda i, j: (0, 0)),
                  pl.BlockSpec((G, D, ts), lambda i, j: (j, 0, 0)), pl.BlockSpec((tf, D), lambda i, j: (j, 0))],
        out_specs=[pl.BlockSpec((tm, D), row), pl.BlockSpec((tm, D), row),
                   pl.BlockSpec((tm, tf), lambda i, j: (i, j)), pl.BlockSpec((tm, tf), lambda i, j: (i, j)),
                   pl.BlockSpec((tm, D), row), pl.BlockSpec((1, D), lambda i, j: (0, 0))],
        out_shape=[SDS((T, D), F32), SDS((T, D), MXU_DTYPE), SDS((T, F), MXU_DTYPE), SDS((T, F), MXU_DTYPE),
                   SDS((T, D), MXU_DTYPE), SDS((1, D), F32)],
        scratch_shapes=[pltpu.VMEM((tm, D), F32)],
        compiler_params=_params(("arbitrary", "arbitrary")),
    )(dy, up, h2, w_mn, w_up, w_down)


def _mix_bwd(dhb, o_mla, o_gdn, proj, mla_w, gdn_w, w_out):
    T, D = dhb.shape
    tm = min(512, T)
    H = MLA_HEADS

    def body(dh_ref, om_ref, og_ref, z_ref, mw_ref, gw_ref, w_ref, dom_ref, dog_ref, dz_ref, dmw_ref, dgw_ref):
        @pl.when(pl.program_id(0) == 0)
        def _():
            dmw_ref[...] = jnp.zeros_like(dmw_ref)
            dgw_ref[...] = jnp.zeros_like(dgw_ref)

        dmix = _mm_nt(dh_ref[...], w_ref[...])
        z = z_ref[...]
        dmw, dzs = [], []
        dgw = jnp.zeros((1, GDN_DIM), F32)
        for h in range(H):
            o = om_ref[h]
            w = mw_ref[h:h + 1, :]
            _, r = _rms(o, w)
            dx, dw = _rms_bwd(dmix[:, h * V_DIM:(h + 1) * V_DIM], o, w, r)
            dom_ref[h] = dx
            dmw.append(dw)
        for h in range(GDN_HEADS):
            o = og_ref[h]
            w = gw_ref[...]
            zh = z[:, h * GDN_DIM:(h + 1) * GDN_DIM]
            sg = _sigmoid(zh)
            yn, r = _rms(o, w)
            dy = dmix[:, H * V_DIM + h * GDN_DIM:H * V_DIM + (h + 1) * GDN_DIM]
            dzs.append(dy * yn * (sg * (1.0 + zh * (1.0 - sg))))
            dx, dw = _rms_bwd(dy * (zh * sg), o, w, r)
            dog_ref[h] = dx
            dgw = dgw + dw
        dz_ref[...] = jnp.concatenate(dzs, axis=-1)
        dmw_ref[...] += jnp.concatenate(dmw, axis=0)
        dgw_ref[...] += dgw

    hspec = pl.BlockSpec((H, tm, V_DIM), lambda i: (0, i, 0))
    return pl.pallas_call(
        body, grid=(T // tm,), name="mix_bwd",
        in_specs=[pl.BlockSpec((tm, D), lambda i: (i, 0)), hspec, hspec,
                  pl.BlockSpec((tm, GDN_WIDTH), lambda i: (i, P_GZ // GDN_WIDTH)),
                  pl.BlockSpec((H, V_DIM), lambda i: (0, 0)), pl.BlockSpec((1, GDN_DIM), lambda i: (0, 0)),
                  pl.BlockSpec((D, D), lambda i: (0, 0))],
        out_specs=[hspec, hspec, pl.BlockSpec((tm, GDN_WIDTH), lambda i: (i, 0)),
                   pl.BlockSpec((H, V_DIM), lambda i: (0, 0)), pl.BlockSpec((1, GDN_DIM), lambda i: (0, 0))],
        out_shape=[SDS((H, T, V_DIM), F32), SDS((H, T, GDN_DIM), F32), SDS((T, GDN_WIDTH), F32),
                   SDS((H, V_DIM), F32), SDS((1, GDN_DIM), F32)],
        compiler_params=_params(("arbitrary",)),
    )(dhb, o_mla, o_gdn, proj, mla_w, gdn_w, w_out)


def _attn_bwd(q4, k4, v4, do4, o4, lse4, B, S, transfer=None):
    H = MLA_HEADS
    bq = min(ATTN_BLOCK, S)
    nq = S // bq
    rows = bq // ATTN_CHAINS

    def body(q_ref, k_ref, v_ref, do_ref, o_ref, lse_ref, dq_ref, dk_ref, dv_ref, delta):
        dq_ref[...] = jnp.zeros_like(dq_ref)
        dk_ref[...] = jnp.zeros_like(dk_ref)
        dv_ref[...] = jnp.zeros_like(dv_ref)
        delta[...] = jnp.sum(do_ref[0] * o_ref[0], axis=-1, keepdims=True)

        col = lax.broadcasted_iota(jnp.int32, (rows, bq), 1)
        row = lax.broadcasted_iota(jnp.int32, (rows, bq), 0)

        def k_step(kj, carry):
            ks = pl.multiple_of(kj * bq, bq)
            k = k_ref[0, pl.ds(ks, bq), :]
            v = v_ref[0, pl.ds(ks, bq), :]

            def q_block(qs, diagonal):
                dks, dvs = [None] * ATTN_CHAINS, [None] * ATTN_CHAINS

                def chain(j):
                    sl = pl.ds(qs + j * rows, rows)
                    q = q_ref[0, sl, :]
                    do = do_ref[0, sl, :].astype(MXU_DTYPE)
                    s = _mm_nt(q, k)
                    dp = _mm_nt(do, v)
                    yield
                    p = jnp.exp(s - lse_ref[0, sl, :])
                    if diagonal:
                        p = jnp.where(col <= row + j * rows, p, 0.0)
                    ds = p * (dp - delta[sl, :])
                    yield
                    dvs[j] = _mm_tn(p, do)
                    dks[j] = _mm_tn(ds, q)
                    dq_ref[0, sl, :] += _mm(ds, k)

                _lockstep([chain(j) for j in range(ATTN_CHAINS)])
                dv_ref[0, pl.ds(ks, bq), :] += functools.reduce(jnp.add, dvs)
                dk_ref[0, pl.ds(ks, bq), :] += functools.reduce(jnp.add, dks)

            q_block(ks, True)

            def q_step(qi, c):
                q_block(pl.multiple_of(qi * bq, bq), False)
                return c

            lax.fori_loop(kj + 1, nq, q_step, 0)
            return carry

        lax.fori_loop(0, nq, k_step, 0)

    spec = lambda d: pl.BlockSpec((1, S, d), lambda h, b: (h, b, 0))
    return _call_beside(
        body, transfer, grid=(H, B), name="attn_bwd",
        in_specs=[spec(QK_DIM), spec(QK_DIM), spec(V_DIM), spec(V_DIM), spec(V_DIM), spec(1)],
        out_specs=[spec(QK_DIM), spec(QK_DIM), spec(V_DIM)],
        out_shape=[SDS((H, B * S, QK_DIM), F32), SDS((H, B * S, QK_DIM), F32), SDS((H, B * S, V_DIM), F32)],
        scratch_shapes=[pltpu.VMEM((S, 1), F32)], semantics=("arbitrary", "arbitrary"),
        args=(q4, k4, v4, do4, o4, lse4))


def _gdn_bwd(qg, kg, vg, gates, states, ainv, u4, w4, do4, B, S, transfer=None):
    H, D, C = GDN_HEADS, GDN_DIM, CHUNK
    NC = S // C
    U = GDN_BWD_UNROLL if NC % GDN_BWD_UNROLL == 0 else 1
    NG = NC // U

    def body(q_ref, k_ref, v_ref, g_ref, st_ref, ai_ref, u_ref, w_ref, do_ref, dq_ref, dk_ref, dv_ref, dgb_ref,
             kd_s, x1_s, x2_s, el_s, dvn_s, ds_s, w2t_s):
        h = pl.program_id(0)
        lane = lax.broadcasted_iota(jnp.int32, (C, LANES), 1)
        ri = lax.broadcasted_iota(jnp.int32, (C, C), 0)
        ci = lax.broadcasted_iota(jnp.int32, (C, C), 1)
        rcol = lax.broadcasted_iota(jnp.int32, (C, 1), 0)

        def rsum(a):
            return jnp.sum(a, axis=-1, keepdims=True)

        def prepare(n):
            cs = n * C
            q = q_ref[0, pl.ds(cs, C), :]
            k = k_ref[0, pl.ds(cs, C), :]
            do = do_ref[0, pl.ds(cs, C), :]
            Gc, bt, Gam, e, f, eL = _chunk_decays(g_ref[pl.ds(cs, C), :], lane, h, ri, ci, rcol)
            At = _mm_nt(q, k) * Gam
            yield
            x1 = _mm_tn(At, do)
            x2 = _mm_tn(q * e, do)
            kd = k * f
            w = w_ref[0, pl.ds(cs, C), :]
            yield
            x1_s[pl.ds(cs, C), :] = x1
            x2_s[n] = x2 - _mm_tn(w, x1)
            w2t_s[n] = _mm_tn(w, kd)
            kd_s[pl.ds(cs, C), :] = kd
            el_s[n] = jnp.broadcast_to(eL, (SUBLANES, LANES))

        def recur(n, dS):
            cs = n * C
            ds_s[n] = dS
            dvn_s[pl.ds(cs, C), :] = x1_s[pl.ds(cs, C), :] + _mm(kd_s[pl.ds(cs, C), :], dS)
            return x2_s[n] + el_s[n, 0:1, :] * dS - _mm(w2t_s[n], dS)

        def local(n):
            cs = n * C
            q = q_ref[0, pl.ds(cs, C), :]
            k = k_ref[0, pl.ds(cs, C), :]
            v = v_ref[0, pl.ds(cs, C), :]
            do = do_ref[0, pl.ds(cs, C), :]
            u = u_ref[0, pl.ds(cs, C), :]
            w = w_ref[0, pl.ds(cs, C), :]
            dvn = dvn_s[pl.ds(cs, C), :]
            dS = ds_s[n]
            Gc, bt, Gam, e, f, eL = _chunk_decays(g_ref[pl.ds(cs, C), :], lane, h, ri, ci, rcol)
            S0 = st_ref[0, n]
            AinvT = ai_ref[0, n]
            qk = _mm_nt(jnp.concatenate([q, k], axis=0), k)
            QK, KK = qk[:C], qk[C:]
            be = bt * e
            sol = jnp.concatenate([u, w], axis=-1)
            vn = u - _mm(w, S0)
            yield
            dAt = jnp.where(ri >= ci, _mm_nt(do, vn), 0.0)
            dqd = _mm_nt(do, S0)
            dw = -_mm_nt(dvn, S0)
            dkd = _mm_nt(vn, dS)
            deL = jnp.sum(rsum(dS * S0), axis=0, keepdims=True)
            yield
            dR = _mm_exact(AinvT, jnp.concatenate([dvn, dw], axis=-1))
            dR1, dR2 = dR[:, :D], dR[:, D:]
            yield
            dL = jnp.where(ri > ci, -_mm_nt(dR, sol), 0.0)
            yield
            dv_ref[0, pl.ds(cs, C), :] = dR1 * bt
            r2 = rsum(dR2 * k)
            X = dL * Gam
            dbt = rsum(dR1 * v) + r2 * e + rsum(X * KK)
            de = r2 * bt + rsum(dqd * q)
            dKK = X * bt
            dQK = dAt * Gam
            dq_ref[0, pl.ds(cs, C), :] = _mm(dQK, k) + dqd * e
            dk_ref[0, pl.ds(cs, C), :] = dR2 * be + _mm(dKK + dKK.T, k) + _mm_tn(dQK, q) + dkd * f
            df = rsum(dkd * k)
            Z = (dL * (bt * KK) + dAt * QK) * Gam
            dG = rsum(Z) - rsum(Z.T) + de * e - df * f
            dGl = jnp.sum(df * f, axis=0, keepdims=True) + deL * eL
            dG = dG + jnp.where(rcol == C - 1, dGl, 0.0)
            dgb_ref[0, pl.ds(cs, C), :] = jnp.where(lane == 0, dG, jnp.where(lane == 1, dbt, 0.0))

        state = [jnp.zeros((D, D), F32)]

        def recur_group(g):
            for j, n in enumerate(reversed(range(g * U, (g + 1) * U))):
                state[0] = recur(n, state[0])
                if j % GDN_RECUR_STEPS_PER_STAGE == GDN_RECUR_STEPS_PER_STAGE - 1:
                    yield

        def stage(fn, g):
            return _together([fn(g * U + j) for j in range(U)])

        for step in range(NG + 2):
            jobs = [(stage, prepare, NG - 1 - step), (None, None, NG - step), (stage, local, NG + 1 - step)]
            _lockstep([recur_group(g) if make is None else make(fn, g) for make, fn, g in jobs if 0 <= g < NG])

    spec = pl.BlockSpec((1, S, D), lambda h, b: (h, b, 0))
    return _call_beside(
        body, transfer, grid=(H, B), name="gdn_bwd",
        in_specs=[spec, spec, spec, pl.BlockSpec((S, LANES), lambda h, b: (b, 0)),
                  pl.BlockSpec((1, NC, D, D), lambda h, b: (h, b, 0, 0)),
                  pl.BlockSpec((1, NC, C, C), lambda h, b: (h, b, 0, 0)), spec, spec, spec],
        out_specs=[spec, spec, spec, spec],
        out_shape=[SDS((H, B * S, D), F32)] * 4,
        scratch_shapes=[pltpu.VMEM((S, D), F32), pltpu.VMEM((S, D), F32), pltpu.VMEM((NC, D, D), F32),
                        pltpu.VMEM((NC, SUBLANES, LANES), F32), pltpu.VMEM((S, D), F32),
                        pltpu.VMEM((NC, D, D), F32), pltpu.VMEM((NC, D, D), F32)],
        semantics=("arbitrary", "arbitrary"), args=(qg, kg, vg, gates, states, ainv, u4, w4, do4))


def _gdn_pre_bwd(proj, conv_w, alog_l, dt_l, dq4, dk4, dv4, dgb4, S):
    T = proj.shape[0]
    tm = min(256, T)
    tiles_per_seq = S // tm
    C3 = 3 * GDN_WIDTH
    H = GDN_HEADS

    def body(u_ref, halo_ref, gab_ref, w_ref, alog_ref, dt_ref, dq_ref, dk_ref, dv_ref, dgb_ref,
             dc_ref, dgab_ref, dcw_ref, dalog_ref, ddt_ref):
        i = pl.program_id(0)

        @pl.when(i == 0)
        def _():
            dcw_ref[...] = jnp.zeros_like(dcw_ref)
            dalog_ref[...] = jnp.zeros_like(dalog_ref)
            ddt_ref[...] = jnp.zeros_like(ddt_ref)

        halo = jnp.where(i % tiles_per_seq == 0, 0.0, halo_ref[...])
        c, sh = _conv_taps(u_ref[...], halo, w_ref[...])
        sg = _sigmoid(c)
        a = c * sg
        das = [None] * (3 * H)
        for h in range(H):
            xq = a[:, h * GDN_DIM:(h + 1) * GDN_DIM]
            xk = a[:, GDN_WIDTH + h * GDN_DIM:GDN_WIDTH + (h + 1) * GDN_DIM]
            das[h] = _l2n_bwd(dq_ref[h], xq, GDN_QSCALE)
            das[H + h] = _l2n_bwd(dk_ref[h], xk, 1.0)
            das[2 * H + h] = dv_ref[h]
        dc = jnp.concatenate(das, axis=-1) * (sg * (1.0 + c * (1.0 - sg)))
        dc_ref[...] = dc
        dcw_ref[...] += jnp.concatenate(
            [jnp.sum(dc * sh[CONV_W - 1 - t], axis=0, keepdims=True) for t in range(CONV_W)], axis=0)
        lane = lax.broadcasted_iota(jnp.int32, (tm, LANES), 1)
        ric = lax.broadcasted_iota(jnp.int32, (tm, LANES), 0) % CHUNK
        dG = jnp.zeros((tm, LANES), F32)
        for h in range(H):
            t = dgb_ref[h]
            dG = dG + jnp.where(lane == h, _pick_lane(t, lane, 0), 0.0) \
                    + jnp.where(lane == h + H, _pick_lane(t, lane, 1), 0.0)
        is_g = lane < H
        dg = jnp.where(is_g, _chunk_rev_cumsum(jnp.where(is_g, dG, 0.0), ric), 0.0)
        gab = gab_ref[...]
        g, beta = _gate_values(gab, alog_ref[...], dt_ref[...], lane)
        dga = jnp.where(is_g, dg * (-jnp.exp(alog_ref[...])) * _sigmoid(gab + dt_ref[...]), 0.0)
        dgb = jnp.where(is_g, 0.0, dG) * beta * (1.0 - beta)
        dgab_ref[...] = dga + dgb
        dalog_ref[...] += jnp.sum(dg * g, axis=0, keepdims=True)
        ddt_ref[...] += jnp.sum(dga, axis=0, keepdims=True)

    hspec = pl.BlockSpec((H, tm, GDN_DIM), lambda i: (0, i, 0))
    vec = pl.BlockSpec((1, LANES), lambda i: (0, 0))
    return pl.pallas_call(
        body, grid=(T // tm,), name="gdn_pre_bwd",
        in_specs=[pl.BlockSpec((tm, C3), lambda i: (i, 0)),
                  pl.BlockSpec((SUBLANES, C3), lambda i: (jnp.maximum(i * (tm // SUBLANES) - 1, 0), 0)),
                  pl.BlockSpec((tm, LANES), lambda i: (i, P_GAB // LANES)),
                  pl.BlockSpec((CONV_W, C3), lambda i: (0, 0)), vec, vec, hspec, hspec, hspec, hspec],
        out_specs=[pl.BlockSpec((tm, C3), lambda i: (i, 0)), pl.BlockSpec((tm, LANES), lambda i: (i, 0)),
                   pl.BlockSpec((CONV_W, C3), lambda i: (0, 0)), vec, vec],
        out_shape=[SDS((T, C3), F32), SDS((T, LANES), F32), SDS((CONV_W, C3), F32),
                   SDS((1, LANES), F32), SDS((1, LANES), F32)],
        compiler_params=_params(("arbitrary",)),
    )(proj, proj, proj, conv_w, alog_l, dt_l, dq4, dk4, dv4, dgb4)


def _conv_bwd_input(dc, conv_w, S):
    T, C3 = dc.shape
    tm = min(256, T)
    tiles_per_seq = S // tm
    nblk = T // SUBLANES

    def body(dc_ref, nxt_ref, w_ref, du_ref):
        i = pl.program_id(0)
        nxt = jnp.where(i % tiles_per_seq == tiles_per_seq - 1, 0.0, nxt_ref[...])
        x = dc_ref[...]
        w = w_ref[...]
        du = w[3:4] * x
        for j in range(1, CONV_W):
            du = du + w[3 - j:4 - j] * _shift_up(x, nxt, j)
        du_ref[...] = du

    return pl.pallas_call(
        body, grid=(T // tm,), name="conv_bwd_input",
        in_specs=[pl.BlockSpec((tm, C3), lambda i: (i, 0)),
                  pl.BlockSpec((SUBLANES, C3), lambda i: (jnp.minimum((i + 1) * (tm // SUBLANES), nblk - 1), 0)),
                  pl.BlockSpec((CONV_W, C3), lambda i: (0, 0))],
        out_specs=pl.BlockSpec((tm, C3), lambda i: (i, 0)),
        out_shape=SDS((T, C3), F32),
        compiler_params=_params(("arbitrary",)),
    )(dc, dc, conv_w)


def _mla_pre_bwd(proj, cosf, sinf, w_qln, w_kvln, w_uq_p, w_ukv, qnw, knw, dq4, dk4, dv4, transfer=None):
    T = proj.shape[0]
    tm = min(256, T)
    H = MLA_HEADS

    def body(ql_ref, kvl_ref, kpe_ref, cos_ref, sin_ref, wq_ref, wkv_ref, uq_ref, ukv_ref, qnw_ref, knw_ref,
             dq_ref, dk_ref, dv_ref,
             dql_ref, dkvl_ref, dkpe_ref, dqraw_ref, dkvraw_ref, qn_ref, kvn_ref, dwq_ref, dwkv_ref, dqnw_ref, dknw_ref):
        @pl.when(pl.program_id(0) == 0)
        def _():
            for r in (dwq_ref, dwkv_ref, dqnw_ref, dknw_ref):
                r[...] = jnp.zeros_like(r)

        cos, sin = cos_ref[...], sin_ref[...]
        qnw_, knw_ = qnw_ref[...], knw_ref[...]
        ql, kvl = ql_ref[...], kvl_ref[...]
        kpe_raw = kpe_ref[...][:, :ROPE]
        rms = functools.partial(_rms, on_mxu=True)
        rms_bwd = functools.partial(_rms_bwd, on_mxu=True)
        qn, rq = rms(ql, wq_ref[...])
        kvn, rkv = rms(kvl, wkv_ref[...])
        qn_ref[...] = qn.astype(MXU_DTYPE)
        kvn_ref[...] = kvn.astype(MXU_DTYPE)
        qraw = _mm(qn, uq_ref[...])
        kvraw = _mm(kvn, ukv_ref[...])
        dq_nope, dq_pe, dkv_parts = [], [], []
        dqnw_n = jnp.zeros((1, NOPE), F32)
        dqnw_p = jnp.zeros((1, ROPE), F32)
        dknw_n = jnp.zeros((1, NOPE), F32)
        dkpe = jnp.zeros((tm, ROPE), F32)
        for h in range(H):
            dq = dq_ref[h] * ATT_SCALE
            x = qraw[:, h * NOPE:(h + 1) * NOPE]
            dx, dw = rms_bwd(dq[:, :NOPE], x, qnw_[:, :NOPE], rms(x, qnw_[:, :NOPE])[1])
            dq_nope.append(dx)
            dqnw_n = dqnw_n + dw
            x = qraw[:, H * NOPE + h * ROPE:H * NOPE + (h + 1) * ROPE]
            dx, dw = rms_bwd(_rope_bwd(dq[:, NOPE:], cos, sin), x, qnw_[:, NOPE:], rms(x, qnw_[:, NOPE:])[1])
            dq_pe.append(dx)
            dqnw_p = dqnw_p + dw
            dk = dk_ref[h]
            x = kvraw[:, h * 256:h * 256 + NOPE]
            dx, dw = rms_bwd(dk[:, :NOPE], x, knw_[:, :NOPE], rms(x, knw_[:, :NOPE])[1])
            dknw_n = dknw_n + dw
            dkpe = dkpe + dk[:, NOPE:]
            dkv_parts += [dx, dv_ref[h]]
        dx, dknw_p = rms_bwd(_rope_bwd(dkpe, cos, sin), kpe_raw, knw_[:, NOPE:], rms(kpe_raw, knw_[:, NOPE:])[1])
        dkpe_ref[...] = jnp.concatenate([dx, jnp.zeros((tm, LANES - ROPE), F32)], axis=-1)
        dqraw = jnp.concatenate(dq_nope + dq_pe, axis=-1).astype(MXU_DTYPE)
        dkvraw = jnp.concatenate(dkv_parts, axis=-1).astype(MXU_DTYPE)
        dqraw_ref[...] = dqraw
        dkvraw_ref[...] = dkvraw
        dx, dw = rms_bwd(_mm_nt(dqraw, uq_ref[...]), ql, wq_ref[...], rq)
        dql_ref[...] = dx
        dwq_ref[...] += dw
        dx, dw = rms_bwd(_mm_nt(dkvraw, ukv_ref[...]), kvl, wkv_ref[...], rkv)
        dkvl_ref[...] = dx
        dwkv_ref[...] += dw
        dqnw_ref[...] += jnp.concatenate([dqnw_n, dqnw_p], axis=-1)
        dknw_ref[...] += jnp.concatenate([dknw_n, dknw_p], axis=-1)

    full = lambda a: pl.BlockSpec(a.shape, lambda i: (0,) * a.ndim)
    rows = lambda n: pl.BlockSpec((tm, n), lambda i: (i, 0))
    const = lambda n: pl.BlockSpec((1, n), lambda i: (0, 0))
    NQ, NKV = w_uq_p.shape[1], w_ukv.shape[1]
    return _call_beside(
        body, transfer, grid=(T // tm,), name="mla_pre_bwd", scratch_shapes=[], semantics=("arbitrary",),
        args=(proj, proj, proj, cosf, sinf, w_qln, w_kvln, w_uq_p, w_ukv, qnw, knw, dq4, dk4, dv4),
        in_specs=[pl.BlockSpec((tm, 256), lambda i: (i, P_QLAT // 256)),
                  pl.BlockSpec((tm, 256), lambda i: (i, P_KVLAT // 256)),
                  pl.BlockSpec((tm, 128), lambda i: (i, P_KPE // 128)),
                  rows(ROPE), rows(ROPE),
                  full(w_qln), full(w_kvln), full(w_uq_p), full(w_ukv), full(qnw), full(knw),
                  pl.BlockSpec((H, tm, QK_DIM), lambda i: (0, i, 0)),
                  pl.BlockSpec((H, tm, QK_DIM), lambda i: (0, i, 0)),
                  pl.BlockSpec((H, tm, V_DIM), lambda i: (0, i, 0))],
        out_specs=[rows(Q_LORA), rows(KV_LORA), rows(LANES), rows(NQ), rows(NKV), rows(Q_LORA), rows(KV_LORA),
                   const(Q_LORA), const(KV_LORA), const(QK_DIM), const(QK_DIM)],
        out_shape=[SDS((T, Q_LORA), F32), SDS((T, KV_LORA), F32), SDS((T, LANES), F32),
                   SDS((T, NQ), MXU_DTYPE), SDS((T, NKV), MXU_DTYPE),
                   SDS((T, Q_LORA), MXU_DTYPE), SDS((T, KV_LORA), MXU_DTYPE),
                   SDS((1, Q_LORA), F32), SDS((1, KV_LORA), F32), SDS((1, QK_DIM), F32), SDS((1, QK_DIM), F32)])


def _in_proj_bwd(dgqkv, dgz, dql, dkvl, dkpe, dgab, w_in_p, dh, x2, w_an):
    T, D = x2.shape
    N = w_in_p.shape[1]
    tm = min(512, T)

    def body(a_ref, b_ref, c_ref, d_ref, e_ref, f_ref, w_ref, dh_ref, x_ref, wn_ref, dx_ref, dp_ref, dwn_ref):
        @pl.when(pl.program_id(0) == 0)
        def _():
            dwn_ref[...] = jnp.zeros_like(dwn_ref)

        dp = jnp.concatenate([a_ref[...], b_ref[...], c_ref[...], d_ref[...], e_ref[...], f_ref[...]],
                             axis=-1).astype(MXU_DTYPE)
        dp_ref[...] = dp
        x = x_ref[...]
        _, r = _rms(x, wn_ref[...])
        dx, dw = _rms_bwd(_mm_nt(dp, w_ref[...]), x, wn_ref[...], r)
        dx_ref[...] = dh_ref[...] + dx
        dwn_ref[...] += dw

    rows = lambda n: pl.BlockSpec((tm, n), lambda i: (i, 0))
    return pl.pallas_call(
        body, grid=(T // tm,), name="in_proj_bwd",
        in_specs=[rows(dgqkv.shape[1]), rows(dgz.shape[1]), rows(dql.shape[1]), rows(dkvl.shape[1]),
                  rows(dkpe.shape[1]), rows(dgab.shape[1]),
                  pl.BlockSpec((D, N), lambda i: (0, 0)), rows(D), rows(D), pl.BlockSpec((1, D), lambda i: (0, 0))],
        out_specs=[rows(D), rows(N), pl.BlockSpec((1, D), lambda i: (0, 0))],
        out_shape=[SDS((T, D), F32), SDS((T, N), MXU_DTYPE), SDS((1, D), F32)],
        compiler_params=_params(("arbitrary",)),
    )(dgqkv, dgz, dql, dkvl, dkpe, dgab, w_in_p, dh, x2, w_an)


def _wgrad(a, b, name, column_shards=False):
    T, M = a.shape
    N = b.shape[1]
    tM = _divisor_tile(M, 1024)
    tN = N // N_DEV if column_shards else _divisor_tile(N, 1536)
    tk = min(T, 2048)
    nk = T // tk

    def body(a_ref, b_ref, o_ref, acc):
        k = pl.program_id(2)

        @pl.when(k == 0)
        def _():
            acc[...] = jnp.zeros_like(acc)

        acc[...] += _mm_tn(a_ref[...], b_ref[...])

        @pl.when(k == nk - 1)
        def _():
            o_ref[...] = acc[...].astype(WIRE_DTYPE).reshape(o_ref.shape)

    if column_shards:
        out_spec, out_shape = pl.BlockSpec((1, tM, tN), lambda i, j, k: (j, i, 0)), SDS((N_DEV, M, tN), WIRE_DTYPE)
    else:
        out_spec, out_shape = pl.BlockSpec((tM, tN), lambda i, j, k: (i, j)), SDS((M, N), WIRE_DTYPE)
    return pl.pallas_call(
        body, grid=(M // tM, N // tN, nk), name=name,
        in_specs=[pl.BlockSpec((tk, tM), lambda i, j, k: (k, i)), pl.BlockSpec((tk, tN), lambda i, j, k: (k, j))],
        out_specs=out_spec, out_shape=out_shape,
        scratch_shapes=[pltpu.VMEM((tM, tN), F32)],
        compiler_params=_params(("arbitrary", "arbitrary", "arbitrary")),
    )(a, b)


def _adamw(g, w, m, v):
    m = ADAM_B1 * m + (1.0 - ADAM_B1) * g
    v = ADAM_B2 * v + (1.0 - ADAM_B2) * jnp.square(g)
    m_hat = m / (1.0 - ADAM_B1 ** ADAM_STEP)
    v_hat = v / (1.0 - ADAM_B2 ** ADAM_STEP)
    return -ADAM_LR * (m_hat / (jnp.sqrt(v_hat) + ADAM_EPS) + ADAM_WD * w), m, v


def _reduce_adamw(parts, w, m, v, name):
    R, C = w.shape
    _, Rp, Cp = parts.shape
    tr = min(R, 256)
    tp = tr if Rp == R else Rp

    def body(p_ref, w_ref, m_ref, v_ref, g_ref, d_ref, nm_ref, nv_ref):
        g = p_ref[0].astype(F32)
        for s in range(1, N_DEV):
            g = g + p_ref[s].astype(F32)
        g = g[:tr, :C]
        g_ref[...] = g
        d_ref[...], nm_ref[...], nv_ref[...] = _adamw(g, w_ref[...], m_ref[...], v_ref[...])

    spec = pl.BlockSpec((tr, C), lambda i: (i, 0))
    return pl.pallas_call(
        body, grid=(R // tr,), name=name,
        in_specs=[pl.BlockSpec((N_DEV, tp, Cp), lambda i: (0, i, 0)), spec, spec, spec],
        out_specs=[spec] * 4, out_shape=[SDS((R, C), F32)] * 4,
        compiler_params=_params(("arbitrary",)),
    )(parts, w, m, v)


SMALL_ROWS, SMALL_COLS = 16, 1024
SMALL_LAYOUT = (
    ("attn_norm_w", 0, 1, 1024, 1024), ("mlp_norm_w", 1, 1, 1024, 1024), ("q_lat_norm_w", 2, 1, 256, 256),
    ("kv_lat_norm_w", 3, 1, 256, 256), ("q_norm_w", 4, 1, 192, 192), ("k_norm_w", 5, 1, 192, 192),
    ("mla_out_norm_w", 6, 4, 128, 128), ("a_log", 10, 1, 128, 4), ("dt_bias", 11, 1, 128, 4),
    ("gdn_norm_w", 12, 1, 128, 128))
LOSS_ENTRY = ("loss", 13, 1, 128, 128)


def _adamw_replicated(parts, ws, ms, vs):
    n = len(SMALL_LAYOUT)

    def body(*refs):
        p_ref = refs[0]
        w_refs, m_refs, v_refs = refs[1:1 + n], refs[1 + n:1 + 2 * n], refs[1 + 2 * n:1 + 3 * n]
        outs = refs[1 + 3 * n:]
        s = p_ref[0]
        for d in range(1, N_DEV):
            s = s + p_ref[d]
        for i, (_, r0, nr, _, pw) in enumerate(SMALL_LAYOUT):
            g = s[r0:r0 + nr, :pw]
            outs[i][...] = g
            outs[n + i][...], outs[2 * n + i][...], outs[3 * n + i][...] = _adamw(
                g, w_refs[i][...], m_refs[i][...], v_refs[i][...])
        _, r0, nr, gw, _ = LOSS_ENTRY
        outs[4 * n][...] = s[r0:r0 + nr, :gw]

    res = pl.pallas_call(
        body, name="adamw_replicated",
        out_shape=[SDS(w.shape, F32) for w in ws] * 4 + [SDS((1, LANES), F32)],
        compiler_params=_params(),
    )(parts, *ws, *ms, *vs)
    return [res[k * n:(k + 1) * n] for k in range(4)], res[4 * n][0, 0]


COPIES_PER_ARRAY = N_DEV - 1


def _two_level_gather(srcs, outs, send_sems, recv_sems, local_sems=None, stage="all"):
    mx, my, mc = lax.axis_index("x"), lax.axis_index("y"), lax.axis_index("c")
    me, sibling = (mx, my, mc), (mx, my, 1 - mc)
    chips = [(1 - mx, my), (mx, 1 - my), (1 - mx, 1 - my)]
    arrays = range(len(srcs))

    def copy(a, k, block, to, src=None):
        px, py, pc = block
        slot = outs[a].at[4 * px + 2 * py + pc]
        sem = a * COPIES_PER_ARRAY + k
        return pltpu.make_async_remote_copy(
            src_ref=slot if src is None else src, dst_ref=slot,
            send_sem=send_sems.at[sem], recv_sem=recv_sems.at[sem], device_id=to, device_id_type=MESH_ID)

    mine = [] if local_sems is None else [
        pltpu.make_async_copy(srcs[a], outs[a].at[4 * mx + 2 * my + mc], local_sems.at[a]) for a in arrays]
    first = []
    for a in arrays:
        first.append(copy(a, 0, me, sibling, src=srcs[a]))
        first += [copy(a, 1 + j, me, (*chip, mc), src=srcs[a]) for j, chip in enumerate(chips)]
    if stage in ("all", "start"):
        for cp in mine + first:
            cp.start()
    if stage in ("all", "finish"):
        forwards = []
        for j, chip in enumerate(chips):
            for a in arrays:
                copy(a, 1 + j, (*chip, mc), me).wait_recv()
                fwd = copy(a, 4 + j, (*chip, mc), sibling)
                fwd.start()
                forwards.append(fwd)
        for a in arrays:
            copy(a, 0, sibling, me).wait_recv()
        for j, chip in enumerate(chips):
            for a in arrays:
                copy(a, 4 + j, (*chip, 1 - mc), me).wait_recv()
        for cp in first + forwards:
            cp.wait_send()
        for cp in mine:
            cp.wait()


def _comm_scratch(n):
    return [pltpu.SemaphoreType.DMA((n * COPIES_PER_ARRAY,)), pltpu.SemaphoreType.DMA((n * COPIES_PER_ARRAY,)),
            pltpu.SemaphoreType.DMA((n,))]


def _any_specs(n):
    return [pl.BlockSpec(memory_space=pl.ANY)] * n


def _gather_weights(shards):
    n = len(shards)

    def body(*refs):
        _two_level_gather(refs[:n], refs[n:2 * n], *refs[2 * n:])

    return pl.pallas_call(
        body, name="gather_weights",
        out_shape=[SDS((N_DEV,) + s.shape, s.dtype) for s in shards],
        in_specs=_any_specs(n), out_specs=_any_specs(n), scratch_shapes=_comm_scratch(n),
    )(*shards)


def _gather_small_grads(gs, loss_lanes):
    gs = list(gs) + [loss_lanes]
    n = len(gs)

    def body(*refs):
        g_refs, out_ref = refs[:n], refs[n]
        tile, send_sems, recv_sems = refs[n + 1:]
        tile[...] = jnp.zeros_like(tile)
        for (_, r0, nr, gw, _), g in zip(SMALL_LAYOUT + (LOSS_ENTRY,), g_refs):
            tile[r0:r0 + nr, 0:gw] = g[...]
        me = 4 * lax.axis_index("x") + 2 * lax.axis_index("y") + lax.axis_index("c")
        out_ref[me] = tile[...]
        _two_level_gather([tile], [out_ref], send_sems, recv_sems)

    return pl.pallas_call(
        body, name="gather_small_grads",
        out_shape=SDS((N_DEV, SMALL_ROWS, SMALL_COLS), F32),
        in_specs=[pl.BlockSpec(memory_space=pltpu.VMEM)] * n,
        out_specs=pl.BlockSpec(memory_space=pltpu.VMEM),
        scratch_shapes=[pltpu.VMEM((SMALL_ROWS, SMALL_COLS), F32),
                        pltpu.SemaphoreType.DMA((COPIES_PER_ARRAY,)), pltpu.SemaphoreType.DMA((COPIES_PER_ARRAY,))],
    )(*gs)


def _exchange_grads(slabs):
    n = len(slabs)

    def body(*refs):
        _exchange(refs[:n], refs[n:2 * n], *refs[2 * n:])

    return pl.pallas_call(
        body, name="exchange_grads",
        out_shape=[SDS(s.shape, s.dtype) for s in slabs],
        in_specs=_any_specs(n), out_specs=_any_specs(n), scratch_shapes=_comm_scratch(n),
    )(*slabs)


class _Transfer:
    def __init__(self, kind, arrays):
        self.kind, self.arrays, self.n = kind, list(arrays), len(arrays)

    def out_shapes(self):
        if self.kind == "gather":
            return [SDS((N_DEV,) + a.shape, a.dtype) for a in self.arrays]
        return [SDS(a.shape, a.dtype) for a in self.arrays]

    def run(self, srcs, outs, sems, stage):
        fn = _two_level_gather if self.kind == "gather" else _exchange
        fn(srcs, outs, *sems, stage=stage)


def _call_beside(body, transfer, *, grid, in_specs, out_specs, out_shape, scratch_shapes, name, semantics, args):
    if transfer is None:
        res = pl.pallas_call(body, grid=grid, in_specs=in_specs, out_specs=out_specs, out_shape=out_shape,
                             scratch_shapes=scratch_shapes, name=name, compiler_params=_params(semantics))(*args)
        return list(res), []
    n_in, n_out, n_s, n = len(in_specs), len(out_specs), len(scratch_shapes), transfer.n

    def wrapped(*refs):
        ins, refs = refs[:n_in], refs[n_in:]
        t_in, refs = refs[:n], refs[n:]
        outs, refs = refs[:n_out], refs[n_out:]
        t_out, refs = refs[:n], refs[n:]
        scratch, sems = refs[:n_s], refs[n_s:]
        first = functools.reduce(jnp.logical_and, [pl.program_id(i) == 0 for i in range(len(grid))])
        last = functools.reduce(jnp.logical_and, [pl.program_id(i) == g - 1 for i, g in enumerate(grid)])

        @pl.when(first)
        def _():
            transfer.run(t_in, t_out, sems, "start")

        body(*ins, *outs, *scratch)

        @pl.when(last)
        def _():
            transfer.run(t_in, t_out, sems, "finish")

    res = pl.pallas_call(
        wrapped, grid=grid, in_specs=list(in_specs) + _any_specs(n), out_specs=list(out_specs) + _any_specs(n),
        out_shape=list(out_shape) + transfer.out_shapes(), scratch_shapes=list(scratch_shapes) + _comm_scratch(n),
        name=name, compiler_params=_params(semantics))(*args, *transfer.arrays)
    return list(res[:n_out]), list(res[n_out:])


EXCHANGE_FLIPS = ((0, 0, 1), (1, 0, 0), (0, 1, 0), (1, 1, 0), (1, 0, 1), (0, 1, 1), (1, 1, 1))


def _exchange(srcs, outs, send_sems, recv_sems, local_sems, stage="all"):
    mx, my, mc = lax.axis_index("x"), lax.axis_index("y"), lax.axis_index("c")
    arrays = range(len(srcs))
    copies = [pltpu.make_async_copy(srcs[a].at[4 * mx + 2 * my + mc], outs[a].at[N_DEV - 1], local_sems.at[a])
              for a in arrays]
    for k, (fx, fy, fc) in enumerate(EXCHANGE_FLIPS):
        px = 1 - mx if fx else mx
        py = 1 - my if fy else my
        pc = 1 - mc if fc else mc
        for a in arrays:
            sem = a * COPIES_PER_ARRAY + k
            copies.append(pltpu.make_async_remote_copy(
                src_ref=srcs[a].at[4 * px + 2 * py + pc], dst_ref=outs[a].at[k],
                send_sem=send_sems.at[sem], recv_sem=recv_sems.at[sem],
                device_id=(px, py, pc), device_id_type=MESH_ID))
    if stage in ("all", "start"):
        for cp in copies:
            cp.start()
    if stage in ("all", "finish"):
        for cp in copies:
            cp.wait()


def _w_in_to_padded(w):
    z = lambda n: jnp.zeros((w.shape[0], n), w.dtype)
    return jnp.concatenate([w[:, O_GQKV:O_GZ], w[:, O_GZ:O_GAB], w[:, O_QLAT:O_KVLAT], w[:, O_KVLAT:O_KPE],
                            w[:, O_KPE:O_GQKV], z(P_GAB - P_KPE - ROPE), w[:, O_GAB:O_END],
                            z(P_WIDTH - P_GAB - (O_END - O_GAB))], axis=1)


def _w_in_from_padded(wp):
    return jnp.concatenate([wp[:, P_QLAT:P_QLAT + 256], wp[:, P_KVLAT:P_KVLAT + 256], wp[:, P_KPE:P_KPE + ROPE],
                            wp[:, P_GQKV:P_GZ], wp[:, P_GZ:P_QLAT], wp[:, P_GAB:P_GAB + (O_END - O_GAB)]], axis=1)


def _w_uq_to_headsplit(w):
    w3 = w.reshape(w.shape[0], MLA_HEADS, QK_DIM)
    return jnp.concatenate([w3[:, :, :NOPE].reshape(w.shape[0], -1), w3[:, :, NOPE:].reshape(w.shape[0], -1)], axis=1)


def _w_uq_from_headsplit(wp):
    n = wp[:, :MLA_HEADS * NOPE].reshape(wp.shape[0], MLA_HEADS, NOPE)
    p = wp[:, MLA_HEADS * NOPE:].reshape(wp.shape[0], MLA_HEADS, ROPE)
    return jnp.concatenate([n, p], axis=2).reshape(wp.shape[0], -1)


def _lane_vec(v4):
    return jnp.pad(v4.reshape(1, -1), ((0, 0), (0, LANES - v4.shape[-1])))


def _local_step(x, positions, target, attn_norm_w, w_in, q_lat_norm_w, w_uq, kv_lat_norm_w, w_ukv, q_norm_w,
                k_norm_w, mla_out_norm_w, conv_w, a_log, dt_bias, gdn_norm_w, w_out, mlp_norm_w, w_up, w_down,
                late_shards=None, exchange=False):
    B, S, D = x.shape
    T = B * S
    x2 = x.reshape(T, D)
    t2 = target.reshape(T, D)
    half = ROPE // 2
    inv_freq = ROPE_THETA ** (-jnp.arange(half, dtype=F32) / half)
    ang = positions.reshape(T, 1).astype(F32) * inv_freq
    cosf = jnp.concatenate([jnp.cos(ang)] * 2, axis=-1)
    sinf = jnp.concatenate([jnp.sin(ang)] * 2, axis=-1)
    w_in_p = _w_in_to_padded(w_in)
    w_uq_p = _w_uq_to_headsplit(w_uq)
    alog_l, dt_l = _lane_vec(a_log), _lane_vec(dt_bias)
    w_an, w_qln, w_kvln, qnw, knw, w_mn, gdn_w = (
        attn_norm_w, q_lat_norm_w, kv_lat_norm_w, q_norm_w, k_norm_w, mlp_norm_w, gdn_norm_w)

    proj, xn = _in_proj(x2, w_an, w_in_p)
    gather = None if late_shards is None else _Transfer("gather", late_shards[:1])
    (q4, k4, v4), late = _mla_pre(proj, cosf, sinf, w_qln, w_kvln, w_uq_p, w_ukv, qnw, knw, gather)
    if late:
        w_out = late[0].reshape(-1, D)
    (o_mla, lse), _ = _attn_fwd(q4, k4, v4, B, S)
    qg, kg, vg, gates = _gdn_pre(proj, conv_w, alog_l, dt_l, S)
    gather = None if late_shards is None else _Transfer("gather", late_shards[1:])
    (o_gdn, states, ainv, u4, w4), late = _gdn_fwd(qg, kg, vg, gates, B, S, gather)
    if late:
        w_up, w_down = late[0], late[1].reshape(-1, D)
    h2, mix = _mix_out(o_mla, o_gdn, proj, x2, mla_out_norm_w, gdn_w, w_out)
    up, hn, dy, sq = _mlp_fwd(h2, w_mn, w_up, w_down, t2)
    loss = (0.5 / D) * jnp.sum(sq[:, 0, 0])

    dh, dhb, dup, act, dyb, d_mlp_norm = _mlp_bwd(dy, up, h2, w_mn, w_up, w_down)
    g_w_down = _wgrad(act, dyb, "wgrad_down")
    g_w_up = _wgrad(hn, dup, "wgrad_up", column_shards=True)
    do_mla, do_gdn, dz, d_mla_w, d_gdn_w = _mix_bwd(dhb, o_mla, o_gdn, proj, mla_out_norm_w, gdn_w, w_out)
    g_w_out = _wgrad(mix, dhb, "wgrad_out")
    first = ("w_down",)
    second = ("w_out",)
    third = ("w_up", "w_uq", "w_ukv")
    mats = dict(w_up=g_w_up, w_down=g_w_down, w_out=g_w_out)

    def sending(names):
        return _Transfer("exchange", [_slabs(n, mats[n]) for n in names]) if exchange else None

    (dq4, dk4, dv4), got = _attn_bwd(q4, k4, v4, do_mla, o_mla, lse, B, S, sending(first))
    mats.update(zip(first, got))
    (dql, dkvl, dkpe, dqraw, dkvraw, qn, kvn, d_wqln, d_wkvln, d_qnw, d_knw), got = _mla_pre_bwd(
        proj, cosf, sinf, w_qln, w_kvln, w_uq_p, w_ukv, qnw, knw, dq4, dk4, dv4, sending(second))
    mats.update(zip(second, got))
    mats.update(w_uq=_wgrad(qn, dqraw, "wgrad_uq"), w_ukv=_wgrad(kvn, dkvraw, "wgrad_ukv"))
    (dqg, dkg, dvg, dgb4), got = _gdn_bwd(qg, kg, vg, gates, states, ainv, u4, w4, do_gdn, B, S, sending(third))
    mats.update(zip(third, got))
    dc, dgab, g_conv, d_alog, d_dt = _gdn_pre_bwd(proj, conv_w, alog_l, dt_l, dqg, dkg, dvg, dgb4, S)
    dgqkv = _conv_bwd_input(dc, conv_w, S)
    grad_x2, dproj, d_attn_norm = _in_proj_bwd(dgqkv, dz, dql, dkvl, dkpe, dgab, w_in_p, dh, x2, w_an)
    mats.update(w_in=_wgrad(xn, dproj, "wgrad_in"), conv_w=g_conv)
    if exchange:
        last = ("w_in", "conv_w")
        mats.update(zip(last, _exchange_grads([_slabs(n, mats[n]) for n in last])))
    small = dict(attn_norm_w=d_attn_norm, mlp_norm_w=d_mlp_norm, q_lat_norm_w=d_wqln, kv_lat_norm_w=d_wkvln,
                 q_norm_w=d_qnw, k_norm_w=d_knw, mla_out_norm_w=d_mla_w, a_log=d_alog, dt_bias=d_dt,
                 gdn_norm_w=d_gdn_w)
    return loss, grad_x2.reshape(B, S, D), mats, [small[n] for n, *_ in SMALL_LAYOUT]


BIG = ("w_in", "w_uq", "w_ukv", "conv_w", "w_out", "w_up", "w_down")
ALL_W = ("attn_norm_w", "w_in", "q_lat_norm_w", "w_uq", "kv_lat_norm_w", "w_ukv", "q_norm_w", "k_norm_w",
         "mla_out_norm_w", "conv_w", "a_log", "dt_bias", "gdn_norm_w", "w_out", "mlp_norm_w", "w_up", "w_down")
WIRE_SHAPE = {"w_in": (1024, 384), "w_uq": (256, 128), "conv_w": (16, 256)}


def _pad2(a, rows, cols):
    return jnp.pad(a, [(0, 0)] * (a.ndim - 2) + [(0, rows - a.shape[-2]), (0, cols - a.shape[-1])])


def _cols_to_full(stack, cols):
    return jnp.moveaxis(stack[:, :, :cols], 0, 1).reshape(stack.shape[1], N_DEV * cols)


def _full_to_cols(full, wire_cols):
    r, n = full.shape
    return _pad2(jnp.moveaxis(full.reshape(r, N_DEV, n // N_DEV), 1, 0), r, wire_cols)


def _slabs(name, g):
    if name == "w_in":
        return _full_to_cols(_w_in_from_padded(g), WIRE_SHAPE["w_in"][1])
    if name == "w_uq":
        return _full_to_cols(_w_uq_from_headsplit(g), WIRE_SHAPE["w_uq"][1])
    if name == "w_ukv":
        return _full_to_cols(g, g.shape[1] // N_DEV)
    if name == "conv_w":
        return _pad2(_full_to_cols(g.astype(WIRE_DTYPE), g.shape[1] // N_DEV), *WIRE_SHAPE["conv_w"])
    if name == "w_up":
        return g
    return g.reshape(N_DEV, -1, g.shape[-1])


def kernel(x, positions, attn_norm_w, w_in, q_lat_norm_w, w_uq, kv_lat_norm_w, w_ukv, q_norm_w, k_norm_w, mla_out_norm_w, conv_w, a_log, dt_bias, gdn_norm_w, w_out, mlp_norm_w, w_up, w_down, loss_target, m_attn_norm_w, m_w_in, m_q_lat_norm_w, m_w_uq, m_kv_lat_norm_w, m_w_ukv, m_q_norm_w, m_k_norm_w, m_mla_out_norm_w, m_conv_w, m_a_log, m_dt_bias, m_gdn_norm_w, m_w_out, m_mlp_norm_w, m_w_up, m_w_down, v_attn_norm_w, v_w_in, v_q_lat_norm_w, v_w_uq, v_kv_lat_norm_w, v_w_ukv, v_q_norm_w, v_k_norm_w, v_mla_out_norm_w, v_conv_w, v_a_log, v_dt_bias, v_gdn_norm_w, v_w_out, v_mlp_norm_w, v_w_up, v_w_down):
    env = dict(locals())
    W = {n: env[n][0] for n in ALL_W}
    Mo = {n: env["m_" + n][0] for n in ALL_W}
    Vo = {n: env["v_" + n][0] for n in ALL_W}

    two_d = lambda a: a.reshape(1, -1) if a.ndim == 1 else a
    D = x.shape[-1]

    s_in, s_uq, s_ukv, s_conv = _gather_weights([
        _pad2(W["w_in"].astype(WIRE_DTYPE), *WIRE_SHAPE["w_in"]),
        _pad2(W["w_uq"].astype(WIRE_DTYPE), *WIRE_SHAPE["w_uq"]),
        W["w_ukv"].astype(WIRE_DTYPE), _pad2(W["conv_w"], *WIRE_SHAPE["conv_w"])])
    late = [W["w_out"].astype(WIRE_DTYPE), W["w_up"].astype(WIRE_DTYPE), W["w_down"].astype(WIRE_DTYPE)]

    loss, grad_x, parts, gs = _local_step(
        x, positions, loss_target, two_d(W["attn_norm_w"]), _cols_to_full(s_in, W["w_in"].shape[1]),
        two_d(W["q_lat_norm_w"]), _cols_to_full(s_uq, W["w_uq"].shape[1]), two_d(W["kv_lat_norm_w"]),
        _cols_to_full(s_ukv, W["w_ukv"].shape[1]), two_d(W["q_norm_w"]), two_d(W["k_norm_w"]),
        W["mla_out_norm_w"], _cols_to_full(s_conv[:, :CONV_W], W["conv_w"].shape[1]), two_d(W["a_log"]),
        two_d(W["dt_bias"]), two_d(W["gdn_norm_w"]), None, two_d(W["mlp_norm_w"]), None, None,
        late_shards=late, exchange=True)
    done = {n: _reduce_adamw(parts[n], W[n], Mo[n], Vo[n], "adamw_" + n) for n in BIG}
    names = [n for n, *_ in SMALL_LAYOUT]
    tiles = _gather_small_grads(gs, jnp.full((1, LANES), loss, F32))
    small, loss = _adamw_replicated(tiles, [two_d(W[n]) for n in names], [two_d(Mo[n]) for n in names],
                                    [two_d(Vo[n]) for n in names])
    for i, n in enumerate(names):
        done[n] = [small[kind][i] for kind in range(4)]
    res = [done[n][kind].reshape(env[n].shape) for kind in range(4) for n in ALL_W]
    return (loss, grad_x, *res)
```

```python
import functools

import jax
import jax.numpy as jnp
from jax import lax
from jax.experimental import pallas as pl
from jax.experimental.pallas import tpu as pltpu

F32 = jnp.float32
MXU_DTYPE = jnp.bfloat16
WIRE_DTYPE = jnp.bfloat16
SDS = jax.ShapeDtypeStruct
HIGHEST = lax.Precision.HIGHEST
MESH_ID = pl.DeviceIdType.MESH

D_MODEL = 1024
MLA_HEADS = 4
Q_LORA = 256
KV_LORA = 256
NOPE = 128
ROPE = 64
QK_DIM = NOPE + ROPE
V_DIM = 128
ROPE_THETA = 10000.0
GDN_HEADS = 4
GDN_DIM = 128
GDN_WIDTH = GDN_HEADS * GDN_DIM
CONV_W = 4
CHUNK = 64
D_FF = 4 * D_MODEL
EPS = 1e-6
ATT_SCALE = QK_DIM ** -0.5
GDN_QSCALE = GDN_DIM ** -0.5
N_DEV = 8
ATTN_BLOCK = 512
ATTN_CHAINS = 2
MLP_FWD_SHARDS = 4
MLP_BWD_SHARDS = 4

ADAM_LR = 0.001
ADAM_B1 = 0.9
ADAM_B2 = 0.999
ADAM_EPS = 1e-08
ADAM_WD = 0.01
ADAM_STEP = 10

LANES = 128
SUBLANES = 8
VMEM_LIMIT = 60 * 1024 * 1024

P_GQKV, P_GZ, P_QLAT, P_KVLAT, P_KPE, P_GAB = 0, 1536, 2048, 2304, 2560, 2688
P_WIDTH = 2816
O_QLAT, O_KVLAT, O_KPE, O_GQKV, O_GZ, O_GAB, O_END = 0, 256, 512, 576, 2112, 2624, 2632


def _params(sem=None, vmem=VMEM_LIMIT):
    kw = dict(vmem_limit_bytes=vmem)
    if sem is not None:
        kw["dimension_semantics"] = sem
    return pltpu.CompilerParams(**kw)


def _mm(a, b):
    return jnp.dot(a.astype(MXU_DTYPE), b.astype(MXU_DTYPE), preferred_element_type=F32)


def _mm_nt(a, b):
    return lax.dot_general(a.astype(MXU_DTYPE), b.astype(MXU_DTYPE), (((1,), (1,)), ((), ())),
                           preferred_element_type=F32)


def _mm_tn(a, b):
    return lax.dot_general(a.astype(MXU_DTYPE), b.astype(MXU_DTYPE), (((0,), (0,)), ((), ())),
                           preferred_element_type=F32)


def _split(a):
    hi = a.astype(MXU_DTYPE)
    return hi, (a - hi.astype(F32)).astype(MXU_DTYPE)


def _mm_split(a, b):
    (ah, al), (bh, bl) = a, b
    dot = lambda x, y: jnp.dot(x, y, preferred_element_type=F32)
    if MXU_DTYPE == F32:
        return dot(ah, bh)
    return dot(ah, bh) + dot(ah, bl) + dot(al, bh)


def _mm_exact(a, b):
    return _mm_split(_split(a), _split(b))


def _row_sum(v, on_mxu=False):
    if not on_mxu:
        return jnp.sum(v, axis=-1, keepdims=True)
    d = v.shape[-1]
    ones = jnp.ones((d, LANES), MXU_DTYPE)
    s = sum(jnp.dot(p, ones, preferred_element_type=F32) for p in _split(v))
    return s[:, :d] if d <= LANES else jnp.tile(s, (1, d // LANES))


def _rms(x, w, on_mxu=False):
    r = lax.rsqrt(_row_sum(x * x, on_mxu) * (1.0 / x.shape[-1]) + EPS)
    return x * r * w, r


def _rms_bwd(dy, x, w, r, on_mxu=False):
    xh = x * r
    dyw = dy * w
    dx = r * (dyw - xh * (_row_sum(dyw * xh, on_mxu) * (1.0 / x.shape[-1])))
    dw = jnp.sum(dy * xh, axis=0, keepdims=True)
    return dx, dw


def _l2n(x, scale):
    return x * (lax.rsqrt(_row_sum(x * x) + EPS) * scale)


def _l2n_bwd(dy, x, scale):
    r = lax.rsqrt(_row_sum(x * x) + EPS)
    xh = x * r
    return (scale * r) * (dy - xh * _row_sum(dy * xh))


def _rot(t):
    return jnp.concatenate([-t[:, ROPE // 2:], t[:, :ROPE // 2]], axis=-1)


def _rot_t(t):
    return jnp.concatenate([t[:, ROPE // 2:], -t[:, :ROPE // 2]], axis=-1)


def _rope(t, cos, sin):
    return t * cos + _rot(t) * sin


def _rope_bwd(d, cos, sin):
    return d * cos + _rot_t(d * sin)


def _sigmoid(x):
    return jax.nn.sigmoid(x)


def _shift_down(x, halo, j):
    if j == 0:
        return x
    xr = pltpu.roll(x, j, 0)
    hr = pltpu.roll(halo, j, 0)
    row = lax.broadcasted_iota(jnp.int32, halo.shape, 0)
    top = jnp.where(row < j, hr, xr[:SUBLANES])
    return jnp.concatenate([top, xr[SUBLANES:]], axis=0)


def _shift_up(x, nxt, j):
    if j == 0:
        return x
    n = x.shape[0]
    xr = pltpu.roll(x, n - j, 0)
    nr = pltpu.roll(nxt, SUBLANES - j, 0)
    row = lax.broadcasted_iota(jnp.int32, nxt.shape, 0)
    bot = jnp.where(row >= SUBLANES - j, nr, xr[n - SUBLANES:])
    return jnp.concatenate([xr[:n - SUBLANES], bot], axis=0)


def _chunk_cumsum(y, row_in_chunk):
    s = 1
    while s < CHUNK:
        y = y + jnp.where(row_in_chunk >= s, pltpu.roll(y, s, 0), 0.0)
        s *= 2
    return y


def _chunk_rev_cumsum(y, row_in_chunk):
    n = y.shape[0]
    s = 1
    while s < CHUNK:
        y = y + jnp.where(row_in_chunk + s < CHUNK, pltpu.roll(y, n - s, 0), 0.0)
        s *= 2
    return y


def _together(generators):
    alive = list(generators)
    while alive:
        nxt = []
        for g in alive:
            try:
                next(g)
                nxt.append(g)
            except StopIteration:
                pass
        alive = nxt
        yield


def _lockstep(generators):
    for _ in _together(generators):
        pass


def _pick_lane(tile, lane, idx):
    return jnp.sum(jnp.where(lane == idx, tile, 0.0), axis=-1, keepdims=True)


def _divisor_tile(n, cap, unit=LANES):
    best = unit
    t = unit
    while t <= min(n, cap):
        if n % t == 0:
            best = t
        t += unit
    return n if n <= cap else best


def _in_proj(x2, w_an, w_in_p):
    T, D = x2.shape
    N = w_in_p.shape[1]
    tm = min(512, T)

    def body(x_ref, wn_ref, w_ref, proj_ref, xn_ref):
        xn, _ = _rms(x_ref[...], wn_ref[...])
        xn = xn.astype(MXU_DTYPE)
        xn_ref[...] = xn
        proj_ref[...] = jnp.dot(xn, w_ref[...], preferred_element_type=F32)

    return pl.pallas_call(
        body, grid=(T // tm,), name="in_proj",
        in_specs=[pl.BlockSpec((tm, D), lambda i: (i, 0)), pl.BlockSpec((1, D), lambda i: (0, 0)),
                  pl.BlockSpec((D, N), lambda i: (0, 0))],
        out_specs=[pl.BlockSpec((tm, N), lambda i: (i, 0)), pl.BlockSpec((tm, D), lambda i: (i, 0))],
        out_shape=[SDS((T, N), F32), SDS((T, D), MXU_DTYPE)],
        compiler_params=_params(("arbitrary",)),
    )(x2, w_an, w_in_p)


def _mla_pre(proj, cosf, sinf, w_qln, w_kvln, w_uq_p, w_ukv, qnw, knw, transfer=None):
    T = proj.shape[0]
    tm = min(256, T)
    H = MLA_HEADS

    def body(ql_ref, kvl_ref, kpe_ref, cos_ref, sin_ref, wq_ref, wkv_ref, uq_ref, ukv_ref, qnw_ref, knw_ref,
             q_out, k_out, v_out):
        rms = functools.partial(_rms, on_mxu=True)
        cos, sin = cos_ref[...], sin_ref[...]
        qnw_, knw_ = qnw_ref[...], knw_ref[...]
        qn, _ = rms(ql_ref[...], wq_ref[...])
        kvn, _ = rms(kvl_ref[...], wkv_ref[...])
        qraw = _mm(qn, uq_ref[...])
        kvraw = _mm(kvn, ukv_ref[...])
        kpe = _rope(rms(kpe_ref[...][:, :ROPE], knw_[:, NOPE:])[0], cos, sin)
        for h in range(H):
            qn_h = rms(qraw[:, h * NOPE:(h + 1) * NOPE], qnw_[:, :NOPE])[0]
            qp_h = _rope(rms(qraw[:, H * NOPE + h * ROPE:H * NOPE + (h + 1) * ROPE], qnw_[:, NOPE:])[0], cos, sin)
            q_out[h] = (jnp.concatenate([qn_h, qp_h], axis=-1) * ATT_SCALE).astype(MXU_DTYPE)
            kn_h = rms(kvraw[:, h * 256:h * 256 + NOPE], knw_[:, :NOPE])[0]
            k_out[h] = jnp.concatenate([kn_h, kpe], axis=-1).astype(MXU_DTYPE)
            v_out[h] = kvraw[:, h * 256 + NOPE:(h + 1) * 256].astype(MXU_DTYPE)

    full = lambda a: pl.BlockSpec(a.shape, lambda i: (0,) * a.ndim)
    return _call_beside(
        body, transfer, grid=(T // tm,), name="mla_pre", scratch_shapes=[], semantics=("arbitrary",),
        args=(proj, proj, proj, cosf, sinf, w_qln, w_kvln, w_uq_p, w_ukv, qnw, knw),
        in_specs=[pl.BlockSpec((tm, 256), lambda i: (i, P_QLAT // 256)),
                  pl.BlockSpec((tm, 256), lambda i: (i, P_KVLAT // 256)),
                  pl.BlockSpec((tm, 128), lambda i: (i, P_KPE // 128)),
                  pl.BlockSpec((tm, ROPE), lambda i: (i, 0)), pl.BlockSpec((tm, ROPE), lambda i: (i, 0)),
                  full(w_qln), full(w_kvln), full(w_uq_p), full(w_ukv), full(qnw), full(knw)],
        out_specs=[pl.BlockSpec((H, tm, QK_DIM), lambda i: (0, i, 0)),
                   pl.BlockSpec((H, tm, QK_DIM), lambda i: (0, i, 0)),
                   pl.BlockSpec((H, tm, V_DIM), lambda i: (0, i, 0))],
        out_shape=[SDS((H, T, QK_DIM), MXU_DTYPE), SDS((H, T, QK_DIM), MXU_DTYPE), SDS((H, T, V_DIM), MXU_DTYPE)])


def _attn_fwd(q4, k4, v4, B, S, transfer=None):
    H = MLA_HEADS
    bq = min(ATTN_BLOCK, S)
    nq = S // bq
    rows = bq // ATTN_CHAINS

    def body(q_ref, k_ref, v_ref, o_ref, lse_ref):
        col = lax.broadcasted_iota(jnp.int32, (rows, bq), 1)
        row = lax.broadcasted_iota(jnp.int32, (rows, bq), 0)

        def q_step(qi, carry):
            qs = pl.multiple_of(qi * bq, bq)
            qsub = [q_ref[0, pl.ds(qs + j * rows, rows), :] for j in range(ATTN_CHAINS)]

            def k_block(ks, cs, diagonal):
                k = k_ref[0, pl.ds(ks, bq), :]
                v = v_ref[0, pl.ds(ks, bq), :]
                out = [None] * ATTN_CHAINS

                def chain(j):
                    m, l, acc = cs[j]
                    s = _mm_nt(qsub[j], k)
                    yield
                    if diagonal:
                        s = jnp.where(col <= row + j * rows, s, -jnp.inf)
                    m_new = jnp.maximum(m, jnp.max(s, axis=-1, keepdims=True))
                    p = jnp.exp(s - m_new)
                    a = jnp.exp(m - m_new)
                    l_new = a * l + jnp.sum(p, axis=-1, keepdims=True)
                    yield
                    out[j] = (m_new, l_new, a * acc + _mm(p, v))

                _lockstep([chain(j) for j in range(ATTN_CHAINS)])
                return tuple(out)

            init = tuple((jnp.full((rows, 1), -jnp.inf, F32), jnp.zeros((rows, 1), F32),
                          jnp.zeros((rows, V_DIM), F32)) for _ in range(ATTN_CHAINS))
            cs = lax.fori_loop(0, qi, lambda kj, c: k_block(pl.multiple_of(kj * bq, bq), c, False), init)
            for j, (m, l, acc) in enumerate(k_block(qs, cs, True)):
                o_ref[0, pl.ds(qs + j * rows, rows), :] = acc / l
                lse_ref[0, pl.ds(qs + j * rows, rows), :] = m + jnp.log(l)
            return carry

        lax.fori_loop(0, nq, q_step, 0)

    spec = lambda d: pl.BlockSpec((1, S, d), lambda h, b: (h, b, 0))
    return _call_beside(
        body, transfer, grid=(H, B), name="attn_fwd",
        in_specs=[spec(QK_DIM), spec(QK_DIM), spec(V_DIM)],
        out_specs=[spec(V_DIM), spec(1)],
        out_shape=[SDS((H, B * S, V_DIM), F32), SDS((H, B * S, 1), F32)],
        scratch_shapes=[], semantics=("arbitrary", "arbitrary"), args=(q4, k4, v4))


def _conv_taps(u, halo, w):
    sh = [_shift_down(u, halo, j) for j in range(CONV_W)]
    c = w[0:1] * sh[3] + w[1:2] * sh[2] + w[2:3] * sh[1] + w[3:4] * sh[0]
    return c, sh


def _gate_values(gab, alog_l, dt_l, lane):
    g = -jnp.exp(alog_l) * jax.nn.softplus(gab + dt_l)
    g = jnp.where(lane < GDN_HEADS, g, 0.0)
    beta = jnp.where((lane >= GDN_HEADS) & (lane < 2 * GDN_HEADS), _sigmoid(gab), 0.0)
    return g, beta


def _gdn_pre(proj, conv_w, alog_l, dt_l, S):
    T = proj.shape[0]
    tm = min(256, T)
    tiles_per_seq = S // tm
    C3 = 3 * GDN_WIDTH
    H = GDN_HEADS

    def body(u_ref, halo_ref, gab_ref, w_ref, alog_ref, dt_ref, q_out, k_out, v_out, gates_out):
        i = pl.program_id(0)
        halo = jnp.where(i % tiles_per_seq == 0, 0.0, halo_ref[...])
        c, _ = _conv_taps(u_ref[...], halo, w_ref[...])
        a = c * _sigmoid(c)
        for h in range(H):
            xq = a[:, h * GDN_DIM:(h + 1) * GDN_DIM]
            xk = a[:, GDN_WIDTH + h * GDN_DIM:GDN_WIDTH + (h + 1) * GDN_DIM]
            q_out[h] = _l2n(xq, GDN_QSCALE)
            k_out[h] = _l2n(xk, 1.0)
            v_out[h] = a[:, 2 * GDN_WIDTH + h * GDN_DIM:2 * GDN_WIDTH + (h + 1) * GDN_DIM]
        lane = lax.broadcasted_iota(jnp.int32, (tm, LANES), 1)
        ric = lax.broadcasted_iota(jnp.int32, (tm, LANES), 0) % CHUNK
        g, beta = _gate_values(gab_ref[...], alog_ref[...], dt_ref[...], lane)
        gates_out[...] = _chunk_cumsum(g, ric) + beta

    hspec = pl.BlockSpec((H, tm, GDN_DIM), lambda i: (0, i, 0))
    return pl.pallas_call(
        body, grid=(T // tm,), name="gdn_pre",
        in_specs=[pl.BlockSpec((tm, C3), lambda i: (i, 0)),
                  pl.BlockSpec((SUBLANES, C3), lambda i: (jnp.maximum(i * (tm // SUBLANES) - 1, 0), 0)),
                  pl.BlockSpec((tm, LANES), lambda i: (i, P_GAB // LANES)),
                  pl.BlockSpec((CONV_W, C3), lambda i: (0, 0)),
                  pl.BlockSpec((1, LANES), lambda i: (0, 0)), pl.BlockSpec((1, LANES), lambda i: (0, 0))],
        out_specs=[hspec, hspec, hspec, pl.BlockSpec((tm, LANES), lambda i: (i, 0))],
        out_shape=[SDS((H, T, GDN_DIM), F32)] * 3 + [SDS((T, LANES), F32)],
        compiler_params=_params(("arbitrary",)),
    )(proj, proj, proj, conv_w, alog_l, dt_l)


def _unit_lower_inverses(Ls, eye):
    Ps = [eye - L for L in Ls]
    Ms = [_split(-L) for L in Ls]
    for _ in range(5):
        sq = [_mm_split(m, m) for m in Ms]
        Ms = [_split(s) for s in sq]
        Ps = [p + _mm_split(_split(p), m) for p, m in zip(Ps, Ms)]
    return Ps


def _chunk_decays(gt, lane, h, ri, ci, rcol):
    Gc = _pick_lane(gt, lane, h)
    bt = _pick_lane(gt, lane, h + GDN_HEADS)
    Gb = jnp.broadcast_to(Gc, (CHUNK, CHUNK))
    Gam = jnp.where(ri >= ci, jnp.exp(Gb - Gb.T), 0.0)
    Gl = jnp.sum(jnp.where(rcol == CHUNK - 1, Gc, 0.0), axis=0, keepdims=True)
    return Gc, bt, Gam, jnp.exp(Gc), jnp.exp(Gl - Gc), jnp.exp(Gl)


GDN_FWD_UNROLL = 16
GDN_BWD_UNROLL = 8
GDN_RECUR_STEPS_PER_STAGE = 2


def _gdn_fwd(qg, kg, vg, gates, B, S, transfer=None):
    H, D, C = GDN_HEADS, GDN_DIM, CHUNK
    NC = S // C
    U = GDN_FWD_UNROLL if NC % GDN_FWD_UNROLL == 0 else 1
    NG = NC // U

    def body(q_ref, k_ref, v_ref, g_ref, o_ref, st_ref, ai_ref, u_ref, w_ref, q2_s, au_s, bc_s, w2_s, el_s):
        h = pl.program_id(0)
        lane = lax.broadcasted_iota(jnp.int32, (C, LANES), 1)
        ri = lax.broadcasted_iota(jnp.int32, (C, C), 0)
        ci = lax.broadcasted_iota(jnp.int32, (C, C), 1)
        rcol = lax.broadcasted_iota(jnp.int32, (C, 1), 0)
        eye = (ri == ci).astype(F32)

        def group(gi, c):
            ns = [gi * U + j for j in range(U)]
            css = [pl.multiple_of(n * C, C) for n in ns]
            qs = [q_ref[0, pl.ds(cs, C), :] for cs in css]
            ks = [k_ref[0, pl.ds(cs, C), :] for cs in css]
            vs = [v_ref[0, pl.ds(cs, C), :] for cs in css]
            decs = [_chunk_decays(g_ref[pl.ds(cs, C), :], lane, h, ri, ci, rcol) for cs in css]
            qks = [_mm_nt(jnp.concatenate([q, k], axis=0), k) for q, k in zip(qs, ks)]
            ainvs = _unit_lower_inverses(
                [jnp.where(ri > ci, d[1] * qk[C:] * d[2], 0.0) for qk, d in zip(qks, decs)], eye)
            sols = [_mm_exact(a, jnp.concatenate([v * d[1], k * (d[1] * d[3])], axis=-1))
                    for a, k, v, d in zip(ainvs, ks, vs, decs)]
            atuw = [_mm(qk[:C] * d[2], sol) for qk, d, sol in zip(qks, decs, sols)]
            kduw = [_mm_tn(k * d[4], sol) for k, d, sol in zip(ks, decs, sols)]
            for n, cs, q, a, sol, au, ku, (Gc, bt, Gam, e, f, eL) in zip(ns, css, qs, ainvs, sols, atuw, kduw, decs):
                u_ref[0, pl.ds(cs, C), :] = sol[:, :D]
                w_ref[0, pl.ds(cs, C), :] = sol[:, D:]
                au_s[pl.ds(cs, C), :] = au[:, :D]
                q2_s[pl.ds(cs, C), :] = q * e - au[:, D:]
                bc_s[n] = ku[:, :D]
                w2_s[n] = ku[:, D:]
                el_s[n] = jnp.broadcast_to(eL, (SUBLANES, LANES))
                ai_ref[0, n] = a.T
            return c

        lax.fori_loop(0, NG, group, 0)

        def step(n, S_):
            cs = pl.multiple_of(n * C, C)
            o_ref[0, pl.ds(cs, C), :] = _mm(q2_s[pl.ds(cs, C), :], S_) + au_s[pl.ds(cs, C), :]
            st_ref[0, n] = S_
            return S_ * el_s[n, 0:1, :] + bc_s[n] - _mm(w2_s[n], S_)

        lax.fori_loop(0, NC, step, jnp.zeros((D, D), F32))

    spec = pl.BlockSpec((1, S, D), lambda h, b: (h, b, 0))
    return _call_beside(
        body, transfer, grid=(H, B), name="gdn_fwd",
        in_specs=[spec, spec, spec, pl.BlockSpec((S, LANES), lambda h, b: (b, 0))],
        out_specs=[spec, pl.BlockSpec((1, NC, D, D), lambda h, b: (h, b, 0, 0)),
                   pl.BlockSpec((1, NC, C, C), lambda h, b: (h, b, 0, 0)), spec, spec],
        out_shape=[SDS((H, B * S, D), F32), SDS((H, B * NC, D, D), F32), SDS((H, B * NC, C, C), F32),
                   SDS((H, B * S, D), F32), SDS((H, B * S, D), F32)],
        scratch_shapes=[pltpu.VMEM((S, D), F32), pltpu.VMEM((S, D), F32), pltpu.VMEM((NC, D, D), F32),
                        pltpu.VMEM((NC, D, D), F32), pltpu.VMEM((NC, SUBLANES, LANES), F32)],
        semantics=("arbitrary", "arbitrary"), args=(qg, kg, vg, gates))


def _mix_out(o_mla, o_gdn, proj, x2, mla_w, gdn_w, w_out):
    T, D = x2.shape
    tm = min(512, T)
    H = MLA_HEADS

    def body(om_ref, og_ref, z_ref, x_ref, mw_ref, gw_ref, w_ref, h_ref, mix_ref):
        z = z_ref[...]
        parts = [_rms(om_ref[h], mw_ref[h:h + 1, :])[0] for h in range(H)]
        for h in range(GDN_HEADS):
            zh = z[:, h * GDN_DIM:(h + 1) * GDN_DIM]
            parts.append(_rms(og_ref[h], gw_ref[...])[0] * (zh * _sigmoid(zh)))
        mix = jnp.concatenate(parts, axis=-1).astype(MXU_DTYPE)
        mix_ref[...] = mix
        h_ref[...] = x_ref[...] + jnp.dot(mix, w_ref[...], preferred_element_type=F32)

    hspec = pl.BlockSpec((H, tm, V_DIM), lambda i: (0, i, 0))
    return pl.pallas_call(
        body, grid=(T // tm,), name="mix_out",
        in_specs=[hspec, hspec, pl.BlockSpec((tm, GDN_WIDTH), lambda i: (i, P_GZ // GDN_WIDTH)),
                  pl.BlockSpec((tm, D), lambda i: (i, 0)),
                  pl.BlockSpec((H, V_DIM), lambda i: (0, 0)), pl.BlockSpec((1, GDN_DIM), lambda i: (0, 0)),
                  pl.BlockSpec((D, D), lambda i: (0, 0))],
        out_specs=[pl.BlockSpec((tm, D), lambda i: (i, 0)), pl.BlockSpec((tm, D), lambda i: (i, 0))],
        out_shape=[SDS((T, D), F32), SDS((T, D), MXU_DTYPE)],
        compiler_params=_params(("arbitrary",)),
    )(o_mla, o_gdn, proj, x2, mla_w, gdn_w, w_out)


def _mlp_fwd(h2, w_mn, w_up, w_down, target):
    T, D = h2.shape
    ns, _, ts = w_up.shape
    F = ns * ts
    tm = min(512, T)
    G = MLP_FWD_SHARDS
    tf, nf = G * ts, ns // G

    def body(h_ref, wn_ref, up_w, down_w, t_ref, up_ref, hn_ref, dy_ref, loss_ref, y_acc):
        j = pl.program_id(1)

        @pl.when(j == 0)
        def _():
            hn_ref[...] = _rms(h_ref[...], wn_ref[...])[0].astype(MXU_DTYPE)
            y_acc[...] = h_ref[...]

        parts = []
        for c in range(G):
            up = jnp.dot(hn_ref[...], up_w[c], preferred_element_type=F32)
            up_ref[:, c * ts:(c + 1) * ts] = up
            r = jnp.maximum(up, 0.0)
            parts.append(_mm(r * r, down_w[c * ts:(c + 1) * ts, :]))
        y_acc[...] += functools.reduce(jnp.add, parts)

        @pl.when(j == nf - 1)
        def _():
            err = y_acc[...] - t_ref[...]
            dy_ref[...] = err / D
            loss_ref[...] = jnp.full((1, SUBLANES, LANES), jnp.sum(err * err), F32)

    return pl.pallas_call(
        body, grid=(T // tm, nf), name="mlp_fwd",
        in_specs=[pl.BlockSpec((tm, D), lambda i, j: (i, 0)), pl.BlockSpec((1, D), lambda i, j: (0, 0)),
                  pl.BlockSpec((G, D, ts), lambda i, j: (j, 0, 0)), pl.BlockSpec((tf, D), lambda i, j: (j, 0)),
                  pl.BlockSpec((tm, D), lambda i, j: (i, 0))],
        out_specs=[pl.BlockSpec((tm, tf), lambda i, j: (i, j)), pl.BlockSpec((tm, D), lambda i, j: (i, 0)),
                   pl.BlockSpec((tm, D), lambda i, j: (i, 0)),
                   pl.BlockSpec((1, SUBLANES, LANES), lambda i, j: (i, 0, 0))],
        out_shape=[SDS((T, F), F32), SDS((T, D), MXU_DTYPE), SDS((T, D), F32),
                   SDS((T // tm, SUBLANES, LANES), F32)],
        scratch_shapes=[pltpu.VMEM((tm, D), F32)],
        compiler_params=_params(("arbitrary", "arbitrary")),
    )(h2, w_mn, w_up, w_down, target)


def _mlp_bwd(dy, up, h2, w_mn, w_up, w_down):
    T, D = h2.shape
    ns, _, ts = w_up.shape
    F = ns * ts
    tm = min(512, T)
    G = MLP_BWD_SHARDS
    tf, nf = G * ts, ns // G

    def body(dy_ref, up_ref, h_ref, wn_ref, up_w, down_w, dh_ref, dhb_ref, dup_ref, act_ref, dyb_ref, dwn_ref, acc):
        i, j = pl.program_id(0), pl.program_id(1)

        @pl.when((i == 0) & (j == 0))
        def _():
            dwn_ref[...] = jnp.zeros_like(dwn_ref)

        @pl.when(j == 0)
        def _():
            acc[...] = jnp.zeros_like(acc)
            dyb_ref[...] = dy_ref[...].astype(MXU_DTYPE)

        parts = []
        for c in range(G):
            cols = slice(c * ts, (c + 1) * ts)
            r = jnp.maximum(up_ref[:, cols], 0.0)
            act_ref[:, cols] = (r * r).astype(MXU_DTYPE)
            dup = (_mm_nt(dyb_ref[...], down_w[cols, :]) * (2.0 * r)).astype(MXU_DTYPE)
            dup_ref[:, cols] = dup
            parts.append(_mm_nt(dup, up_w[c]))
        acc[...] += functools.reduce(jnp.add, parts)

        @pl.when(j == nf - 1)
        def _():
            hv = h_ref[...]
            _, rr = _rms(hv, wn_ref[...])
            dx, dw = _rms_bwd(acc[...], hv, wn_ref[...], rr)
            dh = dy_ref[...] + dx
            dh_ref[...] = dh
            dhb_ref[...] = dh.astype(MXU_DTYPE)
            dwn_ref[...] += dw

    row = lambda i, j: (i, 0)
    return pl.pallas_call(
        body, grid=(T // tm, nf), name="mlp_bwd",
        in_specs=[pl.BlockSpec((tm, D), row), pl.BlockSpec((tm, tf), lambda i, j: (i, j)), pl.BlockSpec((tm, D), row),
                  pl.BlockSpec((1, D), lambda i, j: (0, 0)),
                  pl.BlockSpec((G, D, ts), lambda i, j: (j, 0, 0)), pl.BlockSpec((tf, D), lambda i, j: (j, 0))],
        out_specs=[pl.BlockSpec((tm, D), row), pl.BlockSpec((tm, D), row),
                   pl.BlockSpec((tm, tf), lambda i, j: (i, j)), pl.BlockSpec((tm, tf), lambda i, j: (i, j)),
                   pl.BlockSpec((tm, D), row), pl.BlockSpec((1, D), lambda i, j: (0, 0))],
        out_shape=[SDS((T, D), F32), SDS((T, D), MXU_DTYPE), SDS((T, F), MXU_DTYPE), SDS((T, F), MXU_DTYPE),
                   SDS((T, D), MXU_DTYPE), SDS((1, D), F32)],
        scratch_shapes=[pltpu.VMEM((tm, D), F32)],
        compiler_params=_params(("arbitrary", "arbitrary")),
    )(dy, up, h2, w_mn, w_up, w_down)


def _mix_bwd(dhb, o_mla, o_gdn, proj, mla_w, gdn_w, w_out):
    T, D = dhb.shape
    tm = min(512, T)
    H = MLA_HEADS

    def body(dh_ref, om_ref, og_ref, z_ref, mw_ref, gw_ref, w_ref, dom_ref, dog_ref, dz_ref, dmw_ref, dgw_ref):
        @pl.when(pl.program_id(0) == 0)
        def _():
            dmw_ref[...] = jnp.zeros_like(dmw_ref)
            dgw_ref[...] = jnp.zeros_like(dgw_ref)

        dmix = _mm_nt(dh_ref[...], w_ref[...])
        z = z_ref[...]
        dmw, dzs = [], []
        dgw = jnp.zeros((1, GDN_DIM), F32)
        for h in range(H):
            o = om_ref[h]
            w = mw_ref[h:h + 1, :]
            _, r = _rms(o, w)
            dx, dw = _rms_bwd(dmix[:, h * V_DIM:(h + 1) * V_DIM], o, w, r)
            dom_ref[h] = dx
            dmw.append(dw)
        for h in range(GDN_HEADS):
            o = og_ref[h]
            w = gw_ref[...]
            zh = z[:, h * GDN_DIM:(h + 1) * GDN_DIM]
            sg = _sigmoid(zh)
            yn, r = _rms(o, w)
            dy = dmix[:, H * V_DIM + h * GDN_DIM:H * V_DIM + (h + 1) * GDN_DIM]
            dzs.append(dy * yn * (sg * (1.0 + zh * (1.0 - sg))))
            dx, dw = _rms_bwd(dy * (zh * sg), o, w, r)
            dog_ref[h] = dx
            dgw = dgw + dw
        dz_ref[...] = jnp.concatenate(dzs, axis=-1)
        dmw_ref[...] += jnp.concatenate(dmw, axis=0)
        dgw_ref[...] += dgw

    hspec = pl.BlockSpec((H, tm, V_DIM), lambda i: (0, i, 0))
    return pl.pallas_call(
        body, grid=(T // tm,), name="mix_bwd",
        in_specs=[pl.BlockSpec((tm, D), lambda i: (i, 0)), hspec, hspec,
                  pl.BlockSpec((tm, GDN_WIDTH), lambda i: (i, P_GZ // GDN_WIDTH)),
                  pl.BlockSpec((H, V_DIM), lambda i: (0, 0)), pl.BlockSpec((1, GDN_DIM), lambda i: (0, 0)),
                  pl.BlockSpec((D, D), lambda i: (0, 0))],
        out_specs=[hspec, hspec, pl.BlockSpec((tm, GDN_WIDTH), lambda i: (i, 0)),
                   pl.BlockSpec((H, V_DIM), lambda i: (0, 0)), pl.BlockSpec((1, GDN_DIM), lambda i: (0, 0))],
        out_shape=[SDS((H, T, V_DIM), F32), SDS((H, T, GDN_DIM), F32), SDS((T, GDN_WIDTH), F32),
                   SDS((H, V_DIM), F32), SDS((1, GDN_DIM), F32)],
        compiler_params=_params(("arbitrary",)),
    )(dhb, o_mla, o_gdn, proj, mla_w, gdn_w, w_out)


def _attn_bwd(q4, k4, v4, do4, o4, lse4, B, S, transfer=None):
    H = MLA_HEADS
    bq = min(ATTN_BLOCK, S)
    nq = S // bq
    rows = bq // ATTN_CHAINS

    def body(q_ref, k_ref, v_ref, do_ref, o_ref, lse_ref, dq_ref, dk_ref, dv_ref, delta):
        dq_ref[...] = jnp.zeros_like(dq_ref)
        dk_ref[...] = jnp.zeros_like(dk_ref)
        dv_ref[...] = jnp.zeros_like(dv_ref)
        delta[...] = jnp.sum(do_ref[0] * o_ref[0], axis=-1, keepdims=True)

        col = lax.broadcasted_iota(jnp.int32, (rows, bq), 1)
        row = lax.broadcasted_iota(jnp.int32, (rows, bq), 0)

        def k_step(kj, carry):
            ks = pl.multiple_of(kj * bq, bq)
            k = k_ref[0, pl.ds(ks, bq), :]
            v = v_ref[0, pl.ds(ks, bq), :]

            def q_block(qs, diagonal):
                dks, dvs = [None] * ATTN_CHAINS, [None] * ATTN_CHAINS

                def chain(j):
                    sl = pl.ds(qs + j * rows, rows)
                    q = q_ref[0, sl, :]
                    do = do_ref[0, sl, :].astype(MXU_DTYPE)
                    s = _mm_nt(q, k)
                    dp = _mm_nt(do, v)
                    yield
                    p = jnp.exp(s - lse_ref[0, sl, :])
                    if diagonal:
                        p = jnp.where(col <= row + j * rows, p, 0.0)
                    ds = p * (dp - delta[sl, :])
                    yield
                    dvs[j] = _mm_tn(p, do)
                    dks[j] = _mm_tn(ds, q)
                    dq_ref[0, sl, :] += _mm(ds, k)

                _lockstep([chain(j) for j in range(ATTN_CHAINS)])
                dv_ref[0, pl.ds(ks, bq), :] += functools.reduce(jnp.add, dvs)
                dk_ref[0, pl.ds(ks, bq), :] += functools.reduce(jnp.add, dks)

            q_block(ks, True)

            def q_step(qi, c):
                q_block(pl.multiple_of(qi * bq, bq), False)
                return c

            lax.fori_loop(kj + 1, nq, q_step, 0)
            return carry

        lax.fori_loop(0, nq, k_step, 0)

    spec = lambda d: pl.BlockSpec((1, S, d), lambda h, b: (h, b, 0))
    return _call_beside(
        body, transfer, grid=(H, B), name="attn_bwd",
        in_specs=[spec(QK_DIM), spec(QK_DIM), spec(V_DIM), spec(V_DIM), spec(V_DIM), spec(1)],
        out_specs=[spec(QK_DIM), spec(QK_DIM), spec(V_DIM)],
        out_shape=[SDS((H, B * S, QK_DIM), F32), SDS((H, B * S, QK_DIM), F32), SDS((H, B * S, V_DIM), F32)],
        scratch_shapes=[pltpu.VMEM((S, 1), F32)], semantics=("arbitrary", "arbitrary"),
        args=(q4, k4, v4, do4, o4, lse4))


def _gdn_bwd(qg, kg, vg, gates, states, ainv, u4, w4, do4, B, S, transfer=None):
    H, D, C = GDN_HEADS, GDN_DIM, CHUNK
    NC = S // C
    U = GDN_BWD_UNROLL if NC % GDN_BWD_UNROLL == 0 else 1
    NG = NC // U

    def body(q_ref, k_ref, v_ref, g_ref, st_ref, ai_ref, u_ref, w_ref, do_ref, dq_ref, dk_ref, dv_ref, dgb_ref,
             kd_s, x1_s, x2_s, el_s, dvn_s, ds_s, w2t_s):
        h = pl.program_id(0)
        lane = lax.broadcasted_iota(jnp.int32, (C, LANES), 1)
        ri = lax.broadcasted_iota(jnp.int32, (C, C), 0)
        ci = lax.broadcasted_iota(jnp.int32, (C, C), 1)
        rcol = lax.broadcasted_iota(jnp.int32, (C, 1), 0)

        def rsum(a):
            return jnp.sum(a, axis=-1, keepdims=True)

        def prepare(n):
            cs = n * C
            q = q_ref[0, pl.ds(cs, C), :]
            k = k_ref[0, pl.ds(cs, C), :]
            do = do_ref[0, pl.ds(cs, C), :]
            Gc, bt, Gam, e, f, eL = _chunk_decays(g_ref[pl.ds(cs, C), :], lane, h, ri, ci, rcol)
            At = _mm_nt(q, k) * Gam
            yield
            x1 = _mm_tn(At, do)
            x2 = _mm_tn(q * e, do)
            kd = k * f
            w = w_ref[0, pl.ds(cs, C), :]
            yield
            x1_s[pl.ds(cs, C), :] = x1
            x2_s[n] = x2 - _mm_tn(w, x1)
            w2t_s[n] = _mm_tn(w, kd)
            kd_s[pl.ds(cs, C), :] = kd
            el_s[n] = jnp.broadcast_to(eL, (SUBLANES, LANES))

        def recur(n, dS):
            cs = n * C
            ds_s[n] = dS
            dvn_s[pl.ds(cs, C), :] = x1_s[pl.ds(cs, C), :] + _mm(kd_s[pl.ds(cs, C), :], dS)
            return x2_s[n] + el_s[n, 0:1, :] * dS - _mm(w2t_s[n], dS)

        def local(n):
            cs = n * C
            q = q_ref[0, pl.ds(cs, C), :]
            k = k_ref[0, pl.ds(cs, C), :]
            v = v_ref[0, pl.ds(cs, C), :]
            do = do_ref[0, pl.ds(cs, C), :]
            u = u_ref[0, pl.ds(cs, C), :]
            w = w_ref[0, pl.ds(cs, C), :]
            dvn = dvn_s[pl.ds(cs, C), :]
            dS = ds_s[n]
            Gc, bt, Gam, e, f, eL = _chunk_decays(g_ref[pl.ds(cs, C), :], lane, h, ri, ci, rcol)
            S0 = st_ref[0, n]
            AinvT = ai_ref[0, n]
            qk = _mm_nt(jnp.concatenate([q, k], axis=0), k)
            QK, KK = qk[:C], qk[C:]
            be = bt * e
            sol = jnp.concatenate([u, w], axis=-1)
            vn = u - _mm(w, S0)
            yield
            dAt = jnp.where(ri >= ci, _mm_nt(do, vn), 0.0)
            dqd = _mm_nt(do, S0)
            dw = -_mm_nt(dvn, S0)
            dkd = _mm_nt(vn, dS)
            deL = jnp.sum(rsum(dS * S0), axis=0, keepdims=True)
            yield
            dR = _mm_exact(AinvT, jnp.concatenate([dvn, dw], axis=-1))
            dR1, dR2 = dR[:, :D], dR[:, D:]
            yield
            dL = jnp.where(ri > ci, -_mm_nt(dR, sol), 0.0)
            yield
            dv_ref[0, pl.ds(cs, C), :] = dR1 * bt
            r2 = rsum(dR2 * k)
            X = dL * Gam
            dbt = rsum(dR1 * v) + r2 * e + rsum(X * KK)
            de = r2 * bt + rsum(dqd * q)
            dKK = X * bt
            dQK = dAt * Gam
            dq_ref[0, pl.ds(cs, C), :] = _mm(dQK, k) + dqd * e
            dk_ref[0, pl.ds(cs, C), :] = dR2 * be + _mm(dKK + dKK.T, k) + _mm_tn(dQK, q) + dkd * f
            df = rsum(dkd * k)
            Z = (dL * (bt * KK) + dAt * QK) * Gam
            dG = rsum(Z) - rsum(Z.T) + de * e - df * f
            dGl = jnp.sum(df * f, axis=0, keepdims=True) + deL * eL
            dG = dG + jnp.where(rcol == C - 1, dGl, 0.0)
            dgb_ref[0, pl.ds(cs, C), :] = jnp.where(lane == 0, dG, jnp.where(lane == 1, dbt, 0.0))

        state = [jnp.zeros((D, D), F32)]

        def recur_group(g):
            for j, n in enumerate(reversed(range(g * U, (g + 1) * U))):
                state[0] = recur(n, state[0])
                if j % GDN_RECUR_STEPS_PER_STAGE == GDN_RECUR_STEPS_PER_STAGE - 1:
                    yield

        def stage(fn, g):
            return _together([fn(g * U + j) for j in range(U)])

        for step in range(NG + 2):
            jobs = [(stage, prepare, NG - 1 - step), (None, None, NG - step), (stage, local, NG + 1 - step)]
            _lockstep([recur_group(g) if make is None else make(fn, g) for make, fn, g in jobs if 0 <= g < NG])

    spec = pl.BlockSpec((1, S, D), lambda h, b: (h, b, 0))
    return _call_beside(
        body, transfer, grid=(H, B), name="gdn_bwd",
        in_specs=[spec, spec, spec, pl.BlockSpec((S, LANES), lambda h, b: (b, 0)),
                  pl.BlockSpec((1, NC, D, D), lambda h, b: (h, b, 0, 0)),
                  pl.BlockSpec((1, NC, C, C), lambda h, b: (h, b, 0, 0)), spec, spec, spec],
        out_specs=[spec, spec, spec, spec],
        out_shape=[SDS((H, B * S, D), F32)] * 4,
        scratch_shapes=[pltpu.VMEM((S, D), F32), pltpu.VMEM((S, D), F32), pltpu.VMEM((NC, D, D), F32),
                        pltpu.VMEM((NC, SUBLANES, LANES), F32), pltpu.VMEM((S, D), F32),
                        pltpu.VMEM((NC, D, D), F32), pltpu.VMEM((NC, D, D), F32)],
        semantics=("arbitrary", "arbitrary"), args=(qg, kg, vg, gates, states, ainv, u4, w4, do4))


def _gdn_pre_bwd(proj, conv_w, alog_l, dt_l, dq4, dk4, dv4, dgb4, S):
    T = proj.shape[0]
    tm = min(256, T)
    tiles_per_seq = S // tm
    C3 = 3 * GDN_WIDTH
    H = GDN_HEADS

    def body(u_ref, halo_ref, gab_ref, w_ref, alog_ref, dt_ref, dq_ref, dk_ref, dv_ref, dgb_ref,
             dc_ref, dgab_ref, dcw_ref, dalog_ref, ddt_ref):
        i = pl.program_id(0)

        @pl.when(i == 0)
        def _():
            dcw_ref[...] = jnp.zeros_like(dcw_ref)
            dalog_ref[...] = jnp.zeros_like(dalog_ref)
            ddt_ref[...] = jnp.zeros_like(ddt_ref)

        halo = jnp.where(i % tiles_per_seq == 0, 0.0, halo_ref[...])
        c, sh = _conv_taps(u_ref[...], halo, w_ref[...])
        sg = _sigmoid(c)
        a = c * sg
        das = [None] * (3 * H)
        for h in range(H):
            xq = a[:, h * GDN_DIM:(h + 1) * GDN_DIM]
            xk = a[:, GDN_WIDTH + h * GDN_DIM:GDN_WIDTH + (h + 1) * GDN_DIM]
            das[h] = _l2n_bwd(dq_ref[h], xq, GDN_QSCALE)
            das[H + h] = _l2n_bwd(dk_ref[h], xk, 1.0)
            das[2 * H + h] = dv_ref[h]
        dc = jnp.concatenate(das, axis=-1) * (sg * (1.0 + c * (1.0 - sg)))
        dc_ref[...] = dc
        dcw_ref[...] += jnp.concatenate(
            [jnp.sum(dc * sh[CONV_W - 1 - t], axis=0, keepdims=True) for t in range(CONV_W)], axis=0)
        lane = lax.broadcasted_iota(jnp.int32, (tm, LANES), 1)
        ric = lax.broadcasted_iota(jnp.int32, (tm, LANES), 0) % CHUNK
        dG = jnp.zeros((tm, LANES), F32)
        for h in range(H):
            t = dgb_ref[h]
            dG = dG + jnp.where(lane == h, _pick_lane(t, lane, 0), 0.0) \
                    + jnp.where(lane == h + H, _pick_lane(t, lane, 1), 0.0)
        is_g = lane < H
        dg = jnp.where(is_g, _chunk_rev_cumsum(jnp.where(is_g, dG, 0.0), ric), 0.0)
        gab = gab_ref[...]
        g, beta = _gate_values(gab, alog_ref[...], dt_ref[...], lane)
        dga = jnp.where(is_g, dg * (-jnp.exp(alog_ref[...])) * _sigmoid(gab + dt_ref[...]), 0.0)
        dgb = jnp.where(is_g, 0.0, dG) * beta * (1.0 - beta)
        dgab_ref[...] = dga + dgb
        dalog_ref[...] += jnp.sum(dg * g, axis=0, keepdims=True)
        ddt_ref[...] += jnp.sum(dga, axis=0, keepdims=True)

    hspec = pl.BlockSpec((H, tm, GDN_DIM), lambda i: (0, i, 0))
    vec = pl.BlockSpec((1, LANES), lambda i: (0, 0))
    return pl.pallas_call(
        body, grid=(T // tm,), name="gdn_pre_bwd",
        in_specs=[pl.BlockSpec((tm, C3), lambda i: (i, 0)),
                  pl.BlockSpec((SUBLANES, C3), lambda i: (jnp.maximum(i * (tm // SUBLANES) - 1, 0), 0)),
                  pl.BlockSpec((tm, LANES), lambda i: (i, P_GAB // LANES)),
                  pl.BlockSpec((CONV_W, C3), lambda i: (0, 0)), vec, vec, hspec, hspec, hspec, hspec],
        out_specs=[pl.BlockSpec((tm, C3), lambda i: (i, 0)), pl.BlockSpec((tm, LANES), lambda i: (i, 0)),
                   pl.BlockSpec((CONV_W, C3), lambda i: (0, 0)), vec, vec],
        out_shape=[SDS((T, C3), F32), SDS((T, LANES), F32), SDS((CONV_W, C3), F32),
                   SDS((1, LANES), F32), SDS((1, LANES), F32)],
        compiler_params=_params(("arbitrary",)),
    )(proj, proj, proj, conv_w, alog_l, dt_l, dq4, dk4, dv4, dgb4)


def _conv_bwd_input(dc, conv_w, S):
    T, C3 = dc.shape
    tm = min(256, T)
    tiles_per_seq = S // tm
    nblk = T // SUBLANES

    def body(dc_ref, nxt_ref, w_ref, du_ref):
        i = pl.program_id(0)
        nxt = jnp.where(i % tiles_per_seq == tiles_per_seq - 1, 0.0, nxt_ref[...])
        x = dc_ref[...]
        w = w_ref[...]
        du = w[3:4] * x
        for j in range(1, CONV_W):
            du = du + w[3 - j:4 - j] * _shift_up(x, nxt, j)
        du_ref[...] = du

    return pl.pallas_call(
        body, grid=(T // tm,), name="conv_bwd_input",
        in_specs=[pl.BlockSpec((tm, C3), lambda i: (i, 0)),
                  pl.BlockSpec((SUBLANES, C3), lambda i: (jnp.minimum((i + 1) * (tm // SUBLANES), nblk - 1), 0)),
                  pl.BlockSpec((CONV_W, C3), lambda i: (0, 0))],
        out_specs=pl.BlockSpec((tm, C3), lambda i: (i, 0)),
        out_shape=SDS((T, C3), F32),
        compiler_params=_params(("arbitrary",)),
    )(dc, dc, conv_w)


def _mla_pre_bwd(proj, cosf, sinf, w_qln, w_kvln, w_uq_p, w_ukv, qnw, knw, dq4, dk4, dv4, transfer=None):
    T = proj.shape[0]
    tm = min(256, T)
    H = MLA_HEADS

    def body(ql_ref, kvl_ref, kpe_ref, cos_ref, sin_ref, wq_ref, wkv_ref, uq_ref, ukv_ref, qnw_ref, knw_ref,
             dq_ref, dk_ref, dv_ref,
             dql_ref, dkvl_ref, dkpe_ref, dqraw_ref, dkvraw_ref, qn_ref, kvn_ref, dwq_ref, dwkv_ref, dqnw_ref, dknw_ref):
        @pl.when(pl.program_id(0) == 0)
        def _():
            for r in (dwq_ref, dwkv_ref, dqnw_ref, dknw_ref):
                r[...] = jnp.zeros_like(r)

        cos, sin = cos_ref[...], sin_ref[...]
        qnw_, knw_ = qnw_ref[...], knw_ref[...]
        ql, kvl = ql_ref[...], kvl_ref[...]
        kpe_raw = kpe_ref[...][:, :ROPE]
        rms = functools.partial(_rms, on_mxu=True)
        rms_bwd = functools.partial(_rms_bwd, on_mxu=True)
        qn, rq = rms(ql, wq_ref[...])
        kvn, rkv = rms(kvl, wkv_ref[...])
        qn_ref[...] = qn.astype(MXU_DTYPE)
        kvn_ref[...] = kvn.astype(MXU_DTYPE)
        qraw = _mm(qn, uq_ref[...])
        kvraw = _mm(kvn, ukv_ref[...])
        dq_nope, dq_pe, dkv_parts = [], [], []
        dqnw_n = jnp.zeros((1, NOPE), F32)
        dqnw_p = jnp.zeros((1, ROPE), F32)
        dknw_n = jnp.zeros((1, NOPE), F32)
        dkpe = jnp.zeros((tm, ROPE), F32)
        for h in range(H):
            dq = dq_ref[h] * ATT_SCALE
            x = qraw[:, h * NOPE:(h + 1) * NOPE]
            dx, dw = rms_bwd(dq[:, :NOPE], x, qnw_[:, :NOPE], rms(x, qnw_[:, :NOPE])[1])
            dq_nope.append(dx)
            dqnw_n = dqnw_n + dw
            x = qraw[:, H * NOPE + h * ROPE:H * NOPE + (h + 1) * ROPE]
            dx, dw = rms_bwd(_rope_bwd(dq[:, NOPE:], cos, sin), x, qnw_[:, NOPE:], rms(x, qnw_[:, NOPE:])[1])
            dq_pe.append(dx)
            dqnw_p = dqnw_p + dw
            dk = dk_ref[h]
            x = kvraw[:, h * 256:h * 256 + NOPE]
            dx, dw = rms_bwd(dk[:, :NOPE], x, knw_[:, :NOPE], rms(x, knw_[:, :NOPE])[1])
            dknw_n = dknw_n + dw
            dkpe = dkpe + dk[:, NOPE:]
            dkv_parts += [dx, dv_ref[h]]
        dx, dknw_p = rms_bwd(_rope_bwd(dkpe, cos, sin), kpe_raw, knw_[:, NOPE:], rms(kpe_raw, knw_[:, NOPE:])[1])
        dkpe_ref[...] = jnp.concatenate([dx, jnp.zeros((tm, LANES - ROPE), F32)], axis=-1)
        dqraw = jnp.concatenate(dq_nope + dq_pe, axis=-1).astype(MXU_DTYPE)
        dkvraw = jnp.concatenate(dkv_parts, axis=-1).astype(MXU_DTYPE)
        dqraw_ref[...] = dqraw
        dkvraw_ref[...] = dkvraw
        dx, dw = rms_bwd(_mm_nt(dqraw, uq_ref[...]), ql, wq_ref[...], rq)
        dql_ref[...] = dx
        dwq_ref[...] += dw
        dx, dw = rms_bwd(_mm_nt(dkvraw, ukv_ref[...]), kvl, wkv_ref[...], rkv)
        dkvl_ref[...] = dx
        dwkv_ref[...] += dw
        dqnw_ref[...] += jnp.concatenate([dqnw_n, dqnw_p], axis=-1)
        dknw_ref[...] += jnp.concatenate([dknw_n, dknw_p], axis=-1)

    full = lambda a: pl.BlockSpec(a.shape, lambda i: (0,) * a.ndim)
    rows = lambda n: pl.BlockSpec((tm, n), lambda i: (i, 0))
    const = lambda n: pl.BlockSpec((1, n), lambda i: (0, 0))
    NQ, NKV = w_uq_p.shape[1], w_ukv.shape[1]
    return _call_beside(
        body, transfer, grid=(T // tm,), name="mla_pre_bwd", scratch_shapes=[], semantics=("arbitrary",),
        args=(proj, proj, proj, cosf, sinf, w_qln, w_kvln, w_uq_p, w_ukv, qnw, knw, dq4, dk4, dv4),
        in_specs=[pl.BlockSpec((tm, 256), lambda i: (i, P_QLAT // 256)),
                  pl.BlockSpec((tm, 256), lambda i: (i, P_KVLAT // 256)),
                  pl.BlockSpec((tm, 128), lambda i: (i, P_KPE // 128)),
                  rows(ROPE), rows(ROPE),
                  full(w_qln), full(w_kvln), full(w_uq_p), full(w_ukv), full(qnw), full(knw),
                  pl.BlockSpec((H, tm, QK_DIM), lambda i: (0, i, 0)),
                  pl.BlockSpec((H, tm, QK_DIM), lambda i: (0, i, 0)),
                  pl.BlockSpec((H, tm, V_DIM), lambda i: (0, i, 0))],
        out_specs=[rows(Q_LORA), rows(KV_LORA), rows(LANES), rows(NQ), rows(NKV), rows(Q_LORA), rows(KV_LORA),
                   const(Q_LORA), const(KV_LORA), const(QK_DIM), const(QK_DIM)],
        out_shape=[SDS((T, Q_LORA), F32), SDS((T, KV_LORA), F32), SDS((T, LANES), F32),
                   SDS((T, NQ), MXU_DTYPE), SDS((T, NKV), MXU_DTYPE),
                   SDS((T, Q_LORA), MXU_DTYPE), SDS((T, KV_LORA), MXU_DTYPE),
                   SDS((1, Q_LORA), F32), SDS((1, KV_LORA), F32), SDS((1, QK_DIM), F32), SDS((1, QK_DIM), F32)])


def _in_proj_bwd(dgqkv, dgz, dql, dkvl, dkpe, dgab, w_in_p, dh, x2, w_an):
    T, D = x2.shape
    N = w_in_p.shape[1]
    tm = min(512, T)

    def body(a_ref, b_ref, c_ref, d_ref, e_ref, f_ref, w_ref, dh_ref, x_ref, wn_ref, dx_ref, dp_ref, dwn_ref):
        @pl.when(pl.program_id(0) == 0)
        def _():
            dwn_ref[...] = jnp.zeros_like(dwn_ref)

        dp = jnp.concatenate([a_ref[...], b_ref[...], c_ref[...], d_ref[...], e_ref[...], f_ref[...]],
                             axis=-1).astype(MXU_DTYPE)
        dp_ref[...] = dp
        x = x_ref[...]
        _, r = _rms(x, wn_ref[...])
        dx, dw = _rms_bwd(_mm_nt(dp, w_ref[...]), x, wn_ref[...], r)
        dx_ref[...] = dh_ref[...] + dx
        dwn_ref[...] += dw

    rows = lambda n: pl.BlockSpec((tm, n), lambda i: (i, 0))
    return pl.pallas_call(
        body, grid=(T // tm,), name="in_proj_bwd",
        in_specs=[rows(dgqkv.shape[1]), rows(dgz.shape[1]), rows(dql.shape[1]), rows(dkvl.shape[1]),
                  rows(dkpe.shape[1]), rows(dgab.shape[1]),
                  pl.BlockSpec((D, N), lambda i: (0, 0)), rows(D), rows(D), pl.BlockSpec((1, D), lambda i: (0, 0))],
        out_specs=[rows(D), rows(N), pl.BlockSpec((1, D), lambda i: (0, 0))],
        out_shape=[SDS((T, D), F32), SDS((T, N), MXU_DTYPE), SDS((1, D), F32)],
        compiler_params=_params(("arbitrary",)),
    )(dgqkv, dgz, dql, dkvl, dkpe, dgab, w_in_p, dh, x2, w_an)


def _wgrad(a, b, name, column_shards=False):
    T, M = a.shape
    N = b.shape[1]
    tM = _divisor_tile(M, 1024)
    tN = N // N_DEV if column_shards else _divisor_tile(N, 1536)
    tk = min(T, 2048)
    nk = T // tk

    def body(a_ref, b_ref, o_ref, acc):
        k = pl.program_id(2)

        @pl.when(k == 0)
        def _():
            acc[...] = jnp.zeros_like(acc)

        acc[...] += _mm_tn(a_ref[...], b_ref[...])

        @pl.when(k == nk - 1)
        def _():
            o_ref[...] = acc[...].astype(WIRE_DTYPE).reshape(o_ref.shape)

    if column_shards:
        out_spec, out_shape = pl.BlockSpec((1, tM, tN), lambda i, j, k: (j, i, 0)), SDS((N_DEV, M, tN), WIRE_DTYPE)
    else:
        out_spec, out_shape = pl.BlockSpec((tM, tN), lambda i, j, k: (i, j)), SDS((M, N), WIRE_DTYPE)
    return pl.pallas_call(
        body, grid=(M // tM, N // tN, nk), name=name,
        in_specs=[pl.BlockSpec((tk, tM), lambda i, j, k: (k, i)), pl.BlockSpec((tk, tN), lambda i, j, k: (k, j))],
        out_specs=out_spec, out_shape=out_shape,
        scratch_shapes=[pltpu.VMEM((tM, tN), F32)],
        compiler_params=_params(("arbitrary", "arbitrary", "arbitrary")),
    )(a, b)


def _adamw(g, w, m, v):
    m = ADAM_B1 * m + (1.0 - ADAM_B1) * g
    v = ADAM_B2 * v + (1.0 - ADAM_B2) * jnp.square(g)
    m_hat = m / (1.0 - ADAM_B1 ** ADAM_STEP)
    v_hat = v / (1.0 - ADAM_B2 ** ADAM_STEP)
    return -ADAM_LR * (m_hat / (jnp.sqrt(v_hat) + ADAM_EPS) + ADAM_WD * w), m, v


def _reduce_adamw(parts, w, m, v, name):
    R, C = w.shape
    _, Rp, Cp = parts.shape
    tr = min(R, 256)
    tp = tr if Rp == R else Rp

    def body(p_ref, w_ref, m_ref, v_ref, g_ref, d_ref, nm_ref, nv_ref):
        g = p_ref[0].astype(F32)
        for s in range(1, N_DEV):
            g = g + p_ref[s].astype(F32)
        g = g[:tr, :C]
        g_ref[...] = g
        d_ref[...], nm_ref[...], nv_ref[...] = _adamw(g, w_ref[...], m_ref[...], v_ref[...])

    spec = pl.BlockSpec((tr, C), lambda i: (i, 0))
    return pl.pallas_call(
        body, grid=(R // tr,), name=name,
        in_specs=[pl.BlockSpec((N_DEV, tp, Cp), lambda i: (0, i, 0)), spec, spec, spec],
        out_specs=[spec] * 4, out_shape=[SDS((R, C), F32)] * 4,
        compiler_params=_params(("arbitrary",)),
    )(parts, w, m, v)


SMALL_ROWS, SMALL_COLS = 16, 1024
SMALL_LAYOUT = (
    ("attn_norm_w", 0, 1, 1024, 1024), ("mlp_norm_w", 1, 1, 1024, 1024), ("q_lat_norm_w", 2, 1, 256, 256),
    ("kv_lat_norm_w", 3, 1, 256, 256), ("q_norm_w", 4, 1, 192, 192), ("k_norm_w", 5, 1, 192, 192),
    ("mla_out_norm_w", 6, 4, 128, 128), ("a_log", 10, 1, 128, 4), ("dt_bias", 11, 1, 128, 4),
    ("gdn_norm_w", 12, 1, 128, 128))
LOSS_ENTRY = ("loss", 13, 1, 128, 128)


def _adamw_replicated(parts, ws, ms, vs):
    n = len(SMALL_LAYOUT)

    def body(*refs):
        p_ref = refs[0]
        w_refs, m_refs, v_refs = refs[1:1 + n], refs[1 + n:1 + 2 * n], refs[1 + 2 * n:1 + 3 * n]
        outs = refs[1 + 3 * n:]
        s = p_ref[0]
        for d in range(1, N_DEV):
            s = s + p_ref[d]
        for i, (_, r0, nr, _, pw) in enumerate(SMALL_LAYOUT):
            g = s[r0:r0 + nr, :pw]
            outs[i][...] = g
            outs[n + i][...], outs[2 * n + i][...], outs[3 * n + i][...] = _adamw(
                g, w_refs[i][...], m_refs[i][...], v_refs[i][...])
        _, r0, nr, gw, _ = LOSS_ENTRY
        outs[4 * n][...] = s[r0:r0 + nr, :gw]

    res = pl.pallas_call(
        body, name="adamw_replicated",
        out_shape=[SDS(w.shape, F32) for w in ws] * 4 + [SDS((1, LANES), F32)],
        compiler_params=_params(),
    )(parts, *ws, *ms, *vs)
    return [res[k * n:(k + 1) * n] for k in range(4)], res[4 * n][0, 0]


COPIES_PER_ARRAY = N_DEV - 1


def _two_level_gather(srcs, outs, send_sems, recv_sems, local_sems=None, stage="all"):
    mx, my, mc = lax.axis_index("x"), lax.axis_index("y"), lax.axis_index("c")
    me, sibling = (mx, my, mc), (mx, my, 1 - mc)
    chips = [(1 - mx, my), (mx, 1 - my), (1 - mx, 1 - my)]
    arrays = range(len(srcs))

    def copy(a, k, block, to, src=None):
        px, py, pc = block
        slot = outs[a].at[4 * px + 2 * py + pc]
        sem = a * COPIES_PER_ARRAY + k
        return pltpu.make_async_remote_copy(
            src_ref=slot if src is None else src, dst_ref=slot,
            send_sem=send_sems.at[sem], recv_sem=recv_sems.at[sem], device_id=to, device_id_type=MESH_ID)

    mine = [] if local_sems is None else [
        pltpu.make_async_copy(srcs[a], outs[a].at[4 * mx + 2 * my + mc], local_sems.at[a]) for a in arrays]
    first = []
    for a in arrays:
        first.append(copy(a, 0, me, sibling, src=srcs[a]))
        first += [copy(a, 1 + j, me, (*chip, mc), src=srcs[a]) for j, chip in enumerate(chips)]
    if stage in ("all", "start"):
        for cp in mine + first:
            cp.start()
    if stage in ("all", "finish"):
        forwards = []
        for j, chip in enumerate(chips):
            for a in arrays:
                copy(a, 1 + j, (*chip, mc), me).wait_recv()
                fwd = copy(a, 4 + j, (*chip, mc), sibling)
                fwd.start()
                forwards.append(fwd)
        for a in arrays:
            copy(a, 0, sibling, me).wait_recv()
        for j, chip in enumerate(chips):
            for a in arrays:
                copy(a, 4 + j, (*chip, 1 - mc), me).wait_recv()
        for cp in first + forwards:
            cp.wait_send()
        for cp in mine:
            cp.wait()


def _comm_scratch(n):
    return [pltpu.SemaphoreType.DMA((n * COPIES_PER_ARRAY,)), pltpu.SemaphoreType.DMA((n * COPIES_PER_ARRAY,)),
            pltpu.SemaphoreType.DMA((n,))]


def _any_specs(n):
    return [pl.BlockSpec(memory_space=pl.ANY)] * n


def _gather_weights(shards):
    n = len(shards)

    def body(*refs):
        _two_level_gather(refs[:n], refs[n:2 * n], *refs[2 * n:])

    return pl.pallas_call(
        body, name="gather_weights",
        out_shape=[SDS((N_DEV,) + s.shape, s.dtype) for s in shards],
        in_specs=_any_specs(n), out_specs=_any_specs(n), scratch_shapes=_comm_scratch(n),
    )(*shards)


def _gather_small_grads(gs, loss_lanes):
    gs = list(gs) + [loss_lanes]
    n = len(gs)

    def body(*refs):
        g_refs, out_ref = refs[:n], refs[n]
        tile, send_sems, recv_sems = refs[n + 1:]
        tile[...] = jnp.zeros_like(tile)
        for (_, r0, nr, gw, _), g in zip(SMALL_LAYOUT + (LOSS_ENTRY,), g_refs):
            tile[r0:r0 + nr, 0:gw] = g[...]
        me = 4 * lax.axis_index("x") + 2 * lax.axis_index("y") + lax.axis_index("c")
        out_ref[me] = tile[...]
        _two_level_gather([tile], [out_ref], send_sems, recv_sems)

    return pl.pallas_call(
        body, name="gather_small_grads",
        out_shape=SDS((N_DEV, SMALL_ROWS, SMALL_COLS), F32),
        in_specs=[pl.BlockSpec(memory_space=pltpu.VMEM)] * n,
        out_specs=pl.BlockSpec(memory_space=pltpu.VMEM),
        scratch_shapes=[pltpu.VMEM((SMALL_ROWS, SMALL_COLS), F32),
                        pltpu.SemaphoreType.DMA((COPIES_PER_ARRAY,)), pltpu.SemaphoreType.DMA((COPIES_PER_ARRAY,))],
    )(*gs)


def _exchange_grads(slabs):
    n = len(slabs)

    def body(*refs):
        _exchange(refs[:n], refs[n:2 * n], *refs[2 * n:])

    return pl.pallas_call(
        body, name="exchange_grads",
        out_shape=[SDS(s.shape, s.dtype) for s in slabs],
        in_specs=_any_specs(n), out_specs=_any_specs(n), scratch_shapes=_comm_scratch(n),
    )(*slabs)


class _Transfer:
    def __init__(self, kind, arrays):
        self.kind, self.arrays, self.n = kind, list(arrays), len(arrays)

    def out_shapes(self):
        if self.kind == "gather":
            return [SDS((N_DEV,) + a.shape, a.dtype) for a in self.arrays]
        return [SDS(a.shape, a.dtype) for a in self.arrays]

    def run(self, srcs, outs, sems, stage):
        fn = _two_level_gather if self.kind == "gather" else _exchange
        fn(srcs, outs, *sems, stage=stage)


def _call_beside(body, transfer, *, grid, in_specs, out_specs, out_shape, scratch_shapes, name, semantics, args):
    if transfer is None:
        res = pl.pallas_call(body, grid=grid, in_specs=in_specs, out_specs=out_specs, out_shape=out_shape,
                             scratch_shapes=scratch_shapes, name=name, compiler_params=_params(semantics))(*args)
        return list(res), []
    n_in, n_out, n_s, n = len(in_specs), len(out_specs), len(scratch_shapes), transfer.n

    def wrapped(*refs):
        ins, refs = refs[:n_in], refs[n_in:]
        t_in, refs = refs[:n], refs[n:]
        outs, refs = refs[:n_out], refs[n_out:]
        t_out, refs = refs[:n], refs[n:]
        scratch, sems = refs[:n_s], refs[n_s:]
        first = functools.reduce(jnp.logical_and, [pl.program_id(i) == 0 for i in range(len(grid))])
        last = functools.reduce(jnp.logical_and, [pl.program_id(i) == g - 1 for i, g in enumerate(grid)])

        @pl.when(first)
        def _():
            transfer.run(t_in, t_out, sems, "start")

        body(*ins, *outs, *scratch)

        @pl.when(last)
        def _():
            transfer.run(t_in, t_out, sems, "finish")

    res = pl.pallas_call(
        wrapped, grid=grid, in_specs=list(in_specs) + _any_specs(n), out_specs=list(out_specs) + _any_specs(n),
        out_shape=list(out_shape) + transfer.out_shapes(), scratch_shapes=list(scratch_shapes) + _comm_scratch(n),
        name=name, compiler_params=_params(semantics))(*args, *transfer.arrays)
    return list(res[:n_out]), list(res[n_out:])


EXCHANGE_FLIPS = ((0, 0, 1), (1, 0, 0), (0, 1, 0), (1, 1, 0), (1, 0, 1), (0, 1, 1), (1, 1, 1))


def _exchange(srcs, outs, send_sems, recv_sems, local_sems, stage="all"):
    mx, my, mc = lax.axis_index("x"), lax.axis_index("y"), lax.axis_index("c")
    arrays = range(len(srcs))
    copies = [pltpu.make_async_copy(srcs[a].at[4 * mx + 2 * my + mc], outs[a].at[N_DEV - 1], local_sems.at[a])
              for a in arrays]
    for k, (fx, fy, fc) in enumerate(EXCHANGE_FLIPS):
        px = 1 - mx if fx else mx
        py = 1 - my if fy else my
        pc = 1 - mc if fc else mc
        for a in arrays:
            sem = a * COPIES_PER_ARRAY + k
            copies.append(pltpu.make_async_remote_copy(
                src_ref=srcs[a].at[4 * px + 2 * py + pc], dst_ref=outs[a].at[k],
                send_sem=send_sems.at[sem], recv_sem=recv_sems.at[sem],
                device_id=(px, py, pc), device_id_type=MESH_ID))
    if stage in ("all", "start"):
        for cp in copies:
            cp.start()
    if stage in ("all", "finish"):
        for cp in copies:
            cp.wait()


def _w_in_to_padded(w):
    z = lambda n: jnp.zeros((w.shape[0], n), w.dtype)
    return jnp.concatenate([w[:, O_GQKV:O_GZ], w[:, O_GZ:O_GAB], w[:, O_QLAT:O_KVLAT], w[:, O_KVLAT:O_KPE],
                            w[:, O_KPE:O_GQKV], z(P_GAB - P_KPE - ROPE), w[:, O_GAB:O_END],
                            z(P_WIDTH - P_GAB - (O_END - O_GAB))], axis=1)


def _w_in_from_padded(wp):
    return jnp.concatenate([wp[:, P_QLAT:P_QLAT + 256], wp[:, P_KVLAT:P_KVLAT + 256], wp[:, P_KPE:P_KPE + ROPE],
                            wp[:, P_GQKV:P_GZ], wp[:, P_GZ:P_QLAT], wp[:, P_GAB:P_GAB + (O_END - O_GAB)]], axis=1)


def _w_uq_to_headsplit(w):
    w3 = w.reshape(w.shape[0], MLA_HEADS, QK_DIM)
    return jnp.concatenate([w3[:, :, :NOPE].reshape(w.shape[0], -1), w3[:, :, NOPE:].reshape(w.shape[0], -1)], axis=1)


def _w_uq_from_headsplit(wp):
    n = wp[:, :MLA_HEADS * NOPE].reshape(wp.shape[0], MLA_HEADS, NOPE)
    p = wp[:, MLA_HEADS * NOPE:].reshape(wp.shape[0], MLA_HEADS, ROPE)
    return jnp.concatenate([n, p], axis=2).reshape(wp.shape[0], -1)


def _lane_vec(v4):
    return jnp.pad(v4.reshape(1, -1), ((0, 0), (0, LANES - v4.shape[-1])))


def _local_step(x, positions, target, attn_norm_w, w_in, q_lat_norm_w, w_uq, kv_lat_norm_w, w_ukv, q_norm_w,
                k_norm_w, mla_out_norm_w, conv_w, a_log, dt_bias, gdn_norm_w, w_out, mlp_norm_w, w_up, w_down,
                late_shards=None, exchange=False):
    B, S, D = x.shape
    T = B * S
    x2 = x.reshape(T, D)
    t2 = target.reshape(T, D)
    half = ROPE // 2
    inv_freq = ROPE_THETA ** (-jnp.arange(half, dtype=F32) / half)
    ang = positions.reshape(T, 1).astype(F32) * inv_freq
    cosf = jnp.concatenate([jnp.cos(ang)] * 2, axis=-1)
    sinf = jnp.concatenate([jnp.sin(ang)] * 2, axis=-1)
    w_in_p = _w_in_to_padded(w_in)
    w_uq_p = _w_uq_to_headsplit(w_uq)
    alog_l, dt_l = _lane_vec(a_log), _lane_vec(dt_bias)
    w_an, w_qln, w_kvln, qnw, knw, w_mn, gdn_w = (
        attn_norm_w, q_lat_norm_w, kv_lat_norm_w, q_norm_w, k_norm_w, mlp_norm_w, gdn_norm_w)

    proj, xn = _in_proj(x2, w_an, w_in_p)
    gather = None if late_shards is None else _Transfer("gather", late_shards[:1])
    (q4, k4, v4), late = _mla_pre(proj, cosf, sinf, w_qln, w_kvln, w_uq_p, w_ukv, qnw, knw, gather)
    if late:
        w_out = late[0].reshape(-1, D)
    (o_mla, lse), _ = _attn_fwd(q4, k4, v4, B, S)
    qg, kg, vg, gates = _gdn_pre(proj, conv_w, alog_l, dt_l, S)
    gather = None if late_shards is None else _Transfer("gather", late_shards[1:])
    (o_gdn, states, ainv, u4, w4), late = _gdn_fwd(qg, kg, vg, gates, B, S, gather)
    if late:
        w_up, w_down = late[0], late[1].reshape(-1, D)
    h2, mix = _mix_out(o_mla, o_gdn, proj, x2, mla_out_norm_w, gdn_w, w_out)
    up, hn, dy, sq = _mlp_fwd(h2, w_mn, w_up, w_down, t2)
    loss = (0.5 / D) * jnp.sum(sq[:, 0, 0])

    dh, dhb, dup, act, dyb, d_mlp_norm = _mlp_bwd(dy, up, h2, w_mn, w_up, w_down)
    g_w_down = _wgrad(act, dyb, "wgrad_down")
    g_w_up = _wgrad(hn, dup, "wgrad_up", column_shards=True)
    do_mla, do_gdn, dz, d_mla_w, d_gdn_w = _mix_bwd(dhb, o_mla, o_gdn, proj, mla_out_norm_w, gdn_w, w_out)
    g_w_out = _wgrad(mix, dhb, "wgrad_out")
    first = ("w_down",)
    second = ("w_out",)
    third = ("w_up", "w_uq", "w_ukv")
    mats = dict(w_up=g_w_up, w_down=g_w_down, w_out=g_w_out)

    def sending(names):
        return _Transfer("exchange", [_slabs(n, mats[n]) for n in names]) if exchange else None

    (dq4, dk4, dv4), got = _attn_bwd(q4, k4, v4, do_mla, o_mla, lse, B, S, sending(first))
    mats.update(zip(first, got))
    (dql, dkvl, dkpe, dqraw, dkvraw, qn, kvn, d_wqln, d_wkvln, d_qnw, d_knw), got = _mla_pre_bwd(
        proj, cosf, sinf, w_qln, w_kvln, w_uq_p, w_ukv, qnw, knw, dq4, dk4, dv4, sending(second))
    mats.update(zip(second, got))
    mats.update(w_uq=_wgrad(qn, dqraw, "wgrad_uq"), w_ukv=_wgrad(kvn, dkvraw, "wgrad_ukv"))
    (dqg, dkg, dvg, dgb4), got = _gdn_bwd(qg, kg, vg, gates, states, ainv, u4, w4, do_gdn, B, S, sending(third))
    mats.update(zip(third, got))
    dc, dgab, g_conv, d_alog, d_dt = _gdn_pre_bwd(proj, conv_w, alog_l, dt_l, dqg, dkg, dvg, dgb4, S)
    dgqkv = _conv_bwd_input(dc, conv_w, S)
    grad_x2, dproj, d_attn_norm = _in_proj_bwd(dgqkv, dz, dql, dkvl, dkpe, dgab, w_in_p, dh, x2, w_an)
    mats.update(w_in=_wgrad(xn, dproj, "wgrad_in"), conv_w=g_conv)
    if exchange:
        last = ("w_in", "conv_w")
        mats.update(zip(last, _exchange_grads([_slabs(n, mats[n]) for n in last])))
    small = dict(attn_norm_w=d_attn_norm, mlp_norm_w=d_mlp_norm, q_lat_norm_w=d_wqln, kv_lat_norm_w=d_wkvln,
                 q_norm_w=d_qnw, k_norm_w=d_knw, mla_out_norm_w=d_mla_w, a_log=d_alog, dt_bias=d_dt,
                 gdn_norm_w=d_gdn_w)
    return loss, grad_x2.reshape(B, S, D), mats, [small[n] for n, *_ in SMALL_LAYOUT]


BIG = ("w_in", "w_uq", "w_ukv", "conv_w", "w_out", "w_up", "w_down")
ALL_W = ("attn_norm_w", "w_in", "q_lat_norm_w", "w_uq", "kv_lat_norm_w", "w_ukv", "q_norm_w", "k_norm_w",
         "mla_out_norm_w", "conv_w", "a_log", "dt_bias", "gdn_norm_w", "w_out", "mlp_norm_w", "w_up", "w_down")
WIRE_SHAPE = {"w_in": (1024, 384), "w_uq": (256, 128), "conv_w": (16, 256)}


def _pad2(a, rows, cols):
    return jnp.pad(a, [(0, 0)] * (a.ndim - 2) + [(0, rows - a.shape[-2]), (0, cols - a.shape[-1])])


def _cols_to_full(stack, cols):
    return jnp.moveaxis(stack[:, :, :cols], 0, 1).reshape(stack.shape[1], N_DEV * cols)


def _full_to_cols(full, wire_cols):
    r, n = full.shape
    return _pad2(jnp.moveaxis(full.reshape(r, N_DEV, n // N_DEV), 1, 0), r, wire_cols)


def _slabs(name, g):
    if name == "w_in":
        return _full_to_cols(_w_in_from_padded(g), WIRE_SHAPE["w_in"][1])
    if name == "w_uq":
        return _full_to_cols(_w_uq_from_headsplit(g), WIRE_SHAPE["w_uq"][1])
    if name == "w_ukv":
        return _full_to_cols(g, g.shape[1] // N_DEV)
    if name == "conv_w":
        return _pad2(_full_to_cols(g.astype(WIRE_DTYPE), g.shape[1] // N_DEV), *WIRE_SHAPE["conv_w"])
    if name == "w_up":
        return g
    return g.reshape(N_DEV, -1, g.shape[-1])


def kernel(x, positions, attn_norm_w, w_in, q_lat_norm_w, w_uq, kv_lat_norm_w, w_ukv, q_norm_w, k_norm_w, mla_out_norm_w, conv_w, a_log, dt_bias, gdn_norm_w, w_out, mlp_norm_w, w_up, w_down, loss_target, m_attn_norm_w, m_w_in, m_q_lat_norm_w, m_w_uq, m_kv_lat_norm_w, m_w_ukv, m_q_norm_w, m_k_norm_w, m_mla_out_norm_w, m_conv_w, m_a_log, m_dt_bias, m_gdn_norm_w, m_w_out, m_mlp_norm_w, m_w_up, m_w_down, v_attn_norm_w, v_w_in, v_q_lat_norm_w, v_w_uq, v_kv_lat_norm_w, v_w_ukv, v_q_norm_w, v_k_norm_w, v_mla_out_norm_w, v_conv_w, v_a_log, v_dt_bias, v_gdn_norm_w, v_w_out, v_mlp_norm_w, v_w_up, v_w_down):
    env = dict(locals())
    W = {n: env[n][0] for n in ALL_W}
    Mo = {n: env["m_" + n][0] for n in ALL_W}
    Vo = {n: env["v_" + n][0] for n in ALL_W}

    two_d = lambda a: a.reshape(1, -1) if a.ndim == 1 else a
    D = x.shape[-1]

    s_in, s_uq, s_ukv, s_conv = _gather_weights([
        _pad2(W["w_in"].astype(WIRE_DTYPE), *WIRE_SHAPE["w_in"]),
        _pad2(W["w_uq"].astype(WIRE_DTYPE), *WIRE_SHAPE["w_uq"]),
        W["w_ukv"].astype(WIRE_DTYPE), _pad2(W["conv_w"], *WIRE_SHAPE["conv_w"])])
    late = [W["w_out"].astype(WIRE_DTYPE), W["w_up"].astype(WIRE_DTYPE), W["w_down"].astype(WIRE_DTYPE)]

    loss, grad_x, parts, gs = _local_step(
        x, positions, loss_target, two_d(W["attn_norm_w"]), _cols_to_full(s_in, W["w_in"].shape[1]),
        two_d(W["q_lat_norm_w"]), _cols_to_full(s_uq, W["w_uq"].shape[1]), two_d(W["kv_lat_norm_w"]),
        _cols_to_full(s_ukv, W["w_ukv"].shape[1]), two_d(W["q_norm_w"]), two_d(W["k_norm_w"]),
        W["mla_out_norm_w"], _cols_to_full(s_conv[:, :CONV_W], W["conv_w"].shape[1]), two_d(W["a_log"]),
        two_d(W["dt_bias"]), two_d(W["gdn_norm_w"]), None, two_d(W["mlp_norm_w"]), None, None,
        late_shards=late, exchange=True)
    done = {n: _reduce_adamw(parts[n], W[n], Mo[n], Vo[n], "adamw_" + n) for n in BIG}
    names = [n for n, *_ in SMALL_LAYOUT]
    tiles = _gather_small_grads(gs, jnp.full((1, LANES), loss, F32))
    small, loss = _adamw_replicated(tiles, [two_d(W[n]) for n in names], [two_d(Mo[n]) for n in names],
                                    [two_d(Vo[n]) for n in names])
    for i, n in enumerate(names):
        done[n] = [small[kind][i] for kind in range(4)]
    res = [done[n][kind].reshape(env[n].shape) for kind in range(4) for n in ALL_W]
    return (loss, grad_x, *res)
```

```python
import functools

import jax
import jax.numpy as jnp
from jax import lax
from jax.experimental import pallas as pl
from jax.experimental.pallas import tpu as pltpu

F32 = jnp.float32
MXU_DTYPE = jnp.bfloat16
WIRE_DTYPE = jnp.bfloat16
SDS = jax.ShapeDtypeStruct
HIGHEST = lax.Precision.HIGHEST
MESH_ID = pl.DeviceIdType.MESH

D_MODEL = 1024
MLA_HEADS = 4
Q_LORA = 256
KV_LORA = 256
NOPE = 128
ROPE = 64
QK_DIM = NOPE + ROPE
V_DIM = 128
ROPE_THETA = 10000.0
GDN_HEADS = 4
GDN_DIM = 128
GDN_WIDTH = GDN_HEADS * GDN_DIM
CONV_W = 4
CHUNK = 64
D_FF = 4 * D_MODEL
EPS = 1e-6
ATT_SCALE = QK_DIM ** -0.5
GDN_QSCALE = GDN_DIM ** -0.5
N_DEV = 8
ATTN_BLOCK = 512
ATTN_CHAINS = 2
MLP_FWD_SHARDS = 4
MLP_BWD_SHARDS = 4

ADAM_LR = 0.001
ADAM_B1 = 0.9
ADAM_B2 = 0.999
ADAM_EPS = 1e-08
ADAM_WD = 0.01
ADAM_STEP = 10

LANES = 128
SUBLANES = 8
VMEM_LIMIT = 60 * 1024 * 1024

P_GQKV, P_GZ, P_QLAT, P_KVLAT, P_KPE, P_GAB = 0, 1536, 2048, 2304, 2560, 2688
P_WIDTH = 2816
O_QLAT, O_KVLAT, O_KPE, O_GQKV, O_GZ, O_GAB, O_END = 0, 256, 512, 576, 2112, 2624, 2632


def _params(sem=None, vmem=VMEM_LIMIT):
    kw = dict(vmem_limit_bytes=vmem)
    if sem is not None:
        kw["dimension_semantics"] = sem
    return pltpu.CompilerParams(**kw)


def _mm(a, b):
    return jnp.dot(a.astype(MXU_DTYPE), b.astype(MXU_DTYPE), preferred_element_type=F32)


def _mm_nt(a, b):
    return lax.dot_general(a.astype(MXU_DTYPE), b.astype(MXU_DTYPE), (((1,), (1,)), ((), ())),
                           preferred_element_type=F32)


def _mm_tn(a, b):
    return lax.dot_general(a.astype(MXU_DTYPE), b.astype(MXU_DTYPE), (((0,), (0,)), ((), ())),
                           preferred_element_type=F32)


def _split(a):
    hi = a.astype(MXU_DTYPE)
    return hi, (a - hi.astype(F32)).astype(MXU_DTYPE)


def _mm_split(a, b):
    (ah, al), (bh, bl) = a, b
    dot = lambda x, y: jnp.dot(x, y, preferred_element_type=F32)
    if MXU_DTYPE == F32:
        return dot(ah, bh)
    return dot(ah, bh) + dot(ah, bl) + dot(al, bh)


def _mm_exact(a, b):
    return _mm_split(_split(a), _split(b))


def _row_sum(v, on_mxu=False):
    if not on_mxu:
        return jnp.sum(v, axis=-1, keepdims=True)
    d = v.shape[-1]
    ones = jnp.ones((d, LANES), MXU_DTYPE)
    s = sum(jnp.dot(p, ones, preferred_element_type=F32) for p in _split(v))
    return s[:, :d] if d <= LANES else jnp.tile(s, (1, d // LANES))


def _rms(x, w, on_mxu=False):
    r = lax.rsqrt(_row_sum(x * x, on_mxu) * (1.0 / x.shape[-1]) + EPS)
    return x * r * w, r


def _rms_bwd(dy, x, w, r, on_mxu=False):
    xh = x * r
    dyw = dy * w
    dx = r * (dyw - xh * (_row_sum(dyw * xh, on_mxu) * (1.0 / x.shape[-1])))
    dw = jnp.sum(dy * xh, axis=0, keepdims=True)
    return dx, dw


def _l2n(x, scale):
    return x * (lax.rsqrt(_row_sum(x * x) + EPS) * scale)


def _l2n_bwd(dy, x, scale):
    r = lax.rsqrt(_row_sum(x * x) + EPS)
    xh = x * r
    return (scale * r) * (dy - xh * _row_sum(dy * xh))


def _rot(t):
    return jnp.concatenate([-t[:, ROPE // 2:], t[:, :ROPE // 2]], axis=-1)


def _rot_t(t):
    return jnp.concatenate([t[:, ROPE // 2:], -t[:, :ROPE // 2]], axis=-1)


def _rope(t, cos, sin):
    return t * cos + _rot(t) * sin


def _rope_bwd(d, cos, sin):
    return d * cos + _rot_t(d * sin)


def _sigmoid(x):
    return jax.nn.sigmoid(x)


def _shift_down(x, halo, j):
    if j == 0:
        return x
    xr = pltpu.roll(x, j, 0)
    hr = pltpu.roll(halo, j, 0)
    row = lax.broadcasted_iota(jnp.int32, halo.shape, 0)
    top = jnp.where(row < j, hr, xr[:SUBLANES])
    return jnp.concatenate([top, xr[SUBLANES:]], axis=0)


def _shift_up(x, nxt, j):
    if j == 0:
        return x
    n = x.shape[0]
    xr = pltpu.roll(x, n - j, 0)
    nr = pltpu.roll(nxt, SUBLANES - j, 0)
    row = lax.broadcasted_iota(jnp.int32, nxt.shape, 0)
    bot = jnp.where(row >= SUBLANES - j, nr, xr[n - SUBLANES:])
    return jnp.concatenate([xr[:n - SUBLANES], bot], axis=0)


def _chunk_cumsum(y, row_in_chunk):
    s = 1
    while s < CHUNK:
        y = y + jnp.where(row_in_chunk >= s, pltpu.roll(y, s, 0), 0.0)
        s *= 2
    return y


def _chunk_rev_cumsum(y, row_in_chunk):
    n = y.shape[0]
    s = 1
    while s < CHUNK:
        y = y + jnp.where(row_in_chunk + s < CHUNK, pltpu.roll(y, n - s, 0), 0.0)
        s *= 2
    return y


def _together(generators):
    alive = list(generators)
    while alive:
        nxt = []
        for g in alive:
            try:
                next(g)
                nxt.append(g)
            except StopIteration:
                pass
        alive = nxt
        yield


def _lockstep(generators):
    for _ in _together(generators):
        pass


def _pick_lane(tile, lane, idx):
    return jnp.sum(jnp.where(lane == idx, tile, 0.0), axis=-1, keepdims=True)


def _divisor_tile(n, cap, unit=LANES):
    best = unit
    t = unit
    while t <= min(n, cap):
        if n % t == 0:
            best = t
        t += unit
    return n if n <= cap else best


def _in_proj(x2, w_an, w_in_p):
    T, D = x2.shape
    N = w_in_p.shape[1]
    tm = min(512, T)

    def body(x_ref, wn_ref, w_ref, proj_ref, xn_ref):
        xn, _ = _rms(x_ref[...], wn_ref[...])
        xn = xn.astype(MXU_DTYPE)
        xn_ref[...] = xn
        proj_ref[...] = jnp.dot(xn, w_ref[...], preferred_element_type=F32)

    return pl.pallas_call(
        body, grid=(T // tm,), name="in_proj",
        in_specs=[pl.BlockSpec((tm, D), lambda i: (i, 0)), pl.BlockSpec((1, D), lambda i: (0, 0)),
                  pl.BlockSpec((D, N), lambda i: (0, 0))],
        out_specs=[pl.BlockSpec((tm, N), lambda i: (i, 0)), pl.BlockSpec((tm, D), lambda i: (i, 0))],
        out_shape=[SDS((T, N), F32), SDS((T, D), MXU_DTYPE)],
        compiler_params=_params(("arbitrary",)),
    )(x2, w_an, w_in_p)


def _mla_pre(proj, cosf, sinf, w_qln, w_kvln, w_uq_p, w_ukv, qnw, knw, transfer=None):
    T = proj.shape[0]
    tm = min(256, T)
    H = MLA_HEADS

    def body(ql_ref, kvl_ref, kpe_ref, cos_ref, sin_ref, wq_ref, wkv_ref, uq_ref, ukv_ref, qnw_ref, knw_ref,
             q_out, k_out, v_out):
        rms = functools.partial(_rms, on_mxu=True)
        cos, sin = cos_ref[...], sin_ref[...]
        qnw_, knw_ = qnw_ref[...], knw_ref[...]
        qn, _ = rms(ql_ref[...], wq_ref[...])
        kvn, _ = rms(kvl_ref[...], wkv_ref[...])
        qraw = _mm(qn, uq_ref[...])
        kvraw = _mm(kvn, ukv_ref[...])
        kpe = _rope(rms(kpe_ref[...][:, :ROPE], knw_[:, NOPE:])[0], cos, sin)
        for h in range(H):
            qn_h = rms(qraw[:, h * NOPE:(h + 1) * NOPE], qnw_[:, :NOPE])[0]
            qp_h = _rope(rms(qraw[:, H * NOPE + h * ROPE:H * NOPE + (h + 1) * ROPE], qnw_[:, NOPE:])[0], cos, sin)
            q_out[h] = (jnp.concatenate([qn_h, qp_h], axis=-1) * ATT_SCALE).astype(MXU_DTYPE)
            kn_h = rms(kvraw[:, h * 256:h * 256 + NOPE], knw_[:, :NOPE])[0]
            k_out[h] = jnp.concatenate([kn_h, kpe], axis=-1).astype(MXU_DTYPE)
            v_out[h] = kvraw[:, h * 256 + NOPE:(h + 1) * 256].astype(MXU_DTYPE)

    full = lambda a: pl.BlockSpec(a.shape, lambda i: (0,) * a.ndim)
    return _call_beside(
        body, transfer, grid=(T // tm,), name="mla_pre", scratch_shapes=[], semantics=("arbitrary",),
        args=(proj, proj, proj, cosf, sinf, w_qln, w_kvln, w_uq_p, w_ukv, qnw, knw),
        in_specs=[pl.BlockSpec((tm, 256), lambda i: (i, P_QLAT // 256)),
                  pl.BlockSpec((tm, 256), lambda i: (i, P_KVLAT // 256)),
                  pl.BlockSpec((tm, 128), lambda i: (i, P_KPE // 128)),
                  pl.BlockSpec((tm, ROPE), lambda i: (i, 0)), pl.BlockSpec((tm, ROPE), lambda i: (i, 0)),
                  full(w_qln), full(w_kvln), full(w_uq_p), full(w_ukv), full(qnw), full(knw)],
        out_specs=[pl.BlockSpec((H, tm, QK_DIM), lambda i: (0, i, 0)),
                   pl.BlockSpec((H, tm, QK_DIM), lambda i: (0, i, 0)),
                   pl.BlockSpec((H, tm, V_DIM), lambda i: (0, i, 0))],
        out_shape=[SDS((H, T, QK_DIM), MXU_DTYPE), SDS((H, T, QK_DIM), MXU_DTYPE), SDS((H, T, V_DIM), MXU_DTYPE)])


def _attn_fwd(q4, k4, v4, B, S, transfer=None):
    H = MLA_HEADS
    bq = min(ATTN_BLOCK, S)
    nq = S // bq
    rows = bq // ATTN_CHAINS

    def body(q_ref, k_ref, v_ref, o_ref, lse_ref):
        col = lax.broadcasted_iota(jnp.int32, (rows, bq), 1)
        row = lax.broadcasted_iota(jnp.int32, (rows, bq), 0)

        def q_step(qi, carry):
            qs = pl.multiple_of(qi * bq, bq)
            qsub = [q_ref[0, pl.ds(qs + j * rows, rows), :] for j in range(ATTN_CHAINS)]

            def k_block(ks, cs, diagonal):
                k = k_ref[0, pl.ds(ks, bq), :]
                v = v_ref[0, pl.ds(ks, bq), :]
                out = [None] * ATTN_CHAINS

                def chain(j):
                    m, l, acc = cs[j]
                    s = _mm_nt(qsub[j], k)
                    yield
                    if diagonal:
                        s = jnp.where(col <= row + j * rows, s, -jnp.inf)
                    m_new = jnp.maximum(m, jnp.max(s, axis=-1, keepdims=True))
                    p = jnp.exp(s - m_new)
                    a = jnp.exp(m - m_new)
                    l_new = a * l + jnp.sum(p, axis=-1, keepdims=True)
                    yield
                    out[j] = (m_new, l_new, a * acc + _mm(p, v))

                _lockstep([chain(j) for j in range(ATTN_CHAINS)])
                return tuple(out)

            init = tuple((jnp.full((rows, 1), -jnp.inf, F32), jnp.zeros((rows, 1), F32),
                          jnp.zeros((rows, V_DIM), F32)) for _ in range(ATTN_CHAINS))
            cs = lax.fori_loop(0, qi, lambda kj, c: k_block(pl.multiple_of(kj * bq, bq), c, False), init)
            for j, (m, l, acc) in enumerate(k_block(qs, cs, True)):
                o_ref[0, pl.ds(qs + j * rows, rows), :] = acc / l
                lse_ref[0, pl.ds(qs + j * rows, rows), :] = m + jnp.log(l)
            return carry

        lax.fori_loop(0, nq, q_step, 0)

    spec = lambda d: pl.BlockSpec((1, S, d), lambda h, b: (h, b, 0))
    return _call_beside(
        body, transfer, grid=(H, B), name="attn_fwd",
        in_specs=[spec(QK_DIM), spec(QK_DIM), spec(V_DIM)],
        out_specs=[spec(V_DIM), spec(1)],
        out_shape=[SDS((H, B * S, V_DIM), F32), SDS((H, B * S, 1), F32)],
        scratch_shapes=[], semantics=("arbitrary", "arbitrary"), args=(q4, k4, v4))


def _conv_taps(u, halo, w):
    sh = [_shift_down(u, halo, j) for j in range(CONV_W)]
    c = w[0:1] * sh[3] + w[1:2] * sh[2] + w[2:3] * sh[1] + w[3:4] * sh[0]
    return c, sh


def _gate_values(gab, alog_l, dt_l, lane):
    g = -jnp.exp(alog_l) * jax.nn.softplus(gab + dt_l)
    g = jnp.where(lane < GDN_HEADS, g, 0.0)
    beta = jnp.where((lane >= GDN_HEADS) & (lane < 2 * GDN_HEADS), _sigmoid(gab), 0.0)
    return g, beta


def _gdn_pre(proj, conv_w, alog_l, dt_l, S):
    T = proj.shape[0]
    tm = min(256, T)
    tiles_per_seq = S // tm
    C3 = 3 * GDN_WIDTH
    H = GDN_HEADS

    def body(u_ref, halo_ref, gab_ref, w_ref, alog_ref, dt_ref, q_out, k_out, v_out, gates_out):
        i = pl.program_id(0)
        halo = jnp.where(i % tiles_per_seq == 0, 0.0, halo_ref[...])
        c, _ = _conv_taps(u_ref[...], halo, w_ref[...])
        a = c * _sigmoid(c)
        for h in range(H):
            xq = a[:, h * GDN_DIM:(h + 1) * GDN_DIM]
            xk = a[:, GDN_WIDTH + h * GDN_DIM:GDN_WIDTH + (h + 1) * GDN_DIM]
            q_out[h] = _l2n(xq, GDN_QSCALE)
            k_out[h] = _l2n(xk, 1.0)
            v_out[h] = a[:, 2 * GDN_WIDTH + h * GDN_DIM:2 * GDN_WIDTH + (h + 1) * GDN_DIM]
        lane = lax.broadcasted_iota(jnp.int32, (tm, LANES), 1)
        ric = lax.broadcasted_iota(jnp.int32, (tm, LANES), 0) % CHUNK
        g, beta = _gate_values(gab_ref[...], alog_ref[...], dt_ref[...], lane)
        gates_out[...] = _chunk_cumsum(g, ric) + beta

    hspec = pl.BlockSpec((H, tm, GDN_DIM), lambda i: (0, i, 0))
    return pl.pallas_call(
        body, grid=(T // tm,), name="gdn_pre",
        in_specs=[pl.BlockSpec((tm, C3), lambda i: (i, 0)),
                  pl.BlockSpec((SUBLANES, C3), lambda i: (jnp.maximum(i * (tm // SUBLANES) - 1, 0), 0)),
                  pl.BlockSpec((tm, LANES), lambda i: (i, P_GAB // LANES)),
                  pl.BlockSpec((CONV_W, C3), lambda i: (0, 0)),
                  pl.BlockSpec((1, LANES), lambda i: (0, 0)), pl.BlockSpec((1, LANES), lambda i: (0, 0))],
        out_specs=[hspec, hspec, hspec, pl.BlockSpec((tm, LANES), lambda i: (i, 0))],
        out_shape=[SDS((H, T, GDN_DIM), F32)] * 3 + [SDS((T, LANES), F32)],
        compiler_params=_params(("arbitrary",)),
    )(proj, proj, proj, conv_w, alog_l, dt_l)


def _unit_lower_inverses(Ls, eye):
    Ps = [eye - L for L in Ls]
    Ms = [_split(-L) for L in Ls]
    for _ in range(5):
        sq = [_mm_split(m, m) for m in Ms]
        Ms = [_split(s) for s in sq]
        Ps = [p + _mm_split(_split(p), m) for p, m in zip(Ps, Ms)]
    return Ps


def _chunk_decays(gt, lane, h, ri, ci, rcol):
    Gc = _pick_lane(gt, lane, h)
    bt = _pick_lane(gt, lane, h + GDN_HEADS)
    Gb = jnp.broadcast_to(Gc, (CHUNK, CHUNK))
    Gam = jnp.where(ri >= ci, jnp.exp(Gb - Gb.T), 0.0)
    Gl = jnp.sum(jnp.where(rcol == CHUNK - 1, Gc, 0.0), axis=0, keepdims=True)
    return Gc, bt, Gam, jnp.exp(Gc), jnp.exp(Gl - Gc), jnp.exp(Gl)


GDN_FWD_UNROLL = 16
GDN_BWD_UNROLL = 8
GDN_RECUR_STEPS_PER_STAGE = 2


def _gdn_fwd(qg, kg, vg, gates, B, S, transfer=None):
    H, D, C = GDN_HEADS, GDN_DIM, CHUNK
    NC = S // C
    U = GDN_FWD_UNROLL if NC % GDN_FWD_UNROLL == 0 else 1
    NG = NC // U

    def body(q_ref, k_ref, v_ref, g_ref, o_ref, st_ref, ai_ref, u_ref, w_ref, q2_s, au_s, bc_s, w2_s, el_s):
        h = pl.program_id(0)
        lane = lax.broadcasted_iota(jnp.int32, (C, LANES), 1)
        ri = lax.broadcasted_iota(jnp.int32, (C, C), 0)
        ci = lax.broadcasted_iota(jnp.int32, (C, C), 1)
        rcol = lax.broadcasted_iota(jnp.int32, (C, 1), 0)
        eye = (ri == ci).astype(F32)

        def group(gi, c):
            ns = [gi * U + j for j in range(U)]
            css = [pl.multiple_of(n * C, C) for n in ns]
            qs = [q_ref[0, pl.ds(cs, C), :] for cs in css]
            ks = [k_ref[0, pl.ds(cs, C), :] for cs in css]
            vs = [v_ref[0, pl.ds(cs, C), :] for cs in css]
            decs = [_chunk_decays(g_ref[pl.ds(cs, C), :], lane, h, ri, ci, rcol) for cs in css]
            qks = [_mm_nt(jnp.concatenate([q, k], axis=0), k) for q, k in zip(qs, ks)]
            ainvs = _unit_lower_inverses(
                [jnp.where(ri > ci, d[1] * qk[C:] * d[2], 0.0) for qk, d in zip(qks, decs)], eye)
            sols = [_mm_exact(a, jnp.concatenate([v * d[1], k * (d[1] * d[3])], axis=-1))
                    for a, k, v, d in zip(ainvs, ks, vs, decs)]
            atuw = [_mm(qk[:C] * d[2], sol) for qk, d, sol in zip(qks, decs, sols)]
            kduw = [_mm_tn(k * d[4], sol) for k, d, sol in zip(ks, decs, sols)]
            for n, cs, q, a, sol, au, ku, (Gc, bt, Gam, e, f, eL) in zip(ns, css, qs, ainvs, sols, atuw, kduw, decs):
                u_ref[0, pl.ds(cs, C), :] = sol[:, :D]
                w_ref[0, pl.ds(cs, C), :] = sol[:, D:]
                au_s[pl.ds(cs, C), :] = au[:, :D]
                q2_s[pl.ds(cs, C), :] = q * e - au[:, D:]
                bc_s[n] = ku[:, :D]
                w2_s[n] = ku[:, D:]
                el_s[n] = jnp.broadcast_to(eL, (SUBLANES, LANES))
                ai_ref[0, n] = a.T
            return c

        lax.fori_loop(0, NG, group, 0)

        def step(n, S_):
            cs = pl.multiple_of(n * C, C)
            o_ref[0, pl.ds(cs, C), :] = _mm(q2_s[pl.ds(cs, C), :], S_) + au_s[pl.ds(cs, C), :]
            st_ref[0, n] = S_
            return S_ * el_s[n, 0:1, :] + bc_s[n] - _mm(w2_s[n], S_)

        lax.fori_loop(0, NC, step, jnp.zeros((D, D), F32))

    spec = pl.BlockSpec((1, S, D), lambda h, b: (h, b, 0))
    return _call_beside(
        body, transfer, grid=(H, B), name="gdn_fwd",
        in_specs=[spec, spec, spec, pl.BlockSpec((S, LANES), lambda h, b: (b, 0))],
        out_specs=[spec, pl.BlockSpec((1, NC, D, D), lambda h, b: (h, b, 0, 0)),
                   pl.BlockSpec((1, NC, C, C), lambda h, b: (h, b, 0, 0)), spec, spec],
        out_shape=[SDS((H, B * S, D), F32), SDS((H, B * NC, D, D), F32), SDS((H, B * NC, C, C), F32),
                   SDS((H, B * S, D), F32), SDS((H, B * S, D), F32)],
        scratch_shapes=[pltpu.VMEM((S, D), F32), pltpu.VMEM((S, D), F32), pltpu.VMEM((NC, D, D), F32),
                        pltpu.VMEM((NC, D, D), F32), pltpu.VMEM((NC, SUBLANES, LANES), F32)],
        semantics=("arbitrary", "arbitrary"), args=(qg, kg, vg, gates))


def _mix_out(o_mla, o_gdn, proj, x2, mla_w, gdn_w, w_out):
    T, D = x2.shape
    tm = min(512, T)
    H = MLA_HEADS

    def body(om_ref, og_ref, z_ref, x_ref, mw_ref, gw_ref, w_ref, h_ref, mix_ref):
        z = z_ref[...]
        parts = [_rms(om_ref[h], mw_ref[h:h + 1, :])[0] for h in range(H)]
        for h in range(GDN_HEADS):
            zh = z[:, h * GDN_DIM:(h + 1) * GDN_DIM]
            parts.append(_rms(og_ref[h], gw_ref[...])[0] * (zh * _sigmoid(zh)))
        mix = jnp.concatenate(parts, axis=-1).astype(MXU_DTYPE)
        mix_ref[...] = mix
        h_ref[...] = x_ref[...] + jnp.dot(mix, w_ref[...], preferred_element_type=F32)

    hspec = pl.BlockSpec((H, tm, V_DIM), lambda i: (0, i, 0))
    return pl.pallas_call(
        body, grid=(T // tm,), name="mix_out",
        in_specs=[hspec, hspec, pl.BlockSpec((tm, GDN_WIDTH), lambda i: (i, P_GZ // GDN_WIDTH)),
                  pl.BlockSpec((tm, D), lambda i: (i, 0)),
                  pl.BlockSpec((H, V_DIM), lambda i: (0, 0)), pl.BlockSpec((1, GDN_DIM), lambda i: (0, 0)),
                  pl.BlockSpec((D, D), lambda i: (0, 0))],
        out_specs=[pl.BlockSpec((tm, D), lambda i: (i, 0)), pl.BlockSpec((tm, D), lambda i: (i, 0))],
        out_shape=[SDS((T, D), F32), SDS((T, D), MXU_DTYPE)],
        compiler_params=_params(("arbitrary",)),
    )(o_mla, o_gdn, proj, x2, mla_w, gdn_w, w_out)


def _mlp_fwd(h2, w_mn, w_up, w_down, target):
    T, D = h2.shape
    ns, _, ts = w_up.shape
    F = ns * ts
    tm = min(512, T)
    G = MLP_FWD_SHARDS
    tf, nf = G * ts, ns // G

    def body(h_ref, wn_ref, up_w, down_w, t_ref, up_ref, hn_ref, dy_ref, loss_ref, y_acc):
        j = pl.program_id(1)

        @pl.when(j == 0)
        def _():
            hn_ref[...] = _rms(h_ref[...], wn_ref[...])[0].astype(MXU_DTYPE)
            y_acc[...] = h_ref[...]

        parts = []
        for c in range(G):
            up = jnp.dot(hn_ref[...], up_w[c], preferred_element_type=F32)
            up_ref[:, c * ts:(c + 1) * ts] = up.astype(MXU_DTYPE)
            r = jnp.maximum(up, 0.0)
            parts.append(_mm(r * r, down_w[c * ts:(c + 1) * ts, :]))
        y_acc[...] += functools.reduce(jnp.add, parts)

        @pl.when(j == nf - 1)
        def _():
            err = y_acc[...] - t_ref[...]
            dy_ref[...] = err / D
            loss_ref[...] = jnp.full((1, SUBLANES, LANES), jnp.sum(err * err), F32)

    return pl.pallas_call(
        body, grid=(T // tm, nf), name="mlp_fwd",
        in_specs=[pl.BlockSpec((tm, D), lambda i, j: (i, 0)), pl.BlockSpec((1, D), lambda i, j: (0, 0)),
                  pl.BlockSpec((G, D, ts), lambda i, j: (j, 0, 0)), pl.BlockSpec((tf, D), lambda i, j: (j, 0)),
                  pl.BlockSpec((tm, D), lambda i, j: (i, 0))],
        out_specs=[pl.BlockSpec((tm, tf), lambda i, j: (i, j)), pl.BlockSpec((tm, D), lambda i, j: (i, 0)),
                   pl.BlockSpec((tm, D), lambda i, j: (i, 0)),
                   pl.BlockSpec((1, SUBLANES, LANES), lambda i, j: (i, 0, 0))],
        out_shape=[SDS((T, F), MXU_DTYPE), SDS((T, D), MXU_DTYPE), SDS((T, D), F32),
                   SDS((T // tm, SUBLANES, LANES), F32)],
        scratch_shapes=[pltpu.VMEM((tm, D), F32)],
        compiler_params=_params(("arbitrary", "arbitrary")),
    )(h2, w_mn, w_up, w_down, target)


def _mlp_bwd(dy, up, h2, w_mn, w_up, w_down):
    T, D = h2.shape
    ns, _, ts = w_up.shape
    F = ns * ts
    tm = min(512, T)
    G = MLP_BWD_SHARDS
    tf, nf = G * ts, ns // G

    def body(dy_ref, up_ref, h_ref, wn_ref, up_w, down_w, dh_ref, dhb_ref, dup_ref, act_ref, dyb_ref, dwn_ref, acc):
        i, j = pl.program_id(0), pl.program_id(1)

        @pl.when((i == 0) & (j == 0))
        def _():
            dwn_ref[...] = jnp.zeros_like(dwn_ref)

        @pl.when(j == 0)
        def _():
            acc[...] = jnp.zeros_like(acc)
            dyb_ref[...] = dy_ref[...].astype(MXU_DTYPE)

        parts = []
        for c in range(G):
            cols = slice(c * ts, (c + 1) * ts)
            r = jnp.maximum(up_ref[:, cols].astype(F32), 0.0)
            act_ref[:, cols] = (r * r).astype(MXU_DTYPE)
            dup = (_mm_nt(dyb_ref[...], down_w[cols, :]) * (2.0 * r)).astype(MXU_DTYPE)
            dup_ref[:, cols] = dup
            parts.append(_mm_nt(dup, up_w[c]))
        acc[...] += functools.reduce(jnp.add, parts)

        @pl.when(j == nf - 1)
        def _():
            hv = h_ref[...]
            _, rr = _rms(hv, wn_ref[...])
            dx, dw = _rms_bwd(acc[...], hv, wn_ref[...], rr)
            dh = dy_ref[...] + dx
            dh_ref[...] = dh
            dhb_ref[...] = dh.astype(MXU_DTYPE)
            dwn_ref[...] += dw

    row = lambda i, j: (i, 0)
    return pl.pallas_call(
        body, grid=(T // tm, nf), name="mlp_bwd",
        in_specs=[pl.BlockSpec((tm, D), row), pl.BlockSpec((tm, tf), lambda i, j: (i, j)), pl.BlockSpec((tm, D), row),
                  pl.BlockSpec((1, D), lambda i, j: (0, 0)),
                  pl.BlockSpec((G, D, ts), lambda i, j: (j, 0, 0)), pl.BlockSpec((tf, D), lambda i, j: (j, 0))],
        out_specs=[pl.BlockSpec((tm, D), row), pl.BlockSpec((tm, D), row),
                   pl.BlockSpec((tm, tf), lambda i, j: (i, j)), pl.BlockSpec((tm, tf), lambda i, j: (i, j)),
                   pl.BlockSpec((tm, D), row), pl.BlockSpec((1, D), lambda i, j: (0, 0))],
        out_shape=[SDS((T, D), F32), SDS((T, D), MXU_DTYPE), SDS((T, F), MXU_DTYPE), SDS((T, F), MXU_DTYPE),
                   SDS((T, D), MXU_DTYPE), SDS((1, D), F32)],
        scratch_shapes=[pltpu.VMEM((tm, D), F32)],
        compiler_params=_params(("arbitrary", "arbitrary")),
    )(dy, up, h2, w_mn, w_up, w_down)


def _mix_bwd(dhb, o_mla, o_gdn, proj, mla_w, gdn_w, w_out):
    T, D = dhb.shape
    tm = min(512, T)
    H = MLA_HEADS

    def body(dh_ref, om_ref, og_ref, z_ref, mw_ref, gw_ref, w_ref, dom_ref, dog_ref, dz_ref, dmw_ref, dgw_ref):
        @pl.when(pl.program_id(0) == 0)
        def _():
            dmw_ref[...] = jnp.zeros_like(dmw_ref)
            dgw_ref[...] = jnp.zeros_like(dgw_ref)

        dmix = _mm_nt(dh_ref[...], w_ref[...])
        z = z_ref[...]
        dmw, dzs = [], []
        dgw = jnp.zeros((1, GDN_DIM), F32)
        for h in range(H):
            o = om_ref[h]
            w = mw_ref[h:h + 1, :]
            _, r = _rms(o, w)
            dx, dw = _rms_bwd(dmix[:, h * V_DIM:(h + 1) * V_DIM], o, w, r)
            dom_ref[h] = dx
            dmw.append(dw)
        for h in range(GDN_HEADS):
            o = og_ref[h]
            w = gw_ref[...]
            zh = z[:, h * GDN_DIM:(h + 1) * GDN_DIM]
            sg = _sigmoid(zh)
            yn, r = _rms(o, w)
            dy = dmix[:, H * V_DIM + h * GDN_DIM:H * V_DIM + (h + 1) * GDN_DIM]
            dzs.append(dy * yn * (sg * (1.0 + zh * (1.0 - sg))))
            dx, dw = _rms_bwd(dy * (zh * sg), o, w, r)
            dog_ref[h] = dx
            dgw = dgw + dw
        dz_ref[...] = jnp.concatenate(dzs, axis=-1).astype(MXU_DTYPE)
        dmw_ref[...] += jnp.concatenate(dmw, axis=0)
        dgw_ref[...] += dgw

    hspec = pl.BlockSpec((H, tm, V_DIM), lambda i: (0, i, 0))
    return pl.pallas_call(
        body, grid=(T // tm,), name="mix_bwd",
        in_specs=[pl.BlockSpec((tm, D), lambda i: (i, 0)), hspec, hspec,
                  pl.BlockSpec((tm, GDN_WIDTH), lambda i: (i, P_GZ // GDN_WIDTH)),
                  pl.BlockSpec((H, V_DIM), lambda i: (0, 0)), pl.BlockSpec((1, GDN_DIM), lambda i: (0, 0)),
                  pl.BlockSpec((D, D), lambda i: (0, 0))],
        out_specs=[hspec, hspec, pl.BlockSpec((tm, GDN_WIDTH), lambda i: (i, 0)),
                   pl.BlockSpec((H, V_DIM), lambda i: (0, 0)), pl.BlockSpec((1, GDN_DIM), lambda i: (0, 0))],
        out_shape=[SDS((H, T, V_DIM), F32), SDS((H, T, GDN_DIM), F32), SDS((T, GDN_WIDTH), MXU_DTYPE),
                   SDS((H, V_DIM), F32), SDS((1, GDN_DIM), F32)],
        compiler_params=_params(("arbitrary",)),
    )(dhb, o_mla, o_gdn, proj, mla_w, gdn_w, w_out)


def _attn_bwd(q4, k4, v4, do4, o4, lse4, B, S, transfer=None):
    H = MLA_HEADS
    bq = min(ATTN_BLOCK, S)
    nq = S // bq
    rows = bq // ATTN_CHAINS

    def body(q_ref, k_ref, v_ref, do_ref, o_ref, lse_ref, dq_ref, dk_ref, dv_ref, delta):
        dq_ref[...] = jnp.zeros_like(dq_ref)
        dk_ref[...] = jnp.zeros_like(dk_ref)
        dv_ref[...] = jnp.zeros_like(dv_ref)
        delta[...] = jnp.sum(do_ref[0] * o_ref[0], axis=-1, keepdims=True)

        col = lax.broadcasted_iota(jnp.int32, (rows, bq), 1)
        row = lax.broadcasted_iota(jnp.int32, (rows, bq), 0)

        def k_step(kj, carry):
            ks = pl.multiple_of(kj * bq, bq)
            k = k_ref[0, pl.ds(ks, bq), :]
            v = v_ref[0, pl.ds(ks, bq), :]

            def q_block(qs, diagonal):
                dks, dvs = [None] * ATTN_CHAINS, [None] * ATTN_CHAINS

                def chain(j):
                    sl = pl.ds(qs + j * rows, rows)
                    q = q_ref[0, sl, :]
                    do = do_ref[0, sl, :].astype(MXU_DTYPE)
                    s = _mm_nt(q, k)
                    dp = _mm_nt(do, v)
                    yield
                    p = jnp.exp(s - lse_ref[0, sl, :])
                    if diagonal:
                        p = jnp.where(col <= row + j * rows, p, 0.0)
                    ds = p * (dp - delta[sl, :])
                    yield
                    dvs[j] = _mm_tn(p, do)
                    dks[j] = _mm_tn(ds, q)
                    dq_ref[0, sl, :] += _mm(ds, k)

                _lockstep([chain(j) for j in range(ATTN_CHAINS)])
                dv_ref[0, pl.ds(ks, bq), :] += functools.reduce(jnp.add, dvs)
                dk_ref[0, pl.ds(ks, bq), :] += functools.reduce(jnp.add, dks)

            q_block(ks, True)

            def q_step(qi, c):
                q_block(pl.multiple_of(qi * bq, bq), False)
                return c

            lax.fori_loop(kj + 1, nq, q_step, 0)
            return carry

        lax.fori_loop(0, nq, k_step, 0)

    spec = lambda d: pl.BlockSpec((1, S, d), lambda h, b: (h, b, 0))
    return _call_beside(
        body, transfer, grid=(H, B), name="attn_bwd",
        in_specs=[spec(QK_DIM), spec(QK_DIM), spec(V_DIM), spec(V_DIM), spec(V_DIM), spec(1)],
        out_specs=[spec(QK_DIM), spec(QK_DIM), spec(V_DIM)],
        out_shape=[SDS((H, B * S, QK_DIM), F32), SDS((H, B * S, QK_DIM), F32), SDS((H, B * S, V_DIM), F32)],
        scratch_shapes=[pltpu.VMEM((S, 1), F32)], semantics=("arbitrary", "arbitrary"),
        args=(q4, k4, v4, do4, o4, lse4))


def _gdn_bwd(qg, kg, vg, gates, states, ainv, u4, w4, do4, B, S, transfer=None):
    H, D, C = GDN_HEADS, GDN_DIM, CHUNK
    NC = S // C
    U = GDN_BWD_UNROLL if NC % GDN_BWD_UNROLL == 0 else 1
    NG = NC // U

    def body(q_ref, k_ref, v_ref, g_ref, st_ref, ai_ref, u_ref, w_ref, do_ref, dq_ref, dk_ref, dv_ref, dgb_ref,
             kd_s, x1_s, x2_s, el_s, dvn_s, ds_s, w2t_s):
        h = pl.program_id(0)
        lane = lax.broadcasted_iota(jnp.int32, (C, LANES), 1)
        ri = lax.broadcasted_iota(jnp.int32, (C, C), 0)
        ci = lax.broadcasted_iota(jnp.int32, (C, C), 1)
        rcol = lax.broadcasted_iota(jnp.int32, (C, 1), 0)

        def rsum(a):
            return jnp.sum(a, axis=-1, keepdims=True)

        def prepare(n):
            cs = n * C
            q = q_ref[0, pl.ds(cs, C), :]
            k = k_ref[0, pl.ds(cs, C), :]
            do = do_ref[0, pl.ds(cs, C), :]
            Gc, bt, Gam, e, f, eL = _chunk_decays(g_ref[pl.ds(cs, C), :], lane, h, ri, ci, rcol)
            At = _mm_nt(q, k) * Gam
            yield
            x1 = _mm_tn(At, do)
            x2 = _mm_tn(q * e, do)
            kd = k * f
            w = w_ref[0, pl.ds(cs, C), :]
            yield
            x1_s[pl.ds(cs, C), :] = x1
            x2_s[n] = x2 - _mm_tn(w, x1)
            w2t_s[n] = _mm_tn(w, kd)
            kd_s[pl.ds(cs, C), :] = kd
            el_s[n] = jnp.broadcast_to(eL, (SUBLANES, LANES))

        def recur(n, dS):
            cs = n * C
            ds_s[n] = dS
            dvn_s[pl.ds(cs, C), :] = x1_s[pl.ds(cs, C), :] + _mm(kd_s[pl.ds(cs, C), :], dS)
            return x2_s[n] + el_s[n, 0:1, :] * dS - _mm(w2t_s[n], dS)

        def local(n):
            cs = n * C
            q = q_ref[0, pl.ds(cs, C), :]
            k = k_ref[0, pl.ds(cs, C), :]
            v = v_ref[0, pl.ds(cs, C), :]
            do = do_ref[0, pl.ds(cs, C), :]
            u = u_ref[0, pl.ds(cs, C), :]
            w = w_ref[0, pl.ds(cs, C), :]
            dvn = dvn_s[pl.ds(cs, C), :]
            dS = ds_s[n]
            Gc, bt, Gam, e, f, eL = _chunk_decays(g_ref[pl.ds(cs, C), :], lane, h, ri, ci, rcol)
            S0 = st_ref[0, n]
            AinvT = ai_ref[0, n]
            qk = _mm_nt(jnp.concatenate([q, k], axis=0), k)
            QK, KK = qk[:C], qk[C:]
            be = bt * e
            sol = jnp.concatenate([u, w], axis=-1)
            vn = u - _mm(w, S0)
            yield
            dAt = jnp.where(ri >= ci, _mm_nt(do, vn), 0.0)
            dqd = _mm_nt(do, S0)
            dw = -_mm_nt(dvn, S0)
            dkd = _mm_nt(vn, dS)
            deL = jnp.sum(rsum(dS * S0), axis=0, keepdims=True)
            yield
            dR = _mm_exact(AinvT, jnp.concatenate([dvn, dw], axis=-1))
            dR1, dR2 = dR[:, :D], dR[:, D:]
            yield
            dL = jnp.where(ri > ci, -_mm_nt(dR, sol), 0.0)
            yield
            dv_ref[0, pl.ds(cs, C), :] = dR1 * bt
            r2 = rsum(dR2 * k)
            X = dL * Gam
            dbt = rsum(dR1 * v) + r2 * e + rsum(X * KK)
            de = r2 * bt + rsum(dqd * q)
            dKK = X * bt
            dQK = dAt * Gam
            dq_ref[0, pl.ds(cs, C), :] = _mm(dQK, k) + dqd * e
            dk_ref[0, pl.ds(cs, C), :] = dR2 * be + _mm(dKK + dKK.T, k) + _mm_tn(dQK, q) + dkd * f
            df = rsum(dkd * k)
            Z = (dL * (bt * KK) + dAt * QK) * Gam
            dG = rsum(Z) - rsum(Z.T) + de * e - df * f
            dGl = jnp.sum(df * f, axis=0, keepdims=True) + deL * eL
            dG = dG + jnp.where(rcol == C - 1, dGl, 0.0)
            dgb_ref[0, pl.ds(cs, C), :] = jnp.where(lane == 0, dG, jnp.where(lane == 1, dbt, 0.0))

        state = [jnp.zeros((D, D), F32)]

        def recur_group(g):
            for j, n in enumerate(reversed(range(g * U, (g + 1) * U))):
                state[0] = recur(n, state[0])
                if j % GDN_RECUR_STEPS_PER_STAGE == GDN_RECUR_STEPS_PER_STAGE - 1:
                    yield

        def stage(fn, g):
            return _together([fn(g * U + j) for j in range(U)])

        for step in range(NG + 2):
            jobs = [(stage, prepare, NG - 1 - step), (None, None, NG - step), (stage, local, NG + 1 - step)]
            _lockstep([recur_group(g) if make is None else make(fn, g) for make, fn, g in jobs if 0 <= g < NG])

    spec = pl.BlockSpec((1, S, D), lambda h, b: (h, b, 0))
    return _call_beside(
        body, transfer, grid=(H, B), name="gdn_bwd",
        in_specs=[spec, spec, spec, pl.BlockSpec((S, LANES), lambda h, b: (b, 0)),
                  pl.BlockSpec((1, NC, D, D), lambda h, b: (h, b, 0, 0)),
                  pl.BlockSpec((1, NC, C, C), lambda h, b: (h, b, 0, 0)), spec, spec, spec],
        out_specs=[spec, spec, spec, spec],
        out_shape=[SDS((H, B * S, D), F32)] * 4,
        scratch_shapes=[pltpu.VMEM((S, D), F32), pltpu.VMEM((S, D), F32), pltpu.VMEM((NC, D, D), F32),
                        pltpu.VMEM((NC, SUBLANES, LANES), F32), pltpu.VMEM((S, D), F32),
                        pltpu.VMEM((NC, D, D), F32), pltpu.VMEM((NC, D, D), F32)],
        semantics=("arbitrary", "arbitrary"), args=(qg, kg, vg, gates, states, ainv, u4, w4, do4))


def _gdn_pre_bwd(proj, conv_w, alog_l, dt_l, dq4, dk4, dv4, dgb4, S):
    T = proj.shape[0]
    tm = min(256, T)
    tiles_per_seq = S // tm
    C3 = 3 * GDN_WIDTH
    H = GDN_HEADS

    def body(u_ref, halo_ref, gab_ref, w_ref, alog_ref, dt_ref, dq_ref, dk_ref, dv_ref, dgb_ref,
             dc_ref, dgab_ref, dcw_ref, dalog_ref, ddt_ref):
        i = pl.program_id(0)

        @pl.when(i == 0)
        def _():
            dcw_ref[...] = jnp.zeros_like(dcw_ref)
            dalog_ref[...] = jnp.zeros_like(dalog_ref)
            ddt_ref[...] = jnp.zeros_like(ddt_ref)

        halo = jnp.where(i % tiles_per_seq == 0, 0.0, halo_ref[...])
        c, sh = _conv_taps(u_ref[...], halo, w_ref[...])
        sg = _sigmoid(c)
        a = c * sg
        das = [None] * (3 * H)
        for h in range(H):
            xq = a[:, h * GDN_DIM:(h + 1) * GDN_DIM]
            xk = a[:, GDN_WIDTH + h * GDN_DIM:GDN_WIDTH + (h + 1) * GDN_DIM]
            das[h] = _l2n_bwd(dq_ref[h], xq, GDN_QSCALE)
            das[H + h] = _l2n_bwd(dk_ref[h], xk, 1.0)
            das[2 * H + h] = dv_ref[h]
        dc = jnp.concatenate(das, axis=-1) * (sg * (1.0 + c * (1.0 - sg)))
        dc_ref[...] = dc
        dcw_ref[...] += jnp.concatenate(
            [jnp.sum(dc * sh[CONV_W - 1 - t], axis=0, keepdims=True) for t in range(CONV_W)], axis=0)
        lane = lax.broadcasted_iota(jnp.int32, (tm, LANES), 1)
        ric = lax.broadcasted_iota(jnp.int32, (tm, LANES), 0) % CHUNK
        dG = jnp.zeros((tm, LANES), F32)
        for h in range(H):
            t = dgb_ref[h]
            dG = dG + jnp.where(lane == h, _pick_lane(t, lane, 0), 0.0) \
                    + jnp.where(lane == h + H, _pick_lane(t, lane, 1), 0.0)
        is_g = lane < H
        dg = jnp.where(is_g, _chunk_rev_cumsum(jnp.where(is_g, dG, 0.0), ric), 0.0)
        gab = gab_ref[...]
        g, beta = _gate_values(gab, alog_ref[...], dt_ref[...], lane)
        dga = jnp.where(is_g, dg * (-jnp.exp(alog_ref[...])) * _sigmoid(gab + dt_ref[...]), 0.0)
        dgb = jnp.where(is_g, 0.0, dG) * beta * (1.0 - beta)
        dgab_ref[...] = (dga + dgb).astype(MXU_DTYPE)
        dalog_ref[...] += jnp.sum(dg * g, axis=0, keepdims=True)
        ddt_ref[...] += jnp.sum(dga, axis=0, keepdims=True)

    hspec = pl.BlockSpec((H, tm, GDN_DIM), lambda i: (0, i, 0))
    vec = pl.BlockSpec((1, LANES), lambda i: (0, 0))
    return pl.pallas_call(
        body, grid=(T // tm,), name="gdn_pre_bwd",
        in_specs=[pl.BlockSpec((tm, C3), lambda i: (i, 0)),
                  pl.BlockSpec((SUBLANES, C3), lambda i: (jnp.maximum(i * (tm // SUBLANES) - 1, 0), 0)),
                  pl.BlockSpec((tm, LANES), lambda i: (i, P_GAB // LANES)),
                  pl.BlockSpec((CONV_W, C3), lambda i: (0, 0)), vec, vec, hspec, hspec, hspec, hspec],
        out_specs=[pl.BlockSpec((tm, C3), lambda i: (i, 0)), pl.BlockSpec((tm, LANES), lambda i: (i, 0)),
                   pl.BlockSpec((CONV_W, C3), lambda i: (0, 0)), vec, vec],
        out_shape=[SDS((T, C3), F32), SDS((T, LANES), MXU_DTYPE), SDS((CONV_W, C3), F32),
                   SDS((1, LANES), F32), SDS((1, LANES), F32)],
        compiler_params=_params(("arbitrary",)),
    )(proj, proj, proj, conv_w, alog_l, dt_l, dq4, dk4, dv4, dgb4)


def _conv_bwd_input(dc, conv_w, S):
    T, C3 = dc.shape
    tm = min(256, T)
    tiles_per_seq = S // tm
    nblk = T // SUBLANES

    def body(dc_ref, nxt_ref, w_ref, du_ref):
        i = pl.program_id(0)
        nxt = jnp.where(i % tiles_per_seq == tiles_per_seq - 1, 0.0, nxt_ref[...])
        x = dc_ref[...]
        w = w_ref[...]
        du = w[3:4] * x
        for j in range(1, CONV_W):
            du = du + w[3 - j:4 - j] * _shift_up(x, nxt, j)
        du_ref[...] = du.astype(MXU_DTYPE)

    return pl.pallas_call(
        body, grid=(T // tm,), name="conv_bwd_input",
        in_specs=[pl.BlockSpec((tm, C3), lambda i: (i, 0)),
                  pl.BlockSpec((SUBLANES, C3), lambda i: (jnp.minimum((i + 1) * (tm // SUBLANES), nblk - 1), 0)),
                  pl.BlockSpec((CONV_W, C3), lambda i: (0, 0))],
        out_specs=pl.BlockSpec((tm, C3), lambda i: (i, 0)),
        out_shape=SDS((T, C3), MXU_DTYPE),
        compiler_params=_params(("arbitrary",)),
    )(dc, dc, conv_w)


def _mla_pre_bwd(proj, cosf, sinf, w_qln, w_kvln, w_uq_p, w_ukv, qnw, knw, dq4, dk4, dv4, transfer=None):
    T = proj.shape[0]
    tm = min(256, T)
    H = MLA_HEADS

    def body(ql_ref, kvl_ref, kpe_ref, cos_ref, sin_ref, wq_ref, wkv_ref, uq_ref, ukv_ref, qnw_ref, knw_ref,
             dq_ref, dk_ref, dv_ref,
             dql_ref, dkvl_ref, dkpe_ref, dqraw_ref, dkvraw_ref, qn_ref, kvn_ref, dwq_ref, dwkv_ref, dqnw_ref, dknw_ref):
        @pl.when(pl.program_id(0) == 0)
        def _():
            for r in (dwq_ref, dwkv_ref, dqnw_ref, dknw_ref):
                r[...] = jnp.zeros_like(r)

        cos, sin = cos_ref[...], sin_ref[...]
        qnw_, knw_ = qnw_ref[...], knw_ref[...]
        ql, kvl = ql_ref[...], kvl_ref[...]
        kpe_raw = kpe_ref[...][:, :ROPE]
        rms = functools.partial(_rms, on_mxu=True)
        rms_bwd = functools.partial(_rms_bwd, on_mxu=True)
        qn, rq = rms(ql, wq_ref[...])
        kvn, rkv = rms(kvl, wkv_ref[...])
        qn_ref[...] = qn.astype(MXU_DTYPE)
        kvn_ref[...] = kvn.astype(MXU_DTYPE)
        qraw = _mm(qn, uq_ref[...])
        kvraw = _mm(kvn, ukv_ref[...])
        dq_nope, dq_pe, dkv_parts = [], [], []
        dqnw_n = jnp.zeros((1, NOPE), F32)
        dqnw_p = jnp.zeros((1, ROPE), F32)
        dknw_n = jnp.zeros((1, NOPE), F32)
        dkpe = jnp.zeros((tm, ROPE), F32)
        for h in range(H):
            dq = dq_ref[h] * ATT_SCALE
            x = qraw[:, h * NOPE:(h + 1) * NOPE]
            dx, dw = rms_bwd(dq[:, :NOPE], x, qnw_[:, :NOPE], rms(x, qnw_[:, :NOPE])[1])
            dq_nope.append(dx)
            dqnw_n = dqnw_n + dw
            x = qraw[:, H * NOPE + h * ROPE:H * NOPE + (h + 1) * ROPE]
            dx, dw = rms_bwd(_rope_bwd(dq[:, NOPE:], cos, sin), x, qnw_[:, NOPE:], rms(x, qnw_[:, NOPE:])[1])
            dq_pe.append(dx)
            dqnw_p = dqnw_p + dw
            dk = dk_ref[h]
            x = kvraw[:, h * 256:h * 256 + NOPE]
            dx, dw = rms_bwd(dk[:, :NOPE], x, knw_[:, :NOPE], rms(x, knw_[:, :NOPE])[1])
            dknw_n = dknw_n + dw
            dkpe = dkpe + dk[:, NOPE:]
            dkv_parts += [dx, dv_ref[h]]
        dx, dknw_p = rms_bwd(_rope_bwd(dkpe, cos, sin), kpe_raw, knw_[:, NOPE:], rms(kpe_raw, knw_[:, NOPE:])[1])
        dkpe_ref[...] = jnp.concatenate([dx, jnp.zeros((tm, LANES - ROPE), F32)], axis=-1).astype(MXU_DTYPE)
        dqraw = jnp.concatenate(dq_nope + dq_pe, axis=-1).astype(MXU_DTYPE)
        dkvraw = jnp.concatenate(dkv_parts, axis=-1).astype(MXU_DTYPE)
        dqraw_ref[...] = dqraw
        dkvraw_ref[...] = dkvraw
        dx, dw = rms_bwd(_mm_nt(dqraw, uq_ref[...]), ql, wq_ref[...], rq)
        dql_ref[...] = dx.astype(MXU_DTYPE)
        dwq_ref[...] += dw
        dx, dw = rms_bwd(_mm_nt(dkvraw, ukv_ref[...]), kvl, wkv_ref[...], rkv)
        dkvl_ref[...] = dx.astype(MXU_DTYPE)
        dwkv_ref[...] += dw
        dqnw_ref[...] += jnp.concatenate([dqnw_n, dqnw_p], axis=-1)
        dknw_ref[...] += jnp.concatenate([dknw_n, dknw_p], axis=-1)

    full = lambda a: pl.BlockSpec(a.shape, lambda i: (0,) * a.ndim)
    rows = lambda n: pl.BlockSpec((tm, n), lambda i: (i, 0))
    const = lambda n: pl.BlockSpec((1, n), lambda i: (0, 0))
    NQ, NKV = w_uq_p.shape[1], w_ukv.shape[1]
    return _call_beside(
        body, transfer, grid=(T // tm,), name="mla_pre_bwd", scratch_shapes=[], semantics=("arbitrary",),
        args=(proj, proj, proj, cosf, sinf, w_qln, w_kvln, w_uq_p, w_ukv, qnw, knw, dq4, dk4, dv4),
        in_specs=[pl.BlockSpec((tm, 256), lambda i: (i, P_QLAT // 256)),
                  pl.BlockSpec((tm, 256), lambda i: (i, P_KVLAT // 256)),
                  pl.BlockSpec((tm, 128), lambda i: (i, P_KPE // 128)),
                  rows(ROPE), rows(ROPE),
                  full(w_qln), full(w_kvln), full(w_uq_p), full(w_ukv), full(qnw), full(knw),
                  pl.BlockSpec((H, tm, QK_DIM), lambda i: (0, i, 0)),
                  pl.BlockSpec((H, tm, QK_DIM), lambda i: (0, i, 0)),
                  pl.BlockSpec((H, tm, V_DIM), lambda i: (0, i, 0))],
        out_specs=[rows(Q_LORA), rows(KV_LORA), rows(LANES), rows(NQ), rows(NKV), rows(Q_LORA), rows(KV_LORA),
                   const(Q_LORA), const(KV_LORA), const(QK_DIM), const(QK_DIM)],
        out_shape=[SDS((T, Q_LORA), MXU_DTYPE), SDS((T, KV_LORA), MXU_DTYPE), SDS((T, LANES), MXU_DTYPE),
                   SDS((T, NQ), MXU_DTYPE), SDS((T, NKV), MXU_DTYPE),
                   SDS((T, Q_LORA), MXU_DTYPE), SDS((T, KV_LORA), MXU_DTYPE),
                   SDS((1, Q_LORA), F32), SDS((1, KV_LORA), F32), SDS((1, QK_DIM), F32), SDS((1, QK_DIM), F32)])


def _in_proj_bwd(dgqkv, dgz, dql, dkvl, dkpe, dgab, w_in_p, dh, x2, w_an):
    T, D = x2.shape
    N = w_in_p.shape[1]
    tm = min(512, T)

    def body(a_ref, b_ref, c_ref, d_ref, e_ref, f_ref, w_ref, dh_ref, x_ref, wn_ref, dx_ref, dp_ref, dwn_ref):
        @pl.when(pl.program_id(0) == 0)
        def _():
            dwn_ref[...] = jnp.zeros_like(dwn_ref)

        dp = jnp.concatenate([a_ref[...], b_ref[...], c_ref[...], d_ref[...], e_ref[...], f_ref[...]],
                             axis=-1).astype(MXU_DTYPE)
        dp_ref[...] = dp
        x = x_ref[...]
        _, r = _rms(x, wn_ref[...])
        dx, dw = _rms_bwd(_mm_nt(dp, w_ref[...]), x, wn_ref[...], r)
        dx_ref[...] = dh_ref[...] + dx
        dwn_ref[...] += dw

    rows = lambda n: pl.BlockSpec((tm, n), lambda i: (i, 0))
    return pl.pallas_call(
        body, grid=(T // tm,), name="in_proj_bwd",
        in_specs=[rows(dgqkv.shape[1]), rows(dgz.shape[1]), rows(dql.shape[1]), rows(dkvl.shape[1]),
                  rows(dkpe.shape[1]), rows(dgab.shape[1]),
                  pl.BlockSpec((D, N), lambda i: (0, 0)), rows(D), rows(D), pl.BlockSpec((1, D), lambda i: (0, 0))],
        out_specs=[rows(D), rows(N), pl.BlockSpec((1, D), lambda i: (0, 0))],
        out_shape=[SDS((T, D), F32), SDS((T, N), MXU_DTYPE), SDS((1, D), F32)],
        compiler_params=_params(("arbitrary",)),
    )(dgqkv, dgz, dql, dkvl, dkpe, dgab, w_in_p, dh, x2, w_an)


def _wgrad(a, b, name, column_shards=False):
    T, M = a.shape
    N = b.shape[1]
    tM = _divisor_tile(M, 1024)
    tN = N // N_DEV if column_shards else _divisor_tile(N, 1536)
    tk = min(T, 2048)
    nk = T // tk

    def body(a_ref, b_ref, o_ref, acc):
        k = pl.program_id(2)

        @pl.when(k == 0)
        def _():
            acc[...] = jnp.zeros_like(acc)

        acc[...] += _mm_tn(a_ref[...], b_ref[...])

        @pl.when(k == nk - 1)
        def _():
            o_ref[...] = acc[...].astype(WIRE_DTYPE).reshape(o_ref.shape)

    if column_shards:
        out_spec, out_shape = pl.BlockSpec((1, tM, tN), lambda i, j, k: (j, i, 0)), SDS((N_DEV, M, tN), WIRE_DTYPE)
    else:
        out_spec, out_shape = pl.BlockSpec((tM, tN), lambda i, j, k: (i, j)), SDS((M, N), WIRE_DTYPE)
    return pl.pallas_call(
        body, grid=(M // tM, N // tN, nk), name=name,
        in_specs=[pl.BlockSpec((tk, tM), lambda i, j, k: (k, i)), pl.BlockSpec((tk, tN), lambda i, j, k: (k, j))],
        out_specs=out_spec, out_shape=out_shape,
        scratch_shapes=[pltpu.VMEM((tM, tN), F32)],
        compiler_params=_params(("arbitrary", "arbitrary", "arbitrary")),
    )(a, b)


def _adamw(g, w, m, v):
    m = ADAM_B1 * m + (1.0 - ADAM_B1) * g
    v = ADAM_B2 * v + (1.0 - ADAM_B2) * jnp.square(g)
    m_hat = m / (1.0 - ADAM_B1 ** ADAM_STEP)
    v_hat = v / (1.0 - ADAM_B2 ** ADAM_STEP)
    return -ADAM_LR * (m_hat / (jnp.sqrt(v_hat) + ADAM_EPS) + ADAM_WD * w), m, v


def _reduce_adamw(parts, w, m, v, name):
    R, C = w.shape
    _, Rp, Cp = parts.shape
    tr = min(R, 256)
    tp = tr if Rp == R else Rp

    def body(p_ref, w_ref, m_ref, v_ref, g_ref, d_ref, nm_ref, nv_ref):
        g = p_ref[0].astype(F32)
        for s in range(1, N_DEV):
            g = g + p_ref[s].astype(F32)
        g = g[:tr, :C]
        g_ref[...] = g
        d_ref[...], nm_ref[...], nv_ref[...] = _adamw(g, w_ref[...], m_ref[...], v_ref[...])

    spec = pl.BlockSpec((tr, C), lambda i: (i, 0))
    return pl.pallas_call(
        body, grid=(R // tr,), name=name,
        in_specs=[pl.BlockSpec((N_DEV, tp, Cp), lambda i: (0, i, 0)), spec, spec, spec],
        out_specs=[spec] * 4, out_shape=[SDS((R, C), F32)] * 4,
        compiler_params=_params(("arbitrary",)),
    )(parts, w, m, v)


SMALL_ROWS, SMALL_COLS = 16, 1024
SMALL_LAYOUT = (
    ("attn_norm_w", 0, 1, 1024, 1024), ("mlp_norm_w", 1, 1, 1024, 1024), ("q_lat_norm_w", 2, 1, 256, 256),
    ("kv_lat_norm_w", 3, 1, 256, 256), ("q_norm_w", 4, 1, 192, 192), ("k_norm_w", 5, 1, 192, 192),
    ("mla_out_norm_w", 6, 4, 128, 128), ("a_log", 10, 1, 128, 4), ("dt_bias", 11, 1, 128, 4),
    ("gdn_norm_w", 12, 1, 128, 128))
LOSS_ENTRY = ("loss", 13, 1, 128, 128)


def _adamw_replicated(parts, ws, ms, vs):
    n = len(SMALL_LAYOUT)

    def body(*refs):
        p_ref = refs[0]
        w_refs, m_refs, v_refs = refs[1:1 + n], refs[1 + n:1 + 2 * n], refs[1 + 2 * n:1 + 3 * n]
        outs = refs[1 + 3 * n:]
        s = p_ref[0]
        for d in range(1, N_DEV):
            s = s + p_ref[d]
        for i, (_, r0, nr, _, pw) in enumerate(SMALL_LAYOUT):
            g = s[r0:r0 + nr, :pw]
            outs[i][...] = g
            outs[n + i][...], outs[2 * n + i][...], outs[3 * n + i][...] = _adamw(
                g, w_refs[i][...], m_refs[i][...], v_refs[i][...])
        _, r0, nr, gw, _ = LOSS_ENTRY
        outs[4 * n][...] = s[r0:r0 + nr, :gw]

    res = pl.pallas_call(
        body, name="adamw_replicated",
        out_shape=[SDS(w.shape, F32) for w in ws] * 4 + [SDS((1, LANES), F32)],
        compiler_params=_params(),
    )(parts, *ws, *ms, *vs)
    return [res[k * n:(k + 1) * n] for k in range(4)], res[4 * n][0, 0]


COPIES_PER_ARRAY = N_DEV - 1


def _two_level_gather(srcs, outs, send_sems, recv_sems, local_sems=None, stage="all"):
    mx, my, mc = lax.axis_index("x"), lax.axis_index("y"), lax.axis_index("c")
    me, sibling = (mx, my, mc), (mx, my, 1 - mc)
    chips = [(1 - mx, my), (mx, 1 - my), (1 - mx, 1 - my)]
    arrays = range(len(srcs))

    def copy(a, k, block, to, src=None):
        px, py, pc = block
        slot = outs[a].at[4 * px + 2 * py + pc]
        sem = a * COPIES_PER_ARRAY + k
        return pltpu.make_async_remote_copy(
            src_ref=slot if src is None else src, dst_ref=slot,
            send_sem=send_sems.at[sem], recv_sem=recv_sems.at[sem], device_id=to, device_id_type=MESH_ID)

    mine = [] if local_sems is None else [
        pltpu.make_async_copy(srcs[a], outs[a].at[4 * mx + 2 * my + mc], local_sems.at[a]) for a in arrays]
    first = []
    for a in arrays:
        first.append(copy(a, 0, me, sibling, src=srcs[a]))
        first += [copy(a, 1 + j, me, (*chip, mc), src=srcs[a]) for j, chip in enumerate(chips)]
    if stage in ("all", "start"):
        for cp in mine + first:
            cp.start()
    if stage in ("all", "finish"):
        forwards = []
        for j, chip in enumerate(chips):
            for a in arrays:
                copy(a, 1 + j, (*chip, mc), me).wait_recv()
                fwd = copy(a, 4 + j, (*chip, mc), sibling)
                fwd.start()
                forwards.append(fwd)
        for a in arrays:
            copy(a, 0, sibling, me).wait_recv()
        for j, chip in enumerate(chips):
            for a in arrays:
                copy(a, 4 + j, (*chip, 1 - mc), me).wait_recv()
        for cp in first + forwards:
            cp.wait_send()
        for cp in mine:
            cp.wait()


def _comm_scratch(n):
    return [pltpu.SemaphoreType.DMA((n * COPIES_PER_ARRAY,)), pltpu.SemaphoreType.DMA((n * COPIES_PER_ARRAY,)),
            pltpu.SemaphoreType.DMA((n,))]


def _any_specs(n):
    return [pl.BlockSpec(memory_space=pl.ANY)] * n


def _gather_weights(shards):
    n = len(shards)

    def body(*refs):
        _two_level_gather(refs[:n], refs[n:2 * n], *refs[2 * n:])

    return pl.pallas_call(
        body, name="gather_weights",
        out_shape=[SDS((N_DEV,) + s.shape, s.dtype) for s in shards],
        in_specs=_any_specs(n), out_specs=_any_specs(n), scratch_shapes=_comm_scratch(n),
    )(*shards)


def _gather_small_grads(gs, loss_lanes):
    gs = list(gs) + [loss_lanes]
    n = len(gs)

    def body(*refs):
        g_refs, out_ref = refs[:n], refs[n]
        tile, send_sems, recv_sems = refs[n + 1:]
        tile[...] = jnp.zeros_like(tile)
        for (_, r0, nr, gw, _), g in zip(SMALL_LAYOUT + (LOSS_ENTRY,), g_refs):
            tile[r0:r0 + nr, 0:gw] = g[...]
        me = 4 * lax.axis_index("x") + 2 * lax.axis_index("y") + lax.axis_index("c")
        out_ref[me] = tile[...]
        _two_level_gather([tile], [out_ref], send_sems, recv_sems)

    return pl.pallas_call(
        body, name="gather_small_grads",
        out_shape=SDS((N_DEV, SMALL_ROWS, SMALL_COLS), F32),
        in_specs=[pl.BlockSpec(memory_space=pltpu.VMEM)] * n,
        out_specs=pl.BlockSpec(memory_space=pltpu.VMEM),
        scratch_shapes=[pltpu.VMEM((SMALL_ROWS, SMALL_COLS), F32),
                        pltpu.SemaphoreType.DMA((COPIES_PER_ARRAY,)), pltpu.SemaphoreType.DMA((COPIES_PER_ARRAY,))],
    )(*gs)


def _exchange_grads(slabs):
    n = len(slabs)

    def body(*refs):
        _exchange(refs[:n], refs[n:2 * n], *refs[2 * n:])

    return pl.pallas_call(
        body, name="exchange_grads",
        out_shape=[SDS(s.shape, s.dtype) for s in slabs],
        in_specs=_any_specs(n), out_specs=_any_specs(n), scratch_shapes=_comm_scratch(n),
    )(*slabs)


class _Transfer:
    def __init__(self, kind, arrays):
        self.kind, self.arrays, self.n = kind, list(arrays), len(arrays)

    def out_shapes(self):
        if self.kind == "gather":
            return [SDS((N_DEV,) + a.shape, a.dtype) for a in self.arrays]
        return [SDS(a.shape, a.dtype) for a in self.arrays]

    def run(self, srcs, outs, sems, stage):
        fn = _two_level_gather if self.kind == "gather" else _exchange
        fn(srcs, outs, *sems, stage=stage)


def _call_beside(body, transfer, *, grid, in_specs, out_specs, out_shape, scratch_shapes, name, semantics, args):
    if transfer is None:
        res = pl.pallas_call(body, grid=grid, in_specs=in_specs, out_specs=out_specs, out_shape=out_shape,
                             scratch_shapes=scratch_shapes, name=name, compiler_params=_params(semantics))(*args)
        return list(res), []
    n_in, n_out, n_s, n = len(in_specs), len(out_specs), len(scratch_shapes), transfer.n

    def wrapped(*refs):
        ins, refs = refs[:n_in], refs[n_in:]
        t_in, refs = refs[:n], refs[n:]
        outs, refs = refs[:n_out], refs[n_out:]
        t_out, refs = refs[:n], refs[n:]
        scratch, sems = refs[:n_s], refs[n_s:]
        first = functools.reduce(jnp.logical_and, [pl.program_id(i) == 0 for i in range(len(grid))])
        last = functools.reduce(jnp.logical_and, [pl.program_id(i) == g - 1 for i, g in enumerate(grid)])

        @pl.when(first)
        def _():
            transfer.run(t_in, t_out, sems, "start")

        body(*ins, *outs, *scratch)

        @pl.when(last)
        def _():
            transfer.run(t_in, t_out, sems, "finish")

    res = pl.pallas_call(
        wrapped, grid=grid, in_specs=list(in_specs) + _any_specs(n), out_specs=list(out_specs) + _any_specs(n),
        out_shape=list(out_shape) + transfer.out_shapes(), scratch_shapes=list(scratch_shapes) + _comm_scratch(n),
        name=name, compiler_params=_params(semantics))(*args, *transfer.arrays)
    return list(res[:n_out]), list(res[n_out:])


EXCHANGE_FLIPS = ((0, 0, 1), (1, 0, 0), (0, 1, 0), (1, 1, 0), (1, 0, 1), (0, 1, 1), (1, 1, 1))


def _exchange(srcs, outs, send_sems, recv_sems, local_sems, stage="all"):
    mx, my, mc = lax.axis_index("x"), lax.axis_index("y"), lax.axis_index("c")
    arrays = range(len(srcs))
    copies = [pltpu.make_async_copy(srcs[a].at[4 * mx + 2 * my + mc], outs[a].at[N_DEV - 1], local_sems.at[a])
              for a in arrays]
    for k, (fx, fy, fc) in enumerate(EXCHANGE_FLIPS):
        px = 1 - mx if fx else mx
        py = 1 - my if fy else my
        pc = 1 - mc if fc else mc
        for a in arrays:
            sem = a * COPIES_PER_ARRAY + k
            copies.append(pltpu.make_async_remote_copy(
                src_ref=srcs[a].at[4 * px + 2 * py + pc], dst_ref=outs[a].at[k],
                send_sem=send_sems.at[sem], recv_sem=recv_sems.at[sem],
                device_id=(px, py, pc), device_id_type=MESH_ID))
    if stage in ("all", "start"):
        for cp in copies:
            cp.start()
    if stage in ("all", "finish"):
        for cp in copies:
            cp.wait()


def _w_in_to_padded(w):
    z = lambda n: jnp.zeros((w.shape[0], n), w.dtype)
    return jnp.concatenate([w[:, O_GQKV:O_GZ], w[:, O_GZ:O_GAB], w[:, O_QLAT:O_KVLAT], w[:, O_KVLAT:O_KPE],
                            w[:, O_KPE:O_GQKV], z(P_GAB - P_KPE - ROPE), w[:, O_GAB:O_END],
                            z(P_WIDTH - P_GAB - (O_END - O_GAB))], axis=1)


def _w_in_from_padded(wp):
    return jnp.concatenate([wp[:, P_QLAT:P_QLAT + 256], wp[:, P_KVLAT:P_KVLAT + 256], wp[:, P_KPE:P_KPE + ROPE],
                            wp[:, P_GQKV:P_GZ], wp[:, P_GZ:P_QLAT], wp[:, P_GAB:P_GAB + (O_END - O_GAB)]], axis=1)


def _w_uq_to_headsplit(w):
    w3 = w.reshape(w.shape[0], MLA_HEADS, QK_DIM)
    return jnp.concatenate([w3[:, :, :NOPE].reshape(w.shape[0], -1), w3[:, :, NOPE:].reshape(w.shape[0], -1)], axis=1)


def _w_uq_from_headsplit(wp):
    n = wp[:, :MLA_HEADS * NOPE].reshape(wp.shape[0], MLA_HEADS, NOPE)
    p = wp[:, MLA_HEADS * NOPE:].reshape(wp.shape[0], MLA_HEADS, ROPE)
    return jnp.concatenate([n, p], axis=2).reshape(wp.shape[0], -1)


def _lane_vec(v4):
    return jnp.pad(v4.reshape(1, -1), ((0, 0), (0, LANES - v4.shape[-1])))


def _local_step(x, positions, target, attn_norm_w, w_in, q_lat_norm_w, w_uq, kv_lat_norm_w, w_ukv, q_norm_w,
                k_norm_w, mla_out_norm_w, conv_w, a_log, dt_bias, gdn_norm_w, w_out, mlp_norm_w, w_up, w_down,
                late_shards=None, exchange=False):
    B, S, D = x.shape
    T = B * S
    x2 = x.reshape(T, D)
    t2 = target.reshape(T, D)
    half = ROPE // 2
    inv_freq = ROPE_THETA ** (-jnp.arange(half, dtype=F32) / half)
    ang = positions.reshape(T, 1).astype(F32) * inv_freq
    cosf = jnp.concatenate([jnp.cos(ang)] * 2, axis=-1)
    sinf = jnp.concatenate([jnp.sin(ang)] * 2, axis=-1)
    w_in_p = _w_in_to_padded(w_in)
    w_uq_p = _w_uq_to_headsplit(w_uq)
    alog_l, dt_l = _lane_vec(a_log), _lane_vec(dt_bias)
    w_an, w_qln, w_kvln, qnw, knw, w_mn, gdn_w = (
        attn_norm_w, q_lat_norm_w, kv_lat_norm_w, q_norm_w, k_norm_w, mlp_norm_w, gdn_norm_w)

    proj, xn = _in_proj(x2, w_an, w_in_p)
    gather = None if late_shards is None else _Transfer("gather", late_shards[:1])
    (q4, k4, v4), late = _mla_pre(proj, cosf, sinf, w_qln, w_kvln, w_uq_p, w_ukv, qnw, knw, gather)
    if late:
        w_out = late[0].reshape(-1, D)
    (o_mla, lse), _ = _attn_fwd(q4, k4, v4, B, S)
    qg, kg, vg, gates = _gdn_pre(proj, conv_w, alog_l, dt_l, S)
    gather = None if late_shards is None else _Transfer("gather", late_shards[1:])
    (o_gdn, states, ainv, u4, w4), late = _gdn_fwd(qg, kg, vg, gates, B, S, gather)
    if late:
        w_up, w_down = late[0], late[1].reshape(-1, D)
    h2, mix = _mix_out(o_mla, o_gdn, proj, x2, mla_out_norm_w, gdn_w, w_out)
    up, hn, dy, sq = _mlp_fwd(h2, w_mn, w_up, w_down, t2)
    loss = (0.5 / D) * jnp.sum(sq[:, 0, 0])

    dh, dhb, dup, act, dyb, d_mlp_norm = _mlp_bwd(dy, up, h2, w_mn, w_up, w_down)
    g_w_down = _wgrad(act, dyb, "wgrad_down")
    g_w_up = _wgrad(hn, dup, "wgrad_up", column_shards=True)
    do_mla, do_gdn, dz, d_mla_w, d_gdn_w = _mix_bwd(dhb, o_mla, o_gdn, proj, mla_out_norm_w, gdn_w, w_out)
    g_w_out = _wgrad(mix, dhb, "wgrad_out")
    first = ("w_down",)
    second = ("w_out",)
    third = ("w_up", "w_uq", "w_ukv")
    mats = dict(w_up=g_w_up, w_down=g_w_down, w_out=g_w_out)

    def sending(names):
        return _Transfer("exchange", [_slabs(n, mats[n]) for n in names]) if exchange else None

    (dq4, dk4, dv4), got = _attn_bwd(q4, k4, v4, do_mla, o_mla, lse, B, S, sending(first))
    mats.update(zip(first, got))
    (dql, dkvl, dkpe, dqraw, dkvraw, qn, kvn, d_wqln, d_wkvln, d_qnw, d_knw), got = _mla_pre_bwd(
        proj, cosf, sinf, w_qln, w_kvln, w_uq_p, w_ukv, qnw, knw, dq4, dk4, dv4, sending(second))
    mats.update(zip(second, got))
    mats.update(w_uq=_wgrad(qn, dqraw, "wgrad_uq"), w_ukv=_wgrad(kvn, dkvraw, "wgrad_ukv"))
    (dqg, dkg, dvg, dgb4), got = _gdn_bwd(qg, kg, vg, gates, states, ainv, u4, w4, do_gdn, B, S, sending(third))
    mats.update(zip(third, got))
    dc, dgab, g_conv, d_alog, d_dt = _gdn_pre_bwd(proj, conv_w, alog_l, dt_l, dqg, dkg, dvg, dgb4, S)
    dgqkv = _conv_bwd_input(dc, conv_w, S)
    grad_x2, dproj, d_attn_norm = _in_proj_bwd(dgqkv, dz, dql, dkvl, dkpe, dgab, w_in_p, dh, x2, w_an)
    mats.update(w_in=_wgrad(xn, dproj, "wgrad_in"), conv_w=g_conv)
    if exchange:
        last = ("w_in", "conv_w")
        mats.update(zip(last, _exchange_grads([_slabs(n, mats[n]) for n in last])))
    small = dict(attn_norm_w=d_attn_norm, mlp_norm_w=d_mlp_norm, q_lat_norm_w=d_wqln, kv_lat_norm_w=d_wkvln,
                 q_norm_w=d_qnw, k_norm_w=d_knw, mla_out_norm_w=d_mla_w, a_log=d_alog, dt_bias=d_dt,
                 gdn_norm_w=d_gdn_w)
    return loss, grad_x2.reshape(B, S, D), mats, [small[n] for n, *_ in SMALL_LAYOUT]


BIG = ("w_in", "w_uq", "w_ukv", "conv_w", "w_out", "w_up", "w_down")
ALL_W = ("attn_norm_w", "w_in", "q_lat_norm_w", "w_uq", "kv_lat_norm_w", "w_ukv", "q_norm_w", "k_norm_w",
         "mla_out_norm_w", "conv_w", "a_log", "dt_bias", "gdn_norm_w", "w_out", "mlp_norm_w", "w_up", "w_down")
WIRE_SHAPE = {"w_in": (1024, 384), "w_uq": (256, 128), "conv_w": (16, 256)}


def _pad2(a, rows, cols):
    return jnp.pad(a, [(0, 0)] * (a.ndim - 2) + [(0, rows - a.shape[-2]), (0, cols - a.shape[-1])])


def _cols_to_full(stack, cols):
    return jnp.moveaxis(stack[:, :, :cols], 0, 1).reshape(stack.shape[1], N_DEV * cols)


def _full_to_cols(full, wire_cols):
    r, n = full.shape
    return _pad2(jnp.moveaxis(full.reshape(r, N_DEV, n // N_DEV), 1, 0), r, wire_cols)


def _slabs(name, g):
    if name == "w_in":
        return _full_to_cols(_w_in_from_padded(g), WIRE_SHAPE["w_in"][1])
    if name == "w_uq":
        return _full_to_cols(_w_uq_from_headsplit(g), WIRE_SHAPE["w_uq"][1])
    if name == "w_ukv":
        return _full_to_cols(g, g.shape[1] // N_DEV)
    if name == "conv_w":
        return _pad2(_full_to_cols(g.astype(WIRE_DTYPE), g.shape[1] // N_DEV), *WIRE_SHAPE["conv_w"])
    if name == "w_up":
        return g
    return g.reshape(N_DEV, -1, g.shape[-1])


def kernel(x, positions, attn_norm_w, w_in, q_lat_norm_w, w_uq, kv_lat_norm_w, w_ukv, q_norm_w, k_norm_w, mla_out_norm_w, conv_w, a_log, dt_bias, gdn_norm_w, w_out, mlp_norm_w, w_up, w_down, loss_target, m_attn_norm_w, m_w_in, m_q_lat_norm_w, m_w_uq, m_kv_lat_norm_w, m_w_ukv, m_q_norm_w, m_k_norm_w, m_mla_out_norm_w, m_conv_w, m_a_log, m_dt_bias, m_gdn_norm_w, m_w_out, m_mlp_norm_w, m_w_up, m_w_down, v_attn_norm_w, v_w_in, v_q_lat_norm_w, v_w_uq, v_kv_lat_norm_w, v_w_ukv, v_q_norm_w, v_k_norm_w, v_mla_out_norm_w, v_conv_w, v_a_log, v_dt_bias, v_gdn_norm_w, v_w_out, v_mlp_norm_w, v_w_up, v_w_down):
    env = dict(locals())
    W = {n: env[n][0] for n in ALL_W}
    Mo = {n: env["m_" + n][0] for n in ALL_W}
    Vo = {n: env["v_" + n][0] for n in ALL_W}

    two_d = lambda a: a.reshape(1, -1) if a.ndim == 1 else a
    D = x.shape[-1]

    s_in, s_uq, s_ukv, s_conv = _gather_weights([
        _pad2(W["w_in"].astype(WIRE_DTYPE), *WIRE_SHAPE["w_in"]),
        _pad2(W["w_uq"].astype(WIRE_DTYPE), *WIRE_SHAPE["w_uq"]),
        W["w_ukv"].astype(WIRE_DTYPE), _pad2(W["conv_w"], *WIRE_SHAPE["conv_w"])])
    late = [W["w_out"].astype(WIRE_DTYPE), W["w_up"].astype(WIRE_DTYPE), W["w_down"].astype(WIRE_DTYPE)]

    loss, grad_x, parts, gs = _local_step(
        x, positions, loss_target, two_d(W["attn_norm_w"]), _cols_to_full(s_in, W["w_in"].shape[1]),
        two_d(W["q_lat_norm_w"]), _cols_to_full(s_uq, W["w_uq"].shape[1]), two_d(W["kv_lat_norm_w"]),
        _cols_to_full(s_ukv, W["w_ukv"].shape[1]), two_d(W["q_norm_w"]), two_d(W["k_norm_w"]),
        W["mla_out_norm_w"], _cols_to_full(s_conv[:, :CONV_W], W["conv_w"].shape[1]), two_d(W["a_log"]),
        two_d(W["dt_bias"]), two_d(W["gdn_norm_w"]), None, two_d(W["mlp_norm_w"]), None, None,
        late_shards=late, exchange=True)
    done = {n: _reduce_adamw(parts[n], W[n], Mo[n], Vo[n], "adamw_" + n) for n in BIG}
    names = [n for n, *_ in SMALL_LAYOUT]
    tiles = _gather_small_grads(gs, jnp.full((1, LANES), loss, F32))
    small, loss = _adamw_replicated(tiles, [two_d(W[n]) for n in names], [two_d(Mo[n]) for n in names],
                                    [two_d(Vo[n]) for n in names])
    for i, n in enumerate(names):
        done[n] = [small[kind][i] for kind in range(4)]
    res = [done[n][kind].reshape(env[n].shape) for kind in range(4) for n in ALL_W]
    return (loss, grad_x, *res)
```

```python
import functools

import jax
import jax.numpy as jnp
from jax import lax
from jax.experimental import pallas as pl
from jax.experimental.pallas import tpu as pltpu

F32 = jnp.float32
MXU_DTYPE = jnp.bfloat16
WIRE_DTYPE = jnp.bfloat16
SDS = jax.ShapeDtypeStruct
HIGHEST = lax.Precision.HIGHEST
MESH_ID = pl.DeviceIdType.MESH

D_MODEL = 1024
MLA_HEADS = 4
Q_LORA = 256
KV_LORA = 256
NOPE = 128
ROPE = 64
QK_DIM = NOPE + ROPE
V_DIM = 128
ROPE_THETA = 10000.0
GDN_HEADS = 4
GDN_DIM = 128
GDN_WIDTH = GDN_HEADS * GDN_DIM
CONV_W = 4
CHUNK = 64
D_FF = 4 * D_MODEL
EPS = 1e-6
ATT_SCALE = QK_DIM ** -0.5
GDN_QSCALE = GDN_DIM ** -0.5
N_DEV = 8
ATTN_BLOCK = 512
ATTN_CHAINS = 2
MLP_FWD_SHARDS = 4
MLP_BWD_SHARDS = 4

ADAM_LR = 0.001
ADAM_B1 = 0.9
ADAM_B2 = 0.999
ADAM_EPS = 1e-08
ADAM_WD = 0.01
ADAM_STEP = 10

LANES = 128
SUBLANES = 8
VMEM_LIMIT = 60 * 1024 * 1024

P_GQKV, P_GZ, P_QLAT, P_KVLAT, P_KPE, P_GAB = 0, 1536, 2048, 2304, 2560, 2688
P_WIDTH = 2816
O_QLAT, O_KVLAT, O_KPE, O_GQKV, O_GZ, O_GAB, O_END = 0, 256, 512, 576, 2112, 2624, 2632


def _params(sem=None, vmem=VMEM_LIMIT):
    kw = dict(vmem_limit_bytes=vmem)
    if sem is not None:
        kw["dimension_semantics"] = sem
    return pltpu.CompilerParams(**kw)


def _mm(a, b):
    return jnp.dot(a.astype(MXU_DTYPE), b.astype(MXU_DTYPE), preferred_element_type=F32)


def _mm_nt(a, b):
    return lax.dot_general(a.astype(MXU_DTYPE), b.astype(MXU_DTYPE), (((1,), (1,)), ((), ())),
                           preferred_element_type=F32)


def _mm_tn(a, b):
    return lax.dot_general(a.astype(MXU_DTYPE), b.astype(MXU_DTYPE), (((0,), (0,)), ((), ())),
                           preferred_element_type=F32)


def _split(a):
    hi = a.astype(MXU_DTYPE)
    return hi, (a - hi.astype(F32)).astype(MXU_DTYPE)


def _mm_split(a, b):
    (ah, al), (bh, bl) = a, b
    dot = lambda x, y: jnp.dot(x, y, preferred_element_type=F32)
    if MXU_DTYPE == F32:
        return dot(ah, bh)
    return dot(ah, bh) + dot(ah, bl) + dot(al, bh)


def _mm_exact(a, b):
    return _mm_split(_split(a), _split(b))


def _row_sum(v, on_mxu=False):
    if not on_mxu:
        return jnp.sum(v, axis=-1, keepdims=True)
    d = v.shape[-1]
    ones = jnp.ones((d, LANES), MXU_DTYPE)
    s = sum(jnp.dot(p, ones, preferred_element_type=F32) for p in _split(v))
    return s[:, :d] if d <= LANES else jnp.tile(s, (1, d // LANES))


def _rms(x, w, on_mxu=False):
    r = lax.rsqrt(_row_sum(x * x, on_mxu) * (1.0 / x.shape[-1]) + EPS)
    return x * r * w, r


def _rms_bwd(dy, x, w, r, on_mxu=False):
    xh = x * r
    dyw = dy * w
    dx = r * (dyw - xh * (_row_sum(dyw * xh, on_mxu) * (1.0 / x.shape[-1])))
    dw = jnp.sum(dy * xh, axis=0, keepdims=True)
    return dx, dw


def _l2n(x, scale):
    return x * (lax.rsqrt(_row_sum(x * x) + EPS) * scale)


def _l2n_bwd(dy, x, scale):
    r = lax.rsqrt(_row_sum(x * x) + EPS)
    xh = x * r
    return (scale * r) * (dy - xh * _row_sum(dy * xh))


def _rot(t):
    return jnp.concatenate([-t[:, ROPE // 2:], t[:, :ROPE // 2]], axis=-1)


def _rot_t(t):
    return jnp.concatenate([t[:, ROPE // 2:], -t[:, :ROPE // 2]], axis=-1)


def _rope(t, cos, sin):
    return t * cos + _rot(t) * sin


def _rope_bwd(d, cos, sin):
    return d * cos + _rot_t(d * sin)


def _sigmoid(x):
    return jax.nn.sigmoid(x)


def _shift_down(x, halo, j):
    if j == 0:
        return x
    xr = pltpu.roll(x, j, 0)
    hr = pltpu.roll(halo, j, 0)
    row = lax.broadcasted_iota(jnp.int32, halo.shape, 0)
    top = jnp.where(row < j, hr, xr[:SUBLANES])
    return jnp.concatenate([top, xr[SUBLANES:]], axis=0)


def _shift_up(x, nxt, j):
    if j == 0:
        return x
    n = x.shape[0]
    xr = pltpu.roll(x, n - j, 0)
    nr = pltpu.roll(nxt, SUBLANES - j, 0)
    row = lax.broadcasted_iota(jnp.int32, nxt.shape, 0)
    bot = jnp.where(row >= SUBLANES - j, nr, xr[n - SUBLANES:])
    return jnp.concatenate([xr[:n - SUBLANES], bot], axis=0)


def _chunk_cumsum(y, row_in_chunk):
    s = 1
    while s < CHUNK:
        y = y + jnp.where(row_in_chunk >= s, pltpu.roll(y, s, 0), 0.0)
        s *= 2
    return y


def _chunk_rev_cumsum(y, row_in_chunk):
    n = y.shape[0]
    s = 1
    while s < CHUNK:
        y = y + jnp.where(row_in_chunk + s < CHUNK, pltpu.roll(y, n - s, 0), 0.0)
        s *= 2
    return y


def _together(generators):
    alive = list(generators)
    while alive:
        nxt = []
        for g in alive:
            try:
                next(g)
                nxt.append(g)
            except StopIteration:
                pass
        alive = nxt
        yield


def _lockstep(generators):
    for _ in _together(generators):
        pass


def _pick_lane(tile, lane, idx):
    return jnp.sum(jnp.where(lane == idx, tile, 0.0), axis=-1, keepdims=True)


def _divisor_tile(n, cap, unit=LANES):
    best = unit
    t = unit
    while t <= min(n, cap):
        if n % t == 0:
            best = t
        t += unit
    return n if n <= cap else best


def _in_proj(x2, w_an, w_in_p):
    T, D = x2.shape
    N = w_in_p.shape[1]
    tm = min(512, T)

    def body(x_ref, wn_ref, w_ref, proj_ref, xn_ref):
        xn, _ = _rms(x_ref[...], wn_ref[...])
        xn = xn.astype(MXU_DTYPE)
        xn_ref[...] = xn
        proj_ref[...] = jnp.dot(xn, w_ref[...], preferred_element_type=F32)

    return pl.pallas_call(
        body, grid=(T // tm,), name="in_proj",
        in_specs=[pl.BlockSpec((tm, D), lambda i: (i, 0)), pl.BlockSpec((1, D), lambda i: (0, 0)),
                  pl.BlockSpec((D, N), lambda i: (0, 0))],
        out_specs=[pl.BlockSpec((tm, N), lambda i: (i, 0)), pl.BlockSpec((tm, D), lambda i: (i, 0))],
        out_shape=[SDS((T, N), F32), SDS((T, D), MXU_DTYPE)],
        compiler_params=_params(("arbitrary",)),
    )(x2, w_an, w_in_p)


def _mla_pre(proj, cosf, sinf, w_qln, w_kvln, w_uq_p, w_ukv, qnw, knw, transfer=None):
    T = proj.shape[0]
    tm = min(256, T)
    H = MLA_HEADS

    def body(ql_ref, kvl_ref, kpe_ref, cos_ref, sin_ref, wq_ref, wkv_ref, uq_ref, ukv_ref, qnw_ref, knw_ref,
             q_out, k_out, v_out):
        rms = functools.partial(_rms, on_mxu=True)
        cos, sin = cos_ref[...], sin_ref[...]
        qnw_, knw_ = qnw_ref[...], knw_ref[...]
        qn, _ = rms(ql_ref[...], wq_ref[...])
        kvn, _ = rms(kvl_ref[...], wkv_ref[...])
        qraw = _mm(qn, uq_ref[...])
        kvraw = _mm(kvn, ukv_ref[...])
        kpe = _rope(rms(kpe_ref[...][:, :ROPE], knw_[:, NOPE:])[0], cos, sin)
        for h in range(H):
            qn_h = rms(qraw[:, h * NOPE:(h + 1) * NOPE], qnw_[:, :NOPE])[0]
            qp_h = _rope(rms(qraw[:, H * NOPE + h * ROPE:H * NOPE + (h + 1) * ROPE], qnw_[:, NOPE:])[0], cos, sin)
            q_out[h] = (jnp.concatenate([qn_h, qp_h], axis=-1) * ATT_SCALE).astype(MXU_DTYPE)
            kn_h = rms(kvraw[:, h * 256:h * 256 + NOPE], knw_[:, :NOPE])[0]
            k_out[h] = jnp.concatenate([kn_h, kpe], axis=-1).astype(MXU_DTYPE)
            v_out[h] = kvraw[:, h * 256 + NOPE:(h + 1) * 256].astype(MXU_DTYPE)

    full = lambda a: pl.BlockSpec(a.shape, lambda i: (0,) * a.ndim)
    return _call_beside(
        body, transfer, grid=(T // tm,), name="mla_pre", scratch_shapes=[], semantics=("arbitrary",),
        args=(proj, proj, proj, cosf, sinf, w_qln, w_kvln, w_uq_p, w_ukv, qnw, knw),
        in_specs=[pl.BlockSpec((tm, 256), lambda i: (i, P_QLAT // 256)),
                  pl.BlockSpec((tm, 256), lambda i: (i, P_KVLAT // 256)),
                  pl.BlockSpec((tm, 128), lambda i: (i, P_KPE // 128)),
                  pl.BlockSpec((tm, ROPE), lambda i: (i, 0)), pl.BlockSpec((tm, ROPE), lambda i: (i, 0)),
                  full(w_qln), full(w_kvln), full(w_uq_p), full(w_ukv), full(qnw), full(knw)],
        out_specs=[pl.BlockSpec((H, tm, QK_DIM), lambda i: (0, i, 0)),
                   pl.BlockSpec((H, tm, QK_DIM), lambda i: (0, i, 0)),
                   pl.BlockSpec((H, tm, V_DIM), lambda i: (0, i, 0))],
        out_shape=[SDS((H, T, QK_DIM), MXU_DTYPE), SDS((H, T, QK_DIM), MXU_DTYPE), SDS((H, T, V_DIM), MXU_DTYPE)])


def _attn_fwd(q4, k4, v4, B, S, transfer=None):
    H = MLA_HEADS
    bq = min(ATTN_BLOCK, S)
    nq = S // bq
    rows = bq // ATTN_CHAINS

    def body(q_ref, k_ref, v_ref, o_ref, lse_ref):
        col = lax.broadcasted_iota(jnp.int32, (rows, bq), 1)
        row = lax.broadcasted_iota(jnp.int32, (rows, bq), 0)

        def q_step(qi, carry):
            qs = pl.multiple_of(qi * bq, bq)
            qsub = [q_ref[0, pl.ds(qs + j * rows, rows), :] for j in range(ATTN_CHAINS)]

            def k_block(ks, cs, diagonal):
                k = k_ref[0, pl.ds(ks, bq), :]
                v = v_ref[0, pl.ds(ks, bq), :]
                out = [None] * ATTN_CHAINS

                def chain(j):
                    m, l, acc = cs[j]
                    s = _mm_nt(qsub[j], k)
                    yield
                    if diagonal:
                        s = jnp.where(col <= row + j * rows, s, -jnp.inf)
                    m_new = jnp.maximum(m, jnp.max(s, axis=-1, keepdims=True))
                    p = jnp.exp(s - m_new)
                    a = jnp.exp(m - m_new)
                    l_new = a * l + jnp.sum(p, axis=-1, keepdims=True)
                    yield
                    out[j] = (m_new, l_new, a * acc + _mm(p, v))

                _lockstep([chain(j) for j in range(ATTN_CHAINS)])
                return tuple(out)

            init = tuple((jnp.full((rows, 1), -jnp.inf, F32), jnp.zeros((rows, 1), F32),
                          jnp.zeros((rows, V_DIM), F32)) for _ in range(ATTN_CHAINS))
            cs = lax.fori_loop(0, qi, lambda kj, c: k_block(pl.multiple_of(kj * bq, bq), c, False), init)
            for j, (m, l, acc) in enumerate(k_block(qs, cs, True)):
                o_ref[0, pl.ds(qs + j * rows, rows), :] = acc / l
                lse_ref[0, pl.ds(qs + j * rows, rows), :] = m + jnp.log(l)
            return carry

        lax.fori_loop(0, nq, q_step, 0)

    spec = lambda d: pl.BlockSpec((1, S, d), lambda h, b: (h, b, 0))
    return _call_beside(
        body, transfer, grid=(H, B), name="attn_fwd",
        in_specs=[spec(QK_DIM), spec(QK_DIM), spec(V_DIM)],
        out_specs=[spec(V_DIM), spec(1)],
        out_shape=[SDS((H, B * S, V_DIM), F32), SDS((H, B * S, 1), F32)],
        scratch_shapes=[], semantics=("arbitrary", "arbitrary"), args=(q4, k4, v4))


def _conv_taps(u, halo, w):
    sh = [_shift_down(u, halo, j) for j in range(CONV_W)]
    c = w[0:1] * sh[3] + w[1:2] * sh[2] + w[2:3] * sh[1] + w[3:4] * sh[0]
    return c, sh


def _gate_values(gab, alog_l, dt_l, lane):
    g = -jnp.exp(alog_l) * jax.nn.softplus(gab + dt_l)
    g = jnp.where(lane < GDN_HEADS, g, 0.0)
    beta = jnp.where((lane >= GDN_HEADS) & (lane < 2 * GDN_HEADS), _sigmoid(gab), 0.0)
    return g, beta


def _gdn_pre(proj, conv_w, alog_l, dt_l, S):
    T = proj.shape[0]
    tm = min(256, T)
    tiles_per_seq = S // tm
    C3 = 3 * GDN_WIDTH
    H = GDN_HEADS

    def body(u_ref, halo_ref, gab_ref, w_ref, alog_ref, dt_ref, q_out, k_out, v_out, gates_out):
        i = pl.program_id(0)
        halo = jnp.where(i % tiles_per_seq == 0, 0.0, halo_ref[...])
        c, _ = _conv_taps(u_ref[...], halo, w_ref[...])
        a = c * _sigmoid(c)
        for h in range(H):
            xq = a[:, h * GDN_DIM:(h + 1) * GDN_DIM]
            xk = a[:, GDN_WIDTH + h * GDN_DIM:GDN_WIDTH + (h + 1) * GDN_DIM]
            q_out[h] = _l2n(xq, GDN_QSCALE)
            k_out[h] = _l2n(xk, 1.0)
            v_out[h] = a[:, 2 * GDN_WIDTH + h * GDN_DIM:2 * GDN_WIDTH + (h + 1) * GDN_DIM]
        lane = lax.broadcasted_iota(jnp.int32, (tm, LANES), 1)
        ric = lax.broadcasted_iota(jnp.int32, (tm, LANES), 0) % CHUNK
        g, beta = _gate_values(gab_ref[...], alog_ref[...], dt_ref[...], lane)
        gates_out[...] = _chunk_cumsum(g, ric) + beta

    hspec = pl.BlockSpec((H, tm, GDN_DIM), lambda i: (0, i, 0))
    return pl.pallas_call(
        body, grid=(T // tm,), name="gdn_pre",
        in_specs=[pl.BlockSpec((tm, C3), lambda i: (i, 0)),
                  pl.BlockSpec((SUBLANES, C3), lambda i: (jnp.maximum(i * (tm // SUBLANES) - 1, 0), 0)),
                  pl.BlockSpec((tm, LANES), lambda i: (i, P_GAB // LANES)),
                  pl.BlockSpec((CONV_W, C3), lambda i: (0, 0)),
                  pl.BlockSpec((1, LANES), lambda i: (0, 0)), pl.BlockSpec((1, LANES), lambda i: (0, 0))],
        out_specs=[hspec, hspec, hspec, pl.BlockSpec((tm, LANES), lambda i: (i, 0))],
        out_shape=[SDS((H, T, GDN_DIM), F32)] * 3 + [SDS((T, LANES), F32)],
        compiler_params=_params(("arbitrary",)),
    )(proj, proj, proj, conv_w, alog_l, dt_l)


def _unit_lower_inverses(Ls, eye):
    Ps = [eye - L for L in Ls]
    Ms = [_split(-L) for L in Ls]
    for _ in range(5):
        sq = [_mm_split(m, m) for m in Ms]
        Ms = [_split(s) for s in sq]
        Ps = [p + _mm_split(_split(p), m) for p, m in zip(Ps, Ms)]
    return Ps


def _chunk_decays(gt, lane, h, ri, ci, rcol):
    Gc = _pick_lane(gt, lane, h)
    bt = _pick_lane(gt, lane, h + GDN_HEADS)
    Gb = jnp.broadcast_to(Gc, (CHUNK, CHUNK))
    Gam = jnp.where(ri >= ci, jnp.exp(Gb - Gb.T), 0.0)
    Gl = jnp.sum(jnp.where(rcol == CHUNK - 1, Gc, 0.0), axis=0, keepdims=True)
    return Gc, bt, Gam, jnp.exp(Gc), jnp.exp(Gl - Gc), jnp.exp(Gl)


GDN_FWD_UNROLL = 16
GDN_BWD_UNROLL = 8
GDN_RECUR_STEPS_PER_STAGE = 2


def _gdn_fwd(qg, kg, vg, gates, B, S, transfer=None):
    H, D, C = GDN_HEADS, GDN_DIM, CHUNK
    NC = S // C
    U = GDN_FWD_UNROLL if NC % GDN_FWD_UNROLL == 0 else 1
    NG = NC // U

    def body(q_ref, k_ref, v_ref, g_ref, o_ref, st_ref, ai_ref, u_ref, w_ref, q2_s, au_s, bc_s, w2_s, el_s):
        h = pl.program_id(0)
        lane = lax.broadcasted_iota(jnp.int32, (C, LANES), 1)
        ri = lax.broadcasted_iota(jnp.int32, (C, C), 0)
        ci = lax.broadcasted_iota(jnp.int32, (C, C), 1)
        rcol = lax.broadcasted_iota(jnp.int32, (C, 1), 0)
        eye = (ri == ci).astype(F32)

        def group(gi, c):
            ns = [gi * U + j for j in range(U)]
            css = [pl.multiple_of(n * C, C) for n in ns]
            qs = [q_ref[0, pl.ds(cs, C), :] for cs in css]
            ks = [k_ref[0, pl.ds(cs, C), :] for cs in css]
            vs = [v_ref[0, pl.ds(cs, C), :] for cs in css]
            decs = [_chunk_decays(g_ref[pl.ds(cs, C), :], lane, h, ri, ci, rcol) for cs in css]
            qks = [_mm_nt(jnp.concatenate([q, k], axis=0), k) for q, k in zip(qs, ks)]
            ainvs = _unit_lower_inverses(
                [jnp.where(ri > ci, d[1] * qk[C:] * d[2], 0.0) for qk, d in zip(qks, decs)], eye)
            sols = [_mm_exact(a, jnp.concatenate([v * d[1], k * (d[1] * d[3])], axis=-1))
                    for a, k, v, d in zip(ainvs, ks, vs, decs)]
            atuw = [_mm(qk[:C] * d[2], sol) for qk, d, sol in zip(qks, decs, sols)]
            kduw = [_mm_tn(k * d[4], sol) for k, d, sol in zip(ks, decs, sols)]
            for n, cs, q, a, sol, au, ku, (Gc, bt, Gam, e, f, eL) in zip(ns, css, qs, ainvs, sols, atuw, kduw, decs):
                u_ref[0, pl.ds(cs, C), :] = sol[:, :D]
                w_ref[0, pl.ds(cs, C), :] = sol[:, D:]
                au_s[pl.ds(cs, C), :] = au[:, :D]
                q2_s[pl.ds(cs, C), :] = q * e - au[:, D:]
                bc_s[n] = ku[:, :D]
                w2_s[n] = ku[:, D:]
                el_s[n] = jnp.broadcast_to(eL, (SUBLANES, LANES))
                ai_ref[0, n] = a.T
            return c

        lax.fori_loop(0, NG, group, 0)

        def step(n, S_):
            cs = pl.multiple_of(n * C, C)
            o_ref[0, pl.ds(cs, C), :] = _mm(q2_s[pl.ds(cs, C), :], S_) + au_s[pl.ds(cs, C), :]
            st_ref[0, n] = S_
            return S_ * el_s[n, 0:1, :] + bc_s[n] - _mm(w2_s[n], S_)

        lax.fori_loop(0, NC, step, jnp.zeros((D, D), F32))

    spec = pl.BlockSpec((1, S, D), lambda h, b: (h, b, 0))
    return _call_beside(
        body, transfer, grid=(H, B), name="gdn_fwd",
        in_specs=[spec, spec, spec, pl.BlockSpec((S, LANES), lambda h, b: (b, 0))],
        out_specs=[spec, pl.BlockSpec((1, NC, D, D), lambda h, b: (h, b, 0, 0)),
                   pl.BlockSpec((1, NC, C, C), lambda h, b: (h, b, 0, 0)), spec, spec],
        out_shape=[SDS((H, B * S, D), F32), SDS((H, B * NC, D, D), F32), SDS((H, B * NC, C, C), F32),
                   SDS((H, B * S, D), F32), SDS((H, B * S, D), F32)],
        scratch_shapes=[pltpu.VMEM((S, D), F32), pltpu.VMEM((S, D), F32), pltpu.VMEM((NC, D, D), F32),
                        pltpu.VMEM((NC, D, D), F32), pltpu.VMEM((NC, SUBLANES, LANES), F32)],
        semantics=("arbitrary", "arbitrary"), args=(qg, kg, vg, gates))


def _mix_out(o_mla, o_gdn, proj, x2, mla_w, gdn_w, w_out):
    T, D = x2.shape
    tm = min(512, T)
    H = MLA_HEADS

    def body(om_ref, og_ref, z_ref, x_ref, mw_ref, gw_ref, w_ref, h_ref, mix_ref):
        z = z_ref[...]
        parts = [_rms(om_ref[h], mw_ref[h:h + 1, :])[0] for h in range(H)]
        for h in range(GDN_HEADS):
            zh = z[:, h * GDN_DIM:(h + 1) * GDN_DIM]
            parts.append(_rms(og_ref[h], gw_ref[...])[0] * (zh * _sigmoid(zh)))
        mix = jnp.concatenate(parts, axis=-1).astype(MXU_DTYPE)
        mix_ref[...] = mix
        h_ref[...] = x_ref[...] + jnp.dot(mix, w_ref[...], preferred_element_type=F32)

    hspec = pl.BlockSpec((H, tm, V_DIM), lambda i: (0, i, 0))
    return pl.pallas_call(
        body, grid=(T // tm,), name="mix_out",
        in_specs=[hspec, hspec, pl.BlockSpec((tm, GDN_WIDTH), lambda i: (i, P_GZ // GDN_WIDTH)),
                  pl.BlockSpec((tm, D), lambda i: (i, 0)),
                  pl.BlockSpec((H, V_DIM), lambda i: (0, 0)), pl.BlockSpec((1, GDN_DIM), lambda i: (0, 0)),
                  pl.BlockSpec((D, D), lambda i: (0, 0))],
        out_specs=[pl.BlockSpec((tm, D), lambda i: (i, 0)), pl.BlockSpec((tm, D), lambda i: (i, 0))],
        out_shape=[SDS((T, D), F32), SDS((T, D), MXU_DTYPE)],
        compiler_params=_params(("arbitrary",)),
    )(o_mla, o_gdn, proj, x2, mla_w, gdn_w, w_out)


def _mlp_fwd(h2, w_mn, w_up, w_down, target):
    T, D = h2.shape
    ns, _, ts = w_up.shape
    F = ns * ts
    tm = min(512, T)
    G = MLP_FWD_SHARDS
    tf, nf = G * ts, ns // G

    def body(h_ref, wn_ref, up_w, down_w, t_ref, up_ref, hn_ref, dy_ref, loss_ref, y_acc):
        j = pl.program_id(1)

        @pl.when(j == 0)
        def _():
            hn_ref[...] = _rms(h_ref[...], wn_ref[...])[0].astype(MXU_DTYPE)
            y_acc[...] = h_ref[...]

        parts = []
        for c in range(G):
            up = jnp.dot(hn_ref[...], up_w[c], preferred_element_type=F32)
            up_ref[:, c * ts:(c + 1) * ts] = up.astype(MXU_DTYPE)
            r = jnp.maximum(up, 0.0)
            parts.append(_mm(r * r, down_w[c * ts:(c + 1) * ts, :]))
        y_acc[...] += functools.reduce(jnp.add, parts)

        @pl.when(j == nf - 1)
        def _():
            err = y_acc[...] - t_ref[...]
            dy_ref[...] = err / D
            loss_ref[...] = jnp.full((1, SUBLANES, LANES), jnp.sum(err * err), F32)

    return pl.pallas_call(
        body, grid=(T // tm, nf), name="mlp_fwd",
        in_specs=[pl.BlockSpec((tm, D), lambda i, j: (i, 0)), pl.BlockSpec((1, D), lambda i, j: (0, 0)),
                  pl.BlockSpec((G, D, ts), lambda i, j: (j, 0, 0)), pl.BlockSpec((tf, D), lambda i, j: (j, 0)),
                  pl.BlockSpec((tm, D), lambda i, j: (i, 0))],
        out_specs=[pl.BlockSpec((tm, tf), lambda i, j: (i, j)), pl.BlockSpec((tm, D), lambda i, j: (i, 0)),
                   pl.BlockSpec((tm, D), lambda i, j: (i, 0)),
                   pl.BlockSpec((1, SUBLANES, LANES), lambda i, j: (i, 0, 0))],
        out_shape=[SDS((T, F), MXU_DTYPE), SDS((T, D), MXU_DTYPE), SDS((T, D), F32),
                   SDS((T // tm, SUBLANES, LANES), F32)],
        scratch_shapes=[pltpu.VMEM((tm, D), F32)],
        compiler_params=_params(("arbitrary", "arbitrary")),
    )(h2, w_mn, w_up, w_down, target)


def _mlp_bwd(dy, up, h2, w_mn, w_up, w_down):
    T, D = h2.shape
    ns, _, ts = w_up.shape
    F = ns * ts
    tm = min(512, T)
    G = MLP_BWD_SHARDS
    tf, nf = G * ts, ns // G

    def body(dy_ref, up_ref, h_ref, wn_ref, up_w, down_w, dh_ref, dhb_ref, dup_ref, act_ref, dyb_ref, dwn_ref, acc):
        i, j = pl.program_id(0), pl.program_id(1)

        @pl.when((i == 0) & (j == 0))
        def _():
            dwn_ref[...] = jnp.zeros_like(dwn_ref)

        @pl.when(j == 0)
        def _():
            acc[...] = jnp.zeros_like(acc)
            dyb_ref[...] = dy_ref[...].astype(MXU_DTYPE)

        parts = []
        for c in range(G):
            cols = slice(c * ts, (c + 1) * ts)
            r = jnp.maximum(up_ref[:, cols].astype(F32), 0.0)
            act_ref[:, cols] = (r * r).astype(MXU_DTYPE)
            dup = (_mm_nt(dyb_ref[...], down_w[cols, :]) * (2.0 * r)).astype(MXU_DTYPE)
            dup_ref[:, cols] = dup
            parts.append(_mm_nt(dup, up_w[c]))
        acc[...] += functools.reduce(jnp.add, parts)

        @pl.when(j == nf - 1)
        def _():
            hv = h_ref[...]
            _, rr = _rms(hv, wn_ref[...])
            dx, dw = _rms_bwd(acc[...], hv, wn_ref[...], rr)
            dh = dy_ref[...] + dx
            dh_ref[...] = dh
            dhb_ref[...] = dh.astype(MXU_DTYPE)
            dwn_ref[...] += dw

    row = lambda i, j: (i, 0)
    return pl.pallas_call(
        body, grid=(T // tm, nf), name="mlp_bwd",
        in_specs=[pl.BlockSpec((tm, D), row), pl.BlockSpec((tm, tf), lambda i, j: (i, j)), pl.BlockSpec((tm, D), row),
                  pl.BlockSpec((1, D), lambda i, j: (0, 0)),
                  pl.BlockSpec((G, D, ts), lambda i, j: (j, 0, 0)), pl.BlockSpec((tf, D), lambda i, j: (j, 0))],
        out_specs=[pl.BlockSpec((tm, D), row), pl.BlockSpec((tm, D), row),
                   pl.BlockSpec((tm, tf), lambda i, j: (i, j)), pl.BlockSpec((tm, tf), lambda i, j: (i, j)),
                   pl.BlockSpec((tm, D), row), pl.BlockSpec((1, D), lambda i, j: (0, 0))],
        out_shape=[SDS((T, D), F32), SDS((T, D), MXU_DTYPE), SDS((T, F), MXU_DTYPE), SDS((T, F), MXU_DTYPE),
                   SDS((T, D), MXU_DTYPE), SDS((1, D), F32)],
        scratch_shapes=[pltpu.VMEM((tm, D), F32)],
        compiler_params=_params(("arbitrary", "arbitrary")),
    )(dy, up, h2, w_mn, w_up, w_down)


def _mix_bwd(dhb, o_mla, o_gdn, proj, mla_w, gdn_w, w_out):
    T, D = dhb.shape
    tm = min(512, T)
    H = MLA_HEADS

    def body(dh_ref, om_ref, og_ref, z_ref, mw_ref, gw_ref, w_ref, dom_ref, dog_ref, dz_ref, dmw_ref, dgw_ref):
        @pl.when(pl.program_id(0) == 0)
        def _():
            dmw_ref[...] = jnp.zeros_like(dmw_ref)
            dgw_ref[...] = jnp.zeros_like(dgw_ref)

        dmix = _mm_nt(dh_ref[...], w_ref[...])
        z = z_ref[...]
        dmw, dzs = [], []
        dgw = jnp.zeros((1, GDN_DIM), F32)
        for h in range(H):
            o = om_ref[h]
            w = mw_ref[h:h + 1, :]
            _, r = _rms(o, w)
            dx, dw = _rms_bwd(dmix[:, h * V_DIM:(h + 1) * V_DIM], o, w, r)
            dom_ref[h] = dx
            dmw.append(dw)
        for h in range(GDN_HEADS):
            o = og_ref[h]
            w = gw_ref[...]
            zh = z[:, h * GDN_DIM:(h + 1) * GDN_DIM]
            sg = _sigmoid(zh)
            yn, r = _rms(o, w)
            dy = dmix[:, H * V_DIM + h * GDN_DIM:H * V_DIM + (h + 1) * GDN_DIM]
            dzs.append(dy * yn * (sg * (1.0 + zh * (1.0 - sg))))
            dx, dw = _rms_bwd(dy * (zh * sg), o, w, r)
            dog_ref[h] = dx
            dgw = dgw + dw
        dz_ref[...] = jnp.concatenate(dzs, axis=-1).astype(MXU_DTYPE)
        dmw_ref[...] += jnp.concatenate(dmw, axis=0)
        dgw_ref[...] += dgw

    hspec = pl.BlockSpec((H, tm, V_DIM), lambda i: (0, i, 0))
    return pl.pallas_call(
        body, grid=(T // tm,), name="mix_bwd",
        in_specs=[pl.BlockSpec((tm, D), lambda i: (i, 0)), hspec, hspec,
                  pl.BlockSpec((tm, GDN_WIDTH), lambda i: (i, P_GZ // GDN_WIDTH)),
                  pl.BlockSpec((H, V_DIM), lambda i: (0, 0)), pl.BlockSpec((1, GDN_DIM), lambda i: (0, 0)),
                  pl.BlockSpec((D, D), lambda i: (0, 0))],
        out_specs=[hspec, hspec, pl.BlockSpec((tm, GDN_WIDTH), lambda i: (i, 0)),
                   pl.BlockSpec((H, V_DIM), lambda i: (0, 0)), pl.BlockSpec((1, GDN_DIM), lambda i: (0, 0))],
        out_shape=[SDS((H, T, V_DIM), F32), SDS((H, T, GDN_DIM), F32), SDS((T, GDN_WIDTH), MXU_DTYPE),
                   SDS((H, V_DIM), F32), SDS((1, GDN_DIM), F32)],
        compiler_params=_params(("arbitrary",)),
    )(dhb, o_mla, o_gdn, proj, mla_w, gdn_w, w_out)


def _attn_bwd(q4, k4, v4, do4, o4, lse4, B, S, transfer=None):
    H = MLA_HEADS
    bq = min(ATTN_BLOCK, S)
    nq = S // bq
    rows = bq // ATTN_CHAINS

    def body(q_ref, k_ref, v_ref, do_ref, o_ref, lse_ref, dq_ref, dk_ref, dv_ref, delta):
        dq_ref[...] = jnp.zeros_like(dq_ref)
        dk_ref[...] = jnp.zeros_like(dk_ref)
        dv_ref[...] = jnp.zeros_like(dv_ref)
        delta[...] = jnp.sum(do_ref[0] * o_ref[0], axis=-1, keepdims=True)

        col = lax.broadcasted_iota(jnp.int32, (rows, bq), 1)
        row = lax.broadcasted_iota(jnp.int32, (rows, bq), 0)

        def k_step(kj, carry):
            ks = pl.multiple_of(kj * bq, bq)
            k = k_ref[0, pl.ds(ks, bq), :]
            v = v_ref[0, pl.ds(ks, bq), :]

            def q_block(qs, diagonal):
                dks, dvs = [None] * ATTN_CHAINS, [None] * ATTN_CHAINS

                def chain(j):
                    sl = pl.ds(qs + j * rows, rows)
                    q = q_ref[0, sl, :]
                    do = do_ref[0, sl, :].astype(MXU_DTYPE)
                    s = _mm_nt(q, k)
                    dp = _mm_nt(do, v)
                    yield
                    p = jnp.exp(s - lse_ref[0, sl, :])
                    if diagonal:
                        p = jnp.where(col <= row + j * rows, p, 0.0)
                    ds = p * (dp - delta[sl, :])
                    yield
                    dvs[j] = _mm_tn(p, do)
                    dks[j] = _mm_tn(ds, q)
                    dq_ref[0, sl, :] += _mm(ds, k)

                _lockstep([chain(j) for j in range(ATTN_CHAINS)])
                dv_ref[0, pl.ds(ks, bq), :] += functools.reduce(jnp.add, dvs)
                dk_ref[0, pl.ds(ks, bq), :] += functools.reduce(jnp.add, dks)

            q_block(ks, True)

            def q_step(qi, c):
                q_block(pl.multiple_of(qi * bq, bq), False)
                return c

            lax.fori_loop(kj + 1, nq, q_step, 0)
            return carry

        lax.fori_loop(0, nq, k_step, 0)

    spec = lambda d: pl.BlockSpec((1, S, d), lambda h, b: (h, b, 0))
    return _call_beside(
        body, transfer, grid=(H, B), name="attn_bwd",
        in_specs=[spec(QK_DIM), spec(QK_DIM), spec(V_DIM), spec(V_DIM), spec(V_DIM), spec(1)],
        out_specs=[spec(QK_DIM), spec(QK_DIM), spec(V_DIM)],
        out_shape=[SDS((H, B * S, QK_DIM), F32), SDS((H, B * S, QK_DIM), F32), SDS((H, B * S, V_DIM), F32)],
        scratch_shapes=[pltpu.VMEM((S, 1), F32)], semantics=("arbitrary", "arbitrary"),
        args=(q4, k4, v4, do4, o4, lse4))


def _gdn_bwd(qg, kg, vg, gates, states, ainv, u4, w4, do4, B, S, transfer=None):
    H, D, C = GDN_HEADS, GDN_DIM, CHUNK
    NC = S // C
    U = GDN_BWD_UNROLL if NC % GDN_BWD_UNROLL == 0 else 1
    NG = NC // U

    def body(q_ref, k_ref, v_ref, g_ref, st_ref, ai_ref, u_ref, w_ref, do_ref, dq_ref, dk_ref, dv_ref, dgb_ref,
             kd_s, x1_s, x2_s, el_s, dvn_s, ds_s, w2t_s):
        h = pl.program_id(0)
        lane = lax.broadcasted_iota(jnp.int32, (C, LANES), 1)
        ri = lax.broadcasted_iota(jnp.int32, (C, C), 0)
        ci = lax.broadcasted_iota(jnp.int32, (C, C), 1)
        rcol = lax.broadcasted_iota(jnp.int32, (C, 1), 0)

        def rsum(a):
            return jnp.sum(a, axis=-1, keepdims=True)

        def prepare(n):
            cs = n * C
            q = q_ref[0, pl.ds(cs, C), :]
            k = k_ref[0, pl.ds(cs, C), :]
            do = do_ref[0, pl.ds(cs, C), :]
            Gc, bt, Gam, e, f, eL = _chunk_decays(g_ref[pl.ds(cs, C), :], lane, h, ri, ci, rcol)
            At = _mm_nt(q, k) * Gam
            yield
            x1 = _mm_tn(At, do)
            x2 = _mm_tn(q * e, do)
            kd = k * f
            w = w_ref[0, pl.ds(cs, C), :]
            yield
            x1_s[pl.ds(cs, C), :] = x1
            x2_s[n] = x2 - _mm_tn(w, x1)
            w2t_s[n] = _mm_tn(w, kd)
            kd_s[pl.ds(cs, C), :] = kd
            el_s[n] = jnp.broadcast_to(eL, (SUBLANES, LANES))

        def recur(n, dS):
            cs = n * C
            ds_s[n] = dS
            dvn_s[pl.ds(cs, C), :] = x1_s[pl.ds(cs, C), :] + _mm(kd_s[pl.ds(cs, C), :], dS)
            return x2_s[n] + el_s[n, 0:1, :] * dS - _mm(w2t_s[n], dS)

        def local(n):
            cs = n * C
            q = q_ref[0, pl.ds(cs, C), :]
            k = k_ref[0, pl.ds(cs, C), :]
            v = v_ref[0, pl.ds(cs, C), :]
            do = do_ref[0, pl.ds(cs, C), :]
            u = u_ref[0, pl.ds(cs, C), :]
            w = w_ref[0, pl.ds(cs, C), :]
            dvn = dvn_s[pl.ds(cs, C), :]
            dS = ds_s[n]
            Gc, bt, Gam, e, f, eL = _chunk_decays(g_ref[pl.ds(cs, C), :], lane, h, ri, ci, rcol)
            S0 = st_ref[0, n]
            AinvT = ai_ref[0, n]
            qk = _mm_nt(jnp.concatenate([q, k], axis=0), k)
            QK, KK = qk[:C], qk[C:]
            be = bt * e
            sol = jnp.concatenate([u, w], axis=-1)
            vn = u - _mm(w, S0)
            yield
            dAt = jnp.where(ri >= ci, _mm_nt(do, vn), 0.0)
            dqd = _mm_nt(do, S0)
            dw = -_mm_nt(dvn, S0)
            dkd = _mm_nt(vn, dS)
            deL = jnp.sum(rsum(dS * S0), axis=0, keepdims=True)
            yield
            dR = _mm_exact(AinvT, jnp.concatenate([dvn, dw], axis=-1))
            dR1, dR2 = dR[:, :D], dR[:, D:]
            yield
            dL = jnp.where(ri > ci, -_mm_nt(dR, sol), 0.0)
            yield
            dv_ref[0, pl.ds(cs, C), :] = dR1 * bt
            r2 = rsum(dR2 * k)
            X = dL * Gam
            dbt = rsum(dR1 * v) + r2 * e + rsum(X * KK)
            de = r2 * bt + rsum(dqd * q)
            dKK = X * bt
            dQK = dAt * Gam
            dq_ref[0, pl.ds(cs, C), :] = _mm(dQK, k) + dqd * e
            dk_ref[0, pl.ds(cs, C), :] = dR2 * be + _mm(dKK + dKK.T, k) + _mm_tn(dQK, q) + dkd * f
            df = rsum(dkd * k)
            Z = (dL * (bt * KK) + dAt * QK) * Gam
            dG = rsum(Z) - rsum(Z.T) + de * e - df * f
            dGl = jnp.sum(df * f, axis=0, keepdims=True) + deL * eL
            dG = dG + jnp.where(rcol == C - 1, dGl, 0.0)
            dgb_ref[0, pl.ds(cs, C), :] = jnp.where(lane == 0, dG, jnp.where(lane == 1, dbt, 0.0))

        state = [jnp.zeros((D, D), F32)]

        def recur_group(g):
            for j, n in enumerate(reversed(range(g * U, (g + 1) * U))):
                state[0] = recur(n, state[0])
                if j % GDN_RECUR_STEPS_PER_STAGE == GDN_RECUR_STEPS_PER_STAGE - 1:
                    yield

        def stage(fn, g):
            return _together([fn(g * U + j) for j in range(U)])

        for step in range(NG + 2):
            jobs = [(stage, prepare, NG - 1 - step), (None, None, NG - step), (stage, local, NG + 1 - step)]
            _lockstep([recur_group(g) if make is None else make(fn, g) for make, fn, g in jobs if 0 <= g < NG])

    spec = pl.BlockSpec((1, S, D), lambda h, b: (h, b, 0))
    return _call_beside(
        body, transfer, grid=(H, B), name="gdn_bwd",
        in_specs=[spec, spec, spec, pl.BlockSpec((S, LANES), lambda h, b: (b, 0)),
                  pl.BlockSpec((1, NC, D, D), lambda h, b: (h, b, 0, 0)),
                  pl.BlockSpec((1, NC, C, C), lambda h, b: (h, b, 0, 0)), spec, spec, spec],
        out_specs=[spec, spec, spec, spec],
        out_shape=[SDS((H, B * S, D), F32)] * 4,
        scratch_shapes=[pltpu.VMEM((S, D), F32), pltpu.VMEM((S, D), F32), pltpu.VMEM((NC, D, D), F32),
                        pltpu.VMEM((NC, SUBLANES, LANES), F32), pltpu.VMEM((S, D), F32),
                        pltpu.VMEM((NC, D, D), F32), pltpu.VMEM((NC, D, D), F32)],
        semantics=("arbitrary", "arbitrary"), args=(qg, kg, vg, gates, states, ainv, u4, w4, do4))


def _gdn_pre_bwd(proj, conv_w, alog_l, dt_l, dq4, dk4, dv4, dgb4, S):
    T = proj.shape[0]
    tm = min(256, T)
    tiles_per_seq = S // tm
    C3 = 3 * GDN_WIDTH
    H = GDN_HEADS

    def body(u_ref, halo_ref, gab_ref, w_ref, alog_ref, dt_ref, dq_ref, dk_ref, dv_ref, dgb_ref,
             dc_ref, dgab_ref, dcw_ref, dalog_ref, ddt_ref):
        i = pl.program_id(0)

        @pl.when(i == 0)
        def _():
            dcw_ref[...] = jnp.zeros_like(dcw_ref)
            dalog_ref[...] = jnp.zeros_like(dalog_ref)
            ddt_ref[...] = jnp.zeros_like(ddt_ref)

        halo = jnp.where(i % tiles_per_seq == 0, 0.0, halo_ref[...])
        c, sh = _conv_taps(u_ref[...], halo, w_ref[...])
        sg = _sigmoid(c)
        a = c * sg
        das = [None] * (3 * H)
        for h in range(H):
            xq = a[:, h * GDN_DIM:(h + 1) * GDN_DIM]
            xk = a[:, GDN_WIDTH + h * GDN_DIM:GDN_WIDTH + (h + 1) * GDN_DIM]
            das[h] = _l2n_bwd(dq_ref[h], xq, GDN_QSCALE)
            das[H + h] = _l2n_bwd(dk_ref[h], xk, 1.0)
            das[2 * H + h] = dv_ref[h]
        dc = jnp.concatenate(das, axis=-1) * (sg * (1.0 + c * (1.0 - sg)))
        dc_ref[...] = dc
        dcw_ref[...] += jnp.concatenate(
            [jnp.sum(dc * sh[CONV_W - 1 - t], axis=0, keepdims=True) for t in range(CONV_W)], axis=0)
        lane = lax.broadcasted_iota(jnp.int32, (tm, LANES), 1)
        ric = lax.broadcasted_iota(jnp.int32, (tm, LANES), 0) % CHUNK
        dG = jnp.zeros((tm, LANES), F32)
        for h in range(H):
            t = dgb_ref[h]
            dG = dG + jnp.where(lane == h, _pick_lane(t, lane, 0), 0.0) \
                    + jnp.where(lane == h + H, _pick_lane(t, lane, 1), 0.0)
        is_g = lane < H
        dg = jnp.where(is_g, _chunk_rev_cumsum(jnp.where(is_g, dG, 0.0), ric), 0.0)
        gab = gab_ref[...]
        g, beta = _gate_values(gab, alog_ref[...], dt_ref[...], lane)
        dga = jnp.where(is_g, dg * (-jnp.exp(alog_ref[...])) * _sigmoid(gab + dt_ref[...]), 0.0)
        dgb = jnp.where(is_g, 0.0, dG) * beta * (1.0 - beta)
        dgab_ref[...] = (dga + dgb).astype(MXU_DTYPE)
        dalog_ref[...] += jnp.sum(dg * g, axis=0, keepdims=True)
        ddt_ref[...] += jnp.sum(dga, axis=0, keepdims=True)

    hspec = pl.BlockSpec((H, tm, GDN_DIM), lambda i: (0, i, 0))
    vec = pl.BlockSpec((1, LANES), lambda i: (0, 0))
    return pl.pallas_call(
        body, grid=(T // tm,), name="gdn_pre_bwd",
        in_specs=[pl.BlockSpec((tm, C3), lambda i: (i, 0)),
                  pl.BlockSpec((SUBLANES, C3), lambda i: (jnp.maximum(i * (tm // SUBLANES) - 1, 0), 0)),
                  pl.BlockSpec((tm, LANES), lambda i: (i, P_GAB // LANES)),
                  pl.BlockSpec((CONV_W, C3), lambda i: (0, 0)), vec, vec, hspec, hspec, hspec, hspec],
        out_specs=[pl.BlockSpec((tm, C3), lambda i: (i, 0)), pl.BlockSpec((tm, LANES), lambda i: (i, 0)),
                   pl.BlockSpec((CONV_W, C3), lambda i: (0, 0)), vec, vec],
        out_shape=[SDS((T, C3), F32), SDS((T, LANES), MXU_DTYPE), SDS((CONV_W, C3), F32),
                   SDS((1, LANES), F32), SDS((1, LANES), F32)],
        compiler_params=_params(("arbitrary",)),
    )(proj, proj, proj, conv_w, alog_l, dt_l, dq4, dk4, dv4, dgb4)


def _mla_pre_bwd(proj, cosf, sinf, w_qln, w_kvln, w_uq_p, w_ukv, qnw, knw, dq4, dk4, dv4, transfer=None):
    T = proj.shape[0]
    tm = min(256, T)
    H = MLA_HEADS

    def body(ql_ref, kvl_ref, kpe_ref, cos_ref, sin_ref, wq_ref, wkv_ref, uq_ref, ukv_ref, qnw_ref, knw_ref,
             dq_ref, dk_ref, dv_ref,
             dql_ref, dkvl_ref, dkpe_ref, dqraw_ref, dkvraw_ref, qn_ref, kvn_ref, dwq_ref, dwkv_ref, dqnw_ref, dknw_ref):
        @pl.when(pl.program_id(0) == 0)
        def _():
            for r in (dwq_ref, dwkv_ref, dqnw_ref, dknw_ref):
                r[...] = jnp.zeros_like(r)

        cos, sin = cos_ref[...], sin_ref[...]
        qnw_, knw_ = qnw_ref[...], knw_ref[...]
        ql, kvl = ql_ref[...], kvl_ref[...]
        kpe_raw = kpe_ref[...][:, :ROPE]
        rms = functools.partial(_rms, on_mxu=True)
        rms_bwd = functools.partial(_rms_bwd, on_mxu=True)
        qn, rq = rms(ql, wq_ref[...])
        kvn, rkv = rms(kvl, wkv_ref[...])
        qn_ref[...] = qn.astype(MXU_DTYPE)
        kvn_ref[...] = kvn.astype(MXU_DTYPE)
        qraw = _mm(qn, uq_ref[...])
        kvraw = _mm(kvn, ukv_ref[...])
        dq_nope, dq_pe, dkv_parts = [], [], []
        dqnw_n = jnp.zeros((1, NOPE), F32)
        dqnw_p = jnp.zeros((1, ROPE), F32)
        dknw_n = jnp.zeros((1, NOPE), F32)
        dkpe = jnp.zeros((tm, ROPE), F32)
        for h in range(H):
            dq = dq_ref[h] * ATT_SCALE
            x = qraw[:, h * NOPE:(h + 1) * NOPE]
            dx, dw = rms_bwd(dq[:, :NOPE], x, qnw_[:, :NOPE], rms(x, qnw_[:, :NOPE])[1])
            dq_nope.append(dx)
            dqnw_n = dqnw_n + dw
            x = qraw[:, H * NOPE + h * ROPE:H * NOPE + (h + 1) * ROPE]
            dx, dw = rms_bwd(_rope_bwd(dq[:, NOPE:], cos, sin), x, qnw_[:, NOPE:], rms(x, qnw_[:, NOPE:])[1])
            dq_pe.append(dx)
            dqnw_p = dqnw_p + dw
            dk = dk_ref[h]
            x = kvraw[:, h * 256:h * 256 + NOPE]
            dx, dw = rms_bwd(dk[:, :NOPE], x, knw_[:, :NOPE], rms(x, knw_[:, :NOPE])[1])
            dknw_n = dknw_n + dw
            dkpe = dkpe + dk[:, NOPE:]
            dkv_parts += [dx, dv_ref[h]]
        dx, dknw_p = rms_bwd(_rope_bwd(dkpe, cos, sin), kpe_raw, knw_[:, NOPE:], rms(kpe_raw, knw_[:, NOPE:])[1])
        dkpe_ref[...] = jnp.concatenate([dx, jnp.zeros((tm, LANES - ROPE), F32)], axis=-1).astype(MXU_DTYPE)
        dqraw = jnp.concatenate(dq_nope + dq_pe, axis=-1).astype(MXU_DTYPE)
        dkvraw = jnp.concatenate(dkv_parts, axis=-1).astype(MXU_DTYPE)
        dqraw_ref[...] = dqraw
        dkvraw_ref[...] = dkvraw
        dx, dw = rms_bwd(_mm_nt(dqraw, uq_ref[...]), ql, wq_ref[...], rq)
        dql_ref[...] = dx.astype(MXU_DTYPE)
        dwq_ref[...] += dw
        dx, dw = rms_bwd(_mm_nt(dkvraw, ukv_ref[...]), kvl, wkv_ref[...], rkv)
        dkvl_ref[...] = dx.astype(MXU_DTYPE)
        dwkv_ref[...] += dw
        dqnw_ref[...] += jnp.concatenate([dqnw_n, dqnw_p], axis=-1)
        dknw_ref[...] += jnp.concatenate([dknw_n, dknw_p], axis=-1)

    full = lambda a: pl.BlockSpec(a.shape, lambda i: (0,) * a.ndim)
    rows = lambda n: pl.BlockSpec((tm, n), lambda i: (i, 0))
    const = lambda n: pl.BlockSpec((1, n), lambda i: (0, 0))
    NQ, NKV = w_uq_p.shape[1], w_ukv.shape[1]
    return _call_beside(
        body, transfer, grid=(T // tm,), name="mla_pre_bwd", scratch_shapes=[], semantics=("arbitrary",),
        args=(proj, proj, proj, cosf, sinf, w_qln, w_kvln, w_uq_p, w_ukv, qnw, knw, dq4, dk4, dv4),
        in_specs=[pl.BlockSpec((tm, 256), lambda i: (i, P_QLAT // 256)),
                  pl.BlockSpec((tm, 256), lambda i: (i, P_KVLAT // 256)),
                  pl.BlockSpec((tm, 128), lambda i: (i, P_KPE // 128)),
                  rows(ROPE), rows(ROPE),
                  full(w_qln), full(w_kvln), full(w_uq_p), full(w_ukv), full(qnw), full(knw),
                  pl.BlockSpec((H, tm, QK_DIM), lambda i: (0, i, 0)),
                  pl.BlockSpec((H, tm, QK_DIM), lambda i: (0, i, 0)),
                  pl.BlockSpec((H, tm, V_DIM), lambda i: (0, i, 0))],
        out_specs=[rows(Q_LORA), rows(KV_LORA), rows(LANES), rows(NQ), rows(NKV), rows(Q_LORA), rows(KV_LORA),
                   const(Q_LORA), const(KV_LORA), const(QK_DIM), const(QK_DIM)],
        out_shape=[SDS((T, Q_LORA), MXU_DTYPE), SDS((T, KV_LORA), MXU_DTYPE), SDS((T, LANES), MXU_DTYPE),
                   SDS((T, NQ), MXU_DTYPE), SDS((T, NKV), MXU_DTYPE),
                   SDS((T, Q_LORA), MXU_DTYPE), SDS((T, KV_LORA), MXU_DTYPE),
                   SDS((1, Q_LORA), F32), SDS((1, KV_LORA), F32), SDS((1, QK_DIM), F32), SDS((1, QK_DIM), F32)])


def _in_proj_bwd(dc, conv_w, dgz, dql, dkvl, dkpe, dgab, w_in_p, dh, x2, w_an, S):
    T, D = x2.shape
    N = w_in_p.shape[1]
    C3 = dc.shape[1]
    tm = min(512, T)
    tiles_per_seq = S // tm
    nblk = T // SUBLANES

    def body(dc_ref, nxt_ref, cw_ref, b_ref, c_ref, d_ref, e_ref, f_ref, w_ref, dh_ref, x_ref, wn_ref,
             dx_ref, dp_ref, dwn_ref):
        i = pl.program_id(0)

        @pl.when(i == 0)
        def _():
            dwn_ref[...] = jnp.zeros_like(dwn_ref)

        nxt = jnp.where(i % tiles_per_seq == tiles_per_seq - 1, 0.0, nxt_ref[...])
        dcv, cw = dc_ref[...], cw_ref[...]
        du = cw[3:4] * dcv
        for j in range(1, CONV_W):
            du = du + cw[3 - j:4 - j] * _shift_up(dcv, nxt, j)
        dp = jnp.concatenate([du.astype(MXU_DTYPE), b_ref[...], c_ref[...], d_ref[...], e_ref[...], f_ref[...]],
                             axis=-1).astype(MXU_DTYPE)
        dp_ref[...] = dp
        x = x_ref[...]
        _, r = _rms(x, wn_ref[...])
        dx, dw = _rms_bwd(_mm_nt(dp, w_ref[...]), x, wn_ref[...], r)
        dx_ref[...] = dh_ref[...] + dx
        dwn_ref[...] += dw

    rows = lambda n: pl.BlockSpec((tm, n), lambda i: (i, 0))
    return pl.pallas_call(
        body, grid=(T // tm,), name="in_proj_bwd",
        in_specs=[rows(C3),
                  pl.BlockSpec((SUBLANES, C3), lambda i: (jnp.minimum((i + 1) * (tm // SUBLANES), nblk - 1), 0)),
                  pl.BlockSpec((CONV_W, C3), lambda i: (0, 0)),
                  rows(dgz.shape[1]), rows(dql.shape[1]), rows(dkvl.shape[1]),
                  rows(dkpe.shape[1]), rows(dgab.shape[1]),
                  pl.BlockSpec((D, N), lambda i: (0, 0)), rows(D), rows(D), pl.BlockSpec((1, D), lambda i: (0, 0))],
        out_specs=[rows(D), rows(N), pl.BlockSpec((1, D), lambda i: (0, 0))],
        out_shape=[SDS((T, D), F32), SDS((T, N), MXU_DTYPE), SDS((1, D), F32)],
        compiler_params=_params(("arbitrary",)),
    )(dc, dc, conv_w, dgz, dql, dkvl, dkpe, dgab, w_in_p, dh, x2, w_an)


def _wgrad(a, b, name, column_shards=False):
    T, M = a.shape
    N = b.shape[1]
    tM = _divisor_tile(M, 1024)
    tN = N // N_DEV if column_shards else _divisor_tile(N, 1536)
    tk = min(T, 2048)
    nk = T // tk

    def body(a_ref, b_ref, o_ref, acc):
        k = pl.program_id(2)

        @pl.when(k == 0)
        def _():
            acc[...] = jnp.zeros_like(acc)

        acc[...] += _mm_tn(a_ref[...], b_ref[...])

        @pl.when(k == nk - 1)
        def _():
            o_ref[...] = acc[...].astype(WIRE_DTYPE).reshape(o_ref.shape)

    if column_shards:
        out_spec, out_shape = pl.BlockSpec((1, tM, tN), lambda i, j, k: (j, i, 0)), SDS((N_DEV, M, tN), WIRE_DTYPE)
    else:
        out_spec, out_shape = pl.BlockSpec((tM, tN), lambda i, j, k: (i, j)), SDS((M, N), WIRE_DTYPE)
    return pl.pallas_call(
        body, grid=(M // tM, N // tN, nk), name=name,
        in_specs=[pl.BlockSpec((tk, tM), lambda i, j, k: (k, i)), pl.BlockSpec((tk, tN), lambda i, j, k: (k, j))],
        out_specs=out_spec, out_shape=out_shape,
        scratch_shapes=[pltpu.VMEM((tM, tN), F32)],
        compiler_params=_params(("arbitrary", "arbitrary", "arbitrary")),
    )(a, b)


def _adamw(g, w, m, v):
    m = ADAM_B1 * m + (1.0 - ADAM_B1) * g
    v = ADAM_B2 * v + (1.0 - ADAM_B2) * jnp.square(g)
    m_hat = m / (1.0 - ADAM_B1 ** ADAM_STEP)
    v_hat = v / (1.0 - ADAM_B2 ** ADAM_STEP)
    return -ADAM_LR * (m_hat / (jnp.sqrt(v_hat) + ADAM_EPS) + ADAM_WD * w), m, v


def _reduce_adamw(parts, w, m, v, name):
    R, C = w.shape
    _, Rp, Cp = parts.shape
    tr = min(R, 256)
    tp = tr if Rp == R else Rp

    def body(p_ref, w_ref, m_ref, v_ref, g_ref, d_ref, nm_ref, nv_ref):
        g = p_ref[0].astype(F32)
        for s in range(1, N_DEV):
            g = g + p_ref[s].astype(F32)
        g = g[:tr, :C]
        g_ref[...] = g
        d_ref[...], nm_ref[...], nv_ref[...] = _adamw(g, w_ref[...], m_ref[...], v_ref[...])

    spec = pl.BlockSpec((tr, C), lambda i: (i, 0))
    return pl.pallas_call(
        body, grid=(R // tr,), name=name,
        in_specs=[pl.BlockSpec((N_DEV, tp, Cp), lambda i: (0, i, 0)), spec, spec, spec],
        out_specs=[spec] * 4, out_shape=[SDS((R, C), F32)] * 4,
        compiler_params=_params(("arbitrary",)),
    )(parts, w, m, v)


SMALL_ROWS, SMALL_COLS = 16, 1024
SMALL_LAYOUT = (
    ("attn_norm_w", 0, 1, 1024, 1024), ("mlp_norm_w", 1, 1, 1024, 1024), ("q_lat_norm_w", 2, 1, 256, 256),
    ("kv_lat_norm_w", 3, 1, 256, 256), ("q_norm_w", 4, 1, 192, 192), ("k_norm_w", 5, 1, 192, 192),
    ("mla_out_norm_w", 6, 4, 128, 128), ("a_log", 10, 1, 128, 4), ("dt_bias", 11, 1, 128, 4),
    ("gdn_norm_w", 12, 1, 128, 128))
LOSS_ENTRY = ("loss", 13, 1, 128, 128)


def _adamw_replicated(parts, ws, ms, vs):
    n = len(SMALL_LAYOUT)

    def body(*refs):
        p_ref = refs[0]
        w_refs, m_refs, v_refs = refs[1:1 + n], refs[1 + n:1 + 2 * n], refs[1 + 2 * n:1 + 3 * n]
        outs = refs[1 + 3 * n:]
        s = p_ref[0]
        for d in range(1, N_DEV):
            s = s + p_ref[d]
        for i, (_, r0, nr, _, pw) in enumerate(SMALL_LAYOUT):
            g = s[r0:r0 + nr, :pw]
            outs[i][...] = g
            outs[n + i][...], outs[2 * n + i][...], outs[3 * n + i][...] = _adamw(
                g, w_refs[i][...], m_refs[i][...], v_refs[i][...])
        _, r0, nr, gw, _ = LOSS_ENTRY
        outs[4 * n][...] = s[r0:r0 + nr, :gw]

    res = pl.pallas_call(
        body, name="adamw_replicated",
        out_shape=[SDS(w.shape, F32) for w in ws] * 4 + [SDS((1, LANES), F32)],
        compiler_params=_params(),
    )(parts, *ws, *ms, *vs)
    return [res[k * n:(k + 1) * n] for k in range(4)], res[4 * n][0, 0]


COPIES_PER_ARRAY = N_DEV - 1


def _two_level_gather(srcs, outs, send_sems, recv_sems, local_sems=None, stage="all"):
    mx, my, mc = lax.axis_index("x"), lax.axis_index("y"), lax.axis_index("c")
    me, sibling = (mx, my, mc), (mx, my, 1 - mc)
    chips = [(1 - mx, my), (mx, 1 - my), (1 - mx, 1 - my)]
    arrays = range(len(srcs))

    def copy(a, k, block, to, src=None):
        px, py, pc = block
        slot = outs[a].at[4 * px + 2 * py + pc]
        sem = a * COPIES_PER_ARRAY + k
        return pltpu.make_async_remote_copy(
            src_ref=slot if src is None else src, dst_ref=slot,
            send_sem=send_sems.at[sem], recv_sem=recv_sems.at[sem], device_id=to, device_id_type=MESH_ID)

    mine = [] if local_sems is None else [
        pltpu.make_async_copy(srcs[a], outs[a].at[4 * mx + 2 * my + mc], local_sems.at[a]) for a in arrays]
    first = []
    for a in arrays:
        first.append(copy(a, 0, me, sibling, src=srcs[a]))
        first += [copy(a, 1 + j, me, (*chip, mc), src=srcs[a]) for j, chip in enumerate(chips)]
    if stage in ("all", "start"):
        for cp in mine + first:
            cp.start()
    if stage in ("all", "finish"):
        forwards = []
        for j, chip in enumerate(chips):
            for a in arrays:
                copy(a, 1 + j, (*chip, mc), me).wait_recv()
                fwd = copy(a, 4 + j, (*chip, mc), sibling)
                fwd.start()
                forwards.append(fwd)
        for a in arrays:
            copy(a, 0, sibling, me).wait_recv()
        for j, chip in enumerate(chips):
            for a in arrays:
                copy(a, 4 + j, (*chip, 1 - mc), me).wait_recv()
        for cp in first + forwards:
            cp.wait_send()
        for cp in mine:
            cp.wait()


def _comm_scratch(n):
    return [pltpu.SemaphoreType.DMA((n * COPIES_PER_ARRAY,)), pltpu.SemaphoreType.DMA((n * COPIES_PER_ARRAY,)),
            pltpu.SemaphoreType.DMA((n,))]


def _any_specs(n):
    return [pl.BlockSpec(memory_space=pl.ANY)] * n


def _gather_weights(shards):
    n = len(shards)

    def body(*refs):
        _two_level_gather(refs[:n], refs[n:2 * n], *refs[2 * n:])

    return pl.pallas_call(
        body, name="gather_weights",
        out_shape=[SDS((N_DEV,) + s.shape, s.dtype) for s in shards],
        in_specs=_any_specs(n), out_specs=_any_specs(n), scratch_shapes=_comm_scratch(n),
    )(*shards)


def _gather_small_grads(gs, loss_lanes):
    gs = list(gs) + [loss_lanes]
    n = len(gs)

    def body(*refs):
        g_refs, out_ref = refs[:n], refs[n]
        tile, send_sems, recv_sems = refs[n + 1:]
        tile[...] = jnp.zeros_like(tile)
        for (_, r0, nr, gw, _), g in zip(SMALL_LAYOUT + (LOSS_ENTRY,), g_refs):
            tile[r0:r0 + nr, 0:gw] = g[...]
        me = 4 * lax.axis_index("x") + 2 * lax.axis_index("y") + lax.axis_index("c")
        out_ref[me] = tile[...]
        _two_level_gather([tile], [out_ref], send_sems, recv_sems)

    return pl.pallas_call(
        body, name="gather_small_grads",
        out_shape=SDS((N_DEV, SMALL_ROWS, SMALL_COLS), F32),
        in_specs=[pl.BlockSpec(memory_space=pltpu.VMEM)] * n,
        out_specs=pl.BlockSpec(memory_space=pltpu.VMEM),
        scratch_shapes=[pltpu.VMEM((SMALL_ROWS, SMALL_COLS), F32),
                        pltpu.SemaphoreType.DMA((COPIES_PER_ARRAY,)), pltpu.SemaphoreType.DMA((COPIES_PER_ARRAY,))],
    )(*gs)


def _exchange_grads(slabs):
    n = len(slabs)

    def body(*refs):
        _exchange(refs[:n], refs[n:2 * n], *refs[2 * n:])

    return pl.pallas_call(
        body, name="exchange_grads",
        out_shape=[SDS(s.shape, s.dtype) for s in slabs],
        in_specs=_any_specs(n), out_specs=_any_specs(n), scratch_shapes=_comm_scratch(n),
    )(*slabs)


class _Transfer:
    def __init__(self, kind, arrays):
        self.kind, self.arrays, self.n = kind, list(arrays), len(arrays)

    def out_shapes(self):
        if self.kind == "gather":
            return [SDS((N_DEV,) + a.shape, a.dtype) for a in self.arrays]
        return [SDS(a.shape, a.dtype) for a in self.arrays]

    def run(self, srcs, outs, sems, stage):
        fn = _two_level_gather if self.kind == "gather" else _exchange
        fn(srcs, outs, *sems, stage=stage)


def _call_beside(body, transfer, *, grid, in_specs, out_specs, out_shape, scratch_shapes, name, semantics, args):
    if transfer is None:
        res = pl.pallas_call(body, grid=grid, in_specs=in_specs, out_specs=out_specs, out_shape=out_shape,
                             scratch_shapes=scratch_shapes, name=name, compiler_params=_params(semantics))(*args)
        return list(res), []
    n_in, n_out, n_s, n = len(in_specs), len(out_specs), len(scratch_shapes), transfer.n

    def wrapped(*refs):
        ins, refs = refs[:n_in], refs[n_in:]
        t_in, refs = refs[:n], refs[n:]
        outs, refs = refs[:n_out], refs[n_out:]
        t_out, refs = refs[:n], refs[n:]
        scratch, sems = refs[:n_s], refs[n_s:]
        first = functools.reduce(jnp.logical_and, [pl.program_id(i) == 0 for i in range(len(grid))])
        last = functools.reduce(jnp.logical_and, [pl.program_id(i) == g - 1 for i, g in enumerate(grid)])

        @pl.when(first)
        def _():
            transfer.run(t_in, t_out, sems, "start")

        body(*ins, *outs, *scratch)

        @pl.when(last)
        def _():
            transfer.run(t_in, t_out, sems, "finish")

    res = pl.pallas_call(
        wrapped, grid=grid, in_specs=list(in_specs) + _any_specs(n), out_specs=list(out_specs) + _any_specs(n),
        out_shape=list(out_shape) + transfer.out_shapes(), scratch_shapes=list(scratch_shapes) + _comm_scratch(n),
        name=name, compiler_params=_params(semantics))(*args, *transfer.arrays)
    return list(res[:n_out]), list(res[n_out:])


EXCHANGE_FLIPS = ((0, 0, 1), (1, 0, 0), (0, 1, 0), (1, 1, 0), (1, 0, 1), (0, 1, 1), (1, 1, 1))


def _exchange(srcs, outs, send_sems, recv_sems, local_sems, stage="all"):
    mx, my, mc = lax.axis_index("x"), lax.axis_index("y"), lax.axis_index("c")
    arrays = range(len(srcs))
    copies = [pltpu.make_async_copy(srcs[a].at[4 * mx + 2 * my + mc], outs[a].at[N_DEV - 1], local_sems.at[a])
              for a in arrays]
    for k, (fx, fy, fc) in enumerate(EXCHANGE_FLIPS):
        px = 1 - mx if fx else mx
        py = 1 - my if fy else my
        pc = 1 - mc if fc else mc
        for a in arrays:
            sem = a * COPIES_PER_ARRAY + k
            copies.append(pltpu.make_async_remote_copy(
                src_ref=srcs[a].at[4 * px + 2 * py + pc], dst_ref=outs[a].at[k],
                send_sem=send_sems.at[sem], recv_sem=recv_sems.at[sem],
                device_id=(px, py, pc), device_id_type=MESH_ID))
    if stage in ("all", "start"):
        for cp in copies:
            cp.start()
    if stage in ("all", "finish"):
        for cp in copies:
            cp.wait()


def _w_in_to_padded(w):
    z = lambda n: jnp.zeros((w.shape[0], n), w.dtype)
    return jnp.concatenate([w[:, O_GQKV:O_GZ], w[:, O_GZ:O_GAB], w[:, O_QLAT:O_KVLAT], w[:, O_KVLAT:O_KPE],
                            w[:, O_KPE:O_GQKV], z(P_GAB - P_KPE - ROPE), w[:, O_GAB:O_END],
                            z(P_WIDTH - P_GAB - (O_END - O_GAB))], axis=1)


def _w_in_from_padded(wp):
    return jnp.concatenate([wp[:, P_QLAT:P_QLAT + 256], wp[:, P_KVLAT:P_KVLAT + 256], wp[:, P_KPE:P_KPE + ROPE],
                            wp[:, P_GQKV:P_GZ], wp[:, P_GZ:P_QLAT], wp[:, P_GAB:P_GAB + (O_END - O_GAB)]], axis=1)


def _w_uq_to_headsplit(w):
    w3 = w.reshape(w.shape[0], MLA_HEADS, QK_DIM)
    return jnp.concatenate([w3[:, :, :NOPE].reshape(w.shape[0], -1), w3[:, :, NOPE:].reshape(w.shape[0], -1)], axis=1)


def _w_uq_from_headsplit(wp):
    n = wp[:, :MLA_HEADS * NOPE].reshape(wp.shape[0], MLA_HEADS, NOPE)
    p = wp[:, MLA_HEADS * NOPE:].reshape(wp.shape[0], MLA_HEADS, ROPE)
    return jnp.concatenate([n, p], axis=2).reshape(wp.shape[0], -1)


def _lane_vec(v4):
    return jnp.pad(v4.reshape(1, -1), ((0, 0), (0, LANES - v4.shape[-1])))


def _local_step(x, positions, target, attn_norm_w, w_in, q_lat_norm_w, w_uq, kv_lat_norm_w, w_ukv, q_norm_w,
                k_norm_w, mla_out_norm_w, conv_w, a_log, dt_bias, gdn_norm_w, w_out, mlp_norm_w, w_up, w_down,
                late_shards=None, exchange=False):
    B, S, D = x.shape
    T = B * S
    x2 = x.reshape(T, D)
    t2 = target.reshape(T, D)
    half = ROPE // 2
    inv_freq = ROPE_THETA ** (-jnp.arange(half, dtype=F32) / half)
    ang = positions.reshape(T, 1).astype(F32) * inv_freq
    cosf = jnp.concatenate([jnp.cos(ang)] * 2, axis=-1)
    sinf = jnp.concatenate([jnp.sin(ang)] * 2, axis=-1)
    w_in_p = _w_in_to_padded(w_in)
    w_uq_p = _w_uq_to_headsplit(w_uq)
    alog_l, dt_l = _lane_vec(a_log), _lane_vec(dt_bias)
    w_an, w_qln, w_kvln, qnw, knw, w_mn, gdn_w = (
        attn_norm_w, q_lat_norm_w, kv_lat_norm_w, q_norm_w, k_norm_w, mlp_norm_w, gdn_norm_w)

    proj, xn = _in_proj(x2, w_an, w_in_p)
    gather = None if late_shards is None else _Transfer("gather", late_shards[:1])
    (q4, k4, v4), late = _mla_pre(proj, cosf, sinf, w_qln, w_kvln, w_uq_p, w_ukv, qnw, knw, gather)
    if late:
        w_out = late[0].reshape(-1, D)
    (o_mla, lse), _ = _attn_fwd(q4, k4, v4, B, S)
    qg, kg, vg, gates = _gdn_pre(proj, conv_w, alog_l, dt_l, S)
    gather = None if late_shards is None else _Transfer("gather", late_shards[1:])
    (o_gdn, states, ainv, u4, w4), late = _gdn_fwd(qg, kg, vg, gates, B, S, gather)
    if late:
        w_up, w_down = late[0], late[1].reshape(-1, D)
    h2, mix = _mix_out(o_mla, o_gdn, proj, x2, mla_out_norm_w, gdn_w, w_out)
    up, hn, dy, sq = _mlp_fwd(h2, w_mn, w_up, w_down, t2)
    loss = (0.5 / D) * jnp.sum(sq[:, 0, 0])

    dh, dhb, dup, act, dyb, d_mlp_norm = _mlp_bwd(dy, up, h2, w_mn, w_up, w_down)
    g_w_down = _wgrad(act, dyb, "wgrad_down")
    g_w_up = _wgrad(hn, dup, "wgrad_up", column_shards=True)
    do_mla, do_gdn, dz, d_mla_w, d_gdn_w = _mix_bwd(dhb, o_mla, o_gdn, proj, mla_out_norm_w, gdn_w, w_out)
    g_w_out = _wgrad(mix, dhb, "wgrad_out")
    first = ("w_down",)
    second = ("w_out",)
    third = ("w_up", "w_uq", "w_ukv")
    mats = dict(w_up=g_w_up, w_down=g_w_down, w_out=g_w_out)

    def sending(names):
        return _Transfer("exchange", [_slabs(n, mats[n]) for n in names]) if exchange else None

    (dq4, dk4, dv4), got = _attn_bwd(q4, k4, v4, do_mla, o_mla, lse, B, S, sending(first))
    mats.update(zip(first, got))
    (dql, dkvl, dkpe, dqraw, dkvraw, qn, kvn, d_wqln, d_wkvln, d_qnw, d_knw), got = _mla_pre_bwd(
        proj, cosf, sinf, w_qln, w_kvln, w_uq_p, w_ukv, qnw, knw, dq4, dk4, dv4, sending(second))
    mats.update(zip(second, got))
    mats.update(w_uq=_wgrad(qn, dqraw, "wgrad_uq"), w_ukv=_wgrad(kvn, dkvraw, "wgrad_ukv"))
    (dqg, dkg, dvg, dgb4), got = _gdn_bwd(qg, kg, vg, gates, states, ainv, u4, w4, do_gdn, B, S, sending(third))
    mats.update(zip(third, got))
    dc, dgab, g_conv, d_alog, d_dt = _gdn_pre_bwd(proj, conv_w, alog_l, dt_l, dqg, dkg, dvg, dgb4, S)
    grad_x2, dproj, d_attn_norm = _in_proj_bwd(dc, conv_w, dz, dql, dkvl, dkpe, dgab, w_in_p, dh, x2, w_an, S)
    mats.update(w_in=_wgrad(xn, dproj, "wgrad_in"), conv_w=g_conv)
    if exchange:
        last = ("w_in", "conv_w")
        mats.update(zip(last, _exchange_grads([_slabs(n, mats[n]) for n in last])))
    small = dict(attn_norm_w=d_attn_norm, mlp_norm_w=d_mlp_norm, q_lat_norm_w=d_wqln, kv_lat_norm_w=d_wkvln,
                 q_norm_w=d_qnw, k_norm_w=d_knw, mla_out_norm_w=d_mla_w, a_log=d_alog, dt_bias=d_dt,
                 gdn_norm_w=d_gdn_w)
    return loss, grad_x2.reshape(B, S, D), mats, [small[n] for n, *_ in SMALL_LAYOUT]


BIG = ("w_in", "w_uq", "w_ukv", "conv_w", "w_out", "w_up", "w_down")
ALL_W = ("attn_norm_w", "w_in", "q_lat_norm_w", "w_uq", "kv_lat_norm_w", "w_ukv", "q_norm_w", "k_norm_w",
         "mla_out_norm_w", "conv_w", "a_log", "dt_bias", "gdn_norm_w", "w_out", "mlp_norm_w", "w_up", "w_down")
WIRE_SHAPE = {"w_in": (1024, 384), "w_uq": (256, 128), "conv_w": (16, 256)}


def _pad2(a, rows, cols):
    return jnp.pad(a, [(0, 0)] * (a.ndim - 2) + [(0, rows - a.shape[-2]), (0, cols - a.shape[-1])])


def _cols_to_full(stack, cols):
    return jnp.moveaxis(stack[:, :, :cols], 0, 1).reshape(stack.shape[1], N_DEV * cols)


def _full_to_cols(full, wire_cols):
    r, n = full.shape
    return _pad2(jnp.moveaxis(full.reshape(r, N_DEV, n // N_DEV), 1, 0), r, wire_cols)


def _slabs(name, g):
    if name == "w_in":
        return _full_to_cols(_w_in_from_padded(g), WIRE_SHAPE["w_in"][1])
    if name == "w_uq":
        return _full_to_cols(_w_uq_from_headsplit(g), WIRE_SHAPE["w_uq"][1])
    if name == "w_ukv":
        return _full_to_cols(g, g.shape[1] // N_DEV)
    if name == "conv_w":
        return _pad2(_full_to_cols(g.astype(WIRE_DTYPE), g.shape[1] // N_DEV), *WIRE_SHAPE["conv_w"])
    if name == "w_up":
        return g
    return g.reshape(N_DEV, -1, g.shape[-1])


def kernel(x, positions, attn_norm_w, w_in, q_lat_norm_w, w_uq, kv_lat_norm_w, w_ukv, q_norm_w, k_norm_w, mla_out_norm_w, conv_w, a_log, dt_bias, gdn_norm_w, w_out, mlp_norm_w, w_up, w_down, loss_target, m_attn_norm_w, m_w_in, m_q_lat_norm_w, m_w_uq, m_kv_lat_norm_w, m_w_ukv, m_q_norm_w, m_k_norm_w, m_mla_out_norm_w, m_conv_w, m_a_log, m_dt_bias, m_gdn_norm_w, m_w_out, m_mlp_norm_w, m_w_up, m_w_down, v_attn_norm_w, v_w_in, v_q_lat_norm_w, v_w_uq, v_kv_lat_norm_w, v_w_ukv, v_q_norm_w, v_k_norm_w, v_mla_out_norm_w, v_conv_w, v_a_log, v_dt_bias, v_gdn_norm_w, v_w_out, v_mlp_norm_w, v_w_up, v_w_down):
    env = dict(locals())
    W = {n: env[n][0] for n in ALL_W}
    Mo = {n: env["m_" + n][0] for n in ALL_W}
    Vo = {n: env["v_" + n][0] for n in ALL_W}

    two_d = lambda a: a.reshape(1, -1) if a.ndim == 1 else a
    D = x.shape[-1]

    s_in, s_uq, s_ukv, s_conv = _gather_weights([
        _pad2(W["w_in"].astype(WIRE_DTYPE), *WIRE_SHAPE["w_in"]),
        _pad2(W["w_uq"].astype(WIRE_DTYPE), *WIRE_SHAPE["w_uq"]),
        W["w_ukv"].astype(WIRE_DTYPE), _pad2(W["conv_w"], *WIRE_SHAPE["conv_w"])])
    late = [W["w_out"].astype(WIRE_DTYPE), W["w_up"].astype(WIRE_DTYPE), W["w_down"].astype(WIRE_DTYPE)]

    loss, grad_x, parts, gs = _local_step(
        x, positions, loss_target, two_d(W["attn_norm_w"]), _cols_to_full(s_in, W["w_in"].shape[1]),
        two_d(W["q_lat_norm_w"]), _cols_to_full(s_uq, W["w_uq"].shape[1]), two_d(W["kv_lat_norm_w"]),
        _cols_to_full(s_ukv, W["w_ukv"].shape[1]), two_d(W["q_norm_w"]), two_d(W["k_norm_w"]),
        W["mla_out_norm_w"], _cols_to_full(s_conv[:, :CONV_W], W["conv_w"].shape[1]), two_d(W["a_log"]),
        two_d(W["dt_bias"]), two_d(W["gdn_norm_w"]), None, two_d(W["mlp_norm_w"]), None, None,
        late_shards=late, exchange=True)
    done = {n: _reduce_adamw(parts[n], W[n], Mo[n], Vo[n], "adamw_" + n) for n in BIG}
    names = [n for n, *_ in SMALL_LAYOUT]
    tiles = _gather_small_grads(gs, jnp.full((1, LANES), loss, F32))
    small, loss = _adamw_replicated(tiles, [two_d(W[n]) for n in names], [two_d(Mo[n]) for n in names],
                                    [two_d(Vo[n]) for n in names])
    for i, n in enumerate(names):
        done[n] = [small[kind][i] for kind in range(4)]
    res = [done[n][kind].reshape(env[n].shape) for kind in range(4) for n in ALL_W]
    return (loss, grad_x, *res)
```

```python
import functools

import jax
import jax.numpy as jnp
from jax import lax
from jax.experimental import pallas as pl
from jax.experimental.pallas import tpu as pltpu

F32 = jnp.float32
MXU_DTYPE = jnp.bfloat16
WIRE_DTYPE = jnp.bfloat16
SDS = jax.ShapeDtypeStruct
HIGHEST = lax.Precision.HIGHEST
MESH_ID = pl.DeviceIdType.MESH

D_MODEL = 1024
MLA_HEADS = 4
Q_LORA = 256
KV_LORA = 256
NOPE = 128
ROPE = 64
QK_DIM = NOPE + ROPE
V_DIM = 128
ROPE_THETA = 10000.0
GDN_HEADS = 4
GDN_DIM = 128
GDN_WIDTH = GDN_HEADS * GDN_DIM
CONV_W = 4
CHUNK = 64
D_FF = 4 * D_MODEL
EPS = 1e-6
ATT_SCALE = QK_DIM ** -0.5
GDN_QSCALE = GDN_DIM ** -0.5
N_DEV = 8
ATTN_BLOCK = 512
ATTN_CHAINS = 2
MLP_FWD_SHARDS = 4
MLP_BWD_SHARDS = 4

ADAM_LR = 0.001
ADAM_B1 = 0.9
ADAM_B2 = 0.999
ADAM_EPS = 1e-08
ADAM_WD = 0.01
ADAM_STEP = 10

LANES = 128
SUBLANES = 8
VMEM_LIMIT = 60 * 1024 * 1024

P_GQKV, P_GZ, P_QLAT, P_KVLAT, P_KPE, P_GAB = 0, 1536, 2048, 2304, 2560, 2688
P_WIDTH = 2816
O_QLAT, O_KVLAT, O_KPE, O_GQKV, O_GZ, O_GAB, O_END = 0, 256, 512, 576, 2112, 2624, 2632


def _params(sem=None, vmem=VMEM_LIMIT):
    kw = dict(vmem_limit_bytes=vmem)
    if sem is not None:
        kw["dimension_semantics"] = sem
    return pltpu.CompilerParams(**kw)


def _mm(a, b):
    return jnp.dot(a.astype(MXU_DTYPE), b.astype(MXU_DTYPE), preferred_element_type=F32)


def _mm_nt(a, b):
    return lax.dot_general(a.astype(MXU_DTYPE), b.astype(MXU_DTYPE), (((1,), (1,)), ((), ())),
                           preferred_element_type=F32)


def _mm_tn(a, b):
    return lax.dot_general(a.astype(MXU_DTYPE), b.astype(MXU_DTYPE), (((0,), (0,)), ((), ())),
                           preferred_element_type=F32)


def _split(a):
    hi = a.astype(MXU_DTYPE)
    return hi, (a - hi.astype(F32)).astype(MXU_DTYPE)


def _mm_split(a, b):
    (ah, al), (bh, bl) = a, b
    dot = lambda x, y: jnp.dot(x, y, preferred_element_type=F32)
    if MXU_DTYPE == F32:
        return dot(ah, bh)
    return dot(ah, bh) + dot(ah, bl) + dot(al, bh)


def _mm_exact(a, b):
    return _mm_split(_split(a), _split(b))


def _row_sum(v, on_mxu=False):
    if not on_mxu:
        return jnp.sum(v, axis=-1, keepdims=True)
    d = v.shape[-1]
    ones = jnp.ones((d, LANES), MXU_DTYPE)
    s = sum(jnp.dot(p, ones, preferred_element_type=F32) for p in _split(v))
    return s[:, :d] if d <= LANES else jnp.tile(s, (1, d // LANES))


def _rms(x, w, on_mxu=False):
    r = lax.rsqrt(_row_sum(x * x, on_mxu) * (1.0 / x.shape[-1]) + EPS)
    return x * r * w, r


def _rms_bwd(dy, x, w, r, on_mxu=False):
    xh = x * r
    dyw = dy * w
    dx = r * (dyw - xh * (_row_sum(dyw * xh, on_mxu) * (1.0 / x.shape[-1])))
    dw = jnp.sum(dy * xh, axis=0, keepdims=True)
    return dx, dw


def _l2n(x, scale):
    return x * (lax.rsqrt(_row_sum(x * x) + EPS) * scale)


def _l2n_bwd(dy, x, scale):
    r = lax.rsqrt(_row_sum(x * x) + EPS)
    xh = x * r
    return (scale * r) * (dy - xh * _row_sum(dy * xh))


def _rot(t):
    return jnp.concatenate([-t[:, ROPE // 2:], t[:, :ROPE // 2]], axis=-1)


def _rot_t(t):
    return jnp.concatenate([t[:, ROPE // 2:], -t[:, :ROPE // 2]], axis=-1)


def _rope(t, cos, sin):
    return t * cos + _rot(t) * sin


def _rope_bwd(d, cos, sin):
    return d * cos + _rot_t(d * sin)


def _sigmoid(x):
    return jax.nn.sigmoid(x)


def _shift_down(x, halo, j):
    if j == 0:
        return x
    xr = pltpu.roll(x, j, 0)
    hr = pltpu.roll(halo, j, 0)
    row = lax.broadcasted_iota(jnp.int32, halo.shape, 0)
    top = jnp.where(row < j, hr, xr[:SUBLANES])
    return jnp.concatenate([top, xr[SUBLANES:]], axis=0)


def _shift_up(x, nxt, j):
    if j == 0:
        return x
    n = x.shape[0]
    xr = pltpu.roll(x, n - j, 0)
    nr = pltpu.roll(nxt, SUBLANES - j, 0)
    row = lax.broadcasted_iota(jnp.int32, nxt.shape, 0)
    bot = jnp.where(row >= SUBLANES - j, nr, xr[n - SUBLANES:])
    return jnp.concatenate([xr[:n - SUBLANES], bot], axis=0)


def _chunk_cumsum(y, row_in_chunk):
    s = 1
    while s < CHUNK:
        y = y + jnp.where(row_in_chunk >= s, pltpu.roll(y, s, 0), 0.0)
        s *= 2
    return y


def _chunk_rev_cumsum(y, row_in_chunk):
    n = y.shape[0]
    s = 1
    while s < CHUNK:
        y = y + jnp.where(row_in_chunk + s < CHUNK, pltpu.roll(y, n - s, 0), 0.0)
        s *= 2
    return y


def _together(generators):
    alive = list(generators)
    while alive:
        nxt = []
        for g in alive:
            try:
                next(g)
                nxt.append(g)
            except StopIteration:
                pass
        alive = nxt
        yield


def _lockstep(generators):
    for _ in _together(generators):
        pass


def _pick_lane(tile, lane, idx):
    return jnp.sum(jnp.where(lane == idx, tile, 0.0), axis=-1, keepdims=True)


def _divisor_tile(n, cap, unit=LANES):
    best = unit
    t = unit
    while t <= min(n, cap):
        if n % t == 0:
            best = t
        t += unit
    return n if n <= cap else best


def _in_proj(x2, w_an, w_in_p):
    T, D = x2.shape
    N = w_in_p.shape[1]
    tm = min(512, T)

    def body(x_ref, wn_ref, w_ref, proj_ref, xn_ref):
        xn, _ = _rms(x_ref[...], wn_ref[...])
        xn = xn.astype(MXU_DTYPE)
        xn_ref[...] = xn
        proj_ref[...] = jnp.dot(xn, w_ref[...], preferred_element_type=F32)

    return pl.pallas_call(
        body, grid=(T // tm,), name="in_proj",
        in_specs=[pl.BlockSpec((tm, D), lambda i: (i, 0)), pl.BlockSpec((1, D), lambda i: (0, 0)),
                  pl.BlockSpec((D, N), lambda i: (0, 0))],
        out_specs=[pl.BlockSpec((tm, N), lambda i: (i, 0)), pl.BlockSpec((tm, D), lambda i: (i, 0))],
        out_shape=[SDS((T, N), F32), SDS((T, D), MXU_DTYPE)],
        compiler_params=_params(("arbitrary",)),
    )(x2, w_an, w_in_p)


def _mla_pre(proj, cosf, sinf, w_qln, w_kvln, w_uq_p, w_ukv, qnw, knw, transfer=None):
    T = proj.shape[0]
    tm = min(256, T)
    H = MLA_HEADS

    def body(ql_ref, kvl_ref, kpe_ref, cos_ref, sin_ref, wq_ref, wkv_ref, uq_ref, ukv_ref, qnw_ref, knw_ref,
             q_out, k_out, v_out):
        rms = functools.partial(_rms, on_mxu=True)
        cos, sin = cos_ref[...], sin_ref[...]
        qnw_, knw_ = qnw_ref[...], knw_ref[...]
        qn, _ = rms(ql_ref[...], wq_ref[...])
        kvn, _ = rms(kvl_ref[...], wkv_ref[...])
        qraw = _mm(qn, uq_ref[...])
        kvraw = _mm(kvn, ukv_ref[...])
        kpe = _rope(rms(kpe_ref[...][:, :ROPE], knw_[:, NOPE:])[0], cos, sin)
        for h in range(H):
            qn_h = rms(qraw[:, h * NOPE:(h + 1) * NOPE], qnw_[:, :NOPE])[0]
            qp_h = _rope(rms(qraw[:, H * NOPE + h * ROPE:H * NOPE + (h + 1) * ROPE], qnw_[:, NOPE:])[0], cos, sin)
            q_out[h] = (jnp.concatenate([qn_h, qp_h], axis=-1) * ATT_SCALE).astype(MXU_DTYPE)
            kn_h = rms(kvraw[:, h * 256:h * 256 + NOPE], knw_[:, :NOPE])[0]
            k_out[h] = jnp.concatenate([kn_h, kpe], axis=-1).astype(MXU_DTYPE)
            v_out[h] = kvraw[:, h * 256 + NOPE:(h + 1) * 256].astype(MXU_DTYPE)

    full = lambda a: pl.BlockSpec(a.shape, lambda i: (0,) * a.ndim)
    return _call_beside(
        body, transfer, grid=(T // tm,), name="mla_pre", scratch_shapes=[], semantics=("arbitrary",),
        args=(proj, proj, proj, cosf, sinf, w_qln, w_kvln, w_uq_p, w_ukv, qnw, knw),
        in_specs=[pl.BlockSpec((tm, 256), lambda i: (i, P_QLAT // 256)),
                  pl.BlockSpec((tm, 256), lambda i: (i, P_KVLAT // 256)),
                  pl.BlockSpec((tm, 128), lambda i: (i, P_KPE // 128)),
                  pl.BlockSpec((tm, ROPE), lambda i: (i, 0)), pl.BlockSpec((tm, ROPE), lambda i: (i, 0)),
                  full(w_qln), full(w_kvln), full(w_uq_p), full(w_ukv), full(qnw), full(knw)],
        out_specs=[pl.BlockSpec((H, tm, QK_DIM), lambda i: (0, i, 0)),
                   pl.BlockSpec((H, tm, QK_DIM), lambda i: (0, i, 0)),
                   pl.BlockSpec((H, tm, V_DIM), lambda i: (0, i, 0))],
        out_shape=[SDS((H, T, QK_DIM), MXU_DTYPE), SDS((H, T, QK_DIM), MXU_DTYPE), SDS((H, T, V_DIM), MXU_DTYPE)])


def _attn_fwd(q4, k4, v4, B, S, transfer=None):
    H = MLA_HEADS
    bq = min(ATTN_BLOCK, S)
    nq = S // bq
    rows = bq // ATTN_CHAINS

    def body(q_ref, k_ref, v_ref, o_ref, lse_ref):
        col = lax.broadcasted_iota(jnp.int32, (rows, bq), 1)
        row = lax.broadcasted_iota(jnp.int32, (rows, bq), 0)

        def q_step(qi, carry):
            qs = pl.multiple_of(qi * bq, bq)
            qsub = [q_ref[0, pl.ds(qs + j * rows, rows), :] for j in range(ATTN_CHAINS)]

            def k_block(ks, cs, diagonal):
                k = k_ref[0, pl.ds(ks, bq), :]
                v = v_ref[0, pl.ds(ks, bq), :]
                out = [None] * ATTN_CHAINS

                def chain(j):
                    m, l, acc = cs[j]
                    s = _mm_nt(qsub[j], k)
                    yield
                    if diagonal:
                        s = jnp.where(col <= row + j * rows, s, -jnp.inf)
                    m_new = jnp.maximum(m, jnp.max(s, axis=-1, keepdims=True))
                    p = jnp.exp(s - m_new)
                    a = jnp.exp(m - m_new)
                    l_new = a * l + jnp.sum(p, axis=-1, keepdims=True)
                    yield
                    out[j] = (m_new, l_new, a * acc + _mm(p, v))

                _lockstep([chain(j) for j in range(ATTN_CHAINS)])
                return tuple(out)

            init = tuple((jnp.full((rows, 1), -jnp.inf, F32), jnp.zeros((rows, 1), F32),
                          jnp.zeros((rows, V_DIM), F32)) for _ in range(ATTN_CHAINS))
            cs = lax.fori_loop(0, qi, lambda kj, c: k_block(pl.multiple_of(kj * bq, bq), c, False), init)
            for j, (m, l, acc) in enumerate(k_block(qs, cs, True)):
                o_ref[0, pl.ds(qs + j * rows, rows), :] = acc / l
                lse_ref[0, pl.ds(qs + j * rows, rows), :] = m + jnp.log(l)
            return carry

        lax.fori_loop(0, nq, q_step, 0)

    spec = lambda d: pl.BlockSpec((1, S, d), lambda h, b: (h, b, 0))
    return _call_beside(
        body, transfer, grid=(H, B), name="attn_fwd",
        in_specs=[spec(QK_DIM), spec(QK_DIM), spec(V_DIM)],
        out_specs=[spec(V_DIM), spec(1)],
        out_shape=[SDS((H, B * S, V_DIM), F32), SDS((H, B * S, 1), F32)],
        scratch_shapes=[], semantics=("arbitrary", "arbitrary"), args=(q4, k4, v4))


def _conv_taps(u, halo, w):
    sh = [_shift_down(u, halo, j) for j in range(CONV_W)]
    c = w[0:1] * sh[3] + w[1:2] * sh[2] + w[2:3] * sh[1] + w[3:4] * sh[0]
    return c, sh


def _gate_values(gab, alog_l, dt_l, lane):
    g = -jnp.exp(alog_l) * jax.nn.softplus(gab + dt_l)
    g = jnp.where(lane < GDN_HEADS, g, 0.0)
    beta = jnp.where((lane >= GDN_HEADS) & (lane < 2 * GDN_HEADS), _sigmoid(gab), 0.0)
    return g, beta


def _gdn_pre(proj, conv_w, alog_l, dt_l, S):
    T = proj.shape[0]
    tm = min(256, T)
    tiles_per_seq = S // tm
    C3 = 3 * GDN_WIDTH
    H = GDN_HEADS

    def body(u_ref, halo_ref, gab_ref, w_ref, alog_ref, dt_ref, q_out, k_out, v_out, gates_out):
        i = pl.program_id(0)
        halo = jnp.where(i % tiles_per_seq == 0, 0.0, halo_ref[...])
        c, _ = _conv_taps(u_ref[...], halo, w_ref[...])
        a = c * _sigmoid(c)
        for h in range(H):
            xq = a[:, h * GDN_DIM:(h + 1) * GDN_DIM]
            xk = a[:, GDN_WIDTH + h * GDN_DIM:GDN_WIDTH + (h + 1) * GDN_DIM]
            q_out[h] = _l2n(xq, GDN_QSCALE)
            k_out[h] = _l2n(xk, 1.0)
            v_out[h] = a[:, 2 * GDN_WIDTH + h * GDN_DIM:2 * GDN_WIDTH + (h + 1) * GDN_DIM]
        lane = lax.broadcasted_iota(jnp.int32, (tm, LANES), 1)
        ric = lax.broadcasted_iota(jnp.int32, (tm, LANES), 0) % CHUNK
        g, beta = _gate_values(gab_ref[...], alog_ref[...], dt_ref[...], lane)
        gates_out[...] = _chunk_cumsum(g, ric) + beta

    hspec = pl.BlockSpec((H, tm, GDN_DIM), lambda i: (0, i, 0))
    return pl.pallas_call(
        body, grid=(T // tm,), name="gdn_pre",
        in_specs=[pl.BlockSpec((tm, C3), lambda i: (i, 0)),
                  pl.BlockSpec((SUBLANES, C3), lambda i: (jnp.maximum(i * (tm // SUBLANES) - 1, 0), 0)),
                  pl.BlockSpec((tm, LANES), lambda i: (i, P_GAB // LANES)),
                  pl.BlockSpec((CONV_W, C3), lambda i: (0, 0)),
                  pl.BlockSpec((1, LANES), lambda i: (0, 0)), pl.BlockSpec((1, LANES), lambda i: (0, 0))],
        out_specs=[hspec, hspec, hspec, pl.BlockSpec((tm, LANES), lambda i: (i, 0))],
        out_shape=[SDS((H, T, GDN_DIM), F32)] * 3 + [SDS((T, LANES), F32)],
        compiler_params=_params(("arbitrary",)),
    )(proj, proj, proj, conv_w, alog_l, dt_l)


def _unit_lower_inverses(Ls, eye):
    Ps = [eye - L for L in Ls]
    Ms = [_split(-L) for L in Ls]
    for _ in range(5):
        sq = [_mm_split(m, m) for m in Ms]
        Ms = [_split(s) for s in sq]
        Ps = [p + _mm_split(_split(p), m) for p, m in zip(Ps, Ms)]
    return Ps


def _chunk_decays(gt, lane, h, ri, ci, rcol):
    Gc = _pick_lane(gt, lane, h)
    bt = _pick_lane(gt, lane, h + GDN_HEADS)
    Gb = jnp.broadcast_to(Gc, (CHUNK, CHUNK))
    Gam = jnp.where(ri >= ci, jnp.exp(Gb - Gb.T), 0.0)
    Gl = jnp.sum(jnp.where(rcol == CHUNK - 1, Gc, 0.0), axis=0, keepdims=True)
    return Gc, bt, Gam, jnp.exp(Gc), jnp.exp(Gl - Gc), jnp.exp(Gl)


GDN_FWD_UNROLL = 16
GDN_BWD_UNROLL = 8
GDN_RECUR_STEPS_PER_STAGE = 2


def _gdn_fwd(qg, kg, vg, gates, B, S, transfer=None):
    H, D, C = GDN_HEADS, GDN_DIM, CHUNK
    NC = S // C
    U = GDN_FWD_UNROLL if NC % GDN_FWD_UNROLL == 0 else 1
    NG = NC // U

    def body(q_ref, k_ref, v_ref, g_ref, o_ref, st_ref, ai_ref, u_ref, w_ref, q2_s, au_s, bc_s, w2_s, el_s):
        h = pl.program_id(0)
        lane = lax.broadcasted_iota(jnp.int32, (C, LANES), 1)
        ri = lax.broadcasted_iota(jnp.int32, (C, C), 0)
        ci = lax.broadcasted_iota(jnp.int32, (C, C), 1)
        rcol = lax.broadcasted_iota(jnp.int32, (C, 1), 0)
        eye = (ri == ci).astype(F32)

        def group(gi, c):
            ns = [gi * U + j for j in range(U)]
            css = [pl.multiple_of(n * C, C) for n in ns]
            qs = [q_ref[0, pl.ds(cs, C), :] for cs in css]
            ks = [k_ref[0, pl.ds(cs, C), :] for cs in css]
            vs = [v_ref[0, pl.ds(cs, C), :] for cs in css]
            decs = [_chunk_decays(g_ref[pl.ds(cs, C), :], lane, h, ri, ci, rcol) for cs in css]
            qks = [_mm_nt(jnp.concatenate([q, k], axis=0), k) for q, k in zip(qs, ks)]
            ainvs = _unit_lower_inverses(
                [jnp.where(ri > ci, d[1] * qk[C:] * d[2], 0.0) for qk, d in zip(qks, decs)], eye)
            sols = [_mm_exact(a, jnp.concatenate([v * d[1], k * (d[1] * d[3])], axis=-1))
                    for a, k, v, d in zip(ainvs, ks, vs, decs)]
            atuw = [_mm(qk[:C] * d[2], sol) for qk, d, sol in zip(qks, decs, sols)]
            kduw = [_mm_tn(k * d[4], sol) for k, d, sol in zip(ks, decs, sols)]
            for n, cs, q, a, sol, au, ku, (Gc, bt, Gam, e, f, eL) in zip(ns, css, qs, ainvs, sols, atuw, kduw, decs):
                u_ref[0, pl.ds(cs, C), :] = sol[:, :D]
                w_ref[0, pl.ds(cs, C), :] = sol[:, D:]
                au_s[pl.ds(cs, C), :] = au[:, :D]
                q2_s[pl.ds(cs, C), :] = q * e - au[:, D:]
                bc_s[n] = ku[:, :D]
                w2_s[n] = ku[:, D:]
                el_s[n] = jnp.broadcast_to(eL, (SUBLANES, LANES))
                ai_ref[0, n] = a.T
            return c

        lax.fori_loop(0, NG, group, 0)

        def step(n, S_):
            cs = pl.multiple_of(n * C, C)
            o_ref[0, pl.ds(cs, C), :] = _mm(q2_s[pl.ds(cs, C), :], S_) + au_s[pl.ds(cs, C), :]
            st_ref[0, n] = S_
            return S_ * el_s[n, 0:1, :] + bc_s[n] - _mm(w2_s[n], S_)

        lax.fori_loop(0, NC, step, jnp.zeros((D, D), F32))

    spec = pl.BlockSpec((1, S, D), lambda h, b: (h, b, 0))
    return _call_beside(
        body, transfer, grid=(H, B), name="gdn_fwd",
        in_specs=[spec, spec, spec, pl.BlockSpec((S, LANES), lambda h, b: (b, 0))],
        out_specs=[spec, pl.BlockSpec((1, NC, D, D), lambda h, b: (h, b, 0, 0)),
                   pl.BlockSpec((1, NC, C, C), lambda h, b: (h, b, 0, 0)), spec, spec],
        out_shape=[SDS((H, B * S, D), F32), SDS((H, B * NC, D, D), F32), SDS((H, B * NC, C, C), F32),
                   SDS((H, B * S, D), F32), SDS((H, B * S, D), F32)],
        scratch_shapes=[pltpu.VMEM((S, D), F32), pltpu.VMEM((S, D), F32), pltpu.VMEM((NC, D, D), F32),
                        pltpu.VMEM((NC, D, D), F32), pltpu.VMEM((NC, SUBLANES, LANES), F32)],
        semantics=("arbitrary", "arbitrary"), args=(qg, kg, vg, gates))


def _mix_out(o_mla, o_gdn, proj, x2, mla_w, gdn_w, w_out):
    T, D = x2.shape
    tm = min(512, T)
    H = MLA_HEADS

    def body(om_ref, og_ref, z_ref, x_ref, mw_ref, gw_ref, w_ref, h_ref, mix_ref):
        z = z_ref[...]
        parts = [_rms(om_ref[h], mw_ref[h:h + 1, :])[0] for h in range(H)]
        for h in range(GDN_HEADS):
            zh = z[:, h * GDN_DIM:(h + 1) * GDN_DIM]
            parts.append(_rms(og_ref[h], gw_ref[...])[0] * (zh * _sigmoid(zh)))
        mix = jnp.concatenate(parts, axis=-1).astype(MXU_DTYPE)
        mix_ref[...] = mix
        h_ref[...] = x_ref[...] + jnp.dot(mix, w_ref[...], preferred_element_type=F32)

    hspec = pl.BlockSpec((H, tm, V_DIM), lambda i: (0, i, 0))
    return pl.pallas_call(
        body, grid=(T // tm,), name="mix_out",
        in_specs=[hspec, hspec, pl.BlockSpec((tm, GDN_WIDTH), lambda i: (i, P_GZ // GDN_WIDTH)),
                  pl.BlockSpec((tm, D), lambda i: (i, 0)),
                  pl.BlockSpec((H, V_DIM), lambda i: (0, 0)), pl.BlockSpec((1, GDN_DIM), lambda i: (0, 0)),
                  pl.BlockSpec((D, D), lambda i: (0, 0))],
        out_specs=[pl.BlockSpec((tm, D), lambda i: (i, 0)), pl.BlockSpec((tm, D), lambda i: (i, 0))],
        out_shape=[SDS((T, D), F32), SDS((T, D), MXU_DTYPE)],
        compiler_params=_params(("arbitrary",)),
    )(o_mla, o_gdn, proj, x2, mla_w, gdn_w, w_out)


def _mlp_fwd(h2, w_mn, w_up, w_down, target):
    T, D = h2.shape
    ns, _, ts = w_up.shape
    F = ns * ts
    tm = min(512, T)
    G = MLP_FWD_SHARDS
    tf, nf = G * ts, ns // G

    def body(h_ref, wn_ref, up_w, down_w, t_ref, up_ref, hn_ref, dy_ref, loss_ref, y_acc):
        j = pl.program_id(1)

        @pl.when(j == 0)
        def _():
            hn_ref[...] = _rms(h_ref[...], wn_ref[...])[0].astype(MXU_DTYPE)
            y_acc[...] = h_ref[...]

        parts = []
        for c in range(G):
            up = jnp.dot(hn_ref[...], up_w[c], preferred_element_type=F32)
            up_ref[:, c * ts:(c + 1) * ts] = up.astype(MXU_DTYPE)
            r = jnp.maximum(up, 0.0)
            parts.append(_mm(r * r, down_w[c * ts:(c + 1) * ts, :]))
        y_acc[...] += functools.reduce(jnp.add, parts)

        @pl.when(j == nf - 1)
        def _():
            err = y_acc[...] - t_ref[...]
            dy_ref[...] = err / D
            loss_ref[...] = jnp.full((1, SUBLANES, LANES), jnp.sum(err * err), F32)

    return pl.pallas_call(
        body, grid=(T // tm, nf), name="mlp_fwd",
        in_specs=[pl.BlockSpec((tm, D), lambda i, j: (i, 0)), pl.BlockSpec((1, D), lambda i, j: (0, 0)),
                  pl.BlockSpec((G, D, ts), lambda i, j: (j, 0, 0)), pl.BlockSpec((tf, D), lambda i, j: (j, 0)),
                  pl.BlockSpec((tm, D), lambda i, j: (i, 0))],
        out_specs=[pl.BlockSpec((tm, tf), lambda i, j: (i, j)), pl.BlockSpec((tm, D), lambda i, j: (i, 0)),
                   pl.BlockSpec((tm, D), lambda i, j: (i, 0)),
                   pl.BlockSpec((1, SUBLANES, LANES), lambda i, j: (i, 0, 0))],
        out_shape=[SDS((T, F), MXU_DTYPE), SDS((T, D), MXU_DTYPE), SDS((T, D), F32),
                   SDS((T // tm, SUBLANES, LANES), F32)],
        scratch_shapes=[pltpu.VMEM((tm, D), F32)],
        compiler_params=_params(("arbitrary", "arbitrary")),
    )(h2, w_mn, w_up, w_down, target)


def _mlp_bwd(dy, up, h2, w_mn, w_up, w_down):
    T, D = h2.shape
    ns, _, ts = w_up.shape
    F = ns * ts
    tm = min(512, T)
    G = MLP_BWD_SHARDS
    tf, nf = G * ts, ns // G

    def body(dy_ref, up_ref, h_ref, wn_ref, up_w, down_w, dh_ref, dhb_ref, dup_ref, act_ref, dyb_ref, dwn_ref, acc):
        i, j = pl.program_id(0), pl.program_id(1)

        @pl.when((i == 0) & (j == 0))
        def _():
            dwn_ref[...] = jnp.zeros_like(dwn_ref)

        @pl.when(j == 0)
        def _():
            acc[...] = jnp.zeros_like(acc)
            dyb_ref[...] = dy_ref[...].astype(MXU_DTYPE)

        parts = []
        for c in range(G):
            cols = slice(c * ts, (c + 1) * ts)
            r = jnp.maximum(up_ref[:, cols].astype(F32), 0.0)
            act_ref[:, cols] = (r * r).astype(MXU_DTYPE)
            dup = (_mm_nt(dyb_ref[...], down_w[cols, :]) * (2.0 * r)).astype(MXU_DTYPE)
            dup_ref[:, cols] = dup
            parts.append(_mm_nt(dup, up_w[c]))
        acc[...] += functools.reduce(jnp.add, parts)

        @pl.when(j == nf - 1)
        def _():
            hv = h_ref[...]
            _, rr = _rms(hv, wn_ref[...])
            dx, dw = _rms_bwd(acc[...], hv, wn_ref[...], rr)
            dh = dy_ref[...] + dx
            dh_ref[...] = dh
            dhb_ref[...] = dh.astype(MXU_DTYPE)
            dwn_ref[...] += dw

    row = lambda i, j: (i, 0)
    return pl.pallas_call(
        body, grid=(T // tm, nf), name="mlp_bwd",
        in_specs=[pl.BlockSpec((tm, D), row), pl.BlockSpec((tm, tf), lambda i, j: (i, j)), pl.BlockSpec((tm, D), row),
                  pl.BlockSpec((1, D), lambda i, j: (0, 0)),
                  pl.BlockSpec((G, D, ts), lambda i, j: (j, 0, 0)), pl.BlockSpec((tf, D), lambda i, j: (j, 0))],
        out_specs=[pl.BlockSpec((tm, D), row), pl.BlockSpec((tm, D), row),
                   pl.BlockSpec((tm, tf), lambda i, j: (i, j)), pl.BlockSpec((tm, tf), lambda i, j: (i, j)),
                   pl.BlockSpec((tm, D), row), pl.BlockSpec((1, D), lambda i, j: (0, 0))],
        out_shape=[SDS((T, D), F32), SDS((T, D), MXU_DTYPE), SDS((T, F), MXU_DTYPE), SDS((T, F), MXU_DTYPE),
                   SDS((T, D), MXU_DTYPE), SDS((1, D), F32)],
        scratch_shapes=[pltpu.VMEM((tm, D), F32)],
        compiler_params=_params(("arbitrary", "arbitrary")),
    )(dy, up, h2, w_mn, w_up, w_down)


def _mix_bwd(dhb, o_mla, o_gdn, proj, mla_w, gdn_w, w_out):
    T, D = dhb.shape
    tm = min(512, T)
    H = MLA_HEADS

    def body(dh_ref, om_ref, og_ref, z_ref, mw_ref, gw_ref, w_ref, dom_ref, dog_ref, dz_ref, dmw_ref, dgw_ref):
        @pl.when(pl.program_id(0) == 0)
        def _():
            dmw_ref[...] = jnp.zeros_like(dmw_ref)
            dgw_ref[...] = jnp.zeros_like(dgw_ref)

        dmix = _mm_nt(dh_ref[...], w_ref[...])
        z = z_ref[...]
        dmw, dzs = [], []
        dgw = jnp.zeros((1, GDN_DIM), F32)
        for h in range(H):
            o = om_ref[h]
            w = mw_ref[h:h + 1, :]
            _, r = _rms(o, w)
            dx, dw = _rms_bwd(dmix[:, h * V_DIM:(h + 1) * V_DIM], o, w, r)
            dom_ref[h] = dx
            dmw.append(dw)
        for h in range(GDN_HEADS):
            o = og_ref[h]
            w = gw_ref[...]
            zh = z[:, h * GDN_DIM:(h + 1) * GDN_DIM]
            sg = _sigmoid(zh)
            yn, r = _rms(o, w)
            dy = dmix[:, H * V_DIM + h * GDN_DIM:H * V_DIM + (h + 1) * GDN_DIM]
            dzs.append(dy * yn * (sg * (1.0 + zh * (1.0 - sg))))
            dx, dw = _rms_bwd(dy * (zh * sg), o, w, r)
            dog_ref[h] = dx
            dgw = dgw + dw
        dz_ref[...] = jnp.concatenate(dzs, axis=-1).astype(MXU_DTYPE)
        dmw_ref[...] += jnp.concatenate(dmw, axis=0)
        dgw_ref[...] += dgw

    hspec = pl.BlockSpec((H, tm, V_DIM), lambda i: (0, i, 0))
    return pl.pallas_call(
        body, grid=(T // tm,), name="mix_bwd",
        in_specs=[pl.BlockSpec((tm, D), lambda i: (i, 0)), hspec, hspec,
                  pl.BlockSpec((tm, GDN_WIDTH), lambda i: (i, P_GZ // GDN_WIDTH)),
                  pl.BlockSpec((H, V_DIM), lambda i: (0, 0)), pl.BlockSpec((1, GDN_DIM), lambda i: (0, 0)),
                  pl.BlockSpec((D, D), lambda i: (0, 0))],
        out_specs=[hspec, hspec, pl.BlockSpec((tm, GDN_WIDTH), lambda i: (i, 0)),
                   pl.BlockSpec((H, V_DIM), lambda i: (0, 0)), pl.BlockSpec((1, GDN_DIM), lambda i: (0, 0))],
        out_shape=[SDS((H, T, V_DIM), F32), SDS((H, T, GDN_DIM), F32), SDS((T, GDN_WIDTH), MXU_DTYPE),
                   SDS((H, V_DIM), F32), SDS((1, GDN_DIM), F32)],
        compiler_params=_params(("arbitrary",)),
    )(dhb, o_mla, o_gdn, proj, mla_w, gdn_w, w_out)


def _attn_bwd(q4, k4, v4, do4, o4, lse4, B, S, transfer=None):
    H = MLA_HEADS
    bq = min(ATTN_BLOCK, S)
    nq = S // bq
    rows = bq // ATTN_CHAINS

    def body(q_ref, k_ref, v_ref, do_ref, o_ref, lse_ref, dq_ref, dk_ref, dv_ref, delta):
        dq_ref[...] = jnp.zeros_like(dq_ref)
        dk_ref[...] = jnp.zeros_like(dk_ref)
        dv_ref[...] = jnp.zeros_like(dv_ref)
        delta[...] = jnp.sum(do_ref[0] * o_ref[0], axis=-1, keepdims=True)

        col = lax.broadcasted_iota(jnp.int32, (rows, bq), 1)
        row = lax.broadcasted_iota(jnp.int32, (rows, bq), 0)

        def k_step(kj, carry):
            ks = pl.multiple_of(kj * bq, bq)
            k = k_ref[0, pl.ds(ks, bq), :]
            v = v_ref[0, pl.ds(ks, bq), :]

            def q_block(qs, diagonal):
                dks, dvs = [None] * ATTN_CHAINS, [None] * ATTN_CHAINS

                def chain(j):
                    sl = pl.ds(qs + j * rows, rows)
                    q = q_ref[0, sl, :]
                    do = do_ref[0, sl, :].astype(MXU_DTYPE)
                    s = _mm_nt(q, k)
                    dp = _mm_nt(do, v)
                    yield
                    p = jnp.exp(s - lse_ref[0, sl, :])
                    if diagonal:
                        p = jnp.where(col <= row + j * rows, p, 0.0)
                    ds = p * (dp - delta[sl, :])
                    yield
                    dvs[j] = _mm_tn(p, do)
                    dks[j] = _mm_tn(ds, q)
                    dq_ref[0, sl, :] += _mm(ds, k)

                _lockstep([chain(j) for j in range(ATTN_CHAINS)])
                dv_ref[0, pl.ds(ks, bq), :] += functools.reduce(jnp.add, dvs)
                dk_ref[0, pl.ds(ks, bq), :] += functools.reduce(jnp.add, dks)

            q_block(ks, True)

            def q_step(qi, c):
                q_block(pl.multiple_of(qi * bq, bq), False)
                return c

            lax.fori_loop(kj + 1, nq, q_step, 0)
            return carry

        lax.fori_loop(0, nq, k_step, 0)

    spec = lambda d: pl.BlockSpec((1, S, d), lambda h, b: (h, b, 0))
    return _call_beside(
        body, transfer, grid=(H, B), name="attn_bwd",
        in_specs=[spec(QK_DIM), spec(QK_DIM), spec(V_DIM), spec(V_DIM), spec(V_DIM), spec(1)],
        out_specs=[spec(QK_DIM), spec(QK_DIM), spec(V_DIM)],
        out_shape=[SDS((H, B * S, QK_DIM), F32), SDS((H, B * S, QK_DIM), F32), SDS((H, B * S, V_DIM), F32)],
        scratch_shapes=[pltpu.VMEM((S, 1), F32)], semantics=("arbitrary", "arbitrary"),
        args=(q4, k4, v4, do4, o4, lse4))


def _gdn_bwd(qg, kg, vg, gates, states, ainv, u4, w4, do4, B, S, transfer=None):
    H, D, C = GDN_HEADS, GDN_DIM, CHUNK
    NC = S // C
    U = GDN_BWD_UNROLL if NC % GDN_BWD_UNROLL == 0 else 1
    NG = NC // U

    def body(q_ref, k_ref, v_ref, g_ref, st_ref, ai_ref, u_ref, w_ref, do_ref, dq_ref, dk_ref, dv_ref, dgb_ref,
             kd_s, x1_s, x2_s, el_s, dvn_s, ds_s, w2t_s):
        h = pl.program_id(0)
        lane = lax.broadcasted_iota(jnp.int32, (C, LANES), 1)
        ri = lax.broadcasted_iota(jnp.int32, (C, C), 0)
        ci = lax.broadcasted_iota(jnp.int32, (C, C), 1)
        rcol = lax.broadcasted_iota(jnp.int32, (C, 1), 0)

        def rsum(a):
            return jnp.sum(a, axis=-1, keepdims=True)

        def prepare(n):
            cs = n * C
            q = q_ref[0, pl.ds(cs, C), :]
            k = k_ref[0, pl.ds(cs, C), :]
            do = do_ref[0, pl.ds(cs, C), :]
            Gc, bt, Gam, e, f, eL = _chunk_decays(g_ref[pl.ds(cs, C), :], lane, h, ri, ci, rcol)
            At = _mm_nt(q, k) * Gam
            yield
            x1 = _mm_tn(At, do)
            x2 = _mm_tn(q * e, do)
            kd = k * f
            w = w_ref[0, pl.ds(cs, C), :]
            yield
            x1_s[pl.ds(cs, C), :] = x1
            x2_s[n] = x2 - _mm_tn(w, x1)
            w2t_s[n] = _mm_tn(w, kd)
            kd_s[pl.ds(cs, C), :] = kd
            el_s[n] = jnp.broadcast_to(eL, (SUBLANES, LANES))

        def recur(n, dS):
            cs = n * C
            ds_s[n] = dS
            dvn_s[pl.ds(cs, C), :] = x1_s[pl.ds(cs, C), :] + _mm(kd_s[pl.ds(cs, C), :], dS)
            return x2_s[n] + el_s[n, 0:1, :] * dS - _mm(w2t_s[n], dS)

        def local(n):
            cs = n * C
            q = q_ref[0, pl.ds(cs, C), :]
            k = k_ref[0, pl.ds(cs, C), :]
            v = v_ref[0, pl.ds(cs, C), :]
            do = do_ref[0, pl.ds(cs, C), :]
            u = u_ref[0, pl.ds(cs, C), :]
            w = w_ref[0, pl.ds(cs, C), :]
            dvn = dvn_s[pl.ds(cs, C), :]
            dS = ds_s[n]
            Gc, bt, Gam, e, f, eL = _chunk_decays(g_ref[pl.ds(cs, C), :], lane, h, ri, ci, rcol)
            S0 = st_ref[0, n]
            AinvT = ai_ref[0, n]
            qk = _mm_nt(jnp.concatenate([q, k], axis=0), k)
            QK, KK = qk[:C], qk[C:]
            be = bt * e
            sol = jnp.concatenate([u, w], axis=-1)
            vn = u - _mm(w, S0)
            yield
            dAt = jnp.where(ri >= ci, _mm_nt(do, vn), 0.0)
            dqd = _mm_nt(do, S0)
            dw = -_mm_nt(dvn, S0)
            dkd = _mm_nt(vn, dS)
            deL = jnp.sum(rsum(dS * S0), axis=0, keepdims=True)
            yield
            dR = _mm_exact(AinvT, jnp.concatenate([dvn, dw], axis=-1))
            dR1, dR2 = dR[:, :D], dR[:, D:]
            yield
            dL = jnp.where(ri > ci, -_mm_nt(dR, sol), 0.0)
            yield
            dv_ref[0, pl.ds(cs, C), :] = dR1 * bt
            r2 = rsum(dR2 * k)
            X = dL * Gam
            dbt = rsum(dR1 * v) + r2 * e + rsum(X * KK)
            de = r2 * bt + rsum(dqd * q)
            dKK = X * bt
            dQK = dAt * Gam
            dq_ref[0, pl.ds(cs, C), :] = _mm(dQK, k) + dqd * e
            dk_ref[0, pl.ds(cs, C), :] = dR2 * be + _mm(dKK + dKK.T, k) + _mm_tn(dQK, q) + dkd * f
            df = rsum(dkd * k)
            Z = (dL * (bt * KK) + dAt * QK) * Gam
            dG = rsum(Z) - rsum(Z.T) + de * e - df * f
            dGl = jnp.sum(df * f, axis=0, keepdims=True) + deL * eL
            dG = dG + jnp.where(rcol == C - 1, dGl, 0.0)
            dgb_ref[0, pl.ds(cs, C), :] = jnp.where(lane == 0, dG, jnp.where(lane == 1, dbt, 0.0))

        state = [jnp.zeros((D, D), F32)]

        def recur_group(g):
            for j, n in enumerate(reversed(range(g * U, (g + 1) * U))):
                state[0] = recur(n, state[0])
                if j % GDN_RECUR_STEPS_PER_STAGE == GDN_RECUR_STEPS_PER_STAGE - 1:
                    yield

        def stage(fn, g):
            return _together([fn(g * U + j) for j in range(U)])

        for step in range(NG + 2):
            jobs = [(stage, prepare, NG - 1 - step), (None, None, NG - step), (stage, local, NG + 1 - step)]
            _lockstep([recur_group(g) if make is None else make(fn, g) for make, fn, g in jobs if 0 <= g < NG])

    spec = pl.BlockSpec((1, S, D), lambda h, b: (h, b, 0))
    return _call_beside(
        body, transfer, grid=(H, B), name="gdn_bwd",
        in_specs=[spec, spec, spec, pl.BlockSpec((S, LANES), lambda h, b: (b, 0)),
                  pl.BlockSpec((1, NC, D, D), lambda h, b: (h, b, 0, 0)),
                  pl.BlockSpec((1, NC, C, C), lambda h, b: (h, b, 0, 0)), spec, spec, spec],
        out_specs=[spec, spec, spec, spec],
        out_shape=[SDS((H, B * S, D), F32)] * 4,
        scratch_shapes=[pltpu.VMEM((S, D), F32), pltpu.VMEM((S, D), F32), pltpu.VMEM((NC, D, D), F32),
                        pltpu.VMEM((NC, SUBLANES, LANES), F32), pltpu.VMEM((S, D), F32),
                        pltpu.VMEM((NC, D, D), F32), pltpu.VMEM((NC, D, D), F32)],
        semantics=("arbitrary", "arbitrary"), args=(qg, kg, vg, gates, states, ainv, u4, w4, do4))


def _gdn_pre_bwd(proj, conv_w, alog_l, dt_l, dq4, dk4, dv4, dgb4, S):
    T = proj.shape[0]
    tm = min(256, T)
    tiles_per_seq = S // tm
    C3 = 3 * GDN_WIDTH
    H = GDN_HEADS

    def body(u_ref, halo_ref, gab_ref, w_ref, alog_ref, dt_ref, dq_ref, dk_ref, dv_ref, dgb_ref,
             dc_ref, dgab_ref, dcw_ref, dalog_ref, ddt_ref):
        i = pl.program_id(0)

        @pl.when(i == 0)
        def _():
            dcw_ref[...] = jnp.zeros_like(dcw_ref)
            dalog_ref[...] = jnp.zeros_like(dalog_ref)
            ddt_ref[...] = jnp.zeros_like(ddt_ref)

        halo = jnp.where(i % tiles_per_seq == 0, 0.0, halo_ref[...])
        c, sh = _conv_taps(u_ref[...], halo, w_ref[...])
        sg = _sigmoid(c)
        a = c * sg
        das = [None] * (3 * H)
        for h in range(H):
            xq = a[:, h * GDN_DIM:(h + 1) * GDN_DIM]
            xk = a[:, GDN_WIDTH + h * GDN_DIM:GDN_WIDTH + (h + 1) * GDN_DIM]
            das[h] = _l2n_bwd(dq_ref[h], xq, GDN_QSCALE)
            das[H + h] = _l2n_bwd(dk_ref[h], xk, 1.0)
            das[2 * H + h] = dv_ref[h]
        dc = jnp.concatenate(das, axis=-1) * (sg * (1.0 + c * (1.0 - sg)))
        dc_ref[...] = dc
        dcw_ref[...] += jnp.concatenate(
            [jnp.sum(dc * sh[CONV_W - 1 - t], axis=0, keepdims=True) for t in range(CONV_W)], axis=0)
        lane = lax.broadcasted_iota(jnp.int32, (tm, LANES), 1)
        ric = lax.broadcasted_iota(jnp.int32, (tm, LANES), 0) % CHUNK
        dG = jnp.zeros((tm, LANES), F32)
        for h in range(H):
            t = dgb_ref[h]
            dG = dG + jnp.where(lane == h, _pick_lane(t, lane, 0), 0.0) \
                    + jnp.where(lane == h + H, _pick_lane(t, lane, 1), 0.0)
        is_g = lane < H
        dg = jnp.where(is_g, _chunk_rev_cumsum(jnp.where(is_g, dG, 0.0), ric), 0.0)
        gab = gab_ref[...]
        g, beta = _gate_values(gab, alog_ref[...], dt_ref[...], lane)
        dga = jnp.where(is_g, dg * (-jnp.exp(alog_ref[...])) * _sigmoid(gab + dt_ref[...]), 0.0)
        dgb = jnp.where(is_g, 0.0, dG) * beta * (1.0 - beta)
        dgab_ref[...] = (dga + dgb).astype(MXU_DTYPE)
        dalog_ref[...] += jnp.sum(dg * g, axis=0, keepdims=True)
        ddt_ref[...] += jnp.sum(dga, axis=0, keepdims=True)

    hspec = pl.BlockSpec((H, tm, GDN_DIM), lambda i: (0, i, 0))
    vec = pl.BlockSpec((1, LANES), lambda i: (0, 0))
    return pl.pallas_call(
        body, grid=(T // tm,), name="gdn_pre_bwd",
        in_specs=[pl.BlockSpec((tm, C3), lambda i: (i, 0)),
                  pl.BlockSpec((SUBLANES, C3), lambda i: (jnp.maximum(i * (tm // SUBLANES) - 1, 0), 0)),
                  pl.BlockSpec((tm, LANES), lambda i: (i, P_GAB // LANES)),
                  pl.BlockSpec((CONV_W, C3), lambda i: (0, 0)), vec, vec, hspec, hspec, hspec, hspec],
        out_specs=[pl.BlockSpec((tm, C3), lambda i: (i, 0)), pl.BlockSpec((tm, LANES), lambda i: (i, 0)),
                   pl.BlockSpec((CONV_W, C3), lambda i: (0, 0)), vec, vec],
        out_shape=[SDS((T, C3), F32), SDS((T, LANES), MXU_DTYPE), SDS((CONV_W, C3), F32),
                   SDS((1, LANES), F32), SDS((1, LANES), F32)],
        compiler_params=_params(("arbitrary",)),
    )(proj, proj, proj, conv_w, alog_l, dt_l, dq4, dk4, dv4, dgb4)


def _mla_pre_bwd(proj, cosf, sinf, w_qln, w_kvln, w_uq_p, w_ukv, qnw, knw, dq4, dk4, dv4, transfer=None):
    T = proj.shape[0]
    tm = min(256, T)
    H = MLA_HEADS

    def body(ql_ref, kvl_ref, kpe_ref, cos_ref, sin_ref, wq_ref, wkv_ref, uq_ref, ukv_ref, qnw_ref, knw_ref,
             dq_ref, dk_ref, dv_ref,
             dql_ref, dkvl_ref, dkpe_ref, dqraw_ref, dkvraw_ref, qn_ref, kvn_ref, dwq_ref, dwkv_ref, dqnw_ref, dknw_ref):
        @pl.when(pl.program_id(0) == 0)
        def _():
            for r in (dwq_ref, dwkv_ref, dqnw_ref, dknw_ref):
                r[...] = jnp.zeros_like(r)

        cos, sin = cos_ref[...], sin_ref[...]
        qnw_, knw_ = qnw_ref[...], knw_ref[...]
        ql, kvl = ql_ref[...], kvl_ref[...]
        kpe_raw = kpe_ref[...][:, :ROPE]
        rms = functools.partial(_rms, on_mxu=True)
        rms_bwd = functools.partial(_rms_bwd, on_mxu=True)
        qn, rq = rms(ql, wq_ref[...])
        kvn, rkv = rms(kvl, wkv_ref[...])
        qn_ref[...] = qn.astype(MXU_DTYPE)
        kvn_ref[...] = kvn.astype(MXU_DTYPE)
        qraw = _mm(qn, uq_ref[...])
        kvraw = _mm(kvn, ukv_ref[...])
        dq_nope, dq_pe, dkv_parts = [], [], []
        dqnw_n = jnp.zeros((1, NOPE), F32)
        dqnw_p = jnp.zeros((1, ROPE), F32)
        dknw_n = jnp.zeros((1, NOPE), F32)
        dkpe = jnp.zeros((tm, ROPE), F32)
        for h in range(H):
            dq = dq_ref[h] * ATT_SCALE
            x = qraw[:, h * NOPE:(h + 1) * NOPE]
            dx, dw = rms_bwd(dq[:, :NOPE], x, qnw_[:, :NOPE], rms(x, qnw_[:, :NOPE])[1])
            dq_nope.append(dx)
            dqnw_n = dqnw_n + dw
            x = qraw[:, H * NOPE + h * ROPE:H * NOPE + (h + 1) * ROPE]
            dx, dw = rms_bwd(_rope_bwd(dq[:, NOPE:], cos, sin), x, qnw_[:, NOPE:], rms(x, qnw_[:, NOPE:])[1])
            dq_pe.append(dx)
            dqnw_p = dqnw_p + dw
            dk = dk_ref[h]
            x = kvraw[:, h * 256:h * 256 + NOPE]
            dx, dw = rms_bwd(dk[:, :NOPE], x, knw_[:, :NOPE], rms(x, knw_[:, :NOPE])[1])
            dknw_n = dknw_n + dw
            dkpe = dkpe + dk[:, NOPE:]
            dkv_parts += [dx, dv_ref[h]]
        dx, dknw_p = rms_bwd(_rope_bwd(dkpe, cos, sin), kpe_raw, knw_[:, NOPE:], rms(kpe_raw, knw_[:, NOPE:])[1])
        dkpe_ref[...] = jnp.concatenate([dx, jnp.zeros((tm, LANES - ROPE), F32)], axis=-1).astype(MXU_DTYPE)
        dqraw = jnp.concatenate(dq_nope + dq_pe, axis=-1).astype(MXU_DTYPE)
        dkvraw = jnp.concatenate(dkv_parts, axis=-1).astype(MXU_DTYPE)
        dqraw_ref[...] = dqraw
        dkvraw_ref[...] = dkvraw
        dx, dw = rms_bwd(_mm_nt(dqraw, uq_ref[...]), ql, wq_ref[...], rq)
        dql_ref[...] = dx.astype(MXU_DTYPE)
        dwq_ref[...] += dw
        dx, dw = rms_bwd(_mm_nt(dkvraw, ukv_ref[...]), kvl, wkv_ref[...], rkv)
        dkvl_ref[...] = dx.astype(MXU_DTYPE)
        dwkv_ref[...] += dw
        dqnw_ref[...] += jnp.concatenate([dqnw_n, dqnw_p], axis=-1)
        dknw_ref[...] += jnp.concatenate([dknw_n, dknw_p], axis=-1)

    full = lambda a: pl.BlockSpec(a.shape, lambda i: (0,) * a.ndim)
    rows = lambda n: pl.BlockSpec((tm, n), lambda i: (i, 0))
    const = lambda n: pl.BlockSpec((1, n), lambda i: (0, 0))
    NQ, NKV = w_uq_p.shape[1], w_ukv.shape[1]
    return _call_beside(
        body, transfer, grid=(T // tm,), name="mla_pre_bwd", scratch_shapes=[], semantics=("arbitrary",),
        args=(proj, proj, proj, cosf, sinf, w_qln, w_kvln, w_uq_p, w_ukv, qnw, knw, dq4, dk4, dv4),
        in_specs=[pl.BlockSpec((tm, 256), lambda i: (i, P_QLAT // 256)),
                  pl.BlockSpec((tm, 256), lambda i: (i, P_KVLAT // 256)),
                  pl.BlockSpec((tm, 128), lambda i: (i, P_KPE // 128)),
                  rows(ROPE), rows(ROPE),
                  full(w_qln), full(w_kvln), full(w_uq_p), full(w_ukv), full(qnw), full(knw),
                  pl.BlockSpec((H, tm, QK_DIM), lambda i: (0, i, 0)),
                  pl.BlockSpec((H, tm, QK_DIM), lambda i: (0, i, 0)),
                  pl.BlockSpec((H, tm, V_DIM), lambda i: (0, i, 0))],
        out_specs=[rows(Q_LORA), rows(KV_LORA), rows(LANES), rows(NQ), rows(NKV), rows(Q_LORA), rows(KV_LORA),
                   const(Q_LORA), const(KV_LORA), const(QK_DIM), const(QK_DIM)],
        out_shape=[SDS((T, Q_LORA), MXU_DTYPE), SDS((T, KV_LORA), MXU_DTYPE), SDS((T, LANES), MXU_DTYPE),
                   SDS((T, NQ), MXU_DTYPE), SDS((T, NKV), MXU_DTYPE),
                   SDS((T, Q_LORA), MXU_DTYPE), SDS((T, KV_LORA), MXU_DTYPE),
                   SDS((1, Q_LORA), F32), SDS((1, KV_LORA), F32), SDS((1, QK_DIM), F32), SDS((1, QK_DIM), F32)])


def _in_proj_bwd(dc, conv_w, dgz, dql, dkvl, dkpe, dgab, w_in_p, dh, x2, w_an, S):
    T, D = x2.shape
    N = w_in_p.shape[1]
    C3 = dc.shape[1]
    tm = min(512, S)
    assert S % tm == 0 and T % tm == 0, "a token tile must not straddle two sequences"
    tiles_per_seq = S // tm
    nblk = T // SUBLANES

    def body(dc_ref, nxt_ref, cw_ref, b_ref, c_ref, d_ref, e_ref, f_ref, w_ref, dh_ref, x_ref, wn_ref,
             dx_ref, dp_ref, dwn_ref):
        i = pl.program_id(0)

        @pl.when(i == 0)
        def _():
            dwn_ref[...] = jnp.zeros_like(dwn_ref)

        nxt = jnp.where(i % tiles_per_seq == tiles_per_seq - 1, 0.0, nxt_ref[...])
        dcv, cw = dc_ref[...], cw_ref[...]
        du = cw[3:4] * dcv
        for j in range(1, CONV_W):
            du = du + cw[3 - j:4 - j] * _shift_up(dcv, nxt, j)
        dp = jnp.concatenate([du.astype(MXU_DTYPE), b_ref[...], c_ref[...], d_ref[...], e_ref[...], f_ref[...]],
                             axis=-1).astype(MXU_DTYPE)
        dp_ref[...] = dp
        x = x_ref[...]
        _, r = _rms(x, wn_ref[...])
        dx, dw = _rms_bwd(_mm_nt(dp, w_ref[...]), x, wn_ref[...], r)
        dx_ref[...] = dh_ref[...] + dx
        dwn_ref[...] += dw

    rows = lambda n: pl.BlockSpec((tm, n), lambda i: (i, 0))
    return pl.pallas_call(
        body, grid=(T // tm,), name="in_proj_bwd",
        in_specs=[rows(C3),
                  pl.BlockSpec((SUBLANES, C3), lambda i: (jnp.minimum((i + 1) * (tm // SUBLANES), nblk - 1), 0)),
                  pl.BlockSpec((CONV_W, C3), lambda i: (0, 0)),
                  rows(dgz.shape[1]), rows(dql.shape[1]), rows(dkvl.shape[1]),
                  rows(dkpe.shape[1]), rows(dgab.shape[1]),
                  pl.BlockSpec((D, N), lambda i: (0, 0)), rows(D), rows(D), pl.BlockSpec((1, D), lambda i: (0, 0))],
        out_specs=[rows(D), rows(N), pl.BlockSpec((1, D), lambda i: (0, 0))],
        out_shape=[SDS((T, D), F32), SDS((T, N), MXU_DTYPE), SDS((1, D), F32)],
        compiler_params=_params(("arbitrary",)),
    )(dc, dc, conv_w, dgz, dql, dkvl, dkpe, dgab, w_in_p, dh, x2, w_an)


def _wgrad(a, b, name, column_shards=False):
    T, M = a.shape
    N = b.shape[1]
    tM = _divisor_tile(M, 1024)
    tN = N // N_DEV if column_shards else _divisor_tile(N, 1536)
    tk = min(T, 2048)
    nk = T // tk

    def body(a_ref, b_ref, o_ref, acc):
        k = pl.program_id(2)

        @pl.when(k == 0)
        def _():
            acc[...] = jnp.zeros_like(acc)

        acc[...] += _mm_tn(a_ref[...], b_ref[...])

        @pl.when(k == nk - 1)
        def _():
            o_ref[...] = acc[...].astype(WIRE_DTYPE).reshape(o_ref.shape)

    if column_shards:
        out_spec, out_shape = pl.BlockSpec((1, tM, tN), lambda i, j, k: (j, i, 0)), SDS((N_DEV, M, tN), WIRE_DTYPE)
    else:
        out_spec, out_shape = pl.BlockSpec((tM, tN), lambda i, j, k: (i, j)), SDS((M, N), WIRE_DTYPE)
    return pl.pallas_call(
        body, grid=(M // tM, N // tN, nk), name=name,
        in_specs=[pl.BlockSpec((tk, tM), lambda i, j, k: (k, i)), pl.BlockSpec((tk, tN), lambda i, j, k: (k, j))],
        out_specs=out_spec, out_shape=out_shape,
        scratch_shapes=[pltpu.VMEM((tM, tN), F32)],
        compiler_params=_params(("arbitrary", "arbitrary", "arbitrary")),
    )(a, b)


def _adamw(g, w, m, v):
    m = ADAM_B1 * m + (1.0 - ADAM_B1) * g
    v = ADAM_B2 * v + (1.0 - ADAM_B2) * jnp.square(g)
    m_hat = m / (1.0 - ADAM_B1 ** ADAM_STEP)
    v_hat = v / (1.0 - ADAM_B2 ** ADAM_STEP)
    return -ADAM_LR * (m_hat / (jnp.sqrt(v_hat) + ADAM_EPS) + ADAM_WD * w), m, v


def _reduce_adamw(parts, w, m, v, name):
    R, C = w.shape
    _, Rp, Cp = parts.shape
    tr = min(R, 256)
    tp = tr if Rp == R else Rp

    def body(p_ref, w_ref, m_ref, v_ref, g_ref, d_ref, nm_ref, nv_ref):
        g = p_ref[0].astype(F32)
        for s in range(1, N_DEV):
            g = g + p_ref[s].astype(F32)
        g = g[:tr, :C]
        g_ref[...] = g
        d_ref[...], nm_ref[...], nv_ref[...] = _adamw(g, w_ref[...], m_ref[...], v_ref[...])

    spec = pl.BlockSpec((tr, C), lambda i: (i, 0))
    return pl.pallas_call(
        body, grid=(R // tr,), name=name,
        in_specs=[pl.BlockSpec((N_DEV, tp, Cp), lambda i: (0, i, 0)), spec, spec, spec],
        out_specs=[spec] * 4, out_shape=[SDS((R, C), F32)] * 4,
        compiler_params=_params(("arbitrary",)),
    )(parts, w, m, v)


SMALL_ROWS, SMALL_COLS = 16, 1024
SMALL_LAYOUT = (
    ("attn_norm_w", 0, 1, 1024, 1024), ("mlp_norm_w", 1, 1, 1024, 1024), ("q_lat_norm_w", 2, 1, 256, 256),
    ("kv_lat_norm_w", 3, 1, 256, 256), ("q_norm_w", 4, 1, 192, 192), ("k_norm_w", 5, 1, 192, 192),
    ("mla_out_norm_w", 6, 4, 128, 128), ("a_log", 10, 1, 128, 4), ("dt_bias", 11, 1, 128, 4),
    ("gdn_norm_w", 12, 1, 128, 128))
LOSS_ENTRY = ("loss", 13, 1, 128, 128)


def _adamw_replicated(parts, ws, ms, vs):
    n = len(SMALL_LAYOUT)

    def body(*refs):
        p_ref = refs[0]
        w_refs, m_refs, v_refs = refs[1:1 + n], refs[1 + n:1 + 2 * n], refs[1 + 2 * n:1 + 3 * n]
        outs = refs[1 + 3 * n:]
        s = p_ref[0]
        for d in range(1, N_DEV):
            s = s + p_ref[d]
        for i, (_, r0, nr, _, pw) in enumerate(SMALL_LAYOUT):
            g = s[r0:r0 + nr, :pw]
            outs[i][...] = g
            outs[n + i][...], outs[2 * n + i][...], outs[3 * n + i][...] = _adamw(
                g, w_refs[i][...], m_refs[i][...], v_refs[i][...])
        _, r0, nr, gw, _ = LOSS_ENTRY
        outs[4 * n][...] = s[r0:r0 + nr, :gw]

    res = pl.pallas_call(
        body, name="adamw_replicated",
        out_shape=[SDS(w.shape, F32) for w in ws] * 4 + [SDS((1, LANES), F32)],
        compiler_params=_params(),
    )(parts, *ws, *ms, *vs)
    return [res[k * n:(k + 1) * n] for k in range(4)], res[4 * n][0, 0]


COPIES_PER_ARRAY = N_DEV - 1


def _two_level_gather(srcs, outs, send_sems, recv_sems, local_sems=None, stage="all"):
    mx, my, mc = lax.axis_index("x"), lax.axis_index("y"), lax.axis_index("c")
    me, sibling = (mx, my, mc), (mx, my, 1 - mc)
    chips = [(1 - mx, my), (mx, 1 - my), (1 - mx, 1 - my)]
    arrays = range(len(srcs))

    def copy(a, k, block, to, src=None):
        px, py, pc = block
        slot = outs[a].at[4 * px + 2 * py + pc]
        sem = a * COPIES_PER_ARRAY + k
        return pltpu.make_async_remote_copy(
            src_ref=slot if src is None else src, dst_ref=slot,
            send_sem=send_sems.at[sem], recv_sem=recv_sems.at[sem], device_id=to, device_id_type=MESH_ID)

    mine = [] if local_sems is None else [
        pltpu.make_async_copy(srcs[a], outs[a].at[4 * mx + 2 * my + mc], local_sems.at[a]) for a in arrays]
    first = []
    for a in arrays:
        first.append(copy(a, 0, me, sibling, src=srcs[a]))
        first += [copy(a, 1 + j, me, (*chip, mc), src=srcs[a]) for j, chip in enumerate(chips)]
    if stage in ("all", "start"):
        for cp in mine + first:
            cp.start()
    if stage in ("all", "finish"):
        forwards = []
        for j, chip in enumerate(chips):
            for a in arrays:
                copy(a, 1 + j, (*chip, mc), me).wait_recv()
                fwd = copy(a, 4 + j, (*chip, mc), sibling)
                fwd.start()
                forwards.append(fwd)
        for a in arrays:
            copy(a, 0, sibling, me).wait_recv()
        for j, chip in enumerate(chips):
            for a in arrays:
                copy(a, 4 + j, (*chip, 1 - mc), me).wait_recv()
        for cp in first + forwards:
            cp.wait_send()
        for cp in mine:
            cp.wait()


def _comm_scratch(n):
    return [pltpu.SemaphoreType.DMA((n * COPIES_PER_ARRAY,)), pltpu.SemaphoreType.DMA((n * COPIES_PER_ARRAY,)),
            pltpu.SemaphoreType.DMA((n,))]


def _any_specs(n):
    return [pl.BlockSpec(memory_space=pl.ANY)] * n


def _gather_weights(shards):
    n = len(shards)

    def body(*refs):
        _two_level_gather(refs[:n], refs[n:2 * n], *refs[2 * n:])

    return pl.pallas_call(
        body, name="gather_weights",
        out_shape=[SDS((N_DEV,) + s.shape, s.dtype) for s in shards],
        in_specs=_any_specs(n), out_specs=_any_specs(n), scratch_shapes=_comm_scratch(n),
    )(*shards)


def _gather_small_grads(gs, loss_lanes):
    gs = list(gs) + [loss_lanes]
    n = len(gs)

    def body(*refs):
        g_refs, out_ref = refs[:n], refs[n]
        tile, send_sems, recv_sems = refs[n + 1:]
        tile[...] = jnp.zeros_like(tile)
        for (_, r0, nr, gw, _), g in zip(SMALL_LAYOUT + (LOSS_ENTRY,), g_refs):
            tile[r0:r0 + nr, 0:gw] = g[...]
        me = 4 * lax.axis_index("x") + 2 * lax.axis_index("y") + lax.axis_index("c")
        out_ref[me] = tile[...]
        _two_level_gather([tile], [out_ref], send_sems, recv_sems)

    return pl.pallas_call(
        body, name="gather_small_grads",
        out_shape=SDS((N_DEV, SMALL_ROWS, SMALL_COLS), F32),
        in_specs=[pl.BlockSpec(memory_space=pltpu.VMEM)] * n,
        out_specs=pl.BlockSpec(memory_space=pltpu.VMEM),
        scratch_shapes=[pltpu.VMEM((SMALL_ROWS, SMALL_COLS), F32),
                        pltpu.SemaphoreType.DMA((COPIES_PER_ARRAY,)), pltpu.SemaphoreType.DMA((COPIES_PER_ARRAY,))],
    )(*gs)


def _exchange_grads(slabs):
    n = len(slabs)

    def body(*refs):
        _exchange(refs[:n], refs[n:2 * n], *refs[2 * n:])

    return pl.pallas_call(
        body, name="exchange_grads",
        out_shape=[SDS(s.shape, s.dtype) for s in slabs],
        in_specs=_any_specs(n), out_specs=_any_specs(n), scratch_shapes=_comm_scratch(n),
    )(*slabs)


class _Transfer:
    def __init__(self, kind, arrays):
        self.kind, self.arrays, self.n = kind, list(arrays), len(arrays)

    def out_shapes(self):
        if self.kind == "gather":
            return [SDS((N_DEV,) + a.shape, a.dtype) for a in self.arrays]
        return [SDS(a.shape, a.dtype) for a in self.arrays]

    def run(self, srcs, outs, sems, stage):
        fn = _two_level_gather if self.kind == "gather" else _exchange
        fn(srcs, outs, *sems, stage=stage)


def _call_beside(body, transfer, *, grid, in_specs, out_specs, out_shape, scratch_shapes, name, semantics, args):
    if transfer is None:
        res = pl.pallas_call(body, grid=grid, in_specs=in_specs, out_specs=out_specs, out_shape=out_shape,
                             scratch_shapes=scratch_shapes, name=name, compiler_params=_params(semantics))(*args)
        return list(res), []
    n_in, n_out, n_s, n = len(in_specs), len(out_specs), len(scratch_shapes), transfer.n

    def wrapped(*refs):
        ins, refs = refs[:n_in], refs[n_in:]
        t_in, refs = refs[:n], refs[n:]
        outs, refs = refs[:n_out], refs[n_out:]
        t_out, refs = refs[:n], refs[n:]
        scratch, sems = refs[:n_s], refs[n_s:]
        first = functools.reduce(jnp.logical_and, [pl.program_id(i) == 0 for i in range(len(grid))])
        last = functools.reduce(jnp.logical_and, [pl.program_id(i) == g - 1 for i, g in enumerate(grid)])

        @pl.when(first)
        def _():
            transfer.run(t_in, t_out, sems, "start")

        body(*ins, *outs, *scratch)

        @pl.when(last)
        def _():
            transfer.run(t_in, t_out, sems, "finish")

    res = pl.pallas_call(
        wrapped, grid=grid, in_specs=list(in_specs) + _any_specs(n), out_specs=list(out_specs) + _any_specs(n),
        out_shape=list(out_shape) + transfer.out_shapes(), scratch_shapes=list(scratch_shapes) + _comm_scratch(n),
        name=name, compiler_params=_params(semantics))(*args, *transfer.arrays)
    return list(res[:n_out]), list(res[n_out:])


EXCHANGE_FLIPS = ((0, 0, 1), (1, 0, 0), (0, 1, 0), (1, 1, 0), (1, 0, 1), (0, 1, 1), (1, 1, 1))


def _exchange(srcs, outs, send_sems, recv_sems, local_sems, stage="all"):
    mx, my, mc = lax.axis_index("x"), lax.axis_index("y"), lax.axis_index("c")
    arrays = range(len(srcs))
    copies = [pltpu.make_async_copy(srcs[a].at[4 * mx + 2 * my + mc], outs[a].at[N_DEV - 1], local_sems.at[a])
              for a in arrays]
    for k, (fx, fy, fc) in enumerate(EXCHANGE_FLIPS):
        px = 1 - mx if fx else mx
        py = 1 - my if fy else my
        pc = 1 - mc if fc else mc
        for a in arrays:
            sem = a * COPIES_PER_ARRAY + k
            copies.append(pltpu.make_async_remote_copy(
                src_ref=srcs[a].at[4 * px + 2 * py + pc], dst_ref=outs[a].at[k],
                send_sem=send_sems.at[sem], recv_sem=recv_sems.at[sem],
                device_id=(px, py, pc), device_id_type=MESH_ID))
    if stage in ("all", "start"):
        for cp in copies:
            cp.start()
    if stage in ("all", "finish"):
        for cp in copies:
            cp.wait()


def _w_in_to_padded(w):
    z = lambda n: jnp.zeros((w.shape[0], n), w.dtype)
    return jnp.concatenate([w[:, O_GQKV:O_GZ], w[:, O_GZ:O_GAB], w[:, O_QLAT:O_KVLAT], w[:, O_KVLAT:O_KPE],
                            w[:, O_KPE:O_GQKV], z(P_GAB - P_KPE - ROPE), w[:, O_GAB:O_END],
                            z(P_WIDTH - P_GAB - (O_END - O_GAB))], axis=1)


def _w_in_from_padded(wp):
    return jnp.concatenate([wp[:, P_QLAT:P_QLAT + 256], wp[:, P_KVLAT:P_KVLAT + 256], wp[:, P_KPE:P_KPE + ROPE],
                            wp[:, P_GQKV:P_GZ], wp[:, P_GZ:P_QLAT], wp[:, P_GAB:P_GAB + (O_END - O_GAB)]], axis=1)


W_IN_SHARD_COLS = (O_END - O_QLAT) // N_DEV


def _w_in_shards_to_padded(stack):
    _, R, Cw = stack.shape
    tr = min(R, 256)

    def body(s_ref, o_ref):
        full = jnp.concatenate([s_ref[d].astype(F32)[:, :W_IN_SHARD_COLS] for d in range(N_DEV)], axis=-1)
        o_ref[...] = _w_in_to_padded(full).astype(o_ref.dtype)

    return pl.pallas_call(
        body, grid=(R // tr,), name="w_in_to_padded",
        in_specs=[pl.BlockSpec((N_DEV, tr, Cw), lambda i: (0, i, 0))],
        out_specs=pl.BlockSpec((tr, P_WIDTH), lambda i: (i, 0)),
        out_shape=SDS((R, P_WIDTH), stack.dtype), compiler_params=_params(("arbitrary",)),
    )(stack)


def _w_in_padded_to_slabs(gp, wire_cols):
    R = gp.shape[0]
    tr = min(R, 256)

    def body(g_ref, o_ref):
        orig = _w_in_from_padded(g_ref[...].astype(F32))
        for d in range(N_DEV):
            piece = orig[:, d * W_IN_SHARD_COLS:(d + 1) * W_IN_SHARD_COLS]
            o_ref[d] = _pad2(piece, tr, wire_cols).astype(o_ref.dtype)

    return pl.pallas_call(
        body, grid=(R // tr,), name="w_in_to_slabs",
        in_specs=[pl.BlockSpec((tr, P_WIDTH), lambda i: (i, 0))],
        out_specs=pl.BlockSpec((N_DEV, tr, wire_cols), lambda i: (0, i, 0)),
        out_shape=SDS((N_DEV, R, wire_cols), gp.dtype), compiler_params=_params(("arbitrary",)),
    )(gp)


def _w_uq_to_headsplit(w):
    w3 = w.reshape(w.shape[0], MLA_HEADS, QK_DIM)
    return jnp.concatenate([w3[:, :, :NOPE].reshape(w.shape[0], -1), w3[:, :, NOPE:].reshape(w.shape[0], -1)], axis=1)


def _w_uq_from_headsplit(wp):
    n = wp[:, :MLA_HEADS * NOPE].reshape(wp.shape[0], MLA_HEADS, NOPE)
    p = wp[:, MLA_HEADS * NOPE:].reshape(wp.shape[0], MLA_HEADS, ROPE)
    return jnp.concatenate([n, p], axis=2).reshape(wp.shape[0], -1)


def _lane_vec(v4):
    return jnp.pad(v4.reshape(1, -1), ((0, 0), (0, LANES - v4.shape[-1])))


def _local_step(x, positions, target, attn_norm_w, w_in, q_lat_norm_w, w_uq, kv_lat_norm_w, w_ukv, q_norm_w,
                k_norm_w, mla_out_norm_w, conv_w, a_log, dt_bias, gdn_norm_w, w_out, mlp_norm_w, w_up, w_down,
                late_shards=None, exchange=False):
    B, S, D = x.shape
    T = B * S
    x2 = x.reshape(T, D)
    t2 = target.reshape(T, D)
    half = ROPE // 2
    inv_freq = ROPE_THETA ** (-jnp.arange(half, dtype=F32) / half)
    ang = positions.reshape(T, 1).astype(F32) * inv_freq
    cosf = jnp.concatenate([jnp.cos(ang)] * 2, axis=-1)
    sinf = jnp.concatenate([jnp.sin(ang)] * 2, axis=-1)
    w_in_p = w_in
    w_uq_p = _w_uq_to_headsplit(w_uq)
    alog_l, dt_l = _lane_vec(a_log), _lane_vec(dt_bias)
    w_an, w_qln, w_kvln, qnw, knw, w_mn, gdn_w = (
        attn_norm_w, q_lat_norm_w, kv_lat_norm_w, q_norm_w, k_norm_w, mlp_norm_w, gdn_norm_w)

    proj, xn = _in_proj(x2, w_an, w_in_p)
    gather = None if late_shards is None else _Transfer("gather", late_shards[:1])
    (q4, k4, v4), late = _mla_pre(proj, cosf, sinf, w_qln, w_kvln, w_uq_p, w_ukv, qnw, knw, gather)
    if late:
        w_out = late[0].reshape(-1, D)
    (o_mla, lse), _ = _attn_fwd(q4, k4, v4, B, S)
    qg, kg, vg, gates = _gdn_pre(proj, conv_w, alog_l, dt_l, S)
    gather = None if late_shards is None else _Transfer("gather", late_shards[1:])
    (o_gdn, states, ainv, u4, w4), late = _gdn_fwd(qg, kg, vg, gates, B, S, gather)
    if late:
        w_up, w_down = late[0], late[1].reshape(-1, D)
    h2, mix = _mix_out(o_mla, o_gdn, proj, x2, mla_out_norm_w, gdn_w, w_out)
    up, hn, dy, sq = _mlp_fwd(h2, w_mn, w_up, w_down, t2)
    loss = (0.5 / D) * jnp.sum(sq[:, 0, 0])

    dh, dhb, dup, act, dyb, d_mlp_norm = _mlp_bwd(dy, up, h2, w_mn, w_up, w_down)
    g_w_down = _wgrad(act, dyb, "wgrad_down")
    g_w_up = _wgrad(hn, dup, "wgrad_up", column_shards=True)
    do_mla, do_gdn, dz, d_mla_w, d_gdn_w = _mix_bwd(dhb, o_mla, o_gdn, proj, mla_out_norm_w, gdn_w, w_out)
    g_w_out = _wgrad(mix, dhb, "wgrad_out")
    first = ("w_down",)
    second = ("w_out",)
    third = ("w_up", "w_uq", "w_ukv")
    mats = dict(w_up=g_w_up, w_down=g_w_down, w_out=g_w_out)

    def sending(names):
        return _Transfer("exchange", [_slabs(n, mats[n]) for n in names]) if exchange else None

    (dq4, dk4, dv4), got = _attn_bwd(q4, k4, v4, do_mla, o_mla, lse, B, S, sending(first))
    mats.update(zip(first, got))
    (dql, dkvl, dkpe, dqraw, dkvraw, qn, kvn, d_wqln, d_wkvln, d_qnw, d_knw), got = _mla_pre_bwd(
        proj, cosf, sinf, w_qln, w_kvln, w_uq_p, w_ukv, qnw, knw, dq4, dk4, dv4, sending(second))
    mats.update(zip(second, got))
    mats.update(w_uq=_wgrad(qn, dqraw, "wgrad_uq"), w_ukv=_wgrad(kvn, dkvraw, "wgrad_ukv"))
    (dqg, dkg, dvg, dgb4), got = _gdn_bwd(qg, kg, vg, gates, states, ainv, u4, w4, do_gdn, B, S, sending(third))
    mats.update(zip(third, got))
    dc, dgab, g_conv, d_alog, d_dt = _gdn_pre_bwd(proj, conv_w, alog_l, dt_l, dqg, dkg, dvg, dgb4, S)
    grad_x2, dproj, d_attn_norm = _in_proj_bwd(dc, conv_w, dz, dql, dkvl, dkpe, dgab, w_in_p, dh, x2, w_an, S)
    mats.update(w_in=_wgrad(xn, dproj, "wgrad_in"), conv_w=g_conv)
    if exchange:
        last = ("w_in", "conv_w")
        mats.update(zip(last, _exchange_grads([_slabs(n, mats[n]) for n in last])))
    small = dict(attn_norm_w=d_attn_norm, mlp_norm_w=d_mlp_norm, q_lat_norm_w=d_wqln, kv_lat_norm_w=d_wkvln,
                 q_norm_w=d_qnw, k_norm_w=d_knw, mla_out_norm_w=d_mla_w, a_log=d_alog, dt_bias=d_dt,
                 gdn_norm_w=d_gdn_w)
    return loss, grad_x2.reshape(B, S, D), mats, [small[n] for n, *_ in SMALL_LAYOUT]


BIG = ("w_in", "w_uq", "w_ukv", "conv_w", "w_out", "w_up", "w_down")
ALL_W = ("attn_norm_w", "w_in", "q_lat_norm_w", "w_uq", "kv_lat_norm_w", "w_ukv", "q_norm_w", "k_norm_w",
         "mla_out_norm_w", "conv_w", "a_log", "dt_bias", "gdn_norm_w", "w_out", "mlp_norm_w", "w_up", "w_down")
WIRE_SHAPE = {"w_in": (1024, 384), "w_uq": (256, 128), "conv_w": (16, 256)}


def _pad2(a, rows, cols):
    return jnp.pad(a, [(0, 0)] * (a.ndim - 2) + [(0, rows - a.shape[-2]), (0, cols - a.shape[-1])])


def _cols_to_full(stack, cols):
    return jnp.moveaxis(stack[:, :, :cols], 0, 1).reshape(stack.shape[1], N_DEV * cols)


def _full_to_cols(full, wire_cols):
    r, n = full.shape
    return _pad2(jnp.moveaxis(full.reshape(r, N_DEV, n // N_DEV), 1, 0), r, wire_cols)


def _slabs(name, g):
    if name == "w_in":
        return _w_in_padded_to_slabs(g, WIRE_SHAPE["w_in"][1])
    if name == "w_uq":
        return _full_to_cols(_w_uq_from_headsplit(g), WIRE_SHAPE["w_uq"][1])
    if name == "w_ukv":
        return _full_to_cols(g, g.shape[1] // N_DEV)
    if name == "conv_w":
        return _pad2(_full_to_cols(g.astype(WIRE_DTYPE), g.shape[1] // N_DEV), *WIRE_SHAPE["conv_w"])
    if name == "w_up":
        return g
    return g.reshape(N_DEV, -1, g.shape[-1])


def kernel(x, positions, attn_norm_w, w_in, q_lat_norm_w, w_uq, kv_lat_norm_w, w_ukv, q_norm_w, k_norm_w, mla_out_norm_w, conv_w, a_log, dt_bias, gdn_norm_w, w_out, mlp_norm_w, w_up, w_down, loss_target, m_attn_norm_w, m_w_in, m_q_lat_norm_w, m_w_uq, m_kv_lat_norm_w, m_w_ukv, m_q_norm_w, m_k_norm_w, m_mla_out_norm_w, m_conv_w, m_a_log, m_dt_bias, m_gdn_norm_w, m_w_out, m_mlp_norm_w, m_w_up, m_w_down, v_attn_norm_w, v_w_in, v_q_lat_norm_w, v_w_uq, v_kv_lat_norm_w, v_w_ukv, v_q_norm_w, v_k_norm_w, v_mla_out_norm_w, v_conv_w, v_a_log, v_dt_bias, v_gdn_norm_w, v_w_out, v_mlp_norm_w, v_w_up, v_w_down):
    env = dict(locals())
    W = {n: env[n][0] for n in ALL_W}
    Mo = {n: env["m_" + n][0] for n in ALL_W}
    Vo = {n: env["v_" + n][0] for n in ALL_W}

    two_d = lambda a: a.reshape(1, -1) if a.ndim == 1 else a
    D = x.shape[-1]

    s_in, s_uq, s_ukv, s_conv = _gather_weights([
        _pad2(W["w_in"].astype(WIRE_DTYPE), *WIRE_SHAPE["w_in"]),
        _pad2(W["w_uq"].astype(WIRE_DTYPE), *WIRE_SHAPE["w_uq"]),
        W["w_ukv"].astype(WIRE_DTYPE), _pad2(W["conv_w"], *WIRE_SHAPE["conv_w"])])
    late = [W["w_out"].astype(WIRE_DTYPE), W["w_up"].astype(WIRE_DTYPE), W["w_down"].astype(WIRE_DTYPE)]

    loss, grad_x, parts, gs = _local_step(
        x, positions, loss_target, two_d(W["attn_norm_w"]), _w_in_shards_to_padded(s_in),
        two_d(W["q_lat_norm_w"]), _cols_to_full(s_uq, W["w_uq"].shape[1]), two_d(W["kv_lat_norm_w"]),
        _cols_to_full(s_ukv, W["w_ukv"].shape[1]), two_d(W["q_norm_w"]), two_d(W["k_norm_w"]),
        W["mla_out_norm_w"], _cols_to_full(s_conv[:, :CONV_W], W["conv_w"].shape[1]), two_d(W["a_log"]),
        two_d(W["dt_bias"]), two_d(W["gdn_norm_w"]), None, two_d(W["mlp_norm_w"]), None, None,
        late_shards=late, exchange=True)
    done = {n: _reduce_adamw(parts[n], W[n], Mo[n], Vo[n], "adamw_" + n) for n in BIG}
    names = [n for n, *_ in SMALL_LAYOUT]
    tiles = _gather_small_grads(gs, jnp.full((1, LANES), loss, F32))
    small, loss = _adamw_replicated(tiles, [two_d(W[n]) for n in names], [two_d(Mo[n]) for n in names],
                                    [two_d(Vo[n]) for n in names])
    for i, n in enumerate(names):
        done[n] = [small[kind][i] for kind in range(4)]
    res = [done[n][kind].reshape(env[n].shape) for kind in range(4) for n in ALL_W]
    return (loss, grad_x, *res)
```

```python
import functools

import jax
import jax.numpy as jnp
from jax import lax
from jax.experimental import pallas as pl
from jax.experimental.pallas import tpu as pltpu

F32 = jnp.float32
MXU_DTYPE = jnp.bfloat16
WIRE_DTYPE = jnp.bfloat16
SDS = jax.ShapeDtypeStruct
HIGHEST = lax.Precision.HIGHEST
MESH_ID = pl.DeviceIdType.MESH

D_MODEL = 1024
MLA_HEADS = 4
Q_LORA = 256
KV_LORA = 256
NOPE = 128
ROPE = 64
QK_DIM = NOPE + ROPE
V_DIM = 128
ROPE_THETA = 10000.0
GDN_HEADS = 4
GDN_DIM = 128
GDN_WIDTH = GDN_HEADS * GDN_DIM
CONV_W = 4
CHUNK = 64
D_FF = 4 * D_MODEL
EPS = 1e-6
ATT_SCALE = QK_DIM ** -0.5
GDN_QSCALE = GDN_DIM ** -0.5
N_DEV = 8
ATTN_BLOCK = 512
ATTN_CHAINS = 2
MLP_FWD_SHARDS = 4
MLP_BWD_SHARDS = 4

ADAM_LR = 0.001
ADAM_B1 = 0.9
ADAM_B2 = 0.999
ADAM_EPS = 1e-08
ADAM_WD = 0.01
ADAM_STEP = 10

LANES = 128
SUBLANES = 8
VMEM_LIMIT = 60 * 1024 * 1024

P_GQKV, P_GZ, P_QLAT, P_KVLAT, P_KPE, P_GAB = 0, 1536, 2048, 2304, 2560, 2688
P_WIDTH = 2816
O_QLAT, O_KVLAT, O_KPE, O_GQKV, O_GZ, O_GAB, O_END = 0, 256, 512, 576, 2112, 2624, 2632


def _params(sem=None, vmem=VMEM_LIMIT):
    kw = dict(vmem_limit_bytes=vmem)
    if sem is not None:
        kw["dimension_semantics"] = sem
    return pltpu.CompilerParams(**kw)


def _mm(a, b):
    return jnp.dot(a.astype(MXU_DTYPE), b.astype(MXU_DTYPE), preferred_element_type=F32)


def _mm_nt(a, b):
    return lax.dot_general(a.astype(MXU_DTYPE), b.astype(MXU_DTYPE), (((1,), (1,)), ((), ())),
                           preferred_element_type=F32)


def _mm_tn(a, b):
    return lax.dot_general(a.astype(MXU_DTYPE), b.astype(MXU_DTYPE), (((0,), (0,)), ((), ())),
                           preferred_element_type=F32)


def _split(a):
    hi = a.astype(MXU_DTYPE)
    return hi, (a - hi.astype(F32)).astype(MXU_DTYPE)


def _mm_split(a, b):
    (ah, al), (bh, bl) = a, b
    dot = lambda x, y: jnp.dot(x, y, preferred_element_type=F32)
    if MXU_DTYPE == F32:
        return dot(ah, bh)
    return dot(ah, bh) + dot(ah, bl) + dot(al, bh)


def _mm_exact(a, b):
    return _mm_split(_split(a), _split(b))


def _row_sum(v, on_mxu=False):
    if not on_mxu:
        return jnp.sum(v, axis=-1, keepdims=True)
    d = v.shape[-1]
    ones = jnp.ones((d, LANES), MXU_DTYPE)
    s = sum(jnp.dot(p, ones, preferred_element_type=F32) for p in _split(v))
    return s[:, :d] if d <= LANES else jnp.tile(s, (1, d // LANES))


def _rms(x, w, on_mxu=False):
    r = lax.rsqrt(_row_sum(x * x, on_mxu) * (1.0 / x.shape[-1]) + EPS)
    return x * r * w, r


def _rms_bwd(dy, x, w, r, on_mxu=False):
    xh = x * r
    dyw = dy * w
    dx = r * (dyw - xh * (_row_sum(dyw * xh, on_mxu) * (1.0 / x.shape[-1])))
    dw = jnp.sum(dy * xh, axis=0, keepdims=True)
    return dx, dw


def _l2n(x, scale):
    return x * (lax.rsqrt(_row_sum(x * x) + EPS) * scale)


def _l2n_bwd(dy, x, scale):
    r = lax.rsqrt(_row_sum(x * x) + EPS)
    xh = x * r
    return (scale * r) * (dy - xh * _row_sum(dy * xh))


def _rot(t):
    return jnp.concatenate([-t[:, ROPE // 2:], t[:, :ROPE // 2]], axis=-1)


def _rot_t(t):
    return jnp.concatenate([t[:, ROPE // 2:], -t[:, :ROPE // 2]], axis=-1)


def _rope(t, cos, sin):
    return t * cos + _rot(t) * sin


def _rope_bwd(d, cos, sin):
    return d * cos + _rot_t(d * sin)


def _sigmoid(x):
    return jax.nn.sigmoid(x)


def _shift_down(x, halo, j):
    if j == 0:
        return x
    xr = pltpu.roll(x, j, 0)
    hr = pltpu.roll(halo, j, 0)
    row = lax.broadcasted_iota(jnp.int32, halo.shape, 0)
    top = jnp.where(row < j, hr, xr[:SUBLANES])
    return jnp.concatenate([top, xr[SUBLANES:]], axis=0)


def _shift_up(x, nxt, j):
    if j == 0:
        return x
    n = x.shape[0]
    xr = pltpu.roll(x, n - j, 0)
    nr = pltpu.roll(nxt, SUBLANES - j, 0)
    row = lax.broadcasted_iota(jnp.int32, nxt.shape, 0)
    bot = jnp.where(row >= SUBLANES - j, nr, xr[n - SUBLANES:])
    return jnp.concatenate([xr[:n - SUBLANES], bot], axis=0)


def _chunk_cumsum(y, row_in_chunk):
    s = 1
    while s < CHUNK:
        y = y + jnp.where(row_in_chunk >= s, pltpu.roll(y, s, 0), 0.0)
        s *= 2
    return y


def _chunk_rev_cumsum(y, row_in_chunk):
    n = y.shape[0]
    s = 1
    while s < CHUNK:
        y = y + jnp.where(row_in_chunk + s < CHUNK, pltpu.roll(y, n - s, 0), 0.0)
        s *= 2
    return y


def _together(generators):
    alive = list(generators)
    while alive:
        nxt = []
        for g in alive:
            try:
                next(g)
                nxt.append(g)
            except StopIteration:
                pass
        alive = nxt
        yield


def _lockstep(generators):
    for _ in _together(generators):
        pass


def _pick_lane(tile, lane, idx):
    return jnp.sum(jnp.where(lane == idx, tile, 0.0), axis=-1, keepdims=True)


def _divisor_tile(n, cap, unit=LANES):
    best = unit
    t = unit
    while t <= min(n, cap):
        if n % t == 0:
            best = t
        t += unit
    return n if n <= cap else best


def _in_proj(x2, w_an, w_in_p):
    T, D = x2.shape
    N = w_in_p.shape[1]
    tm = min(512, T)

    def body(x_ref, wn_ref, w_ref, proj_ref, xn_ref):
        xn, _ = _rms(x_ref[...], wn_ref[...])
        xn = xn.astype(MXU_DTYPE)
        xn_ref[...] = xn
        proj_ref[...] = jnp.dot(xn, w_ref[...], preferred_element_type=F32)

    return pl.pallas_call(
        body, grid=(T // tm,), name="in_proj",
        in_specs=[pl.BlockSpec((tm, D), lambda i: (i, 0)), pl.BlockSpec((1, D), lambda i: (0, 0)),
                  pl.BlockSpec((D, N), lambda i: (0, 0))],
        out_specs=[pl.BlockSpec((tm, N), lambda i: (i, 0)), pl.BlockSpec((tm, D), lambda i: (i, 0))],
        out_shape=[SDS((T, N), F32), SDS((T, D), MXU_DTYPE)],
        compiler_params=_params(("arbitrary",)),
    )(x2, w_an, w_in_p)


def _mla_pre(proj, cosf, sinf, w_qln, w_kvln, w_uq_p, w_ukv, qnw, knw, transfer=None):
    T = proj.shape[0]
    tm = min(256, T)
    H = MLA_HEADS

    def body(ql_ref, kvl_ref, kpe_ref, cos_ref, sin_ref, wq_ref, wkv_ref, uq_ref, ukv_ref, qnw_ref, knw_ref,
             q_out, k_out, v_out):
        rms = functools.partial(_rms, on_mxu=True)
        cos, sin = cos_ref[...], sin_ref[...]
        qnw_, knw_ = qnw_ref[...], knw_ref[...]
        qn, _ = rms(ql_ref[...], wq_ref[...])
        kvn, _ = rms(kvl_ref[...], wkv_ref[...])
        qraw = _mm(qn, uq_ref[...])
        kvraw = _mm(kvn, ukv_ref[...])
        kpe = _rope(rms(kpe_ref[...][:, :ROPE], knw_[:, NOPE:])[0], cos, sin)
        for h in range(H):
            qn_h = rms(qraw[:, h * NOPE:(h + 1) * NOPE], qnw_[:, :NOPE])[0]
            qp_h = _rope(rms(qraw[:, H * NOPE + h * ROPE:H * NOPE + (h + 1) * ROPE], qnw_[:, NOPE:])[0], cos, sin)
            q_out[h] = (jnp.concatenate([qn_h, qp_h], axis=-1) * ATT_SCALE).astype(MXU_DTYPE)
            kn_h = rms(kvraw[:, h * 256:h * 256 + NOPE], knw_[:, :NOPE])[0]
            k_out[h] = jnp.concatenate([kn_h, kpe], axis=-1).astype(MXU_DTYPE)
            v_out[h] = kvraw[:, h * 256 + NOPE:(h + 1) * 256].astype(MXU_DTYPE)

    full = lambda a: pl.BlockSpec(a.shape, lambda i: (0,) * a.ndim)
    return _call_beside(
        body, transfer, grid=(T // tm,), name="mla_pre", scratch_shapes=[], semantics=("arbitrary",),
        args=(proj, proj, proj, cosf, sinf, w_qln, w_kvln, w_uq_p, w_ukv, qnw, knw),
        in_specs=[pl.BlockSpec((tm, 256), lambda i: (i, P_QLAT // 256)),
                  pl.BlockSpec((tm, 256), lambda i: (i, P_KVLAT // 256)),
                  pl.BlockSpec((tm, 128), lambda i: (i, P_KPE // 128)),
                  pl.BlockSpec((tm, ROPE), lambda i: (i, 0)), pl.BlockSpec((tm, ROPE), lambda i: (i, 0)),
                  full(w_qln), full(w_kvln), full(w_uq_p), full(w_ukv), full(qnw), full(knw)],
        out_specs=[pl.BlockSpec((H, tm, QK_DIM), lambda i: (0, i, 0)),
                   pl.BlockSpec((H, tm, QK_DIM), lambda i: (0, i, 0)),
                   pl.BlockSpec((H, tm, V_DIM), lambda i: (0, i, 0))],
        out_shape=[SDS((H, T, QK_DIM), MXU_DTYPE), SDS((H, T, QK_DIM), MXU_DTYPE), SDS((H, T, V_DIM), MXU_DTYPE)])


def _attn_fwd(q4, k4, v4, B, S, transfer=None):
    H = MLA_HEADS
    bq = min(ATTN_BLOCK, S)
    nq = S // bq
    rows = bq // ATTN_CHAINS

    def body(q_ref, k_ref, v_ref, o_ref, lse_ref):
        col = lax.broadcasted_iota(jnp.int32, (rows, bq), 1)
        row = lax.broadcasted_iota(jnp.int32, (rows, bq), 0)

        def q_step(qi, carry):
            qs = pl.multiple_of(qi * bq, bq)
            qsub = [q_ref[0, pl.ds(qs + j * rows, rows), :] for j in range(ATTN_CHAINS)]

            def k_block(ks, cs, diagonal):
                k = k_ref[0, pl.ds(ks, bq), :]
                v = v_ref[0, pl.ds(ks, bq), :]
                out = [None] * ATTN_CHAINS

                def chain(j):
                    m, l, acc = cs[j]
                    s = _mm_nt(qsub[j], k)
                    yield
                    if diagonal:
                        s = jnp.where(col <= row + j * rows, s, -jnp.inf)
                    m_new = jnp.maximum(m, jnp.max(s, axis=-1, keepdims=True))
                    p = jnp.exp(s - m_new)
                    a = jnp.exp(m - m_new)
                    l_new = a * l + jnp.sum(p, axis=-1, keepdims=True)
                    yield
                    out[j] = (m_new, l_new, a * acc + _mm(p, v))

                _lockstep([chain(j) for j in range(ATTN_CHAINS)])
                return tuple(out)

            init = tuple((jnp.full((rows, 1), -jnp.inf, F32), jnp.zeros((rows, 1), F32),
                          jnp.zeros((rows, V_DIM), F32)) for _ in range(ATTN_CHAINS))
            cs = lax.fori_loop(0, qi, lambda kj, c: k_block(pl.multiple_of(kj * bq, bq), c, False), init)
            for j, (m, l, acc) in enumerate(k_block(qs, cs, True)):
                o_ref[0, pl.ds(qs + j * rows, rows), :] = acc / l
                lse_ref[0, pl.ds(qs + j * rows, rows), :] = m + jnp.log(l)
            return carry

        lax.fori_loop(0, nq, q_step, 0)

    spec = lambda d: pl.BlockSpec((1, S, d), lambda h, b: (h, b, 0))
    return _call_beside(
        body, transfer, grid=(H, B), name="attn_fwd",
        in_specs=[spec(QK_DIM), spec(QK_DIM), spec(V_DIM)],
        out_specs=[spec(V_DIM), spec(1)],
        out_shape=[SDS((H, B * S, V_DIM), F32), SDS((H, B * S, 1), F32)],
        scratch_shapes=[], semantics=("arbitrary", "arbitrary"), args=(q4, k4, v4))


def _conv_taps(u, halo, w):
    sh = [_shift_down(u, halo, j) for j in range(CONV_W)]
    c = w[0:1] * sh[3] + w[1:2] * sh[2] + w[2:3] * sh[1] + w[3:4] * sh[0]
    return c, sh


def _gate_values(gab, alog_l, dt_l, lane):
    g = -jnp.exp(alog_l) * jax.nn.softplus(gab + dt_l)
    g = jnp.where(lane < GDN_HEADS, g, 0.0)
    beta = jnp.where((lane >= GDN_HEADS) & (lane < 2 * GDN_HEADS), _sigmoid(gab), 0.0)
    return g, beta


def _gdn_pre(proj, conv_w, alog_l, dt_l, S):
    T = proj.shape[0]
    tm = min(256, T)
    tiles_per_seq = S // tm
    C3 = 3 * GDN_WIDTH
    H = GDN_HEADS

    def body(u_ref, halo_ref, gab_ref, w_ref, alog_ref, dt_ref, q_out, k_out, v_out, gates_out):
        i = pl.program_id(0)
        halo = jnp.where(i % tiles_per_seq == 0, 0.0, halo_ref[...])
        c, _ = _conv_taps(u_ref[...], halo, w_ref[...])
        a = c * _sigmoid(c)
        for h in range(H):
            xq = a[:, h * GDN_DIM:(h + 1) * GDN_DIM]
            xk = a[:, GDN_WIDTH + h * GDN_DIM:GDN_WIDTH + (h + 1) * GDN_DIM]
            q_out[h] = _l2n(xq, GDN_QSCALE)
            k_out[h] = _l2n(xk, 1.0)
            v_out[h] = a[:, 2 * GDN_WIDTH + h * GDN_DIM:2 * GDN_WIDTH + (h + 1) * GDN_DIM]
        lane = lax.broadcasted_iota(jnp.int32, (tm, LANES), 1)
        ric = lax.broadcasted_iota(jnp.int32, (tm, LANES), 0) % CHUNK
        g, beta = _gate_values(gab_ref[...], alog_ref[...], dt_ref[...], lane)
        gates_out[...] = _chunk_cumsum(g, ric) + beta

    hspec = pl.BlockSpec((H, tm, GDN_DIM), lambda i: (0, i, 0))
    return pl.pallas_call(
        body, grid=(T // tm,), name="gdn_pre",
        in_specs=[pl.BlockSpec((tm, C3), lambda i: (i, 0)),
                  pl.BlockSpec((SUBLANES, C3), lambda i: (jnp.maximum(i * (tm // SUBLANES) - 1, 0), 0)),
                  pl.BlockSpec((tm, LANES), lambda i: (i, P_GAB // LANES)),
                  pl.BlockSpec((CONV_W, C3), lambda i: (0, 0)),
                  pl.BlockSpec((1, LANES), lambda i: (0, 0)), pl.BlockSpec((1, LANES), lambda i: (0, 0))],
        out_specs=[hspec, hspec, hspec, pl.BlockSpec((tm, LANES), lambda i: (i, 0))],
        out_shape=[SDS((H, T, GDN_DIM), F32)] * 3 + [SDS((T, LANES), F32)],
        compiler_params=_params(("arbitrary",)),
    )(proj, proj, proj, conv_w, alog_l, dt_l)


def _unit_lower_inverses(Ls, eye):
    Ps = [eye - L for L in Ls]
    Ms = [_split(-L) for L in Ls]
    for _ in range(5):
        sq = [_mm_split(m, m) for m in Ms]
        Ms = [_split(s) for s in sq]
        Ps = [p + _mm_split(_split(p), m) for p, m in zip(Ps, Ms)]
    return Ps


def _chunk_decays(gt, lane, h, ri, ci, rcol):
    Gc = _pick_lane(gt, lane, h)
    bt = _pick_lane(gt, lane, h + GDN_HEADS)
    Gb = jnp.broadcast_to(Gc, (CHUNK, CHUNK))
    Gam = jnp.where(ri >= ci, jnp.exp(Gb - Gb.T), 0.0)
    Gl = jnp.sum(jnp.where(rcol == CHUNK - 1, Gc, 0.0), axis=0, keepdims=True)
    return Gc, bt, Gam, jnp.exp(Gc), jnp.exp(Gl - Gc), jnp.exp(Gl)


GDN_FWD_UNROLL = 16
GDN_BWD_UNROLL = 8
GDN_RECUR_STEPS_PER_STAGE = 2


def _gdn_fwd(qg, kg, vg, gates, B, S, transfer=None):
    H, D, C = GDN_HEADS, GDN_DIM, CHUNK
    NC = S // C
    P = 2 if B % 2 == 0 else 1
    Sb, NCb = P * S, P * NC
    U = GDN_FWD_UNROLL if NCb % GDN_FWD_UNROLL == 0 else 1
    NG = NCb // U

    def body(q_ref, k_ref, v_ref, g_ref, o_ref, st_ref, ai_ref, u_ref, w_ref, q2_s, au_s, bc_s, w2_s, el_s):
        h = pl.program_id(0)
        lane = lax.broadcasted_iota(jnp.int32, (C, LANES), 1)
        ri = lax.broadcasted_iota(jnp.int32, (C, C), 0)
        ci = lax.broadcasted_iota(jnp.int32, (C, C), 1)
        rcol = lax.broadcasted_iota(jnp.int32, (C, 1), 0)
        eye = (ri == ci).astype(F32)

        def group(gi, c):
            ns = [gi * U + j for j in range(U)]
            css = [pl.multiple_of(n * C, C) for n in ns]
            qs = [q_ref[0, pl.ds(cs, C), :] for cs in css]
            ks = [k_ref[0, pl.ds(cs, C), :] for cs in css]
            vs = [v_ref[0, pl.ds(cs, C), :] for cs in css]
            decs = [_chunk_decays(g_ref[pl.ds(cs, C), :], lane, h, ri, ci, rcol) for cs in css]
            qks = [_mm_nt(jnp.concatenate([q, k], axis=0), k) for q, k in zip(qs, ks)]
            ainvs = _unit_lower_inverses(
                [jnp.where(ri > ci, d[1] * qk[C:] * d[2], 0.0) for qk, d in zip(qks, decs)], eye)
            sols = [_mm_exact(a, jnp.concatenate([v * d[1], k * (d[1] * d[3])], axis=-1))
                    for a, k, v, d in zip(ainvs, ks, vs, decs)]
            atuw = [_mm(qk[:C] * d[2], sol) for qk, d, sol in zip(qks, decs, sols)]
            kduw = [_mm_tn(k * d[4], sol) for k, d, sol in zip(ks, decs, sols)]
            for n, cs, q, a, sol, au, ku, (Gc, bt, Gam, e, f, eL) in zip(ns, css, qs, ainvs, sols, atuw, kduw, decs):
                u_ref[0, pl.ds(cs, C), :] = sol[:, :D]
                w_ref[0, pl.ds(cs, C), :] = sol[:, D:]
                au_s[pl.ds(cs, C), :] = au[:, :D]
                q2_s[pl.ds(cs, C), :] = q * e - au[:, D:]
                bc_s[n] = ku[:, :D]
                w2_s[n] = ku[:, D:]
                el_s[n] = jnp.broadcast_to(eL, (SUBLANES, LANES))
                ai_ref[0, n] = a.T
            return c

        lax.fori_loop(0, NG, group, 0)

        def step(n, states):
            new = []
            for p, S_ in enumerate(states):
                m = p * NC + n
                cs = pl.multiple_of(m * C, C)
                o_ref[0, pl.ds(cs, C), :] = _mm(q2_s[pl.ds(cs, C), :], S_) + au_s[pl.ds(cs, C), :]
                st_ref[0, m] = S_
                new.append(S_ * el_s[m, 0:1, :] + bc_s[m] - _mm(w2_s[m], S_))
            return tuple(new)

        lax.fori_loop(0, NC, step, tuple(jnp.zeros((D, D), F32) for _ in range(P)))

    spec = pl.BlockSpec((1, Sb, D), lambda h, b: (h, b, 0))
    return _call_beside(
        body, transfer, grid=(H, B // P), name="gdn_fwd",
        in_specs=[spec, spec, spec, pl.BlockSpec((Sb, LANES), lambda h, b: (b, 0))],
        out_specs=[spec, pl.BlockSpec((1, NCb, D, D), lambda h, b: (h, b, 0, 0)),
                   pl.BlockSpec((1, NCb, C, C), lambda h, b: (h, b, 0, 0)), spec, spec],
        out_shape=[SDS((H, B * S, D), F32), SDS((H, B * NC, D, D), F32), SDS((H, B * NC, C, C), F32),
                   SDS((H, B * S, D), F32), SDS((H, B * S, D), F32)],
        scratch_shapes=[pltpu.VMEM((Sb, D), F32), pltpu.VMEM((Sb, D), F32), pltpu.VMEM((NCb, D, D), F32),
                        pltpu.VMEM((NCb, D, D), F32), pltpu.VMEM((NCb, SUBLANES, LANES), F32)],
        semantics=("arbitrary", "arbitrary"), args=(qg, kg, vg, gates))


def _mix_out(o_mla, o_gdn, proj, x2, mla_w, gdn_w, w_out):
    T, D = x2.shape
    tm = min(512, T)
    H = MLA_HEADS

    def body(om_ref, og_ref, z_ref, x_ref, mw_ref, gw_ref, w_ref, h_ref, mix_ref):
        z = z_ref[...]
        parts = [_rms(om_ref[h], mw_ref[h:h + 1, :])[0] for h in range(H)]
        for h in range(GDN_HEADS):
            zh = z[:, h * GDN_DIM:(h + 1) * GDN_DIM]
            parts.append(_rms(og_ref[h], gw_ref[...])[0] * (zh * _sigmoid(zh)))
        mix = jnp.concatenate(parts, axis=-1).astype(MXU_DTYPE)
        mix_ref[...] = mix
        h_ref[...] = x_ref[...] + jnp.dot(mix, w_ref[...], preferred_element_type=F32)

    hspec = pl.BlockSpec((H, tm, V_DIM), lambda i: (0, i, 0))
    return pl.pallas_call(
        body, grid=(T // tm,), name="mix_out",
        in_specs=[hspec, hspec, pl.BlockSpec((tm, GDN_WIDTH), lambda i: (i, P_GZ // GDN_WIDTH)),
                  pl.BlockSpec((tm, D), lambda i: (i, 0)),
                  pl.BlockSpec((H, V_DIM), lambda i: (0, 0)), pl.BlockSpec((1, GDN_DIM), lambda i: (0, 0)),
                  pl.BlockSpec((D, D), lambda i: (0, 0))],
        out_specs=[pl.BlockSpec((tm, D), lambda i: (i, 0)), pl.BlockSpec((tm, D), lambda i: (i, 0))],
        out_shape=[SDS((T, D), F32), SDS((T, D), MXU_DTYPE)],
        compiler_params=_params(("arbitrary",)),
    )(o_mla, o_gdn, proj, x2, mla_w, gdn_w, w_out)


def _mlp_fwd(h2, w_mn, w_up, w_down, target):
    T, D = h2.shape
    ns, _, ts = w_up.shape
    F = ns * ts
    tm = min(512, T)
    G = MLP_FWD_SHARDS
    tf, nf = G * ts, ns // G

    def body(h_ref, wn_ref, up_w, down_w, t_ref, up_ref, hn_ref, dy_ref, loss_ref, y_acc):
        j = pl.program_id(1)

        @pl.when(j == 0)
        def _():
            hn_ref[...] = _rms(h_ref[...], wn_ref[...])[0].astype(MXU_DTYPE)
            y_acc[...] = h_ref[...]

        parts = []
        for c in range(G):
            up = jnp.dot(hn_ref[...], up_w[c], preferred_element_type=F32)
            up_ref[:, c * ts:(c + 1) * ts] = up.astype(MXU_DTYPE)
            r = jnp.maximum(up, 0.0)
            parts.append(_mm(r * r, down_w[c * ts:(c + 1) * ts, :]))
        y_acc[...] += functools.reduce(jnp.add, parts)

        @pl.when(j == nf - 1)
        def _():
            err = y_acc[...] - t_ref[...]
            dy_ref[...] = err / D
            loss_ref[...] = jnp.full((1, SUBLANES, LANES), jnp.sum(err * err), F32)

    return pl.pallas_call(
        body, grid=(T // tm, nf), name="mlp_fwd",
        in_specs=[pl.BlockSpec((tm, D), lambda i, j: (i, 0)), pl.BlockSpec((1, D), lambda i, j: (0, 0)),
                  pl.BlockSpec((G, D, ts), lambda i, j: (j, 0, 0)), pl.BlockSpec((tf, D), lambda i, j: (j, 0)),
                  pl.BlockSpec((tm, D), lambda i, j: (i, 0))],
        out_specs=[pl.BlockSpec((tm, tf), lambda i, j: (i, j)), pl.BlockSpec((tm, D), lambda i, j: (i, 0)),
                   pl.BlockSpec((tm, D), lambda i, j: (i, 0)),
                   pl.BlockSpec((1, SUBLANES, LANES), lambda i, j: (i, 0, 0))],
        out_shape=[SDS((T, F), MXU_DTYPE), SDS((T, D), MXU_DTYPE), SDS((T, D), F32),
                   SDS((T // tm, SUBLANES, LANES), F32)],
        scratch_shapes=[pltpu.VMEM((tm, D), F32)],
        compiler_params=_params(("arbitrary", "arbitrary")),
    )(h2, w_mn, w_up, w_down, target)


def _mlp_bwd(dy, up, h2, w_mn, w_up, w_down):
    T, D = h2.shape
    ns, _, ts = w_up.shape
    F = ns * ts
    tm = min(512, T)
    G = MLP_BWD_SHARDS
    tf, nf = G * ts, ns // G

    def body(dy_ref, up_ref, h_ref, wn_ref, up_w, down_w, dh_ref, dhb_ref, dup_ref, act_ref, dyb_ref, dwn_ref, acc):
        i, j = pl.program_id(0), pl.program_id(1)

        @pl.when((i == 0) & (j == 0))
        def _():
            dwn_ref[...] = jnp.zeros_like(dwn_ref)

        @pl.when(j == 0)
        def _():
            acc[...] = jnp.zeros_like(acc)
            dyb_ref[...] = dy_ref[...].astype(MXU_DTYPE)

        parts = []
        for c in range(G):
            cols = slice(c * ts, (c + 1) * ts)
            r = jnp.maximum(up_ref[:, cols].astype(F32), 0.0)
            act_ref[:, cols] = (r * r).astype(MXU_DTYPE)
            dup = (_mm_nt(dyb_ref[...], down_w[cols, :]) * (2.0 * r)).astype(MXU_DTYPE)
            dup_ref[:, cols] = dup
            parts.append(_mm_nt(dup, up_w[c]))
        acc[...] += functools.reduce(jnp.add, parts)

        @pl.when(j == nf - 1)
        def _():
            hv = h_ref[...]
            _, rr = _rms(hv, wn_ref[...])
            dx, dw = _rms_bwd(acc[...], hv, wn_ref[...], rr)
            dh = dy_ref[...] + dx
            dh_ref[...] = dh
            dhb_ref[...] = dh.astype(MXU_DTYPE)
            dwn_ref[...] += dw

    row = lambda i, j: (i, 0)
    return pl.pallas_call(
        body, grid=(T // tm, nf), name="mlp_bwd",
        in_specs=[pl.BlockSpec((tm, D), row), pl.BlockSpec((tm, tf), lambda i, j: (i, j)), pl.BlockSpec((tm, D), row),
                  pl.BlockSpec((1, D), lambda i, j: (0, 0)),
                  pl.BlockSpec((G, D, ts), lambda i, j: (j, 0, 0)), pl.BlockSpec((tf, D), lambda i, j: (j, 0))],
        out_specs=[pl.BlockSpec((tm, D), row), pl.BlockSpec((tm, D), row),
                   pl.BlockSpec((tm, tf), lambda i, j: (i, j)), pl.BlockSpec((tm, tf), lambda i, j: (i, j)),
                   pl.BlockSpec((tm, D), row), pl.BlockSpec((1, D), lambda i, j: (0, 0))],
        out_shape=[SDS((T, D), F32), SDS((T, D), MXU_DTYPE), SDS((T, F), MXU_DTYPE), SDS((T, F), MXU_DTYPE),
                   SDS((T, D), MXU_DTYPE), SDS((1, D), F32)],
        scratch_shapes=[pltpu.VMEM((tm, D), F32)],
        compiler_params=_params(("arbitrary", "arbitrary")),
    )(dy, up, h2, w_mn, w_up, w_down)


def _mix_bwd(dhb, o_mla, o_gdn, proj, mla_w, gdn_w, w_out):
    T, D = dhb.shape
    tm = min(512, T)
    H = MLA_HEADS

    def body(dh_ref, om_ref, og_ref, z_ref, mw_ref, gw_ref, w_ref, dom_ref, dog_ref, dz_ref, dmw_ref, dgw_ref,
             delta_ref):
        @pl.when(pl.program_id(0) == 0)
        def _():
            dmw_ref[...] = jnp.zeros_like(dmw_ref)
            dgw_ref[...] = jnp.zeros_like(dgw_ref)

        dmix = _mm_nt(dh_ref[...], w_ref[...])
        z = z_ref[...]
        dmw, dzs = [], []
        dgw = jnp.zeros((1, GDN_DIM), F32)
        for h in range(H):
            o = om_ref[h]
            w = mw_ref[h:h + 1, :]
            _, r = _rms(o, w)
            dx, dw = _rms_bwd(dmix[:, h * V_DIM:(h + 1) * V_DIM], o, w, r)
            dom_ref[h] = dx.astype(MXU_DTYPE)
            delta_ref[h] = jnp.sum(dx * o, axis=-1, keepdims=True)
            dmw.append(dw)
        for h in range(GDN_HEADS):
            o = og_ref[h]
            w = gw_ref[...]
            zh = z[:, h * GDN_DIM:(h + 1) * GDN_DIM]
            sg = _sigmoid(zh)
            yn, r = _rms(o, w)
            dy = dmix[:, H * V_DIM + h * GDN_DIM:H * V_DIM + (h + 1) * GDN_DIM]
            dzs.append(dy * yn * (sg * (1.0 + zh * (1.0 - sg))))
            dx, dw = _rms_bwd(dy * (zh * sg), o, w, r)
            dog_ref[h] = dx.astype(MXU_DTYPE)
            dgw = dgw + dw
        dz_ref[...] = jnp.concatenate(dzs, axis=-1).astype(MXU_DTYPE)
        dmw_ref[...] += jnp.concatenate(dmw, axis=0)
        dgw_ref[...] += dgw

    hspec = pl.BlockSpec((H, tm, V_DIM), lambda i: (0, i, 0))
    return pl.pallas_call(
        body, grid=(T // tm,), name="mix_bwd",
        in_specs=[pl.BlockSpec((tm, D), lambda i: (i, 0)), hspec, hspec,
                  pl.BlockSpec((tm, GDN_WIDTH), lambda i: (i, P_GZ // GDN_WIDTH)),
                  pl.BlockSpec((H, V_DIM), lambda i: (0, 0)), pl.BlockSpec((1, GDN_DIM), lambda i: (0, 0)),
                  pl.BlockSpec((D, D), lambda i: (0, 0))],
        out_specs=[hspec, hspec, pl.BlockSpec((tm, GDN_WIDTH), lambda i: (i, 0)),
                   pl.BlockSpec((H, V_DIM), lambda i: (0, 0)), pl.BlockSpec((1, GDN_DIM), lambda i: (0, 0)),
                   pl.BlockSpec((H, tm, 1), lambda i: (0, i, 0))],
        out_shape=[SDS((H, T, V_DIM), MXU_DTYPE), SDS((H, T, GDN_DIM), MXU_DTYPE), SDS((T, GDN_WIDTH), MXU_DTYPE),
                   SDS((H, V_DIM), F32), SDS((1, GDN_DIM), F32), SDS((H, T, 1), F32)],
        compiler_params=_params(("arbitrary",)),
    )(dhb, o_mla, o_gdn, proj, mla_w, gdn_w, w_out)


def _attn_bwd(q4, k4, v4, do4, delta4, lse4, B, S, transfer=None):
    H = MLA_HEADS
    bq = min(ATTN_BLOCK, S)
    nq = S // bq
    rows = bq // ATTN_CHAINS

    def body(q_ref, k_ref, v_ref, do_ref, delta_ref, lse_ref, dq_ref, dk_ref, dv_ref):
        dq_ref[...] = jnp.zeros_like(dq_ref)
        dk_ref[...] = jnp.zeros_like(dk_ref)
        dv_ref[...] = jnp.zeros_like(dv_ref)

        col = lax.broadcasted_iota(jnp.int32, (rows, bq), 1)
        row = lax.broadcasted_iota(jnp.int32, (rows, bq), 0)

        def k_step(kj, carry):
            ks = pl.multiple_of(kj * bq, bq)
            k = k_ref[0, pl.ds(ks, bq), :]
            v = v_ref[0, pl.ds(ks, bq), :]

            def q_block(qs, diagonal):
                dks, dvs = [None] * ATTN_CHAINS, [None] * ATTN_CHAINS

                def chain(j):
                    sl = pl.ds(qs + j * rows, rows)
                    q = q_ref[0, sl, :]
                    do = do_ref[0, sl, :].astype(MXU_DTYPE)
                    s = _mm_nt(q, k)
                    dp = _mm_nt(do, v)
                    yield
                    p = jnp.exp(s - lse_ref[0, sl, :])
                    if diagonal:
                        p = jnp.where(col <= row + j * rows, p, 0.0)
                    ds = p * (dp - delta_ref[0, sl, :])
                    yield
                    dvs[j] = _mm_tn(p, do)
                    dks[j] = _mm_tn(ds, q)
                    dq_ref[0, sl, :] += _mm(ds, k)

                _lockstep([chain(j) for j in range(ATTN_CHAINS)])
                dv_ref[0, pl.ds(ks, bq), :] += functools.reduce(jnp.add, dvs)
                dk_ref[0, pl.ds(ks, bq), :] += functools.reduce(jnp.add, dks)

            q_block(ks, True)

            def q_step(qi, c):
                q_block(pl.multiple_of(qi * bq, bq), False)
                return c

            lax.fori_loop(kj + 1, nq, q_step, 0)
            return carry

        lax.fori_loop(0, nq, k_step, 0)

    spec = lambda d: pl.BlockSpec((1, S, d), lambda h, b: (h, b, 0))
    return _call_beside(
        body, transfer, grid=(H, B), name="attn_bwd",
        in_specs=[spec(QK_DIM), spec(QK_DIM), spec(V_DIM), spec(V_DIM), spec(1), spec(1)],
        out_specs=[spec(QK_DIM), spec(QK_DIM), spec(V_DIM)],
        out_shape=[SDS((H, B * S, QK_DIM), F32), SDS((H, B * S, QK_DIM), F32), SDS((H, B * S, V_DIM), F32)],
        scratch_shapes=[], semantics=("arbitrary", "arbitrary"),
        args=(q4, k4, v4, do4, delta4, lse4))


def _gdn_bwd(qg, kg, vg, gates, states, ainv, u4, w4, do4, B, S, transfer=None):
    H, D, C = GDN_HEADS, GDN_DIM, CHUNK
    NC = S // C
    U = GDN_BWD_UNROLL if NC % GDN_BWD_UNROLL == 0 else 1
    NG = NC // U

    def body(q_ref, k_ref, v_ref, g_ref, st_ref, ai_ref, u_ref, w_ref, do_ref, dq_ref, dk_ref, dv_ref, dgb_ref,
             kd_s, x1_s, x2_s, el_s, dvn_s, ds_s, w2t_s):
        h = pl.program_id(0)
        lane = lax.broadcasted_iota(jnp.int32, (C, LANES), 1)
        ri = lax.broadcasted_iota(jnp.int32, (C, C), 0)
        ci = lax.broadcasted_iota(jnp.int32, (C, C), 1)
        rcol = lax.broadcasted_iota(jnp.int32, (C, 1), 0)

        def rsum(a):
            return jnp.sum(a, axis=-1, keepdims=True)

        def prepare(n):
            cs = n * C
            q = q_ref[0, pl.ds(cs, C), :]
            k = k_ref[0, pl.ds(cs, C), :]
            do = do_ref[0, pl.ds(cs, C), :]
            Gc, bt, Gam, e, f, eL = _chunk_decays(g_ref[pl.ds(cs, C), :], lane, h, ri, ci, rcol)
            At = _mm_nt(q, k) * Gam
            yield
            x1 = _mm_tn(At, do)
            x2 = _mm_tn(q * e, do)
            kd = k * f
            w = w_ref[0, pl.ds(cs, C), :]
            yield
            x1_s[pl.ds(cs, C), :] = x1
            x2_s[n] = x2 - _mm_tn(w, x1)
            w2t_s[n] = _mm_tn(w, kd)
            kd_s[pl.ds(cs, C), :] = kd
            el_s[n] = jnp.broadcast_to(eL, (SUBLANES, LANES))

        def recur(n, dS):
            cs = n * C
            ds_s[n] = dS
            dvn_s[pl.ds(cs, C), :] = x1_s[pl.ds(cs, C), :] + _mm(kd_s[pl.ds(cs, C), :], dS)
            return x2_s[n] + el_s[n, 0:1, :] * dS - _mm(w2t_s[n], dS)

        def local(n):
            cs = n * C
            q = q_ref[0, pl.ds(cs, C), :]
            k = k_ref[0, pl.ds(cs, C), :]
            v = v_ref[0, pl.ds(cs, C), :]
            do = do_ref[0, pl.ds(cs, C), :]
            u = u_ref[0, pl.ds(cs, C), :]
            w = w_ref[0, pl.ds(cs, C), :]
            dvn = dvn_s[pl.ds(cs, C), :]
            dS = ds_s[n]
            Gc, bt, Gam, e, f, eL = _chunk_decays(g_ref[pl.ds(cs, C), :], lane, h, ri, ci, rcol)
            S0 = st_ref[0, n]
            AinvT = ai_ref[0, n]
            qk = _mm_nt(jnp.concatenate([q, k], axis=0), k)
            QK, KK = qk[:C], qk[C:]
            be = bt * e
            sol = jnp.concatenate([u, w], axis=-1)
            vn = u - _mm(w, S0)
            yield
            dAt = jnp.where(ri >= ci, _mm_nt(do, vn), 0.0)
            dqd = _mm_nt(do, S0)
            dw = -_mm_nt(dvn, S0)
            dkd = _mm_nt(vn, dS)
            deL = jnp.sum(rsum(dS * S0), axis=0, keepdims=True)
            yield
            dR = _mm_exact(AinvT, jnp.concatenate([dvn, dw], axis=-1))
            dR1, dR2 = dR[:, :D], dR[:, D:]
            yield
            dL = jnp.where(ri > ci, -_mm_nt(dR, sol), 0.0)
            yield
            dv_ref[0, pl.ds(cs, C), :] = dR1 * bt
            r2 = rsum(dR2 * k)
            X = dL * Gam
            dbt = rsum(dR1 * v) + r2 * e + rsum(X * KK)
            de = r2 * bt + rsum(dqd * q)
            dKK = X * bt
            dQK = dAt * Gam
            dq_ref[0, pl.ds(cs, C), :] = _mm(dQK, k) + dqd * e
            dk_ref[0, pl.ds(cs, C), :] = dR2 * be + _mm(dKK + dKK.T, k) + _mm_tn(dQK, q) + dkd * f
            df = rsum(dkd * k)
            Z = (dL * (bt * KK) + dAt * QK) * Gam
            dG = rsum(Z) - rsum(Z.T) + de * e - df * f
            dGl = jnp.sum(df * f, axis=0, keepdims=True) + deL * eL
            dG = dG + jnp.where(rcol == C - 1, dGl, 0.0)
            dgb_ref[0, pl.ds(cs, C), :] = jnp.where(lane == 0, dG, jnp.where(lane == 1, dbt, 0.0))

        state = [jnp.zeros((D, D), F32)]

        def recur_group(g):
            for j, n in enumerate(reversed(range(g * U, (g + 1) * U))):
                state[0] = recur(n, state[0])
                if j % GDN_RECUR_STEPS_PER_STAGE == GDN_RECUR_STEPS_PER_STAGE - 1:
                    yield

        def stage(fn, g):
            return _together([fn(g * U + j) for j in range(U)])

        for step in range(NG + 2):
            jobs = [(stage, prepare, NG - 1 - step), (None, None, NG - step), (stage, local, NG + 1 - step)]
            _lockstep([recur_group(g) if make is None else make(fn, g) for make, fn, g in jobs if 0 <= g < NG])

    spec = pl.BlockSpec((1, S, D), lambda h, b: (h, b, 0))
    return _call_beside(
        body, transfer, grid=(H, B), name="gdn_bwd",
        in_specs=[spec, spec, spec, pl.BlockSpec((S, LANES), lambda h, b: (b, 0)),
                  pl.BlockSpec((1, NC, D, D), lambda h, b: (h, b, 0, 0)),
                  pl.BlockSpec((1, NC, C, C), lambda h, b: (h, b, 0, 0)), spec, spec, spec],
        out_specs=[spec, spec, spec, spec],
        out_shape=[SDS((H, B * S, D), F32)] * 4,
        scratch_shapes=[pltpu.VMEM((S, D), F32), pltpu.VMEM((S, D), F32), pltpu.VMEM((NC, D, D), F32),
                        pltpu.VMEM((NC, SUBLANES, LANES), F32), pltpu.VMEM((S, D), F32),
                        pltpu.VMEM((NC, D, D), F32), pltpu.VMEM((NC, D, D), F32)],
        semantics=("arbitrary", "arbitrary"), args=(qg, kg, vg, gates, states, ainv, u4, w4, do4))


def _gdn_pre_bwd(proj, conv_w, alog_l, dt_l, dq4, dk4, dv4, dgb4, S):
    T = proj.shape[0]
    tm = min(256, T)
    tiles_per_seq = S // tm
    C3 = 3 * GDN_WIDTH
    H = GDN_HEADS

    def body(u_ref, halo_ref, gab_ref, w_ref, alog_ref, dt_ref, dq_ref, dk_ref, dv_ref, dgb_ref,
             dc_ref, dgab_ref, dcw_ref, dalog_ref, ddt_ref):
        i = pl.program_id(0)

        @pl.when(i == 0)
        def _():
            dcw_ref[...] = jnp.zeros_like(dcw_ref)
            dalog_ref[...] = jnp.zeros_like(dalog_ref)
            ddt_ref[...] = jnp.zeros_like(ddt_ref)

        halo = jnp.where(i % tiles_per_seq == 0, 0.0, halo_ref[...])
        c, sh = _conv_taps(u_ref[...], halo, w_ref[...])
        sg = _sigmoid(c)
        a = c * sg
        das = [None] * (3 * H)
        for h in range(H):
            xq = a[:, h * GDN_DIM:(h + 1) * GDN_DIM]
            xk = a[:, GDN_WIDTH + h * GDN_DIM:GDN_WIDTH + (h + 1) * GDN_DIM]
            das[h] = _l2n_bwd(dq_ref[h], xq, GDN_QSCALE)
            das[H + h] = _l2n_bwd(dk_ref[h], xk, 1.0)
            das[2 * H + h] = dv_ref[h]
        dc = jnp.concatenate(das, axis=-1) * (sg * (1.0 + c * (1.0 - sg)))
        dc_ref[...] = dc
        dcw_ref[...] += jnp.concatenate(
            [jnp.sum(dc * sh[CONV_W - 1 - t], axis=0, keepdims=True) for t in range(CONV_W)], axis=0)
        lane = lax.broadcasted_iota(jnp.int32, (tm, LANES), 1)
        ric = lax.broadcasted_iota(jnp.int32, (tm, LANES), 0) % CHUNK
        dG = jnp.zeros((tm, LANES), F32)
        for h in range(H):
            t = dgb_ref[h]
            dG = dG + jnp.where(lane == h, _pick_lane(t, lane, 0), 0.0) \
                    + jnp.where(lane == h + H, _pick_lane(t, lane, 1), 0.0)
        is_g = lane < H
        dg = jnp.where(is_g, _chunk_rev_cumsum(jnp.where(is_g, dG, 0.0), ric), 0.0)
        gab = gab_ref[...]
        g, beta = _gate_values(gab, alog_ref[...], dt_ref[...], lane)
        dga = jnp.where(is_g, dg * (-jnp.exp(alog_ref[...])) * _sigmoid(gab + dt_ref[...]), 0.0)
        dgb = jnp.where(is_g, 0.0, dG) * beta * (1.0 - beta)
        dgab_ref[...] = (dga + dgb).astype(MXU_DTYPE)
        dalog_ref[...] += jnp.sum(dg * g, axis=0, keepdims=True)
        ddt_ref[...] += jnp.sum(dga, axis=0, keepdims=True)

    hspec = pl.BlockSpec((H, tm, GDN_DIM), lambda i: (0, i, 0))
    vec = pl.BlockSpec((1, LANES), lambda i: (0, 0))
    return pl.pallas_call(
        body, grid=(T // tm,), name="gdn_pre_bwd",
        in_specs=[pl.BlockSpec((tm, C3), lambda i: (i, 0)),
                  pl.BlockSpec((SUBLANES, C3), lambda i: (jnp.maximum(i * (tm // SUBLANES) - 1, 0), 0)),
                  pl.BlockSpec((tm, LANES), lambda i: (i, P_GAB // LANES)),
                  pl.BlockSpec((CONV_W, C3), lambda i: (0, 0)), vec, vec, hspec, hspec, hspec, hspec],
        out_specs=[pl.BlockSpec((tm, C3), lambda i: (i, 0)), pl.BlockSpec((tm, LANES), lambda i: (i, 0)),
                   pl.BlockSpec((CONV_W, C3), lambda i: (0, 0)), vec, vec],
        out_shape=[SDS((T, C3), F32), SDS((T, LANES), MXU_DTYPE), SDS((CONV_W, C3), F32),
                   SDS((1, LANES), F32), SDS((1, LANES), F32)],
        compiler_params=_params(("arbitrary",)),
    )(proj, proj, proj, conv_w, alog_l, dt_l, dq4, dk4, dv4, dgb4)


def _mla_pre_bwd(proj, cosf, sinf, w_qln, w_kvln, w_uq_p, w_ukv, qnw, knw, dq4, dk4, dv4, transfer=None):
    T = proj.shape[0]
    tm = min(256, T)
    H = MLA_HEADS

    def body(ql_ref, kvl_ref, kpe_ref, cos_ref, sin_ref, wq_ref, wkv_ref, uq_ref, ukv_ref, qnw_ref, knw_ref,
             dq_ref, dk_ref, dv_ref,
             dql_ref, dkvl_ref, dkpe_ref, dqraw_ref, dkvraw_ref, qn_ref, kvn_ref, dwq_ref, dwkv_ref, dqnw_ref, dknw_ref):
        @pl.when(pl.program_id(0) == 0)
        def _():
            for r in (dwq_ref, dwkv_ref, dqnw_ref, dknw_ref):
                r[...] = jnp.zeros_like(r)

        cos, sin = cos_ref[...], sin_ref[...]
        qnw_, knw_ = qnw_ref[...], knw_ref[...]
        ql, kvl = ql_ref[...], kvl_ref[...]
        kpe_raw = kpe_ref[...][:, :ROPE]
        rms = functools.partial(_rms, on_mxu=True)
        rms_bwd = functools.partial(_rms_bwd, on_mxu=True)
        qn, rq = rms(ql, wq_ref[...])
        kvn, rkv = rms(kvl, wkv_ref[...])
        qn_ref[...] = qn.astype(MXU_DTYPE)
        kvn_ref[...] = kvn.astype(MXU_DTYPE)
        qraw = _mm(qn, uq_ref[...])
        kvraw = _mm(kvn, ukv_ref[...])
        dq_nope, dq_pe, dkv_parts = [], [], []
        dqnw_n = jnp.zeros((1, NOPE), F32)
        dqnw_p = jnp.zeros((1, ROPE), F32)
        dknw_n = jnp.zeros((1, NOPE), F32)
        dkpe = jnp.zeros((tm, ROPE), F32)
        for h in range(H):
            dq = dq_ref[h] * ATT_SCALE
            x = qraw[:, h * NOPE:(h + 1) * NOPE]
            dx, dw = rms_bwd(dq[:, :NOPE], x, qnw_[:, :NOPE], rms(x, qnw_[:, :NOPE])[1])
            dq_nope.append(dx)
            dqnw_n = dqnw_n + dw
            x = qraw[:, H * NOPE + h * ROPE:H * NOPE + (h + 1) * ROPE]
            dx, dw = rms_bwd(_rope_bwd(dq[:, NOPE:], cos, sin), x, qnw_[:, NOPE:], rms(x, qnw_[:, NOPE:])[1])
            dq_pe.append(dx)
            dqnw_p = dqnw_p + dw
            dk = dk_ref[h]
            x = kvraw[:, h * 256:h * 256 + NOPE]
            dx, dw = rms_bwd(dk[:, :NOPE], x, knw_[:, :NOPE], rms(x, knw_[:, :NOPE])[1])
            dknw_n = dknw_n + dw
            dkpe = dkpe + dk[:, NOPE:]
            dkv_parts += [dx, dv_ref[h]]
        dx, dknw_p = rms_bwd(_rope_bwd(dkpe, cos, sin), kpe_raw, knw_[:, NOPE:], rms(kpe_raw, knw_[:, NOPE:])[1])
        dkpe_ref[...] = jnp.concatenate([dx, jnp.zeros((tm, LANES - ROPE), F32)], axis=-1).astype(MXU_DTYPE)
        dqraw = jnp.concatenate(dq_nope + dq_pe, axis=-1).astype(MXU_DTYPE)
        dkvraw = jnp.concatenate(dkv_parts, axis=-1).astype(MXU_DTYPE)
        dqraw_ref[...] = dqraw
        dkvraw_ref[...] = dkvraw
        dx, dw = rms_bwd(_mm_nt(dqraw, uq_ref[...]), ql, wq_ref[...], rq)
        dql_ref[...] = dx.astype(MXU_DTYPE)
        dwq_ref[...] += dw
        dx, dw = rms_bwd(_mm_nt(dkvraw, ukv_ref[...]), kvl, wkv_ref[...], rkv)
        dkvl_ref[...] = dx.astype(MXU_DTYPE)
        dwkv_ref[...] += dw
        dqnw_ref[...] += jnp.concatenate([dqnw_n, dqnw_p], axis=-1)
        dknw_ref[...] += jnp.concatenate([dknw_n, dknw_p], axis=-1)

    full = lambda a: pl.BlockSpec(a.shape, lambda i: (0,) * a.ndim)
    rows = lambda n: pl.BlockSpec((tm, n), lambda i: (i, 0))
    const = lambda n: pl.BlockSpec((1, n), lambda i: (0, 0))
    NQ, NKV = w_uq_p.shape[1], w_ukv.shape[1]
    return _call_beside(
        body, transfer, grid=(T // tm,), name="mla_pre_bwd", scratch_shapes=[], semantics=("arbitrary",),
        args=(proj, proj, proj, cosf, sinf, w_qln, w_kvln, w_uq_p, w_ukv, qnw, knw, dq4, dk4, dv4),
        in_specs=[pl.BlockSpec((tm, 256), lambda i: (i, P_QLAT // 256)),
                  pl.BlockSpec((tm, 256), lambda i: (i, P_KVLAT // 256)),
                  pl.BlockSpec((tm, 128), lambda i: (i, P_KPE // 128)),
                  rows(ROPE), rows(ROPE),
                  full(w_qln), full(w_kvln), full(w_uq_p), full(w_ukv), full(qnw), full(knw),
                  pl.BlockSpec((H, tm, QK_DIM), lambda i: (0, i, 0)),
                  pl.BlockSpec((H, tm, QK_DIM), lambda i: (0, i, 0)),
                  pl.BlockSpec((H, tm, V_DIM), lambda i: (0, i, 0))],
        out_specs=[rows(Q_LORA), rows(KV_LORA), rows(LANES), rows(NQ), rows(NKV), rows(Q_LORA), rows(KV_LORA),
                   const(Q_LORA), const(KV_LORA), const(QK_DIM), const(QK_DIM)],
        out_shape=[SDS((T, Q_LORA), MXU_DTYPE), SDS((T, KV_LORA), MXU_DTYPE), SDS((T, LANES), MXU_DTYPE),
                   SDS((T, NQ), MXU_DTYPE), SDS((T, NKV), MXU_DTYPE),
                   SDS((T, Q_LORA), MXU_DTYPE), SDS((T, KV_LORA), MXU_DTYPE),
                   SDS((1, Q_LORA), F32), SDS((1, KV_LORA), F32), SDS((1, QK_DIM), F32), SDS((1, QK_DIM), F32)])


def _in_proj_bwd(dc, conv_w, dgz, dql, dkvl, dkpe, dgab, w_in_p, dh, x2, w_an, S):
    T, D = x2.shape
    N = w_in_p.shape[1]
    C3 = dc.shape[1]
    tm = min(512, S)
    assert S % tm == 0 and T % tm == 0, "a token tile must not straddle two sequences"
    tiles_per_seq = S // tm
    nblk = T // SUBLANES

    def body(dc_ref, nxt_ref, cw_ref, b_ref, c_ref, d_ref, e_ref, f_ref, w_ref, dh_ref, x_ref, wn_ref,
             dx_ref, dp_ref, dwn_ref):
        i = pl.program_id(0)

        @pl.when(i == 0)
        def _():
            dwn_ref[...] = jnp.zeros_like(dwn_ref)

        nxt = jnp.where(i % tiles_per_seq == tiles_per_seq - 1, 0.0, nxt_ref[...])
        dcv, cw = dc_ref[...], cw_ref[...]
        du = cw[3:4] * dcv
        for j in range(1, CONV_W):
            du = du + cw[3 - j:4 - j] * _shift_up(dcv, nxt, j)
        dp = jnp.concatenate([du.astype(MXU_DTYPE), b_ref[...], c_ref[...], d_ref[...], e_ref[...], f_ref[...]],
                             axis=-1).astype(MXU_DTYPE)
        dp_ref[...] = dp
        x = x_ref[...]
        _, r = _rms(x, wn_ref[...])
        dx, dw = _rms_bwd(_mm_nt(dp, w_ref[...]), x, wn_ref[...], r)
        dx_ref[...] = dh_ref[...] + dx
        dwn_ref[...] += dw

    rows = lambda n: pl.BlockSpec((tm, n), lambda i: (i, 0))
    return pl.pallas_call(
        body, grid=(T // tm,), name="in_proj_bwd",
        in_specs=[rows(C3),
                  pl.BlockSpec((SUBLANES, C3), lambda i: (jnp.minimum((i + 1) * (tm // SUBLANES), nblk - 1), 0)),
                  pl.BlockSpec((CONV_W, C3), lambda i: (0, 0)),
                  rows(dgz.shape[1]), rows(dql.shape[1]), rows(dkvl.shape[1]),
                  rows(dkpe.shape[1]), rows(dgab.shape[1]),
                  pl.BlockSpec((D, N), lambda i: (0, 0)), rows(D), rows(D), pl.BlockSpec((1, D), lambda i: (0, 0))],
        out_specs=[rows(D), rows(N), pl.BlockSpec((1, D), lambda i: (0, 0))],
        out_shape=[SDS((T, D), F32), SDS((T, N), MXU_DTYPE), SDS((1, D), F32)],
        compiler_params=_params(("arbitrary",)),
    )(dc, dc, conv_w, dgz, dql, dkvl, dkpe, dgab, w_in_p, dh, x2, w_an)


def _wgrad(a, b, name, column_shards=False):
    T, M = a.shape
    N = b.shape[1]
    tM = _divisor_tile(M, 1024)
    tN = N // N_DEV if column_shards else _divisor_tile(N, 1536)
    tk = min(T, 2048)
    nk = T // tk

    def body(a_ref, b_ref, o_ref, acc):
        k = pl.program_id(2)

        @pl.when(k == 0)
        def _():
            acc[...] = jnp.zeros_like(acc)

        acc[...] += _mm_tn(a_ref[...], b_ref[...])

        @pl.when(k == nk - 1)
        def _():
            o_ref[...] = acc[...].astype(WIRE_DTYPE).reshape(o_ref.shape)

    if column_shards:
        out_spec, out_shape = pl.BlockSpec((1, tM, tN), lambda i, j, k: (j, i, 0)), SDS((N_DEV, M, tN), WIRE_DTYPE)
    else:
        out_spec, out_shape = pl.BlockSpec((tM, tN), lambda i, j, k: (i, j)), SDS((M, N), WIRE_DTYPE)
    return pl.pallas_call(
        body, grid=(M // tM, N // tN, nk), name=name,
        in_specs=[pl.BlockSpec((tk, tM), lambda i, j, k: (k, i)), pl.BlockSpec((tk, tN), lambda i, j, k: (k, j))],
        out_specs=out_spec, out_shape=out_shape,
        scratch_shapes=[pltpu.VMEM((tM, tN), F32)],
        compiler_params=_params(("arbitrary", "arbitrary", "arbitrary")),
    )(a, b)


def _adamw(g, w, m, v):
    m = ADAM_B1 * m + (1.0 - ADAM_B1) * g
    v = ADAM_B2 * v + (1.0 - ADAM_B2) * jnp.square(g)
    m_hat = m / (1.0 - ADAM_B1 ** ADAM_STEP)
    v_hat = v / (1.0 - ADAM_B2 ** ADAM_STEP)
    return -ADAM_LR * (m_hat / (jnp.sqrt(v_hat) + ADAM_EPS) + ADAM_WD * w), m, v


def _reduce_adamw(parts, w, m, v, name):
    R, C = w.shape
    _, Rp, Cp = parts.shape
    tr = min(R, 256)
    tp = tr if Rp == R else Rp

    def body(p_ref, w_ref, m_ref, v_ref, g_ref, d_ref, nm_ref, nv_ref):
        g = p_ref[0].astype(F32)
        for s in range(1, N_DEV):
            g = g + p_ref[s].astype(F32)
        g = g[:tr, :C]
        g_ref[...] = g
        d_ref[...], nm_ref[...], nv_ref[...] = _adamw(g, w_ref[...], m_ref[...], v_ref[...])

    spec = pl.BlockSpec((tr, C), lambda i: (i, 0))
    return pl.pallas_call(
        body, grid=(R // tr,), name=name,
        in_specs=[pl.BlockSpec((N_DEV, tp, Cp), lambda i: (0, i, 0)), spec, spec, spec],
        out_specs=[spec] * 4, out_shape=[SDS((R, C), F32)] * 4,
        compiler_params=_params(("arbitrary",)),
    )(parts, w, m, v)


SMALL_ROWS, SMALL_COLS = 16, 1024
SMALL_LAYOUT = (
    ("attn_norm_w", 0, 1, 1024, 1024), ("mlp_norm_w", 1, 1, 1024, 1024), ("q_lat_norm_w", 2, 1, 256, 256),
    ("kv_lat_norm_w", 3, 1, 256, 256), ("q_norm_w", 4, 1, 192, 192), ("k_norm_w", 5, 1, 192, 192),
    ("mla_out_norm_w", 6, 4, 128, 128), ("a_log", 10, 1, 128, 4), ("dt_bias", 11, 1, 128, 4),
    ("gdn_norm_w", 12, 1, 128, 128))
LOSS_ENTRY = ("loss", 13, 1, 128, 128)


def _adamw_replicated(parts, ws, ms, vs):
    n = len(SMALL_LAYOUT)

    def body(*refs):
        p_ref = refs[0]
        w_refs, m_refs, v_refs = refs[1:1 + n], refs[1 + n:1 + 2 * n], refs[1 + 2 * n:1 + 3 * n]
        outs = refs[1 + 3 * n:]
        s = p_ref[0]
        for d in range(1, N_DEV):
            s = s + p_ref[d]
        for i, (_, r0, nr, _, pw) in enumerate(SMALL_LAYOUT):
            g = s[r0:r0 + nr, :pw]
            outs[i][...] = g
            outs[n + i][...], outs[2 * n + i][...], outs[3 * n + i][...] = _adamw(
                g, w_refs[i][...], m_refs[i][...], v_refs[i][...])
        _, r0, nr, gw, _ = LOSS_ENTRY
        outs[4 * n][...] = s[r0:r0 + nr, :gw]

    res = pl.pallas_call(
        body, name="adamw_replicated",
        out_shape=[SDS(w.shape, F32) for w in ws] * 4 + [SDS((1, LANES), F32)],
        compiler_params=_params(),
    )(parts, *ws, *ms, *vs)
    return [res[k * n:(k + 1) * n] for k in range(4)], res[4 * n][0, 0]


COPIES_PER_ARRAY = N_DEV - 1


def _two_level_gather(srcs, outs, send_sems, recv_sems, local_sems=None, stage="all"):
    mx, my, mc = lax.axis_index("x"), lax.axis_index("y"), lax.axis_index("c")
    me, sibling = (mx, my, mc), (mx, my, 1 - mc)
    chips = [(1 - mx, my), (mx, 1 - my), (1 - mx, 1 - my)]
    arrays = range(len(srcs))

    def copy(a, k, block, to, src=None):
        px, py, pc = block
        slot = outs[a].at[4 * px + 2 * py + pc]
        sem = a * COPIES_PER_ARRAY + k
        return pltpu.make_async_remote_copy(
            src_ref=slot if src is None else src, dst_ref=slot,
            send_sem=send_sems.at[sem], recv_sem=recv_sems.at[sem], device_id=to, device_id_type=MESH_ID)

    mine = [] if local_sems is None else [
        pltpu.make_async_copy(srcs[a], outs[a].at[4 * mx + 2 * my + mc], local_sems.at[a]) for a in arrays]
    first = []
    for a in arrays:
        first.append(copy(a, 0, me, sibling, src=srcs[a]))
        first += [copy(a, 1 + j, me, (*chip, mc), src=srcs[a]) for j, chip in enumerate(chips)]
    if stage in ("all", "start"):
        for cp in mine + first:
            cp.start()
    if stage in ("all", "finish"):
        forwards = []
        for j, chip in enumerate(chips):
            for a in arrays:
                copy(a, 1 + j, (*chip, mc), me).wait_recv()
                fwd = copy(a, 4 + j, (*chip, mc), sibling)
                fwd.start()
                forwards.append(fwd)
        for a in arrays:
            copy(a, 0, sibling, me).wait_recv()
        for j, chip in enumerate(chips):
            for a in arrays:
                copy(a, 4 + j, (*chip, 1 - mc), me).wait_recv()
        for cp in first + forwards:
            cp.wait_send()
        for cp in mine:
            cp.wait()


def _comm_scratch(n):
    return [pltpu.SemaphoreType.DMA((n * COPIES_PER_ARRAY,)), pltpu.SemaphoreType.DMA((n * COPIES_PER_ARRAY,)),
            pltpu.SemaphoreType.DMA((n,))]


def _any_specs(n):
    return [pl.BlockSpec(memory_space=pl.ANY)] * n


def _gather_weights(shards):
    n = len(shards)

    def body(*refs):
        _two_level_gather(refs[:n], refs[n:2 * n], *refs[2 * n:])

    return pl.pallas_call(
        body, name="gather_weights",
        out_shape=[SDS((N_DEV,) + s.shape, s.dtype) for s in shards],
        in_specs=_any_specs(n), out_specs=_any_specs(n), scratch_shapes=_comm_scratch(n),
    )(*shards)


def _gather_small_grads(gs, loss_lanes):
    gs = list(gs) + [loss_lanes]
    n = len(gs)

    def body(*refs):
        g_refs, out_ref = refs[:n], refs[n]
        tile, send_sems, recv_sems = refs[n + 1:]
        tile[...] = jnp.zeros_like(tile)
        for (_, r0, nr, gw, _), g in zip(SMALL_LAYOUT + (LOSS_ENTRY,), g_refs):
            tile[r0:r0 + nr, 0:gw] = g[...]
        me = 4 * lax.axis_index("x") + 2 * lax.axis_index("y") + lax.axis_index("c")
        out_ref[me] = tile[...]
        _two_level_gather([tile], [out_ref], send_sems, recv_sems)

    return pl.pallas_call(
        body, name="gather_small_grads",
        out_shape=SDS((N_DEV, SMALL_ROWS, SMALL_COLS), F32),
        in_specs=[pl.BlockSpec(memory_space=pltpu.VMEM)] * n,
        out_specs=pl.BlockSpec(memory_space=pltpu.VMEM),
        scratch_shapes=[pltpu.VMEM((SMALL_ROWS, SMALL_COLS), F32),
                        pltpu.SemaphoreType.DMA((COPIES_PER_ARRAY,)), pltpu.SemaphoreType.DMA((COPIES_PER_ARRAY,))],
    )(*gs)


def _exchange_grads(slabs):
    n = len(slabs)

    def body(*refs):
        _exchange(refs[:n], refs[n:2 * n], *refs[2 * n:])

    return pl.pallas_call(
        body, name="exchange_grads",
        out_shape=[SDS(s.shape, s.dtype) for s in slabs],
        in_specs=_any_specs(n), out_specs=_any_specs(n), scratch_shapes=_comm_scratch(n),
    )(*slabs)


class _Transfer:
    def __init__(self, kind, arrays):
        self.kind, self.arrays, self.n = kind, list(arrays), len(arrays)

    def out_shapes(self):
        if self.kind == "gather":
            return [SDS((N_DEV,) + a.shape, a.dtype) for a in self.arrays]
        return [SDS(a.shape, a.dtype) for a in self.arrays]

    def run(self, srcs, outs, sems, stage):
        fn = _two_level_gather if self.kind == "gather" else _exchange
        fn(srcs, outs, *sems, stage=stage)


def _call_beside(body, transfer, *, grid, in_specs, out_specs, out_shape, scratch_shapes, name, semantics, args):
    if transfer is None:
        res = pl.pallas_call(body, grid=grid, in_specs=in_specs, out_specs=out_specs, out_shape=out_shape,
                             scratch_shapes=scratch_shapes, name=name, compiler_params=_params(semantics))(*args)
        return list(res), []
    n_in, n_out, n_s, n = len(in_specs), len(out_specs), len(scratch_shapes), transfer.n

    def wrapped(*refs):
        ins, refs = refs[:n_in], refs[n_in:]
        t_in, refs = refs[:n], refs[n:]
        outs, refs = refs[:n_out], refs[n_out:]
        t_out, refs = refs[:n], refs[n:]
        scratch, sems = refs[:n_s], refs[n_s:]
        first = functools.reduce(jnp.logical_and, [pl.program_id(i) == 0 for i in range(len(grid))])
        last = functools.reduce(jnp.logical_and, [pl.program_id(i) == g - 1 for i, g in enumerate(grid)])

        @pl.when(first)
        def _():
            transfer.run(t_in, t_out, sems, "start")

        body(*ins, *outs, *scratch)

        @pl.when(last)
        def _():
            transfer.run(t_in, t_out, sems, "finish")

    res = pl.pallas_call(
        wrapped, grid=grid, in_specs=list(in_specs) + _any_specs(n), out_specs=list(out_specs) + _any_specs(n),
        out_shape=list(out_shape) + transfer.out_shapes(), scratch_shapes=list(scratch_shapes) + _comm_scratch(n),
        name=name, compiler_params=_params(semantics))(*args, *transfer.arrays)
    return list(res[:n_out]), list(res[n_out:])


EXCHANGE_FLIPS = ((0, 0, 1), (1, 0, 0), (0, 1, 0), (1, 1, 0), (1, 0, 1), (0, 1, 1), (1, 1, 1))


def _exchange(srcs, outs, send_sems, recv_sems, local_sems, stage="all"):
    mx, my, mc = lax.axis_index("x"), lax.axis_index("y"), lax.axis_index("c")
    arrays = range(len(srcs))
    copies = [pltpu.make_async_copy(srcs[a].at[4 * mx + 2 * my + mc], outs[a].at[N_DEV - 1], local_sems.at[a])
              for a in arrays]
    for k, (fx, fy, fc) in enumerate(EXCHANGE_FLIPS):
        px = 1 - mx if fx else mx
        py = 1 - my if fy else my
        pc = 1 - mc if fc else mc
        for a in arrays:
            sem = a * COPIES_PER_ARRAY + k
            copies.append(pltpu.make_async_remote_copy(
                src_ref=srcs[a].at[4 * px + 2 * py + pc], dst_ref=outs[a].at[k],
                send_sem=send_sems.at[sem], recv_sem=recv_sems.at[sem],
                device_id=(px, py, pc), device_id_type=MESH_ID))
    if stage in ("all", "start"):
        for cp in copies:
            cp.start()
    if stage in ("all", "finish"):
        for cp in copies:
            cp.wait()


def _w_in_to_padded(w):
    z = lambda n: jnp.zeros((w.shape[0], n), w.dtype)
    return jnp.concatenate([w[:, O_GQKV:O_GZ], w[:, O_GZ:O_GAB], w[:, O_QLAT:O_KVLAT], w[:, O_KVLAT:O_KPE],
                            w[:, O_KPE:O_GQKV], z(P_GAB - P_KPE - ROPE), w[:, O_GAB:O_END],
                            z(P_WIDTH - P_GAB - (O_END - O_GAB))], axis=1)


def _w_in_from_padded(wp):
    return jnp.concatenate([wp[:, P_QLAT:P_QLAT + 256], wp[:, P_KVLAT:P_KVLAT + 256], wp[:, P_KPE:P_KPE + ROPE],
                            wp[:, P_GQKV:P_GZ], wp[:, P_GZ:P_QLAT], wp[:, P_GAB:P_GAB + (O_END - O_GAB)]], axis=1)


W_IN_SHARD_COLS = (O_END - O_QLAT) // N_DEV


def _w_in_shards_to_padded(stack):
    _, R, Cw = stack.shape
    tr = min(R, 256)

    def body(s_ref, o_ref):
        full = jnp.concatenate([s_ref[d].astype(F32)[:, :W_IN_SHARD_COLS] for d in range(N_DEV)], axis=-1)
        o_ref[...] = _w_in_to_padded(full).astype(o_ref.dtype)

    return pl.pallas_call(
        body, grid=(R // tr,), name="w_in_to_padded",
        in_specs=[pl.BlockSpec((N_DEV, tr, Cw), lambda i: (0, i, 0))],
        out_specs=pl.BlockSpec((tr, P_WIDTH), lambda i: (i, 0)),
        out_shape=SDS((R, P_WIDTH), stack.dtype), compiler_params=_params(("arbitrary",)),
    )(stack)


def _w_in_padded_to_slabs(gp, wire_cols):
    R = gp.shape[0]
    tr = min(R, 256)

    def body(g_ref, o_ref):
        orig = _w_in_from_padded(g_ref[...].astype(F32))
        for d in range(N_DEV):
            piece = orig[:, d * W_IN_SHARD_COLS:(d + 1) * W_IN_SHARD_COLS]
            o_ref[d] = _pad2(piece, tr, wire_cols).astype(o_ref.dtype)

    return pl.pallas_call(
        body, grid=(R // tr,), name="w_in_to_slabs",
        in_specs=[pl.BlockSpec((tr, P_WIDTH), lambda i: (i, 0))],
        out_specs=pl.BlockSpec((N_DEV, tr, wire_cols), lambda i: (0, i, 0)),
        out_shape=SDS((N_DEV, R, wire_cols), gp.dtype), compiler_params=_params(("arbitrary",)),
    )(gp)


def _w_uq_to_headsplit(w):
    w3 = w.reshape(w.shape[0], MLA_HEADS, QK_DIM)
    return jnp.concatenate([w3[:, :, :NOPE].reshape(w.shape[0], -1), w3[:, :, NOPE:].reshape(w.shape[0], -1)], axis=1)


def _w_uq_from_headsplit(wp):
    n = wp[:, :MLA_HEADS * NOPE].reshape(wp.shape[0], MLA_HEADS, NOPE)
    p = wp[:, MLA_HEADS * NOPE:].reshape(wp.shape[0], MLA_HEADS, ROPE)
    return jnp.concatenate([n, p], axis=2).reshape(wp.shape[0], -1)


def _lane_vec(v4):
    return jnp.pad(v4.reshape(1, -1), ((0, 0), (0, LANES - v4.shape[-1])))


def _local_step(x, positions, target, attn_norm_w, w_in, q_lat_norm_w, w_uq, kv_lat_norm_w, w_ukv, q_norm_w,
                k_norm_w, mla_out_norm_w, conv_w, a_log, dt_bias, gdn_norm_w, w_out, mlp_norm_w, w_up, w_down,
                late_shards=None, exchange=False):
    B, S, D = x.shape
    T = B * S
    x2 = x.reshape(T, D)
    t2 = target.reshape(T, D)
    half = ROPE // 2
    inv_freq = ROPE_THETA ** (-jnp.arange(half, dtype=F32) / half)
    ang = positions.reshape(T, 1).astype(F32) * inv_freq
    cosf = jnp.concatenate([jnp.cos(ang)] * 2, axis=-1)
    sinf = jnp.concatenate([jnp.sin(ang)] * 2, axis=-1)
    w_in_p = w_in
    w_uq_p = _w_uq_to_headsplit(w_uq)
    alog_l, dt_l = _lane_vec(a_log), _lane_vec(dt_bias)
    w_an, w_qln, w_kvln, qnw, knw, w_mn, gdn_w = (
        attn_norm_w, q_lat_norm_w, kv_lat_norm_w, q_norm_w, k_norm_w, mlp_norm_w, gdn_norm_w)

    proj, xn = _in_proj(x2, w_an, w_in_p)
    gather = None if late_shards is None else _Transfer("gather", late_shards[:1])
    (q4, k4, v4), late = _mla_pre(proj, cosf, sinf, w_qln, w_kvln, w_uq_p, w_ukv, qnw, knw, gather)
    if late:
        w_out = late[0].reshape(-1, D)
    (o_mla, lse), _ = _attn_fwd(q4, k4, v4, B, S)
    qg, kg, vg, gates = _gdn_pre(proj, conv_w, alog_l, dt_l, S)
    gather = None if late_shards is None else _Transfer("gather", late_shards[1:])
    (o_gdn, states, ainv, u4, w4), late = _gdn_fwd(qg, kg, vg, gates, B, S, gather)
    if late:
        w_up, w_down = late[0], late[1].reshape(-1, D)
    h2, mix = _mix_out(o_mla, o_gdn, proj, x2, mla_out_norm_w, gdn_w, w_out)
    up, hn, dy, sq = _mlp_fwd(h2, w_mn, w_up, w_down, t2)
    loss = (0.5 / D) * jnp.sum(sq[:, 0, 0])

    dh, dhb, dup, act, dyb, d_mlp_norm = _mlp_bwd(dy, up, h2, w_mn, w_up, w_down)
    g_w_down = _wgrad(act, dyb, "wgrad_down")
    g_w_up = _wgrad(hn, dup, "wgrad_up", column_shards=True)
    do_mla, do_gdn, dz, d_mla_w, d_gdn_w, delta = _mix_bwd(dhb, o_mla, o_gdn, proj, mla_out_norm_w, gdn_w, w_out)
    g_w_out = _wgrad(mix, dhb, "wgrad_out")
    first = ("w_down",)
    second = ("w_out",)
    third = ("w_up", "w_uq", "w_ukv")
    mats = dict(w_up=g_w_up, w_down=g_w_down, w_out=g_w_out)

    def sending(names):
        return _Transfer("exchange", [_slabs(n, mats[n]) for n in names]) if exchange else None

    (dq4, dk4, dv4), got = _attn_bwd(q4, k4, v4, do_mla, delta, lse, B, S, sending(first))
    mats.update(zip(first, got))
    (dql, dkvl, dkpe, dqraw, dkvraw, qn, kvn, d_wqln, d_wkvln, d_qnw, d_knw), got = _mla_pre_bwd(
        proj, cosf, sinf, w_qln, w_kvln, w_uq_p, w_ukv, qnw, knw, dq4, dk4, dv4, sending(second))
    mats.update(zip(second, got))
    mats.update(w_uq=_wgrad(qn, dqraw, "wgrad_uq"), w_ukv=_wgrad(kvn, dkvraw, "wgrad_ukv"))
    (dqg, dkg, dvg, dgb4), got = _gdn_bwd(qg, kg, vg, gates, states, ainv, u4, w4, do_gdn, B, S, sending(third))
    mats.update(zip(third, got))
    dc, dgab, g_conv, d_alog, d_dt = _gdn_pre_bwd(proj, conv_w, alog_l, dt_l, dqg, dkg, dvg, dgb4, S)
    grad_x2, dproj, d_attn_norm = _in_proj_bwd(dc, conv_w, dz, dql, dkvl, dkpe, dgab, w_in_p, dh, x2, w_an, S)
    mats.update(w_in=_wgrad(xn, dproj, "wgrad_in"), conv_w=g_conv)
    if exchange:
        last = ("w_in", "conv_w")
        mats.update(zip(last, _exchange_grads([_slabs(n, mats[n]) for n in last])))
    small = dict(attn_norm_w=d_attn_norm, mlp_norm_w=d_mlp_norm, q_lat_norm_w=d_wqln, kv_lat_norm_w=d_wkvln,
                 q_norm_w=d_qnw, k_norm_w=d_knw, mla_out_norm_w=d_mla_w, a_log=d_alog, dt_bias=d_dt,
                 gdn_norm_w=d_gdn_w)
    return loss, grad_x2.reshape(B, S, D), mats, [small[n] for n, *_ in SMALL_LAYOUT]


BIG = ("w_in", "w_uq", "w_ukv", "conv_w", "w_out", "w_up", "w_down")
ALL_W = ("attn_norm_w", "w_in", "q_lat_norm_w", "w_uq", "kv_lat_norm_w", "w_ukv", "q_norm_w", "k_norm_w",
         "mla_out_norm_w", "conv_w", "a_log", "dt_bias", "gdn_norm_w", "w_out", "mlp_norm_w", "w_up", "w_down")
WIRE_SHAPE = {"w_in": (1024, 384), "w_uq": (256, 128), "conv_w": (16, 256)}


def _pad2(a, rows, cols):
    return jnp.pad(a, [(0, 0)] * (a.ndim - 2) + [(0, rows - a.shape[-2]), (0, cols - a.shape[-1])])


def _cols_to_full(stack, cols):
    return jnp.moveaxis(stack[:, :, :cols], 0, 1).reshape(stack.shape[1], N_DEV * cols)


def _full_to_cols(full, wire_cols):
    r, n = full.shape
    return _pad2(jnp.moveaxis(full.reshape(r, N_DEV, n // N_DEV), 1, 0), r, wire_cols)


def _slabs(name, g):
    if name == "w_in":
        return _w_in_padded_to_slabs(g, WIRE_SHAPE["w_in"][1])
    if name == "w_uq":
        return _full_to_cols(_w_uq_from_headsplit(g), WIRE_SHAPE["w_uq"][1])
    if name == "w_ukv":
        return _full_to_cols(g, g.shape[1] // N_DEV)
    if name == "conv_w":
        return _pad2(_full_to_cols(g.astype(WIRE_DTYPE), g.shape[1] // N_DEV), *WIRE_SHAPE["conv_w"])
    if name == "w_up":
        return g
    return g.reshape(N_DEV, -1, g.shape[-1])


def kernel(x, positions, attn_norm_w, w_in, q_lat_norm_w, w_uq, kv_lat_norm_w, w_ukv, q_norm_w, k_norm_w, mla_out_norm_w, conv_w, a_log, dt_bias, gdn_norm_w, w_out, mlp_norm_w, w_up, w_down, loss_target, m_attn_norm_w, m_w_in, m_q_lat_norm_w, m_w_uq, m_kv_lat_norm_w, m_w_ukv, m_q_norm_w, m_k_norm_w, m_mla_out_norm_w, m_conv_w, m_a_log, m_dt_bias, m_gdn_norm_w, m_w_out, m_mlp_norm_w, m_w_up, m_w_down, v_attn_norm_w, v_w_in, v_q_lat_norm_w, v_w_uq, v_kv_lat_norm_w, v_w_ukv, v_q_norm_w, v_k_norm_w, v_mla_out_norm_w, v_conv_w, v_a_log, v_dt_bias, v_gdn_norm_w, v_w_out, v_mlp_norm_w, v_w_up, v_w_down):
    env = dict(locals())
    W = {n: env[n][0] for n in ALL_W}
    Mo = {n: env["m_" + n][0] for n in ALL_W}
    Vo = {n: env["v_" + n][0] for n in ALL_W}

    two_d = lambda a: a.reshape(1, -1) if a.ndim == 1 else a
    D = x.shape[-1]

    s_in, s_uq, s_ukv, s_conv = _gather_weights([
        _pad2(W["w_in"].astype(WIRE_DTYPE), *WIRE_SHAPE["w_in"]),
        _pad2(W["w_uq"].astype(WIRE_DTYPE), *WIRE_SHAPE["w_uq"]),
        W["w_ukv"].astype(WIRE_DTYPE), _pad2(W["conv_w"], *WIRE_SHAPE["conv_w"])])
    late = [W["w_out"].astype(WIRE_DTYPE), W["w_up"].astype(WIRE_DTYPE), W["w_down"].astype(WIRE_DTYPE)]

    loss, grad_x, parts, gs = _local_step(
        x, positions, loss_target, two_d(W["attn_norm_w"]), _w_in_shards_to_padded(s_in),
        two_d(W["q_lat_norm_w"]), _cols_to_full(s_uq, W["w_uq"].shape[1]), two_d(W["kv_lat_norm_w"]),
        _cols_to_full(s_ukv, W["w_ukv"].shape[1]), two_d(W["q_norm_w"]), two_d(W["k_norm_w"]),
        W["mla_out_norm_w"], _cols_to_full(s_conv[:, :CONV_W], W["conv_w"].shape[1]), two_d(W["a_log"]),
        two_d(W["dt_bias"]), two_d(W["gdn_norm_w"]), None, two_d(W["mlp_norm_w"]), None, None,
        late_shards=late, exchange=True)
    done = {n: _reduce_adamw(parts[n], W[n], Mo[n], Vo[n], "adamw_" + n) for n in BIG}
    names = [n for n, *_ in SMALL_LAYOUT]
    tiles = _gather_small_grads(gs, jnp.full((1, LANES), loss, F32))
    small, loss = _adamw_replicated(tiles, [two_d(W[n]) for n in names], [two_d(Mo[n]) for n in names],
                                    [two_d(Vo[n]) for n in names])
    for i, n in enumerate(names):
        done[n] = [small[kind][i] for kind in range(4)]
    res = [done[n][kind].reshape(env[n].shape) for kind in range(4) for n in ALL_W]
    return (loss, grad_x, *res)
```

```python
import functools

import jax
import jax.numpy as jnp
from jax import lax
from jax.experimental import pallas as pl
from jax.experimental.pallas import tpu as pltpu

F32 = jnp.float32
MXU_DTYPE = jnp.bfloat16
WIRE_DTYPE = jnp.bfloat16
SDS = jax.ShapeDtypeStruct
HIGHEST = lax.Precision.HIGHEST
MESH_ID = pl.DeviceIdType.MESH

D_MODEL = 1024
MLA_HEADS = 4
Q_LORA = 256
KV_LORA = 256
NOPE = 128
ROPE = 64
QK_DIM = NOPE + ROPE
V_DIM = 128
ROPE_THETA = 10000.0
GDN_HEADS = 4
GDN_DIM = 128
GDN_WIDTH = GDN_HEADS * GDN_DIM
CONV_W = 4
CHUNK = 64
D_FF = 4 * D_MODEL
EPS = 1e-6
ATT_SCALE = QK_DIM ** -0.5
GDN_QSCALE = GDN_DIM ** -0.5
N_DEV = 8
ATTN_BLOCK = 512
ATTN_CHAINS = 2
MLP_FWD_SHARDS = 4
MLP_BWD_SHARDS = 4

ADAM_LR = 0.001
ADAM_B1 = 0.9
ADAM_B2 = 0.999
ADAM_EPS = 1e-08
ADAM_WD = 0.01
ADAM_STEP = 10

LANES = 128
SUBLANES = 8
VMEM_LIMIT = 60 * 1024 * 1024

P_GQKV, P_GZ, P_QLAT, P_KVLAT, P_KPE, P_GAB = 0, 1536, 2048, 2304, 2560, 2688
P_WIDTH = 2816
O_QLAT, O_KVLAT, O_KPE, O_GQKV, O_GZ, O_GAB, O_END = 0, 256, 512, 576, 2112, 2624, 2632


def _params(sem=None, vmem=VMEM_LIMIT):
    kw = dict(vmem_limit_bytes=vmem)
    if sem is not None:
        kw["dimension_semantics"] = sem
    return pltpu.CompilerParams(**kw)


def _mm(a, b):
    return jnp.dot(a.astype(MXU_DTYPE), b.astype(MXU_DTYPE), preferred_element_type=F32)


def _mm_nt(a, b):
    return lax.dot_general(a.astype(MXU_DTYPE), b.astype(MXU_DTYPE), (((1,), (1,)), ((), ())),
                           preferred_element_type=F32)


def _mm_tn(a, b):
    return lax.dot_general(a.astype(MXU_DTYPE), b.astype(MXU_DTYPE), (((0,), (0,)), ((), ())),
                           preferred_element_type=F32)


def _split(a):
    hi = a.astype(MXU_DTYPE)
    return hi, (a - hi.astype(F32)).astype(MXU_DTYPE)


def _mm_split(a, b):
    (ah, al), (bh, bl) = a, b
    dot = lambda x, y: jnp.dot(x, y, preferred_element_type=F32)
    if MXU_DTYPE == F32:
        return dot(ah, bh)
    return dot(ah, bh) + dot(ah, bl) + dot(al, bh)


def _mm_exact(a, b):
    return _mm_split(_split(a), _split(b))


def _row_sum(v, on_mxu=False):
    if not on_mxu:
        return jnp.sum(v, axis=-1, keepdims=True)
    d = v.shape[-1]
    ones = jnp.ones((d, LANES), MXU_DTYPE)
    s = sum(jnp.dot(p, ones, preferred_element_type=F32) for p in _split(v))
    return s[:, :d] if d <= LANES else jnp.tile(s, (1, d // LANES))


def _rms(x, w, on_mxu=False):
    r = lax.rsqrt(_row_sum(x * x, on_mxu) * (1.0 / x.shape[-1]) + EPS)
    return x * r * w, r


def _rms_bwd(dy, x, w, r, on_mxu=False):
    xh = x * r
    dyw = dy * w
    dx = r * (dyw - xh * (_row_sum(dyw * xh, on_mxu) * (1.0 / x.shape[-1])))
    dw = jnp.sum(dy * xh, axis=0, keepdims=True)
    return dx, dw


def _l2n(x, scale):
    return x * (lax.rsqrt(_row_sum(x * x) + EPS) * scale)


def _l2n_bwd(dy, x, scale):
    r = lax.rsqrt(_row_sum(x * x) + EPS)
    xh = x * r
    return (scale * r) * (dy - xh * _row_sum(dy * xh))


def _rot(t):
    return jnp.concatenate([-t[:, ROPE // 2:], t[:, :ROPE // 2]], axis=-1)


def _rot_t(t):
    return jnp.concatenate([t[:, ROPE // 2:], -t[:, :ROPE // 2]], axis=-1)


def _rope(t, cos, sin):
    return t * cos + _rot(t) * sin


def _rope_bwd(d, cos, sin):
    return d * cos + _rot_t(d * sin)


def _sigmoid(x):
    return jax.nn.sigmoid(x)


def _shift_down(x, halo, j):
    if j == 0:
        return x
    xr = pltpu.roll(x, j, 0)
    hr = pltpu.roll(halo, j, 0)
    row = lax.broadcasted_iota(jnp.int32, halo.shape, 0)
    top = jnp.where(row < j, hr, xr[:SUBLANES])
    return jnp.concatenate([top, xr[SUBLANES:]], axis=0)


def _shift_up(x, nxt, j):
    if j == 0:
        return x
    n = x.shape[0]
    xr = pltpu.roll(x, n - j, 0)
    nr = pltpu.roll(nxt, SUBLANES - j, 0)
    row = lax.broadcasted_iota(jnp.int32, nxt.shape, 0)
    bot = jnp.where(row >= SUBLANES - j, nr, xr[n - SUBLANES:])
    return jnp.concatenate([xr[:n - SUBLANES], bot], axis=0)


def _chunk_cumsum(y, row_in_chunk):
    s = 1
    while s < CHUNK:
        y = y + jnp.where(row_in_chunk >= s, pltpu.roll(y, s, 0), 0.0)
        s *= 2
    return y


def _chunk_rev_cumsum(y, row_in_chunk):
    n = y.shape[0]
    s = 1
    while s < CHUNK:
        y = y + jnp.where(row_in_chunk + s < CHUNK, pltpu.roll(y, n - s, 0), 0.0)
        s *= 2
    return y


def _together(generators):
    alive = list(generators)
    while alive:
        nxt = []
        for g in alive:
            try:
                next(g)
                nxt.append(g)
            except StopIteration:
                pass
        alive = nxt
        yield


def _lockstep(generators):
    for _ in _together(generators):
        pass


def _pick_lane(tile, lane, idx):
    return jnp.sum(jnp.where(lane == idx, tile, 0.0), axis=-1, keepdims=True)


def _divisor_tile(n, cap, unit=LANES):
    best = unit
    t = unit
    while t <= min(n, cap):
        if n % t == 0:
            best = t
        t += unit
    return n if n <= cap else best


def _in_proj(x2, w_an, w_in_p):
    T, D = x2.shape
    N = w_in_p.shape[1]
    tm = min(512, T)

    def body(x_ref, wn_ref, w_ref, proj_ref, xn_ref):
        xn, _ = _rms(x_ref[...], wn_ref[...])
        xn = xn.astype(MXU_DTYPE)
        xn_ref[...] = xn
        proj_ref[...] = jnp.dot(xn, w_ref[...], preferred_element_type=F32)

    return pl.pallas_call(
        body, grid=(T // tm,), name="in_proj",
        in_specs=[pl.BlockSpec((tm, D), lambda i: (i, 0)), pl.BlockSpec((1, D), lambda i: (0, 0)),
                  pl.BlockSpec((D, N), lambda i: (0, 0))],
        out_specs=[pl.BlockSpec((tm, N), lambda i: (i, 0)), pl.BlockSpec((tm, D), lambda i: (i, 0))],
        out_shape=[SDS((T, N), F32), SDS((T, D), MXU_DTYPE)],
        compiler_params=_params(("arbitrary",)),
    )(x2, w_an, w_in_p)


def _mla_pre(proj, cosf, sinf, w_qln, w_kvln, w_uq_p, w_ukv, qnw, knw, transfer=None):
    T = proj.shape[0]
    tm = min(256, T)
    H = MLA_HEADS

    def body(ql_ref, kvl_ref, kpe_ref, cos_ref, sin_ref, wq_ref, wkv_ref, uq_ref, ukv_ref, qnw_ref, knw_ref,
             q_out, k_out, v_out):
        rms = functools.partial(_rms, on_mxu=True)
        cos, sin = cos_ref[...], sin_ref[...]
        qnw_, knw_ = qnw_ref[...], knw_ref[...]
        qn, _ = rms(ql_ref[...], wq_ref[...])
        kvn, _ = rms(kvl_ref[...], wkv_ref[...])
        qraw = _mm(qn, uq_ref[...])
        kvraw = _mm(kvn, ukv_ref[...])
        kpe = _rope(rms(kpe_ref[...][:, :ROPE], knw_[:, NOPE:])[0], cos, sin)
        for h in range(H):
            qn_h = rms(qraw[:, h * NOPE:(h + 1) * NOPE], qnw_[:, :NOPE])[0]
            qp_h = _rope(rms(qraw[:, H * NOPE + h * ROPE:H * NOPE + (h + 1) * ROPE], qnw_[:, NOPE:])[0], cos, sin)
            q_out[h] = (jnp.concatenate([qn_h, qp_h], axis=-1) * ATT_SCALE).astype(MXU_DTYPE)
            kn_h = rms(kvraw[:, h * 256:h * 256 + NOPE], knw_[:, :NOPE])[0]
            k_out[h] = jnp.concatenate([kn_h, kpe], axis=-1).astype(MXU_DTYPE)
            v_out[h] = kvraw[:, h * 256 + NOPE:(h + 1) * 256].astype(MXU_DTYPE)

    full = lambda a: pl.BlockSpec(a.shape, lambda i: (0,) * a.ndim)
    return _call_beside(
        body, transfer, grid=(T // tm,), name="mla_pre", scratch_shapes=[], semantics=("arbitrary",),
        args=(proj, proj, proj, cosf, sinf, w_qln, w_kvln, w_uq_p, w_ukv, qnw, knw),
        in_specs=[pl.BlockSpec((tm, 256), lambda i: (i, P_QLAT // 256)),
                  pl.BlockSpec((tm, 256), lambda i: (i, P_KVLAT // 256)),
                  pl.BlockSpec((tm, 128), lambda i: (i, P_KPE // 128)),
                  pl.BlockSpec((tm, ROPE), lambda i: (i, 0)), pl.BlockSpec((tm, ROPE), lambda i: (i, 0)),
                  full(w_qln), full(w_kvln), full(w_uq_p), full(w_ukv), full(qnw), full(knw)],
        out_specs=[pl.BlockSpec((H, tm, QK_DIM), lambda i: (0, i, 0)),
                   pl.BlockSpec((H, tm, QK_DIM), lambda i: (0, i, 0)),
                   pl.BlockSpec((H, tm, V_DIM), lambda i: (0, i, 0))],
        out_shape=[SDS((H, T, QK_DIM), MXU_DTYPE), SDS((H, T, QK_DIM), MXU_DTYPE), SDS((H, T, V_DIM), MXU_DTYPE)])


def _attn_fwd(q4, k4, v4, B, S, transfer=None):
    H = MLA_HEADS
    bq = min(ATTN_BLOCK, S)
    nq = S // bq
    rows = bq // ATTN_CHAINS

    def body(q_ref, k_ref, v_ref, o_ref, lse_ref):
        col = lax.broadcasted_iota(jnp.int32, (rows, bq), 1)
        row = lax.broadcasted_iota(jnp.int32, (rows, bq), 0)

        def q_step(qi, carry):
            qs = pl.multiple_of(qi * bq, bq)
            qsub = [q_ref[0, pl.ds(qs + j * rows, rows), :] for j in range(ATTN_CHAINS)]

            def k_block(ks, cs, diagonal):
                k = k_ref[0, pl.ds(ks, bq), :]
                v = v_ref[0, pl.ds(ks, bq), :]
                out = [None] * ATTN_CHAINS

                def chain(j):
                    m, l, acc = cs[j]
                    s = _mm_nt(qsub[j], k)
                    yield
                    if diagonal:
                        s = jnp.where(col <= row + j * rows, s, -jnp.inf)
                    m_new = jnp.maximum(m, jnp.max(s, axis=-1, keepdims=True))
                    p = jnp.exp(s - m_new)
                    a = jnp.exp(m - m_new)
                    l_new = a * l + jnp.sum(p, axis=-1, keepdims=True)
                    yield
                    out[j] = (m_new, l_new, a * acc + _mm(p, v))

                _lockstep([chain(j) for j in range(ATTN_CHAINS)])
                return tuple(out)

            init = tuple((jnp.full((rows, 1), -jnp.inf, F32), jnp.zeros((rows, 1), F32),
                          jnp.zeros((rows, V_DIM), F32)) for _ in range(ATTN_CHAINS))
            cs = lax.fori_loop(0, qi, lambda kj, c: k_block(pl.multiple_of(kj * bq, bq), c, False), init)
            for j, (m, l, acc) in enumerate(k_block(qs, cs, True)):
                o_ref[0, pl.ds(qs + j * rows, rows), :] = acc / l
                lse_ref[0, pl.ds(qs + j * rows, rows), :] = m + jnp.log(l)
            return carry

        lax.fori_loop(0, nq, q_step, 0)

    spec = lambda d: pl.BlockSpec((1, S, d), lambda h, b: (h, b, 0))
    return _call_beside(
        body, transfer, grid=(H, B), name="attn_fwd",
        in_specs=[spec(QK_DIM), spec(QK_DIM), spec(V_DIM)],
        out_specs=[spec(V_DIM), spec(1)],
        out_shape=[SDS((H, B * S, V_DIM), F32), SDS((H, B * S, 1), F32)],
        scratch_shapes=[], semantics=("arbitrary", "arbitrary"), args=(q4, k4, v4))


def _conv_taps(u, halo, w):
    sh = [_shift_down(u, halo, j) for j in range(CONV_W)]
    c = w[0:1] * sh[3] + w[1:2] * sh[2] + w[2:3] * sh[1] + w[3:4] * sh[0]
    return c, sh


def _gate_values(gab, alog_l, dt_l, lane):
    g = -jnp.exp(alog_l) * jax.nn.softplus(gab + dt_l)
    g = jnp.where(lane < GDN_HEADS, g, 0.0)
    beta = jnp.where((lane >= GDN_HEADS) & (lane < 2 * GDN_HEADS), _sigmoid(gab), 0.0)
    return g, beta


def _gdn_pre(proj, conv_w, alog_l, dt_l, S):
    T = proj.shape[0]
    tm = min(256, T)
    tiles_per_seq = S // tm
    C3 = 3 * GDN_WIDTH
    H = GDN_HEADS

    def body(u_ref, halo_ref, gab_ref, w_ref, alog_ref, dt_ref, q_out, k_out, v_out, gates_out):
        i = pl.program_id(0)
        halo = jnp.where(i % tiles_per_seq == 0, 0.0, halo_ref[...])
        c, _ = _conv_taps(u_ref[...], halo, w_ref[...])
        a = c * _sigmoid(c)
        for h in range(H):
            xq = a[:, h * GDN_DIM:(h + 1) * GDN_DIM]
            xk = a[:, GDN_WIDTH + h * GDN_DIM:GDN_WIDTH + (h + 1) * GDN_DIM]
            q_out[h] = _l2n(xq, GDN_QSCALE)
            k_out[h] = _l2n(xk, 1.0)
            v_out[h] = a[:, 2 * GDN_WIDTH + h * GDN_DIM:2 * GDN_WIDTH + (h + 1) * GDN_DIM]
        lane = lax.broadcasted_iota(jnp.int32, (tm, LANES), 1)
        ric = lax.broadcasted_iota(jnp.int32, (tm, LANES), 0) % CHUNK
        g, beta = _gate_values(gab_ref[...], alog_ref[...], dt_ref[...], lane)
        gates_out[...] = _chunk_cumsum(g, ric) + beta

    hspec = pl.BlockSpec((H, tm, GDN_DIM), lambda i: (0, i, 0))
    return pl.pallas_call(
        body, grid=(T // tm,), name="gdn_pre",
        in_specs=[pl.BlockSpec((tm, C3), lambda i: (i, 0)),
                  pl.BlockSpec((SUBLANES, C3), lambda i: (jnp.maximum(i * (tm // SUBLANES) - 1, 0), 0)),
                  pl.BlockSpec((tm, LANES), lambda i: (i, P_GAB // LANES)),
                  pl.BlockSpec((CONV_W, C3), lambda i: (0, 0)),
                  pl.BlockSpec((1, LANES), lambda i: (0, 0)), pl.BlockSpec((1, LANES), lambda i: (0, 0))],
        out_specs=[hspec, hspec, hspec, pl.BlockSpec((tm, LANES), lambda i: (i, 0))],
        out_shape=[SDS((H, T, GDN_DIM), F32)] * 3 + [SDS((T, LANES), F32)],
        compiler_params=_params(("arbitrary",)),
    )(proj, proj, proj, conv_w, alog_l, dt_l)


def _unit_lower_inverses(Ls, eye):
    Ps = [eye - L for L in Ls]
    Ms = [_split(-L) for L in Ls]
    for _ in range(5):
        sq = [_mm_split(m, m) for m in Ms]
        Ms = [_split(s) for s in sq]
        Ps = [p + _mm_split(_split(p), m) for p, m in zip(Ps, Ms)]
    return Ps


def _chunk_decays(gt, lane, h, ri, ci, rcol):
    Gc = _pick_lane(gt, lane, h)
    bt = _pick_lane(gt, lane, h + GDN_HEADS)
    Gb = jnp.broadcast_to(Gc, (CHUNK, CHUNK))
    Gam = jnp.where(ri >= ci, jnp.exp(Gb - Gb.T), 0.0)
    Gl = jnp.sum(jnp.where(rcol == CHUNK - 1, Gc, 0.0), axis=0, keepdims=True)
    return Gc, bt, Gam, jnp.exp(Gc), jnp.exp(Gl - Gc), jnp.exp(Gl)


GDN_FWD_UNROLL = 16
GDN_BWD_UNROLL = 8
GDN_RECUR_STEPS_PER_STAGE = 2


def _gdn_fwd(qg, kg, vg, gates, B, S, transfer=None):
    H, D, C = GDN_HEADS, GDN_DIM, CHUNK
    NC = S // C
    P = 2 if B % 2 == 0 else 1
    Sb, NCb = P * S, P * NC
    U = GDN_FWD_UNROLL if NCb % GDN_FWD_UNROLL == 0 else 1
    NG = NCb // U

    def body(q_ref, k_ref, v_ref, g_ref, o_ref, st_ref, ai_ref, u_ref, w_ref, q2_s, au_s, bc_s, w2_s, el_s):
        h = pl.program_id(0)
        lane = lax.broadcasted_iota(jnp.int32, (C, LANES), 1)
        ri = lax.broadcasted_iota(jnp.int32, (C, C), 0)
        ci = lax.broadcasted_iota(jnp.int32, (C, C), 1)
        rcol = lax.broadcasted_iota(jnp.int32, (C, 1), 0)
        eye = (ri == ci).astype(F32)

        def group(gi, c):
            ns = [gi * U + j for j in range(U)]
            css = [pl.multiple_of(n * C, C) for n in ns]
            qs = [q_ref[0, pl.ds(cs, C), :] for cs in css]
            ks = [k_ref[0, pl.ds(cs, C), :] for cs in css]
            vs = [v_ref[0, pl.ds(cs, C), :] for cs in css]
            decs = [_chunk_decays(g_ref[pl.ds(cs, C), :], lane, h, ri, ci, rcol) for cs in css]
            qks = [_mm_nt(jnp.concatenate([q, k], axis=0), k) for q, k in zip(qs, ks)]
            ainvs = _unit_lower_inverses(
                [jnp.where(ri > ci, d[1] * qk[C:] * d[2], 0.0) for qk, d in zip(qks, decs)], eye)
            sols = [_mm_exact(a, jnp.concatenate([v * d[1], k * (d[1] * d[3])], axis=-1))
                    for a, k, v, d in zip(ainvs, ks, vs, decs)]
            atuw = [_mm(qk[:C] * d[2], sol) for qk, d, sol in zip(qks, decs, sols)]
            kduw = [_mm_tn(k * d[4], sol) for k, d, sol in zip(ks, decs, sols)]
            for n, cs, q, a, sol, au, ku, (Gc, bt, Gam, e, f, eL) in zip(ns, css, qs, ainvs, sols, atuw, kduw, decs):
                u_ref[0, pl.ds(cs, C), :] = sol[:, :D]
                w_ref[0, pl.ds(cs, C), :] = sol[:, D:]
                au_s[pl.ds(cs, C), :] = au[:, :D]
                q2_s[pl.ds(cs, C), :] = q * e - au[:, D:]
                bc_s[n] = ku[:, :D]
                w2_s[n] = ku[:, D:]
                el_s[n] = jnp.broadcast_to(eL, (SUBLANES, LANES))
                ai_ref[0, n] = a.T
            return c

        lax.fori_loop(0, NG, group, 0)

        def step(n, states):
            new = []
            for p, S_ in enumerate(states):
                m = p * NC + n
                cs = pl.multiple_of(m * C, C)
                o_ref[0, pl.ds(cs, C), :] = _mm(q2_s[pl.ds(cs, C), :], S_) + au_s[pl.ds(cs, C), :]
                st_ref[0, m] = S_
                new.append(S_ * el_s[m, 0:1, :] + bc_s[m] - _mm(w2_s[m], S_))
            return tuple(new)

        lax.fori_loop(0, NC, step, tuple(jnp.zeros((D, D), F32) for _ in range(P)))

    spec = pl.BlockSpec((1, Sb, D), lambda h, b: (h, b, 0))
    return _call_beside(
        body, transfer, grid=(H, B // P), name="gdn_fwd",
        in_specs=[spec, spec, spec, pl.BlockSpec((Sb, LANES), lambda h, b: (b, 0))],
        out_specs=[spec, pl.BlockSpec((1, NCb, D, D), lambda h, b: (h, b, 0, 0)),
                   pl.BlockSpec((1, NCb, C, C), lambda h, b: (h, b, 0, 0)), spec, spec],
        out_shape=[SDS((H, B * S, D), F32), SDS((H, B * NC, D, D), F32), SDS((H, B * NC, C, C), F32),
                   SDS((H, B * S, D), F32), SDS((H, B * S, D), F32)],
        scratch_shapes=[pltpu.VMEM((Sb, D), F32), pltpu.VMEM((Sb, D), F32), pltpu.VMEM((NCb, D, D), F32),
                        pltpu.VMEM((NCb, D, D), F32), pltpu.VMEM((NCb, SUBLANES, LANES), F32)],
        semantics=("arbitrary", "arbitrary"), args=(qg, kg, vg, gates))


def _mix_out(o_mla, o_gdn, proj, x2, mla_w, gdn_w, w_out):
    T, D = x2.shape
    tm = min(512, T)
    H = MLA_HEADS

    def body(om_ref, og_ref, z_ref, x_ref, mw_ref, gw_ref, w_ref, h_ref, mix_ref):
        z = z_ref[...]
        parts = [_rms(om_ref[h], mw_ref[h:h + 1, :])[0] for h in range(H)]
        for h in range(GDN_HEADS):
            zh = z[:, h * GDN_DIM:(h + 1) * GDN_DIM]
            parts.append(_rms(og_ref[h], gw_ref[...])[0] * (zh * _sigmoid(zh)))
        mix = jnp.concatenate(parts, axis=-1).astype(MXU_DTYPE)
        mix_ref[...] = mix
        h_ref[...] = x_ref[...] + jnp.dot(mix, w_ref[...], preferred_element_type=F32)

    hspec = pl.BlockSpec((H, tm, V_DIM), lambda i: (0, i, 0))
    return pl.pallas_call(
        body, grid=(T // tm,), name="mix_out",
        in_specs=[hspec, hspec, pl.BlockSpec((tm, GDN_WIDTH), lambda i: (i, P_GZ // GDN_WIDTH)),
                  pl.BlockSpec((tm, D), lambda i: (i, 0)),
                  pl.BlockSpec((H, V_DIM), lambda i: (0, 0)), pl.BlockSpec((1, GDN_DIM), lambda i: (0, 0)),
                  pl.BlockSpec((D, D), lambda i: (0, 0))],
        out_specs=[pl.BlockSpec((tm, D), lambda i: (i, 0)), pl.BlockSpec((tm, D), lambda i: (i, 0))],
        out_shape=[SDS((T, D), F32), SDS((T, D), MXU_DTYPE)],
        compiler_params=_params(("arbitrary",)),
    )(o_mla, o_gdn, proj, x2, mla_w, gdn_w, w_out)


def _mlp_fwd(h2, w_mn, w_up, w_down, target):
    T, D = h2.shape
    ns, _, ts = w_up.shape
    F = ns * ts
    tm = min(512, T)
    G = MLP_FWD_SHARDS
    tf, nf = G * ts, ns // G

    def body(h_ref, wn_ref, up_w, down_w, t_ref, up_ref, hn_ref, dy_ref, loss_ref, y_acc):
        j = pl.program_id(1)

        @pl.when(j == 0)
        def _():
            hn_ref[...] = _rms(h_ref[...], wn_ref[...])[0].astype(MXU_DTYPE)
            y_acc[...] = h_ref[...]

        parts = []
        for c in range(G):
            up = jnp.dot(hn_ref[...], up_w[c], preferred_element_type=F32)
            up_ref[:, c * ts:(c + 1) * ts] = up.astype(MXU_DTYPE)
            r = jnp.maximum(up, 0.0)
            parts.append(_mm(r * r, down_w[c * ts:(c + 1) * ts, :]))
        y_acc[...] += functools.reduce(jnp.add, parts)

        @pl.when(j == nf - 1)
        def _():
            err = y_acc[...] - t_ref[...]
            dy_ref[...] = err / D
            loss_ref[...] = jnp.full((1, SUBLANES, LANES), jnp.sum(err * err), F32)

    return pl.pallas_call(
        body, grid=(T // tm, nf), name="mlp_fwd",
        in_specs=[pl.BlockSpec((tm, D), lambda i, j: (i, 0)), pl.BlockSpec((1, D), lambda i, j: (0, 0)),
                  pl.BlockSpec((G, D, ts), lambda i, j: (j, 0, 0)), pl.BlockSpec((tf, D), lambda i, j: (j, 0)),
                  pl.BlockSpec((tm, D), lambda i, j: (i, 0))],
        out_specs=[pl.BlockSpec((tm, tf), lambda i, j: (i, j)), pl.BlockSpec((tm, D), lambda i, j: (i, 0)),
                   pl.BlockSpec((tm, D), lambda i, j: (i, 0)),
                   pl.BlockSpec((1, SUBLANES, LANES), lambda i, j: (i, 0, 0))],
        out_shape=[SDS((T, F), MXU_DTYPE), SDS((T, D), MXU_DTYPE), SDS((T, D), F32),
                   SDS((T // tm, SUBLANES, LANES), F32)],
        scratch_shapes=[pltpu.VMEM((tm, D), F32)],
        compiler_params=_params(("arbitrary", "arbitrary")),
    )(h2, w_mn, w_up, w_down, target)


def _mlp_bwd(dy, up, h2, w_mn, w_up, w_down):
    T, D = h2.shape
    ns, _, ts = w_up.shape
    F = ns * ts
    tm = min(512, T)
    G = MLP_BWD_SHARDS
    tf, nf = G * ts, ns // G

    def body(dy_ref, up_ref, h_ref, wn_ref, up_w, down_w, dh_ref, dhb_ref, dup_ref, act_ref, dyb_ref, dwn_ref, acc):
        i, j = pl.program_id(0), pl.program_id(1)

        @pl.when((i == 0) & (j == 0))
        def _():
            dwn_ref[...] = jnp.zeros_like(dwn_ref)

        @pl.when(j == 0)
        def _():
            acc[...] = jnp.zeros_like(acc)
            dyb_ref[...] = dy_ref[...].astype(MXU_DTYPE)

        parts = []
        for c in range(G):
            cols = slice(c * ts, (c + 1) * ts)
            r = jnp.maximum(up_ref[:, cols].astype(F32), 0.0)
            act_ref[:, cols] = (r * r).astype(MXU_DTYPE)
            dup = (_mm_nt(dyb_ref[...], down_w[cols, :]) * (2.0 * r)).astype(MXU_DTYPE)
            dup_ref[:, cols] = dup
            parts.append(_mm_nt(dup, up_w[c]))
        acc[...] += functools.reduce(jnp.add, parts)

        @pl.when(j == nf - 1)
        def _():
            hv = h_ref[...]
            _, rr = _rms(hv, wn_ref[...])
            dx, dw = _rms_bwd(acc[...], hv, wn_ref[...], rr)
            dh = dy_ref[...] + dx
            dh_ref[...] = dh
            dhb_ref[...] = dh.astype(MXU_DTYPE)
            dwn_ref[...] += dw

    row = lambda i, j: (i, 0)
    return pl.pallas_call(
        body, grid=(T // tm, nf), name="mlp_bwd",
        in_specs=[pl.BlockSpec((tm, D), row), pl.BlockSpec((tm, tf), lambda i, j: (i, j)), pl.BlockSpec((tm, D), row),
                  pl.BlockSpec((1, D), lambda i, j: (0, 0)),
                  pl.BlockSpec((G, D, ts), lambda i, j: (j, 0, 0)), pl.BlockSpec((tf, D), lambda i, j: (j, 0))],
        out_specs=[pl.BlockSpec((tm, D), row), pl.BlockSpec((tm, D), row),
                   pl.BlockSpec((tm, tf), lambda i, j: (i, j)), pl.BlockSpec((tm, tf), lambda i, j: (i, j)),
                   pl.BlockSpec((tm, D), row), pl.BlockSpec((1, D), lambda i, j: (0, 0))],
        out_shape=[SDS((T, D), F32), SDS((T, D), MXU_DTYPE), SDS((T, F), MXU_DTYPE), SDS((T, F), MXU_DTYPE),
                   SDS((T, D), MXU_DTYPE), SDS((1, D), F32)],
        scratch_shapes=[pltpu.VMEM((tm, D), F32)],
        compiler_params=_params(("arbitrary", "arbitrary")),
    )(dy, up, h2, w_mn, w_up, w_down)


def _mix_bwd(dhb, o_mla, o_gdn, proj, mla_w, gdn_w, w_out):
    T, D = dhb.shape
    tm = min(512, T)
    H = MLA_HEADS

    def body(dh_ref, om_ref, og_ref, z_ref, mw_ref, gw_ref, w_ref, dom_ref, dog_ref, dz_ref, dmw_ref, dgw_ref,
             delta_ref):
        @pl.when(pl.program_id(0) == 0)
        def _():
            dmw_ref[...] = jnp.zeros_like(dmw_ref)
            dgw_ref[...] = jnp.zeros_like(dgw_ref)

        dmix = _mm_nt(dh_ref[...], w_ref[...])
        z = z_ref[...]
        dmw, dzs = [], []
        dgw = jnp.zeros((1, GDN_DIM), F32)
        for h in range(H):
            o = om_ref[h]
            w = mw_ref[h:h + 1, :]
            _, r = _rms(o, w)
            dx, dw = _rms_bwd(dmix[:, h * V_DIM:(h + 1) * V_DIM], o, w, r)
            dom_ref[h] = dx.astype(MXU_DTYPE)
            delta_ref[h] = jnp.sum(dx * o, axis=-1, keepdims=True)
            dmw.append(dw)
        for h in range(GDN_HEADS):
            o = og_ref[h]
            w = gw_ref[...]
            zh = z[:, h * GDN_DIM:(h + 1) * GDN_DIM]
            sg = _sigmoid(zh)
            yn, r = _rms(o, w)
            dy = dmix[:, H * V_DIM + h * GDN_DIM:H * V_DIM + (h + 1) * GDN_DIM]
            dzs.append(dy * yn * (sg * (1.0 + zh * (1.0 - sg))))
            dx, dw = _rms_bwd(dy * (zh * sg), o, w, r)
            dog_ref[h] = dx.astype(MXU_DTYPE)
            dgw = dgw + dw
        dz_ref[...] = jnp.concatenate(dzs, axis=-1).astype(MXU_DTYPE)
        dmw_ref[...] += jnp.concatenate(dmw, axis=0)
        dgw_ref[...] += dgw

    hspec = pl.BlockSpec((H, tm, V_DIM), lambda i: (0, i, 0))
    return pl.pallas_call(
        body, grid=(T // tm,), name="mix_bwd",
        in_specs=[pl.BlockSpec((tm, D), lambda i: (i, 0)), hspec, hspec,
                  pl.BlockSpec((tm, GDN_WIDTH), lambda i: (i, P_GZ // GDN_WIDTH)),
                  pl.BlockSpec((H, V_DIM), lambda i: (0, 0)), pl.BlockSpec((1, GDN_DIM), lambda i: (0, 0)),
                  pl.BlockSpec((D, D), lambda i: (0, 0))],
        out_specs=[hspec, hspec, pl.BlockSpec((tm, GDN_WIDTH), lambda i: (i, 0)),
                   pl.BlockSpec((H, V_DIM), lambda i: (0, 0)), pl.BlockSpec((1, GDN_DIM), lambda i: (0, 0)),
                   pl.BlockSpec((H, tm, 1), lambda i: (0, i, 0))],
        out_shape=[SDS((H, T, V_DIM), MXU_DTYPE), SDS((H, T, GDN_DIM), MXU_DTYPE), SDS((T, GDN_WIDTH), MXU_DTYPE),
                   SDS((H, V_DIM), F32), SDS((1, GDN_DIM), F32), SDS((H, T, 1), F32)],
        compiler_params=_params(("arbitrary",)),
    )(dhb, o_mla, o_gdn, proj, mla_w, gdn_w, w_out)


def _attn_bwd(q4, k4, v4, do4, delta4, lse4, B, S, transfer=None):
    H = MLA_HEADS
    bq = min(ATTN_BLOCK, S)
    nq = S // bq
    rows = bq // ATTN_CHAINS

    def body(q_ref, k_ref, v_ref, do_ref, delta_ref, lse_ref, dq_ref, dk_ref, dv_ref):
        dq_ref[...] = jnp.zeros_like(dq_ref)
        dk_ref[...] = jnp.zeros_like(dk_ref)
        dv_ref[...] = jnp.zeros_like(dv_ref)

        col = lax.broadcasted_iota(jnp.int32, (rows, bq), 1)
        row = lax.broadcasted_iota(jnp.int32, (rows, bq), 0)

        def k_step(kj, carry):
            ks = pl.multiple_of(kj * bq, bq)
            k = k_ref[0, pl.ds(ks, bq), :]
            v = v_ref[0, pl.ds(ks, bq), :]

            def q_block(qs, diagonal):
                dks, dvs = [None] * ATTN_CHAINS, [None] * ATTN_CHAINS

                def chain(j):
                    sl = pl.ds(qs + j * rows, rows)
                    q = q_ref[0, sl, :]
                    do = do_ref[0, sl, :].astype(MXU_DTYPE)
                    s = _mm_nt(q, k)
                    dp = _mm_nt(do, v)
                    yield
                    p = jnp.exp(s - lse_ref[0, sl, :])
                    if diagonal:
                        p = jnp.where(col <= row + j * rows, p, 0.0)
                    ds = p * (dp - delta_ref[0, sl, :])
                    yield
                    dvs[j] = _mm_tn(p, do)
                    dks[j] = _mm_tn(ds, q)
                    dq_ref[0, sl, :] += _mm(ds, k)

                _lockstep([chain(j) for j in range(ATTN_CHAINS)])
                dv_ref[0, pl.ds(ks, bq), :] += functools.reduce(jnp.add, dvs)
                dk_ref[0, pl.ds(ks, bq), :] += functools.reduce(jnp.add, dks)

            q_block(ks, True)

            def q_step(qi, c):
                q_block(pl.multiple_of(qi * bq, bq), False)
                return c

            lax.fori_loop(kj + 1, nq, q_step, 0)
            return carry

        lax.fori_loop(0, nq, k_step, 0)

    spec = lambda d: pl.BlockSpec((1, S, d), lambda h, b: (h, b, 0))
    return _call_beside(
        body, transfer, grid=(H, B), name="attn_bwd",
        in_specs=[spec(QK_DIM), spec(QK_DIM), spec(V_DIM), spec(V_DIM), spec(1), spec(1)],
        out_specs=[spec(QK_DIM), spec(QK_DIM), spec(V_DIM)],
        out_shape=[SDS((H, B * S, QK_DIM), F32), SDS((H, B * S, QK_DIM), F32), SDS((H, B * S, V_DIM), F32)],
        scratch_shapes=[], semantics=("arbitrary", "arbitrary"),
        args=(q4, k4, v4, do4, delta4, lse4))


def _gdn_bwd(qg, kg, vg, gates, states, ainv, u4, w4, do4, B, S, transfer=None):
    H, D, C = GDN_HEADS, GDN_DIM, CHUNK
    NC = S // C
    U = GDN_BWD_UNROLL if NC % GDN_BWD_UNROLL == 0 else 1
    NG = NC // U

    def body(q_ref, k_ref, v_ref, g_ref, st_ref, ai_ref, u_ref, w_ref, do_ref, dq_ref, dk_ref, dv_ref, dgb_ref,
             kd_s, x1_s, x2_s, el_s, dvn_s, ds_s, w2t_s):
        h = pl.program_id(0)
        lane = lax.broadcasted_iota(jnp.int32, (C, LANES), 1)
        ri = lax.broadcasted_iota(jnp.int32, (C, C), 0)
        ci = lax.broadcasted_iota(jnp.int32, (C, C), 1)
        rcol = lax.broadcasted_iota(jnp.int32, (C, 1), 0)

        def rsum(a):
            return jnp.sum(a, axis=-1, keepdims=True)

        def prepare(n):
            cs = n * C
            q = q_ref[0, pl.ds(cs, C), :]
            k = k_ref[0, pl.ds(cs, C), :]
            do = do_ref[0, pl.ds(cs, C), :]
            Gc, bt, Gam, e, f, eL = _chunk_decays(g_ref[pl.ds(cs, C), :], lane, h, ri, ci, rcol)
            At = _mm_nt(q, k) * Gam
            yield
            x1 = _mm_tn(At, do)
            x2 = _mm_tn(q * e, do)
            kd = k * f
            w = w_ref[0, pl.ds(cs, C), :]
            yield
            x1_s[pl.ds(cs, C), :] = x1
            x2_s[n] = x2 - _mm_tn(w, x1)
            w2t_s[n] = _mm_tn(w, kd)
            kd_s[pl.ds(cs, C), :] = kd
            el_s[n] = jnp.broadcast_to(eL, (SUBLANES, LANES))

        def recur(n, dS):
            cs = n * C
            ds_s[n] = dS
            dvn_s[pl.ds(cs, C), :] = x1_s[pl.ds(cs, C), :] + _mm(kd_s[pl.ds(cs, C), :], dS)
            return x2_s[n] + el_s[n, 0:1, :] * dS - _mm(w2t_s[n], dS)

        def local(n):
            cs = n * C
            q = q_ref[0, pl.ds(cs, C), :]
            k = k_ref[0, pl.ds(cs, C), :]
            v = v_ref[0, pl.ds(cs, C), :]
            do = do_ref[0, pl.ds(cs, C), :]
            u = u_ref[0, pl.ds(cs, C), :]
            w = w_ref[0, pl.ds(cs, C), :]
            dvn = dvn_s[pl.ds(cs, C), :]
            dS = ds_s[n]
            Gc, bt, Gam, e, f, eL = _chunk_decays(g_ref[pl.ds(cs, C), :], lane, h, ri, ci, rcol)
            S0 = st_ref[0, n]
            AinvT = ai_ref[0, n]
            qk = _mm_nt(jnp.concatenate([q, k], axis=0), k)
            QK, KK = qk[:C], qk[C:]
            be = bt * e
            sol = jnp.concatenate([u, w], axis=-1)
            vn = u - _mm(w, S0)
            yield
            dAt = jnp.where(ri >= ci, _mm_nt(do, vn), 0.0)
            dqd = _mm_nt(do, S0)
            dw = -_mm_nt(dvn, S0)
            dkd = _mm_nt(vn, dS)
            deL = jnp.sum(rsum(dS * S0), axis=0, keepdims=True)
            yield
            dR = _mm_exact(AinvT, jnp.concatenate([dvn, dw], axis=-1))
            dR1, dR2 = dR[:, :D], dR[:, D:]
            yield
            dL = jnp.where(ri > ci, -_mm_nt(dR, sol), 0.0)
            yield
            dv_ref[0, pl.ds(cs, C), :] = dR1 * bt
            r2 = rsum(dR2 * k)
            X = dL * Gam
            dbt = rsum(dR1 * v) + r2 * e + rsum(X * KK)
            de = r2 * bt + rsum(dqd * q)
            dKK = X * bt
            dQK = dAt * Gam
            dq_ref[0, pl.ds(cs, C), :] = _mm(dQK, k) + dqd * e
            dk_ref[0, pl.ds(cs, C), :] = dR2 * be + _mm(dKK + dKK.T, k) + _mm_tn(dQK, q) + dkd * f
            df = rsum(dkd * k)
            Z = (dL * (bt * KK) + dAt * QK) * Gam
            dG = rsum(Z) - rsum(Z.T) + de * e - df * f
            dGl = jnp.sum(df * f, axis=0, keepdims=True) + deL * eL
            dG = dG + jnp.where(rcol == C - 1, dGl, 0.0)
            dgb_ref[0, pl.ds(cs, C), :] = jnp.where(lane == 0, dG, jnp.where(lane == 1, dbt, 0.0))

        state = [jnp.zeros((D, D), F32)]

        def recur_group(g):
            for j, n in enumerate(reversed(range(g * U, (g + 1) * U))):
                state[0] = recur(n, state[0])
                if j % GDN_RECUR_STEPS_PER_STAGE == GDN_RECUR_STEPS_PER_STAGE - 1:
                    yield

        def stage(fn, g):
            return _together([fn(g * U + j) for j in range(U)])

        for step in range(NG + 2):
            jobs = [(stage, prepare, NG - 1 - step), (None, None, NG - step), (stage, local, NG + 1 - step)]
            _lockstep([recur_group(g) if make is None else make(fn, g) for make, fn, g in jobs if 0 <= g < NG])

    spec = pl.BlockSpec((1, S, D), lambda h, b: (h, b, 0))
    return _call_beside(
        body, transfer, grid=(H, B), name="gdn_bwd",
        in_specs=[spec, spec, spec, pl.BlockSpec((S, LANES), lambda h, b: (b, 0)),
                  pl.BlockSpec((1, NC, D, D), lambda h, b: (h, b, 0, 0)),
                  pl.BlockSpec((1, NC, C, C), lambda h, b: (h, b, 0, 0)), spec, spec, spec],
        out_specs=[spec, spec, spec, spec],
        out_shape=[SDS((H, B * S, D), F32)] * 4,
        scratch_shapes=[pltpu.VMEM((S, D), F32), pltpu.VMEM((S, D), F32), pltpu.VMEM((NC, D, D), F32),
                        pltpu.VMEM((NC, SUBLANES, LANES), F32), pltpu.VMEM((S, D), F32),
                        pltpu.VMEM((NC, D, D), F32), pltpu.VMEM((NC, D, D), F32)],
        semantics=("arbitrary", "arbitrary"), args=(qg, kg, vg, gates, states, ainv, u4, w4, do4))


def _gdn_pre_bwd(proj, conv_w, alog_l, dt_l, dq4, dk4, dv4, dgb4, S):
    T = proj.shape[0]
    tm = min(256, T)
    tiles_per_seq = S // tm
    C3 = 3 * GDN_WIDTH
    H = GDN_HEADS

    def body(u_ref, halo_ref, gab_ref, w_ref, alog_ref, dt_ref, dq_ref, dk_ref, dv_ref, dgb_ref,
             dc_ref, dgab_ref, dcw_ref, dalog_ref, ddt_ref):
        i = pl.program_id(0)

        @pl.when(i == 0)
        def _():
            dcw_ref[...] = jnp.zeros_like(dcw_ref)
            dalog_ref[...] = jnp.zeros_like(dalog_ref)
            ddt_ref[...] = jnp.zeros_like(ddt_ref)

        halo = jnp.where(i % tiles_per_seq == 0, 0.0, halo_ref[...])
        c, sh = _conv_taps(u_ref[...], halo, w_ref[...])
        sg = _sigmoid(c)
        a = c * sg
        das = [None] * (3 * H)
        for h in range(H):
            xq = a[:, h * GDN_DIM:(h + 1) * GDN_DIM]
            xk = a[:, GDN_WIDTH + h * GDN_DIM:GDN_WIDTH + (h + 1) * GDN_DIM]
            das[h] = _l2n_bwd(dq_ref[h], xq, GDN_QSCALE)
            das[H + h] = _l2n_bwd(dk_ref[h], xk, 1.0)
            das[2 * H + h] = dv_ref[h]
        dc = jnp.concatenate(das, axis=-1) * (sg * (1.0 + c * (1.0 - sg)))
        dc_ref[...] = dc
        dcw_ref[...] += jnp.concatenate(
            [jnp.sum(dc * sh[CONV_W - 1 - t], axis=0, keepdims=True) for t in range(CONV_W)], axis=0)
        lane = lax.broadcasted_iota(jnp.int32, (tm, LANES), 1)
        ric = lax.broadcasted_iota(jnp.int32, (tm, LANES), 0) % CHUNK
        dG = jnp.zeros((tm, LANES), F32)
        for h in range(H):
            t = dgb_ref[h]
            dG = dG + jnp.where(lane == h, _pick_lane(t, lane, 0), 0.0) \
                    + jnp.where(lane == h + H, _pick_lane(t, lane, 1), 0.0)
        is_g = lane < H
        dg = jnp.where(is_g, _chunk_rev_cumsum(jnp.where(is_g, dG, 0.0), ric), 0.0)
        gab = gab_ref[...]
        g, beta = _gate_values(gab, alog_ref[...], dt_ref[...], lane)
        dga = jnp.where(is_g, dg * (-jnp.exp(alog_ref[...])) * _sigmoid(gab + dt_ref[...]), 0.0)
        dgb = jnp.where(is_g, 0.0, dG) * beta * (1.0 - beta)
        dgab_ref[...] = (dga + dgb).astype(MXU_DTYPE)
        dalog_ref[...] += jnp.sum(dg * g, axis=0, keepdims=True)
        ddt_ref[...] += jnp.sum(dga, axis=0, keepdims=True)

    hspec = pl.BlockSpec((H, tm, GDN_DIM), lambda i: (0, i, 0))
    vec = pl.BlockSpec((1, LANES), lambda i: (0, 0))
    return pl.pallas_call(
        body, grid=(T // tm,), name="gdn_pre_bwd",
        in_specs=[pl.BlockSpec((tm, C3), lambda i: (i, 0)),
                  pl.BlockSpec((SUBLANES, C3), lambda i: (jnp.maximum(i * (tm // SUBLANES) - 1, 0), 0)),
                  pl.BlockSpec((tm, LANES), lambda i: (i, P_GAB // LANES)),
                  pl.BlockSpec((CONV_W, C3), lambda i: (0, 0)), vec, vec, hspec, hspec, hspec, hspec],
        out_specs=[pl.BlockSpec((tm, C3), lambda i: (i, 0)), pl.BlockSpec((tm, LANES), lambda i: (i, 0)),
                   pl.BlockSpec((CONV_W, C3), lambda i: (0, 0)), vec, vec],
        out_shape=[SDS((T, C3), F32), SDS((T, LANES), MXU_DTYPE), SDS((CONV_W, C3), F32),
                   SDS((1, LANES), F32), SDS((1, LANES), F32)],
        compiler_params=_params(("arbitrary",)),
    )(proj, proj, proj, conv_w, alog_l, dt_l, dq4, dk4, dv4, dgb4)


def _mla_pre_bwd(proj, cosf, sinf, w_qln, w_kvln, w_uq_p, w_ukv, qnw, knw, dq4, dk4, dv4, transfer=None):
    T = proj.shape[0]
    tm = min(256, T)
    H = MLA_HEADS

    def body(ql_ref, kvl_ref, kpe_ref, cos_ref, sin_ref, wq_ref, wkv_ref, uq_ref, ukv_ref, qnw_ref, knw_ref,
             dq_ref, dk_ref, dv_ref,
             dql_ref, dkvl_ref, dkpe_ref, dqraw_ref, dkvraw_ref, qn_ref, kvn_ref, dwq_ref, dwkv_ref, dqnw_ref, dknw_ref):
        @pl.when(pl.program_id(0) == 0)
        def _():
            for r in (dwq_ref, dwkv_ref, dqnw_ref, dknw_ref):
                r[...] = jnp.zeros_like(r)

        cos, sin = cos_ref[...], sin_ref[...]
        qnw_, knw_ = qnw_ref[...], knw_ref[...]
        ql, kvl = ql_ref[...], kvl_ref[...]
        kpe_raw = kpe_ref[...][:, :ROPE]
        rms = functools.partial(_rms, on_mxu=True)
        rms_bwd = functools.partial(_rms_bwd, on_mxu=True)
        qn, rq = rms(ql, wq_ref[...])
        kvn, rkv = rms(kvl, wkv_ref[...])
        qn_ref[...] = qn.astype(MXU_DTYPE)
        kvn_ref[...] = kvn.astype(MXU_DTYPE)
        qraw = _mm(qn, uq_ref[...])
        kvraw = _mm(kvn, ukv_ref[...])
        dq_nope, dq_pe, dkv_parts = [], [], []
        dqnw_n = jnp.zeros((1, NOPE), F32)
        dqnw_p = jnp.zeros((1, ROPE), F32)
        dknw_n = jnp.zeros((1, NOPE), F32)
        dkpe = jnp.zeros((tm, ROPE), F32)
        for h in range(H):
            dq = dq_ref[h] * ATT_SCALE
            x = qraw[:, h * NOPE:(h + 1) * NOPE]
            dx, dw = rms_bwd(dq[:, :NOPE], x, qnw_[:, :NOPE], rms(x, qnw_[:, :NOPE])[1])
            dq_nope.append(dx)
            dqnw_n = dqnw_n + dw
            x = qraw[:, H * NOPE + h * ROPE:H * NOPE + (h + 1) * ROPE]
            dx, dw = rms_bwd(_rope_bwd(dq[:, NOPE:], cos, sin), x, qnw_[:, NOPE:], rms(x, qnw_[:, NOPE:])[1])
            dq_pe.append(dx)
            dqnw_p = dqnw_p + dw
            dk = dk_ref[h]
            x = kvraw[:, h * 256:h * 256 + NOPE]
            dx, dw = rms_bwd(dk[:, :NOPE], x, knw_[:, :NOPE], rms(x, knw_[:, :NOPE])[1])
            dknw_n = dknw_n + dw
            dkpe = dkpe + dk[:, NOPE:]
            dkv_parts += [dx, dv_ref[h]]
        dx, dknw_p = rms_bwd(_rope_bwd(dkpe, cos, sin), kpe_raw, knw_[:, NOPE:], rms(kpe_raw, knw_[:, NOPE:])[1])
        dkpe_ref[...] = jnp.concatenate([dx, jnp.zeros((tm, LANES - ROPE), F32)], axis=-1).astype(MXU_DTYPE)
        dqraw = jnp.concatenate(dq_nope + dq_pe, axis=-1).astype(MXU_DTYPE)
        dkvraw = jnp.concatenate(dkv_parts, axis=-1).astype(MXU_DTYPE)
        dqraw_ref[...] = dqraw
        dkvraw_ref[...] = dkvraw
        dx, dw = rms_bwd(_mm_nt(dqraw, uq_ref[...]), ql, wq_ref[...], rq)
        dql_ref[...] = dx.astype(MXU_DTYPE)
        dwq_ref[...] += dw
        dx, dw = rms_bwd(_mm_nt(dkvraw, ukv_ref[...]), kvl, wkv_ref[...], rkv)
        dkvl_ref[...] = dx.astype(MXU_DTYPE)
        dwkv_ref[...] += dw
        dqnw_ref[...] += jnp.concatenate([dqnw_n, dqnw_p], axis=-1)
        dknw_ref[...] += jnp.concatenate([dknw_n, dknw_p], axis=-1)

    full = lambda a: pl.BlockSpec(a.shape, lambda i: (0,) * a.ndim)
    rows = lambda n: pl.BlockSpec((tm, n), lambda i: (i, 0))
    const = lambda n: pl.BlockSpec((1, n), lambda i: (0, 0))
    NQ, NKV = w_uq_p.shape[1], w_ukv.shape[1]
    return _call_beside(
        body, transfer, grid=(T // tm,), name="mla_pre_bwd", scratch_shapes=[], semantics=("arbitrary",),
        args=(proj, proj, proj, cosf, sinf, w_qln, w_kvln, w_uq_p, w_ukv, qnw, knw, dq4, dk4, dv4),
        in_specs=[pl.BlockSpec((tm, 256), lambda i: (i, P_QLAT // 256)),
                  pl.BlockSpec((tm, 256), lambda i: (i, P_KVLAT // 256)),
                  pl.BlockSpec((tm, 128), lambda i: (i, P_KPE // 128)),
                  rows(ROPE), rows(ROPE),
                  full(w_qln), full(w_kvln), full(w_uq_p), full(w_ukv), full(qnw), full(knw),
                  pl.BlockSpec((H, tm, QK_DIM), lambda i: (0, i, 0)),
                  pl.BlockSpec((H, tm, QK_DIM), lambda i: (0, i, 0)),
                  pl.BlockSpec((H, tm, V_DIM), lambda i: (0, i, 0))],
        out_specs=[rows(Q_LORA), rows(KV_LORA), rows(LANES), rows(NQ), rows(NKV), rows(Q_LORA), rows(KV_LORA),
                   const(Q_LORA), const(KV_LORA), const(QK_DIM), const(QK_DIM)],
        out_shape=[SDS((T, Q_LORA), MXU_DTYPE), SDS((T, KV_LORA), MXU_DTYPE), SDS((T, LANES), MXU_DTYPE),
                   SDS((T, NQ), MXU_DTYPE), SDS((T, NKV), MXU_DTYPE),
                   SDS((T, Q_LORA), MXU_DTYPE), SDS((T, KV_LORA), MXU_DTYPE),
                   SDS((1, Q_LORA), F32), SDS((1, KV_LORA), F32), SDS((1, QK_DIM), F32), SDS((1, QK_DIM), F32)])


def _in_proj_bwd(dc, conv_w, dgz, dql, dkvl, dkpe, dgab, w_in_p, dh, x2, w_an, S):
    T, D = x2.shape
    N = w_in_p.shape[1]
    C3 = dc.shape[1]
    tm = min(512, S)
    assert S % tm == 0 and T % tm == 0, "a token tile must not straddle two sequences"
    tiles_per_seq = S // tm
    nblk = T // SUBLANES

    def body(dc_ref, nxt_ref, cw_ref, b_ref, c_ref, d_ref, e_ref, f_ref, w_ref, dh_ref, x_ref, wn_ref,
             dx_ref, dp_ref, dwn_ref):
        i = pl.program_id(0)

        @pl.when(i == 0)
        def _():
            dwn_ref[...] = jnp.zeros_like(dwn_ref)

        nxt = jnp.where(i % tiles_per_seq == tiles_per_seq - 1, 0.0, nxt_ref[...])
        dcv, cw = dc_ref[...], cw_ref[...]
        du = cw[3:4] * dcv
        for j in range(1, CONV_W):
            du = du + cw[3 - j:4 - j] * _shift_up(dcv, nxt, j)
        dp = jnp.concatenate([du.astype(MXU_DTYPE), b_ref[...], c_ref[...], d_ref[...], e_ref[...], f_ref[...]],
                             axis=-1).astype(MXU_DTYPE)
        dp_ref[...] = dp
        x = x_ref[...]
        _, r = _rms(x, wn_ref[...])
        dx, dw = _rms_bwd(_mm_nt(dp, w_ref[...]), x, wn_ref[...], r)
        dx_ref[...] = dh_ref[...] + dx
        dwn_ref[...] += dw

    rows = lambda n: pl.BlockSpec((tm, n), lambda i: (i, 0))
    return pl.pallas_call(
        body, grid=(T // tm,), name="in_proj_bwd",
        in_specs=[rows(C3),
                  pl.BlockSpec((SUBLANES, C3), lambda i: (jnp.minimum((i + 1) * (tm // SUBLANES), nblk - 1), 0)),
                  pl.BlockSpec((CONV_W, C3), lambda i: (0, 0)),
                  rows(dgz.shape[1]), rows(dql.shape[1]), rows(dkvl.shape[1]),
                  rows(dkpe.shape[1]), rows(dgab.shape[1]),
                  pl.BlockSpec((D, N), lambda i: (0, 0)), rows(D), rows(D), pl.BlockSpec((1, D), lambda i: (0, 0))],
        out_specs=[rows(D), rows(N), pl.BlockSpec((1, D), lambda i: (0, 0))],
        out_shape=[SDS((T, D), F32), SDS((T, N), MXU_DTYPE), SDS((1, D), F32)],
        compiler_params=_params(("arbitrary",)),
    )(dc, dc, conv_w, dgz, dql, dkvl, dkpe, dgab, w_in_p, dh, x2, w_an)


def _wgrad(a, b, name, column_shards=False):
    T, M = a.shape
    N = b.shape[1]
    tM = _divisor_tile(M, 1024)
    tN = N // N_DEV if column_shards else _divisor_tile(N, 1536)
    tk = min(T, 2048)
    nk = T // tk

    def body(a_ref, b_ref, o_ref, acc):
        k = pl.program_id(2)

        @pl.when(k == 0)
        def _():
            acc[...] = jnp.zeros_like(acc)

        acc[...] += _mm_tn(a_ref[...], b_ref[...])

        @pl.when(k == nk - 1)
        def _():
            o_ref[...] = acc[...].astype(WIRE_DTYPE).reshape(o_ref.shape)

    if column_shards:
        out_spec, out_shape = pl.BlockSpec((1, tM, tN), lambda i, j, k: (j, i, 0)), SDS((N_DEV, M, tN), WIRE_DTYPE)
    else:
        out_spec, out_shape = pl.BlockSpec((tM, tN), lambda i, j, k: (i, j)), SDS((M, N), WIRE_DTYPE)
    return pl.pallas_call(
        body, grid=(M // tM, N // tN, nk), name=name,
        in_specs=[pl.BlockSpec((tk, tM), lambda i, j, k: (k, i)), pl.BlockSpec((tk, tN), lambda i, j, k: (k, j))],
        out_specs=out_spec, out_shape=out_shape,
        scratch_shapes=[pltpu.VMEM((tM, tN), F32)],
        compiler_params=_params(("arbitrary", "arbitrary", "arbitrary")),
    )(a, b)


def _adamw(g, w, m, v):
    m = ADAM_B1 * m + (1.0 - ADAM_B1) * g
    v = ADAM_B2 * v + (1.0 - ADAM_B2) * jnp.square(g)
    m_hat = m / (1.0 - ADAM_B1 ** ADAM_STEP)
    v_hat = v / (1.0 - ADAM_B2 ** ADAM_STEP)
    return -ADAM_LR * (m_hat / (jnp.sqrt(v_hat) + ADAM_EPS) + ADAM_WD * w), m, v


def _reduce_adamw(parts, w, m, v, name):
    R, C = w.shape
    _, Rp, Cp = parts.shape
    tr = min(R, 256)
    tp = tr if Rp == R else Rp

    def body(p_ref, w_ref, m_ref, v_ref, g_ref, d_ref, nm_ref, nv_ref):
        g = p_ref[0].astype(F32)
        for s in range(1, N_DEV):
            g = g + p_ref[s].astype(F32)
        g = g[:tr, :C]
        g_ref[...] = g
        d_ref[...], nm_ref[...], nv_ref[...] = _adamw(g, w_ref[...], m_ref[...], v_ref[...])

    spec = pl.BlockSpec((tr, C), lambda i: (i, 0))
    return pl.pallas_call(
        body, grid=(R // tr,), name=name,
        in_specs=[pl.BlockSpec((N_DEV, tp, Cp), lambda i: (0, i, 0)), spec, spec, spec],
        out_specs=[spec] * 4, out_shape=[SDS((R, C), F32)] * 4,
        compiler_params=_params(("arbitrary",)),
    )(parts, w, m, v)


SMALL_ROWS, SMALL_COLS = 16, 1024
SMALL_LAYOUT = (
    ("attn_norm_w", 0, 1, 1024, 1024), ("mlp_norm_w", 1, 1, 1024, 1024), ("q_lat_norm_w", 2, 1, 256, 256),
    ("kv_lat_norm_w", 3, 1, 256, 256), ("q_norm_w", 4, 1, 192, 192), ("k_norm_w", 5, 1, 192, 192),
    ("mla_out_norm_w", 6, 4, 128, 128), ("a_log", 10, 1, 128, 4), ("dt_bias", 11, 1, 128, 4),
    ("gdn_norm_w", 12, 1, 128, 128))
LOSS_ENTRY = ("loss", 13, 1, 128, 128)


def _adamw_replicated(parts, ws, ms, vs):
    n = len(SMALL_LAYOUT)

    def body(*refs):
        p_ref = refs[0]
        w_refs, m_refs, v_refs = refs[1:1 + n], refs[1 + n:1 + 2 * n], refs[1 + 2 * n:1 + 3 * n]
        outs = refs[1 + 3 * n:]
        s = p_ref[0]
        for d in range(1, N_DEV):
            s = s + p_ref[d]
        for i, (_, r0, nr, _, pw) in enumerate(SMALL_LAYOUT):
            g = s[r0:r0 + nr, :pw]
            outs[i][...] = g
            outs[n + i][...], outs[2 * n + i][...], outs[3 * n + i][...] = _adamw(
                g, w_refs[i][...], m_refs[i][...], v_refs[i][...])
        _, r0, nr, gw, _ = LOSS_ENTRY
        outs[4 * n][...] = s[r0:r0 + nr, :gw]

    res = pl.pallas_call(
        body, name="adamw_replicated",
        out_shape=[SDS(w.shape, F32) for w in ws] * 4 + [SDS((1, LANES), F32)],
        compiler_params=_params(),
    )(parts, *ws, *ms, *vs)
    return [res[k * n:(k + 1) * n] for k in range(4)], res[4 * n][0, 0]


COPIES_PER_ARRAY = N_DEV - 1


def _two_level_gather(srcs, outs, send_sems, recv_sems, local_sems=None, stage="all"):
    mx, my, mc = lax.axis_index("x"), lax.axis_index("y"), lax.axis_index("c")
    me, sibling = (mx, my, mc), (mx, my, 1 - mc)
    chips = [(1 - mx, my), (mx, 1 - my), (1 - mx, 1 - my)]
    arrays = range(len(srcs))

    def copy(a, k, block, to, src=None):
        px, py, pc = block
        slot = outs[a].at[4 * px + 2 * py + pc]
        sem = a * COPIES_PER_ARRAY + k
        return pltpu.make_async_remote_copy(
            src_ref=slot if src is None else src, dst_ref=slot,
            send_sem=send_sems.at[sem], recv_sem=recv_sems.at[sem], device_id=to, device_id_type=MESH_ID)

    mine = [] if local_sems is None else [
        pltpu.make_async_copy(srcs[a], outs[a].at[4 * mx + 2 * my + mc], local_sems.at[a]) for a in arrays]
    first = []
    for a in arrays:
        first.append(copy(a, 0, me, sibling, src=srcs[a]))
        first += [copy(a, 1 + j, me, (*chip, mc), src=srcs[a]) for j, chip in enumerate(chips)]
    if stage in ("all", "start"):
        for cp in mine + first:
            cp.start()
    if stage in ("all", "finish"):
        forwards = []
        for j, chip in enumerate(chips):
            for a in arrays:
                copy(a, 1 + j, (*chip, mc), me).wait_recv()
                fwd = copy(a, 4 + j, (*chip, mc), sibling)
                fwd.start()
                forwards.append(fwd)
        for a in arrays:
            copy(a, 0, sibling, me).wait_recv()
        for j, chip in enumerate(chips):
            for a in arrays:
                copy(a, 4 + j, (*chip, 1 - mc), me).wait_recv()
        for cp in first + forwards:
            cp.wait_send()
        for cp in mine:
            cp.wait()


def _comm_scratch(n):
    return [pltpu.SemaphoreType.DMA((n * COPIES_PER_ARRAY,)), pltpu.SemaphoreType.DMA((n * COPIES_PER_ARRAY,)),
            pltpu.SemaphoreType.DMA((n,))]


def _any_specs(n):
    return [pl.BlockSpec(memory_space=pl.ANY)] * n


def _gather_weights(shards):
    n = len(shards)

    def body(*refs):
        _two_level_gather(refs[:n], refs[n:2 * n], *refs[2 * n:])

    return pl.pallas_call(
        body, name="gather_weights",
        out_shape=[SDS((N_DEV,) + s.shape, s.dtype) for s in shards],
        in_specs=_any_specs(n), out_specs=_any_specs(n), scratch_shapes=_comm_scratch(n),
    )(*shards)


def _gather_small_grads(gs, loss_lanes):
    gs = list(gs) + [loss_lanes]
    n = len(gs)

    def body(*refs):
        g_refs, out_ref = refs[:n], refs[n]
        tile, send_sems, recv_sems = refs[n + 1:]
        tile[...] = jnp.zeros_like(tile)
        for (_, r0, nr, gw, _), g in zip(SMALL_LAYOUT + (LOSS_ENTRY,), g_refs):
            tile[r0:r0 + nr, 0:gw] = g[...]
        me = 4 * lax.axis_index("x") + 2 * lax.axis_index("y") + lax.axis_index("c")
        out_ref[me] = tile[...]
        _two_level_gather([tile], [out_ref], send_sems, recv_sems)

    return pl.pallas_call(
        body, name="gather_small_grads",
        out_shape=SDS((N_DEV, SMALL_ROWS, SMALL_COLS), F32),
        in_specs=[pl.BlockSpec(memory_space=pltpu.VMEM)] * n,
        out_specs=pl.BlockSpec(memory_space=pltpu.VMEM),
        scratch_shapes=[pltpu.VMEM((SMALL_ROWS, SMALL_COLS), F32),
                        pltpu.SemaphoreType.DMA((COPIES_PER_ARRAY,)), pltpu.SemaphoreType.DMA((COPIES_PER_ARRAY,))],
    )(*gs)


def _exchange_grads(slabs):
    n = len(slabs)

    def body(*refs):
        _exchange(refs[:n], refs[n:2 * n], *refs[2 * n:])

    return pl.pallas_call(
        body, name="exchange_grads",
        out_shape=[SDS(s.shape, s.dtype) for s in slabs],
        in_specs=_any_specs(n), out_specs=_any_specs(n), scratch_shapes=_comm_scratch(n),
    )(*slabs)


class _Transfer:
    def __init__(self, kind, arrays):
        self.kind, self.arrays, self.n = kind, list(arrays), len(arrays)

    def out_shapes(self):
        if self.kind == "gather":
            return [SDS((N_DEV,) + a.shape, a.dtype) for a in self.arrays]
        return [SDS(a.shape, a.dtype) for a in self.arrays]

    def run(self, srcs, outs, sems, stage):
        fn = _two_level_gather if self.kind == "gather" else _exchange
        fn(srcs, outs, *sems, stage=stage)


def _call_beside(body, transfer, *, grid, in_specs, out_specs, out_shape, scratch_shapes, name, semantics, args):
    if transfer is None:
        res = pl.pallas_call(body, grid=grid, in_specs=in_specs, out_specs=out_specs, out_shape=out_shape,
                             scratch_shapes=scratch_shapes, name=name, compiler_params=_params(semantics))(*args)
        return list(res), []
    n_in, n_out, n_s, n = len(in_specs), len(out_specs), len(scratch_shapes), transfer.n

    def wrapped(*refs):
        ins, refs = refs[:n_in], refs[n_in:]
        t_in, refs = refs[:n], refs[n:]
        outs, refs = refs[:n_out], refs[n_out:]
        t_out, refs = refs[:n], refs[n:]
        scratch, sems = refs[:n_s], refs[n_s:]
        first = functools.reduce(jnp.logical_and, [pl.program_id(i) == 0 for i in range(len(grid))])
        last = functools.reduce(jnp.logical_and, [pl.program_id(i) == g - 1 for i, g in enumerate(grid)])

        @pl.when(first)
        def _():
            transfer.run(t_in, t_out, sems, "start")

        body(*ins, *outs, *scratch)

        @pl.when(last)
        def _():
            transfer.run(t_in, t_out, sems, "finish")

    res = pl.pallas_call(
        wrapped, grid=grid, in_specs=list(in_specs) + _any_specs(n), out_specs=list(out_specs) + _any_specs(n),
        out_shape=list(out_shape) + transfer.out_shapes(), scratch_shapes=list(scratch_shapes) + _comm_scratch(n),
        name=name, compiler_params=_params(semantics))(*args, *transfer.arrays)
    return list(res[:n_out]), list(res[n_out:])


EXCHANGE_FLIPS = ((0, 0, 1), (1, 0, 0), (0, 1, 0), (1, 1, 0), (1, 0, 1), (0, 1, 1), (1, 1, 1))


def _exchange(srcs, outs, send_sems, recv_sems, local_sems, stage="all"):
    mx, my, mc = lax.axis_index("x"), lax.axis_index("y"), lax.axis_index("c")
    arrays = range(len(srcs))
    copies = [pltpu.make_async_copy(srcs[a].at[4 * mx + 2 * my + mc], outs[a].at[N_DEV - 1], local_sems.at[a])
              for a in arrays]
    for k, (fx, fy, fc) in enumerate(EXCHANGE_FLIPS):
        px = 1 - mx if fx else mx
        py = 1 - my if fy else my
        pc = 1 - mc if fc else mc
        for a in arrays:
            sem = a * COPIES_PER_ARRAY + k
            copies.append(pltpu.make_async_remote_copy(
                src_ref=srcs[a].at[4 * px + 2 * py + pc], dst_ref=outs[a].at[k],
                send_sem=send_sems.at[sem], recv_sem=recv_sems.at[sem],
                device_id=(px, py, pc), device_id_type=MESH_ID))
    if stage in ("all", "start"):
        for cp in copies:
            cp.start()
    if stage in ("all", "finish"):
        for cp in copies:
            cp.wait()


def _w_in_to_padded(w):
    z = lambda n: jnp.zeros((w.shape[0], n), w.dtype)
    return jnp.concatenate([w[:, O_GQKV:O_GZ], w[:, O_GZ:O_GAB], w[:, O_QLAT:O_KVLAT], w[:, O_KVLAT:O_KPE],
                            w[:, O_KPE:O_GQKV], z(P_GAB - P_KPE - ROPE), w[:, O_GAB:O_END],
                            z(P_WIDTH - P_GAB - (O_END - O_GAB))], axis=1)


def _w_in_from_padded(wp):
    return jnp.concatenate([wp[:, P_QLAT:P_QLAT + 256], wp[:, P_KVLAT:P_KVLAT + 256], wp[:, P_KPE:P_KPE + ROPE],
                            wp[:, P_GQKV:P_GZ], wp[:, P_GZ:P_QLAT], wp[:, P_GAB:P_GAB + (O_END - O_GAB)]], axis=1)


W_IN_SHARD_COLS = (O_END - O_QLAT) // N_DEV


def _w_in_shards_to_padded(stack):
    _, R, Cw = stack.shape
    tr = min(R, 256)

    def body(s_ref, o_ref):
        full = jnp.concatenate([s_ref[d].astype(F32)[:, :W_IN_SHARD_COLS] for d in range(N_DEV)], axis=-1)
        o_ref[...] = _w_in_to_padded(full).astype(o_ref.dtype)

    return pl.pallas_call(
        body, grid=(R // tr,), name="w_in_to_padded",
        in_specs=[pl.BlockSpec((N_DEV, tr, Cw), lambda i: (0, i, 0))],
        out_specs=pl.BlockSpec((tr, P_WIDTH), lambda i: (i, 0)),
        out_shape=SDS((R, P_WIDTH), stack.dtype), compiler_params=_params(("arbitrary",)),
    )(stack)


def _w_in_padded_to_slabs(gp, wire_cols):
    R = gp.shape[0]
    tr = min(R, 256)

    def body(g_ref, o_ref):
        orig = _w_in_from_padded(g_ref[...].astype(F32))
        for d in range(N_DEV):
            piece = orig[:, d * W_IN_SHARD_COLS:(d + 1) * W_IN_SHARD_COLS]
            o_ref[d] = _pad2(piece, tr, wire_cols).astype(o_ref.dtype)

    return pl.pallas_call(
        body, grid=(R // tr,), name="w_in_to_slabs",
        in_specs=[pl.BlockSpec((tr, P_WIDTH), lambda i: (i, 0))],
        out_specs=pl.BlockSpec((N_DEV, tr, wire_cols), lambda i: (0, i, 0)),
        out_shape=SDS((N_DEV, R, wire_cols), gp.dtype), compiler_params=_params(("arbitrary",)),
    )(gp)


def _w_uq_to_headsplit(w):
    w3 = w.reshape(w.shape[0], MLA_HEADS, QK_DIM)
    return jnp.concatenate([w3[:, :, :NOPE].reshape(w.shape[0], -1), w3[:, :, NOPE:].reshape(w.shape[0], -1)], axis=1)


def _w_uq_from_headsplit(wp):
    n = wp[:, :MLA_HEADS * NOPE].reshape(wp.shape[0], MLA_HEADS, NOPE)
    p = wp[:, MLA_HEADS * NOPE:].reshape(wp.shape[0], MLA_HEADS, ROPE)
    return jnp.concatenate([n, p], axis=2).reshape(wp.shape[0], -1)


def _lane_vec(v4):
    return jnp.pad(v4.reshape(1, -1), ((0, 0), (0, LANES - v4.shape[-1])))


def _local_step(x, positions, target, attn_norm_w, w_in, q_lat_norm_w, w_uq, kv_lat_norm_w, w_ukv, q_norm_w,
                k_norm_w, mla_out_norm_w, conv_w, a_log, dt_bias, gdn_norm_w, w_out, mlp_norm_w, w_up, w_down,
                late_shards=None, exchange=False):
    B, S, D = x.shape
    T = B * S
    x2 = x.reshape(T, D)
    t2 = target.reshape(T, D)
    half = ROPE // 2
    inv_freq = ROPE_THETA ** (-jnp.arange(half, dtype=F32) / half)
    ang = positions.reshape(T, 1).astype(F32) * inv_freq
    cosf = jnp.concatenate([jnp.cos(ang)] * 2, axis=-1)
    sinf = jnp.concatenate([jnp.sin(ang)] * 2, axis=-1)
    w_in_p = w_in
    w_uq_p = _w_uq_to_headsplit(w_uq)
    alog_l, dt_l = _lane_vec(a_log), _lane_vec(dt_bias)
    w_an, w_qln, w_kvln, qnw, knw, w_mn, gdn_w = (
        attn_norm_w, q_lat_norm_w, kv_lat_norm_w, q_norm_w, k_norm_w, mlp_norm_w, gdn_norm_w)

    proj, xn = _in_proj(x2, w_an, w_in_p)
    def gathering(shards):
        return None if late_shards is None else _Transfer("gather", shards)

    (q4, k4, v4), late = _mla_pre(proj, cosf, sinf, w_qln, w_kvln, w_uq_p, w_ukv, qnw, knw,
                                  gathering(late_shards and late_shards[:1]))
    if late:
        w_out = late[0].reshape(-1, D)
    (o_mla, lse), late = _attn_fwd(q4, k4, v4, B, S, gathering(late_shards and late_shards[2:]))
    if late:
        w_down = late[0].reshape(-1, D)
    qg, kg, vg, gates = _gdn_pre(proj, conv_w, alog_l, dt_l, S)
    (o_gdn, states, ainv, u4, w4), late = _gdn_fwd(qg, kg, vg, gates, B, S,
                                                   gathering(late_shards and late_shards[1:2]))
    if late:
        w_up = late[0]
    h2, mix = _mix_out(o_mla, o_gdn, proj, x2, mla_out_norm_w, gdn_w, w_out)
    up, hn, dy, sq = _mlp_fwd(h2, w_mn, w_up, w_down, t2)
    loss = (0.5 / D) * jnp.sum(sq[:, 0, 0])

    dh, dhb, dup, act, dyb, d_mlp_norm = _mlp_bwd(dy, up, h2, w_mn, w_up, w_down)
    g_w_down = _wgrad(act, dyb, "wgrad_down")
    g_w_up = _wgrad(hn, dup, "wgrad_up", column_shards=True)
    do_mla, do_gdn, dz, d_mla_w, d_gdn_w, delta = _mix_bwd(dhb, o_mla, o_gdn, proj, mla_out_norm_w, gdn_w, w_out)
    g_w_out = _wgrad(mix, dhb, "wgrad_out")
    first = ("w_down",)
    second = ("w_out",)
    third = ("w_up", "w_uq", "w_ukv")
    mats = dict(w_up=g_w_up, w_down=g_w_down, w_out=g_w_out)

    def sending(names):
        return _Transfer("exchange", [_slabs(n, mats[n]) for n in names]) if exchange else None

    (dq4, dk4, dv4), got = _attn_bwd(q4, k4, v4, do_mla, delta, lse, B, S, sending(first))
    mats.update(zip(first, got))
    (dql, dkvl, dkpe, dqraw, dkvraw, qn, kvn, d_wqln, d_wkvln, d_qnw, d_knw), got = _mla_pre_bwd(
        proj, cosf, sinf, w_qln, w_kvln, w_uq_p, w_ukv, qnw, knw, dq4, dk4, dv4, sending(second))
    mats.update(zip(second, got))
    mats.update(w_uq=_wgrad(qn, dqraw, "wgrad_uq"), w_ukv=_wgrad(kvn, dkvraw, "wgrad_ukv"))
    (dqg, dkg, dvg, dgb4), got = _gdn_bwd(qg, kg, vg, gates, states, ainv, u4, w4, do_gdn, B, S, sending(third))
    mats.update(zip(third, got))
    dc, dgab, g_conv, d_alog, d_dt = _gdn_pre_bwd(proj, conv_w, alog_l, dt_l, dqg, dkg, dvg, dgb4, S)
    grad_x2, dproj, d_attn_norm = _in_proj_bwd(dc, conv_w, dz, dql, dkvl, dkpe, dgab, w_in_p, dh, x2, w_an, S)
    mats.update(w_in=_wgrad(xn, dproj, "wgrad_in"), conv_w=g_conv)
    if exchange:
        last = ("w_in", "conv_w")
        mats.update(zip(last, _exchange_grads([_slabs(n, mats[n]) for n in last])))
    small = dict(attn_norm_w=d_attn_norm, mlp_norm_w=d_mlp_norm, q_lat_norm_w=d_wqln, kv_lat_norm_w=d_wkvln,
                 q_norm_w=d_qnw, k_norm_w=d_knw, mla_out_norm_w=d_mla_w, a_log=d_alog, dt_bias=d_dt,
                 gdn_norm_w=d_gdn_w)
    return loss, grad_x2.reshape(B, S, D), mats, [small[n] for n, *_ in SMALL_LAYOUT]


BIG = ("w_in", "w_uq", "w_ukv", "conv_w", "w_out", "w_up", "w_down")
ALL_W = ("attn_norm_w", "w_in", "q_lat_norm_w", "w_uq", "kv_lat_norm_w", "w_ukv", "q_norm_w", "k_norm_w",
         "mla_out_norm_w", "conv_w", "a_log", "dt_bias", "gdn_norm_w", "w_out", "mlp_norm_w", "w_up", "w_down")
WIRE_SHAPE = {"w_in": (1024, 384), "w_uq": (256, 128), "conv_w": (16, 256)}


def _pad2(a, rows, cols):
    return jnp.pad(a, [(0, 0)] * (a.ndim - 2) + [(0, rows - a.shape[-2]), (0, cols - a.shape[-1])])


def _cols_to_full(stack, cols):
    return jnp.moveaxis(stack[:, :, :cols], 0, 1).reshape(stack.shape[1], N_DEV * cols)


def _full_to_cols(full, wire_cols):
    r, n = full.shape
    return _pad2(jnp.moveaxis(full.reshape(r, N_DEV, n // N_DEV), 1, 0), r, wire_cols)


def _slabs(name, g):
    if name == "w_in":
        return _w_in_padded_to_slabs(g, WIRE_SHAPE["w_in"][1])
    if name == "w_uq":
        return _full_to_cols(_w_uq_from_headsplit(g), WIRE_SHAPE["w_uq"][1])
    if name == "w_ukv":
        return _full_to_cols(g, g.shape[1] // N_DEV)
    if name == "conv_w":
        return _pad2(_full_to_cols(g.astype(WIRE_DTYPE), g.shape[1] // N_DEV), *WIRE_SHAPE["conv_w"])
    if name == "w_up":
        return g
    return g.reshape(N_DEV, -1, g.shape[-1])


def kernel(x, positions, attn_norm_w, w_in, q_lat_norm_w, w_uq, kv_lat_norm_w, w_ukv, q_norm_w, k_norm_w, mla_out_norm_w, conv_w, a_log, dt_bias, gdn_norm_w, w_out, mlp_norm_w, w_up, w_down, loss_target, m_attn_norm_w, m_w_in, m_q_lat_norm_w, m_w_uq, m_kv_lat_norm_w, m_w_ukv, m_q_norm_w, m_k_norm_w, m_mla_out_norm_w, m_conv_w, m_a_log, m_dt_bias, m_gdn_norm_w, m_w_out, m_mlp_norm_w, m_w_up, m_w_down, v_attn_norm_w, v_w_in, v_q_lat_norm_w, v_w_uq, v_kv_lat_norm_w, v_w_ukv, v_q_norm_w, v_k_norm_w, v_mla_out_norm_w, v_conv_w, v_a_log, v_dt_bias, v_gdn_norm_w, v_w_out, v_mlp_norm_w, v_w_up, v_w_down):
    env = dict(locals())
    W = {n: env[n][0] for n in ALL_W}
    Mo = {n: env["m_" + n][0] for n in ALL_W}
    Vo = {n: env["v_" + n][0] for n in ALL_W}

    two_d = lambda a: a.reshape(1, -1) if a.ndim == 1 else a
    D = x.shape[-1]

    s_in, s_uq, s_ukv, s_conv = _gather_weights([
        _pad2(W["w_in"].astype(WIRE_DTYPE), *WIRE_SHAPE["w_in"]),
        _pad2(W["w_uq"].astype(WIRE_DTYPE), *WIRE_SHAPE["w_uq"]),
        W["w_ukv"].astype(WIRE_DTYPE), _pad2(W["conv_w"], *WIRE_SHAPE["conv_w"])])
    late = [W["w_out"].astype(WIRE_DTYPE), W["w_up"].astype(WIRE_DTYPE), W["w_down"].astype(WIRE_DTYPE)]

    loss, grad_x, parts, gs = _local_step(
        x, positions, loss_target, two_d(W["attn_norm_w"]), _w_in_shards_to_padded(s_in),
        two_d(W["q_lat_norm_w"]), _cols_to_full(s_uq, W["w_uq"].shape[1]), two_d(W["kv_lat_norm_w"]),
        _cols_to_full(s_ukv, W["w_ukv"].shape[1]), two_d(W["q_norm_w"]), two_d(W["k_norm_w"]),
        W["mla_out_norm_w"], _cols_to_full(s_conv[:, :CONV_W], W["conv_w"].shape[1]), two_d(W["a_log"]),
        two_d(W["dt_bias"]), two_d(W["gdn_norm_w"]), None, two_d(W["mlp_norm_w"]), None, None,
        late_shards=late, exchange=True)
    done = {n: _reduce_adamw(parts[n], W[n], Mo[n], Vo[n], "adamw_" + n) for n in BIG}
    names = [n for n, *_ in SMALL_LAYOUT]
    tiles = _gather_small_grads(gs, jnp.full((1, LANES), loss, F32))
    small, loss = _adamw_replicated(tiles, [two_d(W[n]) for n in names], [two_d(Mo[n]) for n in names],
                                    [two_d(Vo[n]) for n in names])
    for i, n in enumerate(names):
        done[n] = [small[kind][i] for kind in range(4)]
    res = [done[n][kind].reshape(env[n].shape) for kind in range(4) for n in ALL_W]
    return (loss, grad_x, *res)
```

```python
import functools

import jax
import jax.numpy as jnp
from jax import lax
from jax.experimental import pallas as pl
from jax.experimental.pallas import tpu as pltpu

F32 = jnp.float32
MXU_DTYPE = jnp.bfloat16
WIRE_DTYPE = jnp.bfloat16
SDS = jax.ShapeDtypeStruct
HIGHEST = lax.Precision.HIGHEST
MESH_ID = pl.DeviceIdType.MESH

D_MODEL = 1024
MLA_HEADS = 4
Q_LORA = 256
KV_LORA = 256
NOPE = 128
ROPE = 64
QK_DIM = NOPE + ROPE
V_DIM = 128
ROPE_THETA = 10000.0
GDN_HEADS = 4
GDN_DIM = 128
GDN_WIDTH = GDN_HEADS * GDN_DIM
CONV_W = 4
CHUNK = 64
D_FF = 4 * D_MODEL
EPS = 1e-6
ATT_SCALE = QK_DIM ** -0.5
GDN_QSCALE = GDN_DIM ** -0.5
N_DEV = 8
ATTN_BLOCK = 512
ATTN_CHAINS = 2
MLP_FWD_SHARDS = 4
MLP_BWD_SHARDS = 4

ADAM_LR = 0.001
ADAM_B1 = 0.9
ADAM_B2 = 0.999
ADAM_EPS = 1e-08
ADAM_WD = 0.01
ADAM_STEP = 10

LANES = 128
SUBLANES = 8
VMEM_LIMIT = 60 * 1024 * 1024

P_GQKV, P_GZ, P_QLAT, P_KVLAT, P_KPE, P_GAB = 0, 1536, 2048, 2304, 2560, 2688
P_WIDTH = 2816
O_QLAT, O_KVLAT, O_KPE, O_GQKV, O_GZ, O_GAB, O_END = 0, 256, 512, 576, 2112, 2624, 2632


def _params(sem=None, vmem=VMEM_LIMIT):
    kw = dict(vmem_limit_bytes=vmem)
    if sem is not None:
        kw["dimension_semantics"] = sem
    return pltpu.CompilerParams(**kw)


def _mm(a, b):
    return jnp.dot(a.astype(MXU_DTYPE), b.astype(MXU_DTYPE), preferred_element_type=F32)


def _mm_nt(a, b):
    return lax.dot_general(a.astype(MXU_DTYPE), b.astype(MXU_DTYPE), (((1,), (1,)), ((), ())),
                           preferred_element_type=F32)


def _mm_tn(a, b):
    return lax.dot_general(a.astype(MXU_DTYPE), b.astype(MXU_DTYPE), (((0,), (0,)), ((), ())),
                           preferred_element_type=F32)


def _split(a):
    hi = a.astype(MXU_DTYPE)
    return hi, (a - hi.astype(F32)).astype(MXU_DTYPE)


def _mm_split(a, b):
    (ah, al), (bh, bl) = a, b
    dot = lambda x, y: jnp.dot(x, y, preferred_element_type=F32)
    if MXU_DTYPE == F32:
        return dot(ah, bh)
    return dot(ah, bh) + dot(ah, bl) + dot(al, bh)


def _mm_exact(a, b):
    return _mm_split(_split(a), _split(b))


def _row_sum(v, on_mxu=False):
    if not on_mxu:
        return jnp.sum(v, axis=-1, keepdims=True)
    d = v.shape[-1]
    ones = jnp.ones((d, LANES), MXU_DTYPE)
    s = sum(jnp.dot(p, ones, preferred_element_type=F32) for p in _split(v))
    return s[:, :d] if d <= LANES else jnp.tile(s, (1, d // LANES))


def _rms(x, w, on_mxu=False):
    r = lax.rsqrt(_row_sum(x * x, on_mxu) * (1.0 / x.shape[-1]) + EPS)
    return x * r * w, r


def _rms_bwd(dy, x, w, r, on_mxu=False):
    xh = x * r
    dyw = dy * w
    dx = r * (dyw - xh * (_row_sum(dyw * xh, on_mxu) * (1.0 / x.shape[-1])))
    dw = jnp.sum(dy * xh, axis=0, keepdims=True)
    return dx, dw


def _l2n(x, scale):
    return x * (lax.rsqrt(_row_sum(x * x) + EPS) * scale)


def _l2n_bwd(dy, x, scale):
    r = lax.rsqrt(_row_sum(x * x) + EPS)
    xh = x * r
    return (scale * r) * (dy - xh * _row_sum(dy * xh))


def _rot(t):
    return jnp.concatenate([-t[:, ROPE // 2:], t[:, :ROPE // 2]], axis=-1)


def _rot_t(t):
    return jnp.concatenate([t[:, ROPE // 2:], -t[:, :ROPE // 2]], axis=-1)


def _rope(t, cos, sin):
    return t * cos + _rot(t) * sin


def _rope_bwd(d, cos, sin):
    return d * cos + _rot_t(d * sin)


def _sigmoid(x):
    return jax.nn.sigmoid(x)


def _shift_down(x, halo, j):
    if j == 0:
        return x
    xr = pltpu.roll(x, j, 0)
    hr = pltpu.roll(halo, j, 0)
    row = lax.broadcasted_iota(jnp.int32, halo.shape, 0)
    top = jnp.where(row < j, hr, xr[:SUBLANES])
    return jnp.concatenate([top, xr[SUBLANES:]], axis=0)


def _shift_up(x, nxt, j):
    if j == 0:
        return x
    n = x.shape[0]
    xr = pltpu.roll(x, n - j, 0)
    nr = pltpu.roll(nxt, SUBLANES - j, 0)
    row = lax.broadcasted_iota(jnp.int32, nxt.shape, 0)
    bot = jnp.where(row >= SUBLANES - j, nr, xr[n - SUBLANES:])
    return jnp.concatenate([xr[:n - SUBLANES], bot], axis=0)


def _chunk_cumsum(y, row_in_chunk):
    s = 1
    while s < CHUNK:
        y = y + jnp.where(row_in_chunk >= s, pltpu.roll(y, s, 0), 0.0)
        s *= 2
    return y


def _chunk_rev_cumsum(y, row_in_chunk):
    n = y.shape[0]
    s = 1
    while s < CHUNK:
        y = y + jnp.where(row_in_chunk + s < CHUNK, pltpu.roll(y, n - s, 0), 0.0)
        s *= 2
    return y


def _together(generators):
    alive = list(generators)
    while alive:
        nxt = []
        for g in alive:
            try:
                next(g)
                nxt.append(g)
            except StopIteration:
                pass
        alive = nxt
        yield


def _lockstep(generators):
    for _ in _together(generators):
        pass


def _pick_lane(tile, lane, idx):
    return jnp.sum(jnp.where(lane == idx, tile, 0.0), axis=-1, keepdims=True)


def _divisor_tile(n, cap, unit=LANES):
    best = unit
    t = unit
    while t <= min(n, cap):
        if n % t == 0:
            best = t
        t += unit
    return n if n <= cap else best


def _in_proj(x2, w_an, w_in_p):
    T, D = x2.shape
    N = w_in_p.shape[1]
    tm = min(512, T)

    def body(x_ref, wn_ref, w_ref, proj_ref, xn_ref):
        xn, _ = _rms(x_ref[...], wn_ref[...])
        xn = xn.astype(MXU_DTYPE)
        xn_ref[...] = xn
        proj_ref[...] = jnp.dot(xn, w_ref[...], preferred_element_type=F32)

    return pl.pallas_call(
        body, grid=(T // tm,), name="in_proj",
        in_specs=[pl.BlockSpec((tm, D), lambda i: (i, 0)), pl.BlockSpec((1, D), lambda i: (0, 0)),
                  pl.BlockSpec((D, N), lambda i: (0, 0))],
        out_specs=[pl.BlockSpec((tm, N), lambda i: (i, 0)), pl.BlockSpec((tm, D), lambda i: (i, 0))],
        out_shape=[SDS((T, N), F32), SDS((T, D), MXU_DTYPE)],
        compiler_params=_params(("arbitrary",)),
    )(x2, w_an, w_in_p)


def _mla_pre(proj, cosf, sinf, w_qln, w_kvln, w_uq_p, w_ukv, qnw, knw, transfer=None):
    T = proj.shape[0]
    tm = min(256, T)
    H = MLA_HEADS

    def body(ql_ref, kvl_ref, kpe_ref, cos_ref, sin_ref, wq_ref, wkv_ref, uq_ref, ukv_ref, qnw_ref, knw_ref,
             q_out, k_out, v_out):
        rms = functools.partial(_rms, on_mxu=True)
        cos, sin = cos_ref[...], sin_ref[...]
        qnw_, knw_ = qnw_ref[...], knw_ref[...]
        qn, _ = rms(ql_ref[...], wq_ref[...])
        kvn, _ = rms(kvl_ref[...], wkv_ref[...])
        qraw = _mm(qn, uq_ref[...])
        kvraw = _mm(kvn, ukv_ref[...])
        kpe = _rope(rms(kpe_ref[...][:, :ROPE], knw_[:, NOPE:])[0], cos, sin)
        for h in range(H):
            qn_h = rms(qraw[:, h * NOPE:(h + 1) * NOPE], qnw_[:, :NOPE])[0]
            qp_h = _rope(rms(qraw[:, H * NOPE + h * ROPE:H * NOPE + (h + 1) * ROPE], qnw_[:, NOPE:])[0], cos, sin)
            q_out[h] = (jnp.concatenate([qn_h, qp_h], axis=-1) * ATT_SCALE).astype(MXU_DTYPE)
            kn_h = rms(kvraw[:, h * 256:h * 256 + NOPE], knw_[:, :NOPE])[0]
            k_out[h] = jnp.concatenate([kn_h, kpe], axis=-1).astype(MXU_DTYPE)
            v_out[h] = kvraw[:, h * 256 + NOPE:(h + 1) * 256].astype(MXU_DTYPE)

    full = lambda a: pl.BlockSpec(a.shape, lambda i: (0,) * a.ndim)
    return _call_beside(
        body, transfer, grid=(T // tm,), name="mla_pre", scratch_shapes=[], semantics=("arbitrary",),
        args=(proj, proj, proj, cosf, sinf, w_qln, w_kvln, w_uq_p, w_ukv, qnw, knw),
        in_specs=[pl.BlockSpec((tm, 256), lambda i: (i, P_QLAT // 256)),
                  pl.BlockSpec((tm, 256), lambda i: (i, P_KVLAT // 256)),
                  pl.BlockSpec((tm, 128), lambda i: (i, P_KPE // 128)),
                  pl.BlockSpec((tm, ROPE), lambda i: (i, 0)), pl.BlockSpec((tm, ROPE), lambda i: (i, 0)),
                  full(w_qln), full(w_kvln), full(w_uq_p), full(w_ukv), full(qnw), full(knw)],
        out_specs=[pl.BlockSpec((H, tm, QK_DIM), lambda i: (0, i, 0)),
                   pl.BlockSpec((H, tm, QK_DIM), lambda i: (0, i, 0)),
                   pl.BlockSpec((H, tm, V_DIM), lambda i: (0, i, 0))],
        out_shape=[SDS((H, T, QK_DIM), MXU_DTYPE), SDS((H, T, QK_DIM), MXU_DTYPE), SDS((H, T, V_DIM), MXU_DTYPE)])


def _attn_fwd(q4, k4, v4, B, S, transfer=None):
    H = MLA_HEADS
    bq = min(ATTN_BLOCK, S)
    nq = S // bq
    rows = bq // ATTN_CHAINS

    def body(q_ref, k_ref, v_ref, o_ref, lse_ref):
        col = lax.broadcasted_iota(jnp.int32, (rows, bq), 1)
        row = lax.broadcasted_iota(jnp.int32, (rows, bq), 0)

        def q_step(qi, carry):
            qs = pl.multiple_of(qi * bq, bq)
            qsub = [q_ref[0, pl.ds(qs + j * rows, rows), :] for j in range(ATTN_CHAINS)]

            def k_block(ks, cs, diagonal):
                k = k_ref[0, pl.ds(ks, bq), :]
                v = v_ref[0, pl.ds(ks, bq), :]
                out = [None] * ATTN_CHAINS

                def chain(j):
                    m, l, acc = cs[j]
                    s = _mm_nt(qsub[j], k)
                    yield
                    if diagonal:
                        s = jnp.where(col <= row + j * rows, s, -jnp.inf)
                    m_new = jnp.maximum(m, jnp.max(s, axis=-1, keepdims=True))
                    p = jnp.exp(s - m_new)
                    a = jnp.exp(m - m_new)
                    l_new = a * l + jnp.sum(p, axis=-1, keepdims=True)
                    yield
                    out[j] = (m_new, l_new, a * acc + _mm(p, v))

                _lockstep([chain(j) for j in range(ATTN_CHAINS)])
                return tuple(out)

            init = tuple((jnp.full((rows, 1), -jnp.inf, F32), jnp.zeros((rows, 1), F32),
                          jnp.zeros((rows, V_DIM), F32)) for _ in range(ATTN_CHAINS))
            cs = lax.fori_loop(0, qi, lambda kj, c: k_block(pl.multiple_of(kj * bq, bq), c, False), init)
            for j, (m, l, acc) in enumerate(k_block(qs, cs, True)):
                o_ref[0, pl.ds(qs + j * rows, rows), :] = acc / l
                lse_ref[0, pl.ds(qs + j * rows, rows), :] = m + jnp.log(l)
            return carry

        lax.fori_loop(0, nq, q_step, 0)

    spec = lambda d: pl.BlockSpec((1, S, d), lambda h, b: (h, b, 0))
    return _call_beside(
        body, transfer, grid=(H, B), name="attn_fwd",
        in_specs=[spec(QK_DIM), spec(QK_DIM), spec(V_DIM)],
        out_specs=[spec(V_DIM), spec(1)],
        out_shape=[SDS((H, B * S, V_DIM), F32), SDS((H, B * S, 1), F32)],
        scratch_shapes=[], semantics=("arbitrary", "arbitrary"), args=(q4, k4, v4))


def _conv_taps(u, halo, w):
    sh = [_shift_down(u, halo, j) for j in range(CONV_W)]
    c = w[0:1] * sh[3] + w[1:2] * sh[2] + w[2:3] * sh[1] + w[3:4] * sh[0]
    return c, sh


def _gate_values(gab, alog_l, dt_l, lane):
    g = -jnp.exp(alog_l) * jax.nn.softplus(gab + dt_l)
    g = jnp.where(lane < GDN_HEADS, g, 0.0)
    beta = jnp.where((lane >= GDN_HEADS) & (lane < 2 * GDN_HEADS), _sigmoid(gab), 0.0)
    return g, beta


def _gdn_pre(proj, conv_w, alog_l, dt_l, S):
    T = proj.shape[0]
    tm = min(256, T)
    tiles_per_seq = S // tm
    C3 = 3 * GDN_WIDTH
    H = GDN_HEADS

    def body(u_ref, halo_ref, gab_ref, w_ref, alog_ref, dt_ref, q_out, k_out, v_out, gates_out):
        i = pl.program_id(0)
        halo = jnp.where(i % tiles_per_seq == 0, 0.0, halo_ref[...])
        c, _ = _conv_taps(u_ref[...], halo, w_ref[...])
        a = c * _sigmoid(c)
        for h in range(H):
            xq = a[:, h * GDN_DIM:(h + 1) * GDN_DIM]
            xk = a[:, GDN_WIDTH + h * GDN_DIM:GDN_WIDTH + (h + 1) * GDN_DIM]
            q_out[h] = _l2n(xq, GDN_QSCALE)
            k_out[h] = _l2n(xk, 1.0)
            v_out[h] = a[:, 2 * GDN_WIDTH + h * GDN_DIM:2 * GDN_WIDTH + (h + 1) * GDN_DIM]
        lane = lax.broadcasted_iota(jnp.int32, (tm, LANES), 1)
        ric = lax.broadcasted_iota(jnp.int32, (tm, LANES), 0) % CHUNK
        g, beta = _gate_values(gab_ref[...], alog_ref[...], dt_ref[...], lane)
        gates_out[...] = _chunk_cumsum(g, ric) + beta

    hspec = pl.BlockSpec((H, tm, GDN_DIM), lambda i: (0, i, 0))
    return pl.pallas_call(
        body, grid=(T // tm,), name="gdn_pre",
        in_specs=[pl.BlockSpec((tm, C3), lambda i: (i, 0)),
                  pl.BlockSpec((SUBLANES, C3), lambda i: (jnp.maximum(i * (tm // SUBLANES) - 1, 0), 0)),
                  pl.BlockSpec((tm, LANES), lambda i: (i, P_GAB // LANES)),
                  pl.BlockSpec((CONV_W, C3), lambda i: (0, 0)),
                  pl.BlockSpec((1, LANES), lambda i: (0, 0)), pl.BlockSpec((1, LANES), lambda i: (0, 0))],
        out_specs=[hspec, hspec, hspec, pl.BlockSpec((tm, LANES), lambda i: (i, 0))],
        out_shape=[SDS((H, T, GDN_DIM), F32)] * 3 + [SDS((T, LANES), F32)],
        compiler_params=_params(("arbitrary",)),
    )(proj, proj, proj, conv_w, alog_l, dt_l)


def _unit_lower_inverses(Ls, eye):
    Ps = [eye - L for L in Ls]
    Ms = [_split(-L) for L in Ls]
    for _ in range(5):
        sq = [_mm_split(m, m) for m in Ms]
        Ms = [_split(s) for s in sq]
        Ps = [p + _mm_split(_split(p), m) for p, m in zip(Ps, Ms)]
    return Ps


def _chunk_decays(gt, lane, h, ri, ci, rcol):
    Gc = _pick_lane(gt, lane, h)
    bt = _pick_lane(gt, lane, h + GDN_HEADS)
    Gb = jnp.broadcast_to(Gc, (CHUNK, CHUNK))
    Gam = jnp.where(ri >= ci, jnp.exp(Gb - Gb.T), 0.0)
    Gl = jnp.sum(jnp.where(rcol == CHUNK - 1, Gc, 0.0), axis=0, keepdims=True)
    return Gc, bt, Gam, jnp.exp(Gc), jnp.exp(Gl - Gc), jnp.exp(Gl)


GDN_FWD_UNROLL = 16
GDN_BWD_UNROLL = 8
GDN_RECUR_STEPS_PER_STAGE = 2


def _gdn_fwd(qg, kg, vg, gates, B, S, transfer=None):
    H, D, C = GDN_HEADS, GDN_DIM, CHUNK
    NC = S // C
    P = 2 if B % 2 == 0 else 1
    Sb, NCb = P * S, P * NC
    U = GDN_FWD_UNROLL if NCb % GDN_FWD_UNROLL == 0 else 1
    NG = NCb // U

    def body(q_ref, k_ref, v_ref, g_ref, o_ref, st_ref, ai_ref, u_ref, w_ref, q2_s, au_s, bc_s, w2_s, el_s):
        h = pl.program_id(0)
        lane = lax.broadcasted_iota(jnp.int32, (C, LANES), 1)
        ri = lax.broadcasted_iota(jnp.int32, (C, C), 0)
        ci = lax.broadcasted_iota(jnp.int32, (C, C), 1)
        rcol = lax.broadcasted_iota(jnp.int32, (C, 1), 0)
        eye = (ri == ci).astype(F32)

        def group(gi, c):
            ns = [gi * U + j for j in range(U)]
            css = [pl.multiple_of(n * C, C) for n in ns]
            qs = [q_ref[0, pl.ds(cs, C), :] for cs in css]
            ks = [k_ref[0, pl.ds(cs, C), :] for cs in css]
            vs = [v_ref[0, pl.ds(cs, C), :] for cs in css]
            decs = [_chunk_decays(g_ref[pl.ds(cs, C), :], lane, h, ri, ci, rcol) for cs in css]
            qks = [_mm_nt(jnp.concatenate([q, k], axis=0), k) for q, k in zip(qs, ks)]
            ainvs = _unit_lower_inverses(
                [jnp.where(ri > ci, d[1] * qk[C:] * d[2], 0.0) for qk, d in zip(qks, decs)], eye)
            sols = [_mm_exact(a, jnp.concatenate([v * d[1], k * (d[1] * d[3])], axis=-1))
                    for a, k, v, d in zip(ainvs, ks, vs, decs)]
            atuw = [_mm(qk[:C] * d[2], sol) for qk, d, sol in zip(qks, decs, sols)]
            kduw = [_mm_tn(k * d[4], sol) for k, d, sol in zip(ks, decs, sols)]
            for n, cs, q, a, sol, au, ku, (Gc, bt, Gam, e, f, eL) in zip(ns, css, qs, ainvs, sols, atuw, kduw, decs):
                u_ref[0, pl.ds(cs, C), :] = sol[:, :D]
                w_ref[0, pl.ds(cs, C), :] = sol[:, D:]
                au_s[pl.ds(cs, C), :] = au[:, :D]
                q2_s[pl.ds(cs, C), :] = q * e - au[:, D:]
                bc_s[n] = ku[:, :D]
                w2_s[n] = ku[:, D:]
                el_s[n] = jnp.broadcast_to(eL, (SUBLANES, LANES))
                ai_ref[0, n] = a.T
            return c

        lax.fori_loop(0, NG, group, 0)

        def step(n, states):
            new = []
            for p, S_ in enumerate(states):
                m = p * NC + n
                cs = pl.multiple_of(m * C, C)
                o_ref[0, pl.ds(cs, C), :] = _mm(q2_s[pl.ds(cs, C), :], S_) + au_s[pl.ds(cs, C), :]
                st_ref[0, m] = S_
                new.append(S_ * el_s[m, 0:1, :] + bc_s[m] - _mm(w2_s[m], S_))
            return tuple(new)

        lax.fori_loop(0, NC, step, tuple(jnp.zeros((D, D), F32) for _ in range(P)))

    spec = pl.BlockSpec((1, Sb, D), lambda h, b: (h, b, 0))
    return _call_beside(
        body, transfer, grid=(H, B // P), name="gdn_fwd",
        in_specs=[spec, spec, spec, pl.BlockSpec((Sb, LANES), lambda h, b: (b, 0))],
        out_specs=[spec, pl.BlockSpec((1, NCb, D, D), lambda h, b: (h, b, 0, 0)),
                   pl.BlockSpec((1, NCb, C, C), lambda h, b: (h, b, 0, 0)), spec, spec],
        out_shape=[SDS((H, B * S, D), F32), SDS((H, B * NC, D, D), F32), SDS((H, B * NC, C, C), F32),
                   SDS((H, B * S, D), F32), SDS((H, B * S, D), F32)],
        scratch_shapes=[pltpu.VMEM((Sb, D), F32), pltpu.VMEM((Sb, D), F32), pltpu.VMEM((NCb, D, D), F32),
                        pltpu.VMEM((NCb, D, D), F32), pltpu.VMEM((NCb, SUBLANES, LANES), F32)],
        semantics=("arbitrary", "arbitrary"), args=(qg, kg, vg, gates))


def _mix_out(o_mla, o_gdn, proj, x2, mla_w, gdn_w, w_out):
    T, D = x2.shape
    tm = min(512, T)
    H = MLA_HEADS

    def body(om_ref, og_ref, z_ref, x_ref, mw_ref, gw_ref, w_ref, h_ref, mix_ref):
        z = z_ref[...]
        parts = [_rms(om_ref[h], mw_ref[h:h + 1, :])[0] for h in range(H)]
        for h in range(GDN_HEADS):
            zh = z[:, h * GDN_DIM:(h + 1) * GDN_DIM]
            parts.append(_rms(og_ref[h], gw_ref[...])[0] * (zh * _sigmoid(zh)))
        mix = jnp.concatenate(parts, axis=-1).astype(MXU_DTYPE)
        mix_ref[...] = mix
        h_ref[...] = x_ref[...] + jnp.dot(mix, w_ref[...], preferred_element_type=F32)

    hspec = pl.BlockSpec((H, tm, V_DIM), lambda i: (0, i, 0))
    return pl.pallas_call(
        body, grid=(T // tm,), name="mix_out",
        in_specs=[hspec, hspec, pl.BlockSpec((tm, GDN_WIDTH), lambda i: (i, P_GZ // GDN_WIDTH)),
                  pl.BlockSpec((tm, D), lambda i: (i, 0)),
                  pl.BlockSpec((H, V_DIM), lambda i: (0, 0)), pl.BlockSpec((1, GDN_DIM), lambda i: (0, 0)),
                  pl.BlockSpec((D, D), lambda i: (0, 0))],
        out_specs=[pl.BlockSpec((tm, D), lambda i: (i, 0)), pl.BlockSpec((tm, D), lambda i: (i, 0))],
        out_shape=[SDS((T, D), F32), SDS((T, D), MXU_DTYPE)],
        compiler_params=_params(("arbitrary",)),
    )(o_mla, o_gdn, proj, x2, mla_w, gdn_w, w_out)


def _mlp_fwd(h2, w_mn, w_up, w_down, target):
    T, D = h2.shape
    ns, _, ts = w_up.shape
    F = ns * ts
    tm = min(512, T)
    G = MLP_FWD_SHARDS
    tf, nf = G * ts, ns // G

    def body(h_ref, wn_ref, up_w, down_w, t_ref, up_ref, hn_ref, dy_ref, loss_ref, y_acc):
        j = pl.program_id(1)

        @pl.when(j == 0)
        def _():
            hn_ref[...] = _rms(h_ref[...], wn_ref[...])[0].astype(MXU_DTYPE)
            y_acc[...] = h_ref[...]

        parts = []
        for c in range(G):
            up = jnp.dot(hn_ref[...], up_w[c], preferred_element_type=F32)
            up_ref[:, c * ts:(c + 1) * ts] = up.astype(MXU_DTYPE)
            r = jnp.maximum(up, 0.0)
            parts.append(_mm(r * r, down_w[c * ts:(c + 1) * ts, :]))
        y_acc[...] += functools.reduce(jnp.add, parts)

        @pl.when(j == nf - 1)
        def _():
            err = y_acc[...] - t_ref[...]
            dy_ref[...] = err / D
            loss_ref[...] = jnp.full((1, SUBLANES, LANES), jnp.sum(err * err), F32)

    return pl.pallas_call(
        body, grid=(T // tm, nf), name="mlp_fwd",
        in_specs=[pl.BlockSpec((tm, D), lambda i, j: (i, 0)), pl.BlockSpec((1, D), lambda i, j: (0, 0)),
                  pl.BlockSpec((G, D, ts), lambda i, j: (j, 0, 0)), pl.BlockSpec((tf, D), lambda i, j: (j, 0)),
                  pl.BlockSpec((tm, D), lambda i, j: (i, 0))],
        out_specs=[pl.BlockSpec((tm, tf), lambda i, j: (i, j)), pl.BlockSpec((tm, D), lambda i, j: (i, 0)),
                   pl.BlockSpec((tm, D), lambda i, j: (i, 0)),
                   pl.BlockSpec((1, SUBLANES, LANES), lambda i, j: (i, 0, 0))],
        out_shape=[SDS((T, F), MXU_DTYPE), SDS((T, D), MXU_DTYPE), SDS((T, D), F32),
                   SDS((T // tm, SUBLANES, LANES), F32)],
        scratch_shapes=[pltpu.VMEM((tm, D), F32)],
        compiler_params=_params(("arbitrary", "arbitrary")),
    )(h2, w_mn, w_up, w_down, target)


def _mlp_bwd(dy, up, h2, w_mn, w_up, w_down):
    T, D = h2.shape
    ns, _, ts = w_up.shape
    F = ns * ts
    tm = min(512, T)
    G = MLP_BWD_SHARDS
    tf, nf = G * ts, ns // G

    def body(dy_ref, up_ref, h_ref, wn_ref, up_w, down_w, dh_ref, dhb_ref, dup_ref, act_ref, dyb_ref, dwn_ref, acc):
        i, j = pl.program_id(0), pl.program_id(1)

        @pl.when((i == 0) & (j == 0))
        def _():
            dwn_ref[...] = jnp.zeros_like(dwn_ref)

        @pl.when(j == 0)
        def _():
            acc[...] = jnp.zeros_like(acc)
            dyb_ref[...] = dy_ref[...].astype(MXU_DTYPE)

        parts = []
        for c in range(G):
            cols = slice(c * ts, (c + 1) * ts)
            r = jnp.maximum(up_ref[:, cols].astype(F32), 0.0)
            act_ref[:, cols] = (r * r).astype(MXU_DTYPE)
            dup = (_mm_nt(dyb_ref[...], down_w[cols, :]) * (2.0 * r)).astype(MXU_DTYPE)
            dup_ref[:, cols] = dup
            parts.append(_mm_nt(dup, up_w[c]))
        acc[...] += functools.reduce(jnp.add, parts)

        @pl.when(j == nf - 1)
        def _():
            hv = h_ref[...]
            _, rr = _rms(hv, wn_ref[...])
            dx, dw = _rms_bwd(acc[...], hv, wn_ref[...], rr)
            dh = dy_ref[...] + dx
            dh_ref[...] = dh
            dhb_ref[...] = dh.astype(MXU_DTYPE)
            dwn_ref[...] += dw

    row = lambda i, j: (i, 0)
    return pl.pallas_call(
        body, grid=(T // tm, nf), name="mlp_bwd",
        in_specs=[pl.BlockSpec((tm, D), row), pl.BlockSpec((tm, tf), lambda i, j: (i, j)), pl.BlockSpec((tm, D), row),
                  pl.BlockSpec((1, D), lambda i, j: (0, 0)),
                  pl.BlockSpec((G, D, ts), lambda i, j: (j, 0, 0)), pl.BlockSpec((tf, D), lambda i, j: (j, 0))],
        out_specs=[pl.BlockSpec((tm, D), row), pl.BlockSpec((tm, D), row),
                   pl.BlockSpec((tm, tf), lambda i, j: (i, j)), pl.BlockSpec((tm, tf), lambda i, j: (i, j)),
                   pl.BlockSpec((tm, D), row), pl.BlockSpec((1, D), lambda i, j: (0, 0))],
        out_shape=[SDS((T, D), F32), SDS((T, D), MXU_DTYPE), SDS((T, F), MXU_DTYPE), SDS((T, F), MXU_DTYPE),
                   SDS((T, D), MXU_DTYPE), SDS((1, D), F32)],
        scratch_shapes=[pltpu.VMEM((tm, D), F32)],
        compiler_params=_params(("arbitrary", "arbitrary")),
    )(dy, up, h2, w_mn, w_up, w_down)


def _mix_bwd(dhb, o_mla, o_gdn, proj, mla_w, gdn_w, w_out):
    T, D = dhb.shape
    tm = min(512, T)
    H = MLA_HEADS

    def body(dh_ref, om_ref, og_ref, z_ref, mw_ref, gw_ref, w_ref, dom_ref, dog_ref, dz_ref, dmw_ref, dgw_ref,
             delta_ref):
        @pl.when(pl.program_id(0) == 0)
        def _():
            dmw_ref[...] = jnp.zeros_like(dmw_ref)
            dgw_ref[...] = jnp.zeros_like(dgw_ref)

        dmix = _mm_nt(dh_ref[...], w_ref[...])
        z = z_ref[...]
        dmw, dzs = [], []
        dgw = jnp.zeros((1, GDN_DIM), F32)
        for h in range(H):
            o = om_ref[h]
            w = mw_ref[h:h + 1, :]
            _, r = _rms(o, w)
            dx, dw = _rms_bwd(dmix[:, h * V_DIM:(h + 1) * V_DIM], o, w, r)
            dom_ref[h] = dx.astype(MXU_DTYPE)
            delta_ref[h] = jnp.sum(dx * o, axis=-1, keepdims=True)
            dmw.append(dw)
        for h in range(GDN_HEADS):
            o = og_ref[h]
            w = gw_ref[...]
            zh = z[:, h * GDN_DIM:(h + 1) * GDN_DIM]
            sg = _sigmoid(zh)
            yn, r = _rms(o, w)
            dy = dmix[:, H * V_DIM + h * GDN_DIM:H * V_DIM + (h + 1) * GDN_DIM]
            dzs.append(dy * yn * (sg * (1.0 + zh * (1.0 - sg))))
            dx, dw = _rms_bwd(dy * (zh * sg), o, w, r)
            dog_ref[h] = dx.astype(MXU_DTYPE)
            dgw = dgw + dw
        dz_ref[...] = jnp.concatenate(dzs, axis=-1).astype(MXU_DTYPE)
        dmw_ref[...] += jnp.concatenate(dmw, axis=0)
        dgw_ref[...] += dgw

    hspec = pl.BlockSpec((H, tm, V_DIM), lambda i: (0, i, 0))
    return pl.pallas_call(
        body, grid=(T // tm,), name="mix_bwd",
        in_specs=[pl.BlockSpec((tm, D), lambda i: (i, 0)), hspec, hspec,
                  pl.BlockSpec((tm, GDN_WIDTH), lambda i: (i, P_GZ // GDN_WIDTH)),
                  pl.BlockSpec((H, V_DIM), lambda i: (0, 0)), pl.BlockSpec((1, GDN_DIM), lambda i: (0, 0)),
                  pl.BlockSpec((D, D), lambda i: (0, 0))],
        out_specs=[hspec, hspec, pl.BlockSpec((tm, GDN_WIDTH), lambda i: (i, 0)),
                   pl.BlockSpec((H, V_DIM), lambda i: (0, 0)), pl.BlockSpec((1, GDN_DIM), lambda i: (0, 0)),
                   pl.BlockSpec((H, tm, 1), lambda i: (0, i, 0))],
        out_shape=[SDS((H, T, V_DIM), MXU_DTYPE), SDS((H, T, GDN_DIM), MXU_DTYPE), SDS((T, GDN_WIDTH), MXU_DTYPE),
                   SDS((H, V_DIM), F32), SDS((1, GDN_DIM), F32), SDS((H, T, 1), F32)],
        compiler_params=_params(("arbitrary",)),
    )(dhb, o_mla, o_gdn, proj, mla_w, gdn_w, w_out)


def _attn_bwd(q4, k4, v4, do4, delta4, lse4, B, S, transfer=None):
    H = MLA_HEADS
    bq = min(ATTN_BLOCK, S)
    nq = S // bq
    rows = bq // ATTN_CHAINS

    def body(q_ref, k_ref, v_ref, do_ref, delta_ref, lse_ref, dq_ref, dk_ref, dv_ref):
        dq_ref[...] = jnp.zeros_like(dq_ref)
        dk_ref[...] = jnp.zeros_like(dk_ref)
        dv_ref[...] = jnp.zeros_like(dv_ref)

        col = lax.broadcasted_iota(jnp.int32, (rows, bq), 1)
        row = lax.broadcasted_iota(jnp.int32, (rows, bq), 0)

        def k_step(kj, carry):
            ks = pl.multiple_of(kj * bq, bq)
            k = k_ref[0, pl.ds(ks, bq), :]
            v = v_ref[0, pl.ds(ks, bq), :]

            def q_block(qs, diagonal):
                dks, dvs = [None] * ATTN_CHAINS, [None] * ATTN_CHAINS

                def chain(j):
                    sl = pl.ds(qs + j * rows, rows)
                    q = q_ref[0, sl, :]
                    do = do_ref[0, sl, :].astype(MXU_DTYPE)
                    s = _mm_nt(q, k)
                    dp = _mm_nt(do, v)
                    yield
                    p = jnp.exp(s - lse_ref[0, sl, :])
                    if diagonal:
                        p = jnp.where(col <= row + j * rows, p, 0.0)
                    ds = p * (dp - delta_ref[0, sl, :])
                    yield
                    dvs[j] = _mm_tn(p, do)
                    dks[j] = _mm_tn(ds, q)
                    dq_ref[0, sl, :] += _mm(ds, k)

                _lockstep([chain(j) for j in range(ATTN_CHAINS)])
                dv_ref[0, pl.ds(ks, bq), :] += functools.reduce(jnp.add, dvs)
                dk_ref[0, pl.ds(ks, bq), :] += functools.reduce(jnp.add, dks)

            q_block(ks, True)

            def q_step(qi, c):
                q_block(pl.multiple_of(qi * bq, bq), False)
                return c

            lax.fori_loop(kj + 1, nq, q_step, 0)
            return carry

        lax.fori_loop(0, nq, k_step, 0)

    spec = lambda d: pl.BlockSpec((1, S, d), lambda h, b: (h, b, 0))
    return _call_beside(
        body, transfer, grid=(H, B), name="attn_bwd",
        in_specs=[spec(QK_DIM), spec(QK_DIM), spec(V_DIM), spec(V_DIM), spec(1), spec(1)],
        out_specs=[spec(QK_DIM), spec(QK_DIM), spec(V_DIM)],
        out_shape=[SDS((H, B * S, QK_DIM), F32), SDS((H, B * S, QK_DIM), F32), SDS((H, B * S, V_DIM), F32)],
        scratch_shapes=[], semantics=("arbitrary", "arbitrary"),
        args=(q4, k4, v4, do4, delta4, lse4))


def _gdn_bwd(qg, kg, vg, gates, states, ainv, u4, w4, do4, B, S, transfer=None):
    H, D, C = GDN_HEADS, GDN_DIM, CHUNK
    NC = S // C
    U = GDN_BWD_UNROLL if NC % GDN_BWD_UNROLL == 0 else 1
    NG = NC // U

    def body(q_ref, k_ref, v_ref, g_ref, st_ref, ai_ref, u_ref, w_ref, do_ref, dq_ref, dk_ref, dv_ref, dgb_ref,
             kd_s, x1_s, x2_s, el_s, dvn_s, ds_s, w2t_s):
        h = pl.program_id(0)
        lane = lax.broadcasted_iota(jnp.int32, (C, LANES), 1)
        ri = lax.broadcasted_iota(jnp.int32, (C, C), 0)
        ci = lax.broadcasted_iota(jnp.int32, (C, C), 1)
        rcol = lax.broadcasted_iota(jnp.int32, (C, 1), 0)

        def rsum(a):
            return jnp.sum(a, axis=-1, keepdims=True)

        def prepare(n):
            cs = n * C
            q = q_ref[0, pl.ds(cs, C), :]
            k = k_ref[0, pl.ds(cs, C), :]
            do = do_ref[0, pl.ds(cs, C), :]
            Gc, bt, Gam, e, f, eL = _chunk_decays(g_ref[pl.ds(cs, C), :], lane, h, ri, ci, rcol)
            At = _mm_nt(q, k) * Gam
            yield
            x1 = _mm_tn(At, do)
            x2 = _mm_tn(q * e, do)
            kd = k * f
            w = w_ref[0, pl.ds(cs, C), :]
            yield
            x1_s[pl.ds(cs, C), :] = x1
            x2_s[n] = x2 - _mm_tn(w, x1)
            w2t_s[n] = _mm_tn(w, kd)
            kd_s[pl.ds(cs, C), :] = kd
            el_s[n] = jnp.broadcast_to(eL, (SUBLANES, LANES))

        def recur(n, dS):
            cs = n * C
            ds_s[n] = dS
            dvn_s[pl.ds(cs, C), :] = x1_s[pl.ds(cs, C), :] + _mm(kd_s[pl.ds(cs, C), :], dS)
            return x2_s[n] + el_s[n, 0:1, :] * dS - _mm(w2t_s[n], dS)

        def local(n):
            cs = n * C
            q = q_ref[0, pl.ds(cs, C), :]
            k = k_ref[0, pl.ds(cs, C), :]
            v = v_ref[0, pl.ds(cs, C), :]
            do = do_ref[0, pl.ds(cs, C), :]
            u = u_ref[0, pl.ds(cs, C), :]
            w = w_ref[0, pl.ds(cs, C), :]
            dvn = dvn_s[pl.ds(cs, C), :]
            dS = ds_s[n]
            Gc, bt, Gam, e, f, eL = _chunk_decays(g_ref[pl.ds(cs, C), :], lane, h, ri, ci, rcol)
            S0 = st_ref[0, n]
            AinvT = ai_ref[0, n]
            qk = _mm_nt(jnp.concatenate([q, k], axis=0), k)
            QK, KK = qk[:C], qk[C:]
            be = bt * e
            sol = jnp.concatenate([u, w], axis=-1)
            vn = u - _mm(w, S0)
            yield
            dAt = jnp.where(ri >= ci, _mm_nt(do, vn), 0.0)
            dqd = _mm_nt(do, S0)
            dw = -_mm_nt(dvn, S0)
            dkd = _mm_nt(vn, dS)
            deL = jnp.sum(rsum(dS * S0), axis=0, keepdims=True)
            yield
            dR = _mm_exact(AinvT, jnp.concatenate([dvn, dw], axis=-1))
            dR1, dR2 = dR[:, :D], dR[:, D:]
            yield
            dL = jnp.where(ri > ci, -_mm_nt(dR, sol), 0.0)
            yield
            dv_ref[0, pl.ds(cs, C), :] = dR1 * bt
            r2 = rsum(dR2 * k)
            X = dL * Gam
            dbt = rsum(dR1 * v) + r2 * e + rsum(X * KK)
            de = r2 * bt + rsum(dqd * q)
            dKK = X * bt
            dQK = dAt * Gam
            dq_ref[0, pl.ds(cs, C), :] = _mm(dQK, k) + dqd * e
            dk_ref[0, pl.ds(cs, C), :] = dR2 * be + _mm(dKK + dKK.T, k) + _mm_tn(dQK, q) + dkd * f
            df = rsum(dkd * k)
            Z = (dL * (bt * KK) + dAt * QK) * Gam
            dG = rsum(Z) - rsum(Z.T) + de * e - df * f
            dGl = jnp.sum(df * f, axis=0, keepdims=True) + deL * eL
            dG = dG + jnp.where(rcol == C - 1, dGl, 0.0)
            dgb_ref[0, pl.ds(cs, C), :] = jnp.where(lane == 0, dG, jnp.where(lane == 1, dbt, 0.0))

        state = [jnp.zeros((D, D), F32)]

        def recur_group(g):
            for j, n in enumerate(reversed(range(g * U, (g + 1) * U))):
                state[0] = recur(n, state[0])
                if j % GDN_RECUR_STEPS_PER_STAGE == GDN_RECUR_STEPS_PER_STAGE - 1:
                    yield

        def stage(fn, g):
            return _together([fn(g * U + j) for j in range(U)])

        for step in range(NG + 2):
            jobs = [(stage, prepare, NG - 1 - step), (None, None, NG - step), (stage, local, NG + 1 - step)]
            _lockstep([recur_group(g) if make is None else make(fn, g) for make, fn, g in jobs if 0 <= g < NG])

    spec = pl.BlockSpec((1, S, D), lambda h, b: (h, b, 0))
    return _call_beside(
        body, transfer, grid=(H, B), name="gdn_bwd",
        in_specs=[spec, spec, spec, pl.BlockSpec((S, LANES), lambda h, b: (b, 0)),
                  pl.BlockSpec((1, NC, D, D), lambda h, b: (h, b, 0, 0)),
                  pl.BlockSpec((1, NC, C, C), lambda h, b: (h, b, 0, 0)), spec, spec, spec],
        out_specs=[spec, spec, spec, spec],
        out_shape=[SDS((H, B * S, D), F32)] * 4,
        scratch_shapes=[pltpu.VMEM((S, D), F32), pltpu.VMEM((S, D), F32), pltpu.VMEM((NC, D, D), F32),
                        pltpu.VMEM((NC, SUBLANES, LANES), F32), pltpu.VMEM((S, D), F32),
                        pltpu.VMEM((NC, D, D), F32), pltpu.VMEM((NC, D, D), F32)],
        semantics=("arbitrary", "arbitrary"), args=(qg, kg, vg, gates, states, ainv, u4, w4, do4))


def _gdn_pre_bwd(proj, conv_w, alog_l, dt_l, dq4, dk4, dv4, dgb4, S):
    T = proj.shape[0]
    tm = min(256, T)
    tiles_per_seq = S // tm
    C3 = 3 * GDN_WIDTH
    H = GDN_HEADS

    def body(u_ref, halo_ref, gab_ref, w_ref, alog_ref, dt_ref, dq_ref, dk_ref, dv_ref, dgb_ref,
             dc_ref, dgab_ref, dcw_ref, dalog_ref, ddt_ref):
        i = pl.program_id(0)

        @pl.when(i == 0)
        def _():
            dcw_ref[...] = jnp.zeros_like(dcw_ref)
            dalog_ref[...] = jnp.zeros_like(dalog_ref)
            ddt_ref[...] = jnp.zeros_like(ddt_ref)

        halo = jnp.where(i % tiles_per_seq == 0, 0.0, halo_ref[...])
        c, sh = _conv_taps(u_ref[...], halo, w_ref[...])
        sg = _sigmoid(c)
        a = c * sg
        das = [None] * (3 * H)
        for h in range(H):
            xq = a[:, h * GDN_DIM:(h + 1) * GDN_DIM]
            xk = a[:, GDN_WIDTH + h * GDN_DIM:GDN_WIDTH + (h + 1) * GDN_DIM]
            das[h] = _l2n_bwd(dq_ref[h], xq, GDN_QSCALE)
            das[H + h] = _l2n_bwd(dk_ref[h], xk, 1.0)
            das[2 * H + h] = dv_ref[h]
        dc = jnp.concatenate(das, axis=-1) * (sg * (1.0 + c * (1.0 - sg)))
        dc_ref[...] = dc
        dcw_ref[...] += jnp.concatenate(
            [jnp.sum(dc * sh[CONV_W - 1 - t], axis=0, keepdims=True) for t in range(CONV_W)], axis=0)
        lane = lax.broadcasted_iota(jnp.int32, (tm, LANES), 1)
        ric = lax.broadcasted_iota(jnp.int32, (tm, LANES), 0) % CHUNK
        dG = jnp.zeros((tm, LANES), F32)
        for h in range(H):
            t = dgb_ref[h]
            dG = dG + jnp.where(lane == h, _pick_lane(t, lane, 0), 0.0) \
                    + jnp.where(lane == h + H, _pick_lane(t, lane, 1), 0.0)
        is_g = lane < H
        dg = jnp.where(is_g, _chunk_rev_cumsum(jnp.where(is_g, dG, 0.0), ric), 0.0)
        gab = gab_ref[...]
        g, beta = _gate_values(gab, alog_ref[...], dt_ref[...], lane)
        dga = jnp.where(is_g, dg * (-jnp.exp(alog_ref[...])) * _sigmoid(gab + dt_ref[...]), 0.0)
        dgb = jnp.where(is_g, 0.0, dG) * beta * (1.0 - beta)
        dgab_ref[...] = (dga + dgb).astype(MXU_DTYPE)
        dalog_ref[...] += jnp.sum(dg * g, axis=0, keepdims=True)
        ddt_ref[...] += jnp.sum(dga, axis=0, keepdims=True)

    hspec = pl.BlockSpec((H, tm, GDN_DIM), lambda i: (0, i, 0))
    vec = pl.BlockSpec((1, LANES), lambda i: (0, 0))
    return pl.pallas_call(
        body, grid=(T // tm,), name="gdn_pre_bwd",
        in_specs=[pl.BlockSpec((tm, C3), lambda i: (i, 0)),
                  pl.BlockSpec((SUBLANES, C3), lambda i: (jnp.maximum(i * (tm // SUBLANES) - 1, 0), 0)),
                  pl.BlockSpec((tm, LANES), lambda i: (i, P_GAB // LANES)),
                  pl.BlockSpec((CONV_W, C3), lambda i: (0, 0)), vec, vec, hspec, hspec, hspec, hspec],
        out_specs=[pl.BlockSpec((tm, C3), lambda i: (i, 0)), pl.BlockSpec((tm, LANES), lambda i: (i, 0)),
                   pl.BlockSpec((CONV_W, C3), lambda i: (0, 0)), vec, vec],
        out_shape=[SDS((T, C3), F32), SDS((T, LANES), MXU_DTYPE), SDS((CONV_W, C3), F32),
                   SDS((1, LANES), F32), SDS((1, LANES), F32)],
        compiler_params=_params(("arbitrary",)),
    )(proj, proj, proj, conv_w, alog_l, dt_l, dq4, dk4, dv4, dgb4)


def _mla_pre_bwd(proj, cosf, sinf, w_qln, w_kvln, w_uq_p, w_ukv, qnw, knw, dq4, dk4, dv4, transfer=None):
    T = proj.shape[0]
    tm = min(256, T)
    H = MLA_HEADS

    def body(ql_ref, kvl_ref, kpe_ref, cos_ref, sin_ref, wq_ref, wkv_ref, uq_ref, ukv_ref, qnw_ref, knw_ref,
             dq_ref, dk_ref, dv_ref,
             dql_ref, dkvl_ref, dkpe_ref, dqraw_ref, dkvraw_ref, qn_ref, kvn_ref, dwq_ref, dwkv_ref, dqnw_ref, dknw_ref):
        @pl.when(pl.program_id(0) == 0)
        def _():
            for r in (dwq_ref, dwkv_ref, dqnw_ref, dknw_ref):
                r[...] = jnp.zeros_like(r)

        cos, sin = cos_ref[...], sin_ref[...]
        qnw_, knw_ = qnw_ref[...], knw_ref[...]
        ql, kvl = ql_ref[...], kvl_ref[...]
        kpe_raw = kpe_ref[...][:, :ROPE]
        rms = functools.partial(_rms, on_mxu=True)
        rms_bwd = functools.partial(_rms_bwd, on_mxu=True)
        qn, rq = rms(ql, wq_ref[...])
        kvn, rkv = rms(kvl, wkv_ref[...])
        qn_ref[...] = qn.astype(MXU_DTYPE)
        kvn_ref[...] = kvn.astype(MXU_DTYPE)
        qraw = _mm(qn, uq_ref[...])
        kvraw = _mm(kvn, ukv_ref[...])
        dq_nope, dq_pe, dkv_parts = [], [], []
        dqnw_n = jnp.zeros((1, NOPE), F32)
        dqnw_p = jnp.zeros((1, ROPE), F32)
        dknw_n = jnp.zeros((1, NOPE), F32)
        dkpe = jnp.zeros((tm, ROPE), F32)
        for h in range(H):
            dq = dq_ref[h] * ATT_SCALE
            x = qraw[:, h * NOPE:(h + 1) * NOPE]
            dx, dw = rms_bwd(dq[:, :NOPE], x, qnw_[:, :NOPE], rms(x, qnw_[:, :NOPE])[1])
            dq_nope.append(dx)
            dqnw_n = dqnw_n + dw
            x = qraw[:, H * NOPE + h * ROPE:H * NOPE + (h + 1) * ROPE]
            dx, dw = rms_bwd(_rope_bwd(dq[:, NOPE:], cos, sin), x, qnw_[:, NOPE:], rms(x, qnw_[:, NOPE:])[1])
            dq_pe.append(dx)
            dqnw_p = dqnw_p + dw
            dk = dk_ref[h]
            x = kvraw[:, h * 256:h * 256 + NOPE]
            dx, dw = rms_bwd(dk[:, :NOPE], x, knw_[:, :NOPE], rms(x, knw_[:, :NOPE])[1])
            dknw_n = dknw_n + dw
            dkpe = dkpe + dk[:, NOPE:]
            dkv_parts += [dx, dv_ref[h]]
        dx, dknw_p = rms_bwd(_rope_bwd(dkpe, cos, sin), kpe_raw, knw_[:, NOPE:], rms(kpe_raw, knw_[:, NOPE:])[1])
        dkpe_ref[...] = jnp.concatenate([dx, jnp.zeros((tm, LANES - ROPE), F32)], axis=-1).astype(MXU_DTYPE)
        dqraw = jnp.concatenate(dq_nope + dq_pe, axis=-1).astype(MXU_DTYPE)
        dkvraw = jnp.concatenate(dkv_parts, axis=-1).astype(MXU_DTYPE)
        dqraw_ref[...] = dqraw
        dkvraw_ref[...] = dkvraw
        dx, dw = rms_bwd(_mm_nt(dqraw, uq_ref[...]), ql, wq_ref[...], rq)
        dql_ref[...] = dx.astype(MXU_DTYPE)
        dwq_ref[...] += dw
        dx, dw = rms_bwd(_mm_nt(dkvraw, ukv_ref[...]), kvl, wkv_ref[...], rkv)
        dkvl_ref[...] = dx.astype(MXU_DTYPE)
        dwkv_ref[...] += dw
        dqnw_ref[...] += jnp.concatenate([dqnw_n, dqnw_p], axis=-1)
        dknw_ref[...] += jnp.concatenate([dknw_n, dknw_p], axis=-1)

    full = lambda a: pl.BlockSpec(a.shape, lambda i: (0,) * a.ndim)
    rows = lambda n: pl.BlockSpec((tm, n), lambda i: (i, 0))
    const = lambda n: pl.BlockSpec((1, n), lambda i: (0, 0))
    NQ, NKV = w_uq_p.shape[1], w_ukv.shape[1]
    return _call_beside(
        body, transfer, grid=(T // tm,), name="mla_pre_bwd", scratch_shapes=[], semantics=("arbitrary",),
        args=(proj, proj, proj, cosf, sinf, w_qln, w_kvln, w_uq_p, w_ukv, qnw, knw, dq4, dk4, dv4),
        in_specs=[pl.BlockSpec((tm, 256), lambda i: (i, P_QLAT // 256)),
                  pl.BlockSpec((tm, 256), lambda i: (i, P_KVLAT // 256)),
                  pl.BlockSpec((tm, 128), lambda i: (i, P_KPE // 128)),
                  rows(ROPE), rows(ROPE),
                  full(w_qln), full(w_kvln), full(w_uq_p), full(w_ukv), full(qnw), full(knw),
                  pl.BlockSpec((H, tm, QK_DIM), lambda i: (0, i, 0)),
                  pl.BlockSpec((H, tm, QK_DIM), lambda i: (0, i, 0)),
                  pl.BlockSpec((H, tm, V_DIM), lambda i: (0, i, 0))],
        out_specs=[rows(Q_LORA), rows(KV_LORA), rows(LANES), rows(NQ), rows(NKV), rows(Q_LORA), rows(KV_LORA),
                   const(Q_LORA), const(KV_LORA), const(QK_DIM), const(QK_DIM)],
        out_shape=[SDS((T, Q_LORA), MXU_DTYPE), SDS((T, KV_LORA), MXU_DTYPE), SDS((T, LANES), MXU_DTYPE),
                   SDS((T, NQ), MXU_DTYPE), SDS((T, NKV), MXU_DTYPE),
                   SDS((T, Q_LORA), MXU_DTYPE), SDS((T, KV_LORA), MXU_DTYPE),
                   SDS((1, Q_LORA), F32), SDS((1, KV_LORA), F32), SDS((1, QK_DIM), F32), SDS((1, QK_DIM), F32)])


def _in_proj_bwd(dc, conv_w, dgz, dql, dkvl, dkpe, dgab, w_in_p, dh, x2, w_an, S):
    T, D = x2.shape
    N = w_in_p.shape[1]
    C3 = dc.shape[1]
    tm = min(512, S)
    assert S % tm == 0 and T % tm == 0, "a token tile must not straddle two sequences"
    tiles_per_seq = S // tm
    nblk = T // SUBLANES

    def body(dc_ref, nxt_ref, cw_ref, b_ref, c_ref, d_ref, e_ref, f_ref, w_ref, dh_ref, x_ref, wn_ref,
             dx_ref, dp_ref, dwn_ref):
        i = pl.program_id(0)

        @pl.when(i == 0)
        def _():
            dwn_ref[...] = jnp.zeros_like(dwn_ref)

        nxt = jnp.where(i % tiles_per_seq == tiles_per_seq - 1, 0.0, nxt_ref[...])
        dcv, cw = dc_ref[...], cw_ref[...]
        du = cw[3:4] * dcv
        for j in range(1, CONV_W):
            du = du + cw[3 - j:4 - j] * _shift_up(dcv, nxt, j)
        dp = jnp.concatenate([du.astype(MXU_DTYPE), b_ref[...], c_ref[...], d_ref[...], e_ref[...], f_ref[...]],
                             axis=-1).astype(MXU_DTYPE)
        dp_ref[...] = dp
        x = x_ref[...]
        _, r = _rms(x, wn_ref[...])
        dx, dw = _rms_bwd(_mm_nt(dp, w_ref[...]), x, wn_ref[...], r)
        dx_ref[...] = dh_ref[...] + dx
        dwn_ref[...] += dw

    rows = lambda n: pl.BlockSpec((tm, n), lambda i: (i, 0))
    return pl.pallas_call(
        body, grid=(T // tm,), name="in_proj_bwd",
        in_specs=[rows(C3),
                  pl.BlockSpec((SUBLANES, C3), lambda i: (jnp.minimum((i + 1) * (tm // SUBLANES), nblk - 1), 0)),
                  pl.BlockSpec((CONV_W, C3), lambda i: (0, 0)),
                  rows(dgz.shape[1]), rows(dql.shape[1]), rows(dkvl.shape[1]),
                  rows(dkpe.shape[1]), rows(dgab.shape[1]),
                  pl.BlockSpec((D, N), lambda i: (0, 0)), rows(D), rows(D), pl.BlockSpec((1, D), lambda i: (0, 0))],
        out_specs=[rows(D), rows(N), pl.BlockSpec((1, D), lambda i: (0, 0))],
        out_shape=[SDS((T, D), F32), SDS((T, N), MXU_DTYPE), SDS((1, D), F32)],
        compiler_params=_params(("arbitrary",)),
    )(dc, dc, conv_w, dgz, dql, dkvl, dkpe, dgab, w_in_p, dh, x2, w_an)


def _wgrad(a, b, name, column_shards=False):
    T, M = a.shape
    N = b.shape[1]
    tM = _divisor_tile(M, 1024)
    tN = N // N_DEV if column_shards else _divisor_tile(N, 1536)
    tk = min(T, 2048)
    nk = T // tk

    def body(a_ref, b_ref, o_ref, acc):
        k = pl.program_id(2)

        @pl.when(k == 0)
        def _():
            acc[...] = jnp.zeros_like(acc)

        acc[...] += _mm_tn(a_ref[...], b_ref[...])

        @pl.when(k == nk - 1)
        def _():
            o_ref[...] = acc[...].astype(WIRE_DTYPE).reshape(o_ref.shape)

    if column_shards:
        out_spec, out_shape = pl.BlockSpec((1, tM, tN), lambda i, j, k: (j, i, 0)), SDS((N_DEV, M, tN), WIRE_DTYPE)
    else:
        out_spec, out_shape = pl.BlockSpec((tM, tN), lambda i, j, k: (i, j)), SDS((M, N), WIRE_DTYPE)
    return pl.pallas_call(
        body, grid=(M // tM, N // tN, nk), name=name,
        in_specs=[pl.BlockSpec((tk, tM), lambda i, j, k: (k, i)), pl.BlockSpec((tk, tN), lambda i, j, k: (k, j))],
        out_specs=out_spec, out_shape=out_shape,
        scratch_shapes=[pltpu.VMEM((tM, tN), F32)],
        compiler_params=_params(("arbitrary", "arbitrary", "arbitrary")),
    )(a, b)


def _adamw(g, w, m, v):
    m = ADAM_B1 * m + (1.0 - ADAM_B1) * g
    v = ADAM_B2 * v + (1.0 - ADAM_B2) * jnp.square(g)
    m_hat = m / (1.0 - ADAM_B1 ** ADAM_STEP)
    v_hat = v / (1.0 - ADAM_B2 ** ADAM_STEP)
    return -ADAM_LR * (m_hat / (jnp.sqrt(v_hat) + ADAM_EPS) + ADAM_WD * w), m, v


def _reduce_adamw(parts, w, m, v, name):
    R, C = w.shape
    _, Rp, Cp = parts.shape
    tr = min(R, 256)
    tp = tr if Rp == R else Rp

    def body(p_ref, w_ref, m_ref, v_ref, g_ref, d_ref, nm_ref, nv_ref):
        g = p_ref[0].astype(F32)
        for s in range(1, N_DEV):
            g = g + p_ref[s].astype(F32)
        g = g[:tr, :C]
        g_ref[...] = g
        d_ref[...], nm_ref[...], nv_ref[...] = _adamw(g, w_ref[...], m_ref[...], v_ref[...])

    spec = pl.BlockSpec((tr, C), lambda i: (i, 0))
    return pl.pallas_call(
        body, grid=(R // tr,), name=name,
        in_specs=[pl.BlockSpec((N_DEV, tp, Cp), lambda i: (0, i, 0)), spec, spec, spec],
        out_specs=[spec] * 4, out_shape=[SDS((R, C), F32)] * 4,
        compiler_params=_params(("arbitrary",)),
    )(parts, w, m, v)


SMALL_ROWS, SMALL_COLS = 16, 1024
SMALL_LAYOUT = (
    ("attn_norm_w", 0, 1, 1024, 1024), ("mlp_norm_w", 1, 1, 1024, 1024), ("q_lat_norm_w", 2, 1, 256, 256),
    ("kv_lat_norm_w", 3, 1, 256, 256), ("q_norm_w", 4, 1, 192, 192), ("k_norm_w", 5, 1, 192, 192),
    ("mla_out_norm_w", 6, 4, 128, 128), ("a_log", 10, 1, 128, 4), ("dt_bias", 11, 1, 128, 4),
    ("gdn_norm_w", 12, 1, 128, 128))
LOSS_ENTRY = ("loss", 13, 1, 128, 128)


def _adamw_replicated(parts, ws, ms, vs):
    n = len(SMALL_LAYOUT)

    def body(*refs):
        p_ref = refs[0]
        w_refs, m_refs, v_refs = refs[1:1 + n], refs[1 + n:1 + 2 * n], refs[1 + 2 * n:1 + 3 * n]
        outs = refs[1 + 3 * n:]
        s = p_ref[0]
        for d in range(1, N_DEV):
            s = s + p_ref[d]
        for i, (_, r0, nr, _, pw) in enumerate(SMALL_LAYOUT):
            g = s[r0:r0 + nr, :pw]
            outs[i][...] = g
            outs[n + i][...], outs[2 * n + i][...], outs[3 * n + i][...] = _adamw(
                g, w_refs[i][...], m_refs[i][...], v_refs[i][...])
        _, r0, nr, gw, _ = LOSS_ENTRY
        outs[4 * n][...] = s[r0:r0 + nr, :gw]

    res = pl.pallas_call(
        body, name="adamw_replicated",
        out_shape=[SDS(w.shape, F32) for w in ws] * 4 + [SDS((1, LANES), F32)],
        compiler_params=_params(),
    )(parts, *ws, *ms, *vs)
    return [res[k * n:(k + 1) * n] for k in range(4)], res[4 * n][0, 0]


COPIES_PER_ARRAY = N_DEV - 1


def _two_level_gather(srcs, outs, send_sems, recv_sems, local_sems=None, stage="all"):
    mx, my, mc = lax.axis_index("x"), lax.axis_index("y"), lax.axis_index("c")
    me, sibling = (mx, my, mc), (mx, my, 1 - mc)
    chips = [(1 - mx, my), (mx, 1 - my), (1 - mx, 1 - my)]
    arrays = range(len(srcs))

    def copy(a, k, block, to, src=None):
        px, py, pc = block
        slot = outs[a].at[4 * px + 2 * py + pc]
        sem = a * COPIES_PER_ARRAY + k
        return pltpu.make_async_remote_copy(
            src_ref=slot if src is None else src, dst_ref=slot,
            send_sem=send_sems.at[sem], recv_sem=recv_sems.at[sem], device_id=to, device_id_type=MESH_ID)

    mine = [] if local_sems is None else [
        pltpu.make_async_copy(srcs[a], outs[a].at[4 * mx + 2 * my + mc], local_sems.at[a]) for a in arrays]
    first = []
    for a in arrays:
        first.append(copy(a, 0, me, sibling, src=srcs[a]))
        first += [copy(a, 1 + j, me, (*chip, mc), src=srcs[a]) for j, chip in enumerate(chips)]
    forwards = [copy(a, 4 + j, (*chip, mc), sibling) for j, chip in enumerate(chips) for a in arrays]
    if stage in ("all", "start"):
        for cp in mine + first:
            cp.start()
    if stage in ("all", "forward"):
        for j, chip in enumerate(chips):
            for a in arrays:
                copy(a, 1 + j, (*chip, mc), me).wait_recv()
                forwards[j * len(srcs) + a].start()
    if stage in ("all", "finish"):
        for a in arrays:
            copy(a, 0, sibling, me).wait_recv()
        for j, chip in enumerate(chips):
            for a in arrays:
                copy(a, 4 + j, (*chip, 1 - mc), me).wait_recv()
        for cp in first + forwards:
            cp.wait_send()
        for cp in mine:
            cp.wait()


def _comm_scratch(n):
    return [pltpu.SemaphoreType.DMA((n * COPIES_PER_ARRAY,)), pltpu.SemaphoreType.DMA((n * COPIES_PER_ARRAY,)),
            pltpu.SemaphoreType.DMA((n,))]


def _any_specs(n):
    return [pl.BlockSpec(memory_space=pl.ANY)] * n


def _gather_weights(shards):
    n = len(shards)

    def body(*refs):
        _two_level_gather(refs[:n], refs[n:2 * n], *refs[2 * n:])

    return pl.pallas_call(
        body, name="gather_weights",
        out_shape=[SDS((N_DEV,) + s.shape, s.dtype) for s in shards],
        in_specs=_any_specs(n), out_specs=_any_specs(n), scratch_shapes=_comm_scratch(n),
    )(*shards)


def _gather_small_grads(gs, loss_lanes):
    gs = list(gs) + [loss_lanes]
    n = len(gs)

    def body(*refs):
        g_refs, out_ref = refs[:n], refs[n]
        tile, send_sems, recv_sems = refs[n + 1:]
        tile[...] = jnp.zeros_like(tile)
        for (_, r0, nr, gw, _), g in zip(SMALL_LAYOUT + (LOSS_ENTRY,), g_refs):
            tile[r0:r0 + nr, 0:gw] = g[...]
        me = 4 * lax.axis_index("x") + 2 * lax.axis_index("y") + lax.axis_index("c")
        out_ref[me] = tile[...]
        _two_level_gather([tile], [out_ref], send_sems, recv_sems)

    return pl.pallas_call(
        body, name="gather_small_grads",
        out_shape=SDS((N_DEV, SMALL_ROWS, SMALL_COLS), F32),
        in_specs=[pl.BlockSpec(memory_space=pltpu.VMEM)] * n,
        out_specs=pl.BlockSpec(memory_space=pltpu.VMEM),
        scratch_shapes=[pltpu.VMEM((SMALL_ROWS, SMALL_COLS), F32),
                        pltpu.SemaphoreType.DMA((COPIES_PER_ARRAY,)), pltpu.SemaphoreType.DMA((COPIES_PER_ARRAY,))],
    )(*gs)


def _exchange_grads(slabs):
    n = len(slabs)

    def body(*refs):
        _exchange(refs[:n], refs[n:2 * n], *refs[2 * n:])

    return pl.pallas_call(
        body, name="exchange_grads",
        out_shape=[SDS(s.shape, s.dtype) for s in slabs],
        in_specs=_any_specs(n), out_specs=_any_specs(n), scratch_shapes=_comm_scratch(n),
    )(*slabs)


class _Transfer:
    def __init__(self, kind, arrays):
        self.kind, self.arrays, self.n = kind, list(arrays), len(arrays)

    def out_shapes(self):
        if self.kind == "gather":
            return [SDS((N_DEV,) + a.shape, a.dtype) for a in self.arrays]
        return [SDS(a.shape, a.dtype) for a in self.arrays]

    def run(self, srcs, outs, sems, stage):
        fn = _two_level_gather if self.kind == "gather" else _exchange
        fn(srcs, outs, *sems, stage=stage)


def _call_beside(body, transfer, *, grid, in_specs, out_specs, out_shape, scratch_shapes, name, semantics, args):
    if transfer is None:
        res = pl.pallas_call(body, grid=grid, in_specs=in_specs, out_specs=out_specs, out_shape=out_shape,
                             scratch_shapes=scratch_shapes, name=name, compiler_params=_params(semantics))(*args)
        return list(res), []
    n_in, n_out, n_s, n = len(in_specs), len(out_specs), len(scratch_shapes), transfer.n
    total = functools.reduce(lambda a, b: a * b, grid, 1)

    def wrapped(*refs):
        ins, refs = refs[:n_in], refs[n_in:]
        t_in, refs = refs[:n], refs[n:]
        outs, refs = refs[:n_out], refs[n_out:]
        t_out, refs = refs[:n], refs[n:]
        scratch, sems = refs[:n_s], refs[n_s:]
        first = functools.reduce(jnp.logical_and, [pl.program_id(i) == 0 for i in range(len(grid))])
        last = functools.reduce(jnp.logical_and, [pl.program_id(i) == g - 1 for i, g in enumerate(grid)])

        @pl.when(first)
        def _():
            transfer.run(t_in, t_out, sems, "start")

        step = functools.reduce(lambda acc, ig: acc * ig[1] + pl.program_id(ig[0]), enumerate(grid), 0)

        @pl.when(step == (3 * total) // 4)
        def _():
            transfer.run(t_in, t_out, sems, "forward")

        body(*ins, *outs, *scratch)

        @pl.when(last)
        def _():
            transfer.run(t_in, t_out, sems, "finish")

    res = pl.pallas_call(
        wrapped, grid=grid, in_specs=list(in_specs) + _any_specs(n), out_specs=list(out_specs) + _any_specs(n),
        out_shape=list(out_shape) + transfer.out_shapes(), scratch_shapes=list(scratch_shapes) + _comm_scratch(n),
        name=name, compiler_params=_params(semantics))(*args, *transfer.arrays)
    return list(res[:n_out]), list(res[n_out:])


EXCHANGE_FLIPS = ((0, 0, 1), (1, 0, 0), (0, 1, 0), (1, 1, 0), (1, 0, 1), (0, 1, 1), (1, 1, 1))


def _exchange(srcs, outs, send_sems, recv_sems, local_sems, stage="all"):
    mx, my, mc = lax.axis_index("x"), lax.axis_index("y"), lax.axis_index("c")
    arrays = range(len(srcs))
    copies = [pltpu.make_async_copy(srcs[a].at[4 * mx + 2 * my + mc], outs[a].at[N_DEV - 1], local_sems.at[a])
              for a in arrays]
    for k, (fx, fy, fc) in enumerate(EXCHANGE_FLIPS):
        px = 1 - mx if fx else mx
        py = 1 - my if fy else my
        pc = 1 - mc if fc else mc
        for a in arrays:
            sem = a * COPIES_PER_ARRAY + k
            copies.append(pltpu.make_async_remote_copy(
                src_ref=srcs[a].at[4 * px + 2 * py + pc], dst_ref=outs[a].at[k],
                send_sem=send_sems.at[sem], recv_sem=recv_sems.at[sem],
                device_id=(px, py, pc), device_id_type=MESH_ID))
    if stage in ("all", "start"):
        for cp in copies:
            cp.start()
    if stage in ("all", "finish"):
        for cp in copies:
            cp.wait()


def _w_in_to_padded(w):
    z = lambda n: jnp.zeros((w.shape[0], n), w.dtype)
    return jnp.concatenate([w[:, O_GQKV:O_GZ], w[:, O_GZ:O_GAB], w[:, O_QLAT:O_KVLAT], w[:, O_KVLAT:O_KPE],
                            w[:, O_KPE:O_GQKV], z(P_GAB - P_KPE - ROPE), w[:, O_GAB:O_END],
                            z(P_WIDTH - P_GAB - (O_END - O_GAB))], axis=1)


def _w_in_from_padded(wp):
    return jnp.concatenate([wp[:, P_QLAT:P_QLAT + 256], wp[:, P_KVLAT:P_KVLAT + 256], wp[:, P_KPE:P_KPE + ROPE],
                            wp[:, P_GQKV:P_GZ], wp[:, P_GZ:P_QLAT], wp[:, P_GAB:P_GAB + (O_END - O_GAB)]], axis=1)


W_IN_SHARD_COLS = (O_END - O_QLAT) // N_DEV


def _w_in_shards_to_padded(stack):
    _, R, Cw = stack.shape
    tr = min(R, 256)

    def body(s_ref, o_ref):
        full = jnp.concatenate([s_ref[d].astype(F32)[:, :W_IN_SHARD_COLS] for d in range(N_DEV)], axis=-1)
        o_ref[...] = _w_in_to_padded(full).astype(o_ref.dtype)

    return pl.pallas_call(
        body, grid=(R // tr,), name="w_in_to_padded",
        in_specs=[pl.BlockSpec((N_DEV, tr, Cw), lambda i: (0, i, 0))],
        out_specs=pl.BlockSpec((tr, P_WIDTH), lambda i: (i, 0)),
        out_shape=SDS((R, P_WIDTH), stack.dtype), compiler_params=_params(("arbitrary",)),
    )(stack)


def _w_in_padded_to_slabs(gp, wire_cols):
    R = gp.shape[0]
    tr = min(R, 256)

    def body(g_ref, o_ref):
        orig = _w_in_from_padded(g_ref[...].astype(F32))
        for d in range(N_DEV):
            piece = orig[:, d * W_IN_SHARD_COLS:(d + 1) * W_IN_SHARD_COLS]
            o_ref[d] = _pad2(piece, tr, wire_cols).astype(o_ref.dtype)

    return pl.pallas_call(
        body, grid=(R // tr,), name="w_in_to_slabs",
        in_specs=[pl.BlockSpec((tr, P_WIDTH), lambda i: (i, 0))],
        out_specs=pl.BlockSpec((N_DEV, tr, wire_cols), lambda i: (0, i, 0)),
        out_shape=SDS((N_DEV, R, wire_cols), gp.dtype), compiler_params=_params(("arbitrary",)),
    )(gp)


def _w_uq_to_headsplit(w):
    w3 = w.reshape(w.shape[0], MLA_HEADS, QK_DIM)
    return jnp.concatenate([w3[:, :, :NOPE].reshape(w.shape[0], -1), w3[:, :, NOPE:].reshape(w.shape[0], -1)], axis=1)


def _w_uq_from_headsplit(wp):
    n = wp[:, :MLA_HEADS * NOPE].reshape(wp.shape[0], MLA_HEADS, NOPE)
    p = wp[:, MLA_HEADS * NOPE:].reshape(wp.shape[0], MLA_HEADS, ROPE)
    return jnp.concatenate([n, p], axis=2).reshape(wp.shape[0], -1)


def _lane_vec(v4):
    return jnp.pad(v4.reshape(1, -1), ((0, 0), (0, LANES - v4.shape[-1])))


def _local_step(x, positions, target, attn_norm_w, w_in, q_lat_norm_w, w_uq, kv_lat_norm_w, w_ukv, q_norm_w,
                k_norm_w, mla_out_norm_w, conv_w, a_log, dt_bias, gdn_norm_w, w_out, mlp_norm_w, w_up, w_down,
                late_shards=None, exchange=False):
    B, S, D = x.shape
    T = B * S
    x2 = x.reshape(T, D)
    t2 = target.reshape(T, D)
    half = ROPE // 2
    inv_freq = ROPE_THETA ** (-jnp.arange(half, dtype=F32) / half)
    ang = positions.reshape(T, 1).astype(F32) * inv_freq
    cosf = jnp.concatenate([jnp.cos(ang)] * 2, axis=-1)
    sinf = jnp.concatenate([jnp.sin(ang)] * 2, axis=-1)
    w_in_p = w_in
    w_uq_p = _w_uq_to_headsplit(w_uq)
    alog_l, dt_l = _lane_vec(a_log), _lane_vec(dt_bias)
    w_an, w_qln, w_kvln, qnw, knw, w_mn, gdn_w = (
        attn_norm_w, q_lat_norm_w, kv_lat_norm_w, q_norm_w, k_norm_w, mlp_norm_w, gdn_norm_w)

    proj, xn = _in_proj(x2, w_an, w_in_p)
    def gathering(shards):
        return None if late_shards is None else _Transfer("gather", shards)

    (q4, k4, v4), late = _mla_pre(proj, cosf, sinf, w_qln, w_kvln, w_uq_p, w_ukv, qnw, knw,
                                  gathering(late_shards and late_shards[:1]))
    if late:
        w_out = late[0].reshape(-1, D)
    (o_mla, lse), late = _attn_fwd(q4, k4, v4, B, S, gathering(late_shards and late_shards[2:]))
    if late:
        w_down = late[0].reshape(-1, D)
    qg, kg, vg, gates = _gdn_pre(proj, conv_w, alog_l, dt_l, S)
    (o_gdn, states, ainv, u4, w4), late = _gdn_fwd(qg, kg, vg, gates, B, S,
                                                   gathering(late_shards and late_shards[1:2]))
    if late:
        w_up = late[0]
    h2, mix = _mix_out(o_mla, o_gdn, proj, x2, mla_out_norm_w, gdn_w, w_out)
    up, hn, dy, sq = _mlp_fwd(h2, w_mn, w_up, w_down, t2)
    loss = (0.5 / D) * jnp.sum(sq[:, 0, 0])

    dh, dhb, dup, act, dyb, d_mlp_norm = _mlp_bwd(dy, up, h2, w_mn, w_up, w_down)
    g_w_down = _wgrad(act, dyb, "wgrad_down")
    g_w_up = _wgrad(hn, dup, "wgrad_up", column_shards=True)
    do_mla, do_gdn, dz, d_mla_w, d_gdn_w, delta = _mix_bwd(dhb, o_mla, o_gdn, proj, mla_out_norm_w, gdn_w, w_out)
    g_w_out = _wgrad(mix, dhb, "wgrad_out")
    first = ("w_down",)
    second = ("w_out",)
    third = ("w_up", "w_uq", "w_ukv")
    mats = dict(w_up=g_w_up, w_down=g_w_down, w_out=g_w_out)

    def sending(names):
        return _Transfer("exchange", [_slabs(n, mats[n]) for n in names]) if exchange else None

    (dq4, dk4, dv4), got = _attn_bwd(q4, k4, v4, do_mla, delta, lse, B, S, sending(first))
    mats.update(zip(first, got))
    (dql, dkvl, dkpe, dqraw, dkvraw, qn, kvn, d_wqln, d_wkvln, d_qnw, d_knw), got = _mla_pre_bwd(
        proj, cosf, sinf, w_qln, w_kvln, w_uq_p, w_ukv, qnw, knw, dq4, dk4, dv4, sending(second))
    mats.update(zip(second, got))
    mats.update(w_uq=_wgrad(qn, dqraw, "wgrad_uq"), w_ukv=_wgrad(kvn, dkvraw, "wgrad_ukv"))
    (dqg, dkg, dvg, dgb4), got = _gdn_bwd(qg, kg, vg, gates, states, ainv, u4, w4, do_gdn, B, S, sending(third))
    mats.update(zip(third, got))
    dc, dgab, g_conv, d_alog, d_dt = _gdn_pre_bwd(proj, conv_w, alog_l, dt_l, dqg, dkg, dvg, dgb4, S)
    grad_x2, dproj, d_attn_norm = _in_proj_bwd(dc, conv_w, dz, dql, dkvl, dkpe, dgab, w_in_p, dh, x2, w_an, S)
    mats.update(w_in=_wgrad(xn, dproj, "wgrad_in"), conv_w=g_conv)
    if exchange:
        last = ("w_in", "conv_w")
        mats.update(zip(last, _exchange_grads([_slabs(n, mats[n]) for n in last])))
    small = dict(attn_norm_w=d_attn_norm, mlp_norm_w=d_mlp_norm, q_lat_norm_w=d_wqln, kv_lat_norm_w=d_wkvln,
                 q_norm_w=d_qnw, k_norm_w=d_knw, mla_out_norm_w=d_mla_w, a_log=d_alog, dt_bias=d_dt,
                 gdn_norm_w=d_gdn_w)
    return loss, grad_x2.reshape(B, S, D), mats, [small[n] for n, *_ in SMALL_LAYOUT]


BIG = ("w_in", "w_uq", "w_ukv", "conv_w", "w_out", "w_up", "w_down")
ALL_W = ("attn_norm_w", "w_in", "q_lat_norm_w", "w_uq", "kv_lat_norm_w", "w_ukv", "q_norm_w", "k_norm_w",
         "mla_out_norm_w", "conv_w", "a_log", "dt_bias", "gdn_norm_w", "w_out", "mlp_norm_w", "w_up", "w_down")
WIRE_SHAPE = {"w_in": (1024, 384), "w_uq": (256, 128), "conv_w": (16, 256)}


def _pad2(a, rows, cols):
    return jnp.pad(a, [(0, 0)] * (a.ndim - 2) + [(0, rows - a.shape[-2]), (0, cols - a.shape[-1])])


def _cols_to_full(stack, cols):
    return jnp.moveaxis(stack[:, :, :cols], 0, 1).reshape(stack.shape[1], N_DEV * cols)


def _full_to_cols(full, wire_cols):
    r, n = full.shape
    return _pad2(jnp.moveaxis(full.reshape(r, N_DEV, n // N_DEV), 1, 0), r, wire_cols)


def _slabs(name, g):
    if name == "w_in":
        return _w_in_padded_to_slabs(g, WIRE_SHAPE["w_in"][1])
    if name == "w_uq":
        return _full_to_cols(_w_uq_from_headsplit(g), WIRE_SHAPE["w_uq"][1])
    if name == "w_ukv":
        return _full_to_cols(g, g.shape[1] // N_DEV)
    if name == "conv_w":
        return _pad2(_full_to_cols(g.astype(WIRE_DTYPE), g.shape[1] // N_DEV), *WIRE_SHAPE["conv_w"])
    if name == "w_up":
        return g
    return g.reshape(N_DEV, -1, g.shape[-1])


def kernel(x, positions, attn_norm_w, w_in, q_lat_norm_w, w_uq, kv_lat_norm_w, w_ukv, q_norm_w, k_norm_w, mla_out_norm_w, conv_w, a_log, dt_bias, gdn_norm_w, w_out, mlp_norm_w, w_up, w_down, loss_target, m_attn_norm_w, m_w_in, m_q_lat_norm_w, m_w_uq, m_kv_lat_norm_w, m_w_ukv, m_q_norm_w, m_k_norm_w, m_mla_out_norm_w, m_conv_w, m_a_log, m_dt_bias, m_gdn_norm_w, m_w_out, m_mlp_norm_w, m_w_up, m_w_down, v_attn_norm_w, v_w_in, v_q_lat_norm_w, v_w_uq, v_kv_lat_norm_w, v_w_ukv, v_q_norm_w, v_k_norm_w, v_mla_out_norm_w, v_conv_w, v_a_log, v_dt_bias, v_gdn_norm_w, v_w_out, v_mlp_norm_w, v_w_up, v_w_down):
    env = dict(locals())
    W = {n: env[n][0] for n in ALL_W}
    Mo = {n: env["m_" + n][0] for n in ALL_W}
    Vo = {n: env["v_" + n][0] for n in ALL_W}

    two_d = lambda a: a.reshape(1, -1) if a.ndim == 1 else a
    D = x.shape[-1]

    s_in, s_uq, s_ukv, s_conv = _gather_weights([
        _pad2(W["w_in"].astype(WIRE_DTYPE), *WIRE_SHAPE["w_in"]),
        _pad2(W["w_uq"].astype(WIRE_DTYPE), *WIRE_SHAPE["w_uq"]),
        W["w_ukv"].astype(WIRE_DTYPE), _pad2(W["conv_w"], *WIRE_SHAPE["conv_w"])])
    late = [W["w_out"].astype(WIRE_DTYPE), W["w_up"].astype(WIRE_DTYPE), W["w_down"].astype(WIRE_DTYPE)]

    loss, grad_x, parts, gs = _local_step(
        x, positions, loss_target, two_d(W["attn_norm_w"]), _w_in_shards_to_padded(s_in),
        two_d(W["q_lat_norm_w"]), _cols_to_full(s_uq, W["w_uq"].shape[1]), two_d(W["kv_lat_norm_w"]),
        _cols_to_full(s_ukv, W["w_ukv"].shape[1]), two_d(W["q_norm_w"]), two_d(W["k_norm_w"]),
        W["mla_out_norm_w"], _cols_to_full(s_conv[:, :CONV_W], W["conv_w"].shape[1]), two_d(W["a_log"]),
        two_d(W["dt_bias"]), two_d(W["gdn_norm_w"]), None, two_d(W["mlp_norm_w"]), None, None,
        late_shards=late, exchange=True)
    done = {n: _reduce_adamw(parts[n], W[n], Mo[n], Vo[n], "adamw_" + n) for n in BIG}
    names = [n for n, *_ in SMALL_LAYOUT]
    tiles = _gather_small_grads(gs, jnp.full((1, LANES), loss, F32))
    small, loss = _adamw_replicated(tiles, [two_d(W[n]) for n in names], [two_d(Mo[n]) for n in names],
                                    [two_d(Vo[n]) for n in names])
    for i, n in enumerate(names):
        done[n] = [small[kind][i] for kind in range(4)]
    res = [done[n][kind].reshape(env[n].shape) for kind in range(4) for n in ALL_W]
    return (loss, grad_x, *res)
```

```python
import functools

import jax
import jax.numpy as jnp
from jax import lax
from jax.experimental import pallas as pl
from jax.experimental.pallas import tpu as pltpu

F32 = jnp.float32
MXU_DTYPE = jnp.bfloat16
WIRE_DTYPE = jnp.bfloat16
SDS = jax.ShapeDtypeStruct
HIGHEST = lax.Precision.HIGHEST
MESH_ID = pl.DeviceIdType.MESH

D_MODEL = 1024
MLA_HEADS = 4
Q_LORA = 256
KV_LORA = 256
NOPE = 128
ROPE = 64
QK_DIM = NOPE + ROPE
V_DIM = 128
ROPE_THETA = 10000.0
GDN_HEADS = 4
GDN_DIM = 128
GDN_WIDTH = GDN_HEADS * GDN_DIM
CONV_W = 4
CHUNK = 64
D_FF = 4 * D_MODEL
EPS = 1e-6
ATT_SCALE = QK_DIM ** -0.5
GDN_QSCALE = GDN_DIM ** -0.5
N_DEV = 8
ATTN_BLOCK = 512
ATTN_CHAINS = 2
MLP_FWD_SHARDS = 4
MLP_BWD_SHARDS = 4

ADAM_LR = 0.001
ADAM_B1 = 0.9
ADAM_B2 = 0.999
ADAM_EPS = 1e-08
ADAM_WD = 0.01
ADAM_STEP = 10

LANES = 128
SUBLANES = 8
VMEM_LIMIT = 60 * 1024 * 1024

P_GQKV, P_GZ, P_QLAT, P_KVLAT, P_KPE, P_GAB = 0, 1536, 2048, 2304, 2560, 2688
P_WIDTH = 2816
O_QLAT, O_KVLAT, O_KPE, O_GQKV, O_GZ, O_GAB, O_END = 0, 256, 512, 576, 2112, 2624, 2632


def _params(sem=None, vmem=VMEM_LIMIT):
    kw = dict(vmem_limit_bytes=vmem)
    if sem is not None:
        kw["dimension_semantics"] = sem
    return pltpu.CompilerParams(**kw)


def _mm(a, b):
    return jnp.dot(a.astype(MXU_DTYPE), b.astype(MXU_DTYPE), preferred_element_type=F32)


def _mm_nt(a, b):
    return lax.dot_general(a.astype(MXU_DTYPE), b.astype(MXU_DTYPE), (((1,), (1,)), ((), ())),
                           preferred_element_type=F32)


def _mm_tn(a, b):
    return lax.dot_general(a.astype(MXU_DTYPE), b.astype(MXU_DTYPE), (((0,), (0,)), ((), ())),
                           preferred_element_type=F32)


def _split(a):
    hi = a.astype(MXU_DTYPE)
    return hi, (a - hi.astype(F32)).astype(MXU_DTYPE)


def _mm_split(a, b):
    (ah, al), (bh, bl) = a, b
    dot = lambda x, y: jnp.dot(x, y, preferred_element_type=F32)
    if MXU_DTYPE == F32:
        return dot(ah, bh)
    return dot(ah, bh) + dot(ah, bl) + dot(al, bh)


def _mm_exact(a, b):
    return _mm_split(_split(a), _split(b))


def _row_sum(v, on_mxu=False):
    if not on_mxu:
        return jnp.sum(v, axis=-1, keepdims=True)
    d = v.shape[-1]
    ones = jnp.ones((d, LANES), MXU_DTYPE)
    s = sum(jnp.dot(p, ones, preferred_element_type=F32) for p in _split(v))
    return s[:, :d] if d <= LANES else jnp.tile(s, (1, d // LANES))


def _rms(x, w, on_mxu=False):
    r = lax.rsqrt(_row_sum(x * x, on_mxu) * (1.0 / x.shape[-1]) + EPS)
    return x * r * w, r


def _rms_bwd(dy, x, w, r, on_mxu=False):
    xh = x * r
    dyw = dy * w
    dx = r * (dyw - xh * (_row_sum(dyw * xh, on_mxu) * (1.0 / x.shape[-1])))
    dw = jnp.sum(dy * xh, axis=0, keepdims=True)
    return dx, dw


def _l2n(x, scale):
    return x * (lax.rsqrt(_row_sum(x * x) + EPS) * scale)


def _l2n_bwd(dy, x, scale):
    r = lax.rsqrt(_row_sum(x * x) + EPS)
    xh = x * r
    return (scale * r) * (dy - xh * _row_sum(dy * xh))


def _rot(t):
    return jnp.concatenate([-t[:, ROPE // 2:], t[:, :ROPE // 2]], axis=-1)


def _rot_t(t):
    return jnp.concatenate([t[:, ROPE // 2:], -t[:, :ROPE // 2]], axis=-1)


def _rope(t, cos, sin):
    return t * cos + _rot(t) * sin


def _rope_bwd(d, cos, sin):
    return d * cos + _rot_t(d * sin)


def _sigmoid(x):
    return jax.nn.sigmoid(x)


def _shift_down(x, halo, j):
    if j == 0:
        return x
    xr = pltpu.roll(x, j, 0)
    hr = pltpu.roll(halo, j, 0)
    row = lax.broadcasted_iota(jnp.int32, halo.shape, 0)
    top = jnp.where(row < j, hr, xr[:SUBLANES])
    return jnp.concatenate([top, xr[SUBLANES:]], axis=0)


def _shift_up(x, nxt, j):
    if j == 0:
        return x
    n = x.shape[0]
    xr = pltpu.roll(x, n - j, 0)
    nr = pltpu.roll(nxt, SUBLANES - j, 0)
    row = lax.broadcasted_iota(jnp.int32, nxt.shape, 0)
    bot = jnp.where(row >= SUBLANES - j, nr, xr[n - SUBLANES:])
    return jnp.concatenate([xr[:n - SUBLANES], bot], axis=0)


def _chunk_cumsum(y, row_in_chunk):
    s = 1
    while s < CHUNK:
        y = y + jnp.where(row_in_chunk >= s, pltpu.roll(y, s, 0), 0.0)
        s *= 2
    return y


def _chunk_rev_cumsum(y, row_in_chunk):
    n = y.shape[0]
    s = 1
    while s < CHUNK:
        y = y + jnp.where(row_in_chunk + s < CHUNK, pltpu.roll(y, n - s, 0), 0.0)
        s *= 2
    return y


def _together(generators):
    alive = list(generators)
    while alive:
        nxt = []
        for g in alive:
            try:
                next(g)
                nxt.append(g)
            except StopIteration:
                pass
        alive = nxt
        yield


def _lockstep(generators):
    for _ in _together(generators):
        pass


def _pick_lane(tile, lane, idx):
    return jnp.sum(jnp.where(lane == idx, tile, 0.0), axis=-1, keepdims=True)


def _divisor_tile(n, cap, unit=LANES):
    best = unit
    t = unit
    while t <= min(n, cap):
        if n % t == 0:
            best = t
        t += unit
    return n if n <= cap else best


def _in_proj(x2, w_an, w_in_p, conv_w, alog_l, dt_l, S):
    T, D = x2.shape
    N = w_in_p.shape[1]
    tm = min(512, S)
    assert S % tm == 0 and T % tm == 0, "a token tile must not straddle two sequences"
    tiles_per_seq = S // tm
    C3 = 3 * GDN_WIDTH
    H = GDN_HEADS

    def body(x_ref, wn_ref, w_ref, cw_ref, alog_ref, dt_ref, proj_ref, xn_ref, q_out, k_out, v_out, gates_out,
             halo_s):
        xn, _ = _rms(x_ref[...], wn_ref[...])
        xn = xn.astype(MXU_DTYPE)
        xn_ref[...] = xn
        proj = jnp.dot(xn, w_ref[...], preferred_element_type=F32)
        proj_ref[...] = proj
        u = proj[:, P_GQKV:P_GQKV + C3]

        @pl.when(pl.program_id(0) == 0)
        def _():
            halo_s[...] = jnp.zeros_like(halo_s)

        halo = jnp.where(pl.program_id(0) % tiles_per_seq == 0, 0.0, halo_s[...])
        halo_s[...] = u[tm - SUBLANES:, :]
        c, _ = _conv_taps(u, halo, cw_ref[...])
        a = c * _sigmoid(c)
        for h in range(H):
            xq = a[:, h * GDN_DIM:(h + 1) * GDN_DIM]
            xk = a[:, GDN_WIDTH + h * GDN_DIM:GDN_WIDTH + (h + 1) * GDN_DIM]
            q_out[h] = _l2n(xq, GDN_QSCALE)
            k_out[h] = _l2n(xk, 1.0)
            v_out[h] = a[:, 2 * GDN_WIDTH + h * GDN_DIM:2 * GDN_WIDTH + (h + 1) * GDN_DIM]
        lane = lax.broadcasted_iota(jnp.int32, (tm, LANES), 1)
        ric = lax.broadcasted_iota(jnp.int32, (tm, LANES), 0) % CHUNK
        g, beta = _gate_values(proj[:, P_GAB:P_GAB + LANES], alog_ref[...], dt_ref[...], lane)
        gates_out[...] = _chunk_cumsum(g, ric) + beta

    hspec = pl.BlockSpec((H, tm, GDN_DIM), lambda i: (0, i, 0))
    vec = pl.BlockSpec((1, LANES), lambda i: (0, 0))
    return pl.pallas_call(
        body, grid=(T // tm,), name="in_proj",
        in_specs=[pl.BlockSpec((tm, D), lambda i: (i, 0)), pl.BlockSpec((1, D), lambda i: (0, 0)),
                  pl.BlockSpec((D, N), lambda i: (0, 0)), pl.BlockSpec((CONV_W, C3), lambda i: (0, 0)), vec, vec],
        out_specs=[pl.BlockSpec((tm, N), lambda i: (i, 0)), pl.BlockSpec((tm, D), lambda i: (i, 0)),
                   hspec, hspec, hspec, pl.BlockSpec((tm, LANES), lambda i: (i, 0))],
        out_shape=[SDS((T, N), F32), SDS((T, D), MXU_DTYPE)] + [SDS((H, T, GDN_DIM), F32)] * 3
                  + [SDS((T, LANES), F32)],
        scratch_shapes=[pltpu.VMEM((SUBLANES, C3), F32)],
        compiler_params=_params(("arbitrary",)),
    )(x2, w_an, w_in_p, conv_w, alog_l, dt_l)


def _mla_pre(proj, cosf, sinf, w_qln, w_kvln, w_uq_p, w_ukv, qnw, knw, transfer=None):
    T = proj.shape[0]
    tm = min(256, T)
    H = MLA_HEADS

    def body(ql_ref, kvl_ref, kpe_ref, cos_ref, sin_ref, wq_ref, wkv_ref, uq_ref, ukv_ref, qnw_ref, knw_ref,
             q_out, k_out, v_out):
        rms = functools.partial(_rms, on_mxu=True)
        cos, sin = cos_ref[...], sin_ref[...]
        qnw_, knw_ = qnw_ref[...], knw_ref[...]
        qn, _ = rms(ql_ref[...], wq_ref[...])
        kvn, _ = rms(kvl_ref[...], wkv_ref[...])
        qraw = _mm(qn, uq_ref[...])
        kvraw = _mm(kvn, ukv_ref[...])
        kpe = _rope(rms(kpe_ref[...][:, :ROPE], knw_[:, NOPE:])[0], cos, sin)
        for h in range(H):
            qn_h = rms(qraw[:, h * NOPE:(h + 1) * NOPE], qnw_[:, :NOPE])[0]
            qp_h = _rope(rms(qraw[:, H * NOPE + h * ROPE:H * NOPE + (h + 1) * ROPE], qnw_[:, NOPE:])[0], cos, sin)
            q_out[h] = (jnp.concatenate([qn_h, qp_h], axis=-1) * ATT_SCALE).astype(MXU_DTYPE)
            kn_h = rms(kvraw[:, h * 256:h * 256 + NOPE], knw_[:, :NOPE])[0]
            k_out[h] = jnp.concatenate([kn_h, kpe], axis=-1).astype(MXU_DTYPE)
            v_out[h] = kvraw[:, h * 256 + NOPE:(h + 1) * 256].astype(MXU_DTYPE)

    full = lambda a: pl.BlockSpec(a.shape, lambda i: (0,) * a.ndim)
    return _call_beside(
        body, transfer, grid=(T // tm,), name="mla_pre", scratch_shapes=[], semantics=("arbitrary",),
        args=(proj, proj, proj, cosf, sinf, w_qln, w_kvln, w_uq_p, w_ukv, qnw, knw),
        in_specs=[pl.BlockSpec((tm, 256), lambda i: (i, P_QLAT // 256)),
                  pl.BlockSpec((tm, 256), lambda i: (i, P_KVLAT // 256)),
                  pl.BlockSpec((tm, 128), lambda i: (i, P_KPE // 128)),
                  pl.BlockSpec((tm, ROPE), lambda i: (i, 0)), pl.BlockSpec((tm, ROPE), lambda i: (i, 0)),
                  full(w_qln), full(w_kvln), full(w_uq_p), full(w_ukv), full(qnw), full(knw)],
        out_specs=[pl.BlockSpec((H, tm, QK_DIM), lambda i: (0, i, 0)),
                   pl.BlockSpec((H, tm, QK_DIM), lambda i: (0, i, 0)),
                   pl.BlockSpec((H, tm, V_DIM), lambda i: (0, i, 0))],
        out_shape=[SDS((H, T, QK_DIM), MXU_DTYPE), SDS((H, T, QK_DIM), MXU_DTYPE), SDS((H, T, V_DIM), MXU_DTYPE)])


def _attn_fwd(q4, k4, v4, B, S, transfer=None):
    H = MLA_HEADS
    bq = min(ATTN_BLOCK, S)
    nq = S // bq
    rows = bq // ATTN_CHAINS

    def body(q_ref, k_ref, v_ref, o_ref, lse_ref):
        col = lax.broadcasted_iota(jnp.int32, (rows, bq), 1)
        row = lax.broadcasted_iota(jnp.int32, (rows, bq), 0)

        def q_step(qi, carry):
            qs = pl.multiple_of(qi * bq, bq)
            qsub = [q_ref[0, pl.ds(qs + j * rows, rows), :] for j in range(ATTN_CHAINS)]

            def k_block(ks, cs, diagonal):
                k = k_ref[0, pl.ds(ks, bq), :]
                v = v_ref[0, pl.ds(ks, bq), :]
                out = [None] * ATTN_CHAINS

                def chain(j):
                    m, l, acc = cs[j]
                    s = _mm_nt(qsub[j], k)
                    yield
                    if diagonal:
                        s = jnp.where(col <= row + j * rows, s, -jnp.inf)
                    m_new = jnp.maximum(m, jnp.max(s, axis=-1, keepdims=True))
                    p = jnp.exp(s - m_new)
                    a = jnp.exp(m - m_new)
                    l_new = a * l + jnp.sum(p, axis=-1, keepdims=True)
                    yield
                    out[j] = (m_new, l_new, a * acc + _mm(p, v))

                _lockstep([chain(j) for j in range(ATTN_CHAINS)])
                return tuple(out)

            init = tuple((jnp.full((rows, 1), -jnp.inf, F32), jnp.zeros((rows, 1), F32),
                          jnp.zeros((rows, V_DIM), F32)) for _ in range(ATTN_CHAINS))
            cs = lax.fori_loop(0, qi, lambda kj, c: k_block(pl.multiple_of(kj * bq, bq), c, False), init)
            for j, (m, l, acc) in enumerate(k_block(qs, cs, True)):
                o_ref[0, pl.ds(qs + j * rows, rows), :] = acc / l
                lse_ref[0, pl.ds(qs + j * rows, rows), :] = m + jnp.log(l)
            return carry

        lax.fori_loop(0, nq, q_step, 0)

    spec = lambda d: pl.BlockSpec((1, S, d), lambda h, b: (h, b, 0))
    return _call_beside(
        body, transfer, grid=(H, B), name="attn_fwd",
        in_specs=[spec(QK_DIM), spec(QK_DIM), spec(V_DIM)],
        out_specs=[spec(V_DIM), spec(1)],
        out_shape=[SDS((H, B * S, V_DIM), F32), SDS((H, B * S, 1), F32)],
        scratch_shapes=[], semantics=("arbitrary", "arbitrary"), args=(q4, k4, v4))


def _conv_taps(u, halo, w):
    sh = [_shift_down(u, halo, j) for j in range(CONV_W)]
    c = w[0:1] * sh[3] + w[1:2] * sh[2] + w[2:3] * sh[1] + w[3:4] * sh[0]
    return c, sh


def _gate_values(gab, alog_l, dt_l, lane):
    g = -jnp.exp(alog_l) * jax.nn.softplus(gab + dt_l)
    g = jnp.where(lane < GDN_HEADS, g, 0.0)
    beta = jnp.where((lane >= GDN_HEADS) & (lane < 2 * GDN_HEADS), _sigmoid(gab), 0.0)
    return g, beta


def _unit_lower_inverses(Ls, eye):
    Ps = [eye - L for L in Ls]
    Ms = [_split(-L) for L in Ls]
    for _ in range(5):
        sq = [_mm_split(m, m) for m in Ms]
        Ms = [_split(s) for s in sq]
        Ps = [p + _mm_split(_split(p), m) for p, m in zip(Ps, Ms)]
    return Ps


def _chunk_decays(gt, lane, h, ri, ci, rcol):
    Gc = _pick_lane(gt, lane, h)
    bt = _pick_lane(gt, lane, h + GDN_HEADS)
    Gb = jnp.broadcast_to(Gc, (CHUNK, CHUNK))
    Gam = jnp.where(ri >= ci, jnp.exp(Gb - Gb.T), 0.0)
    Gl = jnp.sum(jnp.where(rcol == CHUNK - 1, Gc, 0.0), axis=0, keepdims=True)
    return Gc, bt, Gam, jnp.exp(Gc), jnp.exp(Gl - Gc), jnp.exp(Gl)


GDN_FWD_UNROLL = 16
GDN_BWD_UNROLL = 8
GDN_RECUR_STEPS_PER_STAGE = 2


def _gdn_fwd(qg, kg, vg, gates, B, S, transfer=None):
    H, D, C = GDN_HEADS, GDN_DIM, CHUNK
    NC = S // C
    P = 2 if B % 2 == 0 else 1
    Sb, NCb = P * S, P * NC
    U = GDN_FWD_UNROLL if NCb % GDN_FWD_UNROLL == 0 else 1
    NG = NCb // U

    def body(q_ref, k_ref, v_ref, g_ref, o_ref, st_ref, ai_ref, u_ref, w_ref, q2_s, au_s, bc_s, w2_s, el_s):
        h = pl.program_id(0)
        lane = lax.broadcasted_iota(jnp.int32, (C, LANES), 1)
        ri = lax.broadcasted_iota(jnp.int32, (C, C), 0)
        ci = lax.broadcasted_iota(jnp.int32, (C, C), 1)
        rcol = lax.broadcasted_iota(jnp.int32, (C, 1), 0)
        eye = (ri == ci).astype(F32)

        def group(gi, c):
            ns = [gi * U + j for j in range(U)]
            css = [pl.multiple_of(n * C, C) for n in ns]
            qs = [q_ref[0, pl.ds(cs, C), :] for cs in css]
            ks = [k_ref[0, pl.ds(cs, C), :] for cs in css]
            vs = [v_ref[0, pl.ds(cs, C), :] for cs in css]
            decs = [_chunk_decays(g_ref[pl.ds(cs, C), :], lane, h, ri, ci, rcol) for cs in css]
            qks = [_mm_nt(jnp.concatenate([q, k], axis=0), k) for q, k in zip(qs, ks)]
            ainvs = _unit_lower_inverses(
                [jnp.where(ri > ci, d[1] * qk[C:] * d[2], 0.0) for qk, d in zip(qks, decs)], eye)
            sols = [_mm_exact(a, jnp.concatenate([v * d[1], k * (d[1] * d[3])], axis=-1))
                    for a, k, v, d in zip(ainvs, ks, vs, decs)]
            atuw = [_mm(qk[:C] * d[2], sol) for qk, d, sol in zip(qks, decs, sols)]
            kduw = [_mm_tn(k * d[4], sol) for k, d, sol in zip(ks, decs, sols)]
            for n, cs, q, a, sol, au, ku, (Gc, bt, Gam, e, f, eL) in zip(ns, css, qs, ainvs, sols, atuw, kduw, decs):
                u_ref[0, pl.ds(cs, C), :] = sol[:, :D]
                w_ref[0, pl.ds(cs, C), :] = sol[:, D:]
                au_s[pl.ds(cs, C), :] = au[:, :D]
                q2_s[pl.ds(cs, C), :] = q * e - au[:, D:]
                bc_s[n] = ku[:, :D]
                w2_s[n] = ku[:, D:]
                el_s[n] = jnp.broadcast_to(eL, (SUBLANES, LANES))
                ai_ref[0, n] = a.T
            return c

        lax.fori_loop(0, NG, group, 0)

        def step(n, states):
            new = []
            for p, S_ in enumerate(states):
                m = p * NC + n
                cs = pl.multiple_of(m * C, C)
                o_ref[0, pl.ds(cs, C), :] = _mm(q2_s[pl.ds(cs, C), :], S_) + au_s[pl.ds(cs, C), :]
                st_ref[0, m] = S_
                new.append(S_ * el_s[m, 0:1, :] + bc_s[m] - _mm(w2_s[m], S_))
            return tuple(new)

        lax.fori_loop(0, NC, step, tuple(jnp.zeros((D, D), F32) for _ in range(P)))

    spec = pl.BlockSpec((1, Sb, D), lambda h, b: (h, b, 0))
    return _call_beside(
        body, transfer, grid=(H, B // P), name="gdn_fwd",
        in_specs=[spec, spec, spec, pl.BlockSpec((Sb, LANES), lambda h, b: (b, 0))],
        out_specs=[spec, pl.BlockSpec((1, NCb, D, D), lambda h, b: (h, b, 0, 0)),
                   pl.BlockSpec((1, NCb, C, C), lambda h, b: (h, b, 0, 0)), spec, spec],
        out_shape=[SDS((H, B * S, D), F32), SDS((H, B * NC, D, D), F32), SDS((H, B * NC, C, C), F32),
                   SDS((H, B * S, D), F32), SDS((H, B * S, D), F32)],
        scratch_shapes=[pltpu.VMEM((Sb, D), F32), pltpu.VMEM((Sb, D), F32), pltpu.VMEM((NCb, D, D), F32),
                        pltpu.VMEM((NCb, D, D), F32), pltpu.VMEM((NCb, SUBLANES, LANES), F32)],
        semantics=("arbitrary", "arbitrary"), args=(qg, kg, vg, gates))


def _mix_out(o_mla, o_gdn, proj, x2, mla_w, gdn_w, w_out):
    T, D = x2.shape
    tm = min(512, T)
    H = MLA_HEADS

    def body(om_ref, og_ref, z_ref, x_ref, mw_ref, gw_ref, w_ref, h_ref, mix_ref):
        z = z_ref[...]
        parts = [_rms(om_ref[h], mw_ref[h:h + 1, :])[0] for h in range(H)]
        for h in range(GDN_HEADS):
            zh = z[:, h * GDN_DIM:(h + 1) * GDN_DIM]
            parts.append(_rms(og_ref[h], gw_ref[...])[0] * (zh * _sigmoid(zh)))
        mix = jnp.concatenate(parts, axis=-1).astype(MXU_DTYPE)
        mix_ref[...] = mix
        h_ref[...] = x_ref[...] + jnp.dot(mix, w_ref[...], preferred_element_type=F32)

    hspec = pl.BlockSpec((H, tm, V_DIM), lambda i: (0, i, 0))
    return pl.pallas_call(
        body, grid=(T // tm,), name="mix_out",
        in_specs=[hspec, hspec, pl.BlockSpec((tm, GDN_WIDTH), lambda i: (i, P_GZ // GDN_WIDTH)),
                  pl.BlockSpec((tm, D), lambda i: (i, 0)),
                  pl.BlockSpec((H, V_DIM), lambda i: (0, 0)), pl.BlockSpec((1, GDN_DIM), lambda i: (0, 0)),
                  pl.BlockSpec((D, D), lambda i: (0, 0))],
        out_specs=[pl.BlockSpec((tm, D), lambda i: (i, 0)), pl.BlockSpec((tm, D), lambda i: (i, 0))],
        out_shape=[SDS((T, D), F32), SDS((T, D), MXU_DTYPE)],
        compiler_params=_params(("arbitrary",)),
    )(o_mla, o_gdn, proj, x2, mla_w, gdn_w, w_out)


def _mlp_fwd(h2, w_mn, w_up, w_down, target):
    T, D = h2.shape
    ns, _, ts = w_up.shape
    F = ns * ts
    tm = min(512, T)
    G = MLP_FWD_SHARDS
    tf, nf = G * ts, ns // G

    def body(h_ref, wn_ref, up_w, down_w, t_ref, up_ref, hn_ref, dy_ref, loss_ref, y_acc):
        j = pl.program_id(1)

        @pl.when(j == 0)
        def _():
            hn_ref[...] = _rms(h_ref[...], wn_ref[...])[0].astype(MXU_DTYPE)
            y_acc[...] = h_ref[...]

        parts = []
        for c in range(G):
            up = jnp.dot(hn_ref[...], up_w[c], preferred_element_type=F32)
            up_ref[:, c * ts:(c + 1) * ts] = up.astype(MXU_DTYPE)
            r = jnp.maximum(up, 0.0)
            parts.append(_mm(r * r, down_w[c * ts:(c + 1) * ts, :]))
        y_acc[...] += functools.reduce(jnp.add, parts)

        @pl.when(j == nf - 1)
        def _():
            err = y_acc[...] - t_ref[...]
            dy_ref[...] = err / D
            loss_ref[...] = jnp.full((1, SUBLANES, LANES), jnp.sum(err * err), F32)

    return pl.pallas_call(
        body, grid=(T // tm, nf), name="mlp_fwd",
        in_specs=[pl.BlockSpec((tm, D), lambda i, j: (i, 0)), pl.BlockSpec((1, D), lambda i, j: (0, 0)),
                  pl.BlockSpec((G, D, ts), lambda i, j: (j, 0, 0)), pl.BlockSpec((tf, D), lambda i, j: (j, 0)),
                  pl.BlockSpec((tm, D), lambda i, j: (i, 0))],
        out_specs=[pl.BlockSpec((tm, tf), lambda i, j: (i, j)), pl.BlockSpec((tm, D), lambda i, j: (i, 0)),
                   pl.BlockSpec((tm, D), lambda i, j: (i, 0)),
                   pl.BlockSpec((1, SUBLANES, LANES), lambda i, j: (i, 0, 0))],
        out_shape=[SDS((T, F), MXU_DTYPE), SDS((T, D), MXU_DTYPE), SDS((T, D), F32),
                   SDS((T // tm, SUBLANES, LANES), F32)],
        scratch_shapes=[pltpu.VMEM((tm, D), F32)],
        compiler_params=_params(("arbitrary", "arbitrary")),
    )(h2, w_mn, w_up, w_down, target)


def _mlp_bwd(dy, up, h2, w_mn, w_up, w_down):
    T, D = h2.shape
    ns, _, ts = w_up.shape
    F = ns * ts
    tm = min(512, T)
    G = MLP_BWD_SHARDS
    tf, nf = G * ts, ns // G

    def body(dy_ref, up_ref, h_ref, wn_ref, up_w, down_w, dh_ref, dhb_ref, dup_ref, act_ref, dyb_ref, dwn_ref, acc):
        i, j = pl.program_id(0), pl.program_id(1)

        @pl.when((i == 0) & (j == 0))
        def _():
            dwn_ref[...] = jnp.zeros_like(dwn_ref)

        @pl.when(j == 0)
        def _():
            acc[...] = jnp.zeros_like(acc)
            dyb_ref[...] = dy_ref[...].astype(MXU_DTYPE)

        parts = []
        for c in range(G):
            cols = slice(c * ts, (c + 1) * ts)
            r = jnp.maximum(up_ref[:, cols].astype(F32), 0.0)
            act_ref[:, cols] = (r * r).astype(MXU_DTYPE)
            dup = (_mm_nt(dyb_ref[...], down_w[cols, :]) * (2.0 * r)).astype(MXU_DTYPE)
            dup_ref[:, cols] = dup
            parts.append(_mm_nt(dup, up_w[c]))
        acc[...] += functools.reduce(jnp.add, parts)

        @pl.when(j == nf - 1)
        def _():
            hv = h_ref[...]
            _, rr = _rms(hv, wn_ref[...])
            dx, dw = _rms_bwd(acc[...], hv, wn_ref[...], rr)
            dh = dy_ref[...] + dx
            dh_ref[...] = dh
            dhb_ref[...] = dh.astype(MXU_DTYPE)
            dwn_ref[...] += dw

    row = lambda i, j: (i, 0)
    return pl.pallas_call(
        body, grid=(T // tm, nf), name="mlp_bwd",
        in_specs=[pl.BlockSpec((tm, D), row), pl.BlockSpec((tm, tf), lambda i, j: (i, j)), pl.BlockSpec((tm, D), row),
                  pl.BlockSpec((1, D), lambda i, j: (0, 0)),
                  pl.BlockSpec((G, D, ts), lambda i, j: (j, 0, 0)), pl.BlockSpec((tf, D), lambda i, j: (j, 0))],
        out_specs=[pl.BlockSpec((tm, D), row), pl.BlockSpec((tm, D), row),
                   pl.BlockSpec((tm, tf), lambda i, j: (i, j)), pl.BlockSpec((tm, tf), lambda i, j: (i, j)),
                   pl.BlockSpec((tm, D), row), pl.BlockSpec((1, D), lambda i, j: (0, 0))],
        out_shape=[SDS((T, D), F32), SDS((T, D), MXU_DTYPE), SDS((T, F), MXU_DTYPE), SDS((T, F), MXU_DTYPE),
                   SDS((T, D), MXU_DTYPE), SDS((1, D), F32)],
        scratch_shapes=[pltpu.VMEM((tm, D), F32)],
        compiler_params=_params(("arbitrary", "arbitrary")),
    )(dy, up, h2, w_mn, w_up, w_down)


def _mix_bwd(dhb, o_mla, o_gdn, proj, mla_w, gdn_w, w_out):
    T, D = dhb.shape
    tm = min(512, T)
    H = MLA_HEADS

    def body(dh_ref, om_ref, og_ref, z_ref, mw_ref, gw_ref, w_ref, dom_ref, dog_ref, dz_ref, dmw_ref, dgw_ref,
             delta_ref):
        @pl.when(pl.program_id(0) == 0)
        def _():
            dmw_ref[...] = jnp.zeros_like(dmw_ref)
            dgw_ref[...] = jnp.zeros_like(dgw_ref)

        dmix = _mm_nt(dh_ref[...], w_ref[...])
        z = z_ref[...]
        dmw, dzs = [], []
        dgw = jnp.zeros((1, GDN_DIM), F32)
        for h in range(H):
            o = om_ref[h]
            w = mw_ref[h:h + 1, :]
            _, r = _rms(o, w)
            dx, dw = _rms_bwd(dmix[:, h * V_DIM:(h + 1) * V_DIM], o, w, r)
            dom_ref[h] = dx.astype(MXU_DTYPE)
            delta_ref[h] = jnp.sum(dx * o, axis=-1, keepdims=True)
            dmw.append(dw)
        for h in range(GDN_HEADS):
            o = og_ref[h]
            w = gw_ref[...]
            zh = z[:, h * GDN_DIM:(h + 1) * GDN_DIM]
            sg = _sigmoid(zh)
            yn, r = _rms(o, w)
            dy = dmix[:, H * V_DIM + h * GDN_DIM:H * V_DIM + (h + 1) * GDN_DIM]
            dzs.append(dy * yn * (sg * (1.0 + zh * (1.0 - sg))))
            dx, dw = _rms_bwd(dy * (zh * sg), o, w, r)
            dog_ref[h] = dx.astype(MXU_DTYPE)
            dgw = dgw + dw
        dz_ref[...] = jnp.concatenate(dzs, axis=-1).astype(MXU_DTYPE)
        dmw_ref[...] += jnp.concatenate(dmw, axis=0)
        dgw_ref[...] += dgw

    hspec = pl.BlockSpec((H, tm, V_DIM), lambda i: (0, i, 0))
    return pl.pallas_call(
        body, grid=(T // tm,), name="mix_bwd",
        in_specs=[pl.BlockSpec((tm, D), lambda i: (i, 0)), hspec, hspec,
                  pl.BlockSpec((tm, GDN_WIDTH), lambda i: (i, P_GZ // GDN_WIDTH)),
                  pl.BlockSpec((H, V_DIM), lambda i: (0, 0)), pl.BlockSpec((1, GDN_DIM), lambda i: (0, 0)),
                  pl.BlockSpec((D, D), lambda i: (0, 0))],
        out_specs=[hspec, hspec, pl.BlockSpec((tm, GDN_WIDTH), lambda i: (i, 0)),
                   pl.BlockSpec((H, V_DIM), lambda i: (0, 0)), pl.BlockSpec((1, GDN_DIM), lambda i: (0, 0)),
                   pl.BlockSpec((H, tm, 1), lambda i: (0, i, 0))],
        out_shape=[SDS((H, T, V_DIM), MXU_DTYPE), SDS((H, T, GDN_DIM), MXU_DTYPE), SDS((T, GDN_WIDTH), MXU_DTYPE),
                   SDS((H, V_DIM), F32), SDS((1, GDN_DIM), F32), SDS((H, T, 1), F32)],
        compiler_params=_params(("arbitrary",)),
    )(dhb, o_mla, o_gdn, proj, mla_w, gdn_w, w_out)


def _attn_bwd(q4, k4, v4, do4, delta4, lse4, B, S, transfer=None):
    H = MLA_HEADS
    bq = min(ATTN_BLOCK, S)
    nq = S // bq
    rows = bq // ATTN_CHAINS

    def body(q_ref, k_ref, v_ref, do_ref, delta_ref, lse_ref, dq_ref, dk_ref, dv_ref):
        dq_ref[...] = jnp.zeros_like(dq_ref)
        dk_ref[...] = jnp.zeros_like(dk_ref)
        dv_ref[...] = jnp.zeros_like(dv_ref)

        col = lax.broadcasted_iota(jnp.int32, (rows, bq), 1)
        row = lax.broadcasted_iota(jnp.int32, (rows, bq), 0)

        def k_step(kj, carry):
            ks = pl.multiple_of(kj * bq, bq)
            k = k_ref[0, pl.ds(ks, bq), :]
            v = v_ref[0, pl.ds(ks, bq), :]

            def q_block(qs, diagonal):
                dks, dvs = [None] * ATTN_CHAINS, [None] * ATTN_CHAINS

                def chain(j):
                    sl = pl.ds(qs + j * rows, rows)
                    q = q_ref[0, sl, :]
                    do = do_ref[0, sl, :].astype(MXU_DTYPE)
                    s = _mm_nt(q, k)
                    dp = _mm_nt(do, v)
                    yield
                    p = jnp.exp(s - lse_ref[0, sl, :])
                    if diagonal:
                        p = jnp.where(col <= row + j * rows, p, 0.0)
                    ds = p * (dp - delta_ref[0, sl, :])
                    yield
                    dvs[j] = _mm_tn(p, do)
                    dks[j] = _mm_tn(ds, q)
                    dq_ref[0, sl, :] += _mm(ds, k)

                _lockstep([chain(j) for j in range(ATTN_CHAINS)])
                dv_ref[0, pl.ds(ks, bq), :] += functools.reduce(jnp.add, dvs)
                dk_ref[0, pl.ds(ks, bq), :] += functools.reduce(jnp.add, dks)

            q_block(ks, True)

            def q_step(qi, c):
                q_block(pl.multiple_of(qi * bq, bq), False)
                return c

            lax.fori_loop(kj + 1, nq, q_step, 0)
            return carry

        lax.fori_loop(0, nq, k_step, 0)

    spec = lambda d: pl.BlockSpec((1, S, d), lambda h, b: (h, b, 0))
    return _call_beside(
        body, transfer, grid=(H, B), name="attn_bwd",
        in_specs=[spec(QK_DIM), spec(QK_DIM), spec(V_DIM), spec(V_DIM), spec(1), spec(1)],
        out_specs=[spec(QK_DIM), spec(QK_DIM), spec(V_DIM)],
        out_shape=[SDS((H, B * S, QK_DIM), F32), SDS((H, B * S, QK_DIM), F32), SDS((H, B * S, V_DIM), F32)],
        scratch_shapes=[], semantics=("arbitrary", "arbitrary"),
        args=(q4, k4, v4, do4, delta4, lse4))


def _gdn_bwd(qg, kg, vg, gates, states, ainv, u4, w4, do4, B, S, transfer=None):
    H, D, C = GDN_HEADS, GDN_DIM, CHUNK
    NC = S // C
    U = GDN_BWD_UNROLL if NC % GDN_BWD_UNROLL == 0 else 1
    NG = NC // U

    def body(q_ref, k_ref, v_ref, g_ref, st_ref, ai_ref, u_ref, w_ref, do_ref, dq_ref, dk_ref, dv_ref, dgb_ref,
             kd_s, x1_s, x2_s, el_s, dvn_s, ds_s, w2t_s):
        h = pl.program_id(0)
        lane = lax.broadcasted_iota(jnp.int32, (C, LANES), 1)
        ri = lax.broadcasted_iota(jnp.int32, (C, C), 0)
        ci = lax.broadcasted_iota(jnp.int32, (C, C), 1)
        rcol = lax.broadcasted_iota(jnp.int32, (C, 1), 0)

        def rsum(a):
            return jnp.sum(a, axis=-1, keepdims=True)

        def prepare(n):
            cs = n * C
            q = q_ref[0, pl.ds(cs, C), :]
            k = k_ref[0, pl.ds(cs, C), :]
            do = do_ref[0, pl.ds(cs, C), :]
            Gc, bt, Gam, e, f, eL = _chunk_decays(g_ref[pl.ds(cs, C), :], lane, h, ri, ci, rcol)
            At = _mm_nt(q, k) * Gam
            yield
            x1 = _mm_tn(At, do)
            x2 = _mm_tn(q * e, do)
            kd = k * f
            w = w_ref[0, pl.ds(cs, C), :]
            yield
            x1_s[pl.ds(cs, C), :] = x1
            x2_s[n] = x2 - _mm_tn(w, x1)
            w2t_s[n] = _mm_tn(w, kd)
            kd_s[pl.ds(cs, C), :] = kd
            el_s[n] = jnp.broadcast_to(eL, (SUBLANES, LANES))

        def recur(n, dS):
            cs = n * C
            ds_s[n] = dS
            dvn_s[pl.ds(cs, C), :] = x1_s[pl.ds(cs, C), :] + _mm(kd_s[pl.ds(cs, C), :], dS)
            return x2_s[n] + el_s[n, 0:1, :] * dS - _mm(w2t_s[n], dS)

        def local(n):
            cs = n * C
            q = q_ref[0, pl.ds(cs, C), :]
            k = k_ref[0, pl.ds(cs, C), :]
            v = v_ref[0, pl.ds(cs, C), :]
            do = do_ref[0, pl.ds(cs, C), :]
            u = u_ref[0, pl.ds(cs, C), :]
            w = w_ref[0, pl.ds(cs, C), :]
            dvn = dvn_s[pl.ds(cs, C), :]
            dS = ds_s[n]
            Gc, bt, Gam, e, f, eL = _chunk_decays(g_ref[pl.ds(cs, C), :], lane, h, ri, ci, rcol)
            S0 = st_ref[0, n]
            AinvT = ai_ref[0, n]
            qk = _mm_nt(jnp.concatenate([q, k], axis=0), k)
            QK, KK = qk[:C], qk[C:]
            be = bt * e
            sol = jnp.concatenate([u, w], axis=-1)
            vn = u - _mm(w, S0)
            yield
            dAt = jnp.where(ri >= ci, _mm_nt(do, vn), 0.0)
            dqd = _mm_nt(do, S0)
            dw = -_mm_nt(dvn, S0)
            dkd = _mm_nt(vn, dS)
            deL = jnp.sum(rsum(dS * S0), axis=0, keepdims=True)
            yield
            dR = _mm_exact(AinvT, jnp.concatenate([dvn, dw], axis=-1))
            dR1, dR2 = dR[:, :D], dR[:, D:]
            yield
            dL = jnp.where(ri > ci, -_mm_nt(dR, sol), 0.0)
            yield
            dv_ref[0, pl.ds(cs, C), :] = dR1 * bt
            r2 = rsum(dR2 * k)
            X = dL * Gam
            dbt = rsum(dR1 * v) + r2 * e + rsum(X * KK)
            de = r2 * bt + rsum(dqd * q)
            dKK = X * bt
            dQK = dAt * Gam
            dq_ref[0, pl.ds(cs, C), :] = _mm(dQK, k) + dqd * e
            dk_ref[0, pl.ds(cs, C), :] = dR2 * be + _mm(dKK + dKK.T, k) + _mm_tn(dQK, q) + dkd * f
            df = rsum(dkd * k)
            Z = (dL * (bt * KK) + dAt * QK) * Gam
            dG = rsum(Z) - rsum(Z.T) + de * e - df * f
            dGl = jnp.sum(df * f, axis=0, keepdims=True) + deL * eL
            dG = dG + jnp.where(rcol == C - 1, dGl, 0.0)
            dgb_ref[0, pl.ds(cs, C), :] = jnp.where(lane == 0, dG, jnp.where(lane == 1, dbt, 0.0))

        state = [jnp.zeros((D, D), F32)]

        def recur_group(g):
            for j, n in enumerate(reversed(range(g * U, (g + 1) * U))):
                state[0] = recur(n, state[0])
                if j % GDN_RECUR_STEPS_PER_STAGE == GDN_RECUR_STEPS_PER_STAGE - 1:
                    yield

        def stage(fn, g):
            return _together([fn(g * U + j) for j in range(U)])

        for step in range(NG + 2):
            jobs = [(stage, prepare, NG - 1 - step), (None, None, NG - step), (stage, local, NG + 1 - step)]
            _lockstep([recur_group(g) if make is None else make(fn, g) for make, fn, g in jobs if 0 <= g < NG])

    spec = pl.BlockSpec((1, S, D), lambda h, b: (h, b, 0))
    return _call_beside(
        body, transfer, grid=(H, B), name="gdn_bwd",
        in_specs=[spec, spec, spec, pl.BlockSpec((S, LANES), lambda h, b: (b, 0)),
                  pl.BlockSpec((1, NC, D, D), lambda h, b: (h, b, 0, 0)),
                  pl.BlockSpec((1, NC, C, C), lambda h, b: (h, b, 0, 0)), spec, spec, spec],
        out_specs=[spec, spec, spec, spec],
        out_shape=[SDS((H, B * S, D), F32)] * 4,
        scratch_shapes=[pltpu.VMEM((S, D), F32), pltpu.VMEM((S, D), F32), pltpu.VMEM((NC, D, D), F32),
                        pltpu.VMEM((NC, SUBLANES, LANES), F32), pltpu.VMEM((S, D), F32),
                        pltpu.VMEM((NC, D, D), F32), pltpu.VMEM((NC, D, D), F32)],
        semantics=("arbitrary", "arbitrary"), args=(qg, kg, vg, gates, states, ainv, u4, w4, do4))


def _gdn_pre_bwd(proj, conv_w, alog_l, dt_l, dq4, dk4, dv4, dgb4, S):
    T = proj.shape[0]
    tm = min(256, T)
    tiles_per_seq = S // tm
    C3 = 3 * GDN_WIDTH
    H = GDN_HEADS

    def body(u_ref, halo_ref, gab_ref, w_ref, alog_ref, dt_ref, dq_ref, dk_ref, dv_ref, dgb_ref,
             dc_ref, dgab_ref, dcw_ref, dalog_ref, ddt_ref):
        i = pl.program_id(0)

        @pl.when(i == 0)
        def _():
            dcw_ref[...] = jnp.zeros_like(dcw_ref)
            dalog_ref[...] = jnp.zeros_like(dalog_ref)
            ddt_ref[...] = jnp.zeros_like(ddt_ref)

        halo = jnp.where(i % tiles_per_seq == 0, 0.0, halo_ref[...])
        c, sh = _conv_taps(u_ref[...], halo, w_ref[...])
        sg = _sigmoid(c)
        a = c * sg
        das = [None] * (3 * H)
        for h in range(H):
            xq = a[:, h * GDN_DIM:(h + 1) * GDN_DIM]
            xk = a[:, GDN_WIDTH + h * GDN_DIM:GDN_WIDTH + (h + 1) * GDN_DIM]
            das[h] = _l2n_bwd(dq_ref[h], xq, GDN_QSCALE)
            das[H + h] = _l2n_bwd(dk_ref[h], xk, 1.0)
            das[2 * H + h] = dv_ref[h]
        dc = jnp.concatenate(das, axis=-1) * (sg * (1.0 + c * (1.0 - sg)))
        dc_ref[...] = dc
        dcw_ref[...] += jnp.concatenate(
            [jnp.sum(dc * sh[CONV_W - 1 - t], axis=0, keepdims=True) for t in range(CONV_W)], axis=0)
        lane = lax.broadcasted_iota(jnp.int32, (tm, LANES), 1)
        ric = lax.broadcasted_iota(jnp.int32, (tm, LANES), 0) % CHUNK
        dG = jnp.zeros((tm, LANES), F32)
        for h in range(H):
            t = dgb_ref[h]
            dG = dG + jnp.where(lane == h, _pick_lane(t, lane, 0), 0.0) \
                    + jnp.where(lane == h + H, _pick_lane(t, lane, 1), 0.0)
        is_g = lane < H
        dg = jnp.where(is_g, _chunk_rev_cumsum(jnp.where(is_g, dG, 0.0), ric), 0.0)
        gab = gab_ref[...]
        g, beta = _gate_values(gab, alog_ref[...], dt_ref[...], lane)
        dga = jnp.where(is_g, dg * (-jnp.exp(alog_ref[...])) * _sigmoid(gab + dt_ref[...]), 0.0)
        dgb = jnp.where(is_g, 0.0, dG) * beta * (1.0 - beta)
        dgab_ref[...] = (dga + dgb).astype(MXU_DTYPE)
        dalog_ref[...] += jnp.sum(dg * g, axis=0, keepdims=True)
        ddt_ref[...] += jnp.sum(dga, axis=0, keepdims=True)

    hspec = pl.BlockSpec((H, tm, GDN_DIM), lambda i: (0, i, 0))
    vec = pl.BlockSpec((1, LANES), lambda i: (0, 0))
    return pl.pallas_call(
        body, grid=(T // tm,), name="gdn_pre_bwd",
        in_specs=[pl.BlockSpec((tm, C3), lambda i: (i, 0)),
                  pl.BlockSpec((SUBLANES, C3), lambda i: (jnp.maximum(i * (tm // SUBLANES) - 1, 0), 0)),
                  pl.BlockSpec((tm, LANES), lambda i: (i, P_GAB // LANES)),
                  pl.BlockSpec((CONV_W, C3), lambda i: (0, 0)), vec, vec, hspec, hspec, hspec, hspec],
        out_specs=[pl.BlockSpec((tm, C3), lambda i: (i, 0)), pl.BlockSpec((tm, LANES), lambda i: (i, 0)),
                   pl.BlockSpec((CONV_W, C3), lambda i: (0, 0)), vec, vec],
        out_shape=[SDS((T, C3), F32), SDS((T, LANES), MXU_DTYPE), SDS((CONV_W, C3), F32),
                   SDS((1, LANES), F32), SDS((1, LANES), F32)],
        compiler_params=_params(("arbitrary",)),
    )(proj, proj, proj, conv_w, alog_l, dt_l, dq4, dk4, dv4, dgb4)


def _mla_pre_bwd(proj, cosf, sinf, w_qln, w_kvln, w_uq_p, w_ukv, qnw, knw, dq4, dk4, dv4, transfer=None):
    T = proj.shape[0]
    tm = min(256, T)
    H = MLA_HEADS

    def body(ql_ref, kvl_ref, kpe_ref, cos_ref, sin_ref, wq_ref, wkv_ref, uq_ref, ukv_ref, qnw_ref, knw_ref,
             dq_ref, dk_ref, dv_ref,
             dql_ref, dkvl_ref, dkpe_ref, dqraw_ref, dkvraw_ref, qn_ref, kvn_ref, dwq_ref, dwkv_ref, dqnw_ref, dknw_ref):
        @pl.when(pl.program_id(0) == 0)
        def _():
            for r in (dwq_ref, dwkv_ref, dqnw_ref, dknw_ref):
                r[...] = jnp.zeros_like(r)

        cos, sin = cos_ref[...], sin_ref[...]
        qnw_, knw_ = qnw_ref[...], knw_ref[...]
        ql, kvl = ql_ref[...], kvl_ref[...]
        kpe_raw = kpe_ref[...][:, :ROPE]
        rms = functools.partial(_rms, on_mxu=True)
        rms_bwd = functools.partial(_rms_bwd, on_mxu=True)
        qn, rq = rms(ql, wq_ref[...])
        kvn, rkv = rms(kvl, wkv_ref[...])
        qn_ref[...] = qn.astype(MXU_DTYPE)
        kvn_ref[...] = kvn.astype(MXU_DTYPE)
        qraw = _mm(qn, uq_ref[...])
        kvraw = _mm(kvn, ukv_ref[...])
        dq_nope, dq_pe, dkv_parts = [], [], []
        dqnw_n = jnp.zeros((1, NOPE), F32)
        dqnw_p = jnp.zeros((1, ROPE), F32)
        dknw_n = jnp.zeros((1, NOPE), F32)
        dkpe = jnp.zeros((tm, ROPE), F32)
        for h in range(H):
            dq = dq_ref[h] * ATT_SCALE
            x = qraw[:, h * NOPE:(h + 1) * NOPE]
            dx, dw = rms_bwd(dq[:, :NOPE], x, qnw_[:, :NOPE], rms(x, qnw_[:, :NOPE])[1])
            dq_nope.append(dx)
            dqnw_n = dqnw_n + dw
            x = qraw[:, H * NOPE + h * ROPE:H * NOPE + (h + 1) * ROPE]
            dx, dw = rms_bwd(_rope_bwd(dq[:, NOPE:], cos, sin), x, qnw_[:, NOPE:], rms(x, qnw_[:, NOPE:])[1])
            dq_pe.append(dx)
            dqnw_p = dqnw_p + dw
            dk = dk_ref[h]
            x = kvraw[:, h * 256:h * 256 + NOPE]
            dx, dw = rms_bwd(dk[:, :NOPE], x, knw_[:, :NOPE], rms(x, knw_[:, :NOPE])[1])
            dknw_n = dknw_n + dw
            dkpe = dkpe + dk[:, NOPE:]
            dkv_parts += [dx, dv_ref[h]]
        dx, dknw_p = rms_bwd(_rope_bwd(dkpe, cos, sin), kpe_raw, knw_[:, NOPE:], rms(kpe_raw, knw_[:, NOPE:])[1])
        dkpe_ref[...] = jnp.concatenate([dx, jnp.zeros((tm, LANES - ROPE), F32)], axis=-1).astype(MXU_DTYPE)
        dqraw = jnp.concatenate(dq_nope + dq_pe, axis=-1).astype(MXU_DTYPE)
        dkvraw = jnp.concatenate(dkv_parts, axis=-1).astype(MXU_DTYPE)
        dqraw_ref[...] = dqraw
        dkvraw_ref[...] = dkvraw
        dx, dw = rms_bwd(_mm_nt(dqraw, uq_ref[...]), ql, wq_ref[...], rq)
        dql_ref[...] = dx.astype(MXU_DTYPE)
        dwq_ref[...] += dw
        dx, dw = rms_bwd(_mm_nt(dkvraw, ukv_ref[...]), kvl, wkv_ref[...], rkv)
        dkvl_ref[...] = dx.astype(MXU_DTYPE)
        dwkv_ref[...] += dw
        dqnw_ref[...] += jnp.concatenate([dqnw_n, dqnw_p], axis=-1)
        dknw_ref[...] += jnp.concatenate([dknw_n, dknw_p], axis=-1)

    full = lambda a: pl.BlockSpec(a.shape, lambda i: (0,) * a.ndim)
    rows = lambda n: pl.BlockSpec((tm, n), lambda i: (i, 0))
    const = lambda n: pl.BlockSpec((1, n), lambda i: (0, 0))
    NQ, NKV = w_uq_p.shape[1], w_ukv.shape[1]
    return _call_beside(
        body, transfer, grid=(T // tm,), name="mla_pre_bwd", scratch_shapes=[], semantics=("arbitrary",),
        args=(proj, proj, proj, cosf, sinf, w_qln, w_kvln, w_uq_p, w_ukv, qnw, knw, dq4, dk4, dv4),
        in_specs=[pl.BlockSpec((tm, 256), lambda i: (i, P_QLAT // 256)),
                  pl.BlockSpec((tm, 256), lambda i: (i, P_KVLAT // 256)),
                  pl.BlockSpec((tm, 128), lambda i: (i, P_KPE // 128)),
                  rows(ROPE), rows(ROPE),
                  full(w_qln), full(w_kvln), full(w_uq_p), full(w_ukv), full(qnw), full(knw),
                  pl.BlockSpec((H, tm, QK_DIM), lambda i: (0, i, 0)),
                  pl.BlockSpec((H, tm, QK_DIM), lambda i: (0, i, 0)),
                  pl.BlockSpec((H, tm, V_DIM), lambda i: (0, i, 0))],
        out_specs=[rows(Q_LORA), rows(KV_LORA), rows(LANES), rows(NQ), rows(NKV), rows(Q_LORA), rows(KV_LORA),
                   const(Q_LORA), const(KV_LORA), const(QK_DIM), const(QK_DIM)],
        out_shape=[SDS((T, Q_LORA), MXU_DTYPE), SDS((T, KV_LORA), MXU_DTYPE), SDS((T, LANES), MXU_DTYPE),
                   SDS((T, NQ), MXU_DTYPE), SDS((T, NKV), MXU_DTYPE),
                   SDS((T, Q_LORA), MXU_DTYPE), SDS((T, KV_LORA), MXU_DTYPE),
                   SDS((1, Q_LORA), F32), SDS((1, KV_LORA), F32), SDS((1, QK_DIM), F32), SDS((1, QK_DIM), F32)])


def _in_proj_bwd(dc, conv_w, dgz, dql, dkvl, dkpe, dgab, w_in_p, dh, x2, w_an, S):
    T, D = x2.shape
    N = w_in_p.shape[1]
    C3 = dc.shape[1]
    tm = min(512, S)
    assert S % tm == 0 and T % tm == 0, "a token tile must not straddle two sequences"
    tiles_per_seq = S // tm
    nblk = T // SUBLANES

    def body(dc_ref, nxt_ref, cw_ref, b_ref, c_ref, d_ref, e_ref, f_ref, w_ref, dh_ref, x_ref, wn_ref,
             dx_ref, dp_ref, dwn_ref):
        i = pl.program_id(0)

        @pl.when(i == 0)
        def _():
            dwn_ref[...] = jnp.zeros_like(dwn_ref)

        nxt = jnp.where(i % tiles_per_seq == tiles_per_seq - 1, 0.0, nxt_ref[...])
        dcv, cw = dc_ref[...], cw_ref[...]
        du = cw[3:4] * dcv
        for j in range(1, CONV_W):
            du = du + cw[3 - j:4 - j] * _shift_up(dcv, nxt, j)
        dp = jnp.concatenate([du.astype(MXU_DTYPE), b_ref[...], c_ref[...], d_ref[...], e_ref[...], f_ref[...]],
                             axis=-1).astype(MXU_DTYPE)
        dp_ref[...] = dp
        x = x_ref[...]
        _, r = _rms(x, wn_ref[...])
        dx, dw = _rms_bwd(_mm_nt(dp, w_ref[...]), x, wn_ref[...], r)
        dx_ref[...] = dh_ref[...] + dx
        dwn_ref[...] += dw

    rows = lambda n: pl.BlockSpec((tm, n), lambda i: (i, 0))
    return pl.pallas_call(
        body, grid=(T // tm,), name="in_proj_bwd",
        in_specs=[rows(C3),
                  pl.BlockSpec((SUBLANES, C3), lambda i: (jnp.minimum((i + 1) * (tm // SUBLANES), nblk - 1), 0)),
                  pl.BlockSpec((CONV_W, C3), lambda i: (0, 0)),
                  rows(dgz.shape[1]), rows(dql.shape[1]), rows(dkvl.shape[1]),
                  rows(dkpe.shape[1]), rows(dgab.shape[1]),
                  pl.BlockSpec((D, N), lambda i: (0, 0)), rows(D), rows(D), pl.BlockSpec((1, D), lambda i: (0, 0))],
        out_specs=[rows(D), rows(N), pl.BlockSpec((1, D), lambda i: (0, 0))],
        out_shape=[SDS((T, D), F32), SDS((T, N), MXU_DTYPE), SDS((1, D), F32)],
        compiler_params=_params(("arbitrary",)),
    )(dc, dc, conv_w, dgz, dql, dkvl, dkpe, dgab, w_in_p, dh, x2, w_an)


def _wgrad(a, b, name, column_shards=False):
    T, M = a.shape
    N = b.shape[1]
    tM = _divisor_tile(M, 1024)
    tN = N // N_DEV if column_shards else _divisor_tile(N, 1536)
    tk = min(T, 2048)
    nk = T // tk

    def body(a_ref, b_ref, o_ref, acc):
        k = pl.program_id(2)

        @pl.when(k == 0)
        def _():
            acc[...] = jnp.zeros_like(acc)

        acc[...] += _mm_tn(a_ref[...], b_ref[...])

        @pl.when(k == nk - 1)
        def _():
            o_ref[...] = acc[...].astype(WIRE_DTYPE).reshape(o_ref.shape)

    if column_shards:
        out_spec, out_shape = pl.BlockSpec((1, tM, tN), lambda i, j, k: (j, i, 0)), SDS((N_DEV, M, tN), WIRE_DTYPE)
    else:
        out_spec, out_shape = pl.BlockSpec((tM, tN), lambda i, j, k: (i, j)), SDS((M, N), WIRE_DTYPE)
    return pl.pallas_call(
        body, grid=(M // tM, N // tN, nk), name=name,
        in_specs=[pl.BlockSpec((tk, tM), lambda i, j, k: (k, i)), pl.BlockSpec((tk, tN), lambda i, j, k: (k, j))],
        out_specs=out_spec, out_shape=out_shape,
        scratch_shapes=[pltpu.VMEM((tM, tN), F32)],
        compiler_params=_params(("arbitrary", "arbitrary", "arbitrary")),
    )(a, b)


def _adamw(g, w, m, v):
    m = ADAM_B1 * m + (1.0 - ADAM_B1) * g
    v = ADAM_B2 * v + (1.0 - ADAM_B2) * jnp.square(g)
    m_hat = m / (1.0 - ADAM_B1 ** ADAM_STEP)
    v_hat = v / (1.0 - ADAM_B2 ** ADAM_STEP)
    return -ADAM_LR * (m_hat / (jnp.sqrt(v_hat) + ADAM_EPS) + ADAM_WD * w), m, v


def _reduce_adamw(parts, w, m, v, name):
    R, C = w.shape
    _, Rp, Cp = parts.shape
    tr = min(R, 256)
    tp = tr if Rp == R else Rp

    def body(p_ref, w_ref, m_ref, v_ref, g_ref, d_ref, nm_ref, nv_ref):
        g = p_ref[0].astype(F32)
        for s in range(1, N_DEV):
            g = g + p_ref[s].astype(F32)
        g = g[:tr, :C]
        g_ref[...] = g
        d_ref[...], nm_ref[...], nv_ref[...] = _adamw(g, w_ref[...], m_ref[...], v_ref[...])

    spec = pl.BlockSpec((tr, C), lambda i: (i, 0))
    return pl.pallas_call(
        body, grid=(R // tr,), name=name,
        in_specs=[pl.BlockSpec((N_DEV, tp, Cp), lambda i: (0, i, 0)), spec, spec, spec],
        out_specs=[spec] * 4, out_shape=[SDS((R, C), F32)] * 4,
        compiler_params=_params(("arbitrary",)),
    )(parts, w, m, v)


SMALL_ROWS, SMALL_COLS = 16, 1024
SMALL_LAYOUT = (
    ("attn_norm_w", 0, 1, 1024, 1024), ("mlp_norm_w", 1, 1, 1024, 1024), ("q_lat_norm_w", 2, 1, 256, 256),
    ("kv_lat_norm_w", 3, 1, 256, 256), ("q_norm_w", 4, 1, 192, 192), ("k_norm_w", 5, 1, 192, 192),
    ("mla_out_norm_w", 6, 4, 128, 128), ("a_log", 10, 1, 128, 4), ("dt_bias", 11, 1, 128, 4),
    ("gdn_norm_w", 12, 1, 128, 128))
LOSS_ENTRY = ("loss", 13, 1, 128, 128)


def _adamw_replicated(parts, ws, ms, vs):
    n = len(SMALL_LAYOUT)

    def body(*refs):
        p_ref = refs[0]
        w_refs, m_refs, v_refs = refs[1:1 + n], refs[1 + n:1 + 2 * n], refs[1 + 2 * n:1 + 3 * n]
        outs = refs[1 + 3 * n:]
        s = p_ref[0]
        for d in range(1, N_DEV):
            s = s + p_ref[d]
        for i, (_, r0, nr, _, pw) in enumerate(SMALL_LAYOUT):
            g = s[r0:r0 + nr, :pw]
            outs[i][...] = g
            outs[n + i][...], outs[2 * n + i][...], outs[3 * n + i][...] = _adamw(
                g, w_refs[i][...], m_refs[i][...], v_refs[i][...])
        _, r0, nr, gw, _ = LOSS_ENTRY
        outs[4 * n][...] = s[r0:r0 + nr, :gw]

    res = pl.pallas_call(
        body, name="adamw_replicated",
        out_shape=[SDS(w.shape, F32) for w in ws] * 4 + [SDS((1, LANES), F32)],
        compiler_params=_params(),
    )(parts, *ws, *ms, *vs)
    return [res[k * n:(k + 1) * n] for k in range(4)], res[4 * n][0, 0]


COPIES_PER_ARRAY = N_DEV - 1


def _two_level_gather(srcs, outs, send_sems, recv_sems, local_sems=None, stage="all"):
    mx, my, mc = lax.axis_index("x"), lax.axis_index("y"), lax.axis_index("c")
    me, sibling = (mx, my, mc), (mx, my, 1 - mc)
    chips = [(1 - mx, my), (mx, 1 - my), (1 - mx, 1 - my)]
    arrays = range(len(srcs))

    def copy(a, k, block, to, src=None):
        px, py, pc = block
        slot = outs[a].at[4 * px + 2 * py + pc]
        sem = a * COPIES_PER_ARRAY + k
        return pltpu.make_async_remote_copy(
            src_ref=slot if src is None else src, dst_ref=slot,
            send_sem=send_sems.at[sem], recv_sem=recv_sems.at[sem], device_id=to, device_id_type=MESH_ID)

    mine = [] if local_sems is None else [
        pltpu.make_async_copy(srcs[a], outs[a].at[4 * mx + 2 * my + mc], local_sems.at[a]) for a in arrays]
    first = []
    for a in arrays:
        first.append(copy(a, 0, me, sibling, src=srcs[a]))
        first += [copy(a, 1 + j, me, (*chip, mc), src=srcs[a]) for j, chip in enumerate(chips)]
    forwards = [copy(a, 4 + j, (*chip, mc), sibling) for j, chip in enumerate(chips) for a in arrays]
    if stage in ("all", "start"):
        for cp in mine + first:
            cp.start()
    if stage in ("all", "forward"):
        for j, chip in enumerate(chips):
            for a in arrays:
                copy(a, 1 + j, (*chip, mc), me).wait_recv()
                forwards[j * len(srcs) + a].start()
    if stage in ("all", "finish"):
        for a in arrays:
            copy(a, 0, sibling, me).wait_recv()
        for j, chip in enumerate(chips):
            for a in arrays:
                copy(a, 4 + j, (*chip, 1 - mc), me).wait_recv()
        for cp in first + forwards:
            cp.wait_send()
        for cp in mine:
            cp.wait()


def _comm_scratch(n):
    return [pltpu.SemaphoreType.DMA((n * COPIES_PER_ARRAY,)), pltpu.SemaphoreType.DMA((n * COPIES_PER_ARRAY,)),
            pltpu.SemaphoreType.DMA((n,))]


def _any_specs(n):
    return [pl.BlockSpec(memory_space=pl.ANY)] * n


def _gather_weights(shards):
    n = len(shards)

    def body(*refs):
        _two_level_gather(refs[:n], refs[n:2 * n], *refs[2 * n:])

    return pl.pallas_call(
        body, name="gather_weights",
        out_shape=[SDS((N_DEV,) + s.shape, s.dtype) for s in shards],
        in_specs=_any_specs(n), out_specs=_any_specs(n), scratch_shapes=_comm_scratch(n),
    )(*shards)


def _gather_small_grads(gs, loss_lanes):
    gs = list(gs) + [loss_lanes]
    n = len(gs)

    def body(*refs):
        g_refs, out_ref = refs[:n], refs[n]
        tile, send_sems, recv_sems = refs[n + 1:]
        tile[...] = jnp.zeros_like(tile)
        for (_, r0, nr, gw, _), g in zip(SMALL_LAYOUT + (LOSS_ENTRY,), g_refs):
            tile[r0:r0 + nr, 0:gw] = g[...]
        me = 4 * lax.axis_index("x") + 2 * lax.axis_index("y") + lax.axis_index("c")
        out_ref[me] = tile[...]
        _two_level_gather([tile], [out_ref], send_sems, recv_sems)

    return pl.pallas_call(
        body, name="gather_small_grads",
        out_shape=SDS((N_DEV, SMALL_ROWS, SMALL_COLS), F32),
        in_specs=[pl.BlockSpec(memory_space=pltpu.VMEM)] * n,
        out_specs=pl.BlockSpec(memory_space=pltpu.VMEM),
        scratch_shapes=[pltpu.VMEM((SMALL_ROWS, SMALL_COLS), F32),
                        pltpu.SemaphoreType.DMA((COPIES_PER_ARRAY,)), pltpu.SemaphoreType.DMA((COPIES_PER_ARRAY,))],
    )(*gs)


def _exchange_grads(slabs):
    n = len(slabs)

    def body(*refs):
        _exchange(refs[:n], refs[n:2 * n], *refs[2 * n:])

    return pl.pallas_call(
        body, name="exchange_grads",
        out_shape=[SDS(s.shape, s.dtype) for s in slabs],
        in_specs=_any_specs(n), out_specs=_any_specs(n), scratch_shapes=_comm_scratch(n),
    )(*slabs)


class _Transfer:
    def __init__(self, kind, arrays):
        self.kind, self.arrays, self.n = kind, list(arrays), len(arrays)

    def out_shapes(self):
        if self.kind == "gather":
            return [SDS((N_DEV,) + a.shape, a.dtype) for a in self.arrays]
        return [SDS(a.shape, a.dtype) for a in self.arrays]

    def run(self, srcs, outs, sems, stage):
        fn = _two_level_gather if self.kind == "gather" else _exchange
        fn(srcs, outs, *sems, stage=stage)


def _call_beside(body, transfer, *, grid, in_specs, out_specs, out_shape, scratch_shapes, name, semantics, args):
    if transfer is None:
        res = pl.pallas_call(body, grid=grid, in_specs=in_specs, out_specs=out_specs, out_shape=out_shape,
                             scratch_shapes=scratch_shapes, name=name, compiler_params=_params(semantics))(*args)
        return list(res), []
    n_in, n_out, n_s, n = len(in_specs), len(out_specs), len(scratch_shapes), transfer.n
    total = functools.reduce(lambda a, b: a * b, grid, 1)

    def wrapped(*refs):
        ins, refs = refs[:n_in], refs[n_in:]
        t_in, refs = refs[:n], refs[n:]
        outs, refs = refs[:n_out], refs[n_out:]
        t_out, refs = refs[:n], refs[n:]
        scratch, sems = refs[:n_s], refs[n_s:]
        first = functools.reduce(jnp.logical_and, [pl.program_id(i) == 0 for i in range(len(grid))])
        last = functools.reduce(jnp.logical_and, [pl.program_id(i) == g - 1 for i, g in enumerate(grid)])

        @pl.when(first)
        def _():
            transfer.run(t_in, t_out, sems, "start")

        step = functools.reduce(lambda acc, ig: acc * ig[1] + pl.program_id(ig[0]), enumerate(grid), 0)

        @pl.when(step == (3 * total) // 4)
        def _():
            transfer.run(t_in, t_out, sems, "forward")

        body(*ins, *outs, *scratch)

        @pl.when(last)
        def _():
            transfer.run(t_in, t_out, sems, "finish")

    res = pl.pallas_call(
        wrapped, grid=grid, in_specs=list(in_specs) + _any_specs(n), out_specs=list(out_specs) + _any_specs(n),
        out_shape=list(out_shape) + transfer.out_shapes(), scratch_shapes=list(scratch_shapes) + _comm_scratch(n),
        name=name, compiler_params=_params(semantics))(*args, *transfer.arrays)
    return list(res[:n_out]), list(res[n_out:])


EXCHANGE_FLIPS = ((0, 0, 1), (1, 0, 0), (0, 1, 0), (1, 1, 0), (1, 0, 1), (0, 1, 1), (1, 1, 1))


def _exchange(srcs, outs, send_sems, recv_sems, local_sems, stage="all"):
    mx, my, mc = lax.axis_index("x"), lax.axis_index("y"), lax.axis_index("c")
    arrays = range(len(srcs))
    copies = [pltpu.make_async_copy(srcs[a].at[4 * mx + 2 * my + mc], outs[a].at[N_DEV - 1], local_sems.at[a])
              for a in arrays]
    for k, (fx, fy, fc) in enumerate(EXCHANGE_FLIPS):
        px = 1 - mx if fx else mx
        py = 1 - my if fy else my
        pc = 1 - mc if fc else mc
        for a in arrays:
            sem = a * COPIES_PER_ARRAY + k
            copies.append(pltpu.make_async_remote_copy(
                src_ref=srcs[a].at[4 * px + 2 * py + pc], dst_ref=outs[a].at[k],
                send_sem=send_sems.at[sem], recv_sem=recv_sems.at[sem],
                device_id=(px, py, pc), device_id_type=MESH_ID))
    if stage in ("all", "start"):
        for cp in copies:
            cp.start()
    if stage in ("all", "finish"):
        for cp in copies:
            cp.wait()


def _w_in_to_padded(w):
    z = lambda n: jnp.zeros((w.shape[0], n), w.dtype)
    return jnp.concatenate([w[:, O_GQKV:O_GZ], w[:, O_GZ:O_GAB], w[:, O_QLAT:O_KVLAT], w[:, O_KVLAT:O_KPE],
                            w[:, O_KPE:O_GQKV], z(P_GAB - P_KPE - ROPE), w[:, O_GAB:O_END],
                            z(P_WIDTH - P_GAB - (O_END - O_GAB))], axis=1)


def _w_in_from_padded(wp):
    return jnp.concatenate([wp[:, P_QLAT:P_QLAT + 256], wp[:, P_KVLAT:P_KVLAT + 256], wp[:, P_KPE:P_KPE + ROPE],
                            wp[:, P_GQKV:P_GZ], wp[:, P_GZ:P_QLAT], wp[:, P_GAB:P_GAB + (O_END - O_GAB)]], axis=1)


W_IN_SHARD_COLS = (O_END - O_QLAT) // N_DEV


def _w_in_shards_to_padded(stack):
    _, R, Cw = stack.shape
    tr = min(R, 256)

    def body(s_ref, o_ref):
        full = jnp.concatenate([s_ref[d].astype(F32)[:, :W_IN_SHARD_COLS] for d in range(N_DEV)], axis=-1)
        o_ref[...] = _w_in_to_padded(full).astype(o_ref.dtype)

    return pl.pallas_call(
        body, grid=(R // tr,), name="w_in_to_padded",
        in_specs=[pl.BlockSpec((N_DEV, tr, Cw), lambda i: (0, i, 0))],
        out_specs=pl.BlockSpec((tr, P_WIDTH), lambda i: (i, 0)),
        out_shape=SDS((R, P_WIDTH), stack.dtype), compiler_params=_params(("arbitrary",)),
    )(stack)


def _w_in_padded_to_slabs(gp, wire_cols):
    R = gp.shape[0]
    tr = min(R, 256)

    def body(g_ref, o_ref):
        orig = _w_in_from_padded(g_ref[...].astype(F32))
        for d in range(N_DEV):
            piece = orig[:, d * W_IN_SHARD_COLS:(d + 1) * W_IN_SHARD_COLS]
            o_ref[d] = _pad2(piece, tr, wire_cols).astype(o_ref.dtype)

    return pl.pallas_call(
        body, grid=(R // tr,), name="w_in_to_slabs",
        in_specs=[pl.BlockSpec((tr, P_WIDTH), lambda i: (i, 0))],
        out_specs=pl.BlockSpec((N_DEV, tr, wire_cols), lambda i: (0, i, 0)),
        out_shape=SDS((N_DEV, R, wire_cols), gp.dtype), compiler_params=_params(("arbitrary",)),
    )(gp)


def _w_uq_to_headsplit(w):
    w3 = w.reshape(w.shape[0], MLA_HEADS, QK_DIM)
    return jnp.concatenate([w3[:, :, :NOPE].reshape(w.shape[0], -1), w3[:, :, NOPE:].reshape(w.shape[0], -1)], axis=1)


def _w_uq_from_headsplit(wp):
    n = wp[:, :MLA_HEADS * NOPE].reshape(wp.shape[0], MLA_HEADS, NOPE)
    p = wp[:, MLA_HEADS * NOPE:].reshape(wp.shape[0], MLA_HEADS, ROPE)
    return jnp.concatenate([n, p], axis=2).reshape(wp.shape[0], -1)


def _lane_vec(v4):
    return jnp.pad(v4.reshape(1, -1), ((0, 0), (0, LANES - v4.shape[-1])))


def _local_step(x, positions, target, attn_norm_w, w_in, q_lat_norm_w, w_uq, kv_lat_norm_w, w_ukv, q_norm_w,
                k_norm_w, mla_out_norm_w, conv_w, a_log, dt_bias, gdn_norm_w, w_out, mlp_norm_w, w_up, w_down,
                late_shards=None, exchange=False):
    B, S, D = x.shape
    T = B * S
    x2 = x.reshape(T, D)
    t2 = target.reshape(T, D)
    half = ROPE // 2
    inv_freq = ROPE_THETA ** (-jnp.arange(half, dtype=F32) / half)
    ang = positions.reshape(T, 1).astype(F32) * inv_freq
    cosf = jnp.concatenate([jnp.cos(ang)] * 2, axis=-1)
    sinf = jnp.concatenate([jnp.sin(ang)] * 2, axis=-1)
    w_in_p = w_in
    w_uq_p = _w_uq_to_headsplit(w_uq)
    alog_l, dt_l = _lane_vec(a_log), _lane_vec(dt_bias)
    w_an, w_qln, w_kvln, qnw, knw, w_mn, gdn_w = (
        attn_norm_w, q_lat_norm_w, kv_lat_norm_w, q_norm_w, k_norm_w, mlp_norm_w, gdn_norm_w)

    proj, xn, qg, kg, vg, gates = _in_proj(x2, w_an, w_in_p, conv_w, alog_l, dt_l, S)
    def gathering(shards):
        return None if late_shards is None else _Transfer("gather", shards)

    (q4, k4, v4), late = _mla_pre(proj, cosf, sinf, w_qln, w_kvln, w_uq_p, w_ukv, qnw, knw,
                                  gathering(late_shards and late_shards[:1]))
    if late:
        w_out = late[0].reshape(-1, D)
    (o_mla, lse), late = _attn_fwd(q4, k4, v4, B, S, gathering(late_shards and late_shards[2:]))
    if late:
        w_down = late[0].reshape(-1, D)
    (o_gdn, states, ainv, u4, w4), late = _gdn_fwd(qg, kg, vg, gates, B, S,
                                                   gathering(late_shards and late_shards[1:2]))
    if late:
        w_up = late[0]
    h2, mix = _mix_out(o_mla, o_gdn, proj, x2, mla_out_norm_w, gdn_w, w_out)
    up, hn, dy, sq = _mlp_fwd(h2, w_mn, w_up, w_down, t2)
    loss = (0.5 / D) * jnp.sum(sq[:, 0, 0])

    dh, dhb, dup, act, dyb, d_mlp_norm = _mlp_bwd(dy, up, h2, w_mn, w_up, w_down)
    g_w_down = _wgrad(act, dyb, "wgrad_down")
    g_w_up = _wgrad(hn, dup, "wgrad_up", column_shards=True)
    do_mla, do_gdn, dz, d_mla_w, d_gdn_w, delta = _mix_bwd(dhb, o_mla, o_gdn, proj, mla_out_norm_w, gdn_w, w_out)
    g_w_out = _wgrad(mix, dhb, "wgrad_out")
    first = ("w_down",)
    second = ("w_out",)
    third = ("w_up", "w_uq", "w_ukv")
    mats = dict(w_up=g_w_up, w_down=g_w_down, w_out=g_w_out)

    def sending(names):
        return _Transfer("exchange", [_slabs(n, mats[n]) for n in names]) if exchange else None

    (dq4, dk4, dv4), got = _attn_bwd(q4, k4, v4, do_mla, delta, lse, B, S, sending(first))
    mats.update(zip(first, got))
    (dql, dkvl, dkpe, dqraw, dkvraw, qn, kvn, d_wqln, d_wkvln, d_qnw, d_knw), got = _mla_pre_bwd(
        proj, cosf, sinf, w_qln, w_kvln, w_uq_p, w_ukv, qnw, knw, dq4, dk4, dv4, sending(second))
    mats.update(zip(second, got))
    mats.update(w_uq=_wgrad(qn, dqraw, "wgrad_uq"), w_ukv=_wgrad(kvn, dkvraw, "wgrad_ukv"))
    (dqg, dkg, dvg, dgb4), got = _gdn_bwd(qg, kg, vg, gates, states, ainv, u4, w4, do_gdn, B, S, sending(third))
    mats.update(zip(third, got))
    dc, dgab, g_conv, d_alog, d_dt = _gdn_pre_bwd(proj, conv_w, alog_l, dt_l, dqg, dkg, dvg, dgb4, S)
    grad_x2, dproj, d_attn_norm = _in_proj_bwd(dc, conv_w, dz, dql, dkvl, dkpe, dgab, w_in_p, dh, x2, w_an, S)
    mats.update(w_in=_wgrad(xn, dproj, "wgrad_in"), conv_w=g_conv)
    if exchange:
        last = ("w_in", "conv_w")
        mats.update(zip(last, _exchange_grads([_slabs(n, mats[n]) for n in last])))
    small = dict(attn_norm_w=d_attn_norm, mlp_norm_w=d_mlp_norm, q_lat_norm_w=d_wqln, kv_lat_norm_w=d_wkvln,
                 q_norm_w=d_qnw, k_norm_w=d_knw, mla_out_norm_w=d_mla_w, a_log=d_alog, dt_bias=d_dt,
                 gdn_norm_w=d_gdn_w)
    return loss, grad_x2.reshape(B, S, D), mats, [small[n] for n, *_ in SMALL_LAYOUT]


BIG = ("w_in", "w_uq", "w_ukv", "conv_w", "w_out", "w_up", "w_down")
ALL_W = ("attn_norm_w", "w_in", "q_lat_norm_w", "w_uq", "kv_lat_norm_w", "w_ukv", "q_norm_w", "k_norm_w",
         "mla_out_norm_w", "conv_w", "a_log", "dt_bias", "gdn_norm_w", "w_out", "mlp_norm_w", "w_up", "w_down")
WIRE_SHAPE = {"w_in": (1024, 384), "w_uq": (256, 128), "conv_w": (16, 256)}


def _pad2(a, rows, cols):
    return jnp.pad(a, [(0, 0)] * (a.ndim - 2) + [(0, rows - a.shape[-2]), (0, cols - a.shape[-1])])


def _cols_to_full(stack, cols):
    return jnp.moveaxis(stack[:, :, :cols], 0, 1).reshape(stack.shape[1], N_DEV * cols)


def _full_to_cols(full, wire_cols):
    r, n = full.shape
    return _pad2(jnp.moveaxis(full.reshape(r, N_DEV, n // N_DEV), 1, 0), r, wire_cols)


def _slabs(name, g):
    if name == "w_in":
        return _w_in_padded_to_slabs(g, WIRE_SHAPE["w_in"][1])
    if name == "w_uq":
        return _full_to_cols(_w_uq_from_headsplit(g), WIRE_SHAPE["w_uq"][1])
    if name == "w_ukv":
        return _full_to_cols(g, g.shape[1] // N_DEV)
    if name == "conv_w":
        return _pad2(_full_to_cols(g.astype(WIRE_DTYPE), g.shape[1] // N_DEV), *WIRE_SHAPE["conv_w"])
    if name == "w_up":
        return g
    return g.reshape(N_DEV, -1, g.shape[-1])


def kernel(x, positions, attn_norm_w, w_in, q_lat_norm_w, w_uq, kv_lat_norm_w, w_ukv, q_norm_w, k_norm_w, mla_out_norm_w, conv_w, a_log, dt_bias, gdn_norm_w, w_out, mlp_norm_w, w_up, w_down, loss_target, m_attn_norm_w, m_w_in, m_q_lat_norm_w, m_w_uq, m_kv_lat_norm_w, m_w_ukv, m_q_norm_w, m_k_norm_w, m_mla_out_norm_w, m_conv_w, m_a_log, m_dt_bias, m_gdn_norm_w, m_w_out, m_mlp_norm_w, m_w_up, m_w_down, v_attn_norm_w, v_w_in, v_q_lat_norm_w, v_w_uq, v_kv_lat_norm_w, v_w_ukv, v_q_norm_w, v_k_norm_w, v_mla_out_norm_w, v_conv_w, v_a_log, v_dt_bias, v_gdn_norm_w, v_w_out, v_mlp_norm_w, v_w_up, v_w_down):
    env = dict(locals())
    W = {n: env[n][0] for n in ALL_W}
    Mo = {n: env["m_" + n][0] for n in ALL_W}
    Vo = {n: env["v_" + n][0] for n in ALL_W}

    two_d = lambda a: a.reshape(1, -1) if a.ndim == 1 else a
    D = x.shape[-1]

    s_in, s_uq, s_ukv, s_conv = _gather_weights([
        _pad2(W["w_in"].astype(WIRE_DTYPE), *WIRE_SHAPE["w_in"]),
        _pad2(W["w_uq"].astype(WIRE_DTYPE), *WIRE_SHAPE["w_uq"]),
        W["w_ukv"].astype(WIRE_DTYPE), _pad2(W["conv_w"], *WIRE_SHAPE["conv_w"])])
    late = [W["w_out"].astype(WIRE_DTYPE), W["w_up"].astype(WIRE_DTYPE), W["w_down"].astype(WIRE_DTYPE)]

    loss, grad_x, parts, gs = _local_step(
        x, positions, loss_target, two_d(W["attn_norm_w"]), _w_in_shards_to_padded(s_in),
        two_d(W["q_lat_norm_w"]), _cols_to_full(s_uq, W["w_uq"].shape[1]), two_d(W["kv_lat_norm_w"]),
        _cols_to_full(s_ukv, W["w_ukv"].shape[1]), two_d(W["q_norm_w"]), two_d(W["k_norm_w"]),
        W["mla_out_norm_w"], _cols_to_full(s_conv[:, :CONV_W], W["conv_w"].shape[1]), two_d(W["a_log"]),
        two_d(W["dt_bias"]), two_d(W["gdn_norm_w"]), None, two_d(W["mlp_norm_w"]), None, None,
        late_shards=late, exchange=True)
    done = {n: _reduce_adamw(parts[n], W[n], Mo[n], Vo[n], "adamw_" + n) for n in BIG}
    names = [n for n, *_ in SMALL_LAYOUT]
    tiles = _gather_small_grads(gs, jnp.full((1, LANES), loss, F32))
    small, loss = _adamw_replicated(tiles, [two_d(W[n]) for n in names], [two_d(Mo[n]) for n in names],
                                    [two_d(Vo[n]) for n in names])
    for i, n in enumerate(names):
        done[n] = [small[kind][i] for kind in range(4)]
    res = [done[n][kind].reshape(env[n].shape) for kind in range(4) for n in ALL_W]
    return (loss, grad_x, *res)
```

```python
import functools

import jax
import jax.numpy as jnp
from jax import lax
from jax.experimental import pallas as pl
from jax.experimental.pallas import tpu as pltpu

F32 = jnp.float32
MXU_DTYPE = jnp.bfloat16
WIRE_DTYPE = jnp.bfloat16
SDS = jax.ShapeDtypeStruct
HIGHEST = lax.Precision.HIGHEST
MESH_ID = pl.DeviceIdType.MESH

D_MODEL = 1024
MLA_HEADS = 4
Q_LORA = 256
KV_LORA = 256
NOPE = 128
ROPE = 64
QK_DIM = NOPE + ROPE
V_DIM = 128
ROPE_THETA = 10000.0
GDN_HEADS = 4
GDN_DIM = 128
GDN_WIDTH = GDN_HEADS * GDN_DIM
CONV_W = 4
CHUNK = 64
D_FF = 4 * D_MODEL
EPS = 1e-6
ATT_SCALE = QK_DIM ** -0.5
GDN_QSCALE = GDN_DIM ** -0.5
N_DEV = 8
ATTN_BLOCK = 512
ATTN_CHAINS = 2
MLP_FWD_SHARDS = 4
MLP_BWD_SHARDS = 4

ADAM_LR = 0.001
ADAM_B1 = 0.9
ADAM_B2 = 0.999
ADAM_EPS = 1e-08
ADAM_WD = 0.01
ADAM_STEP = 10

LANES = 128
SUBLANES = 8
VMEM_LIMIT = 60 * 1024 * 1024

P_GQKV, P_GZ, P_QLAT, P_KVLAT, P_KPE, P_GAB = 0, 1536, 2048, 2304, 2560, 2688
P_WIDTH = 2816
O_QLAT, O_KVLAT, O_KPE, O_GQKV, O_GZ, O_GAB, O_END = 0, 256, 512, 576, 2112, 2624, 2632


def _params(sem=None, vmem=VMEM_LIMIT):
    kw = dict(vmem_limit_bytes=vmem)
    if sem is not None:
        kw["dimension_semantics"] = sem
    return pltpu.CompilerParams(**kw)


def _mm(a, b):
    return jnp.dot(a.astype(MXU_DTYPE), b.astype(MXU_DTYPE), preferred_element_type=F32)


def _mm_nt(a, b):
    return lax.dot_general(a.astype(MXU_DTYPE), b.astype(MXU_DTYPE), (((1,), (1,)), ((), ())),
                           preferred_element_type=F32)


def _mm_tn(a, b):
    return lax.dot_general(a.astype(MXU_DTYPE), b.astype(MXU_DTYPE), (((0,), (0,)), ((), ())),
                           preferred_element_type=F32)


def _split(a):
    hi = a.astype(MXU_DTYPE)
    return hi, (a - hi.astype(F32)).astype(MXU_DTYPE)


def _mm_split(a, b):
    (ah, al), (bh, bl) = a, b
    dot = lambda x, y: jnp.dot(x, y, preferred_element_type=F32)
    if MXU_DTYPE == F32:
        return dot(ah, bh)
    return dot(ah, bh) + dot(ah, bl) + dot(al, bh)


def _mm_exact(a, b):
    return _mm_split(_split(a), _split(b))


def _row_sum(v, on_mxu=False):
    if not on_mxu:
        return jnp.sum(v, axis=-1, keepdims=True)
    d = v.shape[-1]
    ones = jnp.ones((d, LANES), MXU_DTYPE)
    s = sum(jnp.dot(p, ones, preferred_element_type=F32) for p in _split(v))
    return s[:, :d] if d <= LANES else jnp.tile(s, (1, d // LANES))


def _rms(x, w, on_mxu=False):
    r = lax.rsqrt(_row_sum(x * x, on_mxu) * (1.0 / x.shape[-1]) + EPS)
    return x * r * w, r


def _rms_bwd(dy, x, w, r, on_mxu=False):
    xh = x * r
    dyw = dy * w
    dx = r * (dyw - xh * (_row_sum(dyw * xh, on_mxu) * (1.0 / x.shape[-1])))
    dw = jnp.sum(dy * xh, axis=0, keepdims=True)
    return dx, dw


def _l2n(x, scale):
    return x * (lax.rsqrt(_row_sum(x * x) + EPS) * scale)


def _l2n_bwd(dy, x, scale):
    r = lax.rsqrt(_row_sum(x * x) + EPS)
    xh = x * r
    return (scale * r) * (dy - xh * _row_sum(dy * xh))


def _rot(t):
    return jnp.concatenate([-t[:, ROPE // 2:], t[:, :ROPE // 2]], axis=-1)


def _rot_t(t):
    return jnp.concatenate([t[:, ROPE // 2:], -t[:, :ROPE // 2]], axis=-1)


def _rope(t, cos, sin):
    return t * cos + _rot(t) * sin


def _rope_bwd(d, cos, sin):
    return d * cos + _rot_t(d * sin)


def _sigmoid(x):
    return jax.nn.sigmoid(x)


def _shift_down(x, halo, j):
    if j == 0:
        return x
    xr = pltpu.roll(x, j, 0)
    hr = pltpu.roll(halo, j, 0)
    row = lax.broadcasted_iota(jnp.int32, halo.shape, 0)
    top = jnp.where(row < j, hr, xr[:SUBLANES])
    return jnp.concatenate([top, xr[SUBLANES:]], axis=0)


def _shift_up(x, nxt, j):
    if j == 0:
        return x
    n = x.shape[0]
    xr = pltpu.roll(x, n - j, 0)
    nr = pltpu.roll(nxt, SUBLANES - j, 0)
    row = lax.broadcasted_iota(jnp.int32, nxt.shape, 0)
    bot = jnp.where(row >= SUBLANES - j, nr, xr[n - SUBLANES:])
    return jnp.concatenate([xr[:n - SUBLANES], bot], axis=0)


def _chunk_cumsum(y, row_in_chunk):
    s = 1
    while s < CHUNK:
        y = y + jnp.where(row_in_chunk >= s, pltpu.roll(y, s, 0), 0.0)
        s *= 2
    return y


def _chunk_rev_cumsum(y, row_in_chunk):
    n = y.shape[0]
    s = 1
    while s < CHUNK:
        y = y + jnp.where(row_in_chunk + s < CHUNK, pltpu.roll(y, n - s, 0), 0.0)
        s *= 2
    return y


def _together(generators):
    alive = list(generators)
    while alive:
        nxt = []
        for g in alive:
            try:
                next(g)
                nxt.append(g)
            except StopIteration:
                pass
        alive = nxt
        yield


def _lockstep(generators):
    for _ in _together(generators):
        pass


def _pick_lane(tile, lane, idx):
    return jnp.sum(jnp.where(lane == idx, tile, 0.0), axis=-1, keepdims=True)


def _divisor_tile(n, cap, unit=LANES):
    best = unit
    t = unit
    while t <= min(n, cap):
        if n % t == 0:
            best = t
        t += unit
    return n if n <= cap else best


def _in_proj(x2, w_an, w_in_p, conv_w, alog_l, dt_l, S):
    T, D = x2.shape
    N = w_in_p.shape[1]
    tm = min(512, S)
    assert S % tm == 0 and T % tm == 0, "a token tile must not straddle two sequences"
    tiles_per_seq = S // tm
    C3 = 3 * GDN_WIDTH
    H = GDN_HEADS

    def body(x_ref, wn_ref, w_ref, cw_ref, alog_ref, dt_ref, proj_ref, xn_ref, q_out, k_out, v_out, gates_out,
             halo_s):
        xn, _ = _rms(x_ref[...], wn_ref[...])
        xn = xn.astype(MXU_DTYPE)
        xn_ref[...] = xn
        proj = jnp.dot(xn, w_ref[...], preferred_element_type=F32)
        proj_ref[...] = proj
        u = proj[:, P_GQKV:P_GQKV + C3]

        @pl.when(pl.program_id(0) == 0)
        def _():
            halo_s[...] = jnp.zeros_like(halo_s)

        halo = jnp.where(pl.program_id(0) % tiles_per_seq == 0, 0.0, halo_s[...])
        halo_s[...] = u[tm - SUBLANES:, :]
        c, _ = _conv_taps(u, halo, cw_ref[...])
        a = c * _sigmoid(c)
        for h in range(H):
            xq = a[:, h * GDN_DIM:(h + 1) * GDN_DIM]
            xk = a[:, GDN_WIDTH + h * GDN_DIM:GDN_WIDTH + (h + 1) * GDN_DIM]
            q_out[h] = _l2n(xq, GDN_QSCALE)
            k_out[h] = _l2n(xk, 1.0)
            v_out[h] = a[:, 2 * GDN_WIDTH + h * GDN_DIM:2 * GDN_WIDTH + (h + 1) * GDN_DIM]
        lane = lax.broadcasted_iota(jnp.int32, (tm, LANES), 1)
        ric = lax.broadcasted_iota(jnp.int32, (tm, LANES), 0) % CHUNK
        g, beta = _gate_values(proj[:, P_GAB:P_GAB + LANES], alog_ref[...], dt_ref[...], lane)
        gates_out[...] = _chunk_cumsum(g, ric) + beta

    hspec = pl.BlockSpec((H, tm, GDN_DIM), lambda i: (0, i, 0))
    vec = pl.BlockSpec((1, LANES), lambda i: (0, 0))
    return pl.pallas_call(
        body, grid=(T // tm,), name="in_proj",
        in_specs=[pl.BlockSpec((tm, D), lambda i: (i, 0)), pl.BlockSpec((1, D), lambda i: (0, 0)),
                  pl.BlockSpec((D, N), lambda i: (0, 0)), pl.BlockSpec((CONV_W, C3), lambda i: (0, 0)), vec, vec],
        out_specs=[pl.BlockSpec((tm, N), lambda i: (i, 0)), pl.BlockSpec((tm, D), lambda i: (i, 0)),
                   hspec, hspec, hspec, pl.BlockSpec((tm, LANES), lambda i: (i, 0))],
        out_shape=[SDS((T, N), F32), SDS((T, D), MXU_DTYPE)] + [SDS((H, T, GDN_DIM), F32)] * 3
                  + [SDS((T, LANES), F32)],
        scratch_shapes=[pltpu.VMEM((SUBLANES, C3), F32)],
        compiler_params=_params(("arbitrary",)),
    )(x2, w_an, w_in_p, conv_w, alog_l, dt_l)


def _mla_pre(proj, cosf, sinf, w_qln, w_kvln, w_uq_p, w_ukv, qnw, knw, transfer=None):
    T = proj.shape[0]
    tm = min(256, T)
    H = MLA_HEADS

    def body(ql_ref, kvl_ref, kpe_ref, cos_ref, sin_ref, wq_ref, wkv_ref, uq_ref, ukv_ref, qnw_ref, knw_ref,
             q_out, k_out, v_out):
        rms = functools.partial(_rms, on_mxu=True)
        cos, sin = cos_ref[...], sin_ref[...]
        qnw_, knw_ = qnw_ref[...], knw_ref[...]
        qn, _ = rms(ql_ref[...], wq_ref[...])
        kvn, _ = rms(kvl_ref[...], wkv_ref[...])
        qraw = _mm(qn, uq_ref[...])
        kvraw = _mm(kvn, ukv_ref[...])
        kpe = _rope(rms(kpe_ref[...][:, :ROPE], knw_[:, NOPE:])[0], cos, sin)
        for h in range(H):
            qn_h = rms(qraw[:, h * NOPE:(h + 1) * NOPE], qnw_[:, :NOPE])[0]
            qp_h = _rope(rms(qraw[:, H * NOPE + h * ROPE:H * NOPE + (h + 1) * ROPE], qnw_[:, NOPE:])[0], cos, sin)
            q_out[h] = (jnp.concatenate([qn_h, qp_h], axis=-1) * ATT_SCALE).astype(MXU_DTYPE)
            kn_h = rms(kvraw[:, h * 256:h * 256 + NOPE], knw_[:, :NOPE])[0]
            k_out[h] = jnp.concatenate([kn_h, kpe], axis=-1).astype(MXU_DTYPE)
            v_out[h] = kvraw[:, h * 256 + NOPE:(h + 1) * 256].astype(MXU_DTYPE)

    full = lambda a: pl.BlockSpec(a.shape, lambda i: (0,) * a.ndim)
    return _call_beside(
        body, transfer, grid=(T // tm,), name="mla_pre", scratch_shapes=[], semantics=("arbitrary",),
        args=(proj, proj, proj, cosf, sinf, w_qln, w_kvln, w_uq_p, w_ukv, qnw, knw),
        in_specs=[pl.BlockSpec((tm, 256), lambda i: (i, P_QLAT // 256)),
                  pl.BlockSpec((tm, 256), lambda i: (i, P_KVLAT // 256)),
                  pl.BlockSpec((tm, 128), lambda i: (i, P_KPE // 128)),
                  pl.BlockSpec((tm, ROPE), lambda i: (i, 0)), pl.BlockSpec((tm, ROPE), lambda i: (i, 0)),
                  full(w_qln), full(w_kvln), full(w_uq_p), full(w_ukv), full(qnw), full(knw)],
        out_specs=[pl.BlockSpec((H, tm, QK_DIM), lambda i: (0, i, 0)),
                   pl.BlockSpec((H, tm, QK_DIM), lambda i: (0, i, 0)),
                   pl.BlockSpec((H, tm, V_DIM), lambda i: (0, i, 0))],
        out_shape=[SDS((H, T, QK_DIM), MXU_DTYPE), SDS((H, T, QK_DIM), MXU_DTYPE), SDS((H, T, V_DIM), MXU_DTYPE)])


def _attn_fwd(q4, k4, v4, B, S, transfer=None):
    H = MLA_HEADS
    bq = min(ATTN_BLOCK, S)
    nq = S // bq
    rows = bq // ATTN_CHAINS

    def body(q_ref, k_ref, v_ref, o_ref, lse_ref):
        col = lax.broadcasted_iota(jnp.int32, (rows, bq), 1)
        row = lax.broadcasted_iota(jnp.int32, (rows, bq), 0)

        def q_step(qi, carry):
            qs = pl.multiple_of(qi * bq, bq)
            qsub = [q_ref[0, pl.ds(qs + j * rows, rows), :] for j in range(ATTN_CHAINS)]

            def k_block(ks, cs, diagonal):
                k = k_ref[0, pl.ds(ks, bq), :]
                v = v_ref[0, pl.ds(ks, bq), :]
                out = [None] * ATTN_CHAINS

                def chain(j):
                    m, l, acc = cs[j]
                    s = _mm_nt(qsub[j], k)
                    yield
                    if diagonal:
                        s = jnp.where(col <= row + j * rows, s, -jnp.inf)
                    m_new = jnp.maximum(m, jnp.max(s, axis=-1, keepdims=True))
                    p = jnp.exp(s - m_new)
                    a = jnp.exp(m - m_new)
                    l_new = a * l + jnp.sum(p, axis=-1, keepdims=True)
                    yield
                    out[j] = (m_new, l_new, a * acc + _mm(p, v))

                _lockstep([chain(j) for j in range(ATTN_CHAINS)])
                return tuple(out)

            init = tuple((jnp.full((rows, 1), -jnp.inf, F32), jnp.zeros((rows, 1), F32),
                          jnp.zeros((rows, V_DIM), F32)) for _ in range(ATTN_CHAINS))
            cs = lax.fori_loop(0, qi, lambda kj, c: k_block(pl.multiple_of(kj * bq, bq), c, False), init)
            for j, (m, l, acc) in enumerate(k_block(qs, cs, True)):
                o_ref[0, pl.ds(qs + j * rows, rows), :] = acc / l
                lse_ref[0, pl.ds(qs + j * rows, rows), :] = m + jnp.log(l)
            return carry

        lax.fori_loop(0, nq, q_step, 0)

    spec = lambda d: pl.BlockSpec((1, S, d), lambda h, b: (h, b, 0))
    return _call_beside(
        body, transfer, grid=(H, B), name="attn_fwd",
        in_specs=[spec(QK_DIM), spec(QK_DIM), spec(V_DIM)],
        out_specs=[spec(V_DIM), spec(1)],
        out_shape=[SDS((H, B * S, V_DIM), F32), SDS((H, B * S, 1), F32)],
        scratch_shapes=[], semantics=("arbitrary", "arbitrary"), args=(q4, k4, v4))


def _conv_taps(u, halo, w):
    sh = [_shift_down(u, halo, j) for j in range(CONV_W)]
    c = w[0:1] * sh[3] + w[1:2] * sh[2] + w[2:3] * sh[1] + w[3:4] * sh[0]
    return c, sh


def _gate_values(gab, alog_l, dt_l, lane):
    g = -jnp.exp(alog_l) * jax.nn.softplus(gab + dt_l)
    g = jnp.where(lane < GDN_HEADS, g, 0.0)
    beta = jnp.where((lane >= GDN_HEADS) & (lane < 2 * GDN_HEADS), _sigmoid(gab), 0.0)
    return g, beta


def _unit_lower_inverses(Ls, eye):
    Ps = [eye - L for L in Ls]
    Ms = [_split(-L) for L in Ls]
    for _ in range(5):
        sq = [_mm_split(m, m) for m in Ms]
        Ms = [_split(s) for s in sq]
        Ps = [p + _mm_split(_split(p), m) for p, m in zip(Ps, Ms)]
    return Ps


def _chunk_decays(gt, lane, h, ri, ci, rcol):
    Gc = _pick_lane(gt, lane, h)
    bt = _pick_lane(gt, lane, h + GDN_HEADS)
    Gb = jnp.broadcast_to(Gc, (CHUNK, CHUNK))
    Gam = jnp.where(ri >= ci, jnp.exp(Gb - Gb.T), 0.0)
    Gl = jnp.sum(jnp.where(rcol == CHUNK - 1, Gc, 0.0), axis=0, keepdims=True)
    return Gc, bt, Gam, jnp.exp(Gc), jnp.exp(Gl - Gc), jnp.exp(Gl)


GDN_FWD_UNROLL = 16
GDN_BWD_UNROLL = 8
GDN_RECUR_STEPS_PER_STAGE = 2


def _gdn_fwd(qg, kg, vg, gates, B, S, transfer=None):
    H, D, C = GDN_HEADS, GDN_DIM, CHUNK
    NC = S // C
    P = 2 if B % 2 == 0 else 1
    Sb, NCb = P * S, P * NC
    U = GDN_FWD_UNROLL if NCb % GDN_FWD_UNROLL == 0 else 1
    NG = NCb // U

    def body(q_ref, k_ref, v_ref, g_ref, o_ref, st_ref, ai_ref, u_ref, w_ref, q2_s, au_s, bc_s, w2_s, el_s):
        h = pl.program_id(0)
        lane = lax.broadcasted_iota(jnp.int32, (C, LANES), 1)
        ri = lax.broadcasted_iota(jnp.int32, (C, C), 0)
        ci = lax.broadcasted_iota(jnp.int32, (C, C), 1)
        rcol = lax.broadcasted_iota(jnp.int32, (C, 1), 0)
        eye = (ri == ci).astype(F32)

        def group(gi, c):
            ns = [gi * U + j for j in range(U)]
            css = [pl.multiple_of(n * C, C) for n in ns]
            qs = [q_ref[0, pl.ds(cs, C), :] for cs in css]
            ks = [k_ref[0, pl.ds(cs, C), :] for cs in css]
            vs = [v_ref[0, pl.ds(cs, C), :] for cs in css]
            decs = [_chunk_decays(g_ref[pl.ds(cs, C), :], lane, h, ri, ci, rcol) for cs in css]
            qks = [_mm_nt(jnp.concatenate([q, k], axis=0), k) for q, k in zip(qs, ks)]
            ainvs = _unit_lower_inverses(
                [jnp.where(ri > ci, d[1] * qk[C:] * d[2], 0.0) for qk, d in zip(qks, decs)], eye)
            sols = [_mm_exact(a, jnp.concatenate([v * d[1], k * (d[1] * d[3])], axis=-1))
                    for a, k, v, d in zip(ainvs, ks, vs, decs)]
            atuw = [_mm(qk[:C] * d[2], sol) for qk, d, sol in zip(qks, decs, sols)]
            kduw = [_mm_tn(k * d[4], sol) for k, d, sol in zip(ks, decs, sols)]
            for n, cs, q, a, sol, au, ku, (Gc, bt, Gam, e, f, eL) in zip(ns, css, qs, ainvs, sols, atuw, kduw, decs):
                u_ref[0, pl.ds(cs, C), :] = sol[:, :D]
                w_ref[0, pl.ds(cs, C), :] = sol[:, D:]
                au_s[pl.ds(cs, C), :] = au[:, :D]
                q2_s[pl.ds(cs, C), :] = q * e - au[:, D:]
                bc_s[n] = ku[:, :D]
                w2_s[n] = ku[:, D:]
                el_s[n] = jnp.broadcast_to(eL, (SUBLANES, LANES))
                ai_ref[0, n] = a.T
            return c

        lax.fori_loop(0, NG, group, 0)

        def step(n, states):
            new = []
            for p, S_ in enumerate(states):
                m = p * NC + n
                cs = pl.multiple_of(m * C, C)
                o_ref[0, pl.ds(cs, C), :] = _mm(q2_s[pl.ds(cs, C), :], S_) + au_s[pl.ds(cs, C), :]
                st_ref[0, m] = S_
                new.append(S_ * el_s[m, 0:1, :] + bc_s[m] - _mm(w2_s[m], S_))
            return tuple(new)

        lax.fori_loop(0, NC, step, tuple(jnp.zeros((D, D), F32) for _ in range(P)))

    spec = pl.BlockSpec((1, Sb, D), lambda h, b: (h, b, 0))
    return _call_beside(
        body, transfer, grid=(H, B // P), name="gdn_fwd",
        in_specs=[spec, spec, spec, pl.BlockSpec((Sb, LANES), lambda h, b: (b, 0))],
        out_specs=[spec, pl.BlockSpec((1, NCb, D, D), lambda h, b: (h, b, 0, 0)),
                   pl.BlockSpec((1, NCb, C, C), lambda h, b: (h, b, 0, 0)), spec, spec],
        out_shape=[SDS((H, B * S, D), F32), SDS((H, B * NC, D, D), F32), SDS((H, B * NC, C, C), F32),
                   SDS((H, B * S, D), F32), SDS((H, B * S, D), F32)],
        scratch_shapes=[pltpu.VMEM((Sb, D), F32), pltpu.VMEM((Sb, D), F32), pltpu.VMEM((NCb, D, D), F32),
                        pltpu.VMEM((NCb, D, D), F32), pltpu.VMEM((NCb, SUBLANES, LANES), F32)],
        semantics=("arbitrary", "arbitrary"), args=(qg, kg, vg, gates))


def _mix_out(o_mla, o_gdn, proj, x2, mla_w, gdn_w, w_out):
    T, D = x2.shape
    tm = min(512, T)
    H = MLA_HEADS

    def body(om_ref, og_ref, z_ref, x_ref, mw_ref, gw_ref, w_ref, h_ref, mix_ref):
        z = z_ref[...]
        parts = [_rms(om_ref[h], mw_ref[h:h + 1, :])[0] for h in range(H)]
        for h in range(GDN_HEADS):
            zh = z[:, h * GDN_DIM:(h + 1) * GDN_DIM]
            parts.append(_rms(og_ref[h], gw_ref[...])[0] * (zh * _sigmoid(zh)))
        mix = jnp.concatenate(parts, axis=-1).astype(MXU_DTYPE)
        mix_ref[...] = mix
        h_ref[...] = x_ref[...] + jnp.dot(mix, w_ref[...], preferred_element_type=F32)

    hspec = pl.BlockSpec((H, tm, V_DIM), lambda i: (0, i, 0))
    return pl.pallas_call(
        body, grid=(T // tm,), name="mix_out",
        in_specs=[hspec, hspec, pl.BlockSpec((tm, GDN_WIDTH), lambda i: (i, P_GZ // GDN_WIDTH)),
                  pl.BlockSpec((tm, D), lambda i: (i, 0)),
                  pl.BlockSpec((H, V_DIM), lambda i: (0, 0)), pl.BlockSpec((1, GDN_DIM), lambda i: (0, 0)),
                  pl.BlockSpec((D, D), lambda i: (0, 0))],
        out_specs=[pl.BlockSpec((tm, D), lambda i: (i, 0)), pl.BlockSpec((tm, D), lambda i: (i, 0))],
        out_shape=[SDS((T, D), F32), SDS((T, D), MXU_DTYPE)],
        compiler_params=_params(("arbitrary",)),
    )(o_mla, o_gdn, proj, x2, mla_w, gdn_w, w_out)


def _mlp_fwd(h2, w_mn, w_up, w_down, target):
    T, D = h2.shape
    ns, _, ts = w_up.shape
    F = ns * ts
    tm = min(512, T)
    G = MLP_FWD_SHARDS
    tf, nf = G * ts, ns // G

    def body(h_ref, wn_ref, up_w, down_w, t_ref, up_ref, hn_ref, dy_ref, loss_ref, dyb_ref, y_acc):
        j = pl.program_id(1)

        @pl.when(j == 0)
        def _():
            hn_ref[...] = _rms(h_ref[...], wn_ref[...])[0].astype(MXU_DTYPE)
            y_acc[...] = h_ref[...]

        parts = []
        for c in range(G):
            up = jnp.dot(hn_ref[...], up_w[c], preferred_element_type=F32)
            up_ref[:, c * ts:(c + 1) * ts] = up.astype(MXU_DTYPE)
            r = jnp.maximum(up, 0.0)
            parts.append(_mm(r * r, down_w[c * ts:(c + 1) * ts, :]))
        y_acc[...] += functools.reduce(jnp.add, parts)

        @pl.when(j == nf - 1)
        def _():
            err = y_acc[...] - t_ref[...]
            dy_ref[...] = err / D
            dyb_ref[...] = (err / D).astype(MXU_DTYPE)
            loss_ref[...] = jnp.full((1, SUBLANES, LANES), jnp.sum(err * err), F32)

    return pl.pallas_call(
        body, grid=(T // tm, nf), name="mlp_fwd",
        in_specs=[pl.BlockSpec((tm, D), lambda i, j: (i, 0)), pl.BlockSpec((1, D), lambda i, j: (0, 0)),
                  pl.BlockSpec((G, D, ts), lambda i, j: (j, 0, 0)), pl.BlockSpec((tf, D), lambda i, j: (j, 0)),
                  pl.BlockSpec((tm, D), lambda i, j: (i, 0))],
        out_specs=[pl.BlockSpec((tm, tf), lambda i, j: (i, j)), pl.BlockSpec((tm, D), lambda i, j: (i, 0)),
                   pl.BlockSpec((tm, D), lambda i, j: (i, 0)),
                   pl.BlockSpec((1, SUBLANES, LANES), lambda i, j: (i, 0, 0)),
                   pl.BlockSpec((tm, D), lambda i, j: (i, 0))],
        out_shape=[SDS((T, F), MXU_DTYPE), SDS((T, D), MXU_DTYPE), SDS((T, D), F32),
                   SDS((T // tm, SUBLANES, LANES), F32), SDS((T, D), MXU_DTYPE)],
        scratch_shapes=[pltpu.VMEM((tm, D), F32)],
        compiler_params=_params(("arbitrary", "arbitrary")),
    )(h2, w_mn, w_up, w_down, target)


def _mlp_bwd(dy, dyb, up, h2, w_mn, w_up, w_down, transfer=None):
    T, D = h2.shape
    ns, _, ts = w_up.shape
    F = ns * ts
    tm = min(512, T)
    G = MLP_BWD_SHARDS
    tf, nf = G * ts, ns // G

    def body(dy_ref, dyb_ref, up_ref, h_ref, wn_ref, up_w, down_w, dh_ref, dhb_ref, dup_ref, dwn_ref, acc):
        i, j = pl.program_id(0), pl.program_id(1)

        @pl.when((i == 0) & (j == 0))
        def _():
            dwn_ref[...] = jnp.zeros_like(dwn_ref)

        @pl.when(j == 0)
        def _():
            acc[...] = jnp.zeros_like(acc)

        parts = []
        for c in range(G):
            cols = slice(c * ts, (c + 1) * ts)
            r = jnp.maximum(up_ref[:, cols].astype(F32), 0.0)
            dup = (_mm_nt(dyb_ref[...], down_w[cols, :]) * (2.0 * r)).astype(MXU_DTYPE)
            dup_ref[:, cols] = dup
            parts.append(_mm_nt(dup, up_w[c]))
        acc[...] += functools.reduce(jnp.add, parts)

        @pl.when(j == nf - 1)
        def _():
            hv = h_ref[...]
            _, rr = _rms(hv, wn_ref[...])
            dx, dw = _rms_bwd(acc[...], hv, wn_ref[...], rr)
            dh = dy_ref[...] + dx
            dh_ref[...] = dh
            dhb_ref[...] = dh.astype(MXU_DTYPE)
            dwn_ref[...] += dw

    row = lambda i, j: (i, 0)
    return _call_beside(
        body, transfer, grid=(T // tm, nf), name="mlp_bwd",
        in_specs=[pl.BlockSpec((tm, D), row), pl.BlockSpec((tm, D), row), pl.BlockSpec((tm, tf), lambda i, j: (i, j)),
                  pl.BlockSpec((tm, D), row), pl.BlockSpec((1, D), lambda i, j: (0, 0)),
                  pl.BlockSpec((G, D, ts), lambda i, j: (j, 0, 0)), pl.BlockSpec((tf, D), lambda i, j: (j, 0))],
        out_specs=[pl.BlockSpec((tm, D), row), pl.BlockSpec((tm, D), row),
                   pl.BlockSpec((tm, tf), lambda i, j: (i, j)), pl.BlockSpec((1, D), lambda i, j: (0, 0))],
        out_shape=[SDS((T, D), F32), SDS((T, D), MXU_DTYPE), SDS((T, F), MXU_DTYPE), SDS((1, D), F32)],
        scratch_shapes=[pltpu.VMEM((tm, D), F32)], semantics=("arbitrary", "arbitrary"),
        args=(dy, dyb, up, h2, w_mn, w_up, w_down))


def _mix_bwd(dhb, o_mla, o_gdn, proj, mla_w, gdn_w, w_out):
    T, D = dhb.shape
    tm = min(512, T)
    H = MLA_HEADS

    def body(dh_ref, om_ref, og_ref, z_ref, mw_ref, gw_ref, w_ref, dom_ref, dog_ref, dz_ref, dmw_ref, dgw_ref,
             delta_ref):
        @pl.when(pl.program_id(0) == 0)
        def _():
            dmw_ref[...] = jnp.zeros_like(dmw_ref)
            dgw_ref[...] = jnp.zeros_like(dgw_ref)

        dmix = _mm_nt(dh_ref[...], w_ref[...])
        z = z_ref[...]
        dmw, dzs = [], []
        dgw = jnp.zeros((1, GDN_DIM), F32)
        for h in range(H):
            o = om_ref[h]
            w = mw_ref[h:h + 1, :]
            _, r = _rms(o, w)
            dx, dw = _rms_bwd(dmix[:, h * V_DIM:(h + 1) * V_DIM], o, w, r)
            dom_ref[h] = dx.astype(MXU_DTYPE)
            delta_ref[h] = jnp.sum(dx * o, axis=-1, keepdims=True)
            dmw.append(dw)
        for h in range(GDN_HEADS):
            o = og_ref[h]
            w = gw_ref[...]
            zh = z[:, h * GDN_DIM:(h + 1) * GDN_DIM]
            sg = _sigmoid(zh)
            yn, r = _rms(o, w)
            dy = dmix[:, H * V_DIM + h * GDN_DIM:H * V_DIM + (h + 1) * GDN_DIM]
            dzs.append(dy * yn * (sg * (1.0 + zh * (1.0 - sg))))
            dx, dw = _rms_bwd(dy * (zh * sg), o, w, r)
            dog_ref[h] = dx.astype(MXU_DTYPE)
            dgw = dgw + dw
        dz_ref[...] = jnp.concatenate(dzs, axis=-1).astype(MXU_DTYPE)
        dmw_ref[...] += jnp.concatenate(dmw, axis=0)
        dgw_ref[...] += dgw

    hspec = pl.BlockSpec((H, tm, V_DIM), lambda i: (0, i, 0))
    return pl.pallas_call(
        body, grid=(T // tm,), name="mix_bwd",
        in_specs=[pl.BlockSpec((tm, D), lambda i: (i, 0)), hspec, hspec,
                  pl.BlockSpec((tm, GDN_WIDTH), lambda i: (i, P_GZ // GDN_WIDTH)),
                  pl.BlockSpec((H, V_DIM), lambda i: (0, 0)), pl.BlockSpec((1, GDN_DIM), lambda i: (0, 0)),
                  pl.BlockSpec((D, D), lambda i: (0, 0))],
        out_specs=[hspec, hspec, pl.BlockSpec((tm, GDN_WIDTH), lambda i: (i, 0)),
                   pl.BlockSpec((H, V_DIM), lambda i: (0, 0)), pl.BlockSpec((1, GDN_DIM), lambda i: (0, 0)),
                   pl.BlockSpec((H, tm, 1), lambda i: (0, i, 0))],
        out_shape=[SDS((H, T, V_DIM), MXU_DTYPE), SDS((H, T, GDN_DIM), MXU_DTYPE), SDS((T, GDN_WIDTH), MXU_DTYPE),
                   SDS((H, V_DIM), F32), SDS((1, GDN_DIM), F32), SDS((H, T, 1), F32)],
        compiler_params=_params(("arbitrary",)),
    )(dhb, o_mla, o_gdn, proj, mla_w, gdn_w, w_out)


def _attn_bwd(q4, k4, v4, do4, delta4, lse4, B, S, transfer=None):
    H = MLA_HEADS
    bq = min(ATTN_BLOCK, S)
    nq = S // bq
    rows = bq // ATTN_CHAINS

    def body(q_ref, k_ref, v_ref, do_ref, delta_ref, lse_ref, dq_ref, dk_ref, dv_ref):
        dq_ref[...] = jnp.zeros_like(dq_ref)
        dk_ref[...] = jnp.zeros_like(dk_ref)
        dv_ref[...] = jnp.zeros_like(dv_ref)

        col = lax.broadcasted_iota(jnp.int32, (rows, bq), 1)
        row = lax.broadcasted_iota(jnp.int32, (rows, bq), 0)

        def k_step(kj, carry):
            ks = pl.multiple_of(kj * bq, bq)
            k = k_ref[0, pl.ds(ks, bq), :]
            v = v_ref[0, pl.ds(ks, bq), :]

            def q_block(qs, diagonal):
                dks, dvs = [None] * ATTN_CHAINS, [None] * ATTN_CHAINS

                def chain(j):
                    sl = pl.ds(qs + j * rows, rows)
                    q = q_ref[0, sl, :]
                    do = do_ref[0, sl, :].astype(MXU_DTYPE)
                    s = _mm_nt(q, k)
                    dp = _mm_nt(do, v)
                    yield
                    p = jnp.exp(s - lse_ref[0, sl, :])
                    if diagonal:
                        p = jnp.where(col <= row + j * rows, p, 0.0)
                    ds = p * (dp - delta_ref[0, sl, :])
                    yield
                    dvs[j] = _mm_tn(p, do)
                    dks[j] = _mm_tn(ds, q)
                    dq_ref[0, sl, :] += _mm(ds, k)

                _lockstep([chain(j) for j in range(ATTN_CHAINS)])
                dv_ref[0, pl.ds(ks, bq), :] += functools.reduce(jnp.add, dvs)
                dk_ref[0, pl.ds(ks, bq), :] += functools.reduce(jnp.add, dks)

            q_block(ks, True)

            def q_step(qi, c):
                q_block(pl.multiple_of(qi * bq, bq), False)
                return c

            lax.fori_loop(kj + 1, nq, q_step, 0)
            return carry

        lax.fori_loop(0, nq, k_step, 0)

    spec = lambda d: pl.BlockSpec((1, S, d), lambda h, b: (h, b, 0))
    return _call_beside(
        body, transfer, grid=(H, B), name="attn_bwd",
        in_specs=[spec(QK_DIM), spec(QK_DIM), spec(V_DIM), spec(V_DIM), spec(1), spec(1)],
        out_specs=[spec(QK_DIM), spec(QK_DIM), spec(V_DIM)],
        out_shape=[SDS((H, B * S, QK_DIM), F32), SDS((H, B * S, QK_DIM), F32), SDS((H, B * S, V_DIM), F32)],
        scratch_shapes=[], semantics=("arbitrary", "arbitrary"),
        args=(q4, k4, v4, do4, delta4, lse4))


def _gdn_bwd(qg, kg, vg, gates, states, ainv, u4, w4, do4, B, S, transfer=None):
    H, D, C = GDN_HEADS, GDN_DIM, CHUNK
    NC = S // C
    U = GDN_BWD_UNROLL if NC % GDN_BWD_UNROLL == 0 else 1
    NG = NC // U

    def body(q_ref, k_ref, v_ref, g_ref, st_ref, ai_ref, u_ref, w_ref, do_ref, dq_ref, dk_ref, dv_ref, dgb_ref,
             kd_s, x1_s, x2_s, el_s, dvn_s, ds_s, w2t_s):
        h = pl.program_id(0)
        lane = lax.broadcasted_iota(jnp.int32, (C, LANES), 1)
        ri = lax.broadcasted_iota(jnp.int32, (C, C), 0)
        ci = lax.broadcasted_iota(jnp.int32, (C, C), 1)
        rcol = lax.broadcasted_iota(jnp.int32, (C, 1), 0)

        def rsum(a):
            return jnp.sum(a, axis=-1, keepdims=True)

        def prepare(n):
            cs = n * C
            q = q_ref[0, pl.ds(cs, C), :]
            k = k_ref[0, pl.ds(cs, C), :]
            do = do_ref[0, pl.ds(cs, C), :]
            Gc, bt, Gam, e, f, eL = _chunk_decays(g_ref[pl.ds(cs, C), :], lane, h, ri, ci, rcol)
            At = _mm_nt(q, k) * Gam
            yield
            x1 = _mm_tn(At, do)
            x2 = _mm_tn(q * e, do)
            kd = k * f
            w = w_ref[0, pl.ds(cs, C), :]
            yield
            x1_s[pl.ds(cs, C), :] = x1
            x2_s[n] = x2 - _mm_tn(w, x1)
            w2t_s[n] = _mm_tn(w, kd)
            kd_s[pl.ds(cs, C), :] = kd
            el_s[n] = jnp.broadcast_to(eL, (SUBLANES, LANES))

        def recur(n, dS):
            cs = n * C
            ds_s[n] = dS
            dvn_s[pl.ds(cs, C), :] = x1_s[pl.ds(cs, C), :] + _mm(kd_s[pl.ds(cs, C), :], dS)
            return x2_s[n] + el_s[n, 0:1, :] * dS - _mm(w2t_s[n], dS)

        def local(n):
            cs = n * C
            q = q_ref[0, pl.ds(cs, C), :]
            k = k_ref[0, pl.ds(cs, C), :]
            v = v_ref[0, pl.ds(cs, C), :]
            do = do_ref[0, pl.ds(cs, C), :]
            u = u_ref[0, pl.ds(cs, C), :]
            w = w_ref[0, pl.ds(cs, C), :]
            dvn = dvn_s[pl.ds(cs, C), :]
            dS = ds_s[n]
            Gc, bt, Gam, e, f, eL = _chunk_decays(g_ref[pl.ds(cs, C), :], lane, h, ri, ci, rcol)
            S0 = st_ref[0, n]
            AinvT = ai_ref[0, n]
            qk = _mm_nt(jnp.concatenate([q, k], axis=0), k)
            QK, KK = qk[:C], qk[C:]
            be = bt * e
            sol = jnp.concatenate([u, w], axis=-1)
            vn = u - _mm(w, S0)
            yield
            dAt = jnp.where(ri >= ci, _mm_nt(do, vn), 0.0)
            dqd = _mm_nt(do, S0)
            dw = -_mm_nt(dvn, S0)
            dkd = _mm_nt(vn, dS)
            deL = jnp.sum(rsum(dS * S0), axis=0, keepdims=True)
            yield
            dR = _mm_exact(AinvT, jnp.concatenate([dvn, dw], axis=-1))
            dR1, dR2 = dR[:, :D], dR[:, D:]
            yield
            dL = jnp.where(ri > ci, -_mm_nt(dR, sol), 0.0)
            yield
            dv_ref[0, pl.ds(cs, C), :] = dR1 * bt
            r2 = rsum(dR2 * k)
            X = dL * Gam
            dbt = rsum(dR1 * v) + r2 * e + rsum(X * KK)
            de = r2 * bt + rsum(dqd * q)
            dKK = X * bt
            dQK = dAt * Gam
            dq_ref[0, pl.ds(cs, C), :] = _mm(dQK, k) + dqd * e
            dk_ref[0, pl.ds(cs, C), :] = dR2 * be + _mm(dKK + dKK.T, k) + _mm_tn(dQK, q) + dkd * f
            df = rsum(dkd * k)
            Z = (dL * (bt * KK) + dAt * QK) * Gam
            dG = rsum(Z) - rsum(Z.T) + de * e - df * f
            dGl = jnp.sum(df * f, axis=0, keepdims=True) + deL * eL
            dG = dG + jnp.where(rcol == C - 1, dGl, 0.0)
            dgb_ref[0, pl.ds(cs, C), :] = jnp.where(lane == 0, dG, jnp.where(lane == 1, dbt, 0.0))

        state = [jnp.zeros((D, D), F32)]

        def recur_group(g):
            for j, n in enumerate(reversed(range(g * U, (g + 1) * U))):
                state[0] = recur(n, state[0])
                if j % GDN_RECUR_STEPS_PER_STAGE == GDN_RECUR_STEPS_PER_STAGE - 1:
                    yield

        def stage(fn, g):
            return _together([fn(g * U + j) for j in range(U)])

        for step in range(NG + 2):
            jobs = [(stage, prepare, NG - 1 - step), (None, None, NG - step), (stage, local, NG + 1 - step)]
            _lockstep([recur_group(g) if make is None else make(fn, g) for make, fn, g in jobs if 0 <= g < NG])

    spec = pl.BlockSpec((1, S, D), lambda h, b: (h, b, 0))
    return _call_beside(
        body, transfer, grid=(H, B), name="gdn_bwd",
        in_specs=[spec, spec, spec, pl.BlockSpec((S, LANES), lambda h, b: (b, 0)),
                  pl.BlockSpec((1, NC, D, D), lambda h, b: (h, b, 0, 0)),
                  pl.BlockSpec((1, NC, C, C), lambda h, b: (h, b, 0, 0)), spec, spec, spec],
        out_specs=[spec, spec, spec, spec],
        out_shape=[SDS((H, B * S, D), F32)] * 4,
        scratch_shapes=[pltpu.VMEM((S, D), F32), pltpu.VMEM((S, D), F32), pltpu.VMEM((NC, D, D), F32),
                        pltpu.VMEM((NC, SUBLANES, LANES), F32), pltpu.VMEM((S, D), F32),
                        pltpu.VMEM((NC, D, D), F32), pltpu.VMEM((NC, D, D), F32)],
        semantics=("arbitrary", "arbitrary"), args=(qg, kg, vg, gates, states, ainv, u4, w4, do4))


def _gdn_pre_bwd(proj, conv_w, alog_l, dt_l, dq4, dk4, dv4, dgb4, S):
    T = proj.shape[0]
    tm = min(256, T)
    tiles_per_seq = S // tm
    C3 = 3 * GDN_WIDTH
    H = GDN_HEADS

    def body(u_ref, halo_ref, gab_ref, w_ref, alog_ref, dt_ref, dq_ref, dk_ref, dv_ref, dgb_ref,
             dc_ref, dgab_ref, dcw_ref, dalog_ref, ddt_ref):
        i = pl.program_id(0)

        @pl.when(i == 0)
        def _():
            dcw_ref[...] = jnp.zeros_like(dcw_ref)
            dalog_ref[...] = jnp.zeros_like(dalog_ref)
            ddt_ref[...] = jnp.zeros_like(ddt_ref)

        halo = jnp.where(i % tiles_per_seq == 0, 0.0, halo_ref[...])
        c, sh = _conv_taps(u_ref[...], halo, w_ref[...])
        sg = _sigmoid(c)
        a = c * sg
        das = [None] * (3 * H)
        for h in range(H):
            xq = a[:, h * GDN_DIM:(h + 1) * GDN_DIM]
            xk = a[:, GDN_WIDTH + h * GDN_DIM:GDN_WIDTH + (h + 1) * GDN_DIM]
            das[h] = _l2n_bwd(dq_ref[h], xq, GDN_QSCALE)
            das[H + h] = _l2n_bwd(dk_ref[h], xk, 1.0)
            das[2 * H + h] = dv_ref[h]
        dc = jnp.concatenate(das, axis=-1) * (sg * (1.0 + c * (1.0 - sg)))
        dc_ref[...] = dc
        dcw_ref[...] += jnp.concatenate(
            [jnp.sum(dc * sh[CONV_W - 1 - t], axis=0, keepdims=True) for t in range(CONV_W)], axis=0)
        lane = lax.broadcasted_iota(jnp.int32, (tm, LANES), 1)
        ric = lax.broadcasted_iota(jnp.int32, (tm, LANES), 0) % CHUNK
        dG = jnp.zeros((tm, LANES), F32)
        for h in range(H):
            t = dgb_ref[h]
            dG = dG + jnp.where(lane == h, _pick_lane(t, lane, 0), 0.0) \
                    + jnp.where(lane == h + H, _pick_lane(t, lane, 1), 0.0)
        is_g = lane < H
        dg = jnp.where(is_g, _chunk_rev_cumsum(jnp.where(is_g, dG, 0.0), ric), 0.0)
        gab = gab_ref[...]
        g, beta = _gate_values(gab, alog_ref[...], dt_ref[...], lane)
        dga = jnp.where(is_g, dg * (-jnp.exp(alog_ref[...])) * _sigmoid(gab + dt_ref[...]), 0.0)
        dgb = jnp.where(is_g, 0.0, dG) * beta * (1.0 - beta)
        dgab_ref[...] = (dga + dgb).astype(MXU_DTYPE)
        dalog_ref[...] += jnp.sum(dg * g, axis=0, keepdims=True)
        ddt_ref[...] += jnp.sum(dga, axis=0, keepdims=True)

    hspec = pl.BlockSpec((H, tm, GDN_DIM), lambda i: (0, i, 0))
    vec = pl.BlockSpec((1, LANES), lambda i: (0, 0))
    return pl.pallas_call(
        body, grid=(T // tm,), name="gdn_pre_bwd",
        in_specs=[pl.BlockSpec((tm, C3), lambda i: (i, 0)),
                  pl.BlockSpec((SUBLANES, C3), lambda i: (jnp.maximum(i * (tm // SUBLANES) - 1, 0), 0)),
                  pl.BlockSpec((tm, LANES), lambda i: (i, P_GAB // LANES)),
                  pl.BlockSpec((CONV_W, C3), lambda i: (0, 0)), vec, vec, hspec, hspec, hspec, hspec],
        out_specs=[pl.BlockSpec((tm, C3), lambda i: (i, 0)), pl.BlockSpec((tm, LANES), lambda i: (i, 0)),
                   pl.BlockSpec((CONV_W, C3), lambda i: (0, 0)), vec, vec],
        out_shape=[SDS((T, C3), F32), SDS((T, LANES), MXU_DTYPE), SDS((CONV_W, C3), F32),
                   SDS((1, LANES), F32), SDS((1, LANES), F32)],
        compiler_params=_params(("arbitrary",)),
    )(proj, proj, proj, conv_w, alog_l, dt_l, dq4, dk4, dv4, dgb4)


def _mla_pre_bwd(proj, cosf, sinf, w_qln, w_kvln, w_uq_p, w_ukv, qnw, knw, dq4, dk4, dv4, transfer=None):
    T = proj.shape[0]
    tm = min(256, T)
    H = MLA_HEADS

    def body(ql_ref, kvl_ref, kpe_ref, cos_ref, sin_ref, wq_ref, wkv_ref, uq_ref, ukv_ref, qnw_ref, knw_ref,
             dq_ref, dk_ref, dv_ref,
             dql_ref, dkvl_ref, dkpe_ref, dqraw_ref, dkvraw_ref, qn_ref, kvn_ref, dwq_ref, dwkv_ref, dqnw_ref, dknw_ref):
        @pl.when(pl.program_id(0) == 0)
        def _():
            for r in (dwq_ref, dwkv_ref, dqnw_ref, dknw_ref):
                r[...] = jnp.zeros_like(r)

        cos, sin = cos_ref[...], sin_ref[...]
        qnw_, knw_ = qnw_ref[...], knw_ref[...]
        ql, kvl = ql_ref[...], kvl_ref[...]
        kpe_raw = kpe_ref[...][:, :ROPE]
        rms = functools.partial(_rms, on_mxu=True)
        rms_bwd = functools.partial(_rms_bwd, on_mxu=True)
        qn, rq = rms(ql, wq_ref[...])
        kvn, rkv = rms(kvl, wkv_ref[...])
        qn_ref[...] = qn.astype(MXU_DTYPE)
        kvn_ref[...] = kvn.astype(MXU_DTYPE)
        qraw = _mm(qn, uq_ref[...])
        kvraw = _mm(kvn, ukv_ref[...])
        dq_nope, dq_pe, dkv_parts = [], [], []
        dqnw_n = jnp.zeros((1, NOPE), F32)
        dqnw_p = jnp.zeros((1, ROPE), F32)
        dknw_n = jnp.zeros((1, NOPE), F32)
        dkpe = jnp.zeros((tm, ROPE), F32)
        for h in range(H):
            dq = dq_ref[h] * ATT_SCALE
            x = qraw[:, h * NOPE:(h + 1) * NOPE]
            dx, dw = rms_bwd(dq[:, :NOPE], x, qnw_[:, :NOPE], rms(x, qnw_[:, :NOPE])[1])
            dq_nope.append(dx)
            dqnw_n = dqnw_n + dw
            x = qraw[:, H * NOPE + h * ROPE:H * NOPE + (h + 1) * ROPE]
            dx, dw = rms_bwd(_rope_bwd(dq[:, NOPE:], cos, sin), x, qnw_[:, NOPE:], rms(x, qnw_[:, NOPE:])[1])
            dq_pe.append(dx)
            dqnw_p = dqnw_p + dw
            dk = dk_ref[h]
            x = kvraw[:, h * 256:h * 256 + NOPE]
            dx, dw = rms_bwd(dk[:, :NOPE], x, knw_[:, :NOPE], rms(x, knw_[:, :NOPE])[1])
            dknw_n = dknw_n + dw
            dkpe = dkpe + dk[:, NOPE:]
            dkv_parts += [dx, dv_ref[h]]
        dx, dknw_p = rms_bwd(_rope_bwd(dkpe, cos, sin), kpe_raw, knw_[:, NOPE:], rms(kpe_raw, knw_[:, NOPE:])[1])
        dkpe_ref[...] = jnp.concatenate([dx, jnp.zeros((tm, LANES - ROPE), F32)], axis=-1).astype(MXU_DTYPE)
        dqraw = jnp.concatenate(dq_nope + dq_pe, axis=-1).astype(MXU_DTYPE)
        dkvraw = jnp.concatenate(dkv_parts, axis=-1).astype(MXU_DTYPE)
        dqraw_ref[...] = dqraw
        dkvraw_ref[...] = dkvraw
        dx, dw = rms_bwd(_mm_nt(dqraw, uq_ref[...]), ql, wq_ref[...], rq)
        dql_ref[...] = dx.astype(MXU_DTYPE)
        dwq_ref[...] += dw
        dx, dw = rms_bwd(_mm_nt(dkvraw, ukv_ref[...]), kvl, wkv_ref[...], rkv)
        dkvl_ref[...] = dx.astype(MXU_DTYPE)
        dwkv_ref[...] += dw
        dqnw_ref[...] += jnp.concatenate([dqnw_n, dqnw_p], axis=-1)
        dknw_ref[...] += jnp.concatenate([dknw_n, dknw_p], axis=-1)

    full = lambda a: pl.BlockSpec(a.shape, lambda i: (0,) * a.ndim)
    rows = lambda n: pl.BlockSpec((tm, n), lambda i: (i, 0))
    const = lambda n: pl.BlockSpec((1, n), lambda i: (0, 0))
    NQ, NKV = w_uq_p.shape[1], w_ukv.shape[1]
    return _call_beside(
        body, transfer, grid=(T // tm,), name="mla_pre_bwd", scratch_shapes=[], semantics=("arbitrary",),
        args=(proj, proj, proj, cosf, sinf, w_qln, w_kvln, w_uq_p, w_ukv, qnw, knw, dq4, dk4, dv4),
        in_specs=[pl.BlockSpec((tm, 256), lambda i: (i, P_QLAT // 256)),
                  pl.BlockSpec((tm, 256), lambda i: (i, P_KVLAT // 256)),
                  pl.BlockSpec((tm, 128), lambda i: (i, P_KPE // 128)),
                  rows(ROPE), rows(ROPE),
                  full(w_qln), full(w_kvln), full(w_uq_p), full(w_ukv), full(qnw), full(knw),
                  pl.BlockSpec((H, tm, QK_DIM), lambda i: (0, i, 0)),
                  pl.BlockSpec((H, tm, QK_DIM), lambda i: (0, i, 0)),
                  pl.BlockSpec((H, tm, V_DIM), lambda i: (0, i, 0))],
        out_specs=[rows(Q_LORA), rows(KV_LORA), rows(LANES), rows(NQ), rows(NKV), rows(Q_LORA), rows(KV_LORA),
                   const(Q_LORA), const(KV_LORA), const(QK_DIM), const(QK_DIM)],
        out_shape=[SDS((T, Q_LORA), MXU_DTYPE), SDS((T, KV_LORA), MXU_DTYPE), SDS((T, LANES), MXU_DTYPE),
                   SDS((T, NQ), MXU_DTYPE), SDS((T, NKV), MXU_DTYPE),
                   SDS((T, Q_LORA), MXU_DTYPE), SDS((T, KV_LORA), MXU_DTYPE),
                   SDS((1, Q_LORA), F32), SDS((1, KV_LORA), F32), SDS((1, QK_DIM), F32), SDS((1, QK_DIM), F32)])


def _in_proj_bwd(dc, conv_w, dgz, dql, dkvl, dkpe, dgab, w_in_p, dh, x2, w_an, S):
    T, D = x2.shape
    N = w_in_p.shape[1]
    C3 = dc.shape[1]
    tm = min(512, S)
    assert S % tm == 0 and T % tm == 0, "a token tile must not straddle two sequences"
    tiles_per_seq = S // tm
    nblk = T // SUBLANES

    def body(dc_ref, nxt_ref, cw_ref, b_ref, c_ref, d_ref, e_ref, f_ref, w_ref, dh_ref, x_ref, wn_ref,
             dx_ref, dp_ref, dwn_ref):
        i = pl.program_id(0)

        @pl.when(i == 0)
        def _():
            dwn_ref[...] = jnp.zeros_like(dwn_ref)

        nxt = jnp.where(i % tiles_per_seq == tiles_per_seq - 1, 0.0, nxt_ref[...])
        dcv, cw = dc_ref[...], cw_ref[...]
        du = cw[3:4] * dcv
        for j in range(1, CONV_W):
            du = du + cw[3 - j:4 - j] * _shift_up(dcv, nxt, j)
        dp = jnp.concatenate([du.astype(MXU_DTYPE), b_ref[...], c_ref[...], d_ref[...], e_ref[...], f_ref[...]],
                             axis=-1).astype(MXU_DTYPE)
        dp_ref[...] = dp
        x = x_ref[...]
        _, r = _rms(x, wn_ref[...])
        dx, dw = _rms_bwd(_mm_nt(dp, w_ref[...]), x, wn_ref[...], r)
        dx_ref[...] = dh_ref[...] + dx
        dwn_ref[...] += dw

    rows = lambda n: pl.BlockSpec((tm, n), lambda i: (i, 0))
    return pl.pallas_call(
        body, grid=(T // tm,), name="in_proj_bwd",
        in_specs=[rows(C3),
                  pl.BlockSpec((SUBLANES, C3), lambda i: (jnp.minimum((i + 1) * (tm // SUBLANES), nblk - 1), 0)),
                  pl.BlockSpec((CONV_W, C3), lambda i: (0, 0)),
                  rows(dgz.shape[1]), rows(dql.shape[1]), rows(dkvl.shape[1]),
                  rows(dkpe.shape[1]), rows(dgab.shape[1]),
                  pl.BlockSpec((D, N), lambda i: (0, 0)), rows(D), rows(D), pl.BlockSpec((1, D), lambda i: (0, 0))],
        out_specs=[rows(D), rows(N), pl.BlockSpec((1, D), lambda i: (0, 0))],
        out_shape=[SDS((T, D), F32), SDS((T, N), MXU_DTYPE), SDS((1, D), F32)],
        compiler_params=_params(("arbitrary",)),
    )(dc, dc, conv_w, dgz, dql, dkvl, dkpe, dgab, w_in_p, dh, x2, w_an)


def _relu_squared(t):
    r = jnp.maximum(t.astype(F32), 0.0)
    return (r * r).astype(MXU_DTYPE)


def _wgrad(a, b, name, column_shards=False, a_map=None):
    T, M = a.shape
    N = b.shape[1]
    tM = _divisor_tile(M, 1024)
    tN = N // N_DEV if column_shards else _divisor_tile(N, 1536)
    tk = min(T, 2048)
    nk = T // tk

    def body(a_ref, b_ref, o_ref, acc):
        k = pl.program_id(2)

        @pl.when(k == 0)
        def _():
            acc[...] = jnp.zeros_like(acc)

        acc[...] += _mm_tn(a_ref[...] if a_map is None else a_map(a_ref[...]), b_ref[...])

        @pl.when(k == nk - 1)
        def _():
            o_ref[...] = acc[...].astype(WIRE_DTYPE).reshape(o_ref.shape)

    if column_shards:
        out_spec, out_shape = pl.BlockSpec((1, tM, tN), lambda i, j, k: (j, i, 0)), SDS((N_DEV, M, tN), WIRE_DTYPE)
    else:
        out_spec, out_shape = pl.BlockSpec((tM, tN), lambda i, j, k: (i, j)), SDS((M, N), WIRE_DTYPE)
    return pl.pallas_call(
        body, grid=(M // tM, N // tN, nk), name=name,
        in_specs=[pl.BlockSpec((tk, tM), lambda i, j, k: (k, i)), pl.BlockSpec((tk, tN), lambda i, j, k: (k, j))],
        out_specs=out_spec, out_shape=out_shape,
        scratch_shapes=[pltpu.VMEM((tM, tN), F32)],
        compiler_params=_params(("arbitrary", "arbitrary", "arbitrary")),
    )(a, b)


def _adamw(g, w, m, v):
    m = ADAM_B1 * m + (1.0 - ADAM_B1) * g
    v = ADAM_B2 * v + (1.0 - ADAM_B2) * jnp.square(g)
    m_hat = m / (1.0 - ADAM_B1 ** ADAM_STEP)
    v_hat = v / (1.0 - ADAM_B2 ** ADAM_STEP)
    return -ADAM_LR * (m_hat / (jnp.sqrt(v_hat) + ADAM_EPS) + ADAM_WD * w), m, v


def _reduce_adamw(parts, w, m, v, name):
    R, C = w.shape
    _, Rp, Cp = parts.shape
    tr = min(R, 256)
    tp = tr if Rp == R else Rp

    def body(p_ref, w_ref, m_ref, v_ref, g_ref, d_ref, nm_ref, nv_ref):
        g = p_ref[0].astype(F32)
        for s in range(1, N_DEV):
            g = g + p_ref[s].astype(F32)
        g = g[:tr, :C]
        g_ref[...] = g
        d_ref[...], nm_ref[...], nv_ref[...] = _adamw(g, w_ref[...], m_ref[...], v_ref[...])

    spec = pl.BlockSpec((tr, C), lambda i: (i, 0))
    return pl.pallas_call(
        body, grid=(R // tr,), name=name,
        in_specs=[pl.BlockSpec((N_DEV, tp, Cp), lambda i: (0, i, 0)), spec, spec, spec],
        out_specs=[spec] * 4, out_shape=[SDS((R, C), F32)] * 4,
        compiler_params=_params(("arbitrary",)),
    )(parts, w, m, v)


SMALL_ROWS, SMALL_COLS = 16, 1024
SMALL_LAYOUT = (
    ("attn_norm_w", 0, 1, 1024, 1024), ("mlp_norm_w", 1, 1, 1024, 1024), ("q_lat_norm_w", 2, 1, 256, 256),
    ("kv_lat_norm_w", 3, 1, 256, 256), ("q_norm_w", 4, 1, 192, 192), ("k_norm_w", 5, 1, 192, 192),
    ("mla_out_norm_w", 6, 4, 128, 128), ("a_log", 10, 1, 128, 4), ("dt_bias", 11, 1, 128, 4),
    ("gdn_norm_w", 12, 1, 128, 128))
LOSS_ENTRY = ("loss", 13, 1, 128, 128)


def _adamw_replicated(parts, ws, ms, vs):
    n = len(SMALL_LAYOUT)

    def body(*refs):
        p_ref = refs[0]
        w_refs, m_refs, v_refs = refs[1:1 + n], refs[1 + n:1 + 2 * n], refs[1 + 2 * n:1 + 3 * n]
        outs = refs[1 + 3 * n:]
        s = p_ref[0]
        for d in range(1, N_DEV):
            s = s + p_ref[d]
        for i, (_, r0, nr, _, pw) in enumerate(SMALL_LAYOUT):
            g = s[r0:r0 + nr, :pw]
            outs[i][...] = g
            outs[n + i][...], outs[2 * n + i][...], outs[3 * n + i][...] = _adamw(
                g, w_refs[i][...], m_refs[i][...], v_refs[i][...])
        _, r0, nr, gw, _ = LOSS_ENTRY
        outs[4 * n][...] = s[r0:r0 + nr, :gw]

    res = pl.pallas_call(
        body, name="adamw_replicated",
        out_shape=[SDS(w.shape, F32) for w in ws] * 4 + [SDS((1, LANES), F32)],
        compiler_params=_params(),
    )(parts, *ws, *ms, *vs)
    return [res[k * n:(k + 1) * n] for k in range(4)], res[4 * n][0, 0]


COPIES_PER_ARRAY = N_DEV - 1


def _two_level_gather(srcs, outs, send_sems, recv_sems, local_sems=None, stage="all"):
    mx, my, mc = lax.axis_index("x"), lax.axis_index("y"), lax.axis_index("c")
    me, sibling = (mx, my, mc), (mx, my, 1 - mc)
    chips = [(1 - mx, my), (mx, 1 - my), (1 - mx, 1 - my)]
    arrays = range(len(srcs))

    def copy(a, k, block, to, src=None):
        px, py, pc = block
        slot = outs[a].at[4 * px + 2 * py + pc]
        sem = a * COPIES_PER_ARRAY + k
        return pltpu.make_async_remote_copy(
            src_ref=slot if src is None else src, dst_ref=slot,
            send_sem=send_sems.at[sem], recv_sem=recv_sems.at[sem], device_id=to, device_id_type=MESH_ID)

    mine = [] if local_sems is None else [
        pltpu.make_async_copy(srcs[a], outs[a].at[4 * mx + 2 * my + mc], local_sems.at[a]) for a in arrays]
    first = []
    for a in arrays:
        first.append(copy(a, 0, me, sibling, src=srcs[a]))
        first += [copy(a, 1 + j, me, (*chip, mc), src=srcs[a]) for j, chip in enumerate(chips)]
    forwards = [copy(a, 4 + j, (*chip, mc), sibling) for j, chip in enumerate(chips) for a in arrays]
    if stage in ("all", "start"):
        for cp in mine + first:
            cp.start()
    if stage in ("all", "forward"):
        for j, chip in enumerate(chips):
            for a in arrays:
                copy(a, 1 + j, (*chip, mc), me).wait_recv()
                forwards[j * len(srcs) + a].start()
    if stage in ("all", "finish"):
        for a in arrays:
            copy(a, 0, sibling, me).wait_recv()
        for j, chip in enumerate(chips):
            for a in arrays:
                copy(a, 4 + j, (*chip, 1 - mc), me).wait_recv()
        for cp in first + forwards:
            cp.wait_send()
        for cp in mine:
            cp.wait()


def _comm_scratch(n):
    return [pltpu.SemaphoreType.DMA((n * COPIES_PER_ARRAY,)), pltpu.SemaphoreType.DMA((n * COPIES_PER_ARRAY,)),
            pltpu.SemaphoreType.DMA((n,))]


def _any_specs(n):
    return [pl.BlockSpec(memory_space=pl.ANY)] * n


def _gather_weights(shards):
    n = len(shards)

    def body(*refs):
        _two_level_gather(refs[:n], refs[n:2 * n], *refs[2 * n:])

    return pl.pallas_call(
        body, name="gather_weights",
        out_shape=[SDS((N_DEV,) + s.shape, s.dtype) for s in shards],
        in_specs=_any_specs(n), out_specs=_any_specs(n), scratch_shapes=_comm_scratch(n),
    )(*shards)


def _gather_small_grads(gs, loss_lanes):
    gs = list(gs) + [loss_lanes]
    n = len(gs)

    def body(*refs):
        g_refs, out_ref = refs[:n], refs[n]
        tile, send_sems, recv_sems = refs[n + 1:]
        tile[...] = jnp.zeros_like(tile)
        for (_, r0, nr, gw, _), g in zip(SMALL_LAYOUT + (LOSS_ENTRY,), g_refs):
            tile[r0:r0 + nr, 0:gw] = g[...]
        me = 4 * lax.axis_index("x") + 2 * lax.axis_index("y") + lax.axis_index("c")
        out_ref[me] = tile[...]
        _two_level_gather([tile], [out_ref], send_sems, recv_sems)

    return pl.pallas_call(
        body, name="gather_small_grads",
        out_shape=SDS((N_DEV, SMALL_ROWS, SMALL_COLS), F32),
        in_specs=[pl.BlockSpec(memory_space=pltpu.VMEM)] * n,
        out_specs=pl.BlockSpec(memory_space=pltpu.VMEM),
        scratch_shapes=[pltpu.VMEM((SMALL_ROWS, SMALL_COLS), F32),
                        pltpu.SemaphoreType.DMA((COPIES_PER_ARRAY,)), pltpu.SemaphoreType.DMA((COPIES_PER_ARRAY,))],
    )(*gs)


def _exchange_grads(slabs):
    n = len(slabs)

    def body(*refs):
        _exchange(refs[:n], refs[n:2 * n], *refs[2 * n:])

    return pl.pallas_call(
        body, name="exchange_grads",
        out_shape=[SDS(s.shape, s.dtype) for s in slabs],
        in_specs=_any_specs(n), out_specs=_any_specs(n), scratch_shapes=_comm_scratch(n),
    )(*slabs)


class _Transfer:
    def __init__(self, kind, arrays):
        self.kind, self.arrays, self.n = kind, list(arrays), len(arrays)

    def out_shapes(self):
        if self.kind == "gather":
            return [SDS((N_DEV,) + a.shape, a.dtype) for a in self.arrays]
        return [SDS(a.shape, a.dtype) for a in self.arrays]

    def run(self, srcs, outs, sems, stage):
        fn = _two_level_gather if self.kind == "gather" else _exchange
        fn(srcs, outs, *sems, stage=stage)


def _call_beside(body, transfer, *, grid, in_specs, out_specs, out_shape, scratch_shapes, name, semantics, args):
    if transfer is None:
        res = pl.pallas_call(body, grid=grid, in_specs=in_specs, out_specs=out_specs, out_shape=out_shape,
                             scratch_shapes=scratch_shapes, name=name, compiler_params=_params(semantics))(*args)
        return list(res), []
    n_in, n_out, n_s, n = len(in_specs), len(out_specs), len(scratch_shapes), transfer.n
    total = functools.reduce(lambda a, b: a * b, grid, 1)

    def wrapped(*refs):
        ins, refs = refs[:n_in], refs[n_in:]
        t_in, refs = refs[:n], refs[n:]
        outs, refs = refs[:n_out], refs[n_out:]
        t_out, refs = refs[:n], refs[n:]
        scratch, sems = refs[:n_s], refs[n_s:]
        first = functools.reduce(jnp.logical_and, [pl.program_id(i) == 0 for i in range(len(grid))])
        last = functools.reduce(jnp.logical_and, [pl.program_id(i) == g - 1 for i, g in enumerate(grid)])

        @pl.when(first)
        def _():
            transfer.run(t_in, t_out, sems, "start")

        step = functools.reduce(lambda acc, ig: acc * ig[1] + pl.program_id(ig[0]), enumerate(grid), 0)

        @pl.when(step == (3 * total) // 4)
        def _():
            transfer.run(t_in, t_out, sems, "forward")

        body(*ins, *outs, *scratch)

        @pl.when(last)
        def _():
            transfer.run(t_in, t_out, sems, "finish")

    res = pl.pallas_call(
        wrapped, grid=grid, in_specs=list(in_specs) + _any_specs(n), out_specs=list(out_specs) + _any_specs(n),
        out_shape=list(out_shape) + transfer.out_shapes(), scratch_shapes=list(scratch_shapes) + _comm_scratch(n),
        name=name, compiler_params=_params(semantics))(*args, *transfer.arrays)
    return list(res[:n_out]), list(res[n_out:])


EXCHANGE_FLIPS = ((0, 0, 1), (1, 0, 0), (0, 1, 0), (1, 1, 0), (1, 0, 1), (0, 1, 1), (1, 1, 1))


def _exchange(srcs, outs, send_sems, recv_sems, local_sems, stage="all"):
    mx, my, mc = lax.axis_index("x"), lax.axis_index("y"), lax.axis_index("c")
    arrays = range(len(srcs))
    copies = [pltpu.make_async_copy(srcs[a].at[4 * mx + 2 * my + mc], outs[a].at[N_DEV - 1], local_sems.at[a])
              for a in arrays]
    for k, (fx, fy, fc) in enumerate(EXCHANGE_FLIPS):
        px = 1 - mx if fx else mx
        py = 1 - my if fy else my
        pc = 1 - mc if fc else mc
        for a in arrays:
            sem = a * COPIES_PER_ARRAY + k
            copies.append(pltpu.make_async_remote_copy(
                src_ref=srcs[a].at[4 * px + 2 * py + pc], dst_ref=outs[a].at[k],
                send_sem=send_sems.at[sem], recv_sem=recv_sems.at[sem],
                device_id=(px, py, pc), device_id_type=MESH_ID))
    if stage in ("all", "start"):
        for cp in copies:
            cp.start()
    if stage in ("all", "finish"):
        for cp in copies:
            cp.wait()


def _w_in_to_padded(w):
    z = lambda n: jnp.zeros((w.shape[0], n), w.dtype)
    return jnp.concatenate([w[:, O_GQKV:O_GZ], w[:, O_GZ:O_GAB], w[:, O_QLAT:O_KVLAT], w[:, O_KVLAT:O_KPE],
                            w[:, O_KPE:O_GQKV], z(P_GAB - P_KPE - ROPE), w[:, O_GAB:O_END],
                            z(P_WIDTH - P_GAB - (O_END - O_GAB))], axis=1)


def _w_in_from_padded(wp):
    return jnp.concatenate([wp[:, P_QLAT:P_QLAT + 256], wp[:, P_KVLAT:P_KVLAT + 256], wp[:, P_KPE:P_KPE + ROPE],
                            wp[:, P_GQKV:P_GZ], wp[:, P_GZ:P_QLAT], wp[:, P_GAB:P_GAB + (O_END - O_GAB)]], axis=1)


W_IN_SHARD_COLS = (O_END - O_QLAT) // N_DEV


def _w_in_shards_to_padded(stack):
    _, R, Cw = stack.shape
    tr = min(R, 256)

    def body(s_ref, o_ref):
        full = jnp.concatenate([s_ref[d].astype(F32)[:, :W_IN_SHARD_COLS] for d in range(N_DEV)], axis=-1)
        o_ref[...] = _w_in_to_padded(full).astype(o_ref.dtype)

    return pl.pallas_call(
        body, grid=(R // tr,), name="w_in_to_padded",
        in_specs=[pl.BlockSpec((N_DEV, tr, Cw), lambda i: (0, i, 0))],
        out_specs=pl.BlockSpec((tr, P_WIDTH), lambda i: (i, 0)),
        out_shape=SDS((R, P_WIDTH), stack.dtype), compiler_params=_params(("arbitrary",)),
    )(stack)


def _w_in_padded_to_slabs(gp, wire_cols):
    R = gp.shape[0]
    tr = min(R, 256)

    def body(g_ref, o_ref):
        orig = _w_in_from_padded(g_ref[...].astype(F32))
        for d in range(N_DEV):
            piece = orig[:, d * W_IN_SHARD_COLS:(d + 1) * W_IN_SHARD_COLS]
            o_ref[d] = _pad2(piece, tr, wire_cols).astype(o_ref.dtype)

    return pl.pallas_call(
        body, grid=(R // tr,), name="w_in_to_slabs",
        in_specs=[pl.BlockSpec((tr, P_WIDTH), lambda i: (i, 0))],
        out_specs=pl.BlockSpec((N_DEV, tr, wire_cols), lambda i: (0, i, 0)),
        out_shape=SDS((N_DEV, R, wire_cols), gp.dtype), compiler_params=_params(("arbitrary",)),
    )(gp)


def _w_uq_to_headsplit(w):
    w3 = w.reshape(w.shape[0], MLA_HEADS, QK_DIM)
    return jnp.concatenate([w3[:, :, :NOPE].reshape(w.shape[0], -1), w3[:, :, NOPE:].reshape(w.shape[0], -1)], axis=1)


def _w_uq_from_headsplit(wp):
    n = wp[:, :MLA_HEADS * NOPE].reshape(wp.shape[0], MLA_HEADS, NOPE)
    p = wp[:, MLA_HEADS * NOPE:].reshape(wp.shape[0], MLA_HEADS, ROPE)
    return jnp.concatenate([n, p], axis=2).reshape(wp.shape[0], -1)


def _lane_vec(v4):
    return jnp.pad(v4.reshape(1, -1), ((0, 0), (0, LANES - v4.shape[-1])))


def _local_step(x, positions, target, attn_norm_w, w_in, q_lat_norm_w, w_uq, kv_lat_norm_w, w_ukv, q_norm_w,
                k_norm_w, mla_out_norm_w, conv_w, a_log, dt_bias, gdn_norm_w, w_out, mlp_norm_w, w_up, w_down,
                late_shards=None, exchange=False):
    B, S, D = x.shape
    T = B * S
    x2 = x.reshape(T, D)
    t2 = target.reshape(T, D)
    half = ROPE // 2
    inv_freq = ROPE_THETA ** (-jnp.arange(half, dtype=F32) / half)
    ang = positions.reshape(T, 1).astype(F32) * inv_freq
    cosf = jnp.concatenate([jnp.cos(ang)] * 2, axis=-1)
    sinf = jnp.concatenate([jnp.sin(ang)] * 2, axis=-1)
    w_in_p = w_in
    w_uq_p = _w_uq_to_headsplit(w_uq)
    alog_l, dt_l = _lane_vec(a_log), _lane_vec(dt_bias)
    w_an, w_qln, w_kvln, qnw, knw, w_mn, gdn_w = (
        attn_norm_w, q_lat_norm_w, kv_lat_norm_w, q_norm_w, k_norm_w, mlp_norm_w, gdn_norm_w)

    proj, xn, qg, kg, vg, gates = _in_proj(x2, w_an, w_in_p, conv_w, alog_l, dt_l, S)
    def gathering(shards):
        return None if late_shards is None else _Transfer("gather", shards)

    (q4, k4, v4), late = _mla_pre(proj, cosf, sinf, w_qln, w_kvln, w_uq_p, w_ukv, qnw, knw,
                                  gathering(late_shards and late_shards[:1]))
    if late:
        w_out = late[0].reshape(-1, D)
    (o_mla, lse), late = _attn_fwd(q4, k4, v4, B, S, gathering(late_shards and late_shards[2:]))
    if late:
        w_down = late[0].reshape(-1, D)
    (o_gdn, states, ainv, u4, w4), late = _gdn_fwd(qg, kg, vg, gates, B, S,
                                                   gathering(late_shards and late_shards[1:2]))
    if late:
        w_up = late[0]
    h2, mix = _mix_out(o_mla, o_gdn, proj, x2, mla_out_norm_w, gdn_w, w_out)
    up, hn, dy, sq, dyb = _mlp_fwd(h2, w_mn, w_up, w_down, t2)
    loss = (0.5 / D) * jnp.sum(sq[:, 0, 0])

    first = ("w_down",)
    second = ("w_out",)
    third = ("w_up", "w_uq", "w_ukv")
    mats = dict(w_down=_wgrad(up, dyb, "wgrad_down", a_map=_relu_squared))

    def sending(names):
        return _Transfer("exchange", [_slabs(n, mats[n]) for n in names]) if exchange else None

    (dh, dhb, dup, d_mlp_norm), got = _mlp_bwd(dy, dyb, up, h2, w_mn, w_up, w_down, sending(first))
    mats.update(zip(first, got))
    mats.update(w_up=_wgrad(hn, dup, "wgrad_up", column_shards=True))
    do_mla, do_gdn, dz, d_mla_w, d_gdn_w, delta = _mix_bwd(dhb, o_mla, o_gdn, proj, mla_out_norm_w, gdn_w, w_out)
    mats.update(w_out=_wgrad(mix, dhb, "wgrad_out"))
    (dq4, dk4, dv4), got = _attn_bwd(q4, k4, v4, do_mla, delta, lse, B, S, sending(second))
    mats.update(zip(second, got))
    (dql, dkvl, dkpe, dqraw, dkvraw, qn, kvn, d_wqln, d_wkvln, d_qnw, d_knw), _ = _mla_pre_bwd(
        proj, cosf, sinf, w_qln, w_kvln, w_uq_p, w_ukv, qnw, knw, dq4, dk4, dv4)
    mats.update(w_uq=_wgrad(qn, dqraw, "wgrad_uq"), w_ukv=_wgrad(kvn, dkvraw, "wgrad_ukv"))
    (dqg, dkg, dvg, dgb4), got = _gdn_bwd(qg, kg, vg, gates, states, ainv, u4, w4, do_gdn, B, S, sending(third))
    mats.update(zip(third, got))
    dc, dgab, g_conv, d_alog, d_dt = _gdn_pre_bwd(proj, conv_w, alog_l, dt_l, dqg, dkg, dvg, dgb4, S)
    grad_x2, dproj, d_attn_norm = _in_proj_bwd(dc, conv_w, dz, dql, dkvl, dkpe, dgab, w_in_p, dh, x2, w_an, S)
    mats.update(w_in=_wgrad(xn, dproj, "wgrad_in"), conv_w=g_conv)
    if exchange:
        last = ("w_in", "conv_w")
        mats.update(zip(last, _exchange_grads([_slabs(n, mats[n]) for n in last])))
    small = dict(attn_norm_w=d_attn_norm, mlp_norm_w=d_mlp_norm, q_lat_norm_w=d_wqln, kv_lat_norm_w=d_wkvln,
                 q_norm_w=d_qnw, k_norm_w=d_knw, mla_out_norm_w=d_mla_w, a_log=d_alog, dt_bias=d_dt,
                 gdn_norm_w=d_gdn_w)
    return loss, grad_x2.reshape(B, S, D), mats, [small[n] for n, *_ in SMALL_LAYOUT]


BIG = ("w_in", "w_uq", "w_ukv", "conv_w", "w_out", "w_up", "w_down")
ALL_W = ("attn_norm_w", "w_in", "q_lat_norm_w", "w_uq", "kv_lat_norm_w", "w_ukv", "q_norm_w", "k_norm_w",
         "mla_out_norm_w", "conv_w", "a_log", "dt_bias", "gdn_norm_w", "w_out", "mlp_norm_w", "w_up", "w_down")
WIRE_SHAPE = {"w_in": (1024, 384), "w_uq": (256, 128), "conv_w": (16, 256)}


def _pad2(a, rows, cols):
    return jnp.pad(a, [(0, 0)] * (a.ndim - 2) + [(0, rows - a.shape[-2]), (0, cols - a.shape[-1])])


def _cols_to_full(stack, cols):
    return jnp.moveaxis(stack[:, :, :cols], 0, 1).reshape(stack.shape[1], N_DEV * cols)


def _full_to_cols(full, wire_cols):
    r, n = full.shape
    return _pad2(jnp.moveaxis(full.reshape(r, N_DEV, n // N_DEV), 1, 0), r, wire_cols)


def _slabs(name, g):
    if name == "w_in":
        return _w_in_padded_to_slabs(g, WIRE_SHAPE["w_in"][1])
    if name == "w_uq":
        return _full_to_cols(_w_uq_from_headsplit(g), WIRE_SHAPE["w_uq"][1])
    if name == "w_ukv":
        return _full_to_cols(g, g.shape[1] // N_DEV)
    if name == "conv_w":
        return _pad2(_full_to_cols(g.astype(WIRE_DTYPE), g.shape[1] // N_DEV), *WIRE_SHAPE["conv_w"])
    if name == "w_up":
        return g
    return g.reshape(N_DEV, -1, g.shape[-1])


def kernel(x, positions, attn_norm_w, w_in, q_lat_norm_w, w_uq, kv_lat_norm_w, w_ukv, q_norm_w, k_norm_w, mla_out_norm_w, conv_w, a_log, dt_bias, gdn_norm_w, w_out, mlp_norm_w, w_up, w_down, loss_target, m_attn_norm_w, m_w_in, m_q_lat_norm_w, m_w_uq, m_kv_lat_norm_w, m_w_ukv, m_q_norm_w, m_k_norm_w, m_mla_out_norm_w, m_conv_w, m_a_log, m_dt_bias, m_gdn_norm_w, m_w_out, m_mlp_norm_w, m_w_up, m_w_down, v_attn_norm_w, v_w_in, v_q_lat_norm_w, v_w_uq, v_kv_lat_norm_w, v_w_ukv, v_q_norm_w, v_k_norm_w, v_mla_out_norm_w, v_conv_w, v_a_log, v_dt_bias, v_gdn_norm_w, v_w_out, v_mlp_norm_w, v_w_up, v_w_down):
    env = dict(locals())
    W = {n: env[n][0] for n in ALL_W}
    Mo = {n: env["m_" + n][0] for n in ALL_W}
    Vo = {n: env["v_" + n][0] for n in ALL_W}

    two_d = lambda a: a.reshape(1, -1) if a.ndim == 1 else a
    D = x.shape[-1]

    s_in, s_uq, s_ukv, s_conv = _gather_weights([
        _pad2(W["w_in"].astype(WIRE_DTYPE), *WIRE_SHAPE["w_in"]),
        _pad2(W["w_uq"].astype(WIRE_DTYPE), *WIRE_SHAPE["w_uq"]),
        W["w_ukv"].astype(WIRE_DTYPE), _pad2(W["conv_w"], *WIRE_SHAPE["conv_w"])])
    late = [W["w_out"].astype(WIRE_DTYPE), W["w_up"].astype(WIRE_DTYPE), W["w_down"].astype(WIRE_DTYPE)]

    loss, grad_x, parts, gs = _local_step(
        x, positions, loss_target, two_d(W["attn_norm_w"]), _w_in_shards_to_padded(s_in),
        two_d(W["q_lat_norm_w"]), _cols_to_full(s_uq, W["w_uq"].shape[1]), two_d(W["kv_lat_norm_w"]),
        _cols_to_full(s_ukv, W["w_ukv"].shape[1]), two_d(W["q_norm_w"]), two_d(W["k_norm_w"]),
        W["mla_out_norm_w"], _cols_to_full(s_conv[:, :CONV_W], W["conv_w"].shape[1]), two_d(W["a_log"]),
        two_d(W["dt_bias"]), two_d(W["gdn_norm_w"]), None, two_d(W["mlp_norm_w"]), None, None,
        late_shards=late, exchange=True)
    done = {n: _reduce_adamw(parts[n], W[n], Mo[n], Vo[n], "adamw_" + n) for n in BIG}
    names = [n for n, *_ in SMALL_LAYOUT]
    tiles = _gather_small_grads(gs, jnp.full((1, LANES), loss, F32))
    small, loss = _adamw_replicated(tiles, [two_d(W[n]) for n in names], [two_d(Mo[n]) for n in names],
                                    [two_d(Vo[n]) for n in names])
    for i, n in enumerate(names):
        done[n] = [small[kind][i] for kind in range(4)]
    res = [done[n][kind].reshape(env[n].shape) for kind in range(4) for n in ALL_W]
    return (loss, grad_x, *res)
```

```python
import functools

import jax
import jax.numpy as jnp
from jax import lax
from jax.experimental import pallas as pl
from jax.experimental.pallas import tpu as pltpu

F32 = jnp.float32
MXU_DTYPE = jnp.bfloat16
WIRE_DTYPE = jnp.bfloat16
SDS = jax.ShapeDtypeStruct
HIGHEST = lax.Precision.HIGHEST
MESH_ID = pl.DeviceIdType.MESH

D_MODEL = 1024
MLA_HEADS = 4
Q_LORA = 256
KV_LORA = 256
NOPE = 128
ROPE = 64
QK_DIM = NOPE + ROPE
V_DIM = 128
ROPE_THETA = 10000.0
GDN_HEADS = 4
GDN_DIM = 128
GDN_WIDTH = GDN_HEADS * GDN_DIM
CONV_W = 4
CHUNK = 64
D_FF = 4 * D_MODEL
EPS = 1e-6
ATT_SCALE = QK_DIM ** -0.5
GDN_QSCALE = GDN_DIM ** -0.5
N_DEV = 8
ATTN_BLOCK = 512
ATTN_CHAINS = 2
MLP_FWD_SHARDS = 4
MLP_BWD_SHARDS = 4

ADAM_LR = 0.001
ADAM_B1 = 0.9
ADAM_B2 = 0.999
ADAM_EPS = 1e-08
ADAM_WD = 0.01
ADAM_STEP = 10

LANES = 128
SUBLANES = 8
VMEM_LIMIT = 60 * 1024 * 1024

P_GQKV, P_GZ, P_QLAT, P_KVLAT, P_KPE, P_GAB = 0, 1536, 2048, 2304, 2560, 2688
P_WIDTH = 2816
O_QLAT, O_KVLAT, O_KPE, O_GQKV, O_GZ, O_GAB, O_END = 0, 256, 512, 576, 2112, 2624, 2632


def _params(sem=None, vmem=VMEM_LIMIT):
    kw = dict(vmem_limit_bytes=vmem)
    if sem is not None:
        kw["dimension_semantics"] = sem
    return pltpu.CompilerParams(**kw)


def _mm(a, b):
    return jnp.dot(a.astype(MXU_DTYPE), b.astype(MXU_DTYPE), preferred_element_type=F32)


def _mm_nt(a, b):
    return lax.dot_general(a.astype(MXU_DTYPE), b.astype(MXU_DTYPE), (((1,), (1,)), ((), ())),
                           preferred_element_type=F32)


def _mm_tn(a, b):
    return lax.dot_general(a.astype(MXU_DTYPE), b.astype(MXU_DTYPE), (((0,), (0,)), ((), ())),
                           preferred_element_type=F32)


def _split(a):
    hi = a.astype(MXU_DTYPE)
    return hi, (a - hi.astype(F32)).astype(MXU_DTYPE)


def _mm_split(a, b):
    (ah, al), (bh, bl) = a, b
    dot = lambda x, y: jnp.dot(x, y, preferred_element_type=F32)
    if MXU_DTYPE == F32:
        return dot(ah, bh)
    return dot(ah, bh) + dot(ah, bl) + dot(al, bh)


def _mm_exact(a, b):
    return _mm_split(_split(a), _split(b))


def _row_sum(v, on_mxu=False):
    if not on_mxu:
        return jnp.sum(v, axis=-1, keepdims=True)
    d = v.shape[-1]
    ones = jnp.ones((d, LANES), MXU_DTYPE)
    s = sum(jnp.dot(p, ones, preferred_element_type=F32) for p in _split(v))
    return s[:, :d] if d <= LANES else jnp.tile(s, (1, d // LANES))


def _rms(x, w, on_mxu=False):
    r = lax.rsqrt(_row_sum(x * x, on_mxu) * (1.0 / x.shape[-1]) + EPS)
    return x * r * w, r


def _rms_bwd(dy, x, w, r, on_mxu=False):
    xh = x * r
    dyw = dy * w
    dx = r * (dyw - xh * (_row_sum(dyw * xh, on_mxu) * (1.0 / x.shape[-1])))
    dw = jnp.sum(dy * xh, axis=0, keepdims=True)
    return dx, dw


def _l2n(x, scale):
    return x * (lax.rsqrt(_row_sum(x * x) + EPS) * scale)


def _l2n_bwd(dy, x, scale):
    r = lax.rsqrt(_row_sum(x * x) + EPS)
    xh = x * r
    return (scale * r) * (dy - xh * _row_sum(dy * xh))


def _rot(t):
    return jnp.concatenate([-t[:, ROPE // 2:], t[:, :ROPE // 2]], axis=-1)


def _rot_t(t):
    return jnp.concatenate([t[:, ROPE // 2:], -t[:, :ROPE // 2]], axis=-1)


def _rope(t, cos, sin):
    return t * cos + _rot(t) * sin


def _rope_bwd(d, cos, sin):
    return d * cos + _rot_t(d * sin)


def _sigmoid(x):
    return jax.nn.sigmoid(x)


def _shift_down(x, halo, j):
    if j == 0:
        return x
    xr = pltpu.roll(x, j, 0)
    hr = pltpu.roll(halo, j, 0)
    row = lax.broadcasted_iota(jnp.int32, halo.shape, 0)
    top = jnp.where(row < j, hr, xr[:SUBLANES])
    return jnp.concatenate([top, xr[SUBLANES:]], axis=0)


def _shift_up(x, nxt, j):
    if j == 0:
        return x
    n = x.shape[0]
    xr = pltpu.roll(x, n - j, 0)
    nr = pltpu.roll(nxt, SUBLANES - j, 0)
    row = lax.broadcasted_iota(jnp.int32, nxt.shape, 0)
    bot = jnp.where(row >= SUBLANES - j, nr, xr[n - SUBLANES:])
    return jnp.concatenate([xr[:n - SUBLANES], bot], axis=0)


def _chunk_cumsum(y, row_in_chunk):
    s = 1
    while s < CHUNK:
        y = y + jnp.where(row_in_chunk >= s, pltpu.roll(y, s, 0), 0.0)
        s *= 2
    return y


def _chunk_rev_cumsum(y, row_in_chunk):
    n = y.shape[0]
    s = 1
    while s < CHUNK:
        y = y + jnp.where(row_in_chunk + s < CHUNK, pltpu.roll(y, n - s, 0), 0.0)
        s *= 2
    return y


def _together(generators):
    alive = list(generators)
    while alive:
        nxt = []
        for g in alive:
            try:
                next(g)
                nxt.append(g)
            except StopIteration:
                pass
        alive = nxt
        yield


def _lockstep(generators):
    for _ in _together(generators):
        pass


def _pick_lane(tile, lane, idx):
    return jnp.sum(jnp.where(lane == idx, tile, 0.0), axis=-1, keepdims=True)


def _divisor_tile(n, cap, unit=LANES):
    best = unit
    t = unit
    while t <= min(n, cap):
        if n % t == 0:
            best = t
        t += unit
    return n if n <= cap else best


def _in_proj(x2, w_an, w_in_p, conv_w, alog_l, dt_l, S):
    T, D = x2.shape
    N = w_in_p.shape[1]
    tm = min(512, S)
    assert S % tm == 0 and T % tm == 0, "a token tile must not straddle two sequences"
    tiles_per_seq = S // tm
    C3 = 3 * GDN_WIDTH
    H = GDN_HEADS

    def body(x_ref, wn_ref, w_ref, cw_ref, alog_ref, dt_ref, proj_ref, xn_ref, q_out, k_out, v_out, gates_out,
             halo_s):
        xn, _ = _rms(x_ref[...], wn_ref[...])
        xn = xn.astype(MXU_DTYPE)
        xn_ref[...] = xn
        proj = jnp.dot(xn, w_ref[...], preferred_element_type=F32)
        proj_ref[...] = proj
        u = proj[:, P_GQKV:P_GQKV + C3]

        @pl.when(pl.program_id(0) == 0)
        def _():
            halo_s[...] = jnp.zeros_like(halo_s)

        halo = jnp.where(pl.program_id(0) % tiles_per_seq == 0, 0.0, halo_s[...])
        halo_s[...] = u[tm - SUBLANES:, :]
        c, _ = _conv_taps(u, halo, cw_ref[...])
        a = c * _sigmoid(c)
        for h in range(H):
            xq = a[:, h * GDN_DIM:(h + 1) * GDN_DIM]
            xk = a[:, GDN_WIDTH + h * GDN_DIM:GDN_WIDTH + (h + 1) * GDN_DIM]
            q_out[h] = _l2n(xq, GDN_QSCALE)
            k_out[h] = _l2n(xk, 1.0)
            v_out[h] = a[:, 2 * GDN_WIDTH + h * GDN_DIM:2 * GDN_WIDTH + (h + 1) * GDN_DIM]
        lane = lax.broadcasted_iota(jnp.int32, (tm, LANES), 1)
        ric = lax.broadcasted_iota(jnp.int32, (tm, LANES), 0) % CHUNK
        g, beta = _gate_values(proj[:, P_GAB:P_GAB + LANES], alog_ref[...], dt_ref[...], lane)
        gates_out[...] = _chunk_cumsum(g, ric) + beta

    hspec = pl.BlockSpec((H, tm, GDN_DIM), lambda i: (0, i, 0))
    vec = pl.BlockSpec((1, LANES), lambda i: (0, 0))
    return pl.pallas_call(
        body, grid=(T // tm,), name="in_proj",
        in_specs=[pl.BlockSpec((tm, D), lambda i: (i, 0)), pl.BlockSpec((1, D), lambda i: (0, 0)),
                  pl.BlockSpec((D, N), lambda i: (0, 0)), pl.BlockSpec((CONV_W, C3), lambda i: (0, 0)), vec, vec],
        out_specs=[pl.BlockSpec((tm, N), lambda i: (i, 0)), pl.BlockSpec((tm, D), lambda i: (i, 0)),
                   hspec, hspec, hspec, pl.BlockSpec((tm, LANES), lambda i: (i, 0))],
        out_shape=[SDS((T, N), F32), SDS((T, D), MXU_DTYPE)] + [SDS((H, T, GDN_DIM), F32)] * 3
                  + [SDS((T, LANES), F32)],
        scratch_shapes=[pltpu.VMEM((SUBLANES, C3), F32)],
        compiler_params=_params(("arbitrary",)),
    )(x2, w_an, w_in_p, conv_w, alog_l, dt_l)


def _mla_pre(proj, cosf, sinf, w_qln, w_kvln, w_uq_p, w_ukv, qnw, knw, transfer=None):
    T = proj.shape[0]
    tm = min(256, T)
    H = MLA_HEADS

    def body(ql_ref, kvl_ref, kpe_ref, cos_ref, sin_ref, wq_ref, wkv_ref, uq_ref, ukv_ref, qnw_ref, knw_ref,
             q_out, k_out, v_out):
        rms = functools.partial(_rms, on_mxu=True)
        cos, sin = cos_ref[...], sin_ref[...]
        qnw_, knw_ = qnw_ref[...], knw_ref[...]
        qn, _ = rms(ql_ref[...], wq_ref[...])
        kvn, _ = rms(kvl_ref[...], wkv_ref[...])
        qraw = _mm(qn, uq_ref[...])
        kvraw = _mm(kvn, ukv_ref[...])
        kpe = _rope(rms(kpe_ref[...][:, :ROPE], knw_[:, NOPE:])[0], cos, sin)
        for h in range(H):
            qn_h = rms(qraw[:, h * NOPE:(h + 1) * NOPE], qnw_[:, :NOPE])[0]
            qp_h = _rope(rms(qraw[:, H * NOPE + h * ROPE:H * NOPE + (h + 1) * ROPE], qnw_[:, NOPE:])[0], cos, sin)
            q_out[h] = (jnp.concatenate([qn_h, qp_h], axis=-1) * ATT_SCALE).astype(MXU_DTYPE)
            kn_h = rms(kvraw[:, h * 256:h * 256 + NOPE], knw_[:, :NOPE])[0]
            k_out[h] = jnp.concatenate([kn_h, kpe], axis=-1).astype(MXU_DTYPE)
            v_out[h] = kvraw[:, h * 256 + NOPE:(h + 1) * 256].astype(MXU_DTYPE)

    full = lambda a: pl.BlockSpec(a.shape, lambda i: (0,) * a.ndim)
    return _call_beside(
        body, transfer, grid=(T // tm,), name="mla_pre", scratch_shapes=[], semantics=("arbitrary",),
        args=(proj, proj, proj, cosf, sinf, w_qln, w_kvln, w_uq_p, w_ukv, qnw, knw),
        in_specs=[pl.BlockSpec((tm, 256), lambda i: (i, P_QLAT // 256)),
                  pl.BlockSpec((tm, 256), lambda i: (i, P_KVLAT // 256)),
                  pl.BlockSpec((tm, 128), lambda i: (i, P_KPE // 128)),
                  pl.BlockSpec((tm, ROPE), lambda i: (i, 0)), pl.BlockSpec((tm, ROPE), lambda i: (i, 0)),
                  full(w_qln), full(w_kvln), full(w_uq_p), full(w_ukv), full(qnw), full(knw)],
        out_specs=[pl.BlockSpec((H, tm, QK_DIM), lambda i: (0, i, 0)),
                   pl.BlockSpec((H, tm, QK_DIM), lambda i: (0, i, 0)),
                   pl.BlockSpec((H, tm, V_DIM), lambda i: (0, i, 0))],
        out_shape=[SDS((H, T, QK_DIM), MXU_DTYPE), SDS((H, T, QK_DIM), MXU_DTYPE), SDS((H, T, V_DIM), MXU_DTYPE)])


def _attn_fwd(q4, k4, v4, B, S, transfer=None):
    H = MLA_HEADS
    bq = min(ATTN_BLOCK, S)
    nq = S // bq
    rows = bq // ATTN_CHAINS

    def body(q_ref, k_ref, v_ref, o_ref, lse_ref):
        col = lax.broadcasted_iota(jnp.int32, (rows, bq), 1)
        row = lax.broadcasted_iota(jnp.int32, (rows, bq), 0)

        def q_step(qi, carry):
            qs = pl.multiple_of(qi * bq, bq)
            qsub = [q_ref[0, pl.ds(qs + j * rows, rows), :] for j in range(ATTN_CHAINS)]

            def k_block(ks, cs, diagonal):
                k = k_ref[0, pl.ds(ks, bq), :]
                v = v_ref[0, pl.ds(ks, bq), :]
                out = [None] * ATTN_CHAINS

                def chain(j):
                    m, l, acc = cs[j]
                    s = _mm_nt(qsub[j], k)
                    yield
                    if diagonal:
                        s = jnp.where(col <= row + j * rows, s, -jnp.inf)
                    m_new = jnp.maximum(m, jnp.max(s, axis=-1, keepdims=True))
                    p = jnp.exp(s - m_new)
                    a = jnp.exp(m - m_new)
                    l_new = a * l + jnp.sum(p, axis=-1, keepdims=True)
                    yield
                    out[j] = (m_new, l_new, a * acc + _mm(p, v))

                _lockstep([chain(j) for j in range(ATTN_CHAINS)])
                return tuple(out)

            init = tuple((jnp.full((rows, 1), -jnp.inf, F32), jnp.zeros((rows, 1), F32),
                          jnp.zeros((rows, V_DIM), F32)) for _ in range(ATTN_CHAINS))
            cs = lax.fori_loop(0, qi, lambda kj, c: k_block(pl.multiple_of(kj * bq, bq), c, False), init)
            for j, (m, l, acc) in enumerate(k_block(qs, cs, True)):
                o_ref[0, pl.ds(qs + j * rows, rows), :] = acc / l
                lse_ref[0, pl.ds(qs + j * rows, rows), :] = m + jnp.log(l)
            return carry

        lax.fori_loop(0, nq, q_step, 0)

    spec = lambda d: pl.BlockSpec((1, S, d), lambda h, b: (h, b, 0))
    return _call_beside(
        body, transfer, grid=(H, B), name="attn_fwd",
        in_specs=[spec(QK_DIM), spec(QK_DIM), spec(V_DIM)],
        out_specs=[spec(V_DIM), spec(1)],
        out_shape=[SDS((H, B * S, V_DIM), F32), SDS((H, B * S, 1), F32)],
        scratch_shapes=[], semantics=("arbitrary", "arbitrary"), args=(q4, k4, v4))


def _conv_taps(u, halo, w):
    sh = [_shift_down(u, halo, j) for j in range(CONV_W)]
    c = w[0:1] * sh[3] + w[1:2] * sh[2] + w[2:3] * sh[1] + w[3:4] * sh[0]
    return c, sh


def _gate_values(gab, alog_l, dt_l, lane):
    g = -jnp.exp(alog_l) * jax.nn.softplus(gab + dt_l)
    g = jnp.where(lane < GDN_HEADS, g, 0.0)
    beta = jnp.where((lane >= GDN_HEADS) & (lane < 2 * GDN_HEADS), _sigmoid(gab), 0.0)
    return g, beta


def _unit_lower_inverses(Ls, eye):
    Ps = [eye - L for L in Ls]
    Ms = [_split(-L) for L in Ls]
    for _ in range(5):
        sq = [_mm_split(m, m) for m in Ms]
        Ms = [_split(s) for s in sq]
        Ps = [p + _mm_split(_split(p), m) for p, m in zip(Ps, Ms)]
    return Ps


def _chunk_decays(gt, lane, h, ri, ci, rcol):
    Gc = _pick_lane(gt, lane, h)
    bt = _pick_lane(gt, lane, h + GDN_HEADS)
    Gb = jnp.broadcast_to(Gc, (CHUNK, CHUNK))
    Gam = jnp.where(ri >= ci, jnp.exp(Gb - Gb.T), 0.0)
    Gl = jnp.sum(jnp.where(rcol == CHUNK - 1, Gc, 0.0), axis=0, keepdims=True)
    return Gc, bt, Gam, jnp.exp(Gc), jnp.exp(Gl - Gc), jnp.exp(Gl)


GDN_FWD_UNROLL = 16
GDN_BWD_UNROLL = 8
GDN_RECUR_STEPS_PER_STAGE = 2


def _gdn_fwd(qg, kg, vg, gates, B, S, transfer=None):
    H, D, C = GDN_HEADS, GDN_DIM, CHUNK
    NC = S // C
    P = 2 if B % 2 == 0 else 1
    Sb, NCb = P * S, P * NC
    U = GDN_FWD_UNROLL if NCb % GDN_FWD_UNROLL == 0 else 1
    NG = NCb // U

    def body(q_ref, k_ref, v_ref, g_ref, o_ref, st_ref, ai_ref, u_ref, w_ref, q2_s, au_s, bc_s, w2_s, el_s):
        h = pl.program_id(0)
        lane = lax.broadcasted_iota(jnp.int32, (C, LANES), 1)
        ri = lax.broadcasted_iota(jnp.int32, (C, C), 0)
        ci = lax.broadcasted_iota(jnp.int32, (C, C), 1)
        rcol = lax.broadcasted_iota(jnp.int32, (C, 1), 0)
        eye = (ri == ci).astype(F32)

        def group(gi, c):
            ns = [gi * U + j for j in range(U)]
            css = [pl.multiple_of(n * C, C) for n in ns]
            qs = [q_ref[0, pl.ds(cs, C), :] for cs in css]
            ks = [k_ref[0, pl.ds(cs, C), :] for cs in css]
            vs = [v_ref[0, pl.ds(cs, C), :] for cs in css]
            decs = [_chunk_decays(g_ref[pl.ds(cs, C), :], lane, h, ri, ci, rcol) for cs in css]
            qks = [_mm_nt(jnp.concatenate([q, k], axis=0), k) for q, k in zip(qs, ks)]
            ainvs = _unit_lower_inverses(
                [jnp.where(ri > ci, d[1] * qk[C:] * d[2], 0.0) for qk, d in zip(qks, decs)], eye)
            sols = [_mm_exact(a, jnp.concatenate([v * d[1], k * (d[1] * d[3])], axis=-1))
                    for a, k, v, d in zip(ainvs, ks, vs, decs)]
            atuw = [_mm(qk[:C] * d[2], sol) for qk, d, sol in zip(qks, decs, sols)]
            kduw = [_mm_tn(k * d[4], sol) for k, d, sol in zip(ks, decs, sols)]
            for n, cs, q, a, sol, au, ku, (Gc, bt, Gam, e, f, eL) in zip(ns, css, qs, ainvs, sols, atuw, kduw, decs):
                u_ref[0, pl.ds(cs, C), :] = sol[:, :D]
                w_ref[0, pl.ds(cs, C), :] = sol[:, D:]
                au_s[pl.ds(cs, C), :] = au[:, :D]
                q2_s[pl.ds(cs, C), :] = q * e - au[:, D:]
                bc_s[n] = ku[:, :D]
                w2_s[n] = ku[:, D:]
                el_s[n] = jnp.broadcast_to(eL, (SUBLANES, LANES))
                ai_ref[0, n] = a.T
            return c

        lax.fori_loop(0, NG, group, 0)

        def step(n, states):
            new = []
            for p, S_ in enumerate(states):
                m = p * NC + n
                cs = pl.multiple_of(m * C, C)
                o_ref[0, pl.ds(cs, C), :] = _mm(q2_s[pl.ds(cs, C), :], S_) + au_s[pl.ds(cs, C), :]
                st_ref[0, m] = S_
                new.append(S_ * el_s[m, 0:1, :] + bc_s[m] - _mm(w2_s[m], S_))
            return tuple(new)

        lax.fori_loop(0, NC, step, tuple(jnp.zeros((D, D), F32) for _ in range(P)))

    spec = pl.BlockSpec((1, Sb, D), lambda h, b: (h, b, 0))
    return _call_beside(
        body, transfer, grid=(H, B // P), name="gdn_fwd",
        in_specs=[spec, spec, spec, pl.BlockSpec((Sb, LANES), lambda h, b: (b, 0))],
        out_specs=[spec, pl.BlockSpec((1, NCb, D, D), lambda h, b: (h, b, 0, 0)),
                   pl.BlockSpec((1, NCb, C, C), lambda h, b: (h, b, 0, 0)), spec, spec],
        out_shape=[SDS((H, B * S, D), F32), SDS((H, B * NC, D, D), F32), SDS((H, B * NC, C, C), F32),
                   SDS((H, B * S, D), F32), SDS((H, B * S, D), F32)],
        scratch_shapes=[pltpu.VMEM((Sb, D), F32), pltpu.VMEM((Sb, D), F32), pltpu.VMEM((NCb, D, D), F32),
                        pltpu.VMEM((NCb, D, D), F32), pltpu.VMEM((NCb, SUBLANES, LANES), F32)],
        semantics=("arbitrary", "arbitrary"), args=(qg, kg, vg, gates))


def _mix_out(o_mla, o_gdn, proj, x2, mla_w, gdn_w, w_out):
    T, D = x2.shape
    tm = min(512, T)
    H = MLA_HEADS

    def body(om_ref, og_ref, z_ref, x_ref, mw_ref, gw_ref, w_ref, h_ref, mix_ref):
        z = z_ref[...]
        parts = [_rms(om_ref[h], mw_ref[h:h + 1, :])[0] for h in range(H)]
        for h in range(GDN_HEADS):
            zh = z[:, h * GDN_DIM:(h + 1) * GDN_DIM]
            parts.append(_rms(og_ref[h], gw_ref[...])[0] * (zh * _sigmoid(zh)))
        mix = jnp.concatenate(parts, axis=-1).astype(MXU_DTYPE)
        mix_ref[...] = mix
        h_ref[...] = x_ref[...] + jnp.dot(mix, w_ref[...], preferred_element_type=F32)

    hspec = pl.BlockSpec((H, tm, V_DIM), lambda i: (0, i, 0))
    return pl.pallas_call(
        body, grid=(T // tm,), name="mix_out",
        in_specs=[hspec, hspec, pl.BlockSpec((tm, GDN_WIDTH), lambda i: (i, P_GZ // GDN_WIDTH)),
                  pl.BlockSpec((tm, D), lambda i: (i, 0)),
                  pl.BlockSpec((H, V_DIM), lambda i: (0, 0)), pl.BlockSpec((1, GDN_DIM), lambda i: (0, 0)),
                  pl.BlockSpec((D, D), lambda i: (0, 0))],
        out_specs=[pl.BlockSpec((tm, D), lambda i: (i, 0)), pl.BlockSpec((tm, D), lambda i: (i, 0))],
        out_shape=[SDS((T, D), F32), SDS((T, D), MXU_DTYPE)],
        compiler_params=_params(("arbitrary",)),
    )(o_mla, o_gdn, proj, x2, mla_w, gdn_w, w_out)


def _mlp_fwd(h2, w_mn, w_up, w_down, target):
    T, D = h2.shape
    ns, _, ts = w_up.shape
    F = ns * ts
    tm = min(512, T)
    G = MLP_FWD_SHARDS
    tf, nf = G * ts, ns // G

    def body(h_ref, wn_ref, up_w, down_w, t_ref, up_ref, hn_ref, dy_ref, loss_ref, dyb_ref, y_acc):
        j = pl.program_id(1)

        @pl.when(j == 0)
        def _():
            hn_ref[...] = _rms(h_ref[...], wn_ref[...])[0].astype(MXU_DTYPE)
            y_acc[...] = h_ref[...]

        parts = []
        for c in range(G):
            up = jnp.dot(hn_ref[...], up_w[c], preferred_element_type=F32)
            up_ref[:, c * ts:(c + 1) * ts] = up.astype(MXU_DTYPE)
            r = jnp.maximum(up, 0.0)
            parts.append(_mm(r * r, down_w[c * ts:(c + 1) * ts, :]))
        y_acc[...] += functools.reduce(jnp.add, parts)

        @pl.when(j == nf - 1)
        def _():
            err = y_acc[...] - t_ref[...]
            dy_ref[...] = err / D
            dyb_ref[...] = (err / D).astype(MXU_DTYPE)
            loss_ref[...] = jnp.full((1, SUBLANES, LANES), jnp.sum(err * err), F32)

    return pl.pallas_call(
        body, grid=(T // tm, nf), name="mlp_fwd",
        in_specs=[pl.BlockSpec((tm, D), lambda i, j: (i, 0)), pl.BlockSpec((1, D), lambda i, j: (0, 0)),
                  pl.BlockSpec((G, D, ts), lambda i, j: (j, 0, 0)), pl.BlockSpec((tf, D), lambda i, j: (j, 0)),
                  pl.BlockSpec((tm, D), lambda i, j: (i, 0))],
        out_specs=[pl.BlockSpec((tm, tf), lambda i, j: (i, j)), pl.BlockSpec((tm, D), lambda i, j: (i, 0)),
                   pl.BlockSpec((tm, D), lambda i, j: (i, 0)),
                   pl.BlockSpec((1, SUBLANES, LANES), lambda i, j: (i, 0, 0)),
                   pl.BlockSpec((tm, D), lambda i, j: (i, 0))],
        out_shape=[SDS((T, F), MXU_DTYPE), SDS((T, D), MXU_DTYPE), SDS((T, D), F32),
                   SDS((T // tm, SUBLANES, LANES), F32), SDS((T, D), MXU_DTYPE)],
        scratch_shapes=[pltpu.VMEM((tm, D), F32)],
        compiler_params=_params(("arbitrary", "arbitrary")),
    )(h2, w_mn, w_up, w_down, target)


def _mlp_bwd(dy, dyb, up, h2, w_mn, w_up, w_down, transfer=None):
    T, D = h2.shape
    ns, _, ts = w_up.shape
    F = ns * ts
    tm = min(512, T)
    G = MLP_BWD_SHARDS
    tf, nf = G * ts, ns // G

    def body(dy_ref, dyb_ref, up_ref, h_ref, wn_ref, up_w, down_w, dh_ref, dhb_ref, dup_ref, dwn_ref, acc):
        i, j = pl.program_id(0), pl.program_id(1)

        @pl.when((i == 0) & (j == 0))
        def _():
            dwn_ref[...] = jnp.zeros_like(dwn_ref)

        @pl.when(j == 0)
        def _():
            acc[...] = jnp.zeros_like(acc)

        parts = []
        for c in range(G):
            cols = slice(c * ts, (c + 1) * ts)
            r = jnp.maximum(up_ref[:, cols].astype(F32), 0.0)
            dup = (_mm_nt(dyb_ref[...], down_w[cols, :]) * (2.0 * r)).astype(MXU_DTYPE)
            dup_ref[:, cols] = dup
            parts.append(_mm_nt(dup, up_w[c]))
        acc[...] += functools.reduce(jnp.add, parts)

        @pl.when(j == nf - 1)
        def _():
            hv = h_ref[...]
            _, rr = _rms(hv, wn_ref[...])
            dx, dw = _rms_bwd(acc[...], hv, wn_ref[...], rr)
            dh = dy_ref[...] + dx
            dh_ref[...] = dh
            dhb_ref[...] = dh.astype(MXU_DTYPE)
            dwn_ref[...] += dw

    row = lambda i, j: (i, 0)
    return _call_beside(
        body, transfer, grid=(T // tm, nf), name="mlp_bwd",
        in_specs=[pl.BlockSpec((tm, D), row), pl.BlockSpec((tm, D), row), pl.BlockSpec((tm, tf), lambda i, j: (i, j)),
                  pl.BlockSpec((tm, D), row), pl.BlockSpec((1, D), lambda i, j: (0, 0)),
                  pl.BlockSpec((G, D, ts), lambda i, j: (j, 0, 0)), pl.BlockSpec((tf, D), lambda i, j: (j, 0))],
        out_specs=[pl.BlockSpec((tm, D), row), pl.BlockSpec((tm, D), row),
                   pl.BlockSpec((tm, tf), lambda i, j: (i, j)), pl.BlockSpec((1, D), lambda i, j: (0, 0))],
        out_shape=[SDS((T, D), F32), SDS((T, D), MXU_DTYPE), SDS((T, F), MXU_DTYPE), SDS((1, D), F32)],
        scratch_shapes=[pltpu.VMEM((tm, D), F32)], semantics=("arbitrary", "arbitrary"),
        args=(dy, dyb, up, h2, w_mn, w_up, w_down))


def _mix_bwd(dhb, o_mla, o_gdn, proj, mla_w, gdn_w, w_out):
    T, D = dhb.shape
    tm = min(512, T)
    H = MLA_HEADS

    def body(dh_ref, om_ref, og_ref, z_ref, mw_ref, gw_ref, w_ref, dom_ref, dog_ref, dz_ref, dmw_ref, dgw_ref,
             delta_ref):
        @pl.when(pl.program_id(0) == 0)
        def _():
            dmw_ref[...] = jnp.zeros_like(dmw_ref)
            dgw_ref[...] = jnp.zeros_like(dgw_ref)

        dmix = _mm_nt(dh_ref[...], w_ref[...])
        z = z_ref[...]
        dmw, dzs = [], []
        dgw = jnp.zeros((1, GDN_DIM), F32)
        for h in range(H):
            o = om_ref[h]
            w = mw_ref[h:h + 1, :]
            _, r = _rms(o, w)
            dx, dw = _rms_bwd(dmix[:, h * V_DIM:(h + 1) * V_DIM], o, w, r)
            dom_ref[h] = dx.astype(MXU_DTYPE)
            delta_ref[h] = jnp.sum(dx * o, axis=-1, keepdims=True)
            dmw.append(dw)
        for h in range(GDN_HEADS):
            o = og_ref[h]
            w = gw_ref[...]
            zh = z[:, h * GDN_DIM:(h + 1) * GDN_DIM]
            sg = _sigmoid(zh)
            yn, r = _rms(o, w)
            dy = dmix[:, H * V_DIM + h * GDN_DIM:H * V_DIM + (h + 1) * GDN_DIM]
            dzs.append(dy * yn * (sg * (1.0 + zh * (1.0 - sg))))
            dx, dw = _rms_bwd(dy * (zh * sg), o, w, r)
            dog_ref[h] = dx.astype(MXU_DTYPE)
            dgw = dgw + dw
        dz_ref[...] = jnp.concatenate(dzs, axis=-1).astype(MXU_DTYPE)
        dmw_ref[...] += jnp.concatenate(dmw, axis=0)
        dgw_ref[...] += dgw

    hspec = pl.BlockSpec((H, tm, V_DIM), lambda i: (0, i, 0))
    return pl.pallas_call(
        body, grid=(T // tm,), name="mix_bwd",
        in_specs=[pl.BlockSpec((tm, D), lambda i: (i, 0)), hspec, hspec,
                  pl.BlockSpec((tm, GDN_WIDTH), lambda i: (i, P_GZ // GDN_WIDTH)),
                  pl.BlockSpec((H, V_DIM), lambda i: (0, 0)), pl.BlockSpec((1, GDN_DIM), lambda i: (0, 0)),
                  pl.BlockSpec((D, D), lambda i: (0, 0))],
        out_specs=[hspec, hspec, pl.BlockSpec((tm, GDN_WIDTH), lambda i: (i, 0)),
                   pl.BlockSpec((H, V_DIM), lambda i: (0, 0)), pl.BlockSpec((1, GDN_DIM), lambda i: (0, 0)),
                   pl.BlockSpec((H, tm, 1), lambda i: (0, i, 0))],
        out_shape=[SDS((H, T, V_DIM), MXU_DTYPE), SDS((H, T, GDN_DIM), MXU_DTYPE), SDS((T, GDN_WIDTH), MXU_DTYPE),
                   SDS((H, V_DIM), F32), SDS((1, GDN_DIM), F32), SDS((H, T, 1), F32)],
        compiler_params=_params(("arbitrary",)),
    )(dhb, o_mla, o_gdn, proj, mla_w, gdn_w, w_out)


def _attn_bwd(q4, k4, v4, do4, delta4, lse4, B, S, transfer=None):
    H = MLA_HEADS
    bq = min(ATTN_BLOCK, S)
    nq = S // bq
    rows = bq // ATTN_CHAINS

    def body(q_ref, k_ref, v_ref, do_ref, delta_ref, lse_ref, dq_ref, dk_ref, dv_ref):
        dq_ref[...] = jnp.zeros_like(dq_ref)
        dk_ref[...] = jnp.zeros_like(dk_ref)
        dv_ref[...] = jnp.zeros_like(dv_ref)

        col = lax.broadcasted_iota(jnp.int32, (rows, bq), 1)
        row = lax.broadcasted_iota(jnp.int32, (rows, bq), 0)

        def k_step(kj, carry):
            ks = pl.multiple_of(kj * bq, bq)
            k = k_ref[0, pl.ds(ks, bq), :]
            v = v_ref[0, pl.ds(ks, bq), :]

            def q_block(qs, diagonal):
                dks, dvs = [None] * ATTN_CHAINS, [None] * ATTN_CHAINS

                def chain(j):
                    sl = pl.ds(qs + j * rows, rows)
                    q = q_ref[0, sl, :]
                    do = do_ref[0, sl, :].astype(MXU_DTYPE)
                    s = _mm_nt(q, k)
                    dp = _mm_nt(do, v)
                    yield
                    p = jnp.exp(s - lse_ref[0, sl, :])
                    if diagonal:
                        p = jnp.where(col <= row + j * rows, p, 0.0)
                    ds = p * (dp - delta_ref[0, sl, :])
                    yield
                    dvs[j] = _mm_tn(p, do)
                    dks[j] = _mm_tn(ds, q)
                    dq_ref[0, sl, :] += _mm(ds, k)

                _lockstep([chain(j) for j in range(ATTN_CHAINS)])
                dv_ref[0, pl.ds(ks, bq), :] += functools.reduce(jnp.add, dvs)
                dk_ref[0, pl.ds(ks, bq), :] += functools.reduce(jnp.add, dks)

            q_block(ks, True)

            def q_step(qi, c):
                q_block(pl.multiple_of(qi * bq, bq), False)
                return c

            lax.fori_loop(kj + 1, nq, q_step, 0)
            return carry

        lax.fori_loop(0, nq, k_step, 0)

    spec = lambda d: pl.BlockSpec((1, S, d), lambda h, b: (h, b, 0))
    return _call_beside(
        body, transfer, grid=(H, B), name="attn_bwd",
        in_specs=[spec(QK_DIM), spec(QK_DIM), spec(V_DIM), spec(V_DIM), spec(1), spec(1)],
        out_specs=[spec(QK_DIM), spec(QK_DIM), spec(V_DIM)],
        out_shape=[SDS((H, B * S, QK_DIM), F32), SDS((H, B * S, QK_DIM), F32), SDS((H, B * S, V_DIM), F32)],
        scratch_shapes=[], semantics=("arbitrary", "arbitrary"),
        args=(q4, k4, v4, do4, delta4, lse4))


def _gdn_bwd(qg, kg, vg, gates, states, ainv, u4, w4, do4, B, S, transfer=None):
    H, D, C = GDN_HEADS, GDN_DIM, CHUNK
    NC = S // C
    U = GDN_BWD_UNROLL if NC % GDN_BWD_UNROLL == 0 else 1
    NG = NC // U

    def body(q_ref, k_ref, v_ref, g_ref, st_ref, ai_ref, u_ref, w_ref, do_ref, dq_ref, dk_ref, dv_ref, dgb_ref,
             kd_s, x1_s, x2_s, el_s, dvn_s, ds_s, w2t_s):
        h = pl.program_id(0)
        lane = lax.broadcasted_iota(jnp.int32, (C, LANES), 1)
        ri = lax.broadcasted_iota(jnp.int32, (C, C), 0)
        ci = lax.broadcasted_iota(jnp.int32, (C, C), 1)
        rcol = lax.broadcasted_iota(jnp.int32, (C, 1), 0)

        def rsum(a):
            return jnp.sum(a, axis=-1, keepdims=True)

        def prepare(n):
            cs = n * C
            q = q_ref[0, pl.ds(cs, C), :]
            k = k_ref[0, pl.ds(cs, C), :]
            do = do_ref[0, pl.ds(cs, C), :]
            Gc, bt, Gam, e, f, eL = _chunk_decays(g_ref[pl.ds(cs, C), :], lane, h, ri, ci, rcol)
            At = _mm_nt(q, k) * Gam
            yield
            x1 = _mm_tn(At, do)
            x2 = _mm_tn(q * e, do)
            kd = k * f
            w = w_ref[0, pl.ds(cs, C), :]
            yield
            x1_s[pl.ds(cs, C), :] = x1
            x2_s[n] = x2 - _mm_tn(w, x1)
            w2t_s[n] = _mm_tn(w, kd)
            kd_s[pl.ds(cs, C), :] = kd
            el_s[n] = jnp.broadcast_to(eL, (SUBLANES, LANES))

        def recur(n, dS):
            cs = n * C
            ds_s[n] = dS
            dvn_s[pl.ds(cs, C), :] = x1_s[pl.ds(cs, C), :] + _mm(kd_s[pl.ds(cs, C), :], dS)
            return x2_s[n] + el_s[n, 0:1, :] * dS - _mm(w2t_s[n], dS)

        def local(n):
            cs = n * C
            q = q_ref[0, pl.ds(cs, C), :]
            k = k_ref[0, pl.ds(cs, C), :]
            v = v_ref[0, pl.ds(cs, C), :]
            do = do_ref[0, pl.ds(cs, C), :]
            u = u_ref[0, pl.ds(cs, C), :]
            w = w_ref[0, pl.ds(cs, C), :]
            dvn = dvn_s[pl.ds(cs, C), :]
            dS = ds_s[n]
            Gc, bt, Gam, e, f, eL = _chunk_decays(g_ref[pl.ds(cs, C), :], lane, h, ri, ci, rcol)
            S0 = st_ref[0, n]
            AinvT = ai_ref[0, n]
            qk = _mm_nt(jnp.concatenate([q, k], axis=0), k)
            QK, KK = qk[:C], qk[C:]
            be = bt * e
            sol = jnp.concatenate([u, w], axis=-1)
            vn = u - _mm(w, S0)
            yield
            dAt = jnp.where(ri >= ci, _mm_nt(do, vn), 0.0)
            dqd = _mm_nt(do, S0)
            dw = -_mm_nt(dvn, S0)
            dkd = _mm_nt(vn, dS)
            deL = jnp.sum(rsum(dS * S0), axis=0, keepdims=True)
            yield
            dR = _mm_exact(AinvT, jnp.concatenate([dvn, dw], axis=-1))
            dR1, dR2 = dR[:, :D], dR[:, D:]
            yield
            dL = jnp.where(ri > ci, -_mm_nt(dR, sol), 0.0)
            yield
            dv_ref[0, pl.ds(cs, C), :] = dR1 * bt
            r2 = rsum(dR2 * k)
            X = dL * Gam
            dbt = rsum(dR1 * v) + r2 * e + rsum(X * KK)
            de = r2 * bt + rsum(dqd * q)
            dKK = X * bt
            dQK = dAt * Gam
            dq_ref[0, pl.ds(cs, C), :] = _mm(dQK, k) + dqd * e
            dk_ref[0, pl.ds(cs, C), :] = dR2 * be + _mm(dKK + dKK.T, k) + _mm_tn(dQK, q) + dkd * f
            df = rsum(dkd * k)
            Z = (dL * (bt * KK) + dAt * QK) * Gam
            dG = rsum(Z) - rsum(Z.T) + de * e - df * f
            dGl = jnp.sum(df * f, axis=0, keepdims=True) + deL * eL
            dG = dG + jnp.where(rcol == C - 1, dGl, 0.0)
            dgb_ref[0, pl.ds(cs, C), :] = jnp.where(lane == 0, dG, jnp.where(lane == 1, dbt, 0.0))

        state = [jnp.zeros((D, D), F32)]

        def recur_group(g):
            for j, n in enumerate(reversed(range(g * U, (g + 1) * U))):
                state[0] = recur(n, state[0])
                if j % GDN_RECUR_STEPS_PER_STAGE == GDN_RECUR_STEPS_PER_STAGE - 1:
                    yield

        def stage(fn, g):
            return _together([fn(g * U + j) for j in range(U)])

        for step in range(NG + 2):
            jobs = [(stage, prepare, NG - 1 - step), (None, None, NG - step), (stage, local, NG + 1 - step)]
            _lockstep([recur_group(g) if make is None else make(fn, g) for make, fn, g in jobs if 0 <= g < NG])

    spec = pl.BlockSpec((1, S, D), lambda h, b: (h, b, 0))
    return _call_beside(
        body, transfer, grid=(H, B), name="gdn_bwd",
        in_specs=[spec, spec, spec, pl.BlockSpec((S, LANES), lambda h, b: (b, 0)),
                  pl.BlockSpec((1, NC, D, D), lambda h, b: (h, b, 0, 0)),
                  pl.BlockSpec((1, NC, C, C), lambda h, b: (h, b, 0, 0)), spec, spec, spec],
        out_specs=[spec, spec, spec, spec],
        out_shape=[SDS((H, B * S, D), F32)] * 4,
        scratch_shapes=[pltpu.VMEM((S, D), F32), pltpu.VMEM((S, D), F32), pltpu.VMEM((NC, D, D), F32),
                        pltpu.VMEM((NC, SUBLANES, LANES), F32), pltpu.VMEM((S, D), F32),
                        pltpu.VMEM((NC, D, D), F32), pltpu.VMEM((NC, D, D), F32)],
        semantics=("arbitrary", "arbitrary"), args=(qg, kg, vg, gates, states, ainv, u4, w4, do4))


def _gdn_pre_bwd(proj, conv_w, alog_l, dt_l, dq4, dk4, dv4, dgb4, S):
    T = proj.shape[0]
    tm = min(256, T)
    tiles_per_seq = S // tm
    C3 = 3 * GDN_WIDTH
    H = GDN_HEADS

    def body(u_ref, halo_ref, gab_ref, w_ref, alog_ref, dt_ref, dq_ref, dk_ref, dv_ref, dgb_ref,
             dc_ref, dgab_ref, dcw_ref, dalog_ref, ddt_ref):
        i = pl.program_id(0)

        @pl.when(i == 0)
        def _():
            dcw_ref[...] = jnp.zeros_like(dcw_ref)
            dalog_ref[...] = jnp.zeros_like(dalog_ref)
            ddt_ref[...] = jnp.zeros_like(ddt_ref)

        halo = jnp.where(i % tiles_per_seq == 0, 0.0, halo_ref[...])
        c, sh = _conv_taps(u_ref[...], halo, w_ref[...])
        sg = _sigmoid(c)
        a = c * sg
        das = [None] * (3 * H)
        for h in range(H):
            xq = a[:, h * GDN_DIM:(h + 1) * GDN_DIM]
            xk = a[:, GDN_WIDTH + h * GDN_DIM:GDN_WIDTH + (h + 1) * GDN_DIM]
            das[h] = _l2n_bwd(dq_ref[h], xq, GDN_QSCALE)
            das[H + h] = _l2n_bwd(dk_ref[h], xk, 1.0)
            das[2 * H + h] = dv_ref[h]
        dc = jnp.concatenate(das, axis=-1) * (sg * (1.0 + c * (1.0 - sg)))
        dc_ref[...] = dc
        dcw_ref[...] += jnp.concatenate(
            [jnp.sum(dc * sh[CONV_W - 1 - t], axis=0, keepdims=True) for t in range(CONV_W)], axis=0)
        lane = lax.broadcasted_iota(jnp.int32, (tm, LANES), 1)
        ric = lax.broadcasted_iota(jnp.int32, (tm, LANES), 0) % CHUNK
        dG = jnp.zeros((tm, LANES), F32)
        for h in range(H):
            t = dgb_ref[h]
            dG = dG + jnp.where(lane == h, _pick_lane(t, lane, 0), 0.0) \
                    + jnp.where(lane == h + H, _pick_lane(t, lane, 1), 0.0)
        is_g = lane < H
        dg = jnp.where(is_g, _chunk_rev_cumsum(jnp.where(is_g, dG, 0.0), ric), 0.0)
        gab = gab_ref[...]
        g, beta = _gate_values(gab, alog_ref[...], dt_ref[...], lane)
        dga = jnp.where(is_g, dg * (-jnp.exp(alog_ref[...])) * _sigmoid(gab + dt_ref[...]), 0.0)
        dgb = jnp.where(is_g, 0.0, dG) * beta * (1.0 - beta)
        dgab_ref[...] = (dga + dgb).astype(MXU_DTYPE)
        dalog_ref[...] += jnp.sum(dg * g, axis=0, keepdims=True)
        ddt_ref[...] += jnp.sum(dga, axis=0, keepdims=True)

    hspec = pl.BlockSpec((H, tm, GDN_DIM), lambda i: (0, i, 0))
    vec = pl.BlockSpec((1, LANES), lambda i: (0, 0))
    return pl.pallas_call(
        body, grid=(T // tm,), name="gdn_pre_bwd",
        in_specs=[pl.BlockSpec((tm, C3), lambda i: (i, 0)),
                  pl.BlockSpec((SUBLANES, C3), lambda i: (jnp.maximum(i * (tm // SUBLANES) - 1, 0), 0)),
                  pl.BlockSpec((tm, LANES), lambda i: (i, P_GAB // LANES)),
                  pl.BlockSpec((CONV_W, C3), lambda i: (0, 0)), vec, vec, hspec, hspec, hspec, hspec],
        out_specs=[pl.BlockSpec((tm, C3), lambda i: (i, 0)), pl.BlockSpec((tm, LANES), lambda i: (i, 0)),
                   pl.BlockSpec((CONV_W, C3), lambda i: (0, 0)), vec, vec],
        out_shape=[SDS((T, C3), F32), SDS((T, LANES), MXU_DTYPE), SDS((CONV_W, C3), F32),
                   SDS((1, LANES), F32), SDS((1, LANES), F32)],
        compiler_params=_params(("arbitrary",)),
    )(proj, proj, proj, conv_w, alog_l, dt_l, dq4, dk4, dv4, dgb4)


def _mla_pre_bwd(proj, cosf, sinf, w_qln, w_kvln, w_uq_p, w_ukv, qnw, knw, dq4, dk4, dv4, transfer=None):
    T = proj.shape[0]
    tm = min(256, T)
    H = MLA_HEADS

    def body(ql_ref, kvl_ref, kpe_ref, cos_ref, sin_ref, wq_ref, wkv_ref, uq_ref, ukv_ref, qnw_ref, knw_ref,
             dq_ref, dk_ref, dv_ref,
             dql_ref, dkvl_ref, dkpe_ref, dqraw_ref, dkvraw_ref, qn_ref, kvn_ref, dwq_ref, dwkv_ref, dqnw_ref, dknw_ref):
        @pl.when(pl.program_id(0) == 0)
        def _():
            for r in (dwq_ref, dwkv_ref, dqnw_ref, dknw_ref):
                r[...] = jnp.zeros_like(r)

        cos, sin = cos_ref[...], sin_ref[...]
        qnw_, knw_ = qnw_ref[...], knw_ref[...]
        ql, kvl = ql_ref[...], kvl_ref[...]
        kpe_raw = kpe_ref[...][:, :ROPE]
        rms = functools.partial(_rms, on_mxu=True)
        rms_bwd = functools.partial(_rms_bwd, on_mxu=True)
        qn, rq = rms(ql, wq_ref[...])
        kvn, rkv = rms(kvl, wkv_ref[...])
        qn_ref[...] = qn.astype(MXU_DTYPE)
        kvn_ref[...] = kvn.astype(MXU_DTYPE)
        qraw = _mm(qn, uq_ref[...])
        kvraw = _mm(kvn, ukv_ref[...])
        dq_nope, dq_pe, dkv_parts = [], [], []
        dqnw_n = jnp.zeros((1, NOPE), F32)
        dqnw_p = jnp.zeros((1, ROPE), F32)
        dknw_n = jnp.zeros((1, NOPE), F32)
        dkpe = jnp.zeros((tm, ROPE), F32)
        for h in range(H):
            dq = dq_ref[h] * ATT_SCALE
            x = qraw[:, h * NOPE:(h + 1) * NOPE]
            dx, dw = rms_bwd(dq[:, :NOPE], x, qnw_[:, :NOPE], rms(x, qnw_[:, :NOPE])[1])
            dq_nope.append(dx)
            dqnw_n = dqnw_n + dw
            x = qraw[:, H * NOPE + h * ROPE:H * NOPE + (h + 1) * ROPE]
            dx, dw = rms_bwd(_rope_bwd(dq[:, NOPE:], cos, sin), x, qnw_[:, NOPE:], rms(x, qnw_[:, NOPE:])[1])
            dq_pe.append(dx)
            dqnw_p = dqnw_p + dw
            dk = dk_ref[h]
            x = kvraw[:, h * 256:h * 256 + NOPE]
            dx, dw = rms_bwd(dk[:, :NOPE], x, knw_[:, :NOPE], rms(x, knw_[:, :NOPE])[1])
            dknw_n = dknw_n + dw
            dkpe = dkpe + dk[:, NOPE:]
            dkv_parts += [dx, dv_ref[h]]
        dx, dknw_p = rms_bwd(_rope_bwd(dkpe, cos, sin), kpe_raw, knw_[:, NOPE:], rms(kpe_raw, knw_[:, NOPE:])[1])
        dkpe_ref[...] = jnp.concatenate([dx, jnp.zeros((tm, LANES - ROPE), F32)], axis=-1).astype(MXU_DTYPE)
        dqraw = jnp.concatenate(dq_nope + dq_pe, axis=-1).astype(MXU_DTYPE)
        dkvraw = jnp.concatenate(dkv_parts, axis=-1).astype(MXU_DTYPE)
        dqraw_ref[...] = dqraw
        dkvraw_ref[...] = dkvraw
        dx, dw = rms_bwd(_mm_nt(dqraw, uq_ref[...]), ql, wq_ref[...], rq)
        dql_ref[...] = dx.astype(MXU_DTYPE)
        dwq_ref[...] += dw
        dx, dw = rms_bwd(_mm_nt(dkvraw, ukv_ref[...]), kvl, wkv_ref[...], rkv)
        dkvl_ref[...] = dx.astype(MXU_DTYPE)
        dwkv_ref[...] += dw
        dqnw_ref[...] += jnp.concatenate([dqnw_n, dqnw_p], axis=-1)
        dknw_ref[...] += jnp.concatenate([dknw_n, dknw_p], axis=-1)

    full = lambda a: pl.BlockSpec(a.shape, lambda i: (0,) * a.ndim)
    rows = lambda n: pl.BlockSpec((tm, n), lambda i: (i, 0))
    const = lambda n: pl.BlockSpec((1, n), lambda i: (0, 0))
    NQ, NKV = w_uq_p.shape[1], w_ukv.shape[1]
    return _call_beside(
        body, transfer, grid=(T // tm,), name="mla_pre_bwd", scratch_shapes=[], semantics=("arbitrary",),
        args=(proj, proj, proj, cosf, sinf, w_qln, w_kvln, w_uq_p, w_ukv, qnw, knw, dq4, dk4, dv4),
        in_specs=[pl.BlockSpec((tm, 256), lambda i: (i, P_QLAT // 256)),
                  pl.BlockSpec((tm, 256), lambda i: (i, P_KVLAT // 256)),
                  pl.BlockSpec((tm, 128), lambda i: (i, P_KPE // 128)),
                  rows(ROPE), rows(ROPE),
                  full(w_qln), full(w_kvln), full(w_uq_p), full(w_ukv), full(qnw), full(knw),
                  pl.BlockSpec((H, tm, QK_DIM), lambda i: (0, i, 0)),
                  pl.BlockSpec((H, tm, QK_DIM), lambda i: (0, i, 0)),
                  pl.BlockSpec((H, tm, V_DIM), lambda i: (0, i, 0))],
        out_specs=[rows(Q_LORA), rows(KV_LORA), rows(LANES), rows(NQ), rows(NKV), rows(Q_LORA), rows(KV_LORA),
                   const(Q_LORA), const(KV_LORA), const(QK_DIM), const(QK_DIM)],
        out_shape=[SDS((T, Q_LORA), MXU_DTYPE), SDS((T, KV_LORA), MXU_DTYPE), SDS((T, LANES), MXU_DTYPE),
                   SDS((T, NQ), MXU_DTYPE), SDS((T, NKV), MXU_DTYPE),
                   SDS((T, Q_LORA), MXU_DTYPE), SDS((T, KV_LORA), MXU_DTYPE),
                   SDS((1, Q_LORA), F32), SDS((1, KV_LORA), F32), SDS((1, QK_DIM), F32), SDS((1, QK_DIM), F32)])


def _in_proj_bwd(dc, conv_w, dgz, dql, dkvl, dkpe, dgab, w_in_p, dh, x2, w_an, S):
    T, D = x2.shape
    N = w_in_p.shape[1]
    C3 = dc.shape[1]
    tm = min(512, S)
    assert S % tm == 0 and T % tm == 0, "a token tile must not straddle two sequences"
    tiles_per_seq = S // tm
    nblk = T // SUBLANES

    def body(dc_ref, nxt_ref, cw_ref, b_ref, c_ref, d_ref, e_ref, f_ref, w_ref, dh_ref, x_ref, wn_ref,
             dx_ref, dp_ref, dwn_ref):
        i = pl.program_id(0)

        @pl.when(i == 0)
        def _():
            dwn_ref[...] = jnp.zeros_like(dwn_ref)

        nxt = jnp.where(i % tiles_per_seq == tiles_per_seq - 1, 0.0, nxt_ref[...])
        dcv, cw = dc_ref[...], cw_ref[...]
        du = cw[3:4] * dcv
        for j in range(1, CONV_W):
            du = du + cw[3 - j:4 - j] * _shift_up(dcv, nxt, j)
        dp = jnp.concatenate([du.astype(MXU_DTYPE), b_ref[...], c_ref[...], d_ref[...], e_ref[...], f_ref[...]],
                             axis=-1).astype(MXU_DTYPE)
        dp_ref[...] = dp
        x = x_ref[...]
        _, r = _rms(x, wn_ref[...])
        dx, dw = _rms_bwd(_mm_nt(dp, w_ref[...]), x, wn_ref[...], r)
        dx_ref[...] = dh_ref[...] + dx
        dwn_ref[...] += dw

    rows = lambda n: pl.BlockSpec((tm, n), lambda i: (i, 0))
    return pl.pallas_call(
        body, grid=(T // tm,), name="in_proj_bwd",
        in_specs=[rows(C3),
                  pl.BlockSpec((SUBLANES, C3), lambda i: (jnp.minimum((i + 1) * (tm // SUBLANES), nblk - 1), 0)),
                  pl.BlockSpec((CONV_W, C3), lambda i: (0, 0)),
                  rows(dgz.shape[1]), rows(dql.shape[1]), rows(dkvl.shape[1]),
                  rows(dkpe.shape[1]), rows(dgab.shape[1]),
                  pl.BlockSpec((D, N), lambda i: (0, 0)), rows(D), rows(D), pl.BlockSpec((1, D), lambda i: (0, 0))],
        out_specs=[rows(D), rows(N), pl.BlockSpec((1, D), lambda i: (0, 0))],
        out_shape=[SDS((T, D), F32), SDS((T, N), MXU_DTYPE), SDS((1, D), F32)],
        compiler_params=_params(("arbitrary",)),
    )(dc, dc, conv_w, dgz, dql, dkvl, dkpe, dgab, w_in_p, dh, x2, w_an)


def _relu_squared(t):
    r = jnp.maximum(t.astype(F32), 0.0)
    return (r * r).astype(MXU_DTYPE)


def _wgrad(a, b, name, column_shards=False, a_map=None):
    T, M = a.shape
    N = b.shape[1]
    tM = _divisor_tile(M, 1024)
    tN = N // N_DEV if column_shards else _divisor_tile(N, 1536)
    tk = min(T, 2048)
    nk = T // tk

    def body(a_ref, b_ref, o_ref, acc):
        k = pl.program_id(2)

        @pl.when(k == 0)
        def _():
            acc[...] = jnp.zeros_like(acc)

        acc[...] += _mm_tn(a_ref[...] if a_map is None else a_map(a_ref[...]), b_ref[...])

        @pl.when(k == nk - 1)
        def _():
            o_ref[...] = acc[...].astype(WIRE_DTYPE).reshape(o_ref.shape)

    if column_shards:
        out_spec, out_shape = pl.BlockSpec((1, tM, tN), lambda i, j, k: (j, i, 0)), SDS((N_DEV, M, tN), WIRE_DTYPE)
    else:
        out_spec, out_shape = pl.BlockSpec((tM, tN), lambda i, j, k: (i, j)), SDS((M, N), WIRE_DTYPE)
    return pl.pallas_call(
        body, grid=(M // tM, N // tN, nk), name=name,
        in_specs=[pl.BlockSpec((tk, tM), lambda i, j, k: (k, i)), pl.BlockSpec((tk, tN), lambda i, j, k: (k, j))],
        out_specs=out_spec, out_shape=out_shape,
        scratch_shapes=[pltpu.VMEM((tM, tN), F32)],
        compiler_params=_params(("arbitrary", "arbitrary", "arbitrary")),
    )(a, b)


def _adamw(g, w, m, v):
    m = ADAM_B1 * m + (1.0 - ADAM_B1) * g
    v = ADAM_B2 * v + (1.0 - ADAM_B2) * jnp.square(g)
    m_hat = m / (1.0 - ADAM_B1 ** ADAM_STEP)
    v_hat = v / (1.0 - ADAM_B2 ** ADAM_STEP)
    return -ADAM_LR * (m_hat / (jnp.sqrt(v_hat) + ADAM_EPS) + ADAM_WD * w), m, v


def _reduce_adamw(parts, w, m, v, name):
    R, C = w.shape
    _, Rp, Cp = parts.shape
    tr = min(R, 256)
    tp = tr if Rp == R else Rp

    def body(p_ref, w_ref, m_ref, v_ref, g_ref, d_ref, nm_ref, nv_ref):
        g = p_ref[0].astype(F32)
        for s in range(1, N_DEV):
            g = g + p_ref[s].astype(F32)
        g = g[:tr, :C]
        g_ref[...] = g
        d_ref[...], nm_ref[...], nv_ref[...] = _adamw(g, w_ref[...], m_ref[...], v_ref[...])

    spec = pl.BlockSpec((tr, C), lambda i: (i, 0))
    return pl.pallas_call(
        body, grid=(R // tr,), name=name,
        in_specs=[pl.BlockSpec((N_DEV, tp, Cp), lambda i: (0, i, 0)), spec, spec, spec],
        out_specs=[spec] * 4, out_shape=[SDS((R, C), F32)] * 4,
        compiler_params=_params(("arbitrary",)),
    )(parts, w, m, v)


SMALL_ROWS, SMALL_COLS = 16, 1024
SMALL_LAYOUT = (
    ("attn_norm_w", 0, 1, 1024, 1024), ("mlp_norm_w", 1, 1, 1024, 1024), ("q_lat_norm_w", 2, 1, 256, 256),
    ("kv_lat_norm_w", 3, 1, 256, 256), ("q_norm_w", 4, 1, 192, 192), ("k_norm_w", 5, 1, 192, 192),
    ("mla_out_norm_w", 6, 4, 128, 128), ("a_log", 10, 1, 128, 4), ("dt_bias", 11, 1, 128, 4),
    ("gdn_norm_w", 12, 1, 128, 128))
LOSS_ENTRY = ("loss", 13, 1, 128, 128)


def _adamw_replicated(parts, ws, ms, vs):
    n = len(SMALL_LAYOUT)

    def body(*refs):
        p_ref = refs[0]
        w_refs, m_refs, v_refs = refs[1:1 + n], refs[1 + n:1 + 2 * n], refs[1 + 2 * n:1 + 3 * n]
        outs = refs[1 + 3 * n:]
        s = p_ref[0]
        for d in range(1, N_DEV):
            s = s + p_ref[d]
        for i, (_, r0, nr, _, pw) in enumerate(SMALL_LAYOUT):
            g = s[r0:r0 + nr, :pw]
            outs[i][...] = g
            outs[n + i][...], outs[2 * n + i][...], outs[3 * n + i][...] = _adamw(
                g, w_refs[i][...], m_refs[i][...], v_refs[i][...])
        _, r0, nr, gw, _ = LOSS_ENTRY
        outs[4 * n][...] = s[r0:r0 + nr, :gw]

    res = pl.pallas_call(
        body, name="adamw_replicated",
        out_shape=[SDS(w.shape, F32) for w in ws] * 4 + [SDS((1, LANES), F32)],
        compiler_params=_params(),
    )(parts, *ws, *ms, *vs)
    return [res[k * n:(k + 1) * n] for k in range(4)], res[4 * n][0, 0]


COPIES_PER_ARRAY = N_DEV - 1


def _two_level_gather(srcs, outs, send_sems, recv_sems, local_sems=None, stage="all"):
    mx, my, mc = lax.axis_index("x"), lax.axis_index("y"), lax.axis_index("c")
    me, sibling = (mx, my, mc), (mx, my, 1 - mc)
    chips = [(1 - mx, my), (mx, 1 - my), (1 - mx, 1 - my)]
    arrays = range(len(srcs))

    def copy(a, k, block, to, src=None):
        px, py, pc = block
        slot = outs[a].at[4 * px + 2 * py + pc]
        sem = a * COPIES_PER_ARRAY + k
        return pltpu.make_async_remote_copy(
            src_ref=slot if src is None else src, dst_ref=slot,
            send_sem=send_sems.at[sem], recv_sem=recv_sems.at[sem], device_id=to, device_id_type=MESH_ID)

    mine = [] if local_sems is None else [
        pltpu.make_async_copy(srcs[a], outs[a].at[4 * mx + 2 * my + mc], local_sems.at[a]) for a in arrays]
    first = []
    for a in arrays:
        first.append(copy(a, 0, me, sibling, src=srcs[a]))
        first += [copy(a, 1 + j, me, (*chip, mc), src=srcs[a]) for j, chip in enumerate(chips)]
    forwards = [copy(a, 4 + j, (*chip, mc), sibling) for j, chip in enumerate(chips) for a in arrays]
    if stage in ("all", "start"):
        for cp in mine + first:
            cp.start()
    if stage in ("all", "forward"):
        for j, chip in enumerate(chips):
            for a in arrays:
                copy(a, 1 + j, (*chip, mc), me).wait_recv()
                forwards[j * len(srcs) + a].start()
    if stage in ("all", "finish"):
        for a in arrays:
            copy(a, 0, sibling, me).wait_recv()
        for j, chip in enumerate(chips):
            for a in arrays:
                copy(a, 4 + j, (*chip, 1 - mc), me).wait_recv()
        for cp in first + forwards:
            cp.wait_send()
        for cp in mine:
            cp.wait()


def _comm_scratch(n):
    return [pltpu.SemaphoreType.DMA((n * COPIES_PER_ARRAY,)), pltpu.SemaphoreType.DMA((n * COPIES_PER_ARRAY,)),
            pltpu.SemaphoreType.DMA((n,))]


def _any_specs(n):
    return [pl.BlockSpec(memory_space=pl.ANY)] * n


def _gather_weights(shards):
    n = len(shards)

    def body(*refs):
        _two_level_gather(refs[:n], refs[n:2 * n], *refs[2 * n:])

    return pl.pallas_call(
        body, name="gather_weights",
        out_shape=[SDS((N_DEV,) + s.shape, s.dtype) for s in shards],
        in_specs=_any_specs(n), out_specs=_any_specs(n), scratch_shapes=_comm_scratch(n),
    )(*shards)


def _gather_small_grads(gs, loss_lanes):
    gs = list(gs) + [loss_lanes]
    n = len(gs)

    def body(*refs):
        g_refs, out_ref = refs[:n], refs[n]
        tile, send_sems, recv_sems = refs[n + 1:]
        tile[...] = jnp.zeros_like(tile)
        for (_, r0, nr, gw, _), g in zip(SMALL_LAYOUT + (LOSS_ENTRY,), g_refs):
            tile[r0:r0 + nr, 0:gw] = g[...]
        me = 4 * lax.axis_index("x") + 2 * lax.axis_index("y") + lax.axis_index("c")
        out_ref[me] = tile[...]
        _two_level_gather([tile], [out_ref], send_sems, recv_sems)

    return pl.pallas_call(
        body, name="gather_small_grads",
        out_shape=SDS((N_DEV, SMALL_ROWS, SMALL_COLS), F32),
        in_specs=[pl.BlockSpec(memory_space=pltpu.VMEM)] * n,
        out_specs=pl.BlockSpec(memory_space=pltpu.VMEM),
        scratch_shapes=[pltpu.VMEM((SMALL_ROWS, SMALL_COLS), F32),
                        pltpu.SemaphoreType.DMA((COPIES_PER_ARRAY,)), pltpu.SemaphoreType.DMA((COPIES_PER_ARRAY,))],
    )(*gs)


def _exchange_grads(slabs):
    n = len(slabs)

    def body(*refs):
        _exchange(refs[:n], refs[n:2 * n], *refs[2 * n:])

    return pl.pallas_call(
        body, name="exchange_grads",
        out_shape=[SDS(s.shape, s.dtype) for s in slabs],
        in_specs=_any_specs(n), out_specs=_any_specs(n), scratch_shapes=_comm_scratch(n),
    )(*slabs)


class _Transfer:
    def __init__(self, kind, arrays):
        self.kind, self.arrays, self.n = kind, list(arrays), len(arrays)

    def out_shapes(self):
        if self.kind == "gather":
            return [SDS((N_DEV,) + a.shape, a.dtype) for a in self.arrays]
        return [SDS(a.shape, a.dtype) for a in self.arrays]

    def run(self, srcs, outs, sems, stage):
        fn = _two_level_gather if self.kind == "gather" else _exchange
        fn(srcs, outs, *sems, stage=stage)


def _call_beside(body, transfer, *, grid, in_specs, out_specs, out_shape, scratch_shapes, name, semantics, args):
    if transfer is None:
        res = pl.pallas_call(body, grid=grid, in_specs=in_specs, out_specs=out_specs, out_shape=out_shape,
                             scratch_shapes=scratch_shapes, name=name, compiler_params=_params(semantics))(*args)
        return list(res), []
    n_in, n_out, n_s, n = len(in_specs), len(out_specs), len(scratch_shapes), transfer.n
    total = functools.reduce(lambda a, b: a * b, grid, 1)

    def wrapped(*refs):
        ins, refs = refs[:n_in], refs[n_in:]
        t_in, refs = refs[:n], refs[n:]
        outs, refs = refs[:n_out], refs[n_out:]
        t_out, refs = refs[:n], refs[n:]
        scratch, sems = refs[:n_s], refs[n_s:]
        first = functools.reduce(jnp.logical_and, [pl.program_id(i) == 0 for i in range(len(grid))])
        last = functools.reduce(jnp.logical_and, [pl.program_id(i) == g - 1 for i, g in enumerate(grid)])

        @pl.when(first)
        def _():
            transfer.run(t_in, t_out, sems, "start")

        step = functools.reduce(lambda acc, ig: acc * ig[1] + pl.program_id(ig[0]), enumerate(grid), 0)

        @pl.when(step == (3 * total) // 4)
        def _():
            transfer.run(t_in, t_out, sems, "forward")

        body(*ins, *outs, *scratch)

        @pl.when(last)
        def _():
            transfer.run(t_in, t_out, sems, "finish")

    res = pl.pallas_call(
        wrapped, grid=grid, in_specs=list(in_specs) + _any_specs(n), out_specs=list(out_specs) + _any_specs(n),
        out_shape=list(out_shape) + transfer.out_shapes(), scratch_shapes=list(scratch_shapes) + _comm_scratch(n),
        name=name, compiler_params=_params(semantics))(*args, *transfer.arrays)
    return list(res[:n_out]), list(res[n_out:])


EXCHANGE_FLIPS = ((0, 0, 1), (1, 0, 0), (0, 1, 0), (1, 1, 0), (1, 0, 1), (0, 1, 1), (1, 1, 1))


def _exchange(srcs, outs, send_sems, recv_sems, local_sems, stage="all"):
    mx, my, mc = lax.axis_index("x"), lax.axis_index("y"), lax.axis_index("c")
    arrays = range(len(srcs))
    copies = [pltpu.make_async_copy(srcs[a].at[4 * mx + 2 * my + mc], outs[a].at[N_DEV - 1], local_sems.at[a])
              for a in arrays]
    for k, (fx, fy, fc) in enumerate(EXCHANGE_FLIPS):
        px = 1 - mx if fx else mx
        py = 1 - my if fy else my
        pc = 1 - mc if fc else mc
        for a in arrays:
            sem = a * COPIES_PER_ARRAY + k
            copies.append(pltpu.make_async_remote_copy(
                src_ref=srcs[a].at[4 * px + 2 * py + pc], dst_ref=outs[a].at[k],
                send_sem=send_sems.at[sem], recv_sem=recv_sems.at[sem],
                device_id=(px, py, pc), device_id_type=MESH_ID))
    if stage in ("all", "start"):
        for cp in copies:
            cp.start()
    if stage in ("all", "finish"):
        for cp in copies:
            cp.wait()


def _w_in_to_padded(w):
    z = lambda n: jnp.zeros((w.shape[0], n), w.dtype)
    return jnp.concatenate([w[:, O_GQKV:O_GZ], w[:, O_GZ:O_GAB], w[:, O_QLAT:O_KVLAT], w[:, O_KVLAT:O_KPE],
                            w[:, O_KPE:O_GQKV], z(P_GAB - P_KPE - ROPE), w[:, O_GAB:O_END],
                            z(P_WIDTH - P_GAB - (O_END - O_GAB))], axis=1)


def _w_in_from_padded(wp):
    return jnp.concatenate([wp[:, P_QLAT:P_QLAT + 256], wp[:, P_KVLAT:P_KVLAT + 256], wp[:, P_KPE:P_KPE + ROPE],
                            wp[:, P_GQKV:P_GZ], wp[:, P_GZ:P_QLAT], wp[:, P_GAB:P_GAB + (O_END - O_GAB)]], axis=1)


W_IN_SHARD_COLS = (O_END - O_QLAT) // N_DEV


def _w_in_shards_to_padded(stack):
    _, R, Cw = stack.shape
    tr = min(R, 256)

    def body(s_ref, o_ref):
        full = jnp.concatenate([s_ref[d].astype(F32)[:, :W_IN_SHARD_COLS] for d in range(N_DEV)], axis=-1)
        o_ref[...] = _w_in_to_padded(full).astype(o_ref.dtype)

    return pl.pallas_call(
        body, grid=(R // tr,), name="w_in_to_padded",
        in_specs=[pl.BlockSpec((N_DEV, tr, Cw), lambda i: (0, i, 0))],
        out_specs=pl.BlockSpec((tr, P_WIDTH), lambda i: (i, 0)),
        out_shape=SDS((R, P_WIDTH), stack.dtype), compiler_params=_params(("arbitrary",)),
    )(stack)


def _w_in_padded_to_slabs(gp, wire_cols):
    R = gp.shape[0]
    tr = min(R, 256)

    def body(g_ref, o_ref):
        orig = _w_in_from_padded(g_ref[...].astype(F32))
        for d in range(N_DEV):
            piece = orig[:, d * W_IN_SHARD_COLS:(d + 1) * W_IN_SHARD_COLS]
            o_ref[d] = _pad2(piece, tr, wire_cols).astype(o_ref.dtype)

    return pl.pallas_call(
        body, grid=(R // tr,), name="w_in_to_slabs",
        in_specs=[pl.BlockSpec((tr, P_WIDTH), lambda i: (i, 0))],
        out_specs=pl.BlockSpec((N_DEV, tr, wire_cols), lambda i: (0, i, 0)),
        out_shape=SDS((N_DEV, R, wire_cols), gp.dtype), compiler_params=_params(("arbitrary",)),
    )(gp)


def _w_uq_to_headsplit(w):
    w3 = w.reshape(w.shape[0], MLA_HEADS, QK_DIM)
    return jnp.concatenate([w3[:, :, :NOPE].reshape(w.shape[0], -1), w3[:, :, NOPE:].reshape(w.shape[0], -1)], axis=1)


def _w_uq_from_headsplit(wp):
    n = wp[:, :MLA_HEADS * NOPE].reshape(wp.shape[0], MLA_HEADS, NOPE)
    p = wp[:, MLA_HEADS * NOPE:].reshape(wp.shape[0], MLA_HEADS, ROPE)
    return jnp.concatenate([n, p], axis=2).reshape(wp.shape[0], -1)


def _lane_vec(v4):
    return jnp.pad(v4.reshape(1, -1), ((0, 0), (0, LANES - v4.shape[-1])))


def _local_step(x, positions, target, attn_norm_w, w_in, q_lat_norm_w, w_uq, kv_lat_norm_w, w_ukv, q_norm_w,
                k_norm_w, mla_out_norm_w, conv_w, a_log, dt_bias, gdn_norm_w, w_out, mlp_norm_w, w_up, w_down,
                late_shards=None, exchange=False):
    B, S, D = x.shape
    T = B * S
    x2 = x.reshape(T, D)
    t2 = target.reshape(T, D)
    half = ROPE // 2
    inv_freq = ROPE_THETA ** (-jnp.arange(half, dtype=F32) / half)
    ang = positions.reshape(T, 1).astype(F32) * inv_freq
    cosf = jnp.concatenate([jnp.cos(ang)] * 2, axis=-1)
    sinf = jnp.concatenate([jnp.sin(ang)] * 2, axis=-1)
    w_in_p = w_in
    w_uq_p = _w_uq_to_headsplit(w_uq)
    alog_l, dt_l = _lane_vec(a_log), _lane_vec(dt_bias)
    w_an, w_qln, w_kvln, qnw, knw, w_mn, gdn_w = (
        attn_norm_w, q_lat_norm_w, kv_lat_norm_w, q_norm_w, k_norm_w, mlp_norm_w, gdn_norm_w)

    proj, xn, qg, kg, vg, gates = _in_proj(x2, w_an, w_in_p, conv_w, alog_l, dt_l, S)
    def gathering(shards):
        return None if late_shards is None else _Transfer("gather", shards)

    (q4, k4, v4), late = _mla_pre(proj, cosf, sinf, w_qln, w_kvln, w_uq_p, w_ukv, qnw, knw,
                                  gathering(late_shards and late_shards[:1]))
    if late:
        w_out = late[0].reshape(-1, D)
    (o_mla, lse), late = _attn_fwd(q4, k4, v4, B, S, gathering(late_shards and late_shards[2:]))
    if late:
        w_down = late[0].reshape(-1, D)
    (o_gdn, states, ainv, u4, w4), late = _gdn_fwd(qg, kg, vg, gates, B, S,
                                                   gathering(late_shards and late_shards[1:2]))
    if late:
        w_up = late[0]
    h2, mix = _mix_out(o_mla, o_gdn, proj, x2, mla_out_norm_w, gdn_w, w_out)
    up, hn, dy, sq, dyb = _mlp_fwd(h2, w_mn, w_up, w_down, t2)
    loss = (0.5 / D) * jnp.sum(sq[:, 0, 0])

    first = ("w_down",)
    second = ("w_up",)
    third = ("w_out", "w_uq", "w_ukv")
    mats = dict(w_down=_wgrad(up, dyb, "wgrad_down", a_map=_relu_squared))

    def sending(names):
        return _Transfer("exchange", [_slabs(n, mats[n]) for n in names]) if exchange else None

    (dh, dhb, dup, d_mlp_norm), got = _mlp_bwd(dy, dyb, up, h2, w_mn, w_up, w_down, sending(first))
    mats.update(zip(first, got))
    mats.update(w_up=_wgrad(hn, dup, "wgrad_up", column_shards=True))
    do_mla, do_gdn, dz, d_mla_w, d_gdn_w, delta = _mix_bwd(dhb, o_mla, o_gdn, proj, mla_out_norm_w, gdn_w, w_out)
    mats.update(w_out=_wgrad(mix, dhb, "wgrad_out"))
    (dq4, dk4, dv4), got = _attn_bwd(q4, k4, v4, do_mla, delta, lse, B, S, sending(second))
    mats.update(zip(second, got))
    (dql, dkvl, dkpe, dqraw, dkvraw, qn, kvn, d_wqln, d_wkvln, d_qnw, d_knw), _ = _mla_pre_bwd(
        proj, cosf, sinf, w_qln, w_kvln, w_uq_p, w_ukv, qnw, knw, dq4, dk4, dv4)
    mats.update(w_uq=_wgrad(qn, dqraw, "wgrad_uq"), w_ukv=_wgrad(kvn, dkvraw, "wgrad_ukv"))
    (dqg, dkg, dvg, dgb4), got = _gdn_bwd(qg, kg, vg, gates, states, ainv, u4, w4, do_gdn, B, S, sending(third))
    mats.update(zip(third, got))
    dc, dgab, g_conv, d_alog, d_dt = _gdn_pre_bwd(proj, conv_w, alog_l, dt_l, dqg, dkg, dvg, dgb4, S)
    grad_x2, dproj, d_attn_norm = _in_proj_bwd(dc, conv_w, dz, dql, dkvl, dkpe, dgab, w_in_p, dh, x2, w_an, S)
    mats.update(w_in=_wgrad(xn, dproj, "wgrad_in"), conv_w=g_conv)
    if exchange:
        last = ("w_in", "conv_w")
        mats.update(zip(last, _exchange_grads([_slabs(n, mats[n]) for n in last])))
    small = dict(attn_norm_w=d_attn_norm, mlp_norm_w=d_mlp_norm, q_lat_norm_w=d_wqln, kv_lat_norm_w=d_wkvln,
                 q_norm_w=d_qnw, k_norm_w=d_knw, mla_out_norm_w=d_mla_w, a_log=d_alog, dt_bias=d_dt,
                 gdn_norm_w=d_gdn_w)
    return loss, grad_x2.reshape(B, S, D), mats, [small[n] for n, *_ in SMALL_LAYOUT]


BIG = ("w_in", "w_uq", "w_ukv", "conv_w", "w_out", "w_up", "w_down")
ALL_W = ("attn_norm_w", "w_in", "q_lat_norm_w", "w_uq", "kv_lat_norm_w", "w_ukv", "q_norm_w", "k_norm_w",
         "mla_out_norm_w", "conv_w", "a_log", "dt_bias", "gdn_norm_w", "w_out", "mlp_norm_w", "w_up", "w_down")
WIRE_SHAPE = {"w_in": (1024, 384), "w_uq": (256, 128), "conv_w": (16, 256)}


def _pad2(a, rows, cols):
    return jnp.pad(a, [(0, 0)] * (a.ndim - 2) + [(0, rows - a.shape[-2]), (0, cols - a.shape[-1])])


def _cols_to_full(stack, cols):
    return jnp.moveaxis(stack[:, :, :cols], 0, 1).reshape(stack.shape[1], N_DEV * cols)


def _full_to_cols(full, wire_cols):
    r, n = full.shape
    return _pad2(jnp.moveaxis(full.reshape(r, N_DEV, n // N_DEV), 1, 0), r, wire_cols)


def _slabs(name, g):
    if name == "w_in":
        return _w_in_padded_to_slabs(g, WIRE_SHAPE["w_in"][1])
    if name == "w_uq":
        return _full_to_cols(_w_uq_from_headsplit(g), WIRE_SHAPE["w_uq"][1])
    if name == "w_ukv":
        return _full_to_cols(g, g.shape[1] // N_DEV)
    if name == "conv_w":
        return _pad2(_full_to_cols(g.astype(WIRE_DTYPE), g.shape[1] // N_DEV), *WIRE_SHAPE["conv_w"])
    if name == "w_up":
        return g
    return g.reshape(N_DEV, -1, g.shape[-1])


def kernel(x, positions, attn_norm_w, w_in, q_lat_norm_w, w_uq, kv_lat_norm_w, w_ukv, q_norm_w, k_norm_w, mla_out_norm_w, conv_w, a_log, dt_bias, gdn_norm_w, w_out, mlp_norm_w, w_up, w_down, loss_target, m_attn_norm_w, m_w_in, m_q_lat_norm_w, m_w_uq, m_kv_lat_norm_w, m_w_ukv, m_q_norm_w, m_k_norm_w, m_mla_out_norm_w, m_conv_w, m_a_log, m_dt_bias, m_gdn_norm_w, m_w_out, m_mlp_norm_w, m_w_up, m_w_down, v_attn_norm_w, v_w_in, v_q_lat_norm_w, v_w_uq, v_kv_lat_norm_w, v_w_ukv, v_q_norm_w, v_k_norm_w, v_mla_out_norm_w, v_conv_w, v_a_log, v_dt_bias, v_gdn_norm_w, v_w_out, v_mlp_norm_w, v_w_up, v_w_down):
    env = dict(locals())
    W = {n: env[n][0] for n in ALL_W}
    Mo = {n: env["m_" + n][0] for n in ALL_W}
    Vo = {n: env["v_" + n][0] for n in ALL_W}

    two_d = lambda a: a.reshape(1, -1) if a.ndim == 1 else a
    D = x.shape[-1]

    s_in, s_uq, s_ukv, s_conv = _gather_weights([
        _pad2(W["w_in"].astype(WIRE_DTYPE), *WIRE_SHAPE["w_in"]),
        _pad2(W["w_uq"].astype(WIRE_DTYPE), *WIRE_SHAPE["w_uq"]),
        W["w_ukv"].astype(WIRE_DTYPE), _pad2(W["conv_w"], *WIRE_SHAPE["conv_w"])])
    late = [W["w_out"].astype(WIRE_DTYPE), W["w_up"].astype(WIRE_DTYPE), W["w_down"].astype(WIRE_DTYPE)]

    loss, grad_x, parts, gs = _local_step(
        x, positions, loss_target, two_d(W["attn_norm_w"]), _w_in_shards_to_padded(s_in),
        two_d(W["q_lat_norm_w"]), _cols_to_full(s_uq, W["w_uq"].shape[1]), two_d(W["kv_lat_norm_w"]),
        _cols_to_full(s_ukv, W["w_ukv"].shape[1]), two_d(W["q_norm_w"]), two_d(W["k_norm_w"]),
        W["mla_out_norm_w"], _cols_to_full(s_conv[:, :CONV_W], W["conv_w"].shape[1]), two_d(W["a_log"]),
        two_d(W["dt_bias"]), two_d(W["gdn_norm_w"]), None, two_d(W["mlp_norm_w"]), None, None,
        late_shards=late, exchange=True)
    done = {n: _reduce_adamw(parts[n], W[n], Mo[n], Vo[n], "adamw_" + n) for n in BIG}
    names = [n for n, *_ in SMALL_LAYOUT]
    tiles = _gather_small_grads(gs, jnp.full((1, LANES), loss, F32))
    small, loss = _adamw_replicated(tiles, [two_d(W[n]) for n in names], [two_d(Mo[n]) for n in names],
                                    [two_d(Vo[n]) for n in names])
    for i, n in enumerate(names):
        done[n] = [small[kind][i] for kind in range(4)]
    res = [done[n][kind].reshape(env[n].shape) for kind in range(4) for n in ALL_W]
    return (loss, grad_x, *res)
```

```python
import functools

import jax
import jax.numpy as jnp
from jax import lax
from jax.experimental import pallas as pl
from jax.experimental.pallas import tpu as pltpu

F32 = jnp.float32
MXU_DTYPE = jnp.bfloat16
WIRE_DTYPE = jnp.bfloat16
SDS = jax.ShapeDtypeStruct
HIGHEST = lax.Precision.HIGHEST
MESH_ID = pl.DeviceIdType.MESH

D_MODEL = 1024
MLA_HEADS = 4
Q_LORA = 256
KV_LORA = 256
NOPE = 128
ROPE = 64
QK_DIM = NOPE + ROPE
V_DIM = 128
ROPE_THETA = 10000.0
GDN_HEADS = 4
GDN_DIM = 128
GDN_WIDTH = GDN_HEADS * GDN_DIM
CONV_W = 4
CHUNK = 64
D_FF = 4 * D_MODEL
EPS = 1e-6
ATT_SCALE = QK_DIM ** -0.5
GDN_QSCALE = GDN_DIM ** -0.5
N_DEV = 8
ATTN_BLOCK = 512
ATTN_CHAINS = 2
MLP_FWD_SHARDS = 4
MLP_BWD_SHARDS = 4

ADAM_LR = 0.001
ADAM_B1 = 0.9
ADAM_B2 = 0.999
ADAM_EPS = 1e-08
ADAM_WD = 0.01
ADAM_STEP = 10

LANES = 128
SUBLANES = 8
VMEM_LIMIT = 60 * 1024 * 1024

P_GQKV, P_GZ, P_QLAT, P_KVLAT, P_KPE, P_GAB = 0, 1536, 2048, 2304, 2560, 2688
P_WIDTH = 2816
O_QLAT, O_KVLAT, O_KPE, O_GQKV, O_GZ, O_GAB, O_END = 0, 256, 512, 576, 2112, 2624, 2632


def _params(sem=None, vmem=VMEM_LIMIT):
    kw = dict(vmem_limit_bytes=vmem)
    if sem is not None:
        kw["dimension_semantics"] = sem
    return pltpu.CompilerParams(**kw)


def _mm(a, b):
    return jnp.dot(a.astype(MXU_DTYPE), b.astype(MXU_DTYPE), preferred_element_type=F32)


def _mm_nt(a, b):
    return lax.dot_general(a.astype(MXU_DTYPE), b.astype(MXU_DTYPE), (((1,), (1,)), ((), ())),
                           preferred_element_type=F32)


def _mm_tn(a, b):
    return lax.dot_general(a.astype(MXU_DTYPE), b.astype(MXU_DTYPE), (((0,), (0,)), ((), ())),
                           preferred_element_type=F32)


def _split(a):
    hi = a.astype(MXU_DTYPE)
    return hi, (a - hi.astype(F32)).astype(MXU_DTYPE)


def _mm_split(a, b):
    (ah, al), (bh, bl) = a, b
    dot = lambda x, y: jnp.dot(x, y, preferred_element_type=F32)
    if MXU_DTYPE == F32:
        return dot(ah, bh)
    return dot(ah, bh) + dot(ah, bl) + dot(al, bh)


def _mm_exact(a, b):
    return _mm_split(_split(a), _split(b))


def _row_sum(v, on_mxu=False):
    if not on_mxu:
        return jnp.sum(v, axis=-1, keepdims=True)
    d = v.shape[-1]
    ones = jnp.ones((d, LANES), MXU_DTYPE)
    s = sum(jnp.dot(p, ones, preferred_element_type=F32) for p in _split(v))
    return s[:, :d] if d <= LANES else jnp.tile(s, (1, d // LANES))


def _rms(x, w, on_mxu=False):
    r = lax.rsqrt(_row_sum(x * x, on_mxu) * (1.0 / x.shape[-1]) + EPS)
    return x * r * w, r


def _rms_bwd(dy, x, w, r, on_mxu=False):
    xh = x * r
    dyw = dy * w
    dx = r * (dyw - xh * (_row_sum(dyw * xh, on_mxu) * (1.0 / x.shape[-1])))
    dw = jnp.sum(dy * xh, axis=0, keepdims=True)
    return dx, dw


def _l2n(x, scale):
    return x * (lax.rsqrt(_row_sum(x * x) + EPS) * scale)


def _l2n_bwd(dy, x, scale):
    r = lax.rsqrt(_row_sum(x * x) + EPS)
    xh = x * r
    return (scale * r) * (dy - xh * _row_sum(dy * xh))


def _rot(t):
    return jnp.concatenate([-t[:, ROPE // 2:], t[:, :ROPE // 2]], axis=-1)


def _rot_t(t):
    return jnp.concatenate([t[:, ROPE // 2:], -t[:, :ROPE // 2]], axis=-1)


def _rope(t, cos, sin):
    return t * cos + _rot(t) * sin


def _rope_bwd(d, cos, sin):
    return d * cos + _rot_t(d * sin)


def _sigmoid(x):
    return jax.nn.sigmoid(x)


def _shift_down(x, halo, j):
    if j == 0:
        return x
    xr = pltpu.roll(x, j, 0)
    hr = pltpu.roll(halo, j, 0)
    row = lax.broadcasted_iota(jnp.int32, halo.shape, 0)
    top = jnp.where(row < j, hr, xr[:SUBLANES])
    return jnp.concatenate([top, xr[SUBLANES:]], axis=0)


def _shift_up(x, nxt, j):
    if j == 0:
        return x
    n = x.shape[0]
    xr = pltpu.roll(x, n - j, 0)
    nr = pltpu.roll(nxt, SUBLANES - j, 0)
    row = lax.broadcasted_iota(jnp.int32, nxt.shape, 0)
    bot = jnp.where(row >= SUBLANES - j, nr, xr[n - SUBLANES:])
    return jnp.concatenate([xr[:n - SUBLANES], bot], axis=0)


def _chunk_cumsum(y, row_in_chunk):
    s = 1
    while s < CHUNK:
        y = y + jnp.where(row_in_chunk >= s, pltpu.roll(y, s, 0), 0.0)
        s *= 2
    return y


def _chunk_rev_cumsum(y, row_in_chunk):
    n = y.shape[0]
    s = 1
    while s < CHUNK:
        y = y + jnp.where(row_in_chunk + s < CHUNK, pltpu.roll(y, n - s, 0), 0.0)
        s *= 2
    return y


def _together(generators):
    alive = list(generators)
    while alive:
        nxt = []
        for g in alive:
            try:
                next(g)
                nxt.append(g)
            except StopIteration:
                pass
        alive = nxt
        yield


def _lockstep(generators):
    for _ in _together(generators):
        pass


def _pick_lane(tile, lane, idx):
    return jnp.sum(jnp.where(lane == idx, tile, 0.0), axis=-1, keepdims=True)


def _divisor_tile(n, cap, unit=LANES):
    best = unit
    t = unit
    while t <= min(n, cap):
        if n % t == 0:
            best = t
        t += unit
    return n if n <= cap else best


def _in_proj(x2, w_an, w_in_p, conv_w, alog_l, dt_l, S):
    T, D = x2.shape
    N = w_in_p.shape[1]
    tm = min(512, S)
    assert S % tm == 0 and T % tm == 0, "a token tile must not straddle two sequences"
    tiles_per_seq = S // tm
    C3 = 3 * GDN_WIDTH
    H = GDN_HEADS

    def body(x_ref, wn_ref, w_ref, cw_ref, alog_ref, dt_ref, proj_ref, xn_ref, q_out, k_out, v_out, gates_out,
             halo_s):
        xn, _ = _rms(x_ref[...], wn_ref[...])
        xn = xn.astype(MXU_DTYPE)
        xn_ref[...] = xn
        proj = jnp.dot(xn, w_ref[...], preferred_element_type=F32)
        proj_ref[...] = proj
        u = proj[:, P_GQKV:P_GQKV + C3]

        @pl.when(pl.program_id(0) == 0)
        def _():
            halo_s[...] = jnp.zeros_like(halo_s)

        halo = jnp.where(pl.program_id(0) % tiles_per_seq == 0, 0.0, halo_s[...])
        halo_s[...] = u[tm - SUBLANES:, :]
        c, _ = _conv_taps(u, halo, cw_ref[...])
        a = c * _sigmoid(c)
        for h in range(H):
            xq = a[:, h * GDN_DIM:(h + 1) * GDN_DIM]
            xk = a[:, GDN_WIDTH + h * GDN_DIM:GDN_WIDTH + (h + 1) * GDN_DIM]
            q_out[h] = _l2n(xq, GDN_QSCALE)
            k_out[h] = _l2n(xk, 1.0)
            v_out[h] = a[:, 2 * GDN_WIDTH + h * GDN_DIM:2 * GDN_WIDTH + (h + 1) * GDN_DIM]
        lane = lax.broadcasted_iota(jnp.int32, (tm, LANES), 1)
        ric = lax.broadcasted_iota(jnp.int32, (tm, LANES), 0) % CHUNK
        g, beta = _gate_values(proj[:, P_GAB:P_GAB + LANES], alog_ref[...], dt_ref[...], lane)
        gates_out[...] = _chunk_cumsum(g, ric) + beta

    hspec = pl.BlockSpec((H, tm, GDN_DIM), lambda i: (0, i, 0))
    vec = pl.BlockSpec((1, LANES), lambda i: (0, 0))
    return pl.pallas_call(
        body, grid=(T // tm,), name="in_proj",
        in_specs=[pl.BlockSpec((tm, D), lambda i: (i, 0)), pl.BlockSpec((1, D), lambda i: (0, 0)),
                  pl.BlockSpec((D, N), lambda i: (0, 0)), pl.BlockSpec((CONV_W, C3), lambda i: (0, 0)), vec, vec],
        out_specs=[pl.BlockSpec((tm, N), lambda i: (i, 0)), pl.BlockSpec((tm, D), lambda i: (i, 0)),
                   hspec, hspec, hspec, pl.BlockSpec((tm, LANES), lambda i: (i, 0))],
        out_shape=[SDS((T, N), F32), SDS((T, D), MXU_DTYPE)] + [SDS((H, T, GDN_DIM), F32)] * 3
                  + [SDS((T, LANES), F32)],
        scratch_shapes=[pltpu.VMEM((SUBLANES, C3), F32)],
        compiler_params=_params(("arbitrary",)),
    )(x2, w_an, w_in_p, conv_w, alog_l, dt_l)


def _mla_pre(proj, cosf, sinf, w_qln, w_kvln, w_uq_p, w_ukv, qnw, knw, transfer=None):
    T = proj.shape[0]
    tm = min(256, T)
    H = MLA_HEADS

    def body(ql_ref, kvl_ref, kpe_ref, cos_ref, sin_ref, wq_ref, wkv_ref, uq_ref, ukv_ref, qnw_ref, knw_ref,
             q_out, k_out, v_out):
        rms = functools.partial(_rms, on_mxu=True)
        cos, sin = cos_ref[...], sin_ref[...]
        qnw_, knw_ = qnw_ref[...], knw_ref[...]
        qn, _ = rms(ql_ref[...], wq_ref[...])
        kvn, _ = rms(kvl_ref[...], wkv_ref[...])
        qraw = _mm(qn, uq_ref[...])
        kvraw = _mm(kvn, ukv_ref[...])
        kpe = _rope(rms(kpe_ref[...][:, :ROPE], knw_[:, NOPE:])[0], cos, sin)
        for h in range(H):
            qn_h = rms(qraw[:, h * NOPE:(h + 1) * NOPE], qnw_[:, :NOPE])[0]
            qp_h = _rope(rms(qraw[:, H * NOPE + h * ROPE:H * NOPE + (h + 1) * ROPE], qnw_[:, NOPE:])[0], cos, sin)
            q_out[h] = (jnp.concatenate([qn_h, qp_h], axis=-1) * ATT_SCALE).astype(MXU_DTYPE)
            kn_h = rms(kvraw[:, h * 256:h * 256 + NOPE], knw_[:, :NOPE])[0]
            k_out[h] = jnp.concatenate([kn_h, kpe], axis=-1).astype(MXU_DTYPE)
            v_out[h] = kvraw[:, h * 256 + NOPE:(h + 1) * 256].astype(MXU_DTYPE)

    full = lambda a: pl.BlockSpec(a.shape, lambda i: (0,) * a.ndim)
    return _call_beside(
        body, transfer, grid=(T // tm,), name="mla_pre", scratch_shapes=[], semantics=("arbitrary",),
        args=(proj, proj, proj, cosf, sinf, w_qln, w_kvln, w_uq_p, w_ukv, qnw, knw),
        in_specs=[pl.BlockSpec((tm, 256), lambda i: (i, P_QLAT // 256)),
                  pl.BlockSpec((tm, 256), lambda i: (i, P_KVLAT // 256)),
                  pl.BlockSpec((tm, 128), lambda i: (i, P_KPE // 128)),
                  pl.BlockSpec((tm, ROPE), lambda i: (i, 0)), pl.BlockSpec((tm, ROPE), lambda i: (i, 0)),
                  full(w_qln), full(w_kvln), full(w_uq_p), full(w_ukv), full(qnw), full(knw)],
        out_specs=[pl.BlockSpec((H, tm, QK_DIM), lambda i: (0, i, 0)),
                   pl.BlockSpec((H, tm, QK_DIM), lambda i: (0, i, 0)),
                   pl.BlockSpec((H, tm, V_DIM), lambda i: (0, i, 0))],
        out_shape=[SDS((H, T, QK_DIM), MXU_DTYPE), SDS((H, T, QK_DIM), MXU_DTYPE), SDS((H, T, V_DIM), MXU_DTYPE)])


def _attn_fwd(q4, k4, v4, B, S, transfer=None):
    H = MLA_HEADS
    bq = min(ATTN_BLOCK, S)
    nq = S // bq
    rows = bq // ATTN_CHAINS

    def body(q_ref, k_ref, v_ref, o_ref, lse_ref):
        col = lax.broadcasted_iota(jnp.int32, (rows, bq), 1)
        row = lax.broadcasted_iota(jnp.int32, (rows, bq), 0)

        def q_step(qi, carry):
            qs = pl.multiple_of(qi * bq, bq)
            qsub = [q_ref[0, pl.ds(qs + j * rows, rows), :] for j in range(ATTN_CHAINS)]

            def k_block(ks, cs, diagonal):
                k = k_ref[0, pl.ds(ks, bq), :]
                v = v_ref[0, pl.ds(ks, bq), :]
                out = [None] * ATTN_CHAINS

                def chain(j):
                    m, l, acc = cs[j]
                    s = _mm_nt(qsub[j], k)
                    yield
                    if diagonal:
                        s = jnp.where(col <= row + j * rows, s, -jnp.inf)
                    m_new = jnp.maximum(m, jnp.max(s, axis=-1, keepdims=True))
                    p = jnp.exp(s - m_new)
                    a = jnp.exp(m - m_new)
                    l_new = a * l + jnp.sum(p, axis=-1, keepdims=True)
                    yield
                    out[j] = (m_new, l_new, a * acc + _mm(p, v))

                _lockstep([chain(j) for j in range(ATTN_CHAINS)])
                return tuple(out)

            init = tuple((jnp.full((rows, 1), -jnp.inf, F32), jnp.zeros((rows, 1), F32),
                          jnp.zeros((rows, V_DIM), F32)) for _ in range(ATTN_CHAINS))
            cs = lax.fori_loop(0, qi, lambda kj, c: k_block(pl.multiple_of(kj * bq, bq), c, False), init)
            for j, (m, l, acc) in enumerate(k_block(qs, cs, True)):
                o_ref[0, pl.ds(qs + j * rows, rows), :] = acc / l
                lse_ref[0, pl.ds(qs + j * rows, rows), :] = m + jnp.log(l)
            return carry

        lax.fori_loop(0, nq, q_step, 0)

    spec = lambda d: pl.BlockSpec((1, S, d), lambda h, b: (h, b, 0))
    return _call_beside(
        body, transfer, grid=(H, B), name="attn_fwd",
        in_specs=[spec(QK_DIM), spec(QK_DIM), spec(V_DIM)],
        out_specs=[spec(V_DIM), spec(1)],
        out_shape=[SDS((H, B * S, V_DIM), F32), SDS((H, B * S, 1), F32)],
        scratch_shapes=[], semantics=("arbitrary", "arbitrary"), args=(q4, k4, v4))


def _conv_taps(u, halo, w):
    sh = [_shift_down(u, halo, j) for j in range(CONV_W)]
    c = w[0:1] * sh[3] + w[1:2] * sh[2] + w[2:3] * sh[1] + w[3:4] * sh[0]
    return c, sh


def _gate_values(gab, alog_l, dt_l, lane):
    g = -jnp.exp(alog_l) * jax.nn.softplus(gab + dt_l)
    g = jnp.where(lane < GDN_HEADS, g, 0.0)
    beta = jnp.where((lane >= GDN_HEADS) & (lane < 2 * GDN_HEADS), _sigmoid(gab), 0.0)
    return g, beta


def _unit_lower_inverses(Ls, eye):
    Ps = [eye - L for L in Ls]
    Ms = [_split(-L) for L in Ls]
    for _ in range(5):
        sq = [_mm_split(m, m) for m in Ms]
        Ms = [_split(s) for s in sq]
        Ps = [p + _mm_split(_split(p), m) for p, m in zip(Ps, Ms)]
    return Ps


def _chunk_decays(gt, lane, h, ri, ci, rcol):
    Gc = _pick_lane(gt, lane, h)
    bt = _pick_lane(gt, lane, h + GDN_HEADS)
    Gb = jnp.broadcast_to(Gc, (CHUNK, CHUNK))
    Gam = jnp.where(ri >= ci, jnp.exp(Gb - Gb.T), 0.0)
    Gl = jnp.sum(jnp.where(rcol == CHUNK - 1, Gc, 0.0), axis=0, keepdims=True)
    return Gc, bt, Gam, jnp.exp(Gc), jnp.exp(Gl - Gc), jnp.exp(Gl)


GDN_FWD_UNROLL = 16
GDN_BWD_UNROLL = 8
GDN_RECUR_STEPS_PER_STAGE = 2


def _gdn_fwd(qg, kg, vg, gates, B, S, transfer=None):
    H, D, C = GDN_HEADS, GDN_DIM, CHUNK
    NC = S // C
    P = 2 if B % 2 == 0 else 1
    Sb, NCb = P * S, P * NC
    U = GDN_FWD_UNROLL if NCb % GDN_FWD_UNROLL == 0 else 1
    NG = NCb // U

    def body(q_ref, k_ref, v_ref, g_ref, o_ref, st_ref, ai_ref, u_ref, w_ref, q2_s, au_s, bc_s, w2_s, el_s):
        h = pl.program_id(0)
        lane = lax.broadcasted_iota(jnp.int32, (C, LANES), 1)
        ri = lax.broadcasted_iota(jnp.int32, (C, C), 0)
        ci = lax.broadcasted_iota(jnp.int32, (C, C), 1)
        rcol = lax.broadcasted_iota(jnp.int32, (C, 1), 0)
        eye = (ri == ci).astype(F32)

        def group(gi, c):
            ns = [gi * U + j for j in range(U)]
            css = [pl.multiple_of(n * C, C) for n in ns]
            qs = [q_ref[0, pl.ds(cs, C), :] for cs in css]
            ks = [k_ref[0, pl.ds(cs, C), :] for cs in css]
            vs = [v_ref[0, pl.ds(cs, C), :] for cs in css]
            decs = [_chunk_decays(g_ref[pl.ds(cs, C), :], lane, h, ri, ci, rcol) for cs in css]
            qks = [_mm_nt(jnp.concatenate([q, k], axis=0), k) for q, k in zip(qs, ks)]
            ainvs = _unit_lower_inverses(
                [jnp.where(ri > ci, d[1] * qk[C:] * d[2], 0.0) for qk, d in zip(qks, decs)], eye)
            sols = [_mm_exact(a, jnp.concatenate([v * d[1], k * (d[1] * d[3])], axis=-1))
                    for a, k, v, d in zip(ainvs, ks, vs, decs)]
            atuw = [_mm(qk[:C] * d[2], sol) for qk, d, sol in zip(qks, decs, sols)]
            kduw = [_mm_tn(k * d[4], sol) for k, d, sol in zip(ks, decs, sols)]
            for n, cs, q, a, sol, au, ku, (Gc, bt, Gam, e, f, eL) in zip(ns, css, qs, ainvs, sols, atuw, kduw, decs):
                u_ref[0, pl.ds(cs, C), :] = sol[:, :D]
                w_ref[0, pl.ds(cs, C), :] = sol[:, D:]
                au_s[pl.ds(cs, C), :] = au[:, :D]
                q2_s[pl.ds(cs, C), :] = q * e - au[:, D:]
                bc_s[n] = ku[:, :D]
                w2_s[n] = ku[:, D:]
                el_s[n] = jnp.broadcast_to(eL, (SUBLANES, LANES))
                ai_ref[0, n] = a.T
            return c

        lax.fori_loop(0, NG, group, 0)

        def step(n, states):
            new = []
            for p, S_ in enumerate(states):
                m = p * NC + n
                cs = pl.multiple_of(m * C, C)
                o_ref[0, pl.ds(cs, C), :] = _mm(q2_s[pl.ds(cs, C), :], S_) + au_s[pl.ds(cs, C), :]
                st_ref[0, m] = S_
                new.append(S_ * el_s[m, 0:1, :] + bc_s[m] - _mm(w2_s[m], S_))
            return tuple(new)

        lax.fori_loop(0, NC, step, tuple(jnp.zeros((D, D), F32) for _ in range(P)))

    spec = pl.BlockSpec((1, Sb, D), lambda h, b: (h, b, 0))
    return _call_beside(
        body, transfer, grid=(H, B // P), name="gdn_fwd",
        in_specs=[spec, spec, spec, pl.BlockSpec((Sb, LANES), lambda h, b: (b, 0))],
        out_specs=[spec, pl.BlockSpec((1, NCb, D, D), lambda h, b: (h, b, 0, 0)),
                   pl.BlockSpec((1, NCb, C, C), lambda h, b: (h, b, 0, 0)), spec, spec],
        out_shape=[SDS((H, B * S, D), F32), SDS((H, B * NC, D, D), F32), SDS((H, B * NC, C, C), F32),
                   SDS((H, B * S, D), F32), SDS((H, B * S, D), F32)],
        scratch_shapes=[pltpu.VMEM((Sb, D), F32), pltpu.VMEM((Sb, D), F32), pltpu.VMEM((NCb, D, D), F32),
                        pltpu.VMEM((NCb, D, D), F32), pltpu.VMEM((NCb, SUBLANES, LANES), F32)],
        semantics=("arbitrary", "arbitrary"), args=(qg, kg, vg, gates))


def _mix_out(o_mla, o_gdn, proj, x2, mla_w, gdn_w, w_out):
    T, D = x2.shape
    tm = min(512, T)
    H = MLA_HEADS

    def body(om_ref, og_ref, z_ref, x_ref, mw_ref, gw_ref, w_ref, h_ref, mix_ref):
        z = z_ref[...]
        parts = [_rms(om_ref[h], mw_ref[h:h + 1, :])[0] for h in range(H)]
        for h in range(GDN_HEADS):
            zh = z[:, h * GDN_DIM:(h + 1) * GDN_DIM]
            parts.append(_rms(og_ref[h], gw_ref[...])[0] * (zh * _sigmoid(zh)))
        mix = jnp.concatenate(parts, axis=-1).astype(MXU_DTYPE)
        mix_ref[...] = mix
        h_ref[...] = x_ref[...] + jnp.dot(mix, w_ref[...], preferred_element_type=F32)

    hspec = pl.BlockSpec((H, tm, V_DIM), lambda i: (0, i, 0))
    return pl.pallas_call(
        body, grid=(T // tm,), name="mix_out",
        in_specs=[hspec, hspec, pl.BlockSpec((tm, GDN_WIDTH), lambda i: (i, P_GZ // GDN_WIDTH)),
                  pl.BlockSpec((tm, D), lambda i: (i, 0)),
                  pl.BlockSpec((H, V_DIM), lambda i: (0, 0)), pl.BlockSpec((1, GDN_DIM), lambda i: (0, 0)),
                  pl.BlockSpec((D, D), lambda i: (0, 0))],
        out_specs=[pl.BlockSpec((tm, D), lambda i: (i, 0)), pl.BlockSpec((tm, D), lambda i: (i, 0))],
        out_shape=[SDS((T, D), F32), SDS((T, D), MXU_DTYPE)],
        compiler_params=_params(("arbitrary",)),
    )(o_mla, o_gdn, proj, x2, mla_w, gdn_w, w_out)


def _mlp_fwd(h2, w_mn, w_up, w_down, target):
    T, D = h2.shape
    ns, _, ts = w_up.shape
    F = ns * ts
    tm = min(512, T)
    G = MLP_FWD_SHARDS
    tf, nf = G * ts, ns // G

    def body(h_ref, wn_ref, up_w, down_w, t_ref, up_ref, hn_ref, dy_ref, loss_ref, dyb_ref, y_acc):
        j = pl.program_id(1)

        @pl.when(j == 0)
        def _():
            hn_ref[...] = _rms(h_ref[...], wn_ref[...])[0].astype(MXU_DTYPE)
            y_acc[...] = h_ref[...]

        parts = []
        for c in range(G):
            up = jnp.dot(hn_ref[...], up_w[c], preferred_element_type=F32)
            up_ref[:, c * ts:(c + 1) * ts] = up.astype(MXU_DTYPE)
            r = jnp.maximum(up, 0.0)
            parts.append(_mm(r * r, down_w[c * ts:(c + 1) * ts, :]))
        y_acc[...] += functools.reduce(jnp.add, parts)

        @pl.when(j == nf - 1)
        def _():
            err = y_acc[...] - t_ref[...]
            dy_ref[...] = err / D
            dyb_ref[...] = (err / D).astype(MXU_DTYPE)
            loss_ref[...] = jnp.full((1, SUBLANES, LANES), jnp.sum(err * err), F32)

    return pl.pallas_call(
        body, grid=(T // tm, nf), name="mlp_fwd",
        in_specs=[pl.BlockSpec((tm, D), lambda i, j: (i, 0)), pl.BlockSpec((1, D), lambda i, j: (0, 0)),
                  pl.BlockSpec((G, D, ts), lambda i, j: (j, 0, 0)), pl.BlockSpec((tf, D), lambda i, j: (j, 0)),
                  pl.BlockSpec((tm, D), lambda i, j: (i, 0))],
        out_specs=[pl.BlockSpec((tm, tf), lambda i, j: (i, j)), pl.BlockSpec((tm, D), lambda i, j: (i, 0)),
                   pl.BlockSpec((tm, D), lambda i, j: (i, 0)),
                   pl.BlockSpec((1, SUBLANES, LANES), lambda i, j: (i, 0, 0)),
                   pl.BlockSpec((tm, D), lambda i, j: (i, 0))],
        out_shape=[SDS((T, F), MXU_DTYPE), SDS((T, D), MXU_DTYPE), SDS((T, D), F32),
                   SDS((T // tm, SUBLANES, LANES), F32), SDS((T, D), MXU_DTYPE)],
        scratch_shapes=[pltpu.VMEM((tm, D), F32)],
        compiler_params=_params(("arbitrary", "arbitrary")),
    )(h2, w_mn, w_up, w_down, target)


def _mlp_bwd(dy, dyb, up, h2, w_mn, w_up, w_down, transfer=None):
    T, D = h2.shape
    ns, _, ts = w_up.shape
    F = ns * ts
    tm = min(512, T)
    G = MLP_BWD_SHARDS
    tf, nf = G * ts, ns // G

    def body(dy_ref, dyb_ref, up_ref, h_ref, wn_ref, up_w, down_w, dh_ref, dhb_ref, dup_ref, dwn_ref, acc):
        i, j = pl.program_id(0), pl.program_id(1)

        @pl.when((i == 0) & (j == 0))
        def _():
            dwn_ref[...] = jnp.zeros_like(dwn_ref)

        @pl.when(j == 0)
        def _():
            acc[...] = jnp.zeros_like(acc)

        parts = []
        for c in range(G):
            cols = slice(c * ts, (c + 1) * ts)
            r = jnp.maximum(up_ref[:, cols].astype(F32), 0.0)
            dup = (_mm_nt(dyb_ref[...], down_w[cols, :]) * (2.0 * r)).astype(MXU_DTYPE)
            dup_ref[:, cols] = dup
            parts.append(_mm_nt(dup, up_w[c]))
        acc[...] += functools.reduce(jnp.add, parts)

        @pl.when(j == nf - 1)
        def _():
            hv = h_ref[...]
            _, rr = _rms(hv, wn_ref[...])
            dx, dw = _rms_bwd(acc[...], hv, wn_ref[...], rr)
            dh = dy_ref[...] + dx
            dh_ref[...] = dh
            dhb_ref[...] = dh.astype(MXU_DTYPE)
            dwn_ref[...] += dw

    row = lambda i, j: (i, 0)
    return _call_beside(
        body, transfer, grid=(T // tm, nf), name="mlp_bwd",
        in_specs=[pl.BlockSpec((tm, D), row), pl.BlockSpec((tm, D), row), pl.BlockSpec((tm, tf), lambda i, j: (i, j)),
                  pl.BlockSpec((tm, D), row), pl.BlockSpec((1, D), lambda i, j: (0, 0)),
                  pl.BlockSpec((G, D, ts), lambda i, j: (j, 0, 0)), pl.BlockSpec((tf, D), lambda i, j: (j, 0))],
        out_specs=[pl.BlockSpec((tm, D), row), pl.BlockSpec((tm, D), row),
                   pl.BlockSpec((tm, tf), lambda i, j: (i, j)), pl.BlockSpec((1, D), lambda i, j: (0, 0))],
        out_shape=[SDS((T, D), F32), SDS((T, D), MXU_DTYPE), SDS((T, F), MXU_DTYPE), SDS((1, D), F32)],
        scratch_shapes=[pltpu.VMEM((tm, D), F32)], semantics=("arbitrary", "arbitrary"),
        args=(dy, dyb, up, h2, w_mn, w_up, w_down))


def _mix_bwd(dhb, o_mla, o_gdn, proj, mla_w, gdn_w, w_out):
    T, D = dhb.shape
    tm = min(512, T)
    H = MLA_HEADS

    def body(dh_ref, om_ref, og_ref, z_ref, mw_ref, gw_ref, w_ref, dom_ref, dog_ref, dz_ref, dmw_ref, dgw_ref,
             delta_ref):
        @pl.when(pl.program_id(0) == 0)
        def _():
            dmw_ref[...] = jnp.zeros_like(dmw_ref)
            dgw_ref[...] = jnp.zeros_like(dgw_ref)

        dmix = _mm_nt(dh_ref[...], w_ref[...])
        z = z_ref[...]
        dmw, dzs = [], []
        dgw = jnp.zeros((1, GDN_DIM), F32)
        for h in range(H):
            o = om_ref[h]
            w = mw_ref[h:h + 1, :]
            _, r = _rms(o, w)
            dx, dw = _rms_bwd(dmix[:, h * V_DIM:(h + 1) * V_DIM], o, w, r)
            dom_ref[h] = dx.astype(MXU_DTYPE)
            delta_ref[h] = jnp.sum(dx * o, axis=-1, keepdims=True)
            dmw.append(dw)
        for h in range(GDN_HEADS):
            o = og_ref[h]
            w = gw_ref[...]
            zh = z[:, h * GDN_DIM:(h + 1) * GDN_DIM]
            sg = _sigmoid(zh)
            yn, r = _rms(o, w)
            dy = dmix[:, H * V_DIM + h * GDN_DIM:H * V_DIM + (h + 1) * GDN_DIM]
            dzs.append(dy * yn * (sg * (1.0 + zh * (1.0 - sg))))
            dx, dw = _rms_bwd(dy * (zh * sg), o, w, r)
            dog_ref[h] = dx.astype(MXU_DTYPE)
            dgw = dgw + dw
        dz_ref[...] = jnp.concatenate(dzs, axis=-1).astype(MXU_DTYPE)
        dmw_ref[...] += jnp.concatenate(dmw, axis=0)
        dgw_ref[...] += dgw

    hspec = pl.BlockSpec((H, tm, V_DIM), lambda i: (0, i, 0))
    return pl.pallas_call(
        body, grid=(T // tm,), name="mix_bwd",
        in_specs=[pl.BlockSpec((tm, D), lambda i: (i, 0)), hspec, hspec,
                  pl.BlockSpec((tm, GDN_WIDTH), lambda i: (i, P_GZ // GDN_WIDTH)),
                  pl.BlockSpec((H, V_DIM), lambda i: (0, 0)), pl.BlockSpec((1, GDN_DIM), lambda i: (0, 0)),
                  pl.BlockSpec((D, D), lambda i: (0, 0))],
        out_specs=[hspec, hspec, pl.BlockSpec((tm, GDN_WIDTH), lambda i: (i, 0)),
                   pl.BlockSpec((H, V_DIM), lambda i: (0, 0)), pl.BlockSpec((1, GDN_DIM), lambda i: (0, 0)),
                   pl.BlockSpec((H, tm, 1), lambda i: (0, i, 0))],
        out_shape=[SDS((H, T, V_DIM), MXU_DTYPE), SDS((H, T, GDN_DIM), MXU_DTYPE), SDS((T, GDN_WIDTH), MXU_DTYPE),
                   SDS((H, V_DIM), F32), SDS((1, GDN_DIM), F32), SDS((H, T, 1), F32)],
        compiler_params=_params(("arbitrary",)),
    )(dhb, o_mla, o_gdn, proj, mla_w, gdn_w, w_out)


def _attn_bwd(q4, k4, v4, do4, delta4, lse4, B, S, transfer=None):
    H = MLA_HEADS
    bq = min(ATTN_BLOCK, S)
    nq = S // bq
    rows = bq // ATTN_CHAINS

    def body(q_ref, k_ref, v_ref, do_ref, delta_ref, lse_ref, dq_ref, dk_ref, dv_ref):
        dq_ref[...] = jnp.zeros_like(dq_ref)
        dk_ref[...] = jnp.zeros_like(dk_ref)
        dv_ref[...] = jnp.zeros_like(dv_ref)

        col = lax.broadcasted_iota(jnp.int32, (rows, bq), 1)
        row = lax.broadcasted_iota(jnp.int32, (rows, bq), 0)

        def k_step(kj, carry):
            ks = pl.multiple_of(kj * bq, bq)
            k = k_ref[0, pl.ds(ks, bq), :]
            v = v_ref[0, pl.ds(ks, bq), :]

            def q_block(qs, diagonal):
                dks, dvs = [None] * ATTN_CHAINS, [None] * ATTN_CHAINS

                def chain(j):
                    sl = pl.ds(qs + j * rows, rows)
                    q = q_ref[0, sl, :]
                    do = do_ref[0, sl, :].astype(MXU_DTYPE)
                    s = _mm_nt(q, k)
                    dp = _mm_nt(do, v)
                    yield
                    p = jnp.exp(s - lse_ref[0, sl, :])
                    if diagonal:
                        p = jnp.where(col <= row + j * rows, p, 0.0)
                    ds = p * (dp - delta_ref[0, sl, :])
                    yield
                    dvs[j] = _mm_tn(p, do)
                    dks[j] = _mm_tn(ds, q)
                    dq_ref[0, sl, :] += _mm(ds, k)

                _lockstep([chain(j) for j in range(ATTN_CHAINS)])
                dv_ref[0, pl.ds(ks, bq), :] += functools.reduce(jnp.add, dvs)
                dk_ref[0, pl.ds(ks, bq), :] += functools.reduce(jnp.add, dks)

            q_block(ks, True)

            def q_step(qi, c):
                q_block(pl.multiple_of(qi * bq, bq), False)
                return c

            lax.fori_loop(kj + 1, nq, q_step, 0)
            return carry

        lax.fori_loop(0, nq, k_step, 0)

    spec = lambda d: pl.BlockSpec((1, S, d), lambda h, b: (h, b, 0))
    return _call_beside(
        body, transfer, grid=(H, B), name="attn_bwd",
        in_specs=[spec(QK_DIM), spec(QK_DIM), spec(V_DIM), spec(V_DIM), spec(1), spec(1)],
        out_specs=[spec(QK_DIM), spec(QK_DIM), spec(V_DIM)],
        out_shape=[SDS((H, B * S, QK_DIM), F32), SDS((H, B * S, QK_DIM), F32), SDS((H, B * S, V_DIM), F32)],
        scratch_shapes=[], semantics=("arbitrary", "arbitrary"),
        args=(q4, k4, v4, do4, delta4, lse4))


def _gdn_bwd(qg, kg, vg, gates, states, ainv, u4, w4, do4, B, S, transfer=None):
    H, D, C = GDN_HEADS, GDN_DIM, CHUNK
    NC = S // C
    U = GDN_BWD_UNROLL if NC % GDN_BWD_UNROLL == 0 else 1
    NG = NC // U

    def body(q_ref, k_ref, v_ref, g_ref, st_ref, ai_ref, u_ref, w_ref, do_ref, dq_ref, dk_ref, dv_ref, dgb_ref,
             kd_s, x1_s, x2_s, el_s, dvn_s, ds_s, w2t_s):
        h = pl.program_id(0)
        lane = lax.broadcasted_iota(jnp.int32, (C, LANES), 1)
        ri = lax.broadcasted_iota(jnp.int32, (C, C), 0)
        ci = lax.broadcasted_iota(jnp.int32, (C, C), 1)
        rcol = lax.broadcasted_iota(jnp.int32, (C, 1), 0)

        def rsum(a):
            return jnp.sum(a, axis=-1, keepdims=True)

        def prepare(n):
            cs = n * C
            q = q_ref[0, pl.ds(cs, C), :]
            k = k_ref[0, pl.ds(cs, C), :]
            do = do_ref[0, pl.ds(cs, C), :]
            Gc, bt, Gam, e, f, eL = _chunk_decays(g_ref[pl.ds(cs, C), :], lane, h, ri, ci, rcol)
            At = _mm_nt(q, k) * Gam
            yield
            x1 = _mm_tn(At, do)
            x2 = _mm_tn(q * e, do)
            kd = k * f
            w = w_ref[0, pl.ds(cs, C), :]
            yield
            x1_s[pl.ds(cs, C), :] = x1
            x2_s[n] = x2 - _mm_tn(w, x1)
            w2t_s[n] = _mm_tn(w, kd)
            kd_s[pl.ds(cs, C), :] = kd
            el_s[n] = jnp.broadcast_to(eL, (SUBLANES, LANES))

        def recur(n, dS):
            cs = n * C
            ds_s[n] = dS
            dvn_s[pl.ds(cs, C), :] = x1_s[pl.ds(cs, C), :] + _mm(kd_s[pl.ds(cs, C), :], dS)
            return x2_s[n] + el_s[n, 0:1, :] * dS - _mm(w2t_s[n], dS)

        def local(n):
            cs = n * C
            q = q_ref[0, pl.ds(cs, C), :]
            k = k_ref[0, pl.ds(cs, C), :]
            v = v_ref[0, pl.ds(cs, C), :]
            do = do_ref[0, pl.ds(cs, C), :]
            u = u_ref[0, pl.ds(cs, C), :]
            w = w_ref[0, pl.ds(cs, C), :]
            dvn = dvn_s[pl.ds(cs, C), :]
            dS = ds_s[n]
            Gc, bt, Gam, e, f, eL = _chunk_decays(g_ref[pl.ds(cs, C), :], lane, h, ri, ci, rcol)
            S0 = st_ref[0, n]
            AinvT = ai_ref[0, n]
            qk = _mm_nt(jnp.concatenate([q, k], axis=0), k)
            QK, KK = qk[:C], qk[C:]
            be = bt * e
            sol = jnp.concatenate([u, w], axis=-1)
            vn = u - _mm(w, S0)
            yield
            dAt = jnp.where(ri >= ci, _mm_nt(do, vn), 0.0)
            dqd = _mm_nt(do, S0)
            dw = -_mm_nt(dvn, S0)
            dkd = _mm_nt(vn, dS)
            deL = jnp.sum(rsum(dS * S0), axis=0, keepdims=True)
            yield
            dR = _mm_exact(AinvT, jnp.concatenate([dvn, dw], axis=-1))
            dR1, dR2 = dR[:, :D], dR[:, D:]
            yield
            dL = jnp.where(ri > ci, -_mm_nt(dR, sol), 0.0)
            yield
            dv_ref[0, pl.ds(cs, C), :] = dR1 * bt
            r2 = rsum(dR2 * k)
            X = dL * Gam
            dbt = rsum(dR1 * v) + r2 * e + rsum(X * KK)
            de = r2 * bt + rsum(dqd * q)
            dKK = X * bt
            dQK = dAt * Gam
            dq_ref[0, pl.ds(cs, C), :] = _mm(dQK, k) + dqd * e
            dk_ref[0, pl.ds(cs, C), :] = dR2 * be + _mm(dKK + dKK.T, k) + _mm_tn(dQK, q) + dkd * f
            df = rsum(dkd * k)
            Z = (dL * (bt * KK) + dAt * QK) * Gam
            dG = rsum(Z) - rsum(Z.T) + de * e - df * f
            dGl = jnp.sum(df * f, axis=0, keepdims=True) + deL * eL
            dG = dG + jnp.where(rcol == C - 1, dGl, 0.0)
            dgb_ref[0, pl.ds(cs, C), :] = jnp.where(lane == 0, dG, jnp.where(lane == 1, dbt, 0.0))

        state = [jnp.zeros((D, D), F32)]

        def recur_group(g):
            for j, n in enumerate(reversed(range(g * U, (g + 1) * U))):
                state[0] = recur(n, state[0])
                if j % GDN_RECUR_STEPS_PER_STAGE == GDN_RECUR_STEPS_PER_STAGE - 1:
                    yield

        def stage(fn, g):
            return _together([fn(g * U + j) for j in range(U)])

        for step in range(NG + 2):
            jobs = [(stage, prepare, NG - 1 - step), (None, None, NG - step), (stage, local, NG + 1 - step)]
            _lockstep([recur_group(g) if make is None else make(fn, g) for make, fn, g in jobs if 0 <= g < NG])

    spec = pl.BlockSpec((1, S, D), lambda h, b: (h, b, 0))
    return _call_beside(
        body, transfer, grid=(H, B), name="gdn_bwd",
        in_specs=[spec, spec, spec, pl.BlockSpec((S, LANES), lambda h, b: (b, 0)),
                  pl.BlockSpec((1, NC, D, D), lambda h, b: (h, b, 0, 0)),
                  pl.BlockSpec((1, NC, C, C), lambda h, b: (h, b, 0, 0)), spec, spec, spec],
        out_specs=[spec, spec, spec, spec],
        out_shape=[SDS((H, B * S, D), F32)] * 4,
        scratch_shapes=[pltpu.VMEM((S, D), F32), pltpu.VMEM((S, D), F32), pltpu.VMEM((NC, D, D), F32),
                        pltpu.VMEM((NC, SUBLANES, LANES), F32), pltpu.VMEM((S, D), F32),
                        pltpu.VMEM((NC, D, D), F32), pltpu.VMEM((NC, D, D), F32)],
        semantics=("arbitrary", "arbitrary"), args=(qg, kg, vg, gates, states, ainv, u4, w4, do4))


def _gdn_pre_bwd(proj, conv_w, alog_l, dt_l, dq4, dk4, dv4, dgb4, S):
    T = proj.shape[0]
    tm = min(256, T)
    tiles_per_seq = S // tm
    C3 = 3 * GDN_WIDTH
    H = GDN_HEADS

    def body(u_ref, halo_ref, gab_ref, w_ref, alog_ref, dt_ref, dq_ref, dk_ref, dv_ref, dgb_ref,
             dc_ref, dgab_ref, dcw_ref, dalog_ref, ddt_ref):
        i = pl.program_id(0)

        @pl.when(i == 0)
        def _():
            dcw_ref[...] = jnp.zeros_like(dcw_ref)
            dalog_ref[...] = jnp.zeros_like(dalog_ref)
            ddt_ref[...] = jnp.zeros_like(ddt_ref)

        halo = jnp.where(i % tiles_per_seq == 0, 0.0, halo_ref[...])
        c, sh = _conv_taps(u_ref[...], halo, w_ref[...])
        sg = _sigmoid(c)
        a = c * sg
        das = [None] * (3 * H)
        for h in range(H):
            xq = a[:, h * GDN_DIM:(h + 1) * GDN_DIM]
            xk = a[:, GDN_WIDTH + h * GDN_DIM:GDN_WIDTH + (h + 1) * GDN_DIM]
            das[h] = _l2n_bwd(dq_ref[h], xq, GDN_QSCALE)
            das[H + h] = _l2n_bwd(dk_ref[h], xk, 1.0)
            das[2 * H + h] = dv_ref[h]
        dc = jnp.concatenate(das, axis=-1) * (sg * (1.0 + c * (1.0 - sg)))
        dc_ref[...] = dc
        dcw_ref[...] += jnp.concatenate(
            [jnp.sum(dc * sh[CONV_W - 1 - t], axis=0, keepdims=True) for t in range(CONV_W)], axis=0)
        lane = lax.broadcasted_iota(jnp.int32, (tm, LANES), 1)
        ric = lax.broadcasted_iota(jnp.int32, (tm, LANES), 0) % CHUNK
        dG = jnp.zeros((tm, LANES), F32)
        for h in range(H):
            t = dgb_ref[h]
            dG = dG + jnp.where(lane == h, _pick_lane(t, lane, 0), 0.0) \
                    + jnp.where(lane == h + H, _pick_lane(t, lane, 1), 0.0)
        is_g = lane < H
        dg = jnp.where(is_g, _chunk_rev_cumsum(jnp.where(is_g, dG, 0.0), ric), 0.0)
        gab = gab_ref[...]
        g, beta = _gate_values(gab, alog_ref[...], dt_ref[...], lane)
        dga = jnp.where(is_g, dg * (-jnp.exp(alog_ref[...])) * _sigmoid(gab + dt_ref[...]), 0.0)
        dgb = jnp.where(is_g, 0.0, dG) * beta * (1.0 - beta)
        dgab_ref[...] = (dga + dgb).astype(MXU_DTYPE)
        dalog_ref[...] += jnp.sum(dg * g, axis=0, keepdims=True)
        ddt_ref[...] += jnp.sum(dga, axis=0, keepdims=True)

    hspec = pl.BlockSpec((H, tm, GDN_DIM), lambda i: (0, i, 0))
    vec = pl.BlockSpec((1, LANES), lambda i: (0, 0))
    return pl.pallas_call(
        body, grid=(T // tm,), name="gdn_pre_bwd",
        in_specs=[pl.BlockSpec((tm, C3), lambda i: (i, 0)),
                  pl.BlockSpec((SUBLANES, C3), lambda i: (jnp.maximum(i * (tm // SUBLANES) - 1, 0), 0)),
                  pl.BlockSpec((tm, LANES), lambda i: (i, P_GAB // LANES)),
                  pl.BlockSpec((CONV_W, C3), lambda i: (0, 0)), vec, vec, hspec, hspec, hspec, hspec],
        out_specs=[pl.BlockSpec((tm, C3), lambda i: (i, 0)), pl.BlockSpec((tm, LANES), lambda i: (i, 0)),
                   pl.BlockSpec((CONV_W, C3), lambda i: (0, 0)), vec, vec],
        out_shape=[SDS((T, C3), F32), SDS((T, LANES), MXU_DTYPE), SDS((CONV_W, C3), F32),
                   SDS((1, LANES), F32), SDS((1, LANES), F32)],
        compiler_params=_params(("arbitrary",)),
    )(proj, proj, proj, conv_w, alog_l, dt_l, dq4, dk4, dv4, dgb4)


def _mla_pre_bwd(proj, cosf, sinf, w_qln, w_kvln, w_uq_p, w_ukv, qnw, knw, dq4, dk4, dv4, transfer=None):
    T = proj.shape[0]
    tm = min(256, T)
    H = MLA_HEADS

    def body(ql_ref, kvl_ref, kpe_ref, cos_ref, sin_ref, wq_ref, wkv_ref, uq_ref, ukv_ref, qnw_ref, knw_ref,
             dq_ref, dk_ref, dv_ref,
             dql_ref, dkvl_ref, dkpe_ref, dqraw_ref, dkvraw_ref, qn_ref, kvn_ref, dwq_ref, dwkv_ref, dqnw_ref, dknw_ref):
        @pl.when(pl.program_id(0) == 0)
        def _():
            for r in (dwq_ref, dwkv_ref, dqnw_ref, dknw_ref):
                r[...] = jnp.zeros_like(r)

        cos, sin = cos_ref[...], sin_ref[...]
        qnw_, knw_ = qnw_ref[...], knw_ref[...]
        ql, kvl = ql_ref[...], kvl_ref[...]
        kpe_raw = kpe_ref[...][:, :ROPE]
        rms = functools.partial(_rms, on_mxu=True)
        rms_bwd = functools.partial(_rms_bwd, on_mxu=True)
        qn, rq = rms(ql, wq_ref[...])
        kvn, rkv = rms(kvl, wkv_ref[...])
        qn_ref[...] = qn.astype(MXU_DTYPE)
        kvn_ref[...] = kvn.astype(MXU_DTYPE)
        qraw = _mm(qn, uq_ref[...])
        kvraw = _mm(kvn, ukv_ref[...])
        dq_nope, dq_pe, dkv_parts = [], [], []
        dqnw_n = jnp.zeros((1, NOPE), F32)
        dqnw_p = jnp.zeros((1, ROPE), F32)
        dknw_n = jnp.zeros((1, NOPE), F32)
        dkpe = jnp.zeros((tm, ROPE), F32)
        for h in range(H):
            dq = dq_ref[h] * ATT_SCALE
            x = qraw[:, h * NOPE:(h + 1) * NOPE]
            dx, dw = rms_bwd(dq[:, :NOPE], x, qnw_[:, :NOPE], rms(x, qnw_[:, :NOPE])[1])
            dq_nope.append(dx)
            dqnw_n = dqnw_n + dw
            x = qraw[:, H * NOPE + h * ROPE:H * NOPE + (h + 1) * ROPE]
            dx, dw = rms_bwd(_rope_bwd(dq[:, NOPE:], cos, sin), x, qnw_[:, NOPE:], rms(x, qnw_[:, NOPE:])[1])
            dq_pe.append(dx)
            dqnw_p = dqnw_p + dw
            dk = dk_ref[h]
            x = kvraw[:, h * 256:h * 256 + NOPE]
            dx, dw = rms_bwd(dk[:, :NOPE], x, knw_[:, :NOPE], rms(x, knw_[:, :NOPE])[1])
            dknw_n = dknw_n + dw
            dkpe = dkpe + dk[:, NOPE:]
            dkv_parts += [dx, dv_ref[h]]
        dx, dknw_p = rms_bwd(_rope_bwd(dkpe, cos, sin), kpe_raw, knw_[:, NOPE:], rms(kpe_raw, knw_[:, NOPE:])[1])
        dkpe_ref[...] = jnp.concatenate([dx, jnp.zeros((tm, LANES - ROPE), F32)], axis=-1).astype(MXU_DTYPE)
        dqraw = jnp.concatenate(dq_nope + dq_pe, axis=-1).astype(MXU_DTYPE)
        dkvraw = jnp.concatenate(dkv_parts, axis=-1).astype(MXU_DTYPE)
        dqraw_ref[...] = dqraw
        dkvraw_ref[...] = dkvraw
        dx, dw = rms_bwd(_mm_nt(dqraw, uq_ref[...]), ql, wq_ref[...], rq)
        dql_ref[...] = dx.astype(MXU_DTYPE)
        dwq_ref[...] += dw
        dx, dw = rms_bwd(_mm_nt(dkvraw, ukv_ref[...]), kvl, wkv_ref[...], rkv)
        dkvl_ref[...] = dx.astype(MXU_DTYPE)
        dwkv_ref[...] += dw
        dqnw_ref[...] += jnp.concatenate([dqnw_n, dqnw_p], axis=-1)
        dknw_ref[...] += jnp.concatenate([dknw_n, dknw_p], axis=-1)

    full = lambda a: pl.BlockSpec(a.shape, lambda i: (0,) * a.ndim)
    rows = lambda n: pl.BlockSpec((tm, n), lambda i: (i, 0))
    const = lambda n: pl.BlockSpec((1, n), lambda i: (0, 0))
    NQ, NKV = w_uq_p.shape[1], w_ukv.shape[1]
    return _call_beside(
        body, transfer, grid=(T // tm,), name="mla_pre_bwd", scratch_shapes=[], semantics=("arbitrary",),
        args=(proj, proj, proj, cosf, sinf, w_qln, w_kvln, w_uq_p, w_ukv, qnw, knw, dq4, dk4, dv4),
        in_specs=[pl.BlockSpec((tm, 256), lambda i: (i, P_QLAT // 256)),
                  pl.BlockSpec((tm, 256), lambda i: (i, P_KVLAT // 256)),
                  pl.BlockSpec((tm, 128), lambda i: (i, P_KPE // 128)),
                  rows(ROPE), rows(ROPE),
                  full(w_qln), full(w_kvln), full(w_uq_p), full(w_ukv), full(qnw), full(knw),
                  pl.BlockSpec((H, tm, QK_DIM), lambda i: (0, i, 0)),
                  pl.BlockSpec((H, tm, QK_DIM), lambda i: (0, i, 0)),
                  pl.BlockSpec((H, tm, V_DIM), lambda i: (0, i, 0))],
        out_specs=[rows(Q_LORA), rows(KV_LORA), rows(LANES), rows(NQ), rows(NKV), rows(Q_LORA), rows(KV_LORA),
                   const(Q_LORA), const(KV_LORA), const(QK_DIM), const(QK_DIM)],
        out_shape=[SDS((T, Q_LORA), MXU_DTYPE), SDS((T, KV_LORA), MXU_DTYPE), SDS((T, LANES), MXU_DTYPE),
                   SDS((T, NQ), MXU_DTYPE), SDS((T, NKV), MXU_DTYPE),
                   SDS((T, Q_LORA), MXU_DTYPE), SDS((T, KV_LORA), MXU_DTYPE),
                   SDS((1, Q_LORA), F32), SDS((1, KV_LORA), F32), SDS((1, QK_DIM), F32), SDS((1, QK_DIM), F32)])


def _in_proj_bwd(dc, conv_w, dgz, dql, dkvl, dkpe, dgab, w_in_p, dh, x2, w_an, S):
    T, D = x2.shape
    N = w_in_p.shape[1]
    C3 = dc.shape[1]
    tm = min(512, S)
    assert S % tm == 0 and T % tm == 0, "a token tile must not straddle two sequences"
    tiles_per_seq = S // tm
    nblk = T // SUBLANES

    def body(dc_ref, nxt_ref, cw_ref, b_ref, c_ref, d_ref, e_ref, f_ref, w_ref, dh_ref, x_ref, wn_ref,
             dx_ref, dp_ref, dwn_ref):
        i = pl.program_id(0)

        @pl.when(i == 0)
        def _():
            dwn_ref[...] = jnp.zeros_like(dwn_ref)

        nxt = jnp.where(i % tiles_per_seq == tiles_per_seq - 1, 0.0, nxt_ref[...])
        dcv, cw = dc_ref[...], cw_ref[...]
        du = cw[3:4] * dcv
        for j in range(1, CONV_W):
            du = du + cw[3 - j:4 - j] * _shift_up(dcv, nxt, j)
        dp = jnp.concatenate([du.astype(MXU_DTYPE), b_ref[...], c_ref[...], d_ref[...], e_ref[...], f_ref[...]],
                             axis=-1).astype(MXU_DTYPE)
        dp_ref[...] = dp
        x = x_ref[...]
        _, r = _rms(x, wn_ref[...])
        dx, dw = _rms_bwd(_mm_nt(dp, w_ref[...]), x, wn_ref[...], r)
        dx_ref[...] = dh_ref[...] + dx
        dwn_ref[...] += dw

    rows = lambda n: pl.BlockSpec((tm, n), lambda i: (i, 0))
    return pl.pallas_call(
        body, grid=(T // tm,), name="in_proj_bwd",
        in_specs=[rows(C3),
                  pl.BlockSpec((SUBLANES, C3), lambda i: (jnp.minimum((i + 1) * (tm // SUBLANES), nblk - 1), 0)),
                  pl.BlockSpec((CONV_W, C3), lambda i: (0, 0)),
                  rows(dgz.shape[1]), rows(dql.shape[1]), rows(dkvl.shape[1]),
                  rows(dkpe.shape[1]), rows(dgab.shape[1]),
                  pl.BlockSpec((D, N), lambda i: (0, 0)), rows(D), rows(D), pl.BlockSpec((1, D), lambda i: (0, 0))],
        out_specs=[rows(D), rows(N), pl.BlockSpec((1, D), lambda i: (0, 0))],
        out_shape=[SDS((T, D), F32), SDS((T, N), MXU_DTYPE), SDS((1, D), F32)],
        compiler_params=_params(("arbitrary",)),
    )(dc, dc, conv_w, dgz, dql, dkvl, dkpe, dgab, w_in_p, dh, x2, w_an)


def _relu_squared(t):
    r = jnp.maximum(t.astype(F32), 0.0)
    return (r * r).astype(MXU_DTYPE)


def _wgrad(a, b, name, column_shards=False, a_map=None):
    T, M = a.shape
    N = b.shape[1]
    tM = _divisor_tile(M, 1024)
    tN = N // N_DEV if column_shards else _divisor_tile(N, 1536)
    tk = min(T, 2048)
    nk = T // tk

    def body(a_ref, b_ref, o_ref, acc):
        k = pl.program_id(2)

        @pl.when(k == 0)
        def _():
            acc[...] = jnp.zeros_like(acc)

        acc[...] += _mm_tn(a_ref[...] if a_map is None else a_map(a_ref[...]), b_ref[...])

        @pl.when(k == nk - 1)
        def _():
            o_ref[...] = acc[...].astype(WIRE_DTYPE).reshape(o_ref.shape)

    if column_shards:
        out_spec, out_shape = pl.BlockSpec((1, tM, tN), lambda i, j, k: (j, i, 0)), SDS((N_DEV, M, tN), WIRE_DTYPE)
    else:
        out_spec, out_shape = pl.BlockSpec((tM, tN), lambda i, j, k: (i, j)), SDS((M, N), WIRE_DTYPE)
    return pl.pallas_call(
        body, grid=(M // tM, N // tN, nk), name=name,
        in_specs=[pl.BlockSpec((tk, tM), lambda i, j, k: (k, i)), pl.BlockSpec((tk, tN), lambda i, j, k: (k, j))],
        out_specs=out_spec, out_shape=out_shape,
        scratch_shapes=[pltpu.VMEM((tM, tN), F32)],
        compiler_params=_params(("arbitrary", "arbitrary", "arbitrary")),
    )(a, b)


def _adamw(g, w, m, v):
    m = ADAM_B1 * m + (1.0 - ADAM_B1) * g
    v = ADAM_B2 * v + (1.0 - ADAM_B2) * jnp.square(g)
    m_hat = m / (1.0 - ADAM_B1 ** ADAM_STEP)
    v_hat = v / (1.0 - ADAM_B2 ** ADAM_STEP)
    return -ADAM_LR * (m_hat / (jnp.sqrt(v_hat) + ADAM_EPS) + ADAM_WD * w), m, v


def _reduce_adamw(parts, w, m, v, name):
    R, C = w.shape
    slots, Rp, Cp = parts.shape
    tr = min(R, 256)
    tp = tr if Rp == R else Rp

    def body(p_ref, w_ref, m_ref, v_ref, g_ref, d_ref, nm_ref, nv_ref):
        g = p_ref[0].astype(F32)
        for s in range(1, slots):
            g = g + p_ref[s].astype(F32)
        g = g[:tr, :C]
        g_ref[...] = g
        d_ref[...], nm_ref[...], nv_ref[...] = _adamw(g, w_ref[...], m_ref[...], v_ref[...])

    spec = pl.BlockSpec((tr, C), lambda i: (i, 0))
    return pl.pallas_call(
        body, grid=(R // tr,), name=name,
        in_specs=[pl.BlockSpec((slots, tp, Cp), lambda i: (0, i, 0)), spec, spec, spec],
        out_specs=[spec] * 4, out_shape=[SDS((R, C), F32)] * 4,
        compiler_params=_params(("arbitrary",)),
    )(parts, w, m, v)


SMALL_ROWS, SMALL_COLS = 16, 1024
SMALL_LAYOUT = (
    ("attn_norm_w", 0, 1, 1024, 1024), ("mlp_norm_w", 1, 1, 1024, 1024), ("q_lat_norm_w", 2, 1, 256, 256),
    ("kv_lat_norm_w", 3, 1, 256, 256), ("q_norm_w", 4, 1, 192, 192), ("k_norm_w", 5, 1, 192, 192),
    ("mla_out_norm_w", 6, 4, 128, 128), ("a_log", 10, 1, 128, 4), ("dt_bias", 11, 1, 128, 4),
    ("gdn_norm_w", 12, 1, 128, 128))
LOSS_ENTRY = ("loss", 13, 1, 128, 128)


def _adamw_replicated(parts, ws, ms, vs):
    n = len(SMALL_LAYOUT)

    def body(*refs):
        p_ref = refs[0]
        w_refs, m_refs, v_refs = refs[1:1 + n], refs[1 + n:1 + 2 * n], refs[1 + 2 * n:1 + 3 * n]
        outs = refs[1 + 3 * n:]
        s = p_ref[0]
        for d in range(1, N_DEV):
            s = s + p_ref[d]
        for i, (_, r0, nr, _, pw) in enumerate(SMALL_LAYOUT):
            g = s[r0:r0 + nr, :pw]
            outs[i][...] = g
            outs[n + i][...], outs[2 * n + i][...], outs[3 * n + i][...] = _adamw(
                g, w_refs[i][...], m_refs[i][...], v_refs[i][...])
        _, r0, nr, gw, _ = LOSS_ENTRY
        outs[4 * n][...] = s[r0:r0 + nr, :gw]

    res = pl.pallas_call(
        body, name="adamw_replicated",
        out_shape=[SDS(w.shape, F32) for w in ws] * 4 + [SDS((1, LANES), F32)],
        compiler_params=_params(),
    )(parts, *ws, *ms, *vs)
    return [res[k * n:(k + 1) * n] for k in range(4)], res[4 * n][0, 0]


COPIES_PER_ARRAY = N_DEV - 1


def _two_level_gather(srcs, outs, send_sems, recv_sems, local_sems=None, stage="all"):
    mx, my, mc = lax.axis_index("x"), lax.axis_index("y"), lax.axis_index("c")
    me, sibling = (mx, my, mc), (mx, my, 1 - mc)
    chips = [(1 - mx, my), (mx, 1 - my), (1 - mx, 1 - my)]
    arrays = range(len(srcs))

    def copy(a, k, block, to, src=None):
        px, py, pc = block
        slot = outs[a].at[4 * px + 2 * py + pc]
        sem = a * COPIES_PER_ARRAY + k
        return pltpu.make_async_remote_copy(
            src_ref=slot if src is None else src, dst_ref=slot,
            send_sem=send_sems.at[sem], recv_sem=recv_sems.at[sem], device_id=to, device_id_type=MESH_ID)

    mine = [] if local_sems is None else [
        pltpu.make_async_copy(srcs[a], outs[a].at[4 * mx + 2 * my + mc], local_sems.at[a]) for a in arrays]
    first = []
    for a in arrays:
        first.append(copy(a, 0, me, sibling, src=srcs[a]))
        first += [copy(a, 1 + j, me, (*chip, mc), src=srcs[a]) for j, chip in enumerate(chips)]
    forwards = [copy(a, 4 + j, (*chip, mc), sibling) for j, chip in enumerate(chips) for a in arrays]
    if stage in ("all", "start"):
        for cp in mine + first:
            cp.start()
    if stage in ("all", "forward"):
        for j, chip in enumerate(chips):
            for a in arrays:
                copy(a, 1 + j, (*chip, mc), me).wait_recv()
                forwards[j * len(srcs) + a].start()
    if stage in ("all", "finish"):
        for a in arrays:
            copy(a, 0, sibling, me).wait_recv()
        for j, chip in enumerate(chips):
            for a in arrays:
                copy(a, 4 + j, (*chip, 1 - mc), me).wait_recv()
        for cp in first + forwards:
            cp.wait_send()
        for cp in mine:
            cp.wait()


def _comm_scratch(n):
    return [pltpu.SemaphoreType.DMA((n * COPIES_PER_ARRAY,)), pltpu.SemaphoreType.DMA((n * COPIES_PER_ARRAY,)),
            pltpu.SemaphoreType.DMA((n,))]


def _any_specs(n):
    return [pl.BlockSpec(memory_space=pl.ANY)] * n


def _gather_weights(shards):
    n = len(shards)

    def body(*refs):
        _two_level_gather(refs[:n], refs[n:2 * n], *refs[2 * n:])

    return pl.pallas_call(
        body, name="gather_weights",
        out_shape=[SDS((N_DEV,) + s.shape, s.dtype) for s in shards],
        in_specs=_any_specs(n), out_specs=_any_specs(n), scratch_shapes=_comm_scratch(n),
    )(*shards)


def _gather_small_grads(gs, loss_lanes):
    gs = list(gs) + [loss_lanes]
    n = len(gs)

    def body(*refs):
        g_refs, out_ref = refs[:n], refs[n]
        tile, send_sems, recv_sems = refs[n + 1:]
        tile[...] = jnp.zeros_like(tile)
        for (_, r0, nr, gw, _), g in zip(SMALL_LAYOUT + (LOSS_ENTRY,), g_refs):
            tile[r0:r0 + nr, 0:gw] = g[...]
        me = 4 * lax.axis_index("x") + 2 * lax.axis_index("y") + lax.axis_index("c")
        out_ref[me] = tile[...]
        _two_level_gather([tile], [out_ref], send_sems, recv_sems)

    return pl.pallas_call(
        body, name="gather_small_grads",
        out_shape=SDS((N_DEV, SMALL_ROWS, SMALL_COLS), F32),
        in_specs=[pl.BlockSpec(memory_space=pltpu.VMEM)] * n,
        out_specs=pl.BlockSpec(memory_space=pltpu.VMEM),
        scratch_shapes=[pltpu.VMEM((SMALL_ROWS, SMALL_COLS), F32),
                        pltpu.SemaphoreType.DMA((COPIES_PER_ARRAY,)), pltpu.SemaphoreType.DMA((COPIES_PER_ARRAY,))],
    )(*gs)


def _exchange_grads(slabs):
    n = len(slabs)

    def body(*refs):
        _exchange(refs[:n], refs[n:2 * n], *refs[2 * n:])

    return pl.pallas_call(
        body, name="exchange_grads",
        out_shape=[SDS(s.shape, s.dtype) for s in slabs],
        in_specs=_any_specs(n), out_specs=_any_specs(n), scratch_shapes=_comm_scratch(n),
    )(*slabs)


def _exchange_grads_two_level(big, small):
    _, R, C = big.shape
    chip_flips = ((1, 0), (0, 1), (1, 1))

    def body(big_ref, small_ref, out_ref, small_out, mine_v, sib_v, pre_v, d2d_send, d2d_recv, ici_send, ici_recv,
             local_sems, s_send, s_recv, s_local):
        mx, my, mc = lax.axis_index("x"), lax.axis_index("y"), lax.axis_index("c")
        sibling = (mx, my, 1 - mc)
        chips = [(px, py) for px in range(2) for py in range(2)]
        _exchange([small_ref], [small_out], s_send, s_recv, s_local, stage="start")
        own = [pltpu.make_async_copy(big_ref.at[4 * px + 2 * py + mc], mine_v.at[q], local_sems.at[q])
               for q, (px, py) in enumerate(chips)]
        d2d = [pltpu.make_async_remote_copy(
            src_ref=big_ref.at[4 * px + 2 * py + (1 - mc)], dst_ref=sib_v.at[q], send_sem=d2d_send.at[q],
            recv_sem=d2d_recv.at[q], device_id=sibling, device_id_type=MESH_ID) for q, (px, py) in enumerate(chips)]
        for cp in own + d2d:
            cp.start()
        for cp in own + d2d:
            cp.wait()
        for q in range(4):
            pre_v[q] = (mine_v[q].astype(F32) + sib_v[q].astype(F32)).astype(pre_v.dtype)
        ici = []
        for k, (fx, fy) in enumerate(chip_flips):
            px = 1 - mx if fx else mx
            py = 1 - my if fy else my
            ici.append(pltpu.make_async_remote_copy(
                src_ref=pre_v.at[2 * px + py], dst_ref=out_ref.at[k], send_sem=ici_send.at[k],
                recv_sem=ici_recv.at[k], device_id=(px, py, mc), device_id_type=MESH_ID))
        keep = pltpu.make_async_copy(pre_v.at[2 * mx + my], out_ref.at[3], local_sems.at[4])
        for cp in ici + [keep]:
            cp.start()
        for cp in ici + [keep]:
            cp.wait()
        _exchange([small_ref], [small_out], s_send, s_recv, s_local, stage="finish")

    dma = pltpu.SemaphoreType.DMA
    return pl.pallas_call(
        body, name="exchange_grads",
        out_shape=[SDS((4, R, C), big.dtype), SDS(small.shape, small.dtype)],
        in_specs=_any_specs(2), out_specs=_any_specs(2),
        scratch_shapes=[pltpu.VMEM((4, R, C), big.dtype)] * 3 + [dma((4,)), dma((4,)), dma((3,)), dma((3,)), dma((5,))]
                       + _comm_scratch(1),
        compiler_params=_params(),
    )(big, small)


class _Transfer:
    def __init__(self, kind, arrays):
        self.kind, self.arrays, self.n = kind, list(arrays), len(arrays)

    def out_shapes(self):
        if self.kind == "gather":
            return [SDS((N_DEV,) + a.shape, a.dtype) for a in self.arrays]
        return [SDS(a.shape, a.dtype) for a in self.arrays]

    def run(self, srcs, outs, sems, stage):
        fn = _two_level_gather if self.kind == "gather" else _exchange
        fn(srcs, outs, *sems, stage=stage)


def _call_beside(body, transfer, *, grid, in_specs, out_specs, out_shape, scratch_shapes, name, semantics, args):
    if transfer is None:
        res = pl.pallas_call(body, grid=grid, in_specs=in_specs, out_specs=out_specs, out_shape=out_shape,
                             scratch_shapes=scratch_shapes, name=name, compiler_params=_params(semantics))(*args)
        return list(res), []
    n_in, n_out, n_s, n = len(in_specs), len(out_specs), len(scratch_shapes), transfer.n
    total = functools.reduce(lambda a, b: a * b, grid, 1)

    def wrapped(*refs):
        ins, refs = refs[:n_in], refs[n_in:]
        t_in, refs = refs[:n], refs[n:]
        outs, refs = refs[:n_out], refs[n_out:]
        t_out, refs = refs[:n], refs[n:]
        scratch, sems = refs[:n_s], refs[n_s:]
        first = functools.reduce(jnp.logical_and, [pl.program_id(i) == 0 for i in range(len(grid))])
        last = functools.reduce(jnp.logical_and, [pl.program_id(i) == g - 1 for i, g in enumerate(grid)])

        @pl.when(first)
        def _():
            transfer.run(t_in, t_out, sems, "start")

        step = functools.reduce(lambda acc, ig: acc * ig[1] + pl.program_id(ig[0]), enumerate(grid), 0)

        @pl.when(step == (3 * total) // 4)
        def _():
            transfer.run(t_in, t_out, sems, "forward")

        body(*ins, *outs, *scratch)

        @pl.when(last)
        def _():
            transfer.run(t_in, t_out, sems, "finish")

    res = pl.pallas_call(
        wrapped, grid=grid, in_specs=list(in_specs) + _any_specs(n), out_specs=list(out_specs) + _any_specs(n),
        out_shape=list(out_shape) + transfer.out_shapes(), scratch_shapes=list(scratch_shapes) + _comm_scratch(n),
        name=name, compiler_params=_params(semantics))(*args, *transfer.arrays)
    return list(res[:n_out]), list(res[n_out:])


EXCHANGE_FLIPS = ((0, 0, 1), (1, 0, 0), (0, 1, 0), (1, 1, 0), (1, 0, 1), (0, 1, 1), (1, 1, 1))


def _exchange(srcs, outs, send_sems, recv_sems, local_sems, stage="all"):
    mx, my, mc = lax.axis_index("x"), lax.axis_index("y"), lax.axis_index("c")
    arrays = range(len(srcs))
    copies = [pltpu.make_async_copy(srcs[a].at[4 * mx + 2 * my + mc], outs[a].at[N_DEV - 1], local_sems.at[a])
              for a in arrays]
    for k, (fx, fy, fc) in enumerate(EXCHANGE_FLIPS):
        px = 1 - mx if fx else mx
        py = 1 - my if fy else my
        pc = 1 - mc if fc else mc
        for a in arrays:
            sem = a * COPIES_PER_ARRAY + k
            copies.append(pltpu.make_async_remote_copy(
                src_ref=srcs[a].at[4 * px + 2 * py + pc], dst_ref=outs[a].at[k],
                send_sem=send_sems.at[sem], recv_sem=recv_sems.at[sem],
                device_id=(px, py, pc), device_id_type=MESH_ID))
    if stage in ("all", "start"):
        for cp in copies:
            cp.start()
    if stage in ("all", "finish"):
        for cp in copies:
            cp.wait()


def _w_in_to_padded(w):
    z = lambda n: jnp.zeros((w.shape[0], n), w.dtype)
    return jnp.concatenate([w[:, O_GQKV:O_GZ], w[:, O_GZ:O_GAB], w[:, O_QLAT:O_KVLAT], w[:, O_KVLAT:O_KPE],
                            w[:, O_KPE:O_GQKV], z(P_GAB - P_KPE - ROPE), w[:, O_GAB:O_END],
                            z(P_WIDTH - P_GAB - (O_END - O_GAB))], axis=1)


def _w_in_from_padded(wp):
    return jnp.concatenate([wp[:, P_QLAT:P_QLAT + 256], wp[:, P_KVLAT:P_KVLAT + 256], wp[:, P_KPE:P_KPE + ROPE],
                            wp[:, P_GQKV:P_GZ], wp[:, P_GZ:P_QLAT], wp[:, P_GAB:P_GAB + (O_END - O_GAB)]], axis=1)


W_IN_SHARD_COLS = (O_END - O_QLAT) // N_DEV


def _w_in_shards_to_padded(stack):
    _, R, Cw = stack.shape
    tr = min(R, 256)

    def body(s_ref, o_ref):
        full = jnp.concatenate([s_ref[d].astype(F32)[:, :W_IN_SHARD_COLS] for d in range(N_DEV)], axis=-1)
        o_ref[...] = _w_in_to_padded(full).astype(o_ref.dtype)

    return pl.pallas_call(
        body, grid=(R // tr,), name="w_in_to_padded",
        in_specs=[pl.BlockSpec((N_DEV, tr, Cw), lambda i: (0, i, 0))],
        out_specs=pl.BlockSpec((tr, P_WIDTH), lambda i: (i, 0)),
        out_shape=SDS((R, P_WIDTH), stack.dtype), compiler_params=_params(("arbitrary",)),
    )(stack)


def _w_in_padded_to_slabs(gp, wire_cols):
    R = gp.shape[0]
    tr = min(R, 256)

    def body(g_ref, o_ref):
        orig = _w_in_from_padded(g_ref[...].astype(F32))
        for d in range(N_DEV):
            piece = orig[:, d * W_IN_SHARD_COLS:(d + 1) * W_IN_SHARD_COLS]
            o_ref[d] = _pad2(piece, tr, wire_cols).astype(o_ref.dtype)

    return pl.pallas_call(
        body, grid=(R // tr,), name="w_in_to_slabs",
        in_specs=[pl.BlockSpec((tr, P_WIDTH), lambda i: (i, 0))],
        out_specs=pl.BlockSpec((N_DEV, tr, wire_cols), lambda i: (0, i, 0)),
        out_shape=SDS((N_DEV, R, wire_cols), gp.dtype), compiler_params=_params(("arbitrary",)),
    )(gp)


def _w_uq_to_headsplit(w):
    w3 = w.reshape(w.shape[0], MLA_HEADS, QK_DIM)
    return jnp.concatenate([w3[:, :, :NOPE].reshape(w.shape[0], -1), w3[:, :, NOPE:].reshape(w.shape[0], -1)], axis=1)


def _w_uq_from_headsplit(wp):
    n = wp[:, :MLA_HEADS * NOPE].reshape(wp.shape[0], MLA_HEADS, NOPE)
    p = wp[:, MLA_HEADS * NOPE:].reshape(wp.shape[0], MLA_HEADS, ROPE)
    return jnp.concatenate([n, p], axis=2).reshape(wp.shape[0], -1)


def _lane_vec(v4):
    return jnp.pad(v4.reshape(1, -1), ((0, 0), (0, LANES - v4.shape[-1])))


def _local_step(x, positions, target, attn_norm_w, w_in, q_lat_norm_w, w_uq, kv_lat_norm_w, w_ukv, q_norm_w,
                k_norm_w, mla_out_norm_w, conv_w, a_log, dt_bias, gdn_norm_w, w_out, mlp_norm_w, w_up, w_down,
                late_shards=None, exchange=False):
    B, S, D = x.shape
    T = B * S
    x2 = x.reshape(T, D)
    t2 = target.reshape(T, D)
    half = ROPE // 2
    inv_freq = ROPE_THETA ** (-jnp.arange(half, dtype=F32) / half)
    ang = positions.reshape(T, 1).astype(F32) * inv_freq
    cosf = jnp.concatenate([jnp.cos(ang)] * 2, axis=-1)
    sinf = jnp.concatenate([jnp.sin(ang)] * 2, axis=-1)
    w_in_p = w_in
    w_uq_p = _w_uq_to_headsplit(w_uq)
    alog_l, dt_l = _lane_vec(a_log), _lane_vec(dt_bias)
    w_an, w_qln, w_kvln, qnw, knw, w_mn, gdn_w = (
        attn_norm_w, q_lat_norm_w, kv_lat_norm_w, q_norm_w, k_norm_w, mlp_norm_w, gdn_norm_w)

    proj, xn, qg, kg, vg, gates = _in_proj(x2, w_an, w_in_p, conv_w, alog_l, dt_l, S)
    def gathering(shards):
        return None if late_shards is None else _Transfer("gather", shards)

    (q4, k4, v4), late = _mla_pre(proj, cosf, sinf, w_qln, w_kvln, w_uq_p, w_ukv, qnw, knw,
                                  gathering(late_shards and late_shards[:1]))
    if late:
        w_out = late[0].reshape(-1, D)
    (o_mla, lse), late = _attn_fwd(q4, k4, v4, B, S, gathering(late_shards and late_shards[2:]))
    if late:
        w_down = late[0].reshape(-1, D)
    (o_gdn, states, ainv, u4, w4), late = _gdn_fwd(qg, kg, vg, gates, B, S,
                                                   gathering(late_shards and late_shards[1:2]))
    if late:
        w_up = late[0]
    h2, mix = _mix_out(o_mla, o_gdn, proj, x2, mla_out_norm_w, gdn_w, w_out)
    up, hn, dy, sq, dyb = _mlp_fwd(h2, w_mn, w_up, w_down, t2)
    loss = (0.5 / D) * jnp.sum(sq[:, 0, 0])

    first = ("w_down",)
    second = ("w_up",)
    third = ("w_out", "w_uq", "w_ukv")
    mats = dict(w_down=_wgrad(up, dyb, "wgrad_down", a_map=_relu_squared))

    def sending(names):
        return _Transfer("exchange", [_slabs(n, mats[n]) for n in names]) if exchange else None

    (dh, dhb, dup, d_mlp_norm), got = _mlp_bwd(dy, dyb, up, h2, w_mn, w_up, w_down, sending(first))
    mats.update(zip(first, got))
    mats.update(w_up=_wgrad(hn, dup, "wgrad_up", column_shards=True))
    do_mla, do_gdn, dz, d_mla_w, d_gdn_w, delta = _mix_bwd(dhb, o_mla, o_gdn, proj, mla_out_norm_w, gdn_w, w_out)
    mats.update(w_out=_wgrad(mix, dhb, "wgrad_out"))
    (dq4, dk4, dv4), got = _attn_bwd(q4, k4, v4, do_mla, delta, lse, B, S, sending(second))
    mats.update(zip(second, got))
    (dql, dkvl, dkpe, dqraw, dkvraw, qn, kvn, d_wqln, d_wkvln, d_qnw, d_knw), _ = _mla_pre_bwd(
        proj, cosf, sinf, w_qln, w_kvln, w_uq_p, w_ukv, qnw, knw, dq4, dk4, dv4)
    mats.update(w_uq=_wgrad(qn, dqraw, "wgrad_uq"), w_ukv=_wgrad(kvn, dkvraw, "wgrad_ukv"))
    (dqg, dkg, dvg, dgb4), got = _gdn_bwd(qg, kg, vg, gates, states, ainv, u4, w4, do_gdn, B, S, sending(third))
    mats.update(zip(third, got))
    dc, dgab, g_conv, d_alog, d_dt = _gdn_pre_bwd(proj, conv_w, alog_l, dt_l, dqg, dkg, dvg, dgb4, S)
    grad_x2, dproj, d_attn_norm = _in_proj_bwd(dc, conv_w, dz, dql, dkvl, dkpe, dgab, w_in_p, dh, x2, w_an, S)
    mats.update(w_in=_wgrad(xn, dproj, "wgrad_in"), conv_w=g_conv)
    if exchange:
        last = ("w_in", "conv_w")
        mats.update(zip(last, _exchange_grads_two_level(*[_slabs(n, mats[n]) for n in last])))
    small = dict(attn_norm_w=d_attn_norm, mlp_norm_w=d_mlp_norm, q_lat_norm_w=d_wqln, kv_lat_norm_w=d_wkvln,
                 q_norm_w=d_qnw, k_norm_w=d_knw, mla_out_norm_w=d_mla_w, a_log=d_alog, dt_bias=d_dt,
                 gdn_norm_w=d_gdn_w)
    return loss, grad_x2.reshape(B, S, D), mats, [small[n] for n, *_ in SMALL_LAYOUT]


BIG = ("w_in", "w_uq", "w_ukv", "conv_w", "w_out", "w_up", "w_down")
ALL_W = ("attn_norm_w", "w_in", "q_lat_norm_w", "w_uq", "kv_lat_norm_w", "w_ukv", "q_norm_w", "k_norm_w",
         "mla_out_norm_w", "conv_w", "a_log", "dt_bias", "gdn_norm_w", "w_out", "mlp_norm_w", "w_up", "w_down")
WIRE_SHAPE = {"w_in": (1024, 384), "w_uq": (256, 128), "conv_w": (16, 256)}


def _pad2(a, rows, cols):
    return jnp.pad(a, [(0, 0)] * (a.ndim - 2) + [(0, rows - a.shape[-2]), (0, cols - a.shape[-1])])


def _cols_to_full(stack, cols):
    return jnp.moveaxis(stack[:, :, :cols], 0, 1).reshape(stack.shape[1], N_DEV * cols)


def _full_to_cols(full, wire_cols):
    r, n = full.shape
    return _pad2(jnp.moveaxis(full.reshape(r, N_DEV, n // N_DEV), 1, 0), r, wire_cols)


def _slabs(name, g):
    if name == "w_in":
        return _w_in_padded_to_slabs(g, WIRE_SHAPE["w_in"][1])
    if name == "w_uq":
        return _full_to_cols(_w_uq_from_headsplit(g), WIRE_SHAPE["w_uq"][1])
    if name == "w_ukv":
        return _full_to_cols(g, g.shape[1] // N_DEV)
    if name == "conv_w":
        return _pad2(_full_to_cols(g.astype(WIRE_DTYPE), g.shape[1] // N_DEV), *WIRE_SHAPE["conv_w"])
    if name == "w_up":
        return g
    return g.reshape(N_DEV, -1, g.shape[-1])


def kernel(x, positions, attn_norm_w, w_in, q_lat_norm_w, w_uq, kv_lat_norm_w, w_ukv, q_norm_w, k_norm_w, mla_out_norm_w, conv_w, a_log, dt_bias, gdn_norm_w, w_out, mlp_norm_w, w_up, w_down, loss_target, m_attn_norm_w, m_w_in, m_q_lat_norm_w, m_w_uq, m_kv_lat_norm_w, m_w_ukv, m_q_norm_w, m_k_norm_w, m_mla_out_norm_w, m_conv_w, m_a_log, m_dt_bias, m_gdn_norm_w, m_w_out, m_mlp_norm_w, m_w_up, m_w_down, v_attn_norm_w, v_w_in, v_q_lat_norm_w, v_w_uq, v_kv_lat_norm_w, v_w_ukv, v_q_norm_w, v_k_norm_w, v_mla_out_norm_w, v_conv_w, v_a_log, v_dt_bias, v_gdn_norm_w, v_w_out, v_mlp_norm_w, v_w_up, v_w_down):
    env = dict(locals())
    W = {n: env[n][0] for n in ALL_W}
    Mo = {n: env["m_" + n][0] for n in ALL_W}
    Vo = {n: env["v_" + n][0] for n in ALL_W}

    two_d = lambda a: a.reshape(1, -1) if a.ndim == 1 else a
    D = x.shape[-1]

    s_in, s_uq, s_ukv, s_conv = _gather_weights([
        _pad2(W["w_in"].astype(WIRE_DTYPE), *WIRE_SHAPE["w_in"]),
        _pad2(W["w_uq"].astype(WIRE_DTYPE), *WIRE_SHAPE["w_uq"]),
        W["w_ukv"].astype(WIRE_DTYPE), _pad2(W["conv_w"], *WIRE_SHAPE["conv_w"])])
    late = [W["w_out"].astype(WIRE_DTYPE), W["w_up"].astype(WIRE_DTYPE), W["w_down"].astype(WIRE_DTYPE)]

    loss, grad_x, parts, gs = _local_step(
        x, positions, loss_target, two_d(W["attn_norm_w"]), _w_in_shards_to_padded(s_in),
        two_d(W["q_lat_norm_w"]), _cols_to_full(s_uq, W["w_uq"].shape[1]), two_d(W["kv_lat_norm_w"]),
        _cols_to_full(s_ukv, W["w_ukv"].shape[1]), two_d(W["q_norm_w"]), two_d(W["k_norm_w"]),
        W["mla_out_norm_w"], _cols_to_full(s_conv[:, :CONV_W], W["conv_w"].shape[1]), two_d(W["a_log"]),
        two_d(W["dt_bias"]), two_d(W["gdn_norm_w"]), None, two_d(W["mlp_norm_w"]), None, None,
        late_shards=late, exchange=True)
    done = {n: _reduce_adamw(parts[n], W[n], Mo[n], Vo[n], "adamw_" + n) for n in BIG}
    names = [n for n, *_ in SMALL_LAYOUT]
    tiles = _gather_small_grads(gs, jnp.full((1, LANES), loss, F32))
    small, loss = _adamw_replicated(tiles, [two_d(W[n]) for n in names], [two_d(Mo[n]) for n in names],
                                    [two_d(Vo[n]) for n in names])
    for i, n in enumerate(names):
        done[n] = [small[kind][i] for kind in range(4)]
    res = [done[n][kind].reshape(env[n].shape) for kind in range(4) for n in ALL_W]
    return (loss, grad_x, *res)
```

```python
import functools

import jax
import jax.numpy as jnp
from jax import lax
from jax.experimental import pallas as pl
from jax.experimental.pallas import tpu as pltpu

F32 = jnp.float32
MXU_DTYPE = jnp.bfloat16
WIRE_DTYPE = jnp.bfloat16
SDS = jax.ShapeDtypeStruct
HIGHEST = lax.Precision.HIGHEST
MESH_ID = pl.DeviceIdType.MESH

D_MODEL = 1024
MLA_HEADS = 4
Q_LORA = 256
KV_LORA = 256
NOPE = 128
ROPE = 64
QK_DIM = NOPE + ROPE
V_DIM = 128
ROPE_THETA = 10000.0
GDN_HEADS = 4
GDN_DIM = 128
GDN_WIDTH = GDN_HEADS * GDN_DIM
CONV_W = 4
CHUNK = 64
D_FF = 4 * D_MODEL
EPS = 1e-6
ATT_SCALE = QK_DIM ** -0.5
GDN_QSCALE = GDN_DIM ** -0.5
N_DEV = 8
ATTN_BLOCK = 512
ATTN_CHAINS = 2
MLP_FWD_SHARDS = 4
MLP_BWD_SHARDS = 4

ADAM_LR = 0.001
ADAM_B1 = 0.9
ADAM_B2 = 0.999
ADAM_EPS = 1e-08
ADAM_WD = 0.01
ADAM_STEP = 10

LANES = 128
SUBLANES = 8
VMEM_LIMIT = 60 * 1024 * 1024

P_GQKV, P_GZ, P_QLAT, P_KVLAT, P_KPE, P_GAB = 0, 1536, 2048, 2304, 2560, 2688
P_WIDTH = 2816
O_QLAT, O_KVLAT, O_KPE, O_GQKV, O_GZ, O_GAB, O_END = 0, 256, 512, 576, 2112, 2624, 2632


def _params(sem=None, vmem=VMEM_LIMIT):
    kw = dict(vmem_limit_bytes=vmem)
    if sem is not None:
        kw["dimension_semantics"] = sem
    return pltpu.CompilerParams(**kw)


def _mm(a, b):
    return jnp.dot(a.astype(MXU_DTYPE), b.astype(MXU_DTYPE), preferred_element_type=F32)


def _mm_nt(a, b):
    return lax.dot_general(a.astype(MXU_DTYPE), b.astype(MXU_DTYPE), (((1,), (1,)), ((), ())),
                           preferred_element_type=F32)


def _mm_tn(a, b):
    return lax.dot_general(a.astype(MXU_DTYPE), b.astype(MXU_DTYPE), (((0,), (0,)), ((), ())),
                           preferred_element_type=F32)


def _split(a):
    hi = a.astype(MXU_DTYPE)
    return hi, (a - hi.astype(F32)).astype(MXU_DTYPE)


def _mm_split(a, b):
    (ah, al), (bh, bl) = a, b
    dot = lambda x, y: jnp.dot(x, y, preferred_element_type=F32)
    if MXU_DTYPE == F32:
        return dot(ah, bh)
    return dot(ah, bh) + dot(ah, bl) + dot(al, bh)


def _mm_exact(a, b):
    return _mm_split(_split(a), _split(b))


def _row_sum(v, on_mxu=False):
    if not on_mxu:
        return jnp.sum(v, axis=-1, keepdims=True)
    d = v.shape[-1]
    ones = jnp.ones((d, LANES), MXU_DTYPE)
    s = sum(jnp.dot(p, ones, preferred_element_type=F32) for p in _split(v))
    return s[:, :d] if d <= LANES else jnp.tile(s, (1, d // LANES))


def _rms(x, w, on_mxu=False):
    r = lax.rsqrt(_row_sum(x * x, on_mxu) * (1.0 / x.shape[-1]) + EPS)
    return x * r * w, r


def _rms_bwd(dy, x, w, r, on_mxu=False):
    xh = x * r
    dyw = dy * w
    dx = r * (dyw - xh * (_row_sum(dyw * xh, on_mxu) * (1.0 / x.shape[-1])))
    dw = jnp.sum(dy * xh, axis=0, keepdims=True)
    return dx, dw


def _l2n(x, scale):
    return x * (lax.rsqrt(_row_sum(x * x) + EPS) * scale)


def _l2n_bwd(dy, x, scale):
    r = lax.rsqrt(_row_sum(x * x) + EPS)
    xh = x * r
    return (scale * r) * (dy - xh * _row_sum(dy * xh))


def _rot(t):
    return jnp.concatenate([-t[:, ROPE // 2:], t[:, :ROPE // 2]], axis=-1)


def _rot_t(t):
    return jnp.concatenate([t[:, ROPE // 2:], -t[:, :ROPE // 2]], axis=-1)


def _rope(t, cos, sin):
    return t * cos + _rot(t) * sin


def _rope_bwd(d, cos, sin):
    return d * cos + _rot_t(d * sin)


def _sigmoid(x):
    return jax.nn.sigmoid(x)


def _shift_down(x, halo, j):
    if j == 0:
        return x
    xr = pltpu.roll(x, j, 0)
    hr = pltpu.roll(halo, j, 0)
    row = lax.broadcasted_iota(jnp.int32, halo.shape, 0)
    top = jnp.where(row < j, hr, xr[:SUBLANES])
    return jnp.concatenate([top, xr[SUBLANES:]], axis=0)


def _shift_up(x, nxt, j):
    if j == 0:
        return x
    n = x.shape[0]
    xr = pltpu.roll(x, n - j, 0)
    nr = pltpu.roll(nxt, SUBLANES - j, 0)
    row = lax.broadcasted_iota(jnp.int32, nxt.shape, 0)
    bot = jnp.where(row >= SUBLANES - j, nr, xr[n - SUBLANES:])
    return jnp.concatenate([xr[:n - SUBLANES], bot], axis=0)


def _chunk_cumsum(y, row_in_chunk):
    s = 1
    while s < CHUNK:
        y = y + jnp.where(row_in_chunk >= s, pltpu.roll(y, s, 0), 0.0)
        s *= 2
    return y


def _chunk_rev_cumsum(y, row_in_chunk):
    n = y.shape[0]
    s = 1
    while s < CHUNK:
        y = y + jnp.where(row_in_chunk + s < CHUNK, pltpu.roll(y, n - s, 0), 0.0)
        s *= 2
    return y


def _together(generators):
    alive = list(generators)
    while alive:
        nxt = []
        for g in alive:
            try:
                next(g)
                nxt.append(g)
            except StopIteration:
                pass
        alive = nxt
        yield


def _lockstep(generators):
    for _ in _together(generators):
        pass


def _pick_lane(tile, lane, idx):
    return jnp.sum(jnp.where(lane == idx, tile, 0.0), axis=-1, keepdims=True)


def _divisor_tile(n, cap, unit=LANES):
    best = unit
    t = unit
    while t <= min(n, cap):
        if n % t == 0:
            best = t
        t += unit
    return n if n <= cap else best


def _in_proj(x2, w_an, w_in_p, conv_w, alog_l, dt_l, S):
    T, D = x2.shape
    N = w_in_p.shape[1]
    tm = min(512, S)
    assert S % tm == 0 and T % tm == 0, "a token tile must not straddle two sequences"
    tiles_per_seq = S // tm
    C3 = 3 * GDN_WIDTH
    H = GDN_HEADS

    def body(x_ref, wn_ref, w_ref, cw_ref, alog_ref, dt_ref, proj_ref, xn_ref, q_out, k_out, v_out, gates_out,
             halo_s):
        xn, _ = _rms(x_ref[...], wn_ref[...])
        xn = xn.astype(MXU_DTYPE)
        xn_ref[...] = xn
        proj = jnp.dot(xn, w_ref[...], preferred_element_type=F32)
        proj_ref[...] = proj
        u = proj[:, P_GQKV:P_GQKV + C3]

        @pl.when(pl.program_id(0) == 0)
        def _():
            halo_s[...] = jnp.zeros_like(halo_s)

        halo = jnp.where(pl.program_id(0) % tiles_per_seq == 0, 0.0, halo_s[...])
        halo_s[...] = u[tm - SUBLANES:, :]
        c, _ = _conv_taps(u, halo, cw_ref[...])
        a = c * _sigmoid(c)
        for h in range(H):
            xq = a[:, h * GDN_DIM:(h + 1) * GDN_DIM]
            xk = a[:, GDN_WIDTH + h * GDN_DIM:GDN_WIDTH + (h + 1) * GDN_DIM]
            q_out[h] = _l2n(xq, GDN_QSCALE)
            k_out[h] = _l2n(xk, 1.0)
            v_out[h] = a[:, 2 * GDN_WIDTH + h * GDN_DIM:2 * GDN_WIDTH + (h + 1) * GDN_DIM]
        lane = lax.broadcasted_iota(jnp.int32, (tm, LANES), 1)
        ric = lax.broadcasted_iota(jnp.int32, (tm, LANES), 0) % CHUNK
        g, beta = _gate_values(proj[:, P_GAB:P_GAB + LANES], alog_ref[...], dt_ref[...], lane)
        gates_out[...] = _chunk_cumsum(g, ric) + beta

    hspec = pl.BlockSpec((H, tm, GDN_DIM), lambda i: (0, i, 0))
    vec = pl.BlockSpec((1, LANES), lambda i: (0, 0))
    return pl.pallas_call(
        body, grid=(T // tm,), name="in_proj",
        in_specs=[pl.BlockSpec((tm, D), lambda i: (i, 0)), pl.BlockSpec((1, D), lambda i: (0, 0)),
                  pl.BlockSpec((D, N), lambda i: (0, 0)), pl.BlockSpec((CONV_W, C3), lambda i: (0, 0)), vec, vec],
        out_specs=[pl.BlockSpec((tm, N), lambda i: (i, 0)), pl.BlockSpec((tm, D), lambda i: (i, 0)),
                   hspec, hspec, hspec, pl.BlockSpec((tm, LANES), lambda i: (i, 0))],
        out_shape=[SDS((T, N), F32), SDS((T, D), MXU_DTYPE)] + [SDS((H, T, GDN_DIM), F32)] * 3
                  + [SDS((T, LANES), F32)],
        scratch_shapes=[pltpu.VMEM((SUBLANES, C3), F32)],
        compiler_params=_params(("arbitrary",)),
    )(x2, w_an, w_in_p, conv_w, alog_l, dt_l)


def _mla_pre(proj, cosf, sinf, w_qln, w_kvln, w_uq_p, w_ukv, qnw, knw, transfer=None):
    T = proj.shape[0]
    tm = min(256, T)
    H = MLA_HEADS

    def body(ql_ref, kvl_ref, kpe_ref, cos_ref, sin_ref, wq_ref, wkv_ref, uq_ref, ukv_ref, qnw_ref, knw_ref,
             q_out, k_out, v_out):
        rms = functools.partial(_rms, on_mxu=True)
        cos, sin = cos_ref[...], sin_ref[...]
        qnw_, knw_ = qnw_ref[...], knw_ref[...]
        qn, _ = rms(ql_ref[...], wq_ref[...])
        kvn, _ = rms(kvl_ref[...], wkv_ref[...])
        qraw = _mm(qn, uq_ref[...])
        kvraw = _mm(kvn, ukv_ref[...])
        kpe = _rope(rms(kpe_ref[...][:, :ROPE], knw_[:, NOPE:])[0], cos, sin)
        for h in range(H):
            qn_h = rms(qraw[:, h * NOPE:(h + 1) * NOPE], qnw_[:, :NOPE])[0]
            qp_h = _rope(rms(qraw[:, H * NOPE + h * ROPE:H * NOPE + (h + 1) * ROPE], qnw_[:, NOPE:])[0], cos, sin)
            q_out[h] = (jnp.concatenate([qn_h, qp_h], axis=-1) * ATT_SCALE).astype(MXU_DTYPE)
            kn_h = rms(kvraw[:, h * 256:h * 256 + NOPE], knw_[:, :NOPE])[0]
            k_out[h] = jnp.concatenate([kn_h, kpe], axis=-1).astype(MXU_DTYPE)
            v_out[h] = kvraw[:, h * 256 + NOPE:(h + 1) * 256].astype(MXU_DTYPE)

    full = lambda a: pl.BlockSpec(a.shape, lambda i: (0,) * a.ndim)
    return _call_beside(
        body, transfer, grid=(T // tm,), name="mla_pre", scratch_shapes=[], semantics=("arbitrary",),
        args=(proj, proj, proj, cosf, sinf, w_qln, w_kvln, w_uq_p, w_ukv, qnw, knw),
        in_specs=[pl.BlockSpec((tm, 256), lambda i: (i, P_QLAT // 256)),
                  pl.BlockSpec((tm, 256), lambda i: (i, P_KVLAT // 256)),
                  pl.BlockSpec((tm, 128), lambda i: (i, P_KPE // 128)),
                  pl.BlockSpec((tm, ROPE), lambda i: (i, 0)), pl.BlockSpec((tm, ROPE), lambda i: (i, 0)),
                  full(w_qln), full(w_kvln), full(w_uq_p), full(w_ukv), full(qnw), full(knw)],
        out_specs=[pl.BlockSpec((H, tm, QK_DIM), lambda i: (0, i, 0)),
                   pl.BlockSpec((H, tm, QK_DIM), lambda i: (0, i, 0)),
                   pl.BlockSpec((H, tm, V_DIM), lambda i: (0, i, 0))],
        out_shape=[SDS((H, T, QK_DIM), MXU_DTYPE), SDS((H, T, QK_DIM), MXU_DTYPE), SDS((H, T, V_DIM), MXU_DTYPE)])


def _attn_fwd(q4, k4, v4, B, S, transfer=None):
    H = MLA_HEADS
    bq = min(ATTN_BLOCK, S)
    nq = S // bq
    rows = bq // ATTN_CHAINS

    def body(q_ref, k_ref, v_ref, o_ref, lse_ref):
        col = lax.broadcasted_iota(jnp.int32, (rows, bq), 1)
        row = lax.broadcasted_iota(jnp.int32, (rows, bq), 0)

        def q_step(qi, carry):
            qs = pl.multiple_of(qi * bq, bq)
            qsub = [q_ref[0, pl.ds(qs + j * rows, rows), :] for j in range(ATTN_CHAINS)]

            def k_block(ks, cs, diagonal):
                k = k_ref[0, pl.ds(ks, bq), :]
                v = v_ref[0, pl.ds(ks, bq), :]
                out = [None] * ATTN_CHAINS

                def chain(j):
                    m, l, acc = cs[j]
                    s = _mm_nt(qsub[j], k)
                    yield
                    if diagonal:
                        s = jnp.where(col <= row + j * rows, s, -jnp.inf)
                    m_new = jnp.maximum(m, jnp.max(s, axis=-1, keepdims=True))
                    p = jnp.exp(s - m_new)
                    a = jnp.exp(m - m_new)
                    l_new = a * l + jnp.sum(p, axis=-1, keepdims=True)
                    yield
                    out[j] = (m_new, l_new, a * acc + _mm(p, v))

                _lockstep([chain(j) for j in range(ATTN_CHAINS)])
                return tuple(out)

            init = tuple((jnp.full((rows, 1), -jnp.inf, F32), jnp.zeros((rows, 1), F32),
                          jnp.zeros((rows, V_DIM), F32)) for _ in range(ATTN_CHAINS))
            cs = lax.fori_loop(0, qi, lambda kj, c: k_block(pl.multiple_of(kj * bq, bq), c, False), init)
            for j, (m, l, acc) in enumerate(k_block(qs, cs, True)):
                o_ref[0, pl.ds(qs + j * rows, rows), :] = acc / l
                lse_ref[0, pl.ds(qs + j * rows, rows), :] = m + jnp.log(l)
            return carry

        lax.fori_loop(0, nq, q_step, 0)

    spec = lambda d: pl.BlockSpec((1, S, d), lambda h, b: (h, b, 0))
    return _call_beside(
        body, transfer, grid=(H, B), name="attn_fwd",
        in_specs=[spec(QK_DIM), spec(QK_DIM), spec(V_DIM)],
        out_specs=[spec(V_DIM), spec(1)],
        out_shape=[SDS((H, B * S, V_DIM), F32), SDS((H, B * S, 1), F32)],
        scratch_shapes=[], semantics=("arbitrary", "arbitrary"), args=(q4, k4, v4))


def _conv_taps(u, halo, w):
    sh = [_shift_down(u, halo, j) for j in range(CONV_W)]
    c = w[0:1] * sh[3] + w[1:2] * sh[2] + w[2:3] * sh[1] + w[3:4] * sh[0]
    return c, sh


def _gate_values(gab, alog_l, dt_l, lane):
    g = -jnp.exp(alog_l) * jax.nn.softplus(gab + dt_l)
    g = jnp.where(lane < GDN_HEADS, g, 0.0)
    beta = jnp.where((lane >= GDN_HEADS) & (lane < 2 * GDN_HEADS), _sigmoid(gab), 0.0)
    return g, beta


def _unit_lower_inverses(Ls, eye):
    Ps = [eye - L for L in Ls]
    Ms = [_split(-L) for L in Ls]
    for _ in range(5):
        sq = [_mm_split(m, m) for m in Ms]
        Ms = [_split(s) for s in sq]
        Ps = [p + _mm_split(_split(p), m) for p, m in zip(Ps, Ms)]
    return Ps


def _chunk_decays(gt, lane, h, ri, ci, rcol):
    Gc = _pick_lane(gt, lane, h)
    bt = _pick_lane(gt, lane, h + GDN_HEADS)
    Gb = jnp.broadcast_to(Gc, (CHUNK, CHUNK))
    Gam = jnp.where(ri >= ci, jnp.exp(Gb - Gb.T), 0.0)
    Gl = jnp.sum(jnp.where(rcol == CHUNK - 1, Gc, 0.0), axis=0, keepdims=True)
    return Gc, bt, Gam, jnp.exp(Gc), jnp.exp(Gl - Gc), jnp.exp(Gl)


GDN_FWD_UNROLL = 16
GDN_BWD_UNROLL = 8
GDN_RECUR_STEPS_PER_STAGE = 2


def _gdn_fwd(qg, kg, vg, gates, B, S, transfer=None):
    H, D, C = GDN_HEADS, GDN_DIM, CHUNK
    NC = S // C
    P = 2 if B % 2 == 0 else 1
    Sb, NCb = P * S, P * NC
    U = GDN_FWD_UNROLL if NCb % GDN_FWD_UNROLL == 0 else 1
    NG = NCb // U

    def body(q_ref, k_ref, v_ref, g_ref, o_ref, st_ref, ai_ref, u_ref, w_ref, q2_s, au_s, bc_s, w2_s, el_s):
        h = pl.program_id(0)
        lane = lax.broadcasted_iota(jnp.int32, (C, LANES), 1)
        ri = lax.broadcasted_iota(jnp.int32, (C, C), 0)
        ci = lax.broadcasted_iota(jnp.int32, (C, C), 1)
        rcol = lax.broadcasted_iota(jnp.int32, (C, 1), 0)
        eye = (ri == ci).astype(F32)

        def group(gi, c):
            ns = [gi * U + j for j in range(U)]
            css = [pl.multiple_of(n * C, C) for n in ns]
            qs = [q_ref[0, pl.ds(cs, C), :] for cs in css]
            ks = [k_ref[0, pl.ds(cs, C), :] for cs in css]
            vs = [v_ref[0, pl.ds(cs, C), :] for cs in css]
            decs = [_chunk_decays(g_ref[pl.ds(cs, C), :], lane, h, ri, ci, rcol) for cs in css]
            qks = [_mm_nt(jnp.concatenate([q, k], axis=0), k) for q, k in zip(qs, ks)]
            ainvs = _unit_lower_inverses(
                [jnp.where(ri > ci, d[1] * qk[C:] * d[2], 0.0) for qk, d in zip(qks, decs)], eye)
            sols = [_mm_exact(a, jnp.concatenate([v * d[1], k * (d[1] * d[3])], axis=-1))
                    for a, k, v, d in zip(ainvs, ks, vs, decs)]
            atuw = [_mm(qk[:C] * d[2], sol) for qk, d, sol in zip(qks, decs, sols)]
            kduw = [_mm_tn(k * d[4], sol) for k, d, sol in zip(ks, decs, sols)]
            for n, cs, q, a, sol, au, ku, (Gc, bt, Gam, e, f, eL) in zip(ns, css, qs, ainvs, sols, atuw, kduw, decs):
                u_ref[0, pl.ds(cs, C), :] = sol[:, :D]
                w_ref[0, pl.ds(cs, C), :] = sol[:, D:]
                au_s[pl.ds(cs, C), :] = au[:, :D]
                q2_s[pl.ds(cs, C), :] = q * e - au[:, D:]
                bc_s[n] = ku[:, :D]
                w2_s[n] = ku[:, D:]
                el_s[n] = jnp.broadcast_to(eL, (SUBLANES, LANES))
                ai_ref[0, n] = a.T
            return c

        lax.fori_loop(0, NG, group, 0)

        def step(n, states):
            new = []
            for p, S_ in enumerate(states):
                m = p * NC + n
                cs = pl.multiple_of(m * C, C)
                o_ref[0, pl.ds(cs, C), :] = _mm(q2_s[pl.ds(cs, C), :], S_) + au_s[pl.ds(cs, C), :]
                st_ref[0, m] = S_
                new.append(S_ * el_s[m, 0:1, :] + bc_s[m] - _mm(w2_s[m], S_))
            return tuple(new)

        lax.fori_loop(0, NC, step, tuple(jnp.zeros((D, D), F32) for _ in range(P)))

    spec = pl.BlockSpec((1, Sb, D), lambda h, b: (h, b, 0))
    return _call_beside(
        body, transfer, grid=(H, B // P), name="gdn_fwd",
        in_specs=[spec, spec, spec, pl.BlockSpec((Sb, LANES), lambda h, b: (b, 0))],
        out_specs=[spec, pl.BlockSpec((1, NCb, D, D), lambda h, b: (h, b, 0, 0)),
                   pl.BlockSpec((1, NCb, C, C), lambda h, b: (h, b, 0, 0)), spec, spec],
        out_shape=[SDS((H, B * S, D), F32), SDS((H, B * NC, D, D), F32), SDS((H, B * NC, C, C), F32),
                   SDS((H, B * S, D), F32), SDS((H, B * S, D), F32)],
        scratch_shapes=[pltpu.VMEM((Sb, D), F32), pltpu.VMEM((Sb, D), F32), pltpu.VMEM((NCb, D, D), F32),
                        pltpu.VMEM((NCb, D, D), F32), pltpu.VMEM((NCb, SUBLANES, LANES), F32)],
        semantics=("arbitrary", "arbitrary"), args=(qg, kg, vg, gates))


def _mix_out(o_mla, o_gdn, proj, x2, mla_w, gdn_w, w_out):
    T, D = x2.shape
    tm = min(512, T)
    H = MLA_HEADS

    def body(om_ref, og_ref, z_ref, x_ref, mw_ref, gw_ref, w_ref, h_ref, mix_ref):
        z = z_ref[...]
        parts = [_rms(om_ref[h], mw_ref[h:h + 1, :])[0] for h in range(H)]
        for h in range(GDN_HEADS):
            zh = z[:, h * GDN_DIM:(h + 1) * GDN_DIM]
            parts.append(_rms(og_ref[h], gw_ref[...])[0] * (zh * _sigmoid(zh)))
        mix = jnp.concatenate(parts, axis=-1).astype(MXU_DTYPE)
        mix_ref[...] = mix
        h_ref[...] = x_ref[...] + jnp.dot(mix, w_ref[...], preferred_element_type=F32)

    hspec = pl.BlockSpec((H, tm, V_DIM), lambda i: (0, i, 0))
    return pl.pallas_call(
        body, grid=(T // tm,), name="mix_out",
        in_specs=[hspec, hspec, pl.BlockSpec((tm, GDN_WIDTH), lambda i: (i, P_GZ // GDN_WIDTH)),
                  pl.BlockSpec((tm, D), lambda i: (i, 0)),
                  pl.BlockSpec((H, V_DIM), lambda i: (0, 0)), pl.BlockSpec((1, GDN_DIM), lambda i: (0, 0)),
                  pl.BlockSpec((D, D), lambda i: (0, 0))],
        out_specs=[pl.BlockSpec((tm, D), lambda i: (i, 0)), pl.BlockSpec((tm, D), lambda i: (i, 0))],
        out_shape=[SDS((T, D), F32), SDS((T, D), MXU_DTYPE)],
        compiler_params=_params(("arbitrary",)),
    )(o_mla, o_gdn, proj, x2, mla_w, gdn_w, w_out)


def _mlp_fwd(h2, w_mn, w_up, w_down, target):
    T, D = h2.shape
    ns, _, ts = w_up.shape
    F = ns * ts
    tm = min(512, T)
    G = MLP_FWD_SHARDS
    tf, nf = G * ts, ns // G

    def body(h_ref, wn_ref, up_w, down_w, t_ref, up_ref, hn_ref, dy_ref, loss_ref, dyb_ref, y_acc):
        j = pl.program_id(1)

        @pl.when(j == 0)
        def _():
            hn_ref[...] = _rms(h_ref[...], wn_ref[...])[0].astype(MXU_DTYPE)
            y_acc[...] = h_ref[...]

        parts = []
        for c in range(G):
            up = jnp.dot(hn_ref[...], up_w[c], preferred_element_type=F32)
            up_ref[:, c * ts:(c + 1) * ts] = up.astype(MXU_DTYPE)
            r = jnp.maximum(up, 0.0)
            parts.append(_mm(r * r, down_w[c * ts:(c + 1) * ts, :]))
        y_acc[...] += functools.reduce(jnp.add, parts)

        @pl.when(j == nf - 1)
        def _():
            err = y_acc[...] - t_ref[...]
            dy_ref[...] = err / D
            dyb_ref[...] = (err / D).astype(MXU_DTYPE)
            loss_ref[...] = jnp.full((1, SUBLANES, LANES), jnp.sum(err * err), F32)

    return pl.pallas_call(
        body, grid=(T // tm, nf), name="mlp_fwd",
        in_specs=[pl.BlockSpec((tm, D), lambda i, j: (i, 0)), pl.BlockSpec((1, D), lambda i, j: (0, 0)),
                  pl.BlockSpec((G, D, ts), lambda i, j: (j, 0, 0)), pl.BlockSpec((tf, D), lambda i, j: (j, 0)),
                  pl.BlockSpec((tm, D), lambda i, j: (i, 0))],
        out_specs=[pl.BlockSpec((tm, tf), lambda i, j: (i, j)), pl.BlockSpec((tm, D), lambda i, j: (i, 0)),
                   pl.BlockSpec((tm, D), lambda i, j: (i, 0)),
                   pl.BlockSpec((1, SUBLANES, LANES), lambda i, j: (i, 0, 0)),
                   pl.BlockSpec((tm, D), lambda i, j: (i, 0))],
        out_shape=[SDS((T, F), MXU_DTYPE), SDS((T, D), MXU_DTYPE), SDS((T, D), F32),
                   SDS((T // tm, SUBLANES, LANES), F32), SDS((T, D), MXU_DTYPE)],
        scratch_shapes=[pltpu.VMEM((tm, D), F32)],
        compiler_params=_params(("arbitrary", "arbitrary")),
    )(h2, w_mn, w_up, w_down, target)


def _mlp_bwd(dy, dyb, up, h2, w_mn, w_up, w_down, transfer=None):
    T, D = h2.shape
    ns, _, ts = w_up.shape
    F = ns * ts
    tm = min(512, T)
    G = MLP_BWD_SHARDS
    tf, nf = G * ts, ns // G

    def body(dy_ref, dyb_ref, up_ref, h_ref, wn_ref, up_w, down_w, dh_ref, dhb_ref, dup_ref, dwn_ref, acc):
        i, j = pl.program_id(0), pl.program_id(1)

        @pl.when((i == 0) & (j == 0))
        def _():
            dwn_ref[...] = jnp.zeros_like(dwn_ref)

        @pl.when(j == 0)
        def _():
            acc[...] = jnp.zeros_like(acc)

        parts = []
        for c in range(G):
            cols = slice(c * ts, (c + 1) * ts)
            r = jnp.maximum(up_ref[:, cols].astype(F32), 0.0)
            dup = (_mm_nt(dyb_ref[...], down_w[cols, :]) * (2.0 * r)).astype(MXU_DTYPE)
            dup_ref[:, cols] = dup
            parts.append(_mm_nt(dup, up_w[c]))
        acc[...] += functools.reduce(jnp.add, parts)

        @pl.when(j == nf - 1)
        def _():
            hv = h_ref[...]
            _, rr = _rms(hv, wn_ref[...])
            dx, dw = _rms_bwd(acc[...], hv, wn_ref[...], rr)
            dh = dy_ref[...] + dx
            dh_ref[...] = dh
            dhb_ref[...] = dh.astype(MXU_DTYPE)
            dwn_ref[...] += dw

    row = lambda i, j: (i, 0)
    return _call_beside(
        body, transfer, grid=(T // tm, nf), name="mlp_bwd",
        in_specs=[pl.BlockSpec((tm, D), row), pl.BlockSpec((tm, D), row), pl.BlockSpec((tm, tf), lambda i, j: (i, j)),
                  pl.BlockSpec((tm, D), row), pl.BlockSpec((1, D), lambda i, j: (0, 0)),
                  pl.BlockSpec((G, D, ts), lambda i, j: (j, 0, 0)), pl.BlockSpec((tf, D), lambda i, j: (j, 0))],
        out_specs=[pl.BlockSpec((tm, D), row), pl.BlockSpec((tm, D), row),
                   pl.BlockSpec((tm, tf), lambda i, j: (i, j)), pl.BlockSpec((1, D), lambda i, j: (0, 0))],
        out_shape=[SDS((T, D), F32), SDS((T, D), MXU_DTYPE), SDS((T, F), MXU_DTYPE), SDS((1, D), F32)],
        scratch_shapes=[pltpu.VMEM((tm, D), F32)], semantics=("arbitrary", "arbitrary"),
        args=(dy, dyb, up, h2, w_mn, w_up, w_down))


def _mix_bwd(dhb, o_mla, o_gdn, proj, mla_w, gdn_w, w_out):
    T, D = dhb.shape
    tm = min(512, T)
    H = MLA_HEADS

    def body(dh_ref, om_ref, og_ref, z_ref, mw_ref, gw_ref, w_ref, dom_ref, dog_ref, dz_ref, dmw_ref, dgw_ref,
             delta_ref):
        @pl.when(pl.program_id(0) == 0)
        def _():
            dmw_ref[...] = jnp.zeros_like(dmw_ref)
            dgw_ref[...] = jnp.zeros_like(dgw_ref)

        dmix = _mm_nt(dh_ref[...], w_ref[...])
        z = z_ref[...]
        dmw, dzs = [], []
        dgw = jnp.zeros((1, GDN_DIM), F32)
        for h in range(H):
            o = om_ref[h]
            w = mw_ref[h:h + 1, :]
            _, r = _rms(o, w)
            dx, dw = _rms_bwd(dmix[:, h * V_DIM:(h + 1) * V_DIM], o, w, r)
            dom_ref[h] = dx.astype(MXU_DTYPE)
            delta_ref[h] = jnp.sum(dx * o, axis=-1, keepdims=True)
            dmw.append(dw)
        for h in range(GDN_HEADS):
            o = og_ref[h]
            w = gw_ref[...]
            zh = z[:, h * GDN_DIM:(h + 1) * GDN_DIM]
            sg = _sigmoid(zh)
            yn, r = _rms(o, w)
            dy = dmix[:, H * V_DIM + h * GDN_DIM:H * V_DIM + (h + 1) * GDN_DIM]
            dzs.append(dy * yn * (sg * (1.0 + zh * (1.0 - sg))))
            dx, dw = _rms_bwd(dy * (zh * sg), o, w, r)
            dog_ref[h] = dx.astype(MXU_DTYPE)
            dgw = dgw + dw
        dz_ref[...] = jnp.concatenate(dzs, axis=-1).astype(MXU_DTYPE)
        dmw_ref[...] += jnp.concatenate(dmw, axis=0)
        dgw_ref[...] += dgw

    hspec = pl.BlockSpec((H, tm, V_DIM), lambda i: (0, i, 0))
    return pl.pallas_call(
        body, grid=(T // tm,), name="mix_bwd",
        in_specs=[pl.BlockSpec((tm, D), lambda i: (i, 0)), hspec, hspec,
                  pl.BlockSpec((tm, GDN_WIDTH), lambda i: (i, P_GZ // GDN_WIDTH)),
                  pl.BlockSpec((H, V_DIM), lambda i: (0, 0)), pl.BlockSpec((1, GDN_DIM), lambda i: (0, 0)),
                  pl.BlockSpec((D, D), lambda i: (0, 0))],
        out_specs=[hspec, hspec, pl.BlockSpec((tm, GDN_WIDTH), lambda i: (i, 0)),
                   pl.BlockSpec((H, V_DIM), lambda i: (0, 0)), pl.BlockSpec((1, GDN_DIM), lambda i: (0, 0)),
                   pl.BlockSpec((H, tm, 1), lambda i: (0, i, 0))],
        out_shape=[SDS((H, T, V_DIM), MXU_DTYPE), SDS((H, T, GDN_DIM), MXU_DTYPE), SDS((T, GDN_WIDTH), MXU_DTYPE),
                   SDS((H, V_DIM), F32), SDS((1, GDN_DIM), F32), SDS((H, T, 1), F32)],
        compiler_params=_params(("arbitrary",)),
    )(dhb, o_mla, o_gdn, proj, mla_w, gdn_w, w_out)


def _attn_bwd(q4, k4, v4, do4, delta4, lse4, B, S, transfer=None):
    H = MLA_HEADS
    bq = min(ATTN_BLOCK, S)
    nq = S // bq
    rows = bq // ATTN_CHAINS

    def body(q_ref, k_ref, v_ref, do_ref, delta_ref, lse_ref, dq_ref, dk_ref, dv_ref):
        dq_ref[...] = jnp.zeros_like(dq_ref)
        dk_ref[...] = jnp.zeros_like(dk_ref)
        dv_ref[...] = jnp.zeros_like(dv_ref)

        col = lax.broadcasted_iota(jnp.int32, (rows, bq), 1)
        row = lax.broadcasted_iota(jnp.int32, (rows, bq), 0)

        def k_step(kj, carry):
            ks = pl.multiple_of(kj * bq, bq)
            k = k_ref[0, pl.ds(ks, bq), :]
            v = v_ref[0, pl.ds(ks, bq), :]

            def q_block(qs, diagonal):
                dks, dvs = [None] * ATTN_CHAINS, [None] * ATTN_CHAINS

                def chain(j):
                    sl = pl.ds(qs + j * rows, rows)
                    nk = (j + 1) * rows if diagonal else bq
                    q = q_ref[0, sl, :]
                    do = do_ref[0, sl, :].astype(MXU_DTYPE)
                    s = _mm_nt(q, k[:nk])
                    dp = _mm_nt(do, v[:nk])
                    yield
                    p = jnp.exp(s - lse_ref[0, sl, :])
                    if diagonal:
                        p = jnp.where(col[:, :nk] <= row[:, :nk] + j * rows, p, 0.0)
                    ds = p * (dp - delta_ref[0, sl, :])
                    yield
                    dvs[j] = _mm_tn(p, do)
                    dks[j] = _mm_tn(ds, q)
                    dq_ref[0, sl, :] += _mm(ds, k[:nk])

                _lockstep([chain(j) for j in range(ATTN_CHAINS)])
                if diagonal:
                    for j in range(ATTN_CHAINS):
                        dv_ref[0, pl.ds(ks, (j + 1) * rows), :] += dvs[j]
                        dk_ref[0, pl.ds(ks, (j + 1) * rows), :] += dks[j]
                else:
                    dv_ref[0, pl.ds(ks, bq), :] += functools.reduce(jnp.add, dvs)
                    dk_ref[0, pl.ds(ks, bq), :] += functools.reduce(jnp.add, dks)

            q_block(ks, True)

            def q_step(qi, c):
                q_block(pl.multiple_of(qi * bq, bq), False)
                return c

            lax.fori_loop(kj + 1, nq, q_step, 0)
            return carry

        lax.fori_loop(0, nq, k_step, 0)

    spec = lambda d: pl.BlockSpec((1, S, d), lambda h, b: (h, b, 0))
    return _call_beside(
        body, transfer, grid=(H, B), name="attn_bwd",
        in_specs=[spec(QK_DIM), spec(QK_DIM), spec(V_DIM), spec(V_DIM), spec(1), spec(1)],
        out_specs=[spec(QK_DIM), spec(QK_DIM), spec(V_DIM)],
        out_shape=[SDS((H, B * S, QK_DIM), F32), SDS((H, B * S, QK_DIM), F32), SDS((H, B * S, V_DIM), F32)],
        scratch_shapes=[], semantics=("arbitrary", "arbitrary"),
        args=(q4, k4, v4, do4, delta4, lse4))


def _gdn_bwd(qg, kg, vg, gates, states, ainv, u4, w4, do4, B, S, transfer=None):
    H, D, C = GDN_HEADS, GDN_DIM, CHUNK
    NC = S // C
    U = GDN_BWD_UNROLL if NC % GDN_BWD_UNROLL == 0 else 1
    NG = NC // U

    def body(q_ref, k_ref, v_ref, g_ref, st_ref, ai_ref, u_ref, w_ref, do_ref, dq_ref, dk_ref, dv_ref, dgb_ref,
             kd_s, x1_s, x2_s, el_s, dvn_s, ds_s, w2t_s):
        h = pl.program_id(0)
        lane = lax.broadcasted_iota(jnp.int32, (C, LANES), 1)
        ri = lax.broadcasted_iota(jnp.int32, (C, C), 0)
        ci = lax.broadcasted_iota(jnp.int32, (C, C), 1)
        rcol = lax.broadcasted_iota(jnp.int32, (C, 1), 0)

        def rsum(a):
            return jnp.sum(a, axis=-1, keepdims=True)

        def prepare(n):
            cs = n * C
            q = q_ref[0, pl.ds(cs, C), :]
            k = k_ref[0, pl.ds(cs, C), :]
            do = do_ref[0, pl.ds(cs, C), :]
            Gc, bt, Gam, e, f, eL = _chunk_decays(g_ref[pl.ds(cs, C), :], lane, h, ri, ci, rcol)
            At = _mm_nt(q, k) * Gam
            yield
            x1 = _mm_tn(At, do)
            x2 = _mm_tn(q * e, do)
            kd = k * f
            w = w_ref[0, pl.ds(cs, C), :]
            yield
            x1_s[pl.ds(cs, C), :] = x1
            x2_s[n] = x2 - _mm_tn(w, x1)
            w2t_s[n] = _mm_tn(w, kd)
            kd_s[pl.ds(cs, C), :] = kd
            el_s[n] = jnp.broadcast_to(eL, (SUBLANES, LANES))

        def recur(n, dS):
            cs = n * C
            ds_s[n] = dS
            dvn_s[pl.ds(cs, C), :] = x1_s[pl.ds(cs, C), :] + _mm(kd_s[pl.ds(cs, C), :], dS)
            return x2_s[n] + el_s[n, 0:1, :] * dS - _mm(w2t_s[n], dS)

        def local(n):
            cs = n * C
            q = q_ref[0, pl.ds(cs, C), :]
            k = k_ref[0, pl.ds(cs, C), :]
            v = v_ref[0, pl.ds(cs, C), :]
            do = do_ref[0, pl.ds(cs, C), :]
            u = u_ref[0, pl.ds(cs, C), :]
            w = w_ref[0, pl.ds(cs, C), :]
            dvn = dvn_s[pl.ds(cs, C), :]
            dS = ds_s[n]
            Gc, bt, Gam, e, f, eL = _chunk_decays(g_ref[pl.ds(cs, C), :], lane, h, ri, ci, rcol)
            S0 = st_ref[0, n]
            AinvT = ai_ref[0, n]
            qk = _mm_nt(jnp.concatenate([q, k], axis=0), k)
            QK, KK = qk[:C], qk[C:]
            be = bt * e
            sol = jnp.concatenate([u, w], axis=-1)
            vn = u - _mm(w, S0)
            yield
            dAt = jnp.where(ri >= ci, _mm_nt(do, vn), 0.0)
            dqd = _mm_nt(do, S0)
            dw = -_mm_nt(dvn, S0)
            dkd = _mm_nt(vn, dS)
            deL = jnp.sum(rsum(dS * S0), axis=0, keepdims=True)
            yield
            dR = _mm_exact(AinvT, jnp.concatenate([dvn, dw], axis=-1))
            dR1, dR2 = dR[:, :D], dR[:, D:]
            yield
            dL = jnp.where(ri > ci, -_mm_nt(dR, sol), 0.0)
            yield
            dv_ref[0, pl.ds(cs, C), :] = dR1 * bt
            r2 = rsum(dR2 * k)
            X = dL * Gam
            dbt = rsum(dR1 * v) + r2 * e + rsum(X * KK)
            de = r2 * bt + rsum(dqd * q)
            dKK = X * bt
            dQK = dAt * Gam
            dq_ref[0, pl.ds(cs, C), :] = _mm(dQK, k) + dqd * e
            dk_ref[0, pl.ds(cs, C), :] = dR2 * be + _mm(dKK + dKK.T, k) + _mm_tn(dQK, q) + dkd * f
            df = rsum(dkd * k)
            Z = (dL * (bt * KK) + dAt * QK) * Gam
            dG = rsum(Z) - rsum(Z.T) + de * e - df * f
            dGl = jnp.sum(df * f, axis=0, keepdims=True) + deL * eL
            dG = dG + jnp.where(rcol == C - 1, dGl, 0.0)
            dgb_ref[0, pl.ds(cs, C), :] = jnp.where(lane == 0, dG, jnp.where(lane == 1, dbt, 0.0))

        state = [jnp.zeros((D, D), F32)]

        def recur_group(g):
            for j, n in enumerate(reversed(range(g * U, (g + 1) * U))):
                state[0] = recur(n, state[0])
                if j % GDN_RECUR_STEPS_PER_STAGE == GDN_RECUR_STEPS_PER_STAGE - 1:
                    yield

        def stage(fn, g):
            return _together([fn(g * U + j) for j in range(U)])

        for step in range(NG + 2):
            jobs = [(stage, prepare, NG - 1 - step), (None, None, NG - step), (stage, local, NG + 1 - step)]
            _lockstep([recur_group(g) if make is None else make(fn, g) for make, fn, g in jobs if 0 <= g < NG])

    spec = pl.BlockSpec((1, S, D), lambda h, b: (h, b, 0))
    return _call_beside(
        body, transfer, grid=(H, B), name="gdn_bwd",
        in_specs=[spec, spec, spec, pl.BlockSpec((S, LANES), lambda h, b: (b, 0)),
                  pl.BlockSpec((1, NC, D, D), lambda h, b: (h, b, 0, 0)),
                  pl.BlockSpec((1, NC, C, C), lambda h, b: (h, b, 0, 0)), spec, spec, spec],
        out_specs=[spec, spec, spec, spec],
        out_shape=[SDS((H, B * S, D), F32)] * 4,
        scratch_shapes=[pltpu.VMEM((S, D), F32), pltpu.VMEM((S, D), F32), pltpu.VMEM((NC, D, D), F32),
                        pltpu.VMEM((NC, SUBLANES, LANES), F32), pltpu.VMEM((S, D), F32),
                        pltpu.VMEM((NC, D, D), F32), pltpu.VMEM((NC, D, D), F32)],
        semantics=("arbitrary", "arbitrary"), args=(qg, kg, vg, gates, states, ainv, u4, w4, do4))


def _gdn_pre_bwd(proj, conv_w, alog_l, dt_l, dq4, dk4, dv4, dgb4, S):
    T = proj.shape[0]
    tm = min(256, T)
    tiles_per_seq = S // tm
    C3 = 3 * GDN_WIDTH
    H = GDN_HEADS

    def body(u_ref, halo_ref, gab_ref, w_ref, alog_ref, dt_ref, dq_ref, dk_ref, dv_ref, dgb_ref,
             dc_ref, dgab_ref, dcw_ref, dalog_ref, ddt_ref):
        i = pl.program_id(0)

        @pl.when(i == 0)
        def _():
            dcw_ref[...] = jnp.zeros_like(dcw_ref)
            dalog_ref[...] = jnp.zeros_like(dalog_ref)
            ddt_ref[...] = jnp.zeros_like(ddt_ref)

        halo = jnp.where(i % tiles_per_seq == 0, 0.0, halo_ref[...])
        c, sh = _conv_taps(u_ref[...], halo, w_ref[...])
        sg = _sigmoid(c)
        a = c * sg
        das = [None] * (3 * H)
        for h in range(H):
            xq = a[:, h * GDN_DIM:(h + 1) * GDN_DIM]
            xk = a[:, GDN_WIDTH + h * GDN_DIM:GDN_WIDTH + (h + 1) * GDN_DIM]
            das[h] = _l2n_bwd(dq_ref[h], xq, GDN_QSCALE)
            das[H + h] = _l2n_bwd(dk_ref[h], xk, 1.0)
            das[2 * H + h] = dv_ref[h]
        dc = jnp.concatenate(das, axis=-1) * (sg * (1.0 + c * (1.0 - sg)))
        dc_ref[...] = dc
        dcw_ref[...] += jnp.concatenate(
            [jnp.sum(dc * sh[CONV_W - 1 - t], axis=0, keepdims=True) for t in range(CONV_W)], axis=0)
        lane = lax.broadcasted_iota(jnp.int32, (tm, LANES), 1)
        ric = lax.broadcasted_iota(jnp.int32, (tm, LANES), 0) % CHUNK
        dG = jnp.zeros((tm, LANES), F32)
        for h in range(H):
            t = dgb_ref[h]
            dG = dG + jnp.where(lane == h, _pick_lane(t, lane, 0), 0.0) \
                    + jnp.where(lane == h + H, _pick_lane(t, lane, 1), 0.0)
        is_g = lane < H
        dg = jnp.where(is_g, _chunk_rev_cumsum(jnp.where(is_g, dG, 0.0), ric), 0.0)
        gab = gab_ref[...]
        g, beta = _gate_values(gab, alog_ref[...], dt_ref[...], lane)
        dga = jnp.where(is_g, dg * (-jnp.exp(alog_ref[...])) * _sigmoid(gab + dt_ref[...]), 0.0)
        dgb = jnp.where(is_g, 0.0, dG) * beta * (1.0 - beta)
        dgab_ref[...] = (dga + dgb).astype(MXU_DTYPE)
        dalog_ref[...] += jnp.sum(dg * g, axis=0, keepdims=True)
        ddt_ref[...] += jnp.sum(dga, axis=0, keepdims=True)

    hspec = pl.BlockSpec((H, tm, GDN_DIM), lambda i: (0, i, 0))
    vec = pl.BlockSpec((1, LANES), lambda i: (0, 0))
    return pl.pallas_call(
        body, grid=(T // tm,), name="gdn_pre_bwd",
        in_specs=[pl.BlockSpec((tm, C3), lambda i: (i, 0)),
                  pl.BlockSpec((SUBLANES, C3), lambda i: (jnp.maximum(i * (tm // SUBLANES) - 1, 0), 0)),
                  pl.BlockSpec((tm, LANES), lambda i: (i, P_GAB // LANES)),
                  pl.BlockSpec((CONV_W, C3), lambda i: (0, 0)), vec, vec, hspec, hspec, hspec, hspec],
        out_specs=[pl.BlockSpec((tm, C3), lambda i: (i, 0)), pl.BlockSpec((tm, LANES), lambda i: (i, 0)),
                   pl.BlockSpec((CONV_W, C3), lambda i: (0, 0)), vec, vec],
        out_shape=[SDS((T, C3), F32), SDS((T, LANES), MXU_DTYPE), SDS((CONV_W, C3), F32),
                   SDS((1, LANES), F32), SDS((1, LANES), F32)],
        compiler_params=_params(("arbitrary",)),
    )(proj, proj, proj, conv_w, alog_l, dt_l, dq4, dk4, dv4, dgb4)


def _mla_pre_bwd(proj, cosf, sinf, w_qln, w_kvln, w_uq_p, w_ukv, qnw, knw, dq4, dk4, dv4, transfer=None):
    T = proj.shape[0]
    tm = min(256, T)
    H = MLA_HEADS

    def body(ql_ref, kvl_ref, kpe_ref, cos_ref, sin_ref, wq_ref, wkv_ref, uq_ref, ukv_ref, qnw_ref, knw_ref,
             dq_ref, dk_ref, dv_ref,
             dql_ref, dkvl_ref, dkpe_ref, dqraw_ref, dkvraw_ref, qn_ref, kvn_ref, dwq_ref, dwkv_ref, dqnw_ref, dknw_ref):
        @pl.when(pl.program_id(0) == 0)
        def _():
            for r in (dwq_ref, dwkv_ref, dqnw_ref, dknw_ref):
                r[...] = jnp.zeros_like(r)

        cos, sin = cos_ref[...], sin_ref[...]
        qnw_, knw_ = qnw_ref[...], knw_ref[...]
        ql, kvl = ql_ref[...], kvl_ref[...]
        kpe_raw = kpe_ref[...][:, :ROPE]
        rms = functools.partial(_rms, on_mxu=True)
        rms_bwd = functools.partial(_rms_bwd, on_mxu=True)
        qn, rq = rms(ql, wq_ref[...])
        kvn, rkv = rms(kvl, wkv_ref[...])
        qn_ref[...] = qn.astype(MXU_DTYPE)
        kvn_ref[...] = kvn.astype(MXU_DTYPE)
        qraw = _mm(qn, uq_ref[...])
        kvraw = _mm(kvn, ukv_ref[...])
        dq_nope, dq_pe, dkv_parts = [], [], []
        dqnw_n = jnp.zeros((1, NOPE), F32)
        dqnw_p = jnp.zeros((1, ROPE), F32)
        dknw_n = jnp.zeros((1, NOPE), F32)
        dkpe = jnp.zeros((tm, ROPE), F32)
        for h in range(H):
            dq = dq_ref[h] * ATT_SCALE
            x = qraw[:, h * NOPE:(h + 1) * NOPE]
            dx, dw = rms_bwd(dq[:, :NOPE], x, qnw_[:, :NOPE], rms(x, qnw_[:, :NOPE])[1])
            dq_nope.append(dx)
            dqnw_n = dqnw_n + dw
            x = qraw[:, H * NOPE + h * ROPE:H * NOPE + (h + 1) * ROPE]
            dx, dw = rms_bwd(_rope_bwd(dq[:, NOPE:], cos, sin), x, qnw_[:, NOPE:], rms(x, qnw_[:, NOPE:])[1])
            dq_pe.append(dx)
            dqnw_p = dqnw_p + dw
            dk = dk_ref[h]
            x = kvraw[:, h * 256:h * 256 + NOPE]
            dx, dw = rms_bwd(dk[:, :NOPE], x, knw_[:, :NOPE], rms(x, knw_[:, :NOPE])[1])
            dknw_n = dknw_n + dw
            dkpe = dkpe + dk[:, NOPE:]
            dkv_parts += [dx, dv_ref[h]]
        dx, dknw_p = rms_bwd(_rope_bwd(dkpe, cos, sin), kpe_raw, knw_[:, NOPE:], rms(kpe_raw, knw_[:, NOPE:])[1])
        dkpe_ref[...] = jnp.concatenate([dx, jnp.zeros((tm, LANES - ROPE), F32)], axis=-1).astype(MXU_DTYPE)
        dqraw = jnp.concatenate(dq_nope + dq_pe, axis=-1).astype(MXU_DTYPE)
        dkvraw = jnp.concatenate(dkv_parts, axis=-1).astype(MXU_DTYPE)
        dqraw_ref[...] = dqraw
        dkvraw_ref[...] = dkvraw
        dx, dw = rms_bwd(_mm_nt(dqraw, uq_ref[...]), ql, wq_ref[...], rq)
        dql_ref[...] = dx.astype(MXU_DTYPE)
        dwq_ref[...] += dw
        dx, dw = rms_bwd(_mm_nt(dkvraw, ukv_ref[...]), kvl, wkv_ref[...], rkv)
        dkvl_ref[...] = dx.astype(MXU_DTYPE)
        dwkv_ref[...] += dw
        dqnw_ref[...] += jnp.concatenate([dqnw_n, dqnw_p], axis=-1)
        dknw_ref[...] += jnp.concatenate([dknw_n, dknw_p], axis=-1)

    full = lambda a: pl.BlockSpec(a.shape, lambda i: (0,) * a.ndim)
    rows = lambda n: pl.BlockSpec((tm, n), lambda i: (i, 0))
    const = lambda n: pl.BlockSpec((1, n), lambda i: (0, 0))
    NQ, NKV = w_uq_p.shape[1], w_ukv.shape[1]
    return _call_beside(
        body, transfer, grid=(T // tm,), name="mla_pre_bwd", scratch_shapes=[], semantics=("arbitrary",),
        args=(proj, proj, proj, cosf, sinf, w_qln, w_kvln, w_uq_p, w_ukv, qnw, knw, dq4, dk4, dv4),
        in_specs=[pl.BlockSpec((tm, 256), lambda i: (i, P_QLAT // 256)),
                  pl.BlockSpec((tm, 256), lambda i: (i, P_KVLAT // 256)),
                  pl.BlockSpec((tm, 128), lambda i: (i, P_KPE // 128)),
                  rows(ROPE), rows(ROPE),
                  full(w_qln), full(w_kvln), full(w_uq_p), full(w_ukv), full(qnw), full(knw),
                  pl.BlockSpec((H, tm, QK_DIM), lambda i: (0, i, 0)),
                  pl.BlockSpec((H, tm, QK_DIM), lambda i: (0, i, 0)),
                  pl.BlockSpec((H, tm, V_DIM), lambda i: (0, i, 0))],
        out_specs=[rows(Q_LORA), rows(KV_LORA), rows(LANES), rows(NQ), rows(NKV), rows(Q_LORA), rows(KV_LORA),
                   const(Q_LORA), const(KV_LORA), const(QK_DIM), const(QK_DIM)],
        out_shape=[SDS((T, Q_LORA), MXU_DTYPE), SDS((T, KV_LORA), MXU_DTYPE), SDS((T, LANES), MXU_DTYPE),
                   SDS((T, NQ), MXU_DTYPE), SDS((T, NKV), MXU_DTYPE),
                   SDS((T, Q_LORA), MXU_DTYPE), SDS((T, KV_LORA), MXU_DTYPE),
                   SDS((1, Q_LORA), F32), SDS((1, KV_LORA), F32), SDS((1, QK_DIM), F32), SDS((1, QK_DIM), F32)])


def _in_proj_bwd(dc, conv_w, dgz, dql, dkvl, dkpe, dgab, w_in_p, dh, x2, w_an, S):
    T, D = x2.shape
    N = w_in_p.shape[1]
    C3 = dc.shape[1]
    tm = min(512, S)
    assert S % tm == 0 and T % tm == 0, "a token tile must not straddle two sequences"
    tiles_per_seq = S // tm
    nblk = T // SUBLANES

    def body(dc_ref, nxt_ref, cw_ref, b_ref, c_ref, d_ref, e_ref, f_ref, w_ref, dh_ref, x_ref, wn_ref,
             dx_ref, dp_ref, dwn_ref):
        i = pl.program_id(0)

        @pl.when(i == 0)
        def _():
            dwn_ref[...] = jnp.zeros_like(dwn_ref)

        nxt = jnp.where(i % tiles_per_seq == tiles_per_seq - 1, 0.0, nxt_ref[...])
        dcv, cw = dc_ref[...], cw_ref[...]
        du = cw[3:4] * dcv
        for j in range(1, CONV_W):
            du = du + cw[3 - j:4 - j] * _shift_up(dcv, nxt, j)
        dp = jnp.concatenate([du.astype(MXU_DTYPE), b_ref[...], c_ref[...], d_ref[...], e_ref[...], f_ref[...]],
                             axis=-1).astype(MXU_DTYPE)
        dp_ref[...] = dp
        x = x_ref[...]
        _, r = _rms(x, wn_ref[...])
        dx, dw = _rms_bwd(_mm_nt(dp, w_ref[...]), x, wn_ref[...], r)
        dx_ref[...] = dh_ref[...] + dx
        dwn_ref[...] += dw

    rows = lambda n: pl.BlockSpec((tm, n), lambda i: (i, 0))
    return pl.pallas_call(
        body, grid=(T // tm,), name="in_proj_bwd",
        in_specs=[rows(C3),
                  pl.BlockSpec((SUBLANES, C3), lambda i: (jnp.minimum((i + 1) * (tm // SUBLANES), nblk - 1), 0)),
                  pl.BlockSpec((CONV_W, C3), lambda i: (0, 0)),
                  rows(dgz.shape[1]), rows(dql.shape[1]), rows(dkvl.shape[1]),
                  rows(dkpe.shape[1]), rows(dgab.shape[1]),
                  pl.BlockSpec((D, N), lambda i: (0, 0)), rows(D), rows(D), pl.BlockSpec((1, D), lambda i: (0, 0))],
        out_specs=[rows(D), rows(N), pl.BlockSpec((1, D), lambda i: (0, 0))],
        out_shape=[SDS((T, D), F32), SDS((T, N), MXU_DTYPE), SDS((1, D), F32)],
        compiler_params=_params(("arbitrary",)),
    )(dc, dc, conv_w, dgz, dql, dkvl, dkpe, dgab, w_in_p, dh, x2, w_an)


def _relu_squared(t):
    r = jnp.maximum(t.astype(F32), 0.0)
    return (r * r).astype(MXU_DTYPE)


def _wgrad(a, b, name, column_shards=False, a_map=None, a_cols=None):
    T = a.shape[0]
    m0, M = (0, a.shape[1]) if a_cols is None else a_cols
    N = b.shape[1]
    tM = _divisor_tile(M, 1024)
    assert m0 % tM == 0
    tN = N // N_DEV if column_shards else _divisor_tile(N, 1536)
    tk = min(T, 2048)
    nk = T // tk

    def body(a_ref, b_ref, o_ref, acc):
        k = pl.program_id(2)

        @pl.when(k == 0)
        def _():
            acc[...] = jnp.zeros_like(acc)

        acc[...] += _mm_tn(a_ref[...] if a_map is None else a_map(a_ref[...]), b_ref[...])

        @pl.when(k == nk - 1)
        def _():
            o_ref[...] = acc[...].astype(WIRE_DTYPE).reshape(o_ref.shape)

    if column_shards:
        out_spec, out_shape = pl.BlockSpec((1, tM, tN), lambda i, j, k: (j, i, 0)), SDS((N_DEV, M, tN), WIRE_DTYPE)
    else:
        out_spec, out_shape = pl.BlockSpec((tM, tN), lambda i, j, k: (i, j)), SDS((M, N), WIRE_DTYPE)
    return pl.pallas_call(
        body, grid=(M // tM, N // tN, nk), name=name,
        in_specs=[pl.BlockSpec((tk, tM), lambda i, j, k: (k, i + m0 // tM)),
                  pl.BlockSpec((tk, tN), lambda i, j, k: (k, j))],
        out_specs=out_spec, out_shape=out_shape,
        scratch_shapes=[pltpu.VMEM((tM, tN), F32)],
        compiler_params=_params(("arbitrary", "arbitrary", "arbitrary")),
    )(a, b)


def _adamw(g, w, m, v):
    m = ADAM_B1 * m + (1.0 - ADAM_B1) * g
    v = ADAM_B2 * v + (1.0 - ADAM_B2) * jnp.square(g)
    m_hat = m / (1.0 - ADAM_B1 ** ADAM_STEP)
    v_hat = v / (1.0 - ADAM_B2 ** ADAM_STEP)
    return -ADAM_LR * (m_hat / (jnp.sqrt(v_hat) + ADAM_EPS) + ADAM_WD * w), m, v


def _reduce_adamw(parts, w, m, v, name):
    R, C = w.shape
    pieces = parts if isinstance(parts, (tuple, list)) else (parts,)
    slots, Rp, Cp = pieces[0].shape
    tr = min(R, 256)
    tp = tr if len(pieces) * Rp == R else Rp
    per_piece = max(Rp // tr, 1)

    def body(*refs):
        p_refs, (w_ref, m_ref, v_ref, g_ref, d_ref, nm_ref, nv_ref) = refs[:len(pieces)], refs[len(pieces):]
        i = pl.program_id(0)
        g = None
        for n, p_ref in enumerate(p_refs):
            gn = p_ref[0].astype(F32)
            for s in range(1, slots):
                gn = gn + p_ref[s].astype(F32)
            g = gn if g is None else jnp.where(i // per_piece == n, gn, g)
        g = g[:tr, :C]
        g_ref[...] = g
        d_ref[...], nm_ref[...], nv_ref[...] = _adamw(g, w_ref[...], m_ref[...], v_ref[...])

    def piece_spec(n):
        return pl.BlockSpec((slots, tp, Cp), lambda i: (0, jnp.clip(i - n * per_piece, 0, per_piece - 1), 0))

    spec = pl.BlockSpec((tr, C), lambda i: (i, 0))
    return pl.pallas_call(
        body, grid=(R // tr,), name=name,
        in_specs=[piece_spec(n) for n in range(len(pieces))] + [spec, spec, spec],
        out_specs=[spec] * 4, out_shape=[SDS((R, C), F32)] * 4,
        compiler_params=_params(("arbitrary",)),
    )(*pieces, w, m, v)


SMALL_ROWS, SMALL_COLS = 16, 1024
SMALL_LAYOUT = (
    ("attn_norm_w", 0, 1, 1024, 1024), ("mlp_norm_w", 1, 1, 1024, 1024), ("q_lat_norm_w", 2, 1, 256, 256),
    ("kv_lat_norm_w", 3, 1, 256, 256), ("q_norm_w", 4, 1, 192, 192), ("k_norm_w", 5, 1, 192, 192),
    ("mla_out_norm_w", 6, 4, 128, 128), ("a_log", 10, 1, 128, 4), ("dt_bias", 11, 1, 128, 4),
    ("gdn_norm_w", 12, 1, 128, 128))
LOSS_ENTRY = ("loss", 13, 1, 128, 128)


def _adamw_replicated(parts, ws, ms, vs):
    n = len(SMALL_LAYOUT)

    def body(*refs):
        p_ref = refs[0]
        w_refs, m_refs, v_refs = refs[1:1 + n], refs[1 + n:1 + 2 * n], refs[1 + 2 * n:1 + 3 * n]
        outs = refs[1 + 3 * n:]
        s = p_ref[0]
        for d in range(1, N_DEV):
            s = s + p_ref[d]
        for i, (_, r0, nr, _, pw) in enumerate(SMALL_LAYOUT):
            g = s[r0:r0 + nr, :pw]
            outs[i][...] = g
            outs[n + i][...], outs[2 * n + i][...], outs[3 * n + i][...] = _adamw(
                g, w_refs[i][...], m_refs[i][...], v_refs[i][...])
        _, r0, nr, gw, _ = LOSS_ENTRY
        outs[4 * n][...] = s[r0:r0 + nr, :gw]

    res = pl.pallas_call(
        body, name="adamw_replicated",
        out_shape=[SDS(w.shape, F32) for w in ws] * 4 + [SDS((1, LANES), F32)],
        compiler_params=_params(),
    )(parts, *ws, *ms, *vs)
    return [res[k * n:(k + 1) * n] for k in range(4)], res[4 * n][0, 0]


COPIES_PER_ARRAY = N_DEV - 1


def _two_level_gather(srcs, outs, send_sems, recv_sems, local_sems=None, stage="all"):
    mx, my, mc = lax.axis_index("x"), lax.axis_index("y"), lax.axis_index("c")
    me, sibling = (mx, my, mc), (mx, my, 1 - mc)
    chips = [(1 - mx, my), (mx, 1 - my), (1 - mx, 1 - my)]
    arrays = range(len(srcs))

    def copy(a, k, block, to, src=None):
        px, py, pc = block
        slot = outs[a].at[4 * px + 2 * py + pc]
        sem = a * COPIES_PER_ARRAY + k
        return pltpu.make_async_remote_copy(
            src_ref=slot if src is None else src, dst_ref=slot,
            send_sem=send_sems.at[sem], recv_sem=recv_sems.at[sem], device_id=to, device_id_type=MESH_ID)

    mine = [] if local_sems is None else [
        pltpu.make_async_copy(srcs[a], outs[a].at[4 * mx + 2 * my + mc], local_sems.at[a]) for a in arrays]
    first = []
    for a in arrays:
        first.append(copy(a, 0, me, sibling, src=srcs[a]))
        first += [copy(a, 1 + j, me, (*chip, mc), src=srcs[a]) for j, chip in enumerate(chips)]
    forwards = [copy(a, 4 + j, (*chip, mc), sibling) for j, chip in enumerate(chips) for a in arrays]
    if stage in ("all", "start"):
        for cp in mine + first:
            cp.start()
    if stage in ("all", "forward"):
        for j, chip in enumerate(chips):
            for a in arrays:
                copy(a, 1 + j, (*chip, mc), me).wait_recv()
                forwards[j * len(srcs) + a].start()
    if stage in ("all", "finish"):
        for a in arrays:
            copy(a, 0, sibling, me).wait_recv()
        for j, chip in enumerate(chips):
            for a in arrays:
                copy(a, 4 + j, (*chip, 1 - mc), me).wait_recv()
        for cp in first + forwards:
            cp.wait_send()
        for cp in mine:
            cp.wait()


def _comm_scratch(n):
    return [pltpu.SemaphoreType.DMA((n * COPIES_PER_ARRAY,)), pltpu.SemaphoreType.DMA((n * COPIES_PER_ARRAY,)),
            pltpu.SemaphoreType.DMA((n,))]


def _any_specs(n):
    return [pl.BlockSpec(memory_space=pl.ANY)] * n


def _gather_weights(shards):
    n = len(shards)

    def body(*refs):
        _two_level_gather(refs[:n], refs[n:2 * n], *refs[2 * n:])

    return pl.pallas_call(
        body, name="gather_weights",
        out_shape=[SDS((N_DEV,) + s.shape, s.dtype) for s in shards],
        in_specs=_any_specs(n), out_specs=_any_specs(n), scratch_shapes=_comm_scratch(n),
    )(*shards)


def _gather_small_grads(gs, loss_lanes):
    gs = list(gs) + [loss_lanes]
    n = len(gs)

    def body(*refs):
        g_refs, out_ref = refs[:n], refs[n]
        tile, send_sems, recv_sems = refs[n + 1:]
        tile[...] = jnp.zeros_like(tile)
        for (_, r0, nr, gw, _), g in zip(SMALL_LAYOUT + (LOSS_ENTRY,), g_refs):
            tile[r0:r0 + nr, 0:gw] = g[...]
        me = 4 * lax.axis_index("x") + 2 * lax.axis_index("y") + lax.axis_index("c")
        out_ref[me] = tile[...]
        _two_level_gather([tile], [out_ref], send_sems, recv_sems)

    return pl.pallas_call(
        body, name="gather_small_grads",
        out_shape=SDS((N_DEV, SMALL_ROWS, SMALL_COLS), F32),
        in_specs=[pl.BlockSpec(memory_space=pltpu.VMEM)] * n,
        out_specs=pl.BlockSpec(memory_space=pltpu.VMEM),
        scratch_shapes=[pltpu.VMEM((SMALL_ROWS, SMALL_COLS), F32),
                        pltpu.SemaphoreType.DMA((COPIES_PER_ARRAY,)), pltpu.SemaphoreType.DMA((COPIES_PER_ARRAY,))],
    )(*gs)


def _exchange_grads_two_level(big, small):
    _, R, C = big.shape
    chip_flips = ((1, 0), (0, 1), (1, 1))

    def body(big_ref, small_ref, out_ref, small_out, mine_v, sib_v, pre_v, d2d_send, d2d_recv, ici_send, ici_recv,
             local_sems, s_send, s_recv, s_local):
        mx, my, mc = lax.axis_index("x"), lax.axis_index("y"), lax.axis_index("c")
        sibling = (mx, my, 1 - mc)
        chips = [(px, py) for px in range(2) for py in range(2)]
        _exchange([small_ref], [small_out], s_send, s_recv, s_local, stage="start")
        own = [pltpu.make_async_copy(big_ref.at[4 * px + 2 * py + mc], mine_v.at[q], local_sems.at[q])
               for q, (px, py) in enumerate(chips)]
        d2d = [pltpu.make_async_remote_copy(
            src_ref=big_ref.at[4 * px + 2 * py + (1 - mc)], dst_ref=sib_v.at[q], send_sem=d2d_send.at[q],
            recv_sem=d2d_recv.at[q], device_id=sibling, device_id_type=MESH_ID) for q, (px, py) in enumerate(chips)]
        for cp in own + d2d:
            cp.start()
        for cp in own + d2d:
            cp.wait()
        for q in range(4):
            pre_v[q] = (mine_v[q].astype(F32) + sib_v[q].astype(F32)).astype(pre_v.dtype)
        ici = []
        for k, (fx, fy) in enumerate(chip_flips):
            px = 1 - mx if fx else mx
            py = 1 - my if fy else my
            ici.append(pltpu.make_async_remote_copy(
                src_ref=pre_v.at[2 * px + py], dst_ref=out_ref.at[k], send_sem=ici_send.at[k],
                recv_sem=ici_recv.at[k], device_id=(px, py, mc), device_id_type=MESH_ID))
        keep = pltpu.make_async_copy(pre_v.at[2 * mx + my], out_ref.at[3], local_sems.at[4])
        for cp in ici + [keep]:
            cp.start()
        for cp in ici + [keep]:
            cp.wait()
        _exchange([small_ref], [small_out], s_send, s_recv, s_local, stage="finish")

    dma = pltpu.SemaphoreType.DMA
    return pl.pallas_call(
        body, name="exchange_grads",
        out_shape=[SDS((4, R, C), big.dtype), SDS(small.shape, small.dtype)],
        in_specs=_any_specs(2), out_specs=_any_specs(2),
        scratch_shapes=[pltpu.VMEM((4, R, C), big.dtype)] * 3 + [dma((4,)), dma((4,)), dma((3,)), dma((3,)), dma((5,))]
                       + _comm_scratch(1),
        compiler_params=_params(),
    )(big, small)


class _Transfer:
    def __init__(self, kind, arrays):
        self.kind, self.arrays, self.n = kind, list(arrays), len(arrays)

    def out_shapes(self):
        if self.kind == "gather":
            return [SDS((N_DEV,) + a.shape, a.dtype) for a in self.arrays]
        return [SDS(a.shape, a.dtype) for a in self.arrays]

    def run(self, srcs, outs, sems, stage):
        fn = _two_level_gather if self.kind == "gather" else _exchange
        fn(srcs, outs, *sems, stage=stage)


def _call_beside(body, transfer, *, grid, in_specs, out_specs, out_shape, scratch_shapes, name, semantics, args):
    if transfer is None:
        res = pl.pallas_call(body, grid=grid, in_specs=in_specs, out_specs=out_specs, out_shape=out_shape,
                             scratch_shapes=scratch_shapes, name=name, compiler_params=_params(semantics))(*args)
        return list(res), []
    n_in, n_out, n_s, n = len(in_specs), len(out_specs), len(scratch_shapes), transfer.n
    total = functools.reduce(lambda a, b: a * b, grid, 1)

    def wrapped(*refs):
        ins, refs = refs[:n_in], refs[n_in:]
        t_in, refs = refs[:n], refs[n:]
        outs, refs = refs[:n_out], refs[n_out:]
        t_out, refs = refs[:n], refs[n:]
        scratch, sems = refs[:n_s], refs[n_s:]
        first = functools.reduce(jnp.logical_and, [pl.program_id(i) == 0 for i in range(len(grid))])
        last = functools.reduce(jnp.logical_and, [pl.program_id(i) == g - 1 for i, g in enumerate(grid)])

        @pl.when(first)
        def _():
            transfer.run(t_in, t_out, sems, "start")

        step = functools.reduce(lambda acc, ig: acc * ig[1] + pl.program_id(ig[0]), enumerate(grid), 0)

        @pl.when(step == (3 * total) // 4)
        def _():
            transfer.run(t_in, t_out, sems, "forward")

        body(*ins, *outs, *scratch)

        @pl.when(last)
        def _():
            transfer.run(t_in, t_out, sems, "finish")

    res = pl.pallas_call(
        wrapped, grid=grid, in_specs=list(in_specs) + _any_specs(n), out_specs=list(out_specs) + _any_specs(n),
        out_shape=list(out_shape) + transfer.out_shapes(), scratch_shapes=list(scratch_shapes) + _comm_scratch(n),
        name=name, compiler_params=_params(semantics))(*args, *transfer.arrays)
    return list(res[:n_out]), list(res[n_out:])


EXCHANGE_FLIPS = ((0, 0, 1), (1, 0, 0), (0, 1, 0), (1, 1, 0), (1, 0, 1), (0, 1, 1), (1, 1, 1))


def _exchange(srcs, outs, send_sems, recv_sems, local_sems, stage="all"):
    mx, my, mc = lax.axis_index("x"), lax.axis_index("y"), lax.axis_index("c")
    arrays = range(len(srcs))
    copies = [pltpu.make_async_copy(srcs[a].at[4 * mx + 2 * my + mc], outs[a].at[N_DEV - 1], local_sems.at[a])
              for a in arrays]
    for k, (fx, fy, fc) in enumerate(EXCHANGE_FLIPS):
        px = 1 - mx if fx else mx
        py = 1 - my if fy else my
        pc = 1 - mc if fc else mc
        for a in arrays:
            sem = a * COPIES_PER_ARRAY + k
            copies.append(pltpu.make_async_remote_copy(
                src_ref=srcs[a].at[4 * px + 2 * py + pc], dst_ref=outs[a].at[k],
                send_sem=send_sems.at[sem], recv_sem=recv_sems.at[sem],
                device_id=(px, py, pc), device_id_type=MESH_ID))
    if stage in ("all", "start"):
        for cp in copies:
            cp.start()
    if stage in ("all", "finish"):
        for cp in copies:
            cp.wait()


def _w_in_to_padded(w):
    z = lambda n: jnp.zeros((w.shape[0], n), w.dtype)
    return jnp.concatenate([w[:, O_GQKV:O_GZ], w[:, O_GZ:O_GAB], w[:, O_QLAT:O_KVLAT], w[:, O_KVLAT:O_KPE],
                            w[:, O_KPE:O_GQKV], z(P_GAB - P_KPE - ROPE), w[:, O_GAB:O_END],
                            z(P_WIDTH - P_GAB - (O_END - O_GAB))], axis=1)


def _w_in_from_padded(wp):
    return jnp.concatenate([wp[:, P_QLAT:P_QLAT + 256], wp[:, P_KVLAT:P_KVLAT + 256], wp[:, P_KPE:P_KPE + ROPE],
                            wp[:, P_GQKV:P_GZ], wp[:, P_GZ:P_QLAT], wp[:, P_GAB:P_GAB + (O_END - O_GAB)]], axis=1)


W_IN_SHARD_COLS = (O_END - O_QLAT) // N_DEV


def _w_in_shards_to_padded(stack):
    _, R, Cw = stack.shape
    tr = min(R, 256)

    def body(s_ref, o_ref):
        full = jnp.concatenate([s_ref[d].astype(F32)[:, :W_IN_SHARD_COLS] for d in range(N_DEV)], axis=-1)
        o_ref[...] = _w_in_to_padded(full).astype(o_ref.dtype)

    return pl.pallas_call(
        body, grid=(R // tr,), name="w_in_to_padded",
        in_specs=[pl.BlockSpec((N_DEV, tr, Cw), lambda i: (0, i, 0))],
        out_specs=pl.BlockSpec((tr, P_WIDTH), lambda i: (i, 0)),
        out_shape=SDS((R, P_WIDTH), stack.dtype), compiler_params=_params(("arbitrary",)),
    )(stack)


def _w_in_padded_to_slabs(gp, wire_cols):
    R = gp.shape[0]
    tr = min(R, 256)

    def body(g_ref, o_ref):
        orig = _w_in_from_padded(g_ref[...].astype(F32))
        for d in range(N_DEV):
            piece = orig[:, d * W_IN_SHARD_COLS:(d + 1) * W_IN_SHARD_COLS]
            o_ref[d] = _pad2(piece, tr, wire_cols).astype(o_ref.dtype)

    return pl.pallas_call(
        body, grid=(R // tr,), name="w_in_to_slabs",
        in_specs=[pl.BlockSpec((tr, P_WIDTH), lambda i: (i, 0))],
        out_specs=pl.BlockSpec((N_DEV, tr, wire_cols), lambda i: (0, i, 0)),
        out_shape=SDS((N_DEV, R, wire_cols), gp.dtype), compiler_params=_params(("arbitrary",)),
    )(gp)


def _w_uq_to_headsplit(w):
    w3 = w.reshape(w.shape[0], MLA_HEADS, QK_DIM)
    return jnp.concatenate([w3[:, :, :NOPE].reshape(w.shape[0], -1), w3[:, :, NOPE:].reshape(w.shape[0], -1)], axis=1)


def _w_uq_from_headsplit(wp):
    n = wp[:, :MLA_HEADS * NOPE].reshape(wp.shape[0], MLA_HEADS, NOPE)
    p = wp[:, MLA_HEADS * NOPE:].reshape(wp.shape[0], MLA_HEADS, ROPE)
    return jnp.concatenate([n, p], axis=2).reshape(wp.shape[0], -1)


def _lane_vec(v4):
    return jnp.pad(v4.reshape(1, -1), ((0, 0), (0, LANES - v4.shape[-1])))


def _local_step(x, positions, target, attn_norm_w, w_in, q_lat_norm_w, w_uq, kv_lat_norm_w, w_ukv, q_norm_w,
                k_norm_w, mla_out_norm_w, conv_w, a_log, dt_bias, gdn_norm_w, w_out, mlp_norm_w, w_up, w_down,
                late_shards=None, exchange=False):
    B, S, D = x.shape
    T = B * S
    x2 = x.reshape(T, D)
    t2 = target.reshape(T, D)
    half = ROPE // 2
    inv_freq = ROPE_THETA ** (-jnp.arange(half, dtype=F32) / half)
    ang = positions.reshape(T, 1).astype(F32) * inv_freq
    cosf = jnp.concatenate([jnp.cos(ang)] * 2, axis=-1)
    sinf = jnp.concatenate([jnp.sin(ang)] * 2, axis=-1)
    w_in_p = w_in
    w_uq_p = _w_uq_to_headsplit(w_uq)
    alog_l, dt_l = _lane_vec(a_log), _lane_vec(dt_bias)
    w_an, w_qln, w_kvln, qnw, knw, w_mn, gdn_w = (
        attn_norm_w, q_lat_norm_w, kv_lat_norm_w, q_norm_w, k_norm_w, mlp_norm_w, gdn_norm_w)

    proj, xn, qg, kg, vg, gates = _in_proj(x2, w_an, w_in_p, conv_w, alog_l, dt_l, S)
    def gathering(shards):
        return None if late_shards is None else _Transfer("gather", shards)

    (q4, k4, v4), late = _mla_pre(proj, cosf, sinf, w_qln, w_kvln, w_uq_p, w_ukv, qnw, knw,
                                  gathering(late_shards and late_shards[:1]))
    if late:
        w_out = late[0].reshape(-1, D)
    (o_mla, lse), late = _attn_fwd(q4, k4, v4, B, S, gathering(late_shards and late_shards[2:]))
    if late:
        w_down = late[0].reshape(-1, D)
    (o_gdn, states, ainv, u4, w4), late = _gdn_fwd(qg, kg, vg, gates, B, S,
                                                   gathering(late_shards and late_shards[1:2]))
    if late:
        w_up = late[0]
    h2, mix = _mix_out(o_mla, o_gdn, proj, x2, mla_out_norm_w, gdn_w, w_out)
    up, hn, dy, sq, dyb = _mlp_fwd(h2, w_mn, w_up, w_down, t2)
    loss = (0.5 / D) * jnp.sum(sq[:, 0, 0])

    first = ("w_down",)
    second = ("w_up_top",)
    third = ("w_up_bottom", "w_out", "w_uq", "w_ukv")
    mats = dict(w_down=_wgrad(up, dyb, "wgrad_down", a_map=_relu_squared))

    def sending(names):
        return _Transfer("exchange", [_slabs(n, mats[n]) for n in names]) if exchange else None

    (dh, dhb, dup, d_mlp_norm), got = _mlp_bwd(dy, dyb, up, h2, w_mn, w_up, w_down, sending(first))
    mats.update(zip(first, got))
    mats.update(w_up_top=_wgrad(hn, dup, "wgrad_up_top", column_shards=True, a_cols=(0, D // 2)),
                w_up_bottom=_wgrad(hn, dup, "wgrad_up_bottom", column_shards=True, a_cols=(D // 2, D // 2)))
    do_mla, do_gdn, dz, d_mla_w, d_gdn_w, delta = _mix_bwd(dhb, o_mla, o_gdn, proj, mla_out_norm_w, gdn_w, w_out)
    mats.update(w_out=_wgrad(mix, dhb, "wgrad_out"))
    (dq4, dk4, dv4), got = _attn_bwd(q4, k4, v4, do_mla, delta, lse, B, S, sending(second))
    mats.update(zip(second, got))
    (dql, dkvl, dkpe, dqraw, dkvraw, qn, kvn, d_wqln, d_wkvln, d_qnw, d_knw), _ = _mla_pre_bwd(
        proj, cosf, sinf, w_qln, w_kvln, w_uq_p, w_ukv, qnw, knw, dq4, dk4, dv4)
    mats.update(w_uq=_wgrad(qn, dqraw, "wgrad_uq"), w_ukv=_wgrad(kvn, dkvraw, "wgrad_ukv"))
    (dqg, dkg, dvg, dgb4), got = _gdn_bwd(qg, kg, vg, gates, states, ainv, u4, w4, do_gdn, B, S, sending(third))
    mats.update(zip(third, got))
    dc, dgab, g_conv, d_alog, d_dt = _gdn_pre_bwd(proj, conv_w, alog_l, dt_l, dqg, dkg, dvg, dgb4, S)
    grad_x2, dproj, d_attn_norm = _in_proj_bwd(dc, conv_w, dz, dql, dkvl, dkpe, dgab, w_in_p, dh, x2, w_an, S)
    halves = (mats.pop("w_up_top"), mats.pop("w_up_bottom"))
    mats.update(w_up=halves if exchange else jnp.concatenate(halves, axis=1))
    mats.update(w_in=_wgrad(xn, dproj, "wgrad_in"), conv_w=g_conv)
    if exchange:
        last = ("w_in", "conv_w")
        mats.update(zip(last, _exchange_grads_two_level(*[_slabs(n, mats[n]) for n in last])))
    small = dict(attn_norm_w=d_attn_norm, mlp_norm_w=d_mlp_norm, q_lat_norm_w=d_wqln, kv_lat_norm_w=d_wkvln,
                 q_norm_w=d_qnw, k_norm_w=d_knw, mla_out_norm_w=d_mla_w, a_log=d_alog, dt_bias=d_dt,
                 gdn_norm_w=d_gdn_w)
    return loss, grad_x2.reshape(B, S, D), mats, [small[n] for n, *_ in SMALL_LAYOUT]


BIG = ("w_in", "w_uq", "w_ukv", "conv_w", "w_out", "w_up", "w_down")
ALL_W = ("attn_norm_w", "w_in", "q_lat_norm_w", "w_uq", "kv_lat_norm_w", "w_ukv", "q_norm_w", "k_norm_w",
         "mla_out_norm_w", "conv_w", "a_log", "dt_bias", "gdn_norm_w", "w_out", "mlp_norm_w", "w_up", "w_down")
WIRE_SHAPE = {"w_in": (1024, 384), "w_uq": (256, 128), "conv_w": (16, 256)}


def _pad2(a, rows, cols):
    return jnp.pad(a, [(0, 0)] * (a.ndim - 2) + [(0, rows - a.shape[-2]), (0, cols - a.shape[-1])])


def _cols_to_full(stack, cols):
    return jnp.moveaxis(stack[:, :, :cols], 0, 1).reshape(stack.shape[1], N_DEV * cols)


def _full_to_cols(full, wire_cols):
    r, n = full.shape
    return _pad2(jnp.moveaxis(full.reshape(r, N_DEV, n // N_DEV), 1, 0), r, wire_cols)


def _slabs(name, g):
    if name == "w_in":
        return _w_in_padded_to_slabs(g, WIRE_SHAPE["w_in"][1])
    if name == "w_uq":
        return _full_to_cols(_w_uq_from_headsplit(g), WIRE_SHAPE["w_uq"][1])
    if name == "w_ukv":
        return _full_to_cols(g, g.shape[1] // N_DEV)
    if name == "conv_w":
        return _pad2(_full_to_cols(g.astype(WIRE_DTYPE), g.shape[1] // N_DEV), *WIRE_SHAPE["conv_w"])
    if name == "w_up":
        return g
    return g.reshape(N_DEV, -1, g.shape[-1])


def kernel(x, positions, attn_norm_w, w_in, q_lat_norm_w, w_uq, kv_lat_norm_w, w_ukv, q_norm_w, k_norm_w, mla_out_norm_w, conv_w, a_log, dt_bias, gdn_norm_w, w_out, mlp_norm_w, w_up, w_down, loss_target, m_attn_norm_w, m_w_in, m_q_lat_norm_w, m_w_uq, m_kv_lat_norm_w, m_w_ukv, m_q_norm_w, m_k_norm_w, m_mla_out_norm_w, m_conv_w, m_a_log, m_dt_bias, m_gdn_norm_w, m_w_out, m_mlp_norm_w, m_w_up, m_w_down, v_attn_norm_w, v_w_in, v_q_lat_norm_w, v_w_uq, v_kv_lat_norm_w, v_w_ukv, v_q_norm_w, v_k_norm_w, v_mla_out_norm_w, v_conv_w, v_a_log, v_dt_bias, v_gdn_norm_w, v_w_out, v_mlp_norm_w, v_w_up, v_w_down):
    env = dict(locals())
    W = {n: env[n][0] for n in ALL_W}
    Mo = {n: env["m_" + n][0] for n in ALL_W}
    Vo = {n: env["v_" + n][0] for n in ALL_W}

    two_d = lambda a: a.reshape(1, -1) if a.ndim == 1 else a
    D = x.shape[-1]

    s_in, s_uq, s_ukv, s_conv = _gather_weights([
        _pad2(W["w_in"].astype(WIRE_DTYPE), *WIRE_SHAPE["w_in"]),
        _pad2(W["w_uq"].astype(WIRE_DTYPE), *WIRE_SHAPE["w_uq"]),
        W["w_ukv"].astype(WIRE_DTYPE), _pad2(W["conv_w"], *WIRE_SHAPE["conv_w"])])
    late = [W["w_out"].astype(WIRE_DTYPE), W["w_up"].astype(WIRE_DTYPE), W["w_down"].astype(WIRE_DTYPE)]

    loss, grad_x, parts, gs = _local_step(
        x, positions, loss_target, two_d(W["attn_norm_w"]), _w_in_shards_to_padded(s_in),
        two_d(W["q_lat_norm_w"]), _cols_to_full(s_uq, W["w_uq"].shape[1]), two_d(W["kv_lat_norm_w"]),
        _cols_to_full(s_ukv, W["w_ukv"].shape[1]), two_d(W["q_norm_w"]), two_d(W["k_norm_w"]),
        W["mla_out_norm_w"], _cols_to_full(s_conv[:, :CONV_W], W["conv_w"].shape[1]), two_d(W["a_log"]),
        two_d(W["dt_bias"]), two_d(W["gdn_norm_w"]), None, two_d(W["mlp_norm_w"]), None, None,
        late_shards=late, exchange=True)
    done = {n: _reduce_adamw(parts[n], W[n], Mo[n], Vo[n], "adamw_" + n) for n in BIG}
    names = [n for n, *_ in SMALL_LAYOUT]
    tiles = _gather_small_grads(gs, jnp.full((1, LANES), loss, F32))
    small, loss = _adamw_replicated(tiles, [two_d(W[n]) for n in names], [two_d(Mo[n]) for n in names],
                                    [two_d(Vo[n]) for n in names])
    for i, n in enumerate(names):
        done[n] = [small[kind][i] for kind in range(4)]
    res = [done[n][kind].reshape(env[n].shape) for kind in range(4) for n in ALL_W]
    return (loss, grad_x, *res)
```

```python
import functools

import jax
import jax.numpy as jnp
from jax import lax
from jax.experimental import pallas as pl
from jax.experimental.pallas import tpu as pltpu

F32 = jnp.float32
MXU_DTYPE = jnp.bfloat16
WIRE_DTYPE = jnp.bfloat16
SDS = jax.ShapeDtypeStruct
HIGHEST = lax.Precision.HIGHEST
MESH_ID = pl.DeviceIdType.MESH

D_MODEL = 1024
MLA_HEADS = 4
Q_LORA = 256
KV_LORA = 256
NOPE = 128
ROPE = 64
QK_DIM = NOPE + ROPE
V_DIM = 128
ROPE_THETA = 10000.0
GDN_HEADS = 4
GDN_DIM = 128
GDN_WIDTH = GDN_HEADS * GDN_DIM
CONV_W = 4
CHUNK = 64
D_FF = 4 * D_MODEL
EPS = 1e-6
ATT_SCALE = QK_DIM ** -0.5
GDN_QSCALE = GDN_DIM ** -0.5
N_DEV = 8
ATTN_BLOCK = 512
ATTN_CHAINS = 2
MLP_FWD_SHARDS = 4
MLP_BWD_SHARDS = 4

ADAM_LR = 0.001
ADAM_B1 = 0.9
ADAM_B2 = 0.999
ADAM_EPS = 1e-08
ADAM_WD = 0.01
ADAM_STEP = 10

LANES = 128
SUBLANES = 8
VMEM_LIMIT = 60 * 1024 * 1024

P_GQKV, P_GZ, P_QLAT, P_KVLAT, P_KPE, P_GAB = 0, 1536, 2048, 2304, 2560, 2688
P_WIDTH = 2816
O_QLAT, O_KVLAT, O_KPE, O_GQKV, O_GZ, O_GAB, O_END = 0, 256, 512, 576, 2112, 2624, 2632


def _params(sem=None, vmem=VMEM_LIMIT):
    kw = dict(vmem_limit_bytes=vmem)
    if sem is not None:
        kw["dimension_semantics"] = sem
    return pltpu.CompilerParams(**kw)


def _mm(a, b):
    return jnp.dot(a.astype(MXU_DTYPE), b.astype(MXU_DTYPE), preferred_element_type=F32)


def _mm_nt(a, b):
    return lax.dot_general(a.astype(MXU_DTYPE), b.astype(MXU_DTYPE), (((1,), (1,)), ((), ())),
                           preferred_element_type=F32)


def _mm_tn(a, b):
    return lax.dot_general(a.astype(MXU_DTYPE), b.astype(MXU_DTYPE), (((0,), (0,)), ((), ())),
                           preferred_element_type=F32)


def _split(a):
    hi = a.astype(MXU_DTYPE)
    return hi, (a - hi.astype(F32)).astype(MXU_DTYPE)


def _mm_split(a, b):
    (ah, al), (bh, bl) = a, b
    dot = lambda x, y: jnp.dot(x, y, preferred_element_type=F32)
    if MXU_DTYPE == F32:
        return dot(ah, bh)
    return dot(ah, bh) + dot(ah, bl) + dot(al, bh)


def _mm_exact(a, b):
    return _mm_split(_split(a), _split(b))


def _row_sum(v, on_mxu=False):
    if not on_mxu:
        return jnp.sum(v, axis=-1, keepdims=True)
    d = v.shape[-1]
    ones = jnp.ones((d, LANES), MXU_DTYPE)
    s = sum(jnp.dot(p, ones, preferred_element_type=F32) for p in _split(v))
    return s[:, :d] if d <= LANES else jnp.tile(s, (1, d // LANES))


def _rms(x, w, on_mxu=False):
    r = lax.rsqrt(_row_sum(x * x, on_mxu) * (1.0 / x.shape[-1]) + EPS)
    return x * r * w, r


def _rms_bwd(dy, x, w, r, on_mxu=False):
    xh = x * r
    dyw = dy * w
    dx = r * (dyw - xh * (_row_sum(dyw * xh, on_mxu) * (1.0 / x.shape[-1])))
    dw = jnp.sum(dy * xh, axis=0, keepdims=True)
    return dx, dw


def _l2n(x, scale):
    return x * (lax.rsqrt(_row_sum(x * x) + EPS) * scale)


def _l2n_bwd(dy, x, scale):
    r = lax.rsqrt(_row_sum(x * x) + EPS)
    xh = x * r
    return (scale * r) * (dy - xh * _row_sum(dy * xh))


def _rot(t):
    return jnp.concatenate([-t[:, ROPE // 2:], t[:, :ROPE // 2]], axis=-1)


def _rot_t(t):
    return jnp.concatenate([t[:, ROPE // 2:], -t[:, :ROPE // 2]], axis=-1)


def _rope(t, cos, sin):
    return t * cos + _rot(t) * sin


def _rope_bwd(d, cos, sin):
    return d * cos + _rot_t(d * sin)


def _sigmoid(x):
    return jax.nn.sigmoid(x)


def _shift_down(x, halo, j):
    if j == 0:
        return x
    xr = pltpu.roll(x, j, 0)
    hr = pltpu.roll(halo, j, 0)
    row = lax.broadcasted_iota(jnp.int32, halo.shape, 0)
    top = jnp.where(row < j, hr, xr[:SUBLANES])
    return jnp.concatenate([top, xr[SUBLANES:]], axis=0)


def _shift_up(x, nxt, j):
    if j == 0:
        return x
    n = x.shape[0]
    xr = pltpu.roll(x, n - j, 0)
    nr = pltpu.roll(nxt, SUBLANES - j, 0)
    row = lax.broadcasted_iota(jnp.int32, nxt.shape, 0)
    bot = jnp.where(row >= SUBLANES - j, nr, xr[n - SUBLANES:])
    return jnp.concatenate([xr[:n - SUBLANES], bot], axis=0)


def _chunk_cumsum(y, row_in_chunk):
    s = 1
    while s < CHUNK:
        y = y + jnp.where(row_in_chunk >= s, pltpu.roll(y, s, 0), 0.0)
        s *= 2
    return y


def _chunk_rev_cumsum(y, row_in_chunk):
    n = y.shape[0]
    s = 1
    while s < CHUNK:
        y = y + jnp.where(row_in_chunk + s < CHUNK, pltpu.roll(y, n - s, 0), 0.0)
        s *= 2
    return y


def _together(generators):
    alive = list(generators)
    while alive:
        nxt = []
        for g in alive:
            try:
                next(g)
                nxt.append(g)
            except StopIteration:
                pass
        alive = nxt
        yield


def _lockstep(generators):
    for _ in _together(generators):
        pass


def _pick_lane(tile, lane, idx):
    return jnp.sum(jnp.where(lane == idx, tile, 0.0), axis=-1, keepdims=True)


def _divisor_tile(n, cap, unit=LANES):
    best = unit
    t = unit
    while t <= min(n, cap):
        if n % t == 0:
            best = t
        t += unit
    return n if n <= cap else best


def _in_proj(x2, w_an, w_in_p, conv_w, alog_l, dt_l, S):
    T, D = x2.shape
    N = w_in_p.shape[1]
    tm = min(512, S)
    assert S % tm == 0 and T % tm == 0, "a token tile must not straddle two sequences"
    tiles_per_seq = S // tm
    C3 = 3 * GDN_WIDTH
    H = GDN_HEADS

    def body(x_ref, wn_ref, w_ref, cw_ref, alog_ref, dt_ref, proj_ref, xn_ref, q_out, k_out, v_out, gates_out,
             halo_s):
        xn, _ = _rms(x_ref[...], wn_ref[...])
        xn = xn.astype(MXU_DTYPE)
        xn_ref[...] = xn
        proj = jnp.dot(xn, w_ref[...], preferred_element_type=F32)
        proj_ref[...] = proj
        u = proj[:, P_GQKV:P_GQKV + C3]

        @pl.when(pl.program_id(0) == 0)
        def _():
            halo_s[...] = jnp.zeros_like(halo_s)

        halo = jnp.where(pl.program_id(0) % tiles_per_seq == 0, 0.0, halo_s[...])
        halo_s[...] = u[tm - SUBLANES:, :]
        c, _ = _conv_taps(u, halo, cw_ref[...])
        a = c * _sigmoid(c)
        for h in range(H):
            xq = a[:, h * GDN_DIM:(h + 1) * GDN_DIM]
            xk = a[:, GDN_WIDTH + h * GDN_DIM:GDN_WIDTH + (h + 1) * GDN_DIM]
            q_out[h] = _l2n(xq, GDN_QSCALE)
            k_out[h] = _l2n(xk, 1.0)
            v_out[h] = a[:, 2 * GDN_WIDTH + h * GDN_DIM:2 * GDN_WIDTH + (h + 1) * GDN_DIM]
        lane = lax.broadcasted_iota(jnp.int32, (tm, LANES), 1)
        ric = lax.broadcasted_iota(jnp.int32, (tm, LANES), 0) % CHUNK
        g, beta = _gate_values(proj[:, P_GAB:P_GAB + LANES], alog_ref[...], dt_ref[...], lane)
        gates_out[...] = _chunk_cumsum(g, ric) + beta

    hspec = pl.BlockSpec((H, tm, GDN_DIM), lambda i: (0, i, 0))
    vec = pl.BlockSpec((1, LANES), lambda i: (0, 0))
    return pl.pallas_call(
        body, grid=(T // tm,), name="in_proj",
        in_specs=[pl.BlockSpec((tm, D), lambda i: (i, 0)), pl.BlockSpec((1, D), lambda i: (0, 0)),
                  pl.BlockSpec((D, N), lambda i: (0, 0)), pl.BlockSpec((CONV_W, C3), lambda i: (0, 0)), vec, vec],
        out_specs=[pl.BlockSpec((tm, N), lambda i: (i, 0)), pl.BlockSpec((tm, D), lambda i: (i, 0)),
                   hspec, hspec, hspec, pl.BlockSpec((tm, LANES), lambda i: (i, 0))],
        out_shape=[SDS((T, N), F32), SDS((T, D), MXU_DTYPE)] + [SDS((H, T, GDN_DIM), F32)] * 3
                  + [SDS((T, LANES), F32)],
        scratch_shapes=[pltpu.VMEM((SUBLANES, C3), F32)],
        compiler_params=_params(("arbitrary",)),
    )(x2, w_an, w_in_p, conv_w, alog_l, dt_l)


def _mla_pre(proj, cosf, sinf, w_qln, w_kvln, w_uq_p, w_ukv, qnw, knw, transfer=None):
    T = proj.shape[0]
    tm = min(256, T)
    H = MLA_HEADS

    def body(ql_ref, kvl_ref, kpe_ref, cos_ref, sin_ref, wq_ref, wkv_ref, uq_ref, ukv_ref, qnw_ref, knw_ref,
             q_out, k_out, v_out):
        rms = functools.partial(_rms, on_mxu=True)
        cos, sin = cos_ref[...], sin_ref[...]
        qnw_, knw_ = qnw_ref[...], knw_ref[...]
        qn, _ = rms(ql_ref[...], wq_ref[...])
        kvn, _ = rms(kvl_ref[...], wkv_ref[...])
        qraw = _mm(qn, uq_ref[...])
        kvraw = _mm(kvn, ukv_ref[...])
        kpe = _rope(rms(kpe_ref[...][:, :ROPE], knw_[:, NOPE:])[0], cos, sin)
        for h in range(H):
            qn_h = rms(qraw[:, h * NOPE:(h + 1) * NOPE], qnw_[:, :NOPE])[0]
            qp_h = _rope(rms(qraw[:, H * NOPE + h * ROPE:H * NOPE + (h + 1) * ROPE], qnw_[:, NOPE:])[0], cos, sin)
            q_out[h] = (jnp.concatenate([qn_h, qp_h], axis=-1) * ATT_SCALE).astype(MXU_DTYPE)
            kn_h = rms(kvraw[:, h * 256:h * 256 + NOPE], knw_[:, :NOPE])[0]
            k_out[h] = jnp.concatenate([kn_h, kpe], axis=-1).astype(MXU_DTYPE)
            v_out[h] = kvraw[:, h * 256 + NOPE:(h + 1) * 256].astype(MXU_DTYPE)

    full = lambda a: pl.BlockSpec(a.shape, lambda i: (0,) * a.ndim)
    return _call_beside(
        body, transfer, grid=(T // tm,), name="mla_pre", scratch_shapes=[], semantics=("arbitrary",),
        args=(proj, proj, proj, cosf, sinf, w_qln, w_kvln, w_uq_p, w_ukv, qnw, knw),
        in_specs=[pl.BlockSpec((tm, 256), lambda i: (i, P_QLAT // 256)),
                  pl.BlockSpec((tm, 256), lambda i: (i, P_KVLAT // 256)),
                  pl.BlockSpec((tm, 128), lambda i: (i, P_KPE // 128)),
                  pl.BlockSpec((tm, ROPE), lambda i: (i, 0)), pl.BlockSpec((tm, ROPE), lambda i: (i, 0)),
                  full(w_qln), full(w_kvln), full(w_uq_p), full(w_ukv), full(qnw), full(knw)],
        out_specs=[pl.BlockSpec((H, tm, QK_DIM), lambda i: (0, i, 0)),
                   pl.BlockSpec((H, tm, QK_DIM), lambda i: (0, i, 0)),
                   pl.BlockSpec((H, tm, V_DIM), lambda i: (0, i, 0))],
        out_shape=[SDS((H, T, QK_DIM), MXU_DTYPE), SDS((H, T, QK_DIM), MXU_DTYPE), SDS((H, T, V_DIM), MXU_DTYPE)])


def _attn_fwd(q4, k4, v4, B, S, transfer=None):
    H = MLA_HEADS
    bq = min(ATTN_BLOCK, S)
    nq = S // bq
    rows = bq // ATTN_CHAINS

    def body(q_ref, k_ref, v_ref, o_ref, lse_ref):
        col = lax.broadcasted_iota(jnp.int32, (rows, bq), 1)
        row = lax.broadcasted_iota(jnp.int32, (rows, bq), 0)

        def q_step(qi, carry):
            qs = pl.multiple_of(qi * bq, bq)
            qsub = [q_ref[0, pl.ds(qs + j * rows, rows), :] for j in range(ATTN_CHAINS)]

            def k_block(ks, cs, diagonal):
                k = k_ref[0, pl.ds(ks, bq), :]
                v = v_ref[0, pl.ds(ks, bq), :]
                out = [None] * ATTN_CHAINS

                def chain(j):
                    m, l, acc = cs[j]
                    s = _mm_nt(qsub[j], k)
                    yield
                    if diagonal:
                        s = jnp.where(col <= row + j * rows, s, -jnp.inf)
                    m_new = jnp.maximum(m, jnp.max(s, axis=-1, keepdims=True))
                    p = jnp.exp(s - m_new)
                    a = jnp.exp(m - m_new)
                    l_new = a * l + jnp.sum(p, axis=-1, keepdims=True)
                    yield
                    out[j] = (m_new, l_new, a * acc + _mm(p, v))

                _lockstep([chain(j) for j in range(ATTN_CHAINS)])
                return tuple(out)

            init = tuple((jnp.full((rows, 1), -jnp.inf, F32), jnp.zeros((rows, 1), F32),
                          jnp.zeros((rows, V_DIM), F32)) for _ in range(ATTN_CHAINS))
            cs = lax.fori_loop(0, qi, lambda kj, c: k_block(pl.multiple_of(kj * bq, bq), c, False), init)
            for j, (m, l, acc) in enumerate(k_block(qs, cs, True)):
                o_ref[0, pl.ds(qs + j * rows, rows), :] = acc / l
                lse_ref[0, pl.ds(qs + j * rows, rows), :] = m + jnp.log(l)
            return carry

        lax.fori_loop(0, nq, q_step, 0)

    spec = lambda d: pl.BlockSpec((1, S, d), lambda h, b: (h, b, 0))
    return _call_beside(
        body, transfer, grid=(H, B), name="attn_fwd",
        in_specs=[spec(QK_DIM), spec(QK_DIM), spec(V_DIM)],
        out_specs=[spec(V_DIM), spec(1)],
        out_shape=[SDS((H, B * S, V_DIM), F32), SDS((H, B * S, 1), F32)],
        scratch_shapes=[], semantics=("arbitrary", "arbitrary"), args=(q4, k4, v4))


def _conv_taps(u, halo, w):
    sh = [_shift_down(u, halo, j) for j in range(CONV_W)]
    c = w[0:1] * sh[3] + w[1:2] * sh[2] + w[2:3] * sh[1] + w[3:4] * sh[0]
    return c, sh


def _gate_values(gab, alog_l, dt_l, lane):
    g = -jnp.exp(alog_l) * jax.nn.softplus(gab + dt_l)
    g = jnp.where(lane < GDN_HEADS, g, 0.0)
    beta = jnp.where((lane >= GDN_HEADS) & (lane < 2 * GDN_HEADS), _sigmoid(gab), 0.0)
    return g, beta


def _unit_lower_inverses(Ls, eye):
    Ps = [eye - L for L in Ls]
    Ms = [_split(-L) for L in Ls]
    for _ in range(5):
        sq = [_mm_split(m, m) for m in Ms]
        Ms = [_split(s) for s in sq]
        Ps = [p + _mm_split(_split(p), m) for p, m in zip(Ps, Ms)]
    return Ps


def _chunk_decays(gt, lane, h, ri, ci, rcol):
    Gc = _pick_lane(gt, lane, h)
    bt = _pick_lane(gt, lane, h + GDN_HEADS)
    Gb = jnp.broadcast_to(Gc, (CHUNK, CHUNK))
    Gam = jnp.where(ri >= ci, jnp.exp(Gb - Gb.T), 0.0)
    Gl = jnp.sum(jnp.where(rcol == CHUNK - 1, Gc, 0.0), axis=0, keepdims=True)
    return Gc, bt, Gam, jnp.exp(Gc), jnp.exp(Gl - Gc), jnp.exp(Gl)


GDN_FWD_UNROLL = 16
GDN_BWD_UNROLL = 8
GDN_RECUR_STEPS_PER_STAGE = 2


def _gdn_fwd(qg, kg, vg, gates, B, S, transfer=None):
    H, D, C = GDN_HEADS, GDN_DIM, CHUNK
    NC = S // C
    P = 2 if B % 2 == 0 else 1
    Sb, NCb = P * S, P * NC
    U = GDN_FWD_UNROLL if NCb % GDN_FWD_UNROLL == 0 else 1
    NG = NCb // U

    def body(q_ref, k_ref, v_ref, g_ref, o_ref, st_ref, ai_ref, u_ref, w_ref, q2_s, au_s, bc_s, w2_s, el_s):
        h = pl.program_id(0)
        lane = lax.broadcasted_iota(jnp.int32, (C, LANES), 1)
        ri = lax.broadcasted_iota(jnp.int32, (C, C), 0)
        ci = lax.broadcasted_iota(jnp.int32, (C, C), 1)
        rcol = lax.broadcasted_iota(jnp.int32, (C, 1), 0)
        eye = (ri == ci).astype(F32)

        def group(gi, c):
            ns = [gi * U + j for j in range(U)]
            css = [pl.multiple_of(n * C, C) for n in ns]
            qs = [q_ref[0, pl.ds(cs, C), :] for cs in css]
            ks = [k_ref[0, pl.ds(cs, C), :] for cs in css]
            vs = [v_ref[0, pl.ds(cs, C), :] for cs in css]
            decs = [_chunk_decays(g_ref[pl.ds(cs, C), :], lane, h, ri, ci, rcol) for cs in css]
            qks = [_mm_nt(jnp.concatenate([q, k], axis=0), k) for q, k in zip(qs, ks)]
            ainvs = _unit_lower_inverses(
                [jnp.where(ri > ci, d[1] * qk[C:] * d[2], 0.0) for qk, d in zip(qks, decs)], eye)
            sols = [_mm_exact(a, jnp.concatenate([v * d[1], k * (d[1] * d[3])], axis=-1))
                    for a, k, v, d in zip(ainvs, ks, vs, decs)]
            atuw = [_mm(qk[:C] * d[2], sol) for qk, d, sol in zip(qks, decs, sols)]
            kduw = [_mm_tn(k * d[4], sol) for k, d, sol in zip(ks, decs, sols)]
            for n, cs, q, a, sol, au, ku, (Gc, bt, Gam, e, f, eL) in zip(ns, css, qs, ainvs, sols, atuw, kduw, decs):
                u_ref[0, pl.ds(cs, C), :] = sol[:, :D]
                w_ref[0, pl.ds(cs, C), :] = sol[:, D:]
                au_s[pl.ds(cs, C), :] = au[:, :D]
                q2_s[pl.ds(cs, C), :] = q * e - au[:, D:]
                bc_s[n] = ku[:, :D]
                w2_s[n] = ku[:, D:]
                el_s[n] = jnp.broadcast_to(eL, (SUBLANES, LANES))
                ai_ref[0, n] = a.T
            return c

        lax.fori_loop(0, NG, group, 0)

        def step(n, states):
            new = []
            for p, S_ in enumerate(states):
                m = p * NC + n
                cs = pl.multiple_of(m * C, C)
                o_ref[0, pl.ds(cs, C), :] = _mm(q2_s[pl.ds(cs, C), :], S_) + au_s[pl.ds(cs, C), :]
                st_ref[0, m] = S_
                new.append(S_ * el_s[m, 0:1, :] + bc_s[m] - _mm(w2_s[m], S_))
            return tuple(new)

        lax.fori_loop(0, NC, step, tuple(jnp.zeros((D, D), F32) for _ in range(P)))

    spec = pl.BlockSpec((1, Sb, D), lambda h, b: (h, b, 0))
    return _call_beside(
        body, transfer, grid=(H, B // P), name="gdn_fwd",
        in_specs=[spec, spec, spec, pl.BlockSpec((Sb, LANES), lambda h, b: (b, 0))],
        out_specs=[spec, pl.BlockSpec((1, NCb, D, D), lambda h, b: (h, b, 0, 0)),
                   pl.BlockSpec((1, NCb, C, C), lambda h, b: (h, b, 0, 0)), spec, spec],
        out_shape=[SDS((H, B * S, D), F32), SDS((H, B * NC, D, D), F32), SDS((H, B * NC, C, C), F32),
                   SDS((H, B * S, D), F32), SDS((H, B * S, D), F32)],
        scratch_shapes=[pltpu.VMEM((Sb, D), F32), pltpu.VMEM((Sb, D), F32), pltpu.VMEM((NCb, D, D), F32),
                        pltpu.VMEM((NCb, D, D), F32), pltpu.VMEM((NCb, SUBLANES, LANES), F32)],
        semantics=("arbitrary", "arbitrary"), args=(qg, kg, vg, gates))


def _mix_out(o_mla, o_gdn, proj, x2, mla_w, gdn_w, w_out):
    T, D = x2.shape
    tm = min(512, T)
    H = MLA_HEADS

    def body(om_ref, og_ref, z_ref, x_ref, mw_ref, gw_ref, w_ref, h_ref, mix_ref):
        z = z_ref[...]
        parts = [_rms(om_ref[h], mw_ref[h:h + 1, :])[0] for h in range(H)]
        for h in range(GDN_HEADS):
            zh = z[:, h * GDN_DIM:(h + 1) * GDN_DIM]
            parts.append(_rms(og_ref[h], gw_ref[...])[0] * (zh * _sigmoid(zh)))
        mix = jnp.concatenate(parts, axis=-1).astype(MXU_DTYPE)
        mix_ref[...] = mix
        h_ref[...] = x_ref[...] + jnp.dot(mix, w_ref[...], preferred_element_type=F32)

    hspec = pl.BlockSpec((H, tm, V_DIM), lambda i: (0, i, 0))
    return pl.pallas_call(
        body, grid=(T // tm,), name="mix_out",
        in_specs=[hspec, hspec, pl.BlockSpec((tm, GDN_WIDTH), lambda i: (i, P_GZ // GDN_WIDTH)),
                  pl.BlockSpec((tm, D), lambda i: (i, 0)),
                  pl.BlockSpec((H, V_DIM), lambda i: (0, 0)), pl.BlockSpec((1, GDN_DIM), lambda i: (0, 0)),
                  pl.BlockSpec((D, D), lambda i: (0, 0))],
        out_specs=[pl.BlockSpec((tm, D), lambda i: (i, 0)), pl.BlockSpec((tm, D), lambda i: (i, 0))],
        out_shape=[SDS((T, D), F32), SDS((T, D), MXU_DTYPE)],
        compiler_params=_params(("arbitrary",)),
    )(o_mla, o_gdn, proj, x2, mla_w, gdn_w, w_out)


def _mlp_fwd(h2, w_mn, w_up, w_down, target):
    T, D = h2.shape
    ns, _, ts = w_up.shape
    F = ns * ts
    tm = min(512, T)
    G = MLP_FWD_SHARDS
    tf, nf = G * ts, ns // G

    def body(h_ref, wn_ref, up_w, down_w, t_ref, up_ref, hn_ref, dy_ref, loss_ref, dyb_ref, y_acc):
        j = pl.program_id(1)

        @pl.when(j == 0)
        def _():
            hn_ref[...] = _rms(h_ref[...], wn_ref[...])[0].astype(MXU_DTYPE)
            y_acc[...] = h_ref[...]

        parts = []
        for c in range(G):
            up = jnp.dot(hn_ref[...], up_w[c], preferred_element_type=F32)
            up_ref[:, c * ts:(c + 1) * ts] = up.astype(MXU_DTYPE)
            r = jnp.maximum(up, 0.0)
            parts.append(_mm(r * r, down_w[c * ts:(c + 1) * ts, :]))
        y_acc[...] += functools.reduce(jnp.add, parts)

        @pl.when(j == nf - 1)
        def _():
            err = y_acc[...] - t_ref[...]
            dy_ref[...] = err / D
            dyb_ref[...] = (err / D).astype(MXU_DTYPE)
            loss_ref[...] = jnp.full((1, SUBLANES, LANES), jnp.sum(err * err), F32)

    return pl.pallas_call(
        body, grid=(T // tm, nf), name="mlp_fwd",
        in_specs=[pl.BlockSpec((tm, D), lambda i, j: (i, 0)), pl.BlockSpec((1, D), lambda i, j: (0, 0)),
                  pl.BlockSpec((G, D, ts), lambda i, j: (j, 0, 0)), pl.BlockSpec((tf, D), lambda i, j: (j, 0)),
                  pl.BlockSpec((tm, D), lambda i, j: (i, 0))],
        out_specs=[pl.BlockSpec((tm, tf), lambda i, j: (i, j)), pl.BlockSpec((tm, D), lambda i, j: (i, 0)),
                   pl.BlockSpec((tm, D), lambda i, j: (i, 0)),
                   pl.BlockSpec((1, SUBLANES, LANES), lambda i, j: (i, 0, 0)),
                   pl.BlockSpec((tm, D), lambda i, j: (i, 0))],
        out_shape=[SDS((T, F), MXU_DTYPE), SDS((T, D), MXU_DTYPE), SDS((T, D), F32),
                   SDS((T // tm, SUBLANES, LANES), F32), SDS((T, D), MXU_DTYPE)],
        scratch_shapes=[pltpu.VMEM((tm, D), F32)],
        compiler_params=_params(("arbitrary", "arbitrary")),
    )(h2, w_mn, w_up, w_down, target)


def _mlp_bwd(dy, dyb, up, h2, w_mn, w_up, w_down, transfer=None):
    T, D = h2.shape
    ns, _, ts = w_up.shape
    F = ns * ts
    tm = min(512, T)
    G = MLP_BWD_SHARDS
    tf, nf = G * ts, ns // G

    def body(dy_ref, dyb_ref, up_ref, h_ref, wn_ref, up_w, down_w, dh_ref, dhb_ref, dup_ref, dwn_ref, acc):
        i, j = pl.program_id(0), pl.program_id(1)

        @pl.when((i == 0) & (j == 0))
        def _():
            dwn_ref[...] = jnp.zeros_like(dwn_ref)

        @pl.when(j == 0)
        def _():
            acc[...] = jnp.zeros_like(acc)

        parts = []
        for c in range(G):
            cols = slice(c * ts, (c + 1) * ts)
            r = jnp.maximum(up_ref[:, cols].astype(F32), 0.0)
            dup = (_mm_nt(dyb_ref[...], down_w[cols, :]) * (2.0 * r)).astype(MXU_DTYPE)
            dup_ref[:, cols] = dup
            parts.append(_mm_nt(dup, up_w[c]))
        acc[...] += functools.reduce(jnp.add, parts)

        @pl.when(j == nf - 1)
        def _():
            hv = h_ref[...]
            _, rr = _rms(hv, wn_ref[...])
            dx, dw = _rms_bwd(acc[...], hv, wn_ref[...], rr)
            dh = dy_ref[...] + dx
            dh_ref[...] = dh
            dhb_ref[...] = dh.astype(MXU_DTYPE)
            dwn_ref[...] += dw

    row = lambda i, j: (i, 0)
    return _call_beside(
        body, transfer, grid=(T // tm, nf), name="mlp_bwd",
        in_specs=[pl.BlockSpec((tm, D), row), pl.BlockSpec((tm, D), row), pl.BlockSpec((tm, tf), lambda i, j: (i, j)),
                  pl.BlockSpec((tm, D), row), pl.BlockSpec((1, D), lambda i, j: (0, 0)),
                  pl.BlockSpec((G, D, ts), lambda i, j: (j, 0, 0)), pl.BlockSpec((tf, D), lambda i, j: (j, 0))],
        out_specs=[pl.BlockSpec((tm, D), row), pl.BlockSpec((tm, D), row),
                   pl.BlockSpec((tm, tf), lambda i, j: (i, j)), pl.BlockSpec((1, D), lambda i, j: (0, 0))],
        out_shape=[SDS((T, D), F32), SDS((T, D), MXU_DTYPE), SDS((T, F), MXU_DTYPE), SDS((1, D), F32)],
        scratch_shapes=[pltpu.VMEM((tm, D), F32)], semantics=("arbitrary", "arbitrary"),
        args=(dy, dyb, up, h2, w_mn, w_up, w_down))


def _mix_bwd(dhb, o_mla, o_gdn, proj, mla_w, gdn_w, w_out):
    T, D = dhb.shape
    tm = min(512, T)
    H = MLA_HEADS

    def body(dh_ref, om_ref, og_ref, z_ref, mw_ref, gw_ref, w_ref, dom_ref, dog_ref, dz_ref, dmw_ref, dgw_ref,
             delta_ref):
        @pl.when(pl.program_id(0) == 0)
        def _():
            dmw_ref[...] = jnp.zeros_like(dmw_ref)
            dgw_ref[...] = jnp.zeros_like(dgw_ref)

        dmix = _mm_nt(dh_ref[...], w_ref[...])
        z = z_ref[...]
        dmw, dzs = [], []
        dgw = jnp.zeros((1, GDN_DIM), F32)
        for h in range(H):
            o = om_ref[h]
            w = mw_ref[h:h + 1, :]
            _, r = _rms(o, w)
            dx, dw = _rms_bwd(dmix[:, h * V_DIM:(h + 1) * V_DIM], o, w, r)
            dom_ref[h] = dx.astype(MXU_DTYPE)
            delta_ref[h] = jnp.sum(dx * o, axis=-1, keepdims=True)
            dmw.append(dw)
        for h in range(GDN_HEADS):
            o = og_ref[h]
            w = gw_ref[...]
            zh = z[:, h * GDN_DIM:(h + 1) * GDN_DIM]
            sg = _sigmoid(zh)
            yn, r = _rms(o, w)
            dy = dmix[:, H * V_DIM + h * GDN_DIM:H * V_DIM + (h + 1) * GDN_DIM]
            dzs.append(dy * yn * (sg * (1.0 + zh * (1.0 - sg))))
            dx, dw = _rms_bwd(dy * (zh * sg), o, w, r)
            dog_ref[h] = dx.astype(MXU_DTYPE)
            dgw = dgw + dw
        dz_ref[...] = jnp.concatenate(dzs, axis=-1).astype(MXU_DTYPE)
        dmw_ref[...] += jnp.concatenate(dmw, axis=0)
        dgw_ref[...] += dgw

    hspec = pl.BlockSpec((H, tm, V_DIM), lambda i: (0, i, 0))
    return pl.pallas_call(
        body, grid=(T // tm,), name="mix_bwd",
        in_specs=[pl.BlockSpec((tm, D), lambda i: (i, 0)), hspec, hspec,
                  pl.BlockSpec((tm, GDN_WIDTH), lambda i: (i, P_GZ // GDN_WIDTH)),
                  pl.BlockSpec((H, V_DIM), lambda i: (0, 0)), pl.BlockSpec((1, GDN_DIM), lambda i: (0, 0)),
                  pl.BlockSpec((D, D), lambda i: (0, 0))],
        out_specs=[hspec, hspec, pl.BlockSpec((tm, GDN_WIDTH), lambda i: (i, 0)),
                   pl.BlockSpec((H, V_DIM), lambda i: (0, 0)), pl.BlockSpec((1, GDN_DIM), lambda i: (0, 0)),
                   pl.BlockSpec((H, tm, 1), lambda i: (0, i, 0))],
        out_shape=[SDS((H, T, V_DIM), MXU_DTYPE), SDS((H, T, GDN_DIM), MXU_DTYPE), SDS((T, GDN_WIDTH), MXU_DTYPE),
                   SDS((H, V_DIM), F32), SDS((1, GDN_DIM), F32), SDS((H, T, 1), F32)],
        compiler_params=_params(("arbitrary",)),
    )(dhb, o_mla, o_gdn, proj, mla_w, gdn_w, w_out)


def _attn_bwd(q4, k4, v4, do4, delta4, lse4, B, S, transfer=None):
    H = MLA_HEADS
    bq = min(ATTN_BLOCK, S)
    nq = S // bq
    rows = bq // ATTN_CHAINS

    def body(q_ref, k_ref, v_ref, do_ref, delta_ref, lse_ref, dq_ref, dk_ref, dv_ref):
        dq_ref[...] = jnp.zeros_like(dq_ref)
        dk_ref[...] = jnp.zeros_like(dk_ref)
        dv_ref[...] = jnp.zeros_like(dv_ref)

        col = lax.broadcasted_iota(jnp.int32, (rows, bq), 1)
        row = lax.broadcasted_iota(jnp.int32, (rows, bq), 0)

        def k_step(kj, carry):
            ks = pl.multiple_of(kj * bq, bq)
            k = k_ref[0, pl.ds(ks, bq), :]
            v = v_ref[0, pl.ds(ks, bq), :]

            def q_block(qs, diagonal):
                dks, dvs = [None] * ATTN_CHAINS, [None] * ATTN_CHAINS

                def chain(j):
                    sl = pl.ds(qs + j * rows, rows)
                    q = q_ref[0, sl, :]
                    do = do_ref[0, sl, :].astype(MXU_DTYPE)
                    s = _mm_nt(q, k)
                    dp = _mm_nt(do, v)
                    yield
                    p = jnp.exp(s - lse_ref[0, sl, :])
                    if diagonal:
                        p = jnp.where(col <= row + j * rows, p, 0.0)
                    ds = p * (dp - delta_ref[0, sl, :])
                    yield
                    dvs[j] = _mm_tn(p, do)
                    dks[j] = _mm_tn(ds, q)
                    dq_ref[0, sl, :] += _mm(ds, k)

                _lockstep([chain(j) for j in range(ATTN_CHAINS)])
                dv_ref[0, pl.ds(ks, bq), :] += functools.reduce(jnp.add, dvs)
                dk_ref[0, pl.ds(ks, bq), :] += functools.reduce(jnp.add, dks)

            q_block(ks, True)

            def q_step(qi, c):
                q_block(pl.multiple_of(qi * bq, bq), False)
                return c

            lax.fori_loop(kj + 1, nq, q_step, 0)
            return carry

        lax.fori_loop(0, nq, k_step, 0)

    spec = lambda d: pl.BlockSpec((1, S, d), lambda h, b: (h, b, 0))
    return _call_beside(
        body, transfer, grid=(H, B), name="attn_bwd",
        in_specs=[spec(QK_DIM), spec(QK_DIM), spec(V_DIM), spec(V_DIM), spec(1), spec(1)],
        out_specs=[spec(QK_DIM), spec(QK_DIM), spec(V_DIM)],
        out_shape=[SDS((H, B * S, QK_DIM), F32), SDS((H, B * S, QK_DIM), F32), SDS((H, B * S, V_DIM), F32)],
        scratch_shapes=[], semantics=("arbitrary", "arbitrary"),
        args=(q4, k4, v4, do4, delta4, lse4))


def _gdn_bwd(qg, kg, vg, gates, states, ainv, u4, w4, do4, B, S, transfer=None):
    H, D, C = GDN_HEADS, GDN_DIM, CHUNK
    NC = S // C
    U = GDN_BWD_UNROLL if NC % GDN_BWD_UNROLL == 0 else 1
    NG = NC // U

    def body(q_ref, k_ref, v_ref, g_ref, st_ref, ai_ref, u_ref, w_ref, do_ref, dq_ref, dk_ref, dv_ref, dgb_ref,
             kd_s, x1_s, x2_s, el_s, dvn_s, ds_s, w2t_s):
        h = pl.program_id(0)
        lane = lax.broadcasted_iota(jnp.int32, (C, LANES), 1)
        ri = lax.broadcasted_iota(jnp.int32, (C, C), 0)
        ci = lax.broadcasted_iota(jnp.int32, (C, C), 1)
        rcol = lax.broadcasted_iota(jnp.int32, (C, 1), 0)

        def rsum(a):
            return jnp.sum(a, axis=-1, keepdims=True)

        def prepare(n):
            cs = n * C
            q = q_ref[0, pl.ds(cs, C), :]
            k = k_ref[0, pl.ds(cs, C), :]
            do = do_ref[0, pl.ds(cs, C), :]
            Gc, bt, Gam, e, f, eL = _chunk_decays(g_ref[pl.ds(cs, C), :], lane, h, ri, ci, rcol)
            At = _mm_nt(q, k) * Gam
            yield
            x1 = _mm_tn(At, do)
            x2 = _mm_tn(q * e, do)
            kd = k * f
            w = w_ref[0, pl.ds(cs, C), :]
            yield
            x1_s[pl.ds(cs, C), :] = x1
            x2_s[n] = x2 - _mm_tn(w, x1)
            w2t_s[n] = _mm_tn(w, kd)
            kd_s[pl.ds(cs, C), :] = kd
            el_s[n] = jnp.broadcast_to(eL, (SUBLANES, LANES))

        def recur(n, dS):
            cs = n * C
            ds_s[n] = dS
            dvn_s[pl.ds(cs, C), :] = x1_s[pl.ds(cs, C), :] + _mm(kd_s[pl.ds(cs, C), :], dS)
            return x2_s[n] + el_s[n, 0:1, :] * dS - _mm(w2t_s[n], dS)

        def local(n):
            cs = n * C
            q = q_ref[0, pl.ds(cs, C), :]
            k = k_ref[0, pl.ds(cs, C), :]
            v = v_ref[0, pl.ds(cs, C), :]
            do = do_ref[0, pl.ds(cs, C), :]
            u = u_ref[0, pl.ds(cs, C), :]
            w = w_ref[0, pl.ds(cs, C), :]
            dvn = dvn_s[pl.ds(cs, C), :]
            dS = ds_s[n]
            Gc, bt, Gam, e, f, eL = _chunk_decays(g_ref[pl.ds(cs, C), :], lane, h, ri, ci, rcol)
            S0 = st_ref[0, n]
            AinvT = ai_ref[0, n]
            qk = _mm_nt(jnp.concatenate([q, k], axis=0), k)
            QK, KK = qk[:C], qk[C:]
            be = bt * e
            sol = jnp.concatenate([u, w], axis=-1)
            vn = u - _mm(w, S0)
            yield
            dAt = jnp.where(ri >= ci, _mm_nt(do, vn), 0.0)
            dqd = _mm_nt(do, S0)
            dw = -_mm_nt(dvn, S0)
            dkd = _mm_nt(vn, dS)
            deL = jnp.sum(rsum(dS * S0), axis=0, keepdims=True)
            yield
            dR = _mm_exact(AinvT, jnp.concatenate([dvn, dw], axis=-1))
            dR1, dR2 = dR[:, :D], dR[:, D:]
            yield
            dL = jnp.where(ri > ci, -_mm_nt(dR, sol), 0.0)
            yield
            dv_ref[0, pl.ds(cs, C), :] = dR1 * bt
            r2 = rsum(dR2 * k)
            X = dL * Gam
            dbt = rsum(dR1 * v) + r2 * e + rsum(X * KK)
            de = r2 * bt + rsum(dqd * q)
            dKK = X * bt
            dQK = dAt * Gam
            dq_ref[0, pl.ds(cs, C), :] = _mm(dQK, k) + dqd * e
            dk_ref[0, pl.ds(cs, C), :] = dR2 * be + _mm(dKK + dKK.T, k) + _mm_tn(dQK, q) + dkd * f
            df = rsum(dkd * k)
            Z = (dL * (bt * KK) + dAt * QK) * Gam
            dG = rsum(Z) - rsum(Z.T) + de * e - df * f
            dGl = jnp.sum(df * f, axis=0, keepdims=True) + deL * eL
            dG = dG + jnp.where(rcol == C - 1, dGl, 0.0)
            dgb_ref[0, pl.ds(cs, C), :] = jnp.where(lane == 0, dG, jnp.where(lane == 1, dbt, 0.0))

        state = [jnp.zeros((D, D), F32)]

        def recur_group(g):
            for j, n in enumerate(reversed(range(g * U, (g + 1) * U))):
                state[0] = recur(n, state[0])
                if j % GDN_RECUR_STEPS_PER_STAGE == GDN_RECUR_STEPS_PER_STAGE - 1:
                    yield

        def stage(fn, g):
            return _together([fn(g * U + j) for j in range(U)])

        for step in range(NG + 2):
            jobs = [(stage, prepare, NG - 1 - step), (None, None, NG - step), (stage, local, NG + 1 - step)]
            _lockstep([recur_group(g) if make is None else make(fn, g) for make, fn, g in jobs if 0 <= g < NG])

    spec = pl.BlockSpec((1, S, D), lambda h, b: (h, b, 0))
    return _call_beside(
        body, transfer, grid=(H, B), name="gdn_bwd",
        in_specs=[spec, spec, spec, pl.BlockSpec((S, LANES), lambda h, b: (b, 0)),
                  pl.BlockSpec((1, NC, D, D), lambda h, b: (h, b, 0, 0)),
                  pl.BlockSpec((1, NC, C, C), lambda h, b: (h, b, 0, 0)), spec, spec, spec],
        out_specs=[spec, spec, spec, spec],
        out_shape=[SDS((H, B * S, D), F32)] * 4,
        scratch_shapes=[pltpu.VMEM((S, D), F32), pltpu.VMEM((S, D), F32), pltpu.VMEM((NC, D, D), F32),
                        pltpu.VMEM((NC, SUBLANES, LANES), F32), pltpu.VMEM((S, D), F32),
                        pltpu.VMEM((NC, D, D), F32), pltpu.VMEM((NC, D, D), F32)],
        semantics=("arbitrary", "arbitrary"), args=(qg, kg, vg, gates, states, ainv, u4, w4, do4))


def _gdn_pre_bwd(proj, conv_w, alog_l, dt_l, dq4, dk4, dv4, dgb4, S):
    T = proj.shape[0]
    tm = min(256, T)
    tiles_per_seq = S // tm
    C3 = 3 * GDN_WIDTH
    H = GDN_HEADS

    def body(u_ref, halo_ref, gab_ref, w_ref, alog_ref, dt_ref, dq_ref, dk_ref, dv_ref, dgb_ref,
             dc_ref, dgab_ref, dcw_ref, dalog_ref, ddt_ref):
        i = pl.program_id(0)

        @pl.when(i == 0)
        def _():
            dcw_ref[...] = jnp.zeros_like(dcw_ref)
            dalog_ref[...] = jnp.zeros_like(dalog_ref)
            ddt_ref[...] = jnp.zeros_like(ddt_ref)

        halo = jnp.where(i % tiles_per_seq == 0, 0.0, halo_ref[...])
        c, sh = _conv_taps(u_ref[...], halo, w_ref[...])
        sg = _sigmoid(c)
        a = c * sg
        das = [None] * (3 * H)
        for h in range(H):
            xq = a[:, h * GDN_DIM:(h + 1) * GDN_DIM]
            xk = a[:, GDN_WIDTH + h * GDN_DIM:GDN_WIDTH + (h + 1) * GDN_DIM]
            das[h] = _l2n_bwd(dq_ref[h], xq, GDN_QSCALE)
            das[H + h] = _l2n_bwd(dk_ref[h], xk, 1.0)
            das[2 * H + h] = dv_ref[h]
        dc = jnp.concatenate(das, axis=-1) * (sg * (1.0 + c * (1.0 - sg)))
        dc_ref[...] = dc
        dcw_ref[...] += jnp.concatenate(
            [jnp.sum(dc * sh[CONV_W - 1 - t], axis=0, keepdims=True) for t in range(CONV_W)], axis=0)
        lane = lax.broadcasted_iota(jnp.int32, (tm, LANES), 1)
        ric = lax.broadcasted_iota(jnp.int32, (tm, LANES), 0) % CHUNK
        dG = jnp.zeros((tm, LANES), F32)
        for h in range(H):
            t = dgb_ref[h]
            dG = dG + jnp.where(lane == h, _pick_lane(t, lane, 0), 0.0) \
                    + jnp.where(lane == h + H, _pick_lane(t, lane, 1), 0.0)
        is_g = lane < H
        dg = jnp.where(is_g, _chunk_rev_cumsum(jnp.where(is_g, dG, 0.0), ric), 0.0)
        gab = gab_ref[...]
        g, beta = _gate_values(gab, alog_ref[...], dt_ref[...], lane)
        dga = jnp.where(is_g, dg * (-jnp.exp(alog_ref[...])) * _sigmoid(gab + dt_ref[...]), 0.0)
        dgb = jnp.where(is_g, 0.0, dG) * beta * (1.0 - beta)
        dgab_ref[...] = (dga + dgb).astype(MXU_DTYPE)
        dalog_ref[...] += jnp.sum(dg * g, axis=0, keepdims=True)
        ddt_ref[...] += jnp.sum(dga, axis=0, keepdims=True)

    hspec = pl.BlockSpec((H, tm, GDN_DIM), lambda i: (0, i, 0))
    vec = pl.BlockSpec((1, LANES), lambda i: (0, 0))
    return pl.pallas_call(
        body, grid=(T // tm,), name="gdn_pre_bwd",
        in_specs=[pl.BlockSpec((tm, C3), lambda i: (i, 0)),
                  pl.BlockSpec((SUBLANES, C3), lambda i: (jnp.maximum(i * (tm // SUBLANES) - 1, 0), 0)),
                  pl.BlockSpec((tm, LANES), lambda i: (i, P_GAB // LANES)),
                  pl.BlockSpec((CONV_W, C3), lambda i: (0, 0)), vec, vec, hspec, hspec, hspec, hspec],
        out_specs=[pl.BlockSpec((tm, C3), lambda i: (i, 0)), pl.BlockSpec((tm, LANES), lambda i: (i, 0)),
                   pl.BlockSpec((CONV_W, C3), lambda i: (0, 0)), vec, vec],
        out_shape=[SDS((T, C3), F32), SDS((T, LANES), MXU_DTYPE), SDS((CONV_W, C3), F32),
                   SDS((1, LANES), F32), SDS((1, LANES), F32)],
        compiler_params=_params(("arbitrary",)),
    )(proj, proj, proj, conv_w, alog_l, dt_l, dq4, dk4, dv4, dgb4)


def _mla_pre_bwd(proj, cosf, sinf, w_qln, w_kvln, w_uq_p, w_ukv, qnw, knw, dq4, dk4, dv4, transfer=None):
    T = proj.shape[0]
    tm = min(256, T)
    H = MLA_HEADS

    def body(ql_ref, kvl_ref, kpe_ref, cos_ref, sin_ref, wq_ref, wkv_ref, uq_ref, ukv_ref, qnw_ref, knw_ref,
             dq_ref, dk_ref, dv_ref,
             dql_ref, dkvl_ref, dkpe_ref, dqraw_ref, dkvraw_ref, qn_ref, kvn_ref, dwq_ref, dwkv_ref, dqnw_ref, dknw_ref):
        @pl.when(pl.program_id(0) == 0)
        def _():
            for r in (dwq_ref, dwkv_ref, dqnw_ref, dknw_ref):
                r[...] = jnp.zeros_like(r)

        cos, sin = cos_ref[...], sin_ref[...]
        qnw_, knw_ = qnw_ref[...], knw_ref[...]
        ql, kvl = ql_ref[...], kvl_ref[...]
        kpe_raw = kpe_ref[...][:, :ROPE]
        rms = functools.partial(_rms, on_mxu=True)
        rms_bwd = functools.partial(_rms_bwd, on_mxu=True)
        qn, rq = rms(ql, wq_ref[...])
        kvn, rkv = rms(kvl, wkv_ref[...])
        qn_ref[...] = qn.astype(MXU_DTYPE)
        kvn_ref[...] = kvn.astype(MXU_DTYPE)
        qraw = _mm(qn, uq_ref[...])
        kvraw = _mm(kvn, ukv_ref[...])
        dq_nope, dq_pe, dkv_parts = [], [], []
        dqnw_n = jnp.zeros((1, NOPE), F32)
        dqnw_p = jnp.zeros((1, ROPE), F32)
        dknw_n = jnp.zeros((1, NOPE), F32)
        dkpe = jnp.zeros((tm, ROPE), F32)
        for h in range(H):
            dq = dq_ref[h] * ATT_SCALE
            x = qraw[:, h * NOPE:(h + 1) * NOPE]
            dx, dw = rms_bwd(dq[:, :NOPE], x, qnw_[:, :NOPE], rms(x, qnw_[:, :NOPE])[1])
            dq_nope.append(dx)
            dqnw_n = dqnw_n + dw
            x = qraw[:, H * NOPE + h * ROPE:H * NOPE + (h + 1) * ROPE]
            dx, dw = rms_bwd(_rope_bwd(dq[:, NOPE:], cos, sin), x, qnw_[:, NOPE:], rms(x, qnw_[:, NOPE:])[1])
            dq_pe.append(dx)
            dqnw_p = dqnw_p + dw
            dk = dk_ref[h]
            x = kvraw[:, h * 256:h * 256 + NOPE]
            dx, dw = rms_bwd(dk[:, :NOPE], x, knw_[:, :NOPE], rms(x, knw_[:, :NOPE])[1])
            dknw_n = dknw_n + dw
            dkpe = dkpe + dk[:, NOPE:]
            dkv_parts += [dx, dv_ref[h]]
        dx, dknw_p = rms_bwd(_rope_bwd(dkpe, cos, sin), kpe_raw, knw_[:, NOPE:], rms(kpe_raw, knw_[:, NOPE:])[1])
        dkpe_ref[...] = jnp.concatenate([dx, jnp.zeros((tm, LANES - ROPE), F32)], axis=-1).astype(MXU_DTYPE)
        dqraw = jnp.concatenate(dq_nope + dq_pe, axis=-1).astype(MXU_DTYPE)
        dkvraw = jnp.concatenate(dkv_parts, axis=-1).astype(MXU_DTYPE)
        dqraw_ref[...] = dqraw
        dkvraw_ref[...] = dkvraw
        dx, dw = rms_bwd(_mm_nt(dqraw, uq_ref[...]), ql, wq_ref[...], rq)
        dql_ref[...] = dx.astype(MXU_DTYPE)
        dwq_ref[...] += dw
        dx, dw = rms_bwd(_mm_nt(dkvraw, ukv_ref[...]), kvl, wkv_ref[...], rkv)
        dkvl_ref[...] = dx.astype(MXU_DTYPE)
        dwkv_ref[...] += dw
        dqnw_ref[...] += jnp.concatenate([dqnw_n, dqnw_p], axis=-1)
        dknw_ref[...] += jnp.concatenate([dknw_n, dknw_p], axis=-1)

    full = lambda a: pl.BlockSpec(a.shape, lambda i: (0,) * a.ndim)
    rows = lambda n: pl.BlockSpec((tm, n), lambda i: (i, 0))
    const = lambda n: pl.BlockSpec((1, n), lambda i: (0, 0))
    NQ, NKV = w_uq_p.shape[1], w_ukv.shape[1]
    return _call_beside(
        body, transfer, grid=(T // tm,), name="mla_pre_bwd", scratch_shapes=[], semantics=("arbitrary",),
        args=(proj, proj, proj, cosf, sinf, w_qln, w_kvln, w_uq_p, w_ukv, qnw, knw, dq4, dk4, dv4),
        in_specs=[pl.BlockSpec((tm, 256), lambda i: (i, P_QLAT // 256)),
                  pl.BlockSpec((tm, 256), lambda i: (i, P_KVLAT // 256)),
                  pl.BlockSpec((tm, 128), lambda i: (i, P_KPE // 128)),
                  rows(ROPE), rows(ROPE),
                  full(w_qln), full(w_kvln), full(w_uq_p), full(w_ukv), full(qnw), full(knw),
                  pl.BlockSpec((H, tm, QK_DIM), lambda i: (0, i, 0)),
                  pl.BlockSpec((H, tm, QK_DIM), lambda i: (0, i, 0)),
                  pl.BlockSpec((H, tm, V_DIM), lambda i: (0, i, 0))],
        out_specs=[rows(Q_LORA), rows(KV_LORA), rows(LANES), rows(NQ), rows(NKV), rows(Q_LORA), rows(KV_LORA),
                   const(Q_LORA), const(KV_LORA), const(QK_DIM), const(QK_DIM)],
        out_shape=[SDS((T, Q_LORA), MXU_DTYPE), SDS((T, KV_LORA), MXU_DTYPE), SDS((T, LANES), MXU_DTYPE),
                   SDS((T, NQ), MXU_DTYPE), SDS((T, NKV), MXU_DTYPE),
                   SDS((T, Q_LORA), MXU_DTYPE), SDS((T, KV_LORA), MXU_DTYPE),
                   SDS((1, Q_LORA), F32), SDS((1, KV_LORA), F32), SDS((1, QK_DIM), F32), SDS((1, QK_DIM), F32)])


def _in_proj_bwd(dc, conv_w, dgz, dql, dkvl, dkpe, dgab, w_in_p, dh, x2, w_an, S):
    T, D = x2.shape
    N = w_in_p.shape[1]
    C3 = dc.shape[1]
    tm = min(512, S)
    assert S % tm == 0 and T % tm == 0, "a token tile must not straddle two sequences"
    tiles_per_seq = S // tm
    nblk = T // SUBLANES

    def body(dc_ref, nxt_ref, cw_ref, b_ref, c_ref, d_ref, e_ref, f_ref, w_ref, dh_ref, x_ref, wn_ref,
             dx_ref, dp_ref, dwn_ref):
        i = pl.program_id(0)

        @pl.when(i == 0)
        def _():
            dwn_ref[...] = jnp.zeros_like(dwn_ref)

        nxt = jnp.where(i % tiles_per_seq == tiles_per_seq - 1, 0.0, nxt_ref[...])
        dcv, cw = dc_ref[...], cw_ref[...]
        du = cw[3:4] * dcv
        for j in range(1, CONV_W):
            du = du + cw[3 - j:4 - j] * _shift_up(dcv, nxt, j)
        dp = jnp.concatenate([du.astype(MXU_DTYPE), b_ref[...], c_ref[...], d_ref[...], e_ref[...], f_ref[...]],
                             axis=-1).astype(MXU_DTYPE)
        dp_ref[...] = dp
        x = x_ref[...]
        _, r = _rms(x, wn_ref[...])
        dx, dw = _rms_bwd(_mm_nt(dp, w_ref[...]), x, wn_ref[...], r)
        dx_ref[...] = dh_ref[...] + dx
        dwn_ref[...] += dw

    rows = lambda n: pl.BlockSpec((tm, n), lambda i: (i, 0))
    return pl.pallas_call(
        body, grid=(T // tm,), name="in_proj_bwd",
        in_specs=[rows(C3),
                  pl.BlockSpec((SUBLANES, C3), lambda i: (jnp.minimum((i + 1) * (tm // SUBLANES), nblk - 1), 0)),
                  pl.BlockSpec((CONV_W, C3), lambda i: (0, 0)),
                  rows(dgz.shape[1]), rows(dql.shape[1]), rows(dkvl.shape[1]),
                  rows(dkpe.shape[1]), rows(dgab.shape[1]),
                  pl.BlockSpec((D, N), lambda i: (0, 0)), rows(D), rows(D), pl.BlockSpec((1, D), lambda i: (0, 0))],
        out_specs=[rows(D), rows(N), pl.BlockSpec((1, D), lambda i: (0, 0))],
        out_shape=[SDS((T, D), F32), SDS((T, N), MXU_DTYPE), SDS((1, D), F32)],
        compiler_params=_params(("arbitrary",)),
    )(dc, dc, conv_w, dgz, dql, dkvl, dkpe, dgab, w_in_p, dh, x2, w_an)


def _relu_squared(t):
    r = jnp.maximum(t.astype(F32), 0.0)
    return (r * r).astype(MXU_DTYPE)


def _wgrad(a, b, name, column_shards=False, a_map=None):
    T, M = a.shape
    N = b.shape[1]
    tM = _divisor_tile(M, 1024)
    tN = N // N_DEV if column_shards else _divisor_tile(N, 1536)
    tk = min(T, 2048)
    nk = T // tk

    def body(a_ref, b_ref, o_ref, acc):
        k = pl.program_id(2)

        @pl.when(k == 0)
        def _():
            acc[...] = jnp.zeros_like(acc)

        acc[...] += _mm_tn(a_ref[...] if a_map is None else a_map(a_ref[...]), b_ref[...])

        @pl.when(k == nk - 1)
        def _():
            o_ref[...] = acc[...].astype(WIRE_DTYPE).reshape(o_ref.shape)

    if column_shards:
        out_spec, out_shape = pl.BlockSpec((1, tM, tN), lambda i, j, k: (j, i, 0)), SDS((N_DEV, M, tN), WIRE_DTYPE)
    else:
        out_spec, out_shape = pl.BlockSpec((tM, tN), lambda i, j, k: (i, j)), SDS((M, N), WIRE_DTYPE)
    return pl.pallas_call(
        body, grid=(M // tM, N // tN, nk), name=name,
        in_specs=[pl.BlockSpec((tk, tM), lambda i, j, k: (k, i)), pl.BlockSpec((tk, tN), lambda i, j, k: (k, j))],
        out_specs=out_spec, out_shape=out_shape,
        scratch_shapes=[pltpu.VMEM((tM, tN), F32)],
        compiler_params=_params(("arbitrary", "arbitrary", "arbitrary")),
    )(a, b)


WGRAD_RING_SLOTS = 3


def _wgrad_stream(a, b, name, tile, stream_a=False, column_shards=False, a_map=None):
    T, M = a.shape
    N = b.shape[1]
    n = (M if stream_a else N) // tile
    tk = min(T, 2048)
    assert (M if stream_a else N) % tile == 0 and T % tk == 0 and (a_map is None or stream_a)
    held, src = (b, a) if stream_a else (a, b)

    def body(held_ref, src_ref, o_ref, ring, sems):
        s = pl.program_id(0)

        def fetch(t, slot):
            cols = pl.ds(pl.multiple_of(t * tile, tile), tile)
            return pltpu.make_async_copy(src_ref.at[:, cols], ring.at[slot], sems.at[slot])

        @pl.when(s == 0)
        def _():
            for t in range(min(2, n)):
                fetch(t, t).start()

        @pl.when(s + 2 < n)
        def _():
            fetch(s + 2, (s + 2) % WGRAD_RING_SLOTS).start()

        slot = s % WGRAD_RING_SLOTS
        fetch(s, slot).wait()
        acc = None
        for k in range(T // tk):
            rows = pl.ds(k * tk, tk)
            if stream_a:
                at = ring[slot, rows, :]
                part = _mm_tn(at if a_map is None else a_map(at), held_ref[rows, :])
            else:
                part = _mm_tn(held_ref[rows, :], ring[slot, rows, :])
            acc = part if acc is None else acc + part
        o_ref[...] = acc.astype(WIRE_DTYPE).reshape(o_ref.shape)

    if stream_a:
        out_spec, out_shape = pl.BlockSpec((tile, N), lambda s: (s, 0)), SDS((M, N), WIRE_DTYPE)
    elif column_shards:
        assert tile == N // N_DEV
        out_spec, out_shape = pl.BlockSpec((1, M, tile), lambda s: (s, 0, 0)), SDS((N_DEV, M, tile), WIRE_DTYPE)
    else:
        out_spec, out_shape = pl.BlockSpec((M, tile), lambda s: (0, s)), SDS((M, N), WIRE_DTYPE)
    return pl.pallas_call(
        body, grid=(n,), name=name,
        in_specs=[pl.BlockSpec(held.shape, lambda s: (0, 0)), pl.BlockSpec(memory_space=pl.ANY)],
        out_specs=out_spec, out_shape=out_shape,
        scratch_shapes=[pltpu.VMEM((WGRAD_RING_SLOTS, T, tile), src.dtype),
                        pltpu.SemaphoreType.DMA((WGRAD_RING_SLOTS,))],
        compiler_params=_params(("arbitrary",)),
    )(held, src)


def _adamw(g, w, m, v):
    m = ADAM_B1 * m + (1.0 - ADAM_B1) * g
    v = ADAM_B2 * v + (1.0 - ADAM_B2) * jnp.square(g)
    m_hat = m / (1.0 - ADAM_B1 ** ADAM_STEP)
    v_hat = v / (1.0 - ADAM_B2 ** ADAM_STEP)
    return -ADAM_LR * (m_hat / (jnp.sqrt(v_hat) + ADAM_EPS) + ADAM_WD * w), m, v


def _reduce_adamw(parts, w, m, v, name):
    R, C = w.shape
    slots, Rp, Cp = parts.shape
    tr = min(R, 256)
    tp = tr if Rp == R else Rp

    def body(p_ref, w_ref, m_ref, v_ref, g_ref, d_ref, nm_ref, nv_ref):
        g = p_ref[0].astype(F32)
        for s in range(1, slots):
            g = g + p_ref[s].astype(F32)
        g = g[:tr, :C]
        g_ref[...] = g
        d_ref[...], nm_ref[...], nv_ref[...] = _adamw(g, w_ref[...], m_ref[...], v_ref[...])

    spec = pl.BlockSpec((tr, C), lambda i: (i, 0))
    return pl.pallas_call(
        body, grid=(R // tr,), name=name,
        in_specs=[pl.BlockSpec((slots, tp, Cp), lambda i: (0, i, 0)), spec, spec, spec],
        out_specs=[spec] * 4, out_shape=[SDS((R, C), F32)] * 4,
        compiler_params=_params(("arbitrary",)),
    )(parts, w, m, v)


SMALL_ROWS, SMALL_COLS = 16, 1024
SMALL_LAYOUT = (
    ("attn_norm_w", 0, 1, 1024, 1024), ("mlp_norm_w", 1, 1, 1024, 1024), ("q_lat_norm_w", 2, 1, 256, 256),
    ("kv_lat_norm_w", 3, 1, 256, 256), ("q_norm_w", 4, 1, 192, 192), ("k_norm_w", 5, 1, 192, 192),
    ("mla_out_norm_w", 6, 4, 128, 128), ("a_log", 10, 1, 128, 4), ("dt_bias", 11, 1, 128, 4),
    ("gdn_norm_w", 12, 1, 128, 128))
LOSS_ENTRY = ("loss", 13, 1, 128, 128)


def _adamw_replicated(parts, ws, ms, vs):
    n = len(SMALL_LAYOUT)

    def body(*refs):
        p_ref = refs[0]
        w_refs, m_refs, v_refs = refs[1:1 + n], refs[1 + n:1 + 2 * n], refs[1 + 2 * n:1 + 3 * n]
        outs = refs[1 + 3 * n:]
        s = p_ref[0]
        for d in range(1, N_DEV):
            s = s + p_ref[d]
        for i, (_, r0, nr, _, pw) in enumerate(SMALL_LAYOUT):
            g = s[r0:r0 + nr, :pw]
            outs[i][...] = g
            outs[n + i][...], outs[2 * n + i][...], outs[3 * n + i][...] = _adamw(
                g, w_refs[i][...], m_refs[i][...], v_refs[i][...])
        _, r0, nr, gw, _ = LOSS_ENTRY
        outs[4 * n][...] = s[r0:r0 + nr, :gw]

    res = pl.pallas_call(
        body, name="adamw_replicated",
        out_shape=[SDS(w.shape, F32) for w in ws] * 4 + [SDS((1, LANES), F32)],
        compiler_params=_params(),
    )(parts, *ws, *ms, *vs)
    return [res[k * n:(k + 1) * n] for k in range(4)], res[4 * n][0, 0]


COPIES_PER_ARRAY = N_DEV - 1


def _two_level_gather(srcs, outs, send_sems, recv_sems, local_sems=None, stage="all"):
    mx, my, mc = lax.axis_index("x"), lax.axis_index("y"), lax.axis_index("c")
    me, sibling = (mx, my, mc), (mx, my, 1 - mc)
    chips = [(1 - mx, my), (mx, 1 - my), (1 - mx, 1 - my)]
    arrays = range(len(srcs))

    def copy(a, k, block, to, src=None):
        px, py, pc = block
        slot = outs[a].at[4 * px + 2 * py + pc]
        sem = a * COPIES_PER_ARRAY + k
        return pltpu.make_async_remote_copy(
            src_ref=slot if src is None else src, dst_ref=slot,
            send_sem=send_sems.at[sem], recv_sem=recv_sems.at[sem], device_id=to, device_id_type=MESH_ID)

    mine = [] if local_sems is None else [
        pltpu.make_async_copy(srcs[a], outs[a].at[4 * mx + 2 * my + mc], local_sems.at[a]) for a in arrays]
    first = []
    for a in arrays:
        first.append(copy(a, 0, me, sibling, src=srcs[a]))
        first += [copy(a, 1 + j, me, (*chip, mc), src=srcs[a]) for j, chip in enumerate(chips)]
    forwards = [copy(a, 4 + j, (*chip, mc), sibling) for j, chip in enumerate(chips) for a in arrays]
    if stage in ("all", "start"):
        for cp in mine + first:
            cp.start()
    if stage in ("all", "forward"):
        for j, chip in enumerate(chips):
            for a in arrays:
                copy(a, 1 + j, (*chip, mc), me).wait_recv()
                forwards[j * len(srcs) + a].start()
    if stage in ("all", "finish"):
        for a in arrays:
            copy(a, 0, sibling, me).wait_recv()
        for j, chip in enumerate(chips):
            for a in arrays:
                copy(a, 4 + j, (*chip, 1 - mc), me).wait_recv()
        for cp in first + forwards:
            cp.wait_send()
        for cp in mine:
            cp.wait()


def _comm_scratch(n):
    return [pltpu.SemaphoreType.DMA((n * COPIES_PER_ARRAY,)), pltpu.SemaphoreType.DMA((n * COPIES_PER_ARRAY,)),
            pltpu.SemaphoreType.DMA((n,))]


def _any_specs(n):
    return [pl.BlockSpec(memory_space=pl.ANY)] * n


def _gather_weights(shards):
    n = len(shards)

    def body(*refs):
        _two_level_gather(refs[:n], refs[n:2 * n], *refs[2 * n:])

    return pl.pallas_call(
        body, name="gather_weights",
        out_shape=[SDS((N_DEV,) + s.shape, s.dtype) for s in shards],
        in_specs=_any_specs(n), out_specs=_any_specs(n), scratch_shapes=_comm_scratch(n),
    )(*shards)


def _gather_small_grads(gs, loss_lanes):
    gs = list(gs) + [loss_lanes]
    n = len(gs)

    def body(*refs):
        g_refs, out_ref = refs[:n], refs[n]
        tile, send_sems, recv_sems = refs[n + 1:]
        tile[...] = jnp.zeros_like(tile)
        for (_, r0, nr, gw, _), g in zip(SMALL_LAYOUT + (LOSS_ENTRY,), g_refs):
            tile[r0:r0 + nr, 0:gw] = g[...]
        me = 4 * lax.axis_index("x") + 2 * lax.axis_index("y") + lax.axis_index("c")
        out_ref[me] = tile[...]
        _two_level_gather([tile], [out_ref], send_sems, recv_sems)

    return pl.pallas_call(
        body, name="gather_small_grads",
        out_shape=SDS((N_DEV, SMALL_ROWS, SMALL_COLS), F32),
        in_specs=[pl.BlockSpec(memory_space=pltpu.VMEM)] * n,
        out_specs=pl.BlockSpec(memory_space=pltpu.VMEM),
        scratch_shapes=[pltpu.VMEM((SMALL_ROWS, SMALL_COLS), F32),
                        pltpu.SemaphoreType.DMA((COPIES_PER_ARRAY,)), pltpu.SemaphoreType.DMA((COPIES_PER_ARRAY,))],
    )(*gs)


def _exchange_grads_two_level(big, small):
    _, R, C = big.shape
    chip_flips = ((1, 0), (0, 1), (1, 1))

    def body(big_ref, small_ref, out_ref, small_out, mine_v, sib_v, pre_v, d2d_send, d2d_recv, ici_send, ici_recv,
             local_sems, s_send, s_recv, s_local):
        mx, my, mc = lax.axis_index("x"), lax.axis_index("y"), lax.axis_index("c")
        sibling = (mx, my, 1 - mc)
        chips = [(px, py) for px in range(2) for py in range(2)]
        _exchange([small_ref], [small_out], s_send, s_recv, s_local, stage="start")
        own = [pltpu.make_async_copy(big_ref.at[4 * px + 2 * py + mc], mine_v.at[q], local_sems.at[q])
               for q, (px, py) in enumerate(chips)]
        d2d = [pltpu.make_async_remote_copy(
            src_ref=big_ref.at[4 * px + 2 * py + (1 - mc)], dst_ref=sib_v.at[q], send_sem=d2d_send.at[q],
            recv_sem=d2d_recv.at[q], device_id=sibling, device_id_type=MESH_ID) for q, (px, py) in enumerate(chips)]
        for cp in own + d2d:
            cp.start()
        for cp in own + d2d:
            cp.wait()
        for q in range(4):
            pre_v[q] = (mine_v[q].astype(F32) + sib_v[q].astype(F32)).astype(pre_v.dtype)
        ici = []
        for k, (fx, fy) in enumerate(chip_flips):
            px = 1 - mx if fx else mx
            py = 1 - my if fy else my
            ici.append(pltpu.make_async_remote_copy(
                src_ref=pre_v.at[2 * px + py], dst_ref=out_ref.at[k], send_sem=ici_send.at[k],
                recv_sem=ici_recv.at[k], device_id=(px, py, mc), device_id_type=MESH_ID))
        keep = pltpu.make_async_copy(pre_v.at[2 * mx + my], out_ref.at[3], local_sems.at[4])
        for cp in ici + [keep]:
            cp.start()
        for cp in ici + [keep]:
            cp.wait()
        _exchange([small_ref], [small_out], s_send, s_recv, s_local, stage="finish")

    dma = pltpu.SemaphoreType.DMA
    return pl.pallas_call(
        body, name="exchange_grads",
        out_shape=[SDS((4, R, C), big.dtype), SDS(small.shape, small.dtype)],
        in_specs=_any_specs(2), out_specs=_any_specs(2),
        scratch_shapes=[pltpu.VMEM((4, R, C), big.dtype)] * 3 + [dma((4,)), dma((4,)), dma((3,)), dma((3,)), dma((5,))]
                       + _comm_scratch(1),
        compiler_params=_params(),
    )(big, small)


class _Transfer:
    def __init__(self, kind, arrays):
        self.kind, self.arrays, self.n = kind, list(arrays), len(arrays)

    def out_shapes(self):
        if self.kind == "gather":
            return [SDS((N_DEV,) + a.shape, a.dtype) for a in self.arrays]
        return [SDS(a.shape, a.dtype) for a in self.arrays]

    def run(self, srcs, outs, sems, stage):
        fn = _two_level_gather if self.kind == "gather" else _exchange
        fn(srcs, outs, *sems, stage=stage)


def _call_beside(body, transfer, *, grid, in_specs, out_specs, out_shape, scratch_shapes, name, semantics, args):
    if transfer is None:
        res = pl.pallas_call(body, grid=grid, in_specs=in_specs, out_specs=out_specs, out_shape=out_shape,
                             scratch_shapes=scratch_shapes, name=name, compiler_params=_params(semantics))(*args)
        return list(res), []
    n_in, n_out, n_s, n = len(in_specs), len(out_specs), len(scratch_shapes), transfer.n
    total = functools.reduce(lambda a, b: a * b, grid, 1)

    def wrapped(*refs):
        ins, refs = refs[:n_in], refs[n_in:]
        t_in, refs = refs[:n], refs[n:]
        outs, refs = refs[:n_out], refs[n_out:]
        t_out, refs = refs[:n], refs[n:]
        scratch, sems = refs[:n_s], refs[n_s:]
        first = functools.reduce(jnp.logical_and, [pl.program_id(i) == 0 for i in range(len(grid))])
        last = functools.reduce(jnp.logical_and, [pl.program_id(i) == g - 1 for i, g in enumerate(grid)])

        @pl.when(first)
        def _():
            transfer.run(t_in, t_out, sems, "start")

        step = functools.reduce(lambda acc, ig: acc * ig[1] + pl.program_id(ig[0]), enumerate(grid), 0)

        @pl.when(step == (3 * total) // 4)
        def _():
            transfer.run(t_in, t_out, sems, "forward")

        body(*ins, *outs, *scratch)

        @pl.when(last)
        def _():
            transfer.run(t_in, t_out, sems, "finish")

    res = pl.pallas_call(
        wrapped, grid=grid, in_specs=list(in_specs) + _any_specs(n), out_specs=list(out_specs) + _any_specs(n),
        out_shape=list(out_shape) + transfer.out_shapes(), scratch_shapes=list(scratch_shapes) + _comm_scratch(n),
        name=name, compiler_params=_params(semantics))(*args, *transfer.arrays)
    return list(res[:n_out]), list(res[n_out:])


EXCHANGE_FLIPS = ((0, 0, 1), (1, 0, 0), (0, 1, 0), (1, 1, 0), (1, 0, 1), (0, 1, 1), (1, 1, 1))


def _exchange(srcs, outs, send_sems, recv_sems, local_sems, stage="all"):
    mx, my, mc = lax.axis_index("x"), lax.axis_index("y"), lax.axis_index("c")
    arrays = range(len(srcs))
    copies = [pltpu.make_async_copy(srcs[a].at[4 * mx + 2 * my + mc], outs[a].at[N_DEV - 1], local_sems.at[a])
              for a in arrays]
    for k, (fx, fy, fc) in enumerate(EXCHANGE_FLIPS):
        px = 1 - mx if fx else mx
        py = 1 - my if fy else my
        pc = 1 - mc if fc else mc
        for a in arrays:
            sem = a * COPIES_PER_ARRAY + k
            copies.append(pltpu.make_async_remote_copy(
                src_ref=srcs[a].at[4 * px + 2 * py + pc], dst_ref=outs[a].at[k],
                send_sem=send_sems.at[sem], recv_sem=recv_sems.at[sem],
                device_id=(px, py, pc), device_id_type=MESH_ID))
    if stage in ("all", "start"):
        for cp in copies:
            cp.start()
    if stage in ("all", "finish"):
        for cp in copies:
            cp.wait()


def _w_in_to_padded(w):
    z = lambda n: jnp.zeros((w.shape[0], n), w.dtype)
    return jnp.concatenate([w[:, O_GQKV:O_GZ], w[:, O_GZ:O_GAB], w[:, O_QLAT:O_KVLAT], w[:, O_KVLAT:O_KPE],
                            w[:, O_KPE:O_GQKV], z(P_GAB - P_KPE - ROPE), w[:, O_GAB:O_END],
                            z(P_WIDTH - P_GAB - (O_END - O_GAB))], axis=1)


def _w_in_from_padded(wp):
    return jnp.concatenate([wp[:, P_QLAT:P_QLAT + 256], wp[:, P_KVLAT:P_KVLAT + 256], wp[:, P_KPE:P_KPE + ROPE],
                            wp[:, P_GQKV:P_GZ], wp[:, P_GZ:P_QLAT], wp[:, P_GAB:P_GAB + (O_END - O_GAB)]], axis=1)


W_IN_SHARD_COLS = (O_END - O_QLAT) // N_DEV


def _w_in_shards_to_padded(stack):
    _, R, Cw = stack.shape
    tr = min(R, 256)

    def body(s_ref, o_ref):
        full = jnp.concatenate([s_ref[d].astype(F32)[:, :W_IN_SHARD_COLS] for d in range(N_DEV)], axis=-1)
        o_ref[...] = _w_in_to_padded(full).astype(o_ref.dtype)

    return pl.pallas_call(
        body, grid=(R // tr,), name="w_in_to_padded",
        in_specs=[pl.BlockSpec((N_DEV, tr, Cw), lambda i: (0, i, 0))],
        out_specs=pl.BlockSpec((tr, P_WIDTH), lambda i: (i, 0)),
        out_shape=SDS((R, P_WIDTH), stack.dtype), compiler_params=_params(("arbitrary",)),
    )(stack)


def _w_in_padded_to_slabs(gp, wire_cols):
    R = gp.shape[0]
    tr = min(R, 256)

    def body(g_ref, o_ref):
        orig = _w_in_from_padded(g_ref[...].astype(F32))
        for d in range(N_DEV):
            piece = orig[:, d * W_IN_SHARD_COLS:(d + 1) * W_IN_SHARD_COLS]
            o_ref[d] = _pad2(piece, tr, wire_cols).astype(o_ref.dtype)

    return pl.pallas_call(
        body, grid=(R // tr,), name="w_in_to_slabs",
        in_specs=[pl.BlockSpec((tr, P_WIDTH), lambda i: (i, 0))],
        out_specs=pl.BlockSpec((N_DEV, tr, wire_cols), lambda i: (0, i, 0)),
        out_shape=SDS((N_DEV, R, wire_cols), gp.dtype), compiler_params=_params(("arbitrary",)),
    )(gp)


def _w_uq_to_headsplit(w):
    w3 = w.reshape(w.shape[0], MLA_HEADS, QK_DIM)
    return jnp.concatenate([w3[:, :, :NOPE].reshape(w.shape[0], -1), w3[:, :, NOPE:].reshape(w.shape[0], -1)], axis=1)


def _w_uq_from_headsplit(wp):
    n = wp[:, :MLA_HEADS * NOPE].reshape(wp.shape[0], MLA_HEADS, NOPE)
    p = wp[:, MLA_HEADS * NOPE:].reshape(wp.shape[0], MLA_HEADS, ROPE)
    return jnp.concatenate([n, p], axis=2).reshape(wp.shape[0], -1)


def _lane_vec(v4):
    return jnp.pad(v4.reshape(1, -1), ((0, 0), (0, LANES - v4.shape[-1])))


def _local_step(x, positions, target, attn_norm_w, w_in, q_lat_norm_w, w_uq, kv_lat_norm_w, w_ukv, q_norm_w,
                k_norm_w, mla_out_norm_w, conv_w, a_log, dt_bias, gdn_norm_w, w_out, mlp_norm_w, w_up, w_down,
                late_shards=None, exchange=False):
    B, S, D = x.shape
    T = B * S
    x2 = x.reshape(T, D)
    t2 = target.reshape(T, D)
    half = ROPE // 2
    inv_freq = ROPE_THETA ** (-jnp.arange(half, dtype=F32) / half)
    ang = positions.reshape(T, 1).astype(F32) * inv_freq
    cosf = jnp.concatenate([jnp.cos(ang)] * 2, axis=-1)
    sinf = jnp.concatenate([jnp.sin(ang)] * 2, axis=-1)
    w_in_p = w_in
    w_uq_p = _w_uq_to_headsplit(w_uq)
    alog_l, dt_l = _lane_vec(a_log), _lane_vec(dt_bias)
    w_an, w_qln, w_kvln, qnw, knw, w_mn, gdn_w = (
        attn_norm_w, q_lat_norm_w, kv_lat_norm_w, q_norm_w, k_norm_w, mlp_norm_w, gdn_norm_w)

    proj, xn, qg, kg, vg, gates = _in_proj(x2, w_an, w_in_p, conv_w, alog_l, dt_l, S)
    def gathering(shards):
        return None if late_shards is None else _Transfer("gather", shards)

    (q4, k4, v4), late = _mla_pre(proj, cosf, sinf, w_qln, w_kvln, w_uq_p, w_ukv, qnw, knw,
                                  gathering(late_shards and late_shards[:1]))
    if late:
        w_out = late[0].reshape(-1, D)
    (o_mla, lse), late = _attn_fwd(q4, k4, v4, B, S, gathering(late_shards and late_shards[2:]))
    if late:
        w_down = late[0].reshape(-1, D)
    (o_gdn, states, ainv, u4, w4), late = _gdn_fwd(qg, kg, vg, gates, B, S,
                                                   gathering(late_shards and late_shards[1:2]))
    if late:
        w_up = late[0]
    h2, mix = _mix_out(o_mla, o_gdn, proj, x2, mla_out_norm_w, gdn_w, w_out)
    up, hn, dy, sq, dyb = _mlp_fwd(h2, w_mn, w_up, w_down, t2)
    loss = (0.5 / D) * jnp.sum(sq[:, 0, 0])

    first = ("w_down",)
    second = ("w_up",)
    third = ("w_out", "w_uq", "w_ukv")
    mats = dict(w_down=_wgrad_stream(up, dyb, "wgrad_down", 512, stream_a=True, a_map=_relu_squared))

    def sending(names):
        return _Transfer("exchange", [_slabs(n, mats[n]) for n in names]) if exchange else None

    (dh, dhb, dup, d_mlp_norm), got = _mlp_bwd(dy, dyb, up, h2, w_mn, w_up, w_down, sending(first))
    mats.update(zip(first, got))
    mats.update(w_up=_wgrad_stream(hn, dup, "wgrad_up", D_FF // N_DEV, column_shards=True))
    do_mla, do_gdn, dz, d_mla_w, d_gdn_w, delta = _mix_bwd(dhb, o_mla, o_gdn, proj, mla_out_norm_w, gdn_w, w_out)
    mats.update(w_out=_wgrad(mix, dhb, "wgrad_out"))
    (dq4, dk4, dv4), got = _attn_bwd(q4, k4, v4, do_mla, delta, lse, B, S, sending(second))
    mats.update(zip(second, got))
    (dql, dkvl, dkpe, dqraw, dkvraw, qn, kvn, d_wqln, d_wkvln, d_qnw, d_knw), _ = _mla_pre_bwd(
        proj, cosf, sinf, w_qln, w_kvln, w_uq_p, w_ukv, qnw, knw, dq4, dk4, dv4)
    mats.update(w_uq=_wgrad(qn, dqraw, "wgrad_uq"), w_ukv=_wgrad(kvn, dkvraw, "wgrad_ukv"))
    (dqg, dkg, dvg, dgb4), got = _gdn_bwd(qg, kg, vg, gates, states, ainv, u4, w4, do_gdn, B, S, sending(third))
    mats.update(zip(third, got))
    dc, dgab, g_conv, d_alog, d_dt = _gdn_pre_bwd(proj, conv_w, alog_l, dt_l, dqg, dkg, dvg, dgb4, S)
    grad_x2, dproj, d_attn_norm = _in_proj_bwd(dc, conv_w, dz, dql, dkvl, dkpe, dgab, w_in_p, dh, x2, w_an, S)
    mats.update(w_in=_wgrad_stream(xn, dproj, "wgrad_in", 256), conv_w=g_conv)
    if exchange:
        last = ("w_in", "conv_w")
        mats.update(zip(last, _exchange_grads_two_level(*[_slabs(n, mats[n]) for n in last])))
    small = dict(attn_norm_w=d_attn_norm, mlp_norm_w=d_mlp_norm, q_lat_norm_w=d_wqln, kv_lat_norm_w=d_wkvln,
                 q_norm_w=d_qnw, k_norm_w=d_knw, mla_out_norm_w=d_mla_w, a_log=d_alog, dt_bias=d_dt,
                 gdn_norm_w=d_gdn_w)
    return loss, grad_x2.reshape(B, S, D), mats, [small[n] for n, *_ in SMALL_LAYOUT]


BIG = ("w_in", "w_uq", "w_ukv", "conv_w", "w_out", "w_up", "w_down")
ALL_W = ("attn_norm_w", "w_in", "q_lat_norm_w", "w_uq", "kv_lat_norm_w", "w_ukv", "q_norm_w", "k_norm_w",
         "mla_out_norm_w", "conv_w", "a_log", "dt_bias", "gdn_norm_w", "w_out", "mlp_norm_w", "w_up", "w_down")
WIRE_SHAPE = {"w_in": (1024, 384), "w_uq": (256, 128), "conv_w": (16, 256)}


def _pad2(a, rows, cols):
    return jnp.pad(a, [(0, 0)] * (a.ndim - 2) + [(0, rows - a.shape[-2]), (0, cols - a.shape[-1])])


def _cols_to_full(stack, cols):
    return jnp.moveaxis(stack[:, :, :cols], 0, 1).reshape(stack.shape[1], N_DEV * cols)


def _full_to_cols(full, wire_cols):
    r, n = full.shape
    return _pad2(jnp.moveaxis(full.reshape(r, N_DEV, n // N_DEV), 1, 0), r, wire_cols)


def _slabs(name, g):
    if name == "w_in":
        return _w_in_padded_to_slabs(g, WIRE_SHAPE["w_in"][1])
    if name == "w_uq":
        return _full_to_cols(_w_uq_from_headsplit(g), WIRE_SHAPE["w_uq"][1])
    if name == "w_ukv":
        return _full_to_cols(g, g.shape[1] // N_DEV)
    if name == "conv_w":
        return _pad2(_full_to_cols(g.astype(WIRE_DTYPE), g.shape[1] // N_DEV), *WIRE_SHAPE["conv_w"])
    if name == "w_up":
        return g
    return g.reshape(N_DEV, -1, g.shape[-1])


def kernel(x, positions, attn_norm_w, w_in, q_lat_norm_w, w_uq, kv_lat_norm_w, w_ukv, q_norm_w, k_norm_w, mla_out_norm_w, conv_w, a_log, dt_bias, gdn_norm_w, w_out, mlp_norm_w, w_up, w_down, loss_target, m_attn_norm_w, m_w_in, m_q_lat_norm_w, m_w_uq, m_kv_lat_norm_w, m_w_ukv, m_q_norm_w, m_k_norm_w, m_mla_out_norm_w, m_conv_w, m_a_log, m_dt_bias, m_gdn_norm_w, m_w_out, m_mlp_norm_w, m_w_up, m_w_down, v_attn_norm_w, v_w_in, v_q_lat_norm_w, v_w_uq, v_kv_lat_norm_w, v_w_ukv, v_q_norm_w, v_k_norm_w, v_mla_out_norm_w, v_conv_w, v_a_log, v_dt_bias, v_gdn_norm_w, v_w_out, v_mlp_norm_w, v_w_up, v_w_down):
    env = dict(locals())
    W = {n: env[n][0] for n in ALL_W}
    Mo = {n: env["m_" + n][0] for n in ALL_W}
    Vo = {n: env["v_" + n][0] for n in ALL_W}

    two_d = lambda a: a.reshape(1, -1) if a.ndim == 1 else a
    D = x.shape[-1]

    s_in, s_uq, s_ukv, s_conv = _gather_weights([
        _pad2(W["w_in"].astype(WIRE_DTYPE), *WIRE_SHAPE["w_in"]),
        _pad2(W["w_uq"].astype(WIRE_DTYPE), *WIRE_SHAPE["w_uq"]),
        W["w_ukv"].astype(WIRE_DTYPE), _pad2(W["conv_w"], *WIRE_SHAPE["conv_w"])])
    late = [W["w_out"].astype(WIRE_DTYPE), W["w_up"].astype(WIRE_DTYPE), W["w_down"].astype(WIRE_DTYPE)]

    loss, grad_x, parts, gs = _local_step(
        x, positions, loss_target, two_d(W["attn_norm_w"]), _w_in_shards_to_padded(s_in),
        two_d(W["q_lat_norm_w"]), _cols_to_full(s_uq, W["w_uq"].shape[1]), two_d(W["kv_lat_norm_w"]),
        _cols_to_full(s_ukv, W["w_ukv"].shape[1]), two_d(W["q_norm_w"]), two_d(W["k_norm_w"]),
        W["mla_out_norm_w"], _cols_to_full(s_conv[:, :CONV_W], W["conv_w"].shape[1]), two_d(W["a_log"]),
        two_d(W["dt_bias"]), two_d(W["gdn_norm_w"]), None, two_d(W["mlp_norm_w"]), None, None,
        late_shards=late, exchange=True)
    done = {n: _reduce_adamw(parts[n], W[n], Mo[n], Vo[n], "adamw_" + n) for n in BIG}
    names = [n for n, *_ in SMALL_LAYOUT]
    tiles = _gather_small_grads(gs, jnp.full((1, LANES), loss, F32))
    small, loss = _adamw_replicated(tiles, [two_d(W[n]) for n in names], [two_d(Mo[n]) for n in names],
                                    [two_d(Vo[n]) for n in names])
    for i, n in enumerate(names):
        done[n] = [small[kind][i] for kind in range(4)]
    res = [done[n][kind].reshape(env[n].shape) for kind in range(4) for n in ALL_W]
    return (loss, grad_x, *res)
```

```python
import functools

import jax
import jax.numpy as jnp
from jax import lax
from jax.experimental import pallas as pl
from jax.experimental.pallas import tpu as pltpu

F32 = jnp.float32
MXU_DTYPE = jnp.bfloat16
WIRE_DTYPE = jnp.bfloat16
SDS = jax.ShapeDtypeStruct
HIGHEST = lax.Precision.HIGHEST
MESH_ID = pl.DeviceIdType.MESH

D_MODEL = 1024
MLA_HEADS = 4
Q_LORA = 256
KV_LORA = 256
NOPE = 128
ROPE = 64
QK_DIM = NOPE + ROPE
V_DIM = 128
ROPE_THETA = 10000.0
GDN_HEADS = 4
GDN_DIM = 128
GDN_WIDTH = GDN_HEADS * GDN_DIM
CONV_W = 4
CHUNK = 64
D_FF = 4 * D_MODEL
EPS = 1e-6
ATT_SCALE = QK_DIM ** -0.5
GDN_QSCALE = GDN_DIM ** -0.5
N_DEV = 8
ATTN_BLOCK = 512
ATTN_CHAINS = 2
MLP_FWD_SHARDS = 4
MLP_BWD_SHARDS = 4

ADAM_LR = 0.001
ADAM_B1 = 0.9
ADAM_B2 = 0.999
ADAM_EPS = 1e-08
ADAM_WD = 0.01
ADAM_STEP = 10

LANES = 128
SUBLANES = 8
VMEM_LIMIT = 60 * 1024 * 1024

P_GQKV, P_GZ, P_QLAT, P_KVLAT, P_KPE, P_GAB = 0, 1536, 2048, 2304, 2560, 2688
P_WIDTH = 2816
O_QLAT, O_KVLAT, O_KPE, O_GQKV, O_GZ, O_GAB, O_END = 0, 256, 512, 576, 2112, 2624, 2632


def _params(sem=None, vmem=VMEM_LIMIT):
    kw = dict(vmem_limit_bytes=vmem)
    if sem is not None:
        kw["dimension_semantics"] = sem
    return pltpu.CompilerParams(**kw)


def _mm(a, b):
    return jnp.dot(a.astype(MXU_DTYPE), b.astype(MXU_DTYPE), preferred_element_type=F32)


def _mm_nt(a, b):
    return lax.dot_general(a.astype(MXU_DTYPE), b.astype(MXU_DTYPE), (((1,), (1,)), ((), ())),
                           preferred_element_type=F32)


def _mm_tn(a, b):
    return lax.dot_general(a.astype(MXU_DTYPE), b.astype(MXU_DTYPE), (((0,), (0,)), ((), ())),
                           preferred_element_type=F32)


def _split(a):
    hi = a.astype(MXU_DTYPE)
    return hi, (a - hi.astype(F32)).astype(MXU_DTYPE)


def _mm_split(a, b):
    (ah, al), (bh, bl) = a, b
    dot = lambda x, y: jnp.dot(x, y, preferred_element_type=F32)
    if MXU_DTYPE == F32:
        return dot(ah, bh)
    return dot(ah, bh) + dot(ah, bl) + dot(al, bh)


def _mm_exact(a, b):
    return _mm_split(_split(a), _split(b))


def _row_sum(v, on_mxu=False):
    if not on_mxu:
        return jnp.sum(v, axis=-1, keepdims=True)
    d = v.shape[-1]
    ones = jnp.ones((d, LANES), MXU_DTYPE)
    s = sum(jnp.dot(p, ones, preferred_element_type=F32) for p in _split(v))
    return s[:, :d] if d <= LANES else jnp.tile(s, (1, d // LANES))


def _rms(x, w, on_mxu=False):
    r = lax.rsqrt(_row_sum(x * x, on_mxu) * (1.0 / x.shape[-1]) + EPS)
    return x * r * w, r


def _rms_bwd(dy, x, w, r, on_mxu=False):
    xh = x * r
    dyw = dy * w
    dx = r * (dyw - xh * (_row_sum(dyw * xh, on_mxu) * (1.0 / x.shape[-1])))
    dw = jnp.sum(dy * xh, axis=0, keepdims=True)
    return dx, dw


def _l2n(x, scale):
    return x * (lax.rsqrt(_row_sum(x * x) + EPS) * scale)


def _l2n_bwd(dy, x, scale):
    r = lax.rsqrt(_row_sum(x * x) + EPS)
    xh = x * r
    return (scale * r) * (dy - xh * _row_sum(dy * xh))


def _rot(t):
    return jnp.concatenate([-t[:, ROPE // 2:], t[:, :ROPE // 2]], axis=-1)


def _rot_t(t):
    return jnp.concatenate([t[:, ROPE // 2:], -t[:, :ROPE // 2]], axis=-1)


def _rope(t, cos, sin):
    return t * cos + _rot(t) * sin


def _rope_bwd(d, cos, sin):
    return d * cos + _rot_t(d * sin)


def _sigmoid(x):
    return jax.nn.sigmoid(x)


def _shift_down(x, halo, j):
    if j == 0:
        return x
    xr = pltpu.roll(x, j, 0)
    hr = pltpu.roll(halo, j, 0)
    row = lax.broadcasted_iota(jnp.int32, halo.shape, 0)
    top = jnp.where(row < j, hr, xr[:SUBLANES])
    return jnp.concatenate([top, xr[SUBLANES:]], axis=0)


def _shift_up(x, nxt, j):
    if j == 0:
        return x
    n = x.shape[0]
    xr = pltpu.roll(x, n - j, 0)
    nr = pltpu.roll(nxt, SUBLANES - j, 0)
    row = lax.broadcasted_iota(jnp.int32, nxt.shape, 0)
    bot = jnp.where(row >= SUBLANES - j, nr, xr[n - SUBLANES:])
    return jnp.concatenate([xr[:n - SUBLANES], bot], axis=0)


def _chunk_cumsum(y, row_in_chunk):
    s = 1
    while s < CHUNK:
        y = y + jnp.where(row_in_chunk >= s, pltpu.roll(y, s, 0), 0.0)
        s *= 2
    return y


def _chunk_rev_cumsum(y, row_in_chunk):
    n = y.shape[0]
    s = 1
    while s < CHUNK:
        y = y + jnp.where(row_in_chunk + s < CHUNK, pltpu.roll(y, n - s, 0), 0.0)
        s *= 2
    return y


def _together(generators):
    alive = list(generators)
    while alive:
        nxt = []
        for g in alive:
            try:
                next(g)
                nxt.append(g)
            except StopIteration:
                pass
        alive = nxt
        yield


def _lockstep(generators):
    for _ in _together(generators):
        pass


def _pick_lane(tile, lane, idx):
    return jnp.sum(jnp.where(lane == idx, tile, 0.0), axis=-1, keepdims=True)


def _divisor_tile(n, cap, unit=LANES):
    best = unit
    t = unit
    while t <= min(n, cap):
        if n % t == 0:
            best = t
        t += unit
    return n if n <= cap else best


def _in_proj(x2, w_an, w_in_p, conv_w, alog_l, dt_l, S):
    T, D = x2.shape
    N = w_in_p.shape[1]
    tm = min(512, S)
    assert S % tm == 0 and T % tm == 0, "a token tile must not straddle two sequences"
    tiles_per_seq = S // tm
    C3 = 3 * GDN_WIDTH
    H = GDN_HEADS

    def body(x_ref, wn_ref, w_ref, cw_ref, alog_ref, dt_ref, proj_ref, xn_ref, q_out, k_out, v_out, gates_out,
             halo_s):
        xn, _ = _rms(x_ref[...], wn_ref[...])
        xn = xn.astype(MXU_DTYPE)
        xn_ref[...] = xn
        proj = jnp.dot(xn, w_ref[...], preferred_element_type=F32)
        proj_ref[...] = proj
        u = proj[:, P_GQKV:P_GQKV + C3]

        @pl.when(pl.program_id(0) == 0)
        def _():
            halo_s[...] = jnp.zeros_like(halo_s)

        halo = jnp.where(pl.program_id(0) % tiles_per_seq == 0, 0.0, halo_s[...])
        halo_s[...] = u[tm - SUBLANES:, :]
        c, _ = _conv_taps(u, halo, cw_ref[...])
        a = c * _sigmoid(c)
        for h in range(H):
            xq = a[:, h * GDN_DIM:(h + 1) * GDN_DIM]
            xk = a[:, GDN_WIDTH + h * GDN_DIM:GDN_WIDTH + (h + 1) * GDN_DIM]
            q_out[h] = _l2n(xq, GDN_QSCALE)
            k_out[h] = _l2n(xk, 1.0)
            v_out[h] = a[:, 2 * GDN_WIDTH + h * GDN_DIM:2 * GDN_WIDTH + (h + 1) * GDN_DIM]
        lane = lax.broadcasted_iota(jnp.int32, (tm, LANES), 1)
        ric = lax.broadcasted_iota(jnp.int32, (tm, LANES), 0) % CHUNK
        g, beta = _gate_values(proj[:, P_GAB:P_GAB + LANES], alog_ref[...], dt_ref[...], lane)
        gates_out[...] = _chunk_cumsum(g, ric) + beta

    hspec = pl.BlockSpec((H, tm, GDN_DIM), lambda i: (0, i, 0))
    vec = pl.BlockSpec((1, LANES), lambda i: (0, 0))
    return pl.pallas_call(
        body, grid=(T // tm,), name="in_proj",
        in_specs=[pl.BlockSpec((tm, D), lambda i: (i, 0)), pl.BlockSpec((1, D), lambda i: (0, 0)),
                  pl.BlockSpec((D, N), lambda i: (0, 0)), pl.BlockSpec((CONV_W, C3), lambda i: (0, 0)), vec, vec],
        out_specs=[pl.BlockSpec((tm, N), lambda i: (i, 0)), pl.BlockSpec((tm, D), lambda i: (i, 0)),
                   hspec, hspec, hspec, pl.BlockSpec((tm, LANES), lambda i: (i, 0))],
        out_shape=[SDS((T, N), F32), SDS((T, D), MXU_DTYPE)] + [SDS((H, T, GDN_DIM), F32)] * 3
                  + [SDS((T, LANES), F32)],
        scratch_shapes=[pltpu.VMEM((SUBLANES, C3), F32)],
        compiler_params=_params(("arbitrary",)),
    )(x2, w_an, w_in_p, conv_w, alog_l, dt_l)


def _mla_pre(proj, cosf, sinf, w_qln, w_kvln, w_uq_p, w_ukv, qnw, knw, transfer=None):
    T = proj.shape[0]
    tm = min(256, T)
    H = MLA_HEADS

    def body(ql_ref, kvl_ref, kpe_ref, cos_ref, sin_ref, wq_ref, wkv_ref, uq_ref, ukv_ref, qnw_ref, knw_ref,
             q_out, k_out, v_out):
        rms = functools.partial(_rms, on_mxu=True)
        cos, sin = cos_ref[...], sin_ref[...]
        qnw_, knw_ = qnw_ref[...], knw_ref[...]
        qn, _ = rms(ql_ref[...], wq_ref[...])
        kvn, _ = rms(kvl_ref[...], wkv_ref[...])
        qraw = _mm(qn, uq_ref[...])
        kvraw = _mm(kvn, ukv_ref[...])
        kpe = _rope(rms(kpe_ref[...][:, :ROPE], knw_[:, NOPE:])[0], cos, sin)
        for h in range(H):
            qn_h = rms(qraw[:, h * NOPE:(h + 1) * NOPE], qnw_[:, :NOPE])[0]
            qp_h = _rope(rms(qraw[:, H * NOPE + h * ROPE:H * NOPE + (h + 1) * ROPE], qnw_[:, NOPE:])[0], cos, sin)
            q_out[h] = (jnp.concatenate([qn_h, qp_h], axis=-1) * ATT_SCALE).astype(MXU_DTYPE)
            kn_h = rms(kvraw[:, h * 256:h * 256 + NOPE], knw_[:, :NOPE])[0]
            k_out[h] = jnp.concatenate([kn_h, kpe], axis=-1).astype(MXU_DTYPE)
            v_out[h] = kvraw[:, h * 256 + NOPE:(h + 1) * 256].astype(MXU_DTYPE)

    full = lambda a: pl.BlockSpec(a.shape, lambda i: (0,) * a.ndim)
    return _call_beside(
        body, transfer, grid=(T // tm,), name="mla_pre", scratch_shapes=[], semantics=("arbitrary",),
        args=(proj, proj, proj, cosf, sinf, w_qln, w_kvln, w_uq_p, w_ukv, qnw, knw),
        in_specs=[pl.BlockSpec((tm, 256), lambda i: (i, P_QLAT // 256)),
                  pl.BlockSpec((tm, 256), lambda i: (i, P_KVLAT // 256)),
                  pl.BlockSpec((tm, 128), lambda i: (i, P_KPE // 128)),
                  pl.BlockSpec((tm, ROPE), lambda i: (i, 0)), pl.BlockSpec((tm, ROPE), lambda i: (i, 0)),
                  full(w_qln), full(w_kvln), full(w_uq_p), full(w_ukv), full(qnw), full(knw)],
        out_specs=[pl.BlockSpec((H, tm, QK_DIM), lambda i: (0, i, 0)),
                   pl.BlockSpec((H, tm, QK_DIM), lambda i: (0, i, 0)),
                   pl.BlockSpec((H, tm, V_DIM), lambda i: (0, i, 0))],
        out_shape=[SDS((H, T, QK_DIM), MXU_DTYPE), SDS((H, T, QK_DIM), MXU_DTYPE), SDS((H, T, V_DIM), MXU_DTYPE)])


def _attn_fwd(q4, k4, v4, B, S, transfer=None):
    H = MLA_HEADS
    bq = min(ATTN_BLOCK, S)
    nq = S // bq
    rows = bq // ATTN_CHAINS

    def body(q_ref, k_ref, v_ref, o_ref, lse_ref):
        col = lax.broadcasted_iota(jnp.int32, (rows, bq), 1)
        row = lax.broadcasted_iota(jnp.int32, (rows, bq), 0)

        def q_step(qi, carry):
            qs = pl.multiple_of(qi * bq, bq)
            qsub = [q_ref[0, pl.ds(qs + j * rows, rows), :] for j in range(ATTN_CHAINS)]

            def k_block(ks, cs, diagonal):
                k = k_ref[0, pl.ds(ks, bq), :]
                v = v_ref[0, pl.ds(ks, bq), :]
                out = [None] * ATTN_CHAINS

                def chain(j):
                    m, l, acc = cs[j]
                    s = _mm_nt(qsub[j], k)
                    yield
                    if diagonal:
                        s = jnp.where(col <= row + j * rows, s, -jnp.inf)
                    m_new = jnp.maximum(m, jnp.max(s, axis=-1, keepdims=True))
                    p = jnp.exp(s - m_new)
                    a = jnp.exp(m - m_new)
                    l_new = a * l + jnp.sum(p, axis=-1, keepdims=True)
                    yield
                    out[j] = (m_new, l_new, a * acc + _mm(p, v))

                _lockstep([chain(j) for j in range(ATTN_CHAINS)])
                return tuple(out)

            init = tuple((jnp.full((rows, 1), -jnp.inf, F32), jnp.zeros((rows, 1), F32),
                          jnp.zeros((rows, V_DIM), F32)) for _ in range(ATTN_CHAINS))
            cs = lax.fori_loop(0, qi, lambda kj, c: k_block(pl.multiple_of(kj * bq, bq), c, False), init)
            for j, (m, l, acc) in enumerate(k_block(qs, cs, True)):
                o_ref[0, pl.ds(qs + j * rows, rows), :] = acc / l
                lse_ref[0, pl.ds(qs + j * rows, rows), :] = m + jnp.log(l)
            return carry

        lax.fori_loop(0, nq, q_step, 0)

    spec = lambda d: pl.BlockSpec((1, S, d), lambda h, b: (h, b, 0))
    return _call_beside(
        body, transfer, grid=(H, B), name="attn_fwd",
        in_specs=[spec(QK_DIM), spec(QK_DIM), spec(V_DIM)],
        out_specs=[spec(V_DIM), spec(1)],
        out_shape=[SDS((H, B * S, V_DIM), F32), SDS((H, B * S, 1), F32)],
        scratch_shapes=[], semantics=("arbitrary", "arbitrary"), args=(q4, k4, v4))


def _conv_taps(u, halo, w):
    sh = [_shift_down(u, halo, j) for j in range(CONV_W)]
    c = w[0:1] * sh[3] + w[1:2] * sh[2] + w[2:3] * sh[1] + w[3:4] * sh[0]
    return c, sh


def _gate_values(gab, alog_l, dt_l, lane):
    g = -jnp.exp(alog_l) * jax.nn.softplus(gab + dt_l)
    g = jnp.where(lane < GDN_HEADS, g, 0.0)
    beta = jnp.where((lane >= GDN_HEADS) & (lane < 2 * GDN_HEADS), _sigmoid(gab), 0.0)
    return g, beta


def _unit_lower_inverses(Ls, eye):
    Ps = [eye - L for L in Ls]
    Ms = [_split(-L) for L in Ls]
    for _ in range(5):
        sq = [_mm_split(m, m) for m in Ms]
        Ms = [_split(s) for s in sq]
        Ps = [p + _mm_split(_split(p), m) for p, m in zip(Ps, Ms)]
    return Ps


def _chunk_decays(gt, lane, h, ri, ci, rcol):
    Gc = _pick_lane(gt, lane, h)
    bt = _pick_lane(gt, lane, h + GDN_HEADS)
    Gb = jnp.broadcast_to(Gc, (CHUNK, CHUNK))
    Gam = jnp.where(ri >= ci, jnp.exp(Gb - Gb.T), 0.0)
    Gl = jnp.sum(jnp.where(rcol == CHUNK - 1, Gc, 0.0), axis=0, keepdims=True)
    return Gc, bt, Gam, jnp.exp(Gc), jnp.exp(Gl - Gc), jnp.exp(Gl)


GDN_FWD_UNROLL = 16
GDN_BWD_UNROLL = 8
GDN_RECUR_STEPS_PER_STAGE = 2


def _gdn_fwd(qg, kg, vg, gates, B, S, transfer=None):
    H, D, C = GDN_HEADS, GDN_DIM, CHUNK
    NC = S // C
    P = 2 if B % 2 == 0 else 1
    Sb, NCb = P * S, P * NC
    U = GDN_FWD_UNROLL if NCb % GDN_FWD_UNROLL == 0 else 1
    NG = NCb // U

    def body(q_ref, k_ref, v_ref, g_ref, o_ref, st_ref, ai_ref, u_ref, w_ref, q2_s, au_s, bc_s, w2_s, el_s):
        h = pl.program_id(0)
        lane = lax.broadcasted_iota(jnp.int32, (C, LANES), 1)
        ri = lax.broadcasted_iota(jnp.int32, (C, C), 0)
        ci = lax.broadcasted_iota(jnp.int32, (C, C), 1)
        rcol = lax.broadcasted_iota(jnp.int32, (C, 1), 0)
        eye = (ri == ci).astype(F32)

        def group(gi, c):
            ns = [gi * U + j for j in range(U)]
            css = [pl.multiple_of(n * C, C) for n in ns]
            qs = [q_ref[0, pl.ds(cs, C), :] for cs in css]
            ks = [k_ref[0, pl.ds(cs, C), :] for cs in css]
            vs = [v_ref[0, pl.ds(cs, C), :] for cs in css]
            decs = [_chunk_decays(g_ref[pl.ds(cs, C), :], lane, h, ri, ci, rcol) for cs in css]
            qks = [_mm_nt(jnp.concatenate([q, k], axis=0), k) for q, k in zip(qs, ks)]
            ainvs = _unit_lower_inverses(
                [jnp.where(ri > ci, d[1] * qk[C:] * d[2], 0.0) for qk, d in zip(qks, decs)], eye)
            sols = [_mm_exact(a, jnp.concatenate([v * d[1], k * (d[1] * d[3])], axis=-1))
                    for a, k, v, d in zip(ainvs, ks, vs, decs)]
            atuw = [_mm(qk[:C] * d[2], sol) for qk, d, sol in zip(qks, decs, sols)]
            kduw = [_mm_tn(k * d[4], sol) for k, d, sol in zip(ks, decs, sols)]
            for n, cs, q, a, sol, au, ku, (Gc, bt, Gam, e, f, eL) in zip(ns, css, qs, ainvs, sols, atuw, kduw, decs):
                u_ref[0, pl.ds(cs, C), :] = sol[:, :D]
                w_ref[0, pl.ds(cs, C), :] = sol[:, D:]
                au_s[pl.ds(cs, C), :] = au[:, :D]
                q2_s[pl.ds(cs, C), :] = q * e - au[:, D:]
                bc_s[n] = ku[:, :D]
                w2_s[n] = ku[:, D:]
                el_s[n] = jnp.broadcast_to(eL, (SUBLANES, LANES))
                ai_ref[0, n] = a.T
            return c

        lax.fori_loop(0, NG, group, 0)

        def step(n, states):
            new = []
            for p, S_ in enumerate(states):
                m = p * NC + n
                cs = pl.multiple_of(m * C, C)
                o_ref[0, pl.ds(cs, C), :] = _mm(q2_s[pl.ds(cs, C), :], S_) + au_s[pl.ds(cs, C), :]
                st_ref[0, m] = S_
                new.append(S_ * el_s[m, 0:1, :] + bc_s[m] - _mm(w2_s[m], S_))
            return tuple(new)

        lax.fori_loop(0, NC, step, tuple(jnp.zeros((D, D), F32) for _ in range(P)))

    spec = pl.BlockSpec((1, Sb, D), lambda h, b: (h, b, 0))
    return _call_beside(
        body, transfer, grid=(H, B // P), name="gdn_fwd",
        in_specs=[spec, spec, spec, pl.BlockSpec((Sb, LANES), lambda h, b: (b, 0))],
        out_specs=[spec, pl.BlockSpec((1, NCb, D, D), lambda h, b: (h, b, 0, 0)),
                   pl.BlockSpec((1, NCb, C, C), lambda h, b: (h, b, 0, 0)), spec, spec],
        out_shape=[SDS((H, B * S, D), F32), SDS((H, B * NC, D, D), F32), SDS((H, B * NC, C, C), F32),
                   SDS((H, B * S, D), F32), SDS((H, B * S, D), F32)],
        scratch_shapes=[pltpu.VMEM((Sb, D), F32), pltpu.VMEM((Sb, D), F32), pltpu.VMEM((NCb, D, D), F32),
                        pltpu.VMEM((NCb, D, D), F32), pltpu.VMEM((NCb, SUBLANES, LANES), F32)],
        semantics=("arbitrary", "arbitrary"), args=(qg, kg, vg, gates))


def _mix_out(o_mla, o_gdn, proj, x2, mla_w, gdn_w, w_out):
    T, D = x2.shape
    tm = min(512, T)
    H = MLA_HEADS

    def body(om_ref, og_ref, z_ref, x_ref, mw_ref, gw_ref, w_ref, h_ref, mix_ref):
        z = z_ref[...]
        parts = [_rms(om_ref[h], mw_ref[h:h + 1, :])[0] for h in range(H)]
        for h in range(GDN_HEADS):
            zh = z[:, h * GDN_DIM:(h + 1) * GDN_DIM]
            parts.append(_rms(og_ref[h], gw_ref[...])[0] * (zh * _sigmoid(zh)))
        mix = jnp.concatenate(parts, axis=-1).astype(MXU_DTYPE)
        mix_ref[...] = mix
        h_ref[...] = x_ref[...] + jnp.dot(mix, w_ref[...], preferred_element_type=F32)

    hspec = pl.BlockSpec((H, tm, V_DIM), lambda i: (0, i, 0))
    return pl.pallas_call(
        body, grid=(T // tm,), name="mix_out",
        in_specs=[hspec, hspec, pl.BlockSpec((tm, GDN_WIDTH), lambda i: (i, P_GZ // GDN_WIDTH)),
                  pl.BlockSpec((tm, D), lambda i: (i, 0)),
                  pl.BlockSpec((H, V_DIM), lambda i: (0, 0)), pl.BlockSpec((1, GDN_DIM), lambda i: (0, 0)),
                  pl.BlockSpec((D, D), lambda i: (0, 0))],
        out_specs=[pl.BlockSpec((tm, D), lambda i: (i, 0)), pl.BlockSpec((tm, D), lambda i: (i, 0))],
        out_shape=[SDS((T, D), F32), SDS((T, D), MXU_DTYPE)],
        compiler_params=_params(("arbitrary",)),
    )(o_mla, o_gdn, proj, x2, mla_w, gdn_w, w_out)


def _mlp_fwd(h2, w_mn, w_up, w_down, target):
    T, D = h2.shape
    ns, _, ts = w_up.shape
    F = ns * ts
    tm = min(512, T)
    G = MLP_FWD_SHARDS
    tf, nf = G * ts, ns // G

    def body(h_ref, wn_ref, up_w, down_w, t_ref, up_ref, hn_ref, dy_ref, loss_ref, dyb_ref, y_acc):
        j = pl.program_id(1)

        @pl.when(j == 0)
        def _():
            hn_ref[...] = _rms(h_ref[...], wn_ref[...])[0].astype(MXU_DTYPE)
            y_acc[...] = h_ref[...]

        parts = []
        for c in range(G):
            up = jnp.dot(hn_ref[...], up_w[c], preferred_element_type=F32)
            up_ref[:, c * ts:(c + 1) * ts] = up.astype(MXU_DTYPE)
            r = jnp.maximum(up, 0.0)
            parts.append(_mm(r * r, down_w[c * ts:(c + 1) * ts, :]))
        y_acc[...] += functools.reduce(jnp.add, parts)

        @pl.when(j == nf - 1)
        def _():
            err = y_acc[...] - t_ref[...]
            dy_ref[...] = err / D
            dyb_ref[...] = (err / D).astype(MXU_DTYPE)
            loss_ref[...] = jnp.full((1, SUBLANES, LANES), jnp.sum(err * err), F32)

    return pl.pallas_call(
        body, grid=(T // tm, nf), name="mlp_fwd",
        in_specs=[pl.BlockSpec((tm, D), lambda i, j: (i, 0)), pl.BlockSpec((1, D), lambda i, j: (0, 0)),
                  pl.BlockSpec((G, D, ts), lambda i, j: (j, 0, 0)), pl.BlockSpec((tf, D), lambda i, j: (j, 0)),
                  pl.BlockSpec((tm, D), lambda i, j: (i, 0))],
        out_specs=[pl.BlockSpec((tm, tf), lambda i, j: (i, j)), pl.BlockSpec((tm, D), lambda i, j: (i, 0)),
                   pl.BlockSpec((tm, D), lambda i, j: (i, 0)),
                   pl.BlockSpec((1, SUBLANES, LANES), lambda i, j: (i, 0, 0)),
                   pl.BlockSpec((tm, D), lambda i, j: (i, 0))],
        out_shape=[SDS((T, F), MXU_DTYPE), SDS((T, D), MXU_DTYPE), SDS((T, D), F32),
                   SDS((T // tm, SUBLANES, LANES), F32), SDS((T, D), MXU_DTYPE)],
        scratch_shapes=[pltpu.VMEM((tm, D), F32)],
        compiler_params=_params(("arbitrary", "arbitrary")),
    )(h2, w_mn, w_up, w_down, target)


def _mlp_bwd(dy, dyb, up, h2, w_mn, w_up, w_down, transfer=None):
    T, D = h2.shape
    ns, _, ts = w_up.shape
    F = ns * ts
    tm = min(512, T)
    G = MLP_BWD_SHARDS
    tf, nf = G * ts, ns // G

    def body(dy_ref, dyb_ref, up_ref, h_ref, wn_ref, up_w, down_w, dh_ref, dhb_ref, dup_ref, dwn_ref, acc):
        i, j = pl.program_id(0), pl.program_id(1)

        @pl.when((i == 0) & (j == 0))
        def _():
            dwn_ref[...] = jnp.zeros_like(dwn_ref)

        @pl.when(j == 0)
        def _():
            acc[...] = jnp.zeros_like(acc)

        parts = []
        for c in range(G):
            cols = slice(c * ts, (c + 1) * ts)
            r = jnp.maximum(up_ref[:, cols].astype(F32), 0.0)
            dup = (_mm_nt(dyb_ref[...], down_w[cols, :]) * (2.0 * r)).astype(MXU_DTYPE)
            dup_ref[:, cols] = dup
            parts.append(_mm_nt(dup, up_w[c]))
        acc[...] += functools.reduce(jnp.add, parts)

        @pl.when(j == nf - 1)
        def _():
            hv = h_ref[...]
            _, rr = _rms(hv, wn_ref[...])
            dx, dw = _rms_bwd(acc[...], hv, wn_ref[...], rr)
            dh = dy_ref[...] + dx
            dh_ref[...] = dh
            dhb_ref[...] = dh.astype(MXU_DTYPE)
            dwn_ref[...] += dw

    row = lambda i, j: (i, 0)
    return _call_beside(
        body, transfer, grid=(T // tm, nf), name="mlp_bwd",
        in_specs=[pl.BlockSpec((tm, D), row), pl.BlockSpec((tm, D), row), pl.BlockSpec((tm, tf), lambda i, j: (i, j)),
                  pl.BlockSpec((tm, D), row), pl.BlockSpec((1, D), lambda i, j: (0, 0)),
                  pl.BlockSpec((G, D, ts), lambda i, j: (j, 0, 0)), pl.BlockSpec((tf, D), lambda i, j: (j, 0))],
        out_specs=[pl.BlockSpec((tm, D), row), pl.BlockSpec((tm, D), row),
                   pl.BlockSpec((tm, tf), lambda i, j: (i, j)), pl.BlockSpec((1, D), lambda i, j: (0, 0))],
        out_shape=[SDS((T, D), F32), SDS((T, D), MXU_DTYPE), SDS((T, F), MXU_DTYPE), SDS((1, D), F32)],
        scratch_shapes=[pltpu.VMEM((tm, D), F32)], semantics=("arbitrary", "arbitrary"),
        args=(dy, dyb, up, h2, w_mn, w_up, w_down))


def _mix_bwd(dhb, o_mla, o_gdn, proj, mla_w, gdn_w, w_out):
    T, D = dhb.shape
    tm = min(512, T)
    H = MLA_HEADS

    def body(dh_ref, om_ref, og_ref, z_ref, mw_ref, gw_ref, w_ref, dom_ref, dog_ref, dz_ref, dmw_ref, dgw_ref,
             delta_ref):
        @pl.when(pl.program_id(0) == 0)
        def _():
            dmw_ref[...] = jnp.zeros_like(dmw_ref)
            dgw_ref[...] = jnp.zeros_like(dgw_ref)

        dmix = _mm_nt(dh_ref[...], w_ref[...])
        z = z_ref[...]
        dmw, dzs = [], []
        dgw = jnp.zeros((1, GDN_DIM), F32)
        for h in range(H):
            o = om_ref[h]
            w = mw_ref[h:h + 1, :]
            _, r = _rms(o, w)
            dx, dw = _rms_bwd(dmix[:, h * V_DIM:(h + 1) * V_DIM], o, w, r)
            dom_ref[h] = dx.astype(MXU_DTYPE)
            delta_ref[h] = jnp.sum(dx * o, axis=-1, keepdims=True)
            dmw.append(dw)
        for h in range(GDN_HEADS):
            o = og_ref[h]
            w = gw_ref[...]
            zh = z[:, h * GDN_DIM:(h + 1) * GDN_DIM]
            sg = _sigmoid(zh)
            yn, r = _rms(o, w)
            dy = dmix[:, H * V_DIM + h * GDN_DIM:H * V_DIM + (h + 1) * GDN_DIM]
            dzs.append(dy * yn * (sg * (1.0 + zh * (1.0 - sg))))
            dx, dw = _rms_bwd(dy * (zh * sg), o, w, r)
            dog_ref[h] = dx.astype(MXU_DTYPE)
            dgw = dgw + dw
        dz_ref[...] = jnp.concatenate(dzs, axis=-1).astype(MXU_DTYPE)
        dmw_ref[...] += jnp.concatenate(dmw, axis=0)
        dgw_ref[...] += dgw

    hspec = pl.BlockSpec((H, tm, V_DIM), lambda i: (0, i, 0))
    return pl.pallas_call(
        body, grid=(T // tm,), name="mix_bwd",
        in_specs=[pl.BlockSpec((tm, D), lambda i: (i, 0)), hspec, hspec,
                  pl.BlockSpec((tm, GDN_WIDTH), lambda i: (i, P_GZ // GDN_WIDTH)),
                  pl.BlockSpec((H, V_DIM), lambda i: (0, 0)), pl.BlockSpec((1, GDN_DIM), lambda i: (0, 0)),
                  pl.BlockSpec((D, D), lambda i: (0, 0))],
        out_specs=[hspec, hspec, pl.BlockSpec((tm, GDN_WIDTH), lambda i: (i, 0)),
                   pl.BlockSpec((H, V_DIM), lambda i: (0, 0)), pl.BlockSpec((1, GDN_DIM), lambda i: (0, 0)),
                   pl.BlockSpec((H, tm, 1), lambda i: (0, i, 0))],
        out_shape=[SDS((H, T, V_DIM), MXU_DTYPE), SDS((H, T, GDN_DIM), MXU_DTYPE), SDS((T, GDN_WIDTH), MXU_DTYPE),
                   SDS((H, V_DIM), F32), SDS((1, GDN_DIM), F32), SDS((H, T, 1), F32)],
        compiler_params=_params(("arbitrary",)),
    )(dhb, o_mla, o_gdn, proj, mla_w, gdn_w, w_out)


def _attn_bwd(q4, k4, v4, do4, delta4, lse4, B, S, transfer=None):
    H = MLA_HEADS
    bq = min(ATTN_BLOCK, S)
    nq = S // bq
    rows = bq // ATTN_CHAINS

    def body(q_ref, k_ref, v_ref, do_ref, delta_ref, lse_ref, dq_ref, dk_ref, dv_ref):
        dq_ref[...] = jnp.zeros_like(dq_ref)
        dk_ref[...] = jnp.zeros_like(dk_ref)
        dv_ref[...] = jnp.zeros_like(dv_ref)

        col = lax.broadcasted_iota(jnp.int32, (rows, bq), 1)
        row = lax.broadcasted_iota(jnp.int32, (rows, bq), 0)

        def k_step(kj, carry):
            ks = pl.multiple_of(kj * bq, bq)
            k = k_ref[0, pl.ds(ks, bq), :]
            v = v_ref[0, pl.ds(ks, bq), :]

            def q_block(qs, diagonal):
                dks, dvs = [None] * ATTN_CHAINS, [None] * ATTN_CHAINS

                def chain(j):
                    sl = pl.ds(qs + j * rows, rows)
                    q = q_ref[0, sl, :]
                    do = do_ref[0, sl, :].astype(MXU_DTYPE)
                    s = _mm_nt(q, k)
                    dp = _mm_nt(do, v)
                    yield
                    p = jnp.exp(s - lse_ref[0, sl, :])
                    if diagonal:
                        p = jnp.where(col <= row + j * rows, p, 0.0)
                    ds = p * (dp - delta_ref[0, sl, :])
                    yield
                    dvs[j] = _mm_tn(p, do)
                    dks[j] = _mm_tn(ds, q)
                    dq_ref[0, sl, :] += _mm(ds, k)

                _lockstep([chain(j) for j in range(ATTN_CHAINS)])
                dv_ref[0, pl.ds(ks, bq), :] += functools.reduce(jnp.add, dvs)
                dk_ref[0, pl.ds(ks, bq), :] += functools.reduce(jnp.add, dks)

            q_block(ks, True)

            def q_step(qi, c):
                q_block(pl.multiple_of(qi * bq, bq), False)
                return c

            lax.fori_loop(kj + 1, nq, q_step, 0)
            return carry

        lax.fori_loop(0, nq, k_step, 0)

    spec = lambda d: pl.BlockSpec((1, S, d), lambda h, b: (h, b, 0))
    return _call_beside(
        body, transfer, grid=(H, B), name="attn_bwd",
        in_specs=[spec(QK_DIM), spec(QK_DIM), spec(V_DIM), spec(V_DIM), spec(1), spec(1)],
        out_specs=[spec(QK_DIM), spec(QK_DIM), spec(V_DIM)],
        out_shape=[SDS((H, B * S, QK_DIM), F32), SDS((H, B * S, QK_DIM), F32), SDS((H, B * S, V_DIM), F32)],
        scratch_shapes=[], semantics=("arbitrary", "arbitrary"),
        args=(q4, k4, v4, do4, delta4, lse4))


def _gdn_bwd(qg, kg, vg, gates, states, ainv, u4, w4, do4, B, S, transfer=None):
    H, D, C = GDN_HEADS, GDN_DIM, CHUNK
    NC = S // C
    U = GDN_BWD_UNROLL if NC % GDN_BWD_UNROLL == 0 else 1
    NG = NC // U

    def body(q_ref, k_ref, v_ref, g_ref, st_ref, ai_ref, u_ref, w_ref, do_ref, dq_ref, dk_ref, dv_ref, dgb_ref,
             kd_s, x1_s, x2_s, el_s, dvn_s, ds_s, w2t_s):
        h = pl.program_id(0)
        lane = lax.broadcasted_iota(jnp.int32, (C, LANES), 1)
        ri = lax.broadcasted_iota(jnp.int32, (C, C), 0)
        ci = lax.broadcasted_iota(jnp.int32, (C, C), 1)
        rcol = lax.broadcasted_iota(jnp.int32, (C, 1), 0)

        def rsum(a):
            return jnp.sum(a, axis=-1, keepdims=True)

        def prepare(n):
            cs = n * C
            q = q_ref[0, pl.ds(cs, C), :]
            k = k_ref[0, pl.ds(cs, C), :]
            do = do_ref[0, pl.ds(cs, C), :]
            Gc, bt, Gam, e, f, eL = _chunk_decays(g_ref[pl.ds(cs, C), :], lane, h, ri, ci, rcol)
            At = _mm_nt(q, k) * Gam
            yield
            x1 = _mm_tn(At, do)
            x2 = _mm_tn(q * e, do)
            kd = k * f
            w = w_ref[0, pl.ds(cs, C), :]
            yield
            x1_s[pl.ds(cs, C), :] = x1
            x2_s[n] = x2 - _mm_tn(w, x1)
            w2t_s[n] = _mm_tn(w, kd)
            kd_s[pl.ds(cs, C), :] = kd
            el_s[n] = jnp.broadcast_to(eL, (SUBLANES, LANES))

        def recur(n, dS):
            cs = n * C
            ds_s[n] = dS
            dvn_s[pl.ds(cs, C), :] = x1_s[pl.ds(cs, C), :] + _mm(kd_s[pl.ds(cs, C), :], dS)
            return x2_s[n] + el_s[n, 0:1, :] * dS - _mm(w2t_s[n], dS)

        def local(n):
            cs = n * C
            q = q_ref[0, pl.ds(cs, C), :]
            k = k_ref[0, pl.ds(cs, C), :]
            v = v_ref[0, pl.ds(cs, C), :]
            do = do_ref[0, pl.ds(cs, C), :]
            u = u_ref[0, pl.ds(cs, C), :]
            w = w_ref[0, pl.ds(cs, C), :]
            dvn = dvn_s[pl.ds(cs, C), :]
            dS = ds_s[n]
            Gc, bt, Gam, e, f, eL = _chunk_decays(g_ref[pl.ds(cs, C), :], lane, h, ri, ci, rcol)
            S0 = st_ref[0, n]
            AinvT = ai_ref[0, n]
            qk = _mm_nt(jnp.concatenate([q, k], axis=0), k)
            QK, KK = qk[:C], qk[C:]
            be = bt * e
            sol = jnp.concatenate([u, w], axis=-1)
            vn = u - _mm(w, S0)
            yield
            dAt = jnp.where(ri >= ci, _mm_nt(do, vn), 0.0)
            dqd = _mm_nt(do, S0)
            dw = -_mm_nt(dvn, S0)
            dkd = _mm_nt(vn, dS)
            deL = jnp.sum(rsum(dS * S0), axis=0, keepdims=True)
            yield
            dR = _mm_exact(AinvT, jnp.concatenate([dvn, dw], axis=-1))
            dR1, dR2 = dR[:, :D], dR[:, D:]
            yield
            dL = jnp.where(ri > ci, -_mm_nt(dR, sol), 0.0)
            yield
            dv_ref[0, pl.ds(cs, C), :] = dR1 * bt
            r2 = rsum(dR2 * k)
            X = dL * Gam
            dbt = rsum(dR1 * v) + r2 * e + rsum(X * KK)
            de = r2 * bt + rsum(dqd * q)
            dKK = X * bt
            dQK = dAt * Gam
            dq_ref[0, pl.ds(cs, C), :] = _mm(dQK, k) + dqd * e
            dk_ref[0, pl.ds(cs, C), :] = dR2 * be + _mm(dKK + dKK.T, k) + _mm_tn(dQK, q) + dkd * f
            df = rsum(dkd * k)
            Z = (dL * (bt * KK) + dAt * QK) * Gam
            dG = rsum(Z) - rsum(Z.T) + de * e - df * f
            dGl = jnp.sum(df * f, axis=0, keepdims=True) + deL * eL
            dG = dG + jnp.where(rcol == C - 1, dGl, 0.0)
            dgb_ref[0, pl.ds(cs, C), :] = jnp.where(lane == 0, dG, jnp.where(lane == 1, dbt, 0.0))

        state = [jnp.zeros((D, D), F32)]

        def recur_group(g):
            for j, n in enumerate(reversed(range(g * U, (g + 1) * U))):
                state[0] = recur(n, state[0])
                if j % GDN_RECUR_STEPS_PER_STAGE == GDN_RECUR_STEPS_PER_STAGE - 1:
                    yield

        def stage(fn, g):
            return _together([fn(g * U + j) for j in range(U)])

        for step in range(NG + 2):
            jobs = [(stage, prepare, NG - 1 - step), (None, None, NG - step), (stage, local, NG + 1 - step)]
            _lockstep([recur_group(g) if make is None else make(fn, g) for make, fn, g in jobs if 0 <= g < NG])

    spec = pl.BlockSpec((1, S, D), lambda h, b: (h, b, 0))
    return _call_beside(
        body, transfer, grid=(H, B), name="gdn_bwd",
        in_specs=[spec, spec, spec, pl.BlockSpec((S, LANES), lambda h, b: (b, 0)),
                  pl.BlockSpec((1, NC, D, D), lambda h, b: (h, b, 0, 0)),
                  pl.BlockSpec((1, NC, C, C), lambda h, b: (h, b, 0, 0)), spec, spec, spec],
        out_specs=[spec, spec, spec, spec],
        out_shape=[SDS((H, B * S, D), F32)] * 4,
        scratch_shapes=[pltpu.VMEM((S, D), F32), pltpu.VMEM((S, D), F32), pltpu.VMEM((NC, D, D), F32),
                        pltpu.VMEM((NC, SUBLANES, LANES), F32), pltpu.VMEM((S, D), F32),
                        pltpu.VMEM((NC, D, D), F32), pltpu.VMEM((NC, D, D), F32)],
        semantics=("arbitrary", "arbitrary"), args=(qg, kg, vg, gates, states, ainv, u4, w4, do4))


def _gdn_pre_bwd(proj, conv_w, alog_l, dt_l, dq4, dk4, dv4, dgb4, S):
    T = proj.shape[0]
    tm = min(256, T)
    tiles_per_seq = S // tm
    C3 = 3 * GDN_WIDTH
    H = GDN_HEADS

    def body(u_ref, halo_ref, gab_ref, w_ref, alog_ref, dt_ref, dq_ref, dk_ref, dv_ref, dgb_ref,
             dc_ref, dgab_ref, dcw_ref, dalog_ref, ddt_ref):
        i = pl.program_id(0)

        @pl.when(i == 0)
        def _():
            dcw_ref[...] = jnp.zeros_like(dcw_ref)
            dalog_ref[...] = jnp.zeros_like(dalog_ref)
            ddt_ref[...] = jnp.zeros_like(ddt_ref)

        halo = jnp.where(i % tiles_per_seq == 0, 0.0, halo_ref[...])
        c, sh = _conv_taps(u_ref[...], halo, w_ref[...])
        sg = _sigmoid(c)
        a = c * sg
        das = [None] * (3 * H)
        for h in range(H):
            xq = a[:, h * GDN_DIM:(h + 1) * GDN_DIM]
            xk = a[:, GDN_WIDTH + h * GDN_DIM:GDN_WIDTH + (h + 1) * GDN_DIM]
            das[h] = _l2n_bwd(dq_ref[h], xq, GDN_QSCALE)
            das[H + h] = _l2n_bwd(dk_ref[h], xk, 1.0)
            das[2 * H + h] = dv_ref[h]
        dc = jnp.concatenate(das, axis=-1) * (sg * (1.0 + c * (1.0 - sg)))
        dc_ref[...] = dc
        dcw_ref[...] += jnp.concatenate(
            [jnp.sum(dc * sh[CONV_W - 1 - t], axis=0, keepdims=True) for t in range(CONV_W)], axis=0)
        lane = lax.broadcasted_iota(jnp.int32, (tm, LANES), 1)
        ric = lax.broadcasted_iota(jnp.int32, (tm, LANES), 0) % CHUNK
        dG = jnp.zeros((tm, LANES), F32)
        for h in range(H):
            t = dgb_ref[h]
            dG = dG + jnp.where(lane == h, _pick_lane(t, lane, 0), 0.0) \
                    + jnp.where(lane == h + H, _pick_lane(t, lane, 1), 0.0)
        is_g = lane < H
        dg = jnp.where(is_g, _chunk_rev_cumsum(jnp.where(is_g, dG, 0.0), ric), 0.0)
        gab = gab_ref[...]
        g, beta = _gate_values(gab, alog_ref[...], dt_ref[...], lane)
        dga = jnp.where(is_g, dg * (-jnp.exp(alog_ref[...])) * _sigmoid(gab + dt_ref[...]), 0.0)
        dgb = jnp.where(is_g, 0.0, dG) * beta * (1.0 - beta)
        dgab_ref[...] = (dga + dgb).astype(MXU_DTYPE)
        dalog_ref[...] += jnp.sum(dg * g, axis=0, keepdims=True)
        ddt_ref[...] += jnp.sum(dga, axis=0, keepdims=True)

    hspec = pl.BlockSpec((H, tm, GDN_DIM), lambda i: (0, i, 0))
    vec = pl.BlockSpec((1, LANES), lambda i: (0, 0))
    return pl.pallas_call(
        body, grid=(T // tm,), name="gdn_pre_bwd",
        in_specs=[pl.BlockSpec((tm, C3), lambda i: (i, 0)),
                  pl.BlockSpec((SUBLANES, C3), lambda i: (jnp.maximum(i * (tm // SUBLANES) - 1, 0), 0)),
                  pl.BlockSpec((tm, LANES), lambda i: (i, P_GAB // LANES)),
                  pl.BlockSpec((CONV_W, C3), lambda i: (0, 0)), vec, vec, hspec, hspec, hspec, hspec],
        out_specs=[pl.BlockSpec((tm, C3), lambda i: (i, 0)), pl.BlockSpec((tm, LANES), lambda i: (i, 0)),
                   pl.BlockSpec((CONV_W, C3), lambda i: (0, 0)), vec, vec],
        out_shape=[SDS((T, C3), F32), SDS((T, LANES), MXU_DTYPE), SDS((CONV_W, C3), F32),
                   SDS((1, LANES), F32), SDS((1, LANES), F32)],
        compiler_params=_params(("arbitrary",)),
    )(proj, proj, proj, conv_w, alog_l, dt_l, dq4, dk4, dv4, dgb4)


def _mla_pre_bwd(proj, cosf, sinf, w_qln, w_kvln, w_uq_p, w_ukv, qnw, knw, dq4, dk4, dv4, transfer=None):
    T = proj.shape[0]
    tm = min(256, T)
    H = MLA_HEADS

    def body(ql_ref, kvl_ref, kpe_ref, cos_ref, sin_ref, wq_ref, wkv_ref, uq_ref, ukv_ref, qnw_ref, knw_ref,
             dq_ref, dk_ref, dv_ref,
             dql_ref, dkvl_ref, dkpe_ref, dqraw_ref, dkvraw_ref, qn_ref, kvn_ref, dwq_ref, dwkv_ref, dqnw_ref, dknw_ref):
        @pl.when(pl.program_id(0) == 0)
        def _():
            for r in (dwq_ref, dwkv_ref, dqnw_ref, dknw_ref):
                r[...] = jnp.zeros_like(r)

        cos, sin = cos_ref[...], sin_ref[...]
        qnw_, knw_ = qnw_ref[...], knw_ref[...]
        ql, kvl = ql_ref[...], kvl_ref[...]
        kpe_raw = kpe_ref[...][:, :ROPE]
        rms = functools.partial(_rms, on_mxu=True)
        rms_bwd = functools.partial(_rms_bwd, on_mxu=True)
        qn, rq = rms(ql, wq_ref[...])
        kvn, rkv = rms(kvl, wkv_ref[...])
        qn_ref[...] = qn.astype(MXU_DTYPE)
        kvn_ref[...] = kvn.astype(MXU_DTYPE)
        qraw = _mm(qn, uq_ref[...])
        kvraw = _mm(kvn, ukv_ref[...])
        dq_nope, dq_pe, dkv_parts = [], [], []
        dqnw_n = jnp.zeros((1, NOPE), F32)
        dqnw_p = jnp.zeros((1, ROPE), F32)
        dknw_n = jnp.zeros((1, NOPE), F32)
        dkpe = jnp.zeros((tm, ROPE), F32)
        for h in range(H):
            dq = dq_ref[h] * ATT_SCALE
            x = qraw[:, h * NOPE:(h + 1) * NOPE]
            dx, dw = rms_bwd(dq[:, :NOPE], x, qnw_[:, :NOPE], rms(x, qnw_[:, :NOPE])[1])
            dq_nope.append(dx)
            dqnw_n = dqnw_n + dw
            x = qraw[:, H * NOPE + h * ROPE:H * NOPE + (h + 1) * ROPE]
            dx, dw = rms_bwd(_rope_bwd(dq[:, NOPE:], cos, sin), x, qnw_[:, NOPE:], rms(x, qnw_[:, NOPE:])[1])
            dq_pe.append(dx)
            dqnw_p = dqnw_p + dw
            dk = dk_ref[h]
            x = kvraw[:, h * 256:h * 256 + NOPE]
            dx, dw = rms_bwd(dk[:, :NOPE], x, knw_[:, :NOPE], rms(x, knw_[:, :NOPE])[1])
            dknw_n = dknw_n + dw
            dkpe = dkpe + dk[:, NOPE:]
            dkv_parts += [dx, dv_ref[h]]
        dx, dknw_p = rms_bwd(_rope_bwd(dkpe, cos, sin), kpe_raw, knw_[:, NOPE:], rms(kpe_raw, knw_[:, NOPE:])[1])
        dkpe_ref[...] = jnp.concatenate([dx, jnp.zeros((tm, LANES - ROPE), F32)], axis=-1).astype(MXU_DTYPE)
        dqraw = jnp.concatenate(dq_nope + dq_pe, axis=-1).astype(MXU_DTYPE)
        dkvraw = jnp.concatenate(dkv_parts, axis=-1).astype(MXU_DTYPE)
        dqraw_ref[...] = dqraw
        dkvraw_ref[...] = dkvraw
        dx, dw = rms_bwd(_mm_nt(dqraw, uq_ref[...]), ql, wq_ref[...], rq)
        dql_ref[...] = dx.astype(MXU_DTYPE)
        dwq_ref[...] += dw
        dx, dw = rms_bwd(_mm_nt(dkvraw, ukv_ref[...]), kvl, wkv_ref[...], rkv)
        dkvl_ref[...] = dx.astype(MXU_DTYPE)
        dwkv_ref[...] += dw
        dqnw_ref[...] += jnp.concatenate([dqnw_n, dqnw_p], axis=-1)
        dknw_ref[...] += jnp.concatenate([dknw_n, dknw_p], axis=-1)

    full = lambda a: pl.BlockSpec(a.shape, lambda i: (0,) * a.ndim)
    rows = lambda n: pl.BlockSpec((tm, n), lambda i: (i, 0))
    const = lambda n: pl.BlockSpec((1, n), lambda i: (0, 0))
    NQ, NKV = w_uq_p.shape[1], w_ukv.shape[1]
    return _call_beside(
        body, transfer, grid=(T // tm,), name="mla_pre_bwd", scratch_shapes=[], semantics=("arbitrary",),
        args=(proj, proj, proj, cosf, sinf, w_qln, w_kvln, w_uq_p, w_ukv, qnw, knw, dq4, dk4, dv4),
        in_specs=[pl.BlockSpec((tm, 256), lambda i: (i, P_QLAT // 256)),
                  pl.BlockSpec((tm, 256), lambda i: (i, P_KVLAT // 256)),
                  pl.BlockSpec((tm, 128), lambda i: (i, P_KPE // 128)),
                  rows(ROPE), rows(ROPE),
                  full(w_qln), full(w_kvln), full(w_uq_p), full(w_ukv), full(qnw), full(knw),
                  pl.BlockSpec((H, tm, QK_DIM), lambda i: (0, i, 0)),
                  pl.BlockSpec((H, tm, QK_DIM), lambda i: (0, i, 0)),
                  pl.BlockSpec((H, tm, V_DIM), lambda i: (0, i, 0))],
        out_specs=[rows(Q_LORA), rows(KV_LORA), rows(LANES), rows(NQ), rows(NKV), rows(Q_LORA), rows(KV_LORA),
                   const(Q_LORA), const(KV_LORA), const(QK_DIM), const(QK_DIM)],
        out_shape=[SDS((T, Q_LORA), MXU_DTYPE), SDS((T, KV_LORA), MXU_DTYPE), SDS((T, LANES), MXU_DTYPE),
                   SDS((T, NQ), MXU_DTYPE), SDS((T, NKV), MXU_DTYPE),
                   SDS((T, Q_LORA), MXU_DTYPE), SDS((T, KV_LORA), MXU_DTYPE),
                   SDS((1, Q_LORA), F32), SDS((1, KV_LORA), F32), SDS((1, QK_DIM), F32), SDS((1, QK_DIM), F32)])


def _in_proj_bwd(dc, conv_w, dgz, dql, dkvl, dkpe, dgab, w_in_p, dh, x2, w_an, S):
    T, D = x2.shape
    N = w_in_p.shape[1]
    C3 = dc.shape[1]
    tm = min(512, S)
    assert S % tm == 0 and T % tm == 0, "a token tile must not straddle two sequences"
    tiles_per_seq = S // tm
    nblk = T // SUBLANES

    def body(dc_ref, nxt_ref, cw_ref, b_ref, c_ref, d_ref, e_ref, f_ref, w_ref, dh_ref, x_ref, wn_ref,
             dx_ref, dp_ref, dwn_ref):
        i = pl.program_id(0)

        @pl.when(i == 0)
        def _():
            dwn_ref[...] = jnp.zeros_like(dwn_ref)

        nxt = jnp.where(i % tiles_per_seq == tiles_per_seq - 1, 0.0, nxt_ref[...])
        dcv, cw = dc_ref[...], cw_ref[...]
        du = cw[3:4] * dcv
        for j in range(1, CONV_W):
            du = du + cw[3 - j:4 - j] * _shift_up(dcv, nxt, j)
        dp = jnp.concatenate([du.astype(MXU_DTYPE), b_ref[...], c_ref[...], d_ref[...], e_ref[...], f_ref[...]],
                             axis=-1).astype(MXU_DTYPE)
        dp_ref[...] = dp
        x = x_ref[...]
        _, r = _rms(x, wn_ref[...])
        dx, dw = _rms_bwd(_mm_nt(dp, w_ref[...]), x, wn_ref[...], r)
        dx_ref[...] = dh_ref[...] + dx
        dwn_ref[...] += dw

    rows = lambda n: pl.BlockSpec((tm, n), lambda i: (i, 0))
    return pl.pallas_call(
        body, grid=(T // tm,), name="in_proj_bwd",
        in_specs=[rows(C3),
                  pl.BlockSpec((SUBLANES, C3), lambda i: (jnp.minimum((i + 1) * (tm // SUBLANES), nblk - 1), 0)),
                  pl.BlockSpec((CONV_W, C3), lambda i: (0, 0)),
                  rows(dgz.shape[1]), rows(dql.shape[1]), rows(dkvl.shape[1]),
                  rows(dkpe.shape[1]), rows(dgab.shape[1]),
                  pl.BlockSpec((D, N), lambda i: (0, 0)), rows(D), rows(D), pl.BlockSpec((1, D), lambda i: (0, 0))],
        out_specs=[rows(D), rows(N), pl.BlockSpec((1, D), lambda i: (0, 0))],
        out_shape=[SDS((T, D), F32), SDS((T, N), MXU_DTYPE), SDS((1, D), F32)],
        compiler_params=_params(("arbitrary",)),
    )(dc, dc, conv_w, dgz, dql, dkvl, dkpe, dgab, w_in_p, dh, x2, w_an)


def _relu_squared(t):
    r = jnp.maximum(t.astype(F32), 0.0)
    return (r * r).astype(MXU_DTYPE)


def _wgrad(a, b, name, column_shards=False, a_map=None):
    T, M = a.shape
    N = b.shape[1]
    tM = _divisor_tile(M, 1024)
    tN = N // N_DEV if column_shards else _divisor_tile(N, 1536)
    tk = min(T, 2048)
    nk = T // tk

    def body(a_ref, b_ref, o_ref, acc):
        k = pl.program_id(2)

        @pl.when(k == 0)
        def _():
            acc[...] = jnp.zeros_like(acc)

        acc[...] += _mm_tn(a_ref[...] if a_map is None else a_map(a_ref[...]), b_ref[...])

        @pl.when(k == nk - 1)
        def _():
            o_ref[...] = acc[...].astype(WIRE_DTYPE).reshape(o_ref.shape)

    if column_shards:
        out_spec, out_shape = pl.BlockSpec((1, tM, tN), lambda i, j, k: (j, i, 0)), SDS((N_DEV, M, tN), WIRE_DTYPE)
    else:
        out_spec, out_shape = pl.BlockSpec((tM, tN), lambda i, j, k: (i, j)), SDS((M, N), WIRE_DTYPE)
    return pl.pallas_call(
        body, grid=(M // tM, N // tN, nk), name=name,
        in_specs=[pl.BlockSpec((tk, tM), lambda i, j, k: (k, i)), pl.BlockSpec((tk, tN), lambda i, j, k: (k, j))],
        out_specs=out_spec, out_shape=out_shape,
        scratch_shapes=[pltpu.VMEM((tM, tN), F32)],
        compiler_params=_params(("arbitrary", "arbitrary", "arbitrary")),
    )(a, b)


WGRAD_RING_SLOTS = 3


def _wgrad_stream(a, b, name, tile, stream_a=False, column_shards=False, a_map=None):
    T, M = a.shape
    N = b.shape[1]
    n = (M if stream_a else N) // tile
    tk = min(T, 2048)
    assert (M if stream_a else N) % tile == 0 and T % tk == 0 and (a_map is None or stream_a)
    held, src = (b, a) if stream_a else (a, b)

    def body(held_ref, src_ref, o_ref, ring, sems, *held_t):
        s = pl.program_id(0)

        def fetch(t, slot):
            cols = pl.ds(pl.multiple_of(t * tile, tile), tile)
            return pltpu.make_async_copy(src_ref.at[:, cols], ring.at[slot], sems.at[slot])

        @pl.when(s == 0)
        def _():
            for t in range(min(2, n)):
                fetch(t, t).start()
            if not stream_a:
                for k in range(T // tk):
                    held_t[0][:, pl.ds(k * tk, tk)] = held_ref[pl.ds(k * tk, tk), :].astype(MXU_DTYPE).T

        @pl.when(s + 2 < n)
        def _():
            fetch(s + 2, (s + 2) % WGRAD_RING_SLOTS).start()

        slot = s % WGRAD_RING_SLOTS
        fetch(s, slot).wait()
        acc = None
        for k in range(T // tk):
            rows = pl.ds(k * tk, tk)
            if stream_a:
                at = ring[slot, rows, :]
                part = _mm_tn(at if a_map is None else a_map(at), held_ref[rows, :])
            else:
                part = _mm(held_t[0][:, rows], ring[slot, rows, :])
            acc = part if acc is None else acc + part
        o_ref[...] = acc.astype(WIRE_DTYPE).reshape(o_ref.shape)

    if stream_a:
        out_spec, out_shape = pl.BlockSpec((tile, N), lambda s: (s, 0)), SDS((M, N), WIRE_DTYPE)
    elif column_shards:
        assert tile == N // N_DEV
        out_spec, out_shape = pl.BlockSpec((1, M, tile), lambda s: (s, 0, 0)), SDS((N_DEV, M, tile), WIRE_DTYPE)
    else:
        out_spec, out_shape = pl.BlockSpec((M, tile), lambda s: (0, s)), SDS((M, N), WIRE_DTYPE)
    return pl.pallas_call(
        body, grid=(n,), name=name,
        in_specs=[pl.BlockSpec(held.shape, lambda s: (0, 0)), pl.BlockSpec(memory_space=pl.ANY)],
        out_specs=out_spec, out_shape=out_shape,
        scratch_shapes=[pltpu.VMEM((WGRAD_RING_SLOTS, T, tile), src.dtype),
                        pltpu.SemaphoreType.DMA((WGRAD_RING_SLOTS,))]
        + ([] if stream_a else [pltpu.VMEM((M, T), MXU_DTYPE)]),
        compiler_params=_params(("arbitrary",)),
    )(held, src)


def _adamw(g, w, m, v):
    m = ADAM_B1 * m + (1.0 - ADAM_B1) * g
    v = ADAM_B2 * v + (1.0 - ADAM_B2) * jnp.square(g)
    m_hat = m / (1.0 - ADAM_B1 ** ADAM_STEP)
    v_hat = v / (1.0 - ADAM_B2 ** ADAM_STEP)
    return -ADAM_LR * (m_hat / (jnp.sqrt(v_hat) + ADAM_EPS) + ADAM_WD * w), m, v


def _reduce_adamw(parts, w, m, v, name):
    R, C = w.shape
    slots, Rp, Cp = parts.shape
    tr = min(R, 256)
    tp = tr if Rp == R else Rp

    def body(p_ref, w_ref, m_ref, v_ref, g_ref, d_ref, nm_ref, nv_ref):
        g = p_ref[0].astype(F32)
        for s in range(1, slots):
            g = g + p_ref[s].astype(F32)
        g = g[:tr, :C]
        g_ref[...] = g
        d_ref[...], nm_ref[...], nv_ref[...] = _adamw(g, w_ref[...], m_ref[...], v_ref[...])

    spec = pl.BlockSpec((tr, C), lambda i: (i, 0))
    return pl.pallas_call(
        body, grid=(R // tr,), name=name,
        in_specs=[pl.BlockSpec((slots, tp, Cp), lambda i: (0, i, 0)), spec, spec, spec],
        out_specs=[spec] * 4, out_shape=[SDS((R, C), F32)] * 4,
        compiler_params=_params(("arbitrary",)),
    )(parts, w, m, v)


SMALL_ROWS, SMALL_COLS = 16, 1024
SMALL_LAYOUT = (
    ("attn_norm_w", 0, 1, 1024, 1024), ("mlp_norm_w", 1, 1, 1024, 1024), ("q_lat_norm_w", 2, 1, 256, 256),
    ("kv_lat_norm_w", 3, 1, 256, 256), ("q_norm_w", 4, 1, 192, 192), ("k_norm_w", 5, 1, 192, 192),
    ("mla_out_norm_w", 6, 4, 128, 128), ("a_log", 10, 1, 128, 4), ("dt_bias", 11, 1, 128, 4),
    ("gdn_norm_w", 12, 1, 128, 128))
LOSS_ENTRY = ("loss", 13, 1, 128, 128)


def _adamw_replicated(parts, ws, ms, vs):
    n = len(SMALL_LAYOUT)

    def body(*refs):
        p_ref = refs[0]
        w_refs, m_refs, v_refs = refs[1:1 + n], refs[1 + n:1 + 2 * n], refs[1 + 2 * n:1 + 3 * n]
        outs = refs[1 + 3 * n:]
        s = p_ref[0]
        for d in range(1, N_DEV):
            s = s + p_ref[d]
        for i, (_, r0, nr, _, pw) in enumerate(SMALL_LAYOUT):
            g = s[r0:r0 + nr, :pw]
            outs[i][...] = g
            outs[n + i][...], outs[2 * n + i][...], outs[3 * n + i][...] = _adamw(
                g, w_refs[i][...], m_refs[i][...], v_refs[i][...])
        _, r0, nr, gw, _ = LOSS_ENTRY
        outs[4 * n][...] = s[r0:r0 + nr, :gw]

    res = pl.pallas_call(
        body, name="adamw_replicated",
        out_shape=[SDS(w.shape, F32) for w in ws] * 4 + [SDS((1, LANES), F32)],
        compiler_params=_params(),
    )(parts, *ws, *ms, *vs)
    return [res[k * n:(k + 1) * n] for k in range(4)], res[4 * n][0, 0]


COPIES_PER_ARRAY = N_DEV - 1


def _two_level_gather(srcs, outs, send_sems, recv_sems, local_sems=None, stage="all"):
    mx, my, mc = lax.axis_index("x"), lax.axis_index("y"), lax.axis_index("c")
    me, sibling = (mx, my, mc), (mx, my, 1 - mc)
    chips = [(1 - mx, my), (mx, 1 - my), (1 - mx, 1 - my)]
    arrays = range(len(srcs))

    def copy(a, k, block, to, src=None):
        px, py, pc = block
        slot = outs[a].at[4 * px + 2 * py + pc]
        sem = a * COPIES_PER_ARRAY + k
        return pltpu.make_async_remote_copy(
            src_ref=slot if src is None else src, dst_ref=slot,
            send_sem=send_sems.at[sem], recv_sem=recv_sems.at[sem], device_id=to, device_id_type=MESH_ID)

    mine = [] if local_sems is None else [
        pltpu.make_async_copy(srcs[a], outs[a].at[4 * mx + 2 * my + mc], local_sems.at[a]) for a in arrays]
    first = []
    for a in arrays:
        first.append(copy(a, 0, me, sibling, src=srcs[a]))
        first += [copy(a, 1 + j, me, (*chip, mc), src=srcs[a]) for j, chip in enumerate(chips)]
    forwards = [copy(a, 4 + j, (*chip, mc), sibling) for j, chip in enumerate(chips) for a in arrays]
    if stage in ("all", "start"):
        for cp in mine + first:
            cp.start()
    if stage in ("all", "forward"):
        for j, chip in enumerate(chips):
            for a in arrays:
                copy(a, 1 + j, (*chip, mc), me).wait_recv()
                forwards[j * len(srcs) + a].start()
    if stage in ("all", "finish"):
        for a in arrays:
            copy(a, 0, sibling, me).wait_recv()
        for j, chip in enumerate(chips):
            for a in arrays:
                copy(a, 4 + j, (*chip, 1 - mc), me).wait_recv()
        for cp in first + forwards:
            cp.wait_send()
        for cp in mine:
            cp.wait()


def _comm_scratch(n):
    return [pltpu.SemaphoreType.DMA((n * COPIES_PER_ARRAY,)), pltpu.SemaphoreType.DMA((n * COPIES_PER_ARRAY,)),
            pltpu.SemaphoreType.DMA((n,))]


def _any_specs(n):
    return [pl.BlockSpec(memory_space=pl.ANY)] * n


def _gather_weights(shards):
    n = len(shards)

    def body(*refs):
        _two_level_gather(refs[:n], refs[n:2 * n], *refs[2 * n:])

    return pl.pallas_call(
        body, name="gather_weights",
        out_shape=[SDS((N_DEV,) + s.shape, s.dtype) for s in shards],
        in_specs=_any_specs(n), out_specs=_any_specs(n), scratch_shapes=_comm_scratch(n),
    )(*shards)


def _gather_small_grads(gs, loss_lanes):
    gs = list(gs) + [loss_lanes]
    n = len(gs)

    def body(*refs):
        g_refs, out_ref = refs[:n], refs[n]
        tile, send_sems, recv_sems = refs[n + 1:]
        tile[...] = jnp.zeros_like(tile)
        for (_, r0, nr, gw, _), g in zip(SMALL_LAYOUT + (LOSS_ENTRY,), g_refs):
            tile[r0:r0 + nr, 0:gw] = g[...]
        me = 4 * lax.axis_index("x") + 2 * lax.axis_index("y") + lax.axis_index("c")
        out_ref[me] = tile[...]
        _two_level_gather([tile], [out_ref], send_sems, recv_sems)

    return pl.pallas_call(
        body, name="gather_small_grads",
        out_shape=SDS((N_DEV, SMALL_ROWS, SMALL_COLS), F32),
        in_specs=[pl.BlockSpec(memory_space=pltpu.VMEM)] * n,
        out_specs=pl.BlockSpec(memory_space=pltpu.VMEM),
        scratch_shapes=[pltpu.VMEM((SMALL_ROWS, SMALL_COLS), F32),
                        pltpu.SemaphoreType.DMA((COPIES_PER_ARRAY,)), pltpu.SemaphoreType.DMA((COPIES_PER_ARRAY,))],
    )(*gs)


def _exchange_grads_two_level(big, small):
    _, R, C = big.shape
    chip_flips = ((1, 0), (0, 1), (1, 1))

    def body(big_ref, small_ref, out_ref, small_out, mine_v, sib_v, pre_v, d2d_send, d2d_recv, ici_send, ici_recv,
             local_sems, s_send, s_recv, s_local):
        mx, my, mc = lax.axis_index("x"), lax.axis_index("y"), lax.axis_index("c")
        sibling = (mx, my, 1 - mc)
        chips = [(px, py) for px in range(2) for py in range(2)]
        _exchange([small_ref], [small_out], s_send, s_recv, s_local, stage="start")
        own = [pltpu.make_async_copy(big_ref.at[4 * px + 2 * py + mc], mine_v.at[q], local_sems.at[q])
               for q, (px, py) in enumerate(chips)]
        d2d = [pltpu.make_async_remote_copy(
            src_ref=big_ref.at[4 * px + 2 * py + (1 - mc)], dst_ref=sib_v.at[q], send_sem=d2d_send.at[q],
            recv_sem=d2d_recv.at[q], device_id=sibling, device_id_type=MESH_ID) for q, (px, py) in enumerate(chips)]
        for cp in own + d2d:
            cp.start()
        for cp in own + d2d:
            cp.wait()
        for q in range(4):
            pre_v[q] = (mine_v[q].astype(F32) + sib_v[q].astype(F32)).astype(pre_v.dtype)
        ici = []
        for k, (fx, fy) in enumerate(chip_flips):
            px = 1 - mx if fx else mx
            py = 1 - my if fy else my
            ici.append(pltpu.make_async_remote_copy(
                src_ref=pre_v.at[2 * px + py], dst_ref=out_ref.at[k], send_sem=ici_send.at[k],
                recv_sem=ici_recv.at[k], device_id=(px, py, mc), device_id_type=MESH_ID))
        keep = pltpu.make_async_copy(pre_v.at[2 * mx + my], out_ref.at[3], local_sems.at[4])
        for cp in ici + [keep]:
            cp.start()
        for cp in ici + [keep]:
            cp.wait()
        _exchange([small_ref], [small_out], s_send, s_recv, s_local, stage="finish")

    dma = pltpu.SemaphoreType.DMA
    return pl.pallas_call(
        body, name="exchange_grads",
        out_shape=[SDS((4, R, C), big.dtype), SDS(small.shape, small.dtype)],
        in_specs=_any_specs(2), out_specs=_any_specs(2),
        scratch_shapes=[pltpu.VMEM((4, R, C), big.dtype)] * 3 + [dma((4,)), dma((4,)), dma((3,)), dma((3,)), dma((5,))]
                       + _comm_scratch(1),
        compiler_params=_params(),
    )(big, small)


class _Transfer:
    def __init__(self, kind, arrays):
        self.kind, self.arrays, self.n = kind, list(arrays), len(arrays)

    def out_shapes(self):
        if self.kind == "gather":
            return [SDS((N_DEV,) + a.shape, a.dtype) for a in self.arrays]
        return [SDS(a.shape, a.dtype) for a in self.arrays]

    def run(self, srcs, outs, sems, stage):
        fn = _two_level_gather if self.kind == "gather" else _exchange
        fn(srcs, outs, *sems, stage=stage)


def _call_beside(body, transfer, *, grid, in_specs, out_specs, out_shape, scratch_shapes, name, semantics, args):
    if transfer is None:
        res = pl.pallas_call(body, grid=grid, in_specs=in_specs, out_specs=out_specs, out_shape=out_shape,
                             scratch_shapes=scratch_shapes, name=name, compiler_params=_params(semantics))(*args)
        return list(res), []
    n_in, n_out, n_s, n = len(in_specs), len(out_specs), len(scratch_shapes), transfer.n
    total = functools.reduce(lambda a, b: a * b, grid, 1)

    def wrapped(*refs):
        ins, refs = refs[:n_in], refs[n_in:]
        t_in, refs = refs[:n], refs[n:]
        outs, refs = refs[:n_out], refs[n_out:]
        t_out, refs = refs[:n], refs[n:]
        scratch, sems = refs[:n_s], refs[n_s:]
        first = functools.reduce(jnp.logical_and, [pl.program_id(i) == 0 for i in range(len(grid))])
        last = functools.reduce(jnp.logical_and, [pl.program_id(i) == g - 1 for i, g in enumerate(grid)])

        @pl.when(first)
        def _():
            transfer.run(t_in, t_out, sems, "start")

        step = functools.reduce(lambda acc, ig: acc * ig[1] + pl.program_id(ig[0]), enumerate(grid), 0)

        @pl.when(step == (3 * total) // 4)
        def _():
            transfer.run(t_in, t_out, sems, "forward")

        body(*ins, *outs, *scratch)

        @pl.when(last)
        def _():
            transfer.run(t_in, t_out, sems, "finish")

    res = pl.pallas_call(
        wrapped, grid=grid, in_specs=list(in_specs) + _any_specs(n), out_specs=list(out_specs) + _any_specs(n),
        out_shape=list(out_shape) + transfer.out_shapes(), scratch_shapes=list(scratch_shapes) + _comm_scratch(n),
        name=name, compiler_params=_params(semantics))(*args, *transfer.arrays)
    return list(res[:n_out]), list(res[n_out:])


EXCHANGE_FLIPS = ((0, 0, 1), (1, 0, 0), (0, 1, 0), (1, 1, 0), (1, 0, 1), (0, 1, 1), (1, 1, 1))


def _exchange(srcs, outs, send_sems, recv_sems, local_sems, stage="all"):
    mx, my, mc = lax.axis_index("x"), lax.axis_index("y"), lax.axis_index("c")
    arrays = range(len(srcs))
    copies = [pltpu.make_async_copy(srcs[a].at[4 * mx + 2 * my + mc], outs[a].at[N_DEV - 1], local_sems.at[a])
              for a in arrays]
    for k, (fx, fy, fc) in enumerate(EXCHANGE_FLIPS):
        px = 1 - mx if fx else mx
        py = 1 - my if fy else my
        pc = 1 - mc if fc else mc
        for a in arrays:
            sem = a * COPIES_PER_ARRAY + k
            copies.append(pltpu.make_async_remote_copy(
                src_ref=srcs[a].at[4 * px + 2 * py + pc], dst_ref=outs[a].at[k],
                send_sem=send_sems.at[sem], recv_sem=recv_sems.at[sem],
                device_id=(px, py, pc), device_id_type=MESH_ID))
    if stage in ("all", "start"):
        for cp in copies:
            cp.start()
    if stage in ("all", "finish"):
        for cp in copies:
            cp.wait()


def _w_in_to_padded(w):
    z = lambda n: jnp.zeros((w.shape[0], n), w.dtype)
    return jnp.concatenate([w[:, O_GQKV:O_GZ], w[:, O_GZ:O_GAB], w[:, O_QLAT:O_KVLAT], w[:, O_KVLAT:O_KPE],
                            w[:, O_KPE:O_GQKV], z(P_GAB - P_KPE - ROPE), w[:, O_GAB:O_END],
                            z(P_WIDTH - P_GAB - (O_END - O_GAB))], axis=1)


def _w_in_from_padded(wp):
    return jnp.concatenate([wp[:, P_QLAT:P_QLAT + 256], wp[:, P_KVLAT:P_KVLAT + 256], wp[:, P_KPE:P_KPE + ROPE],
                            wp[:, P_GQKV:P_GZ], wp[:, P_GZ:P_QLAT], wp[:, P_GAB:P_GAB + (O_END - O_GAB)]], axis=1)


W_IN_SHARD_COLS = (O_END - O_QLAT) // N_DEV


def _w_in_shards_to_padded(stack):
    _, R, Cw = stack.shape
    tr = min(R, 256)

    def body(s_ref, o_ref):
        full = jnp.concatenate([s_ref[d].astype(F32)[:, :W_IN_SHARD_COLS] for d in range(N_DEV)], axis=-1)
        o_ref[...] = _w_in_to_padded(full).astype(o_ref.dtype)

    return pl.pallas_call(
        body, grid=(R // tr,), name="w_in_to_padded",
        in_specs=[pl.BlockSpec((N_DEV, tr, Cw), lambda i: (0, i, 0))],
        out_specs=pl.BlockSpec((tr, P_WIDTH), lambda i: (i, 0)),
        out_shape=SDS((R, P_WIDTH), stack.dtype), compiler_params=_params(("arbitrary",)),
    )(stack)


def _w_in_padded_to_slabs(gp, wire_cols):
    R = gp.shape[0]
    tr = min(R, 256)

    def body(g_ref, o_ref):
        orig = _w_in_from_padded(g_ref[...].astype(F32))
        for d in range(N_DEV):
            piece = orig[:, d * W_IN_SHARD_COLS:(d + 1) * W_IN_SHARD_COLS]
            o_ref[d] = _pad2(piece, tr, wire_cols).astype(o_ref.dtype)

    return pl.pallas_call(
        body, grid=(R // tr,), name="w_in_to_slabs",
        in_specs=[pl.BlockSpec((tr, P_WIDTH), lambda i: (i, 0))],
        out_specs=pl.BlockSpec((N_DEV, tr, wire_cols), lambda i: (0, i, 0)),
        out_shape=SDS((N_DEV, R, wire_cols), gp.dtype), compiler_params=_params(("arbitrary",)),
    )(gp)


def _w_uq_to_headsplit(w):
    w3 = w.reshape(w.shape[0], MLA_HEADS, QK_DIM)
    return jnp.concatenate([w3[:, :, :NOPE].reshape(w.shape[0], -1), w3[:, :, NOPE:].reshape(w.shape[0], -1)], axis=1)


def _w_uq_from_headsplit(wp):
    n = wp[:, :MLA_HEADS * NOPE].reshape(wp.shape[0], MLA_HEADS, NOPE)
    p = wp[:, MLA_HEADS * NOPE:].reshape(wp.shape[0], MLA_HEADS, ROPE)
    return jnp.concatenate([n, p], axis=2).reshape(wp.shape[0], -1)


def _lane_vec(v4):
    return jnp.pad(v4.reshape(1, -1), ((0, 0), (0, LANES - v4.shape[-1])))


def _local_step(x, positions, target, attn_norm_w, w_in, q_lat_norm_w, w_uq, kv_lat_norm_w, w_ukv, q_norm_w,
                k_norm_w, mla_out_norm_w, conv_w, a_log, dt_bias, gdn_norm_w, w_out, mlp_norm_w, w_up, w_down,
                late_shards=None, exchange=False):
    B, S, D = x.shape
    T = B * S
    x2 = x.reshape(T, D)
    t2 = target.reshape(T, D)
    half = ROPE // 2
    inv_freq = ROPE_THETA ** (-jnp.arange(half, dtype=F32) / half)
    ang = positions.reshape(T, 1).astype(F32) * inv_freq
    cosf = jnp.concatenate([jnp.cos(ang)] * 2, axis=-1)
    sinf = jnp.concatenate([jnp.sin(ang)] * 2, axis=-1)
    w_in_p = w_in
    w_uq_p = _w_uq_to_headsplit(w_uq)
    alog_l, dt_l = _lane_vec(a_log), _lane_vec(dt_bias)
    w_an, w_qln, w_kvln, qnw, knw, w_mn, gdn_w = (
        attn_norm_w, q_lat_norm_w, kv_lat_norm_w, q_norm_w, k_norm_w, mlp_norm_w, gdn_norm_w)

    proj, xn, qg, kg, vg, gates = _in_proj(x2, w_an, w_in_p, conv_w, alog_l, dt_l, S)
    def gathering(shards):
        return None if late_shards is None else _Transfer("gather", shards)

    (q4, k4, v4), late = _mla_pre(proj, cosf, sinf, w_qln, w_kvln, w_uq_p, w_ukv, qnw, knw,
                                  gathering(late_shards and late_shards[:1]))
    if late:
        w_out = late[0].reshape(-1, D)
    (o_mla, lse), late = _attn_fwd(q4, k4, v4, B, S, gathering(late_shards and late_shards[2:]))
    if late:
        w_down = late[0].reshape(-1, D)
    (o_gdn, states, ainv, u4, w4), late = _gdn_fwd(qg, kg, vg, gates, B, S,
                                                   gathering(late_shards and late_shards[1:2]))
    if late:
        w_up = late[0]
    h2, mix = _mix_out(o_mla, o_gdn, proj, x2, mla_out_norm_w, gdn_w, w_out)
    up, hn, dy, sq, dyb = _mlp_fwd(h2, w_mn, w_up, w_down, t2)
    loss = (0.5 / D) * jnp.sum(sq[:, 0, 0])

    first = ("w_down",)
    second = ("w_up",)
    third = ("w_out", "w_uq", "w_ukv")
    mats = dict(w_down=_wgrad_stream(up, dyb, "wgrad_down", 512, stream_a=True, a_map=_relu_squared))

    def sending(names):
        return _Transfer("exchange", [_slabs(n, mats[n]) for n in names]) if exchange else None

    (dh, dhb, dup, d_mlp_norm), got = _mlp_bwd(dy, dyb, up, h2, w_mn, w_up, w_down, sending(first))
    mats.update(zip(first, got))
    mats.update(w_up=_wgrad_stream(hn, dup, "wgrad_up", D_FF // N_DEV, column_shards=True))
    do_mla, do_gdn, dz, d_mla_w, d_gdn_w, delta = _mix_bwd(dhb, o_mla, o_gdn, proj, mla_out_norm_w, gdn_w, w_out)
    mats.update(w_out=_wgrad(mix, dhb, "wgrad_out"))
    (dq4, dk4, dv4), got = _attn_bwd(q4, k4, v4, do_mla, delta, lse, B, S, sending(second))
    mats.update(zip(second, got))
    (dql, dkvl, dkpe, dqraw, dkvraw, qn, kvn, d_wqln, d_wkvln, d_qnw, d_knw), _ = _mla_pre_bwd(
        proj, cosf, sinf, w_qln, w_kvln, w_uq_p, w_ukv, qnw, knw, dq4, dk4, dv4)
    mats.update(w_uq=_wgrad(qn, dqraw, "wgrad_uq"), w_ukv=_wgrad(kvn, dkvraw, "wgrad_ukv"))
    (dqg, dkg, dvg, dgb4), got = _gdn_bwd(qg, kg, vg, gates, states, ainv, u4, w4, do_gdn, B, S, sending(third))
    mats.update(zip(third, got))
    dc, dgab, g_conv, d_alog, d_dt = _gdn_pre_bwd(proj, conv_w, alog_l, dt_l, dqg, dkg, dvg, dgb4, S)
    grad_x2, dproj, d_attn_norm = _in_proj_bwd(dc, conv_w, dz, dql, dkvl, dkpe, dgab, w_in_p, dh, x2, w_an, S)
    mats.update(w_in=_wgrad_stream(xn, dproj, "wgrad_in", 256), conv_w=g_conv)
    if exchange:
        last = ("w_in", "conv_w")
        mats.update(zip(last, _exchange_grads_two_level(*[_slabs(n, mats[n]) for n in last])))
    small = dict(attn_norm_w=d_attn_norm, mlp_norm_w=d_mlp_norm, q_lat_norm_w=d_wqln, kv_lat_norm_w=d_wkvln,
                 q_norm_w=d_qnw, k_norm_w=d_knw, mla_out_norm_w=d_mla_w, a_log=d_alog, dt_bias=d_dt,
                 gdn_norm_w=d_gdn_w)
    return loss, grad_x2.reshape(B, S, D), mats, [small[n] for n, *_ in SMALL_LAYOUT]


BIG = ("w_in", "w_uq", "w_ukv", "conv_w", "w_out", "w_up", "w_down")
ALL_W = ("attn_norm_w", "w_in", "q_lat_norm_w", "w_uq", "kv_lat_norm_w", "w_ukv", "q_norm_w", "k_norm_w",
         "mla_out_norm_w", "conv_w", "a_log", "dt_bias", "gdn_norm_w", "w_out", "mlp_norm_w", "w_up", "w_down")
WIRE_SHAPE = {"w_in": (1024, 384), "w_uq": (256, 128), "conv_w": (16, 256)}


def _pad2(a, rows, cols):
    return jnp.pad(a, [(0, 0)] * (a.ndim - 2) + [(0, rows - a.shape[-2]), (0, cols - a.shape[-1])])


def _cols_to_full(stack, cols):
    return jnp.moveaxis(stack[:, :, :cols], 0, 1).reshape(stack.shape[1], N_DEV * cols)


def _full_to_cols(full, wire_cols):
    r, n = full.shape
    return _pad2(jnp.moveaxis(full.reshape(r, N_DEV, n // N_DEV), 1, 0), r, wire_cols)


def _slabs(name, g):
    if name == "w_in":
        return _w_in_padded_to_slabs(g, WIRE_SHAPE["w_in"][1])
    if name == "w_uq":
        return _full_to_cols(_w_uq_from_headsplit(g), WIRE_SHAPE["w_uq"][1])
    if name == "w_ukv":
        return _full_to_cols(g, g.shape[1] // N_DEV)
    if name == "conv_w":
        return _pad2(_full_to_cols(g.astype(WIRE_DTYPE), g.shape[1] // N_DEV), *WIRE_SHAPE["conv_w"])
    if name == "w_up":
        return g
    return g.reshape(N_DEV, -1, g.shape[-1])


def kernel(x, positions, attn_norm_w, w_in, q_lat_norm_w, w_uq, kv_lat_norm_w, w_ukv, q_norm_w, k_norm_w, mla_out_norm_w, conv_w, a_log, dt_bias, gdn_norm_w, w_out, mlp_norm_w, w_up, w_down, loss_target, m_attn_norm_w, m_w_in, m_q_lat_norm_w, m_w_uq, m_kv_lat_norm_w, m_w_ukv, m_q_norm_w, m_k_norm_w, m_mla_out_norm_w, m_conv_w, m_a_log, m_dt_bias, m_gdn_norm_w, m_w_out, m_mlp_norm_w, m_w_up, m_w_down, v_attn_norm_w, v_w_in, v_q_lat_norm_w, v_w_uq, v_kv_lat_norm_w, v_w_ukv, v_q_norm_w, v_k_norm_w, v_mla_out_norm_w, v_conv_w, v_a_log, v_dt_bias, v_gdn_norm_w, v_w_out, v_mlp_norm_w, v_w_up, v_w_down):
    env = dict(locals())
    W = {n: env[n][0] for n in ALL_W}
    Mo = {n: env["m_" + n][0] for n in ALL_W}
    Vo = {n: env["v_" + n][0] for n in ALL_W}

    two_d = lambda a: a.reshape(1, -1) if a.ndim == 1 else a
    D = x.shape[-1]

    s_in, s_uq, s_ukv, s_conv = _gather_weights([
        _pad2(W["w_in"].astype(WIRE_DTYPE), *WIRE_SHAPE["w_in"]),
        _pad2(W["w_uq"].astype(WIRE_DTYPE), *WIRE_SHAPE["w_uq"]),
        W["w_ukv"].astype(WIRE_DTYPE), _pad2(W["conv_w"], *WIRE_SHAPE["conv_w"])])
    late = [W["w_out"].astype(WIRE_DTYPE), W["w_up"].astype(WIRE_DTYPE), W["w_down"].astype(WIRE_DTYPE)]

    loss, grad_x, parts, gs = _local_step(
        x, positions, loss_target, two_d(W["attn_norm_w"]), _w_in_shards_to_padded(s_in),
        two_d(W["q_lat_norm_w"]), _cols_to_full(s_uq, W["w_uq"].shape[1]), two_d(W["kv_lat_norm_w"]),
        _cols_to_full(s_ukv, W["w_ukv"].shape[1]), two_d(W["q_norm_w"]), two_d(W["k_norm_w"]),
        W["mla_out_norm_w"], _cols_to_full(s_conv[:, :CONV_W], W["conv_w"].shape[1]), two_d(W["a_log"]),
        two_d(W["dt_bias"]), two_d(W["gdn_norm_w"]), None, two_d(W["mlp_norm_w"]), None, None,
        late_shards=late, exchange=True)
    done = {n: _reduce_adamw(parts[n], W[n], Mo[n], Vo[n], "adamw_" + n) for n in BIG}
    names = [n for n, *_ in SMALL_LAYOUT]
    tiles = _gather_small_grads(gs, jnp.full((1, LANES), loss, F32))
    small, loss = _adamw_replicated(tiles, [two_d(W[n]) for n in names], [two_d(Mo[n]) for n in names],
                                    [two_d(Vo[n]) for n in names])
    for i, n in enumerate(names):
        done[n] = [small[kind][i] for kind in range(4)]
    res = [done[n][kind].reshape(env[n].shape) for kind in range(4) for n in ALL_W]
    return (loss, grad_x, *res)
```

```python
import functools

import jax
import jax.numpy as jnp
from jax import lax
from jax.experimental import pallas as pl
from jax.experimental.pallas import tpu as pltpu

F32 = jnp.float32
MXU_DTYPE = jnp.bfloat16
WIRE_DTYPE = jnp.bfloat16
SDS = jax.ShapeDtypeStruct
HIGHEST = lax.Precision.HIGHEST
MESH_ID = pl.DeviceIdType.MESH

D_MODEL = 1024
MLA_HEADS = 4
Q_LORA = 256
KV_LORA = 256
NOPE = 128
ROPE = 64
QK_DIM = NOPE + ROPE
V_DIM = 128
ROPE_THETA = 10000.0
GDN_HEADS = 4
GDN_DIM = 128
GDN_WIDTH = GDN_HEADS * GDN_DIM
CONV_W = 4
CHUNK = 64
D_FF = 4 * D_MODEL
EPS = 1e-6
ATT_SCALE = QK_DIM ** -0.5
GDN_QSCALE = GDN_DIM ** -0.5
N_DEV = 8
ATTN_BLOCK = 512
ATTN_CHAINS = 2
MLP_FWD_SHARDS = 4
MLP_BWD_SHARDS = 4

ADAM_LR = 0.001
ADAM_B1 = 0.9
ADAM_B2 = 0.999
ADAM_EPS = 1e-08
ADAM_WD = 0.01
ADAM_STEP = 10

LANES = 128
SUBLANES = 8
VMEM_LIMIT = 60 * 1024 * 1024

P_GQKV, P_GZ, P_QLAT, P_KVLAT, P_KPE, P_GAB = 0, 1536, 2048, 2304, 2560, 2688
P_WIDTH = 2816
O_QLAT, O_KVLAT, O_KPE, O_GQKV, O_GZ, O_GAB, O_END = 0, 256, 512, 576, 2112, 2624, 2632


def _params(sem=None, vmem=VMEM_LIMIT):
    kw = dict(vmem_limit_bytes=vmem)
    if sem is not None:
        kw["dimension_semantics"] = sem
    return pltpu.CompilerParams(**kw)


def _mm(a, b):
    return jnp.dot(a.astype(MXU_DTYPE), b.astype(MXU_DTYPE), preferred_element_type=F32)


def _mm_nt(a, b):
    return lax.dot_general(a.astype(MXU_DTYPE), b.astype(MXU_DTYPE), (((1,), (1,)), ((), ())),
                           preferred_element_type=F32)


def _mm_tn(a, b):
    return lax.dot_general(a.astype(MXU_DTYPE), b.astype(MXU_DTYPE), (((0,), (0,)), ((), ())),
                           preferred_element_type=F32)


def _split(a):
    hi = a.astype(MXU_DTYPE)
    return hi, (a - hi.astype(F32)).astype(MXU_DTYPE)


def _mm_split(a, b):
    (ah, al), (bh, bl) = a, b
    dot = lambda x, y: jnp.dot(x, y, preferred_element_type=F32)
    if MXU_DTYPE == F32:
        return dot(ah, bh)
    return dot(ah, bh) + dot(ah, bl) + dot(al, bh)


def _mm_exact(a, b):
    return _mm_split(_split(a), _split(b))


def _row_sum(v, on_mxu=False):
    if not on_mxu:
        return jnp.sum(v, axis=-1, keepdims=True)
    d = v.shape[-1]
    ones = jnp.ones((d, LANES), MXU_DTYPE)
    s = sum(jnp.dot(p, ones, preferred_element_type=F32) for p in _split(v))
    return s[:, :d] if d <= LANES else jnp.tile(s, (1, d // LANES))


def _rms(x, w, on_mxu=False):
    r = lax.rsqrt(_row_sum(x * x, on_mxu) * (1.0 / x.shape[-1]) + EPS)
    return x * r * w, r


def _rms_bwd(dy, x, w, r, on_mxu=False):
    xh = x * r
    dyw = dy * w
    dx = r * (dyw - xh * (_row_sum(dyw * xh, on_mxu) * (1.0 / x.shape[-1])))
    dw = jnp.sum(dy * xh, axis=0, keepdims=True)
    return dx, dw


def _l2n(x, scale):
    return x * (lax.rsqrt(_row_sum(x * x) + EPS) * scale)


def _l2n_bwd(dy, x, scale):
    r = lax.rsqrt(_row_sum(x * x) + EPS)
    xh = x * r
    return (scale * r) * (dy - xh * _row_sum(dy * xh))


def _rot(t):
    return jnp.concatenate([-t[:, ROPE // 2:], t[:, :ROPE // 2]], axis=-1)


def _rot_t(t):
    return jnp.concatenate([t[:, ROPE // 2:], -t[:, :ROPE // 2]], axis=-1)


def _rope(t, cos, sin):
    return t * cos + _rot(t) * sin


def _rope_bwd(d, cos, sin):
    return d * cos + _rot_t(d * sin)


def _sigmoid(x):
    return jax.nn.sigmoid(x)


def _shift_down(x, halo, j):
    if j == 0:
        return x
    xr = pltpu.roll(x, j, 0)
    hr = pltpu.roll(halo, j, 0)
    row = lax.broadcasted_iota(jnp.int32, halo.shape, 0)
    top = jnp.where(row < j, hr, xr[:SUBLANES])
    return jnp.concatenate([top, xr[SUBLANES:]], axis=0)


def _shift_up(x, nxt, j):
    if j == 0:
        return x
    n = x.shape[0]
    xr = pltpu.roll(x, n - j, 0)
    nr = pltpu.roll(nxt, SUBLANES - j, 0)
    row = lax.broadcasted_iota(jnp.int32, nxt.shape, 0)
    bot = jnp.where(row >= SUBLANES - j, nr, xr[n - SUBLANES:])
    return jnp.concatenate([xr[:n - SUBLANES], bot], axis=0)


def _chunk_cumsum(y, row_in_chunk):
    s = 1
    while s < CHUNK:
        y = y + jnp.where(row_in_chunk >= s, pltpu.roll(y, s, 0), 0.0)
        s *= 2
    return y


def _chunk_rev_cumsum(y, row_in_chunk):
    n = y.shape[0]
    s = 1
    while s < CHUNK:
        y = y + jnp.where(row_in_chunk + s < CHUNK, pltpu.roll(y, n - s, 0), 0.0)
        s *= 2
    return y


def _together(generators):
    alive = list(generators)
    while alive:
        nxt = []
        for g in alive:
            try:
                next(g)
                nxt.append(g)
            except StopIteration:
                pass
        alive = nxt
        yield


def _lockstep(generators):
    for _ in _together(generators):
        pass


def _pick_lane(tile, lane, idx):
    return jnp.sum(jnp.where(lane == idx, tile, 0.0), axis=-1, keepdims=True)


def _divisor_tile(n, cap, unit=LANES):
    best = unit
    t = unit
    while t <= min(n, cap):
        if n % t == 0:
            best = t
        t += unit
    return n if n <= cap else best


def _in_proj(x2, w_an, w_in_p, conv_w, alog_l, dt_l, S):
    T, D = x2.shape
    N = w_in_p.shape[1]
    tm = min(512, S)
    assert S % tm == 0 and T % tm == 0, "a token tile must not straddle two sequences"
    tiles_per_seq = S // tm
    C3 = 3 * GDN_WIDTH
    H = GDN_HEADS

    def body(x_ref, wn_ref, w_ref, cw_ref, alog_ref, dt_ref, proj_ref, xn_ref, q_out, k_out, v_out, gates_out,
             halo_s):
        xn, _ = _rms(x_ref[...], wn_ref[...])
        xn = xn.astype(MXU_DTYPE)
        xn_ref[...] = xn
        proj = jnp.dot(xn, w_ref[...], preferred_element_type=F32)
        proj_ref[...] = proj
        u = proj[:, P_GQKV:P_GQKV + C3]

        @pl.when(pl.program_id(0) == 0)
        def _():
            halo_s[...] = jnp.zeros_like(halo_s)

        halo = jnp.where(pl.program_id(0) % tiles_per_seq == 0, 0.0, halo_s[...])
        halo_s[...] = u[tm - SUBLANES:, :]
        c, _ = _conv_taps(u, halo, cw_ref[...])
        a = c * _sigmoid(c)
        for h in range(H):
            xq = a[:, h * GDN_DIM:(h + 1) * GDN_DIM]
            xk = a[:, GDN_WIDTH + h * GDN_DIM:GDN_WIDTH + (h + 1) * GDN_DIM]
            q_out[h] = _l2n(xq, GDN_QSCALE)
            k_out[h] = _l2n(xk, 1.0)
            v_out[h] = a[:, 2 * GDN_WIDTH + h * GDN_DIM:2 * GDN_WIDTH + (h + 1) * GDN_DIM]
        lane = lax.broadcasted_iota(jnp.int32, (tm, LANES), 1)
        ric = lax.broadcasted_iota(jnp.int32, (tm, LANES), 0) % CHUNK
        g, beta = _gate_values(proj[:, P_GAB:P_GAB + LANES], alog_ref[...], dt_ref[...], lane)
        gates_out[...] = _chunk_cumsum(g, ric) + beta

    hspec = pl.BlockSpec((H, tm, GDN_DIM), lambda i: (0, i, 0))
    vec = pl.BlockSpec((1, LANES), lambda i: (0, 0))
    return pl.pallas_call(
        body, grid=(T // tm,), name="in_proj",
        in_specs=[pl.BlockSpec((tm, D), lambda i: (i, 0)), pl.BlockSpec((1, D), lambda i: (0, 0)),
                  pl.BlockSpec((D, N), lambda i: (0, 0)), pl.BlockSpec((CONV_W, C3), lambda i: (0, 0)), vec, vec],
        out_specs=[pl.BlockSpec((tm, N), lambda i: (i, 0)), pl.BlockSpec((tm, D), lambda i: (i, 0)),
                   hspec, hspec, hspec, pl.BlockSpec((tm, LANES), lambda i: (i, 0))],
        out_shape=[SDS((T, N), F32), SDS((T, D), MXU_DTYPE)] + [SDS((H, T, GDN_DIM), F32)] * 3
                  + [SDS((T, LANES), F32)],
        scratch_shapes=[pltpu.VMEM((SUBLANES, C3), F32)],
        compiler_params=_params(("arbitrary",)),
    )(x2, w_an, w_in_p, conv_w, alog_l, dt_l)


def _mla_pre(proj, cosf, sinf, w_qln, w_kvln, w_uq_p, w_ukv, qnw, knw, transfer=None):
    T = proj.shape[0]
    tm = min(256, T)
    H = MLA_HEADS

    def body(ql_ref, kvl_ref, kpe_ref, cos_ref, sin_ref, wq_ref, wkv_ref, uq_ref, ukv_ref, qnw_ref, knw_ref,
             q_out, k_out, v_out):
        rms = functools.partial(_rms, on_mxu=True)
        cos, sin = cos_ref[...], sin_ref[...]
        qnw_, knw_ = qnw_ref[...], knw_ref[...]
        qn, _ = rms(ql_ref[...], wq_ref[...])
        kvn, _ = rms(kvl_ref[...], wkv_ref[...])
        qraw = _mm(qn, uq_ref[...])
        kvraw = _mm(kvn, ukv_ref[...])
        kpe = _rope(rms(kpe_ref[...][:, :ROPE], knw_[:, NOPE:])[0], cos, sin)
        for h in range(H):
            qn_h = rms(qraw[:, h * NOPE:(h + 1) * NOPE], qnw_[:, :NOPE])[0]
            qp_h = _rope(rms(qraw[:, H * NOPE + h * ROPE:H * NOPE + (h + 1) * ROPE], qnw_[:, NOPE:])[0], cos, sin)
            q_out[h] = (jnp.concatenate([qn_h, qp_h], axis=-1) * ATT_SCALE).astype(MXU_DTYPE)
            kn_h = rms(kvraw[:, h * 256:h * 256 + NOPE], knw_[:, :NOPE])[0]
            k_out[h] = jnp.concatenate([kn_h, kpe], axis=-1).astype(MXU_DTYPE)
            v_out[h] = kvraw[:, h * 256 + NOPE:(h + 1) * 256].astype(MXU_DTYPE)

    full = lambda a: pl.BlockSpec(a.shape, lambda i: (0,) * a.ndim)
    return _call_beside(
        body, transfer, grid=(T // tm,), name="mla_pre", scratch_shapes=[], semantics=("arbitrary",),
        args=(proj, proj, proj, cosf, sinf, w_qln, w_kvln, w_uq_p, w_ukv, qnw, knw),
        in_specs=[pl.BlockSpec((tm, 256), lambda i: (i, P_QLAT // 256)),
                  pl.BlockSpec((tm, 256), lambda i: (i, P_KVLAT // 256)),
                  pl.BlockSpec((tm, 128), lambda i: (i, P_KPE // 128)),
                  pl.BlockSpec((tm, ROPE), lambda i: (i, 0)), pl.BlockSpec((tm, ROPE), lambda i: (i, 0)),
                  full(w_qln), full(w_kvln), full(w_uq_p), full(w_ukv), full(qnw), full(knw)],
        out_specs=[pl.BlockSpec((H, tm, QK_DIM), lambda i: (0, i, 0)),
                   pl.BlockSpec((H, tm, QK_DIM), lambda i: (0, i, 0)),
                   pl.BlockSpec((H, tm, V_DIM), lambda i: (0, i, 0))],
        out_shape=[SDS((H, T, QK_DIM), MXU_DTYPE), SDS((H, T, QK_DIM), MXU_DTYPE), SDS((H, T, V_DIM), MXU_DTYPE)])


def _attn_fwd(q4, k4, v4, B, S, transfer=None):
    H = MLA_HEADS
    bq = min(ATTN_BLOCK, S)
    nq = S // bq
    rows = bq // ATTN_CHAINS

    def body(q_ref, k_ref, v_ref, o_ref, lse_ref):
        col = lax.broadcasted_iota(jnp.int32, (rows, bq), 1)
        row = lax.broadcasted_iota(jnp.int32, (rows, bq), 0)

        def q_step(qi, carry):
            qs = pl.multiple_of(qi * bq, bq)
            qsub = [q_ref[0, pl.ds(qs + j * rows, rows), :] for j in range(ATTN_CHAINS)]

            def k_block(ks, cs, diagonal):
                k = k_ref[0, pl.ds(ks, bq), :]
                v = v_ref[0, pl.ds(ks, bq), :]
                out = [None] * ATTN_CHAINS

                def chain(j):
                    m, l, acc = cs[j]
                    s = _mm_nt(qsub[j], k)
                    yield
                    if diagonal:
                        s = jnp.where(col <= row + j * rows, s, -jnp.inf)
                    m_new = jnp.maximum(m, jnp.max(s, axis=-1, keepdims=True))
                    p = jnp.exp(s - m_new)
                    a = jnp.exp(m - m_new)
                    l_new = a * l + jnp.sum(p, axis=-1, keepdims=True)
                    yield
                    out[j] = (m_new, l_new, a * acc + _mm(p, v))

                _lockstep([chain(j) for j in range(ATTN_CHAINS)])
                return tuple(out)

            init = tuple((jnp.full((rows, 1), -jnp.inf, F32), jnp.zeros((rows, 1), F32),
                          jnp.zeros((rows, V_DIM), F32)) for _ in range(ATTN_CHAINS))
            cs = lax.fori_loop(0, qi, lambda kj, c: k_block(pl.multiple_of(kj * bq, bq), c, False), init)
            for j, (m, l, acc) in enumerate(k_block(qs, cs, True)):
                o_ref[0, pl.ds(qs + j * rows, rows), :] = acc / l
                lse_ref[0, pl.ds(qs + j * rows, rows), :] = m + jnp.log(l)
            return carry

        lax.fori_loop(0, nq, q_step, 0)

    spec = lambda d: pl.BlockSpec((1, S, d), lambda h, b: (h, b, 0))
    return _call_beside(
        body, transfer, grid=(H, B), name="attn_fwd",
        in_specs=[spec(QK_DIM), spec(QK_DIM), spec(V_DIM)],
        out_specs=[spec(V_DIM), spec(1)],
        out_shape=[SDS((H, B * S, V_DIM), F32), SDS((H, B * S, 1), F32)],
        scratch_shapes=[], semantics=("arbitrary", "arbitrary"), args=(q4, k4, v4))


def _conv_taps(u, halo, w):
    sh = [_shift_down(u, halo, j) for j in range(CONV_W)]
    c = w[0:1] * sh[3] + w[1:2] * sh[2] + w[2:3] * sh[1] + w[3:4] * sh[0]
    return c, sh


def _gate_values(gab, alog_l, dt_l, lane):
    g = -jnp.exp(alog_l) * jax.nn.softplus(gab + dt_l)
    g = jnp.where(lane < GDN_HEADS, g, 0.0)
    beta = jnp.where((lane >= GDN_HEADS) & (lane < 2 * GDN_HEADS), _sigmoid(gab), 0.0)
    return g, beta


def _unit_lower_inverses(Ls, eye):
    Ps = [eye - L for L in Ls]
    Ms = [_split(-L) for L in Ls]
    for _ in range(5):
        sq = [_mm_split(m, m) for m in Ms]
        Ms = [_split(s) for s in sq]
        Ps = [p + _mm_split(_split(p), m) for p, m in zip(Ps, Ms)]
    return Ps


def _chunk_decays(gt, lane, h, ri, ci, rcol):
    Gc = _pick_lane(gt, lane, h)
    bt = _pick_lane(gt, lane, h + GDN_HEADS)
    Gb = jnp.broadcast_to(Gc, (CHUNK, CHUNK))
    Gam = jnp.where(ri >= ci, jnp.exp(Gb - Gb.T), 0.0)
    Gl = jnp.sum(jnp.where(rcol == CHUNK - 1, Gc, 0.0), axis=0, keepdims=True)
    return Gc, bt, Gam, jnp.exp(Gc), jnp.exp(Gl - Gc), jnp.exp(Gl)


GDN_FWD_UNROLL = 16
GDN_BWD_UNROLL = 8
GDN_RECUR_STEPS_PER_STAGE = 2


def _gdn_fwd(qg, kg, vg, gates, B, S, transfer=None):
    H, D, C = GDN_HEADS, GDN_DIM, CHUNK
    NC = S // C
    P = 2 if B % 2 == 0 else 1
    Sb, NCb = P * S, P * NC
    U = GDN_FWD_UNROLL if NCb % GDN_FWD_UNROLL == 0 else 1
    NG = NCb // U

    def body(q_ref, k_ref, v_ref, g_ref, o_ref, st_ref, ai_ref, u_ref, w_ref, q2_s, au_s, bc_s, w2_s, el_s):
        h = pl.program_id(0)
        lane = lax.broadcasted_iota(jnp.int32, (C, LANES), 1)
        ri = lax.broadcasted_iota(jnp.int32, (C, C), 0)
        ci = lax.broadcasted_iota(jnp.int32, (C, C), 1)
        rcol = lax.broadcasted_iota(jnp.int32, (C, 1), 0)
        eye = (ri == ci).astype(F32)

        def group(gi, c):
            ns = [gi * U + j for j in range(U)]
            css = [pl.multiple_of(n * C, C) for n in ns]
            qs = [q_ref[0, pl.ds(cs, C), :] for cs in css]
            ks = [k_ref[0, pl.ds(cs, C), :] for cs in css]
            vs = [v_ref[0, pl.ds(cs, C), :] for cs in css]
            decs = [_chunk_decays(g_ref[pl.ds(cs, C), :], lane, h, ri, ci, rcol) for cs in css]
            qks = [_mm_nt(jnp.concatenate([q, k], axis=0), k) for q, k in zip(qs, ks)]
            ainvs = _unit_lower_inverses(
                [jnp.where(ri > ci, d[1] * qk[C:] * d[2], 0.0) for qk, d in zip(qks, decs)], eye)
            sols = [_mm_exact(a, jnp.concatenate([v * d[1], k * (d[1] * d[3])], axis=-1))
                    for a, k, v, d in zip(ainvs, ks, vs, decs)]
            atuw = [_mm(qk[:C] * d[2], sol) for qk, d, sol in zip(qks, decs, sols)]
            kduw = [_mm_tn(k * d[4], sol) for k, d, sol in zip(ks, decs, sols)]
            for n, cs, q, a, sol, au, ku, (Gc, bt, Gam, e, f, eL) in zip(ns, css, qs, ainvs, sols, atuw, kduw, decs):
                u_ref[0, pl.ds(cs, C), :] = sol[:, :D]
                w_ref[0, pl.ds(cs, C), :] = sol[:, D:]
                au_s[pl.ds(cs, C), :] = au[:, :D]
                q2_s[pl.ds(cs, C), :] = q * e - au[:, D:]
                bc_s[n] = ku[:, :D]
                w2_s[n] = ku[:, D:]
                el_s[n] = jnp.broadcast_to(eL, (SUBLANES, LANES))
                ai_ref[0, n] = a.T
            return c

        lax.fori_loop(0, NG, group, 0)

        def step(n, states):
            new = []
            for p, S_ in enumerate(states):
                m = p * NC + n
                cs = pl.multiple_of(m * C, C)
                o_ref[0, pl.ds(cs, C), :] = _mm(q2_s[pl.ds(cs, C), :], S_) + au_s[pl.ds(cs, C), :]
                st_ref[0, m] = S_
                new.append(S_ * el_s[m, 0:1, :] + bc_s[m] - _mm(w2_s[m], S_))
            return tuple(new)

        lax.fori_loop(0, NC, step, tuple(jnp.zeros((D, D), F32) for _ in range(P)))

    spec = pl.BlockSpec((1, Sb, D), lambda h, b: (h, b, 0))
    return _call_beside(
        body, transfer, grid=(H, B // P), name="gdn_fwd",
        in_specs=[spec, spec, spec, pl.BlockSpec((Sb, LANES), lambda h, b: (b, 0))],
        out_specs=[spec, pl.BlockSpec((1, NCb, D, D), lambda h, b: (h, b, 0, 0)),
                   pl.BlockSpec((1, NCb, C, C), lambda h, b: (h, b, 0, 0)), spec, spec],
        out_shape=[SDS((H, B * S, D), F32), SDS((H, B * NC, D, D), F32), SDS((H, B * NC, C, C), F32),
                   SDS((H, B * S, D), F32), SDS((H, B * S, D), F32)],
        scratch_shapes=[pltpu.VMEM((Sb, D), F32), pltpu.VMEM((Sb, D), F32), pltpu.VMEM((NCb, D, D), F32),
                        pltpu.VMEM((NCb, D, D), F32), pltpu.VMEM((NCb, SUBLANES, LANES), F32)],
        semantics=("arbitrary", "arbitrary"), args=(qg, kg, vg, gates))


def _mix_out(o_mla, o_gdn, proj, x2, mla_w, gdn_w, w_out):
    T, D = x2.shape
    tm = min(512, T)
    H = MLA_HEADS

    def body(om_ref, og_ref, z_ref, x_ref, mw_ref, gw_ref, w_ref, h_ref, mix_ref):
        z = z_ref[...]
        parts = [_rms(om_ref[h], mw_ref[h:h + 1, :])[0] for h in range(H)]
        for h in range(GDN_HEADS):
            zh = z[:, h * GDN_DIM:(h + 1) * GDN_DIM]
            parts.append(_rms(og_ref[h], gw_ref[...])[0] * (zh * _sigmoid(zh)))
        mix = jnp.concatenate(parts, axis=-1).astype(MXU_DTYPE)
        mix_ref[...] = mix
        h_ref[...] = x_ref[...] + jnp.dot(mix, w_ref[...], preferred_element_type=F32)

    hspec = pl.BlockSpec((H, tm, V_DIM), lambda i: (0, i, 0))
    return pl.pallas_call(
        body, grid=(T // tm,), name="mix_out",
        in_specs=[hspec, hspec, pl.BlockSpec((tm, GDN_WIDTH), lambda i: (i, P_GZ // GDN_WIDTH)),
                  pl.BlockSpec((tm, D), lambda i: (i, 0)),
                  pl.BlockSpec((H, V_DIM), lambda i: (0, 0)), pl.BlockSpec((1, GDN_DIM), lambda i: (0, 0)),
                  pl.BlockSpec((D, D), lambda i: (0, 0))],
        out_specs=[pl.BlockSpec((tm, D), lambda i: (i, 0)), pl.BlockSpec((tm, D), lambda i: (i, 0))],
        out_shape=[SDS((T, D), F32), SDS((T, D), MXU_DTYPE)],
        compiler_params=_params(("arbitrary",)),
    )(o_mla, o_gdn, proj, x2, mla_w, gdn_w, w_out)


def _mlp_fwd(h2, w_mn, w_up, w_down, target):
    T, D = h2.shape
    ns, _, ts = w_up.shape
    F = ns * ts
    tm = min(512, T)
    G = MLP_FWD_SHARDS
    tf, nf = G * ts, ns // G

    def body(h_ref, wn_ref, up_w, down_w, t_ref, up_ref, hn_ref, dy_ref, loss_ref, dyb_ref, y_acc):
        j = pl.program_id(1)

        @pl.when(j == 0)
        def _():
            hn_ref[...] = _rms(h_ref[...], wn_ref[...])[0].astype(MXU_DTYPE)
            y_acc[...] = h_ref[...]

        parts = []
        for c in range(G):
            up = jnp.dot(hn_ref[...], up_w[c], preferred_element_type=F32)
            up_ref[:, c * ts:(c + 1) * ts] = up.astype(MXU_DTYPE)
            r = jnp.maximum(up, 0.0)
            parts.append(_mm(r * r, down_w[c * ts:(c + 1) * ts, :]))
        y_acc[...] += functools.reduce(jnp.add, parts)

        @pl.when(j == nf - 1)
        def _():
            err = y_acc[...] - t_ref[...]
            dy_ref[...] = err / D
            dyb_ref[...] = (err / D).astype(MXU_DTYPE)
            loss_ref[...] = jnp.full((1, SUBLANES, LANES), jnp.sum(err * err), F32)

    return pl.pallas_call(
        body, grid=(T // tm, nf), name="mlp_fwd",
        in_specs=[pl.BlockSpec((tm, D), lambda i, j: (i, 0)), pl.BlockSpec((1, D), lambda i, j: (0, 0)),
                  pl.BlockSpec((G, D, ts), lambda i, j: (j, 0, 0)), pl.BlockSpec((tf, D), lambda i, j: (j, 0)),
                  pl.BlockSpec((tm, D), lambda i, j: (i, 0))],
        out_specs=[pl.BlockSpec((tm, tf), lambda i, j: (i, j)), pl.BlockSpec((tm, D), lambda i, j: (i, 0)),
                   pl.BlockSpec((tm, D), lambda i, j: (i, 0)),
                   pl.BlockSpec((1, SUBLANES, LANES), lambda i, j: (i, 0, 0)),
                   pl.BlockSpec((tm, D), lambda i, j: (i, 0))],
        out_shape=[SDS((T, F), MXU_DTYPE), SDS((T, D), MXU_DTYPE), SDS((T, D), F32),
                   SDS((T // tm, SUBLANES, LANES), F32), SDS((T, D), MXU_DTYPE)],
        scratch_shapes=[pltpu.VMEM((tm, D), F32)],
        compiler_params=_params(("arbitrary", "arbitrary")),
    )(h2, w_mn, w_up, w_down, target)


def _mlp_bwd(dy, dyb, up, h2, w_mn, w_up, w_down, transfer=None):
    T, D = h2.shape
    ns, _, ts = w_up.shape
    F = ns * ts
    tm = min(512, T)
    G = MLP_BWD_SHARDS
    tf, nf = G * ts, ns // G

    def body(dy_ref, dyb_ref, up_ref, h_ref, wn_ref, up_w, down_w, dh_ref, dhb_ref, dup_ref, dwn_ref, acc):
        i, j = pl.program_id(0), pl.program_id(1)

        @pl.when((i == 0) & (j == 0))
        def _():
            dwn_ref[...] = jnp.zeros_like(dwn_ref)

        @pl.when(j == 0)
        def _():
            acc[...] = jnp.zeros_like(acc)

        parts = []
        for c in range(G):
            cols = slice(c * ts, (c + 1) * ts)
            r = jnp.maximum(up_ref[:, cols].astype(F32), 0.0)
            dup = (_mm_nt(dyb_ref[...], down_w[cols, :]) * (2.0 * r)).astype(MXU_DTYPE)
            dup_ref[:, cols] = dup
            parts.append(_mm_nt(dup, up_w[c]))
        acc[...] += functools.reduce(jnp.add, parts)

        @pl.when(j == nf - 1)
        def _():
            hv = h_ref[...]
            _, rr = _rms(hv, wn_ref[...])
            dx, dw = _rms_bwd(acc[...], hv, wn_ref[...], rr)
            dh = dy_ref[...] + dx
            dh_ref[...] = dh
            dhb_ref[...] = dh.astype(MXU_DTYPE)
            dwn_ref[...] += dw

    row = lambda i, j: (i, 0)
    return _call_beside(
        body, transfer, grid=(T // tm, nf), name="mlp_bwd",
        in_specs=[pl.BlockSpec((tm, D), row), pl.BlockSpec((tm, D), row), pl.BlockSpec((tm, tf), lambda i, j: (i, j)),
                  pl.BlockSpec((tm, D), row), pl.BlockSpec((1, D), lambda i, j: (0, 0)),
                  pl.BlockSpec((G, D, ts), lambda i, j: (j, 0, 0)), pl.BlockSpec((tf, D), lambda i, j: (j, 0))],
        out_specs=[pl.BlockSpec((tm, D), row), pl.BlockSpec((tm, D), row),
                   pl.BlockSpec((tm, tf), lambda i, j: (i, j)), pl.BlockSpec((1, D), lambda i, j: (0, 0))],
        out_shape=[SDS((T, D), F32), SDS((T, D), MXU_DTYPE), SDS((T, F), MXU_DTYPE), SDS((1, D), F32)],
        scratch_shapes=[pltpu.VMEM((tm, D), F32)], semantics=("arbitrary", "arbitrary"),
        args=(dy, dyb, up, h2, w_mn, w_up, w_down))


def _mix_bwd(dhb, o_mla, o_gdn, proj, mla_w, gdn_w, w_out):
    T, D = dhb.shape
    tm = min(512, T)
    H = MLA_HEADS

    def body(dh_ref, om_ref, og_ref, z_ref, mw_ref, gw_ref, w_ref, dom_ref, dog_ref, dz_ref, dmw_ref, dgw_ref,
             delta_ref):
        @pl.when(pl.program_id(0) == 0)
        def _():
            dmw_ref[...] = jnp.zeros_like(dmw_ref)
            dgw_ref[...] = jnp.zeros_like(dgw_ref)

        dmix = _mm_nt(dh_ref[...], w_ref[...])
        z = z_ref[...]
        dmw, dzs = [], []
        dgw = jnp.zeros((1, GDN_DIM), F32)
        for h in range(H):
            o = om_ref[h]
            w = mw_ref[h:h + 1, :]
            _, r = _rms(o, w)
            dx, dw = _rms_bwd(dmix[:, h * V_DIM:(h + 1) * V_DIM], o, w, r)
            dom_ref[h] = dx.astype(MXU_DTYPE)
            delta_ref[h] = jnp.sum(dx * o, axis=-1, keepdims=True)
            dmw.append(dw)
        for h in range(GDN_HEADS):
            o = og_ref[h]
            w = gw_ref[...]
            zh = z[:, h * GDN_DIM:(h + 1) * GDN_DIM]
            sg = _sigmoid(zh)
            yn, r = _rms(o, w)
            dy = dmix[:, H * V_DIM + h * GDN_DIM:H * V_DIM + (h + 1) * GDN_DIM]
            dzs.append(dy * yn * (sg * (1.0 + zh * (1.0 - sg))))
            dx, dw = _rms_bwd(dy * (zh * sg), o, w, r)
            dog_ref[h] = dx.astype(MXU_DTYPE)
            dgw = dgw + dw
        dz_ref[...] = jnp.concatenate(dzs, axis=-1).astype(MXU_DTYPE)
        dmw_ref[...] += jnp.concatenate(dmw, axis=0)
        dgw_ref[...] += dgw

    hspec = pl.BlockSpec((H, tm, V_DIM), lambda i: (0, i, 0))
    return pl.pallas_call(
        body, grid=(T // tm,), name="mix_bwd",
        in_specs=[pl.BlockSpec((tm, D), lambda i: (i, 0)), hspec, hspec,
                  pl.BlockSpec((tm, GDN_WIDTH), lambda i: (i, P_GZ // GDN_WIDTH)),
                  pl.BlockSpec((H, V_DIM), lambda i: (0, 0)), pl.BlockSpec((1, GDN_DIM), lambda i: (0, 0)),
                  pl.BlockSpec((D, D), lambda i: (0, 0))],
        out_specs=[hspec, hspec, pl.BlockSpec((tm, GDN_WIDTH), lambda i: (i, 0)),
                   pl.BlockSpec((H, V_DIM), lambda i: (0, 0)), pl.BlockSpec((1, GDN_DIM), lambda i: (0, 0)),
                   pl.BlockSpec((H, tm, 1), lambda i: (0, i, 0))],
        out_shape=[SDS((H, T, V_DIM), MXU_DTYPE), SDS((H, T, GDN_DIM), MXU_DTYPE), SDS((T, GDN_WIDTH), MXU_DTYPE),
                   SDS((H, V_DIM), F32), SDS((1, GDN_DIM), F32), SDS((H, T, 1), F32)],
        compiler_params=_params(("arbitrary",)),
    )(dhb, o_mla, o_gdn, proj, mla_w, gdn_w, w_out)


def _attn_bwd(q4, k4, v4, do4, delta4, lse4, B, S, transfer=None):
    H = MLA_HEADS
    bq = min(ATTN_BLOCK, S)
    nq = S // bq
    rows = bq // ATTN_CHAINS

    def body(q_ref, k_ref, v_ref, do_ref, delta_ref, lse_ref, dq_ref, dk_ref, dv_ref):
        dq_ref[...] = jnp.zeros_like(dq_ref)
        dk_ref[...] = jnp.zeros_like(dk_ref)
        dv_ref[...] = jnp.zeros_like(dv_ref)

        col = lax.broadcasted_iota(jnp.int32, (rows, bq), 1)
        row = lax.broadcasted_iota(jnp.int32, (rows, bq), 0)

        def k_step(kj, carry):
            ks = pl.multiple_of(kj * bq, bq)
            k = k_ref[0, pl.ds(ks, bq), :]
            v = v_ref[0, pl.ds(ks, bq), :]

            def q_block(qs, diagonal):
                dks, dvs = [None] * ATTN_CHAINS, [None] * ATTN_CHAINS

                def chain(j):
                    sl = pl.ds(qs + j * rows, rows)
                    q = q_ref[0, sl, :]
                    do = do_ref[0, sl, :].astype(MXU_DTYPE)
                    s = _mm_nt(q, k)
                    dp = _mm_nt(do, v)
                    yield
                    p = jnp.exp(s - lse_ref[0, sl, :])
                    if diagonal:
                        p = jnp.where(col <= row + j * rows, p, 0.0)
                    ds = p * (dp - delta_ref[0, sl, :])
                    yield
                    dvs[j] = _mm_tn(p, do)
                    dks[j] = _mm_tn(ds, q)
                    dq_ref[0, sl, :] += _mm(ds, k)

                _lockstep([chain(j) for j in range(ATTN_CHAINS)])
                dv_ref[0, pl.ds(ks, bq), :] += functools.reduce(jnp.add, dvs)
                dk_ref[0, pl.ds(ks, bq), :] += functools.reduce(jnp.add, dks)

            q_block(ks, True)

            def q_step(qi, c):
                q_block(pl.multiple_of(qi * bq, bq), False)
                return c

            lax.fori_loop(kj + 1, nq, q_step, 0)
            return carry

        lax.fori_loop(0, nq, k_step, 0)

    spec = lambda d: pl.BlockSpec((1, S, d), lambda h, b: (h, b, 0))
    return _call_beside(
        body, transfer, grid=(H, B), name="attn_bwd",
        in_specs=[spec(QK_DIM), spec(QK_DIM), spec(V_DIM), spec(V_DIM), spec(1), spec(1)],
        out_specs=[spec(QK_DIM), spec(QK_DIM), spec(V_DIM)],
        out_shape=[SDS((H, B * S, QK_DIM), F32), SDS((H, B * S, QK_DIM), F32), SDS((H, B * S, V_DIM), F32)],
        scratch_shapes=[], semantics=("arbitrary", "arbitrary"),
        args=(q4, k4, v4, do4, delta4, lse4))


def _gdn_bwd(qg, kg, vg, gates, states, ainv, u4, w4, do4, B, S, transfer=None):
    H, D, C = GDN_HEADS, GDN_DIM, CHUNK
    NC = S // C
    U = GDN_BWD_UNROLL if NC % GDN_BWD_UNROLL == 0 else 1
    NG = NC // U

    def body(q_ref, k_ref, v_ref, g_ref, st_ref, ai_ref, u_ref, w_ref, do_ref, dq_ref, dk_ref, dv_ref, dgb_ref,
             kd_s, x1_s, x2_s, el_s, dvn_s, ds_s, w2t_s):
        h = pl.program_id(0)
        lane = lax.broadcasted_iota(jnp.int32, (C, LANES), 1)
        ri = lax.broadcasted_iota(jnp.int32, (C, C), 0)
        ci = lax.broadcasted_iota(jnp.int32, (C, C), 1)
        rcol = lax.broadcasted_iota(jnp.int32, (C, 1), 0)

        def rsum(a):
            return jnp.sum(a, axis=-1, keepdims=True)

        def prepare(n):
            cs = n * C
            q = q_ref[0, pl.ds(cs, C), :]
            k = k_ref[0, pl.ds(cs, C), :]
            do = do_ref[0, pl.ds(cs, C), :]
            Gc, bt, Gam, e, f, eL = _chunk_decays(g_ref[pl.ds(cs, C), :], lane, h, ri, ci, rcol)
            At = _mm_nt(q, k) * Gam
            yield
            x1 = _mm_tn(At, do)
            x2 = _mm_tn(q * e, do)
            kd = k * f
            w = w_ref[0, pl.ds(cs, C), :]
            yield
            x1_s[pl.ds(cs, C), :] = x1
            x2_s[n] = x2 - _mm_tn(w, x1)
            w2t_s[n] = _mm_tn(w, kd)
            kd_s[pl.ds(cs, C), :] = kd
            el_s[n] = jnp.broadcast_to(eL, (SUBLANES, LANES))

        def recur(n, dS):
            cs = n * C
            ds_s[n] = dS
            dvn_s[pl.ds(cs, C), :] = x1_s[pl.ds(cs, C), :] + _mm(kd_s[pl.ds(cs, C), :], dS)
            return x2_s[n] + el_s[n, 0:1, :] * dS - _mm(w2t_s[n], dS)

        def local(n):
            cs = n * C
            q = q_ref[0, pl.ds(cs, C), :]
            k = k_ref[0, pl.ds(cs, C), :]
            v = v_ref[0, pl.ds(cs, C), :]
            do = do_ref[0, pl.ds(cs, C), :]
            u = u_ref[0, pl.ds(cs, C), :]
            w = w_ref[0, pl.ds(cs, C), :]
            dvn = dvn_s[pl.ds(cs, C), :]
            dS = ds_s[n]
            Gc, bt, Gam, e, f, eL = _chunk_decays(g_ref[pl.ds(cs, C), :], lane, h, ri, ci, rcol)
            S0 = st_ref[0, n]
            AinvT = ai_ref[0, n]
            qk = _mm_nt(jnp.concatenate([q, k], axis=0), k)
            QK, KK = qk[:C], qk[C:]
            be = bt * e
            sol = jnp.concatenate([u, w], axis=-1)
            vn = u - _mm(w, S0)
            yield
            dAt = jnp.where(ri >= ci, _mm_nt(do, vn), 0.0)
            dqd = _mm_nt(do, S0)
            dw = -_mm_nt(dvn, S0)
            dkd = _mm_nt(vn, dS)
            deL = jnp.sum(rsum(dS * S0), axis=0, keepdims=True)
            yield
            dR = _mm_exact(AinvT, jnp.concatenate([dvn, dw], axis=-1))
            dR1, dR2 = dR[:, :D], dR[:, D:]
            yield
            dL = jnp.where(ri > ci, -_mm_nt(dR, sol), 0.0)
            yield
            dv_ref[0, pl.ds(cs, C), :] = dR1 * bt
            r2 = rsum(dR2 * k)
            X = dL * Gam
            dbt = rsum(dR1 * v) + r2 * e + rsum(X * KK)
            de = r2 * bt + rsum(dqd * q)
            dKK = X * bt
            dQK = dAt * Gam
            dq_ref[0, pl.ds(cs, C), :] = _mm(dQK, k) + dqd * e
            dk_ref[0, pl.ds(cs, C), :] = dR2 * be + _mm(dKK + dKK.T, k) + _mm_tn(dQK, q) + dkd * f
            df = rsum(dkd * k)
            Z = (dL * (bt * KK) + dAt * QK) * Gam
            dG = rsum(Z) - rsum(Z.T) + de * e - df * f
            dGl = jnp.sum(df * f, axis=0, keepdims=True) + deL * eL
            dG = dG + jnp.where(rcol == C - 1, dGl, 0.0)
            dgb_ref[0, pl.ds(cs, C), :] = jnp.where(lane == 0, dG, jnp.where(lane == 1, dbt, 0.0))

        state = [jnp.zeros((D, D), F32)]

        def recur_group(g):
            for j, n in enumerate(reversed(range(g * U, (g + 1) * U))):
                state[0] = recur(n, state[0])
                if j % GDN_RECUR_STEPS_PER_STAGE == GDN_RECUR_STEPS_PER_STAGE - 1:
                    yield

        def stage(fn, g):
            return _together([fn(g * U + j) for j in range(U)])

        for step in range(NG + 2):
            jobs = [(stage, prepare, NG - 1 - step), (None, None, NG - step), (stage, local, NG + 1 - step)]
            _lockstep([recur_group(g) if make is None else make(fn, g) for make, fn, g in jobs if 0 <= g < NG])

    spec = pl.BlockSpec((1, S, D), lambda h, b: (h, b, 0))
    return _call_beside(
        body, transfer, grid=(H, B), name="gdn_bwd",
        in_specs=[spec, spec, spec, pl.BlockSpec((S, LANES), lambda h, b: (b, 0)),
                  pl.BlockSpec((1, NC, D, D), lambda h, b: (h, b, 0, 0)),
                  pl.BlockSpec((1, NC, C, C), lambda h, b: (h, b, 0, 0)), spec, spec, spec],
        out_specs=[spec, spec, spec, spec],
        out_shape=[SDS((H, B * S, D), F32)] * 4,
        scratch_shapes=[pltpu.VMEM((S, D), F32), pltpu.VMEM((S, D), F32), pltpu.VMEM((NC, D, D), F32),
                        pltpu.VMEM((NC, SUBLANES, LANES), F32), pltpu.VMEM((S, D), F32),
                        pltpu.VMEM((NC, D, D), F32), pltpu.VMEM((NC, D, D), F32)],
        semantics=("arbitrary", "arbitrary"), args=(qg, kg, vg, gates, states, ainv, u4, w4, do4))


def _gdn_pre_bwd(proj, conv_w, alog_l, dt_l, dq4, dk4, dv4, dgb4, S):
    T = proj.shape[0]
    tm = min(256, T)
    tiles_per_seq = S // tm
    C3 = 3 * GDN_WIDTH
    H = GDN_HEADS

    def body(u_ref, halo_ref, gab_ref, w_ref, alog_ref, dt_ref, dq_ref, dk_ref, dv_ref, dgb_ref,
             dc_ref, dgab_ref, dcw_ref, dalog_ref, ddt_ref):
        i = pl.program_id(0)

        @pl.when(i == 0)
        def _():
            dcw_ref[...] = jnp.zeros_like(dcw_ref)
            dalog_ref[...] = jnp.zeros_like(dalog_ref)
            ddt_ref[...] = jnp.zeros_like(ddt_ref)

        halo = jnp.where(i % tiles_per_seq == 0, 0.0, halo_ref[...])
        c, sh = _conv_taps(u_ref[...], halo, w_ref[...])
        sg = _sigmoid(c)
        a = c * sg
        das = [None] * (3 * H)
        for h in range(H):
            xq = a[:, h * GDN_DIM:(h + 1) * GDN_DIM]
            xk = a[:, GDN_WIDTH + h * GDN_DIM:GDN_WIDTH + (h + 1) * GDN_DIM]
            das[h] = _l2n_bwd(dq_ref[h], xq, GDN_QSCALE)
            das[H + h] = _l2n_bwd(dk_ref[h], xk, 1.0)
            das[2 * H + h] = dv_ref[h]
        dc = jnp.concatenate(das, axis=-1) * (sg * (1.0 + c * (1.0 - sg)))
        dc_ref[...] = dc
        dcw_ref[...] += jnp.concatenate(
            [jnp.sum(dc * sh[CONV_W - 1 - t], axis=0, keepdims=True) for t in range(CONV_W)], axis=0)
        lane = lax.broadcasted_iota(jnp.int32, (tm, LANES), 1)
        ric = lax.broadcasted_iota(jnp.int32, (tm, LANES), 0) % CHUNK
        dG = jnp.zeros((tm, LANES), F32)
        for h in range(H):
            t = dgb_ref[h]
            dG = dG + jnp.where(lane == h, _pick_lane(t, lane, 0), 0.0) \
                    + jnp.where(lane == h + H, _pick_lane(t, lane, 1), 0.0)
        is_g = lane < H
        dg = jnp.where(is_g, _chunk_rev_cumsum(jnp.where(is_g, dG, 0.0), ric), 0.0)
        gab = gab_ref[...]
        g, beta = _gate_values(gab, alog_ref[...], dt_ref[...], lane)
        dga = jnp.where(is_g, dg * (-jnp.exp(alog_ref[...])) * _sigmoid(gab + dt_ref[...]), 0.0)
        dgb = jnp.where(is_g, 0.0, dG) * beta * (1.0 - beta)
        dgab_ref[...] = (dga + dgb).astype(MXU_DTYPE)
        dalog_ref[...] += jnp.sum(dg * g, axis=0, keepdims=True)
        ddt_ref[...] += jnp.sum(dga, axis=0, keepdims=True)

    hspec = pl.BlockSpec((H, tm, GDN_DIM), lambda i: (0, i, 0))
    vec = pl.BlockSpec((1, LANES), lambda i: (0, 0))
    return pl.pallas_call(
        body, grid=(T // tm,), name="gdn_pre_bwd",
        in_specs=[pl.BlockSpec((tm, C3), lambda i: (i, 0)),
                  pl.BlockSpec((SUBLANES, C3), lambda i: (jnp.maximum(i * (tm // SUBLANES) - 1, 0), 0)),
                  pl.BlockSpec((tm, LANES), lambda i: (i, P_GAB // LANES)),
                  pl.BlockSpec((CONV_W, C3), lambda i: (0, 0)), vec, vec, hspec, hspec, hspec, hspec],
        out_specs=[pl.BlockSpec((tm, C3), lambda i: (i, 0)), pl.BlockSpec((tm, LANES), lambda i: (i, 0)),
                   pl.BlockSpec((CONV_W, C3), lambda i: (0, 0)), vec, vec],
        out_shape=[SDS((T, C3), F32), SDS((T, LANES), MXU_DTYPE), SDS((CONV_W, C3), F32),
                   SDS((1, LANES), F32), SDS((1, LANES), F32)],
        compiler_params=_params(("arbitrary",)),
    )(proj, proj, proj, conv_w, alog_l, dt_l, dq4, dk4, dv4, dgb4)


def _mla_pre_bwd(proj, cosf, sinf, w_qln, w_kvln, w_uq_p, w_ukv, qnw, knw, dq4, dk4, dv4, transfer=None):
    T = proj.shape[0]
    tm = min(256, T)
    H = MLA_HEADS

    def body(ql_ref, kvl_ref, kpe_ref, cos_ref, sin_ref, wq_ref, wkv_ref, uq_ref, ukv_ref, qnw_ref, knw_ref,
             dq_ref, dk_ref, dv_ref,
             dql_ref, dkvl_ref, dkpe_ref, dqraw_ref, dkvraw_ref, qn_ref, kvn_ref, dwq_ref, dwkv_ref, dqnw_ref, dknw_ref):
        @pl.when(pl.program_id(0) == 0)
        def _():
            for r in (dwq_ref, dwkv_ref, dqnw_ref, dknw_ref):
                r[...] = jnp.zeros_like(r)

        cos, sin = cos_ref[...], sin_ref[...]
        qnw_, knw_ = qnw_ref[...], knw_ref[...]
        ql, kvl = ql_ref[...], kvl_ref[...]
        kpe_raw = kpe_ref[...][:, :ROPE]
        rms = functools.partial(_rms, on_mxu=True)
        rms_bwd = functools.partial(_rms_bwd, on_mxu=True)
        qn, rq = rms(ql, wq_ref[...])
        kvn, rkv = rms(kvl, wkv_ref[...])
        qn_ref[...] = qn.astype(MXU_DTYPE)
        kvn_ref[...] = kvn.astype(MXU_DTYPE)
        qraw = _mm(qn, uq_ref[...])
        kvraw = _mm(kvn, ukv_ref[...])
        dq_nope, dq_pe, dkv_parts = [], [], []
        dqnw_n = jnp.zeros((1, NOPE), F32)
        dqnw_p = jnp.zeros((1, ROPE), F32)
        dknw_n = jnp.zeros((1, NOPE), F32)
        dkpe = jnp.zeros((tm, ROPE), F32)
        for h in range(H):
            dq = dq_ref[h] * ATT_SCALE
            x = qraw[:, h * NOPE:(h + 1) * NOPE]
            dx, dw = rms_bwd(dq[:, :NOPE], x, qnw_[:, :NOPE], rms(x, qnw_[:, :NOPE])[1])
            dq_nope.append(dx)
            dqnw_n = dqnw_n + dw
            x = qraw[:, H * NOPE + h * ROPE:H * NOPE + (h + 1) * ROPE]
            dx, dw = rms_bwd(_rope_bwd(dq[:, NOPE:], cos, sin), x, qnw_[:, NOPE:], rms(x, qnw_[:, NOPE:])[1])
            dq_pe.append(dx)
            dqnw_p = dqnw_p + dw
            dk = dk_ref[h]
            x = kvraw[:, h * 256:h * 256 + NOPE]
            dx, dw = rms_bwd(dk[:, :NOPE], x, knw_[:, :NOPE], rms(x, knw_[:, :NOPE])[1])
            dknw_n = dknw_n + dw
            dkpe = dkpe + dk[:, NOPE:]
            dkv_parts += [dx, dv_ref[h]]
        dx, dknw_p = rms_bwd(_rope_bwd(dkpe, cos, sin), kpe_raw, knw_[:, NOPE:], rms(kpe_raw, knw_[:, NOPE:])[1])
        dkpe_ref[...] = jnp.concatenate([dx, jnp.zeros((tm, LANES - ROPE), F32)], axis=-1).astype(MXU_DTYPE)
        dqraw = jnp.concatenate(dq_nope + dq_pe, axis=-1).astype(MXU_DTYPE)
        dkvraw = jnp.concatenate(dkv_parts, axis=-1).astype(MXU_DTYPE)
        dqraw_ref[...] = dqraw
        dkvraw_ref[...] = dkvraw
        dx, dw = rms_bwd(_mm_nt(dqraw, uq_ref[...]), ql, wq_ref[...], rq)
        dql_ref[...] = dx.astype(MXU_DTYPE)
        dwq_ref[...] += dw
        dx, dw = rms_bwd(_mm_nt(dkvraw, ukv_ref[...]), kvl, wkv_ref[...], rkv)
        dkvl_ref[...] = dx.astype(MXU_DTYPE)
        dwkv_ref[...] += dw
        dqnw_ref[...] += jnp.concatenate([dqnw_n, dqnw_p], axis=-1)
        dknw_ref[...] += jnp.concatenate([dknw_n, dknw_p], axis=-1)

    full = lambda a: pl.BlockSpec(a.shape, lambda i: (0,) * a.ndim)
    rows = lambda n: pl.BlockSpec((tm, n), lambda i: (i, 0))
    const = lambda n: pl.BlockSpec((1, n), lambda i: (0, 0))
    NQ, NKV = w_uq_p.shape[1], w_ukv.shape[1]
    return _call_beside(
        body, transfer, grid=(T // tm,), name="mla_pre_bwd", scratch_shapes=[], semantics=("arbitrary",),
        args=(proj, proj, proj, cosf, sinf, w_qln, w_kvln, w_uq_p, w_ukv, qnw, knw, dq4, dk4, dv4),
        in_specs=[pl.BlockSpec((tm, 256), lambda i: (i, P_QLAT // 256)),
                  pl.BlockSpec((tm, 256), lambda i: (i, P_KVLAT // 256)),
                  pl.BlockSpec((tm, 128), lambda i: (i, P_KPE // 128)),
                  rows(ROPE), rows(ROPE),
                  full(w_qln), full(w_kvln), full(w_uq_p), full(w_ukv), full(qnw), full(knw),
                  pl.BlockSpec((H, tm, QK_DIM), lambda i: (0, i, 0)),
                  pl.BlockSpec((H, tm, QK_DIM), lambda i: (0, i, 0)),
                  pl.BlockSpec((H, tm, V_DIM), lambda i: (0, i, 0))],
        out_specs=[rows(Q_LORA), rows(KV_LORA), rows(LANES), rows(NQ), rows(NKV), rows(Q_LORA), rows(KV_LORA),
                   const(Q_LORA), const(KV_LORA), const(QK_DIM), const(QK_DIM)],
        out_shape=[SDS((T, Q_LORA), MXU_DTYPE), SDS((T, KV_LORA), MXU_DTYPE), SDS((T, LANES), MXU_DTYPE),
                   SDS((T, NQ), MXU_DTYPE), SDS((T, NKV), MXU_DTYPE),
                   SDS((T, Q_LORA), MXU_DTYPE), SDS((T, KV_LORA), MXU_DTYPE),
                   SDS((1, Q_LORA), F32), SDS((1, KV_LORA), F32), SDS((1, QK_DIM), F32), SDS((1, QK_DIM), F32)])


def _in_proj_bwd(dc, conv_w, dgz, dql, dkvl, dkpe, dgab, w_in_p, dh, x2, w_an, S):
    T, D = x2.shape
    N = w_in_p.shape[1]
    C3 = dc.shape[1]
    tm = min(512, S)
    assert S % tm == 0 and T % tm == 0, "a token tile must not straddle two sequences"
    tiles_per_seq = S // tm
    nblk = T // SUBLANES

    def body(dc_ref, nxt_ref, cw_ref, b_ref, c_ref, d_ref, e_ref, f_ref, w_ref, dh_ref, x_ref, wn_ref,
             dx_ref, dp_ref, dwn_ref):
        i = pl.program_id(0)

        @pl.when(i == 0)
        def _():
            dwn_ref[...] = jnp.zeros_like(dwn_ref)

        nxt = jnp.where(i % tiles_per_seq == tiles_per_seq - 1, 0.0, nxt_ref[...])
        dcv, cw = dc_ref[...], cw_ref[...]
        du = cw[3:4] * dcv
        for j in range(1, CONV_W):
            du = du + cw[3 - j:4 - j] * _shift_up(dcv, nxt, j)
        dp = jnp.concatenate([du.astype(MXU_DTYPE), b_ref[...], c_ref[...], d_ref[...], e_ref[...], f_ref[...]],
                             axis=-1).astype(MXU_DTYPE)
        dp_ref[...] = dp
        x = x_ref[...]
        _, r = _rms(x, wn_ref[...])
        dx, dw = _rms_bwd(_mm_nt(dp, w_ref[...]), x, wn_ref[...], r)
        dx_ref[...] = dh_ref[...] + dx
        dwn_ref[...] += dw

    rows = lambda n: pl.BlockSpec((tm, n), lambda i: (i, 0))
    return pl.pallas_call(
        body, grid=(T // tm,), name="in_proj_bwd",
        in_specs=[rows(C3),
                  pl.BlockSpec((SUBLANES, C3), lambda i: (jnp.minimum((i + 1) * (tm // SUBLANES), nblk - 1), 0)),
                  pl.BlockSpec((CONV_W, C3), lambda i: (0, 0)),
                  rows(dgz.shape[1]), rows(dql.shape[1]), rows(dkvl.shape[1]),
                  rows(dkpe.shape[1]), rows(dgab.shape[1]),
                  pl.BlockSpec((D, N), lambda i: (0, 0)), rows(D), rows(D), pl.BlockSpec((1, D), lambda i: (0, 0))],
        out_specs=[rows(D), rows(N), pl.BlockSpec((1, D), lambda i: (0, 0))],
        out_shape=[SDS((T, D), F32), SDS((T, N), MXU_DTYPE), SDS((1, D), F32)],
        compiler_params=_params(("arbitrary",)),
    )(dc, dc, conv_w, dgz, dql, dkvl, dkpe, dgab, w_in_p, dh, x2, w_an)


def _relu_squared(t):
    r = jnp.maximum(t.astype(F32), 0.0)
    return (r * r).astype(MXU_DTYPE)


def _wgrad(a, b, name, column_shards=False, a_map=None):
    T, M = a.shape
    N = b.shape[1]
    tM = _divisor_tile(M, 1024)
    tN = N // N_DEV if column_shards else _divisor_tile(N, 1536)
    tk = min(T, 2048)
    nk = T // tk

    def body(a_ref, b_ref, o_ref, acc):
        k = pl.program_id(2)

        @pl.when(k == 0)
        def _():
            acc[...] = jnp.zeros_like(acc)

        acc[...] += _mm_tn(a_ref[...] if a_map is None else a_map(a_ref[...]), b_ref[...])

        @pl.when(k == nk - 1)
        def _():
            o_ref[...] = acc[...].astype(WIRE_DTYPE).reshape(o_ref.shape)

    if column_shards:
        out_spec, out_shape = pl.BlockSpec((1, tM, tN), lambda i, j, k: (j, i, 0)), SDS((N_DEV, M, tN), WIRE_DTYPE)
    else:
        out_spec, out_shape = pl.BlockSpec((tM, tN), lambda i, j, k: (i, j)), SDS((M, N), WIRE_DTYPE)
    return pl.pallas_call(
        body, grid=(M // tM, N // tN, nk), name=name,
        in_specs=[pl.BlockSpec((tk, tM), lambda i, j, k: (k, i)), pl.BlockSpec((tk, tN), lambda i, j, k: (k, j))],
        out_specs=out_spec, out_shape=out_shape,
        scratch_shapes=[pltpu.VMEM((tM, tN), F32)],
        compiler_params=_params(("arbitrary", "arbitrary", "arbitrary")),
    )(a, b)


WGRAD_RING_SLOTS = 3


def _wgrad_stream(a, b, name, tile, stream_a=False, column_shards=False, a_map=None):
    T, M = a.shape
    N = b.shape[1]
    n = (M if stream_a else N) // tile
    tk = min(T, 2048)
    assert (M if stream_a else N) % tile == 0 and T % tk == 0 and (a_map is None or stream_a)
    held, src = (b, a) if stream_a else (a, b)

    def body(held_ref, src_ref, o_ref, ring, sems, *held_t):
        s = pl.program_id(0)

        def fetch(t, slot):
            cols = pl.ds(pl.multiple_of(t * tile, tile), tile)
            return pltpu.make_async_copy(src_ref.at[:, cols], ring.at[slot], sems.at[slot])

        @pl.when(s == 0)
        def _():
            for t in range(min(2, n)):
                fetch(t, t).start()
            if not stream_a:
                for k in range(T // tk):
                    held_t[0][:, pl.ds(k * tk, tk)] = held_ref[pl.ds(k * tk, tk), :].astype(MXU_DTYPE).T

        @pl.when(s + 2 < n)
        def _():
            fetch(s + 2, (s + 2) % WGRAD_RING_SLOTS).start()

        slot = s % WGRAD_RING_SLOTS
        fetch(s, slot).wait()
        acc = None
        for k in range(T // tk):
            rows = pl.ds(k * tk, tk)
            if stream_a:
                at = ring[slot, rows, :]
                part = _mm_tn(at if a_map is None else a_map(at), held_ref[rows, :])
            else:
                part = _mm(held_t[0][:, rows], ring[slot, rows, :])
            acc = part if acc is None else acc + part
        o_ref[...] = acc.astype(WIRE_DTYPE).reshape(o_ref.shape)

    if stream_a:
        out_spec, out_shape = pl.BlockSpec((tile, N), lambda s: (s, 0)), SDS((M, N), WIRE_DTYPE)
    elif column_shards:
        assert tile == N // N_DEV
        out_spec, out_shape = pl.BlockSpec((1, M, tile), lambda s: (s, 0, 0)), SDS((N_DEV, M, tile), WIRE_DTYPE)
    else:
        out_spec, out_shape = pl.BlockSpec((M, tile), lambda s: (0, s)), SDS((M, N), WIRE_DTYPE)
    return pl.pallas_call(
        body, grid=(n,), name=name,
        in_specs=[pl.BlockSpec(held.shape, lambda s: (0, 0)), pl.BlockSpec(memory_space=pl.ANY)],
        out_specs=out_spec, out_shape=out_shape,
        scratch_shapes=[pltpu.VMEM((WGRAD_RING_SLOTS, T, tile), src.dtype),
                        pltpu.SemaphoreType.DMA((WGRAD_RING_SLOTS,))]
        + ([] if stream_a else [pltpu.VMEM((M, T), MXU_DTYPE)]),
        compiler_params=_params(("arbitrary",)),
    )(held, src)


def _adamw(g, w, m, v):
    m = ADAM_B1 * m + (1.0 - ADAM_B1) * g
    v = ADAM_B2 * v + (1.0 - ADAM_B2) * jnp.square(g)
    m_hat = m / (1.0 - ADAM_B1 ** ADAM_STEP)
    v_hat = v / (1.0 - ADAM_B2 ** ADAM_STEP)
    return -ADAM_LR * (m_hat / (jnp.sqrt(v_hat) + ADAM_EPS) + ADAM_WD * w), m, v


def _reduce_adamw(parts, w, m, v, name):
    R, C = w.shape
    slots, Rp, Cp = parts.shape
    tr = min(R, 256)
    tp = tr if Rp == R else Rp

    def body(p_ref, w_ref, m_ref, v_ref, g_ref, d_ref, nm_ref, nv_ref):
        g = p_ref[0].astype(F32)
        for s in range(1, slots):
            g = g + p_ref[s].astype(F32)
        g = g[:tr, :C]
        g_ref[...] = g
        d_ref[...], nm_ref[...], nv_ref[...] = _adamw(g, w_ref[...], m_ref[...], v_ref[...])

    spec = pl.BlockSpec((tr, C), lambda i: (i, 0))
    return pl.pallas_call(
        body, grid=(R // tr,), name=name,
        in_specs=[pl.BlockSpec((slots, tp, Cp), lambda i: (0, i, 0)), spec, spec, spec],
        out_specs=[spec] * 4, out_shape=[SDS((R, C), F32)] * 4,
        compiler_params=_params(("arbitrary",)),
    )(parts, w, m, v)


SMALL_ROWS, SMALL_COLS = 16, 1024
SMALL_LAYOUT = (
    ("attn_norm_w", 0, 1, 1024, 1024), ("mlp_norm_w", 1, 1, 1024, 1024), ("q_lat_norm_w", 2, 1, 256, 256),
    ("kv_lat_norm_w", 3, 1, 256, 256), ("q_norm_w", 4, 1, 192, 192), ("k_norm_w", 5, 1, 192, 192),
    ("mla_out_norm_w", 6, 4, 128, 128), ("a_log", 10, 1, 128, 4), ("dt_bias", 11, 1, 128, 4),
    ("gdn_norm_w", 12, 1, 128, 128))
LOSS_ENTRY = ("loss", 13, 1, 128, 128)


def _adamw_replicated(parts, ws, ms, vs):
    n = len(SMALL_LAYOUT)

    def body(*refs):
        p_ref = refs[0]
        w_refs, m_refs, v_refs = refs[1:1 + n], refs[1 + n:1 + 2 * n], refs[1 + 2 * n:1 + 3 * n]
        outs = refs[1 + 3 * n:]
        s = p_ref[0]
        for d in range(1, N_DEV):
            s = s + p_ref[d]
        for i, (_, r0, nr, _, pw) in enumerate(SMALL_LAYOUT):
            g = s[r0:r0 + nr, :pw]
            outs[i][...] = g
            outs[n + i][...], outs[2 * n + i][...], outs[3 * n + i][...] = _adamw(
                g, w_refs[i][...], m_refs[i][...], v_refs[i][...])
        _, r0, nr, gw, _ = LOSS_ENTRY
        outs[4 * n][...] = s[r0:r0 + nr, :gw]

    res = pl.pallas_call(
        body, name="adamw_replicated",
        out_shape=[SDS(w.shape, F32) for w in ws] * 4 + [SDS((1, LANES), F32)],
        compiler_params=_params(),
    )(parts, *ws, *ms, *vs)
    return [res[k * n:(k + 1) * n] for k in range(4)], res[4 * n][0, 0]


COPIES_PER_ARRAY = N_DEV - 1


def _two_level_gather(srcs, outs, send_sems, recv_sems, local_sems=None, stage="all"):
    mx, my, mc = lax.axis_index("x"), lax.axis_index("y"), lax.axis_index("c")
    me, sibling = (mx, my, mc), (mx, my, 1 - mc)
    chips = [(1 - mx, my), (mx, 1 - my), (1 - mx, 1 - my)]
    arrays = range(len(srcs))

    def copy(a, k, block, to, src=None):
        px, py, pc = block
        slot = outs[a].at[4 * px + 2 * py + pc]
        sem = a * COPIES_PER_ARRAY + k
        return pltpu.make_async_remote_copy(
            src_ref=slot if src is None else src, dst_ref=slot,
            send_sem=send_sems.at[sem], recv_sem=recv_sems.at[sem], device_id=to, device_id_type=MESH_ID)

    mine = [] if local_sems is None else [
        pltpu.make_async_copy(srcs[a], outs[a].at[4 * mx + 2 * my + mc], local_sems.at[a]) for a in arrays]
    first = []
    for a in arrays:
        first.append(copy(a, 0, me, sibling, src=srcs[a]))
        first += [copy(a, 1 + j, me, (*chip, mc), src=srcs[a]) for j, chip in enumerate(chips)]
    forwards = [copy(a, 4 + j, (*chip, mc), sibling) for j, chip in enumerate(chips) for a in arrays]
    if stage in ("all", "start"):
        for cp in mine + first:
            cp.start()
    if stage in ("all", "forward"):
        for j, chip in enumerate(chips):
            for a in arrays:
                copy(a, 1 + j, (*chip, mc), me).wait_recv()
                forwards[j * len(srcs) + a].start()
    if stage in ("all", "finish"):
        for a in arrays:
            copy(a, 0, sibling, me).wait_recv()
        for j, chip in enumerate(chips):
            for a in arrays:
                copy(a, 4 + j, (*chip, 1 - mc), me).wait_recv()
        for cp in first + forwards:
            cp.wait_send()
        for cp in mine:
            cp.wait()


def _comm_scratch(n):
    return [pltpu.SemaphoreType.DMA((n * COPIES_PER_ARRAY,)), pltpu.SemaphoreType.DMA((n * COPIES_PER_ARRAY,)),
            pltpu.SemaphoreType.DMA((n,))]


def _any_specs(n):
    return [pl.BlockSpec(memory_space=pl.ANY)] * n


def _gather_weights(shards):
    n = len(shards)

    def body(*refs):
        _two_level_gather(refs[:n], refs[n:2 * n], *refs[2 * n:])

    return pl.pallas_call(
        body, name="gather_weights",
        out_shape=[SDS((N_DEV,) + s.shape, s.dtype) for s in shards],
        in_specs=_any_specs(n), out_specs=_any_specs(n), scratch_shapes=_comm_scratch(n),
    )(*shards)


def _gather_small_grads(gs, loss_lanes):
    gs = list(gs) + [loss_lanes]
    n = len(gs)

    def body(*refs):
        g_refs, out_ref = refs[:n], refs[n]
        tile, send_sems, recv_sems = refs[n + 1:]
        tile[...] = jnp.zeros_like(tile)
        for (_, r0, nr, gw, _), g in zip(SMALL_LAYOUT + (LOSS_ENTRY,), g_refs):
            tile[r0:r0 + nr, 0:gw] = g[...]
        me = 4 * lax.axis_index("x") + 2 * lax.axis_index("y") + lax.axis_index("c")
        out_ref[me] = tile[...]
        _two_level_gather([tile], [out_ref], send_sems, recv_sems)

    return pl.pallas_call(
        body, name="gather_small_grads",
        out_shape=SDS((N_DEV, SMALL_ROWS, SMALL_COLS), F32),
        in_specs=[pl.BlockSpec(memory_space=pltpu.VMEM)] * n,
        out_specs=pl.BlockSpec(memory_space=pltpu.VMEM),
        scratch_shapes=[pltpu.VMEM((SMALL_ROWS, SMALL_COLS), F32),
                        pltpu.SemaphoreType.DMA((COPIES_PER_ARRAY,)), pltpu.SemaphoreType.DMA((COPIES_PER_ARRAY,))],
    )(*gs)


def _exchange_grads_two_level(big, small):
    _, R, C = big.shape
    chip_flips = ((1, 0), (0, 1), (1, 1))

    def body(big_ref, small_ref, out_ref, small_out, mine_v, sib_v, pre_v, d2d_send, d2d_recv, ici_send, ici_recv,
             local_sems, s_send, s_recv, s_local):
        mx, my, mc = lax.axis_index("x"), lax.axis_index("y"), lax.axis_index("c")
        sibling = (mx, my, 1 - mc)
        chips = [(px, py) for px in range(2) for py in range(2)]
        _exchange([small_ref], [small_out], s_send, s_recv, s_local, stage="start")
        own = [pltpu.make_async_copy(big_ref.at[4 * px + 2 * py + mc], mine_v.at[q], local_sems.at[q])
               for q, (px, py) in enumerate(chips)]
        d2d = [pltpu.make_async_remote_copy(
            src_ref=big_ref.at[4 * px + 2 * py + (1 - mc)], dst_ref=sib_v.at[q], send_sem=d2d_send.at[q],
            recv_sem=d2d_recv.at[q], device_id=sibling, device_id_type=MESH_ID) for q, (px, py) in enumerate(chips)]
        for cp in own + d2d:
            cp.start()
        for cp in own + d2d:
            cp.wait()
        for q in range(4):
            pre_v[q] = (mine_v[q].astype(F32) + sib_v[q].astype(F32)).astype(pre_v.dtype)
        ici = []
        for k, (fx, fy) in enumerate(chip_flips):
            px = 1 - mx if fx else mx
            py = 1 - my if fy else my
            ici.append(pltpu.make_async_remote_copy(
                src_ref=pre_v.at[2 * px + py], dst_ref=out_ref.at[k], send_sem=ici_send.at[k],
                recv_sem=ici_recv.at[k], device_id=(px, py, mc), device_id_type=MESH_ID))
        keep = pltpu.make_async_copy(pre_v.at[2 * mx + my], out_ref.at[3], local_sems.at[4])
        for cp in ici + [keep]:
            cp.start()
        for cp in ici + [keep]:
            cp.wait()
        _exchange([small_ref], [small_out], s_send, s_recv, s_local, stage="finish")

    dma = pltpu.SemaphoreType.DMA
    return pl.pallas_call(
        body, name="exchange_grads",
        out_shape=[SDS((4, R, C), big.dtype), SDS(small.shape, small.dtype)],
        in_specs=_any_specs(2), out_specs=_any_specs(2),
        scratch_shapes=[pltpu.VMEM((4, R, C), big.dtype)] * 3 + [dma((4,)), dma((4,)), dma((3,)), dma((3,)), dma((5,))]
                       + _comm_scratch(1),
        compiler_params=_params(),
    )(big, small)


class _Transfer:
    def __init__(self, kind, arrays):
        self.kind, self.arrays, self.n = kind, list(arrays), len(arrays)

    def out_shapes(self):
        if self.kind == "gather":
            return [SDS((N_DEV,) + a.shape, a.dtype) for a in self.arrays]
        return [SDS(a.shape, a.dtype) for a in self.arrays]

    def run(self, srcs, outs, sems, stage):
        fn = _two_level_gather if self.kind == "gather" else _exchange
        fn(srcs, outs, *sems, stage=stage)


def _call_beside(body, transfer, *, grid, in_specs, out_specs, out_shape, scratch_shapes, name, semantics, args):
    if transfer is None:
        res = pl.pallas_call(body, grid=grid, in_specs=in_specs, out_specs=out_specs, out_shape=out_shape,
                             scratch_shapes=scratch_shapes, name=name, compiler_params=_params(semantics))(*args)
        return list(res), []
    n_in, n_out, n_s, n = len(in_specs), len(out_specs), len(scratch_shapes), transfer.n
    total = functools.reduce(lambda a, b: a * b, grid, 1)

    def wrapped(*refs):
        ins, refs = refs[:n_in], refs[n_in:]
        t_in, refs = refs[:n], refs[n:]
        outs, refs = refs[:n_out], refs[n_out:]
        t_out, refs = refs[:n], refs[n:]
        scratch, sems = refs[:n_s], refs[n_s:]
        first = functools.reduce(jnp.logical_and, [pl.program_id(i) == 0 for i in range(len(grid))])
        last = functools.reduce(jnp.logical_and, [pl.program_id(i) == g - 1 for i, g in enumerate(grid)])

        @pl.when(first)
        def _():
            transfer.run(t_in, t_out, sems, "start")

        step = functools.reduce(lambda acc, ig: acc * ig[1] + pl.program_id(ig[0]), enumerate(grid), 0)

        @pl.when(step == (3 * total) // 4)
        def _():
            transfer.run(t_in, t_out, sems, "forward")

        body(*ins, *outs, *scratch)

        @pl.when(last)
        def _():
            transfer.run(t_in, t_out, sems, "finish")

    res = pl.pallas_call(
        wrapped, grid=grid, in_specs=list(in_specs) + _any_specs(n), out_specs=list(out_specs) + _any_specs(n),
        out_shape=list(out_shape) + transfer.out_shapes(), scratch_shapes=list(scratch_shapes) + _comm_scratch(n),
        name=name, compiler_params=_params(semantics))(*args, *transfer.arrays)
    return list(res[:n_out]), list(res[n_out:])


EXCHANGE_FLIPS = ((0, 0, 1), (1, 0, 0), (0, 1, 0), (1, 1, 0), (1, 0, 1), (0, 1, 1), (1, 1, 1))


def _exchange(srcs, outs, send_sems, recv_sems, local_sems, stage="all"):
    mx, my, mc = lax.axis_index("x"), lax.axis_index("y"), lax.axis_index("c")
    arrays = range(len(srcs))
    copies = [pltpu.make_async_copy(srcs[a].at[4 * mx + 2 * my + mc], outs[a].at[N_DEV - 1], local_sems.at[a])
              for a in arrays]
    for k, (fx, fy, fc) in enumerate(EXCHANGE_FLIPS):
        px = 1 - mx if fx else mx
        py = 1 - my if fy else my
        pc = 1 - mc if fc else mc
        for a in arrays:
            sem = a * COPIES_PER_ARRAY + k
            copies.append(pltpu.make_async_remote_copy(
                src_ref=srcs[a].at[4 * px + 2 * py + pc], dst_ref=outs[a].at[k],
                send_sem=send_sems.at[sem], recv_sem=recv_sems.at[sem],
                device_id=(px, py, pc), device_id_type=MESH_ID))
    if stage in ("all", "start"):
        for cp in copies:
            cp.start()
    if stage in ("all", "finish"):
        for cp in copies:
            cp.wait()


def _w_in_to_padded(w):
    z = lambda n: jnp.zeros((w.shape[0], n), w.dtype)
    return jnp.concatenate([w[:, O_GQKV:O_GZ], w[:, O_GZ:O_GAB], w[:, O_QLAT:O_KVLAT], w[:, O_KVLAT:O_KPE],
                            w[:, O_KPE:O_GQKV], z(P_GAB - P_KPE - ROPE), w[:, O_GAB:O_END],
                            z(P_WIDTH - P_GAB - (O_END - O_GAB))], axis=1)


def _w_in_from_padded(wp):
    return jnp.concatenate([wp[:, P_QLAT:P_QLAT + 256], wp[:, P_KVLAT:P_KVLAT + 256], wp[:, P_KPE:P_KPE + ROPE],
                            wp[:, P_GQKV:P_GZ], wp[:, P_GZ:P_QLAT], wp[:, P_GAB:P_GAB + (O_END - O_GAB)]], axis=1)


W_IN_SHARD_COLS = (O_END - O_QLAT) // N_DEV


def _w_in_shards_to_padded(stack):
    _, R, Cw = stack.shape
    tr = min(R, 256)

    def body(s_ref, o_ref):
        full = jnp.concatenate([s_ref[d].astype(F32)[:, :W_IN_SHARD_COLS] for d in range(N_DEV)], axis=-1)
        o_ref[...] = _w_in_to_padded(full).astype(o_ref.dtype)

    return pl.pallas_call(
        body, grid=(R // tr,), name="w_in_to_padded",
        in_specs=[pl.BlockSpec((N_DEV, tr, Cw), lambda i: (0, i, 0))],
        out_specs=pl.BlockSpec((tr, P_WIDTH), lambda i: (i, 0)),
        out_shape=SDS((R, P_WIDTH), stack.dtype), compiler_params=_params(("arbitrary",)),
    )(stack)


def _w_in_padded_to_slabs(gp, wire_cols):
    R = gp.shape[0]
    tr = min(R, 256)

    def body(g_ref, o_ref):
        orig = _w_in_from_padded(g_ref[...].astype(F32))
        for d in range(N_DEV):
            piece = orig[:, d * W_IN_SHARD_COLS:(d + 1) * W_IN_SHARD_COLS]
            o_ref[d] = _pad2(piece, tr, wire_cols).astype(o_ref.dtype)

    return pl.pallas_call(
        body, grid=(R // tr,), name="w_in_to_slabs",
        in_specs=[pl.BlockSpec((tr, P_WIDTH), lambda i: (i, 0))],
        out_specs=pl.BlockSpec((N_DEV, tr, wire_cols), lambda i: (0, i, 0)),
        out_shape=SDS((N_DEV, R, wire_cols), gp.dtype), compiler_params=_params(("arbitrary",)),
    )(gp)


def _w_uq_to_headsplit(w):
    w3 = w.reshape(w.shape[0], MLA_HEADS, QK_DIM)
    return jnp.concatenate([w3[:, :, :NOPE].reshape(w.shape[0], -1), w3[:, :, NOPE:].reshape(w.shape[0], -1)], axis=1)


def _w_uq_from_headsplit(wp):
    n = wp[:, :MLA_HEADS * NOPE].reshape(wp.shape[0], MLA_HEADS, NOPE)
    p = wp[:, MLA_HEADS * NOPE:].reshape(wp.shape[0], MLA_HEADS, ROPE)
    return jnp.concatenate([n, p], axis=2).reshape(wp.shape[0], -1)


def _lane_vec(v4):
    return jnp.pad(v4.reshape(1, -1), ((0, 0), (0, LANES - v4.shape[-1])))


def _local_step(x, positions, target, attn_norm_w, w_in, q_lat_norm_w, w_uq, kv_lat_norm_w, w_ukv, q_norm_w,
                k_norm_w, mla_out_norm_w, conv_w, a_log, dt_bias, gdn_norm_w, w_out, mlp_norm_w, w_up, w_down,
                late_shards=None, exchange=False):
    B, S, D = x.shape
    T = B * S
    x2 = x.reshape(T, D)
    t2 = target.reshape(T, D)
    half = ROPE // 2
    inv_freq = ROPE_THETA ** (-jnp.arange(half, dtype=F32) / half)
    ang = positions.reshape(T, 1).astype(F32) * inv_freq
    cosf = jnp.concatenate([jnp.cos(ang)] * 2, axis=-1)
    sinf = jnp.concatenate([jnp.sin(ang)] * 2, axis=-1)
    w_in_p = w_in
    w_uq_p = _w_uq_to_headsplit(w_uq)
    alog_l, dt_l = _lane_vec(a_log), _lane_vec(dt_bias)
    w_an, w_qln, w_kvln, qnw, knw, w_mn, gdn_w = (
        attn_norm_w, q_lat_norm_w, kv_lat_norm_w, q_norm_w, k_norm_w, mlp_norm_w, gdn_norm_w)

    proj, xn, qg, kg, vg, gates = _in_proj(x2, w_an, w_in_p, conv_w, alog_l, dt_l, S)
    def gathering(shards):
        return None if late_shards is None else _Transfer("gather", shards)

    (q4, k4, v4), late = _mla_pre(proj, cosf, sinf, w_qln, w_kvln, w_uq_p, w_ukv, qnw, knw,
                                  gathering(late_shards and late_shards[:1]))
    if late:
        w_out = late[0].reshape(-1, D)
    (o_mla, lse), late = _attn_fwd(q4, k4, v4, B, S, gathering(late_shards and late_shards[2:]))
    if late:
        w_down = late[0].reshape(-1, D)
    (o_gdn, states, ainv, u4, w4), late = _gdn_fwd(qg, kg, vg, gates, B, S,
                                                   gathering(late_shards and late_shards[1:2]))
    if late:
        w_up = late[0]
    h2, mix = _mix_out(o_mla, o_gdn, proj, x2, mla_out_norm_w, gdn_w, w_out)
    up, hn, dy, sq, dyb = _mlp_fwd(h2, w_mn, w_up, w_down, t2)
    loss = (0.5 / D) * jnp.sum(sq[:, 0, 0])

    first = ("w_down",)
    second = ("w_up",)
    third = ("w_out", "w_uq", "w_ukv")
    mats = dict(w_down=_wgrad_stream(up, dyb, "wgrad_down", 512, stream_a=True, a_map=_relu_squared))

    def sending(names):
        return _Transfer("exchange", [_slabs(n, mats[n]) for n in names]) if exchange else None

    (dh, dhb, dup, d_mlp_norm), got = _mlp_bwd(dy, dyb, up, h2, w_mn, w_up, w_down, sending(first))
    mats.update(zip(first, got))
    mats.update(w_up=_wgrad_stream(hn, dup, "wgrad_up", D_FF // N_DEV, column_shards=True))
    do_mla, do_gdn, dz, d_mla_w, d_gdn_w, delta = _mix_bwd(dhb, o_mla, o_gdn, proj, mla_out_norm_w, gdn_w, w_out)
    mats.update(w_out=_wgrad_stream(mix, dhb, "wgrad_out", 256))
    (dq4, dk4, dv4), got = _attn_bwd(q4, k4, v4, do_mla, delta, lse, B, S, sending(second))
    mats.update(zip(second, got))
    (dql, dkvl, dkpe, dqraw, dkvraw, qn, kvn, d_wqln, d_wkvln, d_qnw, d_knw), _ = _mla_pre_bwd(
        proj, cosf, sinf, w_qln, w_kvln, w_uq_p, w_ukv, qnw, knw, dq4, dk4, dv4)
    mats.update(w_uq=_wgrad(qn, dqraw, "wgrad_uq"), w_ukv=_wgrad(kvn, dkvraw, "wgrad_ukv"))
    (dqg, dkg, dvg, dgb4), got = _gdn_bwd(qg, kg, vg, gates, states, ainv, u4, w4, do_gdn, B, S, sending(third))
    mats.update(zip(third, got))
    dc, dgab, g_conv, d_alog, d_dt = _gdn_pre_bwd(proj, conv_w, alog_l, dt_l, dqg, dkg, dvg, dgb4, S)
    grad_x2, dproj, d_attn_norm = _in_proj_bwd(dc, conv_w, dz, dql, dkvl, dkpe, dgab, w_in_p, dh, x2, w_an, S)
    mats.update(w_in=_wgrad_stream(xn, dproj, "wgrad_in", 256), conv_w=g_conv)
    if exchange:
        last = ("w_in", "conv_w")
        mats.update(zip(last, _exchange_grads_two_level(*[_slabs(n, mats[n]) for n in last])))
    small = dict(attn_norm_w=d_attn_norm, mlp_norm_w=d_mlp_norm, q_lat_norm_w=d_wqln, kv_lat_norm_w=d_wkvln,
                 q_norm_w=d_qnw, k_norm_w=d_knw, mla_out_norm_w=d_mla_w, a_log=d_alog, dt_bias=d_dt,
                 gdn_norm_w=d_gdn_w)
    return loss, grad_x2.reshape(B, S, D), mats, [small[n] for n, *_ in SMALL_LAYOUT]


BIG = ("w_in", "w_uq", "w_ukv", "conv_w", "w_out", "w_up", "w_down")
ALL_W = ("attn_norm_w", "w_in", "q_lat_norm_w", "w_uq", "kv_lat_norm_w", "w_ukv", "q_norm_w", "k_norm_w",
         "mla_out_norm_w", "conv_w", "a_log", "dt_bias", "gdn_norm_w", "w_out", "mlp_norm_w", "w_up", "w_down")
WIRE_SHAPE = {"w_in": (1024, 384), "w_uq": (256, 128), "conv_w": (16, 256)}


def _pad2(a, rows, cols):
    return jnp.pad(a, [(0, 0)] * (a.ndim - 2) + [(0, rows - a.shape[-2]), (0, cols - a.shape[-1])])


def _cols_to_full(stack, cols):
    return jnp.moveaxis(stack[:, :, :cols], 0, 1).reshape(stack.shape[1], N_DEV * cols)


def _full_to_cols(full, wire_cols):
    r, n = full.shape
    return _pad2(jnp.moveaxis(full.reshape(r, N_DEV, n // N_DEV), 1, 0), r, wire_cols)


def _slabs(name, g):
    if name == "w_in":
        return _w_in_padded_to_slabs(g, WIRE_SHAPE["w_in"][1])
    if name == "w_uq":
        return _full_to_cols(_w_uq_from_headsplit(g), WIRE_SHAPE["w_uq"][1])
    if name == "w_ukv":
        return _full_to_cols(g, g.shape[1] // N_DEV)
    if name == "conv_w":
        return _pad2(_full_to_cols(g.astype(WIRE_DTYPE), g.shape[1] // N_DEV), *WIRE_SHAPE["conv_w"])
    if name == "w_up":
        return g
    return g.reshape(N_DEV, -1, g.shape[-1])


def kernel(x, positions, attn_norm_w, w_in, q_lat_norm_w, w_uq, kv_lat_norm_w, w_ukv, q_norm_w, k_norm_w, mla_out_norm_w, conv_w, a_log, dt_bias, gdn_norm_w, w_out, mlp_norm_w, w_up, w_down, loss_target, m_attn_norm_w, m_w_in, m_q_lat_norm_w, m_w_uq, m_kv_lat_norm_w, m_w_ukv, m_q_norm_w, m_k_norm_w, m_mla_out_norm_w, m_conv_w, m_a_log, m_dt_bias, m_gdn_norm_w, m_w_out, m_mlp_norm_w, m_w_up, m_w_down, v_attn_norm_w, v_w_in, v_q_lat_norm_w, v_w_uq, v_kv_lat_norm_w, v_w_ukv, v_q_norm_w, v_k_norm_w, v_mla_out_norm_w, v_conv_w, v_a_log, v_dt_bias, v_gdn_norm_w, v_w_out, v_mlp_norm_w, v_w_up, v_w_down):
    env = dict(locals())
    W = {n: env[n][0] for n in ALL_W}
    Mo = {n: env["m_" + n][0] for n in ALL_W}
    Vo = {n: env["v_" + n][0] for n in ALL_W}

    two_d = lambda a: a.reshape(1, -1) if a.ndim == 1 else a
    D = x.shape[-1]

    s_in, s_uq, s_ukv, s_conv = _gather_weights([
        _pad2(W["w_in"].astype(WIRE_DTYPE), *WIRE_SHAPE["w_in"]),
        _pad2(W["w_uq"].astype(WIRE_DTYPE), *WIRE_SHAPE["w_uq"]),
        W["w_ukv"].astype(WIRE_DTYPE), _pad2(W["conv_w"], *WIRE_SHAPE["conv_w"])])
    late = [W["w_out"].astype(WIRE_DTYPE), W["w_up"].astype(WIRE_DTYPE), W["w_down"].astype(WIRE_DTYPE)]

    loss, grad_x, parts, gs = _local_step(
        x, positions, loss_target, two_d(W["attn_norm_w"]), _w_in_shards_to_padded(s_in),
        two_d(W["q_lat_norm_w"]), _cols_to_full(s_uq, W["w_uq"].shape[1]), two_d(W["kv_lat_norm_w"]),
        _cols_to_full(s_ukv, W["w_ukv"].shape[1]), two_d(W["q_norm_w"]), two_d(W["k_norm_w"]),
        W["mla_out_norm_w"], _cols_to_full(s_conv[:, :CONV_W], W["conv_w"].shape[1]), two_d(W["a_log"]),
        two_d(W["dt_bias"]), two_d(W["gdn_norm_w"]), None, two_d(W["mlp_norm_w"]), None, None,
        late_shards=late, exchange=True)
    done = {n: _reduce_adamw(parts[n], W[n], Mo[n], Vo[n], "adamw_" + n) for n in BIG}
    names = [n for n, *_ in SMALL_LAYOUT]
    tiles = _gather_small_grads(gs, jnp.full((1, LANES), loss, F32))
    small, loss = _adamw_replicated(tiles, [two_d(W[n]) for n in names], [two_d(Mo[n]) for n in names],
                                    [two_d(Vo[n]) for n in names])
    for i, n in enumerate(names):
        done[n] = [small[kind][i] for kind in range(4)]
    res = [done[n][kind].reshape(env[n].shape) for kind in range(4) for n in ALL_W]
    return (loss, grad_x, *res)
```

```python
import functools

import jax
import jax.numpy as jnp
from jax import lax
from jax.experimental import pallas as pl
from jax.experimental.pallas import tpu as pltpu

F32 = jnp.float32
MXU_DTYPE = jnp.bfloat16
WIRE_DTYPE = jnp.bfloat16
SDS = jax.ShapeDtypeStruct
HIGHEST = lax.Precision.HIGHEST
MESH_ID = pl.DeviceIdType.MESH

D_MODEL = 1024
MLA_HEADS = 4
Q_LORA = 256
KV_LORA = 256
NOPE = 128
ROPE = 64
QK_DIM = NOPE + ROPE
V_DIM = 128
ROPE_THETA = 10000.0
GDN_HEADS = 4
GDN_DIM = 128
GDN_WIDTH = GDN_HEADS * GDN_DIM
CONV_W = 4
CHUNK = 64
D_FF = 4 * D_MODEL
EPS = 1e-6
ATT_SCALE = QK_DIM ** -0.5
GDN_QSCALE = GDN_DIM ** -0.5
N_DEV = 8
ATTN_BLOCK = 512
ATTN_CHAINS = 2
MLP_FWD_SHARDS = 4
MLP_BWD_SHARDS = 4

ADAM_LR = 0.001
ADAM_B1 = 0.9
ADAM_B2 = 0.999
ADAM_EPS = 1e-08
ADAM_WD = 0.01
ADAM_STEP = 10

LANES = 128
SUBLANES = 8
VMEM_LIMIT = 60 * 1024 * 1024

P_GQKV, P_GZ, P_QLAT, P_KVLAT, P_KPE, P_GAB = 0, 1536, 2048, 2304, 2560, 2688
P_WIDTH = 2816
O_QLAT, O_KVLAT, O_KPE, O_GQKV, O_GZ, O_GAB, O_END = 0, 256, 512, 576, 2112, 2624, 2632


def _params(sem=None, vmem=VMEM_LIMIT):
    kw = dict(vmem_limit_bytes=vmem)
    if sem is not None:
        kw["dimension_semantics"] = sem
    return pltpu.CompilerParams(**kw)


def _mm(a, b):
    return jnp.dot(a.astype(MXU_DTYPE), b.astype(MXU_DTYPE), preferred_element_type=F32)


def _mm_nt(a, b):
    return lax.dot_general(a.astype(MXU_DTYPE), b.astype(MXU_DTYPE), (((1,), (1,)), ((), ())),
                           preferred_element_type=F32)


def _mm_tn(a, b):
    return lax.dot_general(a.astype(MXU_DTYPE), b.astype(MXU_DTYPE), (((0,), (0,)), ((), ())),
                           preferred_element_type=F32)


def _split(a):
    hi = a.astype(MXU_DTYPE)
    return hi, (a - hi.astype(F32)).astype(MXU_DTYPE)


def _mm_split(a, b):
    (ah, al), (bh, bl) = a, b
    dot = lambda x, y: jnp.dot(x, y, preferred_element_type=F32)
    if MXU_DTYPE == F32:
        return dot(ah, bh)
    return dot(ah, bh) + dot(ah, bl) + dot(al, bh)


def _mm_exact(a, b):
    return _mm_split(_split(a), _split(b))


def _row_sum(v, on_mxu=False):
    if not on_mxu:
        return jnp.sum(v, axis=-1, keepdims=True)
    d = v.shape[-1]
    ones = jnp.ones((d, LANES), MXU_DTYPE)
    s = sum(jnp.dot(p, ones, preferred_element_type=F32) for p in _split(v))
    return s[:, :d] if d <= LANES else jnp.tile(s, (1, d // LANES))


def _rms(x, w, on_mxu=False):
    r = lax.rsqrt(_row_sum(x * x, on_mxu) * (1.0 / x.shape[-1]) + EPS)
    return x * r * w, r


def _rms_bwd(dy, x, w, r, on_mxu=False):
    xh = x * r
    dyw = dy * w
    dx = r * (dyw - xh * (_row_sum(dyw * xh, on_mxu) * (1.0 / x.shape[-1])))
    dw = jnp.sum(dy * xh, axis=0, keepdims=True)
    return dx, dw


def _l2n(x, scale):
    return x * (lax.rsqrt(_row_sum(x * x) + EPS) * scale)


def _l2n_bwd(dy, x, scale):
    r = lax.rsqrt(_row_sum(x * x) + EPS)
    xh = x * r
    return (scale * r) * (dy - xh * _row_sum(dy * xh))


def _rot(t):
    return jnp.concatenate([-t[:, ROPE // 2:], t[:, :ROPE // 2]], axis=-1)


def _rot_t(t):
    return jnp.concatenate([t[:, ROPE // 2:], -t[:, :ROPE // 2]], axis=-1)


def _rope(t, cos, sin):
    return t * cos + _rot(t) * sin


def _rope_bwd(d, cos, sin):
    return d * cos + _rot_t(d * sin)


def _sigmoid(x):
    return jax.nn.sigmoid(x)


def _shift_down(x, halo, j):
    if j == 0:
        return x
    xr = pltpu.roll(x, j, 0)
    hr = pltpu.roll(halo, j, 0)
    row = lax.broadcasted_iota(jnp.int32, halo.shape, 0)
    top = jnp.where(row < j, hr, xr[:SUBLANES])
    return jnp.concatenate([top, xr[SUBLANES:]], axis=0)


def _shift_up(x, nxt, j):
    if j == 0:
        return x
    n = x.shape[0]
    xr = pltpu.roll(x, n - j, 0)
    nr = pltpu.roll(nxt, SUBLANES - j, 0)
    row = lax.broadcasted_iota(jnp.int32, nxt.shape, 0)
    bot = jnp.where(row >= SUBLANES - j, nr, xr[n - SUBLANES:])
    return jnp.concatenate([xr[:n - SUBLANES], bot], axis=0)


def _chunk_cumsum(y, row_in_chunk):
    s = 1
    while s < CHUNK:
        y = y + jnp.where(row_in_chunk >= s, pltpu.roll(y, s, 0), 0.0)
        s *= 2
    return y


def _chunk_rev_cumsum(y, row_in_chunk):
    n = y.shape[0]
    s = 1
    while s < CHUNK:
        y = y + jnp.where(row_in_chunk + s < CHUNK, pltpu.roll(y, n - s, 0), 0.0)
        s *= 2
    return y


def _together(generators):
    alive = list(generators)
    while alive:
        nxt = []
        for g in alive:
            try:
                next(g)
                nxt.append(g)
            except StopIteration:
                pass
        alive = nxt
        yield


def _lockstep(generators):
    for _ in _together(generators):
        pass


def _pick_lane(tile, lane, idx):
    return jnp.sum(jnp.where(lane == idx, tile, 0.0), axis=-1, keepdims=True)


def _divisor_tile(n, cap, unit=LANES):
    best = unit
    t = unit
    while t <= min(n, cap):
        if n % t == 0:
            best = t
        t += unit
    return n if n <= cap else best


def _in_proj(x2, w_an, w_in_p, conv_w, alog_l, dt_l, S):
    T, D = x2.shape
    N = w_in_p.shape[1]
    tm = min(512, S)
    assert S % tm == 0 and T % tm == 0, "a token tile must not straddle two sequences"
    tiles_per_seq = S // tm
    C3 = 3 * GDN_WIDTH
    H = GDN_HEADS

    def body(x_ref, wn_ref, w_ref, cw_ref, alog_ref, dt_ref, proj_ref, xn_ref, q_out, k_out, v_out, gates_out,
             halo_s):
        xn, _ = _rms(x_ref[...], wn_ref[...])
        xn = xn.astype(MXU_DTYPE)
        xn_ref[...] = xn
        proj = jnp.dot(xn, w_ref[...], preferred_element_type=F32)
        proj_ref[...] = proj
        u = proj[:, P_GQKV:P_GQKV + C3]

        @pl.when(pl.program_id(0) == 0)
        def _():
            halo_s[...] = jnp.zeros_like(halo_s)

        halo = jnp.where(pl.program_id(0) % tiles_per_seq == 0, 0.0, halo_s[...])
        halo_s[...] = u[tm - SUBLANES:, :]
        c, _ = _conv_taps(u, halo, cw_ref[...])
        a = c * _sigmoid(c)
        for h in range(H):
            xq = a[:, h * GDN_DIM:(h + 1) * GDN_DIM]
            xk = a[:, GDN_WIDTH + h * GDN_DIM:GDN_WIDTH + (h + 1) * GDN_DIM]
            q_out[h] = _l2n(xq, GDN_QSCALE)
            k_out[h] = _l2n(xk, 1.0)
            v_out[h] = a[:, 2 * GDN_WIDTH + h * GDN_DIM:2 * GDN_WIDTH + (h + 1) * GDN_DIM]
        lane = lax.broadcasted_iota(jnp.int32, (tm, LANES), 1)
        ric = lax.broadcasted_iota(jnp.int32, (tm, LANES), 0) % CHUNK
        g, beta = _gate_values(proj[:, P_GAB:P_GAB + LANES], alog_ref[...], dt_ref[...], lane)
        gates_out[...] = _chunk_cumsum(g, ric) + beta

    hspec = pl.BlockSpec((H, tm, GDN_DIM), lambda i: (0, i, 0))
    vec = pl.BlockSpec((1, LANES), lambda i: (0, 0))
    return pl.pallas_call(
        body, grid=(T // tm,), name="in_proj",
        in_specs=[pl.BlockSpec((tm, D), lambda i: (i, 0)), pl.BlockSpec((1, D), lambda i: (0, 0)),
                  pl.BlockSpec((D, N), lambda i: (0, 0)), pl.BlockSpec((CONV_W, C3), lambda i: (0, 0)), vec, vec],
        out_specs=[pl.BlockSpec((tm, N), lambda i: (i, 0)), pl.BlockSpec((tm, D), lambda i: (i, 0)),
                   hspec, hspec, hspec, pl.BlockSpec((tm, LANES), lambda i: (i, 0))],
        out_shape=[SDS((T, N), F32), SDS((T, D), MXU_DTYPE)] + [SDS((H, T, GDN_DIM), F32)] * 3
                  + [SDS((T, LANES), F32)],
        scratch_shapes=[pltpu.VMEM((SUBLANES, C3), F32)],
        compiler_params=_params(("arbitrary",)),
    )(x2, w_an, w_in_p, conv_w, alog_l, dt_l)


def _mla_pre(proj, cosf, sinf, w_qln, w_kvln, w_uq_p, w_ukv, qnw, knw, transfer=None):
    T = proj.shape[0]
    tm = min(256, T)
    H = MLA_HEADS

    def body(ql_ref, kvl_ref, kpe_ref, cos_ref, sin_ref, wq_ref, wkv_ref, uq_ref, ukv_ref, qnw_ref, knw_ref,
             q_out, k_out, v_out):
        rms = functools.partial(_rms, on_mxu=True)
        cos, sin = cos_ref[...], sin_ref[...]
        qnw_, knw_ = qnw_ref[...], knw_ref[...]
        qn, _ = rms(ql_ref[...], wq_ref[...])
        kvn, _ = rms(kvl_ref[...], wkv_ref[...])
        qraw = _mm(qn, uq_ref[...])
        kvraw = _mm(kvn, ukv_ref[...])
        kpe = _rope(rms(kpe_ref[...][:, :ROPE], knw_[:, NOPE:])[0], cos, sin)
        for h in range(H):
            qn_h = rms(qraw[:, h * NOPE:(h + 1) * NOPE], qnw_[:, :NOPE])[0]
            qp_h = _rope(rms(qraw[:, H * NOPE + h * ROPE:H * NOPE + (h + 1) * ROPE], qnw_[:, NOPE:])[0], cos, sin)
            q_out[h] = (jnp.concatenate([qn_h, qp_h], axis=-1) * ATT_SCALE).astype(MXU_DTYPE)
            kn_h = rms(kvraw[:, h * 256:h * 256 + NOPE], knw_[:, :NOPE])[0]
            k_out[h] = jnp.concatenate([kn_h, kpe], axis=-1).astype(MXU_DTYPE)
            v_out[h] = kvraw[:, h * 256 + NOPE:(h + 1) * 256].astype(MXU_DTYPE)

    full = lambda a: pl.BlockSpec(a.shape, lambda i: (0,) * a.ndim)
    return _call_beside(
        body, transfer, grid=(T // tm,), name="mla_pre", scratch_shapes=[], semantics=("arbitrary",),
        args=(proj, proj, proj, cosf, sinf, w_qln, w_kvln, w_uq_p, w_ukv, qnw, knw),
        in_specs=[pl.BlockSpec((tm, 256), lambda i: (i, P_QLAT // 256)),
                  pl.BlockSpec((tm, 256), lambda i: (i, P_KVLAT // 256)),
                  pl.BlockSpec((tm, 128), lambda i: (i, P_KPE // 128)),
                  pl.BlockSpec((tm, ROPE), lambda i: (i, 0)), pl.BlockSpec((tm, ROPE), lambda i: (i, 0)),
                  full(w_qln), full(w_kvln), full(w_uq_p), full(w_ukv), full(qnw), full(knw)],
        out_specs=[pl.BlockSpec((H, tm, QK_DIM), lambda i: (0, i, 0)),
                   pl.BlockSpec((H, tm, QK_DIM), lambda i: (0, i, 0)),
                   pl.BlockSpec((H, tm, V_DIM), lambda i: (0, i, 0))],
        out_shape=[SDS((H, T, QK_DIM), MXU_DTYPE), SDS((H, T, QK_DIM), MXU_DTYPE), SDS((H, T, V_DIM), MXU_DTYPE)])


def _attn_fwd(q4, k4, v4, B, S, transfer=None):
    H = MLA_HEADS
    bq = min(ATTN_BLOCK, S)
    nq = S // bq
    rows = bq // ATTN_CHAINS

    def body(q_ref, k_ref, v_ref, o_ref, lse_ref):
        col = lax.broadcasted_iota(jnp.int32, (rows, bq), 1)
        row = lax.broadcasted_iota(jnp.int32, (rows, bq), 0)

        def q_step(qi, carry):
            qs = pl.multiple_of(qi * bq, bq)
            qsub = [q_ref[0, pl.ds(qs + j * rows, rows), :] for j in range(ATTN_CHAINS)]

            def k_block(ks, cs, diagonal):
                k = k_ref[0, pl.ds(ks, bq), :]
                v = v_ref[0, pl.ds(ks, bq), :]
                out = [None] * ATTN_CHAINS

                def chain(j):
                    m, l, acc = cs[j]
                    s = _mm_nt(qsub[j], k)
                    yield
                    if diagonal:
                        s = jnp.where(col <= row + j * rows, s, -jnp.inf)
                    m_new = jnp.maximum(m, jnp.max(s, axis=-1, keepdims=True))
                    p = jnp.exp(s - m_new)
                    a = jnp.exp(m - m_new)
                    l_new = a * l + jnp.sum(p, axis=-1, keepdims=True)
                    yield
                    out[j] = (m_new, l_new, a * acc + _mm(p, v))

                _lockstep([chain(j) for j in range(ATTN_CHAINS)])
                return tuple(out)

            init = tuple((jnp.full((rows, 1), -jnp.inf, F32), jnp.zeros((rows, 1), F32),
                          jnp.zeros((rows, V_DIM), F32)) for _ in range(ATTN_CHAINS))
            cs = lax.fori_loop(0, qi, lambda kj, c: k_block(pl.multiple_of(kj * bq, bq), c, False), init)
            for j, (m, l, acc) in enumerate(k_block(qs, cs, True)):
                o_ref[0, pl.ds(qs + j * rows, rows), :] = acc / l
                lse_ref[0, pl.ds(qs + j * rows, rows), :] = m + jnp.log(l)
            return carry

        lax.fori_loop(0, nq, q_step, 0)

    spec = lambda d: pl.BlockSpec((1, S, d), lambda h, b: (h, b, 0))
    return _call_beside(
        body, transfer, grid=(H, B), name="attn_fwd",
        in_specs=[spec(QK_DIM), spec(QK_DIM), spec(V_DIM)],
        out_specs=[spec(V_DIM), spec(1)],
        out_shape=[SDS((H, B * S, V_DIM), F32), SDS((H, B * S, 1), F32)],
        scratch_shapes=[], semantics=("arbitrary", "arbitrary"), args=(q4, k4, v4))


def _conv_taps(u, halo, w):
    sh = [_shift_down(u, halo, j) for j in range(CONV_W)]
    c = w[0:1] * sh[3] + w[1:2] * sh[2] + w[2:3] * sh[1] + w[3:4] * sh[0]
    return c, sh


def _gate_values(gab, alog_l, dt_l, lane):
    g = -jnp.exp(alog_l) * jax.nn.softplus(gab + dt_l)
    g = jnp.where(lane < GDN_HEADS, g, 0.0)
    beta = jnp.where((lane >= GDN_HEADS) & (lane < 2 * GDN_HEADS), _sigmoid(gab), 0.0)
    return g, beta


def _unit_lower_inverses(Ls, eye):
    Ps = [eye - L for L in Ls]
    Ms = [_split(-L) for L in Ls]
    for _ in range(5):
        sq = [_mm_split(m, m) for m in Ms]
        Ms = [_split(s) for s in sq]
        Ps = [p + _mm_split(_split(p), m) for p, m in zip(Ps, Ms)]
    return Ps


def _chunk_decays(gt, lane, h, ri, ci, rcol):
    Gc = _pick_lane(gt, lane, h)
    bt = _pick_lane(gt, lane, h + GDN_HEADS)
    Gb = jnp.broadcast_to(Gc, (CHUNK, CHUNK))
    Gam = jnp.where(ri >= ci, jnp.exp(Gb - Gb.T), 0.0)
    Gl = jnp.sum(jnp.where(rcol == CHUNK - 1, Gc, 0.0), axis=0, keepdims=True)
    return Gc, bt, Gam, jnp.exp(Gc), jnp.exp(Gl - Gc), jnp.exp(Gl)


GDN_FWD_UNROLL = 16
GDN_BWD_UNROLL = 8
GDN_RECUR_STEPS_PER_STAGE = 2


def _gdn_fwd(qg, kg, vg, gates, B, S, transfer=None):
    H, D, C = GDN_HEADS, GDN_DIM, CHUNK
    NC = S // C
    P = 2 if B % 2 == 0 else 1
    Sb, NCb = P * S, P * NC
    U = GDN_FWD_UNROLL if NCb % GDN_FWD_UNROLL == 0 else 1
    NG = NCb // U

    def body(q_ref, k_ref, v_ref, g_ref, o_ref, st_ref, ai_ref, u_ref, w_ref, q2_s, au_s, bc_s, w2_s, el_s):
        h = pl.program_id(0)
        lane = lax.broadcasted_iota(jnp.int32, (C, LANES), 1)
        ri = lax.broadcasted_iota(jnp.int32, (C, C), 0)
        ci = lax.broadcasted_iota(jnp.int32, (C, C), 1)
        rcol = lax.broadcasted_iota(jnp.int32, (C, 1), 0)
        eye = (ri == ci).astype(F32)

        def group(gi, c):
            ns = [gi * U + j for j in range(U)]
            css = [pl.multiple_of(n * C, C) for n in ns]
            qs = [q_ref[0, pl.ds(cs, C), :] for cs in css]
            ks = [k_ref[0, pl.ds(cs, C), :] for cs in css]
            vs = [v_ref[0, pl.ds(cs, C), :] for cs in css]
            decs = [_chunk_decays(g_ref[pl.ds(cs, C), :], lane, h, ri, ci, rcol) for cs in css]
            qks = [_mm_nt(jnp.concatenate([q, k], axis=0), k) for q, k in zip(qs, ks)]
            ainvs = _unit_lower_inverses(
                [jnp.where(ri > ci, d[1] * qk[C:] * d[2], 0.0) for qk, d in zip(qks, decs)], eye)
            sols = [_mm_exact(a, jnp.concatenate([v * d[1], k * (d[1] * d[3])], axis=-1))
                    for a, k, v, d in zip(ainvs, ks, vs, decs)]
            atuw = [_mm(qk[:C] * d[2], sol) for qk, d, sol in zip(qks, decs, sols)]
            kduw = [_mm_tn(k * d[4], sol) for k, d, sol in zip(ks, decs, sols)]
            for n, cs, q, a, sol, au, ku, (Gc, bt, Gam, e, f, eL) in zip(ns, css, qs, ainvs, sols, atuw, kduw, decs):
                u_ref[0, pl.ds(cs, C), :] = sol[:, :D]
                w_ref[0, pl.ds(cs, C), :] = sol[:, D:]
                au_s[pl.ds(cs, C), :] = au[:, :D]
                q2_s[pl.ds(cs, C), :] = q * e - au[:, D:]
                bc_s[n] = ku[:, :D]
                w2_s[n] = ku[:, D:]
                el_s[n] = jnp.broadcast_to(eL, (SUBLANES, LANES))
                ai_ref[0, n] = a.T
            return c

        lax.fori_loop(0, NG, group, 0)

        def step(n, states):
            new = []
            for p, S_ in enumerate(states):
                m = p * NC + n
                cs = pl.multiple_of(m * C, C)
                o_ref[0, pl.ds(cs, C), :] = _mm(q2_s[pl.ds(cs, C), :], S_) + au_s[pl.ds(cs, C), :]
                st_ref[0, m] = S_
                new.append(S_ * el_s[m, 0:1, :] + bc_s[m] - _mm(w2_s[m], S_))
            return tuple(new)

        lax.fori_loop(0, NC, step, tuple(jnp.zeros((D, D), F32) for _ in range(P)))

    spec = pl.BlockSpec((1, Sb, D), lambda h, b: (h, b, 0))
    return _call_beside(
        body, transfer, grid=(H, B // P), name="gdn_fwd",
        in_specs=[spec, spec, spec, pl.BlockSpec((Sb, LANES), lambda h, b: (b, 0))],
        out_specs=[spec, pl.BlockSpec((1, NCb, D, D), lambda h, b: (h, b, 0, 0)),
                   pl.BlockSpec((1, NCb, C, C), lambda h, b: (h, b, 0, 0)), spec, spec],
        out_shape=[SDS((H, B * S, D), F32), SDS((H, B * NC, D, D), F32), SDS((H, B * NC, C, C), F32),
                   SDS((H, B * S, D), F32), SDS((H, B * S, D), F32)],
        scratch_shapes=[pltpu.VMEM((Sb, D), F32), pltpu.VMEM((Sb, D), F32), pltpu.VMEM((NCb, D, D), F32),
                        pltpu.VMEM((NCb, D, D), F32), pltpu.VMEM((NCb, SUBLANES, LANES), F32)],
        semantics=("arbitrary", "arbitrary"), args=(qg, kg, vg, gates))


def _mix_out(o_mla, o_gdn, proj, x2, mla_w, gdn_w, w_out):
    T, D = x2.shape
    tm = min(512, T)
    H = MLA_HEADS

    def body(om_ref, og_ref, z_ref, x_ref, mw_ref, gw_ref, w_ref, h_ref, mix_ref):
        z = z_ref[...]
        parts = [_rms(om_ref[h], mw_ref[h:h + 1, :])[0] for h in range(H)]
        for h in range(GDN_HEADS):
            zh = z[:, h * GDN_DIM:(h + 1) * GDN_DIM]
            parts.append(_rms(og_ref[h], gw_ref[...])[0] * (zh * _sigmoid(zh)))
        mix = jnp.concatenate(parts, axis=-1).astype(MXU_DTYPE)
        mix_ref[...] = mix
        h_ref[...] = x_ref[...] + jnp.dot(mix, w_ref[...], preferred_element_type=F32)

    hspec = pl.BlockSpec((H, tm, V_DIM), lambda i: (0, i, 0))
    return pl.pallas_call(
        body, grid=(T // tm,), name="mix_out",
        in_specs=[hspec, hspec, pl.BlockSpec((tm, GDN_WIDTH), lambda i: (i, P_GZ // GDN_WIDTH)),
                  pl.BlockSpec((tm, D), lambda i: (i, 0)),
                  pl.BlockSpec((H, V_DIM), lambda i: (0, 0)), pl.BlockSpec((1, GDN_DIM), lambda i: (0, 0)),
                  pl.BlockSpec((D, D), lambda i: (0, 0))],
        out_specs=[pl.BlockSpec((tm, D), lambda i: (i, 0)), pl.BlockSpec((tm, D), lambda i: (i, 0))],
        out_shape=[SDS((T, D), F32), SDS((T, D), MXU_DTYPE)],
        compiler_params=_params(("arbitrary",)),
    )(o_mla, o_gdn, proj, x2, mla_w, gdn_w, w_out)


def _mlp_fwd(h2, w_mn, w_up, w_down, target):
    T, D = h2.shape
    ns, _, ts = w_up.shape
    F = ns * ts
    tm = min(512, T)
    G = MLP_FWD_SHARDS
    tf, nf = G * ts, ns // G

    def body(h_ref, wn_ref, up_w, down_w, t_ref, up_ref, hn_ref, dy_ref, loss_ref, dyb_ref, y_acc):
        j = pl.program_id(1)

        @pl.when(j == 0)
        def _():
            hn_ref[...] = _rms(h_ref[...], wn_ref[...])[0].astype(MXU_DTYPE)
            y_acc[...] = h_ref[...]

        parts = []
        for c in range(G):
            up = jnp.dot(hn_ref[...], up_w[c], preferred_element_type=F32)
            up_ref[:, c * ts:(c + 1) * ts] = up.astype(MXU_DTYPE)
            r = jnp.maximum(up, 0.0)
            parts.append(_mm(r * r, down_w[c * ts:(c + 1) * ts, :]))
        y_acc[...] += functools.reduce(jnp.add, parts)

        @pl.when(j == nf - 1)
        def _():
            err = y_acc[...] - t_ref[...]
            dy_ref[...] = err / D
            dyb_ref[...] = (err / D).astype(MXU_DTYPE)
            loss_ref[...] = jnp.full((1, SUBLANES, LANES), jnp.sum(err * err), F32)

    return pl.pallas_call(
        body, grid=(T // tm, nf), name="mlp_fwd",
        in_specs=[pl.BlockSpec((tm, D), lambda i, j: (i, 0)), pl.BlockSpec((1, D), lambda i, j: (0, 0)),
                  pl.BlockSpec((G, D, ts), lambda i, j: (j, 0, 0)), pl.BlockSpec((tf, D), lambda i, j: (j, 0)),
                  pl.BlockSpec((tm, D), lambda i, j: (i, 0))],
        out_specs=[pl.BlockSpec((tm, tf), lambda i, j: (i, j)), pl.BlockSpec((tm, D), lambda i, j: (i, 0)),
                   pl.BlockSpec((tm, D), lambda i, j: (i, 0)),
                   pl.BlockSpec((1, SUBLANES, LANES), lambda i, j: (i, 0, 0)),
                   pl.BlockSpec((tm, D), lambda i, j: (i, 0))],
        out_shape=[SDS((T, F), MXU_DTYPE), SDS((T, D), MXU_DTYPE), SDS((T, D), F32),
                   SDS((T // tm, SUBLANES, LANES), F32), SDS((T, D), MXU_DTYPE)],
        scratch_shapes=[pltpu.VMEM((tm, D), F32)],
        compiler_params=_params(("arbitrary", "arbitrary")),
    )(h2, w_mn, w_up, w_down, target)


def _mlp_bwd(dy, dyb, up, h2, w_mn, w_up, w_down, transfer=None):
    T, D = h2.shape
    ns, _, ts = w_up.shape
    F = ns * ts
    tm = min(512, T)
    G = MLP_BWD_SHARDS
    tf, nf = G * ts, ns // G

    def body(dy_ref, dyb_ref, up_ref, h_ref, wn_ref, up_w, down_w, dh_ref, dhb_ref, dup_ref, dwn_ref, acc):
        i, j = pl.program_id(0), pl.program_id(1)

        @pl.when((i == 0) & (j == 0))
        def _():
            dwn_ref[...] = jnp.zeros_like(dwn_ref)

        @pl.when(j == 0)
        def _():
            acc[...] = jnp.zeros_like(acc)

        parts = []
        for c in range(G):
            cols = slice(c * ts, (c + 1) * ts)
            r = jnp.maximum(up_ref[:, cols].astype(F32), 0.0)
            dup = (_mm_nt(dyb_ref[...], down_w[cols, :]) * (2.0 * r)).astype(MXU_DTYPE)
            dup_ref[:, cols] = dup
            parts.append(_mm_nt(dup, up_w[c]))
        acc[...] += functools.reduce(jnp.add, parts)

        @pl.when(j == nf - 1)
        def _():
            hv = h_ref[...]
            _, rr = _rms(hv, wn_ref[...])
            dx, dw = _rms_bwd(acc[...], hv, wn_ref[...], rr)
            dh = dy_ref[...] + dx
            dh_ref[...] = dh
            dhb_ref[...] = dh.astype(MXU_DTYPE)
            dwn_ref[...] += dw

    row = lambda i, j: (i, 0)
    return _call_beside(
        body, transfer, grid=(T // tm, nf), name="mlp_bwd",
        in_specs=[pl.BlockSpec((tm, D), row), pl.BlockSpec((tm, D), row), pl.BlockSpec((tm, tf), lambda i, j: (i, j)),
                  pl.BlockSpec((tm, D), row), pl.BlockSpec((1, D), lambda i, j: (0, 0)),
                  pl.BlockSpec((G, D, ts), lambda i, j: (j, 0, 0)), pl.BlockSpec((tf, D), lambda i, j: (j, 0))],
        out_specs=[pl.BlockSpec((tm, D), row), pl.BlockSpec((tm, D), row),
                   pl.BlockSpec((tm, tf), lambda i, j: (i, j)), pl.BlockSpec((1, D), lambda i, j: (0, 0))],
        out_shape=[SDS((T, D), F32), SDS((T, D), MXU_DTYPE), SDS((T, F), MXU_DTYPE), SDS((1, D), F32)],
        scratch_shapes=[pltpu.VMEM((tm, D), F32)], semantics=("arbitrary", "arbitrary"),
        args=(dy, dyb, up, h2, w_mn, w_up, w_down))


def _mix_bwd(dhb, o_mla, o_gdn, proj, mla_w, gdn_w, w_out):
    T, D = dhb.shape
    tm = min(512, T)
    H = MLA_HEADS

    def body(dh_ref, om_ref, og_ref, z_ref, mw_ref, gw_ref, w_ref, dom_ref, dog_ref, dz_ref, dmw_ref, dgw_ref,
             delta_ref):
        @pl.when(pl.program_id(0) == 0)
        def _():
            dmw_ref[...] = jnp.zeros_like(dmw_ref)
            dgw_ref[...] = jnp.zeros_like(dgw_ref)

        dmix = _mm_nt(dh_ref[...], w_ref[...])
        z = z_ref[...]
        dmw, dzs = [], []
        dgw = jnp.zeros((1, GDN_DIM), F32)
        for h in range(H):
            o = om_ref[h]
            w = mw_ref[h:h + 1, :]
            _, r = _rms(o, w)
            dx, dw = _rms_bwd(dmix[:, h * V_DIM:(h + 1) * V_DIM], o, w, r)
            dom_ref[h] = dx.astype(MXU_DTYPE)
            delta_ref[h] = jnp.sum(dx * o, axis=-1, keepdims=True)
            dmw.append(dw)
        for h in range(GDN_HEADS):
            o = og_ref[h]
            w = gw_ref[...]
            zh = z[:, h * GDN_DIM:(h + 1) * GDN_DIM]
            sg = _sigmoid(zh)
            yn, r = _rms(o, w)
            dy = dmix[:, H * V_DIM + h * GDN_DIM:H * V_DIM + (h + 1) * GDN_DIM]
            dzs.append(dy * yn * (sg * (1.0 + zh * (1.0 - sg))))
            dx, dw = _rms_bwd(dy * (zh * sg), o, w, r)
            dog_ref[h] = dx.astype(MXU_DTYPE)
            dgw = dgw + dw
        dz_ref[...] = jnp.concatenate(dzs, axis=-1).astype(MXU_DTYPE)
        dmw_ref[...] += jnp.concatenate(dmw, axis=0)
        dgw_ref[...] += dgw

    hspec = pl.BlockSpec((H, tm, V_DIM), lambda i: (0, i, 0))
    return pl.pallas_call(
        body, grid=(T // tm,), name="mix_bwd",
        in_specs=[pl.BlockSpec((tm, D), lambda i: (i, 0)), hspec, hspec,
                  pl.BlockSpec((tm, GDN_WIDTH), lambda i: (i, P_GZ // GDN_WIDTH)),
                  pl.BlockSpec((H, V_DIM), lambda i: (0, 0)), pl.BlockSpec((1, GDN_DIM), lambda i: (0, 0)),
                  pl.BlockSpec((D, D), lambda i: (0, 0))],
        out_specs=[hspec, hspec, pl.BlockSpec((tm, GDN_WIDTH), lambda i: (i, 0)),
                   pl.BlockSpec((H, V_DIM), lambda i: (0, 0)), pl.BlockSpec((1, GDN_DIM), lambda i: (0, 0)),
                   pl.BlockSpec((H, tm, 1), lambda i: (0, i, 0))],
        out_shape=[SDS((H, T, V_DIM), MXU_DTYPE), SDS((H, T, GDN_DIM), MXU_DTYPE), SDS((T, GDN_WIDTH), MXU_DTYPE),
                   SDS((H, V_DIM), F32), SDS((1, GDN_DIM), F32), SDS((H, T, 1), F32)],
        compiler_params=_params(("arbitrary",)),
    )(dhb, o_mla, o_gdn, proj, mla_w, gdn_w, w_out)


def _attn_bwd(q4, k4, v4, do4, delta4, lse4, B, S, transfer=None):
    H = MLA_HEADS
    bq = min(ATTN_BLOCK, S)
    nq = S // bq
    rows = bq // ATTN_CHAINS

    def body(q_ref, k_ref, v_ref, do_ref, delta_ref, lse_ref, dq_ref, dk_ref, dv_ref):
        dq_ref[...] = jnp.zeros_like(dq_ref)
        dk_ref[...] = jnp.zeros_like(dk_ref)
        dv_ref[...] = jnp.zeros_like(dv_ref)

        col = lax.broadcasted_iota(jnp.int32, (rows, bq), 1)
        row = lax.broadcasted_iota(jnp.int32, (rows, bq), 0)

        def k_step(kj, carry):
            ks = pl.multiple_of(kj * bq, bq)
            k = k_ref[0, pl.ds(ks, bq), :]
            v = v_ref[0, pl.ds(ks, bq), :]

            def q_block(qs, diagonal):
                dks, dvs = [None] * ATTN_CHAINS, [None] * ATTN_CHAINS

                def chain(j):
                    sl = pl.ds(qs + j * rows, rows)
                    q = q_ref[0, sl, :]
                    do = do_ref[0, sl, :].astype(MXU_DTYPE)
                    s = _mm_nt(q, k)
                    dp = _mm_nt(do, v)
                    yield
                    p = jnp.exp(s - lse_ref[0, sl, :])
                    if diagonal:
                        p = jnp.where(col <= row + j * rows, p, 0.0)
                    ds = p * (dp - delta_ref[0, sl, :])
                    yield
                    dvs[j] = _mm_tn(p, do)
                    dks[j] = _mm_tn(ds, q)
                    dq_ref[0, sl, :] += _mm(ds, k)

                _lockstep([chain(j) for j in range(ATTN_CHAINS)])
                dv_ref[0, pl.ds(ks, bq), :] += functools.reduce(jnp.add, dvs)
                dk_ref[0, pl.ds(ks, bq), :] += functools.reduce(jnp.add, dks)

            q_block(ks, True)

            def q_step(qi, c):
                q_block(pl.multiple_of(qi * bq, bq), False)
                return c

            lax.fori_loop(kj + 1, nq, q_step, 0)
            return carry

        lax.fori_loop(0, nq, k_step, 0)

    spec = lambda d: pl.BlockSpec((1, S, d), lambda h, b: (h, b, 0))
    return _call_beside(
        body, transfer, grid=(H, B), name="attn_bwd",
        in_specs=[spec(QK_DIM), spec(QK_DIM), spec(V_DIM), spec(V_DIM), spec(1), spec(1)],
        out_specs=[spec(QK_DIM), spec(QK_DIM), spec(V_DIM)],
        out_shape=[SDS((H, B * S, QK_DIM), F32), SDS((H, B * S, QK_DIM), F32), SDS((H, B * S, V_DIM), F32)],
        scratch_shapes=[], semantics=("arbitrary", "arbitrary"),
        args=(q4, k4, v4, do4, delta4, lse4))


def _gdn_bwd(qg, kg, vg, gates, states, ainv, u4, w4, do4, B, S, transfer=None):
    H, D, C = GDN_HEADS, GDN_DIM, CHUNK
    NC = S // C
    U = GDN_BWD_UNROLL if NC % GDN_BWD_UNROLL == 0 else 1
    NG = NC // U

    def body(q_ref, k_ref, v_ref, g_ref, st_ref, ai_ref, u_ref, w_ref, do_ref, dq_ref, dk_ref, dv_ref, dgb_ref,
             kd_s, x1_s, x2_s, el_s, dvn_s, ds_s, w2t_s):
        h = pl.program_id(0)
        lane = lax.broadcasted_iota(jnp.int32, (C, LANES), 1)
        ri = lax.broadcasted_iota(jnp.int32, (C, C), 0)
        ci = lax.broadcasted_iota(jnp.int32, (C, C), 1)
        rcol = lax.broadcasted_iota(jnp.int32, (C, 1), 0)

        def rsum(a):
            return jnp.sum(a, axis=-1, keepdims=True)

        def prepare(n):
            cs = n * C
            q = q_ref[0, pl.ds(cs, C), :]
            k = k_ref[0, pl.ds(cs, C), :]
            do = do_ref[0, pl.ds(cs, C), :]
            Gc, bt, Gam, e, f, eL = _chunk_decays(g_ref[pl.ds(cs, C), :], lane, h, ri, ci, rcol)
            At = _mm_nt(q, k) * Gam
            yield
            x1 = _mm_tn(At, do)
            x2 = _mm_tn(q * e, do)
            kd = k * f
            w = w_ref[0, pl.ds(cs, C), :]
            yield
            x1_s[pl.ds(cs, C), :] = x1
            x2_s[n] = x2 - _mm_tn(w, x1)
            w2t_s[n] = _mm_tn(w, kd)
            kd_s[pl.ds(cs, C), :] = kd
            el_s[n] = jnp.broadcast_to(eL, (SUBLANES, LANES))

        def recur(n, dS):
            cs = n * C
            ds_s[n] = dS
            dvn_s[pl.ds(cs, C), :] = x1_s[pl.ds(cs, C), :] + _mm(kd_s[pl.ds(cs, C), :], dS)
            return x2_s[n] + el_s[n, 0:1, :] * dS - _mm(w2t_s[n], dS)

        def local(n):
            cs = n * C
            q = q_ref[0, pl.ds(cs, C), :]
            k = k_ref[0, pl.ds(cs, C), :]
            v = v_ref[0, pl.ds(cs, C), :]
            do = do_ref[0, pl.ds(cs, C), :]
            u = u_ref[0, pl.ds(cs, C), :]
            w = w_ref[0, pl.ds(cs, C), :]
            dvn = dvn_s[pl.ds(cs, C), :]
            dS = ds_s[n]
            Gc, bt, Gam, e, f, eL = _chunk_decays(g_ref[pl.ds(cs, C), :], lane, h, ri, ci, rcol)
            S0 = st_ref[0, n]
            AinvT = ai_ref[0, n]
            qk = _mm_nt(jnp.concatenate([q, k], axis=0), k)
            QK, KK = qk[:C], qk[C:]
            be = bt * e
            sol = jnp.concatenate([u, w], axis=-1)
            vn = u - _mm(w, S0)
            yield
            dAt = jnp.where(ri >= ci, _mm_nt(do, vn), 0.0)
            dqd = _mm_nt(do, S0)
            dw = -_mm_nt(dvn, S0)
            dkd = _mm_nt(vn, dS)
            deL = jnp.sum(rsum(dS * S0), axis=0, keepdims=True)
            yield
            dR = _mm_exact(AinvT, jnp.concatenate([dvn, dw], axis=-1))
            dR1, dR2 = dR[:, :D], dR[:, D:]
            yield
            dL = jnp.where(ri > ci, -_mm_nt(dR, sol), 0.0)
            yield
            dv_ref[0, pl.ds(cs, C), :] = dR1 * bt
            r2 = rsum(dR2 * k)
            X = dL * Gam
            dbt = rsum(dR1 * v) + r2 * e + rsum(X * KK)
            de = r2 * bt + rsum(dqd * q)
            dKK = X * bt
            dQK = dAt * Gam
            dq_ref[0, pl.ds(cs, C), :] = _mm(dQK, k) + dqd * e
            dk_ref[0, pl.ds(cs, C), :] = dR2 * be + _mm(dKK + dKK.T, k) + _mm_tn(dQK, q) + dkd * f
            df = rsum(dkd * k)
            Z = (dL * (bt * KK) + dAt * QK) * Gam
            dG = rsum(Z) - rsum(Z.T) + de * e - df * f
            dGl = jnp.sum(df * f, axis=0, keepdims=True) + deL * eL
            dG = dG + jnp.where(rcol == C - 1, dGl, 0.0)
            dgb_ref[0, pl.ds(cs, C), :] = jnp.where(lane == 0, dG, jnp.where(lane == 1, dbt, 0.0))

        state = [jnp.zeros((D, D), F32)]

        def recur_group(g):
            for j, n in enumerate(reversed(range(g * U, (g + 1) * U))):
                state[0] = recur(n, state[0])
                if j % GDN_RECUR_STEPS_PER_STAGE == GDN_RECUR_STEPS_PER_STAGE - 1:
                    yield

        def stage(fn, g):
            return _together([fn(g * U + j) for j in range(U)])

        for step in range(NG + 2):
            jobs = [(stage, prepare, NG - 1 - step), (None, None, NG - step), (stage, local, NG + 1 - step)]
            _lockstep([recur_group(g) if make is None else make(fn, g) for make, fn, g in jobs if 0 <= g < NG])

    spec = pl.BlockSpec((1, S, D), lambda h, b: (h, b, 0))
    return _call_beside(
        body, transfer, grid=(H, B), name="gdn_bwd",
        in_specs=[spec, spec, spec, pl.BlockSpec((S, LANES), lambda h, b: (b, 0)),
                  pl.BlockSpec((1, NC, D, D), lambda h, b: (h, b, 0, 0)),
                  pl.BlockSpec((1, NC, C, C), lambda h, b: (h, b, 0, 0)), spec, spec, spec],
        out_specs=[spec, spec, spec, spec],
        out_shape=[SDS((H, B * S, D), F32)] * 4,
        scratch_shapes=[pltpu.VMEM((S, D), F32), pltpu.VMEM((S, D), F32), pltpu.VMEM((NC, D, D), F32),
                        pltpu.VMEM((NC, SUBLANES, LANES), F32), pltpu.VMEM((S, D), F32),
                        pltpu.VMEM((NC, D, D), F32), pltpu.VMEM((NC, D, D), F32)],
        semantics=("arbitrary", "arbitrary"), args=(qg, kg, vg, gates, states, ainv, u4, w4, do4))


def _gdn_pre_bwd(proj, conv_w, alog_l, dt_l, dq4, dk4, dv4, dgb4, S):
    T = proj.shape[0]
    tm = min(256, T)
    tiles_per_seq = S // tm
    C3 = 3 * GDN_WIDTH
    H = GDN_HEADS

    def body(u_ref, halo_ref, gab_ref, w_ref, alog_ref, dt_ref, dq_ref, dk_ref, dv_ref, dgb_ref,
             dc_ref, dgab_ref, dcw_ref, dalog_ref, ddt_ref):
        i = pl.program_id(0)

        @pl.when(i == 0)
        def _():
            dcw_ref[...] = jnp.zeros_like(dcw_ref)
            dalog_ref[...] = jnp.zeros_like(dalog_ref)
            ddt_ref[...] = jnp.zeros_like(ddt_ref)

        halo = jnp.where(i % tiles_per_seq == 0, 0.0, halo_ref[...])
        c, sh = _conv_taps(u_ref[...], halo, w_ref[...])
        sg = _sigmoid(c)
        a = c * sg
        das = [None] * (3 * H)
        for h in range(H):
            xq = a[:, h * GDN_DIM:(h + 1) * GDN_DIM]
            xk = a[:, GDN_WIDTH + h * GDN_DIM:GDN_WIDTH + (h + 1) * GDN_DIM]
            das[h] = _l2n_bwd(dq_ref[h], xq, GDN_QSCALE)
            das[H + h] = _l2n_bwd(dk_ref[h], xk, 1.0)
            das[2 * H + h] = dv_ref[h]
        dc = jnp.concatenate(das, axis=-1) * (sg * (1.0 + c * (1.0 - sg)))
        dc_ref[...] = dc
        dcw_ref[...] += jnp.concatenate(
            [jnp.sum(dc * sh[CONV_W - 1 - t], axis=0, keepdims=True) for t in range(CONV_W)], axis=0)
        lane = lax.broadcasted_iota(jnp.int32, (tm, LANES), 1)
        ric = lax.broadcasted_iota(jnp.int32, (tm, LANES), 0) % CHUNK
        dG = jnp.zeros((tm, LANES), F32)
        for h in range(H):
            t = dgb_ref[h]
            dG = dG + jnp.where(lane == h, _pick_lane(t, lane, 0), 0.0) \
                    + jnp.where(lane == h + H, _pick_lane(t, lane, 1), 0.0)
        is_g = lane < H
        dg = jnp.where(is_g, _chunk_rev_cumsum(jnp.where(is_g, dG, 0.0), ric), 0.0)
        gab = gab_ref[...]
        g, beta = _gate_values(gab, alog_ref[...], dt_ref[...], lane)
        dga = jnp.where(is_g, dg * (-jnp.exp(alog_ref[...])) * _sigmoid(gab + dt_ref[...]), 0.0)
        dgb = jnp.where(is_g, 0.0, dG) * beta * (1.0 - beta)
        dgab_ref[...] = (dga + dgb).astype(MXU_DTYPE)
        dalog_ref[...] += jnp.sum(dg * g, axis=0, keepdims=True)
        ddt_ref[...] += jnp.sum(dga, axis=0, keepdims=True)

    hspec = pl.BlockSpec((H, tm, GDN_DIM), lambda i: (0, i, 0))
    vec = pl.BlockSpec((1, LANES), lambda i: (0, 0))
    return pl.pallas_call(
        body, grid=(T // tm,), name="gdn_pre_bwd",
        in_specs=[pl.BlockSpec((tm, C3), lambda i: (i, 0)),
                  pl.BlockSpec((SUBLANES, C3), lambda i: (jnp.maximum(i * (tm // SUBLANES) - 1, 0), 0)),
                  pl.BlockSpec((tm, LANES), lambda i: (i, P_GAB // LANES)),
                  pl.BlockSpec((CONV_W, C3), lambda i: (0, 0)), vec, vec, hspec, hspec, hspec, hspec],
        out_specs=[pl.BlockSpec((tm, C3), lambda i: (i, 0)), pl.BlockSpec((tm, LANES), lambda i: (i, 0)),
                   pl.BlockSpec((CONV_W, C3), lambda i: (0, 0)), vec, vec],
        out_shape=[SDS((T, C3), F32), SDS((T, LANES), MXU_DTYPE), SDS((CONV_W, C3), F32),
                   SDS((1, LANES), F32), SDS((1, LANES), F32)],
        compiler_params=_params(("arbitrary",)),
    )(proj, proj, proj, conv_w, alog_l, dt_l, dq4, dk4, dv4, dgb4)


def _mla_pre_bwd(proj, cosf, sinf, w_qln, w_kvln, w_uq_p, w_ukv, qnw, knw, dq4, dk4, dv4, transfer=None):
    T = proj.shape[0]
    tm = min(256, T)
    H = MLA_HEADS

    def body(ql_ref, kvl_ref, kpe_ref, cos_ref, sin_ref, wq_ref, wkv_ref, uq_ref, ukv_ref, qnw_ref, knw_ref,
             dq_ref, dk_ref, dv_ref,
             dql_ref, dkvl_ref, dkpe_ref, dqraw_ref, dkvraw_ref, qn_ref, kvn_ref, dwq_ref, dwkv_ref, dqnw_ref, dknw_ref):
        @pl.when(pl.program_id(0) == 0)
        def _():
            for r in (dwq_ref, dwkv_ref, dqnw_ref, dknw_ref):
                r[...] = jnp.zeros_like(r)

        cos, sin = cos_ref[...], sin_ref[...]
        qnw_, knw_ = qnw_ref[...], knw_ref[...]
        ql, kvl = ql_ref[...], kvl_ref[...]
        kpe_raw = kpe_ref[...][:, :ROPE]
        rms = functools.partial(_rms, on_mxu=True)
        rms_bwd = functools.partial(_rms_bwd, on_mxu=True)
        qn, rq = rms(ql, wq_ref[...])
        kvn, rkv = rms(kvl, wkv_ref[...])
        qn_ref[...] = qn.astype(MXU_DTYPE)
        kvn_ref[...] = kvn.astype(MXU_DTYPE)
        qraw = _mm(qn, uq_ref[...])
        kvraw = _mm(kvn, ukv_ref[...])
        dq_nope, dq_pe, dkv_parts = [], [], []
        dqnw_n = jnp.zeros((1, NOPE), F32)
        dqnw_p = jnp.zeros((1, ROPE), F32)
        dknw_n = jnp.zeros((1, NOPE), F32)
        dkpe = jnp.zeros((tm, ROPE), F32)
        for h in range(H):
            dq = dq_ref[h] * ATT_SCALE
            x = qraw[:, h * NOPE:(h + 1) * NOPE]
            dx, dw = rms_bwd(dq[:, :NOPE], x, qnw_[:, :NOPE], rms(x, qnw_[:, :NOPE])[1])
            dq_nope.append(dx)
            dqnw_n = dqnw_n + dw
            x = qraw[:, H * NOPE + h * ROPE:H * NOPE + (h + 1) * ROPE]
            dx, dw = rms_bwd(_rope_bwd(dq[:, NOPE:], cos, sin), x, qnw_[:, NOPE:], rms(x, qnw_[:, NOPE:])[1])
            dq_pe.append(dx)
            dqnw_p = dqnw_p + dw
            dk = dk_ref[h]
            x = kvraw[:, h * 256:h * 256 + NOPE]
            dx, dw = rms_bwd(dk[:, :NOPE], x, knw_[:, :NOPE], rms(x, knw_[:, :NOPE])[1])
            dknw_n = dknw_n + dw
            dkpe = dkpe + dk[:, NOPE:]
            dkv_parts += [dx, dv_ref[h]]
        dx, dknw_p = rms_bwd(_rope_bwd(dkpe, cos, sin), kpe_raw, knw_[:, NOPE:], rms(kpe_raw, knw_[:, NOPE:])[1])
        dkpe_ref[...] = jnp.concatenate([dx, jnp.zeros((tm, LANES - ROPE), F32)], axis=-1).astype(MXU_DTYPE)
        dqraw = jnp.concatenate(dq_nope + dq_pe, axis=-1).astype(MXU_DTYPE)
        dkvraw = jnp.concatenate(dkv_parts, axis=-1).astype(MXU_DTYPE)
        dqraw_ref[...] = dqraw
        dkvraw_ref[...] = dkvraw
        dx, dw = rms_bwd(_mm_nt(dqraw, uq_ref[...]), ql, wq_ref[...], rq)
        dql_ref[...] = dx.astype(MXU_DTYPE)
        dwq_ref[...] += dw
        dx, dw = rms_bwd(_mm_nt(dkvraw, ukv_ref[...]), kvl, wkv_ref[...], rkv)
        dkvl_ref[...] = dx.astype(MXU_DTYPE)
        dwkv_ref[...] += dw
        dqnw_ref[...] += jnp.concatenate([dqnw_n, dqnw_p], axis=-1)
        dknw_ref[...] += jnp.concatenate([dknw_n, dknw_p], axis=-1)

    full = lambda a: pl.BlockSpec(a.shape, lambda i: (0,) * a.ndim)
    rows = lambda n: pl.BlockSpec((tm, n), lambda i: (i, 0))
    const = lambda n: pl.BlockSpec((1, n), lambda i: (0, 0))
    NQ, NKV = w_uq_p.shape[1], w_ukv.shape[1]
    return _call_beside(
        body, transfer, grid=(T // tm,), name="mla_pre_bwd", scratch_shapes=[], semantics=("arbitrary",),
        args=(proj, proj, proj, cosf, sinf, w_qln, w_kvln, w_uq_p, w_ukv, qnw, knw, dq4, dk4, dv4),
        in_specs=[pl.BlockSpec((tm, 256), lambda i: (i, P_QLAT // 256)),
                  pl.BlockSpec((tm, 256), lambda i: (i, P_KVLAT // 256)),
                  pl.BlockSpec((tm, 128), lambda i: (i, P_KPE // 128)),
                  rows(ROPE), rows(ROPE),
                  full(w_qln), full(w_kvln), full(w_uq_p), full(w_ukv), full(qnw), full(knw),
                  pl.BlockSpec((H, tm, QK_DIM), lambda i: (0, i, 0)),
                  pl.BlockSpec((H, tm, QK_DIM), lambda i: (0, i, 0)),
                  pl.BlockSpec((H, tm, V_DIM), lambda i: (0, i, 0))],
        out_specs=[rows(Q_LORA), rows(KV_LORA), rows(LANES), rows(NQ), rows(NKV), rows(Q_LORA), rows(KV_LORA),
                   const(Q_LORA), const(KV_LORA), const(QK_DIM), const(QK_DIM)],
        out_shape=[SDS((T, Q_LORA), MXU_DTYPE), SDS((T, KV_LORA), MXU_DTYPE), SDS((T, LANES), MXU_DTYPE),
                   SDS((T, NQ), MXU_DTYPE), SDS((T, NKV), MXU_DTYPE),
                   SDS((T, Q_LORA), MXU_DTYPE), SDS((T, KV_LORA), MXU_DTYPE),
                   SDS((1, Q_LORA), F32), SDS((1, KV_LORA), F32), SDS((1, QK_DIM), F32), SDS((1, QK_DIM), F32)])


def _in_proj_bwd(dc, conv_w, dgz, dql, dkvl, dkpe, dgab, w_in_p, dh, x2, w_an, S):
    T, D = x2.shape
    N = w_in_p.shape[1]
    C3 = dc.shape[1]
    tm = min(512, S)
    assert S % tm == 0 and T % tm == 0, "a token tile must not straddle two sequences"
    tiles_per_seq = S // tm
    nblk = T // SUBLANES

    def body(dc_ref, nxt_ref, cw_ref, b_ref, c_ref, d_ref, e_ref, f_ref, w_ref, dh_ref, x_ref, wn_ref,
             dx_ref, dp_ref, dwn_ref):
        i = pl.program_id(0)

        @pl.when(i == 0)
        def _():
            dwn_ref[...] = jnp.zeros_like(dwn_ref)

        nxt = jnp.where(i % tiles_per_seq == tiles_per_seq - 1, 0.0, nxt_ref[...])
        dcv, cw = dc_ref[...], cw_ref[...]
        du = cw[3:4] * dcv
        for j in range(1, CONV_W):
            du = du + cw[3 - j:4 - j] * _shift_up(dcv, nxt, j)
        dp = jnp.concatenate([du.astype(MXU_DTYPE), b_ref[...], c_ref[...], d_ref[...], e_ref[...], f_ref[...]],
                             axis=-1).astype(MXU_DTYPE)
        dp_ref[...] = dp
        x = x_ref[...]
        _, r = _rms(x, wn_ref[...])
        dx, dw = _rms_bwd(_mm_nt(dp, w_ref[...]), x, wn_ref[...], r)
        dx_ref[...] = dh_ref[...] + dx
        dwn_ref[...] += dw

    rows = lambda n: pl.BlockSpec((tm, n), lambda i: (i, 0))
    return pl.pallas_call(
        body, grid=(T // tm,), name="in_proj_bwd",
        in_specs=[rows(C3),
                  pl.BlockSpec((SUBLANES, C3), lambda i: (jnp.minimum((i + 1) * (tm // SUBLANES), nblk - 1), 0)),
                  pl.BlockSpec((CONV_W, C3), lambda i: (0, 0)),
                  rows(dgz.shape[1]), rows(dql.shape[1]), rows(dkvl.shape[1]),
                  rows(dkpe.shape[1]), rows(dgab.shape[1]),
                  pl.BlockSpec((D, N), lambda i: (0, 0)), rows(D), rows(D), pl.BlockSpec((1, D), lambda i: (0, 0))],
        out_specs=[rows(D), rows(N), pl.BlockSpec((1, D), lambda i: (0, 0))],
        out_shape=[SDS((T, D), F32), SDS((T, N), MXU_DTYPE), SDS((1, D), F32)],
        compiler_params=_params(("arbitrary",)),
    )(dc, dc, conv_w, dgz, dql, dkvl, dkpe, dgab, w_in_p, dh, x2, w_an)


def _relu_squared(t):
    r = jnp.maximum(t.astype(F32), 0.0)
    return (r * r).astype(MXU_DTYPE)


def _wgrad(a, b, name, column_shards=False, a_map=None):
    T, M = a.shape
    N = b.shape[1]
    tM = _divisor_tile(M, 1024)
    tN = N // N_DEV if column_shards else _divisor_tile(N, 1536)
    tk = min(T, 2048)
    nk = T // tk

    def body(a_ref, b_ref, o_ref, acc):
        k = pl.program_id(2)

        @pl.when(k == 0)
        def _():
            acc[...] = jnp.zeros_like(acc)

        acc[...] += _mm_tn(a_ref[...] if a_map is None else a_map(a_ref[...]), b_ref[...])

        @pl.when(k == nk - 1)
        def _():
            o_ref[...] = acc[...].astype(WIRE_DTYPE).reshape(o_ref.shape)

    if column_shards:
        out_spec, out_shape = pl.BlockSpec((1, tM, tN), lambda i, j, k: (j, i, 0)), SDS((N_DEV, M, tN), WIRE_DTYPE)
    else:
        out_spec, out_shape = pl.BlockSpec((tM, tN), lambda i, j, k: (i, j)), SDS((M, N), WIRE_DTYPE)
    return pl.pallas_call(
        body, grid=(M // tM, N // tN, nk), name=name,
        in_specs=[pl.BlockSpec((tk, tM), lambda i, j, k: (k, i)), pl.BlockSpec((tk, tN), lambda i, j, k: (k, j))],
        out_specs=out_spec, out_shape=out_shape,
        scratch_shapes=[pltpu.VMEM((tM, tN), F32)],
        compiler_params=_params(("arbitrary", "arbitrary", "arbitrary")),
    )(a, b)


WGRAD_RING_SLOTS = 3


def _wgrad_stream(a, b, name, tile, stream_a=False, column_shards=False, a_map=None):
    T, M = a.shape
    N = b.shape[1]
    n = (M if stream_a else N) // tile
    tk = min(T, 2048)
    assert (M if stream_a else N) % tile == 0 and T % tk == 0 and (a_map is None or stream_a)
    held, src = (b, a) if stream_a else (a, b)

    def body(held_ref, src_ref, o_ref, ring, sems, *held_t):
        s = pl.program_id(0)

        def fetch(t, slot):
            cols = pl.ds(pl.multiple_of(t * tile, tile), tile)
            return pltpu.make_async_copy(src_ref.at[:, cols], ring.at[slot], sems.at[slot])

        @pl.when(s == 0)
        def _():
            for t in range(min(2, n)):
                fetch(t, t).start()
            if not stream_a:
                for k in range(T // tk):
                    held_t[0][:, pl.ds(k * tk, tk)] = held_ref[pl.ds(k * tk, tk), :].astype(MXU_DTYPE).T

        @pl.when(s + 2 < n)
        def _():
            fetch(s + 2, (s + 2) % WGRAD_RING_SLOTS).start()

        slot = s % WGRAD_RING_SLOTS
        fetch(s, slot).wait()
        acc = None if stream_a else _mm(held_t[0][...], ring[slot])
        for k in range(T // tk if stream_a else 0):
            rows = pl.ds(k * tk, tk)
            if stream_a:
                at = ring[slot, rows, :]
                part = _mm_tn(at if a_map is None else a_map(at), held_ref[rows, :])
            else:
                part = _mm(held_t[0][:, rows], ring[slot, rows, :])
            acc = part if acc is None else acc + part
        o_ref[...] = acc.astype(WIRE_DTYPE).reshape(o_ref.shape)

    if stream_a:
        out_spec, out_shape = pl.BlockSpec((tile, N), lambda s: (s, 0)), SDS((M, N), WIRE_DTYPE)
    elif column_shards:
        assert tile == N // N_DEV
        out_spec, out_shape = pl.BlockSpec((1, M, tile), lambda s: (s, 0, 0)), SDS((N_DEV, M, tile), WIRE_DTYPE)
    else:
        out_spec, out_shape = pl.BlockSpec((M, tile), lambda s: (0, s)), SDS((M, N), WIRE_DTYPE)
    return pl.pallas_call(
        body, grid=(n,), name=name,
        in_specs=[pl.BlockSpec(held.shape, lambda s: (0, 0)), pl.BlockSpec(memory_space=pl.ANY)],
        out_specs=out_spec, out_shape=out_shape,
        scratch_shapes=[pltpu.VMEM((WGRAD_RING_SLOTS, T, tile), src.dtype),
                        pltpu.SemaphoreType.DMA((WGRAD_RING_SLOTS,))]
        + ([] if stream_a else [pltpu.VMEM((M, T), MXU_DTYPE)]),
        compiler_params=_params(("arbitrary",)),
    )(held, src)


def _adamw(g, w, m, v):
    m = ADAM_B1 * m + (1.0 - ADAM_B1) * g
    v = ADAM_B2 * v + (1.0 - ADAM_B2) * jnp.square(g)
    m_hat = m / (1.0 - ADAM_B1 ** ADAM_STEP)
    v_hat = v / (1.0 - ADAM_B2 ** ADAM_STEP)
    return -ADAM_LR * (m_hat / (jnp.sqrt(v_hat) + ADAM_EPS) + ADAM_WD * w), m, v


def _reduce_adamw(parts, w, m, v, name):
    R, C = w.shape
    slots, Rp, Cp = parts.shape
    tr = min(R, 256)
    tp = tr if Rp == R else Rp

    def body(p_ref, w_ref, m_ref, v_ref, g_ref, d_ref, nm_ref, nv_ref):
        g = p_ref[0].astype(F32)
        for s in range(1, slots):
            g = g + p_ref[s].astype(F32)
        g = g[:tr, :C]
        g_ref[...] = g
        d_ref[...], nm_ref[...], nv_ref[...] = _adamw(g, w_ref[...], m_ref[...], v_ref[...])

    spec = pl.BlockSpec((tr, C), lambda i: (i, 0))
    return pl.pallas_call(
        body, grid=(R // tr,), name=name,
        in_specs=[pl.BlockSpec((slots, tp, Cp), lambda i: (0, i, 0)), spec, spec, spec],
        out_specs=[spec] * 4, out_shape=[SDS((R, C), F32)] * 4,
        compiler_params=_params(("arbitrary",)),
    )(parts, w, m, v)


SMALL_ROWS, SMALL_COLS = 16, 1024
SMALL_LAYOUT = (
    ("attn_norm_w", 0, 1, 1024, 1024), ("mlp_norm_w", 1, 1, 1024, 1024), ("q_lat_norm_w", 2, 1, 256, 256),
    ("kv_lat_norm_w", 3, 1, 256, 256), ("q_norm_w", 4, 1, 192, 192), ("k_norm_w", 5, 1, 192, 192),
    ("mla_out_norm_w", 6, 4, 128, 128), ("a_log", 10, 1, 128, 4), ("dt_bias", 11, 1, 128, 4),
    ("gdn_norm_w", 12, 1, 128, 128))
LOSS_ENTRY = ("loss", 13, 1, 128, 128)


def _adamw_replicated(parts, ws, ms, vs):
    n = len(SMALL_LAYOUT)

    def body(*refs):
        p_ref = refs[0]
        w_refs, m_refs, v_refs = refs[1:1 + n], refs[1 + n:1 + 2 * n], refs[1 + 2 * n:1 + 3 * n]
        outs = refs[1 + 3 * n:]
        s = p_ref[0]
        for d in range(1, N_DEV):
            s = s + p_ref[d]
        for i, (_, r0, nr, _, pw) in enumerate(SMALL_LAYOUT):
            g = s[r0:r0 + nr, :pw]
            outs[i][...] = g
            outs[n + i][...], outs[2 * n + i][...], outs[3 * n + i][...] = _adamw(
                g, w_refs[i][...], m_refs[i][...], v_refs[i][...])
        _, r0, nr, gw, _ = LOSS_ENTRY
        outs[4 * n][...] = s[r0:r0 + nr, :gw]

    res = pl.pallas_call(
        body, name="adamw_replicated",
        out_shape=[SDS(w.shape, F32) for w in ws] * 4 + [SDS((1, LANES), F32)],
        compiler_params=_params(),
    )(parts, *ws, *ms, *vs)
    return [res[k * n:(k + 1) * n] for k in range(4)], res[4 * n][0, 0]


COPIES_PER_ARRAY = N_DEV - 1


def _two_level_gather(srcs, outs, send_sems, recv_sems, local_sems=None, stage="all"):
    mx, my, mc = lax.axis_index("x"), lax.axis_index("y"), lax.axis_index("c")
    me, sibling = (mx, my, mc), (mx, my, 1 - mc)
    chips = [(1 - mx, my), (mx, 1 - my), (1 - mx, 1 - my)]
    arrays = range(len(srcs))

    def copy(a, k, block, to, src=None):
        px, py, pc = block
        slot = outs[a].at[4 * px + 2 * py + pc]
        sem = a * COPIES_PER_ARRAY + k
        return pltpu.make_async_remote_copy(
            src_ref=slot if src is None else src, dst_ref=slot,
            send_sem=send_sems.at[sem], recv_sem=recv_sems.at[sem], device_id=to, device_id_type=MESH_ID)

    mine = [] if local_sems is None else [
        pltpu.make_async_copy(srcs[a], outs[a].at[4 * mx + 2 * my + mc], local_sems.at[a]) for a in arrays]
    first = []
    for a in arrays:
        first.append(copy(a, 0, me, sibling, src=srcs[a]))
        first += [copy(a, 1 + j, me, (*chip, mc), src=srcs[a]) for j, chip in enumerate(chips)]
    forwards = [copy(a, 4 + j, (*chip, mc), sibling) for j, chip in enumerate(chips) for a in arrays]
    if stage in ("all", "start"):
        for cp in mine + first:
            cp.start()
    if stage in ("all", "forward"):
        for j, chip in enumerate(chips):
            for a in arrays:
                copy(a, 1 + j, (*chip, mc), me).wait_recv()
                forwards[j * len(srcs) + a].start()
    if stage in ("all", "finish"):
        for a in arrays:
            copy(a, 0, sibling, me).wait_recv()
        for j, chip in enumerate(chips):
            for a in arrays:
                copy(a, 4 + j, (*chip, 1 - mc), me).wait_recv()
        for cp in first + forwards:
            cp.wait_send()
        for cp in mine:
            cp.wait()


def _comm_scratch(n):
    return [pltpu.SemaphoreType.DMA((n * COPIES_PER_ARRAY,)), pltpu.SemaphoreType.DMA((n * COPIES_PER_ARRAY,)),
            pltpu.SemaphoreType.DMA((n,))]


def _any_specs(n):
    return [pl.BlockSpec(memory_space=pl.ANY)] * n


def _gather_weights(shards):
    n = len(shards)

    def body(*refs):
        _two_level_gather(refs[:n], refs[n:2 * n], *refs[2 * n:])

    return pl.pallas_call(
        body, name="gather_weights",
        out_shape=[SDS((N_DEV,) + s.shape, s.dtype) for s in shards],
        in_specs=_any_specs(n), out_specs=_any_specs(n), scratch_shapes=_comm_scratch(n),
    )(*shards)


def _gather_small_grads(gs, loss_lanes):
    gs = list(gs) + [loss_lanes]
    n = len(gs)

    def body(*refs):
        g_refs, out_ref = refs[:n], refs[n]
        tile, send_sems, recv_sems = refs[n + 1:]
        tile[...] = jnp.zeros_like(tile)
        for (_, r0, nr, gw, _), g in zip(SMALL_LAYOUT + (LOSS_ENTRY,), g_refs):
            tile[r0:r0 + nr, 0:gw] = g[...]
        me = 4 * lax.axis_index("x") + 2 * lax.axis_index("y") + lax.axis_index("c")
        out_ref[me] = tile[...]
        _two_level_gather([tile], [out_ref], send_sems, recv_sems)

    return pl.pallas_call(
        body, name="gather_small_grads",
        out_shape=SDS((N_DEV, SMALL_ROWS, SMALL_COLS), F32),
        in_specs=[pl.BlockSpec(memory_space=pltpu.VMEM)] * n,
        out_specs=pl.BlockSpec(memory_space=pltpu.VMEM),
        scratch_shapes=[pltpu.VMEM((SMALL_ROWS, SMALL_COLS), F32),
                        pltpu.SemaphoreType.DMA((COPIES_PER_ARRAY,)), pltpu.SemaphoreType.DMA((COPIES_PER_ARRAY,))],
    )(*gs)


def _exchange_grads_two_level(big, small):
    _, R, C = big.shape
    chip_flips = ((1, 0), (0, 1), (1, 1))

    def body(big_ref, small_ref, out_ref, small_out, mine_v, sib_v, pre_v, d2d_send, d2d_recv, ici_send, ici_recv,
             local_sems, s_send, s_recv, s_local):
        mx, my, mc = lax.axis_index("x"), lax.axis_index("y"), lax.axis_index("c")
        sibling = (mx, my, 1 - mc)
        chips = [(px, py) for px in range(2) for py in range(2)]
        _exchange([small_ref], [small_out], s_send, s_recv, s_local, stage="start")
        own = [pltpu.make_async_copy(big_ref.at[4 * px + 2 * py + mc], mine_v.at[q], local_sems.at[q])
               for q, (px, py) in enumerate(chips)]
        d2d = [pltpu.make_async_remote_copy(
            src_ref=big_ref.at[4 * px + 2 * py + (1 - mc)], dst_ref=sib_v.at[q], send_sem=d2d_send.at[q],
            recv_sem=d2d_recv.at[q], device_id=sibling, device_id_type=MESH_ID) for q, (px, py) in enumerate(chips)]
        for cp in own + d2d:
            cp.start()
        for cp in own + d2d:
            cp.wait()
        for q in range(4):
            pre_v[q] = (mine_v[q].astype(F32) + sib_v[q].astype(F32)).astype(pre_v.dtype)
        ici = []
        for k, (fx, fy) in enumerate(chip_flips):
            px = 1 - mx if fx else mx
            py = 1 - my if fy else my
            ici.append(pltpu.make_async_remote_copy(
                src_ref=pre_v.at[2 * px + py], dst_ref=out_ref.at[k], send_sem=ici_send.at[k],
                recv_sem=ici_recv.at[k], device_id=(px, py, mc), device_id_type=MESH_ID))
        keep = pltpu.make_async_copy(pre_v.at[2 * mx + my], out_ref.at[3], local_sems.at[4])
        for cp in ici + [keep]:
            cp.start()
        for cp in ici + [keep]:
            cp.wait()
        _exchange([small_ref], [small_out], s_send, s_recv, s_local, stage="finish")

    dma = pltpu.SemaphoreType.DMA
    return pl.pallas_call(
        body, name="exchange_grads",
        out_shape=[SDS((4, R, C), big.dtype), SDS(small.shape, small.dtype)],
        in_specs=_any_specs(2), out_specs=_any_specs(2),
        scratch_shapes=[pltpu.VMEM((4, R, C), big.dtype)] * 3 + [dma((4,)), dma((4,)), dma((3,)), dma((3,)), dma((5,))]
                       + _comm_scratch(1),
        compiler_params=_params(),
    )(big, small)


class _Transfer:
    def __init__(self, kind, arrays):
        self.kind, self.arrays, self.n = kind, list(arrays), len(arrays)

    def out_shapes(self):
        if self.kind == "gather":
            return [SDS((N_DEV,) + a.shape, a.dtype) for a in self.arrays]
        return [SDS(a.shape, a.dtype) for a in self.arrays]

    def run(self, srcs, outs, sems, stage):
        fn = _two_level_gather if self.kind == "gather" else _exchange
        fn(srcs, outs, *sems, stage=stage)


def _call_beside(body, transfer, *, grid, in_specs, out_specs, out_shape, scratch_shapes, name, semantics, args):
    if transfer is None:
        res = pl.pallas_call(body, grid=grid, in_specs=in_specs, out_specs=out_specs, out_shape=out_shape,
                             scratch_shapes=scratch_shapes, name=name, compiler_params=_params(semantics))(*args)
        return list(res), []
    n_in, n_out, n_s, n = len(in_specs), len(out_specs), len(scratch_shapes), transfer.n
    total = functools.reduce(lambda a, b: a * b, grid, 1)

    def wrapped(*refs):
        ins, refs = refs[:n_in], refs[n_in:]
        t_in, refs = refs[:n], refs[n:]
        outs, refs = refs[:n_out], refs[n_out:]
        t_out, refs = refs[:n], refs[n:]
        scratch, sems = refs[:n_s], refs[n_s:]
        first = functools.reduce(jnp.logical_and, [pl.program_id(i) == 0 for i in range(len(grid))])
        last = functools.reduce(jnp.logical_and, [pl.program_id(i) == g - 1 for i, g in enumerate(grid)])

        @pl.when(first)
        def _():
            transfer.run(t_in, t_out, sems, "start")

        step = functools.reduce(lambda acc, ig: acc * ig[1] + pl.program_id(ig[0]), enumerate(grid), 0)

        @pl.when(step == (3 * total) // 4)
        def _():
            transfer.run(t_in, t_out, sems, "forward")

        body(*ins, *outs, *scratch)

        @pl.when(last)
        def _():
            transfer.run(t_in, t_out, sems, "finish")

    res = pl.pallas_call(
        wrapped, grid=grid, in_specs=list(in_specs) + _any_specs(n), out_specs=list(out_specs) + _any_specs(n),
        out_shape=list(out_shape) + transfer.out_shapes(), scratch_shapes=list(scratch_shapes) + _comm_scratch(n),
        name=name, compiler_params=_params(semantics))(*args, *transfer.arrays)
    return list(res[:n_out]), list(res[n_out:])


EXCHANGE_FLIPS = ((0, 0, 1), (1, 0, 0), (0, 1, 0), (1, 1, 0), (1, 0, 1), (0, 1, 1), (1, 1, 1))


def _exchange(srcs, outs, send_sems, recv_sems, local_sems, stage="all"):
    mx, my, mc = lax.axis_index("x"), lax.axis_index("y"), lax.axis_index("c")
    arrays = range(len(srcs))
    copies = [pltpu.make_async_copy(srcs[a].at[4 * mx + 2 * my + mc], outs[a].at[N_DEV - 1], local_sems.at[a])
              for a in arrays]
    for k, (fx, fy, fc) in enumerate(EXCHANGE_FLIPS):
        px = 1 - mx if fx else mx
        py = 1 - my if fy else my
        pc = 1 - mc if fc else mc
        for a in arrays:
            sem = a * COPIES_PER_ARRAY + k
            copies.append(pltpu.make_async_remote_copy(
                src_ref=srcs[a].at[4 * px + 2 * py + pc], dst_ref=outs[a].at[k],
                send_sem=send_sems.at[sem], recv_sem=recv_sems.at[sem],
                device_id=(px, py, pc), device_id_type=MESH_ID))
    if stage in ("all", "start"):
        for cp in copies:
            cp.start()
    if stage in ("all", "finish"):
        for cp in copies:
            cp.wait()


def _w_in_to_padded(w):
    z = lambda n: jnp.zeros((w.shape[0], n), w.dtype)
    return jnp.concatenate([w[:, O_GQKV:O_GZ], w[:, O_GZ:O_GAB], w[:, O_QLAT:O_KVLAT], w[:, O_KVLAT:O_KPE],
                            w[:, O_KPE:O_GQKV], z(P_GAB - P_KPE - ROPE), w[:, O_GAB:O_END],
                            z(P_WIDTH - P_GAB - (O_END - O_GAB))], axis=1)


def _w_in_from_padded(wp):
    return jnp.concatenate([wp[:, P_QLAT:P_QLAT + 256], wp[:, P_KVLAT:P_KVLAT + 256], wp[:, P_KPE:P_KPE + ROPE],
                            wp[:, P_GQKV:P_GZ], wp[:, P_GZ:P_QLAT], wp[:, P_GAB:P_GAB + (O_END - O_GAB)]], axis=1)


W_IN_SHARD_COLS = (O_END - O_QLAT) // N_DEV


def _w_in_shards_to_padded(stack):
    _, R, Cw = stack.shape
    tr = min(R, 256)

    def body(s_ref, o_ref):
        full = jnp.concatenate([s_ref[d].astype(F32)[:, :W_IN_SHARD_COLS] for d in range(N_DEV)], axis=-1)
        o_ref[...] = _w_in_to_padded(full).astype(o_ref.dtype)

    return pl.pallas_call(
        body, grid=(R // tr,), name="w_in_to_padded",
        in_specs=[pl.BlockSpec((N_DEV, tr, Cw), lambda i: (0, i, 0))],
        out_specs=pl.BlockSpec((tr, P_WIDTH), lambda i: (i, 0)),
        out_shape=SDS((R, P_WIDTH), stack.dtype), compiler_params=_params(("arbitrary",)),
    )(stack)


def _w_in_padded_to_slabs(gp, wire_cols):
    R = gp.shape[0]
    tr = min(R, 256)

    def body(g_ref, o_ref):
        orig = _w_in_from_padded(g_ref[...].astype(F32))
        for d in range(N_DEV):
            piece = orig[:, d * W_IN_SHARD_COLS:(d + 1) * W_IN_SHARD_COLS]
            o_ref[d] = _pad2(piece, tr, wire_cols).astype(o_ref.dtype)

    return pl.pallas_call(
        body, grid=(R // tr,), name="w_in_to_slabs",
        in_specs=[pl.BlockSpec((tr, P_WIDTH), lambda i: (i, 0))],
        out_specs=pl.BlockSpec((N_DEV, tr, wire_cols), lambda i: (0, i, 0)),
        out_shape=SDS((N_DEV, R, wire_cols), gp.dtype), compiler_params=_params(("arbitrary",)),
    )(gp)


def _w_uq_to_headsplit(w):
    w3 = w.reshape(w.shape[0], MLA_HEADS, QK_DIM)
    return jnp.concatenate([w3[:, :, :NOPE].reshape(w.shape[0], -1), w3[:, :, NOPE:].reshape(w.shape[0], -1)], axis=1)


def _w_uq_from_headsplit(wp):
    n = wp[:, :MLA_HEADS * NOPE].reshape(wp.shape[0], MLA_HEADS, NOPE)
    p = wp[:, MLA_HEADS * NOPE:].reshape(wp.shape[0], MLA_HEADS, ROPE)
    return jnp.concatenate([n, p], axis=2).reshape(wp.shape[0], -1)


def _lane_vec(v4):
    return jnp.pad(v4.reshape(1, -1), ((0, 0), (0, LANES - v4.shape[-1])))


def _local_step(x, positions, target, attn_norm_w, w_in, q_lat_norm_w, w_uq, kv_lat_norm_w, w_ukv, q_norm_w,
                k_norm_w, mla_out_norm_w, conv_w, a_log, dt_bias, gdn_norm_w, w_out, mlp_norm_w, w_up, w_down,
                late_shards=None, exchange=False):
    B, S, D = x.shape
    T = B * S
    x2 = x.reshape(T, D)
    t2 = target.reshape(T, D)
    half = ROPE // 2
    inv_freq = ROPE_THETA ** (-jnp.arange(half, dtype=F32) / half)
    ang = positions.reshape(T, 1).astype(F32) * inv_freq
    cosf = jnp.concatenate([jnp.cos(ang)] * 2, axis=-1)
    sinf = jnp.concatenate([jnp.sin(ang)] * 2, axis=-1)
    w_in_p = w_in
    w_uq_p = _w_uq_to_headsplit(w_uq)
    alog_l, dt_l = _lane_vec(a_log), _lane_vec(dt_bias)
    w_an, w_qln, w_kvln, qnw, knw, w_mn, gdn_w = (
        attn_norm_w, q_lat_norm_w, kv_lat_norm_w, q_norm_w, k_norm_w, mlp_norm_w, gdn_norm_w)

    proj, xn, qg, kg, vg, gates = _in_proj(x2, w_an, w_in_p, conv_w, alog_l, dt_l, S)
    def gathering(shards):
        return None if late_shards is None else _Transfer("gather", shards)

    (q4, k4, v4), late = _mla_pre(proj, cosf, sinf, w_qln, w_kvln, w_uq_p, w_ukv, qnw, knw,
                                  gathering(late_shards and late_shards[:1]))
    if late:
        w_out = late[0].reshape(-1, D)
    (o_mla, lse), late = _attn_fwd(q4, k4, v4, B, S, gathering(late_shards and late_shards[2:]))
    if late:
        w_down = late[0].reshape(-1, D)
    (o_gdn, states, ainv, u4, w4), late = _gdn_fwd(qg, kg, vg, gates, B, S,
                                                   gathering(late_shards and late_shards[1:2]))
    if late:
        w_up = late[0]
    h2, mix = _mix_out(o_mla, o_gdn, proj, x2, mla_out_norm_w, gdn_w, w_out)
    up, hn, dy, sq, dyb = _mlp_fwd(h2, w_mn, w_up, w_down, t2)
    loss = (0.5 / D) * jnp.sum(sq[:, 0, 0])

    first = ("w_down",)
    second = ("w_up",)
    third = ("w_out", "w_uq", "w_ukv")
    mats = dict(w_down=_wgrad_stream(up, dyb, "wgrad_down", 512, stream_a=True, a_map=_relu_squared))

    def sending(names):
        return _Transfer("exchange", [_slabs(n, mats[n]) for n in names]) if exchange else None

    (dh, dhb, dup, d_mlp_norm), got = _mlp_bwd(dy, dyb, up, h2, w_mn, w_up, w_down, sending(first))
    mats.update(zip(first, got))
    mats.update(w_up=_wgrad_stream(hn, dup, "wgrad_up", D_FF // N_DEV, column_shards=True))
    do_mla, do_gdn, dz, d_mla_w, d_gdn_w, delta = _mix_bwd(dhb, o_mla, o_gdn, proj, mla_out_norm_w, gdn_w, w_out)
    mats.update(w_out=_wgrad(mix, dhb, "wgrad_out"))
    (dq4, dk4, dv4), got = _attn_bwd(q4, k4, v4, do_mla, delta, lse, B, S, sending(second))
    mats.update(zip(second, got))
    (dql, dkvl, dkpe, dqraw, dkvraw, qn, kvn, d_wqln, d_wkvln, d_qnw, d_knw), _ = _mla_pre_bwd(
        proj, cosf, sinf, w_qln, w_kvln, w_uq_p, w_ukv, qnw, knw, dq4, dk4, dv4)
    mats.update(w_uq=_wgrad(qn, dqraw, "wgrad_uq"), w_ukv=_wgrad(kvn, dkvraw, "wgrad_ukv"))
    (dqg, dkg, dvg, dgb4), got = _gdn_bwd(qg, kg, vg, gates, states, ainv, u4, w4, do_gdn, B, S, sending(third))
    mats.update(zip(third, got))
    dc, dgab, g_conv, d_alog, d_dt = _gdn_pre_bwd(proj, conv_w, alog_l, dt_l, dqg, dkg, dvg, dgb4, S)
    grad_x2, dproj, d_attn_norm = _in_proj_bwd(dc, conv_w, dz, dql, dkvl, dkpe, dgab, w_in_p, dh, x2, w_an, S)
    mats.update(w_in=_wgrad_stream(xn, dproj, "wgrad_in", 256), conv_w=g_conv)
    if exchange:
        last = ("w_in", "conv_w")
        mats.update(zip(last, _exchange_grads_two_level(*[_slabs(n, mats[n]) for n in last])))
    small = dict(attn_norm_w=d_attn_norm, mlp_norm_w=d_mlp_norm, q_lat_norm_w=d_wqln, kv_lat_norm_w=d_wkvln,
                 q_norm_w=d_qnw, k_norm_w=d_knw, mla_out_norm_w=d_mla_w, a_log=d_alog, dt_bias=d_dt,
                 gdn_norm_w=d_gdn_w)
    return loss, grad_x2.reshape(B, S, D), mats, [small[n] for n, *_ in SMALL_LAYOUT]


BIG = ("w_in", "w_uq", "w_ukv", "conv_w", "w_out", "w_up", "w_down")
ALL_W = ("attn_norm_w", "w_in", "q_lat_norm_w", "w_uq", "kv_lat_norm_w", "w_ukv", "q_norm_w", "k_norm_w",
         "mla_out_norm_w", "conv_w", "a_log", "dt_bias", "gdn_norm_w", "w_out", "mlp_norm_w", "w_up", "w_down")
WIRE_SHAPE = {"w_in": (1024, 384), "w_uq": (256, 128), "conv_w": (16, 256)}


def _pad2(a, rows, cols):
    return jnp.pad(a, [(0, 0)] * (a.ndim - 2) + [(0, rows - a.shape[-2]), (0, cols - a.shape[-1])])


def _cols_to_full(stack, cols):
    return jnp.moveaxis(stack[:, :, :cols], 0, 1).reshape(stack.shape[1], N_DEV * cols)


def _full_to_cols(full, wire_cols):
    r, n = full.shape
    return _pad2(jnp.moveaxis(full.reshape(r, N_DEV, n // N_DEV), 1, 0), r, wire_cols)


def _slabs(name, g):
    if name == "w_in":
        return _w_in_padded_to_slabs(g, WIRE_SHAPE["w_in"][1])
    if name == "w_uq":
        return _full_to_cols(_w_uq_from_headsplit(g), WIRE_SHAPE["w_uq"][1])
    if name == "w_ukv":
        return _full_to_cols(g, g.shape[1] // N_DEV)
    if name == "conv_w":
        return _pad2(_full_to_cols(g.astype(WIRE_DTYPE), g.shape[1] // N_DEV), *WIRE_SHAPE["conv_w"])
    if name == "w_up":
        return g
    return g.reshape(N_DEV, -1, g.shape[-1])


def kernel(x, positions, attn_norm_w, w_in, q_lat_norm_w, w_uq, kv_lat_norm_w, w_ukv, q_norm_w, k_norm_w, mla_out_norm_w, conv_w, a_log, dt_bias, gdn_norm_w, w_out, mlp_norm_w, w_up, w_down, loss_target, m_attn_norm_w, m_w_in, m_q_lat_norm_w, m_w_uq, m_kv_lat_norm_w, m_w_ukv, m_q_norm_w, m_k_norm_w, m_mla_out_norm_w, m_conv_w, m_a_log, m_dt_bias, m_gdn_norm_w, m_w_out, m_mlp_norm_w, m_w_up, m_w_down, v_attn_norm_w, v_w_in, v_q_lat_norm_w, v_w_uq, v_kv_lat_norm_w, v_w_ukv, v_q_norm_w, v_k_norm_w, v_mla_out_norm_w, v_conv_w, v_a_log, v_dt_bias, v_gdn_norm_w, v_w_out, v_mlp_norm_w, v_w_up, v_w_down):
    env = dict(locals())
    W = {n: env[n][0] for n in ALL_W}
    Mo = {n: env["m_" + n][0] for n in ALL_W}
    Vo = {n: env["v_" + n][0] for n in ALL_W}

    two_d = lambda a: a.reshape(1, -1) if a.ndim == 1 else a
    D = x.shape[-1]

    s_in, s_uq, s_ukv, s_conv = _gather_weights([
        _pad2(W["w_in"].astype(WIRE_DTYPE), *WIRE_SHAPE["w_in"]),
        _pad2(W["w_uq"].astype(WIRE_DTYPE), *WIRE_SHAPE["w_uq"]),
        W["w_ukv"].astype(WIRE_DTYPE), _pad2(W["conv_w"], *WIRE_SHAPE["conv_w"])])
    late = [W["w_out"].astype(WIRE_DTYPE), W["w_up"].astype(WIRE_DTYPE), W["w_down"].astype(WIRE_DTYPE)]

    loss, grad_x, parts, gs = _local_step(
        x, positions, loss_target, two_d(W["attn_norm_w"]), _w_in_shards_to_padded(s_in),
        two_d(W["q_lat_norm_w"]), _cols_to_full(s_uq, W["w_uq"].shape[1]), two_d(W["kv_lat_norm_w"]),
        _cols_to_full(s_ukv, W["w_ukv"].shape[1]), two_d(W["q_norm_w"]), two_d(W["k_norm_w"]),
        W["mla_out_norm_w"], _cols_to_full(s_conv[:, :CONV_W], W["conv_w"].shape[1]), two_d(W["a_log"]),
        two_d(W["dt_bias"]), two_d(W["gdn_norm_w"]), None, two_d(W["mlp_norm_w"]), None, None,
        late_shards=late, exchange=True)
    done = {n: _reduce_adamw(parts[n], W[n], Mo[n], Vo[n], "adamw_" + n) for n in BIG}
    names = [n for n, *_ in SMALL_LAYOUT]
    tiles = _gather_small_grads(gs, jnp.full((1, LANES), loss, F32))
    small, loss = _adamw_replicated(tiles, [two_d(W[n]) for n in names], [two_d(Mo[n]) for n in names],
                                    [two_d(Vo[n]) for n in names])
    for i, n in enumerate(names):
        done[n] = [small[kind][i] for kind in range(4)]
    res = [done[n][kind].reshape(env[n].shape) for kind in range(4) for n in ALL_W]
    return (loss, grad_x, *res)
```
